```python
import math
import jax
import jax.numpy as jnp
from jax import lax
import numpy as np

D_MODEL = 1024
BATCH = 16
SEQ = 2048
DEPTH = 1

HEAD_DIM = 64
N_HEADS_SB = 8
N_HEADS_DIL = 8
D_SB = N_HEADS_SB * HEAD_DIM
D_DIL = N_HEADS_DIL * HEAD_DIM
D_MIX = D_SB + D_DIL
D_FF = 2816
DIL_CONFIGS = ((128, 1), (512, 4), (2048, 16))
BLOCK = 128
N_BUCKETS = 32
MAX_DISTANCE = 2048
N_MOD = 9
EPS = 1e-6
NEG_INF = -1e30

kernel_name = 'hybrid_stickbreak_dilated_macaron_adaln'


def rmsnorm(x, g):
    x32 = x.astype(jnp.float32)
    y = x32 * lax.rsqrt(jnp.mean(x32 * x32, axis=-1, keepdims=True) + EPS)
    return y.astype(x.dtype) * g


def modulate(x, g, shift, scale):
    return rmsnorm(x, g) * (1 + scale[:, None, :]) + shift[:, None, :]


def swiglu(h, w_gate, w_up, w_down):
    return (jax.nn.silu(h @ w_gate) * (h @ w_up)) @ w_down


def to_heads(a, n_heads):
    b, s, _ = a.shape
    return a.reshape(b, s, n_heads, HEAD_DIM).transpose(0, 2, 1, 3)


def from_heads(a):
    b, h, s, d = a.shape
    return a.transpose(0, 2, 1, 3).reshape(b, s, h * d)


def t5_causal_bucket(n):
    max_exact = N_BUCKETS // 2
    nf = np.maximum(n, 1).astype(np.float32)
    large = max_exact + (np.log(nf / max_exact) / math.log(MAX_DISTANCE / max_exact)
                         * (N_BUCKETS - max_exact)).astype(np.int32)
    large = np.minimum(large, N_BUCKETS - 1)
    return np.where(n < max_exact, n, large).astype(np.int32)


def stick_breaking_attention(q, k, v):
    b, h, s, dh = q.shape
    n_blk = s // BLOCK
    scale = dh ** -0.5
    key_pos = jnp.arange(s)

    def one_block(blk):
        qb = lax.dynamic_slice_in_dim(q, blk * BLOCK, BLOCK, axis=2)
        z = jnp.einsum('bhtd,bhsd->bhts', qb, k).astype(jnp.float32) * scale
        q_pos = blk * BLOCK + jnp.arange(BLOCK)
        causal = key_pos[None, :] < q_pos[:, None]
        log_not = jnp.where(causal, jax.nn.log_sigmoid(-z), 0.0)
        suffix = lax.cumsum(log_not, axis=3, reverse=True) - log_not
        w = jnp.where(causal, jnp.exp(jax.nn.log_sigmoid(z) + suffix), 0.0)
        return jnp.einsum('bhts,bhsd->bhtd', w.astype(v.dtype), v)

    out = lax.map(one_block, jnp.arange(n_blk))
    return out.transpose(1, 2, 0, 3, 4).reshape(b, h, s, dh)


def dilated_config(q, k, v, bias_table, window, dilation):
    b, h, s, dh = q.shape
    n_steps = window // dilation
    sub_len = s // dilation
    n_blk = -(-sub_len // BLOCK)
    pad = n_blk * BLOCK - sub_len

    def to_residue(a):
        a = a.reshape(b, h, sub_len, dilation, dh).transpose(0, 1, 3, 2, 4)
        a = jnp.pad(a, ((0, 0), (0, 0), (0, 0), (0, pad), (0, 0)))
        return a.reshape(b, h, dilation, n_blk, BLOCK, dh)

    def with_prev(a):
        prev = jnp.pad(a, ((0, 0), (0, 0), (0, 0), (1, 0), (0, 0), (0, 0)))[:, :, :, :-1]
        return jnp.concatenate([prev, a], axis=4)

    qr = to_residue(q)
    kb = with_prev(to_residue(k))
    vb = with_prev(to_residue(v))
    z = jnp.einsum('bhrnqd,bhrnkd->bhrnqk', qr, kb).astype(jnp.float32) * (dh ** -0.5)

    step = BLOCK + np.arange(BLOCK)[:, None] - np.arange(2 * BLOCK)[None, :]
    in_band = (step >= 0) & (step <= n_steps)
    has_prev = (np.arange(n_blk)[:, None, None] > 0) | (np.arange(2 * BLOCK)[None, None, :] >= BLOCK)
    valid = in_band[None] & has_prev

    bias = bias_table[t5_causal_bucket(np.arange(n_steps + 1) * dilation)]
    bias = bias[np.clip(step, 0, n_steps)].astype(jnp.float32)
    z = z + bias.transpose(2, 0, 1)[None, :, None, None]
    z = jnp.where(valid[None, None, None], z, NEG_INF)

    m = jnp.max(z, axis=-1, keepdims=True)
    e = jnp.exp(z - m)
    denom = jnp.sum(e, axis=-1)
    o = jnp.einsum('bhrnqk,bhrnkd->bhrnqd', e.astype(v.dtype), vb) / denom[..., None].astype(v.dtype)
    lse = m[..., 0] + jnp.log(denom)

    def from_residue(a):
        a = a.reshape(b, h, dilation, n_blk * BLOCK, a.shape[-1])[:, :, :, :sub_len]
        return a.transpose(0, 1, 3, 2, 4).reshape(b, h, s, a.shape[-1])

    return from_residue(o), from_residue(lse[..., None])[..., 0]


def dilated_attention(q, k, v, bias_table):
    outs, lses = [], []
    for window, dilation in DIL_CONFIGS:
        o, lse = dilated_config(q, k, v, bias_table, window, dilation)
        outs.append(o)
        lses.append(lse)
    alpha = jax.nn.softmax(jnp.stack(lses, axis=0), axis=0)
    return jnp.einsum('cbhs,cbhsd->bhsd', alpha.astype(q.dtype), jnp.stack(outs, axis=0))


def token_mixer(h, w_in, g_sb_out, g_dil_out, w_out, rel_bias):
    qkv = h @ w_in
    q_sb, k_sb, v_sb, q_dil, k_dil, v_dil = jnp.split(
        qkv, [D_SB, 2 * D_SB, 3 * D_SB, 3 * D_SB + D_DIL, 3 * D_SB + 2 * D_DIL], axis=-1)
    o_sb = stick_breaking_attention(to_heads(q_sb, N_HEADS_SB), to_heads(k_sb, N_HEADS_SB),
                                    to_heads(v_sb, N_HEADS_SB))
    o_dil = dilated_attention(to_heads(q_dil, N_HEADS_DIL), to_heads(k_dil, N_HEADS_DIL),
                              to_heads(v_dil, N_HEADS_DIL), rel_bias)
    o_sb = rmsnorm(o_sb, g_sb_out[:, None, :])
    o_dil = rmsnorm(o_dil, g_dil_out[:, None, :])
    o = jnp.concatenate([from_heads(o_sb), from_heads(o_dil)], axis=-1)
    return o @ w_out


def _fwd_setup_inputs(seed: int = 0) -> dict:
    key = jax.random.key(seed)
    ks = jax.random.split(key, 20)
    f32 = jnp.float32

    def nrm(k, shape, scale):
        return jax.random.normal(k, shape, f32) * scale

    def gain(k, shape):
        return 1.0 + 0.05 * jax.random.normal(k, shape, f32)

    L, D = DEPTH, D_MODEL
    return {
        'x': nrm(ks[0], (BATCH, SEQ, D), 1.0),
        'c': nrm(ks[1], (BATCH, D), 1.0),
        'w_ada': nrm(ks[2], (L, D, N_MOD * D), 0.5 * D ** -0.5),
        'b_ada': nrm(ks[3], (L, N_MOD * D), 0.02),
        'g_ffn1': gain(ks[4], (L, D)),
        'w1_gate': nrm(ks[5], (L, D, D_FF), D ** -0.5),
        'w1_up': nrm(ks[6], (L, D, D_FF), D ** -0.5),
        'w1_down': nrm(ks[7], (L, D_FF, D), D_FF ** -0.5),
        'g_mix': gain(ks[8], (L, D)),
        'w_in': nrm(ks[9], (L, D, 3 * D_MIX), D ** -0.5),
        'g_sb_out': gain(ks[10], (L, N_HEADS_SB, HEAD_DIM)),
        'g_dil_out': gain(ks[11], (L, N_HEADS_DIL, HEAD_DIM)),
        'w_out': nrm(ks[12], (L, D_MIX, D), D_MIX ** -0.5),
        'rel_bias': nrm(ks[13], (N_BUCKETS, N_HEADS_DIL), 0.5),
        'g_ffn2': gain(ks[14], (L, D)),
        'w2_gate': nrm(ks[15], (L, D, D_FF), D ** -0.5),
        'w2_up': nrm(ks[16], (L, D, D_FF), D ** -0.5),
        'w2_down': nrm(ks[17], (L, D_FF, D), D_FF ** -0.5),
        'g_final': gain(ks[18], (D,)),
    }


def _fwd_reference(x, c, w_ada, b_ada, g_ffn1, w1_gate, w1_up, w1_down, g_mix, w_in, g_sb_out,
              g_dil_out, w_out, rel_bias, g_ffn2, w2_gate, w2_up, w2_down, g_final):
    for l in range(DEPTH):
        mod = (jax.nn.silu(c) @ w_ada[l] + b_ada[l]).reshape(c.shape[0], N_MOD, D_MODEL)
        sh1, sc1, gt1, sh2, sc2, gt2, sh3, sc3, gt3 = [mod[:, i] for i in range(N_MOD)]
        h = modulate(x, g_ffn1[l], sh1, sc1)
        x = x + 0.5 * gt1[:, None, :] * swiglu(h, w1_gate[l], w1_up[l], w1_down[l])
        h = modulate(x, g_mix[l], sh2, sc2)
        x = x + gt2[:, None, :] * token_mixer(h, w_in[l], g_sb_out[l], g_dil_out[l], w_out[l], rel_bias)
        h = modulate(x, g_ffn2[l], sh3, sc3)
        x = x + 0.5 * gt3[:, None, :] * swiglu(h, w2_gate[l], w2_up[l], w2_down[l])
    return rmsnorm(x, g_final)


import jax as _jax
import jax.numpy as _jnp

TWIN_FORMAT = 'train_step'
FWD_PARAMS = ['x', 'c', 'w_ada', 'b_ada', 'g_ffn1', 'w1_gate', 'w1_up', 'w1_down', 'g_mix', 'w_in', 'g_sb_out', 'g_dil_out', 'w_out', 'rel_bias', 'g_ffn2', 'w2_gate', 'w2_up', 'w2_down', 'g_final']
TWIN_WEIGHTS = ['w_ada', 'b_ada', 'g_ffn1', 'w1_gate', 'w1_up', 'w1_down', 'g_mix', 'w_in', 'g_sb_out', 'g_dil_out', 'w_out', 'rel_bias', 'g_ffn2', 'w2_gate', 'w2_up', 'w2_down', 'g_final']
TWIN_DIFF_INPUT = 'x'
TWIN_INPUTS = ['x', 'c', 'w_ada', 'b_ada', 'g_ffn1', 'w1_gate', 'w1_up', 'w1_down', 'g_mix', 'w_in', 'g_sb_out', 'g_dil_out', 'w_out', 'rel_bias', 'g_ffn2', 'w2_gate', 'w2_up', 'w2_down', 'g_final', 'loss_target', 'm_w_ada', 'm_b_ada', 'm_g_ffn1', 'm_w1_gate', 'm_w1_up', 'm_w1_down', 'm_g_mix', 'm_w_in', 'm_g_sb_out', 'm_g_dil_out', 'm_w_out', 'm_rel_bias', 'm_g_ffn2', 'm_w2_gate', 'm_w2_up', 'm_w2_down', 'm_g_final', 'v_w_ada', 'v_b_ada', 'v_g_ffn1', 'v_w1_gate', 'v_w1_up', 'v_w1_down', 'v_g_mix', 'v_w_in', 'v_g_sb_out', 'v_g_dil_out', 'v_w_out', 'v_rel_bias', 'v_g_ffn2', 'v_w2_gate', 'v_w2_up', 'v_w2_down', 'v_g_final']
TWIN_OUTPUTS = ['loss', 'grad_x', 'grad_w_ada', 'grad_b_ada', 'grad_g_ffn1', 'grad_w1_gate', 'grad_w1_up', 'grad_w1_down', 'grad_g_mix', 'grad_w_in', 'grad_g_sb_out', 'grad_g_dil_out', 'grad_w_out', 'grad_rel_bias', 'grad_g_ffn2', 'grad_w2_gate', 'grad_w2_up', 'grad_w2_down', 'grad_g_final', 'delta_w_ada', 'delta_b_ada', 'delta_g_ffn1', 'delta_w1_gate', 'delta_w1_up', 'delta_w1_down', 'delta_g_mix', 'delta_w_in', 'delta_g_sb_out', 'delta_g_dil_out', 'delta_w_out', 'delta_rel_bias', 'delta_g_ffn2', 'delta_w2_gate', 'delta_w2_up', 'delta_w2_down', 'delta_g_final', 'new_m_w_ada', 'new_m_b_ada', 'new_m_g_ffn1', 'new_m_w1_gate', 'new_m_w1_up', 'new_m_w1_down', 'new_m_g_mix', 'new_m_w_in', 'new_m_g_sb_out', 'new_m_g_dil_out', 'new_m_w_out', 'new_m_rel_bias', 'new_m_g_ffn2', 'new_m_w2_gate', 'new_m_w2_up', 'new_m_w2_down', 'new_m_g_final', 'new_v_w_ada', 'new_v_b_ada', 'new_v_g_ffn1', 'new_v_w1_gate', 'new_v_w1_up', 'new_v_w1_down', 'new_v_g_mix', 'new_v_w_in', 'new_v_g_sb_out', 'new_v_g_dil_out', 'new_v_w_out', 'new_v_rel_bias', 'new_v_g_ffn2', 'new_v_w2_gate', 'new_v_w2_up', 'new_v_w2_down', 'new_v_g_final']
TWIN_LEAF_KINDS = {'loss': 'loss', 'grad_x': 'grad_x', 'grad_w_ada': 'grad_w', 'grad_b_ada': 'grad_w', 'grad_g_ffn1': 'grad_w', 'grad_w1_gate': 'grad_w', 'grad_w1_up': 'grad_w', 'grad_w1_down': 'grad_w', 'grad_g_mix': 'grad_w', 'grad_w_in': 'grad_w', 'grad_g_sb_out': 'grad_w', 'grad_g_dil_out': 'grad_w', 'grad_w_out': 'grad_w', 'grad_rel_bias': 'grad_w', 'grad_g_ffn2': 'grad_w', 'grad_w2_gate': 'grad_w', 'grad_w2_up': 'grad_w', 'grad_w2_down': 'grad_w', 'grad_g_final': 'grad_w', 'delta_w_ada': 'delta_w', 'delta_b_ada': 'delta_w', 'delta_g_ffn1': 'delta_w', 'delta_w1_gate': 'delta_w', 'delta_w1_up': 'delta_w', 'delta_w1_down': 'delta_w', 'delta_g_mix': 'delta_w', 'delta_w_in': 'delta_w', 'delta_g_sb_out': 'delta_w', 'delta_g_dil_out': 'delta_w', 'delta_w_out': 'delta_w', 'delta_rel_bias': 'delta_w', 'delta_g_ffn2': 'delta_w', 'delta_w2_gate': 'delta_w', 'delta_w2_up': 'delta_w', 'delta_w2_down': 'delta_w', 'delta_g_final': 'delta_w', 'new_m_w_ada': 'new_m', 'new_m_b_ada': 'new_m', 'new_m_g_ffn1': 'new_m', 'new_m_w1_gate': 'new_m', 'new_m_w1_up': 'new_m', 'new_m_w1_down': 'new_m', 'new_m_g_mix': 'new_m', 'new_m_w_in': 'new_m', 'new_m_g_sb_out': 'new_m', 'new_m_g_dil_out': 'new_m', 'new_m_w_out': 'new_m', 'new_m_rel_bias': 'new_m', 'new_m_g_ffn2': 'new_m', 'new_m_w2_gate': 'new_m', 'new_m_w2_up': 'new_m', 'new_m_w2_down': 'new_m', 'new_m_g_final': 'new_m', 'new_v_w_ada': 'new_v', 'new_v_b_ada': 'new_v', 'new_v_g_ffn1': 'new_v', 'new_v_w1_gate': 'new_v', 'new_v_w1_up': 'new_v', 'new_v_w1_down': 'new_v', 'new_v_g_mix': 'new_v', 'new_v_w_in': 'new_v', 'new_v_g_sb_out': 'new_v', 'new_v_g_dil_out': 'new_v', 'new_v_w_out': 'new_v', 'new_v_rel_bias': 'new_v', 'new_v_g_ffn2': 'new_v', 'new_v_w2_gate': 'new_v', 'new_v_w2_up': 'new_v', 'new_v_w2_down': 'new_v', 'new_v_g_final': 'new_v'}


def _forward(args):
    return _fwd_reference(*[args[k] for k in FWD_PARAMS])


def _output_shape():
    out = _jax.eval_shape(lambda: _forward(_fwd_setup_inputs(0)))
    return out.shape, out.dtype

N_MICROBATCH = 1
ADAM_LR = 0.001
ADAM_B1 = 0.9
ADAM_B2 = 0.999
ADAM_EPS = 1e-08
ADAM_WD = 0.01
ADAM_STEP = 10
PER_EXAMPLE_BATCH_AXIS = {'x': 0, 'c': 0, 'loss_target': 0}
SHARED_INPUTS = []
_WEIGHT_DTYPES = {'w_ada': _jnp.float32, 'b_ada': _jnp.float32, 'g_ffn1': _jnp.float32, 'w1_gate': _jnp.float32, 'w1_up': _jnp.float32, 'w1_down': _jnp.float32, 'g_mix': _jnp.float32, 'w_in': _jnp.float32, 'g_sb_out': _jnp.float32, 'g_dil_out': _jnp.float32, 'w_out': _jnp.float32, 'rel_bias': _jnp.float32, 'g_ffn2': _jnp.float32, 'w2_gate': _jnp.float32, 'w2_up': _jnp.float32, 'w2_down': _jnp.float32, 'g_final': _jnp.float32}
MOMENT_SCALE = {'w_ada': 1.047524e-01, 'b_ada': 1.897729e-01, 'g_ffn1': 2.803833e-02, 'w1_gate': 1.216761e-02, 'w1_up': 1.182214e-02, 'w1_down': 1.958274e-02, 'g_mix': 5.825876e-02, 'w_in': 5.583749e-02, 'g_sb_out': 6.945201e-02, 'g_dil_out': 1.247605e-01, 'w_out': 9.199991e-02, 'rel_bias': 3.295565e-02, 'g_ffn2': 2.747305e-02, 'w2_gate': 1.169929e-02, 'w2_up': 1.141523e-02, 'w2_down': 1.900769e-02, 'g_final': 3.215685e+01}


def _to_microbatches(a, axis):
    t = _jnp.moveaxis(a, axis, 0)
    t = t.reshape((N_MICROBATCH, t.shape[0] // N_MICROBATCH) + t.shape[1:])
    return _jnp.moveaxis(t, 1, axis + 1)


def setup_inputs(seed: int = 0) -> dict:
    inp = _fwd_setup_inputs(seed)
    key = _jax.random.fold_in(_jax.random.key(seed), 7919)
    shape, _ = _output_shape()
    out = dict(inp)
    out["loss_target"] = _jax.random.normal(_jax.random.fold_in(key, 0), shape, _jnp.float32)
    for i, name in enumerate(TWIN_WEIGHTS):
        w = inp[name].astype(_jnp.float32)
        if MOMENT_SCALE is None:
            s = _jnp.sqrt(_jnp.mean(_jnp.square(w)) + 1e-30)
        else:
            s = MOMENT_SCALE[name]
        km, kv = _jax.random.split(_jax.random.fold_in(key, i + 1))
        out[name] = w
        out["m_" + name] = s * _jax.random.normal(km, w.shape, _jnp.float32)
        out["v_" + name] = (s * s) * _jax.random.uniform(kv, w.shape, _jnp.float32, 0.5, 1.5)
    if N_MICROBATCH > 1:
        for name, axis in PER_EXAMPLE_BATCH_AXIS.items():
            out[name] = _to_microbatches(out[name], axis)
    return {'x': out['x'], 'c': out['c'], 'w_ada': out['w_ada'], 'b_ada': out['b_ada'], 'g_ffn1': out['g_ffn1'], 'w1_gate': out['w1_gate'], 'w1_up': out['w1_up'], 'w1_down': out['w1_down'], 'g_mix': out['g_mix'], 'w_in': out['w_in'], 'g_sb_out': out['g_sb_out'], 'g_dil_out': out['g_dil_out'], 'w_out': out['w_out'], 'rel_bias': out['rel_bias'], 'g_ffn2': out['g_ffn2'], 'w2_gate': out['w2_gate'], 'w2_up': out['w2_up'], 'w2_down': out['w2_down'], 'g_final': out['g_final'], 'loss_target': out['loss_target'], 'm_w_ada': out['m_w_ada'], 'm_b_ada': out['m_b_ada'], 'm_g_ffn1': out['m_g_ffn1'], 'm_w1_gate': out['m_w1_gate'], 'm_w1_up': out['m_w1_up'], 'm_w1_down': out['m_w1_down'], 'm_g_mix': out['m_g_mix'], 'm_w_in': out['m_w_in'], 'm_g_sb_out': out['m_g_sb_out'], 'm_g_dil_out': out['m_g_dil_out'], 'm_w_out': out['m_w_out'], 'm_rel_bias': out['m_rel_bias'], 'm_g_ffn2': out['m_g_ffn2'], 'm_w2_gate': out['m_w2_gate'], 'm_w2_up': out['m_w2_up'], 'm_w2_down': out['m_w2_down'], 'm_g_final': out['m_g_final'], 'v_w_ada': out['v_w_ada'], 'v_b_ada': out['v_b_ada'], 'v_g_ffn1': out['v_g_ffn1'], 'v_w1_gate': out['v_w1_gate'], 'v_w1_up': out['v_w1_up'], 'v_w1_down': out['v_w1_down'], 'v_g_mix': out['v_g_mix'], 'v_w_in': out['v_w_in'], 'v_g_sb_out': out['v_g_sb_out'], 'v_g_dil_out': out['v_g_dil_out'], 'v_w_out': out['v_w_out'], 'v_rel_bias': out['v_rel_bias'], 'v_g_ffn2': out['v_g_ffn2'], 'v_w2_gate': out['v_w2_gate'], 'v_w2_up': out['v_w2_up'], 'v_w2_down': out['v_w2_down'], 'v_g_final': out['v_g_final']}


def _loss(weights, diff, rest, loss_target):
    with _jax.named_scope("forward"):
        args = {**rest, TWIN_DIFF_INPUT: diff, **{k: w.astype(_WEIGHT_DTYPES[k]) for k, w in weights.items()}}
        y = _forward(args)
    with _jax.named_scope("loss_head"):
        err = _jnp.square(y.astype(_jnp.float32) - loss_target)
        return 0.5 * _jnp.sum(_jnp.mean(err, axis=-1)) if err.ndim else 0.5 * err


def _adamw(w, g, m, v):
    m = ADAM_B1 * m + (1.0 - ADAM_B1) * g
    v = ADAM_B2 * v + (1.0 - ADAM_B2) * _jnp.square(g)
    m_hat = m / (1.0 - ADAM_B1 ** ADAM_STEP)
    v_hat = v / (1.0 - ADAM_B2 ** ADAM_STEP)
    delta = -ADAM_LR * (m_hat / (_jnp.sqrt(v_hat) + ADAM_EPS) + ADAM_WD * w)
    return delta, m, v


def reference(x, c, w_ada, b_ada, g_ffn1, w1_gate, w1_up, w1_down, g_mix, w_in, g_sb_out, g_dil_out, w_out, rel_bias, g_ffn2, w2_gate, w2_up, w2_down, g_final, loss_target, m_w_ada, m_b_ada, m_g_ffn1, m_w1_gate, m_w1_up, m_w1_down, m_g_mix, m_w_in, m_g_sb_out, m_g_dil_out, m_w_out, m_rel_bias, m_g_ffn2, m_w2_gate, m_w2_up, m_w2_down, m_g_final, v_w_ada, v_b_ada, v_g_ffn1, v_w1_gate, v_w1_up, v_w1_down, v_g_mix, v_w_in, v_g_sb_out, v_g_dil_out, v_w_out, v_rel_bias, v_g_ffn2, v_w2_gate, v_w2_up, v_w2_down, v_g_final):
    given = dict(x=x, c=c, w_ada=w_ada, b_ada=b_ada, g_ffn1=g_ffn1, w1_gate=w1_gate, w1_up=w1_up, w1_down=w1_down, g_mix=g_mix, w_in=w_in, g_sb_out=g_sb_out, g_dil_out=g_dil_out, w_out=w_out, rel_bias=rel_bias, g_ffn2=g_ffn2, w2_gate=w2_gate, w2_up=w2_up, w2_down=w2_down, g_final=g_final, loss_target=loss_target, m_w_ada=m_w_ada, m_b_ada=m_b_ada, m_g_ffn1=m_g_ffn1, m_w1_gate=m_w1_gate, m_w1_up=m_w1_up, m_w1_down=m_w1_down, m_g_mix=m_g_mix, m_w_in=m_w_in, m_g_sb_out=m_g_sb_out, m_g_dil_out=m_g_dil_out, m_w_out=m_w_out, m_rel_bias=m_rel_bias, m_g_ffn2=m_g_ffn2, m_w2_gate=m_w2_gate, m_w2_up=m_w2_up, m_w2_down=m_w2_down, m_g_final=m_g_final, v_w_ada=v_w_ada, v_b_ada=v_b_ada, v_g_ffn1=v_g_ffn1, v_w1_gate=v_w1_gate, v_w1_up=v_w1_up, v_w1_down=v_w1_down, v_g_mix=v_g_mix, v_w_in=v_w_in, v_g_sb_out=v_g_sb_out, v_g_dil_out=v_g_dil_out, v_w_out=v_w_out, v_rel_bias=v_rel_bias, v_g_ffn2=v_g_ffn2, v_w2_gate=v_w2_gate, v_w2_up=v_w2_up, v_w2_down=v_w2_down, v_g_final=v_g_final)
    weights = {n: given[n] for n in TWIN_WEIGHTS}
    shared = {n: given[n] for n in SHARED_INPUTS}
    per_example = {n: given[n] for n in ['x', 'c']}
    grad_fn = _jax.value_and_grad(_loss, argnums=(0, 1))

    def one_microbatch(ex, loss_target):
        ex = dict(ex)
        diff = ex.pop(TWIN_DIFF_INPUT)
        return grad_fn(weights, diff, {**shared, **ex}, loss_target)

    if N_MICROBATCH == 1:
        loss, (grad_w, grad_x) = one_microbatch(per_example, given["loss_target"])
    else:
        def body(carry, xs):
            loss_sum, grad_sum = carry
            l_k, (gw_k, gx_k) = one_microbatch(xs[0], xs[1])
            with _jax.named_scope("update"):
                return (loss_sum + l_k, _jax.tree.map(_jnp.add, grad_sum, gw_k)), gx_k

        init = (_jnp.zeros((), _jnp.float32), _jax.tree.map(_jnp.zeros_like, weights))
        (loss, grad_w), grad_x = _jax.lax.scan(body, init, (per_example, given["loss_target"]))
    with _jax.named_scope("update"):
        delta_w, new_m, new_v = {}, {}, {}
        for n in TWIN_WEIGHTS:
            delta_w[n], new_m[n], new_v[n] = _adamw(weights[n], grad_w[n], given["m_" + n], given["v_" + n])
    return (loss, grad_x, *[grad_w[n] for n in TWIN_WEIGHTS], *[delta_w[n] for n in TWIN_WEIGHTS],
            *[new_m[n] for n in TWIN_WEIGHTS], *[new_v[n] for n in TWIN_WEIGHTS])
```

```python
import functools
import math

import numpy as np
import jax
import jax.numpy as jnp
from jax import lax
from jax.experimental import pallas as pl
from jax.experimental.pallas import tpu as pltpu

F32 = jnp.float32
BF = jnp.bfloat16
MESH = pl.DeviceIdType.MESH

HEAD_DIM = 64
N_HEADS = 8
D_GRP = N_HEADS * HEAD_DIM
DIL_CONFIGS = ((128, 1), (512, 4), (2048, 16))
N_STEPS = 128
BLOCK = 128
N_BUCKETS = 32
MAX_DISTANCE = 2048
N_MOD = 9
EPS = 1e-6
NEG_INF = -1e30
SCALE = HEAD_DIM ** -0.5

ADAM_LR = 0.001
ADAM_B1 = 0.9
ADAM_B2 = 0.999
ADAM_EPS = 1e-08
ADAM_WD = 0.01
ADAM_STEP = 10

N_CHIPS = 4
N_DEV = 8
VMEM_LIMIT = 56 * 1024 * 1024
TM = 512
TQ = 256
KB = 256
SMALL_ROWS = 120


def _cp(n_axes=0, **kw):
    sem = ("arbitrary",) * n_axes if n_axes else None
    return pltpu.CompilerParams(dimension_semantics=sem, vmem_limit_bytes=VMEM_LIMIT, **kw)


def _nn(a, b):
    return jnp.dot(a, b, preferred_element_type=F32)


def _nt(a, b):
    return lax.dot_general(a, b, (((1,), (1,)), ((), ())), preferred_element_type=F32)


def _tn(a, b):
    return lax.dot_general(a, b, (((0,), (0,)), ((), ())), preferred_element_type=F32)


def _nn2(x, m):
    hi = x.astype(BF)
    lo = (x - hi.astype(F32)).astype(BF)
    return _nn(hi, m) + _nn(lo, m)


def _softplus(z):
    return jnp.maximum(z, 0.0) + jnp.log1p(jnp.exp(-jnp.abs(z)))


def _sds(shape, dtype):
    return jax.ShapeDtypeStruct(shape, dtype)


def _modnorm(x, g, sc, sh, seq):
    t, d = x.shape
    per = seq // TM

    def body(x_ref, g_ref, sc_ref, sh_ref, h_ref):
        xv = x_ref[...]
        r = lax.rsqrt(jnp.mean(xv * xv, axis=-1, keepdims=True) + EPS)
        h_ref[...] = (((xv * r) * g_ref[...]) * (1.0 + sc_ref[...]) + sh_ref[...]).astype(BF)

    return pl.pallas_call(
        body, name="modnorm", grid=(t // TM,),
        in_specs=[pl.BlockSpec((TM, d), lambda m: (m, 0)),
                  pl.BlockSpec((1, d), lambda m: (0, 0)),
                  pl.BlockSpec((None, 1, d), lambda m: (m // per, 0, 0)),
                  pl.BlockSpec((None, 1, d), lambda m: (m // per, 0, 0))],
        out_specs=pl.BlockSpec((TM, d), lambda m: (m, 0)),
        out_shape=_sds((t, d), BF), compiler_params=_cp(1))(x, g, sc, sh)


def _modnorm_bwd_tile(dh, xv, gv, scv, dxo):
    r = lax.rsqrt(jnp.mean(xv * xv, axis=-1, keepdims=True) + EPS)
    n = xv * r
    ng = n * gv
    dsh = jnp.sum(dh, axis=0, keepdims=True)
    dsc = jnp.sum(dh * ng, axis=0, keepdims=True)
    dy = dh * (1.0 + scv)
    dg = jnp.sum(dy * n, axis=0, keepdims=True)
    dn = dy * gv
    dx = dxo + r * (dn - n * jnp.mean(dn * n, axis=-1, keepdims=True))
    return dx, dsh, dsc, dg


def _acc_rows(ref, val, first):
    @pl.when(first)
    def _():
        ref[...] = val

    @pl.when(jnp.logical_not(first))
    def _():
        ref[...] += val


def _ffn_up(h, wgu, f_idx):
    t, d = h.shape
    fs = wgu.shape[-1]

    def body(h_ref, w_ref, a_ref, u_ref, s_ref):
        hv = h_ref[...]
        a = _nn(hv, w_ref[0])
        u = _nn(hv, w_ref[1])
        a_ref[...] = a
        u_ref[...] = u
        s_ref[...] = ((a * jax.nn.sigmoid(a)) * u).astype(BF)

    blk = pl.BlockSpec((None, TM, fs), lambda j, m: (j, m, 0))
    return pl.pallas_call(
        body, name="ffn_up", grid=(N_CHIPS, t // TM),
        in_specs=[pl.BlockSpec((TM, d), lambda j, m: (m, 0)),
                  pl.BlockSpec((None, 2, d, fs), lambda j, m: (j, f_idx, 0, 0))],
        out_specs=[blk, blk, blk],
        out_shape=[_sds((N_CHIPS, t, fs), F32), _sds((N_CHIPS, t, fs), F32), _sds((N_CHIPS, t, fs), BF)],
        compiler_params=_cp(2))(h, wgu)


def _ffn_down(s, wd, f_idx, x, gt, seq, coef):
    _, t, fs = s.shape
    d = x.shape[-1]
    per = seq // TM

    def body(s_ref, w_ref, x_ref, gt_ref, f_ref, xo_ref, acc):
        j = pl.program_id(1)

        @pl.when(j == 0)
        def _():
            acc[...] = jnp.zeros_like(acc)

        acc[...] += _nn(s_ref[...], w_ref[...])

        @pl.when(j == N_CHIPS - 1)
        def _():
            f = acc[...]
            f_ref[...] = f
            xo_ref[...] = x_ref[...] + (coef * gt_ref[...]) * f

    row = pl.BlockSpec((TM, d), lambda m, j: (m, 0))
    return pl.pallas_call(
        body, name="ffn_down", grid=(t // TM, N_CHIPS),
        in_specs=[pl.BlockSpec((None, TM, fs), lambda m, j: (j, m, 0)),
                  pl.BlockSpec((None, None, fs, d), lambda m, j: (j, f_idx, 0, 0)),
                  row,
                  pl.BlockSpec((None, 1, d), lambda m, j: (m // per, 0, 0))],
        out_specs=[row, row],
        out_shape=[_sds((t, d), F32), _sds((t, d), F32)],
        scratch_shapes=[pltpu.VMEM((TM, d), F32)],
        compiler_params=_cp(2))(s, wd, x, gt)


def _ffn_bwd_ds(dxo, gt, f, wd, f_idx, a, u, seq, coef):
    t, d = dxo.shape
    fs = a.shape[-1]
    per = seq // TM
    nb = t // seq

    def body(dxo_ref, gt_ref, f_ref, w_ref, a_ref, u_ref, da_ref, du_ref, df_ref, dgt_ref):
        m = pl.program_id(0)
        j = pl.program_id(1)
        dxv = dxo_ref[...]
        df = ((coef * gt_ref[...]) * dxv).astype(BF)

        @pl.when(j == 0)
        def _():
            df_ref[...] = df
            part = coef * jnp.sum(dxv * f_ref[...], axis=0, keepdims=True)
            _acc_rows(dgt_ref, part, m % per == 0)

        ds = _nt(df, w_ref[...])
        av = a_ref[...]
        sig = jax.nn.sigmoid(av)
        da_ref[...] = (ds * u_ref[...] * (sig * (1.0 + av * (1.0 - sig)))).astype(BF)
        du_ref[...] = (ds * (av * sig)).astype(BF)

    row = pl.BlockSpec((TM, d), lambda m, j: (m, 0))
    blk = pl.BlockSpec((None, TM, fs), lambda m, j: (j, m, 0))
    ex = pl.BlockSpec((None, 1, d), lambda m, j: (m // per, 0, 0))
    return pl.pallas_call(
        body, name="ffn_bwd_ds", grid=(t // TM, N_CHIPS),
        in_specs=[row, ex, row,
                  pl.BlockSpec((None, None, fs, d), lambda m, j: (j, f_idx, 0, 0)),
                  blk, blk],
        out_specs=[blk, blk, row, ex],
        out_shape=[_sds((N_CHIPS, t, fs), BF), _sds((N_CHIPS, t, fs), BF), _sds((t, d), BF),
                   _sds((nb, 1, d), F32)],
        compiler_params=_cp(2))(dxo, gt, f, wd, a, u)


def _ffn_bwd_w(h, da, du, s, df):
    t, d = h.shape
    fs = da.shape[-1]

    def body(h_ref, da_ref, du_ref, s_ref, df_ref, dgu_ref, dwd_ref):
        kt = pl.program_id(1)
        hv = h_ref[...]
        pg = _tn(hv, da_ref[...])
        pu = _tn(hv, du_ref[...])
        pd = _tn(s_ref[...], df_ref[...])

        @pl.when(kt == 0)
        def _():
            dgu_ref[0] = pg
            dgu_ref[1] = pu
            dwd_ref[...] = pd

        @pl.when(kt != 0)
        def _():
            dgu_ref[0] += pg
            dgu_ref[1] += pu
            dwd_ref[...] += pd

    row = pl.BlockSpec((TM, d), lambda j, kt: (kt, 0))
    blk = pl.BlockSpec((None, TM, fs), lambda j, kt: (j, kt, 0))
    return pl.pallas_call(
        body, name="ffn_bwd_w", grid=(N_CHIPS, t // TM),
        in_specs=[row, blk, blk, blk, row],
        out_specs=[pl.BlockSpec((None, 2, d, fs), lambda j, kt: (j, 0, 0, 0)),
                   pl.BlockSpec((None, None, fs, d), lambda j, kt: (j, 0, 0, 0))],
        out_shape=[_sds((N_CHIPS, 2, d, fs), F32), _sds((N_CHIPS, 1, fs, d), F32)],
        compiler_params=_cp(2))(h, da, du, s, df)


def _ffn_bwd_dh(da, du, wgu, f_idx, x, g, sc, dxo, seq):
    _, t, fs = da.shape
    d = x.shape[-1]
    per = seq // TM
    nb = t // seq

    def body(da_ref, du_ref, w_ref, x_ref, g_ref, sc_ref, dxo_ref, dx_ref, dsh_ref, dsc_ref, dg_ref, acc):
        m = pl.program_id(0)
        j = pl.program_id(1)

        @pl.when(j == 0)
        def _():
            acc[...] = jnp.zeros_like(acc)

        acc[...] += _nt(da_ref[...], w_ref[0]) + _nt(du_ref[...], w_ref[1])

        @pl.when(j == N_CHIPS - 1)
        def _():
            dx, dsh, dsc, dg = _modnorm_bwd_tile(acc[...], x_ref[...], g_ref[...], sc_ref[...], dxo_ref[...])
            dx_ref[...] = dx
            _acc_rows(dsh_ref, dsh, m % per == 0)
            _acc_rows(dsc_ref, dsc, m % per == 0)
            _acc_rows(dg_ref, dg, m == 0)

    row = pl.BlockSpec((TM, d), lambda m, j: (m, 0))
    blk = pl.BlockSpec((None, TM, fs), lambda m, j: (j, m, 0))
    ex = pl.BlockSpec((None, 1, d), lambda m, j: (m // per, 0, 0))
    vec = pl.BlockSpec((1, d), lambda m, j: (0, 0))
    return pl.pallas_call(
        body, name="ffn_bwd_dh", grid=(t // TM, N_CHIPS),
        in_specs=[blk, blk, pl.BlockSpec((None, 2, d, fs), lambda m, j: (j, f_idx, 0, 0)),
                  row, vec, ex, row],
        out_specs=[row, ex, ex, vec],
        out_shape=[_sds((t, d), F32), _sds((nb, 1, d), F32), _sds((nb, 1, d), F32), _sds((1, d), F32)],
        scratch_shapes=[pltpu.VMEM((TM, d), F32)],
        compiler_params=_cp(2))(da, du, wgu, x, g, sc, dxo)


def _qkv_proj(h, w_in):
    t, d = h.shape
    wc = w_in.shape[-1]
    nt = 256
    per_chip = wc // nt
    n_tiles = N_CHIPS * per_chip
    per_out = D_GRP // nt

    def body(h_ref, w_ref, o_ref):
        o_ref[...] = _nn(h_ref[...], w_ref[...]).astype(BF)

    return pl.pallas_call(
        body, name="qkv_proj", grid=(n_tiles, t // TM),
        in_specs=[pl.BlockSpec((TM, d), lambda n, m: (m, 0)),
                  pl.BlockSpec((None, None, d, nt), lambda n, m: (n // per_chip, 0, 0, n % per_chip))],
        out_specs=pl.BlockSpec((None, TM, nt), lambda n, m: (n // per_out, m, n % per_out)),
        out_shape=_sds((6, t, D_GRP), BF), compiler_params=_cp(2))(h, w_in)


def _mix_out(on_sb, on_dil, w_out, x, gt, seq):
    t, d = x.shape
    per = seq // TM

    def body(a_ref, b_ref, w_ref, x_ref, gt_ref, t_ref, xo_ref):
        tv = _nn(a_ref[...], w_ref[0:D_GRP, :]) + _nn(b_ref[...], w_ref[D_GRP:2 * D_GRP, :])
        t_ref[...] = tv
        xo_ref[...] = x_ref[...] + gt_ref[...] * tv

    row = pl.BlockSpec((TM, d), lambda m: (m, 0))
    half = pl.BlockSpec((TM, D_GRP), lambda m: (m, 0))
    return pl.pallas_call(
        body, name="mix_out", grid=(t // TM,),
        in_specs=[half, half, pl.BlockSpec((2 * D_GRP, d), lambda m: (0, 0)), row,
                  pl.BlockSpec((None, 1, d), lambda m: (m // per, 0, 0))],
        out_specs=[row, row],
        out_shape=[_sds((t, d), F32), _sds((t, d), F32)],
        compiler_params=_cp(1))(on_sb, on_dil, w_out, x, gt)


def _sb_masks():
    lane = lax.broadcasted_iota(jnp.int32, (1, 2 * HEAD_DIM), 1)
    hm0 = lane < HEAD_DIM
    rel = lax.broadcasted_iota(jnp.int32, (TQ, KB), 0) - lax.broadcasted_iota(jnp.int32, (TQ, KB), 1)
    kr = lax.broadcasted_iota(jnp.int32, (KB, KB), 0)
    kc = lax.broadcasted_iota(jnp.int32, (KB, KB), 1)
    return hm0, rel, kr, kc


def _headnorm_pair(o, gv, hm0):
    o2 = o * o
    ms0 = jnp.sum(jnp.where(hm0, o2, 0.0), axis=-1, keepdims=True) * (1.0 / HEAD_DIM)
    ms1 = jnp.sum(jnp.where(hm0, 0.0, o2), axis=-1, keepdims=True) * (1.0 / HEAD_DIM)
    r = jnp.where(hm0, lax.rsqrt(ms0 + EPS), lax.rsqrt(ms1 + EPS))
    return (o * r) * gv


def _sb_fwd(qkv6, g_sb, nb, seq):
    nq = seq // TQ

    def body(q_ref, k_ref, v_ref, g_ref, o_ref, on_ref):
        qi = pl.program_id(2)
        hm0, rel, kr, kc = _sb_masks()
        upper = (kr > kc).astype(BF)
        qv = q_ref[...]
        outs = []
        for hh in range(2):
            hm = hm0 if hh == 0 else jnp.logical_not(hm0)
            qh = jnp.where(hm, qv, jnp.zeros_like(qv))

            def kbody(it, carry, qh=qh):
                c_l, acc = carry
                kj = qi - it
                ks = pl.multiple_of(kj * KB, KB)
                kb = k_ref[pl.ds(ks, KB), :]
                vb = v_ref[pl.ds(ks, KB), :]
                z = _nt(qh, kb) * SCALE
                causal = (rel + it * KB) > 0
                sp = _softplus(z)
                ln = jnp.where(causal, -sp, 0.0)
                suf = _nn2(ln, upper) + c_l
                w = jnp.where(causal, jnp.exp((z - sp) + suf), 0.0)
                acc = acc + _nn(w.astype(BF), vb)
                c_l = c_l + jnp.sum(ln, axis=-1, keepdims=True)
                return c_l, acc

            _, acc = lax.fori_loop(0, qi + 1, kbody,
                                   (jnp.zeros((TQ, 1), F32), jnp.zeros((TQ, 2 * HEAD_DIM), F32)))
            outs.append(acc)
        o = jnp.where(hm0, outs[0], outs[1])
        o_ref[...] = o
        on_ref[...] = _headnorm_pair(o, g_ref[...], hm0).astype(BF)

    w = 2 * HEAD_DIM
    full = lambda i: pl.BlockSpec((None, None, seq, w), lambda b, hp, q: (i, b, 0, hp))
    qblk = pl.BlockSpec((None, None, TQ, w), lambda b, hp, q: (0, b, q, hp))
    oblk = pl.BlockSpec((None, TQ, w), lambda b, hp, q: (b, q, hp))
    return pl.pallas_call(
        body, name="sb_fwd", grid=(nb, N_HEADS // 2, nq),
        in_specs=[qblk, full(1), full(2), pl.BlockSpec((1, w), lambda b, hp, q: (0, hp))],
        out_specs=[oblk, oblk],
        out_shape=[_sds((nb, seq, D_GRP), F32), _sds((nb, seq, D_GRP), BF)],
        compiler_params=_cp(3))(qkv6, qkv6, qkv6, g_sb)


def _sb_bwd(qkv6, o_raw, do, nb, seq):
    nq = seq // TQ
    nk = seq // KB

    def body(q_ref, k_ref, v_ref, do_ref, out_ref, dk_acc, dv_acc, car):
        qi = pl.program_id(2)
        hm0, rel, kr, kc = _sb_masks()
        upper = (kr > kc).astype(BF)
        lower = (kr < kc).astype(BF)

        @pl.when(qi == 0)
        def _():
            dk_acc[...] = jnp.zeros_like(dk_acc)
            dv_acc[...] = jnp.zeros_like(dv_acc)

        qv = q_ref[...]
        dov = do_ref[...]
        dqs = []
        for hh in range(2):
            hm = hm0 if hh == 0 else jnp.logical_not(hm0)
            qh = jnp.where(hm, qv, jnp.zeros_like(qv))
            doh = jnp.where(hm, dov, 0.0).astype(BF)

            def abody(it, c_l, qh=qh):
                kj = qi - it
                ks = pl.multiple_of(kj * KB, KB)
                z = _nt(qh, k_ref[pl.ds(ks, KB), :]) * SCALE
                causal = (rel + it * KB) > 0
                ln = jnp.where(causal, -_softplus(z), 0.0)
                car[kj] = c_l
                return c_l + jnp.sum(ln, axis=-1, keepdims=True)

            lax.fori_loop(0, qi + 1, abody, jnp.zeros((TQ, 1), F32))

            def bbody(kj, carry, qh=qh, doh=doh):
                c_g, dq = carry
                ks = pl.multiple_of(kj * KB, KB)
                kb = k_ref[pl.ds(ks, KB), :]
                vb = v_ref[pl.ds(ks, KB), :]
                z = _nt(qh, kb) * SCALE
                causal = (rel + (qi - kj) * KB) > 0
                sp = _softplus(z)
                ln = jnp.where(causal, -sp, 0.0)
                lsz = z - sp
                suf = _nn2(ln, upper) + car[kj]
                w = jnp.where(causal, jnp.exp(lsz + suf), 0.0)
                g = w * _nt(doh, vb)
                pre = _nn2(g, lower) + c_g
                sig = jnp.exp(lsz)
                dz = jnp.where(causal, g * (1.0 - sig) - sig * pre, 0.0) * SCALE
                dzb = dz.astype(BF)
                dq = dq + _nn(dzb, kb)
                dk_acc[pl.ds(ks, KB), :] += _tn(dzb, qh)
                dv_acc[pl.ds(ks, KB), :] += _tn(w.astype(BF), doh)
                c_g = c_g + jnp.sum(g, axis=-1, keepdims=True)
                return c_g, dq

            _, dq = lax.fori_loop(0, qi + 1, bbody,
                                  (jnp.zeros((TQ, 1), F32), jnp.zeros((TQ, 2 * HEAD_DIM), F32)))
            dqs.append(dq)
        dq = jnp.where(hm0, dqs[0], dqs[1])
        out_ref[0, pl.ds(pl.multiple_of(qi * TQ, TQ), TQ), :] = dq.astype(BF)

        @pl.when(qi == nq - 1)
        def _():
            out_ref[1] = dk_acc[...].astype(BF)
            out_ref[2] = dv_acc[...].astype(BF)

    w = 2 * HEAD_DIM
    full = lambda i: pl.BlockSpec((None, None, seq, w), lambda b, hp, q: (i, b, 0, hp))
    qblk = pl.BlockSpec((None, None, TQ, w), lambda b, hp, q: (0, b, q, hp))
    oblk = pl.BlockSpec((None, TQ, w), lambda b, hp, q: (b, q, hp))
    del o_raw
    return pl.pallas_call(
        body, name="sb_bwd", grid=(nb, N_HEADS // 2, nq),
        in_specs=[qblk, full(1), full(2), oblk],
        out_specs=pl.BlockSpec((3, None, seq, w), lambda b, hp, q: (0, b, 0, hp)),
        out_shape=_sds((6, nb, seq, D_GRP), BF),
        scratch_shapes=[pltpu.VMEM((seq, w), F32), pltpu.VMEM((seq, w), F32), pltpu.VMEM((nk, TQ, 1), F32)],
        compiler_params=_cp(3))(qkv6, qkv6, qkv6, do)


def _t5_bucket(n):
    max_exact = N_BUCKETS // 2
    nf = np.maximum(n, 1).astype(np.float32)
    large = max_exact + (np.log(nf / max_exact) / math.log(MAX_DISTANCE / max_exact)
                         * (N_BUCKETS - max_exact)).astype(np.int32)
    large = np.minimum(large, N_BUCKETS - 1)
    return np.where(n < max_exact, n, large).astype(np.int32)


def _bucket_map(dilation):
    step = BLOCK + np.arange(BLOCK)[:, None] - np.arange(2 * BLOCK)[None, :]
    return _t5_bucket(np.clip(step, 0, N_STEPS) * dilation)


def _dil_masks():
    lane = lax.broadcasted_iota(jnp.int32, (1, 2 * HEAD_DIM), 1)
    hm0 = lane < HEAD_DIM
    iq = lax.broadcasted_iota(jnp.int32, (BLOCK, BLOCK), 0)
    ik = lax.broadcasted_iota(jnp.int32, (BLOCK, BLOCK), 1)
    return hm0, ik <= iq, ik >= iq


def _dil_probs(qh, kc, kp, b_ref, hh, valid_c, valid_p):
    zc = _nt(qh, kc) * SCALE + b_ref[hh, :, BLOCK:2 * BLOCK]
    zp = _nt(qh, kp) * SCALE + b_ref[hh, :, 0:BLOCK]
    zc = jnp.where(valid_c, zc, NEG_INF)
    zp = jnp.where(valid_p, zp, NEG_INF)
    m = jnp.maximum(jnp.max(zc, axis=-1, keepdims=True), jnp.max(zp, axis=-1, keepdims=True))
    ec = jnp.exp(zc - m)
    ep = jnp.exp(zp - m)
    den = jnp.sum(ec, axis=-1, keepdims=True) + jnp.sum(ep, axis=-1, keepdims=True)
    return ec, ep, den, m


def _dil_fwd(qkv6r, bias, nb, sub_len, dilation):
    n_blk = sub_len // BLOCK
    w = 2 * HEAD_DIM

    def body(q_ref, k_ref, v_ref, b_ref, o_ref, l_ref):
        hm0, valid_c, valid_p0 = _dil_masks()

        def nbody(n, carry):
            rs = pl.multiple_of(n * BLOCK, BLOCK)
            ps = pl.multiple_of(jnp.maximum(n - 1, 0) * BLOCK, BLOCK)
            qv = q_ref[pl.ds(rs, BLOCK), :]
            kc = k_ref[pl.ds(rs, BLOCK), :]
            kp = k_ref[pl.ds(ps, BLOCK), :]
            vc = v_ref[pl.ds(rs, BLOCK), :]
            vp = v_ref[pl.ds(ps, BLOCK), :]
            valid_p = jnp.logical_and(valid_p0, n > 0)
            os, ls = [], []
            for hh in range(2):
                hm = hm0 if hh == 0 else jnp.logical_not(hm0)
                qh = jnp.where(hm, qv, jnp.zeros_like(qv))
                ec, ep, den, m = _dil_probs(qh, kc, kp, b_ref, hh, valid_c, valid_p)
                os.append((_nn(ec.astype(BF), vc) + _nn(ep.astype(BF), vp)) / den)
                ls.append(m + jnp.log(den))
            o_ref[pl.ds(rs, BLOCK), :] = jnp.where(hm0, os[0], os[1])
            l_ref[pl.ds(rs, BLOCK), :] = jnp.where(hm0, ls[0], ls[1])
            return carry

        lax.fori_loop(0, n_blk, nbody, 0)

    seqblk = lambda i: pl.BlockSpec((None, None, sub_len, w), lambda b, g: (i, b, 0, g))
    oblk = pl.BlockSpec((None, sub_len, w), lambda b, g: (b, 0, g))
    shp = _sds((nb, sub_len, dilation * D_GRP), F32)
    return pl.pallas_call(
        body, name="dil_fwd_%d" % dilation, grid=(nb, dilation * (N_HEADS // 2)),
        in_specs=[seqblk(3), seqblk(4), seqblk(5),
                  pl.BlockSpec((2, BLOCK, 2 * BLOCK), lambda b, g: (g % (N_HEADS // 2), 0, 0))],
        out_specs=[oblk, oblk], out_shape=[shp, shp],
        compiler_params=_cp(2))(qkv6r, qkv6r, qkv6r, bias)


def _dil_bwd(qkv6r, bias, do_c, dd_c, nb, sub_len, dilation):
    n_blk = sub_len // BLOCK
    w = 2 * HEAD_DIM
    hp_n = N_HEADS // 2

    def body(q_ref, k_ref, v_ref, b_ref, do_ref, dd_ref, out_ref, a_ref, dk_acc, dv_acc):
        hm0, valid_c, valid_p0 = _dil_masks()
        first = jnp.logical_and(pl.program_id(1) == 0, pl.program_id(2) == 0)

        @pl.when(first)
        def _():
            a_ref[...] = jnp.zeros_like(a_ref)

        dk_acc[...] = jnp.zeros_like(dk_acc)
        dv_acc[...] = jnp.zeros_like(dv_acc)

        def nbody(n, carry):
            rs = pl.multiple_of(n * BLOCK, BLOCK)
            ps = pl.multiple_of(jnp.maximum(n - 1, 0) * BLOCK, BLOCK)
            qv = q_ref[pl.ds(rs, BLOCK), :]
            kc = k_ref[pl.ds(rs, BLOCK), :]
            kp = k_ref[pl.ds(ps, BLOCK), :]
            vc = v_ref[pl.ds(rs, BLOCK), :]
            vp = v_ref[pl.ds(ps, BLOCK), :]
            dov = do_ref[pl.ds(rs, BLOCK), :]
            ddv = dd_ref[pl.ds(rs, BLOCK), :]
            valid_p = jnp.logical_and(valid_p0, n > 0)
            dq = jnp.zeros((BLOCK, w), F32)
            for hh in range(2):
                hm = hm0 if hh == 0 else jnp.logical_not(hm0)
                qh = jnp.where(hm, qv, jnp.zeros_like(qv))
                doh = jnp.where(hm, dov, 0.0).astype(BF)
                ddh = jnp.sum(jnp.where(hm, ddv, 0.0), axis=-1, keepdims=True) * (1.0 / HEAD_DIM)
                ec, ep, den, _ = _dil_probs(qh, kc, kp, b_ref, hh, valid_c, valid_p)
                inv = 1.0 / den
                pc = ec * inv
                pp = ep * inv
                dzc = pc * (_nt(doh, vc) + ddh)
                dzp = pp * (_nt(doh, vp) + ddh)
                a_ref[hh, :, BLOCK:2 * BLOCK] += dzc
                a_ref[hh, :, 0:BLOCK] += dzp
                dzcb = (dzc * SCALE).astype(BF)
                dzpb = (dzp * SCALE).astype(BF)
                dq = jnp.where(hm, _nn(dzcb, kc) + _nn(dzpb, kp), dq)
                dk_acc[pl.ds(rs, BLOCK), :] += _tn(dzcb, qh)
                dk_acc[pl.ds(ps, BLOCK), :] += _tn(dzpb, qh)
                dv_acc[pl.ds(rs, BLOCK), :] += _tn(pc.astype(BF), doh)
                dv_acc[pl.ds(ps, BLOCK), :] += _tn(pp.astype(BF), doh)
            out_ref[0, pl.ds(rs, BLOCK), :] = dq
            return carry

        lax.fori_loop(0, n_blk, nbody, 0)
        out_ref[1] = dk_acc[...]
        out_ref[2] = dv_acc[...]

    col = lambda hp, b, r: r * hp_n + hp
    seqblk = lambda i: pl.BlockSpec((None, None, sub_len, w), lambda hp, b, r: (i, b, 0, col(hp, b, r)))
    oblk = pl.BlockSpec((None, sub_len, w), lambda hp, b, r: (b, 0, col(hp, b, r)))
    return pl.pallas_call(
        body, name="dil_bwd_%d" % dilation, grid=(hp_n, nb, dilation),
        in_specs=[seqblk(3), seqblk(4), seqblk(5),
                  pl.BlockSpec((2, BLOCK, 2 * BLOCK), lambda hp, b, r: (hp, 0, 0)), oblk, oblk],
        out_specs=[pl.BlockSpec((3, None, sub_len, w), lambda hp, b, r: (0, b, 0, col(hp, b, r))),
                   pl.BlockSpec((2, BLOCK, 2 * BLOCK), lambda hp, b, r: (hp, 0, 0))],
        out_shape=[_sds((3, nb, sub_len, dilation * D_GRP), F32), _sds((N_HEADS, BLOCK, 2 * BLOCK), F32)],
        scratch_shapes=[pltpu.VMEM((sub_len, w), F32), pltpu.VMEM((sub_len, w), F32)],
        compiler_params=_cp(3))(qkv6r, qkv6r, qkv6r, bias, do_c, dd_c)


def _group_ones():
    idx = np.arange(D_GRP) // HEAD_DIM
    return jnp.asarray((idx[:, None] == idx[None, :]).astype(np.float32), dtype=BF)


def _dil_alphas(l1, l4, l16):
    mx = jnp.maximum(jnp.maximum(l1, l4), l16)
    e1 = jnp.exp(l1 - mx)
    e4 = jnp.exp(l4 - mx)
    e16 = jnp.exp(l16 - mx)
    den = e1 + e4 + e16
    return e1 / den, e4 / den, e16 / den


def _dil_comb(os, ls, g_dil, ones_g):
    t = os[0].shape[0]

    def body(o1, l1, o4, l4, o16, l16, g_ref, m_ref, o_ref, on_ref):
        a1, a4, a16 = _dil_alphas(l1[...], l4[...], l16[...])
        o = a1 * o1[...] + a4 * o4[...] + a16 * o16[...]
        o_ref[...] = o
        ms = _nn2(o * o, m_ref[...]) * (1.0 / HEAD_DIM)
        on_ref[...] = ((o * lax.rsqrt(ms + EPS)) * g_ref[...]).astype(BF)

    blk = pl.BlockSpec((TM, D_GRP), lambda m: (m, 0))
    return pl.pallas_call(
        body, name="dil_comb", grid=(t // TM,),
        in_specs=[blk] * 6 + [pl.BlockSpec((1, D_GRP), lambda m: (0, 0)),
                              pl.BlockSpec((D_GRP, D_GRP), lambda m: (0, 0))],
        out_specs=[blk, blk],
        out_shape=[_sds((t, D_GRP), F32), _sds((t, D_GRP), BF)],
        compiler_params=_cp(1))(os[0], ls[0], os[1], ls[1], os[2], ls[2], g_dil, ones_g)


def _dil_comb_bwd(do, os, ls, ones_g):
    t = do.shape[0]

    def body(do_ref, o1, l1, o4, l4, o16, l16, m_ref, d1, d4, d16, e1, e4, e16):
        dov = do_ref[...]
        a1, a4, a16 = _dil_alphas(l1[...], l4[...], l16[...])
        mv = m_ref[...]
        sbar = a1 * _nn2(dov * o1[...], mv) + a4 * _nn2(dov * o4[...], mv) + a16 * _nn2(dov * o16[...], mv)
        d1[...] = a1 * dov
        d4[...] = a4 * dov
        d16[...] = a16 * dov
        e1[...] = -a1 * sbar
        e4[...] = -a4 * sbar
        e16[...] = -a16 * sbar

    blk = pl.BlockSpec((TM, D_GRP), lambda m: (m, 0))
    shp = _sds((t, D_GRP), F32)
    return pl.pallas_call(
        body, name="dil_comb_bwd", grid=(t // TM,),
        in_specs=[blk] * 7 + [pl.BlockSpec((D_GRP, D_GRP), lambda m: (0, 0))],
        out_specs=[blk] * 6, out_shape=[shp] * 6,
        compiler_params=_cp(1))(do, os[0], ls[0], os[1], ls[1], os[2], ls[2], ones_g)


def _dqkv_dil_sum(d1, d4, d16, dqkv6):
    t = d1.shape[1]

    def body(a, b, c, alias, o_ref):
        del alias
        o_ref[...] = (a[...] + b[...] + c[...]).astype(BF)

    blk = pl.BlockSpec((3, TM, D_GRP), lambda m: (0, m, 0))
    return pl.pallas_call(
        body, name="dqkv_dil_sum", grid=(t // TM,),
        in_specs=[blk, blk, blk, pl.BlockSpec(memory_space=pl.ANY)],
        out_specs=pl.BlockSpec((3, TM, D_GRP), lambda m: (1, m, 0)),
        out_shape=_sds((6, t, D_GRP), BF), input_output_aliases={3: 0},
        compiler_params=_cp(1))(d1, d4, d16, dqkv6)


def _relbias_grad(a_all, onehot):
    def body(a_ref, oh_ref, o_ref):
        acc = jnp.zeros((N_HEADS, N_BUCKETS), F32)
        for c in range(len(DIL_CONFIGS)):
            av = a_ref[c]
            hi = av.astype(BF)
            lo = (av - hi.astype(F32)).astype(BF)
            acc = acc + _nt(hi, oh_ref[c]) + _nt(lo, oh_ref[c])
        o_ref[...] = acc

    return pl.pallas_call(body, name="relbias_grad", out_shape=_sds((N_HEADS, N_BUCKETS), F32),
                          compiler_params=_cp())(a_all, onehot)


def _headnorm_bwd(dn, o, gv, mv):
    ms = _nn2(o * o, mv) * (1.0 / HEAD_DIM)
    r = lax.rsqrt(ms + EPS)
    nrm = o * r
    dg = jnp.sum(dn * nrm, axis=0, keepdims=True)
    dnn = dn * gv
    do = r * (dnn - nrm * (_nn2(dnn * nrm, mv) * (1.0 / HEAD_DIM)))
    return do, dg


def _mix_bwd_out(dx, gt, tv, w_out, o_sb, o_dil, on_sb, on_dil, g_sb, g_dil, ones_g, seq):
    t, d = dx.shape
    per = seq // TM
    nb = t // seq

    def body(dx_ref, gt_ref, t_ref, w_ref, osb, odl, onsb, ondl, gsb, gdl, m_ref,
             dosb, dodl, dgt_ref, dgsb, dgdl, dw_ref):
        m = pl.program_id(0)
        dxv = dx_ref[...]
        dt = (gt_ref[...] * dxv).astype(BF)
        _acc_rows(dgt_ref, jnp.sum(dxv * t_ref[...], axis=0, keepdims=True), m % per == 0)
        mv = m_ref[...]
        don_sb = _nt(dt, w_ref[0:D_GRP, :])
        don_dl = _nt(dt, w_ref[D_GRP:2 * D_GRP, :])
        do1, dg1 = _headnorm_bwd(don_sb, osb[...], gsb[...], mv)
        do2, dg2 = _headnorm_bwd(don_dl, odl[...], gdl[...], mv)
        dosb[...] = do1
        dodl[...] = do2
        _acc_rows(dgsb, dg1, m == 0)
        _acc_rows(dgdl, dg2, m == 0)
        p1 = _tn(onsb[...], dt)
        p2 = _tn(ondl[...], dt)

        @pl.when(m == 0)
        def _():
            dw_ref[0:D_GRP, :] = p1
            dw_ref[D_GRP:2 * D_GRP, :] = p2

        @pl.when(m != 0)
        def _():
            dw_ref[0:D_GRP, :] += p1
            dw_ref[D_GRP:2 * D_GRP, :] += p2

    row = pl.BlockSpec((TM, d), lambda m: (m, 0))
    half = pl.BlockSpec((TM, D_GRP), lambda m: (m, 0))
    ex = pl.BlockSpec((None, 1, d), lambda m: (m // per, 0, 0))
    gvec = pl.BlockSpec((1, D_GRP), lambda m: (0, 0))
    wblk = pl.BlockSpec((2 * D_GRP, d), lambda m: (0, 0))
    return pl.pallas_call(
        body, name="mix_bwd_out", grid=(t // TM,),
        in_specs=[row, ex, row, wblk, half, half, half, half, gvec, gvec,
                  pl.BlockSpec((D_GRP, D_GRP), lambda m: (0, 0))],
        out_specs=[half, half, ex, gvec, gvec, wblk],
        out_shape=[_sds((t, D_GRP), F32), _sds((t, D_GRP), F32), _sds((nb, 1, d), F32),
                   _sds((1, D_GRP), F32), _sds((1, D_GRP), F32), _sds((2 * D_GRP, d), F32)],
        compiler_params=_cp(1))(dx, gt, tv, w_out, o_sb, o_dil, on_sb, on_dil, g_sb, g_dil, ones_g)


def _dw_in(h, dqkv6):
    t, d = h.shape
    nt = 256
    per_chip = 3
    per_out = D_GRP // nt

    def body(h_ref, g_ref, o_ref):
        p = _tn(h_ref[...], g_ref[...])
        _acc_rows(o_ref, p, pl.program_id(1) == 0)

    return pl.pallas_call(
        body, name="dw_in", grid=(N_CHIPS * per_chip, t // TM),
        in_specs=[pl.BlockSpec((TM, d), lambda n, kt: (kt, 0)),
                  pl.BlockSpec((None, TM, nt), lambda n, kt: (n // per_out, kt, n % per_out))],
        out_specs=pl.BlockSpec((None, None, d, nt), lambda n, kt: (n // per_chip, 0, 0, n % per_chip)),
        out_shape=_sds((N_CHIPS, 1, d, per_chip * nt), F32),
        compiler_params=_cp(2))(h, dqkv6)


def _mix_bwd_dh(dqkv6, w_in, x, g, sc, dxo, seq):
    _, t, _ = dqkv6.shape
    d = x.shape[-1]
    nt = 256
    per_chip = 3
    per_out = D_GRP // nt
    n_tiles = N_CHIPS * per_chip
    per = seq // TM
    nb = t // seq

    def body(g6_ref, w_ref, x_ref, g_ref, sc_ref, dxo_ref, dx_ref, dsh_ref, dsc_ref, dg_ref, acc):
        m = pl.program_id(0)
        n = pl.program_id(1)

        @pl.when(n == 0)
        def _():
            acc[...] = jnp.zeros_like(acc)

        acc[...] += _nt(g6_ref[...], w_ref[...])

        @pl.when(n == n_tiles - 1)
        def _():
            dx, dsh, dsc, dg = _modnorm_bwd_tile(acc[...], x_ref[...], g_ref[...], sc_ref[...], dxo_ref[...])
            dx_ref[...] = dx
            _acc_rows(dsh_ref, dsh, m % per == 0)
            _acc_rows(dsc_ref, dsc, m % per == 0)
            _acc_rows(dg_ref, dg, m == 0)

    row = pl.BlockSpec((TM, d), lambda m, n: (m, 0))
    ex = pl.BlockSpec((None, 1, d), lambda m, n: (m // per, 0, 0))
    vec = pl.BlockSpec((1, d), lambda m, n: (0, 0))
    return pl.pallas_call(
        body, name="mix_bwd_dh", grid=(t // TM, n_tiles),
        in_specs=[pl.BlockSpec((None, TM, nt), lambda m, n: (n // per_out, m, n % per_out)),
                  pl.BlockSpec((None, None, d, nt), lambda m, n: (n // per_chip, 0, 0, n % per_chip)),
                  row, vec, ex, row],
        out_specs=[row, ex, ex, vec],
        out_shape=[_sds((t, d), F32), _sds((nb, 1, d), F32), _sds((nb, 1, d), F32), _sds((1, d), F32)],
        scratch_shapes=[pltpu.VMEM((TM, d), F32)],
        compiler_params=_cp(2))(dqkv6, w_in, x, g, sc, dxo)


def _final_loss(x, g, target):
    t, d = x.shape
    steps = t // TM

    def body(x_ref, g_ref, t_ref, dx_ref, dg_ref, loss_ref, lacc):
        m = pl.program_id(0)
        xv = x_ref[...]
        gv = g_ref[...]
        r = lax.rsqrt(jnp.mean(xv * xv, axis=-1, keepdims=True) + EPS)
        n = xv * r
        err = n * gv - t_ref[...]
        dy = err * (1.0 / d)
        _acc_rows(dg_ref, jnp.sum(dy * n, axis=0, keepdims=True), m == 0)
        dn = dy * gv
        dx_ref[...] = r * (dn - n * jnp.mean(dn * n, axis=-1, keepdims=True))
        _acc_rows(lacc, jnp.sum(err * err, axis=0, keepdims=True), m == 0)

        @pl.when(m == steps - 1)
        def _():
            tot = jnp.sum(lacc[...], axis=-1, keepdims=True) * (0.5 / d)
            loss_ref[...] = jnp.broadcast_to(tot, (1, 128))

    row = pl.BlockSpec((TM, d), lambda m: (m, 0))
    vec = pl.BlockSpec((1, d), lambda m: (0, 0))
    return pl.pallas_call(
        body, name="final_loss", grid=(steps,),
        in_specs=[row, vec, row],
        out_specs=[row, vec, pl.BlockSpec((1, 128), lambda m: (0, 0))],
        out_shape=[_sds((t, d), F32), _sds((1, d), F32), _sds((1, 128), F32)],
        scratch_shapes=[pltpu.VMEM((1, d), F32)],
        compiler_params=_cp(1))(x, g, target)


def _row_tile(rows, cols):
    best = rows
    for tr in range(8, rows + 1, 8):
        if rows % tr == 0 and tr * cols * 4 <= (1 << 20):
            best = tr
    if best * cols * 4 > (1 << 21):
        best = 8
    return best


def _adamw(w, g_arr, g_sel, m, v):
    rows, cols = w.shape
    tr = _row_tile(rows, cols)
    b1c = 1.0 - ADAM_B1 ** ADAM_STEP
    b2c = 1.0 - ADAM_B2 ** ADAM_STEP

    def body(w_ref, g_ref, m_ref, v_ref, go_ref, d_ref, mo_ref, vo_ref):
        gv = g_ref[...]
        mn = ADAM_B1 * m_ref[...] + (1.0 - ADAM_B1) * gv
        vn = ADAM_B2 * v_ref[...] + (1.0 - ADAM_B2) * (gv * gv)
        go_ref[...] = gv
        mo_ref[...] = mn
        vo_ref[...] = vn
        d_ref[...] = -ADAM_LR * ((mn / b1c) / (jnp.sqrt(vn / b2c) + ADAM_EPS) + ADAM_WD * w_ref[...])

    blk = pl.BlockSpec((tr, cols), lambda i: (i, 0))
    shp = _sds((rows, cols), F32)
    return pl.pallas_call(
        body, name="adamw", grid=(rows // tr,),
        in_specs=[blk, pl.BlockSpec((None, tr, cols), lambda i: (g_sel, i, 0)), blk, blk],
        out_specs=[blk] * 4, out_shape=[shp] * 4,
        compiler_params=_cp(1))(w, g_arr, m, v)


def _flip(v, bit):
    return 1 - v if bit else v


def _my_place():
    x, y, c = lax.axis_index("x"), lax.axis_index("y"), lax.axis_index("c")
    return x, y, c


def _ada_fwd(c_pad, w_ada, b_shard):
    d = c_pad.shape[-1]
    cols = w_ada.shape[-1]
    chunk = 384

    def body(c_ref, w_ref, b_ref, call_ref, mod_ref, part, s1, r1, s2, r2):
        x, y, c = _my_place()
        dev = 4 * x + 2 * y + c
        chip = 2 * x + y
        call_ref[dev] = c_ref[...]

        def c_copy(k):
            px, py, pc = _flip(x, (k >> 2) & 1), _flip(y, (k >> 1) & 1), _flip(c, k & 1)
            return px, py, pc

        sends = []
        for k in range(1, N_DEV):
            px, py, pc = c_copy(k)
            cp = pltpu.make_async_remote_copy(src_ref=c_ref, dst_ref=call_ref.at[dev], send_sem=s1.at[k - 1],
                                              recv_sem=r1.at[k - 1], device_id=(px, py, pc), device_id_type=MESH)
            cp.start()
            sends.append(cp)
        for k in range(1, N_DEV):
            px, py, pc = c_copy(k)
            pltpu.make_async_remote_copy(src_ref=c_ref, dst_ref=call_ref.at[4 * px + 2 * py + pc],
                                         send_sem=s1.at[k - 1], recv_sem=r1.at[k - 1],
                                         device_id=(px, py, pc), device_id_type=MESH).wait_recv()
        for cp in sends:
            cp.wait_send()

        cs = call_ref[...].reshape(N_DEV * 8, d)
        sc = (cs * jax.nn.sigmoid(cs)).astype(BF)
        for n0 in range(0, cols, chunk):
            blk = _nn(sc, w_ref[:, n0:n0 + chunk].astype(BF)) + b_ref[:, n0:n0 + chunk]
            part[:, :, n0:n0 + chunk] = blk.reshape(N_DEV, 8, chunk)

        mod_ref[chip] = part[dev]
        sends = []
        for kk in range(1, N_CHIPS):
            px, py = _flip(x, (kk >> 1) & 1), _flip(y, kk & 1)
            cp = pltpu.make_async_remote_copy(src_ref=part.at[4 * px + 2 * py + c], dst_ref=mod_ref.at[chip],
                                              send_sem=s2.at[kk - 1], recv_sem=r2.at[kk - 1],
                                              device_id=(px, py, c), device_id_type=MESH)
            cp.start()
            sends.append(cp)
        for kk in range(1, N_CHIPS):
            px, py = _flip(x, (kk >> 1) & 1), _flip(y, kk & 1)
            pltpu.make_async_remote_copy(src_ref=part.at[dev], dst_ref=mod_ref.at[2 * px + py],
                                         send_sem=s2.at[kk - 1], recv_sem=r2.at[kk - 1],
                                         device_id=(px, py, c), device_id_type=MESH).wait_recv()
        for cp in sends:
            cp.wait_send()

    return pl.pallas_call(
        body, name="ada_fwd",
        out_shape=[_sds((N_DEV, 8, d), F32), _sds((N_CHIPS, 8, cols), F32)],
        scratch_shapes=[pltpu.VMEM((N_DEV, 8, cols), F32),
                        pltpu.SemaphoreType.DMA((N_DEV - 1,)), pltpu.SemaphoreType.DMA((N_DEV - 1,)),
                        pltpu.SemaphoreType.DMA((N_CHIPS - 1,)), pltpu.SemaphoreType.DMA((N_CHIPS - 1,))],
        compiler_params=_cp())(c_pad, w_ada, b_shard)


def _ag_weights(shards):
    n = len(shards)
    any_spec = pl.BlockSpec(memory_space=pl.ANY)

    def body(*refs):
        srcs, outs = refs[:n], refs[n:2 * n]
        lsem, s_ici, r_ici, s_d2d, r_d2d = refs[2 * n:]
        x, y, c = _my_place()
        chip = 2 * x + y
        others = [(_flip(x, (kk >> 1) & 1), _flip(y, kk & 1)) for kk in range(1, N_CHIPS)]

        def half(b, which):
            hr = shards[b].shape[1] // 2
            return pl.ds(pl.multiple_of(which * hr, 16), hr)

        local, sends = [], []
        for b in range(n):
            cp = pltpu.make_async_copy(srcs[b], outs[b].at[chip], lsem.at[b])
            cp.start()
            local.append(cp)
            for i, (px, py) in enumerate(others):
                cp = pltpu.make_async_remote_copy(
                    src_ref=srcs[b].at[:, half(b, c), :], dst_ref=outs[b].at[chip, :, half(b, c), :],
                    send_sem=s_ici.at[3 * b + i], recv_sem=r_ici.at[3 * b + i],
                    device_id=(px, py, c), device_id_type=MESH)
                cp.start()
                sends.append(cp)
        for b in range(n):
            for i, (px, py) in enumerate(others):
                landed = outs[b].at[2 * px + py, :, half(b, c), :]
                pltpu.make_async_remote_copy(
                    src_ref=landed, dst_ref=landed, send_sem=s_ici.at[3 * b + i], recv_sem=r_ici.at[3 * b + i],
                    device_id=(px, py, c), device_id_type=MESH).wait_recv()
                cp = pltpu.make_async_remote_copy(
                    src_ref=landed, dst_ref=landed, send_sem=s_d2d.at[3 * b + i], recv_sem=r_d2d.at[3 * b + i],
                    device_id=(x, y, 1 - c), device_id_type=MESH)
                cp.start()
                sends.append(cp)
        for b in range(n):
            for i, (px, py) in enumerate(others):
                got = outs[b].at[2 * px + py, :, half(b, 1 - c), :]
                pltpu.make_async_remote_copy(
                    src_ref=got, dst_ref=got, send_sem=s_d2d.at[3 * b + i], recv_sem=r_d2d.at[3 * b + i],
                    device_id=(x, y, 1 - c), device_id_type=MESH).wait_recv()
        for cp in sends:
            cp.wait_send()
        for cp in local:
            cp.wait()

    return pl.pallas_call(
        body, name="ag_weights",
        in_specs=[any_spec] * n, out_specs=[any_spec] * n,
        out_shape=[_sds((N_CHIPS,) + s.shape, s.dtype) for s in shards],
        scratch_shapes=[pltpu.SemaphoreType.DMA((n,)),
                        pltpu.SemaphoreType.DMA((3 * n,)), pltpu.SemaphoreType.DMA((3 * n,)),
                        pltpu.SemaphoreType.DMA((3 * n,)), pltpu.SemaphoreType.DMA((3 * n,))],
        compiler_params=_cp())(*shards)


def _rs_d2d(grads):
    n = len(grads)
    any_spec = pl.BlockSpec(memory_space=pl.ANY)

    def body(*refs):
        srcs, lands = refs[:n], refs[n:2 * n]
        ssem, rsem = refs[2 * n:]
        x, y, c = _my_place()
        sends = []
        for b in range(n):
            hr = grads[b].shape[2] // 2
            theirs = pl.ds(pl.multiple_of((1 - c) * hr, 8), hr)
            cp = pltpu.make_async_remote_copy(
                src_ref=srcs[b].at[:, :, theirs, :], dst_ref=lands[b], send_sem=ssem.at[b], recv_sem=rsem.at[b],
                device_id=(x, y, 1 - c), device_id_type=MESH)
            cp.start()
            sends.append(cp)
        for cp in sends:
            cp.wait()

    return pl.pallas_call(
        body, name="rs_d2d",
        in_specs=[any_spec] * n, out_specs=[any_spec] * n,
        out_shape=[_sds(g.shape[:2] + (g.shape[2] // 2, g.shape[3]), F32) for g in grads],
        scratch_shapes=[pltpu.SemaphoreType.DMA((n,)), pltpu.SemaphoreType.DMA((n,))],
        compiler_params=_cp())(*grads)


def _add_halves(core, g, land):
    nchip, ng, rows, cols = g.shape
    hr = rows // 2
    tr = _row_tile(hr, cols)
    steps = hr // tr

    def body(core_ref, g_ref, l_ref, o_ref):
        del core_ref
        o_ref[...] = (g_ref[...] + l_ref[...]).astype(BF)

    return pl.pallas_call(
        body, name="add_halves",
        grid_spec=pltpu.PrefetchScalarGridSpec(
            num_scalar_prefetch=1, grid=(nchip, ng, steps),
            in_specs=[pl.BlockSpec((None, None, tr, cols), lambda j, a, i, cr: (j, a, cr[0] * steps + i, 0)),
                      pl.BlockSpec((None, None, tr, cols), lambda j, a, i, cr: (j, a, i, 0))],
            out_specs=pl.BlockSpec((None, None, tr, cols), lambda j, a, i, cr: (j, a, i, 0))),
        out_shape=_sds((nchip, ng, hr, cols), BF),
        compiler_params=_cp(3))(core, g, land)


def _rs_ici(parts):
    n = len(parts)
    any_spec = pl.BlockSpec(memory_space=pl.ANY)

    def body(*refs):
        srcs, lands = refs[:n], refs[n:2 * n]
        lsem, ssem, rsem = refs[2 * n:]
        x, y, c = _my_place()
        chip = 2 * x + y
        others = [(_flip(x, (kk >> 1) & 1), _flip(y, kk & 1)) for kk in range(1, N_CHIPS)]
        local, sends = [], []
        for b in range(n):
            cp = pltpu.make_async_copy(srcs[b].at[chip], lands[b].at[chip], lsem.at[b])
            cp.start()
            local.append(cp)
            for i, (px, py) in enumerate(others):
                cp = pltpu.make_async_remote_copy(
                    src_ref=srcs[b].at[2 * px + py], dst_ref=lands[b].at[chip],
                    send_sem=ssem.at[3 * b + i], recv_sem=rsem.at[3 * b + i],
                    device_id=(px, py, c), device_id_type=MESH)
                cp.start()
                sends.append(cp)
        for b in range(n):
            for i, (px, py) in enumerate(others):
                slot = lands[b].at[2 * px + py]
                pltpu.make_async_remote_copy(
                    src_ref=slot, dst_ref=slot, send_sem=ssem.at[3 * b + i], recv_sem=rsem.at[3 * b + i],
                    device_id=(px, py, c), device_id_type=MESH).wait_recv()
        for cp in sends:
            cp.wait_send()
        for cp in local:
            cp.wait()

    return pl.pallas_call(
        body, name="rs_ici",
        in_specs=[any_spec] * n, out_specs=[any_spec] * n,
        out_shape=[_sds(p.shape, p.dtype) for p in parts],
        scratch_shapes=[pltpu.SemaphoreType.DMA((n,)),
                        pltpu.SemaphoreType.DMA((3 * n,)), pltpu.SemaphoreType.DMA((3 * n,))],
        compiler_params=_cp())(*parts)


def _sum_chips(land):
    nchip, ng, hr, cols = land.shape
    tr = _row_tile(hr, cols)

    def body(l_ref, o_ref):
        acc = l_ref[0].astype(F32)
        for q in range(1, nchip):
            acc = acc + l_ref[q].astype(F32)
        o_ref[...] = acc

    return pl.pallas_call(
        body, name="sum_chips", grid=(ng, hr // tr),
        in_specs=[pl.BlockSpec((nchip, None, tr, cols), lambda a, i: (0, a, i, 0))],
        out_specs=pl.BlockSpec((None, tr, cols), lambda a, i: (a, i, 0)),
        out_shape=_sds((ng, hr, cols), F32),
        compiler_params=_cp(2))(land)


def _rs_final(halves):
    n = len(halves)
    any_spec = pl.BlockSpec(memory_space=pl.ANY)

    def body(*refs):
        srcs, outs = refs[:n], refs[n:2 * n]
        lsem, ssem, rsem = refs[2 * n:]
        x, y, c = _my_place()
        ops = []
        for b in range(n):
            hr = halves[b].shape[1]
            mine = pl.ds(pl.multiple_of(c * hr, 8), hr)
            cp = pltpu.make_async_copy(srcs[b], outs[b].at[:, mine, :], lsem.at[b])
            cp.start()
            ops.append(cp)
            cp = pltpu.make_async_remote_copy(
                src_ref=srcs[b], dst_ref=outs[b].at[:, mine, :], send_sem=ssem.at[b], recv_sem=rsem.at[b],
                device_id=(x, y, 1 - c), device_id_type=MESH)
            cp.start()
            ops.append(cp)
        for cp in ops:
            cp.wait()

    return pl.pallas_call(
        body, name="rs_final",
        in_specs=[any_spec] * n, out_specs=[any_spec] * n,
        out_shape=[_sds((h.shape[0], 2 * h.shape[1], h.shape[2]), F32) for h in halves],
        scratch_shapes=[pltpu.SemaphoreType.DMA((n,)), pltpu.SemaphoreType.DMA((n,)), pltpu.SemaphoreType.DMA((n,))],
        compiler_params=_cp())(*halves)


def _small_sync(smalls, dmod_blk, c_all):
    d = c_all.shape[-1]
    cols = dmod_blk.shape[-1]
    chunk = 384

    def body(sm_ref, dm_ref, c_ref, sum_ref, gw_ref, sm_all, dm_all, ssem, rsem):
        x, y, c = _my_place()
        dev = 4 * x + 2 * y + c
        chip = 2 * x + y
        sm_all[dev] = sm_ref[...]
        dm_all[dev] = dm_ref[chip]
        sends = []
        for k in range(1, N_DEV):
            px, py, pc = _flip(x, (k >> 2) & 1), _flip(y, (k >> 1) & 1), _flip(c, k & 1)
            a = pltpu.make_async_remote_copy(src_ref=sm_ref, dst_ref=sm_all.at[dev], send_sem=ssem.at[2 * (k - 1)],
                                             recv_sem=rsem.at[2 * (k - 1)], device_id=(px, py, pc),
                                             device_id_type=MESH)
            b = pltpu.make_async_remote_copy(src_ref=dm_ref.at[2 * px + py], dst_ref=dm_all.at[dev],
                                             send_sem=ssem.at[2 * (k - 1) + 1], recv_sem=rsem.at[2 * (k - 1) + 1],
                                             device_id=(px, py, pc), device_id_type=MESH)
            a.start()
            b.start()
            sends += [a, b]
        for k in range(1, N_DEV):
            px, py, pc = _flip(x, (k >> 2) & 1), _flip(y, (k >> 1) & 1), _flip(c, k & 1)
            pdev = 4 * px + 2 * py + pc
            pltpu.make_async_remote_copy(src_ref=sm_ref, dst_ref=sm_all.at[pdev], send_sem=ssem.at[2 * (k - 1)],
                                         recv_sem=rsem.at[2 * (k - 1)], device_id=(px, py, pc),
                                         device_id_type=MESH).wait_recv()
            pltpu.make_async_remote_copy(src_ref=dm_ref.at[chip], dst_ref=dm_all.at[pdev],
                                         send_sem=ssem.at[2 * (k - 1) + 1], recv_sem=rsem.at[2 * (k - 1) + 1],
                                         device_id=(px, py, pc), device_id_type=MESH).wait_recv()
        for cp in sends:
            cp.wait_send()

        tot = sm_all[0]
        for q in range(1, N_DEV):
            tot = tot + sm_all[q]
        sum_ref[...] = tot

        cs = c_ref[...].reshape(N_DEV * 8, d)
        sc = (cs * jax.nn.sigmoid(cs)).astype(BF)
        for n0 in range(0, cols, chunk):
            dmv = dm_all[:, :, n0:n0 + chunk].reshape(N_DEV * 8, chunk).astype(BF)
            gw_ref[:, n0:n0 + chunk] = _tn(sc, dmv)

    return pl.pallas_call(
        body, name="small_sync",
        out_shape=[_sds(smalls.shape, F32), _sds((d, cols), F32)],
        scratch_shapes=[pltpu.VMEM((N_DEV,) + smalls.shape, F32), pltpu.VMEM((N_DEV, 8, cols), F32),
                        pltpu.SemaphoreType.DMA((2 * (N_DEV - 1),)), pltpu.SemaphoreType.DMA((2 * (N_DEV - 1),))],
        compiler_params=_cp())(smalls, dmod_blk, c_all)


def _bucket_onehot():
    maps = np.stack([_bucket_map(dil).reshape(-1) for _, dil in DIL_CONFIGS])
    return (jnp.asarray(maps)[:, None, :] == jnp.arange(N_BUCKETS, dtype=jnp.int32)[None, :, None]).astype(BF)


def _dil_bias(rel_t, onehot):
    def body(r_ref, oh_ref, o_ref):
        rv = r_ref[...]
        hi = rv.astype(BF)
        lo = (rv - hi.astype(F32)).astype(BF)
        for c in range(len(DIL_CONFIGS)):
            o_ref[c] = _nn(hi, oh_ref[c]) + _nn(lo, oh_ref[c])

    return pl.pallas_call(body, name="dil_bias",
                          out_shape=_sds((len(DIL_CONFIGS), N_HEADS, BLOCK * 2 * BLOCK), F32),
                          compiler_params=_cp())(rel_t, onehot)


def _rowsum8(a):
    def body(a_ref, o_ref):
        o_ref[...] = jnp.sum(a_ref[...], axis=0, keepdims=True)

    return pl.pallas_call(body, name="rowsum8", out_shape=_sds((1, a.shape[1]), F32), compiler_params=_cp())(a)


def _local_step(x, mod, target, wgu, wd, w_in, w_out, gains, rel_bias):
    nb, seq, d = x.shape
    t = nb * seq
    x0 = x.reshape(t, d)
    tgt = target.reshape(t, d)
    md = [mod[:, i:i + 1, :] for i in range(N_MOD)]
    sh1, sc1, gt1, sh2, sc2, gt2, sh3, sc3, gt3 = md
    g1, g2, g3 = gains["g_ffn1"], gains["g_mix"], gains["g_ffn2"]
    w_out2 = w_out.reshape(2 * D_GRP, d)
    ones_g = _group_ones()

    h1 = _modnorm(x0, g1, sc1, sh1, seq)
    a1, u1, s1 = _ffn_up(h1, wgu, 0)
    f1, x1 = _ffn_down(s1, wd, 0, x0, gt1, seq, 0.5)

    h2 = _modnorm(x1, g2, sc2, sh2, seq)
    qkv6 = _qkv_proj(h2, w_in)
    qkv6b = qkv6.reshape(6, nb, seq, D_GRP)
    o_sb, on_sb = _sb_fwd(qkv6b, gains["g_sb_out"], nb, seq)
    onehot = _bucket_onehot()
    bias = _dil_bias(rel_bias.T, onehot).reshape(len(DIL_CONFIGS), N_HEADS, BLOCK, 2 * BLOCK)
    o_cs, l_cs, qkv_rs = [], [], []
    for ci, (_, dil) in enumerate(DIL_CONFIGS):
        sub = seq // dil
        qr = qkv6.reshape(6, nb, sub, dil * D_GRP)
        o_c, l_c = _dil_fwd(qr, bias[ci], nb, sub, dil)
        qkv_rs.append(qr)
        o_cs.append(o_c.reshape(t, D_GRP))
        l_cs.append(l_c.reshape(t, D_GRP))
    o_dil, on_dil = _dil_comb(o_cs, l_cs, gains["g_dil_out"], ones_g)
    tmix, x2 = _mix_out(on_sb.reshape(t, D_GRP), on_dil, w_out2, x1, gt2, seq)

    h3 = _modnorm(x2, g3, sc3, sh3, seq)
    a3, u3, s3 = _ffn_up(h3, wgu, 1)
    f3, x3 = _ffn_down(s3, wd, 1, x2, gt3, seq, 0.5)

    dx3, dg_final, loss = _final_loss(x3, gains["g_final"], tgt)

    da3, du3, df3, dgt3 = _ffn_bwd_ds(dx3, gt3, f3, wd, 1, a3, u3, seq, 0.5)
    dgu2, dwd2 = _ffn_bwd_w(h3, da3, du3, s3, df3)
    dx2, dsh3, dsc3, dg3 = _ffn_bwd_dh(da3, du3, wgu, 1, x2, g3, sc3, dx3, seq)

    do_sb, do_dil, dgt2, dg_sb, dg_dil, dw_out = _mix_bwd_out(
        dx2, gt2, tmix, w_out2, o_sb.reshape(t, D_GRP), o_dil, on_sb.reshape(t, D_GRP), on_dil,
        gains["g_sb_out"], gains["g_dil_out"], ones_g, seq)
    dqkv6 = _sb_bwd(qkv6b, o_sb, do_sb.reshape(nb, seq, D_GRP), nb, seq)
    dcs = _dil_comb_bwd(do_dil, o_cs, l_cs, ones_g)
    dsum, a_tiles = [], []
    for ci, (_, dil) in enumerate(DIL_CONFIGS):
        sub = seq // dil
        do_c = dcs[ci].reshape(nb, sub, dil * D_GRP)
        dd_c = dcs[3 + ci].reshape(nb, sub, dil * D_GRP)
        dq_c, a_c = _dil_bwd(qkv_rs[ci], bias[ci], do_c, dd_c, nb, sub, dil)
        dsum.append(dq_c.reshape(3, t, D_GRP))
        a_tiles.append(a_c.reshape(N_HEADS, BLOCK * 2 * BLOCK))
    dqkv6 = _dqkv_dil_sum(dsum[0], dsum[1], dsum[2], dqkv6.reshape(6, t, D_GRP))
    drel = _relbias_grad(jnp.stack(a_tiles), onehot)
    dwin = _dw_in(h2, dqkv6)
    dx1, dsh2, dsc2, dg2 = _mix_bwd_dh(dqkv6, w_in, x1, g2, sc2, dx2, seq)

    da1, du1, df1, dgt1 = _ffn_bwd_ds(dx1, gt1, f1, wd, 0, a1, u1, seq, 0.5)
    dgu1, dwd1 = _ffn_bwd_w(h1, da1, du1, s1, df1)
    dx0, dsh1, dsc1, dg1 = _ffn_bwd_dh(da1, du1, wgu, 0, x0, g1, sc1, dx1, seq)

    dmod = jnp.concatenate([dsh1, dsc1, dgt1, dsh2, dsc2, dgt2, dsh3, dsc3, dgt3], axis=1)
    return dict(grad_x=dx0.reshape(nb, seq, d), loss=loss[0, 0], dmod=dmod.reshape(nb, N_MOD * d),
                dgu1=dgu1, dgu2=dgu2, dwd1=dwd1, dwd2=dwd2, dwin=dwin,
                dwout=dw_out.reshape(N_CHIPS, 1, 2 * D_GRP // N_CHIPS, d),
                dg_ffn1=dg1, dg_mix=dg2, dg_ffn2=dg3, dg_final=dg_final, dg_sb=dg_sb, dg_dil=dg_dil,
                drel=drel.T)


_SMALL_ORDER = (("b_ada", N_MOD * 1024), ("g_ffn1", 1024), ("g_mix", 1024), ("g_ffn2", 1024), ("g_final", 1024),
                ("g_sb_out", D_GRP), ("g_dil_out", D_GRP), ("rel_bias", N_BUCKETS * N_HEADS))


def _pack_small(parts, extra=None):
    flat = [parts[name].reshape(-1).astype(F32) for name, _ in _SMALL_ORDER]
    used = sum(sz for _, sz in _SMALL_ORDER)
    pad = SMALL_ROWS * 128 - used
    tail = jnp.zeros((pad,), F32)
    if extra is not None:
        tail = tail.at[0].set(extra)
    return jnp.concatenate(flat + [tail]).reshape(SMALL_ROWS, 128)


def _unpack_small(packed, shapes):
    flat = packed.reshape(-1)
    out, off = {}, 0
    for name, sz in _SMALL_ORDER:
        out[name] = flat[off:off + sz].reshape(shapes[name])
        off += sz
    return out, flat[off]


def kernel(x, c, w_ada, b_ada, g_ffn1, w1_gate, w1_up, w1_down, g_mix, w_in, g_sb_out, g_dil_out, w_out, rel_bias, g_ffn2, w2_gate, w2_up, w2_down, g_final, loss_target, m_w_ada, m_b_ada, m_g_ffn1, m_w1_gate, m_w1_up, m_w1_down, m_g_mix, m_w_in, m_g_sb_out, m_g_dil_out, m_w_out, m_rel_bias, m_g_ffn2, m_w2_gate, m_w2_up, m_w2_down, m_g_final, v_w_ada, v_b_ada, v_g_ffn1, v_w1_gate, v_w1_up, v_w1_down, v_g_mix, v_w_in, v_g_sb_out, v_g_dil_out, v_w_out, v_rel_bias, v_g_ffn2, v_w2_gate, v_w2_up, v_w2_down, v_g_final):
    nb, seq, d = x.shape
    xi, yi, ci = lax.axis_index("x"), lax.axis_index("y"), lax.axis_index("c")
    chip = 2 * xi + yi
    ada_cols = w_ada.shape[-1]

    c_pad = jnp.zeros((8, d), F32).at[:nb].set(c)
    b_shard = lax.dynamic_slice(b_ada, (0, chip * ada_cols), (1, ada_cols))
    c_all, mod_blk = _ada_fwd(c_pad, w_ada[0], b_shard)
    mod = jnp.transpose(mod_blk[:, :nb, :], (1, 0, 2)).reshape(nb, N_MOD, d)

    shards = [jnp.stack([w1_gate[0], w1_up[0], w2_gate[0], w2_up[0]]).astype(BF),
              jnp.stack([w1_down[0], w2_down[0]]).astype(BF),
              w_in.astype(BF), w_out.astype(BF)]
    wgu, wd, w_in_g, w_out_g = _ag_weights(shards)

    gains = dict(g_ffn1=g_ffn1, g_mix=g_mix, g_ffn2=g_ffn2, g_final=g_final.reshape(1, d),
                 g_sb_out=g_sb_out.reshape(1, D_GRP), g_dil_out=g_dil_out.reshape(1, D_GRP))
    r = _local_step(x, mod, loss_target, wgu, wd, w_in_g, w_out_g, gains, rel_bias)

    grads = [r["dgu1"], r["dgu2"], r["dwd1"], r["dwd2"], r["dwin"], r["dwout"]]
    lands = _rs_d2d(grads)
    core = jnp.reshape(ci, (1,)).astype(jnp.int32)
    parts = [_add_halves(core, g, l) for g, l in zip(grads, lands)]
    lands2 = _rs_ici(parts)
    halves = [_sum_chips(l) for l in lands2]
    gu1, gu2, gd1, gd2, gwin, gwout = _rs_final(halves)

    dmod = r["dmod"]
    dmod_pad = jnp.zeros((8, N_MOD * d), F32).at[:nb].set(dmod)
    dmod_blk = jnp.transpose(dmod_pad.reshape(8, N_CHIPS, ada_cols), (1, 0, 2))
    small_parts = dict(b_ada=_rowsum8(dmod_pad), g_ffn1=r["dg_ffn1"], g_mix=r["dg_mix"], g_ffn2=r["dg_ffn2"],
                       g_final=r["dg_final"], g_sb_out=r["dg_sb"], g_dil_out=r["dg_dil"], rel_bias=r["drel"])
    small_sum, g_wada = _small_sync(_pack_small(small_parts, r["loss"]), dmod_blk, c_all)

    small_w = dict(b_ada=b_ada, g_ffn1=g_ffn1, g_mix=g_mix, g_ffn2=g_ffn2, g_final=g_final,
                   g_sb_out=g_sb_out, g_dil_out=g_dil_out, rel_bias=rel_bias)
    small_m = dict(b_ada=m_b_ada, g_ffn1=m_g_ffn1, g_mix=m_g_mix, g_ffn2=m_g_ffn2, g_final=m_g_final,
                   g_sb_out=m_g_sb_out, g_dil_out=m_g_dil_out, rel_bias=m_rel_bias)
    small_v = dict(b_ada=v_b_ada, g_ffn1=v_g_ffn1, g_mix=v_g_mix, g_ffn2=v_g_ffn2, g_final=v_g_final,
                   g_sb_out=v_g_sb_out, g_dil_out=v_g_dil_out, rel_bias=v_rel_bias)
    shapes = {k: v.shape for k, v in small_w.items()}
    sg, sd, sm, sv = _adamw(_pack_small(small_w), small_sum.reshape(1, SMALL_ROWS, 128), 0,
                            _pack_small(small_m), _pack_small(small_v))
    sg, loss = _unpack_small(sg, shapes)
    sd, _ = _unpack_small(sd, shapes)
    sm, _ = _unpack_small(sm, shapes)
    sv, _ = _unpack_small(sv, shapes)

    big = {}

    def upd(name, w, g_arr, sel, m, v):
        shape = w.shape
        res = _adamw(w.reshape(shape[-2:]), g_arr, sel, m.reshape(shape[-2:]), v.reshape(shape[-2:]))
        big[name] = [a.reshape(shape) for a in res]

    upd("w_ada", w_ada, g_wada.reshape(1, d, ada_cols), 0, m_w_ada, v_w_ada)
    upd("w1_gate", w1_gate, gu1, 0, m_w1_gate, v_w1_gate)
    upd("w1_up", w1_up, gu1, 1, m_w1_up, v_w1_up)
    upd("w1_down", w1_down, gd1, 0, m_w1_down, v_w1_down)
    upd("w_in", w_in, gwin, 0, m_w_in, v_w_in)
    upd("w_out", w_out, gwout, 0, m_w_out, v_w_out)
    upd("w2_gate", w2_gate, gu2, 0, m_w2_gate, v_w2_gate)
    upd("w2_up", w2_up, gu2, 1, m_w2_up, v_w2_up)
    upd("w2_down", w2_down, gd2, 0, m_w2_down, v_w2_down)

    names = ["w_ada", "b_ada", "g_ffn1", "w1_gate", "w1_up", "w1_down", "g_mix", "w_in", "g_sb_out", "g_dil_out",
             "w_out", "rel_bias", "g_ffn2", "w2_gate", "w2_up", "w2_down", "g_final"]
    outs = [loss, r["grad_x"]]
    for k, small in enumerate((sg, sd, sm, sv)):
        for name in names:
            outs.append(big[name][k] if name in big else small[name])
    return tuple(outs)
```

```python
import functools
import math

import numpy as np
import jax
import jax.numpy as jnp
from jax import lax
from jax.experimental import pallas as pl
from jax.experimental.pallas import tpu as pltpu

F32 = jnp.float32
BF = jnp.bfloat16
MESH = pl.DeviceIdType.MESH

HEAD_DIM = 64
N_HEADS = 8
D_GRP = N_HEADS * HEAD_DIM
DIL_CONFIGS = ((128, 1), (512, 4), (2048, 16))
N_STEPS = 128
BLOCK = 128
N_BUCKETS = 32
MAX_DISTANCE = 2048
N_MOD = 9
EPS = 1e-6
NEG_INF = -1e30
SCALE = HEAD_DIM ** -0.5

ADAM_LR = 0.001
ADAM_B1 = 0.9
ADAM_B2 = 0.999
ADAM_EPS = 1e-08
ADAM_WD = 0.01
ADAM_STEP = 10

N_CHIPS = 4
N_DEV = 8
VMEM_LIMIT = 56 * 1024 * 1024
TM = 512
TQ = 256
KB = 256
SMALL_ROWS = 120


def _cp(n_axes=0, **kw):
    sem = ("arbitrary",) * n_axes if n_axes else None
    return pltpu.CompilerParams(dimension_semantics=sem, vmem_limit_bytes=VMEM_LIMIT, **kw)


def _nn(a, b):
    return jnp.dot(a, b, preferred_element_type=F32)


def _nt(a, b):
    return lax.dot_general(a, b, (((1,), (1,)), ((), ())), preferred_element_type=F32)


def _tn(a, b):
    return lax.dot_general(a, b, (((0,), (0,)), ((), ())), preferred_element_type=F32)


def _nn2(x, m):
    hi = x.astype(BF)
    lo = (x - hi.astype(F32)).astype(BF)
    return _nn(hi, m) + _nn(lo, m)


def _softplus(z):
    return jnp.maximum(z, 0.0) + jnp.log1p(jnp.exp(-jnp.abs(z)))


def _sds(shape, dtype):
    return jax.ShapeDtypeStruct(shape, dtype)


def _modnorm(x, g, sc, sh, seq):
    t, d = x.shape
    per = seq // TM

    def body(x_ref, g_ref, sc_ref, sh_ref, h_ref):
        xv = x_ref[...]
        r = lax.rsqrt(jnp.mean(xv * xv, axis=-1, keepdims=True) + EPS)
        h_ref[...] = (((xv * r) * g_ref[...]) * (1.0 + sc_ref[...]) + sh_ref[...]).astype(BF)

    return pl.pallas_call(
        body, name="modnorm", grid=(t // TM,),
        in_specs=[pl.BlockSpec((TM, d), lambda m: (m, 0)),
                  pl.BlockSpec((1, d), lambda m: (0, 0)),
                  pl.BlockSpec((None, 1, d), lambda m: (m // per, 0, 0)),
                  pl.BlockSpec((None, 1, d), lambda m: (m // per, 0, 0))],
        out_specs=pl.BlockSpec((TM, d), lambda m: (m, 0)),
        out_shape=_sds((t, d), BF), compiler_params=_cp(1))(x, g, sc, sh)


def _modnorm_bwd_tile(dh, xv, gv, scv, dxo):
    r = lax.rsqrt(jnp.mean(xv * xv, axis=-1, keepdims=True) + EPS)
    n = xv * r
    ng = n * gv
    dsh = jnp.sum(dh, axis=0, keepdims=True)
    dsc = jnp.sum(dh * ng, axis=0, keepdims=True)
    dy = dh * (1.0 + scv)
    dg = jnp.sum(dy * n, axis=0, keepdims=True)
    dn = dy * gv
    dx = dxo + r * (dn - n * jnp.mean(dn * n, axis=-1, keepdims=True))
    return dx, dsh, dsc, dg


def _acc_rows(ref, val, first):
    @pl.when(first)
    def _():
        ref[...] = val

    @pl.when(jnp.logical_not(first))
    def _():
        ref[...] += val


def _ffn_up(h, wgu, f_idx):
    t, d = h.shape
    fs = wgu.shape[-1]

    def body(h_ref, w_ref, a_ref, u_ref, s_ref):
        hv = h_ref[...]
        a = _nn(hv, w_ref[0])
        u = _nn(hv, w_ref[1])
        a_ref[...] = a
        u_ref[...] = u
        s_ref[...] = ((a * jax.nn.sigmoid(a)) * u).astype(BF)

    blk = pl.BlockSpec((None, TM, fs), lambda j, m: (j, m, 0))
    return pl.pallas_call(
        body, name="ffn_up", grid=(N_CHIPS, t // TM),
        in_specs=[pl.BlockSpec((TM, d), lambda j, m: (m, 0)),
                  pl.BlockSpec((None, 2, d, fs), lambda j, m: (j, f_idx, 0, 0))],
        out_specs=[blk, blk, blk],
        out_shape=[_sds((N_CHIPS, t, fs), F32), _sds((N_CHIPS, t, fs), F32), _sds((N_CHIPS, t, fs), BF)],
        compiler_params=_cp(2))(h, wgu)


def _ffn_down(s, wd, f_idx, x, gt, seq, coef):
    _, t, fs = s.shape
    d = x.shape[-1]
    per = seq // TM

    def body(s_ref, w_ref, x_ref, gt_ref, f_ref, xo_ref, acc):
        j = pl.program_id(1)

        @pl.when(j == 0)
        def _():
            acc[...] = jnp.zeros_like(acc)

        acc[...] += _nn(s_ref[...], w_ref[...])

        @pl.when(j == N_CHIPS - 1)
        def _():
            f = acc[...]
            f_ref[...] = f
            xo_ref[...] = x_ref[...] + (coef * gt_ref[...]) * f

    row = pl.BlockSpec((TM, d), lambda m, j: (m, 0))
    return pl.pallas_call(
        body, name="ffn_down", grid=(t // TM, N_CHIPS),
        in_specs=[pl.BlockSpec((None, TM, fs), lambda m, j: (j, m, 0)),
                  pl.BlockSpec((None, None, fs, d), lambda m, j: (j, f_idx, 0, 0)),
                  row,
                  pl.BlockSpec((None, 1, d), lambda m, j: (m // per, 0, 0))],
        out_specs=[row, row],
        out_shape=[_sds((t, d), F32), _sds((t, d), F32)],
        scratch_shapes=[pltpu.VMEM((TM, d), F32)],
        compiler_params=_cp(2))(s, wd, x, gt)


def _ffn_bwd_ds(dxo, gt, f, wd, f_idx, a, u, seq, coef):
    t, d = dxo.shape
    fs = a.shape[-1]
    per = seq // TM
    nb = t // seq

    def body(dxo_ref, gt_ref, f_ref, w_ref, a_ref, u_ref, da_ref, du_ref, df_ref, dgt_ref):
        m = pl.program_id(0)
        j = pl.program_id(1)
        dxv = dxo_ref[...]
        df = ((coef * gt_ref[...]) * dxv).astype(BF)

        @pl.when(j == 0)
        def _():
            df_ref[...] = df
            part = coef * jnp.sum(dxv * f_ref[...], axis=0, keepdims=True)
            _acc_rows(dgt_ref, part, m % per == 0)

        ds = _nt(df, w_ref[...])
        av = a_ref[...]
        sig = jax.nn.sigmoid(av)
        da_ref[...] = (ds * u_ref[...] * (sig * (1.0 + av * (1.0 - sig)))).astype(BF)
        du_ref[...] = (ds * (av * sig)).astype(BF)

    row = pl.BlockSpec((TM, d), lambda m, j: (m, 0))
    blk = pl.BlockSpec((None, TM, fs), lambda m, j: (j, m, 0))
    ex = pl.BlockSpec((None, 1, d), lambda m, j: (m // per, 0, 0))
    return pl.pallas_call(
        body, name="ffn_bwd_ds", grid=(t // TM, N_CHIPS),
        in_specs=[row, ex, row,
                  pl.BlockSpec((None, None, fs, d), lambda m, j: (j, f_idx, 0, 0)),
                  blk, blk],
        out_specs=[blk, blk, row, ex],
        out_shape=[_sds((N_CHIPS, t, fs), BF), _sds((N_CHIPS, t, fs), BF), _sds((t, d), BF),
                   _sds((nb, 1, d), F32)],
        compiler_params=_cp(2))(dxo, gt, f, wd, a, u)


def _ffn_bwd_w(h, da, du, s, df):
    t, d = h.shape
    fs = da.shape[-1]

    def body(h_ref, da_ref, du_ref, s_ref, df_ref, dgu_ref, dwd_ref):
        kt = pl.program_id(1)
        hv = h_ref[...]
        pg = _tn(hv, da_ref[...])
        pu = _tn(hv, du_ref[...])
        pd = _tn(s_ref[...], df_ref[...])

        @pl.when(kt == 0)
        def _():
            dgu_ref[0] = pg
            dgu_ref[1] = pu
            dwd_ref[...] = pd

        @pl.when(kt != 0)
        def _():
            dgu_ref[0] += pg
            dgu_ref[1] += pu
            dwd_ref[...] += pd

    row = pl.BlockSpec((TM, d), lambda j, kt: (kt, 0))
    blk = pl.BlockSpec((None, TM, fs), lambda j, kt: (j, kt, 0))
    return pl.pallas_call(
        body, name="ffn_bwd_w", grid=(N_CHIPS, t // TM),
        in_specs=[row, blk, blk, blk, row],
        out_specs=[pl.BlockSpec((None, 2, d, fs), lambda j, kt: (j, 0, 0, 0)),
                   pl.BlockSpec((None, None, fs, d), lambda j, kt: (j, 0, 0, 0))],
        out_shape=[_sds((N_CHIPS, 2, d, fs), F32), _sds((N_CHIPS, 1, fs, d), F32)],
        compiler_params=_cp(2))(h, da, du, s, df)


def _ffn_bwd_dh(da, du, wgu, f_idx, x, g, sc, dxo, seq):
    _, t, fs = da.shape
    d = x.shape[-1]
    per = seq // TM
    nb = t // seq

    def body(da_ref, du_ref, w_ref, x_ref, g_ref, sc_ref, dxo_ref, dx_ref, dsh_ref, dsc_ref, dg_ref, acc):
        m = pl.program_id(0)
        j = pl.program_id(1)

        @pl.when(j == 0)
        def _():
            acc[...] = jnp.zeros_like(acc)

        acc[...] += _nt(da_ref[...], w_ref[0]) + _nt(du_ref[...], w_ref[1])

        @pl.when(j == N_CHIPS - 1)
        def _():
            dx, dsh, dsc, dg = _modnorm_bwd_tile(acc[...], x_ref[...], g_ref[...], sc_ref[...], dxo_ref[...])
            dx_ref[...] = dx
            _acc_rows(dsh_ref, dsh, m % per == 0)
            _acc_rows(dsc_ref, dsc, m % per == 0)
            _acc_rows(dg_ref, dg, m == 0)

    row = pl.BlockSpec((TM, d), lambda m, j: (m, 0))
    blk = pl.BlockSpec((None, TM, fs), lambda m, j: (j, m, 0))
    ex = pl.BlockSpec((None, 1, d), lambda m, j: (m // per, 0, 0))
    vec = pl.BlockSpec((1, d), lambda m, j: (0, 0))
    return pl.pallas_call(
        body, name="ffn_bwd_dh", grid=(t // TM, N_CHIPS),
        in_specs=[blk, blk, pl.BlockSpec((None, 2, d, fs), lambda m, j: (j, f_idx, 0, 0)),
                  row, vec, ex, row],
        out_specs=[row, ex, ex, vec],
        out_shape=[_sds((t, d), F32), _sds((nb, 1, d), F32), _sds((nb, 1, d), F32), _sds((1, d), F32)],
        scratch_shapes=[pltpu.VMEM((TM, d), F32)],
        compiler_params=_cp(2))(da, du, wgu, x, g, sc, dxo)


def _qkv_proj(h, w_in):
    t, d = h.shape
    wc = w_in.shape[-1]
    nt = 256
    per_chip = wc // nt
    n_tiles = N_CHIPS * per_chip
    per_out = D_GRP // nt

    def body(h_ref, w_ref, o_ref):
        o_ref[...] = _nn(h_ref[...], w_ref[...]).astype(BF)

    return pl.pallas_call(
        body, name="qkv_proj", grid=(n_tiles, t // TM),
        in_specs=[pl.BlockSpec((TM, d), lambda n, m: (m, 0)),
                  pl.BlockSpec((None, None, d, nt), lambda n, m: (n // per_chip, 0, 0, n % per_chip))],
        out_specs=pl.BlockSpec((None, TM, nt), lambda n, m: (n // per_out, m, n % per_out)),
        out_shape=_sds((6, t, D_GRP), BF), compiler_params=_cp(2))(h, w_in)


def _mix_out(on_sb, on_dil, w_out, x, gt, seq):
    t, d = x.shape
    per = seq // TM

    def body(a_ref, b_ref, w_ref, x_ref, gt_ref, t_ref, xo_ref):
        tv = _nn(a_ref[...], w_ref[0:D_GRP, :]) + _nn(b_ref[...], w_ref[D_GRP:2 * D_GRP, :])
        t_ref[...] = tv
        xo_ref[...] = x_ref[...] + gt_ref[...] * tv

    row = pl.BlockSpec((TM, d), lambda m: (m, 0))
    half = pl.BlockSpec((TM, D_GRP), lambda m: (m, 0))
    return pl.pallas_call(
        body, name="mix_out", grid=(t // TM,),
        in_specs=[half, half, pl.BlockSpec((2 * D_GRP, d), lambda m: (0, 0)), row,
                  pl.BlockSpec((None, 1, d), lambda m: (m // per, 0, 0))],
        out_specs=[row, row],
        out_shape=[_sds((t, d), F32), _sds((t, d), F32)],
        compiler_params=_cp(1))(on_sb, on_dil, w_out, x, gt)


def _sb_masks():
    lane = lax.broadcasted_iota(jnp.int32, (1, 2 * HEAD_DIM), 1)
    hm0 = lane < HEAD_DIM
    rel = lax.broadcasted_iota(jnp.int32, (TQ, KB), 0) - lax.broadcasted_iota(jnp.int32, (TQ, KB), 1)
    kr = lax.broadcasted_iota(jnp.int32, (KB, KB), 0)
    kc = lax.broadcasted_iota(jnp.int32, (KB, KB), 1)
    return hm0, rel, kr, kc


def _headnorm_pair(o, gv, hm0):
    o2 = o * o
    ms0 = jnp.sum(jnp.where(hm0, o2, 0.0), axis=-1, keepdims=True) * (1.0 / HEAD_DIM)
    ms1 = jnp.sum(jnp.where(hm0, 0.0, o2), axis=-1, keepdims=True) * (1.0 / HEAD_DIM)
    r = jnp.where(hm0, lax.rsqrt(ms0 + EPS), lax.rsqrt(ms1 + EPS))
    return (o * r) * gv


def _sb_fwd(qkv6, g_sb, nb, seq):
    nq = seq // TQ

    def body(q_ref, k_ref, v_ref, g_ref, o_ref, on_ref):
        qi = pl.program_id(2)
        hm0, rel, kr, kc = _sb_masks()
        upper = (kr > kc).astype(BF)
        qv = q_ref[...]
        outs = []
        for hh in range(2):
            hm = hm0 if hh == 0 else jnp.logical_not(hm0)
            qh = jnp.where(hm, qv, jnp.zeros_like(qv))

            def kbody(it, carry, qh=qh):
                c_l, acc = carry
                kj = qi - it
                ks = pl.multiple_of(kj * KB, KB)
                kb = k_ref[pl.ds(ks, KB), :]
                vb = v_ref[pl.ds(ks, KB), :]
                z = _nt(qh, kb) * SCALE
                causal = (rel + it * KB) > 0
                sp = _softplus(z)
                ln = jnp.where(causal, -sp, 0.0)
                suf = _nn2(ln, upper) + c_l
                w = jnp.where(causal, jnp.exp((z - sp) + suf), 0.0)
                acc = acc + _nn(w.astype(BF), vb)
                c_l = c_l + jnp.sum(ln, axis=-1, keepdims=True)
                return c_l, acc

            _, acc = lax.fori_loop(0, qi + 1, kbody,
                                   (jnp.zeros((TQ, 1), F32), jnp.zeros((TQ, 2 * HEAD_DIM), F32)))
            outs.append(acc)
        o = jnp.where(hm0, outs[0], outs[1])
        o_ref[...] = o
        on_ref[...] = _headnorm_pair(o, g_ref[...], hm0).astype(BF)

    w = 2 * HEAD_DIM
    full = lambda i: pl.BlockSpec((None, None, seq, w), lambda b, hp, q: (i, b, 0, hp))
    qblk = pl.BlockSpec((None, None, TQ, w), lambda b, hp, q: (0, b, q, hp))
    oblk = pl.BlockSpec((None, TQ, w), lambda b, hp, q: (b, q, hp))
    return pl.pallas_call(
        body, name="sb_fwd", grid=(nb, N_HEADS // 2, nq),
        in_specs=[qblk, full(1), full(2), pl.BlockSpec((1, w), lambda b, hp, q: (0, hp))],
        out_specs=[oblk, oblk],
        out_shape=[_sds((nb, seq, D_GRP), F32), _sds((nb, seq, D_GRP), BF)],
        compiler_params=_cp(3))(qkv6, qkv6, qkv6, g_sb)


def _sb_bwd(qkv6, o_raw, do, nb, seq):
    nq = seq // TQ
    nk = seq // KB

    def body(q_ref, k_ref, v_ref, do_ref, out_ref, dk_acc, dv_acc, car):
        qi = pl.program_id(2)
        hm0, rel, kr, kc = _sb_masks()
        upper = (kr > kc).astype(BF)
        lower = (kr < kc).astype(BF)

        @pl.when(qi == 0)
        def _():
            dk_acc[...] = jnp.zeros_like(dk_acc)
            dv_acc[...] = jnp.zeros_like(dv_acc)

        qv = q_ref[...]
        dov = do_ref[...]
        dqs = []
        for hh in range(2):
            hm = hm0 if hh == 0 else jnp.logical_not(hm0)
            qh = jnp.where(hm, qv, jnp.zeros_like(qv))
            doh = jnp.where(hm, dov, 0.0).astype(BF)

            def abody(it, c_l, qh=qh):
                kj = qi - it
                ks = pl.multiple_of(kj * KB, KB)
                z = _nt(qh, k_ref[pl.ds(ks, KB), :]) * SCALE
                causal = (rel + it * KB) > 0
                ln = jnp.where(causal, -_softplus(z), 0.0)
                car[kj] = c_l
                return c_l + jnp.sum(ln, axis=-1, keepdims=True)

            lax.fori_loop(0, qi + 1, abody, jnp.zeros((TQ, 1), F32))

            def bbody(kj, carry, qh=qh, doh=doh):
                c_g, dq = carry
                ks = pl.multiple_of(kj * KB, KB)
                kb = k_ref[pl.ds(ks, KB), :]
                vb = v_ref[pl.ds(ks, KB), :]
                z = _nt(qh, kb) * SCALE
                causal = (rel + (qi - kj) * KB) > 0
                sp = _softplus(z)
                ln = jnp.where(causal, -sp, 0.0)
                lsz = z - sp
                suf = _nn2(ln, upper) + car[kj]
                w = jnp.where(causal, jnp.exp(lsz + suf), 0.0)
                g = w * _nt(doh, vb)
                pre = _nn2(g, lower) + c_g
                sig = jnp.exp(lsz)
                dz = jnp.where(causal, g * (1.0 - sig) - sig * pre, 0.0) * SCALE
                dzb = dz.astype(BF)
                dq = dq + _nn(dzb, kb)
                dk_acc[pl.ds(ks, KB), :] += _tn(dzb, qh)
                dv_acc[pl.ds(ks, KB), :] += _tn(w.astype(BF), doh)
                c_g = c_g + jnp.sum(g, axis=-1, keepdims=True)
                return c_g, dq

            _, dq = lax.fori_loop(0, qi + 1, bbody,
                                  (jnp.zeros((TQ, 1), F32), jnp.zeros((TQ, 2 * HEAD_DIM), F32)))
            dqs.append(dq)
        dq = jnp.where(hm0, dqs[0], dqs[1])
        out_ref[0, pl.ds(pl.multiple_of(qi * TQ, TQ), TQ), :] = dq.astype(BF)

        @pl.when(qi == nq - 1)
        def _():
            out_ref[1] = dk_acc[...].astype(BF)
            out_ref[2] = dv_acc[...].astype(BF)

    w = 2 * HEAD_DIM
    full = lambda i: pl.BlockSpec((None, None, seq, w), lambda b, hp, q: (i, b, 0, hp))
    qblk = pl.BlockSpec((None, None, TQ, w), lambda b, hp, q: (0, b, q, hp))
    oblk = pl.BlockSpec((None, TQ, w), lambda b, hp, q: (b, q, hp))
    del o_raw
    return pl.pallas_call(
        body, name="sb_bwd", grid=(nb, N_HEADS // 2, nq),
        in_specs=[qblk, full(1), full(2), oblk],
        out_specs=pl.BlockSpec((3, None, seq, w), lambda b, hp, q: (0, b, 0, hp)),
        out_shape=_sds((6, nb, seq, D_GRP), BF),
        scratch_shapes=[pltpu.VMEM((seq, w), F32), pltpu.VMEM((seq, w), F32), pltpu.VMEM((nk, TQ, 1), F32)],
        compiler_params=_cp(3))(qkv6, qkv6, qkv6, do)


def _t5_bucket(n):
    max_exact = N_BUCKETS // 2
    nf = np.maximum(n, 1).astype(np.float32)
    large = max_exact + (np.log(nf / max_exact) / math.log(MAX_DISTANCE / max_exact)
                         * (N_BUCKETS - max_exact)).astype(np.int32)
    large = np.minimum(large, N_BUCKETS - 1)
    return np.where(n < max_exact, n, large).astype(np.int32)


def _bucket_map(dilation):
    step = BLOCK + np.arange(BLOCK)[:, None] - np.arange(2 * BLOCK)[None, :]
    return _t5_bucket(np.clip(step, 0, N_STEPS) * dilation)


def _dil_masks():
    lane = lax.broadcasted_iota(jnp.int32, (1, 2 * HEAD_DIM), 1)
    hm0 = lane < HEAD_DIM
    iq = lax.broadcasted_iota(jnp.int32, (BLOCK, BLOCK), 0)
    ik = lax.broadcasted_iota(jnp.int32, (BLOCK, BLOCK), 1)
    return hm0, ik <= iq, ik >= iq


def _dil_probs(qh, kc, kp, b_ref, hh, valid_c, valid_p):
    zc = _nt(qh, kc) * SCALE + b_ref[hh, :, BLOCK:2 * BLOCK]
    zp = _nt(qh, kp) * SCALE + b_ref[hh, :, 0:BLOCK]
    zc = jnp.where(valid_c, zc, NEG_INF)
    zp = jnp.where(valid_p, zp, NEG_INF)
    m = jnp.maximum(jnp.max(zc, axis=-1, keepdims=True), jnp.max(zp, axis=-1, keepdims=True))
    ec = jnp.exp(zc - m)
    ep = jnp.exp(zp - m)
    den = jnp.sum(ec, axis=-1, keepdims=True) + jnp.sum(ep, axis=-1, keepdims=True)
    return ec, ep, den, m


def _dil_fwd(qkv6r, bias, nb, sub_len, dilation):
    n_blk = sub_len // BLOCK
    w = 2 * HEAD_DIM

    def body(q_ref, k_ref, v_ref, b_ref, o_ref, l_ref):
        hm0, valid_c, valid_p0 = _dil_masks()

        def nbody(n, carry):
            rs = pl.multiple_of(n * BLOCK, BLOCK)
            ps = pl.multiple_of(jnp.maximum(n - 1, 0) * BLOCK, BLOCK)
            qv = q_ref[pl.ds(rs, BLOCK), :]
            kc = k_ref[pl.ds(rs, BLOCK), :]
            kp = k_ref[pl.ds(ps, BLOCK), :]
            vc = v_ref[pl.ds(rs, BLOCK), :]
            vp = v_ref[pl.ds(ps, BLOCK), :]
            valid_p = jnp.logical_and(valid_p0, n > 0)
            os, ls = [], []
            for hh in range(2):
                hm = hm0 if hh == 0 else jnp.logical_not(hm0)
                qh = jnp.where(hm, qv, jnp.zeros_like(qv))
                ec, ep, den, m = _dil_probs(qh, kc, kp, b_ref, hh, valid_c, valid_p)
                os.append((_nn(ec.astype(BF), vc) + _nn(ep.astype(BF), vp)) / den)
                ls.append(m + jnp.log(den))
            o_ref[pl.ds(rs, BLOCK), :] = jnp.where(hm0, os[0], os[1])
            l_ref[pl.ds(rs, BLOCK), :] = jnp.where(hm0, ls[0], ls[1])
            return carry

        lax.fori_loop(0, n_blk, nbody, 0)

    seqblk = lambda i: pl.BlockSpec((None, None, sub_len, w), lambda b, g: (i, b, 0, g))
    oblk = pl.BlockSpec((None, sub_len, w), lambda b, g: (b, 0, g))
    shp = _sds((nb, sub_len, dilation * D_GRP), F32)
    return pl.pallas_call(
        body, name="dil_fwd_%d" % dilation, grid=(nb, dilation * (N_HEADS // 2)),
        in_specs=[seqblk(3), seqblk(4), seqblk(5),
                  pl.BlockSpec((2, BLOCK, 2 * BLOCK), lambda b, g: (g % (N_HEADS // 2), 0, 0))],
        out_specs=[oblk, oblk], out_shape=[shp, shp],
        compiler_params=_cp(2))(qkv6r, qkv6r, qkv6r, bias)


def _dil_bwd(qkv6r, bias, do_c, dd_c, nb, sub_len, dilation):
    n_blk = sub_len // BLOCK
    w = 2 * HEAD_DIM
    hp_n = N_HEADS // 2

    def body(q_ref, k_ref, v_ref, b_ref, do_ref, dd_ref, out_ref, a_ref, dk_acc, dv_acc):
        hm0, valid_c, valid_p0 = _dil_masks()
        first = jnp.logical_and(pl.program_id(1) == 0, pl.program_id(2) == 0)

        @pl.when(first)
        def _():
            a_ref[...] = jnp.zeros_like(a_ref)

        dk_acc[...] = jnp.zeros_like(dk_acc)
        dv_acc[...] = jnp.zeros_like(dv_acc)

        def nbody(n, carry):
            rs = pl.multiple_of(n * BLOCK, BLOCK)
            ps = pl.multiple_of(jnp.maximum(n - 1, 0) * BLOCK, BLOCK)
            qv = q_ref[pl.ds(rs, BLOCK), :]
            kc = k_ref[pl.ds(rs, BLOCK), :]
            kp = k_ref[pl.ds(ps, BLOCK), :]
            vc = v_ref[pl.ds(rs, BLOCK), :]
            vp = v_ref[pl.ds(ps, BLOCK), :]
            dov = do_ref[pl.ds(rs, BLOCK), :]
            ddv = dd_ref[pl.ds(rs, BLOCK), :]
            valid_p = jnp.logical_and(valid_p0, n > 0)
            dq = jnp.zeros((BLOCK, w), F32)
            for hh in range(2):
                hm = hm0 if hh == 0 else jnp.logical_not(hm0)
                qh = jnp.where(hm, qv, jnp.zeros_like(qv))
                doh = jnp.where(hm, dov, 0.0).astype(BF)
                ddh = jnp.sum(jnp.where(hm, ddv, 0.0), axis=-1, keepdims=True) * (1.0 / HEAD_DIM)
                ec, ep, den, _ = _dil_probs(qh, kc, kp, b_ref, hh, valid_c, valid_p)
                inv = 1.0 / den
                pc = ec * inv
                pp = ep * inv
                dzc = pc * (_nt(doh, vc) + ddh)
                dzp = pp * (_nt(doh, vp) + ddh)
                a_ref[hh, :, BLOCK:2 * BLOCK] += dzc
                a_ref[hh, :, 0:BLOCK] += dzp
                dzcb = (dzc * SCALE).astype(BF)
                dzpb = (dzp * SCALE).astype(BF)
                dq = jnp.where(hm, _nn(dzcb, kc) + _nn(dzpb, kp), dq)
                dk_acc[pl.ds(rs, BLOCK), :] += _tn(dzcb, qh)
                dk_acc[pl.ds(ps, BLOCK), :] += _tn(dzpb, qh)
                dv_acc[pl.ds(rs, BLOCK), :] += _tn(pc.astype(BF), doh)
                dv_acc[pl.ds(ps, BLOCK), :] += _tn(pp.astype(BF), doh)
            out_ref[0, pl.ds(rs, BLOCK), :] = dq
            return carry

        lax.fori_loop(0, n_blk, nbody, 0)
        out_ref[1] = dk_acc[...]
        out_ref[2] = dv_acc[...]

    col = lambda hp, b, r: r * hp_n + hp
    seqblk = lambda i: pl.BlockSpec((None, None, sub_len, w), lambda hp, b, r: (i, b, 0, col(hp, b, r)))
    oblk = pl.BlockSpec((None, sub_len, w), lambda hp, b, r: (b, 0, col(hp, b, r)))
    return pl.pallas_call(
        body, name="dil_bwd_%d" % dilation, grid=(hp_n, nb, dilation),
        in_specs=[seqblk(3), seqblk(4), seqblk(5),
                  pl.BlockSpec((2, BLOCK, 2 * BLOCK), lambda hp, b, r: (hp, 0, 0)), oblk, oblk],
        out_specs=[pl.BlockSpec((3, None, sub_len, w), lambda hp, b, r: (0, b, 0, col(hp, b, r))),
                   pl.BlockSpec((2, BLOCK, 2 * BLOCK), lambda hp, b, r: (hp, 0, 0))],
        out_shape=[_sds((3, nb, sub_len, dilation * D_GRP), F32), _sds((N_HEADS, BLOCK, 2 * BLOCK), F32)],
        scratch_shapes=[pltpu.VMEM((sub_len, w), F32), pltpu.VMEM((sub_len, w), F32)],
        compiler_params=_cp(3))(qkv6r, qkv6r, qkv6r, bias, do_c, dd_c)


def _group_ones():
    idx = np.arange(D_GRP) // HEAD_DIM
    return jnp.asarray((idx[:, None] == idx[None, :]).astype(np.float32), dtype=BF)


def _dil_alphas(l1, l4, l16):
    mx = jnp.maximum(jnp.maximum(l1, l4), l16)
    e1 = jnp.exp(l1 - mx)
    e4 = jnp.exp(l4 - mx)
    e16 = jnp.exp(l16 - mx)
    den = e1 + e4 + e16
    return e1 / den, e4 / den, e16 / den


def _dil_comb(os, ls, g_dil, ones_g):
    t = os[0].shape[0]

    def body(o1, l1, o4, l4, o16, l16, g_ref, m_ref, o_ref, on_ref):
        a1, a4, a16 = _dil_alphas(l1[...], l4[...], l16[...])
        o = a1 * o1[...] + a4 * o4[...] + a16 * o16[...]
        o_ref[...] = o
        ms = _nn2(o * o, m_ref[...]) * (1.0 / HEAD_DIM)
        on_ref[...] = ((o * lax.rsqrt(ms + EPS)) * g_ref[...]).astype(BF)

    blk = pl.BlockSpec((TM, D_GRP), lambda m: (m, 0))
    return pl.pallas_call(
        body, name="dil_comb", grid=(t // TM,),
        in_specs=[blk] * 6 + [pl.BlockSpec((1, D_GRP), lambda m: (0, 0)),
                              pl.BlockSpec((D_GRP, D_GRP), lambda m: (0, 0))],
        out_specs=[blk, blk],
        out_shape=[_sds((t, D_GRP), F32), _sds((t, D_GRP), BF)],
        compiler_params=_cp(1))(os[0], ls[0], os[1], ls[1], os[2], ls[2], g_dil, ones_g)


def _dil_comb_bwd(do, os, ls, ones_g):
    t = do.shape[0]

    def body(do_ref, o1, l1, o4, l4, o16, l16, m_ref, d1, d4, d16, e1, e4, e16):
        dov = do_ref[...]
        a1, a4, a16 = _dil_alphas(l1[...], l4[...], l16[...])
        mv = m_ref[...]
        sbar = a1 * _nn2(dov * o1[...], mv) + a4 * _nn2(dov * o4[...], mv) + a16 * _nn2(dov * o16[...], mv)
        d1[...] = a1 * dov
        d4[...] = a4 * dov
        d16[...] = a16 * dov
        e1[...] = -a1 * sbar
        e4[...] = -a4 * sbar
        e16[...] = -a16 * sbar

    blk = pl.BlockSpec((TM, D_GRP), lambda m: (m, 0))
    shp = _sds((t, D_GRP), F32)
    return pl.pallas_call(
        body, name="dil_comb_bwd", grid=(t // TM,),
        in_specs=[blk] * 7 + [pl.BlockSpec((D_GRP, D_GRP), lambda m: (0, 0))],
        out_specs=[blk] * 6, out_shape=[shp] * 6,
        compiler_params=_cp(1))(do, os[0], ls[0], os[1], ls[1], os[2], ls[2], ones_g)


def _dqkv_dil_sum(d1, d4, d16, dqkv6):
    t = d1.shape[1]

    def body(a, b, c, alias, o_ref):
        del alias
        o_ref[...] = (a[...] + b[...] + c[...]).astype(BF)

    blk = pl.BlockSpec((3, TM, D_GRP), lambda m: (0, m, 0))
    return pl.pallas_call(
        body, name="dqkv_dil_sum", grid=(t // TM,),
        in_specs=[blk, blk, blk, pl.BlockSpec(memory_space=pl.ANY)],
        out_specs=pl.BlockSpec((3, TM, D_GRP), lambda m: (1, m, 0)),
        out_shape=_sds((6, t, D_GRP), BF), input_output_aliases={3: 0},
        compiler_params=_cp(1))(d1, d4, d16, dqkv6)


def _relbias_grad(a_all, onehot):
    def body(a_ref, oh_ref, o_ref):
        acc = jnp.zeros((N_HEADS, N_BUCKETS), F32)
        for c in range(len(DIL_CONFIGS)):
            av = a_ref[c]
            hi = av.astype(BF)
            lo = (av - hi.astype(F32)).astype(BF)
            acc = acc + _nt(hi, oh_ref[c]) + _nt(lo, oh_ref[c])
        o_ref[...] = acc

    return pl.pallas_call(body, name="relbias_grad", out_shape=_sds((N_HEADS, N_BUCKETS), F32),
                          compiler_params=_cp())(a_all, onehot)


def _headnorm_bwd(dn, o, gv, mv):
    ms = _nn2(o * o, mv) * (1.0 / HEAD_DIM)
    r = lax.rsqrt(ms + EPS)
    nrm = o * r
    dg = jnp.sum(dn * nrm, axis=0, keepdims=True)
    dnn = dn * gv
    do = r * (dnn - nrm * (_nn2(dnn * nrm, mv) * (1.0 / HEAD_DIM)))
    return do, dg


def _mix_bwd_out(dx, gt, tv, w_out, o_sb, o_dil, on_sb, on_dil, g_sb, g_dil, ones_g, seq):
    t, d = dx.shape
    per = seq // TM
    nb = t // seq

    def body(dx_ref, gt_ref, t_ref, w_ref, osb, odl, onsb, ondl, gsb, gdl, m_ref,
             dosb, dodl, dgt_ref, dgsb, dgdl, dw_ref):
        m = pl.program_id(0)
        dxv = dx_ref[...]
        dt = (gt_ref[...] * dxv).astype(BF)
        _acc_rows(dgt_ref, jnp.sum(dxv * t_ref[...], axis=0, keepdims=True), m % per == 0)
        mv = m_ref[...]
        don_sb = _nt(dt, w_ref[0:D_GRP, :])
        don_dl = _nt(dt, w_ref[D_GRP:2 * D_GRP, :])
        do1, dg1 = _headnorm_bwd(don_sb, osb[...], gsb[...], mv)
        do2, dg2 = _headnorm_bwd(don_dl, odl[...], gdl[...], mv)
        dosb[...] = do1
        dodl[...] = do2
        _acc_rows(dgsb, dg1, m == 0)
        _acc_rows(dgdl, dg2, m == 0)
        p1 = _tn(onsb[...], dt)
        p2 = _tn(ondl[...], dt)

        @pl.when(m == 0)
        def _():
            dw_ref[0:D_GRP, :] = p1
            dw_ref[D_GRP:2 * D_GRP, :] = p2

        @pl.when(m != 0)
        def _():
            dw_ref[0:D_GRP, :] += p1
            dw_ref[D_GRP:2 * D_GRP, :] += p2

    row = pl.BlockSpec((TM, d), lambda m: (m, 0))
    half = pl.BlockSpec((TM, D_GRP), lambda m: (m, 0))
    ex = pl.BlockSpec((None, 1, d), lambda m: (m // per, 0, 0))
    gvec = pl.BlockSpec((1, D_GRP), lambda m: (0, 0))
    wblk = pl.BlockSpec((2 * D_GRP, d), lambda m: (0, 0))
    return pl.pallas_call(
        body, name="mix_bwd_out", grid=(t // TM,),
        in_specs=[row, ex, row, wblk, half, half, half, half, gvec, gvec,
                  pl.BlockSpec((D_GRP, D_GRP), lambda m: (0, 0))],
        out_specs=[half, half, ex, gvec, gvec, wblk],
        out_shape=[_sds((t, D_GRP), F32), _sds((t, D_GRP), F32), _sds((nb, 1, d), F32),
                   _sds((1, D_GRP), F32), _sds((1, D_GRP), F32), _sds((2 * D_GRP, d), F32)],
        compiler_params=_cp(1))(dx, gt, tv, w_out, o_sb, o_dil, on_sb, on_dil, g_sb, g_dil, ones_g)


def _dw_in(h, dqkv6):
    t, d = h.shape
    nt = 256
    per_chip = 3
    per_out = D_GRP // nt

    def body(h_ref, g_ref, o_ref):
        p = _tn(h_ref[...], g_ref[...])
        _acc_rows(o_ref, p, pl.program_id(1) == 0)

    return pl.pallas_call(
        body, name="dw_in", grid=(N_CHIPS * per_chip, t // TM),
        in_specs=[pl.BlockSpec((TM, d), lambda n, kt: (kt, 0)),
                  pl.BlockSpec((None, TM, nt), lambda n, kt: (n // per_out, kt, n % per_out))],
        out_specs=pl.BlockSpec((None, None, d, nt), lambda n, kt: (n // per_chip, 0, 0, n % per_chip)),
        out_shape=_sds((N_CHIPS, 1, d, per_chip * nt), F32),
        compiler_params=_cp(2))(h, dqkv6)


def _mix_bwd_dh(dqkv6, w_in, x, g, sc, dxo, seq):
    _, t, _ = dqkv6.shape
    d = x.shape[-1]
    nt = 256
    per_chip = 3
    per_out = D_GRP // nt
    n_tiles = N_CHIPS * per_chip
    per = seq // TM
    nb = t // seq

    def body(g6_ref, w_ref, x_ref, g_ref, sc_ref, dxo_ref, dx_ref, dsh_ref, dsc_ref, dg_ref, acc):
        m = pl.program_id(0)
        n = pl.program_id(1)

        @pl.when(n == 0)
        def _():
            acc[...] = jnp.zeros_like(acc)

        acc[...] += _nt(g6_ref[...], w_ref[...])

        @pl.when(n == n_tiles - 1)
        def _():
            dx, dsh, dsc, dg = _modnorm_bwd_tile(acc[...], x_ref[...], g_ref[...], sc_ref[...], dxo_ref[...])
            dx_ref[...] = dx
            _acc_rows(dsh_ref, dsh, m % per == 0)
            _acc_rows(dsc_ref, dsc, m % per == 0)
            _acc_rows(dg_ref, dg, m == 0)

    row = pl.BlockSpec((TM, d), lambda m, n: (m, 0))
    ex = pl.BlockSpec((None, 1, d), lambda m, n: (m // per, 0, 0))
    vec = pl.BlockSpec((1, d), lambda m, n: (0, 0))
    return pl.pallas_call(
        body, name="mix_bwd_dh", grid=(t // TM, n_tiles),
        in_specs=[pl.BlockSpec((None, TM, nt), lambda m, n: (n // per_out, m, n % per_out)),
                  pl.BlockSpec((None, None, d, nt), lambda m, n: (n // per_chip, 0, 0, n % per_chip)),
                  row, vec, ex, row],
        out_specs=[row, ex, ex, vec],
        out_shape=[_sds((t, d), F32), _sds((nb, 1, d), F32), _sds((nb, 1, d), F32), _sds((1, d), F32)],
        scratch_shapes=[pltpu.VMEM((TM, d), F32)],
        compiler_params=_cp(2))(dqkv6, w_in, x, g, sc, dxo)


def _final_loss(x, g, target):
    t, d = x.shape
    steps = t // TM

    def body(x_ref, g_ref, t_ref, dx_ref, dg_ref, loss_ref, lacc):
        m = pl.program_id(0)
        xv = x_ref[...]
        gv = g_ref[...]
        r = lax.rsqrt(jnp.mean(xv * xv, axis=-1, keepdims=True) + EPS)
        n = xv * r
        err = n * gv - t_ref[...]
        dy = err * (1.0 / d)
        _acc_rows(dg_ref, jnp.sum(dy * n, axis=0, keepdims=True), m == 0)
        dn = dy * gv
        dx_ref[...] = r * (dn - n * jnp.mean(dn * n, axis=-1, keepdims=True))
        _acc_rows(lacc, jnp.sum(err * err, axis=0, keepdims=True), m == 0)

        @pl.when(m == steps - 1)
        def _():
            tot = jnp.sum(lacc[...], axis=-1, keepdims=True) * (0.5 / d)
            loss_ref[...] = jnp.broadcast_to(tot, (1, 128))

    row = pl.BlockSpec((TM, d), lambda m: (m, 0))
    vec = pl.BlockSpec((1, d), lambda m: (0, 0))
    return pl.pallas_call(
        body, name="final_loss", grid=(steps,),
        in_specs=[row, vec, row],
        out_specs=[row, vec, pl.BlockSpec((1, 128), lambda m: (0, 0))],
        out_shape=[_sds((t, d), F32), _sds((1, d), F32), _sds((1, 128), F32)],
        scratch_shapes=[pltpu.VMEM((1, d), F32)],
        compiler_params=_cp(1))(x, g, target)


def _row_tile(rows, cols):
    best = rows
    for tr in range(8, rows + 1, 8):
        if rows % tr == 0 and tr * cols * 4 <= (1 << 20):
            best = tr
    if best * cols * 4 > (1 << 21):
        best = 8
    return best


def _adamw(w, g_arr, g_sel, m, v):
    rows, cols = w.shape
    tr = _row_tile(rows, cols)
    b1c = 1.0 - ADAM_B1 ** ADAM_STEP
    b2c = 1.0 - ADAM_B2 ** ADAM_STEP

    def body(w_ref, g_ref, m_ref, v_ref, go_ref, d_ref, mo_ref, vo_ref):
        gv = g_ref[...]
        mn = ADAM_B1 * m_ref[...] + (1.0 - ADAM_B1) * gv
        vn = ADAM_B2 * v_ref[...] + (1.0 - ADAM_B2) * (gv * gv)
        go_ref[...] = gv
        mo_ref[...] = mn
        vo_ref[...] = vn
        d_ref[...] = -ADAM_LR * ((mn / b1c) / (jnp.sqrt(vn / b2c) + ADAM_EPS) + ADAM_WD * w_ref[...])

    blk = pl.BlockSpec((tr, cols), lambda i: (i, 0))
    shp = _sds((rows, cols), F32)
    return pl.pallas_call(
        body, name="adamw", grid=(rows // tr,),
        in_specs=[blk, pl.BlockSpec((None, tr, cols), lambda i: (g_sel, i, 0)), blk, blk],
        out_specs=[blk] * 4, out_shape=[shp] * 4,
        compiler_params=_cp(1))(w, g_arr, m, v)


def _flip(v, bit):
    return 1 - v if bit else v


def _my_place():
    x, y, c = lax.axis_index("x"), lax.axis_index("y"), lax.axis_index("c")
    return x, y, c


def _ada_fwd(c_pad, w_ada, b_shard):
    d = c_pad.shape[-1]
    cols = w_ada.shape[-1]
    chunk = 384

    def body(c_ref, w_ref, b_ref, call_ref, mod_ref, part, s1, r1, s2, r2):
        x, y, c = _my_place()
        dev = 4 * x + 2 * y + c
        chip = 2 * x + y
        call_ref[dev] = c_ref[...]

        def c_copy(k):
            px, py, pc = _flip(x, (k >> 2) & 1), _flip(y, (k >> 1) & 1), _flip(c, k & 1)
            return px, py, pc

        sends = []
        for k in range(1, N_DEV):
            px, py, pc = c_copy(k)
            cp = pltpu.make_async_remote_copy(src_ref=c_ref, dst_ref=call_ref.at[dev], send_sem=s1.at[k - 1],
                                              recv_sem=r1.at[k - 1], device_id=(px, py, pc), device_id_type=MESH)
            cp.start()
            sends.append(cp)
        for k in range(1, N_DEV):
            px, py, pc = c_copy(k)
            pltpu.make_async_remote_copy(src_ref=c_ref, dst_ref=call_ref.at[4 * px + 2 * py + pc],
                                         send_sem=s1.at[k - 1], recv_sem=r1.at[k - 1],
                                         device_id=(px, py, pc), device_id_type=MESH).wait_recv()
        for cp in sends:
            cp.wait_send()

        cs = call_ref[...].reshape(N_DEV * 8, d)
        sc = (cs * jax.nn.sigmoid(cs)).astype(BF)
        for n0 in range(0, cols, chunk):
            blk = _nn(sc, w_ref[:, n0:n0 + chunk].astype(BF)) + b_ref[:, n0:n0 + chunk]
            part[:, :, n0:n0 + chunk] = blk.reshape(N_DEV, 8, chunk)

        mod_ref[chip] = part[dev]
        sends = []
        for kk in range(1, N_CHIPS):
            px, py = _flip(x, (kk >> 1) & 1), _flip(y, kk & 1)
            cp = pltpu.make_async_remote_copy(src_ref=part.at[4 * px + 2 * py + c], dst_ref=mod_ref.at[chip],
                                              send_sem=s2.at[kk - 1], recv_sem=r2.at[kk - 1],
                                              device_id=(px, py, c), device_id_type=MESH)
            cp.start()
            sends.append(cp)
        for kk in range(1, N_CHIPS):
            px, py = _flip(x, (kk >> 1) & 1), _flip(y, kk & 1)
            pltpu.make_async_remote_copy(src_ref=part.at[dev], dst_ref=mod_ref.at[2 * px + py],
                                         send_sem=s2.at[kk - 1], recv_sem=r2.at[kk - 1],
                                         device_id=(px, py, c), device_id_type=MESH).wait_recv()
        for cp in sends:
            cp.wait_send()

    return pl.pallas_call(
        body, name="ada_fwd",
        out_shape=[_sds((N_DEV, 8, d), F32), _sds((N_CHIPS, 8, cols), F32)],
        scratch_shapes=[pltpu.VMEM((N_DEV, 8, cols), F32),
                        pltpu.SemaphoreType.DMA((N_DEV - 1,)), pltpu.SemaphoreType.DMA((N_DEV - 1,)),
                        pltpu.SemaphoreType.DMA((N_CHIPS - 1,)), pltpu.SemaphoreType.DMA((N_CHIPS - 1,))],
        compiler_params=_cp())(c_pad, w_ada, b_shard)


def _ag_weights(bufs):
    n = len(bufs)
    any_spec = pl.BlockSpec(memory_space=pl.ANY)

    def body(*refs):
        outs = refs[n:2 * n]
        s_ici, r_ici, s_d2d, r_d2d = refs[2 * n:]
        x, y, c = _my_place()
        chip = 2 * x + y
        others = [(_flip(x, (kk >> 1) & 1), _flip(y, kk & 1)) for kk in range(1, N_CHIPS)]

        def half(b, which):
            hr = bufs[b].shape[2] // 2
            return pl.ds(pl.multiple_of(which * hr, 16), hr)

        sends = []
        for b in range(n):
            for i, (px, py) in enumerate(others):
                mine = outs[b].at[chip, :, half(b, c), :]
                cp = pltpu.make_async_remote_copy(
                    src_ref=mine, dst_ref=mine,
                    send_sem=s_ici.at[3 * b + i], recv_sem=r_ici.at[3 * b + i],
                    device_id=(px, py, c), device_id_type=MESH)
                cp.start()
                sends.append(cp)
        for b in range(n):
            for i, (px, py) in enumerate(others):
                landed = outs[b].at[2 * px + py, :, half(b, c), :]
                pltpu.make_async_remote_copy(
                    src_ref=landed, dst_ref=landed, send_sem=s_ici.at[3 * b + i], recv_sem=r_ici.at[3 * b + i],
                    device_id=(px, py, c), device_id_type=MESH).wait_recv()
                cp = pltpu.make_async_remote_copy(
                    src_ref=landed, dst_ref=landed, send_sem=s_d2d.at[3 * b + i], recv_sem=r_d2d.at[3 * b + i],
                    device_id=(x, y, 1 - c), device_id_type=MESH)
                cp.start()
                sends.append(cp)
        for b in range(n):
            for i, (px, py) in enumerate(others):
                got = outs[b].at[2 * px + py, :, half(b, 1 - c), :]
                pltpu.make_async_remote_copy(
                    src_ref=got, dst_ref=got, send_sem=s_d2d.at[3 * b + i], recv_sem=r_d2d.at[3 * b + i],
                    device_id=(x, y, 1 - c), device_id_type=MESH).wait_recv()
        for cp in sends:
            cp.wait_send()

    return pl.pallas_call(
        body, name="ag_weights",
        in_specs=[any_spec] * n, out_specs=[any_spec] * n,
        out_shape=[_sds(s.shape, s.dtype) for s in bufs],
        input_output_aliases={i: i for i in range(n)},
        scratch_shapes=[pltpu.SemaphoreType.DMA((3 * n,)), pltpu.SemaphoreType.DMA((3 * n,)),
                        pltpu.SemaphoreType.DMA((3 * n,)), pltpu.SemaphoreType.DMA((3 * n,))],
        compiler_params=_cp())(*bufs)


def _rs_d2d(grads):
    n = len(grads)
    any_spec = pl.BlockSpec(memory_space=pl.ANY)

    def body(*refs):
        srcs, lands = refs[:n], refs[n:2 * n]
        ssem, rsem = refs[2 * n:]
        x, y, c = _my_place()
        sends = []
        for b in range(n):
            hr = grads[b].shape[2] // 2
            theirs = pl.ds(pl.multiple_of((1 - c) * hr, 8), hr)
            cp = pltpu.make_async_remote_copy(
                src_ref=srcs[b].at[:, :, theirs, :], dst_ref=lands[b], send_sem=ssem.at[b], recv_sem=rsem.at[b],
                device_id=(x, y, 1 - c), device_id_type=MESH)
            cp.start()
            sends.append(cp)
        for cp in sends:
            cp.wait()

    return pl.pallas_call(
        body, name="rs_d2d",
        in_specs=[any_spec] * n, out_specs=[any_spec] * n,
        out_shape=[_sds(g.shape[:2] + (g.shape[2] // 2, g.shape[3]), F32) for g in grads],
        scratch_shapes=[pltpu.SemaphoreType.DMA((n,)), pltpu.SemaphoreType.DMA((n,))],
        compiler_params=_cp())(*grads)


def _add_halves(core, g, land):
    nchip, ng, rows, cols = g.shape
    hr = rows // 2
    tr = _row_tile(hr, cols)
    steps = hr // tr

    def body(core_ref, g_ref, l_ref, o_ref):
        del core_ref
        o_ref[...] = (g_ref[...] + l_ref[...]).astype(BF)

    return pl.pallas_call(
        body, name="add_halves",
        grid_spec=pltpu.PrefetchScalarGridSpec(
            num_scalar_prefetch=1, grid=(nchip, ng, steps),
            in_specs=[pl.BlockSpec((None, None, tr, cols), lambda j, a, i, cr: (j, a, cr[0] * steps + i, 0)),
                      pl.BlockSpec((None, None, tr, cols), lambda j, a, i, cr: (j, a, i, 0))],
            out_specs=pl.BlockSpec((None, None, tr, cols), lambda j, a, i, cr: (j, a, i, 0))),
        out_shape=_sds((nchip, ng, hr, cols), BF),
        compiler_params=_cp(3))(core, g, land)


def _rs_ici(parts):
    n = len(parts)
    any_spec = pl.BlockSpec(memory_space=pl.ANY)

    def body(*refs):
        srcs, lands = refs[:n], refs[n:2 * n]
        ssem, rsem = refs[2 * n:]
        x, y, c = _my_place()
        chip = 2 * x + y
        others = [(_flip(x, (kk >> 1) & 1), _flip(y, kk & 1)) for kk in range(1, N_CHIPS)]
        sends = []
        for b in range(n):
            for i, (px, py) in enumerate(others):
                cp = pltpu.make_async_remote_copy(
                    src_ref=srcs[b].at[2 * px + py], dst_ref=lands[b].at[chip],
                    send_sem=ssem.at[3 * b + i], recv_sem=rsem.at[3 * b + i],
                    device_id=(px, py, c), device_id_type=MESH)
                cp.start()
                sends.append(cp)
        for b in range(n):
            for i, (px, py) in enumerate(others):
                slot = lands[b].at[2 * px + py]
                pltpu.make_async_remote_copy(
                    src_ref=slot, dst_ref=slot, send_sem=ssem.at[3 * b + i], recv_sem=rsem.at[3 * b + i],
                    device_id=(px, py, c), device_id_type=MESH).wait_recv()
        for cp in sends:
            cp.wait_send()

    return pl.pallas_call(
        body, name="rs_ici",
        in_specs=[any_spec] * n, out_specs=[any_spec] * n,
        out_shape=[_sds(p.shape, p.dtype) for p in parts],
        scratch_shapes=[pltpu.SemaphoreType.DMA((3 * n,)), pltpu.SemaphoreType.DMA((3 * n,))],
        compiler_params=_cp())(*parts)


def _sum_chips(place, part, land):
    nchip, ng, hr, cols = land.shape
    tr = _row_tile(hr, cols)
    steps = hr // tr

    def body(place_ref, p_ref, l1, l2, l3, o_ref):
        del place_ref
        o_ref[...] = ((p_ref[...].astype(F32) + l1[...].astype(F32)) + l2[...].astype(F32)) + l3[...].astype(F32)

    def slot(k):
        return pl.BlockSpec((None, None, tr, cols), lambda a, i, pr: (jnp.bitwise_xor(pr[1], k), a, i, 0))

    return pl.pallas_call(
        body, name="sum_chips",
        grid_spec=pltpu.PrefetchScalarGridSpec(
            num_scalar_prefetch=1, grid=(ng, steps),
            in_specs=[slot(0), slot(1), slot(2), slot(3)],
            out_specs=pl.BlockSpec((None, tr, cols), lambda a, i, pr: (a, pr[0] * steps + i, 0))),
        out_shape=_sds((ng, 2 * hr, cols), F32),
        compiler_params=_cp(2))(place, part, land, land, land)


def _rs_final(bufs):
    n = len(bufs)
    any_spec = pl.BlockSpec(memory_space=pl.ANY)

    def body(*refs):
        outs = refs[n:2 * n]
        ssem, rsem = refs[2 * n:]
        x, y, c = _my_place()
        ops = []
        for b in range(n):
            hr = bufs[b].shape[1] // 2
            mine = outs[b].at[:, pl.ds(pl.multiple_of(c * hr, 8), hr), :]
            cp = pltpu.make_async_remote_copy(
                src_ref=mine, dst_ref=mine, send_sem=ssem.at[b], recv_sem=rsem.at[b],
                device_id=(x, y, 1 - c), device_id_type=MESH)
            cp.start()
            ops.append(cp)
        for b, cp in enumerate(ops):
            cp.wait_send()
            hr = bufs[b].shape[1] // 2
            theirs = outs[b].at[:, pl.ds(pl.multiple_of((1 - c) * hr, 8), hr), :]
            pltpu.make_async_remote_copy(
                src_ref=theirs, dst_ref=theirs, send_sem=ssem.at[b], recv_sem=rsem.at[b],
                device_id=(x, y, 1 - c), device_id_type=MESH).wait_recv()

    return pl.pallas_call(
        body, name="rs_final",
        in_specs=[any_spec] * n, out_specs=[any_spec] * n,
        out_shape=[_sds(h.shape, F32) for h in bufs],
        input_output_aliases={i: i for i in range(n)},
        scratch_shapes=[pltpu.SemaphoreType.DMA((n,)), pltpu.SemaphoreType.DMA((n,))],
        compiler_params=_cp())(*bufs)


def _small_sync(smalls, dmod_blk, c_all):
    d = c_all.shape[-1]
    cols = dmod_blk.shape[-1]
    chunk = 384

    def body(sm_ref, dm_ref, c_ref, sum_ref, gw_ref, sm_all, dm_all, ssem, rsem):
        x, y, c = _my_place()
        dev = 4 * x + 2 * y + c
        chip = 2 * x + y
        sm_all[dev] = sm_ref[...]
        dm_all[dev] = dm_ref[chip]
        sends = []
        for k in range(1, N_DEV):
            px, py, pc = _flip(x, (k >> 2) & 1), _flip(y, (k >> 1) & 1), _flip(c, k & 1)
            a = pltpu.make_async_remote_copy(src_ref=sm_ref, dst_ref=sm_all.at[dev], send_sem=ssem.at[2 * (k - 1)],
                                             recv_sem=rsem.at[2 * (k - 1)], device_id=(px, py, pc),
                                             device_id_type=MESH)
            b = pltpu.make_async_remote_copy(src_ref=dm_ref.at[2 * px + py], dst_ref=dm_all.at[dev],
                                             send_sem=ssem.at[2 * (k - 1) + 1], recv_sem=rsem.at[2 * (k - 1) + 1],
                                             device_id=(px, py, pc), device_id_type=MESH)
            a.start()
            b.start()
            sends += [a, b]
        for k in range(1, N_DEV):
            px, py, pc = _flip(x, (k >> 2) & 1), _flip(y, (k >> 1) & 1), _flip(c, k & 1)
            pdev = 4 * px + 2 * py + pc
            pltpu.make_async_remote_copy(src_ref=sm_ref, dst_ref=sm_all.at[pdev], send_sem=ssem.at[2 * (k - 1)],
                                         recv_sem=rsem.at[2 * (k - 1)], device_id=(px, py, pc),
                                         device_id_type=MESH).wait_recv()
            pltpu.make_async_remote_copy(src_ref=dm_ref.at[chip], dst_ref=dm_all.at[pdev],
                                         send_sem=ssem.at[2 * (k - 1) + 1], recv_sem=rsem.at[2 * (k - 1) + 1],
                                         device_id=(px, py, pc), device_id_type=MESH).wait_recv()
        for cp in sends:
            cp.wait_send()

        tot = sm_all[0]
        for q in range(1, N_DEV):
            tot = tot + sm_all[q]
        sum_ref[...] = tot

        cs = c_ref[...].reshape(N_DEV * 8, d)
        sc = (cs * jax.nn.sigmoid(cs)).astype(BF)
        for n0 in range(0, cols, chunk):
            dmv = dm_all[:, :, n0:n0 + chunk].reshape(N_DEV * 8, chunk).astype(BF)
            gw_ref[:, n0:n0 + chunk] = _tn(sc, dmv)

    return pl.pallas_call(
        body, name="small_sync",
        out_shape=[_sds(smalls.shape, F32), _sds((d, cols), F32)],
        scratch_shapes=[pltpu.VMEM((N_DEV,) + smalls.shape, F32), pltpu.VMEM((N_DEV, 8, cols), F32),
                        pltpu.SemaphoreType.DMA((2 * (N_DEV - 1),)), pltpu.SemaphoreType.DMA((2 * (N_DEV - 1),))],
        compiler_params=_cp())(smalls, dmod_blk, c_all)


def _bucket_onehot():
    maps = np.stack([_bucket_map(dil).reshape(-1) for _, dil in DIL_CONFIGS])
    return (jnp.asarray(maps)[:, None, :] == jnp.arange(N_BUCKETS, dtype=jnp.int32)[None, :, None]).astype(BF)


def _dil_bias(rel_t, onehot):
    def body(r_ref, oh_ref, o_ref):
        rv = r_ref[...]
        hi = rv.astype(BF)
        lo = (rv - hi.astype(F32)).astype(BF)
        for c in range(len(DIL_CONFIGS)):
            o_ref[c] = _nn(hi, oh_ref[c]) + _nn(lo, oh_ref[c])

    return pl.pallas_call(body, name="dil_bias",
                          out_shape=_sds((len(DIL_CONFIGS), N_HEADS, BLOCK * 2 * BLOCK), F32),
                          compiler_params=_cp())(rel_t, onehot)


def _rowsum8(a):
    def body(a_ref, o_ref):
        o_ref[...] = jnp.sum(a_ref[...], axis=0, keepdims=True)

    return pl.pallas_call(body, name="rowsum8", out_shape=_sds((1, a.shape[1]), F32), compiler_params=_cp())(a)


def _local_step(x, mod, target, wgu, wd, w_in, w_out, gains, rel_bias):
    nb, seq, d = x.shape
    t = nb * seq
    x0 = x.reshape(t, d)
    tgt = target.reshape(t, d)
    md = [mod[:, i:i + 1, :] for i in range(N_MOD)]
    sh1, sc1, gt1, sh2, sc2, gt2, sh3, sc3, gt3 = md
    g1, g2, g3 = gains["g_ffn1"], gains["g_mix"], gains["g_ffn2"]
    w_out2 = w_out.reshape(2 * D_GRP, d)
    ones_g = _group_ones()

    h1 = _modnorm(x0, g1, sc1, sh1, seq)
    a1, u1, s1 = _ffn_up(h1, wgu, 0)
    f1, x1 = _ffn_down(s1, wd, 0, x0, gt1, seq, 0.5)

    h2 = _modnorm(x1, g2, sc2, sh2, seq)
    qkv6 = _qkv_proj(h2, w_in)
    qkv6b = qkv6.reshape(6, nb, seq, D_GRP)
    o_sb, on_sb = _sb_fwd(qkv6b, gains["g_sb_out"], nb, seq)
    onehot = _bucket_onehot()
    bias = _dil_bias(rel_bias.T, onehot).reshape(len(DIL_CONFIGS), N_HEADS, BLOCK, 2 * BLOCK)
    o_cs, l_cs, qkv_rs = [], [], []
    for ci, (_, dil) in enumerate(DIL_CONFIGS):
        sub = seq // dil
        qr = qkv6.reshape(6, nb, sub, dil * D_GRP)
        o_c, l_c = _dil_fwd(qr, bias[ci], nb, sub, dil)
        qkv_rs.append(qr)
        o_cs.append(o_c.reshape(t, D_GRP))
        l_cs.append(l_c.reshape(t, D_GRP))
    o_dil, on_dil = _dil_comb(o_cs, l_cs, gains["g_dil_out"], ones_g)
    tmix, x2 = _mix_out(on_sb.reshape(t, D_GRP), on_dil, w_out2, x1, gt2, seq)

    h3 = _modnorm(x2, g3, sc3, sh3, seq)
    a3, u3, s3 = _ffn_up(h3, wgu, 1)
    f3, x3 = _ffn_down(s3, wd, 1, x2, gt3, seq, 0.5)

    dx3, dg_final, loss = _final_loss(x3, gains["g_final"], tgt)

    da3, du3, df3, dgt3 = _ffn_bwd_ds(dx3, gt3, f3, wd, 1, a3, u3, seq, 0.5)
    dgu2, dwd2 = _ffn_bwd_w(h3, da3, du3, s3, df3)
    dx2, dsh3, dsc3, dg3 = _ffn_bwd_dh(da3, du3, wgu, 1, x2, g3, sc3, dx3, seq)

    do_sb, do_dil, dgt2, dg_sb, dg_dil, dw_out = _mix_bwd_out(
        dx2, gt2, tmix, w_out2, o_sb.reshape(t, D_GRP), o_dil, on_sb.reshape(t, D_GRP), on_dil,
        gains["g_sb_out"], gains["g_dil_out"], ones_g, seq)
    dqkv6 = _sb_bwd(qkv6b, o_sb, do_sb.reshape(nb, seq, D_GRP), nb, seq)
    dcs = _dil_comb_bwd(do_dil, o_cs, l_cs, ones_g)
    dsum, a_tiles = [], []
    for ci, (_, dil) in enumerate(DIL_CONFIGS):
        sub = seq // dil
        do_c = dcs[ci].reshape(nb, sub, dil * D_GRP)
        dd_c = dcs[3 + ci].reshape(nb, sub, dil * D_GRP)
        dq_c, a_c = _dil_bwd(qkv_rs[ci], bias[ci], do_c, dd_c, nb, sub, dil)
        dsum.append(dq_c.reshape(3, t, D_GRP))
        a_tiles.append(a_c.reshape(N_HEADS, BLOCK * 2 * BLOCK))
    dqkv6 = _dqkv_dil_sum(dsum[0], dsum[1], dsum[2], dqkv6.reshape(6, t, D_GRP))
    drel = _relbias_grad(jnp.stack(a_tiles), onehot)
    dwin = _dw_in(h2, dqkv6)
    dx1, dsh2, dsc2, dg2 = _mix_bwd_dh(dqkv6, w_in, x1, g2, sc2, dx2, seq)

    da1, du1, df1, dgt1 = _ffn_bwd_ds(dx1, gt1, f1, wd, 0, a1, u1, seq, 0.5)
    dgu1, dwd1 = _ffn_bwd_w(h1, da1, du1, s1, df1)
    dx0, dsh1, dsc1, dg1 = _ffn_bwd_dh(da1, du1, wgu, 0, x0, g1, sc1, dx1, seq)

    dmod = jnp.concatenate([dsh1, dsc1, dgt1, dsh2, dsc2, dgt2, dsh3, dsc3, dgt3], axis=1)
    return dict(grad_x=dx0.reshape(nb, seq, d), loss=loss[0, 0], dmod=dmod.reshape(nb, N_MOD * d),
                dgu1=dgu1, dgu2=dgu2, dwd1=dwd1, dwd2=dwd2, dwin=dwin,
                dwout=dw_out.reshape(N_CHIPS, 1, 2 * D_GRP // N_CHIPS, d),
                dg_ffn1=dg1, dg_mix=dg2, dg_ffn2=dg3, dg_final=dg_final, dg_sb=dg_sb, dg_dil=dg_dil,
                drel=drel.T)


_SMALL_ORDER = (("b_ada", N_MOD * 1024), ("g_ffn1", 1024), ("g_mix", 1024), ("g_ffn2", 1024), ("g_final", 1024),
                ("g_sb_out", D_GRP), ("g_dil_out", D_GRP), ("rel_bias", N_BUCKETS * N_HEADS))


def _pack_small(parts, extra=None):
    flat = [parts[name].reshape(-1).astype(F32) for name, _ in _SMALL_ORDER]
    used = sum(sz for _, sz in _SMALL_ORDER)
    pad = SMALL_ROWS * 128 - used
    tail = jnp.zeros((pad,), F32)
    if extra is not None:
        tail = tail.at[0].set(extra)
    return jnp.concatenate(flat + [tail]).reshape(SMALL_ROWS, 128)


def _unpack_small(packed, shapes):
    flat = packed.reshape(-1)
    out, off = {}, 0
    for name, sz in _SMALL_ORDER:
        out[name] = flat[off:off + sz].reshape(shapes[name])
        off += sz
    return out, flat[off]


def kernel(x, c, w_ada, b_ada, g_ffn1, w1_gate, w1_up, w1_down, g_mix, w_in, g_sb_out, g_dil_out, w_out, rel_bias, g_ffn2, w2_gate, w2_up, w2_down, g_final, loss_target, m_w_ada, m_b_ada, m_g_ffn1, m_w1_gate, m_w1_up, m_w1_down, m_g_mix, m_w_in, m_g_sb_out, m_g_dil_out, m_w_out, m_rel_bias, m_g_ffn2, m_w2_gate, m_w2_up, m_w2_down, m_g_final, v_w_ada, v_b_ada, v_g_ffn1, v_w1_gate, v_w1_up, v_w1_down, v_g_mix, v_w_in, v_g_sb_out, v_g_dil_out, v_w_out, v_rel_bias, v_g_ffn2, v_w2_gate, v_w2_up, v_w2_down, v_g_final):
    nb, seq, d = x.shape
    xi, yi, ci = lax.axis_index("x"), lax.axis_index("y"), lax.axis_index("c")
    chip = 2 * xi + yi
    ada_cols = w_ada.shape[-1]

    c_pad = jnp.zeros((8, d), F32).at[:nb].set(c)
    b_shard = lax.dynamic_slice(b_ada, (0, chip * ada_cols), (1, ada_cols))
    c_all, mod_blk = _ada_fwd(c_pad, w_ada[0], b_shard)
    mod = jnp.transpose(mod_blk[:, :nb, :], (1, 0, 2)).reshape(nb, N_MOD, d)

    shards = [jnp.stack([w1_gate[0], w1_up[0], w2_gate[0], w2_up[0]]).astype(BF),
              jnp.stack([w1_down[0], w2_down[0]]).astype(BF),
              w_in.astype(BF), w_out.astype(BF)]
    bufs = [lax.dynamic_update_slice(lax.empty((N_CHIPS,) + s.shape, BF), s[None], (chip, 0, 0, 0)) for s in shards]
    wgu, wd, w_in_g, w_out_g = _ag_weights(bufs)

    gains = dict(g_ffn1=g_ffn1, g_mix=g_mix, g_ffn2=g_ffn2, g_final=g_final.reshape(1, d),
                 g_sb_out=g_sb_out.reshape(1, D_GRP), g_dil_out=g_dil_out.reshape(1, D_GRP))
    r = _local_step(x, mod, loss_target, wgu, wd, w_in_g, w_out_g, gains, rel_bias)

    grads = [r["dgu1"], r["dgu2"], r["dwd1"], r["dwd2"], r["dwin"], r["dwout"]]
    lands = _rs_d2d(grads)
    core = jnp.reshape(ci, (1,)).astype(jnp.int32)
    parts = [_add_halves(core, g, l) for g, l in zip(grads, lands)]
    lands2 = _rs_ici(parts)
    place = jnp.stack([ci, chip]).astype(jnp.int32)
    halves = [_sum_chips(place, p, l) for p, l in zip(parts, lands2)]
    gu1, gu2, gd1, gd2, gwin, gwout = _rs_final(halves)

    dmod = r["dmod"]
    dmod_pad = jnp.zeros((8, N_MOD * d), F32).at[:nb].set(dmod)
    dmod_blk = jnp.transpose(dmod_pad.reshape(8, N_CHIPS, ada_cols), (1, 0, 2))
    small_parts = dict(b_ada=_rowsum8(dmod_pad), g_ffn1=r["dg_ffn1"], g_mix=r["dg_mix"], g_ffn2=r["dg_ffn2"],
                       g_final=r["dg_final"], g_sb_out=r["dg_sb"], g_dil_out=r["dg_dil"], rel_bias=r["drel"])
    small_sum, g_wada = _small_sync(_pack_small(small_parts, r["loss"]), dmod_blk, c_all)

    small_w = dict(b_ada=b_ada, g_ffn1=g_ffn1, g_mix=g_mix, g_ffn2=g_ffn2, g_final=g_final,
                   g_sb_out=g_sb_out, g_dil_out=g_dil_out, rel_bias=rel_bias)
    small_m = dict(b_ada=m_b_ada, g_ffn1=m_g_ffn1, g_mix=m_g_mix, g_ffn2=m_g_ffn2, g_final=m_g_final,
                   g_sb_out=m_g_sb_out, g_dil_out=m_g_dil_out, rel_bias=m_rel_bias)
    small_v = dict(b_ada=v_b_ada, g_ffn1=v_g_ffn1, g_mix=v_g_mix, g_ffn2=v_g_ffn2, g_final=v_g_final,
                   g_sb_out=v_g_sb_out, g_dil_out=v_g_dil_out, rel_bias=v_rel_bias)
    shapes = {k: v.shape for k, v in small_w.items()}
    sg, sd, sm, sv = _adamw(_pack_small(small_w), small_sum.reshape(1, SMALL_ROWS, 128), 0,
                            _pack_small(small_m), _pack_small(small_v))
    sg, loss = _unpack_small(sg, shapes)
    sd, _ = _unpack_small(sd, shapes)
    sm, _ = _unpack_small(sm, shapes)
    sv, _ = _unpack_small(sv, shapes)

    big = {}

    def upd(name, w, g_arr, sel, m, v):
        shape = w.shape
        res = _adamw(w.reshape(shape[-2:]), g_arr, sel, m.reshape(shape[-2:]), v.reshape(shape[-2:]))
        big[name] = [a.reshape(shape) for a in res]

    upd("w_ada", w_ada, g_wada.reshape(1, d, ada_cols), 0, m_w_ada, v_w_ada)
    upd("w1_gate", w1_gate, gu1, 0, m_w1_gate, v_w1_gate)
    upd("w1_up", w1_up, gu1, 1, m_w1_up, v_w1_up)
    upd("w1_down", w1_down, gd1, 0, m_w1_down, v_w1_down)
    upd("w_in", w_in, gwin, 0, m_w_in, v_w_in)
    upd("w_out", w_out, gwout, 0, m_w_out, v_w_out)
    upd("w2_gate", w2_gate, gu2, 0, m_w2_gate, v_w2_gate)
    upd("w2_up", w2_up, gu2, 1, m_w2_up, v_w2_up)
    upd("w2_down", w2_down, gd2, 0, m_w2_down, v_w2_down)

    names = ["w_ada", "b_ada", "g_ffn1", "w1_gate", "w1_up", "w1_down", "g_mix", "w_in", "g_sb_out", "g_dil_out",
             "w_out", "rel_bias", "g_ffn2", "w2_gate", "w2_up", "w2_down", "g_final"]
    outs = [loss, r["grad_x"]]
    for k, small in enumerate((sg, sd, sm, sv)):
        for name in names:
            outs.append(big[name][k] if name in big else small[name])
    return tuple(outs)
```

```python
import functools
import math

import numpy as np
import jax
import jax.numpy as jnp
from jax import lax
from jax.experimental import pallas as pl
from jax.experimental.pallas import tpu as pltpu

F32 = jnp.float32
BF = jnp.bfloat16
MESH = pl.DeviceIdType.MESH

HEAD_DIM = 64
N_HEADS = 8
D_GRP = N_HEADS * HEAD_DIM
DIL_CONFIGS = ((128, 1), (512, 4), (2048, 16))
N_STEPS = 128
BLOCK = 128
N_BUCKETS = 32
MAX_DISTANCE = 2048
N_MOD = 9
EPS = 1e-6
NEG_INF = -1e30
SCALE = HEAD_DIM ** -0.5

ADAM_LR = 0.001
ADAM_B1 = 0.9
ADAM_B2 = 0.999
ADAM_EPS = 1e-08
ADAM_WD = 0.01
ADAM_STEP = 10

N_CHIPS = 4
N_DEV = 8
VMEM_LIMIT = 56 * 1024 * 1024
TM = 512
TQ = 256
KB = 256
SMALL_ROWS = 120


def _cp(n_axes=0, **kw):
    sem = ("arbitrary",) * n_axes if n_axes else None
    return pltpu.CompilerParams(dimension_semantics=sem, vmem_limit_bytes=VMEM_LIMIT, **kw)


def _nn(a, b):
    return jnp.dot(a, b, preferred_element_type=F32)


def _nt(a, b):
    return lax.dot_general(a, b, (((1,), (1,)), ((), ())), preferred_element_type=F32)


def _tn(a, b):
    return lax.dot_general(a, b, (((0,), (0,)), ((), ())), preferred_element_type=F32)


def _nn2(x, m):
    hi = x.astype(BF)
    lo = (x - hi.astype(F32)).astype(BF)
    return _nn(hi, m) + _nn(lo, m)


def _softplus(z):
    return jnp.maximum(z, 0.0) + jnp.log1p(jnp.exp(-jnp.abs(z)))


def _sds(shape, dtype):
    return jax.ShapeDtypeStruct(shape, dtype)


def _modnorm(x, g, sc, sh, seq):
    t, d = x.shape
    per = seq // TM

    def body(x_ref, g_ref, sc_ref, sh_ref, h_ref):
        xv = x_ref[...]
        r = lax.rsqrt(jnp.mean(xv * xv, axis=-1, keepdims=True) + EPS)
        h_ref[...] = (((xv * r) * g_ref[...]) * (1.0 + sc_ref[...]) + sh_ref[...]).astype(BF)

    return pl.pallas_call(
        body, name="modnorm", grid=(t // TM,),
        in_specs=[pl.BlockSpec((TM, d), lambda m: (m, 0)),
                  pl.BlockSpec((1, d), lambda m: (0, 0)),
                  pl.BlockSpec((None, 1, d), lambda m: (m // per, 0, 0)),
                  pl.BlockSpec((None, 1, d), lambda m: (m // per, 0, 0))],
        out_specs=pl.BlockSpec((TM, d), lambda m: (m, 0)),
        out_shape=_sds((t, d), BF), compiler_params=_cp(1))(x, g, sc, sh)


def _modnorm_bwd_tile(dh, xv, gv, scv, dxo):
    r = lax.rsqrt(jnp.mean(xv * xv, axis=-1, keepdims=True) + EPS)
    n = xv * r
    ng = n * gv
    dsh = jnp.sum(dh, axis=0, keepdims=True)
    dsc = jnp.sum(dh * ng, axis=0, keepdims=True)
    dy = dh * (1.0 + scv)
    dg = jnp.sum(dy * n, axis=0, keepdims=True)
    dn = dy * gv
    dx = dxo + r * (dn - n * jnp.mean(dn * n, axis=-1, keepdims=True))
    return dx, dsh, dsc, dg


def _acc_rows(ref, val, first):
    @pl.when(first)
    def _():
        ref[...] = val

    @pl.when(jnp.logical_not(first))
    def _():
        ref[...] += val


def _ffn_up(h, wgu, f_idx):
    t, d = h.shape
    fs = wgu.shape[-1]

    def body(h_ref, w_ref, a_ref, u_ref, s_ref):
        hv = h_ref[...]
        a = _nn(hv, w_ref[0])
        u = _nn(hv, w_ref[1])
        a_ref[...] = a
        u_ref[...] = u
        s_ref[...] = ((a * jax.nn.sigmoid(a)) * u).astype(BF)

    blk = pl.BlockSpec((None, TM, fs), lambda j, m: (j, m, 0))
    return pl.pallas_call(
        body, name="ffn_up", grid=(N_CHIPS, t // TM),
        in_specs=[pl.BlockSpec((TM, d), lambda j, m: (m, 0)),
                  pl.BlockSpec((None, 2, d, fs), lambda j, m: (j, f_idx, 0, 0))],
        out_specs=[blk, blk, blk],
        out_shape=[_sds((N_CHIPS, t, fs), F32), _sds((N_CHIPS, t, fs), F32), _sds((N_CHIPS, t, fs), BF)],
        compiler_params=_cp(2))(h, wgu)


def _ffn_down(s, wd, f_idx, x, gt, seq, coef):
    _, t, fs = s.shape
    d = x.shape[-1]
    per = seq // TM

    def body(s_ref, w_ref, x_ref, gt_ref, f_ref, xo_ref, acc):
        j = pl.program_id(1)

        @pl.when(j == 0)
        def _():
            acc[...] = jnp.zeros_like(acc)

        acc[...] += _nn(s_ref[...], w_ref[...])

        @pl.when(j == N_CHIPS - 1)
        def _():
            f = acc[...]
            f_ref[...] = f
            xo_ref[...] = x_ref[...] + (coef * gt_ref[...]) * f

    row = pl.BlockSpec((TM, d), lambda m, j: (m, 0))
    return pl.pallas_call(
        body, name="ffn_down", grid=(t // TM, N_CHIPS),
        in_specs=[pl.BlockSpec((None, TM, fs), lambda m, j: (j, m, 0)),
                  pl.BlockSpec((None, None, fs, d), lambda m, j: (j, f_idx, 0, 0)),
                  row,
                  pl.BlockSpec((None, 1, d), lambda m, j: (m // per, 0, 0))],
        out_specs=[row, row],
        out_shape=[_sds((t, d), F32), _sds((t, d), F32)],
        scratch_shapes=[pltpu.VMEM((TM, d), F32)],
        compiler_params=_cp(2))(s, wd, x, gt)


def _ffn_bwd_ds(dxo, gt, f, wd, f_idx, a, u, seq, coef):
    t, d = dxo.shape
    fs = a.shape[-1]
    per = seq // TM
    nb = t // seq

    def body(dxo_ref, gt_ref, f_ref, w_ref, a_ref, u_ref, da_ref, du_ref, df_ref, dgt_ref):
        m = pl.program_id(0)
        j = pl.program_id(1)
        dxv = dxo_ref[...]
        df = ((coef * gt_ref[...]) * dxv).astype(BF)

        @pl.when(j == 0)
        def _():
            df_ref[...] = df
            part = coef * jnp.sum(dxv * f_ref[...], axis=0, keepdims=True)
            _acc_rows(dgt_ref, part, m % per == 0)

        ds = _nt(df, w_ref[...])
        av = a_ref[...]
        sig = jax.nn.sigmoid(av)
        da_ref[...] = (ds * u_ref[...] * (sig * (1.0 + av * (1.0 - sig)))).astype(BF)
        du_ref[...] = (ds * (av * sig)).astype(BF)

    row = pl.BlockSpec((TM, d), lambda m, j: (m, 0))
    blk = pl.BlockSpec((None, TM, fs), lambda m, j: (j, m, 0))
    ex = pl.BlockSpec((None, 1, d), lambda m, j: (m // per, 0, 0))
    return pl.pallas_call(
        body, name="ffn_bwd_ds", grid=(t // TM, N_CHIPS),
        in_specs=[row, ex, row,
                  pl.BlockSpec((None, None, fs, d), lambda m, j: (j, f_idx, 0, 0)),
                  blk, blk],
        out_specs=[blk, blk, row, ex],
        out_shape=[_sds((N_CHIPS, t, fs), BF), _sds((N_CHIPS, t, fs), BF), _sds((t, d), BF),
                   _sds((nb, 1, d), F32)],
        compiler_params=_cp(2))(dxo, gt, f, wd, a, u)


def _ffn_bwd_w(h, da, du, s, df):
    t, d = h.shape
    fs = da.shape[-1]

    def body(h_ref, da_ref, du_ref, s_ref, df_ref, dgu_ref, dwd_ref):
        kt = pl.program_id(1)
        hv = h_ref[...]
        pg = _tn(hv, da_ref[...])
        pu = _tn(hv, du_ref[...])
        pd = _tn(s_ref[...], df_ref[...])

        @pl.when(kt == 0)
        def _():
            dgu_ref[0] = pg
            dgu_ref[1] = pu
            dwd_ref[...] = pd

        @pl.when(kt != 0)
        def _():
            dgu_ref[0] += pg
            dgu_ref[1] += pu
            dwd_ref[...] += pd

    row = pl.BlockSpec((TM, d), lambda j, kt: (kt, 0))
    blk = pl.BlockSpec((None, TM, fs), lambda j, kt: (j, kt, 0))
    return pl.pallas_call(
        body, name="ffn_bwd_w", grid=(N_CHIPS, t // TM),
        in_specs=[row, blk, blk, blk, row],
        out_specs=[pl.BlockSpec((None, 2, d, fs), lambda j, kt: (j, 0, 0, 0)),
                   pl.BlockSpec((None, None, fs, d), lambda j, kt: (j, 0, 0, 0))],
        out_shape=[_sds((N_CHIPS, 2, d, fs), F32), _sds((N_CHIPS, 1, fs, d), F32)],
        compiler_params=_cp(2))(h, da, du, s, df)


def _ffn_bwd_dh(da, du, wgu, f_idx, x, g, sc, dxo, seq):
    _, t, fs = da.shape
    d = x.shape[-1]
    per = seq // TM
    nb = t // seq

    def body(da_ref, du_ref, w_ref, x_ref, g_ref, sc_ref, dxo_ref, dx_ref, dsh_ref, dsc_ref, dg_ref, acc):
        m = pl.program_id(0)
        j = pl.program_id(1)

        @pl.when(j == 0)
        def _():
            acc[...] = jnp.zeros_like(acc)

        acc[...] += _nt(da_ref[...], w_ref[0]) + _nt(du_ref[...], w_ref[1])

        @pl.when(j == N_CHIPS - 1)
        def _():
            dx, dsh, dsc, dg = _modnorm_bwd_tile(acc[...], x_ref[...], g_ref[...], sc_ref[...], dxo_ref[...])
            dx_ref[...] = dx
            _acc_rows(dsh_ref, dsh, m % per == 0)
            _acc_rows(dsc_ref, dsc, m % per == 0)
            _acc_rows(dg_ref, dg, m == 0)

    row = pl.BlockSpec((TM, d), lambda m, j: (m, 0))
    blk = pl.BlockSpec((None, TM, fs), lambda m, j: (j, m, 0))
    ex = pl.BlockSpec((None, 1, d), lambda m, j: (m // per, 0, 0))
    vec = pl.BlockSpec((1, d), lambda m, j: (0, 0))
    return pl.pallas_call(
        body, name="ffn_bwd_dh", grid=(t // TM, N_CHIPS),
        in_specs=[blk, blk, pl.BlockSpec((None, 2, d, fs), lambda m, j: (j, f_idx, 0, 0)),
                  row, vec, ex, row],
        out_specs=[row, ex, ex, vec],
        out_shape=[_sds((t, d), F32), _sds((nb, 1, d), F32), _sds((nb, 1, d), F32), _sds((1, d), F32)],
        scratch_shapes=[pltpu.VMEM((TM, d), F32)],
        compiler_params=_cp(2))(da, du, wgu, x, g, sc, dxo)


def _qkv_proj(h, w_in):
    t, d = h.shape
    wc = w_in.shape[-1]
    nt = 256
    per_chip = wc // nt
    n_tiles = N_CHIPS * per_chip
    per_out = D_GRP // nt

    def body(h_ref, w_ref, o_ref):
        o_ref[...] = _nn(h_ref[...], w_ref[...]).astype(BF)

    return pl.pallas_call(
        body, name="qkv_proj", grid=(n_tiles, t // TM),
        in_specs=[pl.BlockSpec((TM, d), lambda n, m: (m, 0)),
                  pl.BlockSpec((None, None, d, nt), lambda n, m: (n // per_chip, 0, 0, n % per_chip))],
        out_specs=pl.BlockSpec((None, TM, nt), lambda n, m: (n // per_out, m, n % per_out)),
        out_shape=_sds((6, t, D_GRP), BF), compiler_params=_cp(2))(h, w_in)


def _mix_out(on_sb, on_dil, w_out, x, gt, seq):
    t, d = x.shape
    per = seq // TM

    def body(a_ref, b_ref, w_ref, x_ref, gt_ref, t_ref, xo_ref):
        tv = _nn(a_ref[...], w_ref[0:D_GRP, :]) + _nn(b_ref[...], w_ref[D_GRP:2 * D_GRP, :])
        t_ref[...] = tv
        xo_ref[...] = x_ref[...] + gt_ref[...] * tv

    row = pl.BlockSpec((TM, d), lambda m: (m, 0))
    half = pl.BlockSpec((TM, D_GRP), lambda m: (m, 0))
    return pl.pallas_call(
        body, name="mix_out", grid=(t // TM,),
        in_specs=[half, half, pl.BlockSpec((2 * D_GRP, d), lambda m: (0, 0)), row,
                  pl.BlockSpec((None, 1, d), lambda m: (m // per, 0, 0))],
        out_specs=[row, row],
        out_shape=[_sds((t, d), F32), _sds((t, d), F32)],
        compiler_params=_cp(1))(on_sb, on_dil, w_out, x, gt)


def _sb_masks():
    lane = lax.broadcasted_iota(jnp.int32, (1, 2 * HEAD_DIM), 1)
    hm0 = lane < HEAD_DIM
    rel = lax.broadcasted_iota(jnp.int32, (TQ, KB), 0) - lax.broadcasted_iota(jnp.int32, (TQ, KB), 1)
    kr = lax.broadcasted_iota(jnp.int32, (KB, KB), 0)
    kc = lax.broadcasted_iota(jnp.int32, (KB, KB), 1)
    return hm0, rel, kr, kc


def _headnorm_pair(o, gv, hm0):
    o2 = o * o
    ms0 = jnp.sum(jnp.where(hm0, o2, 0.0), axis=-1, keepdims=True) * (1.0 / HEAD_DIM)
    ms1 = jnp.sum(jnp.where(hm0, 0.0, o2), axis=-1, keepdims=True) * (1.0 / HEAD_DIM)
    r = jnp.where(hm0, lax.rsqrt(ms0 + EPS), lax.rsqrt(ms1 + EPS))
    return (o * r) * gv


SB_DEAD = -104.0


def _alive(c_l):
    return (jnp.max(c_l) > SB_DEAD).astype(jnp.int32)


def _sb_fwd(qkv6, g_sb, nb, seq):
    nq = seq // TQ

    def body(q_ref, k_ref, v_ref, g_ref, o_ref, on_ref):
        qi = pl.program_id(2)
        hm0, rel, kr, kc = _sb_masks()
        upper = (kr > kc).astype(BF)
        qv = q_ref[...]
        outs = []
        for hh in range(2):
            hm = hm0 if hh == 0 else jnp.logical_not(hm0)
            qh = jnp.where(hm, qv, jnp.zeros_like(qv))

            def block(kj, c_l, causal, qh=qh):
                ks = pl.multiple_of(kj * KB, KB)
                z = _nt(qh, k_ref[pl.ds(ks, KB), :]) * SCALE
                sp = _softplus(z)
                ln = -sp if causal is None else jnp.where(causal, -sp, 0.0)
                suf = _nn2(ln, upper)
                w = jnp.exp((z - sp) + (suf + c_l))
                if causal is not None:
                    w = jnp.where(causal, w, 0.0)
                pv = _nn(w.astype(BF), v_ref[pl.ds(ks, KB), :])
                return pv, c_l + (suf[:, 0:1] + ln[:, 0:1])

            acc, c_l = block(qi, jnp.zeros((TQ, 1), F32), rel > 0)

            def cond(carry):
                it, alive, _, _ = carry
                return jnp.logical_and(it <= qi, alive > 0)

            def kbody(carry):
                it, _, c_l, acc = carry
                pv, c_l = block(qi - it, c_l, None)
                return it + 1, _alive(c_l), c_l, acc + pv

            _, _, _, acc = lax.while_loop(cond, kbody, (jnp.int32(1), _alive(c_l), c_l, acc))
            outs.append(acc)
        o = jnp.where(hm0, outs[0], outs[1])
        o_ref[...] = o
        on_ref[...] = _headnorm_pair(o, g_ref[...], hm0).astype(BF)

    w = 2 * HEAD_DIM
    full = lambda i: pl.BlockSpec((None, None, seq, w), lambda b, hp, q: (i, b, 0, hp))
    qblk = pl.BlockSpec((None, None, TQ, w), lambda b, hp, q: (0, b, q, hp))
    oblk = pl.BlockSpec((None, TQ, w), lambda b, hp, q: (b, q, hp))
    return pl.pallas_call(
        body, name="sb_fwd", grid=(nb, N_HEADS // 2, nq),
        in_specs=[qblk, full(1), full(2), pl.BlockSpec((1, w), lambda b, hp, q: (0, hp))],
        out_specs=[oblk, oblk],
        out_shape=[_sds((nb, seq, D_GRP), F32), _sds((nb, seq, D_GRP), BF)],
        compiler_params=_cp(3))(qkv6, qkv6, qkv6, g_sb)


def _sb_bwd(qkv6, o_raw, do, nb, seq):
    nq = seq // TQ
    nk = seq // KB

    def body(q_ref, k_ref, v_ref, do_ref, out_ref, dk_acc, dv_acc, car):
        qi = pl.program_id(2)
        hm0, rel, kr, kc = _sb_masks()
        upper = (kr > kc).astype(BF)
        lower = (kr < kc).astype(BF)
        ones = jnp.ones((KB, 2 * HEAD_DIM), BF)

        @pl.when(qi == 0)
        def _():
            dk_acc[...] = jnp.zeros_like(dk_acc)
            dv_acc[...] = jnp.zeros_like(dv_acc)

        qv = q_ref[...]
        dov = do_ref[...]
        dqs = []
        for hh in range(2):
            hm = hm0 if hh == 0 else jnp.logical_not(hm0)
            qh = jnp.where(hm, qv, jnp.zeros_like(qv))
            doh = jnp.where(hm, dov, 0.0).astype(BF)

            def logits(kj, causal, qh=qh):
                ks = pl.multiple_of(kj * KB, KB)
                kb = k_ref[pl.ds(ks, KB), :]
                z = _nt(qh, kb) * SCALE
                sp = _softplus(z)
                ln = -sp if causal is None else jnp.where(causal, -sp, 0.0)
                return ks, kb, z - sp, ln

            car[qi] = jnp.zeros((TQ, 1), F32)
            c_l = _nn2(logits(qi, rel > 0)[3], ones)[:, 0:1]

            def acond(carry):
                it, alive, _ = carry
                return jnp.logical_and(it <= qi, alive > 0)

            def abody(carry):
                it, _, c_l = carry
                car[qi - it] = c_l
                c_l = c_l + _nn2(logits(qi - it, None)[3], ones)[:, 0:1]
                return it + 1, _alive(c_l), c_l

            n_used, _, _ = lax.while_loop(acond, abody, (jnp.int32(1), _alive(c_l), c_l))

            def grads(kj, causal, c_g, dq, doh=doh, qh=qh):
                ks, kb, lsz, ln = logits(kj, causal)
                vb = v_ref[pl.ds(ks, KB), :]
                w = jnp.exp(lsz + (_nn2(ln, upper) + car[kj]))
                if causal is not None:
                    w = jnp.where(causal, w, 0.0)
                g = w * _nt(doh, vb)
                pre = _nn2(g, lower)
                sig = jnp.exp(lsz)
                dz = g * (1.0 - sig) - sig * (pre + c_g)
                if causal is not None:
                    dz = jnp.where(causal, dz, 0.0)
                dzb = (dz * SCALE).astype(BF)
                dk_acc[pl.ds(ks, KB), :] += _tn(dzb, qh)
                dv_acc[pl.ds(ks, KB), :] += _tn(w.astype(BF), doh)
                return c_g + (pre[:, KB - 1:KB] + g[:, KB - 1:KB]), dq + _nn(dzb, kb)

            c_g, dq = lax.fori_loop(qi - n_used + 1, qi, lambda kj, cr: grads(kj, None, *cr),
                                    (jnp.zeros((TQ, 1), F32), jnp.zeros((TQ, 2 * HEAD_DIM), F32)))
            _, dq = grads(qi, rel > 0, c_g, dq)
            dqs.append(dq)
        dq = jnp.where(hm0, dqs[0], dqs[1])
        out_ref[0, pl.ds(pl.multiple_of(qi * TQ, TQ), TQ), :] = dq.astype(BF)

        @pl.when(qi == nq - 1)
        def _():
            out_ref[1] = dk_acc[...].astype(BF)
            out_ref[2] = dv_acc[...].astype(BF)

    w = 2 * HEAD_DIM
    full = lambda i: pl.BlockSpec((None, None, seq, w), lambda b, hp, q: (i, b, 0, hp))
    qblk = pl.BlockSpec((None, None, TQ, w), lambda b, hp, q: (0, b, q, hp))
    oblk = pl.BlockSpec((None, TQ, w), lambda b, hp, q: (b, q, hp))
    del o_raw
    return pl.pallas_call(
        body, name="sb_bwd", grid=(nb, N_HEADS // 2, nq),
        in_specs=[qblk, full(1), full(2), oblk],
        out_specs=pl.BlockSpec((3, None, seq, w), lambda b, hp, q: (0, b, 0, hp)),
        out_shape=_sds((6, nb, seq, D_GRP), BF),
        scratch_shapes=[pltpu.VMEM((seq, w), F32), pltpu.VMEM((seq, w), F32), pltpu.VMEM((nk, TQ, 1), F32)],
        compiler_params=_cp(3))(qkv6, qkv6, qkv6, do)


def _t5_bucket(n):
    max_exact = N_BUCKETS // 2
    nf = np.maximum(n, 1).astype(np.float32)
    large = max_exact + (np.log(nf / max_exact) / math.log(MAX_DISTANCE / max_exact)
                         * (N_BUCKETS - max_exact)).astype(np.int32)
    large = np.minimum(large, N_BUCKETS - 1)
    return np.where(n < max_exact, n, large).astype(np.int32)


def _bucket_map(dilation):
    step = BLOCK + np.arange(BLOCK)[:, None] - np.arange(2 * BLOCK)[None, :]
    return _t5_bucket(np.clip(step, 0, N_STEPS) * dilation)


def _dil_masks():
    lane = lax.broadcasted_iota(jnp.int32, (1, 2 * HEAD_DIM), 1)
    hm0 = lane < HEAD_DIM
    iq = lax.broadcasted_iota(jnp.int32, (BLOCK, BLOCK), 0)
    ik = lax.broadcasted_iota(jnp.int32, (BLOCK, BLOCK), 1)
    return hm0, ik <= iq, ik >= iq


def _dil_probs(qh, kc, kp, b_ref, hh, valid_c, valid_p):
    zc = _nt(qh, kc) * SCALE + b_ref[hh, :, BLOCK:2 * BLOCK]
    zp = _nt(qh, kp) * SCALE + b_ref[hh, :, 0:BLOCK]
    zc = jnp.where(valid_c, zc, NEG_INF)
    zp = jnp.where(valid_p, zp, NEG_INF)
    m = jnp.maximum(jnp.max(zc, axis=-1, keepdims=True), jnp.max(zp, axis=-1, keepdims=True))
    ec = jnp.exp(zc - m)
    ep = jnp.exp(zp - m)
    den = jnp.sum(ec, axis=-1, keepdims=True) + jnp.sum(ep, axis=-1, keepdims=True)
    return ec, ep, den, m


def _dil_fwd(qkv6r, bias, nb, sub_len, dilation):
    n_blk = sub_len // BLOCK
    w = 2 * HEAD_DIM

    def body(q_ref, k_ref, v_ref, b_ref, o_ref, l_ref):
        hm0, valid_c, valid_p0 = _dil_masks()

        def nbody(n, carry):
            rs = pl.multiple_of(n * BLOCK, BLOCK)
            ps = pl.multiple_of(jnp.maximum(n - 1, 0) * BLOCK, BLOCK)
            qv = q_ref[pl.ds(rs, BLOCK), :]
            kc = k_ref[pl.ds(rs, BLOCK), :]
            kp = k_ref[pl.ds(ps, BLOCK), :]
            vc = v_ref[pl.ds(rs, BLOCK), :]
            vp = v_ref[pl.ds(ps, BLOCK), :]
            valid_p = jnp.logical_and(valid_p0, n > 0)
            os, ls = [], []
            for hh in range(2):
                hm = hm0 if hh == 0 else jnp.logical_not(hm0)
                qh = jnp.where(hm, qv, jnp.zeros_like(qv))
                ec, ep, den, m = _dil_probs(qh, kc, kp, b_ref, hh, valid_c, valid_p)
                os.append((_nn(ec.astype(BF), vc) + _nn(ep.astype(BF), vp)) / den)
                ls.append(m + jnp.log(den))
            o_ref[pl.ds(rs, BLOCK), :] = jnp.where(hm0, os[0], os[1])
            l_ref[pl.ds(rs, BLOCK), :] = jnp.where(hm0, ls[0], ls[1])
            return carry

        lax.fori_loop(0, n_blk, nbody, 0)

    seqblk = lambda i: pl.BlockSpec((None, None, sub_len, w), lambda b, g: (i, b, 0, g))
    oblk = pl.BlockSpec((None, sub_len, w), lambda b, g: (b, 0, g))
    shp = _sds((nb, sub_len, dilation * D_GRP), F32)
    return pl.pallas_call(
        body, name="dil_fwd_%d" % dilation, grid=(nb, dilation * (N_HEADS // 2)),
        in_specs=[seqblk(3), seqblk(4), seqblk(5),
                  pl.BlockSpec((2, BLOCK, 2 * BLOCK), lambda b, g: (g % (N_HEADS // 2), 0, 0))],
        out_specs=[oblk, oblk], out_shape=[shp, shp],
        compiler_params=_cp(2))(qkv6r, qkv6r, qkv6r, bias)


def _dil_bwd(qkv6r, bias, do_c, dd_c, nb, sub_len, dilation):
    n_blk = sub_len // BLOCK
    w = 2 * HEAD_DIM
    hp_n = N_HEADS // 2

    def body(q_ref, k_ref, v_ref, b_ref, do_ref, dd_ref, out_ref, a_ref, dk_acc, dv_acc):
        hm0, valid_c, valid_p0 = _dil_masks()
        first = jnp.logical_and(pl.program_id(1) == 0, pl.program_id(2) == 0)

        @pl.when(first)
        def _():
            a_ref[...] = jnp.zeros_like(a_ref)

        dk_acc[...] = jnp.zeros_like(dk_acc)
        dv_acc[...] = jnp.zeros_like(dv_acc)

        def nbody(n, carry):
            rs = pl.multiple_of(n * BLOCK, BLOCK)
            ps = pl.multiple_of(jnp.maximum(n - 1, 0) * BLOCK, BLOCK)
            qv = q_ref[pl.ds(rs, BLOCK), :]
            kc = k_ref[pl.ds(rs, BLOCK), :]
            kp = k_ref[pl.ds(ps, BLOCK), :]
            vc = v_ref[pl.ds(rs, BLOCK), :]
            vp = v_ref[pl.ds(ps, BLOCK), :]
            dov = do_ref[pl.ds(rs, BLOCK), :]
            ddv = dd_ref[pl.ds(rs, BLOCK), :]
            valid_p = jnp.logical_and(valid_p0, n > 0)
            dq = jnp.zeros((BLOCK, w), F32)
            for hh in range(2):
                hm = hm0 if hh == 0 else jnp.logical_not(hm0)
                qh = jnp.where(hm, qv, jnp.zeros_like(qv))
                doh = jnp.where(hm, dov, 0.0).astype(BF)
                ddh = jnp.sum(jnp.where(hm, ddv, 0.0), axis=-1, keepdims=True) * (1.0 / HEAD_DIM)
                ec, ep, den, _ = _dil_probs(qh, kc, kp, b_ref, hh, valid_c, valid_p)
                inv = 1.0 / den
                pc = ec * inv
                pp = ep * inv
                dzc = pc * (_nt(doh, vc) + ddh)
                dzp = pp * (_nt(doh, vp) + ddh)
                a_ref[hh, :, BLOCK:2 * BLOCK] += dzc
                a_ref[hh, :, 0:BLOCK] += dzp
                dzcb = (dzc * SCALE).astype(BF)
                dzpb = (dzp * SCALE).astype(BF)
                dq = jnp.where(hm, _nn(dzcb, kc) + _nn(dzpb, kp), dq)
                dk_acc[pl.ds(rs, BLOCK), :] += _tn(dzcb, qh)
                dk_acc[pl.ds(ps, BLOCK), :] += _tn(dzpb, qh)
                dv_acc[pl.ds(rs, BLOCK), :] += _tn(pc.astype(BF), doh)
                dv_acc[pl.ds(ps, BLOCK), :] += _tn(pp.astype(BF), doh)
            out_ref[0, pl.ds(rs, BLOCK), :] = dq
            return carry

        lax.fori_loop(0, n_blk, nbody, 0)
        out_ref[1] = dk_acc[...]
        out_ref[2] = dv_acc[...]

    col = lambda hp, b, r: r * hp_n + hp
    seqblk = lambda i: pl.BlockSpec((None, None, sub_len, w), lambda hp, b, r: (i, b, 0, col(hp, b, r)))
    oblk = pl.BlockSpec((None, sub_len, w), lambda hp, b, r: (b, 0, col(hp, b, r)))
    return pl.pallas_call(
        body, name="dil_bwd_%d" % dilation, grid=(hp_n, nb, dilation),
        in_specs=[seqblk(3), seqblk(4), seqblk(5),
                  pl.BlockSpec((2, BLOCK, 2 * BLOCK), lambda hp, b, r: (hp, 0, 0)), oblk, oblk],
        out_specs=[pl.BlockSpec((3, None, sub_len, w), lambda hp, b, r: (0, b, 0, col(hp, b, r))),
                   pl.BlockSpec((2, BLOCK, 2 * BLOCK), lambda hp, b, r: (hp, 0, 0))],
        out_shape=[_sds((3, nb, sub_len, dilation * D_GRP), F32), _sds((N_HEADS, BLOCK, 2 * BLOCK), F32)],
        scratch_shapes=[pltpu.VMEM((sub_len, w), F32), pltpu.VMEM((sub_len, w), F32)],
        compiler_params=_cp(3))(qkv6r, qkv6r, qkv6r, bias, do_c, dd_c)


def _group_ones():
    idx = np.arange(D_GRP) // HEAD_DIM
    return jnp.asarray((idx[:, None] == idx[None, :]).astype(np.float32), dtype=BF)


def _dil_alphas(l1, l4, l16):
    mx = jnp.maximum(jnp.maximum(l1, l4), l16)
    e1 = jnp.exp(l1 - mx)
    e4 = jnp.exp(l4 - mx)
    e16 = jnp.exp(l16 - mx)
    den = e1 + e4 + e16
    return e1 / den, e4 / den, e16 / den


def _dil_comb(os, ls, g_dil, ones_g):
    t = os[0].shape[0]

    def body(o1, l1, o4, l4, o16, l16, g_ref, m_ref, o_ref, on_ref):
        a1, a4, a16 = _dil_alphas(l1[...], l4[...], l16[...])
        o = a1 * o1[...] + a4 * o4[...] + a16 * o16[...]
        o_ref[...] = o
        ms = _nn2(o * o, m_ref[...]) * (1.0 / HEAD_DIM)
        on_ref[...] = ((o * lax.rsqrt(ms + EPS)) * g_ref[...]).astype(BF)

    blk = pl.BlockSpec((TM, D_GRP), lambda m: (m, 0))
    return pl.pallas_call(
        body, name="dil_comb", grid=(t // TM,),
        in_specs=[blk] * 6 + [pl.BlockSpec((1, D_GRP), lambda m: (0, 0)),
                              pl.BlockSpec((D_GRP, D_GRP), lambda m: (0, 0))],
        out_specs=[blk, blk],
        out_shape=[_sds((t, D_GRP), F32), _sds((t, D_GRP), BF)],
        compiler_params=_cp(1))(os[0], ls[0], os[1], ls[1], os[2], ls[2], g_dil, ones_g)


def _dil_comb_bwd(do, os, ls, ones_g):
    t = do.shape[0]

    def body(do_ref, o1, l1, o4, l4, o16, l16, m_ref, d1, d4, d16, e1, e4, e16):
        dov = do_ref[...]
        a1, a4, a16 = _dil_alphas(l1[...], l4[...], l16[...])
        mv = m_ref[...]
        sbar = a1 * _nn2(dov * o1[...], mv) + a4 * _nn2(dov * o4[...], mv) + a16 * _nn2(dov * o16[...], mv)
        d1[...] = a1 * dov
        d4[...] = a4 * dov
        d16[...] = a16 * dov
        e1[...] = -a1 * sbar
        e4[...] = -a4 * sbar
        e16[...] = -a16 * sbar

    blk = pl.BlockSpec((TM, D_GRP), lambda m: (m, 0))
    shp = _sds((t, D_GRP), F32)
    return pl.pallas_call(
        body, name="dil_comb_bwd", grid=(t // TM,),
        in_specs=[blk] * 7 + [pl.BlockSpec((D_GRP, D_GRP), lambda m: (0, 0))],
        out_specs=[blk] * 6, out_shape=[shp] * 6,
        compiler_params=_cp(1))(do, os[0], ls[0], os[1], ls[1], os[2], ls[2], ones_g)


def _dqkv_dil_sum(d1, d4, d16, dqkv6):
    t = d1.shape[1]

    def body(a, b, c, alias, o_ref):
        del alias
        o_ref[...] = (a[...] + b[...] + c[...]).astype(BF)

    blk = pl.BlockSpec((3, TM, D_GRP), lambda m: (0, m, 0))
    return pl.pallas_call(
        body, name="dqkv_dil_sum", grid=(t // TM,),
        in_specs=[blk, blk, blk, pl.BlockSpec(memory_space=pl.ANY)],
        out_specs=pl.BlockSpec((3, TM, D_GRP), lambda m: (1, m, 0)),
        out_shape=_sds((6, t, D_GRP), BF), input_output_aliases={3: 0},
        compiler_params=_cp(1))(d1, d4, d16, dqkv6)


def _relbias_grad(a_all, onehot):
    def body(a_ref, oh_ref, o_ref):
        acc = jnp.zeros((N_HEADS, N_BUCKETS), F32)
        for c in range(len(DIL_CONFIGS)):
            av = a_ref[c]
            hi = av.astype(BF)
            lo = (av - hi.astype(F32)).astype(BF)
            acc = acc + _nt(hi, oh_ref[c]) + _nt(lo, oh_ref[c])
        o_ref[...] = acc

    return pl.pallas_call(body, name="relbias_grad", out_shape=_sds((N_HEADS, N_BUCKETS), F32),
                          compiler_params=_cp())(a_all, onehot)


def _headnorm_bwd(dn, o, gv, mv):
    ms = _nn2(o * o, mv) * (1.0 / HEAD_DIM)
    r = lax.rsqrt(ms + EPS)
    nrm = o * r
    dg = jnp.sum(dn * nrm, axis=0, keepdims=True)
    dnn = dn * gv
    do = r * (dnn - nrm * (_nn2(dnn * nrm, mv) * (1.0 / HEAD_DIM)))
    return do, dg


def _mix_bwd_out(dx, gt, tv, w_out, o_sb, o_dil, on_sb, on_dil, g_sb, g_dil, ones_g, seq):
    t, d = dx.shape
    per = seq // TM
    nb = t // seq

    def body(dx_ref, gt_ref, t_ref, w_ref, osb, odl, onsb, ondl, gsb, gdl, m_ref,
             dosb, dodl, dgt_ref, dgsb, dgdl, dw_ref):
        m = pl.program_id(0)
        dxv = dx_ref[...]
        dt = (gt_ref[...] * dxv).astype(BF)
        _acc_rows(dgt_ref, jnp.sum(dxv * t_ref[...], axis=0, keepdims=True), m % per == 0)
        mv = m_ref[...]
        don_sb = _nt(dt, w_ref[0:D_GRP, :])
        don_dl = _nt(dt, w_ref[D_GRP:2 * D_GRP, :])
        do1, dg1 = _headnorm_bwd(don_sb, osb[...], gsb[...], mv)
        do2, dg2 = _headnorm_bwd(don_dl, odl[...], gdl[...], mv)
        dosb[...] = do1
        dodl[...] = do2
        _acc_rows(dgsb, dg1, m == 0)
        _acc_rows(dgdl, dg2, m == 0)
        p1 = _tn(onsb[...], dt)
        p2 = _tn(ondl[...], dt)

        @pl.when(m == 0)
        def _():
            dw_ref[0:D_GRP, :] = p1
            dw_ref[D_GRP:2 * D_GRP, :] = p2

        @pl.when(m != 0)
        def _():
            dw_ref[0:D_GRP, :] += p1
            dw_ref[D_GRP:2 * D_GRP, :] += p2

    row = pl.BlockSpec((TM, d), lambda m: (m, 0))
    half = pl.BlockSpec((TM, D_GRP), lambda m: (m, 0))
    ex = pl.BlockSpec((None, 1, d), lambda m: (m // per, 0, 0))
    gvec = pl.BlockSpec((1, D_GRP), lambda m: (0, 0))
    wblk = pl.BlockSpec((2 * D_GRP, d), lambda m: (0, 0))
    return pl.pallas_call(
        body, name="mix_bwd_out", grid=(t // TM,),
        in_specs=[row, ex, row, wblk, half, half, half, half, gvec, gvec,
                  pl.BlockSpec((D_GRP, D_GRP), lambda m: (0, 0))],
        out_specs=[half, half, ex, gvec, gvec, wblk],
        out_shape=[_sds((t, D_GRP), F32), _sds((t, D_GRP), F32), _sds((nb, 1, d), F32),
                   _sds((1, D_GRP), F32), _sds((1, D_GRP), F32), _sds((2 * D_GRP, d), F32)],
        compiler_params=_cp(1))(dx, gt, tv, w_out, o_sb, o_dil, on_sb, on_dil, g_sb, g_dil, ones_g)


def _dw_in(h, dqkv6):
    t, d = h.shape
    nt = 256
    per_chip = 3
    per_out = D_GRP // nt

    def body(h_ref, g_ref, o_ref):
        p = _tn(h_ref[...], g_ref[...])
        _acc_rows(o_ref, p, pl.program_id(1) == 0)

    return pl.pallas_call(
        body, name="dw_in", grid=(N_CHIPS * per_chip, t // TM),
        in_specs=[pl.BlockSpec((TM, d), lambda n, kt: (kt, 0)),
                  pl.BlockSpec((None, TM, nt), lambda n, kt: (n // per_out, kt, n % per_out))],
        out_specs=pl.BlockSpec((None, None, d, nt), lambda n, kt: (n // per_chip, 0, 0, n % per_chip)),
        out_shape=_sds((N_CHIPS, 1, d, per_chip * nt), F32),
        compiler_params=_cp(2))(h, dqkv6)


def _mix_bwd_dh(dqkv6, w_in, x, g, sc, dxo, seq):
    _, t, _ = dqkv6.shape
    d = x.shape[-1]
    nt = 256
    per_chip = 3
    per_out = D_GRP // nt
    n_tiles = N_CHIPS * per_chip
    per = seq // TM
    nb = t // seq

    def body(g6_ref, w_ref, x_ref, g_ref, sc_ref, dxo_ref, dx_ref, dsh_ref, dsc_ref, dg_ref, acc):
        m = pl.program_id(0)
        n = pl.program_id(1)

        @pl.when(n == 0)
        def _():
            acc[...] = jnp.zeros_like(acc)

        acc[...] += _nt(g6_ref[...], w_ref[...])

        @pl.when(n == n_tiles - 1)
        def _():
            dx, dsh, dsc, dg = _modnorm_bwd_tile(acc[...], x_ref[...], g_ref[...], sc_ref[...], dxo_ref[...])
            dx_ref[...] = dx
            _acc_rows(dsh_ref, dsh, m % per == 0)
            _acc_rows(dsc_ref, dsc, m % per == 0)
            _acc_rows(dg_ref, dg, m == 0)

    row = pl.BlockSpec((TM, d), lambda m, n: (m, 0))
    ex = pl.BlockSpec((None, 1, d), lambda m, n: (m // per, 0, 0))
    vec = pl.BlockSpec((1, d), lambda m, n: (0, 0))
    return pl.pallas_call(
        body, name="mix_bwd_dh", grid=(t // TM, n_tiles),
        in_specs=[pl.BlockSpec((None, TM, nt), lambda m, n: (n // per_out, m, n % per_out)),
                  pl.BlockSpec((None, None, d, nt), lambda m, n: (n // per_chip, 0, 0, n % per_chip)),
                  row, vec, ex, row],
        out_specs=[row, ex, ex, vec],
        out_shape=[_sds((t, d), F32), _sds((nb, 1, d), F32), _sds((nb, 1, d), F32), _sds((1, d), F32)],
        scratch_shapes=[pltpu.VMEM((TM, d), F32)],
        compiler_params=_cp(2))(dqkv6, w_in, x, g, sc, dxo)


def _final_loss(x, g, target):
    t, d = x.shape
    steps = t // TM

    def body(x_ref, g_ref, t_ref, dx_ref, dg_ref, loss_ref, lacc):
        m = pl.program_id(0)
        xv = x_ref[...]
        gv = g_ref[...]
        r = lax.rsqrt(jnp.mean(xv * xv, axis=-1, keepdims=True) + EPS)
        n = xv * r
        err = n * gv - t_ref[...]
        dy = err * (1.0 / d)
        _acc_rows(dg_ref, jnp.sum(dy * n, axis=0, keepdims=True), m == 0)
        dn = dy * gv
        dx_ref[...] = r * (dn - n * jnp.mean(dn * n, axis=-1, keepdims=True))
        _acc_rows(lacc, jnp.sum(err * err, axis=0, keepdims=True), m == 0)

        @pl.when(m == steps - 1)
        def _():
            tot = jnp.sum(lacc[...], axis=-1, keepdims=True) * (0.5 / d)
            loss_ref[...] = jnp.broadcast_to(tot, (1, 128))

    row = pl.BlockSpec((TM, d), lambda m: (m, 0))
    vec = pl.BlockSpec((1, d), lambda m: (0, 0))
    return pl.pallas_call(
        body, name="final_loss", grid=(steps,),
        in_specs=[row, vec, row],
        out_specs=[row, vec, pl.BlockSpec((1, 128), lambda m: (0, 0))],
        out_shape=[_sds((t, d), F32), _sds((1, d), F32), _sds((1, 128), F32)],
        scratch_shapes=[pltpu.VMEM((1, d), F32)],
        compiler_params=_cp(1))(x, g, target)


def _row_tile(rows, cols):
    best = rows
    for tr in range(8, rows + 1, 8):
        if rows % tr == 0 and tr * cols * 4 <= (1 << 20):
            best = tr
    if best * cols * 4 > (1 << 21):
        best = 8
    return best


def _adamw(w, g_arr, g_sel, m, v):
    rows, cols = w.shape
    tr = _row_tile(rows, cols)
    b1c = 1.0 - ADAM_B1 ** ADAM_STEP
    b2c = 1.0 - ADAM_B2 ** ADAM_STEP

    def body(w_ref, g_ref, m_ref, v_ref, go_ref, d_ref, mo_ref, vo_ref):
        gv = g_ref[...]
        mn = ADAM_B1 * m_ref[...] + (1.0 - ADAM_B1) * gv
        vn = ADAM_B2 * v_ref[...] + (1.0 - ADAM_B2) * (gv * gv)
        go_ref[...] = gv
        mo_ref[...] = mn
        vo_ref[...] = vn
        d_ref[...] = -ADAM_LR * ((mn / b1c) / (jnp.sqrt(vn / b2c) + ADAM_EPS) + ADAM_WD * w_ref[...])

    blk = pl.BlockSpec((tr, cols), lambda i: (i, 0))
    shp = _sds((rows, cols), F32)
    return pl.pallas_call(
        body, name="adamw", grid=(rows // tr,),
        in_specs=[blk, pl.BlockSpec((None, tr, cols), lambda i: (g_sel, i, 0)), blk, blk],
        out_specs=[blk] * 4, out_shape=[shp] * 4,
        compiler_params=_cp(1))(w, g_arr, m, v)


def _flip(v, bit):
    return 1 - v if bit else v


def _my_place():
    x, y, c = lax.axis_index("x"), lax.axis_index("y"), lax.axis_index("c")
    return x, y, c


def _ada_fwd(c_pad, w_ada, b_shard):
    d = c_pad.shape[-1]
    cols = w_ada.shape[-1]
    chunk = 384

    def body(c_ref, w_ref, b_ref, call_ref, mod_ref, part, s1, r1, s2, r2):
        x, y, c = _my_place()
        dev = 4 * x + 2 * y + c
        chip = 2 * x + y
        call_ref[dev] = c_ref[...]

        def c_copy(k):
            px, py, pc = _flip(x, (k >> 2) & 1), _flip(y, (k >> 1) & 1), _flip(c, k & 1)
            return px, py, pc

        sends = []
        for k in range(1, N_DEV):
            px, py, pc = c_copy(k)
            cp = pltpu.make_async_remote_copy(src_ref=c_ref, dst_ref=call_ref.at[dev], send_sem=s1.at[k - 1],
                                              recv_sem=r1.at[k - 1], device_id=(px, py, pc), device_id_type=MESH)
            cp.start()
            sends.append(cp)
        for k in range(1, N_DEV):
            px, py, pc = c_copy(k)
            pltpu.make_async_remote_copy(src_ref=c_ref, dst_ref=call_ref.at[4 * px + 2 * py + pc],
                                         send_sem=s1.at[k - 1], recv_sem=r1.at[k - 1],
                                         device_id=(px, py, pc), device_id_type=MESH).wait_recv()
        for cp in sends:
            cp.wait_send()

        cs = call_ref[...].reshape(N_DEV * 8, d)
        sc = (cs * jax.nn.sigmoid(cs)).astype(BF)
        for n0 in range(0, cols, chunk):
            blk = _nn(sc, w_ref[:, n0:n0 + chunk].astype(BF)) + b_ref[:, n0:n0 + chunk]
            part[:, :, n0:n0 + chunk] = blk.reshape(N_DEV, 8, chunk)

        mod_ref[chip] = part[dev]
        sends = []
        for kk in range(1, N_CHIPS):
            px, py = _flip(x, (kk >> 1) & 1), _flip(y, kk & 1)
            cp = pltpu.make_async_remote_copy(src_ref=part.at[4 * px + 2 * py + c], dst_ref=mod_ref.at[chip],
                                              send_sem=s2.at[kk - 1], recv_sem=r2.at[kk - 1],
                                              device_id=(px, py, c), device_id_type=MESH)
            cp.start()
            sends.append(cp)
        for kk in range(1, N_CHIPS):
            px, py = _flip(x, (kk >> 1) & 1), _flip(y, kk & 1)
            pltpu.make_async_remote_copy(src_ref=part.at[dev], dst_ref=mod_ref.at[2 * px + py],
                                         send_sem=s2.at[kk - 1], recv_sem=r2.at[kk - 1],
                                         device_id=(px, py, c), device_id_type=MESH).wait_recv()
        for cp in sends:
            cp.wait_send()

    return pl.pallas_call(
        body, name="ada_fwd",
        out_shape=[_sds((N_DEV, 8, d), F32), _sds((N_CHIPS, 8, cols), F32)],
        scratch_shapes=[pltpu.VMEM((N_DEV, 8, cols), F32),
                        pltpu.SemaphoreType.DMA((N_DEV - 1,)), pltpu.SemaphoreType.DMA((N_DEV - 1,)),
                        pltpu.SemaphoreType.DMA((N_CHIPS - 1,)), pltpu.SemaphoreType.DMA((N_CHIPS - 1,))],
        compiler_params=_cp())(c_pad, w_ada, b_shard)


def _ag_weights(bufs):
    n = len(bufs)
    any_spec = pl.BlockSpec(memory_space=pl.ANY)

    def body(*refs):
        outs = refs[n:2 * n]
        s_ici, r_ici, s_d2d, r_d2d = refs[2 * n:]
        x, y, c = _my_place()
        chip = 2 * x + y
        others = [(_flip(x, (kk >> 1) & 1), _flip(y, kk & 1)) for kk in range(1, N_CHIPS)]

        def half(b, which):
            hr = bufs[b].shape[2] // 2
            return pl.ds(pl.multiple_of(which * hr, 16), hr)

        sends = []
        for b in range(n):
            for i, (px, py) in enumerate(others):
                mine = outs[b].at[chip, :, half(b, c), :]
                cp = pltpu.make_async_remote_copy(
                    src_ref=mine, dst_ref=mine,
                    send_sem=s_ici.at[3 * b + i], recv_sem=r_ici.at[3 * b + i],
                    device_id=(px, py, c), device_id_type=MESH)
                cp.start()
                sends.append(cp)
        for b in range(n):
            for i, (px, py) in enumerate(others):
                landed = outs[b].at[2 * px + py, :, half(b, c), :]
                pltpu.make_async_remote_copy(
                    src_ref=landed, dst_ref=landed, send_sem=s_ici.at[3 * b + i], recv_sem=r_ici.at[3 * b + i],
                    device_id=(px, py, c), device_id_type=MESH).wait_recv()
                cp = pltpu.make_async_remote_copy(
                    src_ref=landed, dst_ref=landed, send_sem=s_d2d.at[3 * b + i], recv_sem=r_d2d.at[3 * b + i],
                    device_id=(x, y, 1 - c), device_id_type=MESH)
                cp.start()
                sends.append(cp)
        for b in range(n):
            for i, (px, py) in enumerate(others):
                got = outs[b].at[2 * px + py, :, half(b, 1 - c), :]
                pltpu.make_async_remote_copy(
                    src_ref=got, dst_ref=got, send_sem=s_d2d.at[3 * b + i], recv_sem=r_d2d.at[3 * b + i],
                    device_id=(x, y, 1 - c), device_id_type=MESH).wait_recv()
        for cp in sends:
            cp.wait_send()

    return pl.pallas_call(
        body, name="ag_weights",
        in_specs=[any_spec] * n, out_specs=[any_spec] * n,
        out_shape=[_sds(s.shape, s.dtype) for s in bufs],
        input_output_aliases={i: i for i in range(n)},
        scratch_shapes=[pltpu.SemaphoreType.DMA((3 * n,)), pltpu.SemaphoreType.DMA((3 * n,)),
                        pltpu.SemaphoreType.DMA((3 * n,)), pltpu.SemaphoreType.DMA((3 * n,))],
        compiler_params=_cp())(*bufs)


def _rs_d2d(grads):
    n = len(grads)
    any_spec = pl.BlockSpec(memory_space=pl.ANY)

    def body(*refs):
        srcs, lands = refs[:n], refs[n:2 * n]
        ssem, rsem = refs[2 * n:]
        x, y, c = _my_place()
        sends = []
        for b in range(n):
            hr = grads[b].shape[2] // 2
            theirs = pl.ds(pl.multiple_of((1 - c) * hr, 8), hr)
            cp = pltpu.make_async_remote_copy(
                src_ref=srcs[b].at[:, :, theirs, :], dst_ref=lands[b], send_sem=ssem.at[b], recv_sem=rsem.at[b],
                device_id=(x, y, 1 - c), device_id_type=MESH)
            cp.start()
            sends.append(cp)
        for cp in sends:
            cp.wait()

    return pl.pallas_call(
        body, name="rs_d2d",
        in_specs=[any_spec] * n, out_specs=[any_spec] * n,
        out_shape=[_sds(g.shape[:2] + (g.shape[2] // 2, g.shape[3]), F32) for g in grads],
        scratch_shapes=[pltpu.SemaphoreType.DMA((n,)), pltpu.SemaphoreType.DMA((n,))],
        compiler_params=_cp())(*grads)


def _add_halves(core, g, land):
    nchip, ng, rows, cols = g.shape
    hr = rows // 2
    tr = _row_tile(hr, cols)
    steps = hr // tr

    def body(core_ref, g_ref, l_ref, o_ref):
        del core_ref
        o_ref[...] = (g_ref[...] + l_ref[...]).astype(BF)

    return pl.pallas_call(
        body, name="add_halves",
        grid_spec=pltpu.PrefetchScalarGridSpec(
            num_scalar_prefetch=1, grid=(nchip, ng, steps),
            in_specs=[pl.BlockSpec((None, None, tr, cols), lambda j, a, i, cr: (j, a, cr[0] * steps + i, 0)),
                      pl.BlockSpec((None, None, tr, cols), lambda j, a, i, cr: (j, a, i, 0))],
            out_specs=pl.BlockSpec((None, None, tr, cols), lambda j, a, i, cr: (j, a, i, 0))),
        out_shape=_sds((nchip, ng, hr, cols), BF),
        compiler_params=_cp(3))(core, g, land)


def _rs_ici(parts):
    n = len(parts)
    any_spec = pl.BlockSpec(memory_space=pl.ANY)

    def body(*refs):
        srcs, lands = refs[:n], refs[n:2 * n]
        ssem, rsem = refs[2 * n:]
        x, y, c = _my_place()
        chip = 2 * x + y
        others = [(_flip(x, (kk >> 1) & 1), _flip(y, kk & 1)) for kk in range(1, N_CHIPS)]
        sends = []
        for b in range(n):
            for i, (px, py) in enumerate(others):
                cp = pltpu.make_async_remote_copy(
                    src_ref=srcs[b].at[2 * px + py], dst_ref=lands[b].at[chip],
                    send_sem=ssem.at[3 * b + i], recv_sem=rsem.at[3 * b + i],
                    device_id=(px, py, c), device_id_type=MESH)
                cp.start()
                sends.append(cp)
        for b in range(n):
            for i, (px, py) in enumerate(others):
                slot = lands[b].at[2 * px + py]
                pltpu.make_async_remote_copy(
                    src_ref=slot, dst_ref=slot, send_sem=ssem.at[3 * b + i], recv_sem=rsem.at[3 * b + i],
                    device_id=(px, py, c), device_id_type=MESH).wait_recv()
        for cp in sends:
            cp.wait_send()

    return pl.pallas_call(
        body, name="rs_ici",
        in_specs=[any_spec] * n, out_specs=[any_spec] * n,
        out_shape=[_sds(p.shape, p.dtype) for p in parts],
        scratch_shapes=[pltpu.SemaphoreType.DMA((3 * n,)), pltpu.SemaphoreType.DMA((3 * n,))],
        compiler_params=_cp())(*parts)


def _sum_chips(place, part, land):
    nchip, ng, hr, cols = land.shape
    tr = _row_tile(hr, cols)
    steps = hr // tr

    def body(place_ref, p_ref, l1, l2, l3, o_ref):
        del place_ref
        o_ref[...] = ((p_ref[...].astype(F32) + l1[...].astype(F32)) + l2[...].astype(F32)) + l3[...].astype(F32)

    def slot(k):
        return pl.BlockSpec((None, None, tr, cols), lambda a, i, pr: (jnp.bitwise_xor(pr[1], k), a, i, 0))

    return pl.pallas_call(
        body, name="sum_chips",
        grid_spec=pltpu.PrefetchScalarGridSpec(
            num_scalar_prefetch=1, grid=(ng, steps),
            in_specs=[slot(0), slot(1), slot(2), slot(3)],
            out_specs=pl.BlockSpec((None, tr, cols), lambda a, i, pr: (a, pr[0] * steps + i, 0))),
        out_shape=_sds((ng, 2 * hr, cols), F32),
        compiler_params=_cp(2))(place, part, land, land, land)


def _rs_final(bufs):
    n = len(bufs)
    any_spec = pl.BlockSpec(memory_space=pl.ANY)

    def body(*refs):
        outs = refs[n:2 * n]
        ssem, rsem = refs[2 * n:]
        x, y, c = _my_place()
        ops = []
        for b in range(n):
            hr = bufs[b].shape[1] // 2
            mine = outs[b].at[:, pl.ds(pl.multiple_of(c * hr, 8), hr), :]
            cp = pltpu.make_async_remote_copy(
                src_ref=mine, dst_ref=mine, send_sem=ssem.at[b], recv_sem=rsem.at[b],
                device_id=(x, y, 1 - c), device_id_type=MESH)
            cp.start()
            ops.append(cp)
        for b, cp in enumerate(ops):
            cp.wait_send()
            hr = bufs[b].shape[1] // 2
            theirs = outs[b].at[:, pl.ds(pl.multiple_of((1 - c) * hr, 8), hr), :]
            pltpu.make_async_remote_copy(
                src_ref=theirs, dst_ref=theirs, send_sem=ssem.at[b], recv_sem=rsem.at[b],
                device_id=(x, y, 1 - c), device_id_type=MESH).wait_recv()

    return pl.pallas_call(
        body, name="rs_final",
        in_specs=[any_spec] * n, out_specs=[any_spec] * n,
        out_shape=[_sds(h.shape, F32) for h in bufs],
        input_output_aliases={i: i for i in range(n)},
        scratch_shapes=[pltpu.SemaphoreType.DMA((n,)), pltpu.SemaphoreType.DMA((n,))],
        compiler_params=_cp())(*bufs)


def _small_sync(smalls, dmod_blk, c_all):
    d = c_all.shape[-1]
    cols = dmod_blk.shape[-1]
    chunk = 384

    def body(sm_ref, dm_ref, c_ref, sum_ref, gw_ref, sm_all, dm_all, ssem, rsem):
        x, y, c = _my_place()
        dev = 4 * x + 2 * y + c
        chip = 2 * x + y
        sm_all[dev] = sm_ref[...]
        dm_all[dev] = dm_ref[chip]
        sends = []
        for k in range(1, N_DEV):
            px, py, pc = _flip(x, (k >> 2) & 1), _flip(y, (k >> 1) & 1), _flip(c, k & 1)
            a = pltpu.make_async_remote_copy(src_ref=sm_ref, dst_ref=sm_all.at[dev], send_sem=ssem.at[2 * (k - 1)],
                                             recv_sem=rsem.at[2 * (k - 1)], device_id=(px, py, pc),
                                             device_id_type=MESH)
            b = pltpu.make_async_remote_copy(src_ref=dm_ref.at[2 * px + py], dst_ref=dm_all.at[dev],
                                             send_sem=ssem.at[2 * (k - 1) + 1], recv_sem=rsem.at[2 * (k - 1) + 1],
                                             device_id=(px, py, pc), device_id_type=MESH)
            a.start()
            b.start()
            sends += [a, b]
        for k in range(1, N_DEV):
            px, py, pc = _flip(x, (k >> 2) & 1), _flip(y, (k >> 1) & 1), _flip(c, k & 1)
            pdev = 4 * px + 2 * py + pc
            pltpu.make_async_remote_copy(src_ref=sm_ref, dst_ref=sm_all.at[pdev], send_sem=ssem.at[2 * (k - 1)],
                                         recv_sem=rsem.at[2 * (k - 1)], device_id=(px, py, pc),
                                         device_id_type=MESH).wait_recv()
            pltpu.make_async_remote_copy(src_ref=dm_ref.at[chip], dst_ref=dm_all.at[pdev],
                                         send_sem=ssem.at[2 * (k - 1) + 1], recv_sem=rsem.at[2 * (k - 1) + 1],
                                         device_id=(px, py, pc), device_id_type=MESH).wait_recv()
        for cp in sends:
            cp.wait_send()

        tot = sm_all[0]
        for q in range(1, N_DEV):
            tot = tot + sm_all[q]
        sum_ref[...] = tot

        cs = c_ref[...].reshape(N_DEV * 8, d)
        sc = (cs * jax.nn.sigmoid(cs)).astype(BF)
        for n0 in range(0, cols, chunk):
            dmv = dm_all[:, :, n0:n0 + chunk].reshape(N_DEV * 8, chunk).astype(BF)
            gw_ref[:, n0:n0 + chunk] = _tn(sc, dmv)

    return pl.pallas_call(
        body, name="small_sync",
        out_shape=[_sds(smalls.shape, F32), _sds((d, cols), F32)],
        scratch_shapes=[pltpu.VMEM((N_DEV,) + smalls.shape, F32), pltpu.VMEM((N_DEV, 8, cols), F32),
                        pltpu.SemaphoreType.DMA((2 * (N_DEV - 1),)), pltpu.SemaphoreType.DMA((2 * (N_DEV - 1),))],
        compiler_params=_cp())(smalls, dmod_blk, c_all)


def _bucket_onehot():
    maps = np.stack([_bucket_map(dil).reshape(-1) for _, dil in DIL_CONFIGS])
    return (jnp.asarray(maps)[:, None, :] == jnp.arange(N_BUCKETS, dtype=jnp.int32)[None, :, None]).astype(BF)


def _dil_bias(rel_t, onehot):
    def body(r_ref, oh_ref, o_ref):
        rv = r_ref[...]
        hi = rv.astype(BF)
        lo = (rv - hi.astype(F32)).astype(BF)
        for c in range(len(DIL_CONFIGS)):
            o_ref[c] = _nn(hi, oh_ref[c]) + _nn(lo, oh_ref[c])

    return pl.pallas_call(body, name="dil_bias",
                          out_shape=_sds((len(DIL_CONFIGS), N_HEADS, BLOCK * 2 * BLOCK), F32),
                          compiler_params=_cp())(rel_t, onehot)


def _rowsum8(a):
    def body(a_ref, o_ref):
        o_ref[...] = jnp.sum(a_ref[...], axis=0, keepdims=True)

    return pl.pallas_call(body, name="rowsum8", out_shape=_sds((1, a.shape[1]), F32), compiler_params=_cp())(a)


def _local_step(x, mod, target, wgu, wd, w_in, w_out, gains, rel_bias):
    nb, seq, d = x.shape
    t = nb * seq
    x0 = x.reshape(t, d)
    tgt = target.reshape(t, d)
    md = [mod[:, i:i + 1, :] for i in range(N_MOD)]
    sh1, sc1, gt1, sh2, sc2, gt2, sh3, sc3, gt3 = md
    g1, g2, g3 = gains["g_ffn1"], gains["g_mix"], gains["g_ffn2"]
    w_out2 = w_out.reshape(2 * D_GRP, d)
    ones_g = _group_ones()

    h1 = _modnorm(x0, g1, sc1, sh1, seq)
    a1, u1, s1 = _ffn_up(h1, wgu, 0)
    f1, x1 = _ffn_down(s1, wd, 0, x0, gt1, seq, 0.5)

    h2 = _modnorm(x1, g2, sc2, sh2, seq)
    qkv6 = _qkv_proj(h2, w_in)
    qkv6b = qkv6.reshape(6, nb, seq, D_GRP)
    o_sb, on_sb = _sb_fwd(qkv6b, gains["g_sb_out"], nb, seq)
    onehot = _bucket_onehot()
    bias = _dil_bias(rel_bias.T, onehot).reshape(len(DIL_CONFIGS), N_HEADS, BLOCK, 2 * BLOCK)
    o_cs, l_cs, qkv_rs = [], [], []
    for ci, (_, dil) in enumerate(DIL_CONFIGS):
        sub = seq // dil
        qr = qkv6.reshape(6, nb, sub, dil * D_GRP)
        o_c, l_c = _dil_fwd(qr, bias[ci], nb, sub, dil)
        qkv_rs.append(qr)
        o_cs.append(o_c.reshape(t, D_GRP))
        l_cs.append(l_c.reshape(t, D_GRP))
    o_dil, on_dil = _dil_comb(o_cs, l_cs, gains["g_dil_out"], ones_g)
    tmix, x2 = _mix_out(on_sb.reshape(t, D_GRP), on_dil, w_out2, x1, gt2, seq)

    h3 = _modnorm(x2, g3, sc3, sh3, seq)
    a3, u3, s3 = _ffn_up(h3, wgu, 1)
    f3, x3 = _ffn_down(s3, wd, 1, x2, gt3, seq, 0.5)

    dx3, dg_final, loss = _final_loss(x3, gains["g_final"], tgt)

    da3, du3, df3, dgt3 = _ffn_bwd_ds(dx3, gt3, f3, wd, 1, a3, u3, seq, 0.5)
    dgu2, dwd2 = _ffn_bwd_w(h3, da3, du3, s3, df3)
    dx2, dsh3, dsc3, dg3 = _ffn_bwd_dh(da3, du3, wgu, 1, x2, g3, sc3, dx3, seq)

    do_sb, do_dil, dgt2, dg_sb, dg_dil, dw_out = _mix_bwd_out(
        dx2, gt2, tmix, w_out2, o_sb.reshape(t, D_GRP), o_dil, on_sb.reshape(t, D_GRP), on_dil,
        gains["g_sb_out"], gains["g_dil_out"], ones_g, seq)
    dqkv6 = _sb_bwd(qkv6b, o_sb, do_sb.reshape(nb, seq, D_GRP), nb, seq)
    dcs = _dil_comb_bwd(do_dil, o_cs, l_cs, ones_g)
    dsum, a_tiles = [], []
    for ci, (_, dil) in enumerate(DIL_CONFIGS):
        sub = seq // dil
        do_c = dcs[ci].reshape(nb, sub, dil * D_GRP)
        dd_c = dcs[3 + ci].reshape(nb, sub, dil * D_GRP)
        dq_c, a_c = _dil_bwd(qkv_rs[ci], bias[ci], do_c, dd_c, nb, sub, dil)
        dsum.append(dq_c.reshape(3, t, D_GRP))
        a_tiles.append(a_c.reshape(N_HEADS, BLOCK * 2 * BLOCK))
    dqkv6 = _dqkv_dil_sum(dsum[0], dsum[1], dsum[2], dqkv6.reshape(6, t, D_GRP))
    drel = _relbias_grad(jnp.stack(a_tiles), onehot)
    dwin = _dw_in(h2, dqkv6)
    dx1, dsh2, dsc2, dg2 = _mix_bwd_dh(dqkv6, w_in, x1, g2, sc2, dx2, seq)

    da1, du1, df1, dgt1 = _ffn_bwd_ds(dx1, gt1, f1, wd, 0, a1, u1, seq, 0.5)
    dgu1, dwd1 = _ffn_bwd_w(h1, da1, du1, s1, df1)
    dx0, dsh1, dsc1, dg1 = _ffn_bwd_dh(da1, du1, wgu, 0, x0, g1, sc1, dx1, seq)

    dmod = jnp.concatenate([dsh1, dsc1, dgt1, dsh2, dsc2, dgt2, dsh3, dsc3, dgt3], axis=1)
    return dict(grad_x=dx0.reshape(nb, seq, d), loss=loss[0, 0], dmod=dmod.reshape(nb, N_MOD * d),
                dgu1=dgu1, dgu2=dgu2, dwd1=dwd1, dwd2=dwd2, dwin=dwin,
                dwout=dw_out.reshape(N_CHIPS, 1, 2 * D_GRP // N_CHIPS, d),
                dg_ffn1=dg1, dg_mix=dg2, dg_ffn2=dg3, dg_final=dg_final, dg_sb=dg_sb, dg_dil=dg_dil,
                drel=drel.T)


_SMALL_ORDER = (("b_ada", N_MOD * 1024), ("g_ffn1", 1024), ("g_mix", 1024), ("g_ffn2", 1024), ("g_final", 1024),
                ("g_sb_out", D_GRP), ("g_dil_out", D_GRP), ("rel_bias", N_BUCKETS * N_HEADS))


def _pack_small(parts, extra=None):
    flat = [parts[name].reshape(-1).astype(F32) for name, _ in _SMALL_ORDER]
    used = sum(sz for _, sz in _SMALL_ORDER)
    pad = SMALL_ROWS * 128 - used
    tail = jnp.zeros((pad,), F32)
    if extra is not None:
        tail = tail.at[0].set(extra)
    return jnp.concatenate(flat + [tail]).reshape(SMALL_ROWS, 128)


def _unpack_small(packed, shapes):
    flat = packed.reshape(-1)
    out, off = {}, 0
    for name, sz in _SMALL_ORDER:
        out[name] = flat[off:off + sz].reshape(shapes[name])
        off += sz
    return out, flat[off]


def kernel(x, c, w_ada, b_ada, g_ffn1, w1_gate, w1_up, w1_down, g_mix, w_in, g_sb_out, g_dil_out, w_out, rel_bias, g_ffn2, w2_gate, w2_up, w2_down, g_final, loss_target, m_w_ada, m_b_ada, m_g_ffn1, m_w1_gate, m_w1_up, m_w1_down, m_g_mix, m_w_in, m_g_sb_out, m_g_dil_out, m_w_out, m_rel_bias, m_g_ffn2, m_w2_gate, m_w2_up, m_w2_down, m_g_final, v_w_ada, v_b_ada, v_g_ffn1, v_w1_gate, v_w1_up, v_w1_down, v_g_mix, v_w_in, v_g_sb_out, v_g_dil_out, v_w_out, v_rel_bias, v_g_ffn2, v_w2_gate, v_w2_up, v_w2_down, v_g_final):
    nb, seq, d = x.shape
    xi, yi, ci = lax.axis_index("x"), lax.axis_index("y"), lax.axis_index("c")
    chip = 2 * xi + yi
    ada_cols = w_ada.shape[-1]

    c_pad = jnp.zeros((8, d), F32).at[:nb].set(c)
    b_shard = lax.dynamic_slice(b_ada, (0, chip * ada_cols), (1, ada_cols))
    c_all, mod_blk = _ada_fwd(c_pad, w_ada[0], b_shard)
    mod = jnp.transpose(mod_blk[:, :nb, :], (1, 0, 2)).reshape(nb, N_MOD, d)

    shards = [jnp.stack([w1_gate[0], w1_up[0], w2_gate[0], w2_up[0]]).astype(BF),
              jnp.stack([w1_down[0], w2_down[0]]).astype(BF),
              w_in.astype(BF), w_out.astype(BF)]
    bufs = [lax.dynamic_update_slice(lax.empty((N_CHIPS,) + s.shape, BF), s[None], (chip, 0, 0, 0)) for s in shards]
    wgu, wd, w_in_g, w_out_g = _ag_weights(bufs)

    gains = dict(g_ffn1=g_ffn1, g_mix=g_mix, g_ffn2=g_ffn2, g_final=g_final.reshape(1, d),
                 g_sb_out=g_sb_out.reshape(1, D_GRP), g_dil_out=g_dil_out.reshape(1, D_GRP))
    r = _local_step(x, mod, loss_target, wgu, wd, w_in_g, w_out_g, gains, rel_bias)

    grads = [r["dgu1"], r["dgu2"], r["dwd1"], r["dwd2"], r["dwin"], r["dwout"]]
    lands = _rs_d2d(grads)
    core = jnp.reshape(ci, (1,)).astype(jnp.int32)
    parts = [_add_halves(core, g, l) for g, l in zip(grads, lands)]
    lands2 = _rs_ici(parts)
    place = jnp.stack([ci, chip]).astype(jnp.int32)
    halves = [_sum_chips(place, p, l) for p, l in zip(parts, lands2)]
    gu1, gu2, gd1, gd2, gwin, gwout = _rs_final(halves)

    dmod = r["dmod"]
    dmod_pad = jnp.zeros((8, N_MOD * d), F32).at[:nb].set(dmod)
    dmod_blk = jnp.transpose(dmod_pad.reshape(8, N_CHIPS, ada_cols), (1, 0, 2))
    small_parts = dict(b_ada=_rowsum8(dmod_pad), g_ffn1=r["dg_ffn1"], g_mix=r["dg_mix"], g_ffn2=r["dg_ffn2"],
                       g_final=r["dg_final"], g_sb_out=r["dg_sb"], g_dil_out=r["dg_dil"], rel_bias=r["drel"])
    small_sum, g_wada = _small_sync(_pack_small(small_parts, r["loss"]), dmod_blk, c_all)

    small_w = dict(b_ada=b_ada, g_ffn1=g_ffn1, g_mix=g_mix, g_ffn2=g_ffn2, g_final=g_final,
                   g_sb_out=g_sb_out, g_dil_out=g_dil_out, rel_bias=rel_bias)
    small_m = dict(b_ada=m_b_ada, g_ffn1=m_g_ffn1, g_mix=m_g_mix, g_ffn2=m_g_ffn2, g_final=m_g_final,
                   g_sb_out=m_g_sb_out, g_dil_out=m_g_dil_out, rel_bias=m_rel_bias)
    small_v = dict(b_ada=v_b_ada, g_ffn1=v_g_ffn1, g_mix=v_g_mix, g_ffn2=v_g_ffn2, g_final=v_g_final,
                   g_sb_out=v_g_sb_out, g_dil_out=v_g_dil_out, rel_bias=v_rel_bias)
    shapes = {k: v.shape for k, v in small_w.items()}
    sg, sd, sm, sv = _adamw(_pack_small(small_w), small_sum.reshape(1, SMALL_ROWS, 128), 0,
                            _pack_small(small_m), _pack_small(small_v))
    sg, loss = _unpack_small(sg, shapes)
    sd, _ = _unpack_small(sd, shapes)
    sm, _ = _unpack_small(sm, shapes)
    sv, _ = _unpack_small(sv, shapes)

    big = {}

    def upd(name, w, g_arr, sel, m, v):
        shape = w.shape
        res = _adamw(w.reshape(shape[-2:]), g_arr, sel, m.reshape(shape[-2:]), v.reshape(shape[-2:]))
        big[name] = [a.reshape(shape) for a in res]

    upd("w_ada", w_ada, g_wada.reshape(1, d, ada_cols), 0, m_w_ada, v_w_ada)
    upd("w1_gate", w1_gate, gu1, 0, m_w1_gate, v_w1_gate)
    upd("w1_up", w1_up, gu1, 1, m_w1_up, v_w1_up)
    upd("w1_down", w1_down, gd1, 0, m_w1_down, v_w1_down)
    upd("w_in", w_in, gwin, 0, m_w_in, v_w_in)
    upd("w_out", w_out, gwout, 0, m_w_out, v_w_out)
    upd("w2_gate", w2_gate, gu2, 0, m_w2_gate, v_w2_gate)
    upd("w2_up", w2_up, gu2, 1, m_w2_up, v_w2_up)
    upd("w2_down", w2_down, gd2, 0, m_w2_down, v_w2_down)

    names = ["w_ada", "b_ada", "g_ffn1", "w1_gate", "w1_up", "w1_down", "g_mix", "w_in", "g_sb_out", "g_dil_out",
             "w_out", "rel_bias", "g_ffn2", "w2_gate", "w2_up", "w2_down", "g_final"]
    outs = [loss, r["grad_x"]]
    for k, small in enumerate((sg, sd, sm, sv)):
        for name in names:
            outs.append(big[name][k] if name in big else small[name])
    return tuple(outs)
```

```python
import functools
import math

import numpy as np
import jax
import jax.numpy as jnp
from jax import lax
from jax.experimental import pallas as pl
from jax.experimental.pallas import tpu as pltpu

F32 = jnp.float32
BF = jnp.bfloat16
MESH = pl.DeviceIdType.MESH

HEAD_DIM = 64
N_HEADS = 8
D_GRP = N_HEADS * HEAD_DIM
DIL_CONFIGS = ((128, 1), (512, 4), (2048, 16))
N_STEPS = 128
BLOCK = 128
N_BUCKETS = 32
MAX_DISTANCE = 2048
N_MOD = 9
EPS = 1e-6
NEG_INF = -1e30
SCALE = HEAD_DIM ** -0.5

ADAM_LR = 0.001
ADAM_B1 = 0.9
ADAM_B2 = 0.999
ADAM_EPS = 1e-08
ADAM_WD = 0.01
ADAM_STEP = 10

N_CHIPS = 4
N_DEV = 8
VMEM_LIMIT = 56 * 1024 * 1024
TM = 512
TQ = 256
KB = 256
SMALL_ROWS = 120


def _cp(n_axes=0, **kw):
    sem = ("arbitrary",) * n_axes if n_axes else None
    return pltpu.CompilerParams(dimension_semantics=sem, vmem_limit_bytes=VMEM_LIMIT, **kw)


def _nn(a, b):
    return jnp.dot(a, b, preferred_element_type=F32)


def _nt(a, b):
    return lax.dot_general(a, b, (((1,), (1,)), ((), ())), preferred_element_type=F32)


def _tn(a, b):
    return lax.dot_general(a, b, (((0,), (0,)), ((), ())), preferred_element_type=F32)


def _nn2(x, m):
    hi = x.astype(BF)
    lo = (x - hi.astype(F32)).astype(BF)
    return _nn(hi, m) + _nn(lo, m)


def _softplus(z):
    return jnp.maximum(z, 0.0) + jnp.log1p(jnp.exp(-jnp.abs(z)))


def _sds(shape, dtype):
    return jax.ShapeDtypeStruct(shape, dtype)


def _modnorm(x, g, sc, sh, seq):
    t, d = x.shape
    per = seq // TM

    def body(x_ref, g_ref, sc_ref, sh_ref, h_ref):
        xv = x_ref[...]
        r = lax.rsqrt(jnp.mean(xv * xv, axis=-1, keepdims=True) + EPS)
        h_ref[...] = (((xv * r) * g_ref[...]) * (1.0 + sc_ref[...]) + sh_ref[...]).astype(BF)

    return pl.pallas_call(
        body, name="modnorm", grid=(t // TM,),
        in_specs=[pl.BlockSpec((TM, d), lambda m: (m, 0)),
                  pl.BlockSpec((1, d), lambda m: (0, 0)),
                  pl.BlockSpec((None, 1, d), lambda m: (m // per, 0, 0)),
                  pl.BlockSpec((None, 1, d), lambda m: (m // per, 0, 0))],
        out_specs=pl.BlockSpec((TM, d), lambda m: (m, 0)),
        out_shape=_sds((t, d), BF), compiler_params=_cp(1))(x, g, sc, sh)


def _modnorm_bwd_tile(dh, xv, gv, scv, dxo):
    r = lax.rsqrt(jnp.mean(xv * xv, axis=-1, keepdims=True) + EPS)
    n = xv * r
    ng = n * gv
    dsh = jnp.sum(dh, axis=0, keepdims=True)
    dsc = jnp.sum(dh * ng, axis=0, keepdims=True)
    dy = dh * (1.0 + scv)
    dg = jnp.sum(dy * n, axis=0, keepdims=True)
    dn = dy * gv
    dx = dxo + r * (dn - n * jnp.mean(dn * n, axis=-1, keepdims=True))
    return dx, dsh, dsc, dg


def _acc_rows(ref, val, first):
    @pl.when(first)
    def _():
        ref[...] = val

    @pl.when(jnp.logical_not(first))
    def _():
        ref[...] += val


def _ffn_up(h, wgu, carry=None):
    t, d = h.shape
    fs = wgu.shape[-1]

    def body(h_ref, w_ref, a_ref, u_ref, s_ref):
        hv = h_ref[...]
        a = _nn(hv, w_ref[0])
        u = _nn(hv, w_ref[1])
        a_ref[...] = a
        u_ref[...] = u
        s_ref[...] = ((a * jax.nn.sigmoid(a)) * u).astype(BF)

    blk = pl.BlockSpec((None, TM, fs), lambda j, m: (j, m, 0))
    return _call(
        body, "ffn_up", (N_CHIPS, t // TM),
        [pl.BlockSpec((TM, d), lambda j, m: (m, 0)),
         pl.BlockSpec((None, 2, d, fs), lambda j, m: (j, 0, 0, 0))],
        [blk, blk, blk],
        [_sds((N_CHIPS, t, fs), F32), _sds((N_CHIPS, t, fs), F32), _sds((N_CHIPS, t, fs), BF)],
        (h, wgu), carry=carry)


def _ffn_down(s, wd, x, gt, seq, coef):
    _, t, fs = s.shape
    d = x.shape[-1]
    per = seq // TM

    def body(s_ref, w_ref, x_ref, gt_ref, f_ref, xo_ref, acc):
        j = pl.program_id(1)

        @pl.when(j == 0)
        def _():
            acc[...] = jnp.zeros_like(acc)

        acc[...] += _nn(s_ref[...], w_ref[...])

        @pl.when(j == N_CHIPS - 1)
        def _():
            f = acc[...]
            f_ref[...] = f
            xo_ref[...] = x_ref[...] + (coef * gt_ref[...]) * f

    row = pl.BlockSpec((TM, d), lambda m, j: (m, 0))
    return pl.pallas_call(
        body, name="ffn_down", grid=(t // TM, N_CHIPS),
        in_specs=[pl.BlockSpec((None, TM, fs), lambda m, j: (j, m, 0)),
                  pl.BlockSpec((None, None, fs, d), lambda m, j: (j, 0, 0, 0)),
                  row,
                  pl.BlockSpec((None, 1, d), lambda m, j: (m // per, 0, 0))],
        out_specs=[row, row],
        out_shape=[_sds((t, d), F32), _sds((t, d), F32)],
        scratch_shapes=[pltpu.VMEM((TM, d), F32)],
        compiler_params=_cp(2))(s, wd, x, gt)


def _ffn_bwd_ds(dxo, gt, f, wd, a, u, seq, coef, carry=None):
    t, d = dxo.shape
    fs = a.shape[-1]
    per = seq // TM
    nb = t // seq

    def body(dxo_ref, gt_ref, f_ref, w_ref, a_ref, u_ref, da_ref, du_ref, df_ref, dgt_ref):
        m = pl.program_id(0)
        j = pl.program_id(1)
        dxv = dxo_ref[...]
        df = ((coef * gt_ref[...]) * dxv).astype(BF)

        @pl.when(j == 0)
        def _():
            df_ref[...] = df
            part = coef * jnp.sum(dxv * f_ref[...], axis=0, keepdims=True)
            _acc_rows(dgt_ref, part, m % per == 0)

        ds = _nt(df, w_ref[...])
        av = a_ref[...]
        sig = jax.nn.sigmoid(av)
        da_ref[...] = (ds * u_ref[...] * (sig * (1.0 + av * (1.0 - sig)))).astype(BF)
        du_ref[...] = (ds * (av * sig)).astype(BF)

    row = pl.BlockSpec((TM, d), lambda m, j: (m, 0))
    blk = pl.BlockSpec((None, TM, fs), lambda m, j: (j, m, 0))
    ex = pl.BlockSpec((None, 1, d), lambda m, j: (m // per, 0, 0))
    return _call(
        body, "ffn_bwd_ds", (t // TM, N_CHIPS),
        [row, ex, row, pl.BlockSpec((None, None, fs, d), lambda m, j: (j, 0, 0, 0)), blk, blk],
        [blk, blk, row, ex],
        [_sds((N_CHIPS, t, fs), BF), _sds((N_CHIPS, t, fs), BF), _sds((t, d), BF), _sds((nb, 1, d), F32)],
        (dxo, gt, f, wd, a, u), carry=carry)


def _ffn_bwd_w(h, da, du, s, df):
    t, d = h.shape
    fs = da.shape[-1]

    def body(h_ref, da_ref, du_ref, s_ref, df_ref, dgu_ref, dwd_ref):
        kt = pl.program_id(1)
        hv = h_ref[...]
        pg = _tn(hv, da_ref[...])
        pu = _tn(hv, du_ref[...])
        pd = _tn(s_ref[...], df_ref[...])

        @pl.when(kt == 0)
        def _():
            dgu_ref[0] = pg
            dgu_ref[1] = pu
            dwd_ref[...] = pd

        @pl.when(kt != 0)
        def _():
            dgu_ref[0] += pg
            dgu_ref[1] += pu
            dwd_ref[...] += pd

    row = pl.BlockSpec((TM, d), lambda j, kt: (kt, 0))
    blk = pl.BlockSpec((None, TM, fs), lambda j, kt: (j, kt, 0))
    return pl.pallas_call(
        body, name="ffn_bwd_w", grid=(N_CHIPS, t // TM),
        in_specs=[row, blk, blk, blk, row],
        out_specs=[pl.BlockSpec((None, 2, d, fs), lambda j, kt: (j, 0, 0, 0)),
                   pl.BlockSpec((None, None, fs, d), lambda j, kt: (j, 0, 0, 0))],
        out_shape=[_sds((N_CHIPS, 2, d, fs), F32), _sds((N_CHIPS, 1, fs, d), F32)],
        compiler_params=_cp(2))(h, da, du, s, df)


def _ffn_bwd_dh(da, du, wgu, x, g, sc, dxo, seq, carry=None):
    _, t, fs = da.shape
    d = x.shape[-1]
    per = seq // TM
    nb = t // seq

    def body(da_ref, du_ref, w_ref, x_ref, g_ref, sc_ref, dxo_ref, dx_ref, dsh_ref, dsc_ref, dg_ref, acc):
        m = pl.program_id(0)
        j = pl.program_id(1)

        @pl.when(j == 0)
        def _():
            acc[...] = jnp.zeros_like(acc)

        acc[...] += _nt(da_ref[...], w_ref[0]) + _nt(du_ref[...], w_ref[1])

        @pl.when(j == N_CHIPS - 1)
        def _():
            dx, dsh, dsc, dg = _modnorm_bwd_tile(acc[...], x_ref[...], g_ref[...], sc_ref[...], dxo_ref[...])
            dx_ref[...] = dx
            _acc_rows(dsh_ref, dsh, m % per == 0)
            _acc_rows(dsc_ref, dsc, m % per == 0)
            _acc_rows(dg_ref, dg, m == 0)

    row = pl.BlockSpec((TM, d), lambda m, j: (m, 0))
    blk = pl.BlockSpec((None, TM, fs), lambda m, j: (j, m, 0))
    ex = pl.BlockSpec((None, 1, d), lambda m, j: (m // per, 0, 0))
    vec = pl.BlockSpec((1, d), lambda m, j: (0, 0))
    return _call(
        body, "ffn_bwd_dh", (t // TM, N_CHIPS),
        [blk, blk, pl.BlockSpec((None, 2, d, fs), lambda m, j: (j, 0, 0, 0)), row, vec, ex, row],
        [row, ex, ex, vec],
        [_sds((t, d), F32), _sds((nb, 1, d), F32), _sds((nb, 1, d), F32), _sds((1, d), F32)],
        (da, du, wgu, x, g, sc, dxo), scratch=[pltpu.VMEM((TM, d), F32)], carry=carry)


def _qkv_proj(h, w_in, carry=None):
    t, d = h.shape
    wc = w_in.shape[-1]
    nt = 256
    per_chip = wc // nt
    n_tiles = N_CHIPS * per_chip
    per_out = D_GRP // nt

    def body(h_ref, w_ref, o_ref):
        o_ref[...] = _nn(h_ref[...], w_ref[...]).astype(BF)

    return _call(
        body, "qkv_proj", (n_tiles, t // TM),
        [pl.BlockSpec((TM, d), lambda n, m: (m, 0)),
         pl.BlockSpec((None, None, d, nt), lambda n, m: (n // per_chip, 0, 0, n % per_chip))],
        [pl.BlockSpec((None, TM, nt), lambda n, m: (n // per_out, m, n % per_out))],
        [_sds((6, t, D_GRP), BF)], (h, w_in), carry=carry)


def _mix_out(on_sb, on_dil, w_out, x, gt, seq):
    t, d = x.shape
    per = seq // TM

    def body(a_ref, b_ref, w_ref, x_ref, gt_ref, t_ref, xo_ref):
        tv = _nn(a_ref[...], w_ref[0:D_GRP, :]) + _nn(b_ref[...], w_ref[D_GRP:2 * D_GRP, :])
        t_ref[...] = tv
        xo_ref[...] = x_ref[...] + gt_ref[...] * tv

    row = pl.BlockSpec((TM, d), lambda m: (m, 0))
    half = pl.BlockSpec((TM, D_GRP), lambda m: (m, 0))
    return pl.pallas_call(
        body, name="mix_out", grid=(t // TM,),
        in_specs=[half, half, pl.BlockSpec((2 * D_GRP, d), lambda m: (0, 0)), row,
                  pl.BlockSpec((None, 1, d), lambda m: (m // per, 0, 0))],
        out_specs=[row, row],
        out_shape=[_sds((t, d), F32), _sds((t, d), F32)],
        compiler_params=_cp(1))(on_sb, on_dil, w_out, x, gt)


def _sb_masks():
    lane = lax.broadcasted_iota(jnp.int32, (1, 2 * HEAD_DIM), 1)
    hm0 = lane < HEAD_DIM
    rel = lax.broadcasted_iota(jnp.int32, (TQ, KB), 0) - lax.broadcasted_iota(jnp.int32, (TQ, KB), 1)
    kr = lax.broadcasted_iota(jnp.int32, (KB, KB), 0)
    kc = lax.broadcasted_iota(jnp.int32, (KB, KB), 1)
    return hm0, rel, kr, kc


def _headnorm_pair(o, gv, hm0):
    o2 = o * o
    ms0 = jnp.sum(jnp.where(hm0, o2, 0.0), axis=-1, keepdims=True) * (1.0 / HEAD_DIM)
    ms1 = jnp.sum(jnp.where(hm0, 0.0, o2), axis=-1, keepdims=True) * (1.0 / HEAD_DIM)
    r = jnp.where(hm0, lax.rsqrt(ms0 + EPS), lax.rsqrt(ms1 + EPS))
    return (o * r) * gv


SB_DEAD = -104.0


def _alive(c_l):
    return (jnp.max(c_l) > SB_DEAD).astype(jnp.int32)


def _sb_fwd(qkv6, g_sb, nb, seq):
    nq = seq // TQ

    def body(q_ref, k_ref, v_ref, g_ref, o_ref, on_ref):
        qi = pl.program_id(2)
        hm0, rel, kr, kc = _sb_masks()
        upper = (kr > kc).astype(BF)
        qv = q_ref[...]
        outs = []
        for hh in range(2):
            hm = hm0 if hh == 0 else jnp.logical_not(hm0)
            qh = jnp.where(hm, qv, jnp.zeros_like(qv))

            def block(kj, c_l, causal, qh=qh):
                ks = pl.multiple_of(kj * KB, KB)
                z = _nt(qh, k_ref[pl.ds(ks, KB), :]) * SCALE
                sp = _softplus(z)
                ln = -sp if causal is None else jnp.where(causal, -sp, 0.0)
                suf = _nn2(ln, upper)
                w = jnp.exp((z - sp) + (suf + c_l))
                if causal is not None:
                    w = jnp.where(causal, w, 0.0)
                pv = _nn(w.astype(BF), v_ref[pl.ds(ks, KB), :])
                return pv, c_l + (suf[:, 0:1] + ln[:, 0:1])

            acc, c_l = block(qi, jnp.zeros((TQ, 1), F32), rel > 0)

            def cond(carry):
                it, alive, _, _ = carry
                return jnp.logical_and(it <= qi, alive > 0)

            def kbody(carry):
                it, _, c_l, acc = carry
                pv, c_l = block(qi - it, c_l, None)
                return it + 1, _alive(c_l), c_l, acc + pv

            _, _, _, acc = lax.while_loop(cond, kbody, (jnp.int32(1), _alive(c_l), c_l, acc))
            outs.append(acc)
        o = jnp.where(hm0, outs[0], outs[1])
        o_ref[...] = o
        on_ref[...] = _headnorm_pair(o, g_ref[...], hm0).astype(BF)

    w = 2 * HEAD_DIM
    full = lambda i: pl.BlockSpec((None, None, seq, w), lambda b, hp, q: (i, b, 0, hp))
    qblk = pl.BlockSpec((None, None, TQ, w), lambda b, hp, q: (0, b, q, hp))
    oblk = pl.BlockSpec((None, TQ, w), lambda b, hp, q: (b, q, hp))
    return pl.pallas_call(
        body, name="sb_fwd", grid=(nb, N_HEADS // 2, nq),
        in_specs=[qblk, full(1), full(2), pl.BlockSpec((1, w), lambda b, hp, q: (0, hp))],
        out_specs=[oblk, oblk],
        out_shape=[_sds((nb, seq, D_GRP), F32), _sds((nb, seq, D_GRP), BF)],
        compiler_params=_cp(3))(qkv6, qkv6, qkv6, g_sb)


def _sb_bwd(qkv6, do, nb, seq, carry=None):
    nq = seq // TQ
    nk = seq // KB

    def body(q_ref, k_ref, v_ref, do_ref, out_ref, dk_acc, dv_acc, car):
        qi = pl.program_id(2)
        hm0, rel, kr, kc = _sb_masks()
        upper = (kr > kc).astype(BF)
        lower = (kr < kc).astype(BF)
        ones = jnp.ones((KB, 2 * HEAD_DIM), BF)

        @pl.when(qi == 0)
        def _():
            dk_acc[...] = jnp.zeros_like(dk_acc)
            dv_acc[...] = jnp.zeros_like(dv_acc)

        qv = q_ref[...]
        dov = do_ref[...]
        dqs = []
        for hh in range(2):
            hm = hm0 if hh == 0 else jnp.logical_not(hm0)
            qh = jnp.where(hm, qv, jnp.zeros_like(qv))
            doh = jnp.where(hm, dov, 0.0).astype(BF)

            def logits(kj, causal, qh=qh):
                ks = pl.multiple_of(kj * KB, KB)
                kb = k_ref[pl.ds(ks, KB), :]
                z = _nt(qh, kb) * SCALE
                sp = _softplus(z)
                ln = -sp if causal is None else jnp.where(causal, -sp, 0.0)
                return ks, kb, z - sp, ln

            car[qi] = jnp.zeros((TQ, 1), F32)
            c_l = _nn2(logits(qi, rel > 0)[3], ones)[:, 0:1]

            def acond(carry):
                it, alive, _ = carry
                return jnp.logical_and(it <= qi, alive > 0)

            def abody(carry):
                it, _, c_l = carry
                car[qi - it] = c_l
                c_l = c_l + _nn2(logits(qi - it, None)[3], ones)[:, 0:1]
                return it + 1, _alive(c_l), c_l

            n_used, _, _ = lax.while_loop(acond, abody, (jnp.int32(1), _alive(c_l), c_l))

            def grads(kj, causal, c_g, dq, doh=doh, qh=qh):
                ks, kb, lsz, ln = logits(kj, causal)
                vb = v_ref[pl.ds(ks, KB), :]
                w = jnp.exp(lsz + (_nn2(ln, upper) + car[kj]))
                if causal is not None:
                    w = jnp.where(causal, w, 0.0)
                g = w * _nt(doh, vb)
                pre = _nn2(g, lower)
                sig = jnp.exp(lsz)
                dz = g * (1.0 - sig) - sig * (pre + c_g)
                if causal is not None:
                    dz = jnp.where(causal, dz, 0.0)
                dzb = (dz * SCALE).astype(BF)
                dk_acc[pl.ds(ks, KB), :] += _tn(dzb, qh)
                dv_acc[pl.ds(ks, KB), :] += _tn(w.astype(BF), doh)
                return c_g + (pre[:, KB - 1:KB] + g[:, KB - 1:KB]), dq + _nn(dzb, kb)

            c_g, dq = lax.fori_loop(qi - n_used + 1, qi, lambda kj, cr: grads(kj, None, *cr),
                                    (jnp.zeros((TQ, 1), F32), jnp.zeros((TQ, 2 * HEAD_DIM), F32)))
            _, dq = grads(qi, rel > 0, c_g, dq)
            dqs.append(dq)
        dq = jnp.where(hm0, dqs[0], dqs[1])
        out_ref[0, pl.ds(pl.multiple_of(qi * TQ, TQ), TQ), :] = dq.astype(BF)

        @pl.when(qi == nq - 1)
        def _():
            out_ref[1] = dk_acc[...].astype(BF)
            out_ref[2] = dv_acc[...].astype(BF)

    w = 2 * HEAD_DIM
    full = lambda i: pl.BlockSpec((None, None, seq, w), lambda b, hp, q: (i, b, 0, hp))
    qblk = pl.BlockSpec((None, None, TQ, w), lambda b, hp, q: (0, b, q, hp))
    oblk = pl.BlockSpec((None, TQ, w), lambda b, hp, q: (b, q, hp))
    return _call(
        body, "sb_bwd", (nb, N_HEADS // 2, nq),
        [qblk, full(1), full(2), oblk],
        [pl.BlockSpec((3, None, seq, w), lambda b, hp, q: (0, b, 0, hp))],
        [_sds((6, nb, seq, D_GRP), BF)], (qkv6, qkv6, qkv6, do),
        scratch=[pltpu.VMEM((seq, w), F32), pltpu.VMEM((seq, w), F32), pltpu.VMEM((nk, TQ, 1), F32)],
        carry=carry)


def _t5_bucket(n):
    max_exact = N_BUCKETS // 2
    nf = np.maximum(n, 1).astype(np.float32)
    large = max_exact + (np.log(nf / max_exact) / math.log(MAX_DISTANCE / max_exact)
                         * (N_BUCKETS - max_exact)).astype(np.int32)
    large = np.minimum(large, N_BUCKETS - 1)
    return np.where(n < max_exact, n, large).astype(np.int32)


def _bucket_map(dilation):
    step = BLOCK + np.arange(BLOCK)[:, None] - np.arange(2 * BLOCK)[None, :]
    return _t5_bucket(np.clip(step, 0, N_STEPS) * dilation)


def _dil_masks():
    lane = lax.broadcasted_iota(jnp.int32, (1, 2 * HEAD_DIM), 1)
    hm0 = lane < HEAD_DIM
    iq = lax.broadcasted_iota(jnp.int32, (BLOCK, BLOCK), 0)
    ik = lax.broadcasted_iota(jnp.int32, (BLOCK, BLOCK), 1)
    return hm0, ik <= iq, ik >= iq


def _dil_probs(qh, kc, kp, b_ref, hh, valid_c, valid_p):
    zc = _nt(qh, kc) * SCALE + b_ref[hh, :, BLOCK:2 * BLOCK]
    zp = _nt(qh, kp) * SCALE + b_ref[hh, :, 0:BLOCK]
    zc = jnp.where(valid_c, zc, NEG_INF)
    zp = jnp.where(valid_p, zp, NEG_INF)
    m = jnp.maximum(jnp.max(zc, axis=-1, keepdims=True), jnp.max(zp, axis=-1, keepdims=True))
    ec = jnp.exp(zc - m)
    ep = jnp.exp(zp - m)
    den = jnp.sum(ec, axis=-1, keepdims=True) + jnp.sum(ep, axis=-1, keepdims=True)
    return ec, ep, den, m


def _dil_fwd(qkv6r, bias, nb, sub_len, dilation):
    n_blk = sub_len // BLOCK
    w = 2 * HEAD_DIM

    def body(q_ref, k_ref, v_ref, b_ref, o_ref, l_ref):
        hm0, valid_c, valid_p0 = _dil_masks()

        def nbody(n, carry):
            rs = pl.multiple_of(n * BLOCK, BLOCK)
            ps = pl.multiple_of(jnp.maximum(n - 1, 0) * BLOCK, BLOCK)
            qv = q_ref[pl.ds(rs, BLOCK), :]
            kc = k_ref[pl.ds(rs, BLOCK), :]
            kp = k_ref[pl.ds(ps, BLOCK), :]
            vc = v_ref[pl.ds(rs, BLOCK), :]
            vp = v_ref[pl.ds(ps, BLOCK), :]
            valid_p = jnp.logical_and(valid_p0, n > 0)
            os, ls = [], []
            for hh in range(2):
                hm = hm0 if hh == 0 else jnp.logical_not(hm0)
                qh = jnp.where(hm, qv, jnp.zeros_like(qv))
                ec, ep, den, m = _dil_probs(qh, kc, kp, b_ref, hh, valid_c, valid_p)
                os.append((_nn(ec.astype(BF), vc) + _nn(ep.astype(BF), vp)) / den)
                ls.append(m + jnp.log(den))
            o_ref[pl.ds(rs, BLOCK), :] = jnp.where(hm0, os[0], os[1])
            l_ref[pl.ds(rs, BLOCK), :] = jnp.where(hm0, ls[0], ls[1])
            return carry

        lax.fori_loop(0, n_blk, nbody, 0)

    seqblk = lambda i: pl.BlockSpec((None, None, sub_len, w), lambda b, g: (i, b, 0, g))
    oblk = pl.BlockSpec((None, sub_len, w), lambda b, g: (b, 0, g))
    shp = _sds((nb, sub_len, dilation * D_GRP), F32)
    return pl.pallas_call(
        body, name="dil_fwd_%d" % dilation, grid=(nb, dilation * (N_HEADS // 2)),
        in_specs=[seqblk(3), seqblk(4), seqblk(5),
                  pl.BlockSpec((2, BLOCK, 2 * BLOCK), lambda b, g: (g % (N_HEADS // 2), 0, 0))],
        out_specs=[oblk, oblk], out_shape=[shp, shp],
        compiler_params=_cp(2))(qkv6r, qkv6r, qkv6r, bias)


def _dil_bwd(qkv6r, bias, do_c, dd_c, nb, sub_len, dilation, carry=None):
    n_blk = sub_len // BLOCK
    w = 2 * HEAD_DIM
    hp_n = N_HEADS // 2

    def body(q_ref, k_ref, v_ref, b_ref, do_ref, dd_ref, out_ref, a_ref, dk_acc, dv_acc):
        hm0, valid_c, valid_p0 = _dil_masks()
        first = jnp.logical_and(pl.program_id(1) == 0, pl.program_id(2) == 0)

        @pl.when(first)
        def _():
            a_ref[...] = jnp.zeros_like(a_ref)

        dk_acc[...] = jnp.zeros_like(dk_acc)
        dv_acc[...] = jnp.zeros_like(dv_acc)

        def nbody(n, carry):
            rs = pl.multiple_of(n * BLOCK, BLOCK)
            ps = pl.multiple_of(jnp.maximum(n - 1, 0) * BLOCK, BLOCK)
            qv = q_ref[pl.ds(rs, BLOCK), :]
            kc = k_ref[pl.ds(rs, BLOCK), :]
            kp = k_ref[pl.ds(ps, BLOCK), :]
            vc = v_ref[pl.ds(rs, BLOCK), :]
            vp = v_ref[pl.ds(ps, BLOCK), :]
            dov = do_ref[pl.ds(rs, BLOCK), :]
            ddv = dd_ref[pl.ds(rs, BLOCK), :]
            valid_p = jnp.logical_and(valid_p0, n > 0)
            dq = jnp.zeros((BLOCK, w), F32)
            for hh in range(2):
                hm = hm0 if hh == 0 else jnp.logical_not(hm0)
                qh = jnp.where(hm, qv, jnp.zeros_like(qv))
                doh = jnp.where(hm, dov, 0.0).astype(BF)
                ddh = jnp.sum(jnp.where(hm, ddv, 0.0), axis=-1, keepdims=True) * (1.0 / HEAD_DIM)
                ec, ep, den, _ = _dil_probs(qh, kc, kp, b_ref, hh, valid_c, valid_p)
                inv = 1.0 / den
                pc = ec * inv
                pp = ep * inv
                dzc = pc * (_nt(doh, vc) + ddh)
                dzp = pp * (_nt(doh, vp) + ddh)
                a_ref[hh, :, BLOCK:2 * BLOCK] += dzc
                a_ref[hh, :, 0:BLOCK] += dzp
                dzcb = (dzc * SCALE).astype(BF)
                dzpb = (dzp * SCALE).astype(BF)
                dq = jnp.where(hm, _nn(dzcb, kc) + _nn(dzpb, kp), dq)
                dk_acc[pl.ds(rs, BLOCK), :] += _tn(dzcb, qh)
                dk_acc[pl.ds(ps, BLOCK), :] += _tn(dzpb, qh)
                dv_acc[pl.ds(rs, BLOCK), :] += _tn(pc.astype(BF), doh)
                dv_acc[pl.ds(ps, BLOCK), :] += _tn(pp.astype(BF), doh)
            out_ref[0, pl.ds(rs, BLOCK), :] = dq
            return carry

        lax.fori_loop(0, n_blk, nbody, 0)
        out_ref[1] = dk_acc[...]
        out_ref[2] = dv_acc[...]

    col = lambda hp, b, r: r * hp_n + hp
    seqblk = lambda i: pl.BlockSpec((None, None, sub_len, w), lambda hp, b, r: (i, b, 0, col(hp, b, r)))
    oblk = pl.BlockSpec((None, sub_len, w), lambda hp, b, r: (b, 0, col(hp, b, r)))
    return _call(
        body, "dil_bwd_%d" % dilation, (hp_n, nb, dilation),
        [seqblk(3), seqblk(4), seqblk(5),
         pl.BlockSpec((2, BLOCK, 2 * BLOCK), lambda hp, b, r: (hp, 0, 0)), oblk, oblk],
        [pl.BlockSpec((3, None, sub_len, w), lambda hp, b, r: (0, b, 0, col(hp, b, r))),
         pl.BlockSpec((2, BLOCK, 2 * BLOCK), lambda hp, b, r: (hp, 0, 0))],
        [_sds((3, nb, sub_len, dilation * D_GRP), F32), _sds((N_HEADS, BLOCK, 2 * BLOCK), F32)],
        (qkv6r, qkv6r, qkv6r, bias, do_c, dd_c),
        scratch=[pltpu.VMEM((sub_len, w), F32), pltpu.VMEM((sub_len, w), F32)], carry=carry)


def _group_ones():
    idx = np.arange(D_GRP) // HEAD_DIM
    return jnp.asarray((idx[:, None] == idx[None, :]).astype(np.float32), dtype=BF)


def _dil_alphas(l1, l4, l16):
    mx = jnp.maximum(jnp.maximum(l1, l4), l16)
    e1 = jnp.exp(l1 - mx)
    e4 = jnp.exp(l4 - mx)
    e16 = jnp.exp(l16 - mx)
    den = e1 + e4 + e16
    return e1 / den, e4 / den, e16 / den


def _dil_comb(os, ls, g_dil, ones_g):
    t = os[0].shape[0]

    def body(o1, l1, o4, l4, o16, l16, g_ref, m_ref, o_ref, on_ref):
        a1, a4, a16 = _dil_alphas(l1[...], l4[...], l16[...])
        o = a1 * o1[...] + a4 * o4[...] + a16 * o16[...]
        o_ref[...] = o
        ms = _nn2(o * o, m_ref[...]) * (1.0 / HEAD_DIM)
        on_ref[...] = ((o * lax.rsqrt(ms + EPS)) * g_ref[...]).astype(BF)

    blk = pl.BlockSpec((TM, D_GRP), lambda m: (m, 0))
    return pl.pallas_call(
        body, name="dil_comb", grid=(t // TM,),
        in_specs=[blk] * 6 + [pl.BlockSpec((1, D_GRP), lambda m: (0, 0)),
                              pl.BlockSpec((D_GRP, D_GRP), lambda m: (0, 0))],
        out_specs=[blk, blk],
        out_shape=[_sds((t, D_GRP), F32), _sds((t, D_GRP), BF)],
        compiler_params=_cp(1))(os[0], ls[0], os[1], ls[1], os[2], ls[2], g_dil, ones_g)


def _dil_comb_bwd(do, os, ls, ones_g):
    t = do.shape[0]

    def body(do_ref, o1, l1, o4, l4, o16, l16, m_ref, d1, d4, d16, e1, e4, e16):
        dov = do_ref[...]
        a1, a4, a16 = _dil_alphas(l1[...], l4[...], l16[...])
        mv = m_ref[...]
        sbar = a1 * _nn2(dov * o1[...], mv) + a4 * _nn2(dov * o4[...], mv) + a16 * _nn2(dov * o16[...], mv)
        d1[...] = a1 * dov
        d4[...] = a4 * dov
        d16[...] = a16 * dov
        e1[...] = -a1 * sbar
        e4[...] = -a4 * sbar
        e16[...] = -a16 * sbar

    blk = pl.BlockSpec((TM, D_GRP), lambda m: (m, 0))
    shp = _sds((t, D_GRP), F32)
    return pl.pallas_call(
        body, name="dil_comb_bwd", grid=(t // TM,),
        in_specs=[blk] * 7 + [pl.BlockSpec((D_GRP, D_GRP), lambda m: (0, 0))],
        out_specs=[blk] * 6, out_shape=[shp] * 6,
        compiler_params=_cp(1))(do, os[0], ls[0], os[1], ls[1], os[2], ls[2], ones_g)


def _dqkv_dil_sum(d1, d4, d16, dqkv6):
    t = d1.shape[1]

    def body(a, b, c, alias, o_ref):
        del alias
        o_ref[...] = (a[...] + b[...] + c[...]).astype(BF)

    blk = pl.BlockSpec((3, TM, D_GRP), lambda m: (0, m, 0))
    return pl.pallas_call(
        body, name="dqkv_dil_sum", grid=(t // TM,),
        in_specs=[blk, blk, blk, pl.BlockSpec(memory_space=pl.ANY)],
        out_specs=pl.BlockSpec((3, TM, D_GRP), lambda m: (1, m, 0)),
        out_shape=_sds((6, t, D_GRP), BF), input_output_aliases={3: 0},
        compiler_params=_cp(1))(d1, d4, d16, dqkv6)


def _relbias_grad(a_all, onehot):
    def body(a_ref, oh_ref, o_ref):
        acc = jnp.zeros((N_HEADS, N_BUCKETS), F32)
        for c in range(len(DIL_CONFIGS)):
            av = a_ref[c]
            hi = av.astype(BF)
            lo = (av - hi.astype(F32)).astype(BF)
            acc = acc + _nt(hi, oh_ref[c]) + _nt(lo, oh_ref[c])
        o_ref[...] = acc

    return pl.pallas_call(body, name="relbias_grad", out_shape=_sds((N_HEADS, N_BUCKETS), F32),
                          compiler_params=_cp())(a_all, onehot)


def _headnorm_bwd(dn, o, gv, mv):
    ms = _nn2(o * o, mv) * (1.0 / HEAD_DIM)
    r = lax.rsqrt(ms + EPS)
    nrm = o * r
    dg = jnp.sum(dn * nrm, axis=0, keepdims=True)
    dnn = dn * gv
    do = r * (dnn - nrm * (_nn2(dnn * nrm, mv) * (1.0 / HEAD_DIM)))
    return do, dg


def _mix_bwd_out(dx, gt, tv, w_out, o_sb, o_dil, on_sb, on_dil, g_sb, g_dil, ones_g, seq):
    t, d = dx.shape
    per = seq // TM
    nb = t // seq

    def body(dx_ref, gt_ref, t_ref, w_ref, osb, odl, onsb, ondl, gsb, gdl, m_ref,
             dosb, dodl, dgt_ref, dgsb, dgdl, dw_ref):
        m = pl.program_id(0)
        dxv = dx_ref[...]
        dt = (gt_ref[...] * dxv).astype(BF)
        _acc_rows(dgt_ref, jnp.sum(dxv * t_ref[...], axis=0, keepdims=True), m % per == 0)
        mv = m_ref[...]
        don_sb = _nt(dt, w_ref[0:D_GRP, :])
        don_dl = _nt(dt, w_ref[D_GRP:2 * D_GRP, :])
        do1, dg1 = _headnorm_bwd(don_sb, osb[...], gsb[...], mv)
        do2, dg2 = _headnorm_bwd(don_dl, odl[...], gdl[...], mv)
        dosb[...] = do1
        dodl[...] = do2
        _acc_rows(dgsb, dg1, m == 0)
        _acc_rows(dgdl, dg2, m == 0)
        p1 = _tn(onsb[...], dt)
        p2 = _tn(ondl[...], dt)

        @pl.when(m == 0)
        def _():
            dw_ref[0:D_GRP, :] = p1
            dw_ref[D_GRP:2 * D_GRP, :] = p2

        @pl.when(m != 0)
        def _():
            dw_ref[0:D_GRP, :] += p1
            dw_ref[D_GRP:2 * D_GRP, :] += p2

    row = pl.BlockSpec((TM, d), lambda m: (m, 0))
    half = pl.BlockSpec((TM, D_GRP), lambda m: (m, 0))
    ex = pl.BlockSpec((None, 1, d), lambda m: (m // per, 0, 0))
    gvec = pl.BlockSpec((1, D_GRP), lambda m: (0, 0))
    wblk = pl.BlockSpec((2 * D_GRP, d), lambda m: (0, 0))
    return pl.pallas_call(
        body, name="mix_bwd_out", grid=(t // TM,),
        in_specs=[row, ex, row, wblk, half, half, half, half, gvec, gvec,
                  pl.BlockSpec((D_GRP, D_GRP), lambda m: (0, 0))],
        out_specs=[half, half, ex, gvec, gvec, wblk],
        out_shape=[_sds((t, D_GRP), F32), _sds((t, D_GRP), F32), _sds((nb, 1, d), F32),
                   _sds((1, D_GRP), F32), _sds((1, D_GRP), F32), _sds((2 * D_GRP, d), F32)],
        compiler_params=_cp(1))(dx, gt, tv, w_out, o_sb, o_dil, on_sb, on_dil, g_sb, g_dil, ones_g)


def _dw_in(h, dqkv6, carry=None):
    t, d = h.shape
    nt = 256
    per_chip = 3
    per_out = D_GRP // nt

    def body(h_ref, g_ref, o_ref):
        p = _tn(h_ref[...], g_ref[...])
        _acc_rows(o_ref, p, pl.program_id(1) == 0)

    return _call(
        body, "dw_in", (N_CHIPS * per_chip, t // TM),
        [pl.BlockSpec((TM, d), lambda n, kt: (kt, 0)),
         pl.BlockSpec((None, TM, nt), lambda n, kt: (n // per_out, kt, n % per_out))],
        [pl.BlockSpec((None, None, d, nt), lambda n, kt: (n // per_chip, 0, 0, n % per_chip))],
        [_sds((N_CHIPS, 1, d, per_chip * nt), F32)], (h, dqkv6), carry=carry)


def _mix_bwd_dh(dqkv6, w_in, x, g, sc, dxo, seq, carry=None):
    _, t, _ = dqkv6.shape
    d = x.shape[-1]
    nt = 256
    per_chip = 3
    per_out = D_GRP // nt
    n_tiles = N_CHIPS * per_chip
    per = seq // TM
    nb = t // seq

    def body(g6_ref, w_ref, x_ref, g_ref, sc_ref, dxo_ref, dx_ref, dsh_ref, dsc_ref, dg_ref, acc):
        m = pl.program_id(0)
        n = pl.program_id(1)

        @pl.when(n == 0)
        def _():
            acc[...] = jnp.zeros_like(acc)

        acc[...] += _nt(g6_ref[...], w_ref[...])

        @pl.when(n == n_tiles - 1)
        def _():
            dx, dsh, dsc, dg = _modnorm_bwd_tile(acc[...], x_ref[...], g_ref[...], sc_ref[...], dxo_ref[...])
            dx_ref[...] = dx
            _acc_rows(dsh_ref, dsh, m % per == 0)
            _acc_rows(dsc_ref, dsc, m % per == 0)
            _acc_rows(dg_ref, dg, m == 0)

    row = pl.BlockSpec((TM, d), lambda m, n: (m, 0))
    ex = pl.BlockSpec((None, 1, d), lambda m, n: (m // per, 0, 0))
    vec = pl.BlockSpec((1, d), lambda m, n: (0, 0))
    return _call(
        body, "mix_bwd_dh", (t // TM, n_tiles),
        [pl.BlockSpec((None, TM, nt), lambda m, n: (n // per_out, m, n % per_out)),
         pl.BlockSpec((None, None, d, nt), lambda m, n: (n // per_chip, 0, 0, n % per_chip)),
         row, vec, ex, row],
        [row, ex, ex, vec],
        [_sds((t, d), F32), _sds((nb, 1, d), F32), _sds((nb, 1, d), F32), _sds((1, d), F32)],
        (dqkv6, w_in, x, g, sc, dxo), scratch=[pltpu.VMEM((TM, d), F32)], carry=carry)


def _final_loss(x, g, target):
    t, d = x.shape
    steps = t // TM

    def body(x_ref, g_ref, t_ref, dx_ref, dg_ref, loss_ref, lacc):
        m = pl.program_id(0)
        xv = x_ref[...]
        gv = g_ref[...]
        r = lax.rsqrt(jnp.mean(xv * xv, axis=-1, keepdims=True) + EPS)
        n = xv * r
        err = n * gv - t_ref[...]
        dy = err * (1.0 / d)
        _acc_rows(dg_ref, jnp.sum(dy * n, axis=0, keepdims=True), m == 0)
        dn = dy * gv
        dx_ref[...] = r * (dn - n * jnp.mean(dn * n, axis=-1, keepdims=True))
        _acc_rows(lacc, jnp.sum(err * err, axis=0, keepdims=True), m == 0)

        @pl.when(m == steps - 1)
        def _():
            tot = jnp.sum(lacc[...], axis=-1, keepdims=True) * (0.5 / d)
            loss_ref[...] = jnp.broadcast_to(tot, (1, 128))

    row = pl.BlockSpec((TM, d), lambda m: (m, 0))
    vec = pl.BlockSpec((1, d), lambda m: (0, 0))
    return pl.pallas_call(
        body, name="final_loss", grid=(steps,),
        in_specs=[row, vec, row],
        out_specs=[row, vec, pl.BlockSpec((1, 128), lambda m: (0, 0))],
        out_shape=[_sds((t, d), F32), _sds((1, d), F32), _sds((1, 128), F32)],
        scratch_shapes=[pltpu.VMEM((1, d), F32)],
        compiler_params=_cp(1))(x, g, target)


def _row_tile(rows, cols):
    best = rows
    for tr in range(8, rows + 1, 8):
        if rows % tr == 0 and tr * cols * 4 <= (1 << 20):
            best = tr
    if best * cols * 4 > (1 << 21):
        best = 8
    return best


def _adamw(w, g_arr, g_sel, m, v):
    rows, cols = w.shape
    tr = _row_tile(rows, cols)
    b1c = 1.0 - ADAM_B1 ** ADAM_STEP
    b2c = 1.0 - ADAM_B2 ** ADAM_STEP

    def body(w_ref, g_ref, m_ref, v_ref, go_ref, d_ref, mo_ref, vo_ref):
        gv = g_ref[...]
        mn = ADAM_B1 * m_ref[...] + (1.0 - ADAM_B1) * gv
        vn = ADAM_B2 * v_ref[...] + (1.0 - ADAM_B2) * (gv * gv)
        go_ref[...] = gv
        mo_ref[...] = mn
        vo_ref[...] = vn
        d_ref[...] = -ADAM_LR * ((mn / b1c) / (jnp.sqrt(vn / b2c) + ADAM_EPS) + ADAM_WD * w_ref[...])

    blk = pl.BlockSpec((tr, cols), lambda i: (i, 0))
    shp = _sds((rows, cols), F32)
    return pl.pallas_call(
        body, name="adamw", grid=(rows // tr,),
        in_specs=[blk, pl.BlockSpec((None, tr, cols), lambda i: (g_sel, i, 0)), blk, blk],
        out_specs=[blk] * 4, out_shape=[shp] * 4,
        compiler_params=_cp(1))(w, g_arr, m, v)


def _flip(v, bit):
    return 1 - v if bit else v


def _my_place():
    x, y, c = lax.axis_index("x"), lax.axis_index("y"), lax.axis_index("c")
    return x, y, c


class _Exchange:
    def __init__(self, operands, out_shape, aliases, sems, start, finish):
        self.operands, self.out_shape, self.aliases, self.sems = list(operands), list(out_shape), dict(aliases), list(sems)
        self.start, self.finish = start, finish


def _join(exchanges):
    exchanges = [e for e in exchanges if e is not None]
    if not exchanges:
        return None
    ops, outs, sems, aliases, spans = [], [], [], {}, []
    for e in exchanges:
        spans.append((len(ops), len(outs), len(sems), e))
        for i, j in e.aliases.items():
            aliases[len(ops) + i] = len(outs) + j
        ops += e.operands
        outs += e.out_shape
        sems += e.sems

    def run(which):
        def go(ins, res, sm):
            for io, oo, so, e in spans:
                getattr(e, which)(ins[io:io + len(e.operands)], res[oo:oo + len(e.out_shape)], sm[so:so + len(e.sems)])
        return go

    return _Exchange(ops, outs, aliases, sems, run("start"), run("finish"))


def _call(body, name, grid, in_specs, out_specs, out_shape, args, scratch=(), carry=None):
    in_specs, out_specs, out_shape, scratch = list(in_specs), list(out_specs), list(out_shape), list(scratch)
    if carry is None:
        return pl.pallas_call(body, name=name, grid=grid, in_specs=in_specs, out_specs=out_specs,
                              out_shape=out_shape, scratch_shapes=scratch,
                              compiler_params=_cp(len(grid)))(*args)
    n_in, n_out, n_s = len(in_specs), len(out_specs), len(scratch)
    c_in, c_out = len(carry.operands), len(carry.out_shape)
    any_spec = pl.BlockSpec(memory_space=pl.ANY)

    def wrapped(*refs):
        ins, cins = refs[:n_in], refs[n_in:n_in + c_in]
        o0 = n_in + c_in
        outs, couts = refs[o0:o0 + n_out], refs[o0 + n_out:o0 + n_out + c_out]
        s0 = o0 + n_out + c_out
        scr, sems = refs[s0:s0 + n_s], refs[s0 + n_s:]
        first = pl.program_id(0) == 0
        last = pl.program_id(0) == grid[0] - 1
        for ax in range(1, len(grid)):
            first = jnp.logical_and(first, pl.program_id(ax) == 0)
            last = jnp.logical_and(last, pl.program_id(ax) == grid[ax] - 1)

        @pl.when(first)
        def _():
            carry.start(cins, couts, sems)

        body(*ins, *outs, *scr)

        @pl.when(last)
        def _():
            carry.finish(cins, couts, sems)

    return pl.pallas_call(
        wrapped, name=name, grid=grid, in_specs=in_specs + [any_spec] * c_in,
        out_specs=out_specs + [any_spec] * c_out, out_shape=out_shape + carry.out_shape,
        scratch_shapes=scratch + carry.sems,
        input_output_aliases={n_in + i: n_out + j for i, j in carry.aliases.items()},
        compiler_params=_cp(len(grid)))(*args, *carry.operands)


def _alone(name, ex):
    any_spec = pl.BlockSpec(memory_space=pl.ANY)
    c_in, c_out = len(ex.operands), len(ex.out_shape)

    def body(*refs):
        ins, outs, sems = refs[:c_in], refs[c_in:c_in + c_out], refs[c_in + c_out:]
        ex.start(ins, outs, sems)
        ex.finish(ins, outs, sems)

    return pl.pallas_call(
        body, name=name, in_specs=[any_spec] * c_in, out_specs=[any_spec] * c_out, out_shape=ex.out_shape,
        scratch_shapes=ex.sems, input_output_aliases=ex.aliases, compiler_params=_cp())(*ex.operands)


def _ada_fwd(c_pad, w_ada, b_shard):
    d = c_pad.shape[-1]
    cols = w_ada.shape[-1]
    chunk = 384

    def body(c_ref, w_ref, b_ref, call_ref, mod_ref, part, s1, r1, s2, r2):
        x, y, c = _my_place()
        dev = 4 * x + 2 * y + c
        chip = 2 * x + y
        call_ref[dev] = c_ref[...]

        def c_copy(k):
            px, py, pc = _flip(x, (k >> 2) & 1), _flip(y, (k >> 1) & 1), _flip(c, k & 1)
            return px, py, pc

        sends = []
        for k in range(1, N_DEV):
            px, py, pc = c_copy(k)
            cp = pltpu.make_async_remote_copy(src_ref=c_ref, dst_ref=call_ref.at[dev], send_sem=s1.at[k - 1],
                                              recv_sem=r1.at[k - 1], device_id=(px, py, pc), device_id_type=MESH)
            cp.start()
            sends.append(cp)
        for k in range(1, N_DEV):
            px, py, pc = c_copy(k)
            pltpu.make_async_remote_copy(src_ref=c_ref, dst_ref=call_ref.at[4 * px + 2 * py + pc],
                                         send_sem=s1.at[k - 1], recv_sem=r1.at[k - 1],
                                         device_id=(px, py, pc), device_id_type=MESH).wait_recv()
        for cp in sends:
            cp.wait_send()

        cs = call_ref[...].reshape(N_DEV * 8, d)
        sc = (cs * jax.nn.sigmoid(cs)).astype(BF)
        for n0 in range(0, cols, chunk):
            blk = _nn(sc, w_ref[:, n0:n0 + chunk].astype(BF)) + b_ref[:, n0:n0 + chunk]
            part[:, :, n0:n0 + chunk] = blk.reshape(N_DEV, 8, chunk)

        mod_ref[chip] = part[dev]
        sends = []
        for kk in range(1, N_CHIPS):
            px, py = _flip(x, (kk >> 1) & 1), _flip(y, kk & 1)
            cp = pltpu.make_async_remote_copy(src_ref=part.at[4 * px + 2 * py + c], dst_ref=mod_ref.at[chip],
                                              send_sem=s2.at[kk - 1], recv_sem=r2.at[kk - 1],
                                              device_id=(px, py, c), device_id_type=MESH)
            cp.start()
            sends.append(cp)
        for kk in range(1, N_CHIPS):
            px, py = _flip(x, (kk >> 1) & 1), _flip(y, kk & 1)
            pltpu.make_async_remote_copy(src_ref=part.at[dev], dst_ref=mod_ref.at[2 * px + py],
                                         send_sem=s2.at[kk - 1], recv_sem=r2.at[kk - 1],
                                         device_id=(px, py, c), device_id_type=MESH).wait_recv()
        for cp in sends:
            cp.wait_send()

    return pl.pallas_call(
        body, name="ada_fwd",
        out_shape=[_sds((N_DEV, 8, d), F32), _sds((N_CHIPS, 8, cols), F32)],
        scratch_shapes=[pltpu.VMEM((N_DEV, 8, cols), F32),
                        pltpu.SemaphoreType.DMA((N_DEV - 1,)), pltpu.SemaphoreType.DMA((N_DEV - 1,)),
                        pltpu.SemaphoreType.DMA((N_CHIPS - 1,)), pltpu.SemaphoreType.DMA((N_CHIPS - 1,))],
        compiler_params=_cp())(c_pad, w_ada, b_shard)


def _ag_weights(bufs):
    n = len(bufs)

    def place():
        x, y, c = _my_place()
        others = [(_flip(x, (kk >> 1) & 1), _flip(y, kk & 1)) for kk in range(1, N_CHIPS)]
        return x, y, c, 2 * x + y, others

    def half(b, which):
        hr = bufs[b].shape[2] // 2
        return pl.ds(pl.multiple_of(which * hr, 16), hr)

    def ici(outs, sems, b, i, slot, x, y, c, px, py):
        rows = outs[b].at[slot, :, half(b, c), :]
        return pltpu.make_async_remote_copy(
            src_ref=rows, dst_ref=rows, send_sem=sems[0].at[3 * b + i], recv_sem=sems[1].at[3 * b + i],
            device_id=(px, py, c), device_id_type=MESH)

    def d2d(outs, sems, b, i, slot, x, y, c, which):
        rows = outs[b].at[slot, :, half(b, which), :]
        return pltpu.make_async_remote_copy(
            src_ref=rows, dst_ref=rows, send_sem=sems[2].at[3 * b + i], recv_sem=sems[3].at[3 * b + i],
            device_id=(x, y, 1 - c), device_id_type=MESH)

    def start(ins, outs, sems):
        x, y, c, chip, others = place()
        for b in range(n):
            for i, (px, py) in enumerate(others):
                ici(outs, sems, b, i, chip, x, y, c, px, py).start()

    def finish(ins, outs, sems):
        x, y, c, chip, others = place()
        for b in range(n):
            for i, (px, py) in enumerate(others):
                ici(outs, sems, b, i, 2 * px + py, x, y, c, px, py).wait_recv()
                d2d(outs, sems, b, i, 2 * px + py, x, y, c, c).start()
        for b in range(n):
            for i, (px, py) in enumerate(others):
                d2d(outs, sems, b, i, 2 * px + py, x, y, c, 1 - c).wait_recv()
        for b in range(n):
            for i, (px, py) in enumerate(others):
                ici(outs, sems, b, i, chip, x, y, c, px, py).wait_send()
                d2d(outs, sems, b, i, 2 * px + py, x, y, c, c).wait_send()

    return _Exchange(bufs, [_sds(s.shape, s.dtype) for s in bufs], {i: i for i in range(n)},
                     [pltpu.SemaphoreType.DMA((3 * n,))] * 4, start, finish)


def _rs_d2d(grads):
    n = len(grads)

    def copy(ins, outs, sems, b):
        x, y, c = _my_place()
        hr = grads[b].shape[2] // 2
        theirs = pl.ds(pl.multiple_of((1 - c) * hr, 8), hr)
        return pltpu.make_async_remote_copy(
            src_ref=ins[b].at[:, :, theirs, :], dst_ref=outs[b], send_sem=sems[0].at[b], recv_sem=sems[1].at[b],
            device_id=(x, y, 1 - c), device_id_type=MESH)

    def start(ins, outs, sems):
        for b in range(n):
            copy(ins, outs, sems, b).start()

    def finish(ins, outs, sems):
        for b in range(n):
            copy(ins, outs, sems, b).wait()

    return _Exchange(grads, [_sds(g.shape[:2] + (g.shape[2] // 2, g.shape[3]), F32) for g in grads], {},
                     [pltpu.SemaphoreType.DMA((n,))] * 2, start, finish)


def _add_halves(core, g, land):
    nchip, ng, rows, cols = g.shape
    hr = rows // 2
    tr = _row_tile(hr, cols)
    steps = hr // tr

    def body(core_ref, g_ref, l_ref, o_ref):
        del core_ref
        o_ref[...] = (g_ref[...] + l_ref[...]).astype(BF)

    return pl.pallas_call(
        body, name="add_halves",
        grid_spec=pltpu.PrefetchScalarGridSpec(
            num_scalar_prefetch=1, grid=(nchip, ng, steps),
            in_specs=[pl.BlockSpec((None, None, tr, cols), lambda j, a, i, cr: (j, a, cr[0] * steps + i, 0)),
                      pl.BlockSpec((None, None, tr, cols), lambda j, a, i, cr: (j, a, i, 0))],
            out_specs=pl.BlockSpec((None, None, tr, cols), lambda j, a, i, cr: (j, a, i, 0))),
        out_shape=_sds((nchip, ng, hr, cols), BF),
        compiler_params=_cp(3))(core, g, land)


def _rs_ici(parts):
    n = len(parts)

    def copies(ins, outs, sems):
        x, y, c = _my_place()
        chip = 2 * x + y
        for b in range(n):
            for kk in range(1, N_CHIPS):
                px, py = _flip(x, (kk >> 1) & 1), _flip(y, kk & 1)
                k = 3 * b + kk - 1
                send = pltpu.make_async_remote_copy(
                    src_ref=ins[b].at[2 * px + py], dst_ref=outs[b].at[chip],
                    send_sem=sems[0].at[k], recv_sem=sems[1].at[k], device_id=(px, py, c), device_id_type=MESH)
                slot = outs[b].at[2 * px + py]
                recv = pltpu.make_async_remote_copy(
                    src_ref=slot, dst_ref=slot, send_sem=sems[0].at[k], recv_sem=sems[1].at[k],
                    device_id=(px, py, c), device_id_type=MESH)
                yield send, recv

    def start(ins, outs, sems):
        for send, _ in copies(ins, outs, sems):
            send.start()

    def finish(ins, outs, sems):
        for send, recv in copies(ins, outs, sems):
            recv.wait_recv()
            send.wait_send()

    return _Exchange(parts, [_sds(p.shape, p.dtype) for p in parts], {},
                     [pltpu.SemaphoreType.DMA((3 * n,))] * 2, start, finish)


def _sum_chips(place, part, land):
    nchip, ng, hr, cols = land.shape
    tr = _row_tile(hr, cols)
    steps = hr // tr

    def body(place_ref, p_ref, l1, l2, l3, o_ref):
        del place_ref
        o_ref[...] = ((p_ref[...].astype(F32) + l1[...].astype(F32)) + l2[...].astype(F32)) + l3[...].astype(F32)

    def slot(k):
        return pl.BlockSpec((None, None, tr, cols), lambda a, i, pr: (jnp.bitwise_xor(pr[1], k), a, i, 0))

    return pl.pallas_call(
        body, name="sum_chips",
        grid_spec=pltpu.PrefetchScalarGridSpec(
            num_scalar_prefetch=1, grid=(ng, steps),
            in_specs=[slot(0), slot(1), slot(2), slot(3)],
            out_specs=pl.BlockSpec((None, tr, cols), lambda a, i, pr: (a, pr[0] * steps + i, 0))),
        out_shape=_sds((ng, 2 * hr, cols), F32),
        compiler_params=_cp(2))(place, part, land, land, land)


def _rs_final(bufs):
    n = len(bufs)

    def copy(outs, sems, b, which):
        x, y, c = _my_place()
        hr = bufs[b].shape[1] // 2
        rows = outs[b].at[:, pl.ds(pl.multiple_of((c if which == 0 else 1 - c) * hr, 8), hr), :]
        return pltpu.make_async_remote_copy(
            src_ref=rows, dst_ref=rows, send_sem=sems[0].at[b], recv_sem=sems[1].at[b],
            device_id=(x, y, 1 - c), device_id_type=MESH)

    def start(ins, outs, sems):
        for b in range(n):
            copy(outs, sems, b, 0).start()

    def finish(ins, outs, sems):
        for b in range(n):
            copy(outs, sems, b, 0).wait_send()
            copy(outs, sems, b, 1).wait_recv()

    return _Exchange(bufs, [_sds(h.shape, F32) for h in bufs], {i: i for i in range(n)},
                     [pltpu.SemaphoreType.DMA((n,))] * 2, start, finish)


def _small_sync(smalls, dmod_blk, c_all):
    d = c_all.shape[-1]
    cols = dmod_blk.shape[-1]
    chunk = 384

    def body(sm_ref, dm_ref, c_ref, sum_ref, gw_ref, sm_all, dm_all, ssem, rsem):
        x, y, c = _my_place()
        dev = 4 * x + 2 * y + c
        chip = 2 * x + y
        sm_all[dev] = sm_ref[...]
        dm_all[dev] = dm_ref[chip]
        sends = []
        for k in range(1, N_DEV):
            px, py, pc = _flip(x, (k >> 2) & 1), _flip(y, (k >> 1) & 1), _flip(c, k & 1)
            a = pltpu.make_async_remote_copy(src_ref=sm_ref, dst_ref=sm_all.at[dev], send_sem=ssem.at[2 * (k - 1)],
                                             recv_sem=rsem.at[2 * (k - 1)], device_id=(px, py, pc),
                                             device_id_type=MESH)
            b = pltpu.make_async_remote_copy(src_ref=dm_ref.at[2 * px + py], dst_ref=dm_all.at[dev],
                                             send_sem=ssem.at[2 * (k - 1) + 1], recv_sem=rsem.at[2 * (k - 1) + 1],
                                             device_id=(px, py, pc), device_id_type=MESH)
            a.start()
            b.start()
            sends += [a, b]
        for k in range(1, N_DEV):
            px, py, pc = _flip(x, (k >> 2) & 1), _flip(y, (k >> 1) & 1), _flip(c, k & 1)
            pdev = 4 * px + 2 * py + pc
            pltpu.make_async_remote_copy(src_ref=sm_ref, dst_ref=sm_all.at[pdev], send_sem=ssem.at[2 * (k - 1)],
                                         recv_sem=rsem.at[2 * (k - 1)], device_id=(px, py, pc),
                                         device_id_type=MESH).wait_recv()
            pltpu.make_async_remote_copy(src_ref=dm_ref.at[chip], dst_ref=dm_all.at[pdev],
                                         send_sem=ssem.at[2 * (k - 1) + 1], recv_sem=rsem.at[2 * (k - 1) + 1],
                                         device_id=(px, py, pc), device_id_type=MESH).wait_recv()
        for cp in sends:
            cp.wait_send()

        tot = sm_all[0]
        for q in range(1, N_DEV):
            tot = tot + sm_all[q]
        sum_ref[...] = tot

        cs = c_ref[...].reshape(N_DEV * 8, d)
        sc = (cs * jax.nn.sigmoid(cs)).astype(BF)
        for n0 in range(0, cols, chunk):
            dmv = dm_all[:, :, n0:n0 + chunk].reshape(N_DEV * 8, chunk).astype(BF)
            gw_ref[:, n0:n0 + chunk] = _tn(sc, dmv)

    return pl.pallas_call(
        body, name="small_sync",
        out_shape=[_sds(smalls.shape, F32), _sds((d, cols), F32)],
        scratch_shapes=[pltpu.VMEM((N_DEV,) + smalls.shape, F32), pltpu.VMEM((N_DEV, 8, cols), F32),
                        pltpu.SemaphoreType.DMA((2 * (N_DEV - 1),)), pltpu.SemaphoreType.DMA((2 * (N_DEV - 1),))],
        compiler_params=_cp())(smalls, dmod_blk, c_all)


def _bucket_onehot():
    maps = np.stack([_bucket_map(dil).reshape(-1) for _, dil in DIL_CONFIGS])
    return (jnp.asarray(maps)[:, None, :] == jnp.arange(N_BUCKETS, dtype=jnp.int32)[None, :, None]).astype(BF)


def _dil_bias(rel_t, onehot):
    def body(r_ref, oh_ref, o_ref):
        rv = r_ref[...]
        hi = rv.astype(BF)
        lo = (rv - hi.astype(F32)).astype(BF)
        for c in range(len(DIL_CONFIGS)):
            o_ref[c] = _nn(hi, oh_ref[c]) + _nn(lo, oh_ref[c])

    return pl.pallas_call(body, name="dil_bias",
                          out_shape=_sds((len(DIL_CONFIGS), N_HEADS, BLOCK * 2 * BLOCK), F32),
                          compiler_params=_cp())(rel_t, onehot)


def _rowsum8(a):
    def body(a_ref, o_ref):
        o_ref[...] = jnp.sum(a_ref[...], axis=0, keepdims=True)

    return pl.pallas_call(body, name="rowsum8", out_shape=_sds((1, a.shape[1]), F32), compiler_params=_cp())(a)


def _local_step(x, mod, target, w, gains, rel_bias, place=None):
    nb, seq, d = x.shape
    t = nb * seq
    dist = place is not None
    core = place[0:1] if dist else None
    x0 = x.reshape(t, d)
    tgt = target.reshape(t, d)
    md = [mod[:, i:i + 1, :] for i in range(N_MOD)]
    sh1, sc1, gt1, sh2, sc2, gt2, sh3, sc3, gt3 = md
    g1, g2, g3 = gains["g_ffn1"], gains["g_mix"], gains["g_ffn2"]
    ones_g = _group_ones()

    def partial_sums(grads, lands):
        return [_add_halves(core, g, l) for g, l in zip(grads, lands)]

    def chip_sums(parts, lands):
        return [_sum_chips(place, p, l) for p, l in zip(parts, lands)]

    h1 = _modnorm(x0, g1, sc1, sh1, seq)
    res = _ffn_up(h1, w["gu1"], carry=_ag_weights([w["win"], w["wout"]]) if dist else None)
    a1, u1, s1 = res[:3]
    w_in, w_out = res[3:] if dist else (w["win"], w["wout"])
    w_out2 = w_out.reshape(2 * D_GRP, d)
    f1, x1 = _ffn_down(s1, w["d1"], x0, gt1, seq, 0.5)

    h2 = _modnorm(x1, g2, sc2, sh2, seq)
    res = _qkv_proj(h2, w_in, carry=_ag_weights([w["gu2"], w["d2"]]) if dist else None)
    qkv6 = res[0]
    wgu2, wd2 = res[1:] if dist else (w["gu2"], w["d2"])
    qkv6b = qkv6.reshape(6, nb, seq, D_GRP)
    o_sb, on_sb = _sb_fwd(qkv6b, gains["g_sb_out"], nb, seq)
    onehot = _bucket_onehot()
    bias = _dil_bias(rel_bias.T, onehot).reshape(len(DIL_CONFIGS), N_HEADS, BLOCK, 2 * BLOCK)
    o_cs, l_cs, qkv_rs = [], [], []
    for ci, (_, dil) in enumerate(DIL_CONFIGS):
        sub = seq // dil
        qr = qkv6.reshape(6, nb, sub, dil * D_GRP)
        o_c, l_c = _dil_fwd(qr, bias[ci], nb, sub, dil)
        qkv_rs.append(qr)
        o_cs.append(o_c.reshape(t, D_GRP))
        l_cs.append(l_c.reshape(t, D_GRP))
    o_dil, on_dil = _dil_comb(o_cs, l_cs, gains["g_dil_out"], ones_g)
    tmix, x2 = _mix_out(on_sb.reshape(t, D_GRP), on_dil, w_out2, x1, gt2, seq)

    h3 = _modnorm(x2, g3, sc3, sh3, seq)
    a3, u3, s3 = _ffn_up(h3, wgu2)
    f3, x3 = _ffn_down(s3, wd2, x2, gt3, seq, 0.5)

    dx3, dg_final, loss = _final_loss(x3, gains["g_final"], tgt)

    da3, du3, df3, dgt3 = _ffn_bwd_ds(dx3, gt3, f3, wd2, a3, u3, seq, 0.5)
    grads2 = _ffn_bwd_w(h3, da3, du3, s3, df3)
    res = _ffn_bwd_dh(da3, du3, wgu2, x2, g3, sc3, dx3, seq, carry=_rs_d2d(grads2) if dist else None)
    dx2, dsh3, dsc3, dg3 = res[:4]
    parts2 = partial_sums(grads2, res[4:]) if dist else None

    do_sb, do_dil, dgt2, dg_sb, dg_dil, dw_out = _mix_bwd_out(
        dx2, gt2, tmix, w_out2, o_sb.reshape(t, D_GRP), o_dil, on_sb.reshape(t, D_GRP), on_dil,
        gains["g_sb_out"], gains["g_dil_out"], ones_g, seq)
    dw_out = dw_out.reshape(N_CHIPS, 1, 2 * D_GRP // N_CHIPS, d)
    res = _sb_bwd(qkv6b, do_sb.reshape(nb, seq, D_GRP), nb, seq, carry=_rs_ici(parts2) if dist else None)
    dqkv6 = res[0]
    halves2 = chip_sums(parts2, res[1:]) if dist else None
    dcs = _dil_comb_bwd(do_dil, o_cs, l_cs, ones_g)
    dsum, a_tiles = [], []
    for ci, (_, dil) in enumerate(DIL_CONFIGS):
        sub = seq // dil
        do_c = dcs[ci].reshape(nb, sub, dil * D_GRP)
        dd_c = dcs[3 + ci].reshape(nb, sub, dil * D_GRP)
        res = _dil_bwd(qkv_rs[ci], bias[ci], do_c, dd_c, nb, sub, dil,
                       carry=_rs_final(halves2) if dist and ci == 0 else None)
        if dist and ci == 0:
            grads2 = res[2:]
        dsum.append(res[0].reshape(3, t, D_GRP))
        a_tiles.append(res[1].reshape(N_HEADS, BLOCK * 2 * BLOCK))
    dqkv6 = _dqkv_dil_sum(dsum[0], dsum[1], dsum[2], dqkv6.reshape(6, t, D_GRP))
    drel = _relbias_grad(jnp.stack(a_tiles), onehot)
    dx1, dsh2, dsc2, dg2 = _mix_bwd_dh(dqkv6, w_in, x1, g2, sc2, dx2, seq)

    da1, du1, df1, dgt1 = _ffn_bwd_ds(dx1, gt1, f1, w["d1"], a1, u1, seq, 0.5)
    grads1 = _ffn_bwd_w(h1, da1, du1, s1, df1)
    res = _dw_in(h2, dqkv6, carry=_rs_d2d(grads1) if dist else None)
    grads_m = [res[0], dw_out]
    parts1 = partial_sums(grads1, res[1:]) if dist else None
    res = _ffn_bwd_dh(da1, du1, w["gu1"], x0, g1, sc1, dx1, seq,
                      carry=_join([_rs_ici(parts1), _rs_d2d(grads_m)]) if dist else None)
    dx0, dsh1, dsc1, dg1 = res[:4]
    if dist:
        halves1 = chip_sums(parts1, res[4:6])
        parts_m = partial_sums(grads_m, res[6:8])
        res = _alone("rs_tail", _join([_rs_final(halves1), _rs_ici(parts_m)]))
        grads1 = res[:2]
        grads_m = _alone("rs_last", _rs_final(chip_sums(parts_m, res[2:4])))

    dmod = jnp.concatenate([dsh1, dsc1, dgt1, dsh2, dsc2, dgt2, dsh3, dsc3, dgt3], axis=1)
    return dict(grad_x=dx0.reshape(nb, seq, d), loss=loss[0, 0], dmod=dmod.reshape(nb, N_MOD * d),
                dgu1=grads1[0], dwd1=grads1[1], dgu2=grads2[0], dwd2=grads2[1], dwin=grads_m[0], dwout=grads_m[1],
                dg_ffn1=dg1, dg_mix=dg2, dg_ffn2=dg3, dg_final=dg_final, dg_sb=dg_sb, dg_dil=dg_dil,
                drel=drel.T)


_SMALL_ORDER = (("b_ada", N_MOD * 1024), ("g_ffn1", 1024), ("g_mix", 1024), ("g_ffn2", 1024), ("g_final", 1024),
                ("g_sb_out", D_GRP), ("g_dil_out", D_GRP), ("rel_bias", N_BUCKETS * N_HEADS))


def _pack_small(parts, extra=None):
    flat = [parts[name].reshape(-1).astype(F32) for name, _ in _SMALL_ORDER]
    used = sum(sz for _, sz in _SMALL_ORDER)
    pad = SMALL_ROWS * 128 - used
    tail = jnp.zeros((pad,), F32)
    if extra is not None:
        tail = tail.at[0].set(extra)
    return jnp.concatenate(flat + [tail]).reshape(SMALL_ROWS, 128)


def _unpack_small(packed, shapes):
    flat = packed.reshape(-1)
    out, off = {}, 0
    for name, sz in _SMALL_ORDER:
        out[name] = flat[off:off + sz].reshape(shapes[name])
        off += sz
    return out, flat[off]


def kernel(x, c, w_ada, b_ada, g_ffn1, w1_gate, w1_up, w1_down, g_mix, w_in, g_sb_out, g_dil_out, w_out, rel_bias, g_ffn2, w2_gate, w2_up, w2_down, g_final, loss_target, m_w_ada, m_b_ada, m_g_ffn1, m_w1_gate, m_w1_up, m_w1_down, m_g_mix, m_w_in, m_g_sb_out, m_g_dil_out, m_w_out, m_rel_bias, m_g_ffn2, m_w2_gate, m_w2_up, m_w2_down, m_g_final, v_w_ada, v_b_ada, v_g_ffn1, v_w1_gate, v_w1_up, v_w1_down, v_g_mix, v_w_in, v_g_sb_out, v_g_dil_out, v_w_out, v_rel_bias, v_g_ffn2, v_w2_gate, v_w2_up, v_w2_down, v_g_final):
    nb, seq, d = x.shape
    xi, yi, ci = lax.axis_index("x"), lax.axis_index("y"), lax.axis_index("c")
    chip = 2 * xi + yi
    ada_cols = w_ada.shape[-1]

    c_pad = jnp.zeros((8, d), F32).at[:nb].set(c)
    b_shard = lax.dynamic_slice(b_ada, (0, chip * ada_cols), (1, ada_cols))
    c_all, mod_blk = _ada_fwd(c_pad, w_ada[0], b_shard)
    mod = jnp.transpose(mod_blk[:, :nb, :], (1, 0, 2)).reshape(nb, N_MOD, d)

    shards = dict(gu1=jnp.stack([w1_gate[0], w1_up[0]]), d1=w1_down, win=w_in, wout=w_out,
                  gu2=jnp.stack([w2_gate[0], w2_up[0]]), d2=w2_down)
    bufs = {k: lax.dynamic_update_slice(lax.empty((N_CHIPS,) + s.shape, BF), s.astype(BF)[None], (chip, 0, 0, 0))
            for k, s in shards.items()}
    bufs["gu1"], bufs["d1"] = _alone("ag_first", _ag_weights([bufs["gu1"], bufs["d1"]]))

    gains = dict(g_ffn1=g_ffn1, g_mix=g_mix, g_ffn2=g_ffn2, g_final=g_final.reshape(1, d),
                 g_sb_out=g_sb_out.reshape(1, D_GRP), g_dil_out=g_dil_out.reshape(1, D_GRP))
    place = jnp.stack([ci, chip]).astype(jnp.int32)
    r = _local_step(x, mod, loss_target, bufs, gains, rel_bias, place)
    gu1, gu2, gd1, gd2, gwin, gwout = r["dgu1"], r["dgu2"], r["dwd1"], r["dwd2"], r["dwin"], r["dwout"]

    dmod = r["dmod"]
    dmod_pad = jnp.zeros((8, N_MOD * d), F32).at[:nb].set(dmod)
    dmod_blk = jnp.transpose(dmod_pad.reshape(8, N_CHIPS, ada_cols), (1, 0, 2))
    small_parts = dict(b_ada=_rowsum8(dmod_pad), g_ffn1=r["dg_ffn1"], g_mix=r["dg_mix"], g_ffn2=r["dg_ffn2"],
                       g_final=r["dg_final"], g_sb_out=r["dg_sb"], g_dil_out=r["dg_dil"], rel_bias=r["drel"])
    small_sum, g_wada = _small_sync(_pack_small(small_parts, r["loss"]), dmod_blk, c_all)

    small_w = dict(b_ada=b_ada, g_ffn1=g_ffn1, g_mix=g_mix, g_ffn2=g_ffn2, g_final=g_final,
                   g_sb_out=g_sb_out, g_dil_out=g_dil_out, rel_bias=rel_bias)
    small_m = dict(b_ada=m_b_ada, g_ffn1=m_g_ffn1, g_mix=m_g_mix, g_ffn2=m_g_ffn2, g_final=m_g_final,
                   g_sb_out=m_g_sb_out, g_dil_out=m_g_dil_out, rel_bias=m_rel_bias)
    small_v = dict(b_ada=v_b_ada, g_ffn1=v_g_ffn1, g_mix=v_g_mix, g_ffn2=v_g_ffn2, g_final=v_g_final,
                   g_sb_out=v_g_sb_out, g_dil_out=v_g_dil_out, rel_bias=v_rel_bias)
    shapes = {k: v.shape for k, v in small_w.items()}
    sg, sd, sm, sv = _adamw(_pack_small(small_w), small_sum.reshape(1, SMALL_ROWS, 128), 0,
                            _pack_small(small_m), _pack_small(small_v))
    sg, loss = _unpack_small(sg, shapes)
    sd, _ = _unpack_small(sd, shapes)
    sm, _ = _unpack_small(sm, shapes)
    sv, _ = _unpack_small(sv, shapes)

    big = {}

    def upd(name, w, g_arr, sel, m, v):
        shape = w.shape
        res = _adamw(w.reshape(shape[-2:]), g_arr, sel, m.reshape(shape[-2:]), v.reshape(shape[-2:]))
        big[name] = [a.reshape(shape) for a in res]

    upd("w_ada", w_ada, g_wada.reshape(1, d, ada_cols), 0, m_w_ada, v_w_ada)
    upd("w1_gate", w1_gate, gu1, 0, m_w1_gate, v_w1_gate)
    upd("w1_up", w1_up, gu1, 1, m_w1_up, v_w1_up)
    upd("w1_down", w1_down, gd1, 0, m_w1_down, v_w1_down)
    upd("w_in", w_in, gwin, 0, m_w_in, v_w_in)
    upd("w_out", w_out, gwout, 0, m_w_out, v_w_out)
    upd("w2_gate", w2_gate, gu2, 0, m_w2_gate, v_w2_gate)
    upd("w2_up", w2_up, gu2, 1, m_w2_up, v_w2_up)
    upd("w2_down", w2_down, gd2, 0, m_w2_down, v_w2_down)

    names = ["w_ada", "b_ada", "g_ffn1", "w1_gate", "w1_up", "w1_down", "g_mix", "w_in", "g_sb_out", "g_dil_out",
             "w_out", "rel_bias", "g_ffn2", "w2_gate", "w2_up", "w2_down", "g_final"]
    outs = [loss, r["grad_x"]]
    for k, small in enumerate((sg, sd, sm, sv)):
        for name in names:
            outs.append(big[name][k] if name in big else small[name])
    return tuple(outs)
```

```python
import functools
import math

import numpy as np
import jax
import jax.numpy as jnp
from jax import lax
from jax.experimental import pallas as pl
from jax.experimental.pallas import tpu as pltpu

F32 = jnp.float32
BF = jnp.bfloat16
MESH = pl.DeviceIdType.MESH

HEAD_DIM = 64
N_HEADS = 8
D_GRP = N_HEADS * HEAD_DIM
DIL_CONFIGS = ((128, 1), (512, 4), (2048, 16))
N_STEPS = 128
BLOCK = 128
N_BUCKETS = 32
MAX_DISTANCE = 2048
N_MOD = 9
EPS = 1e-6
NEG_INF = -1e30
SCALE = HEAD_DIM ** -0.5

ADAM_LR = 0.001
ADAM_B1 = 0.9
ADAM_B2 = 0.999
ADAM_EPS = 1e-08
ADAM_WD = 0.01
ADAM_STEP = 10

N_CHIPS = 4
N_DEV = 8
VMEM_LIMIT = 56 * 1024 * 1024
TM = 512
TQ = 256
KB = 256
SMALL_ROWS = 120


def _cp(n_axes=0, **kw):
    sem = ("arbitrary",) * n_axes if n_axes else None
    return pltpu.CompilerParams(dimension_semantics=sem, vmem_limit_bytes=VMEM_LIMIT, **kw)


def _nn(a, b):
    return jnp.dot(a, b, preferred_element_type=F32)


def _nt(a, b):
    return lax.dot_general(a, b, (((1,), (1,)), ((), ())), preferred_element_type=F32)


def _tn(a, b):
    return lax.dot_general(a, b, (((0,), (0,)), ((), ())), preferred_element_type=F32)


def _nn2(x, m):
    hi = x.astype(BF)
    lo = (x - hi.astype(F32)).astype(BF)
    return _nn(hi, m) + _nn(lo, m)


def _softplus(z):
    return jnp.maximum(z, 0.0) + jnp.log1p(jnp.exp(-jnp.abs(z)))


def _sds(shape, dtype):
    return jax.ShapeDtypeStruct(shape, dtype)


def _whole(a):
    nd = a.ndim
    return pl.BlockSpec(a.shape, lambda *_: (0,) * nd, pipeline_mode=pl.Buffered(1))


def _modnorm(x, g, sc, sh, seq):
    t, d = x.shape
    per = seq // TM

    def body(x_ref, g_ref, sc_ref, sh_ref, h_ref):
        xv = x_ref[...]
        r = lax.rsqrt(jnp.mean(xv * xv, axis=-1, keepdims=True) + EPS)
        h_ref[...] = (((xv * r) * g_ref[...]) * (1.0 + sc_ref[...]) + sh_ref[...]).astype(BF)

    return pl.pallas_call(
        body, name="modnorm", grid=(t // TM,),
        in_specs=[pl.BlockSpec((TM, d), lambda m: (m, 0)),
                  pl.BlockSpec((1, d), lambda m: (0, 0)),
                  pl.BlockSpec((None, 1, d), lambda m: (m // per, 0, 0)),
                  pl.BlockSpec((None, 1, d), lambda m: (m // per, 0, 0))],
        out_specs=pl.BlockSpec((TM, d), lambda m: (m, 0)),
        out_shape=_sds((t, d), BF), compiler_params=_cp(1))(x, g, sc, sh)


def _modnorm_bwd_tile(dh, xv, gv, scv, dxo):
    r = lax.rsqrt(jnp.mean(xv * xv, axis=-1, keepdims=True) + EPS)
    n = xv * r
    ng = n * gv
    dsh = jnp.sum(dh, axis=0, keepdims=True)
    dsc = jnp.sum(dh * ng, axis=0, keepdims=True)
    dy = dh * (1.0 + scv)
    dg = jnp.sum(dy * n, axis=0, keepdims=True)
    dn = dy * gv
    dx = dxo + r * (dn - n * jnp.mean(dn * n, axis=-1, keepdims=True))
    return dx, dsh, dsc, dg


def _acc_rows(ref, val, first):
    @pl.when(first)
    def _():
        ref[...] = val

    @pl.when(jnp.logical_not(first))
    def _():
        ref[...] += val


def _ffn_up(h, wgu, carry=None):
    t, d = h.shape
    fs = wgu.shape[-1]

    def body(h_ref, w_ref, p_ref, q_ref, s_ref):
        hv = h_ref[...]
        a = _nn(hv, w_ref[0])
        u = _nn(hv, w_ref[1])
        sig = jax.nn.sigmoid(a)
        q = a * sig
        p_ref[...] = (u * (sig * (1.0 + a * (1.0 - sig)))).astype(BF)
        q_ref[...] = q.astype(BF)
        s_ref[...] = (q * u).astype(BF)

    blk = pl.BlockSpec((None, TM, fs), lambda j, m: (j, m, 0))
    return _call(
        body, "ffn_up", (N_CHIPS, t // TM),
        [pl.BlockSpec((TM, d), lambda j, m: (m, 0)),
         pl.BlockSpec((None, 2, d, fs), lambda j, m: (j, 0, 0, 0))],
        [blk, blk, blk],
        [_sds((N_CHIPS, t, fs), BF)] * 3,
        (h, wgu), carry=carry)


def _ffn_down(s, wd, x, gt, seq, coef):
    _, t, fs = s.shape
    d = x.shape[-1]
    per = seq // TM

    def body(s_ref, w_ref, x_ref, gt_ref, f_ref, xo_ref):
        f = _nn(s_ref[0], w_ref[0, 0])
        for j in range(1, N_CHIPS):
            f = f + _nn(s_ref[j], w_ref[j, 0])
        f_ref[...] = f
        xo_ref[...] = x_ref[...] + (coef * gt_ref[...]) * f

    row = pl.BlockSpec((TM, d), lambda m: (m, 0))
    return pl.pallas_call(
        body, name="ffn_down", grid=(t // TM,),
        in_specs=[pl.BlockSpec((N_CHIPS, TM, fs), lambda m: (0, m, 0)),
                  _whole(wd), row,
                  pl.BlockSpec((None, 1, d), lambda m: (m // per, 0, 0))],
        out_specs=[row, row],
        out_shape=[_sds((t, d), F32), _sds((t, d), F32)],
        compiler_params=_cp(1))(s, wd, x, gt)


def _ffn_bwd_ds(dxo, gt, f, wd, p, q, seq, coef, carry=None):
    t, d = dxo.shape
    fs = p.shape[-1]
    per = seq // TM
    nb = t // seq

    def body(dxo_ref, gt_ref, f_ref, w_ref, p_ref, q_ref, da_ref, du_ref, df_ref, dgt_ref):
        m = pl.program_id(0)
        dxv = dxo_ref[...]
        df = ((coef * gt_ref[...]) * dxv).astype(BF)
        df_ref[...] = df
        _acc_rows(dgt_ref, coef * jnp.sum(dxv * f_ref[...], axis=0, keepdims=True), m % per == 0)
        for j in range(N_CHIPS):
            ds = _nt(df, w_ref[j, 0])
            da_ref[j] = (ds * p_ref[j].astype(F32)).astype(BF)
            du_ref[j] = (ds * q_ref[j].astype(F32)).astype(BF)

    row = pl.BlockSpec((TM, d), lambda m: (m, 0))
    blk = pl.BlockSpec((N_CHIPS, TM, fs), lambda m: (0, m, 0))
    ex = pl.BlockSpec((None, 1, d), lambda m: (m // per, 0, 0))
    return _call(
        body, "ffn_bwd_ds", (t // TM,),
        [row, ex, row, _whole(wd), blk, blk],
        [blk, blk, row, ex],
        [_sds((N_CHIPS, t, fs), BF), _sds((N_CHIPS, t, fs), BF), _sds((t, d), BF), _sds((nb, 1, d), F32)],
        (dxo, gt, f, wd, p, q), carry=carry)


def _ffn_bwd_w(h, da, du, s, df):
    t, d = h.shape
    fs = da.shape[-1]

    def body(h_ref, da_ref, du_ref, s_ref, df_ref, dgu_ref, dwd_ref):
        kt = pl.program_id(1)
        hv = h_ref[...]
        pg = _tn(hv, da_ref[...])
        pu = _tn(hv, du_ref[...])
        pd = _tn(s_ref[...], df_ref[...])

        @pl.when(kt == 0)
        def _():
            dgu_ref[0] = pg
            dgu_ref[1] = pu
            dwd_ref[...] = pd

        @pl.when(kt != 0)
        def _():
            dgu_ref[0] += pg
            dgu_ref[1] += pu
            dwd_ref[...] += pd

    row = pl.BlockSpec((TM, d), lambda j, kt: (kt, 0))
    blk = pl.BlockSpec((None, TM, fs), lambda j, kt: (j, kt, 0))
    return pl.pallas_call(
        body, name="ffn_bwd_w", grid=(N_CHIPS, t // TM),
        in_specs=[row, blk, blk, blk, row],
        out_specs=[pl.BlockSpec((None, 2, d, fs), lambda j, kt: (j, 0, 0, 0)),
                   pl.BlockSpec((None, None, fs, d), lambda j, kt: (j, 0, 0, 0))],
        out_shape=[_sds((N_CHIPS, 2, d, fs), F32), _sds((N_CHIPS, 1, fs, d), F32)],
        compiler_params=_cp(2))(h, da, du, s, df)


def _ffn_bwd_dh(da, du, wgu, x, g, sc, dxo, seq, carry=None):
    _, t, fs = da.shape
    d = x.shape[-1]
    per = seq // TM
    nb = t // seq

    def body(da_ref, du_ref, w_ref, x_ref, g_ref, sc_ref, dxo_ref, dx_ref, dsh_ref, dsc_ref, dg_ref):
        m = pl.program_id(0)
        dh = _nt(da_ref[0], w_ref[0, 0]) + _nt(du_ref[0], w_ref[0, 1])
        for j in range(1, N_CHIPS):
            dh = dh + _nt(da_ref[j], w_ref[j, 0]) + _nt(du_ref[j], w_ref[j, 1])
        dx, dsh, dsc, dg = _modnorm_bwd_tile(dh, x_ref[...], g_ref[...], sc_ref[...], dxo_ref[...])
        dx_ref[...] = dx
        _acc_rows(dsh_ref, dsh, m % per == 0)
        _acc_rows(dsc_ref, dsc, m % per == 0)
        _acc_rows(dg_ref, dg, m == 0)

    row = pl.BlockSpec((TM, d), lambda m: (m, 0))
    blk = pl.BlockSpec((N_CHIPS, TM, fs), lambda m: (0, m, 0))
    ex = pl.BlockSpec((None, 1, d), lambda m: (m // per, 0, 0))
    vec = pl.BlockSpec((1, d), lambda m: (0, 0))
    return _call(
        body, "ffn_bwd_dh", (t // TM,),
        [blk, blk, _whole(wgu), row, vec, ex, row],
        [row, ex, ex, vec],
        [_sds((t, d), F32), _sds((nb, 1, d), F32), _sds((nb, 1, d), F32), _sds((1, d), F32)],
        (da, du, wgu, x, g, sc, dxo), carry=carry)


def _qkv_proj(h, w_in, carry=None):
    t, d = h.shape
    wc = w_in.shape[-1]

    def body(h_ref, w_ref, o_ref):
        hv = h_ref[...]
        for j in range(N_CHIPS):
            r = _nn(hv, w_ref[j, 0]).astype(BF)
            for a, lc, off, width in _col_pieces(j, wc):
                o_ref[a, :, lc:lc + width] = r[:, off:off + width]

    return _call(
        body, "qkv_proj", (t // TM,),
        [pl.BlockSpec((TM, d), lambda m: (m, 0)), _whole(w_in)],
        [pl.BlockSpec((6, TM, D_GRP), lambda m: (0, m, 0))],
        [_sds((6, t, D_GRP), BF)], (h, w_in), carry=carry)


def _col_pieces(j, wc):
    out, off = [], 0
    while off < wc:
        a, lc = divmod(j * wc + off, D_GRP)
        width = min(D_GRP - lc, wc - off)
        out.append((a, lc, off, width))
        off += width
    return out


def _chip_cols(g6_ref, j, wc):
    return jnp.concatenate([g6_ref[a, :, lc:lc + width] for a, lc, _, width in _col_pieces(j, wc)], axis=1)


def _mix_out(on_sb, on_dil, w_out, x, gt, seq):
    t, d = x.shape
    per = seq // TM

    def body(a_ref, b_ref, w_ref, x_ref, gt_ref, t_ref, xo_ref):
        tv = _nn(a_ref[...], w_ref[0:D_GRP, :]) + _nn(b_ref[...], w_ref[D_GRP:2 * D_GRP, :])
        t_ref[...] = tv
        xo_ref[...] = x_ref[...] + gt_ref[...] * tv

    row = pl.BlockSpec((TM, d), lambda m: (m, 0))
    half = pl.BlockSpec((TM, D_GRP), lambda m: (m, 0))
    return pl.pallas_call(
        body, name="mix_out", grid=(t // TM,),
        in_specs=[half, half, pl.BlockSpec((2 * D_GRP, d), lambda m: (0, 0)), row,
                  pl.BlockSpec((None, 1, d), lambda m: (m // per, 0, 0))],
        out_specs=[row, row],
        out_shape=[_sds((t, d), F32), _sds((t, d), F32)],
        compiler_params=_cp(1))(on_sb, on_dil, w_out, x, gt)


def _sb_masks():
    lane = lax.broadcasted_iota(jnp.int32, (1, 2 * HEAD_DIM), 1)
    hm0 = lane < HEAD_DIM
    rel = lax.broadcasted_iota(jnp.int32, (TQ, KB), 0) - lax.broadcasted_iota(jnp.int32, (TQ, KB), 1)
    kr = lax.broadcasted_iota(jnp.int32, (KB, KB), 0)
    kc = lax.broadcasted_iota(jnp.int32, (KB, KB), 1)
    return hm0, rel, kr, kc


def _headnorm_pair(o, gv, hm0):
    o2 = o * o
    ms0 = jnp.sum(jnp.where(hm0, o2, 0.0), axis=-1, keepdims=True) * (1.0 / HEAD_DIM)
    ms1 = jnp.sum(jnp.where(hm0, 0.0, o2), axis=-1, keepdims=True) * (1.0 / HEAD_DIM)
    r = jnp.where(hm0, lax.rsqrt(ms0 + EPS), lax.rsqrt(ms1 + EPS))
    return (o * r) * gv


SB_DEAD = -104.0


def _alive(c_l):
    return (jnp.max(c_l) > SB_DEAD).astype(jnp.int32)


def _sb_fwd(qkv6, g_sb, nb, seq):
    nq = seq // TQ

    def body(q_ref, k_ref, v_ref, g_ref, o_ref, on_ref):
        qi = pl.program_id(2)
        hm0, rel, kr, kc = _sb_masks()
        upper = (kr > kc).astype(BF)
        qv = q_ref[...]
        outs = []
        for hh in range(2):
            hm = hm0 if hh == 0 else jnp.logical_not(hm0)
            qh = jnp.where(hm, qv, jnp.zeros_like(qv))

            def block(kj, c_l, causal, qh=qh):
                ks = pl.multiple_of(kj * KB, KB)
                z = _nt(qh, k_ref[pl.ds(ks, KB), :]) * SCALE
                sp = _softplus(z)
                ln = -sp if causal is None else jnp.where(causal, -sp, 0.0)
                suf = _nn2(ln, upper)
                w = jnp.exp((z - sp) + (suf + c_l))
                if causal is not None:
                    w = jnp.where(causal, w, 0.0)
                pv = _nn(w.astype(BF), v_ref[pl.ds(ks, KB), :])
                return pv, c_l + (suf[:, 0:1] + ln[:, 0:1])

            acc, c_l = block(qi, jnp.zeros((TQ, 1), F32), rel > 0)

            def cond(carry):
                it, alive, _, _ = carry
                return jnp.logical_and(it <= qi, alive > 0)

            def kbody(carry):
                it, _, c_l, acc = carry
                pv, c_l = block(qi - it, c_l, None)
                return it + 1, _alive(c_l), c_l, acc + pv

            _, _, _, acc = lax.while_loop(cond, kbody, (jnp.int32(1), _alive(c_l), c_l, acc))
            outs.append(acc)
        o = jnp.where(hm0, outs[0], outs[1])
        o_ref[...] = o
        on_ref[...] = _headnorm_pair(o, g_ref[...], hm0).astype(BF)

    w = 2 * HEAD_DIM
    full = lambda i: pl.BlockSpec((None, None, seq, w), lambda b, hp, q: (i, b, 0, hp))
    qblk = pl.BlockSpec((None, None, TQ, w), lambda b, hp, q: (0, b, q, hp))
    oblk = pl.BlockSpec((None, TQ, w), lambda b, hp, q: (b, q, hp))
    return pl.pallas_call(
        body, name="sb_fwd", grid=(nb, N_HEADS // 2, nq),
        in_specs=[qblk, full(1), full(2), pl.BlockSpec((1, w), lambda b, hp, q: (0, hp))],
        out_specs=[oblk, oblk],
        out_shape=[_sds((nb, seq, D_GRP), F32), _sds((nb, seq, D_GRP), BF)],
        compiler_params=_cp(3))(qkv6, qkv6, qkv6, g_sb)


def _sb_bwd(qkv6, do, nb, seq, carry=None):
    nq = seq // TQ
    nk = seq // KB

    def body(q_ref, k_ref, v_ref, do_ref, out_ref, dk_acc, dv_acc, car):
        qi = pl.program_id(2)
        hm0, rel, kr, kc = _sb_masks()
        upper = (kr > kc).astype(BF)
        lower = (kr < kc).astype(BF)
        ones = jnp.ones((KB, 2 * HEAD_DIM), BF)

        @pl.when(qi == 0)
        def _():
            dk_acc[...] = jnp.zeros_like(dk_acc)
            dv_acc[...] = jnp.zeros_like(dv_acc)

        qv = q_ref[...]
        dov = do_ref[...]
        dqs = []
        for hh in range(2):
            hm = hm0 if hh == 0 else jnp.logical_not(hm0)
            qh = jnp.where(hm, qv, jnp.zeros_like(qv))
            doh = jnp.where(hm, dov, 0.0).astype(BF)

            def logits(kj, causal, qh=qh):
                ks = pl.multiple_of(kj * KB, KB)
                kb = k_ref[pl.ds(ks, KB), :]
                z = _nt(qh, kb) * SCALE
                sp = _softplus(z)
                ln = -sp if causal is None else jnp.where(causal, -sp, 0.0)
                return ks, kb, z - sp, ln

            car[qi] = jnp.zeros((TQ, 1), F32)
            c_l = _nn2(logits(qi, rel > 0)[3], ones)[:, 0:1]

            def acond(carry):
                it, alive, _ = carry
                return jnp.logical_and(it <= qi, alive > 0)

            def abody(carry):
                it, _, c_l = carry
                car[qi - it] = c_l
                c_l = c_l + _nn2(logits(qi - it, None)[3], ones)[:, 0:1]
                return it + 1, _alive(c_l), c_l

            n_used, _, _ = lax.while_loop(acond, abody, (jnp.int32(1), _alive(c_l), c_l))

            def grads(kj, causal, c_g, dq, doh=doh, qh=qh):
                ks, kb, lsz, ln = logits(kj, causal)
                vb = v_ref[pl.ds(ks, KB), :]
                w = jnp.exp(lsz + (_nn2(ln, upper) + car[kj]))
                if causal is not None:
                    w = jnp.where(causal, w, 0.0)
                g = w * _nt(doh, vb)
                pre = _nn2(g, lower)
                sig = jnp.exp(lsz)
                dz = g * (1.0 - sig) - sig * (pre + c_g)
                if causal is not None:
                    dz = jnp.where(causal, dz, 0.0)
                dzb = (dz * SCALE).astype(BF)
                dk_acc[pl.ds(ks, KB), :] += _tn(dzb, qh)
                dv_acc[pl.ds(ks, KB), :] += _tn(w.astype(BF), doh)
                return c_g + (pre[:, KB - 1:KB] + g[:, KB - 1:KB]), dq + _nn(dzb, kb)

            c_g, dq = lax.fori_loop(qi - n_used + 1, qi, lambda kj, cr: grads(kj, None, *cr),
                                    (jnp.zeros((TQ, 1), F32), jnp.zeros((TQ, 2 * HEAD_DIM), F32)))
            _, dq = grads(qi, rel > 0, c_g, dq)
            dqs.append(dq)
        dq = jnp.where(hm0, dqs[0], dqs[1])
        out_ref[0, pl.ds(pl.multiple_of(qi * TQ, TQ), TQ), :] = dq.astype(BF)

        @pl.when(qi == nq - 1)
        def _():
            out_ref[1] = dk_acc[...].astype(BF)
            out_ref[2] = dv_acc[...].astype(BF)

    w = 2 * HEAD_DIM
    full = lambda i: pl.BlockSpec((None, None, seq, w), lambda b, hp, q: (i, b, 0, hp))
    qblk = pl.BlockSpec((None, None, TQ, w), lambda b, hp, q: (0, b, q, hp))
    oblk = pl.BlockSpec((None, TQ, w), lambda b, hp, q: (b, q, hp))
    return _call(
        body, "sb_bwd", (nb, N_HEADS // 2, nq),
        [qblk, full(1), full(2), oblk],
        [pl.BlockSpec((3, None, seq, w), lambda b, hp, q: (0, b, 0, hp))],
        [_sds((6, nb, seq, D_GRP), BF)], (qkv6, qkv6, qkv6, do),
        scratch=[pltpu.VMEM((seq, w), F32), pltpu.VMEM((seq, w), F32), pltpu.VMEM((nk, TQ, 1), F32)],
        carry=carry)


def _t5_bucket(n):
    max_exact = N_BUCKETS // 2
    nf = np.maximum(n, 1).astype(np.float32)
    large = max_exact + (np.log(nf / max_exact) / math.log(MAX_DISTANCE / max_exact)
                         * (N_BUCKETS - max_exact)).astype(np.int32)
    large = np.minimum(large, N_BUCKETS - 1)
    return np.where(n < max_exact, n, large).astype(np.int32)


def _bucket_map(dilation):
    step = BLOCK + np.arange(BLOCK)[:, None] - np.arange(2 * BLOCK)[None, :]
    return _t5_bucket(np.clip(step, 0, N_STEPS) * dilation)


def _dil_masks():
    lane = lax.broadcasted_iota(jnp.int32, (1, 2 * HEAD_DIM), 1)
    hm0 = lane < HEAD_DIM
    iq = lax.broadcasted_iota(jnp.int32, (BLOCK, BLOCK), 0)
    ik = lax.broadcasted_iota(jnp.int32, (BLOCK, BLOCK), 1)
    return hm0, ik <= iq, ik >= iq


def _dil_probs(qh, kc, kp, b_ref, hh, valid_c, valid_p):
    zc = _nt(qh, kc) * SCALE + b_ref[hh, :, BLOCK:2 * BLOCK]
    zp = _nt(qh, kp) * SCALE + b_ref[hh, :, 0:BLOCK]
    zc = jnp.where(valid_c, zc, NEG_INF)
    zp = jnp.where(valid_p, zp, NEG_INF)
    m = jnp.maximum(jnp.max(zc, axis=-1, keepdims=True), jnp.max(zp, axis=-1, keepdims=True))
    ec = jnp.exp(zc - m)
    ep = jnp.exp(zp - m)
    den = jnp.sum(ec, axis=-1, keepdims=True) + jnp.sum(ep, axis=-1, keepdims=True)
    return ec, ep, den, m


def _dil_fwd(qkv6r, bias, nb, sub_len, dilation):
    n_blk = sub_len // BLOCK
    w = 2 * HEAD_DIM

    def body(q_ref, k_ref, v_ref, b_ref, o_ref, l_ref):
        hm0, valid_c, valid_p0 = _dil_masks()

        def nbody(n, carry):
            rs = pl.multiple_of(n * BLOCK, BLOCK)
            ps = pl.multiple_of(jnp.maximum(n - 1, 0) * BLOCK, BLOCK)
            qv = q_ref[pl.ds(rs, BLOCK), :]
            kc = k_ref[pl.ds(rs, BLOCK), :]
            kp = k_ref[pl.ds(ps, BLOCK), :]
            vc = v_ref[pl.ds(rs, BLOCK), :]
            vp = v_ref[pl.ds(ps, BLOCK), :]
            valid_p = jnp.logical_and(valid_p0, n > 0)
            os, ls = [], []
            for hh in range(2):
                hm = hm0 if hh == 0 else jnp.logical_not(hm0)
                qh = jnp.where(hm, qv, jnp.zeros_like(qv))
                ec, ep, den, m = _dil_probs(qh, kc, kp, b_ref, hh, valid_c, valid_p)
                os.append((_nn(ec.astype(BF), vc) + _nn(ep.astype(BF), vp)) / den)
                ls.append(m + jnp.log(den))
            o_ref[pl.ds(rs, BLOCK), :] = jnp.where(hm0, os[0], os[1])
            l_ref[pl.ds(rs, BLOCK), :] = jnp.where(hm0, ls[0], ls[1])
            return carry

        lax.fori_loop(0, n_blk, nbody, 0)

    seqblk = lambda i: pl.BlockSpec((None, None, sub_len, w), lambda b, g: (i, b, 0, g))
    oblk = pl.BlockSpec((None, sub_len, w), lambda b, g: (b, 0, g))
    shp = _sds((nb, sub_len, dilation * D_GRP), F32)
    return pl.pallas_call(
        body, name="dil_fwd_%d" % dilation, grid=(nb, dilation * (N_HEADS // 2)),
        in_specs=[seqblk(3), seqblk(4), seqblk(5),
                  pl.BlockSpec((2, BLOCK, 2 * BLOCK), lambda b, g: (g % (N_HEADS // 2), 0, 0))],
        out_specs=[oblk, oblk], out_shape=[shp, shp],
        compiler_params=_cp(2))(qkv6r, qkv6r, qkv6r, bias)


def _dil_bwd(qkv6r, bias, do_c, dd_c, nb, sub_len, dilation, carry=None):
    n_blk = sub_len // BLOCK
    w = 2 * HEAD_DIM
    hp_n = N_HEADS // 2

    def body(q_ref, k_ref, v_ref, b_ref, do_ref, dd_ref, out_ref, a_ref, dk_acc, dv_acc):
        hm0, valid_c, valid_p0 = _dil_masks()
        first = jnp.logical_and(pl.program_id(1) == 0, pl.program_id(2) == 0)

        @pl.when(first)
        def _():
            a_ref[...] = jnp.zeros_like(a_ref)

        dk_acc[...] = jnp.zeros_like(dk_acc)
        dv_acc[...] = jnp.zeros_like(dv_acc)

        def nbody(n, carry):
            rs = pl.multiple_of(n * BLOCK, BLOCK)
            ps = pl.multiple_of(jnp.maximum(n - 1, 0) * BLOCK, BLOCK)
            qv = q_ref[pl.ds(rs, BLOCK), :]
            kc = k_ref[pl.ds(rs, BLOCK), :]
            kp = k_ref[pl.ds(ps, BLOCK), :]
            vc = v_ref[pl.ds(rs, BLOCK), :]
            vp = v_ref[pl.ds(ps, BLOCK), :]
            dov = do_ref[pl.ds(rs, BLOCK), :]
            ddv = dd_ref[pl.ds(rs, BLOCK), :]
            valid_p = jnp.logical_and(valid_p0, n > 0)
            dq = jnp.zeros((BLOCK, w), F32)
            for hh in range(2):
                hm = hm0 if hh == 0 else jnp.logical_not(hm0)
                qh = jnp.where(hm, qv, jnp.zeros_like(qv))
                doh = jnp.where(hm, dov, 0.0).astype(BF)
                ddh = jnp.sum(jnp.where(hm, ddv, 0.0), axis=-1, keepdims=True) * (1.0 / HEAD_DIM)
                ec, ep, den, _ = _dil_probs(qh, kc, kp, b_ref, hh, valid_c, valid_p)
                inv = 1.0 / den
                pc = ec * inv
                pp = ep * inv
                dzc = pc * (_nt(doh, vc) + ddh)
                dzp = pp * (_nt(doh, vp) + ddh)
                a_ref[hh, :, BLOCK:2 * BLOCK] += dzc
                a_ref[hh, :, 0:BLOCK] += dzp
                dzcb = (dzc * SCALE).astype(BF)
                dzpb = (dzp * SCALE).astype(BF)
                dq = jnp.where(hm, _nn(dzcb, kc) + _nn(dzpb, kp), dq)
                dk_acc[pl.ds(rs, BLOCK), :] += _tn(dzcb, qh)
                dk_acc[pl.ds(ps, BLOCK), :] += _tn(dzpb, qh)
                dv_acc[pl.ds(rs, BLOCK), :] += _tn(pc.astype(BF), doh)
                dv_acc[pl.ds(ps, BLOCK), :] += _tn(pp.astype(BF), doh)
            out_ref[0, pl.ds(rs, BLOCK), :] = dq
            return carry

        lax.fori_loop(0, n_blk, nbody, 0)
        out_ref[1] = dk_acc[...]
        out_ref[2] = dv_acc[...]

    col = lambda hp, b, r: r * hp_n + hp
    seqblk = lambda i: pl.BlockSpec((None, None, sub_len, w), lambda hp, b, r: (i, b, 0, col(hp, b, r)))
    oblk = pl.BlockSpec((None, sub_len, w), lambda hp, b, r: (b, 0, col(hp, b, r)))
    return _call(
        body, "dil_bwd_%d" % dilation, (hp_n, nb, dilation),
        [seqblk(3), seqblk(4), seqblk(5),
         pl.BlockSpec((2, BLOCK, 2 * BLOCK), lambda hp, b, r: (hp, 0, 0)), oblk, oblk],
        [pl.BlockSpec((3, None, sub_len, w), lambda hp, b, r: (0, b, 0, col(hp, b, r))),
         pl.BlockSpec((2, BLOCK, 2 * BLOCK), lambda hp, b, r: (hp, 0, 0))],
        [_sds((3, nb, sub_len, dilation * D_GRP), F32), _sds((N_HEADS, BLOCK, 2 * BLOCK), F32)],
        (qkv6r, qkv6r, qkv6r, bias, do_c, dd_c),
        scratch=[pltpu.VMEM((sub_len, w), F32), pltpu.VMEM((sub_len, w), F32)], carry=carry)


def _group_ones():
    idx = np.arange(D_GRP) // HEAD_DIM
    return jnp.asarray((idx[:, None] == idx[None, :]).astype(np.float32), dtype=BF)


def _dil_alphas(l1, l4, l16):
    mx = jnp.maximum(jnp.maximum(l1, l4), l16)
    e1 = jnp.exp(l1 - mx)
    e4 = jnp.exp(l4 - mx)
    e16 = jnp.exp(l16 - mx)
    den = e1 + e4 + e16
    return e1 / den, e4 / den, e16 / den


def _dil_comb(os, ls, g_dil, ones_g):
    t = os[0].shape[0]

    def body(o1, l1, o4, l4, o16, l16, g_ref, m_ref, o_ref, on_ref):
        a1, a4, a16 = _dil_alphas(l1[...], l4[...], l16[...])
        o = a1 * o1[...] + a4 * o4[...] + a16 * o16[...]
        o_ref[...] = o
        ms = _nn2(o * o, m_ref[...]) * (1.0 / HEAD_DIM)
        on_ref[...] = ((o * lax.rsqrt(ms + EPS)) * g_ref[...]).astype(BF)

    blk = pl.BlockSpec((TM, D_GRP), lambda m: (m, 0))
    return pl.pallas_call(
        body, name="dil_comb", grid=(t // TM,),
        in_specs=[blk] * 6 + [pl.BlockSpec((1, D_GRP), lambda m: (0, 0)),
                              pl.BlockSpec((D_GRP, D_GRP), lambda m: (0, 0))],
        out_specs=[blk, blk],
        out_shape=[_sds((t, D_GRP), F32), _sds((t, D_GRP), BF)],
        compiler_params=_cp(1))(os[0], ls[0], os[1], ls[1], os[2], ls[2], g_dil, ones_g)


def _dil_comb_bwd(do, os, ls, ones_g):
    t = do.shape[0]

    def body(do_ref, o1, l1, o4, l4, o16, l16, m_ref, d1, d4, d16, e1, e4, e16):
        dov = do_ref[...]
        a1, a4, a16 = _dil_alphas(l1[...], l4[...], l16[...])
        mv = m_ref[...]
        sbar = a1 * _nn2(dov * o1[...], mv) + a4 * _nn2(dov * o4[...], mv) + a16 * _nn2(dov * o16[...], mv)
        d1[...] = a1 * dov
        d4[...] = a4 * dov
        d16[...] = a16 * dov
        e1[...] = -a1 * sbar
        e4[...] = -a4 * sbar
        e16[...] = -a16 * sbar

    blk = pl.BlockSpec((TM, D_GRP), lambda m: (m, 0))
    shp = _sds((t, D_GRP), F32)
    return pl.pallas_call(
        body, name="dil_comb_bwd", grid=(t // TM,),
        in_specs=[blk] * 7 + [pl.BlockSpec((D_GRP, D_GRP), lambda m: (0, 0))],
        out_specs=[blk] * 6, out_shape=[shp] * 6,
        compiler_params=_cp(1))(do, os[0], ls[0], os[1], ls[1], os[2], ls[2], ones_g)


def _dqkv_dil_sum(d1, d4, d16, dqkv6):
    t = d1.shape[1]

    def body(a, b, c, alias, o_ref):
        del alias
        o_ref[...] = (a[...] + b[...] + c[...]).astype(BF)

    blk = pl.BlockSpec((3, TM, D_GRP), lambda m: (0, m, 0))
    return pl.pallas_call(
        body, name="dqkv_dil_sum", grid=(t // TM,),
        in_specs=[blk, blk, blk, pl.BlockSpec(memory_space=pl.ANY)],
        out_specs=pl.BlockSpec((3, TM, D_GRP), lambda m: (1, m, 0)),
        out_shape=_sds((6, t, D_GRP), BF), input_output_aliases={3: 0},
        compiler_params=_cp(1))(d1, d4, d16, dqkv6)


def _relbias_grad(a_all, onehot):
    def body(a_ref, oh_ref, o_ref):
        acc = jnp.zeros((N_HEADS, N_BUCKETS), F32)
        for c in range(len(DIL_CONFIGS)):
            av = a_ref[c]
            hi = av.astype(BF)
            lo = (av - hi.astype(F32)).astype(BF)
            acc = acc + _nt(hi, oh_ref[c]) + _nt(lo, oh_ref[c])
        o_ref[...] = acc

    return pl.pallas_call(body, name="relbias_grad", out_shape=_sds((N_HEADS, N_BUCKETS), F32),
                          compiler_params=_cp())(a_all, onehot)


def _headnorm_bwd(dn, o, gv, mv):
    ms = _nn2(o * o, mv) * (1.0 / HEAD_DIM)
    r = lax.rsqrt(ms + EPS)
    nrm = o * r
    dg = jnp.sum(dn * nrm, axis=0, keepdims=True)
    dnn = dn * gv
    do = r * (dnn - nrm * (_nn2(dnn * nrm, mv) * (1.0 / HEAD_DIM)))
    return do, dg


def _mix_bwd_out(dx, gt, tv, w_out, o_sb, o_dil, on_sb, on_dil, g_sb, g_dil, ones_g, seq):
    t, d = dx.shape
    per = seq // TM
    nb = t // seq

    def body(dx_ref, gt_ref, t_ref, w_ref, osb, odl, onsb, ondl, gsb, gdl, m_ref,
             dosb, dodl, dgt_ref, dgsb, dgdl, dw_ref):
        m = pl.program_id(0)
        dxv = dx_ref[...]
        dt = (gt_ref[...] * dxv).astype(BF)
        _acc_rows(dgt_ref, jnp.sum(dxv * t_ref[...], axis=0, keepdims=True), m % per == 0)
        mv = m_ref[...]
        don_sb = _nt(dt, w_ref[0:D_GRP, :])
        don_dl = _nt(dt, w_ref[D_GRP:2 * D_GRP, :])
        do1, dg1 = _headnorm_bwd(don_sb, osb[...], gsb[...], mv)
        do2, dg2 = _headnorm_bwd(don_dl, odl[...], gdl[...], mv)
        dosb[...] = do1
        dodl[...] = do2
        _acc_rows(dgsb, dg1, m == 0)
        _acc_rows(dgdl, dg2, m == 0)
        p1 = _tn(onsb[...], dt)
        p2 = _tn(ondl[...], dt)

        @pl.when(m == 0)
        def _():
            dw_ref[0:D_GRP, :] = p1
            dw_ref[D_GRP:2 * D_GRP, :] = p2

        @pl.when(m != 0)
        def _():
            dw_ref[0:D_GRP, :] += p1
            dw_ref[D_GRP:2 * D_GRP, :] += p2

    row = pl.BlockSpec((TM, d), lambda m: (m, 0))
    half = pl.BlockSpec((TM, D_GRP), lambda m: (m, 0))
    ex = pl.BlockSpec((None, 1, d), lambda m: (m // per, 0, 0))
    gvec = pl.BlockSpec((1, D_GRP), lambda m: (0, 0))
    wblk = pl.BlockSpec((2 * D_GRP, d), lambda m: (0, 0))
    return pl.pallas_call(
        body, name="mix_bwd_out", grid=(t // TM,),
        in_specs=[row, ex, row, wblk, half, half, half, half, gvec, gvec,
                  pl.BlockSpec((D_GRP, D_GRP), lambda m: (0, 0))],
        out_specs=[half, half, ex, gvec, gvec, wblk],
        out_shape=[_sds((t, D_GRP), F32), _sds((t, D_GRP), F32), _sds((nb, 1, d), F32),
                   _sds((1, D_GRP), F32), _sds((1, D_GRP), F32), _sds((2 * D_GRP, d), F32)],
        compiler_params=_cp(1))(dx, gt, tv, w_out, o_sb, o_dil, on_sb, on_dil, g_sb, g_dil, ones_g)


def _dw_in(h, dqkv6, carry=None):
    t, d = h.shape
    wc = 6 * D_GRP // N_CHIPS

    def body(h_ref, g_ref, o_ref):
        kt = pl.program_id(0)
        hv = h_ref[...]
        for j in range(N_CHIPS):
            p = _tn(hv, _chip_cols(g_ref, j, wc))

            @pl.when(kt == 0)
            def _(p=p, j=j):
                o_ref[j, 0] = p

            @pl.when(kt != 0)
            def _(p=p, j=j):
                o_ref[j, 0] += p

    return _call(
        body, "dw_in", (t // TM,),
        [pl.BlockSpec((TM, d), lambda kt: (kt, 0)), pl.BlockSpec((6, TM, D_GRP), lambda kt: (0, kt, 0))],
        [pl.BlockSpec((N_CHIPS, 1, d, wc), lambda kt: (0, 0, 0, 0))],
        [_sds((N_CHIPS, 1, d, wc), F32)], (h, dqkv6), carry=carry)


def _mix_bwd_dh(dqkv6, w_in, x, g, sc, dxo, seq, carry=None):
    _, t, _ = dqkv6.shape
    d = x.shape[-1]
    wc = w_in.shape[-1]
    per = seq // TM
    nb = t // seq

    def body(g6_ref, w_ref, x_ref, g_ref, sc_ref, dxo_ref, dx_ref, dsh_ref, dsc_ref, dg_ref):
        m = pl.program_id(0)
        dh = _nt(_chip_cols(g6_ref, 0, wc), w_ref[0, 0])
        for j in range(1, N_CHIPS):
            dh = dh + _nt(_chip_cols(g6_ref, j, wc), w_ref[j, 0])
        dx, dsh, dsc, dg = _modnorm_bwd_tile(dh, x_ref[...], g_ref[...], sc_ref[...], dxo_ref[...])
        dx_ref[...] = dx
        _acc_rows(dsh_ref, dsh, m % per == 0)
        _acc_rows(dsc_ref, dsc, m % per == 0)
        _acc_rows(dg_ref, dg, m == 0)

    row = pl.BlockSpec((TM, d), lambda m: (m, 0))
    ex = pl.BlockSpec((None, 1, d), lambda m: (m // per, 0, 0))
    vec = pl.BlockSpec((1, d), lambda m: (0, 0))
    return _call(
        body, "mix_bwd_dh", (t // TM,),
        [pl.BlockSpec((6, TM, D_GRP), lambda m: (0, m, 0)), _whole(w_in), row, vec, ex, row],
        [row, ex, ex, vec],
        [_sds((t, d), F32), _sds((nb, 1, d), F32), _sds((nb, 1, d), F32), _sds((1, d), F32)],
        (dqkv6, w_in, x, g, sc, dxo), carry=carry)


def _final_loss(x, g, target):
    t, d = x.shape
    steps = t // TM

    def body(x_ref, g_ref, t_ref, dx_ref, dg_ref, loss_ref, lacc):
        m = pl.program_id(0)
        xv = x_ref[...]
        gv = g_ref[...]
        r = lax.rsqrt(jnp.mean(xv * xv, axis=-1, keepdims=True) + EPS)
        n = xv * r
        err = n * gv - t_ref[...]
        dy = err * (1.0 / d)
        _acc_rows(dg_ref, jnp.sum(dy * n, axis=0, keepdims=True), m == 0)
        dn = dy * gv
        dx_ref[...] = r * (dn - n * jnp.mean(dn * n, axis=-1, keepdims=True))
        _acc_rows(lacc, jnp.sum(err * err, axis=0, keepdims=True), m == 0)

        @pl.when(m == steps - 1)
        def _():
            tot = jnp.sum(lacc[...], axis=-1, keepdims=True) * (0.5 / d)
            loss_ref[...] = jnp.broadcast_to(tot, (1, 128))

    row = pl.BlockSpec((TM, d), lambda m: (m, 0))
    vec = pl.BlockSpec((1, d), lambda m: (0, 0))
    return pl.pallas_call(
        body, name="final_loss", grid=(steps,),
        in_specs=[row, vec, row],
        out_specs=[row, vec, pl.BlockSpec((1, 128), lambda m: (0, 0))],
        out_shape=[_sds((t, d), F32), _sds((1, d), F32), _sds((1, 128), F32)],
        scratch_shapes=[pltpu.VMEM((1, d), F32)],
        compiler_params=_cp(1))(x, g, target)


def _row_tile(rows, cols):
    best = rows
    for tr in range(8, rows + 1, 8):
        if rows % tr == 0 and tr * cols * 4 <= (1 << 20):
            best = tr
    if best * cols * 4 > (1 << 21):
        best = 8
    return best


def _adamw(w, g_arr, g_sel, m, v):
    rows, cols = w.shape
    tr = _row_tile(rows, cols)
    b1c = 1.0 - ADAM_B1 ** ADAM_STEP
    b2c = 1.0 - ADAM_B2 ** ADAM_STEP

    def body(w_ref, g_ref, m_ref, v_ref, go_ref, d_ref, mo_ref, vo_ref):
        gv = g_ref[...]
        mn = ADAM_B1 * m_ref[...] + (1.0 - ADAM_B1) * gv
        vn = ADAM_B2 * v_ref[...] + (1.0 - ADAM_B2) * (gv * gv)
        go_ref[...] = gv
        mo_ref[...] = mn
        vo_ref[...] = vn
        d_ref[...] = -ADAM_LR * ((mn / b1c) / (jnp.sqrt(vn / b2c) + ADAM_EPS) + ADAM_WD * w_ref[...])

    blk = pl.BlockSpec((tr, cols), lambda i: (i, 0))
    shp = _sds((rows, cols), F32)
    return pl.pallas_call(
        body, name="adamw", grid=(rows // tr,),
        in_specs=[blk, pl.BlockSpec((None, tr, cols), lambda i: (g_sel, i, 0)), blk, blk],
        out_specs=[blk] * 4, out_shape=[shp] * 4,
        compiler_params=_cp(1))(w, g_arr, m, v)


def _flip(v, bit):
    return 1 - v if bit else v


def _my_place():
    x, y, c = lax.axis_index("x"), lax.axis_index("y"), lax.axis_index("c")
    return x, y, c


class _Exchange:
    def __init__(self, operands, out_shape, aliases, sems, start, finish):
        self.operands, self.out_shape, self.aliases, self.sems = list(operands), list(out_shape), dict(aliases), list(sems)
        self.start, self.finish = start, finish


def _join(exchanges):
    exchanges = [e for e in exchanges if e is not None]
    if not exchanges:
        return None
    ops, outs, sems, aliases, spans = [], [], [], {}, []
    for e in exchanges:
        spans.append((len(ops), len(outs), len(sems), e))
        for i, j in e.aliases.items():
            aliases[len(ops) + i] = len(outs) + j
        ops += e.operands
        outs += e.out_shape
        sems += e.sems

    def run(which):
        def go(ins, res, sm):
            for io, oo, so, e in spans:
                getattr(e, which)(ins[io:io + len(e.operands)], res[oo:oo + len(e.out_shape)], sm[so:so + len(e.sems)])
        return go

    return _Exchange(ops, outs, aliases, sems, run("start"), run("finish"))


def _call(body, name, grid, in_specs, out_specs, out_shape, args, scratch=(), carry=None):
    in_specs, out_specs, out_shape, scratch = list(in_specs), list(out_specs), list(out_shape), list(scratch)
    if carry is None:
        return pl.pallas_call(body, name=name, grid=grid, in_specs=in_specs, out_specs=out_specs,
                              out_shape=out_shape, scratch_shapes=scratch,
                              compiler_params=_cp(len(grid)))(*args)
    n_in, n_out, n_s = len(in_specs), len(out_specs), len(scratch)
    c_in, c_out = len(carry.operands), len(carry.out_shape)
    any_spec = pl.BlockSpec(memory_space=pl.ANY)

    def wrapped(*refs):
        ins, cins = refs[:n_in], refs[n_in:n_in + c_in]
        o0 = n_in + c_in
        outs, couts = refs[o0:o0 + n_out], refs[o0 + n_out:o0 + n_out + c_out]
        s0 = o0 + n_out + c_out
        scr, sems = refs[s0:s0 + n_s], refs[s0 + n_s:]
        first = pl.program_id(0) == 0
        last = pl.program_id(0) == grid[0] - 1
        for ax in range(1, len(grid)):
            first = jnp.logical_and(first, pl.program_id(ax) == 0)
            last = jnp.logical_and(last, pl.program_id(ax) == grid[ax] - 1)

        @pl.when(first)
        def _():
            carry.start(cins, couts, sems)

        body(*ins, *outs, *scr)

        @pl.when(last)
        def _():
            carry.finish(cins, couts, sems)

    return pl.pallas_call(
        wrapped, name=name, grid=grid, in_specs=in_specs + [any_spec] * c_in,
        out_specs=out_specs + [any_spec] * c_out, out_shape=out_shape + carry.out_shape,
        scratch_shapes=scratch + carry.sems,
        input_output_aliases={n_in + i: n_out + j for i, j in carry.aliases.items()},
        compiler_params=_cp(len(grid)))(*args, *carry.operands)


def _alone(name, ex):
    any_spec = pl.BlockSpec(memory_space=pl.ANY)
    c_in, c_out = len(ex.operands), len(ex.out_shape)

    def body(*refs):
        ins, outs, sems = refs[:c_in], refs[c_in:c_in + c_out], refs[c_in + c_out:]
        ex.start(ins, outs, sems)
        ex.finish(ins, outs, sems)

    return pl.pallas_call(
        body, name=name, in_specs=[any_spec] * c_in, out_specs=[any_spec] * c_out, out_shape=ex.out_shape,
        scratch_shapes=ex.sems, input_output_aliases=ex.aliases, compiler_params=_cp())(*ex.operands)


def _ada_fwd(c_pad, w_ada, b_shard):
    d = c_pad.shape[-1]
    cols = w_ada.shape[-1]
    chunk = 384

    def body(c_ref, w_ref, b_ref, call_ref, mod_ref, part, s1, r1, s2, r2):
        x, y, c = _my_place()
        dev = 4 * x + 2 * y + c
        chip = 2 * x + y
        call_ref[dev] = c_ref[...]

        def c_copy(k):
            px, py, pc = _flip(x, (k >> 2) & 1), _flip(y, (k >> 1) & 1), _flip(c, k & 1)
            return px, py, pc

        sends = []
        for k in range(1, N_DEV):
            px, py, pc = c_copy(k)
            cp = pltpu.make_async_remote_copy(src_ref=c_ref, dst_ref=call_ref.at[dev], send_sem=s1.at[k - 1],
                                              recv_sem=r1.at[k - 1], device_id=(px, py, pc), device_id_type=MESH)
            cp.start()
            sends.append(cp)
        for k in range(1, N_DEV):
            px, py, pc = c_copy(k)
            pltpu.make_async_remote_copy(src_ref=c_ref, dst_ref=call_ref.at[4 * px + 2 * py + pc],
                                         send_sem=s1.at[k - 1], recv_sem=r1.at[k - 1],
                                         device_id=(px, py, pc), device_id_type=MESH).wait_recv()
        for cp in sends:
            cp.wait_send()

        cs = call_ref[...].reshape(N_DEV * 8, d)
        sc = (cs * jax.nn.sigmoid(cs)).astype(BF)
        for n0 in range(0, cols, chunk):
            blk = _nn(sc, w_ref[:, n0:n0 + chunk].astype(BF)) + b_ref[:, n0:n0 + chunk]
            part[:, :, n0:n0 + chunk] = blk.reshape(N_DEV, 8, chunk)

        mod_ref[chip] = part[dev]
        sends = []
        for kk in range(1, N_CHIPS):
            px, py = _flip(x, (kk >> 1) & 1), _flip(y, kk & 1)
            cp = pltpu.make_async_remote_copy(src_ref=part.at[4 * px + 2 * py + c], dst_ref=mod_ref.at[chip],
                                              send_sem=s2.at[kk - 1], recv_sem=r2.at[kk - 1],
                                              device_id=(px, py, c), device_id_type=MESH)
            cp.start()
            sends.append(cp)
        for kk in range(1, N_CHIPS):
            px, py = _flip(x, (kk >> 1) & 1), _flip(y, kk & 1)
            pltpu.make_async_remote_copy(src_ref=part.at[dev], dst_ref=mod_ref.at[2 * px + py],
                                         send_sem=s2.at[kk - 1], recv_sem=r2.at[kk - 1],
                                         device_id=(px, py, c), device_id_type=MESH).wait_recv()
        for cp in sends:
            cp.wait_send()

    return pl.pallas_call(
        body, name="ada_fwd",
        out_shape=[_sds((N_DEV, 8, d), F32), _sds((N_CHIPS, 8, cols), F32)],
        scratch_shapes=[pltpu.VMEM((N_DEV, 8, cols), F32),
                        pltpu.SemaphoreType.DMA((N_DEV - 1,)), pltpu.SemaphoreType.DMA((N_DEV - 1,)),
                        pltpu.SemaphoreType.DMA((N_CHIPS - 1,)), pltpu.SemaphoreType.DMA((N_CHIPS - 1,))],
        compiler_params=_cp())(c_pad, w_ada, b_shard)


def _ag_weights(bufs):
    n = len(bufs)

    def place():
        x, y, c = _my_place()
        others = [(_flip(x, (kk >> 1) & 1), _flip(y, kk & 1)) for kk in range(1, N_CHIPS)]
        return x, y, c, 2 * x + y, others

    def half(b, which):
        hr = bufs[b].shape[2] // 2
        return pl.ds(pl.multiple_of(which * hr, 16), hr)

    def ici(outs, sems, b, i, slot, x, y, c, px, py):
        rows = outs[b].at[slot, :, half(b, c), :]
        return pltpu.make_async_remote_copy(
            src_ref=rows, dst_ref=rows, send_sem=sems[0].at[3 * b + i], recv_sem=sems[1].at[3 * b + i],
            device_id=(px, py, c), device_id_type=MESH)

    def d2d(outs, sems, b, i, slot, x, y, c, which):
        rows = outs[b].at[slot, :, half(b, which), :]
        return pltpu.make_async_remote_copy(
            src_ref=rows, dst_ref=rows, send_sem=sems[2].at[3 * b + i], recv_sem=sems[3].at[3 * b + i],
            device_id=(x, y, 1 - c), device_id_type=MESH)

    def start(ins, outs, sems):
        x, y, c, chip, others = place()
        for b in range(n):
            for i, (px, py) in enumerate(others):
                ici(outs, sems, b, i, chip, x, y, c, px, py).start()

    def finish(ins, outs, sems):
        x, y, c, chip, others = place()
        for b in range(n):
            for i, (px, py) in enumerate(others):
                ici(outs, sems, b, i, 2 * px + py, x, y, c, px, py).wait_recv()
                d2d(outs, sems, b, i, 2 * px + py, x, y, c, c).start()
        for b in range(n):
            for i, (px, py) in enumerate(others):
                d2d(outs, sems, b, i, 2 * px + py, x, y, c, 1 - c).wait_recv()
        for b in range(n):
            for i, (px, py) in enumerate(others):
                ici(outs, sems, b, i, chip, x, y, c, px, py).wait_send()
                d2d(outs, sems, b, i, 2 * px + py, x, y, c, c).wait_send()

    return _Exchange(bufs, [_sds(s.shape, s.dtype) for s in bufs], {i: i for i in range(n)},
                     [pltpu.SemaphoreType.DMA((3 * n,))] * 4, start, finish)


def _rs_d2d(grads):
    n = len(grads)

    def copy(ins, outs, sems, b):
        x, y, c = _my_place()
        hr = grads[b].shape[2] // 2
        theirs = pl.ds(pl.multiple_of((1 - c) * hr, 8), hr)
        return pltpu.make_async_remote_copy(
            src_ref=ins[b].at[:, :, theirs, :], dst_ref=outs[b], send_sem=sems[0].at[b], recv_sem=sems[1].at[b],
            device_id=(x, y, 1 - c), device_id_type=MESH)

    def start(ins, outs, sems):
        for b in range(n):
            copy(ins, outs, sems, b).start()

    def finish(ins, outs, sems):
        for b in range(n):
            copy(ins, outs, sems, b).wait()

    return _Exchange(grads, [_sds(g.shape[:2] + (g.shape[2] // 2, g.shape[3]), F32) for g in grads], {},
                     [pltpu.SemaphoreType.DMA((n,))] * 2, start, finish)


def _add_halves(core, g, land):
    nchip, ng, rows, cols = g.shape
    hr = rows // 2
    tr = _row_tile(hr, cols)
    steps = hr // tr

    def body(core_ref, g_ref, l_ref, o_ref):
        del core_ref
        o_ref[...] = (g_ref[...] + l_ref[...]).astype(BF)

    return pl.pallas_call(
        body, name="add_halves",
        grid_spec=pltpu.PrefetchScalarGridSpec(
            num_scalar_prefetch=1, grid=(nchip, ng, steps),
            in_specs=[pl.BlockSpec((None, None, tr, cols), lambda j, a, i, cr: (j, a, cr[0] * steps + i, 0)),
                      pl.BlockSpec((None, None, tr, cols), lambda j, a, i, cr: (j, a, i, 0))],
            out_specs=pl.BlockSpec((None, None, tr, cols), lambda j, a, i, cr: (j, a, i, 0))),
        out_shape=_sds((nchip, ng, hr, cols), BF),
        compiler_params=_cp(3))(core, g, land)


def _rs_ici(parts):
    n = len(parts)

    def copies(ins, outs, sems):
        x, y, c = _my_place()
        chip = 2 * x + y
        for b in range(n):
            for kk in range(1, N_CHIPS):
                px, py = _flip(x, (kk >> 1) & 1), _flip(y, kk & 1)
                k = 3 * b + kk - 1
                send = pltpu.make_async_remote_copy(
                    src_ref=ins[b].at[2 * px + py], dst_ref=outs[b].at[chip],
                    send_sem=sems[0].at[k], recv_sem=sems[1].at[k], device_id=(px, py, c), device_id_type=MESH)
                slot = outs[b].at[2 * px + py]
                recv = pltpu.make_async_remote_copy(
                    src_ref=slot, dst_ref=slot, send_sem=sems[0].at[k], recv_sem=sems[1].at[k],
                    device_id=(px, py, c), device_id_type=MESH)
                yield send, recv

    def start(ins, outs, sems):
        for send, _ in copies(ins, outs, sems):
            send.start()

    def finish(ins, outs, sems):
        for send, recv in copies(ins, outs, sems):
            recv.wait_recv()
            send.wait_send()

    return _Exchange(parts, [_sds(p.shape, p.dtype) for p in parts], {},
                     [pltpu.SemaphoreType.DMA((3 * n,))] * 2, start, finish)


def _sum_chips(place, part, land):
    nchip, ng, hr, cols = land.shape
    tr = _row_tile(hr, cols)
    steps = hr // tr

    def body(place_ref, p_ref, l1, l2, l3, o_ref):
        del place_ref
        o_ref[...] = ((p_ref[...].astype(F32) + l1[...].astype(F32)) + l2[...].astype(F32)) + l3[...].astype(F32)

    def slot(k):
        return pl.BlockSpec((None, None, tr, cols), lambda a, i, pr: (jnp.bitwise_xor(pr[1], k), a, i, 0))

    return pl.pallas_call(
        body, name="sum_chips",
        grid_spec=pltpu.PrefetchScalarGridSpec(
            num_scalar_prefetch=1, grid=(ng, steps),
            in_specs=[slot(0), slot(1), slot(2), slot(3)],
            out_specs=pl.BlockSpec((None, tr, cols), lambda a, i, pr: (a, pr[0] * steps + i, 0))),
        out_shape=_sds((ng, 2 * hr, cols), F32),
        compiler_params=_cp(2))(place, part, land, land, land)


def _rs_final(bufs):
    n = len(bufs)

    def copy(outs, sems, b, which):
        x, y, c = _my_place()
        hr = bufs[b].shape[1] // 2
        rows = outs[b].at[:, pl.ds(pl.multiple_of((c if which == 0 else 1 - c) * hr, 8), hr), :]
        return pltpu.make_async_remote_copy(
            src_ref=rows, dst_ref=rows, send_sem=sems[0].at[b], recv_sem=sems[1].at[b],
            device_id=(x, y, 1 - c), device_id_type=MESH)

    def start(ins, outs, sems):
        for b in range(n):
            copy(outs, sems, b, 0).start()

    def finish(ins, outs, sems):
        for b in range(n):
            copy(outs, sems, b, 0).wait_send()
            copy(outs, sems, b, 1).wait_recv()

    return _Exchange(bufs, [_sds(h.shape, F32) for h in bufs], {i: i for i in range(n)},
                     [pltpu.SemaphoreType.DMA((n,))] * 2, start, finish)


def _small_sync(smalls, dmod_blk, c_all):
    d = c_all.shape[-1]
    cols = dmod_blk.shape[-1]
    chunk = 384

    def body(sm_ref, dm_ref, c_ref, sum_ref, gw_ref, sm_all, dm_all, ssem, rsem):
        x, y, c = _my_place()
        dev = 4 * x + 2 * y + c
        chip = 2 * x + y
        sm_all[dev] = sm_ref[...]
        dm_all[dev] = dm_ref[chip]
        sends = []
        for k in range(1, N_DEV):
            px, py, pc = _flip(x, (k >> 2) & 1), _flip(y, (k >> 1) & 1), _flip(c, k & 1)
            a = pltpu.make_async_remote_copy(src_ref=sm_ref, dst_ref=sm_all.at[dev], send_sem=ssem.at[2 * (k - 1)],
                                             recv_sem=rsem.at[2 * (k - 1)], device_id=(px, py, pc),
                                             device_id_type=MESH)
            b = pltpu.make_async_remote_copy(src_ref=dm_ref.at[2 * px + py], dst_ref=dm_all.at[dev],
                                             send_sem=ssem.at[2 * (k - 1) + 1], recv_sem=rsem.at[2 * (k - 1) + 1],
                                             device_id=(px, py, pc), device_id_type=MESH)
            a.start()
            b.start()
            sends += [a, b]
        for k in range(1, N_DEV):
            px, py, pc = _flip(x, (k >> 2) & 1), _flip(y, (k >> 1) & 1), _flip(c, k & 1)
            pdev = 4 * px + 2 * py + pc
            pltpu.make_async_remote_copy(src_ref=sm_ref, dst_ref=sm_all.at[pdev], send_sem=ssem.at[2 * (k - 1)],
                                         recv_sem=rsem.at[2 * (k - 1)], device_id=(px, py, pc),
                                         device_id_type=MESH).wait_recv()
            pltpu.make_async_remote_copy(src_ref=dm_ref.at[chip], dst_ref=dm_all.at[pdev],
                                         send_sem=ssem.at[2 * (k - 1) + 1], recv_sem=rsem.at[2 * (k - 1) + 1],
                                         device_id=(px, py, pc), device_id_type=MESH).wait_recv()
        for cp in sends:
            cp.wait_send()

        tot = sm_all[0]
        for q in range(1, N_DEV):
            tot = tot + sm_all[q]
        sum_ref[...] = tot

        cs = c_ref[...].reshape(N_DEV * 8, d)
        sc = (cs * jax.nn.sigmoid(cs)).astype(BF)
        for n0 in range(0, cols, chunk):
            dmv = dm_all[:, :, n0:n0 + chunk].reshape(N_DEV * 8, chunk).astype(BF)
            gw_ref[:, n0:n0 + chunk] = _tn(sc, dmv)

    return pl.pallas_call(
        body, name="small_sync",
        out_shape=[_sds(smalls.shape, F32), _sds((d, cols), F32)],
        scratch_shapes=[pltpu.VMEM((N_DEV,) + smalls.shape, F32), pltpu.VMEM((N_DEV, 8, cols), F32),
                        pltpu.SemaphoreType.DMA((2 * (N_DEV - 1),)), pltpu.SemaphoreType.DMA((2 * (N_DEV - 1),))],
        compiler_params=_cp())(smalls, dmod_blk, c_all)


def _bucket_onehot():
    maps = np.stack([_bucket_map(dil).reshape(-1) for _, dil in DIL_CONFIGS])
    return (jnp.asarray(maps)[:, None, :] == jnp.arange(N_BUCKETS, dtype=jnp.int32)[None, :, None]).astype(BF)


def _dil_bias(rel_t, onehot):
    def body(r_ref, oh_ref, o_ref):
        rv = r_ref[...]
        hi = rv.astype(BF)
        lo = (rv - hi.astype(F32)).astype(BF)
        for c in range(len(DIL_CONFIGS)):
            o_ref[c] = _nn(hi, oh_ref[c]) + _nn(lo, oh_ref[c])

    return pl.pallas_call(body, name="dil_bias",
                          out_shape=_sds((len(DIL_CONFIGS), N_HEADS, BLOCK * 2 * BLOCK), F32),
                          compiler_params=_cp())(rel_t, onehot)


def _rowsum8(a):
    def body(a_ref, o_ref):
        o_ref[...] = jnp.sum(a_ref[...], axis=0, keepdims=True)

    return pl.pallas_call(body, name="rowsum8", out_shape=_sds((1, a.shape[1]), F32), compiler_params=_cp())(a)


def _local_step(x, mod, target, w, gains, rel_bias, place=None):
    nb, seq, d = x.shape
    t = nb * seq
    dist = place is not None
    core = place[0:1] if dist else None
    x0 = x.reshape(t, d)
    tgt = target.reshape(t, d)
    md = [mod[:, i:i + 1, :] for i in range(N_MOD)]
    sh1, sc1, gt1, sh2, sc2, gt2, sh3, sc3, gt3 = md
    g1, g2, g3 = gains["g_ffn1"], gains["g_mix"], gains["g_ffn2"]
    ones_g = _group_ones()

    def partial_sums(grads, lands):
        return [_add_halves(core, g, l) for g, l in zip(grads, lands)]

    def chip_sums(parts, lands):
        return [_sum_chips(place, p, l) for p, l in zip(parts, lands)]

    h1 = _modnorm(x0, g1, sc1, sh1, seq)
    res = _ffn_up(h1, w["gu1"], carry=_ag_weights([w["win"], w["wout"]]) if dist else None)
    a1, u1, s1 = res[:3]
    w_in, w_out = res[3:] if dist else (w["win"], w["wout"])
    w_out2 = w_out.reshape(2 * D_GRP, d)
    f1, x1 = _ffn_down(s1, w["d1"], x0, gt1, seq, 0.5)

    h2 = _modnorm(x1, g2, sc2, sh2, seq)
    res = _qkv_proj(h2, w_in, carry=_ag_weights([w["gu2"], w["d2"]]) if dist else None)
    qkv6 = res[0]
    wgu2, wd2 = res[1:] if dist else (w["gu2"], w["d2"])
    qkv6b = qkv6.reshape(6, nb, seq, D_GRP)
    o_sb, on_sb = _sb_fwd(qkv6b, gains["g_sb_out"], nb, seq)
    onehot = _bucket_onehot()
    bias = _dil_bias(rel_bias.T, onehot).reshape(len(DIL_CONFIGS), N_HEADS, BLOCK, 2 * BLOCK)
    o_cs, l_cs, qkv_rs = [], [], []
    for ci, (_, dil) in enumerate(DIL_CONFIGS):
        sub = seq // dil
        qr = qkv6.reshape(6, nb, sub, dil * D_GRP)
        o_c, l_c = _dil_fwd(qr, bias[ci], nb, sub, dil)
        qkv_rs.append(qr)
        o_cs.append(o_c.reshape(t, D_GRP))
        l_cs.append(l_c.reshape(t, D_GRP))
    o_dil, on_dil = _dil_comb(o_cs, l_cs, gains["g_dil_out"], ones_g)
    tmix, x2 = _mix_out(on_sb.reshape(t, D_GRP), on_dil, w_out2, x1, gt2, seq)

    h3 = _modnorm(x2, g3, sc3, sh3, seq)
    a3, u3, s3 = _ffn_up(h3, wgu2)
    f3, x3 = _ffn_down(s3, wd2, x2, gt3, seq, 0.5)

    dx3, dg_final, loss = _final_loss(x3, gains["g_final"], tgt)

    da3, du3, df3, dgt3 = _ffn_bwd_ds(dx3, gt3, f3, wd2, a3, u3, seq, 0.5)
    grads2 = _ffn_bwd_w(h3, da3, du3, s3, df3)
    res = _ffn_bwd_dh(da3, du3, wgu2, x2, g3, sc3, dx3, seq, carry=_rs_d2d(grads2) if dist else None)
    dx2, dsh3, dsc3, dg3 = res[:4]
    parts2 = partial_sums(grads2, res[4:]) if dist else None

    do_sb, do_dil, dgt2, dg_sb, dg_dil, dw_out = _mix_bwd_out(
        dx2, gt2, tmix, w_out2, o_sb.reshape(t, D_GRP), o_dil, on_sb.reshape(t, D_GRP), on_dil,
        gains["g_sb_out"], gains["g_dil_out"], ones_g, seq)
    dw_out = dw_out.reshape(N_CHIPS, 1, 2 * D_GRP // N_CHIPS, d)
    res = _sb_bwd(qkv6b, do_sb.reshape(nb, seq, D_GRP), nb, seq, carry=_rs_ici(parts2) if dist else None)
    dqkv6 = res[0]
    halves2 = chip_sums(parts2, res[1:]) if dist else None
    dcs = _dil_comb_bwd(do_dil, o_cs, l_cs, ones_g)
    dsum, a_tiles = [], []
    for ci, (_, dil) in enumerate(DIL_CONFIGS):
        sub = seq // dil
        do_c = dcs[ci].reshape(nb, sub, dil * D_GRP)
        dd_c = dcs[3 + ci].reshape(nb, sub, dil * D_GRP)
        res = _dil_bwd(qkv_rs[ci], bias[ci], do_c, dd_c, nb, sub, dil,
                       carry=_rs_final(halves2) if dist and ci == 0 else None)
        if dist and ci == 0:
            grads2 = res[2:]
        dsum.append(res[0].reshape(3, t, D_GRP))
        a_tiles.append(res[1].reshape(N_HEADS, BLOCK * 2 * BLOCK))
    dqkv6 = _dqkv_dil_sum(dsum[0], dsum[1], dsum[2], dqkv6.reshape(6, t, D_GRP))
    drel = _relbias_grad(jnp.stack(a_tiles), onehot)
    dx1, dsh2, dsc2, dg2 = _mix_bwd_dh(dqkv6, w_in, x1, g2, sc2, dx2, seq)

    da1, du1, df1, dgt1 = _ffn_bwd_ds(dx1, gt1, f1, w["d1"], a1, u1, seq, 0.5)
    grads1 = _ffn_bwd_w(h1, da1, du1, s1, df1)
    res = _dw_in(h2, dqkv6, carry=_rs_d2d(grads1) if dist else None)
    grads_m = [res[0], dw_out]
    parts1 = partial_sums(grads1, res[1:]) if dist else None
    res = _ffn_bwd_dh(da1, du1, w["gu1"], x0, g1, sc1, dx1, seq,
                      carry=_join([_rs_ici(parts1), _rs_d2d(grads_m)]) if dist else None)
    dx0, dsh1, dsc1, dg1 = res[:4]
    if dist:
        halves1 = chip_sums(parts1, res[4:6])
        parts_m = partial_sums(grads_m, res[6:8])
        res = _alone("rs_tail", _join([_rs_final(halves1), _rs_ici(parts_m)]))
        grads1 = res[:2]
        grads_m = _alone("rs_last", _rs_final(chip_sums(parts_m, res[2:4])))

    dmod = jnp.concatenate([dsh1, dsc1, dgt1, dsh2, dsc2, dgt2, dsh3, dsc3, dgt3], axis=1)
    return dict(grad_x=dx0.reshape(nb, seq, d), loss=loss[0, 0], dmod=dmod.reshape(nb, N_MOD * d),
                dgu1=grads1[0], dwd1=grads1[1], dgu2=grads2[0], dwd2=grads2[1], dwin=grads_m[0], dwout=grads_m[1],
                dg_ffn1=dg1, dg_mix=dg2, dg_ffn2=dg3, dg_final=dg_final, dg_sb=dg_sb, dg_dil=dg_dil,
                drel=drel.T)


_SMALL_ORDER = (("b_ada", N_MOD * 1024), ("g_ffn1", 1024), ("g_mix", 1024), ("g_ffn2", 1024), ("g_final", 1024),
                ("g_sb_out", D_GRP), ("g_dil_out", D_GRP), ("rel_bias", N_BUCKETS * N_HEADS))


def _pack_small(parts, extra=None):
    flat = [parts[name].reshape(-1).astype(F32) for name, _ in _SMALL_ORDER]
    used = sum(sz for _, sz in _SMALL_ORDER)
    pad = SMALL_ROWS * 128 - used
    tail = jnp.zeros((pad,), F32)
    if extra is not None:
        tail = tail.at[0].set(extra)
    return jnp.concatenate(flat + [tail]).reshape(SMALL_ROWS, 128)


def _unpack_small(packed, shapes):
    flat = packed.reshape(-1)
    out, off = {}, 0
    for name, sz in _SMALL_ORDER:
        out[name] = flat[off:off + sz].reshape(shapes[name])
        off += sz
    return out, flat[off]


def kernel(x, c, w_ada, b_ada, g_ffn1, w1_gate, w1_up, w1_down, g_mix, w_in, g_sb_out, g_dil_out, w_out, rel_bias, g_ffn2, w2_gate, w2_up, w2_down, g_final, loss_target, m_w_ada, m_b_ada, m_g_ffn1, m_w1_gate, m_w1_up, m_w1_down, m_g_mix, m_w_in, m_g_sb_out, m_g_dil_out, m_w_out, m_rel_bias, m_g_ffn2, m_w2_gate, m_w2_up, m_w2_down, m_g_final, v_w_ada, v_b_ada, v_g_ffn1, v_w1_gate, v_w1_up, v_w1_down, v_g_mix, v_w_in, v_g_sb_out, v_g_dil_out, v_w_out, v_rel_bias, v_g_ffn2, v_w2_gate, v_w2_up, v_w2_down, v_g_final):
    nb, seq, d = x.shape
    xi, yi, ci = lax.axis_index("x"), lax.axis_index("y"), lax.axis_index("c")
    chip = 2 * xi + yi
    ada_cols = w_ada.shape[-1]

    c_pad = jnp.zeros((8, d), F32).at[:nb].set(c)
    b_shard = lax.dynamic_slice(b_ada, (0, chip * ada_cols), (1, ada_cols))
    c_all, mod_blk = _ada_fwd(c_pad, w_ada[0], b_shard)
    mod = jnp.transpose(mod_blk[:, :nb, :], (1, 0, 2)).reshape(nb, N_MOD, d)

    shards = dict(gu1=jnp.stack([w1_gate[0], w1_up[0]]), d1=w1_down, win=w_in, wout=w_out,
                  gu2=jnp.stack([w2_gate[0], w2_up[0]]), d2=w2_down)
    bufs = {k: lax.dynamic_update_slice(lax.empty((N_CHIPS,) + s.shape, BF), s.astype(BF)[None], (chip, 0, 0, 0))
            for k, s in shards.items()}
    bufs["gu1"], bufs["d1"] = _alone("ag_first", _ag_weights([bufs["gu1"], bufs["d1"]]))

    gains = dict(g_ffn1=g_ffn1, g_mix=g_mix, g_ffn2=g_ffn2, g_final=g_final.reshape(1, d),
                 g_sb_out=g_sb_out.reshape(1, D_GRP), g_dil_out=g_dil_out.reshape(1, D_GRP))
    place = jnp.stack([ci, chip]).astype(jnp.int32)
    r = _local_step(x, mod, loss_target, bufs, gains, rel_bias, place)
    gu1, gu2, gd1, gd2, gwin, gwout = r["dgu1"], r["dgu2"], r["dwd1"], r["dwd2"], r["dwin"], r["dwout"]

    dmod = r["dmod"]
    dmod_pad = jnp.zeros((8, N_MOD * d), F32).at[:nb].set(dmod)
    dmod_blk = jnp.transpose(dmod_pad.reshape(8, N_CHIPS, ada_cols), (1, 0, 2))
    small_parts = dict(b_ada=_rowsum8(dmod_pad), g_ffn1=r["dg_ffn1"], g_mix=r["dg_mix"], g_ffn2=r["dg_ffn2"],
                       g_final=r["dg_final"], g_sb_out=r["dg_sb"], g_dil_out=r["dg_dil"], rel_bias=r["drel"])
    small_sum, g_wada = _small_sync(_pack_small(small_parts, r["loss"]), dmod_blk, c_all)

    small_w = dict(b_ada=b_ada, g_ffn1=g_ffn1, g_mix=g_mix, g_ffn2=g_ffn2, g_final=g_final,
                   g_sb_out=g_sb_out, g_dil_out=g_dil_out, rel_bias=rel_bias)
    small_m = dict(b_ada=m_b_ada, g_ffn1=m_g_ffn1, g_mix=m_g_mix, g_ffn2=m_g_ffn2, g_final=m_g_final,
                   g_sb_out=m_g_sb_out, g_dil_out=m_g_dil_out, rel_bias=m_rel_bias)
    small_v = dict(b_ada=v_b_ada, g_ffn1=v_g_ffn1, g_mix=v_g_mix, g_ffn2=v_g_ffn2, g_final=v_g_final,
                   g_sb_out=v_g_sb_out, g_dil_out=v_g_dil_out, rel_bias=v_rel_bias)
    shapes = {k: v.shape for k, v in small_w.items()}
    sg, sd, sm, sv = _adamw(_pack_small(small_w), small_sum.reshape(1, SMALL_ROWS, 128), 0,
                            _pack_small(small_m), _pack_small(small_v))
    sg, loss = _unpack_small(sg, shapes)
    sd, _ = _unpack_small(sd, shapes)
    sm, _ = _unpack_small(sm, shapes)
    sv, _ = _unpack_small(sv, shapes)

    big = {}

    def upd(name, w, g_arr, sel, m, v):
        shape = w.shape
        res = _adamw(w.reshape(shape[-2:]), g_arr, sel, m.reshape(shape[-2:]), v.reshape(shape[-2:]))
        big[name] = [a.reshape(shape) for a in res]

    upd("w_ada", w_ada, g_wada.reshape(1, d, ada_cols), 0, m_w_ada, v_w_ada)
    upd("w1_gate", w1_gate, gu1, 0, m_w1_gate, v_w1_gate)
    upd("w1_up", w1_up, gu1, 1, m_w1_up, v_w1_up)
    upd("w1_down", w1_down, gd1, 0, m_w1_down, v_w1_down)
    upd("w_in", w_in, gwin, 0, m_w_in, v_w_in)
    upd("w_out", w_out, gwout, 0, m_w_out, v_w_out)
    upd("w2_gate", w2_gate, gu2, 0, m_w2_gate, v_w2_gate)
    upd("w2_up", w2_up, gu2, 1, m_w2_up, v_w2_up)
    upd("w2_down", w2_down, gd2, 0, m_w2_down, v_w2_down)

    names = ["w_ada", "b_ada", "g_ffn1", "w1_gate", "w1_up", "w1_down", "g_mix", "w_in", "g_sb_out", "g_dil_out",
             "w_out", "rel_bias", "g_ffn2", "w2_gate", "w2_up", "w2_down", "g_final"]
    outs = [loss, r["grad_x"]]
    for k, small in enumerate((sg, sd, sm, sv)):
        for name in names:
            outs.append(big[name][k] if name in big else small[name])
    return tuple(outs)
```

```python
import functools
import math

import numpy as np
import jax
import jax.numpy as jnp
from jax import lax
from jax.experimental import pallas as pl
from jax.experimental.pallas import tpu as pltpu

F32 = jnp.float32
BF = jnp.bfloat16
MESH = pl.DeviceIdType.MESH

HEAD_DIM = 64
N_HEADS = 8
D_GRP = N_HEADS * HEAD_DIM
DIL_CONFIGS = ((128, 1), (512, 4), (2048, 16))
N_STEPS = 128
BLOCK = 128
N_BUCKETS = 32
MAX_DISTANCE = 2048
N_MOD = 9
EPS = 1e-6
NEG_INF = -1e30
SCALE = HEAD_DIM ** -0.5

ADAM_LR = 0.001
ADAM_B1 = 0.9
ADAM_B2 = 0.999
ADAM_EPS = 1e-08
ADAM_WD = 0.01
ADAM_STEP = 10

N_CHIPS = 4
N_DEV = 8
VMEM_LIMIT = 56 * 1024 * 1024
TM = 512
TQ = 256
KB = 256
SMALL_ROWS = 120


def _cp(n_axes=0, **kw):
    sem = ("arbitrary",) * n_axes if n_axes else None
    return pltpu.CompilerParams(dimension_semantics=sem, vmem_limit_bytes=VMEM_LIMIT, **kw)


def _nn(a, b):
    return jnp.dot(a, b, preferred_element_type=F32)


def _nt(a, b):
    return lax.dot_general(a, b, (((1,), (1,)), ((), ())), preferred_element_type=F32)


def _tn(a, b):
    return lax.dot_general(a, b, (((0,), (0,)), ((), ())), preferred_element_type=F32)


def _nn2(x, m):
    hi = x.astype(BF)
    lo = (x - hi.astype(F32)).astype(BF)
    return _nn(hi, m) + _nn(lo, m)


def _softplus(z):
    return jnp.maximum(z, 0.0) + jnp.log1p(jnp.exp(-jnp.abs(z)))


def _sds(shape, dtype):
    return jax.ShapeDtypeStruct(shape, dtype)


def _whole(a):
    nd = a.ndim
    return pl.BlockSpec(a.shape, lambda *_: (0,) * nd, pipeline_mode=pl.Buffered(1))


def _modnorm(x, g, sc, sh, seq):
    t, d = x.shape
    per = seq // TM

    def body(x_ref, g_ref, sc_ref, sh_ref, h_ref):
        xv = x_ref[...]
        r = lax.rsqrt(jnp.mean(xv * xv, axis=-1, keepdims=True) + EPS)
        h_ref[...] = (((xv * r) * g_ref[...]) * (1.0 + sc_ref[...]) + sh_ref[...]).astype(BF)

    return pl.pallas_call(
        body, name="modnorm", grid=(t // TM,),
        in_specs=[pl.BlockSpec((TM, d), lambda m: (m, 0)),
                  pl.BlockSpec((1, d), lambda m: (0, 0)),
                  pl.BlockSpec((None, 1, d), lambda m: (m // per, 0, 0)),
                  pl.BlockSpec((None, 1, d), lambda m: (m // per, 0, 0))],
        out_specs=pl.BlockSpec((TM, d), lambda m: (m, 0)),
        out_shape=_sds((t, d), BF), compiler_params=_cp(1))(x, g, sc, sh)


def _modnorm_bwd_tile(dh, xv, gv, scv, dxo):
    r = lax.rsqrt(jnp.mean(xv * xv, axis=-1, keepdims=True) + EPS)
    n = xv * r
    ng = n * gv
    dsh = jnp.sum(dh, axis=0, keepdims=True)
    dsc = jnp.sum(dh * ng, axis=0, keepdims=True)
    dy = dh * (1.0 + scv)
    dg = jnp.sum(dy * n, axis=0, keepdims=True)
    dn = dy * gv
    dx = dxo + r * (dn - n * jnp.mean(dn * n, axis=-1, keepdims=True))
    return dx, dsh, dsc, dg


def _acc_rows(ref, val, first):
    @pl.when(first)
    def _():
        ref[...] = val

    @pl.when(jnp.logical_not(first))
    def _():
        ref[...] += val


def _ffn_up(h, wgu, carry=None):
    t, d = h.shape
    fs = wgu.shape[-1]

    def body(h_ref, w_ref, p_ref, q_ref, s_ref):
        hv = h_ref[...]
        a = _nn(hv, w_ref[0])
        u = _nn(hv, w_ref[1])
        sig = jax.nn.sigmoid(a)
        q = a * sig
        p_ref[...] = (u * (sig * (1.0 + a * (1.0 - sig)))).astype(BF)
        q_ref[...] = q.astype(BF)
        s_ref[...] = (q * u).astype(BF)

    blk = pl.BlockSpec((None, TM, fs), lambda j, m: (j, m, 0))
    return _call(
        body, "ffn_up", (N_CHIPS, t // TM),
        [pl.BlockSpec((TM, d), lambda j, m: (m, 0)),
         pl.BlockSpec((None, 2, d, fs), lambda j, m: (j, 0, 0, 0))],
        [blk, blk, blk],
        [_sds((N_CHIPS, t, fs), BF)] * 3,
        (h, wgu), carry=carry)


def _ffn_down(s, wd, x, gt, seq, coef):
    _, t, fs = s.shape
    d = x.shape[-1]
    per = seq // TM

    def body(s_ref, w_ref, x_ref, gt_ref, f_ref, xo_ref):
        f = _nn(s_ref[0], w_ref[0, 0])
        for j in range(1, N_CHIPS):
            f = f + _nn(s_ref[j], w_ref[j, 0])
        f_ref[...] = f
        xo_ref[...] = x_ref[...] + (coef * gt_ref[...]) * f

    row = pl.BlockSpec((TM, d), lambda m: (m, 0))
    return pl.pallas_call(
        body, name="ffn_down", grid=(t // TM,),
        in_specs=[pl.BlockSpec((N_CHIPS, TM, fs), lambda m: (0, m, 0)),
                  _whole(wd), row,
                  pl.BlockSpec((None, 1, d), lambda m: (m // per, 0, 0))],
        out_specs=[row, row],
        out_shape=[_sds((t, d), F32), _sds((t, d), F32)],
        compiler_params=_cp(1))(s, wd, x, gt)


def _ffn_bwd_ds(dxo, gt, f, wd, p, q, seq, coef, carry=None):
    t, d = dxo.shape
    fs = p.shape[-1]
    per = seq // TM
    nb = t // seq

    def body(dxo_ref, gt_ref, f_ref, w_ref, p_ref, q_ref, da_ref, du_ref, df_ref, dgt_ref):
        m = pl.program_id(0)
        dxv = dxo_ref[...]
        df = ((coef * gt_ref[...]) * dxv).astype(BF)
        df_ref[...] = df
        _acc_rows(dgt_ref, coef * jnp.sum(dxv * f_ref[...], axis=0, keepdims=True), m % per == 0)
        for j in range(N_CHIPS):
            ds = _nt(df, w_ref[j, 0])
            da_ref[j] = (ds * p_ref[j].astype(F32)).astype(BF)
            du_ref[j] = (ds * q_ref[j].astype(F32)).astype(BF)

    row = pl.BlockSpec((TM, d), lambda m: (m, 0))
    blk = pl.BlockSpec((N_CHIPS, TM, fs), lambda m: (0, m, 0))
    ex = pl.BlockSpec((None, 1, d), lambda m: (m // per, 0, 0))
    return _call(
        body, "ffn_bwd_ds", (t // TM,),
        [row, ex, row, _whole(wd), blk, blk],
        [blk, blk, row, ex],
        [_sds((N_CHIPS, t, fs), BF), _sds((N_CHIPS, t, fs), BF), _sds((t, d), BF), _sds((nb, 1, d), F32)],
        (dxo, gt, f, wd, p, q), carry=carry)


def _ffn_bwd_w(h, da, du, s, df):
    t, d = h.shape
    fs = da.shape[-1]

    def body(h_ref, da_ref, du_ref, s_ref, df_ref, dgu_ref, dwd_ref):
        kt = pl.program_id(1)
        hv = h_ref[...]
        pg = _tn(hv, da_ref[...])
        pu = _tn(hv, du_ref[...])
        pd = _tn(s_ref[...], df_ref[...])

        @pl.when(kt == 0)
        def _():
            dgu_ref[0] = pg
            dgu_ref[1] = pu
            dwd_ref[...] = pd

        @pl.when(kt != 0)
        def _():
            dgu_ref[0] += pg
            dgu_ref[1] += pu
            dwd_ref[...] += pd

    row = pl.BlockSpec((TM, d), lambda j, kt: (kt, 0))
    blk = pl.BlockSpec((None, TM, fs), lambda j, kt: (j, kt, 0))
    return pl.pallas_call(
        body, name="ffn_bwd_w", grid=(N_CHIPS, t // TM),
        in_specs=[row, blk, blk, blk, row],
        out_specs=[pl.BlockSpec((None, 2, d, fs), lambda j, kt: (j, 0, 0, 0)),
                   pl.BlockSpec((None, None, fs, d), lambda j, kt: (j, 0, 0, 0))],
        out_shape=[_sds((N_CHIPS, 2, d, fs), F32), _sds((N_CHIPS, 1, fs, d), F32)],
        compiler_params=_cp(2))(h, da, du, s, df)


def _ffn_bwd_dh(da, du, wgu, x, g, sc, dxo, seq, carry=None):
    _, t, fs = da.shape
    d = x.shape[-1]
    per = seq // TM
    nb = t // seq

    def body(da_ref, du_ref, w_ref, x_ref, g_ref, sc_ref, dxo_ref, dx_ref, dsh_ref, dsc_ref, dg_ref):
        m = pl.program_id(0)
        dh = _nt(da_ref[0], w_ref[0, 0]) + _nt(du_ref[0], w_ref[0, 1])
        for j in range(1, N_CHIPS):
            dh = dh + _nt(da_ref[j], w_ref[j, 0]) + _nt(du_ref[j], w_ref[j, 1])
        dx, dsh, dsc, dg = _modnorm_bwd_tile(dh, x_ref[...], g_ref[...], sc_ref[...], dxo_ref[...])
        dx_ref[...] = dx
        _acc_rows(dsh_ref, dsh, m % per == 0)
        _acc_rows(dsc_ref, dsc, m % per == 0)
        _acc_rows(dg_ref, dg, m == 0)

    row = pl.BlockSpec((TM, d), lambda m: (m, 0))
    blk = pl.BlockSpec((N_CHIPS, TM, fs), lambda m: (0, m, 0))
    ex = pl.BlockSpec((None, 1, d), lambda m: (m // per, 0, 0))
    vec = pl.BlockSpec((1, d), lambda m: (0, 0))
    return _call(
        body, "ffn_bwd_dh", (t // TM,),
        [blk, blk, _whole(wgu), row, vec, ex, row],
        [row, ex, ex, vec],
        [_sds((t, d), F32), _sds((nb, 1, d), F32), _sds((nb, 1, d), F32), _sds((1, d), F32)],
        (da, du, wgu, x, g, sc, dxo), carry=carry)


def _qkv_proj(h, w_in, carry=None):
    t, d = h.shape
    wc = w_in.shape[-1]

    def body(h_ref, w_ref, o_ref):
        hv = h_ref[...]
        for j in range(N_CHIPS):
            r = _nn(hv, w_ref[j, 0]).astype(BF)
            for a, lc, off, width in _col_pieces(j, wc):
                o_ref[a, :, lc:lc + width] = r[:, off:off + width]

    return _call(
        body, "qkv_proj", (t // TM,),
        [pl.BlockSpec((TM, d), lambda m: (m, 0)), _whole(w_in)],
        [pl.BlockSpec((6, TM, D_GRP), lambda m: (0, m, 0))],
        [_sds((6, t, D_GRP), BF)], (h, w_in), carry=carry)


def _col_pieces(j, wc):
    out, off = [], 0
    while off < wc:
        a, lc = divmod(j * wc + off, D_GRP)
        width = min(D_GRP - lc, wc - off)
        out.append((a, lc, off, width))
        off += width
    return out


def _chip_cols(g6_ref, j, wc):
    return jnp.concatenate([g6_ref[a, :, lc:lc + width] for a, lc, _, width in _col_pieces(j, wc)], axis=1)


def _mix_out(on_sb, on_dil, w_out, x, gt, seq):
    t, d = x.shape
    per = seq // TM

    def body(a_ref, b_ref, w_ref, x_ref, gt_ref, t_ref, xo_ref):
        tv = _nn(a_ref[...], w_ref[0:D_GRP, :]) + _nn(b_ref[...], w_ref[D_GRP:2 * D_GRP, :])
        t_ref[...] = tv
        xo_ref[...] = x_ref[...] + gt_ref[...] * tv

    row = pl.BlockSpec((TM, d), lambda m: (m, 0))
    half = pl.BlockSpec((TM, D_GRP), lambda m: (m, 0))
    return pl.pallas_call(
        body, name="mix_out", grid=(t // TM,),
        in_specs=[half, half, pl.BlockSpec((2 * D_GRP, d), lambda m: (0, 0)), row,
                  pl.BlockSpec((None, 1, d), lambda m: (m // per, 0, 0))],
        out_specs=[row, row],
        out_shape=[_sds((t, d), F32), _sds((t, d), F32)],
        compiler_params=_cp(1))(on_sb, on_dil, w_out, x, gt)


def _sb_masks():
    lane = lax.broadcasted_iota(jnp.int32, (1, 2 * HEAD_DIM), 1)
    hm0 = lane < HEAD_DIM
    rel = lax.broadcasted_iota(jnp.int32, (TQ, KB), 0) - lax.broadcasted_iota(jnp.int32, (TQ, KB), 1)
    kr = lax.broadcasted_iota(jnp.int32, (KB, KB), 0)
    kc = lax.broadcasted_iota(jnp.int32, (KB, KB), 1)
    return hm0, rel, kr, kc


def _headnorm_pair(o, gv, hm0):
    o2 = o * o
    ms0 = jnp.sum(jnp.where(hm0, o2, 0.0), axis=-1, keepdims=True) * (1.0 / HEAD_DIM)
    ms1 = jnp.sum(jnp.where(hm0, 0.0, o2), axis=-1, keepdims=True) * (1.0 / HEAD_DIM)
    r = jnp.where(hm0, lax.rsqrt(ms0 + EPS), lax.rsqrt(ms1 + EPS))
    return (o * r) * gv


SB_DEAD = -104.0


def _alive(c_l):
    return (jnp.max(c_l) > SB_DEAD).astype(jnp.int32)


def _sb_fwd(qkv6, g_sb, nb, seq, carry=None):
    nq = seq // TQ

    def body(q_ref, k_ref, v_ref, g_ref, o_ref, on_ref):
        qi = pl.program_id(2)
        hm0, rel, kr, kc = _sb_masks()
        upper = (kr > kc).astype(BF)
        qv = q_ref[...]
        outs = []
        for hh in range(2):
            hm = hm0 if hh == 0 else jnp.logical_not(hm0)
            qh = jnp.where(hm, qv, jnp.zeros_like(qv))

            def block(kj, c_l, causal, qh=qh):
                ks = pl.multiple_of(kj * KB, KB)
                z = _nt(qh, k_ref[pl.ds(ks, KB), :]) * SCALE
                sp = _softplus(z)
                ln = -sp if causal is None else jnp.where(causal, -sp, 0.0)
                suf = _nn2(ln, upper)
                w = jnp.exp((z - sp) + (suf + c_l))
                if causal is not None:
                    w = jnp.where(causal, w, 0.0)
                pv = _nn(w.astype(BF), v_ref[pl.ds(ks, KB), :])
                return pv, c_l + (suf[:, 0:1] + ln[:, 0:1])

            acc, c_l = block(qi, jnp.zeros((TQ, 1), F32), rel > 0)

            def cond(carry):
                it, alive, _, _ = carry
                return jnp.logical_and(it <= qi, alive > 0)

            def kbody(carry):
                it, _, c_l, acc = carry
                pv, c_l = block(qi - it, c_l, None)
                return it + 1, _alive(c_l), c_l, acc + pv

            _, _, _, acc = lax.while_loop(cond, kbody, (jnp.int32(1), _alive(c_l), c_l, acc))
            outs.append(acc)
        o = jnp.where(hm0, outs[0], outs[1])
        o_ref[...] = o
        on_ref[...] = _headnorm_pair(o, g_ref[...], hm0).astype(BF)

    w = 2 * HEAD_DIM
    full = lambda i: pl.BlockSpec((None, None, seq, w), lambda b, hp, q: (i, b, 0, hp))
    qblk = pl.BlockSpec((None, None, TQ, w), lambda b, hp, q: (0, b, q, hp))
    oblk = pl.BlockSpec((None, TQ, w), lambda b, hp, q: (b, q, hp))
    return _call(
        body, "sb_fwd", (nb, N_HEADS // 2, nq),
        [qblk, full(1), full(2), pl.BlockSpec((1, w), lambda b, hp, q: (0, hp))],
        [oblk, oblk],
        [_sds((nb, seq, D_GRP), F32), _sds((nb, seq, D_GRP), BF)],
        (qkv6, qkv6, qkv6, g_sb), carry=carry)


def _sb_bwd(qkv6, do, nb, seq, carry=None):
    nq = seq // TQ
    nk = seq // KB

    def body(q_ref, k_ref, v_ref, do_ref, out_ref, dk_acc, dv_acc, car):
        qi = pl.program_id(2)
        hm0, rel, kr, kc = _sb_masks()
        upper = (kr > kc).astype(BF)
        lower = (kr < kc).astype(BF)
        ones = jnp.ones((KB, 2 * HEAD_DIM), BF)

        @pl.when(qi == 0)
        def _():
            dk_acc[...] = jnp.zeros_like(dk_acc)
            dv_acc[...] = jnp.zeros_like(dv_acc)

        qv = q_ref[...]
        dov = do_ref[...]
        dqs = []
        for hh in range(2):
            hm = hm0 if hh == 0 else jnp.logical_not(hm0)
            qh = jnp.where(hm, qv, jnp.zeros_like(qv))
            doh = jnp.where(hm, dov, 0.0).astype(BF)

            def logits(kj, causal, qh=qh):
                ks = pl.multiple_of(kj * KB, KB)
                kb = k_ref[pl.ds(ks, KB), :]
                z = _nt(qh, kb) * SCALE
                sp = _softplus(z)
                ln = -sp if causal is None else jnp.where(causal, -sp, 0.0)
                return ks, kb, z - sp, ln

            car[qi] = jnp.zeros((TQ, 1), F32)
            c_l = _nn2(logits(qi, rel > 0)[3], ones)[:, 0:1]

            def acond(carry):
                it, alive, _ = carry
                return jnp.logical_and(it <= qi, alive > 0)

            def abody(carry):
                it, _, c_l = carry
                car[qi - it] = c_l
                c_l = c_l + _nn2(logits(qi - it, None)[3], ones)[:, 0:1]
                return it + 1, _alive(c_l), c_l

            n_used, _, _ = lax.while_loop(acond, abody, (jnp.int32(1), _alive(c_l), c_l))

            def grads(kj, causal, c_g, dq, doh=doh, qh=qh):
                ks, kb, lsz, ln = logits(kj, causal)
                vb = v_ref[pl.ds(ks, KB), :]
                w = jnp.exp(lsz + (_nn2(ln, upper) + car[kj]))
                if causal is not None:
                    w = jnp.where(causal, w, 0.0)
                g = w * _nt(doh, vb)
                pre = _nn2(g, lower)
                sig = jnp.exp(lsz)
                dz = g * (1.0 - sig) - sig * (pre + c_g)
                if causal is not None:
                    dz = jnp.where(causal, dz, 0.0)
                dzb = (dz * SCALE).astype(BF)
                dk_acc[pl.ds(ks, KB), :] += _tn(dzb, qh)
                dv_acc[pl.ds(ks, KB), :] += _tn(w.astype(BF), doh)
                return c_g + (pre[:, KB - 1:KB] + g[:, KB - 1:KB]), dq + _nn(dzb, kb)

            c_g, dq = lax.fori_loop(qi - n_used + 1, qi, lambda kj, cr: grads(kj, None, *cr),
                                    (jnp.zeros((TQ, 1), F32), jnp.zeros((TQ, 2 * HEAD_DIM), F32)))
            _, dq = grads(qi, rel > 0, c_g, dq)
            dqs.append(dq)
        dq = jnp.where(hm0, dqs[0], dqs[1])
        out_ref[0, pl.ds(pl.multiple_of(qi * TQ, TQ), TQ), :] = dq.astype(BF)

        @pl.when(qi == nq - 1)
        def _():
            out_ref[1] = dk_acc[...].astype(BF)
            out_ref[2] = dv_acc[...].astype(BF)

    w = 2 * HEAD_DIM
    full = lambda i: pl.BlockSpec((None, None, seq, w), lambda b, hp, q: (i, b, 0, hp))
    qblk = pl.BlockSpec((None, None, TQ, w), lambda b, hp, q: (0, b, q, hp))
    oblk = pl.BlockSpec((None, TQ, w), lambda b, hp, q: (b, q, hp))
    return _call(
        body, "sb_bwd", (nb, N_HEADS // 2, nq),
        [qblk, full(1), full(2), oblk],
        [pl.BlockSpec((3, None, seq, w), lambda b, hp, q: (0, b, 0, hp))],
        [_sds((6, nb, seq, D_GRP), BF)], (qkv6, qkv6, qkv6, do),
        scratch=[pltpu.VMEM((seq, w), F32), pltpu.VMEM((seq, w), F32), pltpu.VMEM((nk, TQ, 1), F32)],
        carry=carry)


def _t5_bucket(n):
    max_exact = N_BUCKETS // 2
    nf = np.maximum(n, 1).astype(np.float32)
    large = max_exact + (np.log(nf / max_exact) / math.log(MAX_DISTANCE / max_exact)
                         * (N_BUCKETS - max_exact)).astype(np.int32)
    large = np.minimum(large, N_BUCKETS - 1)
    return np.where(n < max_exact, n, large).astype(np.int32)


def _bucket_map(dilation):
    step = BLOCK + np.arange(BLOCK)[:, None] - np.arange(2 * BLOCK)[None, :]
    return _t5_bucket(np.clip(step, 0, N_STEPS) * dilation)


GRP_HEADS = 4
GRP_W = GRP_HEADS * HEAD_DIM


def _dil_masks():
    lane = lax.broadcasted_iota(jnp.int32, (1, GRP_W), 1)
    heads = [jnp.logical_and(lane >= HEAD_DIM * i, lane < HEAD_DIM * (i + 1)) for i in range(GRP_HEADS)]
    iq = lax.broadcasted_iota(jnp.int32, (BLOCK, BLOCK), 0)
    ik = lax.broadcasted_iota(jnp.int32, (BLOCK, BLOCK), 1)
    return heads, ik <= iq, ik >= iq


def _dil_rows(n):
    rs = pl.multiple_of(n * BLOCK, BLOCK)
    ps = pl.multiple_of(jnp.maximum(n - 1, 0) * BLOCK, BLOCK)
    return pl.ds(rs, BLOCK), pl.ds(ps, BLOCK)


def _dil_probs(qh, kc, kp, b_ref, hh, valid_c, valid_p):
    zc = _nt(qh, kc) * SCALE + b_ref[hh, :, BLOCK:2 * BLOCK]
    zp = _nt(qh, kp) * SCALE + b_ref[hh, :, 0:BLOCK]
    zc = jnp.where(valid_c, zc, NEG_INF)
    zp = jnp.where(valid_p, zp, NEG_INF)
    m = jnp.maximum(jnp.max(zc, axis=-1, keepdims=True), jnp.max(zp, axis=-1, keepdims=True))
    ec = jnp.exp(zc - m)
    ep = jnp.exp(zp - m)
    den = jnp.sum(ec, axis=-1, keepdims=True) + jnp.sum(ep, axis=-1, keepdims=True)
    return ec, ep, den, m


def _dil_fwd(qkv6r, bias, nb, sub_len, dilation):
    n_blk = sub_len // BLOCK

    def body(q_ref, k_ref, v_ref, b_ref, o_ref, l_ref):
        heads, valid_c, valid_p0 = _dil_masks()

        def nbody(n, carry):
            cur, prev = _dil_rows(n)
            valid_p = jnp.logical_and(valid_p0, n > 0)
            for gi in range(N_HEADS // GRP_HEADS):
                lanes = slice(gi * GRP_W, (gi + 1) * GRP_W)
                qv, kc, kp = q_ref[cur, lanes], k_ref[cur, lanes], k_ref[prev, lanes]
                vc, vp = v_ref[cur, lanes], v_ref[prev, lanes]
                o = jnp.zeros((BLOCK, GRP_W), F32)
                lse = jnp.zeros((BLOCK, GRP_W), F32)
                for i, hm in enumerate(heads):
                    qh = jnp.where(hm, qv, jnp.zeros_like(qv))
                    ec, ep, den, m = _dil_probs(qh, kc, kp, b_ref, gi * GRP_HEADS + i, valid_c, valid_p)
                    o = jnp.where(hm, (_nn(ec.astype(BF), vc) + _nn(ep.astype(BF), vp)) / den, o)
                    lse = jnp.where(hm, m + jnp.log(den), lse)
                o_ref[cur, lanes] = o
                l_ref[cur, lanes] = lse
            return carry

        lax.fori_loop(0, n_blk, nbody, 0)

    seqblk = lambda i: pl.BlockSpec((None, None, sub_len, D_GRP), lambda b, r: (i, b, 0, r))
    oblk = pl.BlockSpec((None, sub_len, D_GRP), lambda b, r: (b, 0, r))
    shp = _sds((nb, sub_len, dilation * D_GRP), F32)
    return pl.pallas_call(
        body, name="dil_fwd_%d" % dilation, grid=(nb, dilation),
        in_specs=[seqblk(3), seqblk(4), seqblk(5), _whole(bias)],
        out_specs=[oblk, oblk], out_shape=[shp, shp],
        compiler_params=_cp(2))(qkv6r, qkv6r, qkv6r, bias)


def _dil_bwd(qkv6r, bias, do_c, dd_c, nb, sub_len, dilation, carry=None):
    n_blk = sub_len // BLOCK

    def body(q_ref, k_ref, v_ref, b_ref, do_ref, dd_ref, out_ref, a_ref):
        heads, valid_c, valid_p0 = _dil_masks()
        first = jnp.logical_and(pl.program_id(0) == 0, pl.program_id(1) == 0)

        @pl.when(first)
        def _():
            a_ref[...] = jnp.zeros_like(a_ref)

        out_ref[1] = jnp.zeros((sub_len, D_GRP), F32)
        out_ref[2] = jnp.zeros((sub_len, D_GRP), F32)

        def nbody(n, carry):
            cur, prev = _dil_rows(n)
            valid_p = jnp.logical_and(valid_p0, n > 0)
            for gi in range(N_HEADS // GRP_HEADS):
                lanes = slice(gi * GRP_W, (gi + 1) * GRP_W)
                qv, kc, kp = q_ref[cur, lanes], k_ref[cur, lanes], k_ref[prev, lanes]
                vc, vp = v_ref[cur, lanes], v_ref[prev, lanes]
                dov, ddv = do_ref[cur, lanes], dd_ref[cur, lanes]
                dq = jnp.zeros((BLOCK, GRP_W), F32)
                dkc = jnp.zeros((BLOCK, GRP_W), F32)
                dkp = jnp.zeros((BLOCK, GRP_W), F32)
                dvc = jnp.zeros((BLOCK, GRP_W), F32)
                dvp = jnp.zeros((BLOCK, GRP_W), F32)
                for i, hm in enumerate(heads):
                    h = gi * GRP_HEADS + i
                    qh = jnp.where(hm, qv, jnp.zeros_like(qv))
                    doh = jnp.where(hm, dov, 0.0).astype(BF)
                    ddh = jnp.sum(jnp.where(hm, ddv, 0.0), axis=-1, keepdims=True) * (1.0 / HEAD_DIM)
                    ec, ep, den, _ = _dil_probs(qh, kc, kp, b_ref, h, valid_c, valid_p)
                    inv = 1.0 / den
                    pc = ec * inv
                    pp = ep * inv
                    dzc = pc * (_nt(doh, vc) + ddh)
                    dzp = pp * (_nt(doh, vp) + ddh)
                    a_ref[h, :, BLOCK:2 * BLOCK] += dzc
                    a_ref[h, :, 0:BLOCK] += dzp
                    dzcb = (dzc * SCALE).astype(BF)
                    dzpb = (dzp * SCALE).astype(BF)
                    dq = jnp.where(hm, _nn(dzcb, kc) + _nn(dzpb, kp), dq)
                    dkc = dkc + _tn(dzcb, qh)
                    dkp = dkp + _tn(dzpb, qh)
                    dvc = dvc + _tn(pc.astype(BF), doh)
                    dvp = dvp + _tn(pp.astype(BF), doh)
                out_ref[0, cur, lanes] = dq
                out_ref[1, cur, lanes] += dkc
                out_ref[1, prev, lanes] += dkp
                out_ref[2, cur, lanes] += dvc
                out_ref[2, prev, lanes] += dvp
            return carry

        lax.fori_loop(0, n_blk, nbody, 0)

    seqblk = lambda i: pl.BlockSpec((None, None, sub_len, D_GRP), lambda b, r: (i, b, 0, r))
    oblk = pl.BlockSpec((None, sub_len, D_GRP), lambda b, r: (b, 0, r))
    return _call(
        body, "dil_bwd_%d" % dilation, (nb, dilation),
        [seqblk(3), seqblk(4), seqblk(5), _whole(bias), oblk, oblk],
        [pl.BlockSpec((3, None, sub_len, D_GRP), lambda b, r: (0, b, 0, r)),
         pl.BlockSpec((N_HEADS, BLOCK, 2 * BLOCK), lambda b, r: (0, 0, 0))],
        [_sds((3, nb, sub_len, dilation * D_GRP), F32), _sds((N_HEADS, BLOCK, 2 * BLOCK), F32)],
        (qkv6r, qkv6r, qkv6r, bias, do_c, dd_c), carry=carry)


def _group_ones():
    idx = np.arange(D_GRP) // HEAD_DIM
    return jnp.asarray((idx[:, None] == idx[None, :]).astype(np.float32), dtype=BF)


def _dil_alphas(l1, l4, l16):
    mx = jnp.maximum(jnp.maximum(l1, l4), l16)
    e1 = jnp.exp(l1 - mx)
    e4 = jnp.exp(l4 - mx)
    e16 = jnp.exp(l16 - mx)
    den = e1 + e4 + e16
    return e1 / den, e4 / den, e16 / den


def _dil_comb(os, ls, g_dil, ones_g):
    t = os[0].shape[0]

    def body(o1, l1, o4, l4, o16, l16, g_ref, m_ref, o_ref, on_ref):
        a1, a4, a16 = _dil_alphas(l1[...], l4[...], l16[...])
        o = a1 * o1[...] + a4 * o4[...] + a16 * o16[...]
        o_ref[...] = o
        ms = _nn2(o * o, m_ref[...]) * (1.0 / HEAD_DIM)
        on_ref[...] = ((o * lax.rsqrt(ms + EPS)) * g_ref[...]).astype(BF)

    blk = pl.BlockSpec((TM, D_GRP), lambda m: (m, 0))
    return pl.pallas_call(
        body, name="dil_comb", grid=(t // TM,),
        in_specs=[blk] * 6 + [pl.BlockSpec((1, D_GRP), lambda m: (0, 0)),
                              pl.BlockSpec((D_GRP, D_GRP), lambda m: (0, 0))],
        out_specs=[blk, blk],
        out_shape=[_sds((t, D_GRP), F32), _sds((t, D_GRP), BF)],
        compiler_params=_cp(1))(os[0], ls[0], os[1], ls[1], os[2], ls[2], g_dil, ones_g)


def _dil_comb_bwd(do, os, ls, ones_g):
    t = do.shape[0]

    def body(do_ref, o1, l1, o4, l4, o16, l16, m_ref, d1, d4, d16, e1, e4, e16):
        dov = do_ref[...]
        a1, a4, a16 = _dil_alphas(l1[...], l4[...], l16[...])
        mv = m_ref[...]
        sbar = a1 * _nn2(dov * o1[...], mv) + a4 * _nn2(dov * o4[...], mv) + a16 * _nn2(dov * o16[...], mv)
        d1[...] = a1 * dov
        d4[...] = a4 * dov
        d16[...] = a16 * dov
        e1[...] = -a1 * sbar
        e4[...] = -a4 * sbar
        e16[...] = -a16 * sbar

    blk = pl.BlockSpec((TM, D_GRP), lambda m: (m, 0))
    shp = _sds((t, D_GRP), F32)
    return pl.pallas_call(
        body, name="dil_comb_bwd", grid=(t // TM,),
        in_specs=[blk] * 7 + [pl.BlockSpec((D_GRP, D_GRP), lambda m: (0, 0))],
        out_specs=[blk] * 6, out_shape=[shp] * 6,
        compiler_params=_cp(1))(do, os[0], ls[0], os[1], ls[1], os[2], ls[2], ones_g)


def _dqkv_dil_sum(d1, d4, d16, dqkv6):
    t = d1.shape[1]

    def body(a, b, c, alias, o_ref):
        del alias
        o_ref[...] = (a[...] + b[...] + c[...]).astype(BF)

    blk = pl.BlockSpec((3, TM, D_GRP), lambda m: (0, m, 0))
    return pl.pallas_call(
        body, name="dqkv_dil_sum", grid=(t // TM,),
        in_specs=[blk, blk, blk, pl.BlockSpec(memory_space=pl.ANY)],
        out_specs=pl.BlockSpec((3, TM, D_GRP), lambda m: (1, m, 0)),
        out_shape=_sds((6, t, D_GRP), BF), input_output_aliases={3: 0},
        compiler_params=_cp(1))(d1, d4, d16, dqkv6)


def _relbias_grad(a_all, onehot):
    def body(a_ref, oh_ref, o_ref):
        acc = jnp.zeros((N_HEADS, N_BUCKETS), F32)
        for c in range(len(DIL_CONFIGS)):
            av = a_ref[c]
            hi = av.astype(BF)
            lo = (av - hi.astype(F32)).astype(BF)
            acc = acc + _nt(hi, oh_ref[c]) + _nt(lo, oh_ref[c])
        o_ref[...] = acc

    return pl.pallas_call(body, name="relbias_grad", out_shape=_sds((N_HEADS, N_BUCKETS), F32),
                          compiler_params=_cp())(a_all, onehot)


def _headnorm_bwd(dn, o, gv, mv):
    ms = _nn2(o * o, mv) * (1.0 / HEAD_DIM)
    r = lax.rsqrt(ms + EPS)
    nrm = o * r
    dg = jnp.sum(dn * nrm, axis=0, keepdims=True)
    dnn = dn * gv
    do = r * (dnn - nrm * (_nn2(dnn * nrm, mv) * (1.0 / HEAD_DIM)))
    return do, dg


def _mix_bwd_out(dx, gt, tv, w_out, o_sb, o_dil, on_sb, on_dil, g_sb, g_dil, ones_g, seq):
    t, d = dx.shape
    per = seq // TM
    nb = t // seq

    def body(dx_ref, gt_ref, t_ref, w_ref, osb, odl, onsb, ondl, gsb, gdl, m_ref,
             dosb, dodl, dgt_ref, dgsb, dgdl, dw_ref):
        m = pl.program_id(0)
        dxv = dx_ref[...]
        dt = (gt_ref[...] * dxv).astype(BF)
        _acc_rows(dgt_ref, jnp.sum(dxv * t_ref[...], axis=0, keepdims=True), m % per == 0)
        mv = m_ref[...]
        don_sb = _nt(dt, w_ref[0:D_GRP, :])
        don_dl = _nt(dt, w_ref[D_GRP:2 * D_GRP, :])
        do1, dg1 = _headnorm_bwd(don_sb, osb[...], gsb[...], mv)
        do2, dg2 = _headnorm_bwd(don_dl, odl[...], gdl[...], mv)
        dosb[...] = do1
        dodl[...] = do2
        _acc_rows(dgsb, dg1, m == 0)
        _acc_rows(dgdl, dg2, m == 0)
        p1 = _tn(onsb[...], dt)
        p2 = _tn(ondl[...], dt)

        @pl.when(m == 0)
        def _():
            dw_ref[0:D_GRP, :] = p1
            dw_ref[D_GRP:2 * D_GRP, :] = p2

        @pl.when(m != 0)
        def _():
            dw_ref[0:D_GRP, :] += p1
            dw_ref[D_GRP:2 * D_GRP, :] += p2

    row = pl.BlockSpec((TM, d), lambda m: (m, 0))
    half = pl.BlockSpec((TM, D_GRP), lambda m: (m, 0))
    ex = pl.BlockSpec((None, 1, d), lambda m: (m // per, 0, 0))
    gvec = pl.BlockSpec((1, D_GRP), lambda m: (0, 0))
    wblk = pl.BlockSpec((2 * D_GRP, d), lambda m: (0, 0))
    return pl.pallas_call(
        body, name="mix_bwd_out", grid=(t // TM,),
        in_specs=[row, ex, row, wblk, half, half, half, half, gvec, gvec,
                  pl.BlockSpec((D_GRP, D_GRP), lambda m: (0, 0))],
        out_specs=[half, half, ex, gvec, gvec, wblk],
        out_shape=[_sds((t, D_GRP), F32), _sds((t, D_GRP), F32), _sds((nb, 1, d), F32),
                   _sds((1, D_GRP), F32), _sds((1, D_GRP), F32), _sds((2 * D_GRP, d), F32)],
        compiler_params=_cp(1))(dx, gt, tv, w_out, o_sb, o_dil, on_sb, on_dil, g_sb, g_dil, ones_g)


def _dw_in(h, dqkv6, carry=None):
    t, d = h.shape
    wc = 6 * D_GRP // N_CHIPS

    def body(h_ref, g_ref, o_ref):
        kt = pl.program_id(0)
        hv = h_ref[...]
        for j in range(N_CHIPS):
            p = _tn(hv, _chip_cols(g_ref, j, wc))

            @pl.when(kt == 0)
            def _(p=p, j=j):
                o_ref[j, 0] = p

            @pl.when(kt != 0)
            def _(p=p, j=j):
                o_ref[j, 0] += p

    return _call(
        body, "dw_in", (t // TM,),
        [pl.BlockSpec((TM, d), lambda kt: (kt, 0)), pl.BlockSpec((6, TM, D_GRP), lambda kt: (0, kt, 0))],
        [pl.BlockSpec((N_CHIPS, 1, d, wc), lambda kt: (0, 0, 0, 0))],
        [_sds((N_CHIPS, 1, d, wc), F32)], (h, dqkv6), carry=carry)


def _mix_bwd_dh(dqkv6, w_in, x, g, sc, dxo, seq, carry=None):
    _, t, _ = dqkv6.shape
    d = x.shape[-1]
    wc = w_in.shape[-1]
    per = seq // TM
    nb = t // seq

    def body(g6_ref, w_ref, x_ref, g_ref, sc_ref, dxo_ref, dx_ref, dsh_ref, dsc_ref, dg_ref):
        m = pl.program_id(0)
        dh = _nt(_chip_cols(g6_ref, 0, wc), w_ref[0, 0])
        for j in range(1, N_CHIPS):
            dh = dh + _nt(_chip_cols(g6_ref, j, wc), w_ref[j, 0])
        dx, dsh, dsc, dg = _modnorm_bwd_tile(dh, x_ref[...], g_ref[...], sc_ref[...], dxo_ref[...])
        dx_ref[...] = dx
        _acc_rows(dsh_ref, dsh, m % per == 0)
        _acc_rows(dsc_ref, dsc, m % per == 0)
        _acc_rows(dg_ref, dg, m == 0)

    row = pl.BlockSpec((TM, d), lambda m: (m, 0))
    ex = pl.BlockSpec((None, 1, d), lambda m: (m // per, 0, 0))
    vec = pl.BlockSpec((1, d), lambda m: (0, 0))
    return _call(
        body, "mix_bwd_dh", (t // TM,),
        [pl.BlockSpec((6, TM, D_GRP), lambda m: (0, m, 0)), _whole(w_in), row, vec, ex, row],
        [row, ex, ex, vec],
        [_sds((t, d), F32), _sds((nb, 1, d), F32), _sds((nb, 1, d), F32), _sds((1, d), F32)],
        (dqkv6, w_in, x, g, sc, dxo), carry=carry)


def _final_loss(x, g, target):
    t, d = x.shape
    steps = t // TM

    def body(x_ref, g_ref, t_ref, dx_ref, dg_ref, loss_ref, lacc):
        m = pl.program_id(0)
        xv = x_ref[...]
        gv = g_ref[...]
        r = lax.rsqrt(jnp.mean(xv * xv, axis=-1, keepdims=True) + EPS)
        n = xv * r
        err = n * gv - t_ref[...]
        dy = err * (1.0 / d)
        _acc_rows(dg_ref, jnp.sum(dy * n, axis=0, keepdims=True), m == 0)
        dn = dy * gv
        dx_ref[...] = r * (dn - n * jnp.mean(dn * n, axis=-1, keepdims=True))
        _acc_rows(lacc, jnp.sum(err * err, axis=0, keepdims=True), m == 0)

        @pl.when(m == steps - 1)
        def _():
            tot = jnp.sum(lacc[...], axis=-1, keepdims=True) * (0.5 / d)
            loss_ref[...] = jnp.broadcast_to(tot, (1, 128))

    row = pl.BlockSpec((TM, d), lambda m: (m, 0))
    vec = pl.BlockSpec((1, d), lambda m: (0, 0))
    return pl.pallas_call(
        body, name="final_loss", grid=(steps,),
        in_specs=[row, vec, row],
        out_specs=[row, vec, pl.BlockSpec((1, 128), lambda m: (0, 0))],
        out_shape=[_sds((t, d), F32), _sds((1, d), F32), _sds((1, 128), F32)],
        scratch_shapes=[pltpu.VMEM((1, d), F32)],
        compiler_params=_cp(1))(x, g, target)


def _row_tile(rows, cols):
    best = rows
    for tr in range(8, rows + 1, 8):
        if rows % tr == 0 and tr * cols * 4 <= (1 << 20):
            best = tr
    if best * cols * 4 > (1 << 21):
        best = 8
    return best


def _adamw(w, g_arr, g_sel, m, v):
    rows, cols = w.shape
    tr = _row_tile(rows, cols)
    b1c = 1.0 - ADAM_B1 ** ADAM_STEP
    b2c = 1.0 - ADAM_B2 ** ADAM_STEP

    def body(w_ref, g_ref, m_ref, v_ref, go_ref, d_ref, mo_ref, vo_ref):
        gv = g_ref[...]
        mn = ADAM_B1 * m_ref[...] + (1.0 - ADAM_B1) * gv
        vn = ADAM_B2 * v_ref[...] + (1.0 - ADAM_B2) * (gv * gv)
        go_ref[...] = gv
        mo_ref[...] = mn
        vo_ref[...] = vn
        d_ref[...] = -ADAM_LR * ((mn / b1c) / (jnp.sqrt(vn / b2c) + ADAM_EPS) + ADAM_WD * w_ref[...])

    blk = pl.BlockSpec((tr, cols), lambda i: (i, 0))
    shp = _sds((rows, cols), F32)
    return pl.pallas_call(
        body, name="adamw", grid=(rows // tr,),
        in_specs=[blk, pl.BlockSpec((None, tr, cols), lambda i: (g_sel, i, 0)), blk, blk],
        out_specs=[blk] * 4, out_shape=[shp] * 4,
        compiler_params=_cp(1))(w, g_arr, m, v)


def _flip(v, bit):
    return 1 - v if bit else v


def _my_place():
    x, y, c = lax.axis_index("x"), lax.axis_index("y"), lax.axis_index("c")
    return x, y, c


class _Exchange:
    def __init__(self, operands, out_shape, aliases, sems, start, finish):
        self.operands, self.out_shape, self.aliases, self.sems = list(operands), list(out_shape), dict(aliases), list(sems)
        self.start, self.finish = start, finish


def _join(exchanges):
    exchanges = [e for e in exchanges if e is not None]
    if not exchanges:
        return None
    ops, outs, sems, aliases, spans = [], [], [], {}, []
    for e in exchanges:
        spans.append((len(ops), len(outs), len(sems), e))
        for i, j in e.aliases.items():
            aliases[len(ops) + i] = len(outs) + j
        ops += e.operands
        outs += e.out_shape
        sems += e.sems

    def run(which):
        def go(ins, res, sm):
            for io, oo, so, e in spans:
                getattr(e, which)(ins[io:io + len(e.operands)], res[oo:oo + len(e.out_shape)], sm[so:so + len(e.sems)])
        return go

    return _Exchange(ops, outs, aliases, sems, run("start"), run("finish"))


def _call(body, name, grid, in_specs, out_specs, out_shape, args, scratch=(), carry=None):
    in_specs, out_specs, out_shape, scratch = list(in_specs), list(out_specs), list(out_shape), list(scratch)
    if carry is None:
        return pl.pallas_call(body, name=name, grid=grid, in_specs=in_specs, out_specs=out_specs,
                              out_shape=out_shape, scratch_shapes=scratch,
                              compiler_params=_cp(len(grid)))(*args)
    n_in, n_out, n_s = len(in_specs), len(out_specs), len(scratch)
    c_in, c_out = len(carry.operands), len(carry.out_shape)
    any_spec = pl.BlockSpec(memory_space=pl.ANY)

    def wrapped(*refs):
        ins, cins = refs[:n_in], refs[n_in:n_in + c_in]
        o0 = n_in + c_in
        outs, couts = refs[o0:o0 + n_out], refs[o0 + n_out:o0 + n_out + c_out]
        s0 = o0 + n_out + c_out
        scr, sems = refs[s0:s0 + n_s], refs[s0 + n_s:]
        first = pl.program_id(0) == 0
        last = pl.program_id(0) == grid[0] - 1
        for ax in range(1, len(grid)):
            first = jnp.logical_and(first, pl.program_id(ax) == 0)
            last = jnp.logical_and(last, pl.program_id(ax) == grid[ax] - 1)

        @pl.when(first)
        def _():
            carry.start(cins, couts, sems)

        body(*ins, *outs, *scr)

        @pl.when(last)
        def _():
            carry.finish(cins, couts, sems)

    return pl.pallas_call(
        wrapped, name=name, grid=grid, in_specs=in_specs + [any_spec] * c_in,
        out_specs=out_specs + [any_spec] * c_out, out_shape=out_shape + carry.out_shape,
        scratch_shapes=scratch + carry.sems,
        input_output_aliases={n_in + i: n_out + j for i, j in carry.aliases.items()},
        compiler_params=_cp(len(grid)))(*args, *carry.operands)


def _alone(name, ex):
    any_spec = pl.BlockSpec(memory_space=pl.ANY)
    c_in, c_out = len(ex.operands), len(ex.out_shape)

    def body(*refs):
        ins, outs, sems = refs[:c_in], refs[c_in:c_in + c_out], refs[c_in + c_out:]
        ex.start(ins, outs, sems)
        ex.finish(ins, outs, sems)

    return pl.pallas_call(
        body, name=name, in_specs=[any_spec] * c_in, out_specs=[any_spec] * c_out, out_shape=ex.out_shape,
        scratch_shapes=ex.sems, input_output_aliases=ex.aliases, compiler_params=_cp())(*ex.operands)


def _ada_fwd(c_pad, w_ada, b_shard):
    d = c_pad.shape[-1]
    cols = w_ada.shape[-1]
    chunk = 384

    def body(c_ref, w_ref, b_ref, call_ref, mod_ref, part, s1, r1, s2, r2):
        x, y, c = _my_place()
        dev = 4 * x + 2 * y + c
        chip = 2 * x + y
        call_ref[dev] = c_ref[...]

        def c_copy(k):
            px, py, pc = _flip(x, (k >> 2) & 1), _flip(y, (k >> 1) & 1), _flip(c, k & 1)
            return px, py, pc

        sends = []
        for k in range(1, N_DEV):
            px, py, pc = c_copy(k)
            cp = pltpu.make_async_remote_copy(src_ref=c_ref, dst_ref=call_ref.at[dev], send_sem=s1.at[k - 1],
                                              recv_sem=r1.at[k - 1], device_id=(px, py, pc), device_id_type=MESH)
            cp.start()
            sends.append(cp)
        for k in range(1, N_DEV):
            px, py, pc = c_copy(k)
            pltpu.make_async_remote_copy(src_ref=c_ref, dst_ref=call_ref.at[4 * px + 2 * py + pc],
                                         send_sem=s1.at[k - 1], recv_sem=r1.at[k - 1],
                                         device_id=(px, py, pc), device_id_type=MESH).wait_recv()
        for cp in sends:
            cp.wait_send()

        cs = call_ref[...].reshape(N_DEV * 8, d)
        sc = (cs * jax.nn.sigmoid(cs)).astype(BF)
        for n0 in range(0, cols, chunk):
            blk = _nn(sc, w_ref[:, n0:n0 + chunk].astype(BF)) + b_ref[:, n0:n0 + chunk]
            part[:, :, n0:n0 + chunk] = blk.reshape(N_DEV, 8, chunk)

        mod_ref[chip] = part[dev]
        sends = []
        for kk in range(1, N_CHIPS):
            px, py = _flip(x, (kk >> 1) & 1), _flip(y, kk & 1)
            cp = pltpu.make_async_remote_copy(src_ref=part.at[4 * px + 2 * py + c], dst_ref=mod_ref.at[chip],
                                              send_sem=s2.at[kk - 1], recv_sem=r2.at[kk - 1],
                                              device_id=(px, py, c), device_id_type=MESH)
            cp.start()
            sends.append(cp)
        for kk in range(1, N_CHIPS):
            px, py = _flip(x, (kk >> 1) & 1), _flip(y, kk & 1)
            pltpu.make_async_remote_copy(src_ref=part.at[dev], dst_ref=mod_ref.at[2 * px + py],
                                         send_sem=s2.at[kk - 1], recv_sem=r2.at[kk - 1],
                                         device_id=(px, py, c), device_id_type=MESH).wait_recv()
        for cp in sends:
            cp.wait_send()

    return pl.pallas_call(
        body, name="ada_fwd",
        out_shape=[_sds((N_DEV, 8, d), F32), _sds((N_CHIPS, 8, cols), F32)],
        scratch_shapes=[pltpu.VMEM((N_DEV, 8, cols), F32),
                        pltpu.SemaphoreType.DMA((N_DEV - 1,)), pltpu.SemaphoreType.DMA((N_DEV - 1,)),
                        pltpu.SemaphoreType.DMA((N_CHIPS - 1,)), pltpu.SemaphoreType.DMA((N_CHIPS - 1,))],
        compiler_params=_cp())(c_pad, w_ada, b_shard)


def _ag_weights(bufs):
    n = len(bufs)

    def place():
        x, y, c = _my_place()
        others = [(_flip(x, (kk >> 1) & 1), _flip(y, kk & 1)) for kk in range(1, N_CHIPS)]
        return x, y, c, 2 * x + y, others

    def half(b, which):
        hr = bufs[b].shape[2] // 2
        return pl.ds(pl.multiple_of(which * hr, 16), hr)

    def ici(outs, sems, b, i, slot, x, y, c, px, py):
        rows = outs[b].at[slot, :, half(b, c), :]
        return pltpu.make_async_remote_copy(
            src_ref=rows, dst_ref=rows, send_sem=sems[0].at[3 * b + i], recv_sem=sems[1].at[3 * b + i],
            device_id=(px, py, c), device_id_type=MESH)

    def d2d(outs, sems, b, i, slot, x, y, c, which):
        rows = outs[b].at[slot, :, half(b, which), :]
        return pltpu.make_async_remote_copy(
            src_ref=rows, dst_ref=rows, send_sem=sems[2].at[3 * b + i], recv_sem=sems[3].at[3 * b + i],
            device_id=(x, y, 1 - c), device_id_type=MESH)

    def start(ins, outs, sems):
        x, y, c, chip, others = place()
        for b in range(n):
            for i, (px, py) in enumerate(others):
                ici(outs, sems, b, i, chip, x, y, c, px, py).start()

    def finish(ins, outs, sems):
        x, y, c, chip, others = place()
        for b in range(n):
            for i, (px, py) in enumerate(others):
                ici(outs, sems, b, i, 2 * px + py, x, y, c, px, py).wait_recv()
                d2d(outs, sems, b, i, 2 * px + py, x, y, c, c).start()
        for b in range(n):
            for i, (px, py) in enumerate(others):
                d2d(outs, sems, b, i, 2 * px + py, x, y, c, 1 - c).wait_recv()
        for b in range(n):
            for i, (px, py) in enumerate(others):
                ici(outs, sems, b, i, chip, x, y, c, px, py).wait_send()
                d2d(outs, sems, b, i, 2 * px + py, x, y, c, c).wait_send()

    return _Exchange(bufs, [_sds(s.shape, s.dtype) for s in bufs], {i: i for i in range(n)},
                     [pltpu.SemaphoreType.DMA((3 * n,))] * 4, start, finish)


def _rs_d2d(grads):
    n = len(grads)

    def copy(ins, outs, sems, b):
        x, y, c = _my_place()
        hr = grads[b].shape[2] // 2
        theirs = pl.ds(pl.multiple_of((1 - c) * hr, 8), hr)
        return pltpu.make_async_remote_copy(
            src_ref=ins[b].at[:, :, theirs, :], dst_ref=outs[b], send_sem=sems[0].at[b], recv_sem=sems[1].at[b],
            device_id=(x, y, 1 - c), device_id_type=MESH)

    def start(ins, outs, sems):
        for b in range(n):
            copy(ins, outs, sems, b).start()

    def finish(ins, outs, sems):
        for b in range(n):
            copy(ins, outs, sems, b).wait()

    return _Exchange(grads, [_sds(g.shape[:2] + (g.shape[2] // 2, g.shape[3]), F32) for g in grads], {},
                     [pltpu.SemaphoreType.DMA((n,))] * 2, start, finish)


def _add_halves(core, g, land):
    nchip, ng, rows, cols = g.shape
    hr = rows // 2
    tr = _row_tile(hr, cols)
    steps = hr // tr

    def body(core_ref, g_ref, l_ref, o_ref):
        del core_ref
        o_ref[...] = (g_ref[...] + l_ref[...]).astype(BF)

    return pl.pallas_call(
        body, name="add_halves",
        grid_spec=pltpu.PrefetchScalarGridSpec(
            num_scalar_prefetch=1, grid=(nchip, ng, steps),
            in_specs=[pl.BlockSpec((None, None, tr, cols), lambda j, a, i, cr: (j, a, cr[0] * steps + i, 0)),
                      pl.BlockSpec((None, None, tr, cols), lambda j, a, i, cr: (j, a, i, 0))],
            out_specs=pl.BlockSpec((None, None, tr, cols), lambda j, a, i, cr: (j, a, i, 0))),
        out_shape=_sds((nchip, ng, hr, cols), BF),
        compiler_params=_cp(3))(core, g, land)


def _rs_ici(parts):
    n = len(parts)

    def copies(ins, outs, sems):
        x, y, c = _my_place()
        chip = 2 * x + y
        for b in range(n):
            for kk in range(1, N_CHIPS):
                px, py = _flip(x, (kk >> 1) & 1), _flip(y, kk & 1)
                k = 3 * b + kk - 1
                send = pltpu.make_async_remote_copy(
                    src_ref=ins[b].at[2 * px + py], dst_ref=outs[b].at[chip],
                    send_sem=sems[0].at[k], recv_sem=sems[1].at[k], device_id=(px, py, c), device_id_type=MESH)
                slot = outs[b].at[2 * px + py]
                recv = pltpu.make_async_remote_copy(
                    src_ref=slot, dst_ref=slot, send_sem=sems[0].at[k], recv_sem=sems[1].at[k],
                    device_id=(px, py, c), device_id_type=MESH)
                yield send, recv

    def start(ins, outs, sems):
        for send, _ in copies(ins, outs, sems):
            send.start()

    def finish(ins, outs, sems):
        for send, recv in copies(ins, outs, sems):
            recv.wait_recv()
            send.wait_send()

    return _Exchange(parts, [_sds(p.shape, p.dtype) for p in parts], {},
                     [pltpu.SemaphoreType.DMA((3 * n,))] * 2, start, finish)


def _sum_chips(place, part, land):
    nchip, ng, hr, cols = land.shape
    tr = _row_tile(hr, cols)
    steps = hr // tr

    def body(place_ref, p_ref, l1, l2, l3, o_ref):
        del place_ref
        o_ref[...] = ((p_ref[...].astype(F32) + l1[...].astype(F32)) + l2[...].astype(F32)) + l3[...].astype(F32)

    def slot(k):
        return pl.BlockSpec((None, None, tr, cols), lambda a, i, pr: (jnp.bitwise_xor(pr[1], k), a, i, 0))

    return pl.pallas_call(
        body, name="sum_chips",
        grid_spec=pltpu.PrefetchScalarGridSpec(
            num_scalar_prefetch=1, grid=(ng, steps),
            in_specs=[slot(0), slot(1), slot(2), slot(3)],
            out_specs=pl.BlockSpec((None, tr, cols), lambda a, i, pr: (a, pr[0] * steps + i, 0))),
        out_shape=_sds((ng, 2 * hr, cols), F32),
        compiler_params=_cp(2))(place, part, land, land, land)


def _rs_final(bufs):
    n = len(bufs)

    def copy(outs, sems, b, which):
        x, y, c = _my_place()
        hr = bufs[b].shape[1] // 2
        rows = outs[b].at[:, pl.ds(pl.multiple_of((c if which == 0 else 1 - c) * hr, 8), hr), :]
        return pltpu.make_async_remote_copy(
            src_ref=rows, dst_ref=rows, send_sem=sems[0].at[b], recv_sem=sems[1].at[b],
            device_id=(x, y, 1 - c), device_id_type=MESH)

    def start(ins, outs, sems):
        for b in range(n):
            copy(outs, sems, b, 0).start()

    def finish(ins, outs, sems):
        for b in range(n):
            copy(outs, sems, b, 0).wait_send()
            copy(outs, sems, b, 1).wait_recv()

    return _Exchange(bufs, [_sds(h.shape, F32) for h in bufs], {i: i for i in range(n)},
                     [pltpu.SemaphoreType.DMA((n,))] * 2, start, finish)


def _small_sync(smalls, dmod_blk, c_all):
    d = c_all.shape[-1]
    cols = dmod_blk.shape[-1]
    chunk = 384

    def body(sm_ref, dm_ref, c_ref, sum_ref, gw_ref, sm_all, dm_all, ssem, rsem):
        x, y, c = _my_place()
        dev = 4 * x + 2 * y + c
        chip = 2 * x + y
        sm_all[dev] = sm_ref[...]
        dm_all[dev] = dm_ref[chip]
        sends = []
        for k in range(1, N_DEV):
            px, py, pc = _flip(x, (k >> 2) & 1), _flip(y, (k >> 1) & 1), _flip(c, k & 1)
            a = pltpu.make_async_remote_copy(src_ref=sm_ref, dst_ref=sm_all.at[dev], send_sem=ssem.at[2 * (k - 1)],
                                             recv_sem=rsem.at[2 * (k - 1)], device_id=(px, py, pc),
                                             device_id_type=MESH)
            b = pltpu.make_async_remote_copy(src_ref=dm_ref.at[2 * px + py], dst_ref=dm_all.at[dev],
                                             send_sem=ssem.at[2 * (k - 1) + 1], recv_sem=rsem.at[2 * (k - 1) + 1],
                                             device_id=(px, py, pc), device_id_type=MESH)
            a.start()
            b.start()
            sends += [a, b]
        for k in range(1, N_DEV):
            px, py, pc = _flip(x, (k >> 2) & 1), _flip(y, (k >> 1) & 1), _flip(c, k & 1)
            pdev = 4 * px + 2 * py + pc
            pltpu.make_async_remote_copy(src_ref=sm_ref, dst_ref=sm_all.at[pdev], send_sem=ssem.at[2 * (k - 1)],
                                         recv_sem=rsem.at[2 * (k - 1)], device_id=(px, py, pc),
                                         device_id_type=MESH).wait_recv()
            pltpu.make_async_remote_copy(src_ref=dm_ref.at[chip], dst_ref=dm_all.at[pdev],
                                         send_sem=ssem.at[2 * (k - 1) + 1], recv_sem=rsem.at[2 * (k - 1) + 1],
                                         device_id=(px, py, pc), device_id_type=MESH).wait_recv()
        for cp in sends:
            cp.wait_send()

        tot = sm_all[0]
        for q in range(1, N_DEV):
            tot = tot + sm_all[q]
        sum_ref[...] = tot

        cs = c_ref[...].reshape(N_DEV * 8, d)
        sc = (cs * jax.nn.sigmoid(cs)).astype(BF)
        for n0 in range(0, cols, chunk):
            dmv = dm_all[:, :, n0:n0 + chunk].reshape(N_DEV * 8, chunk).astype(BF)
            gw_ref[:, n0:n0 + chunk] = _tn(sc, dmv)

    return pl.pallas_call(
        body, name="small_sync",
        out_shape=[_sds(smalls.shape, F32), _sds((d, cols), F32)],
        scratch_shapes=[pltpu.VMEM((N_DEV,) + smalls.shape, F32), pltpu.VMEM((N_DEV, 8, cols), F32),
                        pltpu.SemaphoreType.DMA((2 * (N_DEV - 1),)), pltpu.SemaphoreType.DMA((2 * (N_DEV - 1),))],
        compiler_params=_cp())(smalls, dmod_blk, c_all)


def _bucket_onehot():
    maps = np.stack([_bucket_map(dil).reshape(-1) for _, dil in DIL_CONFIGS])
    return (jnp.asarray(maps)[:, None, :] == jnp.arange(N_BUCKETS, dtype=jnp.int32)[None, :, None]).astype(BF)


def _dil_bias(rel_t, onehot):
    def body(r_ref, oh_ref, o_ref):
        rv = r_ref[...]
        hi = rv.astype(BF)
        lo = (rv - hi.astype(F32)).astype(BF)
        for c in range(len(DIL_CONFIGS)):
            o_ref[c] = _nn(hi, oh_ref[c]) + _nn(lo, oh_ref[c])

    return pl.pallas_call(body, name="dil_bias",
                          out_shape=_sds((len(DIL_CONFIGS), N_HEADS, BLOCK * 2 * BLOCK), F32),
                          compiler_params=_cp())(rel_t, onehot)


def _rowsum8(a):
    def body(a_ref, o_ref):
        o_ref[...] = jnp.sum(a_ref[...], axis=0, keepdims=True)

    return pl.pallas_call(body, name="rowsum8", out_shape=_sds((1, a.shape[1]), F32), compiler_params=_cp())(a)


def _local_step(x, mod, target, w, gains, rel_bias, place=None):
    nb, seq, d = x.shape
    t = nb * seq
    dist = place is not None
    core = place[0:1] if dist else None
    x0 = x.reshape(t, d)
    tgt = target.reshape(t, d)
    md = [mod[:, i:i + 1, :] for i in range(N_MOD)]
    sh1, sc1, gt1, sh2, sc2, gt2, sh3, sc3, gt3 = md
    g1, g2, g3 = gains["g_ffn1"], gains["g_mix"], gains["g_ffn2"]
    ones_g = _group_ones()

    def partial_sums(grads, lands):
        return [_add_halves(core, g, l) for g, l in zip(grads, lands)]

    def chip_sums(parts, lands):
        return [_sum_chips(place, p, l) for p, l in zip(parts, lands)]

    h1 = _modnorm(x0, g1, sc1, sh1, seq)
    res = _ffn_up(h1, w["gu1"], carry=_ag_weights([w["d1"], w["win"], w["wout"]]) if dist else None)
    a1, u1, s1 = res[:3]
    wd1, w_in, w_out = res[3:] if dist else (w["d1"], w["win"], w["wout"])
    w_out2 = w_out.reshape(2 * D_GRP, d)
    f1, x1 = _ffn_down(s1, wd1, x0, gt1, seq, 0.5)

    h2 = _modnorm(x1, g2, sc2, sh2, seq)
    qkv6 = _qkv_proj(h2, w_in)[0]
    qkv6b = qkv6.reshape(6, nb, seq, D_GRP)
    res = _sb_fwd(qkv6b, gains["g_sb_out"], nb, seq, carry=_ag_weights([w["gu2"], w["d2"]]) if dist else None)
    o_sb, on_sb = res[:2]
    wgu2, wd2 = res[2:] if dist else (w["gu2"], w["d2"])
    onehot = _bucket_onehot()
    bias = _dil_bias(rel_bias.T, onehot).reshape(len(DIL_CONFIGS), N_HEADS, BLOCK, 2 * BLOCK)
    o_cs, l_cs, qkv_rs = [], [], []
    for ci, (_, dil) in enumerate(DIL_CONFIGS):
        sub = seq // dil
        qr = qkv6.reshape(6, nb, sub, dil * D_GRP)
        o_c, l_c = _dil_fwd(qr, bias[ci], nb, sub, dil)
        qkv_rs.append(qr)
        o_cs.append(o_c.reshape(t, D_GRP))
        l_cs.append(l_c.reshape(t, D_GRP))
    o_dil, on_dil = _dil_comb(o_cs, l_cs, gains["g_dil_out"], ones_g)
    tmix, x2 = _mix_out(on_sb.reshape(t, D_GRP), on_dil, w_out2, x1, gt2, seq)

    h3 = _modnorm(x2, g3, sc3, sh3, seq)
    a3, u3, s3 = _ffn_up(h3, wgu2)
    f3, x3 = _ffn_down(s3, wd2, x2, gt3, seq, 0.5)

    dx3, dg_final, loss = _final_loss(x3, gains["g_final"], tgt)

    da3, du3, df3, dgt3 = _ffn_bwd_ds(dx3, gt3, f3, wd2, a3, u3, seq, 0.5)
    grads2 = _ffn_bwd_w(h3, da3, du3, s3, df3)
    res = _ffn_bwd_dh(da3, du3, wgu2, x2, g3, sc3, dx3, seq, carry=_rs_d2d(grads2) if dist else None)
    dx2, dsh3, dsc3, dg3 = res[:4]
    parts2 = partial_sums(grads2, res[4:]) if dist else None

    do_sb, do_dil, dgt2, dg_sb, dg_dil, dw_out = _mix_bwd_out(
        dx2, gt2, tmix, w_out2, o_sb.reshape(t, D_GRP), o_dil, on_sb.reshape(t, D_GRP), on_dil,
        gains["g_sb_out"], gains["g_dil_out"], ones_g, seq)
    dw_out = dw_out.reshape(N_CHIPS, 1, 2 * D_GRP // N_CHIPS, d)
    res = _sb_bwd(qkv6b, do_sb.reshape(nb, seq, D_GRP), nb, seq, carry=_rs_ici(parts2) if dist else None)
    dqkv6 = res[0]
    halves2 = chip_sums(parts2, res[1:]) if dist else None
    dcs = _dil_comb_bwd(do_dil, o_cs, l_cs, ones_g)
    dsum, a_tiles = [], []
    for ci, (_, dil) in enumerate(DIL_CONFIGS):
        sub = seq // dil
        do_c = dcs[ci].reshape(nb, sub, dil * D_GRP)
        dd_c = dcs[3 + ci].reshape(nb, sub, dil * D_GRP)
        res = _dil_bwd(qkv_rs[ci], bias[ci], do_c, dd_c, nb, sub, dil,
                       carry=_rs_final(halves2) if dist and ci == 0 else None)
        if dist and ci == 0:
            grads2 = res[2:]
        dsum.append(res[0].reshape(3, t, D_GRP))
        a_tiles.append(res[1].reshape(N_HEADS, BLOCK * 2 * BLOCK))
    dqkv6 = _dqkv_dil_sum(dsum[0], dsum[1], dsum[2], dqkv6.reshape(6, t, D_GRP))
    drel = _relbias_grad(jnp.stack(a_tiles), onehot)
    dx1, dsh2, dsc2, dg2 = _mix_bwd_dh(dqkv6, w_in, x1, g2, sc2, dx2, seq)

    da1, du1, df1, dgt1 = _ffn_bwd_ds(dx1, gt1, f1, wd1, a1, u1, seq, 0.5)
    grads1 = _ffn_bwd_w(h1, da1, du1, s1, df1)
    res = _dw_in(h2, dqkv6, carry=_rs_d2d(grads1) if dist else None)
    grads_m = [res[0], dw_out]
    parts1 = partial_sums(grads1, res[1:]) if dist else None
    res = _ffn_bwd_dh(da1, du1, w["gu1"], x0, g1, sc1, dx1, seq,
                      carry=_join([_rs_ici(parts1), _rs_d2d(grads_m)]) if dist else None)
    dx0, dsh1, dsc1, dg1 = res[:4]
    if dist:
        halves1 = chip_sums(parts1, res[4:6])
        parts_m = partial_sums(grads_m, res[6:8])
        res = _alone("rs_tail", _join([_rs_final(halves1), _rs_ici(parts_m)]))
        grads1 = res[:2]
        grads_m = _alone("rs_last", _rs_final(chip_sums(parts_m, res[2:4])))

    dmod = jnp.concatenate([dsh1, dsc1, dgt1, dsh2, dsc2, dgt2, dsh3, dsc3, dgt3], axis=1)
    return dict(grad_x=dx0.reshape(nb, seq, d), loss=loss[0, 0], dmod=dmod.reshape(nb, N_MOD * d),
                dgu1=grads1[0], dwd1=grads1[1], dgu2=grads2[0], dwd2=grads2[1], dwin=grads_m[0], dwout=grads_m[1],
                dg_ffn1=dg1, dg_mix=dg2, dg_ffn2=dg3, dg_final=dg_final, dg_sb=dg_sb, dg_dil=dg_dil,
                drel=drel.T)


_SMALL_ORDER = (("b_ada", N_MOD * 1024), ("g_ffn1", 1024), ("g_mix", 1024), ("g_ffn2", 1024), ("g_final", 1024),
                ("g_sb_out", D_GRP), ("g_dil_out", D_GRP), ("rel_bias", N_BUCKETS * N_HEADS))


def _pack_small(parts, extra=None):
    flat = [parts[name].reshape(-1).astype(F32) for name, _ in _SMALL_ORDER]
    used = sum(sz for _, sz in _SMALL_ORDER)
    pad = SMALL_ROWS * 128 - used
    tail = jnp.zeros((pad,), F32)
    if extra is not None:
        tail = tail.at[0].set(extra)
    return jnp.concatenate(flat + [tail]).reshape(SMALL_ROWS, 128)


def _unpack_small(packed, shapes):
    flat = packed.reshape(-1)
    out, off = {}, 0
    for name, sz in _SMALL_ORDER:
        out[name] = flat[off:off + sz].reshape(shapes[name])
        off += sz
    return out, flat[off]


def kernel(x, c, w_ada, b_ada, g_ffn1, w1_gate, w1_up, w1_down, g_mix, w_in, g_sb_out, g_dil_out, w_out, rel_bias, g_ffn2, w2_gate, w2_up, w2_down, g_final, loss_target, m_w_ada, m_b_ada, m_g_ffn1, m_w1_gate, m_w1_up, m_w1_down, m_g_mix, m_w_in, m_g_sb_out, m_g_dil_out, m_w_out, m_rel_bias, m_g_ffn2, m_w2_gate, m_w2_up, m_w2_down, m_g_final, v_w_ada, v_b_ada, v_g_ffn1, v_w1_gate, v_w1_up, v_w1_down, v_g_mix, v_w_in, v_g_sb_out, v_g_dil_out, v_w_out, v_rel_bias, v_g_ffn2, v_w2_gate, v_w2_up, v_w2_down, v_g_final):
    nb, seq, d = x.shape
    xi, yi, ci = lax.axis_index("x"), lax.axis_index("y"), lax.axis_index("c")
    chip = 2 * xi + yi
    ada_cols = w_ada.shape[-1]

    c_pad = jnp.zeros((8, d), F32).at[:nb].set(c)
    b_shard = lax.dynamic_slice(b_ada, (0, chip * ada_cols), (1, ada_cols))
    c_all, mod_blk = _ada_fwd(c_pad, w_ada[0], b_shard)
    mod = jnp.transpose(mod_blk[:, :nb, :], (1, 0, 2)).reshape(nb, N_MOD, d)

    shards = dict(gu1=jnp.stack([w1_gate[0], w1_up[0]]), d1=w1_down, win=w_in, wout=w_out,
                  gu2=jnp.stack([w2_gate[0], w2_up[0]]), d2=w2_down)
    bufs = {k: lax.dynamic_update_slice(lax.empty((N_CHIPS,) + s.shape, BF), s.astype(BF)[None], (chip, 0, 0, 0))
            for k, s in shards.items()}
    bufs["gu1"] = _alone("ag_first", _ag_weights([bufs["gu1"]]))[0]

    gains = dict(g_ffn1=g_ffn1, g_mix=g_mix, g_ffn2=g_ffn2, g_final=g_final.reshape(1, d),
                 g_sb_out=g_sb_out.reshape(1, D_GRP), g_dil_out=g_dil_out.reshape(1, D_GRP))
    place = jnp.stack([ci, chip]).astype(jnp.int32)
    r = _local_step(x, mod, loss_target, bufs, gains, rel_bias, place)
    gu1, gu2, gd1, gd2, gwin, gwout = r["dgu1"], r["dgu2"], r["dwd1"], r["dwd2"], r["dwin"], r["dwout"]

    dmod = r["dmod"]
    dmod_pad = jnp.zeros((8, N_MOD * d), F32).at[:nb].set(dmod)
    dmod_blk = jnp.transpose(dmod_pad.reshape(8, N_CHIPS, ada_cols), (1, 0, 2))
    small_parts = dict(b_ada=_rowsum8(dmod_pad), g_ffn1=r["dg_ffn1"], g_mix=r["dg_mix"], g_ffn2=r["dg_ffn2"],
                       g_final=r["dg_final"], g_sb_out=r["dg_sb"], g_dil_out=r["dg_dil"], rel_bias=r["drel"])
    small_sum, g_wada = _small_sync(_pack_small(small_parts, r["loss"]), dmod_blk, c_all)

    small_w = dict(b_ada=b_ada, g_ffn1=g_ffn1, g_mix=g_mix, g_ffn2=g_ffn2, g_final=g_final,
                   g_sb_out=g_sb_out, g_dil_out=g_dil_out, rel_bias=rel_bias)
    small_m = dict(b_ada=m_b_ada, g_ffn1=m_g_ffn1, g_mix=m_g_mix, g_ffn2=m_g_ffn2, g_final=m_g_final,
                   g_sb_out=m_g_sb_out, g_dil_out=m_g_dil_out, rel_bias=m_rel_bias)
    small_v = dict(b_ada=v_b_ada, g_ffn1=v_g_ffn1, g_mix=v_g_mix, g_ffn2=v_g_ffn2, g_final=v_g_final,
                   g_sb_out=v_g_sb_out, g_dil_out=v_g_dil_out, rel_bias=v_rel_bias)
    shapes = {k: v.shape for k, v in small_w.items()}
    sg, sd, sm, sv = _adamw(_pack_small(small_w), small_sum.reshape(1, SMALL_ROWS, 128), 0,
                            _pack_small(small_m), _pack_small(small_v))
    sg, loss = _unpack_small(sg, shapes)
    sd, _ = _unpack_small(sd, shapes)
    sm, _ = _unpack_small(sm, shapes)
    sv, _ = _unpack_small(sv, shapes)

    big = {}

    def upd(name, w, g_arr, sel, m, v):
        shape = w.shape
        res = _adamw(w.reshape(shape[-2:]), g_arr, sel, m.reshape(shape[-2:]), v.reshape(shape[-2:]))
        big[name] = [a.reshape(shape) for a in res]

    upd("w_ada", w_ada, g_wada.reshape(1, d, ada_cols), 0, m_w_ada, v_w_ada)
    upd("w1_gate", w1_gate, gu1, 0, m_w1_gate, v_w1_gate)
    upd("w1_up", w1_up, gu1, 1, m_w1_up, v_w1_up)
    upd("w1_down", w1_down, gd1, 0, m_w1_down, v_w1_down)
    upd("w_in", w_in, gwin, 0, m_w_in, v_w_in)
    upd("w_out", w_out, gwout, 0, m_w_out, v_w_out)
    upd("w2_gate", w2_gate, gu2, 0, m_w2_gate, v_w2_gate)
    upd("w2_up", w2_up, gu2, 1, m_w2_up, v_w2_up)
    upd("w2_down", w2_down, gd2, 0, m_w2_down, v_w2_down)

    names = ["w_ada", "b_ada", "g_ffn1", "w1_gate", "w1_up", "w1_down", "g_mix", "w_in", "g_sb_out", "g_dil_out",
             "w_out", "rel_bias", "g_ffn2", "w2_gate", "w2_up", "w2_down", "g_final"]
    outs = [loss, r["grad_x"]]
    for k, small in enumerate((sg, sd, sm, sv)):
        for name in names:
            outs.append(big[name][k] if name in big else small[name])
    return tuple(outs)
```

```python
import functools
import math

import numpy as np
import jax
import jax.numpy as jnp
from jax import lax
from jax.experimental import pallas as pl
from jax.experimental.pallas import tpu as pltpu

F32 = jnp.float32
BF = jnp.bfloat16
MESH = pl.DeviceIdType.MESH

HEAD_DIM = 64
N_HEADS = 8
D_GRP = N_HEADS * HEAD_DIM
DIL_CONFIGS = ((128, 1), (512, 4), (2048, 16))
N_STEPS = 128
BLOCK = 128
N_BUCKETS = 32
MAX_DISTANCE = 2048
N_MOD = 9
EPS = 1e-6
NEG_INF = -1e30
SCALE = HEAD_DIM ** -0.5

ADAM_LR = 0.001
ADAM_B1 = 0.9
ADAM_B2 = 0.999
ADAM_EPS = 1e-08
ADAM_WD = 0.01
ADAM_STEP = 10

N_CHIPS = 4
N_DEV = 8
VMEM_LIMIT = 56 * 1024 * 1024
TM = 512
TQ = 256
KB = 256
SMALL_ROWS = 120


def _cp(n_axes=0, **kw):
    sem = ("arbitrary",) * n_axes if n_axes else None
    return pltpu.CompilerParams(dimension_semantics=sem, vmem_limit_bytes=VMEM_LIMIT, **kw)


def _nn(a, b):
    return jnp.dot(a, b, preferred_element_type=F32)


def _nt(a, b):
    return lax.dot_general(a, b, (((1,), (1,)), ((), ())), preferred_element_type=F32)


def _tn(a, b):
    return lax.dot_general(a, b, (((0,), (0,)), ((), ())), preferred_element_type=F32)


def _nn2(x, m):
    hi = x.astype(BF)
    lo = (x - hi.astype(F32)).astype(BF)
    return _nn(hi, m) + _nn(lo, m)


def _softplus(z):
    return jnp.maximum(z, 0.0) + jnp.log1p(jnp.exp(-jnp.abs(z)))


def _sds(shape, dtype):
    return jax.ShapeDtypeStruct(shape, dtype)


def _whole(a):
    nd = a.ndim
    return pl.BlockSpec(a.shape, lambda *_: (0,) * nd, pipeline_mode=pl.Buffered(1))


def _modnorm(x, g, sc, sh, seq):
    t, d = x.shape
    per = seq // TM

    def body(x_ref, g_ref, sc_ref, sh_ref, h_ref):
        xv = x_ref[...]
        r = lax.rsqrt(jnp.mean(xv * xv, axis=-1, keepdims=True) + EPS)
        h_ref[...] = (((xv * r) * g_ref[...]) * (1.0 + sc_ref[...]) + sh_ref[...]).astype(BF)

    return pl.pallas_call(
        body, name="modnorm", grid=(t // TM,),
        in_specs=[pl.BlockSpec((TM, d), lambda m: (m, 0)),
                  pl.BlockSpec((1, d), lambda m: (0, 0)),
                  pl.BlockSpec((None, 1, d), lambda m: (m // per, 0, 0)),
                  pl.BlockSpec((None, 1, d), lambda m: (m // per, 0, 0))],
        out_specs=pl.BlockSpec((TM, d), lambda m: (m, 0)),
        out_shape=_sds((t, d), BF), compiler_params=_cp(1))(x, g, sc, sh)


def _modnorm_bwd_tile(dh, xv, gv, scv, dxo):
    r = lax.rsqrt(jnp.mean(xv * xv, axis=-1, keepdims=True) + EPS)
    n = xv * r
    ng = n * gv
    dsh = jnp.sum(dh, axis=0, keepdims=True)
    dsc = jnp.sum(dh * ng, axis=0, keepdims=True)
    dy = dh * (1.0 + scv)
    dg = jnp.sum(dy * n, axis=0, keepdims=True)
    dn = dy * gv
    dx = dxo + r * (dn - n * jnp.mean(dn * n, axis=-1, keepdims=True))
    return dx, dsh, dsc, dg


def _acc_rows(ref, val, first):
    @pl.when(first)
    def _():
        ref[...] = val

    @pl.when(jnp.logical_not(first))
    def _():
        ref[...] += val


def _ffn_up(h, wgu, carry=None):
    t, d = h.shape
    fs = wgu.shape[-1]

    def body(h_ref, w_ref, p_ref, q_ref, s_ref):
        hv = h_ref[...]
        a = _nn(hv, w_ref[0])
        u = _nn(hv, w_ref[1])
        sig = jax.nn.sigmoid(a)
        q = a * sig
        p_ref[...] = (u * (sig * (1.0 + a * (1.0 - sig)))).astype(BF)
        q_ref[...] = q.astype(BF)
        s_ref[...] = (q * u).astype(BF)

    blk = pl.BlockSpec((None, TM, fs), lambda j, m: (j, m, 0))
    return _call(
        body, "ffn_up", (N_CHIPS, t // TM),
        [pl.BlockSpec((TM, d), lambda j, m: (m, 0)),
         pl.BlockSpec((None, 2, d, fs), lambda j, m: (j, 0, 0, 0))],
        [blk, blk, blk],
        [_sds((N_CHIPS, t, fs), BF)] * 3,
        (h, wgu), carry=carry)


def _ffn_down(s, wd, x, gt, seq, coef):
    _, t, fs = s.shape
    d = x.shape[-1]
    per = seq // TM

    def body(s_ref, w_ref, x_ref, gt_ref, f_ref, xo_ref):
        f = _nn(s_ref[0], w_ref[0, 0])
        for j in range(1, N_CHIPS):
            f = f + _nn(s_ref[j], w_ref[j, 0])
        f_ref[...] = f
        xo_ref[...] = x_ref[...] + (coef * gt_ref[...]) * f

    row = pl.BlockSpec((TM, d), lambda m: (m, 0))
    return pl.pallas_call(
        body, name="ffn_down", grid=(t // TM,),
        in_specs=[pl.BlockSpec((N_CHIPS, TM, fs), lambda m: (0, m, 0)),
                  _whole(wd), row,
                  pl.BlockSpec((None, 1, d), lambda m: (m // per, 0, 0))],
        out_specs=[row, row],
        out_shape=[_sds((t, d), F32), _sds((t, d), F32)],
        compiler_params=_cp(1))(s, wd, x, gt)


def _ffn_bwd_ds(dxo, gt, f, wd, p, q, seq, coef, carry=None):
    t, d = dxo.shape
    fs = p.shape[-1]
    per = seq // TM
    nb = t // seq

    def body(dxo_ref, gt_ref, f_ref, w_ref, p_ref, q_ref, da_ref, du_ref, df_ref, dgt_ref):
        m = pl.program_id(0)
        dxv = dxo_ref[...]
        df = ((coef * gt_ref[...]) * dxv).astype(BF)
        df_ref[...] = df
        _acc_rows(dgt_ref, coef * jnp.sum(dxv * f_ref[...], axis=0, keepdims=True), m % per == 0)
        for j in range(N_CHIPS):
            ds = _nt(df, w_ref[j, 0])
            da_ref[j] = (ds * p_ref[j].astype(F32)).astype(BF)
            du_ref[j] = (ds * q_ref[j].astype(F32)).astype(BF)

    row = pl.BlockSpec((TM, d), lambda m: (m, 0))
    blk = pl.BlockSpec((N_CHIPS, TM, fs), lambda m: (0, m, 0))
    ex = pl.BlockSpec((None, 1, d), lambda m: (m // per, 0, 0))
    return _call(
        body, "ffn_bwd_ds", (t // TM,),
        [row, ex, row, _whole(wd), blk, blk],
        [blk, blk, row, ex],
        [_sds((N_CHIPS, t, fs), BF), _sds((N_CHIPS, t, fs), BF), _sds((t, d), BF), _sds((nb, 1, d), F32)],
        (dxo, gt, f, wd, p, q), carry=carry)


def _ffn_bwd_w(h, da, du, s, df):
    t, d = h.shape
    fs = da.shape[-1]

    def body(h_ref, da_ref, du_ref, s_ref, df_ref, dgu_ref, dwd_ref):
        kt = pl.program_id(1)
        hv = h_ref[...]
        pg = _tn(hv, da_ref[...])
        pu = _tn(hv, du_ref[...])
        pd = _tn(s_ref[...], df_ref[...])

        @pl.when(kt == 0)
        def _():
            dgu_ref[0] = pg
            dgu_ref[1] = pu
            dwd_ref[...] = pd

        @pl.when(kt != 0)
        def _():
            dgu_ref[0] += pg
            dgu_ref[1] += pu
            dwd_ref[...] += pd

    row = pl.BlockSpec((TM, d), lambda j, kt: (kt, 0))
    blk = pl.BlockSpec((None, TM, fs), lambda j, kt: (j, kt, 0))
    return pl.pallas_call(
        body, name="ffn_bwd_w", grid=(N_CHIPS, t // TM),
        in_specs=[row, blk, blk, blk, row],
        out_specs=[pl.BlockSpec((None, 2, d, fs), lambda j, kt: (j, 0, 0, 0)),
                   pl.BlockSpec((None, None, fs, d), lambda j, kt: (j, 0, 0, 0))],
        out_shape=[_sds((N_CHIPS, 2, d, fs), F32), _sds((N_CHIPS, 1, fs, d), F32)],
        compiler_params=_cp(2))(h, da, du, s, df)


def _ffn_bwd_dh(da, du, wgu, x, g, sc, dxo, seq, carry=None):
    _, t, fs = da.shape
    d = x.shape[-1]
    per = seq // TM
    nb = t // seq

    def body(da_ref, du_ref, w_ref, x_ref, g_ref, sc_ref, dxo_ref, dx_ref, dsh_ref, dsc_ref, dg_ref):
        m = pl.program_id(0)
        dh = _nt(da_ref[0], w_ref[0, 0]) + _nt(du_ref[0], w_ref[0, 1])
        for j in range(1, N_CHIPS):
            dh = dh + _nt(da_ref[j], w_ref[j, 0]) + _nt(du_ref[j], w_ref[j, 1])
        dx, dsh, dsc, dg = _modnorm_bwd_tile(dh, x_ref[...], g_ref[...], sc_ref[...], dxo_ref[...])
        dx_ref[...] = dx
        _acc_rows(dsh_ref, dsh, m % per == 0)
        _acc_rows(dsc_ref, dsc, m % per == 0)
        _acc_rows(dg_ref, dg, m == 0)

    row = pl.BlockSpec((TM, d), lambda m: (m, 0))
    blk = pl.BlockSpec((N_CHIPS, TM, fs), lambda m: (0, m, 0))
    ex = pl.BlockSpec((None, 1, d), lambda m: (m // per, 0, 0))
    vec = pl.BlockSpec((1, d), lambda m: (0, 0))
    return _call(
        body, "ffn_bwd_dh", (t // TM,),
        [blk, blk, _whole(wgu), row, vec, ex, row],
        [row, ex, ex, vec],
        [_sds((t, d), F32), _sds((nb, 1, d), F32), _sds((nb, 1, d), F32), _sds((1, d), F32)],
        (da, du, wgu, x, g, sc, dxo), carry=carry)


def _qkv_proj(h, w_in, carry=None):
    t, d = h.shape
    wc = w_in.shape[-1]

    dils = [dil for _, dil in DIL_CONFIGS if dil > 1]

    def body(h_ref, w_ref, o_ref, *rest):
        res_refs, buf = rest[:len(dils)], rest[len(dils)]
        hv = h_ref[...]
        for j in range(N_CHIPS):
            rf = _nn(hv, w_ref[j, 0])
            r = rf.astype(BF)
            for a, lc, off, width in _col_pieces(j, wc):
                o_ref[a, :, lc:lc + width] = r[:, off:off + width]
                if a < 3:
                    continue
                for c0 in range(0, width, 128):
                    cg = (lc + c0) // 128
                    buf[...] = rf[:, off + c0:off + c0 + 128]
                    for ref, dil in zip(res_refs, dils):
                        for rr in range(dil):
                            ref[a - 3, :, rr * D_GRP + cg * 128:rr * D_GRP + (cg + 1) * 128] = (
                                buf[pl.ds(rr, TM // dil, stride=dil), :].astype(BF))

    return _call(
        body, "qkv_proj", (t // TM,),
        [pl.BlockSpec((TM, d), lambda m: (m, 0)), _whole(w_in)],
        [pl.BlockSpec((6, TM, D_GRP), lambda m: (0, m, 0))]
        + [pl.BlockSpec((3, TM // dil, dil * D_GRP), lambda m: (0, m, 0)) for dil in dils],
        [_sds((6, t, D_GRP), BF)] + [_sds((3, t // dil, dil * D_GRP), BF) for dil in dils],
        (h, w_in), scratch=[pltpu.VMEM((TM, 128), F32)], carry=carry)


def _col_pieces(j, wc):
    out, off = [], 0
    while off < wc:
        a, lc = divmod(j * wc + off, D_GRP)
        width = min(D_GRP - lc, wc - off)
        out.append((a, lc, off, width))
        off += width
    return out


def _chip_cols(g6_ref, j, wc):
    return jnp.concatenate([g6_ref[a, :, lc:lc + width] for a, lc, _, width in _col_pieces(j, wc)], axis=1)


def _mix_out(on_sb, on_dil, w_out, x, gt, seq):
    t, d = x.shape
    per = seq // TM

    def body(a_ref, b_ref, w_ref, x_ref, gt_ref, t_ref, xo_ref):
        tv = _nn(a_ref[...], w_ref[0:D_GRP, :]) + _nn(b_ref[...], w_ref[D_GRP:2 * D_GRP, :])
        t_ref[...] = tv
        xo_ref[...] = x_ref[...] + gt_ref[...] * tv

    row = pl.BlockSpec((TM, d), lambda m: (m, 0))
    half = pl.BlockSpec((TM, D_GRP), lambda m: (m, 0))
    return pl.pallas_call(
        body, name="mix_out", grid=(t // TM,),
        in_specs=[half, half, pl.BlockSpec((2 * D_GRP, d), lambda m: (0, 0)), row,
                  pl.BlockSpec((None, 1, d), lambda m: (m // per, 0, 0))],
        out_specs=[row, row],
        out_shape=[_sds((t, d), F32), _sds((t, d), F32)],
        compiler_params=_cp(1))(on_sb, on_dil, w_out, x, gt)


def _sb_masks():
    lane = lax.broadcasted_iota(jnp.int32, (1, 2 * HEAD_DIM), 1)
    hm0 = lane < HEAD_DIM
    rel = lax.broadcasted_iota(jnp.int32, (TQ, KB), 0) - lax.broadcasted_iota(jnp.int32, (TQ, KB), 1)
    kr = lax.broadcasted_iota(jnp.int32, (KB, KB), 0)
    kc = lax.broadcasted_iota(jnp.int32, (KB, KB), 1)
    return hm0, rel, kr, kc


def _headnorm_pair(o, gv, hm0):
    o2 = o * o
    ms0 = jnp.sum(jnp.where(hm0, o2, 0.0), axis=-1, keepdims=True) * (1.0 / HEAD_DIM)
    ms1 = jnp.sum(jnp.where(hm0, 0.0, o2), axis=-1, keepdims=True) * (1.0 / HEAD_DIM)
    r = jnp.where(hm0, lax.rsqrt(ms0 + EPS), lax.rsqrt(ms1 + EPS))
    return (o * r) * gv


SB_DEAD = -104.0


def _alive(c_l):
    return (jnp.max(c_l) > SB_DEAD).astype(jnp.int32)


def _sb_fwd(qkv6, g_sb, nb, seq, carry=None):
    nq = seq // TQ

    def body(q_ref, k_ref, v_ref, g_ref, o_ref, on_ref):
        qi = pl.program_id(2)
        hm0, rel, kr, kc = _sb_masks()
        upper = (kr > kc).astype(BF)
        qv = q_ref[...]
        outs = []
        for hh in range(2):
            hm = hm0 if hh == 0 else jnp.logical_not(hm0)
            qh = jnp.where(hm, qv, jnp.zeros_like(qv))

            def block(kj, c_l, causal, qh=qh):
                ks = pl.multiple_of(kj * KB, KB)
                z = _nt(qh, k_ref[pl.ds(ks, KB), :]) * SCALE
                sp = _softplus(z)
                ln = -sp if causal is None else jnp.where(causal, -sp, 0.0)
                suf = _nn2(ln, upper)
                w = jnp.exp((z - sp) + (suf + c_l))
                if causal is not None:
                    w = jnp.where(causal, w, 0.0)
                pv = _nn(w.astype(BF), v_ref[pl.ds(ks, KB), :])
                return pv, c_l + (suf[:, 0:1] + ln[:, 0:1])

            acc, c_l = block(qi, jnp.zeros((TQ, 1), F32), rel > 0)

            def cond(carry):
                it, alive, _, _ = carry
                return jnp.logical_and(it <= qi, alive > 0)

            def kbody(carry):
                it, _, c_l, acc = carry
                pv, c_l = block(qi - it, c_l, None)
                return it + 1, _alive(c_l), c_l, acc + pv

            _, _, _, acc = lax.while_loop(cond, kbody, (jnp.int32(1), _alive(c_l), c_l, acc))
            outs.append(acc)
        o = jnp.where(hm0, outs[0], outs[1])
        o_ref[...] = o
        on_ref[...] = _headnorm_pair(o, g_ref[...], hm0).astype(BF)

    w = 2 * HEAD_DIM
    full = lambda i: pl.BlockSpec((None, None, seq, w), lambda b, hp, q: (i, b, 0, hp))
    qblk = pl.BlockSpec((None, None, TQ, w), lambda b, hp, q: (0, b, q, hp))
    oblk = pl.BlockSpec((None, TQ, w), lambda b, hp, q: (b, q, hp))
    return _call(
        body, "sb_fwd", (nb, N_HEADS // 2, nq),
        [qblk, full(1), full(2), pl.BlockSpec((1, w), lambda b, hp, q: (0, hp))],
        [oblk, oblk],
        [_sds((nb, seq, D_GRP), F32), _sds((nb, seq, D_GRP), BF)],
        (qkv6, qkv6, qkv6, g_sb), carry=carry)


def _sb_bwd(qkv6, do, nb, seq, carry=None):
    nq = seq // TQ
    nk = seq // KB

    def body(q_ref, k_ref, v_ref, do_ref, out_ref, dk_acc, dv_acc, car):
        qi = pl.program_id(2)
        hm0, rel, kr, kc = _sb_masks()
        upper = (kr > kc).astype(BF)
        lower = (kr < kc).astype(BF)
        ones = jnp.ones((KB, 2 * HEAD_DIM), BF)

        @pl.when(qi == 0)
        def _():
            dk_acc[...] = jnp.zeros_like(dk_acc)
            dv_acc[...] = jnp.zeros_like(dv_acc)

        qv = q_ref[...]
        dov = do_ref[...]
        dqs = []
        for hh in range(2):
            hm = hm0 if hh == 0 else jnp.logical_not(hm0)
            qh = jnp.where(hm, qv, jnp.zeros_like(qv))
            doh = jnp.where(hm, dov, 0.0).astype(BF)

            def logits(kj, causal, qh=qh):
                ks = pl.multiple_of(kj * KB, KB)
                kb = k_ref[pl.ds(ks, KB), :]
                z = _nt(qh, kb) * SCALE
                sp = _softplus(z)
                ln = -sp if causal is None else jnp.where(causal, -sp, 0.0)
                return ks, kb, z - sp, ln

            car[qi] = jnp.zeros((TQ, 1), F32)
            c_l = _nn2(logits(qi, rel > 0)[3], ones)[:, 0:1]

            def acond(carry):
                it, alive, _ = carry
                return jnp.logical_and(it <= qi, alive > 0)

            def abody(carry):
                it, _, c_l = carry
                car[qi - it] = c_l
                c_l = c_l + _nn2(logits(qi - it, None)[3], ones)[:, 0:1]
                return it + 1, _alive(c_l), c_l

            n_used, _, _ = lax.while_loop(acond, abody, (jnp.int32(1), _alive(c_l), c_l))

            def grads(kj, causal, c_g, dq, doh=doh, qh=qh):
                ks, kb, lsz, ln = logits(kj, causal)
                vb = v_ref[pl.ds(ks, KB), :]
                w = jnp.exp(lsz + (_nn2(ln, upper) + car[kj]))
                if causal is not None:
                    w = jnp.where(causal, w, 0.0)
                g = w * _nt(doh, vb)
                pre = _nn2(g, lower)
                sig = jnp.exp(lsz)
                dz = g * (1.0 - sig) - sig * (pre + c_g)
                if causal is not None:
                    dz = jnp.where(causal, dz, 0.0)
                dzb = (dz * SCALE).astype(BF)
                dk_acc[pl.ds(ks, KB), :] += _tn(dzb, qh)
                dv_acc[pl.ds(ks, KB), :] += _tn(w.astype(BF), doh)
                return c_g + (pre[:, KB - 1:KB] + g[:, KB - 1:KB]), dq + _nn(dzb, kb)

            c_g, dq = lax.fori_loop(qi - n_used + 1, qi, lambda kj, cr: grads(kj, None, *cr),
                                    (jnp.zeros((TQ, 1), F32), jnp.zeros((TQ, 2 * HEAD_DIM), F32)))
            _, dq = grads(qi, rel > 0, c_g, dq)
            dqs.append(dq)
        dq = jnp.where(hm0, dqs[0], dqs[1])
        out_ref[0, pl.ds(pl.multiple_of(qi * TQ, TQ), TQ), :] = dq.astype(BF)

        @pl.when(qi == nq - 1)
        def _():
            out_ref[1] = dk_acc[...].astype(BF)
            out_ref[2] = dv_acc[...].astype(BF)

    w = 2 * HEAD_DIM
    full = lambda i: pl.BlockSpec((None, None, seq, w), lambda b, hp, q: (i, b, 0, hp))
    qblk = pl.BlockSpec((None, None, TQ, w), lambda b, hp, q: (0, b, q, hp))
    oblk = pl.BlockSpec((None, TQ, w), lambda b, hp, q: (b, q, hp))
    return _call(
        body, "sb_bwd", (nb, N_HEADS // 2, nq),
        [qblk, full(1), full(2), oblk],
        [pl.BlockSpec((3, None, seq, w), lambda b, hp, q: (0, b, 0, hp))],
        [_sds((6, nb, seq, D_GRP), BF)], (qkv6, qkv6, qkv6, do),
        scratch=[pltpu.VMEM((seq, w), F32), pltpu.VMEM((seq, w), F32), pltpu.VMEM((nk, TQ, 1), F32)],
        carry=carry)


def _t5_bucket(n):
    max_exact = N_BUCKETS // 2
    nf = np.maximum(n, 1).astype(np.float32)
    large = max_exact + (np.log(nf / max_exact) / math.log(MAX_DISTANCE / max_exact)
                         * (N_BUCKETS - max_exact)).astype(np.int32)
    large = np.minimum(large, N_BUCKETS - 1)
    return np.where(n < max_exact, n, large).astype(np.int32)


def _bucket_map(dilation):
    step = BLOCK + np.arange(BLOCK)[:, None] - np.arange(2 * BLOCK)[None, :]
    return _t5_bucket(np.clip(step, 0, N_STEPS) * dilation)


GRP_HEADS = 4
GRP_W = GRP_HEADS * HEAD_DIM


def _dil_masks():
    lane = lax.broadcasted_iota(jnp.int32, (1, GRP_W), 1)
    heads = [jnp.logical_and(lane >= HEAD_DIM * i, lane < HEAD_DIM * (i + 1)) for i in range(GRP_HEADS)]
    iq = lax.broadcasted_iota(jnp.int32, (BLOCK, BLOCK), 0)
    ik = lax.broadcasted_iota(jnp.int32, (BLOCK, BLOCK), 1)
    return heads, ik <= iq, ik >= iq


def _dil_rows(n):
    rs = pl.multiple_of(n * BLOCK, BLOCK)
    ps = pl.multiple_of(jnp.maximum(n - 1, 0) * BLOCK, BLOCK)
    return pl.ds(rs, BLOCK), pl.ds(ps, BLOCK)


def _dil_probs(qh, kc, kp, b_ref, hh, valid_c, valid_p):
    zc = _nt(qh, kc) * SCALE + b_ref[hh, :, BLOCK:2 * BLOCK]
    zp = _nt(qh, kp) * SCALE + b_ref[hh, :, 0:BLOCK]
    zc = jnp.where(valid_c, zc, NEG_INF)
    zp = jnp.where(valid_p, zp, NEG_INF)
    m = jnp.maximum(jnp.max(zc, axis=-1, keepdims=True), jnp.max(zp, axis=-1, keepdims=True))
    ec = jnp.exp(zc - m)
    ep = jnp.exp(zp - m)
    den = jnp.sum(ec, axis=-1, keepdims=True) + jnp.sum(ep, axis=-1, keepdims=True)
    return ec, ep, den, m


def _dil_fwd(qkv6r, base, bias, nb, sub_len, dilation):
    n_blk = sub_len // BLOCK

    def body(q_ref, k_ref, v_ref, b_ref, o_ref, l_ref):
        heads, valid_c, valid_p0 = _dil_masks()

        def nbody(n, carry):
            cur, prev = _dil_rows(n)
            valid_p = jnp.logical_and(valid_p0, n > 0)
            for gi in range(N_HEADS // GRP_HEADS):
                lanes = slice(gi * GRP_W, (gi + 1) * GRP_W)
                qv, kc, kp = q_ref[cur, lanes], k_ref[cur, lanes], k_ref[prev, lanes]
                vc, vp = v_ref[cur, lanes], v_ref[prev, lanes]
                o = jnp.zeros((BLOCK, GRP_W), F32)
                lse = jnp.zeros((BLOCK, GRP_W), F32)
                for i, hm in enumerate(heads):
                    qh = jnp.where(hm, qv, jnp.zeros_like(qv))
                    ec, ep, den, m = _dil_probs(qh, kc, kp, b_ref, gi * GRP_HEADS + i, valid_c, valid_p)
                    o = jnp.where(hm, (_nn(ec.astype(BF), vc) + _nn(ep.astype(BF), vp)) / den, o)
                    lse = jnp.where(hm, m + jnp.log(den), lse)
                o_ref[cur, lanes] = o
                l_ref[cur, lanes] = lse
            return carry

        lax.fori_loop(0, n_blk, nbody, 0)

    seqblk = lambda i: pl.BlockSpec((None, None, sub_len, D_GRP), lambda b, r: (i, b, 0, r))
    oblk = pl.BlockSpec((None, sub_len, D_GRP), lambda b, r: (b, 0, r))
    shp = _sds((nb, sub_len, dilation * D_GRP), F32)
    return pl.pallas_call(
        body, name="dil_fwd_%d" % dilation, grid=(nb, dilation),
        in_specs=[seqblk(base), seqblk(base + 1), seqblk(base + 2), _whole(bias)],
        out_specs=[oblk, oblk], out_shape=[shp, shp],
        compiler_params=_cp(2))(qkv6r, qkv6r, qkv6r, bias)


def _dil_bwd(qkv6r, base, bias, do_c, dd_c, nb, sub_len, dilation, carry=None):
    n_blk = sub_len // BLOCK

    def body(q_ref, k_ref, v_ref, b_ref, do_ref, dd_ref, out_ref, a_ref):
        heads, valid_c, valid_p0 = _dil_masks()
        first = jnp.logical_and(pl.program_id(0) == 0, pl.program_id(1) == 0)

        @pl.when(first)
        def _():
            a_ref[...] = jnp.zeros_like(a_ref)

        out_ref[1] = jnp.zeros((sub_len, D_GRP), F32)
        out_ref[2] = jnp.zeros((sub_len, D_GRP), F32)

        def nbody(n, carry):
            cur, prev = _dil_rows(n)
            valid_p = jnp.logical_and(valid_p0, n > 0)
            for gi in range(N_HEADS // GRP_HEADS):
                lanes = slice(gi * GRP_W, (gi + 1) * GRP_W)
                qv, kc, kp = q_ref[cur, lanes], k_ref[cur, lanes], k_ref[prev, lanes]
                vc, vp = v_ref[cur, lanes], v_ref[prev, lanes]
                dov, ddv = do_ref[cur, lanes], dd_ref[cur, lanes]
                dq = jnp.zeros((BLOCK, GRP_W), F32)
                dkc = jnp.zeros((BLOCK, GRP_W), F32)
                dkp = jnp.zeros((BLOCK, GRP_W), F32)
                dvc = jnp.zeros((BLOCK, GRP_W), F32)
                dvp = jnp.zeros((BLOCK, GRP_W), F32)
                for i, hm in enumerate(heads):
                    h = gi * GRP_HEADS + i
                    qh = jnp.where(hm, qv, jnp.zeros_like(qv))
                    doh = jnp.where(hm, dov, 0.0).astype(BF)
                    ddh = jnp.sum(jnp.where(hm, ddv, 0.0), axis=-1, keepdims=True) * (1.0 / HEAD_DIM)
                    ec, ep, den, _ = _dil_probs(qh, kc, kp, b_ref, h, valid_c, valid_p)
                    inv = 1.0 / den
                    pc = ec * inv
                    pp = ep * inv
                    dzc = pc * (_nt(doh, vc) + ddh)
                    dzp = pp * (_nt(doh, vp) + ddh)
                    a_ref[h, :, BLOCK:2 * BLOCK] += dzc
                    a_ref[h, :, 0:BLOCK] += dzp
                    dzcb = (dzc * SCALE).astype(BF)
                    dzpb = (dzp * SCALE).astype(BF)
                    dq = jnp.where(hm, _nn(dzcb, kc) + _nn(dzpb, kp), dq)
                    dkc = dkc + _tn(dzcb, qh)
                    dkp = dkp + _tn(dzpb, qh)
                    dvc = dvc + _tn(pc.astype(BF), doh)
                    dvp = dvp + _tn(pp.astype(BF), doh)
                out_ref[0, cur, lanes] = dq
                out_ref[1, cur, lanes] += dkc
                out_ref[1, prev, lanes] += dkp
                out_ref[2, cur, lanes] += dvc
                out_ref[2, prev, lanes] += dvp
            return carry

        lax.fori_loop(0, n_blk, nbody, 0)

    seqblk = lambda i: pl.BlockSpec((None, None, sub_len, D_GRP), lambda b, r: (i, b, 0, r))
    oblk = pl.BlockSpec((None, sub_len, D_GRP), lambda b, r: (b, 0, r))
    return _call(
        body, "dil_bwd_%d" % dilation, (nb, dilation),
        [seqblk(base), seqblk(base + 1), seqblk(base + 2), _whole(bias), oblk, oblk],
        [pl.BlockSpec((3, None, sub_len, D_GRP), lambda b, r: (0, b, 0, r)),
         pl.BlockSpec((N_HEADS, BLOCK, 2 * BLOCK), lambda b, r: (0, 0, 0))],
        [_sds((3, nb, sub_len, dilation * D_GRP), F32), _sds((N_HEADS, BLOCK, 2 * BLOCK), F32)],
        (qkv6r, qkv6r, qkv6r, bias, do_c, dd_c), carry=carry)


def _group_ones():
    idx = np.arange(D_GRP) // HEAD_DIM
    return jnp.asarray((idx[:, None] == idx[None, :]).astype(np.float32), dtype=BF)


def _dil_alphas(l1, l4, l16):
    mx = jnp.maximum(jnp.maximum(l1, l4), l16)
    e1 = jnp.exp(l1 - mx)
    e4 = jnp.exp(l4 - mx)
    e16 = jnp.exp(l16 - mx)
    den = e1 + e4 + e16
    return e1 / den, e4 / den, e16 / den


def _residue_spec(dil):
    return pl.BlockSpec((TM // dil, dil * D_GRP), lambda m: (m, 0))


def _from_residue(src, dil, cg, buf):
    if dil == 1:
        return src[:, cg * 128:(cg + 1) * 128]
    for r in range(dil):
        buf[pl.ds(r, TM // dil, stride=dil), :] = src[:, r * D_GRP + cg * 128:r * D_GRP + (cg + 1) * 128]
    return buf[...]


def _to_residue(dst, dil, cg, buf, val):
    if dil == 1:
        dst[:, cg * 128:(cg + 1) * 128] = val
        return
    buf[...] = val
    for r in range(dil):
        dst[:, r * D_GRP + cg * 128:r * D_GRP + (cg + 1) * 128] = buf[pl.ds(r, TM // dil, stride=dil), :]


def _pair_sum(x, hm0):
    s0 = jnp.sum(jnp.where(hm0, x, 0.0), axis=-1, keepdims=True)
    s1 = jnp.sum(jnp.where(hm0, 0.0, x), axis=-1, keepdims=True)
    return jnp.where(hm0, s0, s1)


def _dil_comb(os, ls, g_dil):
    t = os[0].shape[0]
    dils = [dil for _, dil in DIL_CONFIGS]

    def body(o1, l1, o4, l4, o16, l16, g_ref, o_ref, on_ref, b0, b1, b2, b3):
        hm0 = lax.broadcasted_iota(jnp.int32, (1, 128), 1) < HEAD_DIM
        for cg in range(D_GRP // 128):
            lanes = slice(cg * 128, (cg + 1) * 128)
            ov = [_from_residue(src, dil, cg, buf) for src, dil, buf in zip((o1, o4, o16), dils, (None, b0, b1))]
            lv = [_from_residue(src, dil, cg, buf) for src, dil, buf in zip((l1, l4, l16), dils, (None, b2, b3))]
            a1, a4, a16 = _dil_alphas(*lv)
            o = a1 * ov[0] + a4 * ov[1] + a16 * ov[2]
            o_ref[:, lanes] = o
            on_ref[:, lanes] = _headnorm_pair(o, g_ref[:, lanes], hm0).astype(BF)

    blk = pl.BlockSpec((TM, D_GRP), lambda m: (m, 0))
    specs = [_residue_spec(dil) for dil in dils for _ in range(2)]
    return pl.pallas_call(
        body, name="dil_comb", grid=(t // TM,),
        in_specs=specs + [pl.BlockSpec((1, D_GRP), lambda m: (0, 0))],
        out_specs=[blk, blk],
        out_shape=[_sds((t, D_GRP), F32), _sds((t, D_GRP), BF)],
        scratch_shapes=[pltpu.VMEM((TM, 128), F32)] * 4,
        compiler_params=_cp(1))(os[0], ls[0], os[1], ls[1], os[2], ls[2], g_dil)


def _dil_comb_bwd(do, os, ls):
    t = do.shape[0]
    dils = [dil for _, dil in DIL_CONFIGS]

    def body(do_ref, o1, l1, o4, l4, o16, l16, d1, d4, d16, e1, e4, e16, b0, b1, b2, b3):
        hm0 = lax.broadcasted_iota(jnp.int32, (1, 128), 1) < HEAD_DIM
        for cg in range(D_GRP // 128):
            dov = do_ref[:, cg * 128:(cg + 1) * 128]
            ov = [_from_residue(src, dil, cg, buf) for src, dil, buf in zip((o1, o4, o16), dils, (None, b0, b1))]
            lv = [_from_residue(src, dil, cg, buf) for src, dil, buf in zip((l1, l4, l16), dils, (None, b2, b3))]
            al = _dil_alphas(*lv)
            sbar = al[0] * _pair_sum(dov * ov[0], hm0)
            for a_c, o_c in zip(al[1:], ov[1:]):
                sbar = sbar + a_c * _pair_sum(dov * o_c, hm0)
            for a_c, dil, dref, eref in zip(al, dils, (d1, d4, d16), (e1, e4, e16)):
                _to_residue(dref, dil, cg, b0, a_c * dov)
                _to_residue(eref, dil, cg, b1, -a_c * sbar)

    specs = [_residue_spec(dil) for dil in dils]
    return pl.pallas_call(
        body, name="dil_comb_bwd", grid=(t // TM,),
        in_specs=[pl.BlockSpec((TM, D_GRP), lambda m: (m, 0))] + [sp for sp in specs for _ in range(2)],
        out_specs=specs + specs,
        out_shape=[_sds((t // dil, dil * D_GRP), F32) for dil in dils] * 2,
        scratch_shapes=[pltpu.VMEM((TM, 128), F32)] * 4,
        compiler_params=_cp(1))(do, os[0], ls[0], os[1], ls[1], os[2], ls[2])


def _dqkv_dil_sum(ds, dqkv6):
    t = dqkv6.shape[1]
    dils = [dil for _, dil in DIL_CONFIGS]

    def body(*refs):
        srcs, o_ref, acc = refs[:len(dils)], refs[len(dils) + 1], refs[len(dils) + 2]
        for a in range(3):
            for cg in range(D_GRP // 128):
                for src, dil in zip(srcs, dils):
                    for r in range(dil):
                        part = src[a, :, r * D_GRP + cg * 128:r * D_GRP + (cg + 1) * 128]
                        rows = pl.ds(r, TM // dil, stride=dil) if dil > 1 else slice(None)
                        if dil == dils[0]:
                            acc[rows, :] = part
                        else:
                            acc[rows, :] += part
                o_ref[a, :, cg * 128:(cg + 1) * 128] = acc[...].astype(BF)

    return pl.pallas_call(
        body, name="dqkv_dil_sum", grid=(t // TM,),
        in_specs=[pl.BlockSpec((3, TM // dil, dil * D_GRP), lambda m: (0, m, 0)) for dil in dils]
        + [pl.BlockSpec(memory_space=pl.ANY)],
        out_specs=pl.BlockSpec((3, TM, D_GRP), lambda m: (1, m, 0)),
        out_shape=_sds((6, t, D_GRP), BF), input_output_aliases={len(dils): 0},
        scratch_shapes=[pltpu.VMEM((TM, 128), F32)],
        compiler_params=_cp(1))(*ds, dqkv6)


def _relbias_grad(a_all, onehot):
    def body(a_ref, oh_ref, o_ref):
        acc = jnp.zeros((N_HEADS, N_BUCKETS), F32)
        for c in range(len(DIL_CONFIGS)):
            av = a_ref[c]
            hi = av.astype(BF)
            lo = (av - hi.astype(F32)).astype(BF)
            acc = acc + _nt(hi, oh_ref[c]) + _nt(lo, oh_ref[c])
        o_ref[...] = acc

    return pl.pallas_call(body, name="relbias_grad", out_shape=_sds((N_HEADS, N_BUCKETS), F32),
                          compiler_params=_cp())(a_all, onehot)


def _headnorm_bwd(dn, o, gv, mv):
    ms = _nn2(o * o, mv) * (1.0 / HEAD_DIM)
    r = lax.rsqrt(ms + EPS)
    nrm = o * r
    dg = jnp.sum(dn * nrm, axis=0, keepdims=True)
    dnn = dn * gv
    do = r * (dnn - nrm * (_nn2(dnn * nrm, mv) * (1.0 / HEAD_DIM)))
    return do, dg


def _mix_bwd_out(dx, gt, tv, w_out, o_sb, o_dil, on_sb, on_dil, g_sb, g_dil, ones_g, seq):
    t, d = dx.shape
    per = seq // TM
    nb = t // seq

    def body(dx_ref, gt_ref, t_ref, w_ref, osb, odl, onsb, ondl, gsb, gdl, m_ref,
             dosb, dodl, dgt_ref, dgsb, dgdl, dw_ref):
        m = pl.program_id(0)
        dxv = dx_ref[...]
        dt = (gt_ref[...] * dxv).astype(BF)
        _acc_rows(dgt_ref, jnp.sum(dxv * t_ref[...], axis=0, keepdims=True), m % per == 0)
        mv = m_ref[...]
        don_sb = _nt(dt, w_ref[0:D_GRP, :])
        don_dl = _nt(dt, w_ref[D_GRP:2 * D_GRP, :])
        do1, dg1 = _headnorm_bwd(don_sb, osb[...], gsb[...], mv)
        do2, dg2 = _headnorm_bwd(don_dl, odl[...], gdl[...], mv)
        dosb[...] = do1
        dodl[...] = do2
        _acc_rows(dgsb, dg1, m == 0)
        _acc_rows(dgdl, dg2, m == 0)
        p1 = _tn(onsb[...], dt)
        p2 = _tn(ondl[...], dt)

        @pl.when(m == 0)
        def _():
            dw_ref[0:D_GRP, :] = p1
            dw_ref[D_GRP:2 * D_GRP, :] = p2

        @pl.when(m != 0)
        def _():
            dw_ref[0:D_GRP, :] += p1
            dw_ref[D_GRP:2 * D_GRP, :] += p2

    row = pl.BlockSpec((TM, d), lambda m: (m, 0))
    half = pl.BlockSpec((TM, D_GRP), lambda m: (m, 0))
    ex = pl.BlockSpec((None, 1, d), lambda m: (m // per, 0, 0))
    gvec = pl.BlockSpec((1, D_GRP), lambda m: (0, 0))
    wblk = pl.BlockSpec((2 * D_GRP, d), lambda m: (0, 0))
    return pl.pallas_call(
        body, name="mix_bwd_out", grid=(t // TM,),
        in_specs=[row, ex, row, wblk, half, half, half, half, gvec, gvec,
                  pl.BlockSpec((D_GRP, D_GRP), lambda m: (0, 0))],
        out_specs=[half, half, ex, gvec, gvec, wblk],
        out_shape=[_sds((t, D_GRP), F32), _sds((t, D_GRP), F32), _sds((nb, 1, d), F32),
                   _sds((1, D_GRP), F32), _sds((1, D_GRP), F32), _sds((2 * D_GRP, d), F32)],
        compiler_params=_cp(1))(dx, gt, tv, w_out, o_sb, o_dil, on_sb, on_dil, g_sb, g_dil, ones_g)


def _dw_in(h, dqkv6, carry=None):
    t, d = h.shape
    wc = 6 * D_GRP // N_CHIPS

    def body(h_ref, g_ref, o_ref):
        kt = pl.program_id(0)
        hv = h_ref[...]
        for j in range(N_CHIPS):
            p = _tn(hv, _chip_cols(g_ref, j, wc))

            @pl.when(kt == 0)
            def _(p=p, j=j):
                o_ref[j, 0] = p

            @pl.when(kt != 0)
            def _(p=p, j=j):
                o_ref[j, 0] += p

    return _call(
        body, "dw_in", (t // TM,),
        [pl.BlockSpec((TM, d), lambda kt: (kt, 0)), pl.BlockSpec((6, TM, D_GRP), lambda kt: (0, kt, 0))],
        [pl.BlockSpec((N_CHIPS, 1, d, wc), lambda kt: (0, 0, 0, 0))],
        [_sds((N_CHIPS, 1, d, wc), F32)], (h, dqkv6), carry=carry)


def _mix_bwd_dh(dqkv6, w_in, x, g, sc, dxo, seq, carry=None):
    _, t, _ = dqkv6.shape
    d = x.shape[-1]
    wc = w_in.shape[-1]
    per = seq // TM
    nb = t // seq

    def body(g6_ref, w_ref, x_ref, g_ref, sc_ref, dxo_ref, dx_ref, dsh_ref, dsc_ref, dg_ref):
        m = pl.program_id(0)
        dh = _nt(_chip_cols(g6_ref, 0, wc), w_ref[0, 0])
        for j in range(1, N_CHIPS):
            dh = dh + _nt(_chip_cols(g6_ref, j, wc), w_ref[j, 0])
        dx, dsh, dsc, dg = _modnorm_bwd_tile(dh, x_ref[...], g_ref[...], sc_ref[...], dxo_ref[...])
        dx_ref[...] = dx
        _acc_rows(dsh_ref, dsh, m % per == 0)
        _acc_rows(dsc_ref, dsc, m % per == 0)
        _acc_rows(dg_ref, dg, m == 0)

    row = pl.BlockSpec((TM, d), lambda m: (m, 0))
    ex = pl.BlockSpec((None, 1, d), lambda m: (m // per, 0, 0))
    vec = pl.BlockSpec((1, d), lambda m: (0, 0))
    return _call(
        body, "mix_bwd_dh", (t // TM,),
        [pl.BlockSpec((6, TM, D_GRP), lambda m: (0, m, 0)), _whole(w_in), row, vec, ex, row],
        [row, ex, ex, vec],
        [_sds((t, d), F32), _sds((nb, 1, d), F32), _sds((nb, 1, d), F32), _sds((1, d), F32)],
        (dqkv6, w_in, x, g, sc, dxo), carry=carry)


def _final_loss(x, g, target):
    t, d = x.shape
    steps = t // TM

    def body(x_ref, g_ref, t_ref, dx_ref, dg_ref, loss_ref, lacc):
        m = pl.program_id(0)
        xv = x_ref[...]
        gv = g_ref[...]
        r = lax.rsqrt(jnp.mean(xv * xv, axis=-1, keepdims=True) + EPS)
        n = xv * r
        err = n * gv - t_ref[...]
        dy = err * (1.0 / d)
        _acc_rows(dg_ref, jnp.sum(dy * n, axis=0, keepdims=True), m == 0)
        dn = dy * gv
        dx_ref[...] = r * (dn - n * jnp.mean(dn * n, axis=-1, keepdims=True))
        _acc_rows(lacc, jnp.sum(err * err, axis=0, keepdims=True), m == 0)

        @pl.when(m == steps - 1)
        def _():
            tot = jnp.sum(lacc[...], axis=-1, keepdims=True) * (0.5 / d)
            loss_ref[...] = jnp.broadcast_to(tot, (1, 128))

    row = pl.BlockSpec((TM, d), lambda m: (m, 0))
    vec = pl.BlockSpec((1, d), lambda m: (0, 0))
    return pl.pallas_call(
        body, name="final_loss", grid=(steps,),
        in_specs=[row, vec, row],
        out_specs=[row, vec, pl.BlockSpec((1, 128), lambda m: (0, 0))],
        out_shape=[_sds((t, d), F32), _sds((1, d), F32), _sds((1, 128), F32)],
        scratch_shapes=[pltpu.VMEM((1, d), F32)],
        compiler_params=_cp(1))(x, g, target)


def _row_tile(rows, cols):
    best = rows
    for tr in range(8, rows + 1, 8):
        if rows % tr == 0 and tr * cols * 4 <= (1 << 20):
            best = tr
    if best * cols * 4 > (1 << 21):
        best = 8
    return best


def _adamw(w, g_arr, g_sel, m, v):
    rows, cols = w.shape
    tr = _row_tile(rows, cols)
    b1c = 1.0 - ADAM_B1 ** ADAM_STEP
    b2c = 1.0 - ADAM_B2 ** ADAM_STEP

    def body(w_ref, g_ref, m_ref, v_ref, go_ref, d_ref, mo_ref, vo_ref):
        gv = g_ref[...]
        mn = ADAM_B1 * m_ref[...] + (1.0 - ADAM_B1) * gv
        vn = ADAM_B2 * v_ref[...] + (1.0 - ADAM_B2) * (gv * gv)
        go_ref[...] = gv
        mo_ref[...] = mn
        vo_ref[...] = vn
        d_ref[...] = -ADAM_LR * ((mn / b1c) / (jnp.sqrt(vn / b2c) + ADAM_EPS) + ADAM_WD * w_ref[...])

    blk = pl.BlockSpec((tr, cols), lambda i: (i, 0))
    shp = _sds((rows, cols), F32)
    return pl.pallas_call(
        body, name="adamw", grid=(rows // tr,),
        in_specs=[blk, pl.BlockSpec((None, tr, cols), lambda i: (g_sel, i, 0)), blk, blk],
        out_specs=[blk] * 4, out_shape=[shp] * 4,
        compiler_params=_cp(1))(w, g_arr, m, v)


def _flip(v, bit):
    return 1 - v if bit else v


def _my_place():
    x, y, c = lax.axis_index("x"), lax.axis_index("y"), lax.axis_index("c")
    return x, y, c


class _Exchange:
    def __init__(self, operands, out_shape, aliases, sems, start, finish):
        self.operands, self.out_shape, self.aliases, self.sems = list(operands), list(out_shape), dict(aliases), list(sems)
        self.start, self.finish = start, finish


def _join(exchanges):
    exchanges = [e for e in exchanges if e is not None]
    if not exchanges:
        return None
    ops, outs, sems, aliases, spans = [], [], [], {}, []
    for e in exchanges:
        spans.append((len(ops), len(outs), len(sems), e))
        for i, j in e.aliases.items():
            aliases[len(ops) + i] = len(outs) + j
        ops += e.operands
        outs += e.out_shape
        sems += e.sems

    def run(which):
        def go(ins, res, sm):
            for io, oo, so, e in spans:
                getattr(e, which)(ins[io:io + len(e.operands)], res[oo:oo + len(e.out_shape)], sm[so:so + len(e.sems)])
        return go

    return _Exchange(ops, outs, aliases, sems, run("start"), run("finish"))


def _call(body, name, grid, in_specs, out_specs, out_shape, args, scratch=(), carry=None):
    in_specs, out_specs, out_shape, scratch = list(in_specs), list(out_specs), list(out_shape), list(scratch)
    if carry is None:
        return pl.pallas_call(body, name=name, grid=grid, in_specs=in_specs, out_specs=out_specs,
                              out_shape=out_shape, scratch_shapes=scratch,
                              compiler_params=_cp(len(grid)))(*args)
    n_in, n_out, n_s = len(in_specs), len(out_specs), len(scratch)
    c_in, c_out = len(carry.operands), len(carry.out_shape)
    any_spec = pl.BlockSpec(memory_space=pl.ANY)

    def wrapped(*refs):
        ins, cins = refs[:n_in], refs[n_in:n_in + c_in]
        o0 = n_in + c_in
        outs, couts = refs[o0:o0 + n_out], refs[o0 + n_out:o0 + n_out + c_out]
        s0 = o0 + n_out + c_out
        scr, sems = refs[s0:s0 + n_s], refs[s0 + n_s:]
        first = pl.program_id(0) == 0
        last = pl.program_id(0) == grid[0] - 1
        for ax in range(1, len(grid)):
            first = jnp.logical_and(first, pl.program_id(ax) == 0)
            last = jnp.logical_and(last, pl.program_id(ax) == grid[ax] - 1)

        @pl.when(first)
        def _():
            carry.start(cins, couts, sems)

        body(*ins, *outs, *scr)

        @pl.when(last)
        def _():
            carry.finish(cins, couts, sems)

    return pl.pallas_call(
        wrapped, name=name, grid=grid, in_specs=in_specs + [any_spec] * c_in,
        out_specs=out_specs + [any_spec] * c_out, out_shape=out_shape + carry.out_shape,
        scratch_shapes=scratch + carry.sems,
        input_output_aliases={n_in + i: n_out + j for i, j in carry.aliases.items()},
        compiler_params=_cp(len(grid)))(*args, *carry.operands)


def _alone(name, ex):
    any_spec = pl.BlockSpec(memory_space=pl.ANY)
    c_in, c_out = len(ex.operands), len(ex.out_shape)

    def body(*refs):
        ins, outs, sems = refs[:c_in], refs[c_in:c_in + c_out], refs[c_in + c_out:]
        ex.start(ins, outs, sems)
        ex.finish(ins, outs, sems)

    return pl.pallas_call(
        body, name=name, in_specs=[any_spec] * c_in, out_specs=[any_spec] * c_out, out_shape=ex.out_shape,
        scratch_shapes=ex.sems, input_output_aliases=ex.aliases, compiler_params=_cp())(*ex.operands)


def _ada_fwd(c_pad, w_ada, b_shard):
    d = c_pad.shape[-1]
    cols = w_ada.shape[-1]
    chunk = 384

    def body(c_ref, w_ref, b_ref, call_ref, mod_ref, part, s1, r1, s2, r2):
        x, y, c = _my_place()
        dev = 4 * x + 2 * y + c
        chip = 2 * x + y
        call_ref[dev] = c_ref[...]

        def c_copy(k):
            px, py, pc = _flip(x, (k >> 2) & 1), _flip(y, (k >> 1) & 1), _flip(c, k & 1)
            return px, py, pc

        sends = []
        for k in range(1, N_DEV):
            px, py, pc = c_copy(k)
            cp = pltpu.make_async_remote_copy(src_ref=c_ref, dst_ref=call_ref.at[dev], send_sem=s1.at[k - 1],
                                              recv_sem=r1.at[k - 1], device_id=(px, py, pc), device_id_type=MESH)
            cp.start()
            sends.append(cp)
        for k in range(1, N_DEV):
            px, py, pc = c_copy(k)
            pltpu.make_async_remote_copy(src_ref=c_ref, dst_ref=call_ref.at[4 * px + 2 * py + pc],
                                         send_sem=s1.at[k - 1], recv_sem=r1.at[k - 1],
                                         device_id=(px, py, pc), device_id_type=MESH).wait_recv()
        for cp in sends:
            cp.wait_send()

        cs = call_ref[...].reshape(N_DEV * 8, d)
        sc = (cs * jax.nn.sigmoid(cs)).astype(BF)
        for n0 in range(0, cols, chunk):
            blk = _nn(sc, w_ref[:, n0:n0 + chunk].astype(BF)) + b_ref[:, n0:n0 + chunk]
            part[:, :, n0:n0 + chunk] = blk.reshape(N_DEV, 8, chunk)

        mod_ref[chip] = part[dev]
        sends = []
        for kk in range(1, N_CHIPS):
            px, py = _flip(x, (kk >> 1) & 1), _flip(y, kk & 1)
            cp = pltpu.make_async_remote_copy(src_ref=part.at[4 * px + 2 * py + c], dst_ref=mod_ref.at[chip],
                                              send_sem=s2.at[kk - 1], recv_sem=r2.at[kk - 1],
                                              device_id=(px, py, c), device_id_type=MESH)
            cp.start()
            sends.append(cp)
        for kk in range(1, N_CHIPS):
            px, py = _flip(x, (kk >> 1) & 1), _flip(y, kk & 1)
            pltpu.make_async_remote_copy(src_ref=part.at[dev], dst_ref=mod_ref.at[2 * px + py],
                                         send_sem=s2.at[kk - 1], recv_sem=r2.at[kk - 1],
                                         device_id=(px, py, c), device_id_type=MESH).wait_recv()
        for cp in sends:
            cp.wait_send()

    return pl.pallas_call(
        body, name="ada_fwd",
        out_shape=[_sds((N_DEV, 8, d), F32), _sds((N_CHIPS, 8, cols), F32)],
        scratch_shapes=[pltpu.VMEM((N_DEV, 8, cols), F32),
                        pltpu.SemaphoreType.DMA((N_DEV - 1,)), pltpu.SemaphoreType.DMA((N_DEV - 1,)),
                        pltpu.SemaphoreType.DMA((N_CHIPS - 1,)), pltpu.SemaphoreType.DMA((N_CHIPS - 1,))],
        compiler_params=_cp())(c_pad, w_ada, b_shard)


def _ag_weights(bufs):
    n = len(bufs)

    def place():
        x, y, c = _my_place()
        others = [(_flip(x, (kk >> 1) & 1), _flip(y, kk & 1)) for kk in range(1, N_CHIPS)]
        return x, y, c, 2 * x + y, others

    def half(b, which):
        hr = bufs[b].shape[2] // 2
        return pl.ds(pl.multiple_of(which * hr, 16), hr)

    def ici(outs, sems, b, i, slot, x, y, c, px, py):
        rows = outs[b].at[slot, :, half(b, c), :]
        return pltpu.make_async_remote_copy(
            src_ref=rows, dst_ref=rows, send_sem=sems[0].at[3 * b + i], recv_sem=sems[1].at[3 * b + i],
            device_id=(px, py, c), device_id_type=MESH)

    def d2d(outs, sems, b, i, slot, x, y, c, which):
        rows = outs[b].at[slot, :, half(b, which), :]
        return pltpu.make_async_remote_copy(
            src_ref=rows, dst_ref=rows, send_sem=sems[2].at[3 * b + i], recv_sem=sems[3].at[3 * b + i],
            device_id=(x, y, 1 - c), device_id_type=MESH)

    def start(ins, outs, sems):
        x, y, c, chip, others = place()
        for b in range(n):
            for i, (px, py) in enumerate(others):
                ici(outs, sems, b, i, chip, x, y, c, px, py).start()

    def finish(ins, outs, sems):
        x, y, c, chip, others = place()
        for b in range(n):
            for i, (px, py) in enumerate(others):
                ici(outs, sems, b, i, 2 * px + py, x, y, c, px, py).wait_recv()
                d2d(outs, sems, b, i, 2 * px + py, x, y, c, c).start()
        for b in range(n):
            for i, (px, py) in enumerate(others):
                d2d(outs, sems, b, i, 2 * px + py, x, y, c, 1 - c).wait_recv()
        for b in range(n):
            for i, (px, py) in enumerate(others):
                ici(outs, sems, b, i, chip, x, y, c, px, py).wait_send()
                d2d(outs, sems, b, i, 2 * px + py, x, y, c, c).wait_send()

    return _Exchange(bufs, [_sds(s.shape, s.dtype) for s in bufs], {i: i for i in range(n)},
                     [pltpu.SemaphoreType.DMA((3 * n,))] * 4, start, finish)


def _rs_d2d(grads):
    n = len(grads)

    def copy(ins, outs, sems, b):
        x, y, c = _my_place()
        hr = grads[b].shape[2] // 2
        theirs = pl.ds(pl.multiple_of((1 - c) * hr, 8), hr)
        return pltpu.make_async_remote_copy(
            src_ref=ins[b].at[:, :, theirs, :], dst_ref=outs[b], send_sem=sems[0].at[b], recv_sem=sems[1].at[b],
            device_id=(x, y, 1 - c), device_id_type=MESH)

    def start(ins, outs, sems):
        for b in range(n):
            copy(ins, outs, sems, b).start()

    def finish(ins, outs, sems):
        for b in range(n):
            copy(ins, outs, sems, b).wait()

    return _Exchange(grads, [_sds(g.shape[:2] + (g.shape[2] // 2, g.shape[3]), F32) for g in grads], {},
                     [pltpu.SemaphoreType.DMA((n,))] * 2, start, finish)


def _add_halves(core, g, land):
    nchip, ng, rows, cols = g.shape
    hr = rows // 2
    tr = _row_tile(hr, cols)
    steps = hr // tr

    def body(core_ref, g_ref, l_ref, o_ref):
        del core_ref
        o_ref[...] = (g_ref[...] + l_ref[...]).astype(BF)

    return pl.pallas_call(
        body, name="add_halves",
        grid_spec=pltpu.PrefetchScalarGridSpec(
            num_scalar_prefetch=1, grid=(nchip, ng, steps),
            in_specs=[pl.BlockSpec((None, None, tr, cols), lambda j, a, i, cr: (j, a, cr[0] * steps + i, 0)),
                      pl.BlockSpec((None, None, tr, cols), lambda j, a, i, cr: (j, a, i, 0))],
            out_specs=pl.BlockSpec((None, None, tr, cols), lambda j, a, i, cr: (j, a, i, 0))),
        out_shape=_sds((nchip, ng, hr, cols), BF),
        compiler_params=_cp(3))(core, g, land)


def _rs_ici(parts):
    n = len(parts)

    def copies(ins, outs, sems):
        x, y, c = _my_place()
        chip = 2 * x + y
        for b in range(n):
            for kk in range(1, N_CHIPS):
                px, py = _flip(x, (kk >> 1) & 1), _flip(y, kk & 1)
                k = 3 * b + kk - 1
                send = pltpu.make_async_remote_copy(
                    src_ref=ins[b].at[2 * px + py], dst_ref=outs[b].at[chip],
                    send_sem=sems[0].at[k], recv_sem=sems[1].at[k], device_id=(px, py, c), device_id_type=MESH)
                slot = outs[b].at[2 * px + py]
                recv = pltpu.make_async_remote_copy(
                    src_ref=slot, dst_ref=slot, send_sem=sems[0].at[k], recv_sem=sems[1].at[k],
                    device_id=(px, py, c), device_id_type=MESH)
                yield send, recv

    def start(ins, outs, sems):
        for send, _ in copies(ins, outs, sems):
            send.start()

    def finish(ins, outs, sems):
        for send, recv in copies(ins, outs, sems):
            recv.wait_recv()
            send.wait_send()

    return _Exchange(parts, [_sds(p.shape, p.dtype) for p in parts], {},
                     [pltpu.SemaphoreType.DMA((3 * n,))] * 2, start, finish)


def _sum_chips(place, part, land):
    nchip, ng, hr, cols = land.shape
    tr = _row_tile(hr, cols)
    steps = hr // tr

    def body(place_ref, p_ref, l1, l2, l3, o_ref):
        del place_ref
        o_ref[...] = ((p_ref[...].astype(F32) + l1[...].astype(F32)) + l2[...].astype(F32)) + l3[...].astype(F32)

    def slot(k):
        return pl.BlockSpec((None, None, tr, cols), lambda a, i, pr: (jnp.bitwise_xor(pr[1], k), a, i, 0))

    return pl.pallas_call(
        body, name="sum_chips",
        grid_spec=pltpu.PrefetchScalarGridSpec(
            num_scalar_prefetch=1, grid=(ng, steps),
            in_specs=[slot(0), slot(1), slot(2), slot(3)],
            out_specs=pl.BlockSpec((None, tr, cols), lambda a, i, pr: (a, pr[0] * steps + i, 0))),
        out_shape=_sds((ng, 2 * hr, cols), F32),
        compiler_params=_cp(2))(place, part, land, land, land)


def _rs_final(bufs):
    n = len(bufs)

    def copy(outs, sems, b, which):
        x, y, c = _my_place()
        hr = bufs[b].shape[1] // 2
        rows = outs[b].at[:, pl.ds(pl.multiple_of((c if which == 0 else 1 - c) * hr, 8), hr), :]
        return pltpu.make_async_remote_copy(
            src_ref=rows, dst_ref=rows, send_sem=sems[0].at[b], recv_sem=sems[1].at[b],
            device_id=(x, y, 1 - c), device_id_type=MESH)

    def start(ins, outs, sems):
        for b in range(n):
            copy(outs, sems, b, 0).start()

    def finish(ins, outs, sems):
        for b in range(n):
            copy(outs, sems, b, 0).wait_send()
            copy(outs, sems, b, 1).wait_recv()

    return _Exchange(bufs, [_sds(h.shape, F32) for h in bufs], {i: i for i in range(n)},
                     [pltpu.SemaphoreType.DMA((n,))] * 2, start, finish)


def _small_sync(smalls, dmod_blk, c_all):
    d = c_all.shape[-1]
    cols = dmod_blk.shape[-1]
    chunk = 384

    def body(sm_ref, dm_ref, c_ref, sum_ref, gw_ref, sm_all, dm_all, ssem, rsem):
        x, y, c = _my_place()
        dev = 4 * x + 2 * y + c
        chip = 2 * x + y
        sm_all[dev] = sm_ref[...]
        dm_all[dev] = dm_ref[chip]
        sends = []
        for k in range(1, N_DEV):
            px, py, pc = _flip(x, (k >> 2) & 1), _flip(y, (k >> 1) & 1), _flip(c, k & 1)
            a = pltpu.make_async_remote_copy(src_ref=sm_ref, dst_ref=sm_all.at[dev], send_sem=ssem.at[2 * (k - 1)],
                                             recv_sem=rsem.at[2 * (k - 1)], device_id=(px, py, pc),
                                             device_id_type=MESH)
            b = pltpu.make_async_remote_copy(src_ref=dm_ref.at[2 * px + py], dst_ref=dm_all.at[dev],
                                             send_sem=ssem.at[2 * (k - 1) + 1], recv_sem=rsem.at[2 * (k - 1) + 1],
                                             device_id=(px, py, pc), device_id_type=MESH)
            a.start()
            b.start()
            sends += [a, b]
        for k in range(1, N_DEV):
            px, py, pc = _flip(x, (k >> 2) & 1), _flip(y, (k >> 1) & 1), _flip(c, k & 1)
            pdev = 4 * px + 2 * py + pc
            pltpu.make_async_remote_copy(src_ref=sm_ref, dst_ref=sm_all.at[pdev], send_sem=ssem.at[2 * (k - 1)],
                                         recv_sem=rsem.at[2 * (k - 1)], device_id=(px, py, pc),
                                         device_id_type=MESH).wait_recv()
            pltpu.make_async_remote_copy(src_ref=dm_ref.at[chip], dst_ref=dm_all.at[pdev],
                                         send_sem=ssem.at[2 * (k - 1) + 1], recv_sem=rsem.at[2 * (k - 1) + 1],
                                         device_id=(px, py, pc), device_id_type=MESH).wait_recv()
        for cp in sends:
            cp.wait_send()

        tot = sm_all[0]
        for q in range(1, N_DEV):
            tot = tot + sm_all[q]
        sum_ref[...] = tot

        cs = c_ref[...].reshape(N_DEV * 8, d)
        sc = (cs * jax.nn.sigmoid(cs)).astype(BF)
        for n0 in range(0, cols, chunk):
            dmv = dm_all[:, :, n0:n0 + chunk].reshape(N_DEV * 8, chunk).astype(BF)
            gw_ref[:, n0:n0 + chunk] = _tn(sc, dmv)

    return pl.pallas_call(
        body, name="small_sync",
        out_shape=[_sds(smalls.shape, F32), _sds((d, cols), F32)],
        scratch_shapes=[pltpu.VMEM((N_DEV,) + smalls.shape, F32), pltpu.VMEM((N_DEV, 8, cols), F32),
                        pltpu.SemaphoreType.DMA((2 * (N_DEV - 1),)), pltpu.SemaphoreType.DMA((2 * (N_DEV - 1),))],
        compiler_params=_cp())(smalls, dmod_blk, c_all)


def _bucket_onehot():
    maps = np.stack([_bucket_map(dil).reshape(-1) for _, dil in DIL_CONFIGS])
    return (jnp.asarray(maps)[:, None, :] == jnp.arange(N_BUCKETS, dtype=jnp.int32)[None, :, None]).astype(BF)


def _dil_bias(rel_t, onehot):
    def body(r_ref, oh_ref, o_ref):
        rv = r_ref[...]
        hi = rv.astype(BF)
        lo = (rv - hi.astype(F32)).astype(BF)
        for c in range(len(DIL_CONFIGS)):
            o_ref[c] = _nn(hi, oh_ref[c]) + _nn(lo, oh_ref[c])

    return pl.pallas_call(body, name="dil_bias",
                          out_shape=_sds((len(DIL_CONFIGS), N_HEADS, BLOCK * 2 * BLOCK), F32),
                          compiler_params=_cp())(rel_t, onehot)


def _rowsum8(a):
    def body(a_ref, o_ref):
        o_ref[...] = jnp.sum(a_ref[...], axis=0, keepdims=True)

    return pl.pallas_call(body, name="rowsum8", out_shape=_sds((1, a.shape[1]), F32), compiler_params=_cp())(a)


def _local_step(x, mod, target, w, gains, rel_bias, place=None):
    nb, seq, d = x.shape
    t = nb * seq
    dist = place is not None
    core = place[0:1] if dist else None
    x0 = x.reshape(t, d)
    tgt = target.reshape(t, d)
    md = [mod[:, i:i + 1, :] for i in range(N_MOD)]
    sh1, sc1, gt1, sh2, sc2, gt2, sh3, sc3, gt3 = md
    g1, g2, g3 = gains["g_ffn1"], gains["g_mix"], gains["g_ffn2"]
    ones_g = _group_ones()

    def partial_sums(grads, lands):
        return [_add_halves(core, g, l) for g, l in zip(grads, lands)]

    def chip_sums(parts, lands):
        return [_sum_chips(place, p, l) for p, l in zip(parts, lands)]

    h1 = _modnorm(x0, g1, sc1, sh1, seq)
    res = _ffn_up(h1, w["gu1"], carry=_ag_weights([w["d1"], w["win"], w["wout"]]) if dist else None)
    a1, u1, s1 = res[:3]
    wd1, w_in, w_out = res[3:] if dist else (w["d1"], w["win"], w["wout"])
    w_out2 = w_out.reshape(2 * D_GRP, d)
    f1, x1 = _ffn_down(s1, wd1, x0, gt1, seq, 0.5)

    h2 = _modnorm(x1, g2, sc2, sh2, seq)
    qkv6, qkv_r4, qkv_r16 = _qkv_proj(h2, w_in)
    qkv6b = qkv6.reshape(6, nb, seq, D_GRP)
    res = _sb_fwd(qkv6b, gains["g_sb_out"], nb, seq, carry=_ag_weights([w["gu2"], w["d2"]]) if dist else None)
    o_sb, on_sb = res[:2]
    wgu2, wd2 = res[2:] if dist else (w["gu2"], w["d2"])
    onehot = _bucket_onehot()
    bias = _dil_bias(rel_bias.T, onehot).reshape(len(DIL_CONFIGS), N_HEADS, BLOCK, 2 * BLOCK)
    o_cs, l_cs = [], []
    qkv_rs = [(qkv6b, 3), (qkv_r4, 0), (qkv_r16, 0)]
    for ci, (_, dil) in enumerate(DIL_CONFIGS):
        sub = seq // dil
        arr, base = qkv_rs[ci]
        arr = arr.reshape(base + 3, nb, sub, dil * D_GRP)
        qkv_rs[ci] = (arr, base)
        o_c, l_c = _dil_fwd(arr, base, bias[ci], nb, sub, dil)
        o_cs.append(o_c.reshape(t // dil, dil * D_GRP))
        l_cs.append(l_c.reshape(t // dil, dil * D_GRP))
    o_dil, on_dil = _dil_comb(o_cs, l_cs, gains["g_dil_out"])
    tmix, x2 = _mix_out(on_sb.reshape(t, D_GRP), on_dil, w_out2, x1, gt2, seq)

    h3 = _modnorm(x2, g3, sc3, sh3, seq)
    a3, u3, s3 = _ffn_up(h3, wgu2)
    f3, x3 = _ffn_down(s3, wd2, x2, gt3, seq, 0.5)

    dx3, dg_final, loss = _final_loss(x3, gains["g_final"], tgt)

    da3, du3, df3, dgt3 = _ffn_bwd_ds(dx3, gt3, f3, wd2, a3, u3, seq, 0.5)
    grads2 = _ffn_bwd_w(h3, da3, du3, s3, df3)
    res = _ffn_bwd_dh(da3, du3, wgu2, x2, g3, sc3, dx3, seq, carry=_rs_d2d(grads2) if dist else None)
    dx2, dsh3, dsc3, dg3 = res[:4]
    parts2 = partial_sums(grads2, res[4:]) if dist else None

    do_sb, do_dil, dgt2, dg_sb, dg_dil, dw_out = _mix_bwd_out(
        dx2, gt2, tmix, w_out2, o_sb.reshape(t, D_GRP), o_dil, on_sb.reshape(t, D_GRP), on_dil,
        gains["g_sb_out"], gains["g_dil_out"], ones_g, seq)
    dw_out = dw_out.reshape(N_CHIPS, 1, 2 * D_GRP // N_CHIPS, d)
    res = _sb_bwd(qkv6b, do_sb.reshape(nb, seq, D_GRP), nb, seq, carry=_rs_ici(parts2) if dist else None)
    dqkv6 = res[0]
    halves2 = chip_sums(parts2, res[1:]) if dist else None
    dcs = _dil_comb_bwd(do_dil, o_cs, l_cs)
    dsum, a_tiles = [], []
    for ci, (_, dil) in enumerate(DIL_CONFIGS):
        sub = seq // dil
        do_c = dcs[ci].reshape(nb, sub, dil * D_GRP)
        dd_c = dcs[3 + ci].reshape(nb, sub, dil * D_GRP)
        res = _dil_bwd(qkv_rs[ci][0], qkv_rs[ci][1], bias[ci], do_c, dd_c, nb, sub, dil,
                       carry=_rs_final(halves2) if dist and ci == 0 else None)
        if dist and ci == 0:
            grads2 = res[2:]
        dsum.append(res[0].reshape(3, t // dil, dil * D_GRP))
        a_tiles.append(res[1].reshape(N_HEADS, BLOCK * 2 * BLOCK))
    dqkv6 = _dqkv_dil_sum(dsum, dqkv6.reshape(6, t, D_GRP))
    drel = _relbias_grad(jnp.stack(a_tiles), onehot)
    dx1, dsh2, dsc2, dg2 = _mix_bwd_dh(dqkv6, w_in, x1, g2, sc2, dx2, seq)

    da1, du1, df1, dgt1 = _ffn_bwd_ds(dx1, gt1, f1, wd1, a1, u1, seq, 0.5)
    grads1 = _ffn_bwd_w(h1, da1, du1, s1, df1)
    res = _dw_in(h2, dqkv6, carry=_rs_d2d(grads1) if dist else None)
    grads_m = [res[0], dw_out]
    parts1 = partial_sums(grads1, res[1:]) if dist else None
    res = _ffn_bwd_dh(da1, du1, w["gu1"], x0, g1, sc1, dx1, seq,
                      carry=_join([_rs_ici(parts1), _rs_d2d(grads_m)]) if dist else None)
    dx0, dsh1, dsc1, dg1 = res[:4]
    if dist:
        halves1 = chip_sums(parts1, res[4:6])
        parts_m = partial_sums(grads_m, res[6:8])
        res = _alone("rs_tail", _join([_rs_final(halves1), _rs_ici(parts_m)]))
        grads1 = res[:2]
        grads_m = _alone("rs_last", _rs_final(chip_sums(parts_m, res[2:4])))

    dmod = jnp.concatenate([dsh1, dsc1, dgt1, dsh2, dsc2, dgt2, dsh3, dsc3, dgt3], axis=1)
    return dict(grad_x=dx0.reshape(nb, seq, d), loss=loss[0, 0], dmod=dmod.reshape(nb, N_MOD * d),
                dgu1=grads1[0], dwd1=grads1[1], dgu2=grads2[0], dwd2=grads2[1], dwin=grads_m[0], dwout=grads_m[1],
                dg_ffn1=dg1, dg_mix=dg2, dg_ffn2=dg3, dg_final=dg_final, dg_sb=dg_sb, dg_dil=dg_dil,
                drel=drel.T)


_SMALL_ORDER = (("b_ada", N_MOD * 1024), ("g_ffn1", 1024), ("g_mix", 1024), ("g_ffn2", 1024), ("g_final", 1024),
                ("g_sb_out", D_GRP), ("g_dil_out", D_GRP), ("rel_bias", N_BUCKETS * N_HEADS))


def _pack_small(parts, extra=None):
    flat = [parts[name].reshape(-1).astype(F32) for name, _ in _SMALL_ORDER]
    used = sum(sz for _, sz in _SMALL_ORDER)
    pad = SMALL_ROWS * 128 - used
    tail = jnp.zeros((pad,), F32)
    if extra is not None:
        tail = tail.at[0].set(extra)
    return jnp.concatenate(flat + [tail]).reshape(SMALL_ROWS, 128)


def _unpack_small(packed, shapes):
    flat = packed.reshape(-1)
    out, off = {}, 0
    for name, sz in _SMALL_ORDER:
        out[name] = flat[off:off + sz].reshape(shapes[name])
        off += sz
    return out, flat[off]


def kernel(x, c, w_ada, b_ada, g_ffn1, w1_gate, w1_up, w1_down, g_mix, w_in, g_sb_out, g_dil_out, w_out, rel_bias, g_ffn2, w2_gate, w2_up, w2_down, g_final, loss_target, m_w_ada, m_b_ada, m_g_ffn1, m_w1_gate, m_w1_up, m_w1_down, m_g_mix, m_w_in, m_g_sb_out, m_g_dil_out, m_w_out, m_rel_bias, m_g_ffn2, m_w2_gate, m_w2_up, m_w2_down, m_g_final, v_w_ada, v_b_ada, v_g_ffn1, v_w1_gate, v_w1_up, v_w1_down, v_g_mix, v_w_in, v_g_sb_out, v_g_dil_out, v_w_out, v_rel_bias, v_g_ffn2, v_w2_gate, v_w2_up, v_w2_down, v_g_final):
    nb, seq, d = x.shape
    xi, yi, ci = lax.axis_index("x"), lax.axis_index("y"), lax.axis_index("c")
    chip = 2 * xi + yi
    ada_cols = w_ada.shape[-1]

    c_pad = jnp.zeros((8, d), F32).at[:nb].set(c)
    b_shard = lax.dynamic_slice(b_ada, (0, chip * ada_cols), (1, ada_cols))
    c_all, mod_blk = _ada_fwd(c_pad, w_ada[0], b_shard)
    mod = jnp.transpose(mod_blk[:, :nb, :], (1, 0, 2)).reshape(nb, N_MOD, d)

    shards = dict(gu1=jnp.stack([w1_gate[0], w1_up[0]]), d1=w1_down, win=w_in, wout=w_out,
                  gu2=jnp.stack([w2_gate[0], w2_up[0]]), d2=w2_down)
    bufs = {k: lax.dynamic_update_slice(lax.empty((N_CHIPS,) + s.shape, BF), s.astype(BF)[None], (chip, 0, 0, 0))
            for k, s in shards.items()}
    bufs["gu1"] = _alone("ag_first", _ag_weights([bufs["gu1"]]))[0]

    gains = dict(g_ffn1=g_ffn1, g_mix=g_mix, g_ffn2=g_ffn2, g_final=g_final.reshape(1, d),
                 g_sb_out=g_sb_out.reshape(1, D_GRP), g_dil_out=g_dil_out.reshape(1, D_GRP))
    place = jnp.stack([ci, chip]).astype(jnp.int32)
    r = _local_step(x, mod, loss_target, bufs, gains, rel_bias, place)
    gu1, gu2, gd1, gd2, gwin, gwout = r["dgu1"], r["dgu2"], r["dwd1"], r["dwd2"], r["dwin"], r["dwout"]

    dmod = r["dmod"]
    dmod_pad = jnp.zeros((8, N_MOD * d), F32).at[:nb].set(dmod)
    dmod_blk = jnp.transpose(dmod_pad.reshape(8, N_CHIPS, ada_cols), (1, 0, 2))
    small_parts = dict(b_ada=_rowsum8(dmod_pad), g_ffn1=r["dg_ffn1"], g_mix=r["dg_mix"], g_ffn2=r["dg_ffn2"],
                       g_final=r["dg_final"], g_sb_out=r["dg_sb"], g_dil_out=r["dg_dil"], rel_bias=r["drel"])
    small_sum, g_wada = _small_sync(_pack_small(small_parts, r["loss"]), dmod_blk, c_all)

    small_w = dict(b_ada=b_ada, g_ffn1=g_ffn1, g_mix=g_mix, g_ffn2=g_ffn2, g_final=g_final,
                   g_sb_out=g_sb_out, g_dil_out=g_dil_out, rel_bias=rel_bias)
    small_m = dict(b_ada=m_b_ada, g_ffn1=m_g_ffn1, g_mix=m_g_mix, g_ffn2=m_g_ffn2, g_final=m_g_final,
                   g_sb_out=m_g_sb_out, g_dil_out=m_g_dil_out, rel_bias=m_rel_bias)
    small_v = dict(b_ada=v_b_ada, g_ffn1=v_g_ffn1, g_mix=v_g_mix, g_ffn2=v_g_ffn2, g_final=v_g_final,
                   g_sb_out=v_g_sb_out, g_dil_out=v_g_dil_out, rel_bias=v_rel_bias)
    shapes = {k: v.shape for k, v in small_w.items()}
    sg, sd, sm, sv = _adamw(_pack_small(small_w), small_sum.reshape(1, SMALL_ROWS, 128), 0,
                            _pack_small(small_m), _pack_small(small_v))
    sg, loss = _unpack_small(sg, shapes)
    sd, _ = _unpack_small(sd, shapes)
    sm, _ = _unpack_small(sm, shapes)
    sv, _ = _unpack_small(sv, shapes)

    big = {}

    def upd(name, w, g_arr, sel, m, v):
        shape = w.shape
        res = _adamw(w.reshape(shape[-2:]), g_arr, sel, m.reshape(shape[-2:]), v.reshape(shape[-2:]))
        big[name] = [a.reshape(shape) for a in res]

    upd("w_ada", w_ada, g_wada.reshape(1, d, ada_cols), 0, m_w_ada, v_w_ada)
    upd("w1_gate", w1_gate, gu1, 0, m_w1_gate, v_w1_gate)
    upd("w1_up", w1_up, gu1, 1, m_w1_up, v_w1_up)
    upd("w1_down", w1_down, gd1, 0, m_w1_down, v_w1_down)
    upd("w_in", w_in, gwin, 0, m_w_in, v_w_in)
    upd("w_out", w_out, gwout, 0, m_w_out, v_w_out)
    upd("w2_gate", w2_gate, gu2, 0, m_w2_gate, v_w2_gate)
    upd("w2_up", w2_up, gu2, 1, m_w2_up, v_w2_up)
    upd("w2_down", w2_down, gd2, 0, m_w2_down, v_w2_down)

    names = ["w_ada", "b_ada", "g_ffn1", "w1_gate", "w1_up", "w1_down", "g_mix", "w_in", "g_sb_out", "g_dil_out",
             "w_out", "rel_bias", "g_ffn2", "w2_gate", "w2_up", "w2_down", "g_final"]
    outs = [loss, r["grad_x"]]
    for k, small in enumerate((sg, sd, sm, sv)):
        for name in names:
            outs.append(big[name][k] if name in big else small[name])
    return tuple(outs)
```

```python
import functools
import math

import numpy as np
import jax
import jax.numpy as jnp
from jax import lax
from jax.experimental import pallas as pl
from jax.experimental.pallas import tpu as pltpu

F32 = jnp.float32
BF = jnp.bfloat16
MESH = pl.DeviceIdType.MESH

HEAD_DIM = 64
N_HEADS = 8
D_GRP = N_HEADS * HEAD_DIM
DIL_CONFIGS = ((128, 1), (512, 4), (2048, 16))
N_STEPS = 128
BLOCK = 128
N_BUCKETS = 32
MAX_DISTANCE = 2048
N_MOD = 9
EPS = 1e-6
NEG_INF = -1e30
SCALE = HEAD_DIM ** -0.5

ADAM_LR = 0.001
ADAM_B1 = 0.9
ADAM_B2 = 0.999
ADAM_EPS = 1e-08
ADAM_WD = 0.01
ADAM_STEP = 10

N_CHIPS = 4
N_DEV = 8
VMEM_LIMIT = 56 * 1024 * 1024
TM = 512
TQ = 256
KB = 256
SMALL_ROWS = 120


def _cp(n_axes=0, **kw):
    sem = ("arbitrary",) * n_axes if n_axes else None
    return pltpu.CompilerParams(dimension_semantics=sem, vmem_limit_bytes=VMEM_LIMIT, **kw)


def _nn(a, b):
    return jnp.dot(a, b, preferred_element_type=F32)


def _nt(a, b):
    return lax.dot_general(a, b, (((1,), (1,)), ((), ())), preferred_element_type=F32)


def _tn(a, b):
    return lax.dot_general(a, b, (((0,), (0,)), ((), ())), preferred_element_type=F32)


def _nn2(x, m):
    hi = x.astype(BF)
    lo = (x - hi.astype(F32)).astype(BF)
    return _nn(hi, m) + _nn(lo, m)


def _softplus(z):
    return jnp.maximum(z, 0.0) + jnp.log1p(jnp.exp(-jnp.abs(z)))


def _sds(shape, dtype):
    return jax.ShapeDtypeStruct(shape, dtype)


def _whole(a):
    nd = a.ndim
    return pl.BlockSpec(a.shape, lambda *_: (0,) * nd, pipeline_mode=pl.Buffered(1))


def _modnorm(x, g, sc, sh, seq):
    t, d = x.shape
    per = seq // TM

    def body(x_ref, g_ref, sc_ref, sh_ref, h_ref):
        xv = x_ref[...]
        r = lax.rsqrt(jnp.mean(xv * xv, axis=-1, keepdims=True) + EPS)
        h_ref[...] = (((xv * r) * g_ref[...]) * (1.0 + sc_ref[...]) + sh_ref[...]).astype(BF)

    return pl.pallas_call(
        body, name="modnorm", grid=(t // TM,),
        in_specs=[pl.BlockSpec((TM, d), lambda m: (m, 0)),
                  pl.BlockSpec((1, d), lambda m: (0, 0)),
                  pl.BlockSpec((None, 1, d), lambda m: (m // per, 0, 0)),
                  pl.BlockSpec((None, 1, d), lambda m: (m // per, 0, 0))],
        out_specs=pl.BlockSpec((TM, d), lambda m: (m, 0)),
        out_shape=_sds((t, d), BF), compiler_params=_cp(1))(x, g, sc, sh)


def _modnorm_bwd_tile(dh, xv, gv, scv, dxo):
    r = lax.rsqrt(jnp.mean(xv * xv, axis=-1, keepdims=True) + EPS)
    n = xv * r
    ng = n * gv
    dsh = jnp.sum(dh, axis=0, keepdims=True)
    dsc = jnp.sum(dh * ng, axis=0, keepdims=True)
    dy = dh * (1.0 + scv)
    dg = jnp.sum(dy * n, axis=0, keepdims=True)
    dn = dy * gv
    dx = dxo + r * (dn - n * jnp.mean(dn * n, axis=-1, keepdims=True))
    return dx, dsh, dsc, dg


def _acc_rows(ref, val, first):
    @pl.when(first)
    def _():
        ref[...] = val

    @pl.when(jnp.logical_not(first))
    def _():
        ref[...] += val


def _ffn_up(h, wgu, carry=None):
    t, d = h.shape
    fs = wgu.shape[-1]

    def body(h_ref, w_ref, p_ref, q_ref, s_ref):
        hv = h_ref[...]
        a = _nn(hv, w_ref[0])
        u = _nn(hv, w_ref[1])
        sig = jax.nn.sigmoid(a)
        q = a * sig
        p_ref[...] = (u * (sig * (1.0 + a * (1.0 - sig)))).astype(BF)
        q_ref[...] = q.astype(BF)
        s_ref[...] = (q * u).astype(BF)

    blk = pl.BlockSpec((None, TM, fs), lambda j, m: (j, m, 0))
    return _call(
        body, "ffn_up", (N_CHIPS, t // TM),
        [pl.BlockSpec((TM, d), lambda j, m: (m, 0)),
         pl.BlockSpec((None, 2, d, fs), lambda j, m: (j, 0, 0, 0))],
        [blk, blk, blk],
        [_sds((N_CHIPS, t, fs), BF)] * 3,
        (h, wgu), carry=carry)


def _ffn_down(s, wd, x, gt, seq, coef):
    _, t, fs = s.shape
    d = x.shape[-1]
    per = seq // TM

    def body(s_ref, w_ref, x_ref, gt_ref, f_ref, xo_ref):
        f = _nn(s_ref[0], w_ref[0, 0])
        for j in range(1, N_CHIPS):
            f = f + _nn(s_ref[j], w_ref[j, 0])
        f_ref[...] = f
        xo_ref[...] = x_ref[...] + (coef * gt_ref[...]) * f

    row = pl.BlockSpec((TM, d), lambda m: (m, 0))
    return pl.pallas_call(
        body, name="ffn_down", grid=(t // TM,),
        in_specs=[pl.BlockSpec((N_CHIPS, TM, fs), lambda m: (0, m, 0)),
                  _whole(wd), row,
                  pl.BlockSpec((None, 1, d), lambda m: (m // per, 0, 0))],
        out_specs=[row, row],
        out_shape=[_sds((t, d), F32), _sds((t, d), F32)],
        compiler_params=_cp(1))(s, wd, x, gt)


def _ffn_bwd_ds(dxo, gt, f, wd, p, q, seq, coef, carry=None):
    t, d = dxo.shape
    fs = p.shape[-1]
    per = seq // TM
    nb = t // seq

    def body(dxo_ref, gt_ref, f_ref, w_ref, p_ref, q_ref, da_ref, du_ref, df_ref, dgt_ref):
        m = pl.program_id(0)
        dxv = dxo_ref[...]
        df = ((coef * gt_ref[...]) * dxv).astype(BF)
        df_ref[...] = df
        _acc_rows(dgt_ref, coef * jnp.sum(dxv * f_ref[...], axis=0, keepdims=True), m % per == 0)
        for j in range(N_CHIPS):
            ds = _nt(df, w_ref[j, 0])
            da_ref[j] = (ds * p_ref[j].astype(F32)).astype(BF)
            du_ref[j] = (ds * q_ref[j].astype(F32)).astype(BF)

    row = pl.BlockSpec((TM, d), lambda m: (m, 0))
    blk = pl.BlockSpec((N_CHIPS, TM, fs), lambda m: (0, m, 0))
    ex = pl.BlockSpec((None, 1, d), lambda m: (m // per, 0, 0))
    return _call(
        body, "ffn_bwd_ds", (t // TM,),
        [row, ex, row, _whole(wd), blk, blk],
        [blk, blk, row, ex],
        [_sds((N_CHIPS, t, fs), BF), _sds((N_CHIPS, t, fs), BF), _sds((t, d), BF), _sds((nb, 1, d), F32)],
        (dxo, gt, f, wd, p, q), carry=carry)


def _ffn_bwd_w(h, da, du, s, df):
    t, d = h.shape
    fs = da.shape[-1]

    def body(h_ref, da_ref, du_ref, s_ref, df_ref, dgu_ref, dwd_ref):
        kt = pl.program_id(1)
        hv = h_ref[...]
        pg = _tn(hv, da_ref[...])
        pu = _tn(hv, du_ref[...])
        pd = _tn(s_ref[...], df_ref[...])

        @pl.when(kt == 0)
        def _():
            dgu_ref[0] = pg
            dgu_ref[1] = pu
            dwd_ref[...] = pd

        @pl.when(kt != 0)
        def _():
            dgu_ref[0] += pg
            dgu_ref[1] += pu
            dwd_ref[...] += pd

    row = pl.BlockSpec((TM, d), lambda j, kt: (kt, 0))
    blk = pl.BlockSpec((None, TM, fs), lambda j, kt: (j, kt, 0))
    return pl.pallas_call(
        body, name="ffn_bwd_w", grid=(N_CHIPS, t // TM),
        in_specs=[row, blk, blk, blk, row],
        out_specs=[pl.BlockSpec((None, 2, d, fs), lambda j, kt: (j, 0, 0, 0)),
                   pl.BlockSpec((None, None, fs, d), lambda j, kt: (j, 0, 0, 0))],
        out_shape=[_sds((N_CHIPS, 2, d, fs), F32), _sds((N_CHIPS, 1, fs, d), F32)],
        compiler_params=_cp(2))(h, da, du, s, df)


def _ffn_bwd_dh(da, du, wgu, x, g, sc, dxo, seq, carry=None):
    _, t, fs = da.shape
    d = x.shape[-1]
    per = seq // TM
    nb = t // seq

    def body(da_ref, du_ref, w_ref, x_ref, g_ref, sc_ref, dxo_ref, dx_ref, dsh_ref, dsc_ref, dg_ref):
        m = pl.program_id(0)
        dh = _nt(da_ref[0], w_ref[0, 0]) + _nt(du_ref[0], w_ref[0, 1])
        for j in range(1, N_CHIPS):
            dh = dh + _nt(da_ref[j], w_ref[j, 0]) + _nt(du_ref[j], w_ref[j, 1])
        dx, dsh, dsc, dg = _modnorm_bwd_tile(dh, x_ref[...], g_ref[...], sc_ref[...], dxo_ref[...])
        dx_ref[...] = dx
        _acc_rows(dsh_ref, dsh, m % per == 0)
        _acc_rows(dsc_ref, dsc, m % per == 0)
        _acc_rows(dg_ref, dg, m == 0)

    row = pl.BlockSpec((TM, d), lambda m: (m, 0))
    blk = pl.BlockSpec((N_CHIPS, TM, fs), lambda m: (0, m, 0))
    ex = pl.BlockSpec((None, 1, d), lambda m: (m // per, 0, 0))
    vec = pl.BlockSpec((1, d), lambda m: (0, 0))
    return _call(
        body, "ffn_bwd_dh", (t // TM,),
        [blk, blk, _whole(wgu), row, vec, ex, row],
        [row, ex, ex, vec],
        [_sds((t, d), F32), _sds((nb, 1, d), F32), _sds((nb, 1, d), F32), _sds((1, d), F32)],
        (da, du, wgu, x, g, sc, dxo), carry=carry)


def _qkv_proj(h, w_in, carry=None):
    t, d = h.shape
    wc = w_in.shape[-1]

    dils = [dil for _, dil in DIL_CONFIGS if dil > 1]

    def body(h_ref, w_ref, o_ref, *rest):
        res_refs, buf = rest[:len(dils)], rest[len(dils)]
        hv = h_ref[...]
        for j in range(N_CHIPS):
            rf = _nn(hv, w_ref[j, 0])
            r = rf.astype(BF)
            for a, lc, off, width in _col_pieces(j, wc):
                o_ref[a, :, lc:lc + width] = r[:, off:off + width]
                if a < 3:
                    continue
                for c0 in range(0, width, 128):
                    cg = (lc + c0) // 128
                    buf[...] = rf[:, off + c0:off + c0 + 128]
                    for ref, dil in zip(res_refs, dils):
                        for rr in range(dil):
                            ref[a - 3, :, rr * D_GRP + cg * 128:rr * D_GRP + (cg + 1) * 128] = (
                                buf[pl.ds(rr, TM // dil, stride=dil), :].astype(BF))

    return _call(
        body, "qkv_proj", (t // TM,),
        [pl.BlockSpec((TM, d), lambda m: (m, 0)), _whole(w_in)],
        [pl.BlockSpec((6, TM, D_GRP), lambda m: (0, m, 0))]
        + [pl.BlockSpec((3, TM // dil, dil * D_GRP), lambda m: (0, m, 0)) for dil in dils],
        [_sds((6, t, D_GRP), BF)] + [_sds((3, t // dil, dil * D_GRP), BF) for dil in dils],
        (h, w_in), scratch=[pltpu.VMEM((TM, 128), F32)], carry=carry)


def _col_pieces(j, wc):
    out, off = [], 0
    while off < wc:
        a, lc = divmod(j * wc + off, D_GRP)
        width = min(D_GRP - lc, wc - off)
        out.append((a, lc, off, width))
        off += width
    return out


def _chip_cols(g6_ref, j, wc):
    return jnp.concatenate([g6_ref[a, :, lc:lc + width] for a, lc, _, width in _col_pieces(j, wc)], axis=1)


def _mix_out(on_sb, on_dil, w_out, x, gt, seq):
    t, d = x.shape
    per = seq // TM

    def body(a_ref, b_ref, w_ref, x_ref, gt_ref, t_ref, xo_ref):
        tv = _nn(a_ref[...], w_ref[0:D_GRP, :]) + _nn(b_ref[...], w_ref[D_GRP:2 * D_GRP, :])
        t_ref[...] = tv
        xo_ref[...] = x_ref[...] + gt_ref[...] * tv

    row = pl.BlockSpec((TM, d), lambda m: (m, 0))
    half = pl.BlockSpec((TM, D_GRP), lambda m: (m, 0))
    return pl.pallas_call(
        body, name="mix_out", grid=(t // TM,),
        in_specs=[half, half, pl.BlockSpec((2 * D_GRP, d), lambda m: (0, 0)), row,
                  pl.BlockSpec((None, 1, d), lambda m: (m // per, 0, 0))],
        out_specs=[row, row],
        out_shape=[_sds((t, d), F32), _sds((t, d), F32)],
        compiler_params=_cp(1))(on_sb, on_dil, w_out, x, gt)


def _sb_masks():
    lane = lax.broadcasted_iota(jnp.int32, (1, 2 * HEAD_DIM), 1)
    hm0 = lane < HEAD_DIM
    rel = lax.broadcasted_iota(jnp.int32, (TQ, KB), 0) - lax.broadcasted_iota(jnp.int32, (TQ, KB), 1)
    kr = lax.broadcasted_iota(jnp.int32, (KB, KB), 0)
    kc = lax.broadcasted_iota(jnp.int32, (KB, KB), 1)
    return hm0, rel, kr, kc


def _headnorm_pair(o, gv, hm0):
    o2 = o * o
    ms0 = jnp.sum(jnp.where(hm0, o2, 0.0), axis=-1, keepdims=True) * (1.0 / HEAD_DIM)
    ms1 = jnp.sum(jnp.where(hm0, 0.0, o2), axis=-1, keepdims=True) * (1.0 / HEAD_DIM)
    r = jnp.where(hm0, lax.rsqrt(ms0 + EPS), lax.rsqrt(ms1 + EPS))
    return (o * r) * gv


SB_DEAD = -104.0


def _alive(c_l):
    return (jnp.max(c_l) > SB_DEAD).astype(jnp.int32)


def _sb_fwd(qkv6, g_sb, nb, seq, carry=None):
    nq = seq // TQ

    def body(q_ref, k_ref, v_ref, g_ref, o_ref, on_ref):
        qi = pl.program_id(2)
        hm0, rel, kr, kc = _sb_masks()
        upper = (kr > kc).astype(BF)
        qv = q_ref[...]
        qhs = [jnp.where(hm0, qv, jnp.zeros_like(qv)), jnp.where(hm0, jnp.zeros_like(qv), qv)]

        def block(kj, causal, c_ls, accs):
            ks = pl.multiple_of(kj * KB, KB)
            kb = k_ref[pl.ds(ks, KB), :]
            vb = v_ref[pl.ds(ks, KB), :]
            new_c, new_acc = [], []
            for qh, c_l, acc in zip(qhs, c_ls, accs):
                z = _nt(qh, kb) * SCALE
                sp = _softplus(z)
                ln = -sp if causal is None else jnp.where(causal, -sp, 0.0)
                suf = _nn2(ln, upper)
                w = jnp.exp((z - sp) + (suf + c_l))
                if causal is not None:
                    w = jnp.where(causal, w, 0.0)
                new_acc.append(acc + _nn(w.astype(BF), vb))
                new_c.append(c_l + (suf[:, 0:1] + ln[:, 0:1]))
            return new_c, new_acc

        zc = jnp.zeros((TQ, 1), F32)
        za = jnp.zeros((TQ, 2 * HEAD_DIM), F32)
        c_ls, accs = block(qi, rel > 0, [zc, zc], [za, za])

        def cond(carry):
            return jnp.logical_and(carry[0] <= qi, carry[1] > 0)

        def kbody(carry):
            it, _, c0, c1, a0, a1 = carry
            (c0, c1), (a0, a1) = block(qi - it, None, [c0, c1], [a0, a1])
            return it + 1, jnp.maximum(_alive(c0), _alive(c1)), c0, c1, a0, a1

        init = (jnp.int32(1), jnp.maximum(_alive(c_ls[0]), _alive(c_ls[1])), c_ls[0], c_ls[1], accs[0], accs[1])
        outs = lax.while_loop(cond, kbody, init)[4:]
        o = jnp.where(hm0, outs[0], outs[1])
        o_ref[...] = o
        on_ref[...] = _headnorm_pair(o, g_ref[...], hm0).astype(BF)

    w = 2 * HEAD_DIM
    full = lambda i: pl.BlockSpec((None, None, seq, w), lambda b, hp, q: (i, b, 0, hp))
    qblk = pl.BlockSpec((None, None, TQ, w), lambda b, hp, q: (0, b, q, hp))
    oblk = pl.BlockSpec((None, TQ, w), lambda b, hp, q: (b, q, hp))
    return _call(
        body, "sb_fwd", (nb, N_HEADS // 2, nq),
        [qblk, full(1), full(2), pl.BlockSpec((1, w), lambda b, hp, q: (0, hp))],
        [oblk, oblk],
        [_sds((nb, seq, D_GRP), F32), _sds((nb, seq, D_GRP), BF)],
        (qkv6, qkv6, qkv6, g_sb), carry=carry)


def _sb_bwd(qkv6, do, nb, seq, carry=None):
    nq = seq // TQ
    nk = seq // KB

    def body(q_ref, k_ref, v_ref, do_ref, out_ref, dk_acc, dv_acc, g_st, s_st):
        qi = pl.program_id(2)
        hm0, rel, kr, kc = _sb_masks()
        upper = (kr > kc).astype(BF)
        lower = (kr < kc).astype(BF)

        @pl.when(qi == 0)
        def _():
            dk_acc[...] = jnp.zeros_like(dk_acc)
            dv_acc[...] = jnp.zeros_like(dv_acc)

        qv = q_ref[...]
        dov = do_ref[...]
        qhs = [jnp.where(hm0, qv, jnp.zeros_like(qv)), jnp.where(hm0, jnp.zeros_like(qv), qv)]
        dohs = [jnp.where(hm0, dov, 0.0).astype(BF), jnp.where(hm0, 0.0, dov).astype(BF)]

        def weights(kj, causal, c_ls):
            ks = pl.multiple_of(kj * KB, KB)
            kb = k_ref[pl.ds(ks, KB), :]
            vb = v_ref[pl.ds(ks, KB), :]
            new_c, dv = [], None
            for hh, (qh, doh, c_l) in enumerate(zip(qhs, dohs, c_ls)):
                z = _nt(qh, kb) * SCALE
                sp = _softplus(z)
                ln = -sp if causal is None else jnp.where(causal, -sp, 0.0)
                suf = _nn2(ln, upper)
                lsz = z - sp
                w = jnp.exp(lsz + (suf + c_l))
                if causal is not None:
                    w = jnp.where(causal, w, 0.0)
                g_st[hh, kj] = w * _nt(doh, vb)
                s_st[hh, kj] = jnp.exp(lsz)
                part = _tn(w.astype(BF), doh)
                dv = part if dv is None else dv + part
                new_c.append(c_l + (suf[:, 0:1] + ln[:, 0:1]))
            dv_acc[pl.ds(ks, KB), :] += dv
            return new_c

        zc = jnp.zeros((TQ, 1), F32)
        c_ls = weights(qi, rel > 0, [zc, zc])

        def acond(carry):
            return jnp.logical_and(carry[0] <= qi, carry[1] > 0)

        def abody(carry):
            it, _, c0, c1 = carry
            c0, c1 = weights(qi - it, None, [c0, c1])
            return it + 1, jnp.maximum(_alive(c0), _alive(c1)), c0, c1

        n_used = lax.while_loop(
            acond, abody, (jnp.int32(1), jnp.maximum(_alive(c_ls[0]), _alive(c_ls[1])), c_ls[0], c_ls[1]))[0]

        def grads(kj, causal, c_gs, dqs):
            ks = pl.multiple_of(kj * KB, KB)
            kb = k_ref[pl.ds(ks, KB), :]
            new_c, new_dq, dk = [], [], None
            for hh, (qh, c_g, dq) in enumerate(zip(qhs, c_gs, dqs)):
                g = g_st[hh, kj]
                sig = s_st[hh, kj]
                pre = _nn2(g, lower)
                dz = g * (1.0 - sig) - sig * (pre + c_g)
                if causal is not None:
                    dz = jnp.where(causal, dz, 0.0)
                dzb = (dz * SCALE).astype(BF)
                part = _tn(dzb, qh)
                dk = part if dk is None else dk + part
                new_dq.append(dq + _nn(dzb, kb))
                new_c.append(c_g + (pre[:, KB - 1:KB] + g[:, KB - 1:KB]))
            dk_acc[pl.ds(ks, KB), :] += dk
            return new_c, new_dq

        za = jnp.zeros((TQ, 2 * HEAD_DIM), F32)

        def bbody(kj, carry):
            (c0, c1), (d0, d1) = grads(kj, None, carry[:2], carry[2:])
            return c0, c1, d0, d1

        c0, c1, d0, d1 = lax.fori_loop(qi - n_used + 1, qi, bbody, (zc, zc, za, za))
        _, dqs = grads(qi, rel > 0, [c0, c1], [d0, d1])
        dq = jnp.where(hm0, dqs[0], dqs[1])
        out_ref[0, pl.ds(pl.multiple_of(qi * TQ, TQ), TQ), :] = dq.astype(BF)

        @pl.when(qi == nq - 1)
        def _():
            out_ref[1] = dk_acc[...].astype(BF)
            out_ref[2] = dv_acc[...].astype(BF)

    w = 2 * HEAD_DIM
    full = lambda i: pl.BlockSpec((None, None, seq, w), lambda b, hp, q: (i, b, 0, hp))
    qblk = pl.BlockSpec((None, None, TQ, w), lambda b, hp, q: (0, b, q, hp))
    oblk = pl.BlockSpec((None, TQ, w), lambda b, hp, q: (b, q, hp))
    return _call(
        body, "sb_bwd", (nb, N_HEADS // 2, nq),
        [qblk, full(1), full(2), oblk],
        [pl.BlockSpec((3, None, seq, w), lambda b, hp, q: (0, b, 0, hp))],
        [_sds((6, nb, seq, D_GRP), BF)], (qkv6, qkv6, qkv6, do),
        scratch=[pltpu.VMEM((seq, w), F32), pltpu.VMEM((seq, w), F32),
                 pltpu.VMEM((2, nk, TQ, KB), F32), pltpu.VMEM((2, nk, TQ, KB), F32)],
        carry=carry)


def _t5_bucket(n):
    max_exact = N_BUCKETS // 2
    nf = np.maximum(n, 1).astype(np.float32)
    large = max_exact + (np.log(nf / max_exact) / math.log(MAX_DISTANCE / max_exact)
                         * (N_BUCKETS - max_exact)).astype(np.int32)
    large = np.minimum(large, N_BUCKETS - 1)
    return np.where(n < max_exact, n, large).astype(np.int32)


def _bucket_map(dilation):
    step = BLOCK + np.arange(BLOCK)[:, None] - np.arange(2 * BLOCK)[None, :]
    return _t5_bucket(np.clip(step, 0, N_STEPS) * dilation)


GRP_HEADS = 4
GRP_W = GRP_HEADS * HEAD_DIM


def _dil_masks():
    lane = lax.broadcasted_iota(jnp.int32, (1, GRP_W), 1)
    heads = [jnp.logical_and(lane >= HEAD_DIM * i, lane < HEAD_DIM * (i + 1)) for i in range(GRP_HEADS)]
    iq = lax.broadcasted_iota(jnp.int32, (BLOCK, BLOCK), 0)
    ik = lax.broadcasted_iota(jnp.int32, (BLOCK, BLOCK), 1)
    return heads, ik <= iq, ik >= iq


def _dil_rows(n):
    rs = pl.multiple_of(n * BLOCK, BLOCK)
    ps = pl.multiple_of(jnp.maximum(n - 1, 0) * BLOCK, BLOCK)
    return pl.ds(rs, BLOCK), pl.ds(ps, BLOCK)


def _dil_probs(qh, kc, kp, b_ref, hh, valid_c, valid_p):
    zc = _nt(qh, kc) * SCALE + b_ref[hh, :, BLOCK:2 * BLOCK]
    zp = _nt(qh, kp) * SCALE + b_ref[hh, :, 0:BLOCK]
    zc = jnp.where(valid_c, zc, NEG_INF)
    zp = jnp.where(valid_p, zp, NEG_INF)
    m = jnp.maximum(jnp.max(zc, axis=-1, keepdims=True), jnp.max(zp, axis=-1, keepdims=True))
    ec = jnp.exp(zc - m)
    ep = jnp.exp(zp - m)
    den = jnp.sum(ec, axis=-1, keepdims=True) + jnp.sum(ep, axis=-1, keepdims=True)
    return ec, ep, den, m


def _dil_fwd(qkv6r, base, bias, nb, sub_len, dilation):
    n_blk = sub_len // BLOCK

    def body(q_ref, k_ref, v_ref, b_ref, o_ref, l_ref):
        heads, valid_c, valid_p0 = _dil_masks()

        def nbody(n, carry):
            cur, prev = _dil_rows(n)
            valid_p = jnp.logical_and(valid_p0, n > 0)
            for gi in range(N_HEADS // GRP_HEADS):
                lanes = slice(gi * GRP_W, (gi + 1) * GRP_W)
                qv, kc, kp = q_ref[cur, lanes], k_ref[cur, lanes], k_ref[prev, lanes]
                vc, vp = v_ref[cur, lanes], v_ref[prev, lanes]
                o = jnp.zeros((BLOCK, GRP_W), F32)
                lse = jnp.zeros((BLOCK, GRP_W), F32)
                for i, hm in enumerate(heads):
                    qh = jnp.where(hm, qv, jnp.zeros_like(qv))
                    ec, ep, den, m = _dil_probs(qh, kc, kp, b_ref, gi * GRP_HEADS + i, valid_c, valid_p)
                    o = jnp.where(hm, (_nn(ec.astype(BF), vc) + _nn(ep.astype(BF), vp)) / den, o)
                    lse = jnp.where(hm, m + jnp.log(den), lse)
                o_ref[cur, lanes] = o
                l_ref[cur, lanes] = lse
            return carry

        lax.fori_loop(0, n_blk, nbody, 0)

    seqblk = lambda i: pl.BlockSpec((None, None, sub_len, D_GRP), lambda b, r: (i, b, 0, r))
    oblk = pl.BlockSpec((None, sub_len, D_GRP), lambda b, r: (b, 0, r))
    shp = _sds((nb, sub_len, dilation * D_GRP), F32)
    return pl.pallas_call(
        body, name="dil_fwd_%d" % dilation, grid=(nb, dilation),
        in_specs=[seqblk(base), seqblk(base + 1), seqblk(base + 2), _whole(bias)],
        out_specs=[oblk, oblk], out_shape=[shp, shp],
        compiler_params=_cp(2))(qkv6r, qkv6r, qkv6r, bias)


def _dil_bwd(qkv6r, base, bias, do_c, dd_c, nb, sub_len, dilation, carry=None):
    n_blk = sub_len // BLOCK

    def body(q_ref, k_ref, v_ref, b_ref, do_ref, dd_ref, out_ref, a_ref):
        heads, valid_c, valid_p0 = _dil_masks()
        first = jnp.logical_and(pl.program_id(0) == 0, pl.program_id(1) == 0)

        @pl.when(first)
        def _():
            a_ref[...] = jnp.zeros_like(a_ref)

        out_ref[1] = jnp.zeros((sub_len, D_GRP), F32)
        out_ref[2] = jnp.zeros((sub_len, D_GRP), F32)

        def nbody(n, carry):
            cur, prev = _dil_rows(n)
            valid_p = jnp.logical_and(valid_p0, n > 0)
            for gi in range(N_HEADS // GRP_HEADS):
                lanes = slice(gi * GRP_W, (gi + 1) * GRP_W)
                qv, kc, kp = q_ref[cur, lanes], k_ref[cur, lanes], k_ref[prev, lanes]
                vc, vp = v_ref[cur, lanes], v_ref[prev, lanes]
                dov, ddv = do_ref[cur, lanes], dd_ref[cur, lanes]
                dq = jnp.zeros((BLOCK, GRP_W), F32)
                dkc = jnp.zeros((BLOCK, GRP_W), F32)
                dkp = jnp.zeros((BLOCK, GRP_W), F32)
                dvc = jnp.zeros((BLOCK, GRP_W), F32)
                dvp = jnp.zeros((BLOCK, GRP_W), F32)
                for i, hm in enumerate(heads):
                    h = gi * GRP_HEADS + i
                    qh = jnp.where(hm, qv, jnp.zeros_like(qv))
                    doh = jnp.where(hm, dov, 0.0).astype(BF)
                    ddh = jnp.sum(jnp.where(hm, ddv, 0.0), axis=-1, keepdims=True) * (1.0 / HEAD_DIM)
                    ec, ep, den, _ = _dil_probs(qh, kc, kp, b_ref, h, valid_c, valid_p)
                    inv = 1.0 / den
                    pc = ec * inv
                    pp = ep * inv
                    dzc = pc * (_nt(doh, vc) + ddh)
                    dzp = pp * (_nt(doh, vp) + ddh)
                    a_ref[h, :, BLOCK:2 * BLOCK] += dzc
                    a_ref[h, :, 0:BLOCK] += dzp
                    dzcb = (dzc * SCALE).astype(BF)
                    dzpb = (dzp * SCALE).astype(BF)
                    dq = jnp.where(hm, _nn(dzcb, kc) + _nn(dzpb, kp), dq)
                    dkc = dkc + _tn(dzcb, qh)
                    dkp = dkp + _tn(dzpb, qh)
                    dvc = dvc + _tn(pc.astype(BF), doh)
                    dvp = dvp + _tn(pp.astype(BF), doh)
                out_ref[0, cur, lanes] = dq
                out_ref[1, cur, lanes] += dkc
                out_ref[1, prev, lanes] += dkp
                out_ref[2, cur, lanes] += dvc
                out_ref[2, prev, lanes] += dvp
            return carry

        lax.fori_loop(0, n_blk, nbody, 0)

    seqblk = lambda i: pl.BlockSpec((None, None, sub_len, D_GRP), lambda b, r: (i, b, 0, r))
    oblk = pl.BlockSpec((None, sub_len, D_GRP), lambda b, r: (b, 0, r))
    return _call(
        body, "dil_bwd_%d" % dilation, (nb, dilation),
        [seqblk(base), seqblk(base + 1), seqblk(base + 2), _whole(bias), oblk, oblk],
        [pl.BlockSpec((3, None, sub_len, D_GRP), lambda b, r: (0, b, 0, r)),
         pl.BlockSpec((N_HEADS, BLOCK, 2 * BLOCK), lambda b, r: (0, 0, 0))],
        [_sds((3, nb, sub_len, dilation * D_GRP), F32), _sds((N_HEADS, BLOCK, 2 * BLOCK), F32)],
        (qkv6r, qkv6r, qkv6r, bias, do_c, dd_c), carry=carry)


def _group_ones():
    idx = np.arange(D_GRP) // HEAD_DIM
    return jnp.asarray((idx[:, None] == idx[None, :]).astype(np.float32), dtype=BF)


def _dil_alphas(l1, l4, l16):
    mx = jnp.maximum(jnp.maximum(l1, l4), l16)
    e1 = jnp.exp(l1 - mx)
    e4 = jnp.exp(l4 - mx)
    e16 = jnp.exp(l16 - mx)
    den = e1 + e4 + e16
    return e1 / den, e4 / den, e16 / den


def _residue_spec(dil):
    return pl.BlockSpec((TM // dil, dil * D_GRP), lambda m: (m, 0))


def _from_residue(src, dil, cg, buf):
    if dil == 1:
        return src[:, cg * 128:(cg + 1) * 128]
    for r in range(dil):
        buf[pl.ds(r, TM // dil, stride=dil), :] = src[:, r * D_GRP + cg * 128:r * D_GRP + (cg + 1) * 128]
    return buf[...]


def _to_residue(dst, dil, cg, buf, val):
    if dil == 1:
        dst[:, cg * 128:(cg + 1) * 128] = val
        return
    buf[...] = val
    for r in range(dil):
        dst[:, r * D_GRP + cg * 128:r * D_GRP + (cg + 1) * 128] = buf[pl.ds(r, TM // dil, stride=dil), :]


def _pair_sum(x, hm0):
    s0 = jnp.sum(jnp.where(hm0, x, 0.0), axis=-1, keepdims=True)
    s1 = jnp.sum(jnp.where(hm0, 0.0, x), axis=-1, keepdims=True)
    return jnp.where(hm0, s0, s1)


def _dil_comb(os, ls, g_dil):
    t = os[0].shape[0]
    dils = [dil for _, dil in DIL_CONFIGS]

    def body(o1, l1, o4, l4, o16, l16, g_ref, o_ref, on_ref, b0, b1, b2, b3):
        hm0 = lax.broadcasted_iota(jnp.int32, (1, 128), 1) < HEAD_DIM
        for cg in range(D_GRP // 128):
            lanes = slice(cg * 128, (cg + 1) * 128)
            ov = [_from_residue(src, dil, cg, buf) for src, dil, buf in zip((o1, o4, o16), dils, (None, b0, b1))]
            lv = [_from_residue(src, dil, cg, buf) for src, dil, buf in zip((l1, l4, l16), dils, (None, b2, b3))]
            a1, a4, a16 = _dil_alphas(*lv)
            o = a1 * ov[0] + a4 * ov[1] + a16 * ov[2]
            o_ref[:, lanes] = o
            on_ref[:, lanes] = _headnorm_pair(o, g_ref[:, lanes], hm0).astype(BF)

    blk = pl.BlockSpec((TM, D_GRP), lambda m: (m, 0))
    specs = [_residue_spec(dil) for dil in dils for _ in range(2)]
    return pl.pallas_call(
        body, name="dil_comb", grid=(t // TM,),
        in_specs=specs + [pl.BlockSpec((1, D_GRP), lambda m: (0, 0))],
        out_specs=[blk, blk],
        out_shape=[_sds((t, D_GRP), F32), _sds((t, D_GRP), BF)],
        scratch_shapes=[pltpu.VMEM((TM, 128), F32)] * 4,
        compiler_params=_cp(1))(os[0], ls[0], os[1], ls[1], os[2], ls[2], g_dil)


def _dil_comb_bwd(do, os, ls):
    t = do.shape[0]
    dils = [dil for _, dil in DIL_CONFIGS]

    def body(do_ref, o1, l1, o4, l4, o16, l16, d1, d4, d16, e1, e4, e16, b0, b1, b2, b3):
        hm0 = lax.broadcasted_iota(jnp.int32, (1, 128), 1) < HEAD_DIM
        for cg in range(D_GRP // 128):
            dov = do_ref[:, cg * 128:(cg + 1) * 128]
            ov = [_from_residue(src, dil, cg, buf) for src, dil, buf in zip((o1, o4, o16), dils, (None, b0, b1))]
            lv = [_from_residue(src, dil, cg, buf) for src, dil, buf in zip((l1, l4, l16), dils, (None, b2, b3))]
            al = _dil_alphas(*lv)
            sbar = al[0] * _pair_sum(dov * ov[0], hm0)
            for a_c, o_c in zip(al[1:], ov[1:]):
                sbar = sbar + a_c * _pair_sum(dov * o_c, hm0)
            for a_c, dil, dref, eref in zip(al, dils, (d1, d4, d16), (e1, e4, e16)):
                _to_residue(dref, dil, cg, b0, a_c * dov)
                _to_residue(eref, dil, cg, b1, -a_c * sbar)

    specs = [_residue_spec(dil) for dil in dils]
    return pl.pallas_call(
        body, name="dil_comb_bwd", grid=(t // TM,),
        in_specs=[pl.BlockSpec((TM, D_GRP), lambda m: (m, 0))] + [sp for sp in specs for _ in range(2)],
        out_specs=specs + specs,
        out_shape=[_sds((t // dil, dil * D_GRP), F32) for dil in dils] * 2,
        scratch_shapes=[pltpu.VMEM((TM, 128), F32)] * 4,
        compiler_params=_cp(1))(do, os[0], ls[0], os[1], ls[1], os[2], ls[2])


def _dqkv_dil_sum(ds, dqkv6):
    t = dqkv6.shape[1]
    dils = [dil for _, dil in DIL_CONFIGS]

    def body(*refs):
        srcs, o_ref, acc = refs[:len(dils)], refs[len(dils) + 1], refs[len(dils) + 2]
        for a in range(3):
            for cg in range(D_GRP // 128):
                for src, dil in zip(srcs, dils):
                    for r in range(dil):
                        part = src[a, :, r * D_GRP + cg * 128:r * D_GRP + (cg + 1) * 128]
                        rows = pl.ds(r, TM // dil, stride=dil) if dil > 1 else slice(None)
                        if dil == dils[0]:
                            acc[rows, :] = part
                        else:
                            acc[rows, :] += part
                o_ref[a, :, cg * 128:(cg + 1) * 128] = acc[...].astype(BF)

    return pl.pallas_call(
        body, name="dqkv_dil_sum", grid=(t // TM,),
        in_specs=[pl.BlockSpec((3, TM // dil, dil * D_GRP), lambda m: (0, m, 0)) for dil in dils]
        + [pl.BlockSpec(memory_space=pl.ANY)],
        out_specs=pl.BlockSpec((3, TM, D_GRP), lambda m: (1, m, 0)),
        out_shape=_sds((6, t, D_GRP), BF), input_output_aliases={len(dils): 0},
        scratch_shapes=[pltpu.VMEM((TM, 128), F32)],
        compiler_params=_cp(1))(*ds, dqkv6)


def _relbias_grad(a_all, onehot):
    def body(a_ref, oh_ref, o_ref):
        acc = jnp.zeros((N_HEADS, N_BUCKETS), F32)
        for c in range(len(DIL_CONFIGS)):
            av = a_ref[c]
            hi = av.astype(BF)
            lo = (av - hi.astype(F32)).astype(BF)
            acc = acc + _nt(hi, oh_ref[c]) + _nt(lo, oh_ref[c])
        o_ref[...] = acc

    return pl.pallas_call(body, name="relbias_grad", out_shape=_sds((N_HEADS, N_BUCKETS), F32),
                          compiler_params=_cp())(a_all, onehot)


def _headnorm_bwd(dn, o, gv, mv):
    ms = _nn2(o * o, mv) * (1.0 / HEAD_DIM)
    r = lax.rsqrt(ms + EPS)
    nrm = o * r
    dg = jnp.sum(dn * nrm, axis=0, keepdims=True)
    dnn = dn * gv
    do = r * (dnn - nrm * (_nn2(dnn * nrm, mv) * (1.0 / HEAD_DIM)))
    return do, dg


def _mix_bwd_out(dx, gt, tv, w_out, o_sb, o_dil, on_sb, on_dil, g_sb, g_dil, ones_g, seq):
    t, d = dx.shape
    per = seq // TM
    nb = t // seq

    def body(dx_ref, gt_ref, t_ref, w_ref, osb, odl, onsb, ondl, gsb, gdl, m_ref,
             dosb, dodl, dgt_ref, dgsb, dgdl, dw_ref):
        m = pl.program_id(0)
        dxv = dx_ref[...]
        dt = (gt_ref[...] * dxv).astype(BF)
        _acc_rows(dgt_ref, jnp.sum(dxv * t_ref[...], axis=0, keepdims=True), m % per == 0)
        mv = m_ref[...]
        don_sb = _nt(dt, w_ref[0:D_GRP, :])
        don_dl = _nt(dt, w_ref[D_GRP:2 * D_GRP, :])
        do1, dg1 = _headnorm_bwd(don_sb, osb[...], gsb[...], mv)
        do2, dg2 = _headnorm_bwd(don_dl, odl[...], gdl[...], mv)
        dosb[...] = do1
        dodl[...] = do2
        _acc_rows(dgsb, dg1, m == 0)
        _acc_rows(dgdl, dg2, m == 0)
        p1 = _tn(onsb[...], dt)
        p2 = _tn(ondl[...], dt)

        @pl.when(m == 0)
        def _():
            dw_ref[0:D_GRP, :] = p1
            dw_ref[D_GRP:2 * D_GRP, :] = p2

        @pl.when(m != 0)
        def _():
            dw_ref[0:D_GRP, :] += p1
            dw_ref[D_GRP:2 * D_GRP, :] += p2

    row = pl.BlockSpec((TM, d), lambda m: (m, 0))
    half = pl.BlockSpec((TM, D_GRP), lambda m: (m, 0))
    ex = pl.BlockSpec((None, 1, d), lambda m: (m // per, 0, 0))
    gvec = pl.BlockSpec((1, D_GRP), lambda m: (0, 0))
    wblk = pl.BlockSpec((2 * D_GRP, d), lambda m: (0, 0))
    return pl.pallas_call(
        body, name="mix_bwd_out", grid=(t // TM,),
        in_specs=[row, ex, row, wblk, half, half, half, half, gvec, gvec,
                  pl.BlockSpec((D_GRP, D_GRP), lambda m: (0, 0))],
        out_specs=[half, half, ex, gvec, gvec, wblk],
        out_shape=[_sds((t, D_GRP), F32), _sds((t, D_GRP), F32), _sds((nb, 1, d), F32),
                   _sds((1, D_GRP), F32), _sds((1, D_GRP), F32), _sds((2 * D_GRP, d), F32)],
        compiler_params=_cp(1))(dx, gt, tv, w_out, o_sb, o_dil, on_sb, on_dil, g_sb, g_dil, ones_g)


def _dw_in(h, dqkv6, carry=None):
    t, d = h.shape
    wc = 6 * D_GRP // N_CHIPS

    def body(h_ref, g_ref, o_ref):
        kt = pl.program_id(0)
        hv = h_ref[...]
        for j in range(N_CHIPS):
            p = _tn(hv, _chip_cols(g_ref, j, wc))

            @pl.when(kt == 0)
            def _(p=p, j=j):
                o_ref[j, 0] = p

            @pl.when(kt != 0)
            def _(p=p, j=j):
                o_ref[j, 0] += p

    return _call(
        body, "dw_in", (t // TM,),
        [pl.BlockSpec((TM, d), lambda kt: (kt, 0)), pl.BlockSpec((6, TM, D_GRP), lambda kt: (0, kt, 0))],
        [pl.BlockSpec((N_CHIPS, 1, d, wc), lambda kt: (0, 0, 0, 0))],
        [_sds((N_CHIPS, 1, d, wc), F32)], (h, dqkv6), carry=carry)


def _mix_bwd_dh(dqkv6, w_in, x, g, sc, dxo, seq, carry=None):
    _, t, _ = dqkv6.shape
    d = x.shape[-1]
    wc = w_in.shape[-1]
    per = seq // TM
    nb = t // seq

    def body(g6_ref, w_ref, x_ref, g_ref, sc_ref, dxo_ref, dx_ref, dsh_ref, dsc_ref, dg_ref):
        m = pl.program_id(0)
        dh = _nt(_chip_cols(g6_ref, 0, wc), w_ref[0, 0])
        for j in range(1, N_CHIPS):
            dh = dh + _nt(_chip_cols(g6_ref, j, wc), w_ref[j, 0])
        dx, dsh, dsc, dg = _modnorm_bwd_tile(dh, x_ref[...], g_ref[...], sc_ref[...], dxo_ref[...])
        dx_ref[...] = dx
        _acc_rows(dsh_ref, dsh, m % per == 0)
        _acc_rows(dsc_ref, dsc, m % per == 0)
        _acc_rows(dg_ref, dg, m == 0)

    row = pl.BlockSpec((TM, d), lambda m: (m, 0))
    ex = pl.BlockSpec((None, 1, d), lambda m: (m // per, 0, 0))
    vec = pl.BlockSpec((1, d), lambda m: (0, 0))
    return _call(
        body, "mix_bwd_dh", (t // TM,),
        [pl.BlockSpec((6, TM, D_GRP), lambda m: (0, m, 0)), _whole(w_in), row, vec, ex, row],
        [row, ex, ex, vec],
        [_sds((t, d), F32), _sds((nb, 1, d), F32), _sds((nb, 1, d), F32), _sds((1, d), F32)],
        (dqkv6, w_in, x, g, sc, dxo), carry=carry)


def _final_loss(x, g, target):
    t, d = x.shape
    steps = t // TM

    def body(x_ref, g_ref, t_ref, dx_ref, dg_ref, loss_ref, lacc):
        m = pl.program_id(0)
        xv = x_ref[...]
        gv = g_ref[...]
        r = lax.rsqrt(jnp.mean(xv * xv, axis=-1, keepdims=True) + EPS)
        n = xv * r
        err = n * gv - t_ref[...]
        dy = err * (1.0 / d)
        _acc_rows(dg_ref, jnp.sum(dy * n, axis=0, keepdims=True), m == 0)
        dn = dy * gv
        dx_ref[...] = r * (dn - n * jnp.mean(dn * n, axis=-1, keepdims=True))
        _acc_rows(lacc, jnp.sum(err * err, axis=0, keepdims=True), m == 0)

        @pl.when(m == steps - 1)
        def _():
            tot = jnp.sum(lacc[...], axis=-1, keepdims=True) * (0.5 / d)
            loss_ref[...] = jnp.broadcast_to(tot, (1, 128))

    row = pl.BlockSpec((TM, d), lambda m: (m, 0))
    vec = pl.BlockSpec((1, d), lambda m: (0, 0))
    return pl.pallas_call(
        body, name="final_loss", grid=(steps,),
        in_specs=[row, vec, row],
        out_specs=[row, vec, pl.BlockSpec((1, 128), lambda m: (0, 0))],
        out_shape=[_sds((t, d), F32), _sds((1, d), F32), _sds((1, 128), F32)],
        scratch_shapes=[pltpu.VMEM((1, d), F32)],
        compiler_params=_cp(1))(x, g, target)


def _row_tile(rows, cols):
    best = rows
    for tr in range(8, rows + 1, 8):
        if rows % tr == 0 and tr * cols * 4 <= (1 << 20):
            best = tr
    if best * cols * 4 > (1 << 21):
        best = 8
    return best


def _adamw(w, g_arr, g_sel, m, v):
    rows, cols = w.shape
    tr = _row_tile(rows, cols)
    b1c = 1.0 - ADAM_B1 ** ADAM_STEP
    b2c = 1.0 - ADAM_B2 ** ADAM_STEP

    def body(w_ref, g_ref, m_ref, v_ref, go_ref, d_ref, mo_ref, vo_ref):
        gv = g_ref[...]
        mn = ADAM_B1 * m_ref[...] + (1.0 - ADAM_B1) * gv
        vn = ADAM_B2 * v_ref[...] + (1.0 - ADAM_B2) * (gv * gv)
        go_ref[...] = gv
        mo_ref[...] = mn
        vo_ref[...] = vn
        d_ref[...] = -ADAM_LR * ((mn / b1c) / (jnp.sqrt(vn / b2c) + ADAM_EPS) + ADAM_WD * w_ref[...])

    blk = pl.BlockSpec((tr, cols), lambda i: (i, 0))
    shp = _sds((rows, cols), F32)
    return pl.pallas_call(
        body, name="adamw", grid=(rows // tr,),
        in_specs=[blk, pl.BlockSpec((None, tr, cols), lambda i: (g_sel, i, 0)), blk, blk],
        out_specs=[blk] * 4, out_shape=[shp] * 4,
        compiler_params=_cp(1))(w, g_arr, m, v)


def _flip(v, bit):
    return 1 - v if bit else v


def _my_place():
    x, y, c = lax.axis_index("x"), lax.axis_index("y"), lax.axis_index("c")
    return x, y, c


class _Exchange:
    def __init__(self, operands, out_shape, aliases, sems, start, finish):
        self.operands, self.out_shape, self.aliases, self.sems = list(operands), list(out_shape), dict(aliases), list(sems)
        self.start, self.finish = start, finish


def _join(exchanges):
    exchanges = [e for e in exchanges if e is not None]
    if not exchanges:
        return None
    ops, outs, sems, aliases, spans = [], [], [], {}, []
    for e in exchanges:
        spans.append((len(ops), len(outs), len(sems), e))
        for i, j in e.aliases.items():
            aliases[len(ops) + i] = len(outs) + j
        ops += e.operands
        outs += e.out_shape
        sems += e.sems

    def run(which):
        def go(ins, res, sm):
            for io, oo, so, e in spans:
                getattr(e, which)(ins[io:io + len(e.operands)], res[oo:oo + len(e.out_shape)], sm[so:so + len(e.sems)])
        return go

    return _Exchange(ops, outs, aliases, sems, run("start"), run("finish"))


def _call(body, name, grid, in_specs, out_specs, out_shape, args, scratch=(), carry=None):
    in_specs, out_specs, out_shape, scratch = list(in_specs), list(out_specs), list(out_shape), list(scratch)
    if carry is None:
        return pl.pallas_call(body, name=name, grid=grid, in_specs=in_specs, out_specs=out_specs,
                              out_shape=out_shape, scratch_shapes=scratch,
                              compiler_params=_cp(len(grid)))(*args)
    n_in, n_out, n_s = len(in_specs), len(out_specs), len(scratch)
    c_in, c_out = len(carry.operands), len(carry.out_shape)
    any_spec = pl.BlockSpec(memory_space=pl.ANY)

    def wrapped(*refs):
        ins, cins = refs[:n_in], refs[n_in:n_in + c_in]
        o0 = n_in + c_in
        outs, couts = refs[o0:o0 + n_out], refs[o0 + n_out:o0 + n_out + c_out]
        s0 = o0 + n_out + c_out
        scr, sems = refs[s0:s0 + n_s], refs[s0 + n_s:]
        first = pl.program_id(0) == 0
        last = pl.program_id(0) == grid[0] - 1
        for ax in range(1, len(grid)):
            first = jnp.logical_and(first, pl.program_id(ax) == 0)
            last = jnp.logical_and(last, pl.program_id(ax) == grid[ax] - 1)

        @pl.when(first)
        def _():
            carry.start(cins, couts, sems)

        body(*ins, *outs, *scr)

        @pl.when(last)
        def _():
            carry.finish(cins, couts, sems)

    return pl.pallas_call(
        wrapped, name=name, grid=grid, in_specs=in_specs + [any_spec] * c_in,
        out_specs=out_specs + [any_spec] * c_out, out_shape=out_shape + carry.out_shape,
        scratch_shapes=scratch + carry.sems,
        input_output_aliases={n_in + i: n_out + j for i, j in carry.aliases.items()},
        compiler_params=_cp(len(grid)))(*args, *carry.operands)


def _alone(name, ex):
    any_spec = pl.BlockSpec(memory_space=pl.ANY)
    c_in, c_out = len(ex.operands), len(ex.out_shape)

    def body(*refs):
        ins, outs, sems = refs[:c_in], refs[c_in:c_in + c_out], refs[c_in + c_out:]
        ex.start(ins, outs, sems)
        ex.finish(ins, outs, sems)

    return pl.pallas_call(
        body, name=name, in_specs=[any_spec] * c_in, out_specs=[any_spec] * c_out, out_shape=ex.out_shape,
        scratch_shapes=ex.sems, input_output_aliases=ex.aliases, compiler_params=_cp())(*ex.operands)


def _ada_fwd(c_pad, w_ada, b_shard):
    d = c_pad.shape[-1]
    cols = w_ada.shape[-1]
    chunk = 384

    def body(c_ref, w_ref, b_ref, call_ref, mod_ref, part, s1, r1, s2, r2):
        x, y, c = _my_place()
        dev = 4 * x + 2 * y + c
        chip = 2 * x + y
        call_ref[dev] = c_ref[...]

        def c_copy(k):
            px, py, pc = _flip(x, (k >> 2) & 1), _flip(y, (k >> 1) & 1), _flip(c, k & 1)
            return px, py, pc

        sends = []
        for k in range(1, N_DEV):
            px, py, pc = c_copy(k)
            cp = pltpu.make_async_remote_copy(src_ref=c_ref, dst_ref=call_ref.at[dev], send_sem=s1.at[k - 1],
                                              recv_sem=r1.at[k - 1], device_id=(px, py, pc), device_id_type=MESH)
            cp.start()
            sends.append(cp)
        for k in range(1, N_DEV):
            px, py, pc = c_copy(k)
            pltpu.make_async_remote_copy(src_ref=c_ref, dst_ref=call_ref.at[4 * px + 2 * py + pc],
                                         send_sem=s1.at[k - 1], recv_sem=r1.at[k - 1],
                                         device_id=(px, py, pc), device_id_type=MESH).wait_recv()
        for cp in sends:
            cp.wait_send()

        cs = call_ref[...].reshape(N_DEV * 8, d)
        sc = (cs * jax.nn.sigmoid(cs)).astype(BF)
        for n0 in range(0, cols, chunk):
            blk = _nn(sc, w_ref[:, n0:n0 + chunk].astype(BF)) + b_ref[:, n0:n0 + chunk]
            part[:, :, n0:n0 + chunk] = blk.reshape(N_DEV, 8, chunk)

        mod_ref[chip] = part[dev]
        sends = []
        for kk in range(1, N_CHIPS):
            px, py = _flip(x, (kk >> 1) & 1), _flip(y, kk & 1)
            cp = pltpu.make_async_remote_copy(src_ref=part.at[4 * px + 2 * py + c], dst_ref=mod_ref.at[chip],
                                              send_sem=s2.at[kk - 1], recv_sem=r2.at[kk - 1],
                                              device_id=(px, py, c), device_id_type=MESH)
            cp.start()
            sends.append(cp)
        for kk in range(1, N_CHIPS):
            px, py = _flip(x, (kk >> 1) & 1), _flip(y, kk & 1)
            pltpu.make_async_remote_copy(src_ref=part.at[dev], dst_ref=mod_ref.at[2 * px + py],
                                         send_sem=s2.at[kk - 1], recv_sem=r2.at[kk - 1],
                                         device_id=(px, py, c), device_id_type=MESH).wait_recv()
        for cp in sends:
            cp.wait_send()

    return pl.pallas_call(
        body, name="ada_fwd",
        out_shape=[_sds((N_DEV, 8, d), F32), _sds((N_CHIPS, 8, cols), F32)],
        scratch_shapes=[pltpu.VMEM((N_DEV, 8, cols), F32),
                        pltpu.SemaphoreType.DMA((N_DEV - 1,)), pltpu.SemaphoreType.DMA((N_DEV - 1,)),
                        pltpu.SemaphoreType.DMA((N_CHIPS - 1,)), pltpu.SemaphoreType.DMA((N_CHIPS - 1,))],
        compiler_params=_cp())(c_pad, w_ada, b_shard)


def _ag_weights(bufs):
    n = len(bufs)

    def place():
        x, y, c = _my_place()
        others = [(_flip(x, (kk >> 1) & 1), _flip(y, kk & 1)) for kk in range(1, N_CHIPS)]
        return x, y, c, 2 * x + y, others

    def half(b, which):
        hr = bufs[b].shape[2] // 2
        return pl.ds(pl.multiple_of(which * hr, 16), hr)

    def ici(outs, sems, b, i, slot, x, y, c, px, py):
        rows = outs[b].at[slot, :, half(b, c), :]
        return pltpu.make_async_remote_copy(
            src_ref=rows, dst_ref=rows, send_sem=sems[0].at[3 * b + i], recv_sem=sems[1].at[3 * b + i],
            device_id=(px, py, c), device_id_type=MESH)

    def d2d(outs, sems, b, i, slot, x, y, c, which):
        rows = outs[b].at[slot, :, half(b, which), :]
        return pltpu.make_async_remote_copy(
            src_ref=rows, dst_ref=rows, send_sem=sems[2].at[3 * b + i], recv_sem=sems[3].at[3 * b + i],
            device_id=(x, y, 1 - c), device_id_type=MESH)

    def start(ins, outs, sems):
        x, y, c, chip, others = place()
        for b in range(n):
            for i, (px, py) in enumerate(others):
                ici(outs, sems, b, i, chip, x, y, c, px, py).start()

    def finish(ins, outs, sems):
        x, y, c, chip, others = place()
        for b in range(n):
            for i, (px, py) in enumerate(others):
                ici(outs, sems, b, i, 2 * px + py, x, y, c, px, py).wait_recv()
                d2d(outs, sems, b, i, 2 * px + py, x, y, c, c).start()
        for b in range(n):
            for i, (px, py) in enumerate(others):
                d2d(outs, sems, b, i, 2 * px + py, x, y, c, 1 - c).wait_recv()
        for b in range(n):
            for i, (px, py) in enumerate(others):
                ici(outs, sems, b, i, chip, x, y, c, px, py).wait_send()
                d2d(outs, sems, b, i, 2 * px + py, x, y, c, c).wait_send()

    return _Exchange(bufs, [_sds(s.shape, s.dtype) for s in bufs], {i: i for i in range(n)},
                     [pltpu.SemaphoreType.DMA((3 * n,))] * 4, start, finish)


def _rs_d2d(grads):
    n = len(grads)

    def copy(ins, outs, sems, b):
        x, y, c = _my_place()
        hr = grads[b].shape[2] // 2
        theirs = pl.ds(pl.multiple_of((1 - c) * hr, 8), hr)
        return pltpu.make_async_remote_copy(
            src_ref=ins[b].at[:, :, theirs, :], dst_ref=outs[b], send_sem=sems[0].at[b], recv_sem=sems[1].at[b],
            device_id=(x, y, 1 - c), device_id_type=MESH)

    def start(ins, outs, sems):
        for b in range(n):
            copy(ins, outs, sems, b).start()

    def finish(ins, outs, sems):
        for b in range(n):
            copy(ins, outs, sems, b).wait()

    return _Exchange(grads, [_sds(g.shape[:2] + (g.shape[2] // 2, g.shape[3]), F32) for g in grads], {},
                     [pltpu.SemaphoreType.DMA((n,))] * 2, start, finish)


def _add_halves(core, g, land):
    nchip, ng, rows, cols = g.shape
    hr = rows // 2
    tr = _row_tile(hr, cols)
    steps = hr // tr

    def body(core_ref, g_ref, l_ref, o_ref):
        del core_ref
        o_ref[...] = (g_ref[...] + l_ref[...]).astype(BF)

    return pl.pallas_call(
        body, name="add_halves",
        grid_spec=pltpu.PrefetchScalarGridSpec(
            num_scalar_prefetch=1, grid=(nchip, ng, steps),
            in_specs=[pl.BlockSpec((None, None, tr, cols), lambda j, a, i, cr: (j, a, cr[0] * steps + i, 0)),
                      pl.BlockSpec((None, None, tr, cols), lambda j, a, i, cr: (j, a, i, 0))],
            out_specs=pl.BlockSpec((None, None, tr, cols), lambda j, a, i, cr: (j, a, i, 0))),
        out_shape=_sds((nchip, ng, hr, cols), BF),
        compiler_params=_cp(3))(core, g, land)


def _rs_ici(parts):
    n = len(parts)

    def copies(ins, outs, sems):
        x, y, c = _my_place()
        chip = 2 * x + y
        for b in range(n):
            for kk in range(1, N_CHIPS):
                px, py = _flip(x, (kk >> 1) & 1), _flip(y, kk & 1)
                k = 3 * b + kk - 1
                send = pltpu.make_async_remote_copy(
                    src_ref=ins[b].at[2 * px + py], dst_ref=outs[b].at[chip],
                    send_sem=sems[0].at[k], recv_sem=sems[1].at[k], device_id=(px, py, c), device_id_type=MESH)
                slot = outs[b].at[2 * px + py]
                recv = pltpu.make_async_remote_copy(
                    src_ref=slot, dst_ref=slot, send_sem=sems[0].at[k], recv_sem=sems[1].at[k],
                    device_id=(px, py, c), device_id_type=MESH)
                yield send, recv

    def start(ins, outs, sems):
        for send, _ in copies(ins, outs, sems):
            send.start()

    def finish(ins, outs, sems):
        for send, recv in copies(ins, outs, sems):
            recv.wait_recv()
            send.wait_send()

    return _Exchange(parts, [_sds(p.shape, p.dtype) for p in parts], {},
                     [pltpu.SemaphoreType.DMA((3 * n,))] * 2, start, finish)


def _sum_chips(place, part, land):
    nchip, ng, hr, cols = land.shape
    tr = _row_tile(hr, cols)
    steps = hr // tr

    def body(place_ref, p_ref, l1, l2, l3, o_ref):
        del place_ref
        o_ref[...] = ((p_ref[...].astype(F32) + l1[...].astype(F32)) + l2[...].astype(F32)) + l3[...].astype(F32)

    def slot(k):
        return pl.BlockSpec((None, None, tr, cols), lambda a, i, pr: (jnp.bitwise_xor(pr[1], k), a, i, 0))

    return pl.pallas_call(
        body, name="sum_chips",
        grid_spec=pltpu.PrefetchScalarGridSpec(
            num_scalar_prefetch=1, grid=(ng, steps),
            in_specs=[slot(0), slot(1), slot(2), slot(3)],
            out_specs=pl.BlockSpec((None, tr, cols), lambda a, i, pr: (a, pr[0] * steps + i, 0))),
        out_shape=_sds((ng, 2 * hr, cols), F32),
        compiler_params=_cp(2))(place, part, land, land, land)


def _rs_final(bufs):
    n = len(bufs)

    def copy(outs, sems, b, which):
        x, y, c = _my_place()
        hr = bufs[b].shape[1] // 2
        rows = outs[b].at[:, pl.ds(pl.multiple_of((c if which == 0 else 1 - c) * hr, 8), hr), :]
        return pltpu.make_async_remote_copy(
            src_ref=rows, dst_ref=rows, send_sem=sems[0].at[b], recv_sem=sems[1].at[b],
            device_id=(x, y, 1 - c), device_id_type=MESH)

    def start(ins, outs, sems):
        for b in range(n):
            copy(outs, sems, b, 0).start()

    def finish(ins, outs, sems):
        for b in range(n):
            copy(outs, sems, b, 0).wait_send()
            copy(outs, sems, b, 1).wait_recv()

    return _Exchange(bufs, [_sds(h.shape, F32) for h in bufs], {i: i for i in range(n)},
                     [pltpu.SemaphoreType.DMA((n,))] * 2, start, finish)


def _small_sync(smalls, dmod_blk, c_all):
    d = c_all.shape[-1]
    cols = dmod_blk.shape[-1]
    chunk = 384

    def body(sm_ref, dm_ref, c_ref, sum_ref, gw_ref, sm_all, dm_all, ssem, rsem):
        x, y, c = _my_place()
        dev = 4 * x + 2 * y + c
        chip = 2 * x + y
        sm_all[dev] = sm_ref[...]
        dm_all[dev] = dm_ref[chip]
        sends = []
        for k in range(1, N_DEV):
            px, py, pc = _flip(x, (k >> 2) & 1), _flip(y, (k >> 1) & 1), _flip(c, k & 1)
            a = pltpu.make_async_remote_copy(src_ref=sm_ref, dst_ref=sm_all.at[dev], send_sem=ssem.at[2 * (k - 1)],
                                             recv_sem=rsem.at[2 * (k - 1)], device_id=(px, py, pc),
                                             device_id_type=MESH)
            b = pltpu.make_async_remote_copy(src_ref=dm_ref.at[2 * px + py], dst_ref=dm_all.at[dev],
                                             send_sem=ssem.at[2 * (k - 1) + 1], recv_sem=rsem.at[2 * (k - 1) + 1],
                                             device_id=(px, py, pc), device_id_type=MESH)
            a.start()
            b.start()
            sends += [a, b]
        for k in range(1, N_DEV):
            px, py, pc = _flip(x, (k >> 2) & 1), _flip(y, (k >> 1) & 1), _flip(c, k & 1)
            pdev = 4 * px + 2 * py + pc
            pltpu.make_async_remote_copy(src_ref=sm_ref, dst_ref=sm_all.at[pdev], send_sem=ssem.at[2 * (k - 1)],
                                         recv_sem=rsem.at[2 * (k - 1)], device_id=(px, py, pc),
                                         device_id_type=MESH).wait_recv()
            pltpu.make_async_remote_copy(src_ref=dm_ref.at[chip], dst_ref=dm_all.at[pdev],
                                         send_sem=ssem.at[2 * (k - 1) + 1], recv_sem=rsem.at[2 * (k - 1) + 1],
                                         device_id=(px, py, pc), device_id_type=MESH).wait_recv()
        for cp in sends:
            cp.wait_send()

        tot = sm_all[0]
        for q in range(1, N_DEV):
            tot = tot + sm_all[q]
        sum_ref[...] = tot

        cs = c_ref[...].reshape(N_DEV * 8, d)
        sc = (cs * jax.nn.sigmoid(cs)).astype(BF)
        for n0 in range(0, cols, chunk):
            dmv = dm_all[:, :, n0:n0 + chunk].reshape(N_DEV * 8, chunk).astype(BF)
            gw_ref[:, n0:n0 + chunk] = _tn(sc, dmv)

    return pl.pallas_call(
        body, name="small_sync",
        out_shape=[_sds(smalls.shape, F32), _sds((d, cols), F32)],
        scratch_shapes=[pltpu.VMEM((N_DEV,) + smalls.shape, F32), pltpu.VMEM((N_DEV, 8, cols), F32),
                        pltpu.SemaphoreType.DMA((2 * (N_DEV - 1),)), pltpu.SemaphoreType.DMA((2 * (N_DEV - 1),))],
        compiler_params=_cp())(smalls, dmod_blk, c_all)


def _bucket_onehot():
    maps = np.stack([_bucket_map(dil).reshape(-1) for _, dil in DIL_CONFIGS])
    return (jnp.asarray(maps)[:, None, :] == jnp.arange(N_BUCKETS, dtype=jnp.int32)[None, :, None]).astype(BF)


def _dil_bias(rel_t, onehot):
    def body(r_ref, oh_ref, o_ref):
        rv = r_ref[...]
        hi = rv.astype(BF)
        lo = (rv - hi.astype(F32)).astype(BF)
        for c in range(len(DIL_CONFIGS)):
            o_ref[c] = _nn(hi, oh_ref[c]) + _nn(lo, oh_ref[c])

    return pl.pallas_call(body, name="dil_bias",
                          out_shape=_sds((len(DIL_CONFIGS), N_HEADS, BLOCK * 2 * BLOCK), F32),
                          compiler_params=_cp())(rel_t, onehot)


def _rowsum8(a):
    def body(a_ref, o_ref):
        o_ref[...] = jnp.sum(a_ref[...], axis=0, keepdims=True)

    return pl.pallas_call(body, name="rowsum8", out_shape=_sds((1, a.shape[1]), F32), compiler_params=_cp())(a)


def _local_step(x, mod, target, w, gains, rel_bias, place=None):
    nb, seq, d = x.shape
    t = nb * seq
    dist = place is not None
    core = place[0:1] if dist else None
    x0 = x.reshape(t, d)
    tgt = target.reshape(t, d)
    md = [mod[:, i:i + 1, :] for i in range(N_MOD)]
    sh1, sc1, gt1, sh2, sc2, gt2, sh3, sc3, gt3 = md
    g1, g2, g3 = gains["g_ffn1"], gains["g_mix"], gains["g_ffn2"]
    ones_g = _group_ones()

    def partial_sums(grads, lands):
        return [_add_halves(core, g, l) for g, l in zip(grads, lands)]

    def chip_sums(parts, lands):
        return [_sum_chips(place, p, l) for p, l in zip(parts, lands)]

    h1 = _modnorm(x0, g1, sc1, sh1, seq)
    res = _ffn_up(h1, w["gu1"], carry=_ag_weights([w["d1"], w["win"], w["wout"]]) if dist else None)
    a1, u1, s1 = res[:3]
    wd1, w_in, w_out = res[3:] if dist else (w["d1"], w["win"], w["wout"])
    w_out2 = w_out.reshape(2 * D_GRP, d)
    f1, x1 = _ffn_down(s1, wd1, x0, gt1, seq, 0.5)

    h2 = _modnorm(x1, g2, sc2, sh2, seq)
    qkv6, qkv_r4, qkv_r16 = _qkv_proj(h2, w_in)
    qkv6b = qkv6.reshape(6, nb, seq, D_GRP)
    res = _sb_fwd(qkv6b, gains["g_sb_out"], nb, seq, carry=_ag_weights([w["gu2"], w["d2"]]) if dist else None)
    o_sb, on_sb = res[:2]
    wgu2, wd2 = res[2:] if dist else (w["gu2"], w["d2"])
    onehot = _bucket_onehot()
    bias = _dil_bias(rel_bias.T, onehot).reshape(len(DIL_CONFIGS), N_HEADS, BLOCK, 2 * BLOCK)
    o_cs, l_cs = [], []
    qkv_rs = [(qkv6b, 3), (qkv_r4, 0), (qkv_r16, 0)]
    for ci, (_, dil) in enumerate(DIL_CONFIGS):
        sub = seq // dil
        arr, base = qkv_rs[ci]
        arr = arr.reshape(base + 3, nb, sub, dil * D_GRP)
        qkv_rs[ci] = (arr, base)
        o_c, l_c = _dil_fwd(arr, base, bias[ci], nb, sub, dil)
        o_cs.append(o_c.reshape(t // dil, dil * D_GRP))
        l_cs.append(l_c.reshape(t // dil, dil * D_GRP))
    o_dil, on_dil = _dil_comb(o_cs, l_cs, gains["g_dil_out"])
    tmix, x2 = _mix_out(on_sb.reshape(t, D_GRP), on_dil, w_out2, x1, gt2, seq)

    h3 = _modnorm(x2, g3, sc3, sh3, seq)
    a3, u3, s3 = _ffn_up(h3, wgu2)
    f3, x3 = _ffn_down(s3, wd2, x2, gt3, seq, 0.5)

    dx3, dg_final, loss = _final_loss(x3, gains["g_final"], tgt)

    da3, du3, df3, dgt3 = _ffn_bwd_ds(dx3, gt3, f3, wd2, a3, u3, seq, 0.5)
    grads2 = _ffn_bwd_w(h3, da3, du3, s3, df3)
    res = _ffn_bwd_dh(da3, du3, wgu2, x2, g3, sc3, dx3, seq, carry=_rs_d2d(grads2) if dist else None)
    dx2, dsh3, dsc3, dg3 = res[:4]
    parts2 = partial_sums(grads2, res[4:]) if dist else None

    do_sb, do_dil, dgt2, dg_sb, dg_dil, dw_out = _mix_bwd_out(
        dx2, gt2, tmix, w_out2, o_sb.reshape(t, D_GRP), o_dil, on_sb.reshape(t, D_GRP), on_dil,
        gains["g_sb_out"], gains["g_dil_out"], ones_g, seq)
    dw_out = dw_out.reshape(N_CHIPS, 1, 2 * D_GRP // N_CHIPS, d)
    res = _sb_bwd(qkv6b, do_sb.reshape(nb, seq, D_GRP), nb, seq, carry=_rs_ici(parts2) if dist else None)
    dqkv6 = res[0]
    halves2 = chip_sums(parts2, res[1:]) if dist else None
    dcs = _dil_comb_bwd(do_dil, o_cs, l_cs)
    dsum, a_tiles = [], []
    for ci, (_, dil) in enumerate(DIL_CONFIGS):
        sub = seq // dil
        do_c = dcs[ci].reshape(nb, sub, dil * D_GRP)
        dd_c = dcs[3 + ci].reshape(nb, sub, dil * D_GRP)
        res = _dil_bwd(qkv_rs[ci][0], qkv_rs[ci][1], bias[ci], do_c, dd_c, nb, sub, dil,
                       carry=_rs_final(halves2) if dist and ci == 0 else None)
        if dist and ci == 0:
            grads2 = res[2:]
        dsum.append(res[0].reshape(3, t // dil, dil * D_GRP))
        a_tiles.append(res[1].reshape(N_HEADS, BLOCK * 2 * BLOCK))
    dqkv6 = _dqkv_dil_sum(dsum, dqkv6.reshape(6, t, D_GRP))
    drel = _relbias_grad(jnp.stack(a_tiles), onehot)
    dx1, dsh2, dsc2, dg2 = _mix_bwd_dh(dqkv6, w_in, x1, g2, sc2, dx2, seq)

    da1, du1, df1, dgt1 = _ffn_bwd_ds(dx1, gt1, f1, wd1, a1, u1, seq, 0.5)
    grads1 = _ffn_bwd_w(h1, da1, du1, s1, df1)
    res = _dw_in(h2, dqkv6, carry=_rs_d2d(grads1) if dist else None)
    grads_m = [res[0], dw_out]
    parts1 = partial_sums(grads1, res[1:]) if dist else None
    res = _ffn_bwd_dh(da1, du1, w["gu1"], x0, g1, sc1, dx1, seq,
                      carry=_join([_rs_ici(parts1), _rs_d2d(grads_m)]) if dist else None)
    dx0, dsh1, dsc1, dg1 = res[:4]
    if dist:
        halves1 = chip_sums(parts1, res[4:6])
        parts_m = partial_sums(grads_m, res[6:8])
        res = _alone("rs_tail", _join([_rs_final(halves1), _rs_ici(parts_m)]))
        grads1 = res[:2]
        grads_m = _alone("rs_last", _rs_final(chip_sums(parts_m, res[2:4])))

    dmod = jnp.concatenate([dsh1, dsc1, dgt1, dsh2, dsc2, dgt2, dsh3, dsc3, dgt3], axis=1)
    return dict(grad_x=dx0.reshape(nb, seq, d), loss=loss[0, 0], dmod=dmod.reshape(nb, N_MOD * d),
                dgu1=grads1[0], dwd1=grads1[1], dgu2=grads2[0], dwd2=grads2[1], dwin=grads_m[0], dwout=grads_m[1],
                dg_ffn1=dg1, dg_mix=dg2, dg_ffn2=dg3, dg_final=dg_final, dg_sb=dg_sb, dg_dil=dg_dil,
                drel=drel.T)


_SMALL_ORDER = (("b_ada", N_MOD * 1024), ("g_ffn1", 1024), ("g_mix", 1024), ("g_ffn2", 1024), ("g_final", 1024),
                ("g_sb_out", D_GRP), ("g_dil_out", D_GRP), ("rel_bias", N_BUCKETS * N_HEADS))


def _pack_small(parts, extra=None):
    flat = [parts[name].reshape(-1).astype(F32) for name, _ in _SMALL_ORDER]
    used = sum(sz for _, sz in _SMALL_ORDER)
    pad = SMALL_ROWS * 128 - used
    tail = jnp.zeros((pad,), F32)
    if extra is not None:
        tail = tail.at[0].set(extra)
    return jnp.concatenate(flat + [tail]).reshape(SMALL_ROWS, 128)


def _unpack_small(packed, shapes):
    flat = packed.reshape(-1)
    out, off = {}, 0
    for name, sz in _SMALL_ORDER:
        out[name] = flat[off:off + sz].reshape(shapes[name])
        off += sz
    return out, flat[off]


def kernel(x, c, w_ada, b_ada, g_ffn1, w1_gate, w1_up, w1_down, g_mix, w_in, g_sb_out, g_dil_out, w_out, rel_bias, g_ffn2, w2_gate, w2_up, w2_down, g_final, loss_target, m_w_ada, m_b_ada, m_g_ffn1, m_w1_gate, m_w1_up, m_w1_down, m_g_mix, m_w_in, m_g_sb_out, m_g_dil_out, m_w_out, m_rel_bias, m_g_ffn2, m_w2_gate, m_w2_up, m_w2_down, m_g_final, v_w_ada, v_b_ada, v_g_ffn1, v_w1_gate, v_w1_up, v_w1_down, v_g_mix, v_w_in, v_g_sb_out, v_g_dil_out, v_w_out, v_rel_bias, v_g_ffn2, v_w2_gate, v_w2_up, v_w2_down, v_g_final):
    nb, seq, d = x.shape
    xi, yi, ci = lax.axis_index("x"), lax.axis_index("y"), lax.axis_index("c")
    chip = 2 * xi + yi
    ada_cols = w_ada.shape[-1]

    c_pad = jnp.zeros((8, d), F32).at[:nb].set(c)
    b_shard = lax.dynamic_slice(b_ada, (0, chip * ada_cols), (1, ada_cols))
    c_all, mod_blk = _ada_fwd(c_pad, w_ada[0], b_shard)
    mod = jnp.transpose(mod_blk[:, :nb, :], (1, 0, 2)).reshape(nb, N_MOD, d)

    shards = dict(gu1=jnp.stack([w1_gate[0], w1_up[0]]), d1=w1_down, win=w_in, wout=w_out,
                  gu2=jnp.stack([w2_gate[0], w2_up[0]]), d2=w2_down)
    bufs = {k: lax.dynamic_update_slice(lax.empty((N_CHIPS,) + s.shape, BF), s.astype(BF)[None], (chip, 0, 0, 0))
            for k, s in shards.items()}
    bufs["gu1"] = _alone("ag_first", _ag_weights([bufs["gu1"]]))[0]

    gains = dict(g_ffn1=g_ffn1, g_mix=g_mix, g_ffn2=g_ffn2, g_final=g_final.reshape(1, d),
                 g_sb_out=g_sb_out.reshape(1, D_GRP), g_dil_out=g_dil_out.reshape(1, D_GRP))
    place = jnp.stack([ci, chip]).astype(jnp.int32)
    r = _local_step(x, mod, loss_target, bufs, gains, rel_bias, place)
    gu1, gu2, gd1, gd2, gwin, gwout = r["dgu1"], r["dgu2"], r["dwd1"], r["dwd2"], r["dwin"], r["dwout"]

    dmod = r["dmod"]
    dmod_pad = jnp.zeros((8, N_MOD * d), F32).at[:nb].set(dmod)
    dmod_blk = jnp.transpose(dmod_pad.reshape(8, N_CHIPS, ada_cols), (1, 0, 2))
    small_parts = dict(b_ada=_rowsum8(dmod_pad), g_ffn1=r["dg_ffn1"], g_mix=r["dg_mix"], g_ffn2=r["dg_ffn2"],
                       g_final=r["dg_final"], g_sb_out=r["dg_sb"], g_dil_out=r["dg_dil"], rel_bias=r["drel"])
    small_sum, g_wada = _small_sync(_pack_small(small_parts, r["loss"]), dmod_blk, c_all)

    small_w = dict(b_ada=b_ada, g_ffn1=g_ffn1, g_mix=g_mix, g_ffn2=g_ffn2, g_final=g_final,
                   g_sb_out=g_sb_out, g_dil_out=g_dil_out, rel_bias=rel_bias)
    small_m = dict(b_ada=m_b_ada, g_ffn1=m_g_ffn1, g_mix=m_g_mix, g_ffn2=m_g_ffn2, g_final=m_g_final,
                   g_sb_out=m_g_sb_out, g_dil_out=m_g_dil_out, rel_bias=m_rel_bias)
    small_v = dict(b_ada=v_b_ada, g_ffn1=v_g_ffn1, g_mix=v_g_mix, g_ffn2=v_g_ffn2, g_final=v_g_final,
                   g_sb_out=v_g_sb_out, g_dil_out=v_g_dil_out, rel_bias=v_rel_bias)
    shapes = {k: v.shape for k, v in small_w.items()}
    sg, sd, sm, sv = _adamw(_pack_small(small_w), small_sum.reshape(1, SMALL_ROWS, 128), 0,
                            _pack_small(small_m), _pack_small(small_v))
    sg, loss = _unpack_small(sg, shapes)
    sd, _ = _unpack_small(sd, shapes)
    sm, _ = _unpack_small(sm, shapes)
    sv, _ = _unpack_small(sv, shapes)

    big = {}

    def upd(name, w, g_arr, sel, m, v):
        shape = w.shape
        res = _adamw(w.reshape(shape[-2:]), g_arr, sel, m.reshape(shape[-2:]), v.reshape(shape[-2:]))
        big[name] = [a.reshape(shape) for a in res]

    upd("w_ada", w_ada, g_wada.reshape(1, d, ada_cols), 0, m_w_ada, v_w_ada)
    upd("w1_gate", w1_gate, gu1, 0, m_w1_gate, v_w1_gate)
    upd("w1_up", w1_up, gu1, 1, m_w1_up, v_w1_up)
    upd("w1_down", w1_down, gd1, 0, m_w1_down, v_w1_down)
    upd("w_in", w_in, gwin, 0, m_w_in, v_w_in)
    upd("w_out", w_out, gwout, 0, m_w_out, v_w_out)
    upd("w2_gate", w2_gate, gu2, 0, m_w2_gate, v_w2_gate)
    upd("w2_up", w2_up, gu2, 1, m_w2_up, v_w2_up)
    upd("w2_down", w2_down, gd2, 0, m_w2_down, v_w2_down)

    names = ["w_ada", "b_ada", "g_ffn1", "w1_gate", "w1_up", "w1_down", "g_mix", "w_in", "g_sb_out", "g_dil_out",
             "w_out", "rel_bias", "g_ffn2", "w2_gate", "w2_up", "w2_down", "g_final"]
    outs = [loss, r["grad_x"]]
    for k, small in enumerate((sg, sd, sm, sv)):
        for name in names:
            outs.append(big[name][k] if name in big else small[name])
    return tuple(outs)
```

```python
import functools
import math

import numpy as np
import jax
import jax.numpy as jnp
from jax import lax
from jax.experimental import pallas as pl
from jax.experimental.pallas import tpu as pltpu

F32 = jnp.float32
BF = jnp.bfloat16
MESH = pl.DeviceIdType.MESH

HEAD_DIM = 64
N_HEADS = 8
D_GRP = N_HEADS * HEAD_DIM
DIL_CONFIGS = ((128, 1), (512, 4), (2048, 16))
N_STEPS = 128
BLOCK = 128
N_BUCKETS = 32
MAX_DISTANCE = 2048
N_MOD = 9
EPS = 1e-6
NEG_INF = -1e30
SCALE = HEAD_DIM ** -0.5

ADAM_LR = 0.001
ADAM_B1 = 0.9
ADAM_B2 = 0.999
ADAM_EPS = 1e-08
ADAM_WD = 0.01
ADAM_STEP = 10

N_CHIPS = 4
N_DEV = 8
VMEM_LIMIT = 56 * 1024 * 1024
TM = 512
TQ = 256
KB = 256
SMALL_ROWS = 120


def _cp(n_axes=0, **kw):
    sem = ("arbitrary",) * n_axes if n_axes else None
    return pltpu.CompilerParams(dimension_semantics=sem, vmem_limit_bytes=VMEM_LIMIT, **kw)


def _nn(a, b):
    return jnp.dot(a, b, preferred_element_type=F32)


def _nt(a, b):
    return lax.dot_general(a, b, (((1,), (1,)), ((), ())), preferred_element_type=F32)


def _tn(a, b):
    return lax.dot_general(a, b, (((0,), (0,)), ((), ())), preferred_element_type=F32)


def _nn2(x, m):
    hi = x.astype(BF)
    lo = (x - hi.astype(F32)).astype(BF)
    return _nn(hi, m) + _nn(lo, m)


def _softplus(z):
    return jnp.maximum(z, 0.0) + jnp.log1p(jnp.exp(-jnp.abs(z)))


def _sds(shape, dtype):
    return jax.ShapeDtypeStruct(shape, dtype)


def _whole(a):
    nd = a.ndim
    return pl.BlockSpec(a.shape, lambda *_: (0,) * nd, pipeline_mode=pl.Buffered(1))


def _modnorm(x, g, sc, sh, seq):
    t, d = x.shape
    per = seq // TM

    def body(x_ref, g_ref, sc_ref, sh_ref, h_ref):
        xv = x_ref[...]
        r = lax.rsqrt(jnp.mean(xv * xv, axis=-1, keepdims=True) + EPS)
        h_ref[...] = (((xv * r) * g_ref[...]) * (1.0 + sc_ref[...]) + sh_ref[...]).astype(BF)

    return pl.pallas_call(
        body, name="modnorm", grid=(t // TM,),
        in_specs=[pl.BlockSpec((TM, d), lambda m: (m, 0)),
                  pl.BlockSpec((1, d), lambda m: (0, 0)),
                  pl.BlockSpec((None, 1, d), lambda m: (m // per, 0, 0)),
                  pl.BlockSpec((None, 1, d), lambda m: (m // per, 0, 0))],
        out_specs=pl.BlockSpec((TM, d), lambda m: (m, 0)),
        out_shape=_sds((t, d), BF), compiler_params=_cp(1))(x, g, sc, sh)


def _modnorm_bwd_tile(dh, xv, gv, scv, dxo):
    r = lax.rsqrt(jnp.mean(xv * xv, axis=-1, keepdims=True) + EPS)
    n = xv * r
    ng = n * gv
    dsh = jnp.sum(dh, axis=0, keepdims=True)
    dsc = jnp.sum(dh * ng, axis=0, keepdims=True)
    dy = dh * (1.0 + scv)
    dg = jnp.sum(dy * n, axis=0, keepdims=True)
    dn = dy * gv
    dx = dxo + r * (dn - n * jnp.mean(dn * n, axis=-1, keepdims=True))
    return dx, dsh, dsc, dg


def _acc_rows(ref, val, first):
    @pl.when(first)
    def _():
        ref[...] = val

    @pl.when(jnp.logical_not(first))
    def _():
        ref[...] += val


def _ffn_up(h, wgu, carry=None):
    t, d = h.shape
    fs = wgu.shape[-1]

    def body(h_ref, w_ref, p_ref, q_ref, s_ref):
        hv = h_ref[...]
        a = _nn(hv, w_ref[0])
        u = _nn(hv, w_ref[1])
        sig = jax.nn.sigmoid(a)
        q = a * sig
        p_ref[...] = (u * (sig * (1.0 + a * (1.0 - sig)))).astype(BF)
        q_ref[...] = q.astype(BF)
        s_ref[...] = (q * u).astype(BF)

    blk = pl.BlockSpec((None, TM, fs), lambda j, m: (j, m, 0))
    return _call(
        body, "ffn_up", (N_CHIPS, t // TM),
        [pl.BlockSpec((TM, d), lambda j, m: (m, 0)),
         pl.BlockSpec((None, 2, d, fs), lambda j, m: (j, 0, 0, 0))],
        [blk, blk, blk],
        [_sds((N_CHIPS, t, fs), BF)] * 3,
        (h, wgu), carry=carry)


def _ffn_down(s, wd, x, gt, seq, coef):
    _, t, fs = s.shape
    d = x.shape[-1]
    per = seq // TM

    def body(s_ref, w_ref, x_ref, gt_ref, f_ref, xo_ref):
        f = _nn(s_ref[0], w_ref[0, 0])
        for j in range(1, N_CHIPS):
            f = f + _nn(s_ref[j], w_ref[j, 0])
        f_ref[...] = f
        xo_ref[...] = x_ref[...] + (coef * gt_ref[...]) * f

    row = pl.BlockSpec((TM, d), lambda m: (m, 0))
    return pl.pallas_call(
        body, name="ffn_down", grid=(t // TM,),
        in_specs=[pl.BlockSpec((N_CHIPS, TM, fs), lambda m: (0, m, 0)),
                  _whole(wd), row,
                  pl.BlockSpec((None, 1, d), lambda m: (m // per, 0, 0))],
        out_specs=[row, row],
        out_shape=[_sds((t, d), F32), _sds((t, d), F32)],
        compiler_params=_cp(1))(s, wd, x, gt)


def _ffn_bwd_ds(dxo, gt, f, wd, p, q, seq, coef, carry=None):
    t, d = dxo.shape
    fs = p.shape[-1]
    per = seq // TM
    nb = t // seq

    def body(dxo_ref, gt_ref, f_ref, w_ref, p_ref, q_ref, da_ref, du_ref, df_ref, dgt_ref):
        m = pl.program_id(0)
        dxv = dxo_ref[...]
        df = ((coef * gt_ref[...]) * dxv).astype(BF)
        df_ref[...] = df
        _acc_rows(dgt_ref, coef * jnp.sum(dxv * f_ref[...], axis=0, keepdims=True), m % per == 0)
        for j in range(N_CHIPS):
            ds = _nt(df, w_ref[j, 0])
            da_ref[j] = (ds * p_ref[j].astype(F32)).astype(BF)
            du_ref[j] = (ds * q_ref[j].astype(F32)).astype(BF)

    row = pl.BlockSpec((TM, d), lambda m: (m, 0))
    blk = pl.BlockSpec((N_CHIPS, TM, fs), lambda m: (0, m, 0))
    ex = pl.BlockSpec((None, 1, d), lambda m: (m // per, 0, 0))
    return _call(
        body, "ffn_bwd_ds", (t // TM,),
        [row, ex, row, _whole(wd), blk, blk],
        [blk, blk, row, ex],
        [_sds((N_CHIPS, t, fs), BF), _sds((N_CHIPS, t, fs), BF), _sds((t, d), BF), _sds((nb, 1, d), F32)],
        (dxo, gt, f, wd, p, q), carry=carry)


TK_W = 1024


def _ffn_bwd_w(h, da, du, s, df):
    t, d = h.shape
    fs = da.shape[-1]

    def body(h_ref, da_ref, du_ref, s_ref, df_ref, o_ref):
        kt = pl.program_id(1)
        hv = h_ref[...]
        parts = (_tn(da_ref[...], hv), _tn(du_ref[...], hv), _tn(s_ref[...], df_ref[...]))

        @pl.when(kt == 0)
        def _():
            for i, p in enumerate(parts):
                o_ref[i] = p

        @pl.when(kt != 0)
        def _():
            for i, p in enumerate(parts):
                o_ref[i] += p

    row = pl.BlockSpec((TK_W, d), lambda j, kt: (kt, 0))
    blk = pl.BlockSpec((None, TK_W, fs), lambda j, kt: (j, kt, 0))
    return pl.pallas_call(
        body, name="ffn_bwd_w", grid=(N_CHIPS, t // TK_W),
        in_specs=[row, blk, blk, blk, row],
        out_specs=pl.BlockSpec((None, 3, fs, d), lambda j, kt: (j, 0, 0, 0)),
        out_shape=_sds((N_CHIPS, 3, fs, d), F32),
        compiler_params=_cp(2))(h, da, du, s, df)


def _ffn_bwd_dh(da, du, wgu, x, g, sc, dxo, seq, carry=None):
    _, t, fs = da.shape
    d = x.shape[-1]
    per = seq // TM
    nb = t // seq

    def body(da_ref, du_ref, w_ref, x_ref, g_ref, sc_ref, dxo_ref, dx_ref, dsh_ref, dsc_ref, dg_ref):
        m = pl.program_id(0)
        dh = _nt(da_ref[0], w_ref[0, 0]) + _nt(du_ref[0], w_ref[0, 1])
        for j in range(1, N_CHIPS):
            dh = dh + _nt(da_ref[j], w_ref[j, 0]) + _nt(du_ref[j], w_ref[j, 1])
        dx, dsh, dsc, dg = _modnorm_bwd_tile(dh, x_ref[...], g_ref[...], sc_ref[...], dxo_ref[...])
        dx_ref[...] = dx
        _acc_rows(dsh_ref, dsh, m % per == 0)
        _acc_rows(dsc_ref, dsc, m % per == 0)
        _acc_rows(dg_ref, dg, m == 0)

    row = pl.BlockSpec((TM, d), lambda m: (m, 0))
    blk = pl.BlockSpec((N_CHIPS, TM, fs), lambda m: (0, m, 0))
    ex = pl.BlockSpec((None, 1, d), lambda m: (m // per, 0, 0))
    vec = pl.BlockSpec((1, d), lambda m: (0, 0))
    return _call(
        body, "ffn_bwd_dh", (t // TM,),
        [blk, blk, _whole(wgu), row, vec, ex, row],
        [row, ex, ex, vec],
        [_sds((t, d), F32), _sds((nb, 1, d), F32), _sds((nb, 1, d), F32), _sds((1, d), F32)],
        (da, du, wgu, x, g, sc, dxo), carry=carry)


def _qkv_proj(h, w_in, carry=None):
    t, d = h.shape
    wc = w_in.shape[-1]

    dils = [dil for _, dil in DIL_CONFIGS if dil > 1]

    def body(h_ref, w_ref, o_ref, *rest):
        res_refs, buf = rest[:len(dils)], rest[len(dils)]
        hv = h_ref[...]
        for j in range(N_CHIPS):
            rf = _nn(hv, w_ref[j, 0])
            r = rf.astype(BF)
            for a, lc, off, width in _col_pieces(j, wc):
                o_ref[a, :, lc:lc + width] = r[:, off:off + width]
                if a < 3:
                    continue
                for c0 in range(0, width, 128):
                    cg = (lc + c0) // 128
                    buf[...] = rf[:, off + c0:off + c0 + 128]
                    for ref, dil in zip(res_refs, dils):
                        for rr in range(dil):
                            ref[a - 3, :, rr * D_GRP + cg * 128:rr * D_GRP + (cg + 1) * 128] = (
                                buf[pl.ds(rr, TM // dil, stride=dil), :].astype(BF))

    return _call(
        body, "qkv_proj", (t // TM,),
        [pl.BlockSpec((TM, d), lambda m: (m, 0)), _whole(w_in)],
        [pl.BlockSpec((6, TM, D_GRP), lambda m: (0, m, 0))]
        + [pl.BlockSpec((3, TM // dil, dil * D_GRP), lambda m: (0, m, 0)) for dil in dils],
        [_sds((6, t, D_GRP), BF)] + [_sds((3, t // dil, dil * D_GRP), BF) for dil in dils],
        (h, w_in), scratch=[pltpu.VMEM((TM, 128), F32)], carry=carry)


def _col_pieces(j, wc):
    out, off = [], 0
    while off < wc:
        a, lc = divmod(j * wc + off, D_GRP)
        width = min(D_GRP - lc, wc - off)
        out.append((a, lc, off, width))
        off += width
    return out


def _chip_cols(g6_ref, j, wc):
    return jnp.concatenate([g6_ref[a, :, lc:lc + width] for a, lc, _, width in _col_pieces(j, wc)], axis=1)


def _mix_out(on_sb, on_dil, w_out, x, gt, seq):
    t, d = x.shape
    per = seq // TM

    def body(a_ref, b_ref, w_ref, x_ref, gt_ref, t_ref, xo_ref):
        tv = _nn(a_ref[...], w_ref[0:D_GRP, :]) + _nn(b_ref[...], w_ref[D_GRP:2 * D_GRP, :])
        t_ref[...] = tv
        xo_ref[...] = x_ref[...] + gt_ref[...] * tv

    row = pl.BlockSpec((TM, d), lambda m: (m, 0))
    half = pl.BlockSpec((TM, D_GRP), lambda m: (m, 0))
    return pl.pallas_call(
        body, name="mix_out", grid=(t // TM,),
        in_specs=[half, half, pl.BlockSpec((2 * D_GRP, d), lambda m: (0, 0)), row,
                  pl.BlockSpec((None, 1, d), lambda m: (m // per, 0, 0))],
        out_specs=[row, row],
        out_shape=[_sds((t, d), F32), _sds((t, d), F32)],
        compiler_params=_cp(1))(on_sb, on_dil, w_out, x, gt)


def _sb_masks():
    lane = lax.broadcasted_iota(jnp.int32, (1, 2 * HEAD_DIM), 1)
    hm0 = lane < HEAD_DIM
    rel = lax.broadcasted_iota(jnp.int32, (TQ, KB), 0) - lax.broadcasted_iota(jnp.int32, (TQ, KB), 1)
    kr = lax.broadcasted_iota(jnp.int32, (KB, KB), 0)
    kc = lax.broadcasted_iota(jnp.int32, (KB, KB), 1)
    return hm0, rel, kr, kc


def _headnorm_pair(o, gv, hm0):
    o2 = o * o
    ms0 = jnp.sum(jnp.where(hm0, o2, 0.0), axis=-1, keepdims=True) * (1.0 / HEAD_DIM)
    ms1 = jnp.sum(jnp.where(hm0, 0.0, o2), axis=-1, keepdims=True) * (1.0 / HEAD_DIM)
    r = jnp.where(hm0, lax.rsqrt(ms0 + EPS), lax.rsqrt(ms1 + EPS))
    return (o * r) * gv


SB_DEAD = -104.0


def _alive(c_l):
    return (jnp.max(c_l) > SB_DEAD).astype(jnp.int32)


def _sb_fwd(qkv6, g_sb, nb, seq, carry=None):
    nq = seq // TQ

    def body(q_ref, k_ref, v_ref, g_ref, o_ref, on_ref):
        qi = pl.program_id(2)
        hm0, rel, kr, kc = _sb_masks()
        upper = (kr > kc).astype(BF)
        qv = q_ref[...]
        qhs = [jnp.where(hm0, qv, jnp.zeros_like(qv)), jnp.where(hm0, jnp.zeros_like(qv), qv)]

        def block(kj, causal, c_ls, accs):
            ks = pl.multiple_of(kj * KB, KB)
            kb = k_ref[pl.ds(ks, KB), :]
            vb = v_ref[pl.ds(ks, KB), :]
            new_c, new_acc = [], []
            for qh, c_l, acc in zip(qhs, c_ls, accs):
                z = _nt(qh, kb) * SCALE
                sp = _softplus(z)
                ln = -sp if causal is None else jnp.where(causal, -sp, 0.0)
                suf = _nn2(ln, upper)
                w = jnp.exp((z - sp) + (suf + c_l))
                if causal is not None:
                    w = jnp.where(causal, w, 0.0)
                new_acc.append(acc + _nn(w.astype(BF), vb))
                new_c.append(c_l + (suf[:, 0:1] + ln[:, 0:1]))
            return new_c, new_acc

        zc = jnp.zeros((TQ, 1), F32)
        za = jnp.zeros((TQ, 2 * HEAD_DIM), F32)
        c_ls, accs = block(qi, rel > 0, [zc, zc], [za, za])

        def cond(carry):
            return jnp.logical_and(carry[0] <= qi, carry[1] > 0)

        def kbody(carry):
            it, _, c0, c1, a0, a1 = carry
            (c0, c1), (a0, a1) = block(qi - it, None, [c0, c1], [a0, a1])
            return it + 1, jnp.maximum(_alive(c0), _alive(c1)), c0, c1, a0, a1

        init = (jnp.int32(1), jnp.maximum(_alive(c_ls[0]), _alive(c_ls[1])), c_ls[0], c_ls[1], accs[0], accs[1])
        outs = lax.while_loop(cond, kbody, init)[4:]
        o = jnp.where(hm0, outs[0], outs[1])
        o_ref[...] = o
        on_ref[...] = _headnorm_pair(o, g_ref[...], hm0).astype(BF)

    w = 2 * HEAD_DIM
    full = lambda i: pl.BlockSpec((None, None, seq, w), lambda b, hp, q: (i, b, 0, hp))
    qblk = pl.BlockSpec((None, None, TQ, w), lambda b, hp, q: (0, b, q, hp))
    oblk = pl.BlockSpec((None, TQ, w), lambda b, hp, q: (b, q, hp))
    return _call(
        body, "sb_fwd", (nb, N_HEADS // 2, nq),
        [qblk, full(1), full(2), pl.BlockSpec((1, w), lambda b, hp, q: (0, hp))],
        [oblk, oblk],
        [_sds((nb, seq, D_GRP), F32), _sds((nb, seq, D_GRP), BF)],
        (qkv6, qkv6, qkv6, g_sb), carry=carry)


def _sb_bwd(qkv6, do, nb, seq, carry=None):
    nq = seq // TQ
    nk = seq // KB

    def body(q_ref, k_ref, v_ref, do_ref, out_ref, dk_acc, dv_acc, g_st, s_st):
        qi = pl.program_id(2)
        hm0, rel, kr, kc = _sb_masks()
        upper = (kr > kc).astype(BF)
        lower = (kr < kc).astype(BF)

        @pl.when(qi == 0)
        def _():
            dk_acc[...] = jnp.zeros_like(dk_acc)
            dv_acc[...] = jnp.zeros_like(dv_acc)

        qv = q_ref[...]
        dov = do_ref[...]
        qhs = [jnp.where(hm0, qv, jnp.zeros_like(qv)), jnp.where(hm0, jnp.zeros_like(qv), qv)]
        dohs = [jnp.where(hm0, dov, 0.0).astype(BF), jnp.where(hm0, 0.0, dov).astype(BF)]

        def weights(kj, causal, c_ls):
            ks = pl.multiple_of(kj * KB, KB)
            kb = k_ref[pl.ds(ks, KB), :]
            vb = v_ref[pl.ds(ks, KB), :]
            new_c, dv = [], None
            for hh, (qh, doh, c_l) in enumerate(zip(qhs, dohs, c_ls)):
                z = _nt(qh, kb) * SCALE
                sp = _softplus(z)
                ln = -sp if causal is None else jnp.where(causal, -sp, 0.0)
                suf = _nn2(ln, upper)
                lsz = z - sp
                w = jnp.exp(lsz + (suf + c_l))
                if causal is not None:
                    w = jnp.where(causal, w, 0.0)
                g_st[hh, kj] = w * _nt(doh, vb)
                s_st[hh, kj] = jnp.exp(lsz)
                part = _tn(w.astype(BF), doh)
                dv = part if dv is None else dv + part
                new_c.append(c_l + (suf[:, 0:1] + ln[:, 0:1]))
            dv_acc[pl.ds(ks, KB), :] += dv
            return new_c

        zc = jnp.zeros((TQ, 1), F32)
        c_ls = weights(qi, rel > 0, [zc, zc])

        def acond(carry):
            return jnp.logical_and(carry[0] <= qi, carry[1] > 0)

        def abody(carry):
            it, _, c0, c1 = carry
            c0, c1 = weights(qi - it, None, [c0, c1])
            return it + 1, jnp.maximum(_alive(c0), _alive(c1)), c0, c1

        n_used = lax.while_loop(
            acond, abody, (jnp.int32(1), jnp.maximum(_alive(c_ls[0]), _alive(c_ls[1])), c_ls[0], c_ls[1]))[0]

        def grads(kj, causal, c_gs, dqs):
            ks = pl.multiple_of(kj * KB, KB)
            kb = k_ref[pl.ds(ks, KB), :]
            new_c, new_dq, dk = [], [], None
            for hh, (qh, c_g, dq) in enumerate(zip(qhs, c_gs, dqs)):
                g = g_st[hh, kj]
                sig = s_st[hh, kj]
                pre = _nn2(g, lower)
                dz = g * (1.0 - sig) - sig * (pre + c_g)
                if causal is not None:
                    dz = jnp.where(causal, dz, 0.0)
                dzb = (dz * SCALE).astype(BF)
                part = _tn(dzb, qh)
                dk = part if dk is None else dk + part
                new_dq.append(dq + _nn(dzb, kb))
                new_c.append(c_g + (pre[:, KB - 1:KB] + g[:, KB - 1:KB]))
            dk_acc[pl.ds(ks, KB), :] += dk
            return new_c, new_dq

        za = jnp.zeros((TQ, 2 * HEAD_DIM), F32)

        def bbody(kj, carry):
            (c0, c1), (d0, d1) = grads(kj, None, carry[:2], carry[2:])
            return c0, c1, d0, d1

        c0, c1, d0, d1 = lax.fori_loop(qi - n_used + 1, qi, bbody, (zc, zc, za, za))
        _, dqs = grads(qi, rel > 0, [c0, c1], [d0, d1])
        dq = jnp.where(hm0, dqs[0], dqs[1])
        out_ref[0, pl.ds(pl.multiple_of(qi * TQ, TQ), TQ), :] = dq.astype(BF)

        @pl.when(qi == nq - 1)
        def _():
            out_ref[1] = dk_acc[...].astype(BF)
            out_ref[2] = dv_acc[...].astype(BF)

    w = 2 * HEAD_DIM
    full = lambda i: pl.BlockSpec((None, None, seq, w), lambda b, hp, q: (i, b, 0, hp))
    qblk = pl.BlockSpec((None, None, TQ, w), lambda b, hp, q: (0, b, q, hp))
    oblk = pl.BlockSpec((None, TQ, w), lambda b, hp, q: (b, q, hp))
    return _call(
        body, "sb_bwd", (nb, N_HEADS // 2, nq),
        [qblk, full(1), full(2), oblk],
        [pl.BlockSpec((3, None, seq, w), lambda b, hp, q: (0, b, 0, hp))],
        [_sds((6, nb, seq, D_GRP), BF)], (qkv6, qkv6, qkv6, do),
        scratch=[pltpu.VMEM((seq, w), F32), pltpu.VMEM((seq, w), F32),
                 pltpu.VMEM((2, nk, TQ, KB), F32), pltpu.VMEM((2, nk, TQ, KB), F32)],
        carry=carry)


def _t5_bucket(n):
    max_exact = N_BUCKETS // 2
    nf = np.maximum(n, 1).astype(np.float32)
    large = max_exact + (np.log(nf / max_exact) / math.log(MAX_DISTANCE / max_exact)
                         * (N_BUCKETS - max_exact)).astype(np.int32)
    large = np.minimum(large, N_BUCKETS - 1)
    return np.where(n < max_exact, n, large).astype(np.int32)


def _bucket_map(dilation):
    step = BLOCK + np.arange(BLOCK)[:, None] - np.arange(2 * BLOCK)[None, :]
    return _t5_bucket(np.clip(step, 0, N_STEPS) * dilation)


GRP_HEADS = 4
GRP_W = GRP_HEADS * HEAD_DIM


def _dil_masks():
    lane = lax.broadcasted_iota(jnp.int32, (1, GRP_W), 1)
    heads = [jnp.logical_and(lane >= HEAD_DIM * i, lane < HEAD_DIM * (i + 1)) for i in range(GRP_HEADS)]
    iq = lax.broadcasted_iota(jnp.int32, (BLOCK, BLOCK), 0)
    ik = lax.broadcasted_iota(jnp.int32, (BLOCK, BLOCK), 1)
    return heads, ik <= iq, ik >= iq


def _dil_rows(n):
    rs = pl.multiple_of(n * BLOCK, BLOCK)
    ps = pl.multiple_of(jnp.maximum(n - 1, 0) * BLOCK, BLOCK)
    return pl.ds(rs, BLOCK), pl.ds(ps, BLOCK)


def _dil_probs(qh, kc, kp, b_ref, hh, valid_c, valid_p):
    zc = _nt(qh, kc) * SCALE + b_ref[hh, :, BLOCK:2 * BLOCK]
    zp = _nt(qh, kp) * SCALE + b_ref[hh, :, 0:BLOCK]
    zc = jnp.where(valid_c, zc, NEG_INF)
    zp = jnp.where(valid_p, zp, NEG_INF)
    m = jnp.maximum(jnp.max(zc, axis=-1, keepdims=True), jnp.max(zp, axis=-1, keepdims=True))
    ec = jnp.exp(zc - m)
    ep = jnp.exp(zp - m)
    den = jnp.sum(ec, axis=-1, keepdims=True) + jnp.sum(ep, axis=-1, keepdims=True)
    return ec, ep, den, m


def _dil_fwd(qkv6r, base, bias, nb, sub_len, dilation):
    n_blk = sub_len // BLOCK

    def body(q_ref, k_ref, v_ref, b_ref, o_ref, l_ref):
        heads, valid_c, valid_p0 = _dil_masks()

        def nbody(n, carry):
            cur, prev = _dil_rows(n)
            valid_p = jnp.logical_and(valid_p0, n > 0)
            for gi in range(N_HEADS // GRP_HEADS):
                lanes = slice(gi * GRP_W, (gi + 1) * GRP_W)
                qv, kc, kp = q_ref[cur, lanes], k_ref[cur, lanes], k_ref[prev, lanes]
                vc, vp = v_ref[cur, lanes], v_ref[prev, lanes]
                o = jnp.zeros((BLOCK, GRP_W), F32)
                lse = jnp.zeros((BLOCK, GRP_W), F32)
                for i, hm in enumerate(heads):
                    qh = jnp.where(hm, qv, jnp.zeros_like(qv))
                    ec, ep, den, m = _dil_probs(qh, kc, kp, b_ref, gi * GRP_HEADS + i, valid_c, valid_p)
                    o = jnp.where(hm, (_nn(ec.astype(BF), vc) + _nn(ep.astype(BF), vp)) / den, o)
                    lse = jnp.where(hm, m + jnp.log(den), lse)
                o_ref[cur, lanes] = o
                l_ref[cur, lanes] = lse
            return carry

        lax.fori_loop(0, n_blk, nbody, 0)

    seqblk = lambda i: pl.BlockSpec((None, None, sub_len, D_GRP), lambda b, r: (i, b, 0, r))
    oblk = pl.BlockSpec((None, sub_len, D_GRP), lambda b, r: (b, 0, r))
    shp = _sds((nb, sub_len, dilation * D_GRP), F32)
    return pl.pallas_call(
        body, name="dil_fwd_%d" % dilation, grid=(nb, dilation),
        in_specs=[seqblk(base), seqblk(base + 1), seqblk(base + 2), _whole(bias)],
        out_specs=[oblk, oblk], out_shape=[shp, shp],
        compiler_params=_cp(2))(qkv6r, qkv6r, qkv6r, bias)


def _dil_bwd(qkv6r, base, bias, do_c, dd_c, nb, sub_len, dilation, carry=None):
    n_blk = sub_len // BLOCK

    def body(q_ref, k_ref, v_ref, b_ref, do_ref, dd_ref, out_ref, a_ref):
        heads, valid_c, valid_p0 = _dil_masks()
        first = jnp.logical_and(pl.program_id(0) == 0, pl.program_id(1) == 0)

        @pl.when(first)
        def _():
            a_ref[...] = jnp.zeros_like(a_ref)

        out_ref[1] = jnp.zeros((sub_len, D_GRP), F32)
        out_ref[2] = jnp.zeros((sub_len, D_GRP), F32)

        def nbody(n, carry):
            cur, prev = _dil_rows(n)
            valid_p = jnp.logical_and(valid_p0, n > 0)
            for gi in range(N_HEADS // GRP_HEADS):
                lanes = slice(gi * GRP_W, (gi + 1) * GRP_W)
                qv, kc, kp = q_ref[cur, lanes], k_ref[cur, lanes], k_ref[prev, lanes]
                vc, vp = v_ref[cur, lanes], v_ref[prev, lanes]
                dov, ddv = do_ref[cur, lanes], dd_ref[cur, lanes]
                dq = jnp.zeros((BLOCK, GRP_W), F32)
                dkc = jnp.zeros((BLOCK, GRP_W), F32)
                dkp = jnp.zeros((BLOCK, GRP_W), F32)
                dvc = jnp.zeros((BLOCK, GRP_W), F32)
                dvp = jnp.zeros((BLOCK, GRP_W), F32)
                for i, hm in enumerate(heads):
                    h = gi * GRP_HEADS + i
                    qh = jnp.where(hm, qv, jnp.zeros_like(qv))
                    doh = jnp.where(hm, dov, 0.0).astype(BF)
                    ddh = jnp.sum(jnp.where(hm, ddv, 0.0), axis=-1, keepdims=True) * (1.0 / HEAD_DIM)
                    ec, ep, den, _ = _dil_probs(qh, kc, kp, b_ref, h, valid_c, valid_p)
                    inv = 1.0 / den
                    pc = ec * inv
                    pp = ep * inv
                    dzc = pc * (_nt(doh, vc) + ddh)
                    dzp = pp * (_nt(doh, vp) + ddh)
                    a_ref[h, :, BLOCK:2 * BLOCK] += dzc
                    a_ref[h, :, 0:BLOCK] += dzp
                    dzcb = (dzc * SCALE).astype(BF)
                    dzpb = (dzp * SCALE).astype(BF)
                    dq = jnp.where(hm, _nn(dzcb, kc) + _nn(dzpb, kp), dq)
                    dkc = dkc + _tn(dzcb, qh)
                    dkp = dkp + _tn(dzpb, qh)
                    dvc = dvc + _tn(pc.astype(BF), doh)
                    dvp = dvp + _tn(pp.astype(BF), doh)
                out_ref[0, cur, lanes] = dq
                out_ref[1, cur, lanes] += dkc
                out_ref[1, prev, lanes] += dkp
                out_ref[2, cur, lanes] += dvc
                out_ref[2, prev, lanes] += dvp
            return carry

        lax.fori_loop(0, n_blk, nbody, 0)

    seqblk = lambda i: pl.BlockSpec((None, None, sub_len, D_GRP), lambda b, r: (i, b, 0, r))
    oblk = pl.BlockSpec((None, sub_len, D_GRP), lambda b, r: (b, 0, r))
    return _call(
        body, "dil_bwd_%d" % dilation, (nb, dilation),
        [seqblk(base), seqblk(base + 1), seqblk(base + 2), _whole(bias), oblk, oblk],
        [pl.BlockSpec((3, None, sub_len, D_GRP), lambda b, r: (0, b, 0, r)),
         pl.BlockSpec((N_HEADS, BLOCK, 2 * BLOCK), lambda b, r: (0, 0, 0))],
        [_sds((3, nb, sub_len, dilation * D_GRP), F32), _sds((N_HEADS, BLOCK, 2 * BLOCK), F32)],
        (qkv6r, qkv6r, qkv6r, bias, do_c, dd_c), carry=carry)


def _group_ones():
    idx = np.arange(D_GRP) // HEAD_DIM
    return jnp.asarray((idx[:, None] == idx[None, :]).astype(np.float32), dtype=BF)


def _dil_alphas(l1, l4, l16):
    mx = jnp.maximum(jnp.maximum(l1, l4), l16)
    e1 = jnp.exp(l1 - mx)
    e4 = jnp.exp(l4 - mx)
    e16 = jnp.exp(l16 - mx)
    den = e1 + e4 + e16
    return e1 / den, e4 / den, e16 / den


def _residue_spec(dil):
    return pl.BlockSpec((TM // dil, dil * D_GRP), lambda m: (m, 0))


def _from_residue(src, dil, cg, buf):
    if dil == 1:
        return src[:, cg * 128:(cg + 1) * 128]
    for r in range(dil):
        buf[pl.ds(r, TM // dil, stride=dil), :] = src[:, r * D_GRP + cg * 128:r * D_GRP + (cg + 1) * 128]
    return buf[...]


def _to_residue(dst, dil, cg, buf, val):
    if dil == 1:
        dst[:, cg * 128:(cg + 1) * 128] = val
        return
    buf[...] = val
    for r in range(dil):
        dst[:, r * D_GRP + cg * 128:r * D_GRP + (cg + 1) * 128] = buf[pl.ds(r, TM // dil, stride=dil), :]


def _pair_sum(x, hm0):
    s0 = jnp.sum(jnp.where(hm0, x, 0.0), axis=-1, keepdims=True)
    s1 = jnp.sum(jnp.where(hm0, 0.0, x), axis=-1, keepdims=True)
    return jnp.where(hm0, s0, s1)


def _dil_comb(os, ls, g_dil):
    t = os[0].shape[0]
    dils = [dil for _, dil in DIL_CONFIGS]

    def body(o1, l1, o4, l4, o16, l16, g_ref, o_ref, on_ref, b0, b1, b2, b3):
        hm0 = lax.broadcasted_iota(jnp.int32, (1, 128), 1) < HEAD_DIM
        for cg in range(D_GRP // 128):
            lanes = slice(cg * 128, (cg + 1) * 128)
            ov = [_from_residue(src, dil, cg, buf) for src, dil, buf in zip((o1, o4, o16), dils, (None, b0, b1))]
            lv = [_from_residue(src, dil, cg, buf) for src, dil, buf in zip((l1, l4, l16), dils, (None, b2, b3))]
            a1, a4, a16 = _dil_alphas(*lv)
            o = a1 * ov[0] + a4 * ov[1] + a16 * ov[2]
            o_ref[:, lanes] = o
            on_ref[:, lanes] = _headnorm_pair(o, g_ref[:, lanes], hm0).astype(BF)

    blk = pl.BlockSpec((TM, D_GRP), lambda m: (m, 0))
    specs = [_residue_spec(dil) for dil in dils for _ in range(2)]
    return pl.pallas_call(
        body, name="dil_comb", grid=(t // TM,),
        in_specs=specs + [pl.BlockSpec((1, D_GRP), lambda m: (0, 0))],
        out_specs=[blk, blk],
        out_shape=[_sds((t, D_GRP), F32), _sds((t, D_GRP), BF)],
        scratch_shapes=[pltpu.VMEM((TM, 128), F32)] * 4,
        compiler_params=_cp(1))(os[0], ls[0], os[1], ls[1], os[2], ls[2], g_dil)


def _dil_comb_bwd(do, os, ls):
    t = do.shape[0]
    dils = [dil for _, dil in DIL_CONFIGS]

    def body(do_ref, o1, l1, o4, l4, o16, l16, d1, d4, d16, e1, e4, e16, b0, b1, b2, b3):
        hm0 = lax.broadcasted_iota(jnp.int32, (1, 128), 1) < HEAD_DIM
        for cg in range(D_GRP // 128):
            dov = do_ref[:, cg * 128:(cg + 1) * 128]
            ov = [_from_residue(src, dil, cg, buf) for src, dil, buf in zip((o1, o4, o16), dils, (None, b0, b1))]
            lv = [_from_residue(src, dil, cg, buf) for src, dil, buf in zip((l1, l4, l16), dils, (None, b2, b3))]
            al = _dil_alphas(*lv)
            sbar = al[0] * _pair_sum(dov * ov[0], hm0)
            for a_c, o_c in zip(al[1:], ov[1:]):
                sbar = sbar + a_c * _pair_sum(dov * o_c, hm0)
            for a_c, dil, dref, eref in zip(al, dils, (d1, d4, d16), (e1, e4, e16)):
                _to_residue(dref, dil, cg, b0, a_c * dov)
                _to_residue(eref, dil, cg, b1, -a_c * sbar)

    specs = [_residue_spec(dil) for dil in dils]
    return pl.pallas_call(
        body, name="dil_comb_bwd", grid=(t // TM,),
        in_specs=[pl.BlockSpec((TM, D_GRP), lambda m: (m, 0))] + [sp for sp in specs for _ in range(2)],
        out_specs=specs + specs,
        out_shape=[_sds((t // dil, dil * D_GRP), F32) for dil in dils] * 2,
        scratch_shapes=[pltpu.VMEM((TM, 128), F32)] * 4,
        compiler_params=_cp(1))(do, os[0], ls[0], os[1], ls[1], os[2], ls[2])


def _dqkv_dil_sum(ds, dqkv6):
    t = dqkv6.shape[1]
    dils = [dil for _, dil in DIL_CONFIGS]

    def body(*refs):
        srcs, o_ref, acc = refs[:len(dils)], refs[len(dils) + 1], refs[len(dils) + 2]
        for a in range(3):
            for cg in range(D_GRP // 128):
                for src, dil in zip(srcs, dils):
                    for r in range(dil):
                        part = src[a, :, r * D_GRP + cg * 128:r * D_GRP + (cg + 1) * 128]
                        rows = pl.ds(r, TM // dil, stride=dil) if dil > 1 else slice(None)
                        if dil == dils[0]:
                            acc[rows, :] = part
                        else:
                            acc[rows, :] += part
                o_ref[a, :, cg * 128:(cg + 1) * 128] = acc[...].astype(BF)

    return pl.pallas_call(
        body, name="dqkv_dil_sum", grid=(t // TM,),
        in_specs=[pl.BlockSpec((3, TM // dil, dil * D_GRP), lambda m: (0, m, 0)) for dil in dils]
        + [pl.BlockSpec(memory_space=pl.ANY)],
        out_specs=pl.BlockSpec((3, TM, D_GRP), lambda m: (1, m, 0)),
        out_shape=_sds((6, t, D_GRP), BF), input_output_aliases={len(dils): 0},
        scratch_shapes=[pltpu.VMEM((TM, 128), F32)],
        compiler_params=_cp(1))(*ds, dqkv6)


def _relbias_grad(a_all, onehot):
    def body(a_ref, oh_ref, o_ref):
        acc = jnp.zeros((N_HEADS, N_BUCKETS), F32)
        for c in range(len(DIL_CONFIGS)):
            av = a_ref[c]
            hi = av.astype(BF)
            lo = (av - hi.astype(F32)).astype(BF)
            acc = acc + _nt(hi, oh_ref[c]) + _nt(lo, oh_ref[c])
        o_ref[...] = acc

    return pl.pallas_call(body, name="relbias_grad", out_shape=_sds((N_HEADS, N_BUCKETS), F32),
                          compiler_params=_cp())(a_all, onehot)


def _headnorm_bwd(dn, o, gv, mv):
    ms = _nn2(o * o, mv) * (1.0 / HEAD_DIM)
    r = lax.rsqrt(ms + EPS)
    nrm = o * r
    dg = jnp.sum(dn * nrm, axis=0, keepdims=True)
    dnn = dn * gv
    do = r * (dnn - nrm * (_nn2(dnn * nrm, mv) * (1.0 / HEAD_DIM)))
    return do, dg


def _mix_bwd_out(dx, gt, tv, w_out, o_sb, o_dil, on_sb, on_dil, g_sb, g_dil, ones_g, seq):
    t, d = dx.shape
    per = seq // TM
    nb = t // seq

    def body(dx_ref, gt_ref, t_ref, w_ref, osb, odl, onsb, ondl, gsb, gdl, m_ref,
             dosb, dodl, dgt_ref, dgsb, dgdl, dw_ref):
        m = pl.program_id(0)
        dxv = dx_ref[...]
        dt = (gt_ref[...] * dxv).astype(BF)
        _acc_rows(dgt_ref, jnp.sum(dxv * t_ref[...], axis=0, keepdims=True), m % per == 0)
        mv = m_ref[...]
        don_sb = _nt(dt, w_ref[0:D_GRP, :])
        don_dl = _nt(dt, w_ref[D_GRP:2 * D_GRP, :])
        do1, dg1 = _headnorm_bwd(don_sb, osb[...], gsb[...], mv)
        do2, dg2 = _headnorm_bwd(don_dl, odl[...], gdl[...], mv)
        dosb[...] = do1
        dodl[...] = do2
        _acc_rows(dgsb, dg1, m == 0)
        _acc_rows(dgdl, dg2, m == 0)
        p1 = _tn(onsb[...], dt)
        p2 = _tn(ondl[...], dt)

        @pl.when(m == 0)
        def _():
            dw_ref[0:D_GRP, :] = p1
            dw_ref[D_GRP:2 * D_GRP, :] = p2

        @pl.when(m != 0)
        def _():
            dw_ref[0:D_GRP, :] += p1
            dw_ref[D_GRP:2 * D_GRP, :] += p2

    row = pl.BlockSpec((TM, d), lambda m: (m, 0))
    half = pl.BlockSpec((TM, D_GRP), lambda m: (m, 0))
    ex = pl.BlockSpec((None, 1, d), lambda m: (m // per, 0, 0))
    gvec = pl.BlockSpec((1, D_GRP), lambda m: (0, 0))
    wblk = pl.BlockSpec((2 * D_GRP, d), lambda m: (0, 0))
    return pl.pallas_call(
        body, name="mix_bwd_out", grid=(t // TM,),
        in_specs=[row, ex, row, wblk, half, half, half, half, gvec, gvec,
                  pl.BlockSpec((D_GRP, D_GRP), lambda m: (0, 0))],
        out_specs=[half, half, ex, gvec, gvec, wblk],
        out_shape=[_sds((t, D_GRP), F32), _sds((t, D_GRP), F32), _sds((nb, 1, d), F32),
                   _sds((1, D_GRP), F32), _sds((1, D_GRP), F32), _sds((2 * D_GRP, d), F32)],
        compiler_params=_cp(1))(dx, gt, tv, w_out, o_sb, o_dil, on_sb, on_dil, g_sb, g_dil, ones_g)


def _dw_in(h, dqkv6, carry=None):
    t, d = h.shape
    wc = 6 * D_GRP // N_CHIPS

    def body(h_ref, g_ref, o_ref):
        kt = pl.program_id(0)
        hv = h_ref[...]
        for j in range(N_CHIPS):
            p = _tn(hv, _chip_cols(g_ref, j, wc))

            @pl.when(kt == 0)
            def _(p=p, j=j):
                o_ref[j, 0] = p

            @pl.when(kt != 0)
            def _(p=p, j=j):
                o_ref[j, 0] += p

    return _call(
        body, "dw_in", (t // TM,),
        [pl.BlockSpec((TM, d), lambda kt: (kt, 0)), pl.BlockSpec((6, TM, D_GRP), lambda kt: (0, kt, 0))],
        [pl.BlockSpec((N_CHIPS, 1, d, wc), lambda kt: (0, 0, 0, 0))],
        [_sds((N_CHIPS, 1, d, wc), F32)], (h, dqkv6), carry=carry)


def _mix_bwd_dh(dqkv6, w_in, x, g, sc, dxo, seq, carry=None):
    _, t, _ = dqkv6.shape
    d = x.shape[-1]
    wc = w_in.shape[-1]
    per = seq // TM
    nb = t // seq

    def body(g6_ref, w_ref, x_ref, g_ref, sc_ref, dxo_ref, dx_ref, dsh_ref, dsc_ref, dg_ref):
        m = pl.program_id(0)
        dh = _nt(_chip_cols(g6_ref, 0, wc), w_ref[0, 0])
        for j in range(1, N_CHIPS):
            dh = dh + _nt(_chip_cols(g6_ref, j, wc), w_ref[j, 0])
        dx, dsh, dsc, dg = _modnorm_bwd_tile(dh, x_ref[...], g_ref[...], sc_ref[...], dxo_ref[...])
        dx_ref[...] = dx
        _acc_rows(dsh_ref, dsh, m % per == 0)
        _acc_rows(dsc_ref, dsc, m % per == 0)
        _acc_rows(dg_ref, dg, m == 0)

    row = pl.BlockSpec((TM, d), lambda m: (m, 0))
    ex = pl.BlockSpec((None, 1, d), lambda m: (m // per, 0, 0))
    vec = pl.BlockSpec((1, d), lambda m: (0, 0))
    return _call(
        body, "mix_bwd_dh", (t // TM,),
        [pl.BlockSpec((6, TM, D_GRP), lambda m: (0, m, 0)), _whole(w_in), row, vec, ex, row],
        [row, ex, ex, vec],
        [_sds((t, d), F32), _sds((nb, 1, d), F32), _sds((nb, 1, d), F32), _sds((1, d), F32)],
        (dqkv6, w_in, x, g, sc, dxo), carry=carry)


def _final_loss(x, g, target):
    t, d = x.shape
    steps = t // TM

    def body(x_ref, g_ref, t_ref, dx_ref, dg_ref, loss_ref, lacc):
        m = pl.program_id(0)
        xv = x_ref[...]
        gv = g_ref[...]
        r = lax.rsqrt(jnp.mean(xv * xv, axis=-1, keepdims=True) + EPS)
        n = xv * r
        err = n * gv - t_ref[...]
        dy = err * (1.0 / d)
        _acc_rows(dg_ref, jnp.sum(dy * n, axis=0, keepdims=True), m == 0)
        dn = dy * gv
        dx_ref[...] = r * (dn - n * jnp.mean(dn * n, axis=-1, keepdims=True))
        _acc_rows(lacc, jnp.sum(err * err, axis=0, keepdims=True), m == 0)

        @pl.when(m == steps - 1)
        def _():
            tot = jnp.sum(lacc[...], axis=-1, keepdims=True) * (0.5 / d)
            loss_ref[...] = jnp.broadcast_to(tot, (1, 128))

    row = pl.BlockSpec((TM, d), lambda m: (m, 0))
    vec = pl.BlockSpec((1, d), lambda m: (0, 0))
    return pl.pallas_call(
        body, name="final_loss", grid=(steps,),
        in_specs=[row, vec, row],
        out_specs=[row, vec, pl.BlockSpec((1, 128), lambda m: (0, 0))],
        out_shape=[_sds((t, d), F32), _sds((1, d), F32), _sds((1, 128), F32)],
        scratch_shapes=[pltpu.VMEM((1, d), F32)],
        compiler_params=_cp(1))(x, g, target)


def _row_tile(rows, cols):
    best = rows
    for tr in range(8, rows + 1, 8):
        if rows % tr == 0 and tr * cols * 4 <= (1 << 20):
            best = tr
    if best * cols * 4 > (1 << 21):
        best = 8
    return best


def _adamw(w, g_arr, g_sel, m, v):
    rows, cols = w.shape
    tr = _row_tile(rows, cols)
    b1c = 1.0 - ADAM_B1 ** ADAM_STEP
    b2c = 1.0 - ADAM_B2 ** ADAM_STEP

    def body(w_ref, g_ref, m_ref, v_ref, go_ref, d_ref, mo_ref, vo_ref):
        gv = g_ref[...]
        mn = ADAM_B1 * m_ref[...] + (1.0 - ADAM_B1) * gv
        vn = ADAM_B2 * v_ref[...] + (1.0 - ADAM_B2) * (gv * gv)
        go_ref[...] = gv
        mo_ref[...] = mn
        vo_ref[...] = vn
        d_ref[...] = -ADAM_LR * ((mn / b1c) / (jnp.sqrt(vn / b2c) + ADAM_EPS) + ADAM_WD * w_ref[...])

    blk = pl.BlockSpec((tr, cols), lambda i: (i, 0))
    shp = _sds((rows, cols), F32)
    return pl.pallas_call(
        body, name="adamw", grid=(rows // tr,),
        in_specs=[blk, pl.BlockSpec((None, tr, cols), lambda i: (g_sel, i, 0)), blk, blk],
        out_specs=[blk] * 4, out_shape=[shp] * 4,
        compiler_params=_cp(1))(w, g_arr, m, v)


def _flip(v, bit):
    return 1 - v if bit else v


def _my_place():
    x, y, c = lax.axis_index("x"), lax.axis_index("y"), lax.axis_index("c")
    return x, y, c


class _Exchange:
    def __init__(self, operands, out_shape, aliases, sems, start, finish):
        self.operands, self.out_shape, self.aliases, self.sems = list(operands), list(out_shape), dict(aliases), list(sems)
        self.start, self.finish = start, finish


def _join(exchanges):
    exchanges = [e for e in exchanges if e is not None]
    if not exchanges:
        return None
    ops, outs, sems, aliases, spans = [], [], [], {}, []
    for e in exchanges:
        spans.append((len(ops), len(outs), len(sems), e))
        for i, j in e.aliases.items():
            aliases[len(ops) + i] = len(outs) + j
        ops += e.operands
        outs += e.out_shape
        sems += e.sems

    def run(which):
        def go(ins, res, sm):
            for io, oo, so, e in spans:
                getattr(e, which)(ins[io:io + len(e.operands)], res[oo:oo + len(e.out_shape)], sm[so:so + len(e.sems)])
        return go

    return _Exchange(ops, outs, aliases, sems, run("start"), run("finish"))


def _call(body, name, grid, in_specs, out_specs, out_shape, args, scratch=(), carry=None):
    in_specs, out_specs, out_shape, scratch = list(in_specs), list(out_specs), list(out_shape), list(scratch)
    if carry is None:
        return pl.pallas_call(body, name=name, grid=grid, in_specs=in_specs, out_specs=out_specs,
                              out_shape=out_shape, scratch_shapes=scratch,
                              compiler_params=_cp(len(grid)))(*args)
    n_in, n_out, n_s = len(in_specs), len(out_specs), len(scratch)
    c_in, c_out = len(carry.operands), len(carry.out_shape)
    any_spec = pl.BlockSpec(memory_space=pl.ANY)

    def wrapped(*refs):
        ins, cins = refs[:n_in], refs[n_in:n_in + c_in]
        o0 = n_in + c_in
        outs, couts = refs[o0:o0 + n_out], refs[o0 + n_out:o0 + n_out + c_out]
        s0 = o0 + n_out + c_out
        scr, sems = refs[s0:s0 + n_s], refs[s0 + n_s:]
        first = pl.program_id(0) == 0
        last = pl.program_id(0) == grid[0] - 1
        for ax in range(1, len(grid)):
            first = jnp.logical_and(first, pl.program_id(ax) == 0)
            last = jnp.logical_and(last, pl.program_id(ax) == grid[ax] - 1)

        @pl.when(first)
        def _():
            carry.start(cins, couts, sems)

        body(*ins, *outs, *scr)

        @pl.when(last)
        def _():
            carry.finish(cins, couts, sems)

    return pl.pallas_call(
        wrapped, name=name, grid=grid, in_specs=in_specs + [any_spec] * c_in,
        out_specs=out_specs + [any_spec] * c_out, out_shape=out_shape + carry.out_shape,
        scratch_shapes=scratch + carry.sems,
        input_output_aliases={n_in + i: n_out + j for i, j in carry.aliases.items()},
        compiler_params=_cp(len(grid)))(*args, *carry.operands)


def _whole_call(body, name, args, out_shape, scratch, carry=None):
    vm = pl.BlockSpec(memory_space=pltpu.VMEM)
    any_spec = pl.BlockSpec(memory_space=pl.ANY)
    out_shape, scratch = list(out_shape), list(scratch)
    n_in, n_out, n_s = len(args), len(out_shape), len(scratch)
    if carry is None:
        return pl.pallas_call(body, name=name, in_specs=[vm] * n_in, out_specs=[vm] * n_out, out_shape=out_shape,
                              scratch_shapes=scratch, compiler_params=_cp())(*args)
    c_in, c_out = len(carry.operands), len(carry.out_shape)

    def wrapped(*refs):
        ins, cins = refs[:n_in], refs[n_in:n_in + c_in]
        o0 = n_in + c_in
        outs, couts = refs[o0:o0 + n_out], refs[o0 + n_out:o0 + n_out + c_out]
        s0 = o0 + n_out + c_out
        scr, sems = refs[s0:s0 + n_s], refs[s0 + n_s:]
        carry.start(cins, couts, sems)
        body(*ins, *outs, *scr)
        carry.finish(cins, couts, sems)

    return pl.pallas_call(
        wrapped, name=name, in_specs=[vm] * n_in + [any_spec] * c_in, out_specs=[vm] * n_out + [any_spec] * c_out,
        out_shape=out_shape + carry.out_shape, scratch_shapes=scratch + carry.sems,
        input_output_aliases={n_in + i: n_out + j for i, j in carry.aliases.items()},
        compiler_params=_cp())(*args, *carry.operands)


def _alone(name, ex):
    any_spec = pl.BlockSpec(memory_space=pl.ANY)
    c_in, c_out = len(ex.operands), len(ex.out_shape)

    def body(*refs):
        ins, outs, sems = refs[:c_in], refs[c_in:c_in + c_out], refs[c_in + c_out:]
        ex.start(ins, outs, sems)
        ex.finish(ins, outs, sems)

    return pl.pallas_call(
        body, name=name, in_specs=[any_spec] * c_in, out_specs=[any_spec] * c_out, out_shape=ex.out_shape,
        scratch_shapes=ex.sems, input_output_aliases=ex.aliases, compiler_params=_cp())(*ex.operands)


def _ada_fwd(c_pad, w_ada, b_shard, carry=None):
    d = c_pad.shape[-1]
    cols = w_ada.shape[-1]
    chunk = 384

    def body(c_ref, w_ref, b_ref, call_ref, mod_ref, part, s1, r1, s2, r2):
        x, y, c = _my_place()
        dev = 4 * x + 2 * y + c
        chip = 2 * x + y
        call_ref[dev] = c_ref[...]

        def c_copy(k):
            px, py, pc = _flip(x, (k >> 2) & 1), _flip(y, (k >> 1) & 1), _flip(c, k & 1)
            return px, py, pc

        sends = []
        for k in range(1, N_DEV):
            px, py, pc = c_copy(k)
            cp = pltpu.make_async_remote_copy(src_ref=c_ref, dst_ref=call_ref.at[dev], send_sem=s1.at[k - 1],
                                              recv_sem=r1.at[k - 1], device_id=(px, py, pc), device_id_type=MESH)
            cp.start()
            sends.append(cp)
        for k in range(1, N_DEV):
            px, py, pc = c_copy(k)
            pltpu.make_async_remote_copy(src_ref=c_ref, dst_ref=call_ref.at[4 * px + 2 * py + pc],
                                         send_sem=s1.at[k - 1], recv_sem=r1.at[k - 1],
                                         device_id=(px, py, pc), device_id_type=MESH).wait_recv()
        for cp in sends:
            cp.wait_send()

        cs = call_ref[...].reshape(N_DEV * 8, d)
        sc = (cs * jax.nn.sigmoid(cs)).astype(BF)
        for n0 in range(0, cols, chunk):
            blk = _nn(sc, w_ref[:, n0:n0 + chunk].astype(BF)) + b_ref[:, n0:n0 + chunk]
            part[:, :, n0:n0 + chunk] = blk.reshape(N_DEV, 8, chunk)

        mod_ref[chip] = part[dev]
        sends = []
        for kk in range(1, N_CHIPS):
            px, py = _flip(x, (kk >> 1) & 1), _flip(y, kk & 1)
            cp = pltpu.make_async_remote_copy(src_ref=part.at[4 * px + 2 * py + c], dst_ref=mod_ref.at[chip],
                                              send_sem=s2.at[kk - 1], recv_sem=r2.at[kk - 1],
                                              device_id=(px, py, c), device_id_type=MESH)
            cp.start()
            sends.append(cp)
        for kk in range(1, N_CHIPS):
            px, py = _flip(x, (kk >> 1) & 1), _flip(y, kk & 1)
            pltpu.make_async_remote_copy(src_ref=part.at[dev], dst_ref=mod_ref.at[2 * px + py],
                                         send_sem=s2.at[kk - 1], recv_sem=r2.at[kk - 1],
                                         device_id=(px, py, c), device_id_type=MESH).wait_recv()
        for cp in sends:
            cp.wait_send()

    return _whole_call(
        body, "ada_fwd", (c_pad, w_ada, b_shard),
        [_sds((N_DEV, 8, d), F32), _sds((N_CHIPS, 8, cols), F32)],
        [pltpu.VMEM((N_DEV, 8, cols), F32),
         pltpu.SemaphoreType.DMA((N_DEV - 1,)), pltpu.SemaphoreType.DMA((N_DEV - 1,)),
         pltpu.SemaphoreType.DMA((N_CHIPS - 1,)), pltpu.SemaphoreType.DMA((N_CHIPS - 1,))], carry=carry)


def _ag_weights(bufs):
    n = len(bufs)

    def place():
        x, y, c = _my_place()
        others = [(_flip(x, (kk >> 1) & 1), _flip(y, kk & 1)) for kk in range(1, N_CHIPS)]
        return x, y, c, 2 * x + y, others

    def half(b, which):
        hr = bufs[b].shape[2] // 2
        return pl.ds(pl.multiple_of(which * hr, 16), hr)

    def ici(outs, sems, b, i, slot, x, y, c, px, py):
        rows = outs[b].at[slot, :, half(b, c), :]
        return pltpu.make_async_remote_copy(
            src_ref=rows, dst_ref=rows, send_sem=sems[0].at[3 * b + i], recv_sem=sems[1].at[3 * b + i],
            device_id=(px, py, c), device_id_type=MESH)

    def d2d(outs, sems, b, i, slot, x, y, c, which):
        rows = outs[b].at[slot, :, half(b, which), :]
        return pltpu.make_async_remote_copy(
            src_ref=rows, dst_ref=rows, send_sem=sems[2].at[3 * b + i], recv_sem=sems[3].at[3 * b + i],
            device_id=(x, y, 1 - c), device_id_type=MESH)

    def start(ins, outs, sems):
        x, y, c, chip, others = place()
        for b in range(n):
            for i, (px, py) in enumerate(others):
                ici(outs, sems, b, i, chip, x, y, c, px, py).start()

    def finish(ins, outs, sems):
        x, y, c, chip, others = place()
        for b in range(n):
            for i, (px, py) in enumerate(others):
                ici(outs, sems, b, i, 2 * px + py, x, y, c, px, py).wait_recv()
                d2d(outs, sems, b, i, 2 * px + py, x, y, c, c).start()
        for b in range(n):
            for i, (px, py) in enumerate(others):
                d2d(outs, sems, b, i, 2 * px + py, x, y, c, 1 - c).wait_recv()
        for b in range(n):
            for i, (px, py) in enumerate(others):
                ici(outs, sems, b, i, chip, x, y, c, px, py).wait_send()
                d2d(outs, sems, b, i, 2 * px + py, x, y, c, c).wait_send()

    return _Exchange(bufs, [_sds(s.shape, s.dtype) for s in bufs], {i: i for i in range(n)},
                     [pltpu.SemaphoreType.DMA((3 * n,))] * 4, start, finish)


def _rs_d2d(grads):
    n = len(grads)

    def copy(ins, outs, sems, b):
        x, y, c = _my_place()
        hr = grads[b].shape[2] // 2
        theirs = pl.ds(pl.multiple_of((1 - c) * hr, 8), hr)
        return pltpu.make_async_remote_copy(
            src_ref=ins[b].at[:, :, theirs, :], dst_ref=outs[b], send_sem=sems[0].at[b], recv_sem=sems[1].at[b],
            device_id=(x, y, 1 - c), device_id_type=MESH)

    def start(ins, outs, sems):
        for b in range(n):
            copy(ins, outs, sems, b).start()

    def finish(ins, outs, sems):
        for b in range(n):
            copy(ins, outs, sems, b).wait()

    return _Exchange(grads, [_sds(g.shape[:2] + (g.shape[2] // 2, g.shape[3]), F32) for g in grads], {},
                     [pltpu.SemaphoreType.DMA((n,))] * 2, start, finish)


def _add_halves(core, g, land):
    nchip, ng, rows, cols = g.shape
    hr = rows // 2
    tr = _row_tile(hr, cols)
    steps = hr // tr

    def body(core_ref, g_ref, l_ref, o_ref):
        del core_ref
        o_ref[...] = (g_ref[...] + l_ref[...]).astype(BF)

    return pl.pallas_call(
        body, name="add_halves",
        grid_spec=pltpu.PrefetchScalarGridSpec(
            num_scalar_prefetch=1, grid=(nchip, ng, steps),
            in_specs=[pl.BlockSpec((None, None, tr, cols), lambda j, a, i, cr: (j, a, cr[0] * steps + i, 0)),
                      pl.BlockSpec((None, None, tr, cols), lambda j, a, i, cr: (j, a, i, 0))],
            out_specs=pl.BlockSpec((None, None, tr, cols), lambda j, a, i, cr: (j, a, i, 0))),
        out_shape=_sds((nchip, ng, hr, cols), BF),
        compiler_params=_cp(3))(core, g, land)


def _rs_ici(parts):
    n = len(parts)

    def copies(ins, outs, sems):
        x, y, c = _my_place()
        chip = 2 * x + y
        for b in range(n):
            for kk in range(1, N_CHIPS):
                px, py = _flip(x, (kk >> 1) & 1), _flip(y, kk & 1)
                k = 3 * b + kk - 1
                send = pltpu.make_async_remote_copy(
                    src_ref=ins[b].at[2 * px + py], dst_ref=outs[b].at[chip],
                    send_sem=sems[0].at[k], recv_sem=sems[1].at[k], device_id=(px, py, c), device_id_type=MESH)
                slot = outs[b].at[2 * px + py]
                recv = pltpu.make_async_remote_copy(
                    src_ref=slot, dst_ref=slot, send_sem=sems[0].at[k], recv_sem=sems[1].at[k],
                    device_id=(px, py, c), device_id_type=MESH)
                yield send, recv

    def start(ins, outs, sems):
        for send, _ in copies(ins, outs, sems):
            send.start()

    def finish(ins, outs, sems):
        for send, recv in copies(ins, outs, sems):
            recv.wait_recv()
            send.wait_send()

    return _Exchange(parts, [_sds(p.shape, p.dtype) for p in parts], {},
                     [pltpu.SemaphoreType.DMA((3 * n,))] * 2, start, finish)


def _sum_chips(place, part, land):
    nchip, ng, hr, cols = land.shape
    tr = _row_tile(hr, cols)
    steps = hr // tr

    def body(place_ref, p_ref, l1, l2, l3, o_ref):
        del place_ref
        o_ref[...] = ((p_ref[...].astype(F32) + l1[...].astype(F32)) + l2[...].astype(F32)) + l3[...].astype(F32)

    def slot(k):
        return pl.BlockSpec((None, None, tr, cols), lambda a, i, pr: (jnp.bitwise_xor(pr[1], k), a, i, 0))

    return pl.pallas_call(
        body, name="sum_chips",
        grid_spec=pltpu.PrefetchScalarGridSpec(
            num_scalar_prefetch=1, grid=(ng, steps),
            in_specs=[slot(0), slot(1), slot(2), slot(3)],
            out_specs=pl.BlockSpec((None, tr, cols), lambda a, i, pr: (a, pr[0] * steps + i, 0))),
        out_shape=_sds((ng, 2 * hr, cols), F32),
        compiler_params=_cp(2))(place, part, land, land, land)


def _rs_final(bufs):
    n = len(bufs)

    def copy(outs, sems, b, which):
        x, y, c = _my_place()
        hr = bufs[b].shape[1] // 2
        rows = outs[b].at[:, pl.ds(pl.multiple_of((c if which == 0 else 1 - c) * hr, 8), hr), :]
        return pltpu.make_async_remote_copy(
            src_ref=rows, dst_ref=rows, send_sem=sems[0].at[b], recv_sem=sems[1].at[b],
            device_id=(x, y, 1 - c), device_id_type=MESH)

    def start(ins, outs, sems):
        for b in range(n):
            copy(outs, sems, b, 0).start()

    def finish(ins, outs, sems):
        for b in range(n):
            copy(outs, sems, b, 0).wait_send()
            copy(outs, sems, b, 1).wait_recv()

    return _Exchange(bufs, [_sds(h.shape, F32) for h in bufs], {i: i for i in range(n)},
                     [pltpu.SemaphoreType.DMA((n,))] * 2, start, finish)


def _small_sync(smalls, dmod_blk, c_all, carry=None):
    d = c_all.shape[-1]
    cols = dmod_blk.shape[-1]
    chunk = 384

    def body(sm_ref, dm_ref, c_ref, sum_ref, gw_ref, sm_all, dm_all, ssem, rsem):
        x, y, c = _my_place()
        dev = 4 * x + 2 * y + c
        chip = 2 * x + y
        sm_all[dev] = sm_ref[...]
        dm_all[dev] = dm_ref[chip]
        sends = []
        for k in range(1, N_DEV):
            px, py, pc = _flip(x, (k >> 2) & 1), _flip(y, (k >> 1) & 1), _flip(c, k & 1)
            a = pltpu.make_async_remote_copy(src_ref=sm_ref, dst_ref=sm_all.at[dev], send_sem=ssem.at[2 * (k - 1)],
                                             recv_sem=rsem.at[2 * (k - 1)], device_id=(px, py, pc),
                                             device_id_type=MESH)
            b = pltpu.make_async_remote_copy(src_ref=dm_ref.at[2 * px + py], dst_ref=dm_all.at[dev],
                                             send_sem=ssem.at[2 * (k - 1) + 1], recv_sem=rsem.at[2 * (k - 1) + 1],
                                             device_id=(px, py, pc), device_id_type=MESH)
            a.start()
            b.start()
            sends += [a, b]
        for k in range(1, N_DEV):
            px, py, pc = _flip(x, (k >> 2) & 1), _flip(y, (k >> 1) & 1), _flip(c, k & 1)
            pdev = 4 * px + 2 * py + pc
            pltpu.make_async_remote_copy(src_ref=sm_ref, dst_ref=sm_all.at[pdev], send_sem=ssem.at[2 * (k - 1)],
                                         recv_sem=rsem.at[2 * (k - 1)], device_id=(px, py, pc),
                                         device_id_type=MESH).wait_recv()
            pltpu.make_async_remote_copy(src_ref=dm_ref.at[chip], dst_ref=dm_all.at[pdev],
                                         send_sem=ssem.at[2 * (k - 1) + 1], recv_sem=rsem.at[2 * (k - 1) + 1],
                                         device_id=(px, py, pc), device_id_type=MESH).wait_recv()
        for cp in sends:
            cp.wait_send()

        tot = sm_all[0]
        for q in range(1, N_DEV):
            tot = tot + sm_all[q]
        sum_ref[...] = tot

        cs = c_ref[...].reshape(N_DEV * 8, d)
        sc = (cs * jax.nn.sigmoid(cs)).astype(BF)
        for n0 in range(0, cols, chunk):
            dmv = dm_all[:, :, n0:n0 + chunk].reshape(N_DEV * 8, chunk).astype(BF)
            gw_ref[:, n0:n0 + chunk] = _tn(sc, dmv)

    return _whole_call(
        body, "small_sync", (smalls, dmod_blk, c_all),
        [_sds(smalls.shape, F32), _sds((d, cols), F32)],
        [pltpu.VMEM((N_DEV,) + smalls.shape, F32), pltpu.VMEM((N_DEV, 8, cols), F32),
         pltpu.SemaphoreType.DMA((2 * (N_DEV - 1),)), pltpu.SemaphoreType.DMA((2 * (N_DEV - 1),))], carry=carry)


def _bucket_onehot():
    maps = np.stack([_bucket_map(dil).reshape(-1) for _, dil in DIL_CONFIGS])
    return (jnp.asarray(maps)[:, None, :] == jnp.arange(N_BUCKETS, dtype=jnp.int32)[None, :, None]).astype(BF)


def _dil_bias(rel_t, onehot):
    def body(r_ref, oh_ref, o_ref):
        rv = r_ref[...]
        hi = rv.astype(BF)
        lo = (rv - hi.astype(F32)).astype(BF)
        for c in range(len(DIL_CONFIGS)):
            o_ref[c] = _nn(hi, oh_ref[c]) + _nn(lo, oh_ref[c])

    return pl.pallas_call(body, name="dil_bias",
                          out_shape=_sds((len(DIL_CONFIGS), N_HEADS, BLOCK * 2 * BLOCK), F32),
                          compiler_params=_cp())(rel_t, onehot)


def _rowsum8(a):
    def body(a_ref, o_ref):
        o_ref[...] = jnp.sum(a_ref[...], axis=0, keepdims=True)

    return pl.pallas_call(body, name="rowsum8", out_shape=_sds((1, a.shape[1]), F32), compiler_params=_cp())(a)


def _local_step(x, mod, target, w, gains, rel_bias, place=None):
    nb, seq, d = x.shape
    t = nb * seq
    dist = place is not None
    core = place[0:1] if dist else None
    x0 = x.reshape(t, d)
    tgt = target.reshape(t, d)
    md = [mod[:, i:i + 1, :] for i in range(N_MOD)]
    sh1, sc1, gt1, sh2, sc2, gt2, sh3, sc3, gt3 = md
    g1, g2, g3 = gains["g_ffn1"], gains["g_mix"], gains["g_ffn2"]
    ones_g = _group_ones()

    def partial_sums(grads, lands):
        return [_add_halves(core, g, l) for g, l in zip(grads, lands)]

    def chip_sums(parts, lands):
        return [_sum_chips(place, p, l) for p, l in zip(parts, lands)]

    h1 = _modnorm(x0, g1, sc1, sh1, seq)
    res = _ffn_up(h1, w["gu1"], carry=_ag_weights([w["d1"], w["win"], w["wout"]]) if dist else None)
    a1, u1, s1 = res[:3]
    wd1, w_in, w_out = res[3:] if dist else (w["d1"], w["win"], w["wout"])
    w_out2 = w_out.reshape(2 * D_GRP, d)
    f1, x1 = _ffn_down(s1, wd1, x0, gt1, seq, 0.5)

    h2 = _modnorm(x1, g2, sc2, sh2, seq)
    qkv6, qkv_r4, qkv_r16 = _qkv_proj(h2, w_in)
    qkv6b = qkv6.reshape(6, nb, seq, D_GRP)
    res = _sb_fwd(qkv6b, gains["g_sb_out"], nb, seq, carry=_ag_weights([w["gu2"], w["d2"]]) if dist else None)
    o_sb, on_sb = res[:2]
    wgu2, wd2 = res[2:] if dist else (w["gu2"], w["d2"])
    onehot = _bucket_onehot()
    bias = _dil_bias(rel_bias.T, onehot).reshape(len(DIL_CONFIGS), N_HEADS, BLOCK, 2 * BLOCK)
    o_cs, l_cs = [], []
    qkv_rs = [(qkv6b, 3), (qkv_r4, 0), (qkv_r16, 0)]
    for ci, (_, dil) in enumerate(DIL_CONFIGS):
        sub = seq // dil
        arr, base = qkv_rs[ci]
        arr = arr.reshape(base + 3, nb, sub, dil * D_GRP)
        qkv_rs[ci] = (arr, base)
        o_c, l_c = _dil_fwd(arr, base, bias[ci], nb, sub, dil)
        o_cs.append(o_c.reshape(t // dil, dil * D_GRP))
        l_cs.append(l_c.reshape(t // dil, dil * D_GRP))
    o_dil, on_dil = _dil_comb(o_cs, l_cs, gains["g_dil_out"])
    tmix, x2 = _mix_out(on_sb.reshape(t, D_GRP), on_dil, w_out2, x1, gt2, seq)

    h3 = _modnorm(x2, g3, sc3, sh3, seq)
    a3, u3, s3 = _ffn_up(h3, wgu2)
    f3, x3 = _ffn_down(s3, wd2, x2, gt3, seq, 0.5)

    dx3, dg_final, loss = _final_loss(x3, gains["g_final"], tgt)

    da3, du3, df3, dgt3 = _ffn_bwd_ds(dx3, gt3, f3, wd2, a3, u3, seq, 0.5)
    grads2 = [_ffn_bwd_w(h3, da3, du3, s3, df3)]
    res = _ffn_bwd_dh(da3, du3, wgu2, x2, g3, sc3, dx3, seq, carry=_rs_d2d(grads2) if dist else None)
    dx2, dsh3, dsc3, dg3 = res[:4]
    parts2 = partial_sums(grads2, res[4:]) if dist else None

    do_sb, do_dil, dgt2, dg_sb, dg_dil, dw_out = _mix_bwd_out(
        dx2, gt2, tmix, w_out2, o_sb.reshape(t, D_GRP), o_dil, on_sb.reshape(t, D_GRP), on_dil,
        gains["g_sb_out"], gains["g_dil_out"], ones_g, seq)
    dw_out = dw_out.reshape(N_CHIPS, 1, 2 * D_GRP // N_CHIPS, d)
    res = _sb_bwd(qkv6b, do_sb.reshape(nb, seq, D_GRP), nb, seq, carry=_rs_ici(parts2) if dist else None)
    dqkv6 = res[0]
    halves2 = chip_sums(parts2, res[1:]) if dist else None
    dcs = _dil_comb_bwd(do_dil, o_cs, l_cs)
    dsum, a_tiles = [], []
    for ci, (_, dil) in enumerate(DIL_CONFIGS):
        sub = seq // dil
        do_c = dcs[ci].reshape(nb, sub, dil * D_GRP)
        dd_c = dcs[3 + ci].reshape(nb, sub, dil * D_GRP)
        res = _dil_bwd(qkv_rs[ci][0], qkv_rs[ci][1], bias[ci], do_c, dd_c, nb, sub, dil,
                       carry=_rs_final(halves2) if dist and ci == 0 else None)
        if dist and ci == 0:
            grads2 = res[2:]
        dsum.append(res[0].reshape(3, t // dil, dil * D_GRP))
        a_tiles.append(res[1].reshape(N_HEADS, BLOCK * 2 * BLOCK))
    dqkv6 = _dqkv_dil_sum(dsum, dqkv6.reshape(6, t, D_GRP))
    drel = _relbias_grad(jnp.stack(a_tiles), onehot)
    dx1, dsh2, dsc2, dg2 = _mix_bwd_dh(dqkv6, w_in, x1, g2, sc2, dx2, seq)

    da1, du1, df1, dgt1 = _ffn_bwd_ds(dx1, gt1, f1, wd1, a1, u1, seq, 0.5)
    grads1 = [_ffn_bwd_w(h1, da1, du1, s1, df1)]
    res = _dw_in(h2, dqkv6, carry=_rs_d2d(grads1) if dist else None)
    grads_m = [res[0], dw_out]
    parts1 = partial_sums(grads1, res[1:]) if dist else None
    res = _ffn_bwd_dh(da1, du1, w["gu1"], x0, g1, sc1, dx1, seq,
                      carry=_join([_rs_ici(parts1), _rs_d2d(grads_m)]) if dist else None)
    dx0, dsh1, dsc1, dg1 = res[:4]
    pending = None
    if dist:
        pending = (chip_sums(parts1, res[4:5]), partial_sums(grads_m, res[5:7]))

    dmod = jnp.concatenate([dsh1, dsc1, dgt1, dsh2, dsc2, dgt2, dsh3, dsc3, dgt3], axis=1)
    return dict(grad_x=dx0.reshape(nb, seq, d), loss=loss[0, 0], dmod=dmod.reshape(nb, N_MOD * d),
                dffn1=grads1[0], dffn2=grads2[0], dwin=grads_m[0], dwout=grads_m[1], pending=pending,
                dg_ffn1=dg1, dg_mix=dg2, dg_ffn2=dg3, dg_final=dg_final, dg_sb=dg_sb, dg_dil=dg_dil,
                drel=drel.T)


_SMALL_ORDER = (("b_ada", N_MOD * 1024), ("g_ffn1", 1024), ("g_mix", 1024), ("g_ffn2", 1024), ("g_final", 1024),
                ("g_sb_out", D_GRP), ("g_dil_out", D_GRP), ("rel_bias", N_BUCKETS * N_HEADS))


def _pack_small(parts, extra=None):
    flat = [parts[name].reshape(-1).astype(F32) for name, _ in _SMALL_ORDER]
    used = sum(sz for _, sz in _SMALL_ORDER)
    pad = SMALL_ROWS * 128 - used
    tail = jnp.zeros((pad,), F32)
    if extra is not None:
        tail = tail.at[0].set(extra)
    return jnp.concatenate(flat + [tail]).reshape(SMALL_ROWS, 128)


def _unpack_small(packed, shapes):
    flat = packed.reshape(-1)
    out, off = {}, 0
    for name, sz in _SMALL_ORDER:
        out[name] = flat[off:off + sz].reshape(shapes[name])
        off += sz
    return out, flat[off]


def kernel(x, c, w_ada, b_ada, g_ffn1, w1_gate, w1_up, w1_down, g_mix, w_in, g_sb_out, g_dil_out, w_out, rel_bias, g_ffn2, w2_gate, w2_up, w2_down, g_final, loss_target, m_w_ada, m_b_ada, m_g_ffn1, m_w1_gate, m_w1_up, m_w1_down, m_g_mix, m_w_in, m_g_sb_out, m_g_dil_out, m_w_out, m_rel_bias, m_g_ffn2, m_w2_gate, m_w2_up, m_w2_down, m_g_final, v_w_ada, v_b_ada, v_g_ffn1, v_w1_gate, v_w1_up, v_w1_down, v_g_mix, v_w_in, v_g_sb_out, v_g_dil_out, v_w_out, v_rel_bias, v_g_ffn2, v_w2_gate, v_w2_up, v_w2_down, v_g_final):
    nb, seq, d = x.shape
    xi, yi, ci = lax.axis_index("x"), lax.axis_index("y"), lax.axis_index("c")
    chip = 2 * xi + yi
    ada_cols = w_ada.shape[-1]

    c_pad = jnp.zeros((8, d), F32).at[:nb].set(c)
    b_shard = lax.dynamic_slice(b_ada, (0, chip * ada_cols), (1, ada_cols))
    shards = dict(gu1=jnp.stack([w1_gate[0], w1_up[0]]), d1=w1_down, win=w_in, wout=w_out,
                  gu2=jnp.stack([w2_gate[0], w2_up[0]]), d2=w2_down)
    bufs = {k: lax.dynamic_update_slice(lax.empty((N_CHIPS,) + s.shape, BF), s.astype(BF)[None], (chip, 0, 0, 0))
            for k, s in shards.items()}
    c_all, mod_blk, bufs["gu1"] = _ada_fwd(c_pad, w_ada[0], b_shard, carry=_ag_weights([bufs["gu1"]]))
    mod = jnp.transpose(mod_blk[:, :nb, :], (1, 0, 2)).reshape(nb, N_MOD, d)

    gains = dict(g_ffn1=g_ffn1, g_mix=g_mix, g_ffn2=g_ffn2, g_final=g_final.reshape(1, d),
                 g_sb_out=g_sb_out.reshape(1, D_GRP), g_dil_out=g_dil_out.reshape(1, D_GRP))
    place = jnp.stack([ci, chip]).astype(jnp.int32)
    r = _local_step(x, mod, loss_target, bufs, gains, rel_bias, place)

    dmod = r["dmod"]
    dmod_pad = jnp.zeros((8, N_MOD * d), F32).at[:nb].set(dmod)
    dmod_blk = jnp.transpose(dmod_pad.reshape(8, N_CHIPS, ada_cols), (1, 0, 2))
    small_parts = dict(b_ada=_rowsum8(dmod_pad), g_ffn1=r["dg_ffn1"], g_mix=r["dg_mix"], g_ffn2=r["dg_ffn2"],
                       g_final=r["dg_final"], g_sb_out=r["dg_sb"], g_dil_out=r["dg_dil"], rel_bias=r["drel"])
    halves1, parts_m = r["pending"]
    res = _small_sync(_pack_small(small_parts, r["loss"]), dmod_blk, c_all,
                      carry=_join([_rs_final(halves1), _rs_ici(parts_m)]))
    small_sum, g_wada, gffn1 = res[:3]
    halves_m = [_sum_chips(place, p, l) for p, l in zip(parts_m, res[3:5])]
    gwin, gwout = _alone("rs_last", _rs_final(halves_m))
    gffn2 = r["dffn2"]

    small_w = dict(b_ada=b_ada, g_ffn1=g_ffn1, g_mix=g_mix, g_ffn2=g_ffn2, g_final=g_final,
                   g_sb_out=g_sb_out, g_dil_out=g_dil_out, rel_bias=rel_bias)
    small_m = dict(b_ada=m_b_ada, g_ffn1=m_g_ffn1, g_mix=m_g_mix, g_ffn2=m_g_ffn2, g_final=m_g_final,
                   g_sb_out=m_g_sb_out, g_dil_out=m_g_dil_out, rel_bias=m_rel_bias)
    small_v = dict(b_ada=v_b_ada, g_ffn1=v_g_ffn1, g_mix=v_g_mix, g_ffn2=v_g_ffn2, g_final=v_g_final,
                   g_sb_out=v_g_sb_out, g_dil_out=v_g_dil_out, rel_bias=v_rel_bias)
    shapes = {k: v.shape for k, v in small_w.items()}
    sg, sd, sm, sv = _adamw(_pack_small(small_w), small_sum.reshape(1, SMALL_ROWS, 128), 0,
                            _pack_small(small_m), _pack_small(small_v))
    sg, loss = _unpack_small(sg, shapes)
    sd, _ = _unpack_small(sd, shapes)
    sm, _ = _unpack_small(sm, shapes)
    sv, _ = _unpack_small(sv, shapes)

    big = {}

    def upd(name, w, g_arr, sel, m, v, transposed=False):
        swap = (lambda a: jnp.swapaxes(a, -1, -2)) if transposed else (lambda a: a)
        w2, m2, v2 = [swap(a)[0] for a in (w, m, v)]
        big[name] = [swap(a[None]) for a in _adamw(w2, g_arr, sel, m2, v2)]

    upd("w_ada", w_ada, g_wada.reshape(1, d, ada_cols), 0, m_w_ada, v_w_ada)
    upd("w1_gate", w1_gate, gffn1, 0, m_w1_gate, v_w1_gate, transposed=True)
    upd("w1_up", w1_up, gffn1, 1, m_w1_up, v_w1_up, transposed=True)
    upd("w1_down", w1_down, gffn1, 2, m_w1_down, v_w1_down)
    upd("w_in", w_in, gwin, 0, m_w_in, v_w_in)
    upd("w_out", w_out, gwout, 0, m_w_out, v_w_out)
    upd("w2_gate", w2_gate, gffn2, 0, m_w2_gate, v_w2_gate, transposed=True)
    upd("w2_up", w2_up, gffn2, 1, m_w2_up, v_w2_up, transposed=True)
    upd("w2_down", w2_down, gffn2, 2, m_w2_down, v_w2_down)

    names = ["w_ada", "b_ada", "g_ffn1", "w1_gate", "w1_up", "w1_down", "g_mix", "w_in", "g_sb_out", "g_dil_out",
             "w_out", "rel_bias", "g_ffn2", "w2_gate", "w2_up", "w2_down", "g_final"]
    outs = [loss, r["grad_x"]]
    for k, small in enumerate((sg, sd, sm, sv)):
        for name in names:
            outs.append(big[name][k] if name in big else small[name])
    return tuple(outs)
```

```python
import functools
import math

import numpy as np
import jax
import jax.numpy as jnp
from jax import lax
from jax.experimental import pallas as pl
from jax.experimental.pallas import tpu as pltpu

F32 = jnp.float32
BF = jnp.bfloat16
MESH = pl.DeviceIdType.MESH

HEAD_DIM = 64
N_HEADS = 8
D_GRP = N_HEADS * HEAD_DIM
DIL_CONFIGS = ((128, 1), (512, 4), (2048, 16))
N_STEPS = 128
BLOCK = 128
N_BUCKETS = 32
MAX_DISTANCE = 2048
N_MOD = 9
EPS = 1e-6
NEG_INF = -1e30
SCALE = HEAD_DIM ** -0.5

ADAM_LR = 0.001
ADAM_B1 = 0.9
ADAM_B2 = 0.999
ADAM_EPS = 1e-08
ADAM_WD = 0.01
ADAM_STEP = 10

N_CHIPS = 4
N_DEV = 8
VMEM_LIMIT = 56 * 1024 * 1024
TM = 512
TQ = 256
KB = 256
SMALL_ROWS = 120


def _cp(n_axes=0, **kw):
    sem = ("arbitrary",) * n_axes if n_axes else None
    return pltpu.CompilerParams(dimension_semantics=sem, vmem_limit_bytes=VMEM_LIMIT, **kw)


def _nn(a, b):
    return jnp.dot(a, b, preferred_element_type=F32)


def _nt(a, b):
    return lax.dot_general(a, b, (((1,), (1,)), ((), ())), preferred_element_type=F32)


def _tn(a, b):
    return lax.dot_general(a, b, (((0,), (0,)), ((), ())), preferred_element_type=F32)


def _nn2(x, m):
    hi = x.astype(BF)
    lo = (x - hi.astype(F32)).astype(BF)
    return _nn(hi, m) + _nn(lo, m)


def _softplus(z):
    return jnp.maximum(z, 0.0) + jnp.log1p(jnp.exp(-jnp.abs(z)))


def _sds(shape, dtype):
    return jax.ShapeDtypeStruct(shape, dtype)


def _whole(a):
    nd = a.ndim
    return pl.BlockSpec(a.shape, lambda *_: (0,) * nd, pipeline_mode=pl.Buffered(1))


def _modnorm(x, g, sc, sh, seq):
    t, d = x.shape
    per = seq // TM

    def body(x_ref, g_ref, sc_ref, sh_ref, h_ref):
        xv = x_ref[...]
        r = lax.rsqrt(jnp.mean(xv * xv, axis=-1, keepdims=True) + EPS)
        h_ref[...] = (((xv * r) * g_ref[...]) * (1.0 + sc_ref[...]) + sh_ref[...]).astype(BF)

    return pl.pallas_call(
        body, name="modnorm", grid=(t // TM,),
        in_specs=[pl.BlockSpec((TM, d), lambda m: (m, 0)),
                  pl.BlockSpec((1, d), lambda m: (0, 0)),
                  pl.BlockSpec((None, 1, d), lambda m: (m // per, 0, 0)),
                  pl.BlockSpec((None, 1, d), lambda m: (m // per, 0, 0))],
        out_specs=pl.BlockSpec((TM, d), lambda m: (m, 0)),
        out_shape=_sds((t, d), BF), compiler_params=_cp(1))(x, g, sc, sh)


def _modnorm_bwd_tile(dh, xv, gv, scv, dxo):
    r = lax.rsqrt(jnp.mean(xv * xv, axis=-1, keepdims=True) + EPS)
    n = xv * r
    ng = n * gv
    dsh = jnp.sum(dh, axis=0, keepdims=True)
    dsc = jnp.sum(dh * ng, axis=0, keepdims=True)
    dy = dh * (1.0 + scv)
    dg = jnp.sum(dy * n, axis=0, keepdims=True)
    dn = dy * gv
    dx = dxo + r * (dn - n * jnp.mean(dn * n, axis=-1, keepdims=True))
    return dx, dsh, dsc, dg


def _acc_rows(ref, val, first):
    @pl.when(first)
    def _():
        ref[...] = val

    @pl.when(jnp.logical_not(first))
    def _():
        ref[...] += val


def _ffn_up(h, wgu, carry=None):
    t, d = h.shape
    fs = wgu.shape[-1]

    def body(h_ref, w_ref, p_ref, q_ref, s_ref):
        hv = h_ref[...]
        a = _nn(hv, w_ref[0])
        u = _nn(hv, w_ref[1])
        sig = jax.nn.sigmoid(a)
        q = a * sig
        p_ref[...] = (u * (sig * (1.0 + a * (1.0 - sig)))).astype(BF)
        q_ref[...] = q.astype(BF)
        s_ref[...] = (q * u).astype(BF)

    blk = pl.BlockSpec((None, TM, fs), lambda j, m: (j, m, 0))
    return _call(
        body, "ffn_up", (N_CHIPS, t // TM),
        [pl.BlockSpec((TM, d), lambda j, m: (m, 0)),
         pl.BlockSpec((None, 2, d, fs), lambda j, m: (j, 0, 0, 0))],
        [blk, blk, blk],
        [_sds((N_CHIPS, t, fs), BF)] * 3,
        (h, wgu), carry=carry)


def _ffn_down(s, wd, x, gt, seq, coef):
    _, t, fs = s.shape
    d = x.shape[-1]
    per = seq // TM

    def body(s_ref, w_ref, x_ref, gt_ref, f_ref, xo_ref):
        f = _nn(s_ref[0], w_ref[0, 0])
        for j in range(1, N_CHIPS):
            f = f + _nn(s_ref[j], w_ref[j, 0])
        f_ref[...] = f
        xo_ref[...] = x_ref[...] + (coef * gt_ref[...]) * f

    row = pl.BlockSpec((TM, d), lambda m: (m, 0))
    return pl.pallas_call(
        body, name="ffn_down", grid=(t // TM,),
        in_specs=[pl.BlockSpec((N_CHIPS, TM, fs), lambda m: (0, m, 0)),
                  _whole(wd), row,
                  pl.BlockSpec((None, 1, d), lambda m: (m // per, 0, 0))],
        out_specs=[row, row],
        out_shape=[_sds((t, d), F32), _sds((t, d), F32)],
        compiler_params=_cp(1))(s, wd, x, gt)


def _ffn_bwd_ds(dxo, gt, f, wd, p, q, seq, coef, carry=None):
    t, d = dxo.shape
    fs = p.shape[-1]
    per = seq // TM
    nb = t // seq

    def body(dxo_ref, gt_ref, f_ref, w_ref, p_ref, q_ref, da_ref, du_ref, df_ref, dgt_ref):
        m = pl.program_id(0)
        dxv = dxo_ref[...]
        df = ((coef * gt_ref[...]) * dxv).astype(BF)
        df_ref[...] = df
        _acc_rows(dgt_ref, coef * jnp.sum(dxv * f_ref[...], axis=0, keepdims=True), m % per == 0)
        for j in range(N_CHIPS):
            ds = _nt(df, w_ref[j, 0])
            da_ref[j] = (ds * p_ref[j].astype(F32)).astype(BF)
            du_ref[j] = (ds * q_ref[j].astype(F32)).astype(BF)

    row = pl.BlockSpec((TM, d), lambda m: (m, 0))
    blk = pl.BlockSpec((N_CHIPS, TM, fs), lambda m: (0, m, 0))
    ex = pl.BlockSpec((None, 1, d), lambda m: (m // per, 0, 0))
    return _call(
        body, "ffn_bwd_ds", (t // TM,),
        [row, ex, row, _whole(wd), blk, blk],
        [blk, blk, row, ex],
        [_sds((N_CHIPS, t, fs), BF), _sds((N_CHIPS, t, fs), BF), _sds((t, d), BF), _sds((nb, 1, d), F32)],
        (dxo, gt, f, wd, p, q), carry=carry)


TK_W = 1024


def _ffn_bwd_w(h, da, du, s, df):
    t, d = h.shape
    fs = da.shape[-1]

    def body(h_ref, da_ref, du_ref, s_ref, df_ref, o_ref):
        kt = pl.program_id(1)
        hv = h_ref[...]
        parts = (_tn(da_ref[...], hv), _tn(du_ref[...], hv), _tn(s_ref[...], df_ref[...]))

        @pl.when(kt == 0)
        def _():
            for i, p in enumerate(parts):
                o_ref[i] = p

        @pl.when(kt != 0)
        def _():
            for i, p in enumerate(parts):
                o_ref[i] += p

    row = pl.BlockSpec((TK_W, d), lambda j, kt: (kt, 0))
    blk = pl.BlockSpec((None, TK_W, fs), lambda j, kt: (j, kt, 0))
    return pl.pallas_call(
        body, name="ffn_bwd_w", grid=(N_CHIPS, t // TK_W),
        in_specs=[row, blk, blk, blk, row],
        out_specs=pl.BlockSpec((None, 3, fs, d), lambda j, kt: (j, 0, 0, 0)),
        out_shape=_sds((N_CHIPS, 3, fs, d), F32),
        compiler_params=_cp(2))(h, da, du, s, df)


def _ffn_bwd_dh(da, du, wgu, x, g, sc, dxo, seq, carry=None):
    _, t, fs = da.shape
    d = x.shape[-1]
    per = seq // TM
    nb = t // seq

    def body(da_ref, du_ref, w_ref, x_ref, g_ref, sc_ref, dxo_ref, dx_ref, dsh_ref, dsc_ref, dg_ref):
        m = pl.program_id(0)
        dh = _nt(da_ref[0], w_ref[0, 0]) + _nt(du_ref[0], w_ref[0, 1])
        for j in range(1, N_CHIPS):
            dh = dh + _nt(da_ref[j], w_ref[j, 0]) + _nt(du_ref[j], w_ref[j, 1])
        dx, dsh, dsc, dg = _modnorm_bwd_tile(dh, x_ref[...], g_ref[...], sc_ref[...], dxo_ref[...])
        dx_ref[...] = dx
        _acc_rows(dsh_ref, dsh, m % per == 0)
        _acc_rows(dsc_ref, dsc, m % per == 0)
        _acc_rows(dg_ref, dg, m == 0)

    row = pl.BlockSpec((TM, d), lambda m: (m, 0))
    blk = pl.BlockSpec((N_CHIPS, TM, fs), lambda m: (0, m, 0))
    ex = pl.BlockSpec((None, 1, d), lambda m: (m // per, 0, 0))
    vec = pl.BlockSpec((1, d), lambda m: (0, 0))
    return _call(
        body, "ffn_bwd_dh", (t // TM,),
        [blk, blk, _whole(wgu), row, vec, ex, row],
        [row, ex, ex, vec],
        [_sds((t, d), F32), _sds((nb, 1, d), F32), _sds((nb, 1, d), F32), _sds((1, d), F32)],
        (da, du, wgu, x, g, sc, dxo), carry=carry)


def _qkv_proj(h, w_in, carry=None):
    t, d = h.shape
    wc = w_in.shape[-1]

    dils = [dil for _, dil in DIL_CONFIGS if dil > 1]

    def body(h_ref, w_ref, o_ref, *rest):
        res_refs, buf = rest[:len(dils)], rest[len(dils)]
        hv = h_ref[...]
        for j in range(N_CHIPS):
            rf = _nn(hv, w_ref[j, 0])
            r = rf.astype(BF)
            for a, lc, off, width in _col_pieces(j, wc):
                o_ref[a, :, lc:lc + width] = r[:, off:off + width]
                if a < 3:
                    continue
                for c0 in range(0, width, 128):
                    cg = (lc + c0) // 128
                    buf[...] = rf[:, off + c0:off + c0 + 128]
                    for ref, dil in zip(res_refs, dils):
                        for rr in range(dil):
                            ref[a - 3, :, rr * D_GRP + cg * 128:rr * D_GRP + (cg + 1) * 128] = (
                                buf[pl.ds(rr, TM // dil, stride=dil), :].astype(BF))

    return _call(
        body, "qkv_proj", (t // TM,),
        [pl.BlockSpec((TM, d), lambda m: (m, 0)), _whole(w_in)],
        [pl.BlockSpec((6, TM, D_GRP), lambda m: (0, m, 0))]
        + [pl.BlockSpec((3, TM // dil, dil * D_GRP), lambda m: (0, m, 0)) for dil in dils],
        [_sds((6, t, D_GRP), BF)] + [_sds((3, t // dil, dil * D_GRP), BF) for dil in dils],
        (h, w_in), scratch=[pltpu.VMEM((TM, 128), F32)], carry=carry)


def _col_pieces(j, wc):
    out, off = [], 0
    while off < wc:
        a, lc = divmod(j * wc + off, D_GRP)
        width = min(D_GRP - lc, wc - off)
        out.append((a, lc, off, width))
        off += width
    return out


def _chip_cols(g6_ref, j, wc):
    return jnp.concatenate([g6_ref[a, :, lc:lc + width] for a, lc, _, width in _col_pieces(j, wc)], axis=1)


def _mix_out(on_sb, on_dil, w_out, x, gt, seq):
    t, d = x.shape
    per = seq // TM

    def body(a_ref, b_ref, w_ref, x_ref, gt_ref, t_ref, xo_ref):
        tv = _nn(a_ref[...], w_ref[0:D_GRP, :]) + _nn(b_ref[...], w_ref[D_GRP:2 * D_GRP, :])
        t_ref[...] = tv
        xo_ref[...] = x_ref[...] + gt_ref[...] * tv

    row = pl.BlockSpec((TM, d), lambda m: (m, 0))
    half = pl.BlockSpec((TM, D_GRP), lambda m: (m, 0))
    return pl.pallas_call(
        body, name="mix_out", grid=(t // TM,),
        in_specs=[half, half, pl.BlockSpec((2 * D_GRP, d), lambda m: (0, 0)), row,
                  pl.BlockSpec((None, 1, d), lambda m: (m // per, 0, 0))],
        out_specs=[row, row],
        out_shape=[_sds((t, d), F32), _sds((t, d), F32)],
        compiler_params=_cp(1))(on_sb, on_dil, w_out, x, gt)


def _sb_masks():
    lane = lax.broadcasted_iota(jnp.int32, (1, 2 * HEAD_DIM), 1)
    hm0 = lane < HEAD_DIM
    rel = lax.broadcasted_iota(jnp.int32, (TQ, KB), 0) - lax.broadcasted_iota(jnp.int32, (TQ, KB), 1)
    kr = lax.broadcasted_iota(jnp.int32, (KB, KB), 0)
    kc = lax.broadcasted_iota(jnp.int32, (KB, KB), 1)
    return hm0, rel, kr, kc


def _headnorm_pair(o, gv, hm0):
    o2 = o * o
    ms0 = jnp.sum(jnp.where(hm0, o2, 0.0), axis=-1, keepdims=True) * (1.0 / HEAD_DIM)
    ms1 = jnp.sum(jnp.where(hm0, 0.0, o2), axis=-1, keepdims=True) * (1.0 / HEAD_DIM)
    r = jnp.where(hm0, lax.rsqrt(ms0 + EPS), lax.rsqrt(ms1 + EPS))
    return (o * r) * gv


SB_DEAD = -104.0


def _alive(c_l):
    return (jnp.max(c_l) > SB_DEAD).astype(jnp.int32)


def _sb_fwd(qkv6, g_sb, nb, seq, carry=None):
    nq = seq // TQ

    def body(q_ref, k_ref, v_ref, g_ref, o_ref, on_ref):
        qi = pl.program_id(2)
        hm0, rel, kr, kc = _sb_masks()
        upper = (kr > kc).astype(BF)
        qv = q_ref[...]
        qhs = [jnp.where(hm0, qv, jnp.zeros_like(qv)), jnp.where(hm0, jnp.zeros_like(qv), qv)]

        def block(kj, causal, c_ls, accs):
            ks = pl.multiple_of(kj * KB, KB)
            kb = k_ref[pl.ds(ks, KB), :]
            vb = v_ref[pl.ds(ks, KB), :]
            new_c, new_acc = [], []
            for qh, c_l, acc in zip(qhs, c_ls, accs):
                z = _nt(qh, kb) * SCALE
                sp = _softplus(z)
                ln = -sp if causal is None else jnp.where(causal, -sp, 0.0)
                suf = _nn2(ln, upper)
                w = jnp.exp((z - sp) + (suf + c_l))
                if causal is not None:
                    w = jnp.where(causal, w, 0.0)
                new_acc.append(acc + _nn(w.astype(BF), vb))
                new_c.append(c_l + (suf[:, 0:1] + ln[:, 0:1]))
            return new_c, new_acc

        zc = jnp.zeros((TQ, 1), F32)
        za = jnp.zeros((TQ, 2 * HEAD_DIM), F32)
        c_ls, accs = block(qi, rel > 0, [zc, zc], [za, za])

        def cond(carry):
            return jnp.logical_and(carry[0] <= qi, carry[1] > 0)

        def kbody(carry):
            it, _, c0, c1, a0, a1 = carry
            (c0, c1), (a0, a1) = block(qi - it, None, [c0, c1], [a0, a1])
            return it + 1, jnp.maximum(_alive(c0), _alive(c1)), c0, c1, a0, a1

        init = (jnp.int32(1), jnp.maximum(_alive(c_ls[0]), _alive(c_ls[1])), c_ls[0], c_ls[1], accs[0], accs[1])
        outs = lax.while_loop(cond, kbody, init)[4:]
        o = jnp.where(hm0, outs[0], outs[1])
        o_ref[...] = o
        on_ref[...] = _headnorm_pair(o, g_ref[...], hm0).astype(BF)

    w = 2 * HEAD_DIM
    full = lambda i: pl.BlockSpec((None, None, seq, w), lambda b, hp, q: (i, b, 0, hp))
    qblk = pl.BlockSpec((None, None, TQ, w), lambda b, hp, q: (0, b, q, hp))
    oblk = pl.BlockSpec((None, TQ, w), lambda b, hp, q: (b, q, hp))
    return _call(
        body, "sb_fwd", (nb, N_HEADS // 2, nq),
        [qblk, full(1), full(2), pl.BlockSpec((1, w), lambda b, hp, q: (0, hp))],
        [oblk, oblk],
        [_sds((nb, seq, D_GRP), F32), _sds((nb, seq, D_GRP), BF)],
        (qkv6, qkv6, qkv6, g_sb), carry=carry)


def _sb_bwd(qkv6, do, nb, seq, carry=None):
    nq = seq // TQ
    nk = seq // KB

    def body(q_ref, k_ref, v_ref, do_ref, out_ref, dk_acc, dv_acc, g_st, s_st):
        qi = pl.program_id(2)
        hm0, rel, kr, kc = _sb_masks()
        upper = (kr > kc).astype(BF)
        lower = (kr < kc).astype(BF)

        @pl.when(qi == 0)
        def _():
            dk_acc[...] = jnp.zeros_like(dk_acc)
            dv_acc[...] = jnp.zeros_like(dv_acc)

        qv = q_ref[...]
        dov = do_ref[...]
        qhs = [jnp.where(hm0, qv, jnp.zeros_like(qv)), jnp.where(hm0, jnp.zeros_like(qv), qv)]
        dohs = [jnp.where(hm0, dov, 0.0).astype(BF), jnp.where(hm0, 0.0, dov).astype(BF)]

        def weights(kj, causal, c_ls):
            ks = pl.multiple_of(kj * KB, KB)
            kb = k_ref[pl.ds(ks, KB), :]
            vb = v_ref[pl.ds(ks, KB), :]
            new_c, dv = [], None
            for hh, (qh, doh, c_l) in enumerate(zip(qhs, dohs, c_ls)):
                z = _nt(qh, kb) * SCALE
                sp = _softplus(z)
                ln = -sp if causal is None else jnp.where(causal, -sp, 0.0)
                suf = _nn2(ln, upper)
                lsz = z - sp
                w = jnp.exp(lsz + (suf + c_l))
                if causal is not None:
                    w = jnp.where(causal, w, 0.0)
                g_st[hh, kj] = w * _nt(doh, vb)
                s_st[hh, kj] = jnp.exp(lsz)
                part = _tn(w.astype(BF), doh)
                dv = part if dv is None else dv + part
                new_c.append(c_l + (suf[:, 0:1] + ln[:, 0:1]))
            dv_acc[pl.ds(ks, KB), :] += dv
            return new_c

        zc = jnp.zeros((TQ, 1), F32)
        c_ls = weights(qi, rel > 0, [zc, zc])

        def acond(carry):
            return jnp.logical_and(carry[0] <= qi, carry[1] > 0)

        def abody(carry):
            it, _, c0, c1 = carry
            c0, c1 = weights(qi - it, None, [c0, c1])
            return it + 1, jnp.maximum(_alive(c0), _alive(c1)), c0, c1

        n_used = lax.while_loop(
            acond, abody, (jnp.int32(1), jnp.maximum(_alive(c_ls[0]), _alive(c_ls[1])), c_ls[0], c_ls[1]))[0]

        def grads(kj, causal, c_gs, dqs):
            ks = pl.multiple_of(kj * KB, KB)
            kb = k_ref[pl.ds(ks, KB), :]
            new_c, new_dq, dk = [], [], None
            for hh, (qh, c_g, dq) in enumerate(zip(qhs, c_gs, dqs)):
                g = g_st[hh, kj]
                sig = s_st[hh, kj]
                pre = _nn2(g, lower)
                dz = g * (1.0 - sig) - sig * (pre + c_g)
                if causal is not None:
                    dz = jnp.where(causal, dz, 0.0)
                dzb = (dz * SCALE).astype(BF)
                part = _tn(dzb, qh)
                dk = part if dk is None else dk + part
                new_dq.append(dq + _nn(dzb, kb))
                new_c.append(c_g + (pre[:, KB - 1:KB] + g[:, KB - 1:KB]))
            dk_acc[pl.ds(ks, KB), :] += dk
            return new_c, new_dq

        za = jnp.zeros((TQ, 2 * HEAD_DIM), F32)

        def bbody(kj, carry):
            (c0, c1), (d0, d1) = grads(kj, None, carry[:2], carry[2:])
            return c0, c1, d0, d1

        c0, c1, d0, d1 = lax.fori_loop(qi - n_used + 1, qi, bbody, (zc, zc, za, za))
        _, dqs = grads(qi, rel > 0, [c0, c1], [d0, d1])
        dq = jnp.where(hm0, dqs[0], dqs[1])
        out_ref[0, pl.ds(pl.multiple_of(qi * TQ, TQ), TQ), :] = dq.astype(BF)

        @pl.when(qi == nq - 1)
        def _():
            out_ref[1] = dk_acc[...].astype(BF)
            out_ref[2] = dv_acc[...].astype(BF)

    w = 2 * HEAD_DIM
    full = lambda i: pl.BlockSpec((None, None, seq, w), lambda b, hp, q: (i, b, 0, hp))
    qblk = pl.BlockSpec((None, None, TQ, w), lambda b, hp, q: (0, b, q, hp))
    oblk = pl.BlockSpec((None, TQ, w), lambda b, hp, q: (b, q, hp))
    return _call(
        body, "sb_bwd", (nb, N_HEADS // 2, nq),
        [qblk, full(1), full(2), oblk],
        [pl.BlockSpec((3, None, seq, w), lambda b, hp, q: (0, b, 0, hp))],
        [_sds((6, nb, seq, D_GRP), BF)], (qkv6, qkv6, qkv6, do),
        scratch=[pltpu.VMEM((seq, w), F32), pltpu.VMEM((seq, w), F32),
                 pltpu.VMEM((2, nk, TQ, KB), F32), pltpu.VMEM((2, nk, TQ, KB), F32)],
        carry=carry)


def _t5_bucket(n):
    max_exact = N_BUCKETS // 2
    nf = np.maximum(n, 1).astype(np.float32)
    large = max_exact + (np.log(nf / max_exact) / math.log(MAX_DISTANCE / max_exact)
                         * (N_BUCKETS - max_exact)).astype(np.int32)
    large = np.minimum(large, N_BUCKETS - 1)
    return np.where(n < max_exact, n, large).astype(np.int32)


def _bucket_map(dilation):
    step = BLOCK + np.arange(BLOCK)[:, None] - np.arange(2 * BLOCK)[None, :]
    return _t5_bucket(np.clip(step, 0, N_STEPS) * dilation)


GRP_HEADS = 4
GRP_W = GRP_HEADS * HEAD_DIM


def _dil_masks():
    lane = lax.broadcasted_iota(jnp.int32, (1, GRP_W), 1)
    heads = [jnp.logical_and(lane >= HEAD_DIM * i, lane < HEAD_DIM * (i + 1)) for i in range(GRP_HEADS)]
    iq = lax.broadcasted_iota(jnp.int32, (BLOCK, 2 * BLOCK), 0)
    ik = lax.broadcasted_iota(jnp.int32, (BLOCK, 2 * BLOCK), 1)
    in_cur = jnp.logical_and(ik >= BLOCK, ik - BLOCK <= iq)
    in_prev = jnp.logical_and(ik < BLOCK, ik >= iq)
    return heads, in_cur, in_prev


def _dil_probs(qh, kcat, b_ref, hh, valid):
    z = jnp.where(valid, _nt(qh, kcat) * SCALE + b_ref[hh], NEG_INF)
    m = jnp.max(z, axis=-1, keepdims=True)
    e = jnp.exp(z - m)
    return e, jnp.sum(e, axis=-1, keepdims=True), m


def _dil_pad_copy(dst, src):
    dst[0:BLOCK, :] = jnp.zeros((BLOCK, dst.shape[1]), dst.dtype)
    dst[BLOCK:, :] = src[...]


def _dil_fwd(qkv6r, base, bias, nb, sub_len, dilation):
    n_blk = sub_len // BLOCK

    def body(q_ref, k_ref, v_ref, b_ref, o_ref, l_ref, kpad, vpad):
        heads, valid_c, valid_p0 = _dil_masks()
        _dil_pad_copy(kpad, k_ref)
        _dil_pad_copy(vpad, v_ref)

        def nbody(n, carry):
            cur = pl.ds(pl.multiple_of(n * BLOCK, BLOCK), BLOCK)
            both = pl.ds(pl.multiple_of(n * BLOCK, BLOCK), 2 * BLOCK)
            valid = jnp.logical_or(valid_c, jnp.logical_and(valid_p0, n > 0))
            for gi in range(N_HEADS // GRP_HEADS):
                lanes = slice(gi * GRP_W, (gi + 1) * GRP_W)
                qv, kcat, vcat = q_ref[cur, lanes], kpad[both, lanes], vpad[both, lanes]
                o = jnp.zeros((BLOCK, GRP_W), F32)
                lse = jnp.zeros((BLOCK, GRP_W), F32)
                for i, hm in enumerate(heads):
                    qh = jnp.where(hm, qv, jnp.zeros_like(qv))
                    e, den, m = _dil_probs(qh, kcat, b_ref, gi * GRP_HEADS + i, valid)
                    o = jnp.where(hm, _nn(e.astype(BF), vcat) / den, o)
                    lse = jnp.where(hm, m + jnp.log(den), lse)
                o_ref[cur, lanes] = o
                l_ref[cur, lanes] = lse
            return carry

        lax.fori_loop(0, n_blk, nbody, 0)

    seqblk = lambda i: pl.BlockSpec((None, None, sub_len, D_GRP), lambda b, r: (i, b, 0, r))
    oblk = pl.BlockSpec((None, sub_len, D_GRP), lambda b, r: (b, 0, r))
    shp = _sds((nb, sub_len, dilation * D_GRP), F32)
    return pl.pallas_call(
        body, name="dil_fwd_%d" % dilation, grid=(nb, dilation),
        in_specs=[seqblk(base), seqblk(base + 1), seqblk(base + 2), _whole(bias)],
        out_specs=[oblk, oblk], out_shape=[shp, shp],
        scratch_shapes=[pltpu.VMEM((sub_len + BLOCK, D_GRP), BF)] * 2,
        compiler_params=_cp(2))(qkv6r, qkv6r, qkv6r, bias)


def _dil_bwd(qkv6r, base, bias, do_c, dd_c, nb, sub_len, dilation, carry=None):
    n_blk = sub_len // BLOCK
    n_g = 2 if sub_len > 4 * BLOCK else 1
    wid = D_GRP // n_g
    hd_g = N_HEADS // n_g

    def body(q_ref, k_ref, v_ref, b_ref, do_ref, dd_ref, out_ref, a_ref, kpad, vpad, dkpad, dvpad):
        heads, valid_c, valid_p0 = _dil_masks()
        first = jnp.logical_and(pl.program_id(1) == 0, pl.program_id(2) == 0)

        @pl.when(first)
        def _():
            a_ref[...] = jnp.zeros_like(a_ref)

        _dil_pad_copy(kpad, k_ref)
        _dil_pad_copy(vpad, v_ref)
        dkpad[...] = jnp.zeros_like(dkpad)
        dvpad[...] = jnp.zeros_like(dvpad)

        def nbody(n, carry):
            cur = pl.ds(pl.multiple_of(n * BLOCK, BLOCK), BLOCK)
            both = pl.ds(pl.multiple_of(n * BLOCK, BLOCK), 2 * BLOCK)
            valid = jnp.logical_or(valid_c, jnp.logical_and(valid_p0, n > 0))
            for gi in range(hd_g // GRP_HEADS):
                lanes = slice(gi * GRP_W, (gi + 1) * GRP_W)
                qv, kcat, vcat = q_ref[cur, lanes], kpad[both, lanes], vpad[both, lanes]
                dov, ddv = do_ref[cur, lanes], dd_ref[cur, lanes]
                dq = jnp.zeros((BLOCK, GRP_W), F32)
                dk = jnp.zeros((2 * BLOCK, GRP_W), F32)
                dv = jnp.zeros((2 * BLOCK, GRP_W), F32)
                for i, hm in enumerate(heads):
                    h = gi * GRP_HEADS + i
                    qh = jnp.where(hm, qv, jnp.zeros_like(qv))
                    doh = jnp.where(hm, dov, 0.0).astype(BF)
                    ddh = jnp.sum(jnp.where(hm, ddv, 0.0), axis=-1, keepdims=True) * (1.0 / HEAD_DIM)
                    e, den, _ = _dil_probs(qh, kcat, b_ref, h, valid)
                    p = e * (1.0 / den)
                    dz = p * (_nt(doh, vcat) + ddh)
                    a_ref[h] += dz
                    dzb = (dz * SCALE).astype(BF)
                    dq = jnp.where(hm, _nn(dzb, kcat), dq)
                    dk = dk + _tn(dzb, qh)
                    dv = dv + _tn(p.astype(BF), doh)
                out_ref[0, cur, lanes] = dq
                dkpad[both, lanes] += dk
                dvpad[both, lanes] += dv
            return carry

        lax.fori_loop(0, n_blk, nbody, 0)
        out_ref[1] = dkpad[BLOCK:, :]
        out_ref[2] = dvpad[BLOCK:, :]

    col = lambda g, r: r * n_g + g
    seqblk = lambda i: pl.BlockSpec((None, None, sub_len, wid), lambda g, b, r: (i, b, 0, col(g, r)))
    oblk = pl.BlockSpec((None, sub_len, wid), lambda g, b, r: (b, 0, col(g, r)))
    hblk = pl.BlockSpec((hd_g, BLOCK, 2 * BLOCK), lambda g, b, r: (g, 0, 0))
    return _call(
        body, "dil_bwd_%d" % dilation, (n_g, nb, dilation),
        [seqblk(base), seqblk(base + 1), seqblk(base + 2), hblk, oblk, oblk],
        [pl.BlockSpec((3, None, sub_len, wid), lambda g, b, r: (0, b, 0, col(g, r))), hblk],
        [_sds((3, nb, sub_len, dilation * D_GRP), F32), _sds((N_HEADS, BLOCK, 2 * BLOCK), F32)],
        (qkv6r, qkv6r, qkv6r, bias, do_c, dd_c),
        scratch=[pltpu.VMEM((sub_len + BLOCK, wid), BF)] * 2 + [pltpu.VMEM((sub_len + BLOCK, wid), F32)] * 2,
        carry=carry)


def _group_ones():
    idx = np.arange(D_GRP) // HEAD_DIM
    return jnp.asarray((idx[:, None] == idx[None, :]).astype(np.float32), dtype=BF)


def _dil_alphas(l1, l4, l16):
    mx = jnp.maximum(jnp.maximum(l1, l4), l16)
    e1 = jnp.exp(l1 - mx)
    e4 = jnp.exp(l4 - mx)
    e16 = jnp.exp(l16 - mx)
    den = e1 + e4 + e16
    return e1 / den, e4 / den, e16 / den


def _residue_spec(dil):
    return pl.BlockSpec((TM // dil, dil * D_GRP), lambda m: (m, 0))


def _from_residue(src, dil, cg, buf):
    if dil == 1:
        return src[:, cg * 128:(cg + 1) * 128]
    for r in range(dil):
        buf[pl.ds(r, TM // dil, stride=dil), :] = src[:, r * D_GRP + cg * 128:r * D_GRP + (cg + 1) * 128]
    return buf[...]


def _to_residue(dst, dil, cg, buf, val):
    if dil == 1:
        dst[:, cg * 128:(cg + 1) * 128] = val
        return
    buf[...] = val
    for r in range(dil):
        dst[:, r * D_GRP + cg * 128:r * D_GRP + (cg + 1) * 128] = buf[pl.ds(r, TM // dil, stride=dil), :]


def _pair_sum(x, hm0):
    s0 = jnp.sum(jnp.where(hm0, x, 0.0), axis=-1, keepdims=True)
    s1 = jnp.sum(jnp.where(hm0, 0.0, x), axis=-1, keepdims=True)
    return jnp.where(hm0, s0, s1)


def _dil_comb(os, ls, g_dil):
    t = os[0].shape[0]
    dils = [dil for _, dil in DIL_CONFIGS]

    def body(o1, l1, o4, l4, o16, l16, g_ref, o_ref, on_ref, b0, b1, b2, b3):
        hm0 = lax.broadcasted_iota(jnp.int32, (1, 128), 1) < HEAD_DIM
        for cg in range(D_GRP // 128):
            lanes = slice(cg * 128, (cg + 1) * 128)
            ov = [_from_residue(src, dil, cg, buf) for src, dil, buf in zip((o1, o4, o16), dils, (None, b0, b1))]
            lv = [_from_residue(src, dil, cg, buf) for src, dil, buf in zip((l1, l4, l16), dils, (None, b2, b3))]
            a1, a4, a16 = _dil_alphas(*lv)
            o = a1 * ov[0] + a4 * ov[1] + a16 * ov[2]
            o_ref[:, lanes] = o
            on_ref[:, lanes] = _headnorm_pair(o, g_ref[:, lanes], hm0).astype(BF)

    blk = pl.BlockSpec((TM, D_GRP), lambda m: (m, 0))
    specs = [_residue_spec(dil) for dil in dils for _ in range(2)]
    return pl.pallas_call(
        body, name="dil_comb", grid=(t // TM,),
        in_specs=specs + [pl.BlockSpec((1, D_GRP), lambda m: (0, 0))],
        out_specs=[blk, blk],
        out_shape=[_sds((t, D_GRP), F32), _sds((t, D_GRP), BF)],
        scratch_shapes=[pltpu.VMEM((TM, 128), F32)] * 4,
        compiler_params=_cp(1))(os[0], ls[0], os[1], ls[1], os[2], ls[2], g_dil)


def _dil_comb_bwd(do, os, ls):
    t = do.shape[0]
    dils = [dil for _, dil in DIL_CONFIGS]

    def body(do_ref, o1, l1, o4, l4, o16, l16, d1, d4, d16, e1, e4, e16, b0, b1, b2, b3):
        hm0 = lax.broadcasted_iota(jnp.int32, (1, 128), 1) < HEAD_DIM
        for cg in range(D_GRP // 128):
            dov = do_ref[:, cg * 128:(cg + 1) * 128]
            ov = [_from_residue(src, dil, cg, buf) for src, dil, buf in zip((o1, o4, o16), dils, (None, b0, b1))]
            lv = [_from_residue(src, dil, cg, buf) for src, dil, buf in zip((l1, l4, l16), dils, (None, b2, b3))]
            al = _dil_alphas(*lv)
            sbar = al[0] * _pair_sum(dov * ov[0], hm0)
            for a_c, o_c in zip(al[1:], ov[1:]):
                sbar = sbar + a_c * _pair_sum(dov * o_c, hm0)
            for a_c, dil, dref, eref in zip(al, dils, (d1, d4, d16), (e1, e4, e16)):
                _to_residue(dref, dil, cg, b0, a_c * dov)
                _to_residue(eref, dil, cg, b1, -a_c * sbar)

    specs = [_residue_spec(dil) for dil in dils]
    return pl.pallas_call(
        body, name="dil_comb_bwd", grid=(t // TM,),
        in_specs=[pl.BlockSpec((TM, D_GRP), lambda m: (m, 0))] + [sp for sp in specs for _ in range(2)],
        out_specs=specs + specs,
        out_shape=[_sds((t // dil, dil * D_GRP), F32) for dil in dils] * 2,
        scratch_shapes=[pltpu.VMEM((TM, 128), F32)] * 4,
        compiler_params=_cp(1))(do, os[0], ls[0], os[1], ls[1], os[2], ls[2])


def _dqkv_dil_sum(ds, dqkv6):
    t = dqkv6.shape[1]
    dils = [dil for _, dil in DIL_CONFIGS]

    def body(*refs):
        srcs, o_ref, acc = refs[:len(dils)], refs[len(dils) + 1], refs[len(dils) + 2]
        for a in range(3):
            for cg in range(D_GRP // 128):
                for src, dil in zip(srcs, dils):
                    for r in range(dil):
                        part = src[a, :, r * D_GRP + cg * 128:r * D_GRP + (cg + 1) * 128]
                        rows = pl.ds(r, TM // dil, stride=dil) if dil > 1 else slice(None)
                        if dil == dils[0]:
                            acc[rows, :] = part
                        else:
                            acc[rows, :] += part
                o_ref[a, :, cg * 128:(cg + 1) * 128] = acc[...].astype(BF)

    return pl.pallas_call(
        body, name="dqkv_dil_sum", grid=(t // TM,),
        in_specs=[pl.BlockSpec((3, TM // dil, dil * D_GRP), lambda m: (0, m, 0)) for dil in dils]
        + [pl.BlockSpec(memory_space=pl.ANY)],
        out_specs=pl.BlockSpec((3, TM, D_GRP), lambda m: (1, m, 0)),
        out_shape=_sds((6, t, D_GRP), BF), input_output_aliases={len(dils): 0},
        scratch_shapes=[pltpu.VMEM((TM, 128), F32)],
        compiler_params=_cp(1))(*ds, dqkv6)


def _relbias_grad(a_all, onehot):
    def body(a_ref, oh_ref, o_ref):
        acc = jnp.zeros((N_HEADS, N_BUCKETS), F32)
        for c in range(len(DIL_CONFIGS)):
            av = a_ref[c]
            hi = av.astype(BF)
            lo = (av - hi.astype(F32)).astype(BF)
            acc = acc + _nt(hi, oh_ref[c]) + _nt(lo, oh_ref[c])
        o_ref[...] = acc

    return pl.pallas_call(body, name="relbias_grad", out_shape=_sds((N_HEADS, N_BUCKETS), F32),
                          compiler_params=_cp())(a_all, onehot)


def _headnorm_bwd(dn, o, gv, mv):
    ms = _nn2(o * o, mv) * (1.0 / HEAD_DIM)
    r = lax.rsqrt(ms + EPS)
    nrm = o * r
    dg = jnp.sum(dn * nrm, axis=0, keepdims=True)
    dnn = dn * gv
    do = r * (dnn - nrm * (_nn2(dnn * nrm, mv) * (1.0 / HEAD_DIM)))
    return do, dg


def _mix_bwd_out(dx, gt, tv, w_out, o_sb, o_dil, on_sb, on_dil, g_sb, g_dil, ones_g, seq):
    t, d = dx.shape
    per = seq // TM
    nb = t // seq

    def body(dx_ref, gt_ref, t_ref, w_ref, osb, odl, onsb, ondl, gsb, gdl, m_ref,
             dosb, dodl, dgt_ref, dgsb, dgdl, dw_ref):
        m = pl.program_id(0)
        dxv = dx_ref[...]
        dt = (gt_ref[...] * dxv).astype(BF)
        _acc_rows(dgt_ref, jnp.sum(dxv * t_ref[...], axis=0, keepdims=True), m % per == 0)
        mv = m_ref[...]
        don_sb = _nt(dt, w_ref[0:D_GRP, :])
        don_dl = _nt(dt, w_ref[D_GRP:2 * D_GRP, :])
        do1, dg1 = _headnorm_bwd(don_sb, osb[...], gsb[...], mv)
        do2, dg2 = _headnorm_bwd(don_dl, odl[...], gdl[...], mv)
        dosb[...] = do1
        dodl[...] = do2
        _acc_rows(dgsb, dg1, m == 0)
        _acc_rows(dgdl, dg2, m == 0)
        p1 = _tn(onsb[...], dt)
        p2 = _tn(ondl[...], dt)

        @pl.when(m == 0)
        def _():
            dw_ref[0:D_GRP, :] = p1
            dw_ref[D_GRP:2 * D_GRP, :] = p2

        @pl.when(m != 0)
        def _():
            dw_ref[0:D_GRP, :] += p1
            dw_ref[D_GRP:2 * D_GRP, :] += p2

    row = pl.BlockSpec((TM, d), lambda m: (m, 0))
    half = pl.BlockSpec((TM, D_GRP), lambda m: (m, 0))
    ex = pl.BlockSpec((None, 1, d), lambda m: (m // per, 0, 0))
    gvec = pl.BlockSpec((1, D_GRP), lambda m: (0, 0))
    wblk = pl.BlockSpec((2 * D_GRP, d), lambda m: (0, 0))
    return pl.pallas_call(
        body, name="mix_bwd_out", grid=(t // TM,),
        in_specs=[row, ex, row, wblk, half, half, half, half, gvec, gvec,
                  pl.BlockSpec((D_GRP, D_GRP), lambda m: (0, 0))],
        out_specs=[half, half, ex, gvec, gvec, wblk],
        out_shape=[_sds((t, D_GRP), F32), _sds((t, D_GRP), F32), _sds((nb, 1, d), F32),
                   _sds((1, D_GRP), F32), _sds((1, D_GRP), F32), _sds((2 * D_GRP, d), F32)],
        compiler_params=_cp(1))(dx, gt, tv, w_out, o_sb, o_dil, on_sb, on_dil, g_sb, g_dil, ones_g)


def _dw_in(h, dqkv6, carry=None):
    t, d = h.shape
    wc = 6 * D_GRP // N_CHIPS

    def body(h_ref, g_ref, o_ref):
        kt = pl.program_id(0)
        hv = h_ref[...]
        for j in range(N_CHIPS):
            p = _tn(hv, _chip_cols(g_ref, j, wc))

            @pl.when(kt == 0)
            def _(p=p, j=j):
                o_ref[j, 0] = p

            @pl.when(kt != 0)
            def _(p=p, j=j):
                o_ref[j, 0] += p

    return _call(
        body, "dw_in", (t // TM,),
        [pl.BlockSpec((TM, d), lambda kt: (kt, 0)), pl.BlockSpec((6, TM, D_GRP), lambda kt: (0, kt, 0))],
        [pl.BlockSpec((N_CHIPS, 1, d, wc), lambda kt: (0, 0, 0, 0))],
        [_sds((N_CHIPS, 1, d, wc), F32)], (h, dqkv6), carry=carry)


def _mix_bwd_dh(dqkv6, w_in, x, g, sc, dxo, seq, carry=None):
    _, t, _ = dqkv6.shape
    d = x.shape[-1]
    wc = w_in.shape[-1]
    per = seq // TM
    nb = t // seq

    def body(g6_ref, w_ref, x_ref, g_ref, sc_ref, dxo_ref, dx_ref, dsh_ref, dsc_ref, dg_ref):
        m = pl.program_id(0)
        dh = _nt(_chip_cols(g6_ref, 0, wc), w_ref[0, 0])
        for j in range(1, N_CHIPS):
            dh = dh + _nt(_chip_cols(g6_ref, j, wc), w_ref[j, 0])
        dx, dsh, dsc, dg = _modnorm_bwd_tile(dh, x_ref[...], g_ref[...], sc_ref[...], dxo_ref[...])
        dx_ref[...] = dx
        _acc_rows(dsh_ref, dsh, m % per == 0)
        _acc_rows(dsc_ref, dsc, m % per == 0)
        _acc_rows(dg_ref, dg, m == 0)

    row = pl.BlockSpec((TM, d), lambda m: (m, 0))
    ex = pl.BlockSpec((None, 1, d), lambda m: (m // per, 0, 0))
    vec = pl.BlockSpec((1, d), lambda m: (0, 0))
    return _call(
        body, "mix_bwd_dh", (t // TM,),
        [pl.BlockSpec((6, TM, D_GRP), lambda m: (0, m, 0)), _whole(w_in), row, vec, ex, row],
        [row, ex, ex, vec],
        [_sds((t, d), F32), _sds((nb, 1, d), F32), _sds((nb, 1, d), F32), _sds((1, d), F32)],
        (dqkv6, w_in, x, g, sc, dxo), carry=carry)


def _final_loss(x, g, target):
    t, d = x.shape
    steps = t // TM

    def body(x_ref, g_ref, t_ref, dx_ref, dg_ref, loss_ref, lacc):
        m = pl.program_id(0)
        xv = x_ref[...]
        gv = g_ref[...]
        r = lax.rsqrt(jnp.mean(xv * xv, axis=-1, keepdims=True) + EPS)
        n = xv * r
        err = n * gv - t_ref[...]
        dy = err * (1.0 / d)
        _acc_rows(dg_ref, jnp.sum(dy * n, axis=0, keepdims=True), m == 0)
        dn = dy * gv
        dx_ref[...] = r * (dn - n * jnp.mean(dn * n, axis=-1, keepdims=True))
        _acc_rows(lacc, jnp.sum(err * err, axis=0, keepdims=True), m == 0)

        @pl.when(m == steps - 1)
        def _():
            tot = jnp.sum(lacc[...], axis=-1, keepdims=True) * (0.5 / d)
            loss_ref[...] = jnp.broadcast_to(tot, (1, 128))

    row = pl.BlockSpec((TM, d), lambda m: (m, 0))
    vec = pl.BlockSpec((1, d), lambda m: (0, 0))
    return pl.pallas_call(
        body, name="final_loss", grid=(steps,),
        in_specs=[row, vec, row],
        out_specs=[row, vec, pl.BlockSpec((1, 128), lambda m: (0, 0))],
        out_shape=[_sds((t, d), F32), _sds((1, d), F32), _sds((1, 128), F32)],
        scratch_shapes=[pltpu.VMEM((1, d), F32)],
        compiler_params=_cp(1))(x, g, target)


def _row_tile(rows, cols):
    best = rows
    for tr in range(8, rows + 1, 8):
        if rows % tr == 0 and tr * cols * 4 <= (1 << 20):
            best = tr
    if best * cols * 4 > (1 << 21):
        best = 8
    return best


def _adamw(w, g_arr, g_sel, m, v):
    rows, cols = w.shape
    tr = _row_tile(rows, cols)
    b1c = 1.0 - ADAM_B1 ** ADAM_STEP
    b2c = 1.0 - ADAM_B2 ** ADAM_STEP

    def body(w_ref, g_ref, m_ref, v_ref, go_ref, d_ref, mo_ref, vo_ref):
        gv = g_ref[...]
        mn = ADAM_B1 * m_ref[...] + (1.0 - ADAM_B1) * gv
        vn = ADAM_B2 * v_ref[...] + (1.0 - ADAM_B2) * (gv * gv)
        go_ref[...] = gv
        mo_ref[...] = mn
        vo_ref[...] = vn
        d_ref[...] = -ADAM_LR * ((mn / b1c) / (jnp.sqrt(vn / b2c) + ADAM_EPS) + ADAM_WD * w_ref[...])

    blk = pl.BlockSpec((tr, cols), lambda i: (i, 0))
    shp = _sds((rows, cols), F32)
    return pl.pallas_call(
        body, name="adamw", grid=(rows // tr,),
        in_specs=[blk, pl.BlockSpec((None, tr, cols), lambda i: (g_sel, i, 0)), blk, blk],
        out_specs=[blk] * 4, out_shape=[shp] * 4,
        compiler_params=_cp(1))(w, g_arr, m, v)


def _flip(v, bit):
    return 1 - v if bit else v


def _my_place():
    x, y, c = lax.axis_index("x"), lax.axis_index("y"), lax.axis_index("c")
    return x, y, c


class _Exchange:
    def __init__(self, operands, out_shape, aliases, sems, start, finish):
        self.operands, self.out_shape, self.aliases, self.sems = list(operands), list(out_shape), dict(aliases), list(sems)
        self.start, self.finish = start, finish


def _join(exchanges):
    exchanges = [e for e in exchanges if e is not None]
    if not exchanges:
        return None
    ops, outs, sems, aliases, spans = [], [], [], {}, []
    for e in exchanges:
        spans.append((len(ops), len(outs), len(sems), e))
        for i, j in e.aliases.items():
            aliases[len(ops) + i] = len(outs) + j
        ops += e.operands
        outs += e.out_shape
        sems += e.sems

    def run(which):
        def go(ins, res, sm):
            for io, oo, so, e in spans:
                getattr(e, which)(ins[io:io + len(e.operands)], res[oo:oo + len(e.out_shape)], sm[so:so + len(e.sems)])
        return go

    return _Exchange(ops, outs, aliases, sems, run("start"), run("finish"))


def _call(body, name, grid, in_specs, out_specs, out_shape, args, scratch=(), carry=None):
    in_specs, out_specs, out_shape, scratch = list(in_specs), list(out_specs), list(out_shape), list(scratch)
    if carry is None:
        return pl.pallas_call(body, name=name, grid=grid, in_specs=in_specs, out_specs=out_specs,
                              out_shape=out_shape, scratch_shapes=scratch,
                              compiler_params=_cp(len(grid)))(*args)
    n_in, n_out, n_s = len(in_specs), len(out_specs), len(scratch)
    c_in, c_out = len(carry.operands), len(carry.out_shape)
    any_spec = pl.BlockSpec(memory_space=pl.ANY)

    def wrapped(*refs):
        ins, cins = refs[:n_in], refs[n_in:n_in + c_in]
        o0 = n_in + c_in
        outs, couts = refs[o0:o0 + n_out], refs[o0 + n_out:o0 + n_out + c_out]
        s0 = o0 + n_out + c_out
        scr, sems = refs[s0:s0 + n_s], refs[s0 + n_s:]
        first = pl.program_id(0) == 0
        last = pl.program_id(0) == grid[0] - 1
        for ax in range(1, len(grid)):
            first = jnp.logical_and(first, pl.program_id(ax) == 0)
            last = jnp.logical_and(last, pl.program_id(ax) == grid[ax] - 1)

        @pl.when(first)
        def _():
            carry.start(cins, couts, sems)

        body(*ins, *outs, *scr)

        @pl.when(last)
        def _():
            carry.finish(cins, couts, sems)

    return pl.pallas_call(
        wrapped, name=name, grid=grid, in_specs=in_specs + [any_spec] * c_in,
        out_specs=out_specs + [any_spec] * c_out, out_shape=out_shape + carry.out_shape,
        scratch_shapes=scratch + carry.sems,
        input_output_aliases={n_in + i: n_out + j for i, j in carry.aliases.items()},
        compiler_params=_cp(len(grid)))(*args, *carry.operands)


def _whole_call(body, name, args, out_shape, scratch, carry=None):
    vm = pl.BlockSpec(memory_space=pltpu.VMEM)
    any_spec = pl.BlockSpec(memory_space=pl.ANY)
    out_shape, scratch = list(out_shape), list(scratch)
    n_in, n_out, n_s = len(args), len(out_shape), len(scratch)
    if carry is None:
        return pl.pallas_call(body, name=name, in_specs=[vm] * n_in, out_specs=[vm] * n_out, out_shape=out_shape,
                              scratch_shapes=scratch, compiler_params=_cp())(*args)
    c_in, c_out = len(carry.operands), len(carry.out_shape)

    def wrapped(*refs):
        ins, cins = refs[:n_in], refs[n_in:n_in + c_in]
        o0 = n_in + c_in
        outs, couts = refs[o0:o0 + n_out], refs[o0 + n_out:o0 + n_out + c_out]
        s0 = o0 + n_out + c_out
        scr, sems = refs[s0:s0 + n_s], refs[s0 + n_s:]
        carry.start(cins, couts, sems)
        body(*ins, *outs, *scr)
        carry.finish(cins, couts, sems)

    return pl.pallas_call(
        wrapped, name=name, in_specs=[vm] * n_in + [any_spec] * c_in, out_specs=[vm] * n_out + [any_spec] * c_out,
        out_shape=out_shape + carry.out_shape, scratch_shapes=scratch + carry.sems,
        input_output_aliases={n_in + i: n_out + j for i, j in carry.aliases.items()},
        compiler_params=_cp())(*args, *carry.operands)


def _alone(name, ex):
    any_spec = pl.BlockSpec(memory_space=pl.ANY)
    c_in, c_out = len(ex.operands), len(ex.out_shape)

    def body(*refs):
        ins, outs, sems = refs[:c_in], refs[c_in:c_in + c_out], refs[c_in + c_out:]
        ex.start(ins, outs, sems)
        ex.finish(ins, outs, sems)

    return pl.pallas_call(
        body, name=name, in_specs=[any_spec] * c_in, out_specs=[any_spec] * c_out, out_shape=ex.out_shape,
        scratch_shapes=ex.sems, input_output_aliases=ex.aliases, compiler_params=_cp())(*ex.operands)


def _ada_fwd(c_pad, w_ada, b_shard, carry=None):
    d = c_pad.shape[-1]
    cols = w_ada.shape[-1]
    chunk = 384

    def body(c_ref, w_ref, b_ref, call_ref, mod_ref, part, s1, r1, s2, r2):
        x, y, c = _my_place()
        dev = 4 * x + 2 * y + c
        chip = 2 * x + y
        call_ref[dev] = c_ref[...]

        def c_copy(k):
            px, py, pc = _flip(x, (k >> 2) & 1), _flip(y, (k >> 1) & 1), _flip(c, k & 1)
            return px, py, pc

        sends = []
        for k in range(1, N_DEV):
            px, py, pc = c_copy(k)
            cp = pltpu.make_async_remote_copy(src_ref=c_ref, dst_ref=call_ref.at[dev], send_sem=s1.at[k - 1],
                                              recv_sem=r1.at[k - 1], device_id=(px, py, pc), device_id_type=MESH)
            cp.start()
            sends.append(cp)
        for k in range(1, N_DEV):
            px, py, pc = c_copy(k)
            pltpu.make_async_remote_copy(src_ref=c_ref, dst_ref=call_ref.at[4 * px + 2 * py + pc],
                                         send_sem=s1.at[k - 1], recv_sem=r1.at[k - 1],
                                         device_id=(px, py, pc), device_id_type=MESH).wait_recv()
        for cp in sends:
            cp.wait_send()

        cs = call_ref[...].reshape(N_DEV * 8, d)
        sc = (cs * jax.nn.sigmoid(cs)).astype(BF)
        for n0 in range(0, cols, chunk):
            blk = _nn(sc, w_ref[:, n0:n0 + chunk].astype(BF)) + b_ref[:, n0:n0 + chunk]
            part[:, :, n0:n0 + chunk] = blk.reshape(N_DEV, 8, chunk)

        mod_ref[chip] = part[dev]
        sends = []
        for kk in range(1, N_CHIPS):
            px, py = _flip(x, (kk >> 1) & 1), _flip(y, kk & 1)
            cp = pltpu.make_async_remote_copy(src_ref=part.at[4 * px + 2 * py + c], dst_ref=mod_ref.at[chip],
                                              send_sem=s2.at[kk - 1], recv_sem=r2.at[kk - 1],
                                              device_id=(px, py, c), device_id_type=MESH)
            cp.start()
            sends.append(cp)
        for kk in range(1, N_CHIPS):
            px, py = _flip(x, (kk >> 1) & 1), _flip(y, kk & 1)
            pltpu.make_async_remote_copy(src_ref=part.at[dev], dst_ref=mod_ref.at[2 * px + py],
                                         send_sem=s2.at[kk - 1], recv_sem=r2.at[kk - 1],
                                         device_id=(px, py, c), device_id_type=MESH).wait_recv()
        for cp in sends:
            cp.wait_send()

    return _whole_call(
        body, "ada_fwd", (c_pad, w_ada, b_shard),
        [_sds((N_DEV, 8, d), F32), _sds((N_CHIPS, 8, cols), F32)],
        [pltpu.VMEM((N_DEV, 8, cols), F32),
         pltpu.SemaphoreType.DMA((N_DEV - 1,)), pltpu.SemaphoreType.DMA((N_DEV - 1,)),
         pltpu.SemaphoreType.DMA((N_CHIPS - 1,)), pltpu.SemaphoreType.DMA((N_CHIPS - 1,))], carry=carry)


def _ag_weights(bufs):
    n = len(bufs)

    def place():
        x, y, c = _my_place()
        others = [(_flip(x, (kk >> 1) & 1), _flip(y, kk & 1)) for kk in range(1, N_CHIPS)]
        return x, y, c, 2 * x + y, others

    def half(b, which):
        hr = bufs[b].shape[2] // 2
        return pl.ds(pl.multiple_of(which * hr, 16), hr)

    def ici(outs, sems, b, i, slot, x, y, c, px, py):
        rows = outs[b].at[slot, :, half(b, c), :]
        return pltpu.make_async_remote_copy(
            src_ref=rows, dst_ref=rows, send_sem=sems[0].at[3 * b + i], recv_sem=sems[1].at[3 * b + i],
            device_id=(px, py, c), device_id_type=MESH)

    def d2d(outs, sems, b, i, slot, x, y, c, which):
        rows = outs[b].at[slot, :, half(b, which), :]
        return pltpu.make_async_remote_copy(
            src_ref=rows, dst_ref=rows, send_sem=sems[2].at[3 * b + i], recv_sem=sems[3].at[3 * b + i],
            device_id=(x, y, 1 - c), device_id_type=MESH)

    def start(ins, outs, sems):
        x, y, c, chip, others = place()
        for b in range(n):
            for i, (px, py) in enumerate(others):
                ici(outs, sems, b, i, chip, x, y, c, px, py).start()

    def finish(ins, outs, sems):
        x, y, c, chip, others = place()
        for b in range(n):
            for i, (px, py) in enumerate(others):
                ici(outs, sems, b, i, 2 * px + py, x, y, c, px, py).wait_recv()
                d2d(outs, sems, b, i, 2 * px + py, x, y, c, c).start()
        for b in range(n):
            for i, (px, py) in enumerate(others):
                d2d(outs, sems, b, i, 2 * px + py, x, y, c, 1 - c).wait_recv()
        for b in range(n):
            for i, (px, py) in enumerate(others):
                ici(outs, sems, b, i, chip, x, y, c, px, py).wait_send()
                d2d(outs, sems, b, i, 2 * px + py, x, y, c, c).wait_send()

    return _Exchange(bufs, [_sds(s.shape, s.dtype) for s in bufs], {i: i for i in range(n)},
                     [pltpu.SemaphoreType.DMA((3 * n,))] * 4, start, finish)


def _rs_d2d(grads):
    n = len(grads)

    def copy(ins, outs, sems, b):
        x, y, c = _my_place()
        hr = grads[b].shape[2] // 2
        theirs = pl.ds(pl.multiple_of((1 - c) * hr, 8), hr)
        return pltpu.make_async_remote_copy(
            src_ref=ins[b].at[:, :, theirs, :], dst_ref=outs[b], send_sem=sems[0].at[b], recv_sem=sems[1].at[b],
            device_id=(x, y, 1 - c), device_id_type=MESH)

    def start(ins, outs, sems):
        for b in range(n):
            copy(ins, outs, sems, b).start()

    def finish(ins, outs, sems):
        for b in range(n):
            copy(ins, outs, sems, b).wait()

    return _Exchange(grads, [_sds(g.shape[:2] + (g.shape[2] // 2, g.shape[3]), F32) for g in grads], {},
                     [pltpu.SemaphoreType.DMA((n,))] * 2, start, finish)


def _add_halves(core, g, land):
    nchip, ng, rows, cols = g.shape
    hr = rows // 2
    tr = _row_tile(hr, cols)
    steps = hr // tr

    def body(core_ref, g_ref, l_ref, o_ref):
        del core_ref
        o_ref[...] = (g_ref[...] + l_ref[...]).astype(BF)

    return pl.pallas_call(
        body, name="add_halves",
        grid_spec=pltpu.PrefetchScalarGridSpec(
            num_scalar_prefetch=1, grid=(nchip, ng, steps),
            in_specs=[pl.BlockSpec((None, None, tr, cols), lambda j, a, i, cr: (j, a, cr[0] * steps + i, 0)),
                      pl.BlockSpec((None, None, tr, cols), lambda j, a, i, cr: (j, a, i, 0))],
            out_specs=pl.BlockSpec((None, None, tr, cols), lambda j, a, i, cr: (j, a, i, 0))),
        out_shape=_sds((nchip, ng, hr, cols), BF),
        compiler_params=_cp(3))(core, g, land)


def _rs_ici(parts):
    n = len(parts)

    def copies(ins, outs, sems):
        x, y, c = _my_place()
        chip = 2 * x + y
        for b in range(n):
            for kk in range(1, N_CHIPS):
                px, py = _flip(x, (kk >> 1) & 1), _flip(y, kk & 1)
                k = 3 * b + kk - 1
                send = pltpu.make_async_remote_copy(
                    src_ref=ins[b].at[2 * px + py], dst_ref=outs[b].at[chip],
                    send_sem=sems[0].at[k], recv_sem=sems[1].at[k], device_id=(px, py, c), device_id_type=MESH)
                slot = outs[b].at[2 * px + py]
                recv = pltpu.make_async_remote_copy(
                    src_ref=slot, dst_ref=slot, send_sem=sems[0].at[k], recv_sem=sems[1].at[k],
                    device_id=(px, py, c), device_id_type=MESH)
                yield send, recv

    def start(ins, outs, sems):
        for send, _ in copies(ins, outs, sems):
            send.start()

    def finish(ins, outs, sems):
        for send, recv in copies(ins, outs, sems):
            recv.wait_recv()
            send.wait_send()

    return _Exchange(parts, [_sds(p.shape, p.dtype) for p in parts], {},
                     [pltpu.SemaphoreType.DMA((3 * n,))] * 2, start, finish)


def _sum_chips(place, part, land):
    nchip, ng, hr, cols = land.shape
    tr = _row_tile(hr, cols)
    steps = hr // tr

    def body(place_ref, p_ref, l1, l2, l3, o_ref):
        del place_ref
        o_ref[...] = ((p_ref[...].astype(F32) + l1[...].astype(F32)) + l2[...].astype(F32)) + l3[...].astype(F32)

    def slot(k):
        return pl.BlockSpec((None, None, tr, cols), lambda a, i, pr: (jnp.bitwise_xor(pr[1], k), a, i, 0))

    return pl.pallas_call(
        body, name="sum_chips",
        grid_spec=pltpu.PrefetchScalarGridSpec(
            num_scalar_prefetch=1, grid=(ng, steps),
            in_specs=[slot(0), slot(1), slot(2), slot(3)],
            out_specs=pl.BlockSpec((None, tr, cols), lambda a, i, pr: (a, pr[0] * steps + i, 0))),
        out_shape=_sds((ng, 2 * hr, cols), F32),
        compiler_params=_cp(2))(place, part, land, land, land)


def _rs_final(bufs):
    n = len(bufs)

    def copy(outs, sems, b, which):
        x, y, c = _my_place()
        hr = bufs[b].shape[1] // 2
        rows = outs[b].at[:, pl.ds(pl.multiple_of((c if which == 0 else 1 - c) * hr, 8), hr), :]
        return pltpu.make_async_remote_copy(
            src_ref=rows, dst_ref=rows, send_sem=sems[0].at[b], recv_sem=sems[1].at[b],
            device_id=(x, y, 1 - c), device_id_type=MESH)

    def start(ins, outs, sems):
        for b in range(n):
            copy(outs, sems, b, 0).start()

    def finish(ins, outs, sems):
        for b in range(n):
            copy(outs, sems, b, 0).wait_send()
            copy(outs, sems, b, 1).wait_recv()

    return _Exchange(bufs, [_sds(h.shape, F32) for h in bufs], {i: i for i in range(n)},
                     [pltpu.SemaphoreType.DMA((n,))] * 2, start, finish)


def _small_sync(smalls, dmod_blk, c_all, carry=None):
    d = c_all.shape[-1]
    cols = dmod_blk.shape[-1]
    chunk = 384

    def body(sm_ref, dm_ref, c_ref, sum_ref, gw_ref, sm_all, dm_all, ssem, rsem):
        x, y, c = _my_place()
        dev = 4 * x + 2 * y + c
        chip = 2 * x + y
        sm_all[dev] = sm_ref[...]
        dm_all[dev] = dm_ref[chip]
        sends = []
        for k in range(1, N_DEV):
            px, py, pc = _flip(x, (k >> 2) & 1), _flip(y, (k >> 1) & 1), _flip(c, k & 1)
            a = pltpu.make_async_remote_copy(src_ref=sm_ref, dst_ref=sm_all.at[dev], send_sem=ssem.at[2 * (k - 1)],
                                             recv_sem=rsem.at[2 * (k - 1)], device_id=(px, py, pc),
                                             device_id_type=MESH)
            b = pltpu.make_async_remote_copy(src_ref=dm_ref.at[2 * px + py], dst_ref=dm_all.at[dev],
                                             send_sem=ssem.at[2 * (k - 1) + 1], recv_sem=rsem.at[2 * (k - 1) + 1],
                                             device_id=(px, py, pc), device_id_type=MESH)
            a.start()
            b.start()
            sends += [a, b]
        for k in range(1, N_DEV):
            px, py, pc = _flip(x, (k >> 2) & 1), _flip(y, (k >> 1) & 1), _flip(c, k & 1)
            pdev = 4 * px + 2 * py + pc
            pltpu.make_async_remote_copy(src_ref=sm_ref, dst_ref=sm_all.at[pdev], send_sem=ssem.at[2 * (k - 1)],
                                         recv_sem=rsem.at[2 * (k - 1)], device_id=(px, py, pc),
                                         device_id_type=MESH).wait_recv()
            pltpu.make_async_remote_copy(src_ref=dm_ref.at[chip], dst_ref=dm_all.at[pdev],
                                         send_sem=ssem.at[2 * (k - 1) + 1], recv_sem=rsem.at[2 * (k - 1) + 1],
                                         device_id=(px, py, pc), device_id_type=MESH).wait_recv()
        for cp in sends:
            cp.wait_send()

        tot = sm_all[0]
        for q in range(1, N_DEV):
            tot = tot + sm_all[q]
        sum_ref[...] = tot

        cs = c_ref[...].reshape(N_DEV * 8, d)
        sc = (cs * jax.nn.sigmoid(cs)).astype(BF)
        for n0 in range(0, cols, chunk):
            dmv = dm_all[:, :, n0:n0 + chunk].reshape(N_DEV * 8, chunk).astype(BF)
            gw_ref[:, n0:n0 + chunk] = _tn(sc, dmv)

    return _whole_call(
        body, "small_sync", (smalls, dmod_blk, c_all),
        [_sds(smalls.shape, F32), _sds((d, cols), F32)],
        [pltpu.VMEM((N_DEV,) + smalls.shape, F32), pltpu.VMEM((N_DEV, 8, cols), F32),
         pltpu.SemaphoreType.DMA((2 * (N_DEV - 1),)), pltpu.SemaphoreType.DMA((2 * (N_DEV - 1),))], carry=carry)


def _bucket_onehot():
    maps = np.stack([_bucket_map(dil).reshape(-1) for _, dil in DIL_CONFIGS])
    return (jnp.asarray(maps)[:, None, :] == jnp.arange(N_BUCKETS, dtype=jnp.int32)[None, :, None]).astype(BF)


def _dil_bias(rel_t, onehot):
    def body(r_ref, oh_ref, o_ref):
        rv = r_ref[...]
        hi = rv.astype(BF)
        lo = (rv - hi.astype(F32)).astype(BF)
        for c in range(len(DIL_CONFIGS)):
            o_ref[c] = _nn(hi, oh_ref[c]) + _nn(lo, oh_ref[c])

    return pl.pallas_call(body, name="dil_bias",
                          out_shape=_sds((len(DIL_CONFIGS), N_HEADS, BLOCK * 2 * BLOCK), F32),
                          compiler_params=_cp())(rel_t, onehot)


def _rowsum8(a):
    def body(a_ref, o_ref):
        o_ref[...] = jnp.sum(a_ref[...], axis=0, keepdims=True)

    return pl.pallas_call(body, name="rowsum8", out_shape=_sds((1, a.shape[1]), F32), compiler_params=_cp())(a)


def _local_step(x, mod, target, w, gains, rel_bias, place=None):
    nb, seq, d = x.shape
    t = nb * seq
    dist = place is not None
    core = place[0:1] if dist else None
    x0 = x.reshape(t, d)
    tgt = target.reshape(t, d)
    md = [mod[:, i:i + 1, :] for i in range(N_MOD)]
    sh1, sc1, gt1, sh2, sc2, gt2, sh3, sc3, gt3 = md
    g1, g2, g3 = gains["g_ffn1"], gains["g_mix"], gains["g_ffn2"]
    ones_g = _group_ones()

    def partial_sums(grads, lands):
        return [_add_halves(core, g, l) for g, l in zip(grads, lands)]

    def chip_sums(parts, lands):
        return [_sum_chips(place, p, l) for p, l in zip(parts, lands)]

    h1 = _modnorm(x0, g1, sc1, sh1, seq)
    res = _ffn_up(h1, w["gu1"], carry=_ag_weights([w["d1"], w["win"], w["wout"]]) if dist else None)
    a1, u1, s1 = res[:3]
    wd1, w_in, w_out = res[3:] if dist else (w["d1"], w["win"], w["wout"])
    w_out2 = w_out.reshape(2 * D_GRP, d)
    f1, x1 = _ffn_down(s1, wd1, x0, gt1, seq, 0.5)

    h2 = _modnorm(x1, g2, sc2, sh2, seq)
    qkv6, qkv_r4, qkv_r16 = _qkv_proj(h2, w_in)
    qkv6b = qkv6.reshape(6, nb, seq, D_GRP)
    res = _sb_fwd(qkv6b, gains["g_sb_out"], nb, seq, carry=_ag_weights([w["gu2"], w["d2"]]) if dist else None)
    o_sb, on_sb = res[:2]
    wgu2, wd2 = res[2:] if dist else (w["gu2"], w["d2"])
    onehot = _bucket_onehot()
    bias = _dil_bias(rel_bias.T, onehot).reshape(len(DIL_CONFIGS), N_HEADS, BLOCK, 2 * BLOCK)
    o_cs, l_cs = [], []
    qkv_rs = [(qkv6b, 3), (qkv_r4, 0), (qkv_r16, 0)]
    for ci, (_, dil) in enumerate(DIL_CONFIGS):
        sub = seq // dil
        arr, base = qkv_rs[ci]
        arr = arr.reshape(base + 3, nb, sub, dil * D_GRP)
        qkv_rs[ci] = (arr, base)
        o_c, l_c = _dil_fwd(arr, base, bias[ci], nb, sub, dil)
        o_cs.append(o_c.reshape(t // dil, dil * D_GRP))
        l_cs.append(l_c.reshape(t // dil, dil * D_GRP))
    o_dil, on_dil = _dil_comb(o_cs, l_cs, gains["g_dil_out"])
    tmix, x2 = _mix_out(on_sb.reshape(t, D_GRP), on_dil, w_out2, x1, gt2, seq)

    h3 = _modnorm(x2, g3, sc3, sh3, seq)
    a3, u3, s3 = _ffn_up(h3, wgu2)
    f3, x3 = _ffn_down(s3, wd2, x2, gt3, seq, 0.5)

    dx3, dg_final, loss = _final_loss(x3, gains["g_final"], tgt)

    da3, du3, df3, dgt3 = _ffn_bwd_ds(dx3, gt3, f3, wd2, a3, u3, seq, 0.5)
    grads2 = [_ffn_bwd_w(h3, da3, du3, s3, df3)]
    res = _ffn_bwd_dh(da3, du3, wgu2, x2, g3, sc3, dx3, seq, carry=_rs_d2d(grads2) if dist else None)
    dx2, dsh3, dsc3, dg3 = res[:4]
    parts2 = partial_sums(grads2, res[4:]) if dist else None

    do_sb, do_dil, dgt2, dg_sb, dg_dil, dw_out = _mix_bwd_out(
        dx2, gt2, tmix, w_out2, o_sb.reshape(t, D_GRP), o_dil, on_sb.reshape(t, D_GRP), on_dil,
        gains["g_sb_out"], gains["g_dil_out"], ones_g, seq)
    dw_out = dw_out.reshape(N_CHIPS, 1, 2 * D_GRP // N_CHIPS, d)
    res = _sb_bwd(qkv6b, do_sb.reshape(nb, seq, D_GRP), nb, seq, carry=_rs_ici(parts2) if dist else None)
    dqkv6 = res[0]
    halves2 = chip_sums(parts2, res[1:]) if dist else None
    dcs = _dil_comb_bwd(do_dil, o_cs, l_cs)
    dsum, a_tiles = [], []
    for ci, (_, dil) in enumerate(DIL_CONFIGS):
        sub = seq // dil
        do_c = dcs[ci].reshape(nb, sub, dil * D_GRP)
        dd_c = dcs[3 + ci].reshape(nb, sub, dil * D_GRP)
        res = _dil_bwd(qkv_rs[ci][0], qkv_rs[ci][1], bias[ci], do_c, dd_c, nb, sub, dil,
                       carry=_rs_final(halves2) if dist and ci == 0 else None)
        if dist and ci == 0:
            grads2 = res[2:]
        dsum.append(res[0].reshape(3, t // dil, dil * D_GRP))
        a_tiles.append(res[1].reshape(N_HEADS, BLOCK * 2 * BLOCK))
    dqkv6 = _dqkv_dil_sum(dsum, dqkv6.reshape(6, t, D_GRP))
    drel = _relbias_grad(jnp.stack(a_tiles), onehot)
    dx1, dsh2, dsc2, dg2 = _mix_bwd_dh(dqkv6, w_in, x1, g2, sc2, dx2, seq)

    da1, du1, df1, dgt1 = _ffn_bwd_ds(dx1, gt1, f1, wd1, a1, u1, seq, 0.5)
    grads1 = [_ffn_bwd_w(h1, da1, du1, s1, df1)]
    res = _dw_in(h2, dqkv6, carry=_rs_d2d(grads1) if dist else None)
    grads_m = [res[0], dw_out]
    parts1 = partial_sums(grads1, res[1:]) if dist else None
    res = _ffn_bwd_dh(da1, du1, w["gu1"], x0, g1, sc1, dx1, seq,
                      carry=_join([_rs_ici(parts1), _rs_d2d(grads_m)]) if dist else None)
    dx0, dsh1, dsc1, dg1 = res[:4]
    pending = None
    if dist:
        pending = (chip_sums(parts1, res[4:5]), partial_sums(grads_m, res[5:7]))

    dmod = jnp.concatenate([dsh1, dsc1, dgt1, dsh2, dsc2, dgt2, dsh3, dsc3, dgt3], axis=1)
    return dict(grad_x=dx0.reshape(nb, seq, d), loss=loss[0, 0], dmod=dmod.reshape(nb, N_MOD * d),
                dffn1=grads1[0], dffn2=grads2[0], dwin=grads_m[0], dwout=grads_m[1], pending=pending,
                dg_ffn1=dg1, dg_mix=dg2, dg_ffn2=dg3, dg_final=dg_final, dg_sb=dg_sb, dg_dil=dg_dil,
                drel=drel.T)


_SMALL_ORDER = (("b_ada", N_MOD * 1024), ("g_ffn1", 1024), ("g_mix", 1024), ("g_ffn2", 1024), ("g_final", 1024),
                ("g_sb_out", D_GRP), ("g_dil_out", D_GRP), ("rel_bias", N_BUCKETS * N_HEADS))


def _pack_small(parts, extra=None):
    flat = [parts[name].reshape(-1).astype(F32) for name, _ in _SMALL_ORDER]
    used = sum(sz for _, sz in _SMALL_ORDER)
    pad = SMALL_ROWS * 128 - used
    tail = jnp.zeros((pad,), F32)
    if extra is not None:
        tail = tail.at[0].set(extra)
    return jnp.concatenate(flat + [tail]).reshape(SMALL_ROWS, 128)


def _unpack_small(packed, shapes):
    flat = packed.reshape(-1)
    out, off = {}, 0
    for name, sz in _SMALL_ORDER:
        out[name] = flat[off:off + sz].reshape(shapes[name])
        off += sz
    return out, flat[off]


def kernel(x, c, w_ada, b_ada, g_ffn1, w1_gate, w1_up, w1_down, g_mix, w_in, g_sb_out, g_dil_out, w_out, rel_bias, g_ffn2, w2_gate, w2_up, w2_down, g_final, loss_target, m_w_ada, m_b_ada, m_g_ffn1, m_w1_gate, m_w1_up, m_w1_down, m_g_mix, m_w_in, m_g_sb_out, m_g_dil_out, m_w_out, m_rel_bias, m_g_ffn2, m_w2_gate, m_w2_up, m_w2_down, m_g_final, v_w_ada, v_b_ada, v_g_ffn1, v_w1_gate, v_w1_up, v_w1_down, v_g_mix, v_w_in, v_g_sb_out, v_g_dil_out, v_w_out, v_rel_bias, v_g_ffn2, v_w2_gate, v_w2_up, v_w2_down, v_g_final):
    nb, seq, d = x.shape
    xi, yi, ci = lax.axis_index("x"), lax.axis_index("y"), lax.axis_index("c")
    chip = 2 * xi + yi
    ada_cols = w_ada.shape[-1]

    c_pad = jnp.zeros((8, d), F32).at[:nb].set(c)
    b_shard = lax.dynamic_slice(b_ada, (0, chip * ada_cols), (1, ada_cols))
    shards = dict(gu1=jnp.stack([w1_gate[0], w1_up[0]]), d1=w1_down, win=w_in, wout=w_out,
                  gu2=jnp.stack([w2_gate[0], w2_up[0]]), d2=w2_down)
    bufs = {k: lax.dynamic_update_slice(lax.empty((N_CHIPS,) + s.shape, BF), s.astype(BF)[None], (chip, 0, 0, 0))
            for k, s in shards.items()}
    c_all, mod_blk, bufs["gu1"] = _ada_fwd(c_pad, w_ada[0], b_shard, carry=_ag_weights([bufs["gu1"]]))
    mod = jnp.transpose(mod_blk[:, :nb, :], (1, 0, 2)).reshape(nb, N_MOD, d)

    gains = dict(g_ffn1=g_ffn1, g_mix=g_mix, g_ffn2=g_ffn2, g_final=g_final.reshape(1, d),
                 g_sb_out=g_sb_out.reshape(1, D_GRP), g_dil_out=g_dil_out.reshape(1, D_GRP))
    place = jnp.stack([ci, chip]).astype(jnp.int32)
    r = _local_step(x, mod, loss_target, bufs, gains, rel_bias, place)

    dmod = r["dmod"]
    dmod_pad = jnp.zeros((8, N_MOD * d), F32).at[:nb].set(dmod)
    dmod_blk = jnp.transpose(dmod_pad.reshape(8, N_CHIPS, ada_cols), (1, 0, 2))
    small_parts = dict(b_ada=_rowsum8(dmod_pad), g_ffn1=r["dg_ffn1"], g_mix=r["dg_mix"], g_ffn2=r["dg_ffn2"],
                       g_final=r["dg_final"], g_sb_out=r["dg_sb"], g_dil_out=r["dg_dil"], rel_bias=r["drel"])
    halves1, parts_m = r["pending"]
    res = _small_sync(_pack_small(small_parts, r["loss"]), dmod_blk, c_all,
                      carry=_join([_rs_final(halves1), _rs_ici(parts_m)]))
    small_sum, g_wada, gffn1 = res[:3]
    halves_m = [_sum_chips(place, p, l) for p, l in zip(parts_m, res[3:5])]
    gwin, gwout = _alone("rs_last", _rs_final(halves_m))
    gffn2 = r["dffn2"]

    small_w = dict(b_ada=b_ada, g_ffn1=g_ffn1, g_mix=g_mix, g_ffn2=g_ffn2, g_final=g_final,
                   g_sb_out=g_sb_out, g_dil_out=g_dil_out, rel_bias=rel_bias)
    small_m = dict(b_ada=m_b_ada, g_ffn1=m_g_ffn1, g_mix=m_g_mix, g_ffn2=m_g_ffn2, g_final=m_g_final,
                   g_sb_out=m_g_sb_out, g_dil_out=m_g_dil_out, rel_bias=m_rel_bias)
    small_v = dict(b_ada=v_b_ada, g_ffn1=v_g_ffn1, g_mix=v_g_mix, g_ffn2=v_g_ffn2, g_final=v_g_final,
                   g_sb_out=v_g_sb_out, g_dil_out=v_g_dil_out, rel_bias=v_rel_bias)
    shapes = {k: v.shape for k, v in small_w.items()}
    sg, sd, sm, sv = _adamw(_pack_small(small_w), small_sum.reshape(1, SMALL_ROWS, 128), 0,
                            _pack_small(small_m), _pack_small(small_v))
    sg, loss = _unpack_small(sg, shapes)
    sd, _ = _unpack_small(sd, shapes)
    sm, _ = _unpack_small(sm, shapes)
    sv, _ = _unpack_small(sv, shapes)

    big = {}

    def upd(name, w, g_arr, sel, m, v, transposed=False):
        swap = (lambda a: jnp.swapaxes(a, -1, -2)) if transposed else (lambda a: a)
        w2, m2, v2 = [swap(a)[0] for a in (w, m, v)]
        big[name] = [swap(a[None]) for a in _adamw(w2, g_arr, sel, m2, v2)]

    upd("w_ada", w_ada, g_wada.reshape(1, d, ada_cols), 0, m_w_ada, v_w_ada)
    upd("w1_gate", w1_gate, gffn1, 0, m_w1_gate, v_w1_gate, transposed=True)
    upd("w1_up", w1_up, gffn1, 1, m_w1_up, v_w1_up, transposed=True)
    upd("w1_down", w1_down, gffn1, 2, m_w1_down, v_w1_down)
    upd("w_in", w_in, gwin, 0, m_w_in, v_w_in)
    upd("w_out", w_out, gwout, 0, m_w_out, v_w_out)
    upd("w2_gate", w2_gate, gffn2, 0, m_w2_gate, v_w2_gate, transposed=True)
    upd("w2_up", w2_up, gffn2, 1, m_w2_up, v_w2_up, transposed=True)
    upd("w2_down", w2_down, gffn2, 2, m_w2_down, v_w2_down)

    names = ["w_ada", "b_ada", "g_ffn1", "w1_gate", "w1_up", "w1_down", "g_mix", "w_in", "g_sb_out", "g_dil_out",
             "w_out", "rel_bias", "g_ffn2", "w2_gate", "w2_up", "w2_down", "g_final"]
    outs = [loss, r["grad_x"]]
    for k, small in enumerate((sg, sd, sm, sv)):
        for name in names:
            outs.append(big[name][k] if name in big else small[name])
    return tuple(outs)
```

```python
import functools
import math

import numpy as np
import jax
import jax.numpy as jnp
from jax import lax
from jax.experimental import pallas as pl
from jax.experimental.pallas import tpu as pltpu

F32 = jnp.float32
BF = jnp.bfloat16
MESH = pl.DeviceIdType.MESH

HEAD_DIM = 64
N_HEADS = 8
D_GRP = N_HEADS * HEAD_DIM
DIL_CONFIGS = ((128, 1), (512, 4), (2048, 16))
N_STEPS = 128
BLOCK = 128
N_BUCKETS = 32
MAX_DISTANCE = 2048
N_MOD = 9
EPS = 1e-6
NEG_INF = -1e30
SCALE = HEAD_DIM ** -0.5

ADAM_LR = 0.001
ADAM_B1 = 0.9
ADAM_B2 = 0.999
ADAM_EPS = 1e-08
ADAM_WD = 0.01
ADAM_STEP = 10

N_CHIPS = 4
N_DEV = 8
VMEM_LIMIT = 56 * 1024 * 1024
TM = 512
TQ = 256
KB = 256
SMALL_ROWS = 120


def _cp(n_axes=0, **kw):
    sem = ("arbitrary",) * n_axes if n_axes else None
    return pltpu.CompilerParams(dimension_semantics=sem, vmem_limit_bytes=VMEM_LIMIT, **kw)


def _nn(a, b):
    return jnp.dot(a, b, preferred_element_type=F32)


def _nt(a, b):
    return lax.dot_general(a, b, (((1,), (1,)), ((), ())), preferred_element_type=F32)


def _tn(a, b):
    return lax.dot_general(a, b, (((0,), (0,)), ((), ())), preferred_element_type=F32)


def _nn2(x, m):
    hi = x.astype(BF)
    lo = (x - hi.astype(F32)).astype(BF)
    return _nn(hi, m) + _nn(lo, m)


def _softplus(z):
    return jnp.maximum(z, 0.0) + jnp.log1p(jnp.exp(-jnp.abs(z)))


def _sds(shape, dtype):
    return jax.ShapeDtypeStruct(shape, dtype)


def _whole(a):
    nd = a.ndim
    return pl.BlockSpec(a.shape, lambda *_: (0,) * nd, pipeline_mode=pl.Buffered(1))


def _modnorm_bwd_tile(dh, xv, gv, scv, dxo):
    r = lax.rsqrt(jnp.mean(xv * xv, axis=-1, keepdims=True) + EPS)
    n = xv * r
    ng = n * gv
    dsh = jnp.sum(dh, axis=0, keepdims=True)
    dsc = jnp.sum(dh * ng, axis=0, keepdims=True)
    dy = dh * (1.0 + scv)
    dg = jnp.sum(dy * n, axis=0, keepdims=True)
    dn = dy * gv
    dx = dxo + r * (dn - n * jnp.mean(dn * n, axis=-1, keepdims=True))
    return dx, dsh, dsc, dg


def _acc_rows(ref, val, first):
    @pl.when(first)
    def _():
        ref[...] = val

    @pl.when(jnp.logical_not(first))
    def _():
        ref[...] += val


def _modnorm_tile(x_ref, g_ref, sc_ref, sh_ref):
    xv = x_ref[...]
    r = lax.rsqrt(jnp.mean(xv * xv, axis=-1, keepdims=True) + EPS)
    return (((xv * r) * g_ref[...]) * (1.0 + sc_ref[...]) + sh_ref[...]).astype(BF)


def _ffn_up(x, g, sc, sh, wgu, seq, carry=None):
    t, d = x.shape
    fs = wgu.shape[-1]
    per = seq // TM

    def body(x_ref, g_ref, sc_ref, sh_ref, w_ref, h_ref, p_ref, q_ref, s_ref):
        hv = _modnorm_tile(x_ref, g_ref, sc_ref, sh_ref)
        h_ref[...] = hv
        for j in range(N_CHIPS):
            a = _nn(hv, w_ref[j, 0])
            u = _nn(hv, w_ref[j, 1])
            sig = jax.nn.sigmoid(a)
            q = a * sig
            p_ref[j] = (u * (sig * (1.0 + a * (1.0 - sig)))).astype(BF)
            q_ref[j] = q.astype(BF)
            s_ref[j] = (q * u).astype(BF)

    row = pl.BlockSpec((TM, d), lambda m: (m, 0))
    ex = pl.BlockSpec((None, 1, d), lambda m: (m // per, 0, 0))
    blk = pl.BlockSpec((N_CHIPS, TM, fs), lambda m: (0, m, 0))
    return _call(
        body, "ffn_up", (t // TM,),
        [row, pl.BlockSpec((1, d), lambda m: (0, 0)), ex, ex, _whole(wgu)],
        [row, blk, blk, blk],
        [_sds((t, d), BF)] + [_sds((N_CHIPS, t, fs), BF)] * 3,
        (x, g, sc, sh, wgu), carry=carry)


def _ffn_down(s, wd, x, gt, seq, coef):
    _, t, fs = s.shape
    d = x.shape[-1]
    per = seq // TM

    def body(s_ref, w_ref, x_ref, gt_ref, f_ref, xo_ref):
        f = _nn(s_ref[0], w_ref[0, 0])
        for j in range(1, N_CHIPS):
            f = f + _nn(s_ref[j], w_ref[j, 0])
        f_ref[...] = f
        xo_ref[...] = x_ref[...] + (coef * gt_ref[...]) * f

    row = pl.BlockSpec((TM, d), lambda m: (m, 0))
    return pl.pallas_call(
        body, name="ffn_down", grid=(t // TM,),
        in_specs=[pl.BlockSpec((N_CHIPS, TM, fs), lambda m: (0, m, 0)),
                  _whole(wd), row,
                  pl.BlockSpec((None, 1, d), lambda m: (m // per, 0, 0))],
        out_specs=[row, row],
        out_shape=[_sds((t, d), F32), _sds((t, d), F32)],
        compiler_params=_cp(1))(s, wd, x, gt)


def _ffn_bwd_ds(dxo, gt, f, wd, p, q, seq, coef, carry=None):
    t, d = dxo.shape
    fs = p.shape[-1]
    per = seq // TM
    nb = t // seq

    def body(dxo_ref, gt_ref, f_ref, w_ref, p_ref, q_ref, da_ref, du_ref, df_ref, dgt_ref):
        m = pl.program_id(0)
        dxv = dxo_ref[...]
        df = ((coef * gt_ref[...]) * dxv).astype(BF)
        df_ref[...] = df
        _acc_rows(dgt_ref, coef * jnp.sum(dxv * f_ref[...], axis=0, keepdims=True), m % per == 0)
        for j in range(N_CHIPS):
            ds = _nt(df, w_ref[j, 0])
            da_ref[j] = (ds * p_ref[j].astype(F32)).astype(BF)
            du_ref[j] = (ds * q_ref[j].astype(F32)).astype(BF)

    row = pl.BlockSpec((TM, d), lambda m: (m, 0))
    blk = pl.BlockSpec((N_CHIPS, TM, fs), lambda m: (0, m, 0))
    ex = pl.BlockSpec((None, 1, d), lambda m: (m // per, 0, 0))
    return _call(
        body, "ffn_bwd_ds", (t // TM,),
        [row, ex, row, _whole(wd), blk, blk],
        [blk, blk, row, ex],
        [_sds((N_CHIPS, t, fs), BF), _sds((N_CHIPS, t, fs), BF), _sds((t, d), BF), _sds((nb, 1, d), F32)],
        (dxo, gt, f, wd, p, q), carry=carry)


TK_W = 1024


def _ffn_bwd_w(h, da, du, s, df):
    t, d = h.shape
    fs = da.shape[-1]

    def body(h_ref, da_ref, du_ref, s_ref, df_ref, o_ref):
        kt = pl.program_id(1)
        hv = h_ref[...]
        parts = (_tn(da_ref[...], hv), _tn(du_ref[...], hv), _tn(s_ref[...], df_ref[...]))

        @pl.when(kt == 0)
        def _():
            for i, p in enumerate(parts):
                o_ref[i] = p

        @pl.when(kt != 0)
        def _():
            for i, p in enumerate(parts):
                o_ref[i] += p

    row = pl.BlockSpec((TK_W, d), lambda j, kt: (kt, 0))
    blk = pl.BlockSpec((None, TK_W, fs), lambda j, kt: (j, kt, 0))
    return pl.pallas_call(
        body, name="ffn_bwd_w", grid=(N_CHIPS, t // TK_W),
        in_specs=[row, blk, blk, blk, row],
        out_specs=pl.BlockSpec((None, 3, fs, d), lambda j, kt: (j, 0, 0, 0)),
        out_shape=_sds((N_CHIPS, 3, fs, d), F32),
        compiler_params=_cp(2))(h, da, du, s, df)


def _ffn_bwd_dh(da, du, wgu, x, g, sc, dxo, seq, carry=None):
    _, t, fs = da.shape
    d = x.shape[-1]
    per = seq // TM
    nb = t // seq

    def body(da_ref, du_ref, w_ref, x_ref, g_ref, sc_ref, dxo_ref, dx_ref, dsh_ref, dsc_ref, dg_ref):
        m = pl.program_id(0)
        dh = _nt(da_ref[0], w_ref[0, 0]) + _nt(du_ref[0], w_ref[0, 1])
        for j in range(1, N_CHIPS):
            dh = dh + _nt(da_ref[j], w_ref[j, 0]) + _nt(du_ref[j], w_ref[j, 1])
        dx, dsh, dsc, dg = _modnorm_bwd_tile(dh, x_ref[...], g_ref[...], sc_ref[...], dxo_ref[...])
        dx_ref[...] = dx
        _acc_rows(dsh_ref, dsh, m % per == 0)
        _acc_rows(dsc_ref, dsc, m % per == 0)
        _acc_rows(dg_ref, dg, m == 0)

    row = pl.BlockSpec((TM, d), lambda m: (m, 0))
    blk = pl.BlockSpec((N_CHIPS, TM, fs), lambda m: (0, m, 0))
    ex = pl.BlockSpec((None, 1, d), lambda m: (m // per, 0, 0))
    vec = pl.BlockSpec((1, d), lambda m: (0, 0))
    return _call(
        body, "ffn_bwd_dh", (t // TM,),
        [blk, blk, _whole(wgu), row, vec, ex, row],
        [row, ex, ex, vec],
        [_sds((t, d), F32), _sds((nb, 1, d), F32), _sds((nb, 1, d), F32), _sds((1, d), F32)],
        (da, du, wgu, x, g, sc, dxo), carry=carry)


def _qkv_proj(x, g, sc, sh, w_in, seq, carry=None):
    t, d = x.shape
    wc = w_in.shape[-1]
    per = seq // TM

    dils = [dil for _, dil in DIL_CONFIGS if dil > 1]

    def body(x_ref, g_ref, sc_ref, sh_ref, w_ref, h_ref, o_ref, *rest):
        res_refs, buf = rest[:len(dils)], rest[len(dils)]
        hv = _modnorm_tile(x_ref, g_ref, sc_ref, sh_ref)
        h_ref[...] = hv
        for j in range(N_CHIPS):
            rf = _nn(hv, w_ref[j, 0])
            r = rf.astype(BF)
            for a, lc, off, width in _col_pieces(j, wc):
                o_ref[a, :, lc:lc + width] = r[:, off:off + width]
                if a < 3:
                    continue
                for c0 in range(0, width, 128):
                    cg = (lc + c0) // 128
                    buf[...] = rf[:, off + c0:off + c0 + 128]
                    for ref, dil in zip(res_refs, dils):
                        for rr in range(dil):
                            ref[a - 3, :, rr * D_GRP + cg * 128:rr * D_GRP + (cg + 1) * 128] = (
                                buf[pl.ds(rr, TM // dil, stride=dil), :].astype(BF))

    row = pl.BlockSpec((TM, d), lambda m: (m, 0))
    ex = pl.BlockSpec((None, 1, d), lambda m: (m // per, 0, 0))
    return _call(
        body, "qkv_proj", (t // TM,),
        [row, pl.BlockSpec((1, d), lambda m: (0, 0)), ex, ex, _whole(w_in)],
        [row, pl.BlockSpec((6, TM, D_GRP), lambda m: (0, m, 0))]
        + [pl.BlockSpec((3, TM // dil, dil * D_GRP), lambda m: (0, m, 0)) for dil in dils],
        [_sds((t, d), BF), _sds((6, t, D_GRP), BF)] + [_sds((3, t // dil, dil * D_GRP), BF) for dil in dils],
        (x, g, sc, sh, w_in), scratch=[pltpu.VMEM((TM, 128), F32)], carry=carry)


def _col_pieces(j, wc):
    out, off = [], 0
    while off < wc:
        a, lc = divmod(j * wc + off, D_GRP)
        width = min(D_GRP - lc, wc - off)
        out.append((a, lc, off, width))
        off += width
    return out


def _chip_cols(g6_ref, j, wc):
    return jnp.concatenate([g6_ref[a, :, lc:lc + width] for a, lc, _, width in _col_pieces(j, wc)], axis=1)


def _mix_out(on_sb, on_dil, w_out, x, gt, seq):
    t, d = x.shape
    per = seq // TM

    def body(a_ref, b_ref, w_ref, x_ref, gt_ref, t_ref, xo_ref):
        tv = _nn(a_ref[...], w_ref[0:D_GRP, :]) + _nn(b_ref[...], w_ref[D_GRP:2 * D_GRP, :])
        t_ref[...] = tv
        xo_ref[...] = x_ref[...] + gt_ref[...] * tv

    row = pl.BlockSpec((TM, d), lambda m: (m, 0))
    half = pl.BlockSpec((TM, D_GRP), lambda m: (m, 0))
    return pl.pallas_call(
        body, name="mix_out", grid=(t // TM,),
        in_specs=[half, half, pl.BlockSpec((2 * D_GRP, d), lambda m: (0, 0)), row,
                  pl.BlockSpec((None, 1, d), lambda m: (m // per, 0, 0))],
        out_specs=[row, row],
        out_shape=[_sds((t, d), F32), _sds((t, d), F32)],
        compiler_params=_cp(1))(on_sb, on_dil, w_out, x, gt)


def _sb_masks():
    lane = lax.broadcasted_iota(jnp.int32, (1, 2 * HEAD_DIM), 1)
    hm0 = lane < HEAD_DIM
    rel = lax.broadcasted_iota(jnp.int32, (TQ, KB), 0) - lax.broadcasted_iota(jnp.int32, (TQ, KB), 1)
    kr = lax.broadcasted_iota(jnp.int32, (KB, KB), 0)
    kc = lax.broadcasted_iota(jnp.int32, (KB, KB), 1)
    return hm0, rel, kr, kc


def _headnorm_pair(o, gv, hm0):
    o2 = o * o
    ms0 = jnp.sum(jnp.where(hm0, o2, 0.0), axis=-1, keepdims=True) * (1.0 / HEAD_DIM)
    ms1 = jnp.sum(jnp.where(hm0, 0.0, o2), axis=-1, keepdims=True) * (1.0 / HEAD_DIM)
    r = jnp.where(hm0, lax.rsqrt(ms0 + EPS), lax.rsqrt(ms1 + EPS))
    return (o * r) * gv


SB_DEAD = -104.0


def _alive(c_l):
    return (jnp.max(c_l) > SB_DEAD).astype(jnp.int32)


def _sb_fwd(qkv6, g_sb, nb, seq, carry=None):
    nq = seq // TQ

    def body(q_ref, k_ref, v_ref, g_ref, o_ref, on_ref):
        qi = pl.program_id(2)
        hm0, rel, kr, kc = _sb_masks()
        upper = (kr > kc).astype(BF)
        qv = q_ref[...]
        qhs = [jnp.where(hm0, qv, jnp.zeros_like(qv)), jnp.where(hm0, jnp.zeros_like(qv), qv)]

        def block(kj, causal, c_ls, accs):
            ks = pl.multiple_of(kj * KB, KB)
            kb = k_ref[pl.ds(ks, KB), :]
            vb = v_ref[pl.ds(ks, KB), :]
            new_c, new_acc = [], []
            for qh, c_l, acc in zip(qhs, c_ls, accs):
                z = _nt(qh, kb) * SCALE
                sp = _softplus(z)
                ln = -sp if causal is None else jnp.where(causal, -sp, 0.0)
                suf = _nn2(ln, upper)
                w = jnp.exp((z - sp) + (suf + c_l))
                if causal is not None:
                    w = jnp.where(causal, w, 0.0)
                new_acc.append(acc + _nn(w.astype(BF), vb))
                new_c.append(c_l + (suf[:, 0:1] + ln[:, 0:1]))
            return new_c, new_acc

        zc = jnp.zeros((TQ, 1), F32)
        za = jnp.zeros((TQ, 2 * HEAD_DIM), F32)
        c_ls, accs = block(qi, rel > 0, [zc, zc], [za, za])

        def cond(carry):
            return jnp.logical_and(carry[0] <= qi, carry[1] > 0)

        def kbody(carry):
            it, _, c0, c1, a0, a1 = carry
            (c0, c1), (a0, a1) = block(qi - it, None, [c0, c1], [a0, a1])
            return it + 1, jnp.maximum(_alive(c0), _alive(c1)), c0, c1, a0, a1

        init = (jnp.int32(1), jnp.maximum(_alive(c_ls[0]), _alive(c_ls[1])), c_ls[0], c_ls[1], accs[0], accs[1])
        outs = lax.while_loop(cond, kbody, init)[4:]
        o = jnp.where(hm0, outs[0], outs[1])
        o_ref[...] = o
        on_ref[...] = _headnorm_pair(o, g_ref[...], hm0).astype(BF)

    w = 2 * HEAD_DIM
    full = lambda i: pl.BlockSpec((None, None, seq, w), lambda b, hp, q: (i, b, 0, hp))
    qblk = pl.BlockSpec((None, None, TQ, w), lambda b, hp, q: (0, b, q, hp))
    oblk = pl.BlockSpec((None, TQ, w), lambda b, hp, q: (b, q, hp))
    return _call(
        body, "sb_fwd", (nb, N_HEADS // 2, nq),
        [qblk, full(1), full(2), pl.BlockSpec((1, w), lambda b, hp, q: (0, hp))],
        [oblk, oblk],
        [_sds((nb, seq, D_GRP), F32), _sds((nb, seq, D_GRP), BF)],
        (qkv6, qkv6, qkv6, g_sb), carry=carry)


def _sb_bwd(qkv6, do, nb, seq, carry=None):
    nq = seq // TQ
    nk = seq // KB

    def body(q_ref, k_ref, v_ref, do_ref, out_ref, dk_acc, dv_acc, g_st, s_st):
        qi = pl.program_id(2)
        hm0, rel, kr, kc = _sb_masks()
        upper = (kr > kc).astype(BF)
        lower = (kr < kc).astype(BF)

        @pl.when(qi == 0)
        def _():
            dk_acc[...] = jnp.zeros_like(dk_acc)
            dv_acc[...] = jnp.zeros_like(dv_acc)

        qv = q_ref[...]
        dov = do_ref[...]
        qhs = [jnp.where(hm0, qv, jnp.zeros_like(qv)), jnp.where(hm0, jnp.zeros_like(qv), qv)]
        dohs = [jnp.where(hm0, dov, 0.0).astype(BF), jnp.where(hm0, 0.0, dov).astype(BF)]

        def weights(kj, causal, c_ls):
            ks = pl.multiple_of(kj * KB, KB)
            kb = k_ref[pl.ds(ks, KB), :]
            vb = v_ref[pl.ds(ks, KB), :]
            new_c, dv = [], None
            for hh, (qh, doh, c_l) in enumerate(zip(qhs, dohs, c_ls)):
                z = _nt(qh, kb) * SCALE
                sp = _softplus(z)
                ln = -sp if causal is None else jnp.where(causal, -sp, 0.0)
                suf = _nn2(ln, upper)
                lsz = z - sp
                w = jnp.exp(lsz + (suf + c_l))
                if causal is not None:
                    w = jnp.where(causal, w, 0.0)
                g_st[hh, kj] = w * _nt(doh, vb)
                s_st[hh, kj] = jnp.exp(lsz)
                part = _tn(w.astype(BF), doh)
                dv = part if dv is None else dv + part
                new_c.append(c_l + (suf[:, 0:1] + ln[:, 0:1]))
            dv_acc[pl.ds(ks, KB), :] += dv
            return new_c

        zc = jnp.zeros((TQ, 1), F32)
        c_ls = weights(qi, rel > 0, [zc, zc])

        def acond(carry):
            return jnp.logical_and(carry[0] <= qi, carry[1] > 0)

        def abody(carry):
            it, _, c0, c1 = carry
            c0, c1 = weights(qi - it, None, [c0, c1])
            return it + 1, jnp.maximum(_alive(c0), _alive(c1)), c0, c1

        n_used = lax.while_loop(
            acond, abody, (jnp.int32(1), jnp.maximum(_alive(c_ls[0]), _alive(c_ls[1])), c_ls[0], c_ls[1]))[0]

        def grads(kj, causal, c_gs, dqs):
            ks = pl.multiple_of(kj * KB, KB)
            kb = k_ref[pl.ds(ks, KB), :]
            new_c, new_dq, dk = [], [], None
            for hh, (qh, c_g, dq) in enumerate(zip(qhs, c_gs, dqs)):
                g = g_st[hh, kj]
                sig = s_st[hh, kj]
                pre = _nn2(g, lower)
                dz = g * (1.0 - sig) - sig * (pre + c_g)
                if causal is not None:
                    dz = jnp.where(causal, dz, 0.0)
                dzb = (dz * SCALE).astype(BF)
                part = _tn(dzb, qh)
                dk = part if dk is None else dk + part
                new_dq.append(dq + _nn(dzb, kb))
                new_c.append(c_g + (pre[:, KB - 1:KB] + g[:, KB - 1:KB]))
            dk_acc[pl.ds(ks, KB), :] += dk
            return new_c, new_dq

        za = jnp.zeros((TQ, 2 * HEAD_DIM), F32)

        def bbody(kj, carry):
            (c0, c1), (d0, d1) = grads(kj, None, carry[:2], carry[2:])
            return c0, c1, d0, d1

        c0, c1, d0, d1 = lax.fori_loop(qi - n_used + 1, qi, bbody, (zc, zc, za, za))
        _, dqs = grads(qi, rel > 0, [c0, c1], [d0, d1])
        dq = jnp.where(hm0, dqs[0], dqs[1])
        out_ref[0, pl.ds(pl.multiple_of(qi * TQ, TQ), TQ), :] = dq.astype(BF)

        @pl.when(qi == nq - 1)
        def _():
            out_ref[1] = dk_acc[...].astype(BF)
            out_ref[2] = dv_acc[...].astype(BF)

    w = 2 * HEAD_DIM
    full = lambda i: pl.BlockSpec((None, None, seq, w), lambda b, hp, q: (i, b, 0, hp))
    qblk = pl.BlockSpec((None, None, TQ, w), lambda b, hp, q: (0, b, q, hp))
    oblk = pl.BlockSpec((None, TQ, w), lambda b, hp, q: (b, q, hp))
    return _call(
        body, "sb_bwd", (nb, N_HEADS // 2, nq),
        [qblk, full(1), full(2), oblk],
        [pl.BlockSpec((3, None, seq, w), lambda b, hp, q: (0, b, 0, hp))],
        [_sds((6, nb, seq, D_GRP), BF)], (qkv6, qkv6, qkv6, do),
        scratch=[pltpu.VMEM((seq, w), F32), pltpu.VMEM((seq, w), F32),
                 pltpu.VMEM((2, nk, TQ, KB), F32), pltpu.VMEM((2, nk, TQ, KB), F32)],
        carry=carry)


def _t5_bucket(n):
    max_exact = N_BUCKETS // 2
    nf = np.maximum(n, 1).astype(np.float32)
    large = max_exact + (np.log(nf / max_exact) / math.log(MAX_DISTANCE / max_exact)
                         * (N_BUCKETS - max_exact)).astype(np.int32)
    large = np.minimum(large, N_BUCKETS - 1)
    return np.where(n < max_exact, n, large).astype(np.int32)


def _bucket_map(dilation):
    step = BLOCK + np.arange(BLOCK)[:, None] - np.arange(2 * BLOCK)[None, :]
    return _t5_bucket(np.clip(step, 0, N_STEPS) * dilation)


GRP_HEADS = 4
GRP_W = GRP_HEADS * HEAD_DIM


def _dil_masks():
    lane = lax.broadcasted_iota(jnp.int32, (1, GRP_W), 1)
    heads = [jnp.logical_and(lane >= HEAD_DIM * i, lane < HEAD_DIM * (i + 1)) for i in range(GRP_HEADS)]
    iq = lax.broadcasted_iota(jnp.int32, (BLOCK, BLOCK), 0)
    ik = lax.broadcasted_iota(jnp.int32, (BLOCK, BLOCK), 1)
    return heads, ik <= iq, ik >= iq


def _dil_rows(n):
    rs = pl.multiple_of(n * BLOCK, BLOCK)
    ps = pl.multiple_of(jnp.maximum(n - 1, 0) * BLOCK, BLOCK)
    return pl.ds(rs, BLOCK), pl.ds(ps, BLOCK)


def _dil_probs(qh, kc, kp, b_ref, hh, valid_c, valid_p):
    zc = _nt(qh, kc) * SCALE + b_ref[hh, :, BLOCK:2 * BLOCK]
    zp = _nt(qh, kp) * SCALE + b_ref[hh, :, 0:BLOCK]
    zc = jnp.where(valid_c, zc, NEG_INF)
    zp = jnp.where(valid_p, zp, NEG_INF)
    m = jnp.maximum(jnp.max(zc, axis=-1, keepdims=True), jnp.max(zp, axis=-1, keepdims=True))
    ec = jnp.exp(zc - m)
    ep = jnp.exp(zp - m)
    den = jnp.sum(ec, axis=-1, keepdims=True) + jnp.sum(ep, axis=-1, keepdims=True)
    return ec, ep, den, m


def _dil_fwd(qkv6r, base, bias, nb, sub_len, dilation):
    n_blk = sub_len // BLOCK

    def body(q_ref, k_ref, v_ref, b_ref, o_ref, l_ref):
        heads, valid_c, valid_p0 = _dil_masks()

        def nbody(n, carry):
            cur, prev = _dil_rows(n)
            valid_p = jnp.logical_and(valid_p0, n > 0)
            for gi in range(N_HEADS // GRP_HEADS):
                lanes = slice(gi * GRP_W, (gi + 1) * GRP_W)
                qv, kc, kp = q_ref[cur, lanes], k_ref[cur, lanes], k_ref[prev, lanes]
                vc, vp = v_ref[cur, lanes], v_ref[prev, lanes]
                o = jnp.zeros((BLOCK, GRP_W), F32)
                lse = jnp.zeros((BLOCK, GRP_W), F32)
                for i, hm in enumerate(heads):
                    qh = jnp.where(hm, qv, jnp.zeros_like(qv))
                    ec, ep, den, m = _dil_probs(qh, kc, kp, b_ref, gi * GRP_HEADS + i, valid_c, valid_p)
                    o = jnp.where(hm, (_nn(ec.astype(BF), vc) + _nn(ep.astype(BF), vp)) / den, o)
                    lse = jnp.where(hm, m + jnp.log(den), lse)
                o_ref[cur, lanes] = o
                l_ref[cur, lanes] = lse
            return carry

        lax.fori_loop(0, n_blk, nbody, 0)

    seqblk = lambda i: pl.BlockSpec((None, None, sub_len, D_GRP), lambda b, r: (i, b, 0, r))
    oblk = pl.BlockSpec((None, sub_len, D_GRP), lambda b, r: (b, 0, r))
    shp = _sds((nb, sub_len, dilation * D_GRP), F32)
    return pl.pallas_call(
        body, name="dil_fwd_%d" % dilation, grid=(nb, dilation),
        in_specs=[seqblk(base), seqblk(base + 1), seqblk(base + 2), _whole(bias)],
        out_specs=[oblk, oblk], out_shape=[shp, shp],
        compiler_params=_cp(2))(qkv6r, qkv6r, qkv6r, bias)


def _dil_bwd(qkv6r, base, bias, do_c, dd_c, nb, sub_len, dilation, carry=None):
    n_blk = sub_len // BLOCK

    def body(q_ref, k_ref, v_ref, b_ref, do_ref, dd_ref, out_ref, a_ref):
        heads, valid_c, valid_p0 = _dil_masks()
        first = jnp.logical_and(pl.program_id(0) == 0, pl.program_id(1) == 0)

        @pl.when(first)
        def _():
            a_ref[...] = jnp.zeros_like(a_ref)

        out_ref[1] = jnp.zeros((sub_len, D_GRP), F32)
        out_ref[2] = jnp.zeros((sub_len, D_GRP), F32)

        def nbody(n, carry):
            cur, prev = _dil_rows(n)
            valid_p = jnp.logical_and(valid_p0, n > 0)
            for gi in range(N_HEADS // GRP_HEADS):
                lanes = slice(gi * GRP_W, (gi + 1) * GRP_W)
                qv, kc, kp = q_ref[cur, lanes], k_ref[cur, lanes], k_ref[prev, lanes]
                vc, vp = v_ref[cur, lanes], v_ref[prev, lanes]
                dov, ddv = do_ref[cur, lanes], dd_ref[cur, lanes]
                dq = jnp.zeros((BLOCK, GRP_W), F32)
                dkc = jnp.zeros((BLOCK, GRP_W), F32)
                dkp = jnp.zeros((BLOCK, GRP_W), F32)
                dvc = jnp.zeros((BLOCK, GRP_W), F32)
                dvp = jnp.zeros((BLOCK, GRP_W), F32)
                for i, hm in enumerate(heads):
                    h = gi * GRP_HEADS + i
                    qh = jnp.where(hm, qv, jnp.zeros_like(qv))
                    doh = jnp.where(hm, dov, 0.0).astype(BF)
                    ddh = jnp.sum(jnp.where(hm, ddv, 0.0), axis=-1, keepdims=True) * (1.0 / HEAD_DIM)
                    ec, ep, den, _ = _dil_probs(qh, kc, kp, b_ref, h, valid_c, valid_p)
                    inv = 1.0 / den
                    pc = ec * inv
                    pp = ep * inv
                    dzc = pc * (_nt(doh, vc) + ddh)
                    dzp = pp * (_nt(doh, vp) + ddh)
                    a_ref[h, :, BLOCK:2 * BLOCK] += dzc
                    a_ref[h, :, 0:BLOCK] += dzp
                    dzcb = (dzc * SCALE).astype(BF)
                    dzpb = (dzp * SCALE).astype(BF)
                    dq = jnp.where(hm, _nn(dzcb, kc) + _nn(dzpb, kp), dq)
                    dkc = dkc + _tn(dzcb, qh)
                    dkp = dkp + _tn(dzpb, qh)
                    dvc = dvc + _tn(pc.astype(BF), doh)
                    dvp = dvp + _tn(pp.astype(BF), doh)
                out_ref[0, cur, lanes] = dq
                out_ref[1, cur, lanes] += dkc
                out_ref[1, prev, lanes] += dkp
                out_ref[2, cur, lanes] += dvc
                out_ref[2, prev, lanes] += dvp
            return carry

        lax.fori_loop(0, n_blk, nbody, 0)

    seqblk = lambda i: pl.BlockSpec((None, None, sub_len, D_GRP), lambda b, r: (i, b, 0, r))
    oblk = pl.BlockSpec((None, sub_len, D_GRP), lambda b, r: (b, 0, r))
    return _call(
        body, "dil_bwd_%d" % dilation, (nb, dilation),
        [seqblk(base), seqblk(base + 1), seqblk(base + 2), _whole(bias), oblk, oblk],
        [pl.BlockSpec((3, None, sub_len, D_GRP), lambda b, r: (0, b, 0, r)),
         pl.BlockSpec((N_HEADS, BLOCK, 2 * BLOCK), lambda b, r: (0, 0, 0))],
        [_sds((3, nb, sub_len, dilation * D_GRP), F32), _sds((N_HEADS, BLOCK, 2 * BLOCK), F32)],
        (qkv6r, qkv6r, qkv6r, bias, do_c, dd_c), carry=carry)


def _group_ones():
    idx = np.arange(D_GRP) // HEAD_DIM
    return jnp.asarray((idx[:, None] == idx[None, :]).astype(np.float32), dtype=BF)


def _dil_alphas(l1, l4, l16):
    mx = jnp.maximum(jnp.maximum(l1, l4), l16)
    e1 = jnp.exp(l1 - mx)
    e4 = jnp.exp(l4 - mx)
    e16 = jnp.exp(l16 - mx)
    den = e1 + e4 + e16
    return e1 / den, e4 / den, e16 / den


def _residue_spec(dil):
    return pl.BlockSpec((TM // dil, dil * D_GRP), lambda m: (m, 0))


def _from_residue(src, dil, cg, buf):
    if dil == 1:
        return src[:, cg * 128:(cg + 1) * 128]
    for r in range(dil):
        buf[pl.ds(r, TM // dil, stride=dil), :] = src[:, r * D_GRP + cg * 128:r * D_GRP + (cg + 1) * 128]
    return buf[...]


def _to_residue(dst, dil, cg, buf, val):
    if dil == 1:
        dst[:, cg * 128:(cg + 1) * 128] = val
        return
    buf[...] = val
    for r in range(dil):
        dst[:, r * D_GRP + cg * 128:r * D_GRP + (cg + 1) * 128] = buf[pl.ds(r, TM // dil, stride=dil), :]


def _pair_sum(x, hm0):
    s0 = jnp.sum(jnp.where(hm0, x, 0.0), axis=-1, keepdims=True)
    s1 = jnp.sum(jnp.where(hm0, 0.0, x), axis=-1, keepdims=True)
    return jnp.where(hm0, s0, s1)


def _dil_comb(os, ls, g_dil):
    t = os[0].shape[0]
    dils = [dil for _, dil in DIL_CONFIGS]

    def body(o1, l1, o4, l4, o16, l16, g_ref, o_ref, on_ref, b0, b1, b2, b3):
        hm0 = lax.broadcasted_iota(jnp.int32, (1, 128), 1) < HEAD_DIM
        for cg in range(D_GRP // 128):
            lanes = slice(cg * 128, (cg + 1) * 128)
            ov = [_from_residue(src, dil, cg, buf) for src, dil, buf in zip((o1, o4, o16), dils, (None, b0, b1))]
            lv = [_from_residue(src, dil, cg, buf) for src, dil, buf in zip((l1, l4, l16), dils, (None, b2, b3))]
            a1, a4, a16 = _dil_alphas(*lv)
            o = a1 * ov[0] + a4 * ov[1] + a16 * ov[2]
            o_ref[:, lanes] = o
            on_ref[:, lanes] = _headnorm_pair(o, g_ref[:, lanes], hm0).astype(BF)

    blk = pl.BlockSpec((TM, D_GRP), lambda m: (m, 0))
    specs = [_residue_spec(dil) for dil in dils for _ in range(2)]
    return pl.pallas_call(
        body, name="dil_comb", grid=(t // TM,),
        in_specs=specs + [pl.BlockSpec((1, D_GRP), lambda m: (0, 0))],
        out_specs=[blk, blk],
        out_shape=[_sds((t, D_GRP), F32), _sds((t, D_GRP), BF)],
        scratch_shapes=[pltpu.VMEM((TM, 128), F32)] * 4,
        compiler_params=_cp(1))(os[0], ls[0], os[1], ls[1], os[2], ls[2], g_dil)


def _dil_comb_bwd(do, os, ls):
    t = do.shape[0]
    dils = [dil for _, dil in DIL_CONFIGS]

    def body(do_ref, o1, l1, o4, l4, o16, l16, d1, d4, d16, e1, e4, e16, b0, b1, b2, b3):
        hm0 = lax.broadcasted_iota(jnp.int32, (1, 128), 1) < HEAD_DIM
        for cg in range(D_GRP // 128):
            dov = do_ref[:, cg * 128:(cg + 1) * 128]
            ov = [_from_residue(src, dil, cg, buf) for src, dil, buf in zip((o1, o4, o16), dils, (None, b0, b1))]
            lv = [_from_residue(src, dil, cg, buf) for src, dil, buf in zip((l1, l4, l16), dils, (None, b2, b3))]
            al = _dil_alphas(*lv)
            sbar = al[0] * _pair_sum(dov * ov[0], hm0)
            for a_c, o_c in zip(al[1:], ov[1:]):
                sbar = sbar + a_c * _pair_sum(dov * o_c, hm0)
            for a_c, dil, dref, eref in zip(al, dils, (d1, d4, d16), (e1, e4, e16)):
                _to_residue(dref, dil, cg, b0, a_c * dov)
                _to_residue(eref, dil, cg, b1, -a_c * sbar)

    specs = [_residue_spec(dil) for dil in dils]
    return pl.pallas_call(
        body, name="dil_comb_bwd", grid=(t // TM,),
        in_specs=[pl.BlockSpec((TM, D_GRP), lambda m: (m, 0))] + [sp for sp in specs for _ in range(2)],
        out_specs=specs + specs,
        out_shape=[_sds((t // dil, dil * D_GRP), F32) for dil in dils] * 2,
        scratch_shapes=[pltpu.VMEM((TM, 128), F32)] * 4,
        compiler_params=_cp(1))(do, os[0], ls[0], os[1], ls[1], os[2], ls[2])


def _dqkv_dil_sum(ds, dqkv6):
    t = dqkv6.shape[1]
    dils = [dil for _, dil in DIL_CONFIGS]

    def body(*refs):
        srcs, o_ref, acc = refs[:len(dils)], refs[len(dils) + 1], refs[len(dils) + 2]
        for a in range(3):
            for cg in range(D_GRP // 128):
                for src, dil in zip(srcs, dils):
                    for r in range(dil):
                        part = src[a, :, r * D_GRP + cg * 128:r * D_GRP + (cg + 1) * 128]
                        rows = pl.ds(r, TM // dil, stride=dil) if dil > 1 else slice(None)
                        if dil == dils[0]:
                            acc[rows, :] = part
                        else:
                            acc[rows, :] += part
                o_ref[a, :, cg * 128:(cg + 1) * 128] = acc[...].astype(BF)

    return pl.pallas_call(
        body, name="dqkv_dil_sum", grid=(t // TM,),
        in_specs=[pl.BlockSpec((3, TM // dil, dil * D_GRP), lambda m: (0, m, 0)) for dil in dils]
        + [pl.BlockSpec(memory_space=pl.ANY)],
        out_specs=pl.BlockSpec((3, TM, D_GRP), lambda m: (1, m, 0)),
        out_shape=_sds((6, t, D_GRP), BF), input_output_aliases={len(dils): 0},
        scratch_shapes=[pltpu.VMEM((TM, 128), F32)],
        compiler_params=_cp(1))(*ds, dqkv6)


def _relbias_grad(a_all, onehot):
    def body(a_ref, oh_ref, o_ref):
        acc = jnp.zeros((N_HEADS, N_BUCKETS), F32)
        for c in range(len(DIL_CONFIGS)):
            av = a_ref[c]
            hi = av.astype(BF)
            lo = (av - hi.astype(F32)).astype(BF)
            acc = acc + _nt(hi, oh_ref[c]) + _nt(lo, oh_ref[c])
        o_ref[...] = acc

    return pl.pallas_call(body, name="relbias_grad", out_shape=_sds((N_HEADS, N_BUCKETS), F32),
                          compiler_params=_cp())(a_all, onehot)


def _headnorm_bwd(dn, o, gv, mv):
    ms = _nn2(o * o, mv) * (1.0 / HEAD_DIM)
    r = lax.rsqrt(ms + EPS)
    nrm = o * r
    dg = jnp.sum(dn * nrm, axis=0, keepdims=True)
    dnn = dn * gv
    do = r * (dnn - nrm * (_nn2(dnn * nrm, mv) * (1.0 / HEAD_DIM)))
    return do, dg


def _mix_bwd_out(dx, gt, tv, w_out, o_sb, o_dil, on_sb, on_dil, g_sb, g_dil, ones_g, seq):
    t, d = dx.shape
    per = seq // TM
    nb = t // seq

    def body(dx_ref, gt_ref, t_ref, w_ref, osb, odl, onsb, ondl, gsb, gdl, m_ref,
             dosb, dodl, dgt_ref, dgsb, dgdl, dw_ref):
        m = pl.program_id(0)
        dxv = dx_ref[...]
        dt = (gt_ref[...] * dxv).astype(BF)
        _acc_rows(dgt_ref, jnp.sum(dxv * t_ref[...], axis=0, keepdims=True), m % per == 0)
        mv = m_ref[...]
        don_sb = _nt(dt, w_ref[0:D_GRP, :])
        don_dl = _nt(dt, w_ref[D_GRP:2 * D_GRP, :])
        do1, dg1 = _headnorm_bwd(don_sb, osb[...], gsb[...], mv)
        do2, dg2 = _headnorm_bwd(don_dl, odl[...], gdl[...], mv)
        dosb[...] = do1
        dodl[...] = do2
        _acc_rows(dgsb, dg1, m == 0)
        _acc_rows(dgdl, dg2, m == 0)
        p1 = _tn(onsb[...], dt)
        p2 = _tn(ondl[...], dt)

        @pl.when(m == 0)
        def _():
            dw_ref[0:D_GRP, :] = p1
            dw_ref[D_GRP:2 * D_GRP, :] = p2

        @pl.when(m != 0)
        def _():
            dw_ref[0:D_GRP, :] += p1
            dw_ref[D_GRP:2 * D_GRP, :] += p2

    row = pl.BlockSpec((TM, d), lambda m: (m, 0))
    half = pl.BlockSpec((TM, D_GRP), lambda m: (m, 0))
    ex = pl.BlockSpec((None, 1, d), lambda m: (m // per, 0, 0))
    gvec = pl.BlockSpec((1, D_GRP), lambda m: (0, 0))
    wblk = pl.BlockSpec((2 * D_GRP, d), lambda m: (0, 0))
    return pl.pallas_call(
        body, name="mix_bwd_out", grid=(t // TM,),
        in_specs=[row, ex, row, wblk, half, half, half, half, gvec, gvec,
                  pl.BlockSpec((D_GRP, D_GRP), lambda m: (0, 0))],
        out_specs=[half, half, ex, gvec, gvec, wblk],
        out_shape=[_sds((t, D_GRP), F32), _sds((t, D_GRP), F32), _sds((nb, 1, d), F32),
                   _sds((1, D_GRP), F32), _sds((1, D_GRP), F32), _sds((2 * D_GRP, d), F32)],
        compiler_params=_cp(1))(dx, gt, tv, w_out, o_sb, o_dil, on_sb, on_dil, g_sb, g_dil, ones_g)


def _dw_in(h, dqkv6, carry=None):
    t, d = h.shape
    wc = 6 * D_GRP // N_CHIPS

    def body(h_ref, g_ref, o_ref):
        kt = pl.program_id(0)
        hv = h_ref[...]
        for j in range(N_CHIPS):
            p = _tn(hv, _chip_cols(g_ref, j, wc))

            @pl.when(kt == 0)
            def _(p=p, j=j):
                o_ref[j, 0] = p

            @pl.when(kt != 0)
            def _(p=p, j=j):
                o_ref[j, 0] += p

    return _call(
        body, "dw_in", (t // TM,),
        [pl.BlockSpec((TM, d), lambda kt: (kt, 0)), pl.BlockSpec((6, TM, D_GRP), lambda kt: (0, kt, 0))],
        [pl.BlockSpec((N_CHIPS, 1, d, wc), lambda kt: (0, 0, 0, 0))],
        [_sds((N_CHIPS, 1, d, wc), F32)], (h, dqkv6), carry=carry)


def _mix_bwd_dh(dqkv6, w_in, x, g, sc, dxo, seq, carry=None):
    _, t, _ = dqkv6.shape
    d = x.shape[-1]
    wc = w_in.shape[-1]
    per = seq // TM
    nb = t // seq

    def body(g6_ref, w_ref, x_ref, g_ref, sc_ref, dxo_ref, dx_ref, dsh_ref, dsc_ref, dg_ref):
        m = pl.program_id(0)
        dh = _nt(_chip_cols(g6_ref, 0, wc), w_ref[0, 0])
        for j in range(1, N_CHIPS):
            dh = dh + _nt(_chip_cols(g6_ref, j, wc), w_ref[j, 0])
        dx, dsh, dsc, dg = _modnorm_bwd_tile(dh, x_ref[...], g_ref[...], sc_ref[...], dxo_ref[...])
        dx_ref[...] = dx
        _acc_rows(dsh_ref, dsh, m % per == 0)
        _acc_rows(dsc_ref, dsc, m % per == 0)
        _acc_rows(dg_ref, dg, m == 0)

    row = pl.BlockSpec((TM, d), lambda m: (m, 0))
    ex = pl.BlockSpec((None, 1, d), lambda m: (m // per, 0, 0))
    vec = pl.BlockSpec((1, d), lambda m: (0, 0))
    return _call(
        body, "mix_bwd_dh", (t // TM,),
        [pl.BlockSpec((6, TM, D_GRP), lambda m: (0, m, 0)), _whole(w_in), row, vec, ex, row],
        [row, ex, ex, vec],
        [_sds((t, d), F32), _sds((nb, 1, d), F32), _sds((nb, 1, d), F32), _sds((1, d), F32)],
        (dqkv6, w_in, x, g, sc, dxo), carry=carry)


def _ffn_down_loss(s, wd, x, gt, seq, coef, g, target):
    _, t, fs = s.shape
    d = x.shape[-1]
    per = seq // TM
    steps = t // TM

    def body(s_ref, w_ref, x_ref, gt_ref, g_ref, t_ref, f_ref, dx_ref, dg_ref, loss_ref, lacc):
        m = pl.program_id(0)
        f = _nn(s_ref[0], w_ref[0, 0])
        for j in range(1, N_CHIPS):
            f = f + _nn(s_ref[j], w_ref[j, 0])
        f_ref[...] = f
        xv = x_ref[...] + (coef * gt_ref[...]) * f
        gv = g_ref[...]
        r = lax.rsqrt(jnp.mean(xv * xv, axis=-1, keepdims=True) + EPS)
        n = xv * r
        err = n * gv - t_ref[...]
        dy = err * (1.0 / d)
        _acc_rows(dg_ref, jnp.sum(dy * n, axis=0, keepdims=True), m == 0)
        dn = dy * gv
        dx_ref[...] = r * (dn - n * jnp.mean(dn * n, axis=-1, keepdims=True))
        _acc_rows(lacc, jnp.sum(err * err, axis=0, keepdims=True), m == 0)

        @pl.when(m == steps - 1)
        def _():
            tot = jnp.sum(lacc[...], axis=-1, keepdims=True) * (0.5 / d)
            loss_ref[...] = jnp.broadcast_to(tot, (1, 128))

    row = pl.BlockSpec((TM, d), lambda m: (m, 0))
    vec = pl.BlockSpec((1, d), lambda m: (0, 0))
    return pl.pallas_call(
        body, name="ffn_down_loss", grid=(steps,),
        in_specs=[pl.BlockSpec((N_CHIPS, TM, fs), lambda m: (0, m, 0)), _whole(wd), row,
                  pl.BlockSpec((None, 1, d), lambda m: (m // per, 0, 0)), vec, row],
        out_specs=[row, row, vec, pl.BlockSpec((1, 128), lambda m: (0, 0))],
        out_shape=[_sds((t, d), F32), _sds((t, d), F32), _sds((1, d), F32), _sds((1, 128), F32)],
        scratch_shapes=[pltpu.VMEM((1, d), F32)],
        compiler_params=_cp(1))(s, wd, x, gt, g, target)


def _row_tile(rows, cols):
    best = rows
    for tr in range(8, rows + 1, 8):
        if rows % tr == 0 and tr * cols * 4 <= (1 << 20):
            best = tr
    if best * cols * 4 > (1 << 21):
        best = 8
    return best


def _adamw(w, g_arr, g_sel, m, v):
    rows, cols = w.shape
    tr = _row_tile(rows, cols)
    b1c = 1.0 - ADAM_B1 ** ADAM_STEP
    b2c = 1.0 - ADAM_B2 ** ADAM_STEP

    def body(w_ref, g_ref, m_ref, v_ref, go_ref, d_ref, mo_ref, vo_ref):
        gv = g_ref[...]
        mn = ADAM_B1 * m_ref[...] + (1.0 - ADAM_B1) * gv
        vn = ADAM_B2 * v_ref[...] + (1.0 - ADAM_B2) * (gv * gv)
        go_ref[...] = gv
        mo_ref[...] = mn
        vo_ref[...] = vn
        d_ref[...] = -ADAM_LR * ((mn / b1c) / (jnp.sqrt(vn / b2c) + ADAM_EPS) + ADAM_WD * w_ref[...])

    blk = pl.BlockSpec((tr, cols), lambda i: (i, 0))
    shp = _sds((rows, cols), F32)
    return pl.pallas_call(
        body, name="adamw", grid=(rows // tr,),
        in_specs=[blk, pl.BlockSpec((None, tr, cols), lambda i: (g_sel, i, 0)), blk, blk],
        out_specs=[blk] * 4, out_shape=[shp] * 4,
        compiler_params=_cp(1))(w, g_arr, m, v)


def _flip(v, bit):
    return 1 - v if bit else v


def _my_place():
    x, y, c = lax.axis_index("x"), lax.axis_index("y"), lax.axis_index("c")
    return x, y, c


class _Exchange:
    def __init__(self, operands, out_shape, aliases, sems, start, finish):
        self.operands, self.out_shape, self.aliases, self.sems = list(operands), list(out_shape), dict(aliases), list(sems)
        self.start, self.finish = start, finish


def _join(exchanges):
    exchanges = [e for e in exchanges if e is not None]
    if not exchanges:
        return None
    ops, outs, sems, aliases, spans = [], [], [], {}, []
    for e in exchanges:
        spans.append((len(ops), len(outs), len(sems), e))
        for i, j in e.aliases.items():
            aliases[len(ops) + i] = len(outs) + j
        ops += e.operands
        outs += e.out_shape
        sems += e.sems

    def run(which):
        def go(ins, res, sm):
            for io, oo, so, e in spans:
                getattr(e, which)(ins[io:io + len(e.operands)], res[oo:oo + len(e.out_shape)], sm[so:so + len(e.sems)])
        return go

    return _Exchange(ops, outs, aliases, sems, run("start"), run("finish"))


def _call(body, name, grid, in_specs, out_specs, out_shape, args, scratch=(), carry=None):
    in_specs, out_specs, out_shape, scratch = list(in_specs), list(out_specs), list(out_shape), list(scratch)
    if carry is None:
        return pl.pallas_call(body, name=name, grid=grid, in_specs=in_specs, out_specs=out_specs,
                              out_shape=out_shape, scratch_shapes=scratch,
                              compiler_params=_cp(len(grid)))(*args)
    n_in, n_out, n_s = len(in_specs), len(out_specs), len(scratch)
    c_in, c_out = len(carry.operands), len(carry.out_shape)
    any_spec = pl.BlockSpec(memory_space=pl.ANY)

    def wrapped(*refs):
        ins, cins = refs[:n_in], refs[n_in:n_in + c_in]
        o0 = n_in + c_in
        outs, couts = refs[o0:o0 + n_out], refs[o0 + n_out:o0 + n_out + c_out]
        s0 = o0 + n_out + c_out
        scr, sems = refs[s0:s0 + n_s], refs[s0 + n_s:]
        first = pl.program_id(0) == 0
        last = pl.program_id(0) == grid[0] - 1
        for ax in range(1, len(grid)):
            first = jnp.logical_and(first, pl.program_id(ax) == 0)
            last = jnp.logical_and(last, pl.program_id(ax) == grid[ax] - 1)

        @pl.when(first)
        def _():
            carry.start(cins, couts, sems)

        body(*ins, *outs, *scr)

        @pl.when(last)
        def _():
            carry.finish(cins, couts, sems)

    return pl.pallas_call(
        wrapped, name=name, grid=grid, in_specs=in_specs + [any_spec] * c_in,
        out_specs=out_specs + [any_spec] * c_out, out_shape=out_shape + carry.out_shape,
        scratch_shapes=scratch + carry.sems,
        input_output_aliases={n_in + i: n_out + j for i, j in carry.aliases.items()},
        compiler_params=_cp(len(grid)))(*args, *carry.operands)


def _whole_call(body, name, args, out_shape, scratch, carry=None):
    vm = pl.BlockSpec(memory_space=pltpu.VMEM)
    any_spec = pl.BlockSpec(memory_space=pl.ANY)
    out_shape, scratch = list(out_shape), list(scratch)
    n_in, n_out, n_s = len(args), len(out_shape), len(scratch)
    if carry is None:
        return pl.pallas_call(body, name=name, in_specs=[vm] * n_in, out_specs=[vm] * n_out, out_shape=out_shape,
                              scratch_shapes=scratch, compiler_params=_cp())(*args)
    c_in, c_out = len(carry.operands), len(carry.out_shape)

    def wrapped(*refs):
        ins, cins = refs[:n_in], refs[n_in:n_in + c_in]
        o0 = n_in + c_in
        outs, couts = refs[o0:o0 + n_out], refs[o0 + n_out:o0 + n_out + c_out]
        s0 = o0 + n_out + c_out
        scr, sems = refs[s0:s0 + n_s], refs[s0 + n_s:]
        carry.start(cins, couts, sems)
        body(*ins, *outs, *scr)
        carry.finish(cins, couts, sems)

    return pl.pallas_call(
        wrapped, name=name, in_specs=[vm] * n_in + [any_spec] * c_in, out_specs=[vm] * n_out + [any_spec] * c_out,
        out_shape=out_shape + carry.out_shape, scratch_shapes=scratch + carry.sems,
        input_output_aliases={n_in + i: n_out + j for i, j in carry.aliases.items()},
        compiler_params=_cp())(*args, *carry.operands)


def _alone(name, ex):
    any_spec = pl.BlockSpec(memory_space=pl.ANY)
    c_in, c_out = len(ex.operands), len(ex.out_shape)

    def body(*refs):
        ins, outs, sems = refs[:c_in], refs[c_in:c_in + c_out], refs[c_in + c_out:]
        ex.start(ins, outs, sems)
        ex.finish(ins, outs, sems)

    return pl.pallas_call(
        body, name=name, in_specs=[any_spec] * c_in, out_specs=[any_spec] * c_out, out_shape=ex.out_shape,
        scratch_shapes=ex.sems, input_output_aliases=ex.aliases, compiler_params=_cp())(*ex.operands)


def _ada_fwd(c_pad, w_ada, b_shard, carry=None):
    d = c_pad.shape[-1]
    cols = w_ada.shape[-1]
    chunk = 384

    def body(c_ref, w_ref, b_ref, call_ref, mod_ref, part, s1, r1, s2, r2):
        x, y, c = _my_place()
        dev = 4 * x + 2 * y + c
        chip = 2 * x + y
        call_ref[dev] = c_ref[...]

        def c_copy(k):
            px, py, pc = _flip(x, (k >> 2) & 1), _flip(y, (k >> 1) & 1), _flip(c, k & 1)
            return px, py, pc

        sends = []
        for k in range(1, N_DEV):
            px, py, pc = c_copy(k)
            cp = pltpu.make_async_remote_copy(src_ref=c_ref, dst_ref=call_ref.at[dev], send_sem=s1.at[k - 1],
                                              recv_sem=r1.at[k - 1], device_id=(px, py, pc), device_id_type=MESH)
            cp.start()
            sends.append(cp)
        for k in range(1, N_DEV):
            px, py, pc = c_copy(k)
            pltpu.make_async_remote_copy(src_ref=c_ref, dst_ref=call_ref.at[4 * px + 2 * py + pc],
                                         send_sem=s1.at[k - 1], recv_sem=r1.at[k - 1],
                                         device_id=(px, py, pc), device_id_type=MESH).wait_recv()
        for cp in sends:
            cp.wait_send()

        cs = call_ref[...].reshape(N_DEV * 8, d)
        sc = (cs * jax.nn.sigmoid(cs)).astype(BF)
        for n0 in range(0, cols, chunk):
            blk = _nn(sc, w_ref[:, n0:n0 + chunk].astype(BF)) + b_ref[:, n0:n0 + chunk]
            part[:, :, n0:n0 + chunk] = blk.reshape(N_DEV, 8, chunk)

        mod_ref[chip] = part[dev]
        sends = []
        for kk in range(1, N_CHIPS):
            px, py = _flip(x, (kk >> 1) & 1), _flip(y, kk & 1)
            cp = pltpu.make_async_remote_copy(src_ref=part.at[4 * px + 2 * py + c], dst_ref=mod_ref.at[chip],
                                              send_sem=s2.at[kk - 1], recv_sem=r2.at[kk - 1],
                                              device_id=(px, py, c), device_id_type=MESH)
            cp.start()
            sends.append(cp)
        for kk in range(1, N_CHIPS):
            px, py = _flip(x, (kk >> 1) & 1), _flip(y, kk & 1)
            pltpu.make_async_remote_copy(src_ref=part.at[dev], dst_ref=mod_ref.at[2 * px + py],
                                         send_sem=s2.at[kk - 1], recv_sem=r2.at[kk - 1],
                                         device_id=(px, py, c), device_id_type=MESH).wait_recv()
        for cp in sends:
            cp.wait_send()

    return _whole_call(
        body, "ada_fwd", (c_pad, w_ada, b_shard),
        [_sds((N_DEV, 8, d), F32), _sds((N_CHIPS, 8, cols), F32)],
        [pltpu.VMEM((N_DEV, 8, cols), F32),
         pltpu.SemaphoreType.DMA((N_DEV - 1,)), pltpu.SemaphoreType.DMA((N_DEV - 1,)),
         pltpu.SemaphoreType.DMA((N_CHIPS - 1,)), pltpu.SemaphoreType.DMA((N_CHIPS - 1,))], carry=carry)


def _ag_weights(bufs):
    n = len(bufs)

    def place():
        x, y, c = _my_place()
        others = [(_flip(x, (kk >> 1) & 1), _flip(y, kk & 1)) for kk in range(1, N_CHIPS)]
        return x, y, c, 2 * x + y, others

    def half(b, which):
        hr = bufs[b].shape[2] // 2
        return pl.ds(pl.multiple_of(which * hr, 16), hr)

    def ici(outs, sems, b, i, slot, x, y, c, px, py):
        rows = outs[b].at[slot, :, half(b, c), :]
        return pltpu.make_async_remote_copy(
            src_ref=rows, dst_ref=rows, send_sem=sems[0].at[3 * b + i], recv_sem=sems[1].at[3 * b + i],
            device_id=(px, py, c), device_id_type=MESH)

    def d2d(outs, sems, b, i, slot, x, y, c, which):
        rows = outs[b].at[slot, :, half(b, which), :]
        return pltpu.make_async_remote_copy(
            src_ref=rows, dst_ref=rows, send_sem=sems[2].at[3 * b + i], recv_sem=sems[3].at[3 * b + i],
            device_id=(x, y, 1 - c), device_id_type=MESH)

    def start(ins, outs, sems):
        x, y, c, chip, others = place()
        for b in range(n):
            for i, (px, py) in enumerate(others):
                ici(outs, sems, b, i, chip, x, y, c, px, py).start()

    def finish(ins, outs, sems):
        x, y, c, chip, others = place()
        for b in range(n):
            for i, (px, py) in enumerate(others):
                ici(outs, sems, b, i, 2 * px + py, x, y, c, px, py).wait_recv()
                d2d(outs, sems, b, i, 2 * px + py, x, y, c, c).start()
        for b in range(n):
            for i, (px, py) in enumerate(others):
                d2d(outs, sems, b, i, 2 * px + py, x, y, c, 1 - c).wait_recv()
        for b in range(n):
            for i, (px, py) in enumerate(others):
                ici(outs, sems, b, i, chip, x, y, c, px, py).wait_send()
                d2d(outs, sems, b, i, 2 * px + py, x, y, c, c).wait_send()

    return _Exchange(bufs, [_sds(s.shape, s.dtype) for s in bufs], {i: i for i in range(n)},
                     [pltpu.SemaphoreType.DMA((3 * n,))] * 4, start, finish)


def _rs_d2d(grads):
    n = len(grads)

    def copy(ins, outs, sems, b):
        x, y, c = _my_place()
        hr = grads[b].shape[2] // 2
        theirs = pl.ds(pl.multiple_of((1 - c) * hr, 8), hr)
        return pltpu.make_async_remote_copy(
            src_ref=ins[b].at[:, :, theirs, :], dst_ref=outs[b], send_sem=sems[0].at[b], recv_sem=sems[1].at[b],
            device_id=(x, y, 1 - c), device_id_type=MESH)

    def start(ins, outs, sems):
        for b in range(n):
            copy(ins, outs, sems, b).start()

    def finish(ins, outs, sems):
        for b in range(n):
            copy(ins, outs, sems, b).wait()

    return _Exchange(grads, [_sds(g.shape[:2] + (g.shape[2] // 2, g.shape[3]), F32) for g in grads], {},
                     [pltpu.SemaphoreType.DMA((n,))] * 2, start, finish)


def _add_halves(core, g, land):
    nchip, ng, rows, cols = g.shape
    hr = rows // 2
    tr = _row_tile(hr, cols)
    steps = hr // tr

    def body(core_ref, g_ref, l_ref, o_ref):
        del core_ref
        o_ref[...] = (g_ref[...] + l_ref[...]).astype(BF)

    return pl.pallas_call(
        body, name="add_halves",
        grid_spec=pltpu.PrefetchScalarGridSpec(
            num_scalar_prefetch=1, grid=(nchip, ng, steps),
            in_specs=[pl.BlockSpec((None, None, tr, cols), lambda j, a, i, cr: (j, a, cr[0] * steps + i, 0)),
                      pl.BlockSpec((None, None, tr, cols), lambda j, a, i, cr: (j, a, i, 0))],
            out_specs=pl.BlockSpec((None, None, tr, cols), lambda j, a, i, cr: (j, a, i, 0))),
        out_shape=_sds((nchip, ng, hr, cols), BF),
        compiler_params=_cp(3))(core, g, land)


def _rs_ici(parts):
    n = len(parts)

    def copies(ins, outs, sems):
        x, y, c = _my_place()
        chip = 2 * x + y
        for b in range(n):
            for kk in range(1, N_CHIPS):
                px, py = _flip(x, (kk >> 1) & 1), _flip(y, kk & 1)
                k = 3 * b + kk - 1
                send = pltpu.make_async_remote_copy(
                    src_ref=ins[b].at[2 * px + py], dst_ref=outs[b].at[chip],
                    send_sem=sems[0].at[k], recv_sem=sems[1].at[k], device_id=(px, py, c), device_id_type=MESH)
                slot = outs[b].at[2 * px + py]
                recv = pltpu.make_async_remote_copy(
                    src_ref=slot, dst_ref=slot, send_sem=sems[0].at[k], recv_sem=sems[1].at[k],
                    device_id=(px, py, c), device_id_type=MESH)
                yield send, recv

    def start(ins, outs, sems):
        for send, _ in copies(ins, outs, sems):
            send.start()

    def finish(ins, outs, sems):
        for send, recv in copies(ins, outs, sems):
            recv.wait_recv()
            send.wait_send()

    return _Exchange(parts, [_sds(p.shape, p.dtype) for p in parts], {},
                     [pltpu.SemaphoreType.DMA((3 * n,))] * 2, start, finish)


def _sum_chips(place, part, land):
    nchip, ng, hr, cols = land.shape
    tr = _row_tile(hr, cols)
    steps = hr // tr

    def body(place_ref, p_ref, l1, l2, l3, o_ref):
        del place_ref
        o_ref[...] = ((p_ref[...].astype(F32) + l1[...].astype(F32)) + l2[...].astype(F32)) + l3[...].astype(F32)

    def slot(k):
        return pl.BlockSpec((None, None, tr, cols), lambda a, i, pr: (jnp.bitwise_xor(pr[1], k), a, i, 0))

    return pl.pallas_call(
        body, name="sum_chips",
        grid_spec=pltpu.PrefetchScalarGridSpec(
            num_scalar_prefetch=1, grid=(ng, steps),
            in_specs=[slot(0), slot(1), slot(2), slot(3)],
            out_specs=pl.BlockSpec((None, tr, cols), lambda a, i, pr: (a, pr[0] * steps + i, 0))),
        out_shape=_sds((ng, 2 * hr, cols), F32),
        compiler_params=_cp(2))(place, part, land, land, land)


def _rs_final(bufs):
    n = len(bufs)

    def copy(outs, sems, b, which):
        x, y, c = _my_place()
        hr = bufs[b].shape[1] // 2
        rows = outs[b].at[:, pl.ds(pl.multiple_of((c if which == 0 else 1 - c) * hr, 8), hr), :]
        return pltpu.make_async_remote_copy(
            src_ref=rows, dst_ref=rows, send_sem=sems[0].at[b], recv_sem=sems[1].at[b],
            device_id=(x, y, 1 - c), device_id_type=MESH)

    def start(ins, outs, sems):
        for b in range(n):
            copy(outs, sems, b, 0).start()

    def finish(ins, outs, sems):
        for b in range(n):
            copy(outs, sems, b, 0).wait_send()
            copy(outs, sems, b, 1).wait_recv()

    return _Exchange(bufs, [_sds(h.shape, F32) for h in bufs], {i: i for i in range(n)},
                     [pltpu.SemaphoreType.DMA((n,))] * 2, start, finish)


def _small_sync(smalls, dmod_blk, c_all, carry=None):
    d = c_all.shape[-1]
    cols = dmod_blk.shape[-1]
    chunk = 384

    def body(sm_ref, dm_ref, c_ref, sum_ref, gw_ref, sm_all, dm_all, ssem, rsem):
        x, y, c = _my_place()
        dev = 4 * x + 2 * y + c
        chip = 2 * x + y
        sm_all[dev] = sm_ref[...]
        dm_all[dev] = dm_ref[chip]
        sends = []
        for k in range(1, N_DEV):
            px, py, pc = _flip(x, (k >> 2) & 1), _flip(y, (k >> 1) & 1), _flip(c, k & 1)
            a = pltpu.make_async_remote_copy(src_ref=sm_ref, dst_ref=sm_all.at[dev], send_sem=ssem.at[2 * (k - 1)],
                                             recv_sem=rsem.at[2 * (k - 1)], device_id=(px, py, pc),
                                             device_id_type=MESH)
            b = pltpu.make_async_remote_copy(src_ref=dm_ref.at[2 * px + py], dst_ref=dm_all.at[dev],
                                             send_sem=ssem.at[2 * (k - 1) + 1], recv_sem=rsem.at[2 * (k - 1) + 1],
                                             device_id=(px, py, pc), device_id_type=MESH)
            a.start()
            b.start()
            sends += [a, b]
        for k in range(1, N_DEV):
            px, py, pc = _flip(x, (k >> 2) & 1), _flip(y, (k >> 1) & 1), _flip(c, k & 1)
            pdev = 4 * px + 2 * py + pc
            pltpu.make_async_remote_copy(src_ref=sm_ref, dst_ref=sm_all.at[pdev], send_sem=ssem.at[2 * (k - 1)],
                                         recv_sem=rsem.at[2 * (k - 1)], device_id=(px, py, pc),
                                         device_id_type=MESH).wait_recv()
            pltpu.make_async_remote_copy(src_ref=dm_ref.at[chip], dst_ref=dm_all.at[pdev],
                                         send_sem=ssem.at[2 * (k - 1) + 1], recv_sem=rsem.at[2 * (k - 1) + 1],
                                         device_id=(px, py, pc), device_id_type=MESH).wait_recv()
        for cp in sends:
            cp.wait_send()

        tot = sm_all[0]
        for q in range(1, N_DEV):
            tot = tot + sm_all[q]
        sum_ref[...] = tot

        cs = c_ref[...].reshape(N_DEV * 8, d)
        sc = (cs * jax.nn.sigmoid(cs)).astype(BF)
        for n0 in range(0, cols, chunk):
            dmv = dm_all[:, :, n0:n0 + chunk].reshape(N_DEV * 8, chunk).astype(BF)
            gw_ref[:, n0:n0 + chunk] = _tn(sc, dmv)

    return _whole_call(
        body, "small_sync", (smalls, dmod_blk, c_all),
        [_sds(smalls.shape, F32), _sds((d, cols), F32)],
        [pltpu.VMEM((N_DEV,) + smalls.shape, F32), pltpu.VMEM((N_DEV, 8, cols), F32),
         pltpu.SemaphoreType.DMA((2 * (N_DEV - 1),)), pltpu.SemaphoreType.DMA((2 * (N_DEV - 1),))], carry=carry)


def _bucket_onehot():
    maps = np.stack([_bucket_map(dil).reshape(-1) for _, dil in DIL_CONFIGS])
    return (jnp.asarray(maps)[:, None, :] == jnp.arange(N_BUCKETS, dtype=jnp.int32)[None, :, None]).astype(BF)


def _dil_bias(rel_t, onehot):
    def body(r_ref, oh_ref, o_ref):
        rv = r_ref[...]
        hi = rv.astype(BF)
        lo = (rv - hi.astype(F32)).astype(BF)
        for c in range(len(DIL_CONFIGS)):
            o_ref[c] = _nn(hi, oh_ref[c]) + _nn(lo, oh_ref[c])

    return pl.pallas_call(body, name="dil_bias",
                          out_shape=_sds((len(DIL_CONFIGS), N_HEADS, BLOCK * 2 * BLOCK), F32),
                          compiler_params=_cp())(rel_t, onehot)


def _rowsum8(a):
    def body(a_ref, o_ref):
        o_ref[...] = jnp.sum(a_ref[...], axis=0, keepdims=True)

    return pl.pallas_call(body, name="rowsum8", out_shape=_sds((1, a.shape[1]), F32), compiler_params=_cp())(a)


def _local_step(x, mod, target, w, gains, rel_bias, place=None):
    nb, seq, d = x.shape
    t = nb * seq
    dist = place is not None
    core = place[0:1] if dist else None
    x0 = x.reshape(t, d)
    tgt = target.reshape(t, d)
    md = [mod[:, i:i + 1, :] for i in range(N_MOD)]
    sh1, sc1, gt1, sh2, sc2, gt2, sh3, sc3, gt3 = md
    g1, g2, g3 = gains["g_ffn1"], gains["g_mix"], gains["g_ffn2"]
    ones_g = _group_ones()

    def partial_sums(grads, lands):
        return [_add_halves(core, g, l) for g, l in zip(grads, lands)]

    def chip_sums(parts, lands):
        return [_sum_chips(place, p, l) for p, l in zip(parts, lands)]

    res = _ffn_up(x0, g1, sc1, sh1, w["gu1"], seq,
                  carry=_ag_weights([w["d1"], w["win"], w["wout"]]) if dist else None)
    h1, a1, u1, s1 = res[:4]
    wd1, w_in, w_out = res[4:] if dist else (w["d1"], w["win"], w["wout"])
    w_out2 = w_out.reshape(2 * D_GRP, d)
    f1, x1 = _ffn_down(s1, wd1, x0, gt1, seq, 0.5)

    h2, qkv6, qkv_r4, qkv_r16 = _qkv_proj(x1, g2, sc2, sh2, w_in, seq)
    qkv6b = qkv6.reshape(6, nb, seq, D_GRP)
    res = _sb_fwd(qkv6b, gains["g_sb_out"], nb, seq, carry=_ag_weights([w["gu2"], w["d2"]]) if dist else None)
    o_sb, on_sb = res[:2]
    wgu2, wd2 = res[2:] if dist else (w["gu2"], w["d2"])
    onehot = _bucket_onehot()
    bias = _dil_bias(rel_bias.T, onehot).reshape(len(DIL_CONFIGS), N_HEADS, BLOCK, 2 * BLOCK)
    o_cs, l_cs = [], []
    qkv_rs = [(qkv6b, 3), (qkv_r4, 0), (qkv_r16, 0)]
    for ci, (_, dil) in enumerate(DIL_CONFIGS):
        sub = seq // dil
        arr, base = qkv_rs[ci]
        arr = arr.reshape(base + 3, nb, sub, dil * D_GRP)
        qkv_rs[ci] = (arr, base)
        o_c, l_c = _dil_fwd(arr, base, bias[ci], nb, sub, dil)
        o_cs.append(o_c.reshape(t // dil, dil * D_GRP))
        l_cs.append(l_c.reshape(t // dil, dil * D_GRP))
    o_dil, on_dil = _dil_comb(o_cs, l_cs, gains["g_dil_out"])
    tmix, x2 = _mix_out(on_sb.reshape(t, D_GRP), on_dil, w_out2, x1, gt2, seq)

    h3, a3, u3, s3 = _ffn_up(x2, g3, sc3, sh3, wgu2, seq)
    f3, dx3, dg_final, loss = _ffn_down_loss(s3, wd2, x2, gt3, seq, 0.5, gains["g_final"], tgt)

    da3, du3, df3, dgt3 = _ffn_bwd_ds(dx3, gt3, f3, wd2, a3, u3, seq, 0.5)
    grads2 = [_ffn_bwd_w(h3, da3, du3, s3, df3)]
    res = _ffn_bwd_dh(da3, du3, wgu2, x2, g3, sc3, dx3, seq, carry=_rs_d2d(grads2) if dist else None)
    dx2, dsh3, dsc3, dg3 = res[:4]
    parts2 = partial_sums(grads2, res[4:]) if dist else None

    do_sb, do_dil, dgt2, dg_sb, dg_dil, dw_out = _mix_bwd_out(
        dx2, gt2, tmix, w_out2, o_sb.reshape(t, D_GRP), o_dil, on_sb.reshape(t, D_GRP), on_dil,
        gains["g_sb_out"], gains["g_dil_out"], ones_g, seq)
    dw_out = dw_out.reshape(N_CHIPS, 1, 2 * D_GRP // N_CHIPS, d)
    res = _sb_bwd(qkv6b, do_sb.reshape(nb, seq, D_GRP), nb, seq, carry=_rs_ici(parts2) if dist else None)
    dqkv6 = res[0]
    halves2 = chip_sums(parts2, res[1:]) if dist else None
    dcs = _dil_comb_bwd(do_dil, o_cs, l_cs)
    dsum, a_tiles = [], []
    for ci, (_, dil) in enumerate(DIL_CONFIGS):
        sub = seq // dil
        do_c = dcs[ci].reshape(nb, sub, dil * D_GRP)
        dd_c = dcs[3 + ci].reshape(nb, sub, dil * D_GRP)
        res = _dil_bwd(qkv_rs[ci][0], qkv_rs[ci][1], bias[ci], do_c, dd_c, nb, sub, dil,
                       carry=_rs_final(halves2) if dist and ci == 0 else None)
        if dist and ci == 0:
            grads2 = res[2:]
        dsum.append(res[0].reshape(3, t // dil, dil * D_GRP))
        a_tiles.append(res[1].reshape(N_HEADS, BLOCK * 2 * BLOCK))
    dqkv6 = _dqkv_dil_sum(dsum, dqkv6.reshape(6, t, D_GRP))
    drel = _relbias_grad(jnp.stack(a_tiles), onehot)
    dx1, dsh2, dsc2, dg2 = _mix_bwd_dh(dqkv6, w_in, x1, g2, sc2, dx2, seq)

    da1, du1, df1, dgt1 = _ffn_bwd_ds(dx1, gt1, f1, wd1, a1, u1, seq, 0.5)
    grads1 = [_ffn_bwd_w(h1, da1, du1, s1, df1)]
    res = _dw_in(h2, dqkv6, carry=_rs_d2d(grads1) if dist else None)
    grads_m = [res[0], dw_out]
    parts1 = partial_sums(grads1, res[1:]) if dist else None
    res = _ffn_bwd_dh(da1, du1, w["gu1"], x0, g1, sc1, dx1, seq,
                      carry=_join([_rs_ici(parts1), _rs_d2d(grads_m)]) if dist else None)
    dx0, dsh1, dsc1, dg1 = res[:4]
    pending = None
    if dist:
        pending = (chip_sums(parts1, res[4:5]), partial_sums(grads_m, res[5:7]))

    dmod = jnp.concatenate([dsh1, dsc1, dgt1, dsh2, dsc2, dgt2, dsh3, dsc3, dgt3], axis=1)
    return dict(grad_x=dx0.reshape(nb, seq, d), loss=loss[0, 0], dmod=dmod.reshape(nb, N_MOD * d),
                dffn1=grads1[0], dffn2=grads2[0], dwin=grads_m[0], dwout=grads_m[1], pending=pending,
                dg_ffn1=dg1, dg_mix=dg2, dg_ffn2=dg3, dg_final=dg_final, dg_sb=dg_sb, dg_dil=dg_dil,
                drel=drel.T)


_SMALL_ORDER = (("b_ada", N_MOD * 1024), ("g_ffn1", 1024), ("g_mix", 1024), ("g_ffn2", 1024), ("g_final", 1024),
                ("g_sb_out", D_GRP), ("g_dil_out", D_GRP), ("rel_bias", N_BUCKETS * N_HEADS))


def _pack_small(parts, extra=None):
    flat = [parts[name].reshape(-1).astype(F32) for name, _ in _SMALL_ORDER]
    used = sum(sz for _, sz in _SMALL_ORDER)
    pad = SMALL_ROWS * 128 - used
    tail = jnp.zeros((pad,), F32)
    if extra is not None:
        tail = tail.at[0].set(extra)
    return jnp.concatenate(flat + [tail]).reshape(SMALL_ROWS, 128)


def _unpack_small(packed, shapes):
    flat = packed.reshape(-1)
    out, off = {}, 0
    for name, sz in _SMALL_ORDER:
        out[name] = flat[off:off + sz].reshape(shapes[name])
        off += sz
    return out, flat[off]


def kernel(x, c, w_ada, b_ada, g_ffn1, w1_gate, w1_up, w1_down, g_mix, w_in, g_sb_out, g_dil_out, w_out, rel_bias, g_ffn2, w2_gate, w2_up, w2_down, g_final, loss_target, m_w_ada, m_b_ada, m_g_ffn1, m_w1_gate, m_w1_up, m_w1_down, m_g_mix, m_w_in, m_g_sb_out, m_g_dil_out, m_w_out, m_rel_bias, m_g_ffn2, m_w2_gate, m_w2_up, m_w2_down, m_g_final, v_w_ada, v_b_ada, v_g_ffn1, v_w1_gate, v_w1_up, v_w1_down, v_g_mix, v_w_in, v_g_sb_out, v_g_dil_out, v_w_out, v_rel_bias, v_g_ffn2, v_w2_gate, v_w2_up, v_w2_down, v_g_final):
    nb, seq, d = x.shape
    xi, yi, ci = lax.axis_index("x"), lax.axis_index("y"), lax.axis_index("c")
    chip = 2 * xi + yi
    ada_cols = w_ada.shape[-1]

    c_pad = jnp.zeros((8, d), F32).at[:nb].set(c)
    b_shard = lax.dynamic_slice(b_ada, (0, chip * ada_cols), (1, ada_cols))
    shards = dict(gu1=jnp.stack([w1_gate[0], w1_up[0]]), d1=w1_down, win=w_in, wout=w_out,
                  gu2=jnp.stack([w2_gate[0], w2_up[0]]), d2=w2_down)
    bufs = {k: lax.dynamic_update_slice(lax.empty((N_CHIPS,) + s.shape, BF), s.astype(BF)[None], (chip, 0, 0, 0))
            for k, s in shards.items()}
    c_all, mod_blk, bufs["gu1"] = _ada_fwd(c_pad, w_ada[0], b_shard, carry=_ag_weights([bufs["gu1"]]))
    mod = jnp.transpose(mod_blk[:, :nb, :], (1, 0, 2)).reshape(nb, N_MOD, d)

    gains = dict(g_ffn1=g_ffn1, g_mix=g_mix, g_ffn2=g_ffn2, g_final=g_final.reshape(1, d),
                 g_sb_out=g_sb_out.reshape(1, D_GRP), g_dil_out=g_dil_out.reshape(1, D_GRP))
    place = jnp.stack([ci, chip]).astype(jnp.int32)
    r = _local_step(x, mod, loss_target, bufs, gains, rel_bias, place)

    dmod = r["dmod"]
    dmod_pad = jnp.zeros((8, N_MOD * d), F32).at[:nb].set(dmod)
    dmod_blk = jnp.transpose(dmod_pad.reshape(8, N_CHIPS, ada_cols), (1, 0, 2))
    small_parts = dict(b_ada=_rowsum8(dmod_pad), g_ffn1=r["dg_ffn1"], g_mix=r["dg_mix"], g_ffn2=r["dg_ffn2"],
                       g_final=r["dg_final"], g_sb_out=r["dg_sb"], g_dil_out=r["dg_dil"], rel_bias=r["drel"])
    halves1, parts_m = r["pending"]
    res = _small_sync(_pack_small(small_parts, r["loss"]), dmod_blk, c_all,
                      carry=_join([_rs_final(halves1), _rs_ici(parts_m)]))
    small_sum, g_wada, gffn1 = res[:3]
    halves_m = [_sum_chips(place, p, l) for p, l in zip(parts_m, res[3:5])]
    gwin, gwout = _alone("rs_last", _rs_final(halves_m))
    gffn2 = r["dffn2"]

    small_w = dict(b_ada=b_ada, g_ffn1=g_ffn1, g_mix=g_mix, g_ffn2=g_ffn2, g_final=g_final,
                   g_sb_out=g_sb_out, g_dil_out=g_dil_out, rel_bias=rel_bias)
    small_m = dict(b_ada=m_b_ada, g_ffn1=m_g_ffn1, g_mix=m_g_mix, g_ffn2=m_g_ffn2, g_final=m_g_final,
                   g_sb_out=m_g_sb_out, g_dil_out=m_g_dil_out, rel_bias=m_rel_bias)
    small_v = dict(b_ada=v_b_ada, g_ffn1=v_g_ffn1, g_mix=v_g_mix, g_ffn2=v_g_ffn2, g_final=v_g_final,
                   g_sb_out=v_g_sb_out, g_dil_out=v_g_dil_out, rel_bias=v_rel_bias)
    shapes = {k: v.shape for k, v in small_w.items()}
    sg, sd, sm, sv = _adamw(_pack_small(small_w), small_sum.reshape(1, SMALL_ROWS, 128), 0,
                            _pack_small(small_m), _pack_small(small_v))
    sg, loss = _unpack_small(sg, shapes)
    sd, _ = _unpack_small(sd, shapes)
    sm, _ = _unpack_small(sm, shapes)
    sv, _ = _unpack_small(sv, shapes)

    big = {}

    def upd(name, w, g_arr, sel, m, v, transposed=False):
        swap = (lambda a: jnp.swapaxes(a, -1, -2)) if transposed else (lambda a: a)
        w2, m2, v2 = [swap(a)[0] for a in (w, m, v)]
        big[name] = [swap(a[None]) for a in _adamw(w2, g_arr, sel, m2, v2)]

    upd("w_ada", w_ada, g_wada.reshape(1, d, ada_cols), 0, m_w_ada, v_w_ada)
    upd("w1_gate", w1_gate, gffn1, 0, m_w1_gate, v_w1_gate, transposed=True)
    upd("w1_up", w1_up, gffn1, 1, m_w1_up, v_w1_up, transposed=True)
    upd("w1_down", w1_down, gffn1, 2, m_w1_down, v_w1_down)
    upd("w_in", w_in, gwin, 0, m_w_in, v_w_in)
    upd("w_out", w_out, gwout, 0, m_w_out, v_w_out)
    upd("w2_gate", w2_gate, gffn2, 0, m_w2_gate, v_w2_gate, transposed=True)
    upd("w2_up", w2_up, gffn2, 1, m_w2_up, v_w2_up, transposed=True)
    upd("w2_down", w2_down, gffn2, 2, m_w2_down, v_w2_down)

    names = ["w_ada", "b_ada", "g_ffn1", "w1_gate", "w1_up", "w1_down", "g_mix", "w_in", "g_sb_out", "g_dil_out",
             "w_out", "rel_bias", "g_ffn2", "w2_gate", "w2_up", "w2_down", "g_final"]
    outs = [loss, r["grad_x"]]
    for k, small in enumerate((sg, sd, sm, sv)):
        for name in names:
            outs.append(big[name][k] if name in big else small[name])
    return tuple(outs)
```

```python
import functools
import math

import numpy as np
import jax
import jax.numpy as jnp
from jax import lax
from jax.experimental import pallas as pl
from jax.experimental.pallas import tpu as pltpu

F32 = jnp.float32
BF = jnp.bfloat16
MESH = pl.DeviceIdType.MESH

HEAD_DIM = 64
N_HEADS = 8
D_GRP = N_HEADS * HEAD_DIM
DIL_CONFIGS = ((128, 1), (512, 4), (2048, 16))
N_STEPS = 128
BLOCK = 128
N_BUCKETS = 32
MAX_DISTANCE = 2048
N_MOD = 9
EPS = 1e-6
NEG_INF = -1e30
SCALE = HEAD_DIM ** -0.5

ADAM_LR = 0.001
ADAM_B1 = 0.9
ADAM_B2 = 0.999
ADAM_EPS = 1e-08
ADAM_WD = 0.01
ADAM_STEP = 10

N_CHIPS = 4
N_DEV = 8
VMEM_LIMIT = 56 * 1024 * 1024
TM = 512
TQ = 256
KB = 256
SMALL_ROWS = 120


def _cp(n_axes=0, **kw):
    sem = ("arbitrary",) * n_axes if n_axes else None
    return pltpu.CompilerParams(dimension_semantics=sem, vmem_limit_bytes=VMEM_LIMIT, **kw)


def _nn(a, b):
    return jnp.dot(a, b, preferred_element_type=F32)


def _nt(a, b):
    return lax.dot_general(a, b, (((1,), (1,)), ((), ())), preferred_element_type=F32)


def _tn(a, b):
    return lax.dot_general(a, b, (((0,), (0,)), ((), ())), preferred_element_type=F32)


def _nn2(x, m):
    hi = x.astype(BF)
    lo = (x - hi.astype(F32)).astype(BF)
    return _nn(hi, m) + _nn(lo, m)


def _softplus(z):
    return jnp.maximum(z, 0.0) + jnp.log1p(jnp.exp(-jnp.abs(z)))


def _sds(shape, dtype):
    return jax.ShapeDtypeStruct(shape, dtype)


def _whole(a):
    nd = a.ndim
    return pl.BlockSpec(a.shape, lambda *_: (0,) * nd, pipeline_mode=pl.Buffered(1))


def _modnorm_bwd_tile(dh, xv, gv, scv, dxo):
    r = lax.rsqrt(jnp.mean(xv * xv, axis=-1, keepdims=True) + EPS)
    n = xv * r
    ng = n * gv
    dsh = jnp.sum(dh, axis=0, keepdims=True)
    dsc = jnp.sum(dh * ng, axis=0, keepdims=True)
    dy = dh * (1.0 + scv)
    dg = jnp.sum(dy * n, axis=0, keepdims=True)
    dn = dy * gv
    dx = dxo + r * (dn - n * jnp.mean(dn * n, axis=-1, keepdims=True))
    return dx, dsh, dsc, dg


def _acc_rows(ref, val, first):
    @pl.when(first)
    def _():
        ref[...] = val

    @pl.when(jnp.logical_not(first))
    def _():
        ref[...] += val


def _modnorm_tile(x_ref, g_ref, sc_ref, sh_ref):
    xv = x_ref[...]
    r = lax.rsqrt(jnp.mean(xv * xv, axis=-1, keepdims=True) + EPS)
    return (((xv * r) * g_ref[...]) * (1.0 + sc_ref[...]) + sh_ref[...]).astype(BF)


def _ffn_up(x, g, sc, sh, wgu, seq, carry=None):
    t, d = x.shape
    fs = wgu.shape[-1]
    per = seq // TM

    def body(x_ref, g_ref, sc_ref, sh_ref, w_ref, h_ref, p_ref, q_ref, s_ref):
        hv = _modnorm_tile(x_ref, g_ref, sc_ref, sh_ref)
        h_ref[...] = hv
        for j in range(N_CHIPS):
            a = _nn(hv, w_ref[j, 0])
            u = _nn(hv, w_ref[j, 1])
            sig = jax.nn.sigmoid(a)
            q = a * sig
            p_ref[j] = (u * (sig * (1.0 + a * (1.0 - sig)))).astype(BF)
            q_ref[j] = q.astype(BF)
            s_ref[j] = (q * u).astype(BF)

    row = pl.BlockSpec((TM, d), lambda m: (m, 0))
    ex = pl.BlockSpec((None, 1, d), lambda m: (m // per, 0, 0))
    blk = pl.BlockSpec((N_CHIPS, TM, fs), lambda m: (0, m, 0))
    return _call(
        body, "ffn_up", (t // TM,),
        [row, pl.BlockSpec((1, d), lambda m: (0, 0)), ex, ex, _whole(wgu)],
        [row, blk, blk, blk],
        [_sds((t, d), BF)] + [_sds((N_CHIPS, t, fs), BF)] * 3,
        (x, g, sc, sh, wgu), carry=carry)


def _ffn_down(s, wd, x, gt, seq, coef):
    _, t, fs = s.shape
    d = x.shape[-1]
    per = seq // TM

    def body(s_ref, w_ref, x_ref, gt_ref, f_ref, xo_ref):
        f = _nn(s_ref[0], w_ref[0, 0])
        for j in range(1, N_CHIPS):
            f = f + _nn(s_ref[j], w_ref[j, 0])
        f_ref[...] = f
        xo_ref[...] = x_ref[...] + (coef * gt_ref[...]) * f

    row = pl.BlockSpec((TM, d), lambda m: (m, 0))
    return pl.pallas_call(
        body, name="ffn_down", grid=(t // TM,),
        in_specs=[pl.BlockSpec((N_CHIPS, TM, fs), lambda m: (0, m, 0)),
                  _whole(wd), row,
                  pl.BlockSpec((None, 1, d), lambda m: (m // per, 0, 0))],
        out_specs=[row, row],
        out_shape=[_sds((t, d), F32), _sds((t, d), F32)],
        compiler_params=_cp(1))(s, wd, x, gt)


def _ffn_bwd_ds(dxo, gt, f, wd, p, q, seq, coef, carry=None):
    t, d = dxo.shape
    fs = p.shape[-1]
    per = seq // TM
    nb = t // seq

    def body(dxo_ref, gt_ref, f_ref, w_ref, p_ref, q_ref, da_ref, du_ref, df_ref, dgt_ref):
        m = pl.program_id(0)
        dxv = dxo_ref[...]
        df = ((coef * gt_ref[...]) * dxv).astype(BF)
        df_ref[...] = df
        _acc_rows(dgt_ref, coef * jnp.sum(dxv * f_ref[...], axis=0, keepdims=True), m % per == 0)
        for j in range(N_CHIPS):
            ds = _nt(df, w_ref[j, 0])
            da_ref[j] = (ds * p_ref[j].astype(F32)).astype(BF)
            du_ref[j] = (ds * q_ref[j].astype(F32)).astype(BF)

    row = pl.BlockSpec((TM, d), lambda m: (m, 0))
    blk = pl.BlockSpec((N_CHIPS, TM, fs), lambda m: (0, m, 0))
    ex = pl.BlockSpec((None, 1, d), lambda m: (m // per, 0, 0))
    return _call(
        body, "ffn_bwd_ds", (t // TM,),
        [row, ex, row, _whole(wd), blk, blk],
        [blk, blk, row, ex],
        [_sds((N_CHIPS, t, fs), BF), _sds((N_CHIPS, t, fs), BF), _sds((t, d), BF), _sds((nb, 1, d), F32)],
        (dxo, gt, f, wd, p, q), carry=carry)


TM_X = 256


def _ffn_bwd_x(dxo, gt, f, wd, p, q, wgu, x, g, sc, seq, coef):
    t, d = dxo.shape
    fs = p.shape[-1]
    per = seq // TM_X
    nb = t // seq

    def body(dxo_ref, gt_ref, f_ref, wd_ref, p_ref, q_ref, w_ref, x_ref, g_ref, sc_ref,
             da_ref, du_ref, df_ref, dgt_ref, dx_ref, dsh_ref, dsc_ref, dg_ref):
        m = pl.program_id(0)
        dxv = dxo_ref[...]
        df = ((coef * gt_ref[...]) * dxv).astype(BF)
        df_ref[...] = df
        _acc_rows(dgt_ref, coef * jnp.sum(dxv * f_ref[...], axis=0, keepdims=True), m % per == 0)
        dh = None
        for j in range(N_CHIPS):
            ds = _nt(df, wd_ref[j, 0])
            da = (ds * p_ref[j].astype(F32)).astype(BF)
            du = (ds * q_ref[j].astype(F32)).astype(BF)
            da_ref[j] = da
            du_ref[j] = du
            part = _nt(da, w_ref[j, 0]) + _nt(du, w_ref[j, 1])
            dh = part if dh is None else dh + part
        dx, dsh, dsc, dg = _modnorm_bwd_tile(dh, x_ref[...], g_ref[...], sc_ref[...], dxv)
        dx_ref[...] = dx
        _acc_rows(dsh_ref, dsh, m % per == 0)
        _acc_rows(dsc_ref, dsc, m % per == 0)
        _acc_rows(dg_ref, dg, m == 0)

    row = pl.BlockSpec((TM_X, d), lambda m: (m, 0))
    blk = pl.BlockSpec((N_CHIPS, TM_X, fs), lambda m: (0, m, 0))
    ex = pl.BlockSpec((None, 1, d), lambda m: (m // per, 0, 0))
    vec = pl.BlockSpec((1, d), lambda m: (0, 0))
    exs = _sds((nb, 1, d), F32)
    return pl.pallas_call(
        body, name="ffn_bwd_x", grid=(t // TM_X,),
        in_specs=[row, ex, row, _whole(wd), blk, blk, _whole(wgu), row, vec, ex],
        out_specs=[blk, blk, row, ex, row, ex, ex, vec],
        out_shape=[_sds((N_CHIPS, t, fs), BF), _sds((N_CHIPS, t, fs), BF), _sds((t, d), BF), exs,
                   _sds((t, d), F32), exs, exs, _sds((1, d), F32)],
        compiler_params=_cp(1))(dxo, gt, f, wd, p, q, wgu, x, g, sc)


TK_W = 1024


def _ffn_bwd_w(h, da, du, s, df):
    t, d = h.shape
    fs = da.shape[-1]

    def body(h_ref, da_ref, du_ref, s_ref, df_ref, o_ref):
        kt = pl.program_id(1)
        hv = h_ref[...]
        parts = (_tn(da_ref[...], hv), _tn(du_ref[...], hv), _tn(s_ref[...], df_ref[...]))

        @pl.when(kt == 0)
        def _():
            for i, p in enumerate(parts):
                o_ref[i] = p

        @pl.when(kt != 0)
        def _():
            for i, p in enumerate(parts):
                o_ref[i] += p

    row = pl.BlockSpec((TK_W, d), lambda j, kt: (kt, 0))
    blk = pl.BlockSpec((None, TK_W, fs), lambda j, kt: (j, kt, 0))
    return pl.pallas_call(
        body, name="ffn_bwd_w", grid=(N_CHIPS, t // TK_W),
        in_specs=[row, blk, blk, blk, row],
        out_specs=pl.BlockSpec((None, 3, fs, d), lambda j, kt: (j, 0, 0, 0)),
        out_shape=_sds((N_CHIPS, 3, fs, d), F32),
        compiler_params=_cp(2))(h, da, du, s, df)


def _ffn_bwd_dh(da, du, wgu, x, g, sc, dxo, seq, carry=None):
    _, t, fs = da.shape
    d = x.shape[-1]
    per = seq // TM
    nb = t // seq

    def body(da_ref, du_ref, w_ref, x_ref, g_ref, sc_ref, dxo_ref, dx_ref, dsh_ref, dsc_ref, dg_ref):
        m = pl.program_id(0)
        dh = _nt(da_ref[0], w_ref[0, 0]) + _nt(du_ref[0], w_ref[0, 1])
        for j in range(1, N_CHIPS):
            dh = dh + _nt(da_ref[j], w_ref[j, 0]) + _nt(du_ref[j], w_ref[j, 1])
        dx, dsh, dsc, dg = _modnorm_bwd_tile(dh, x_ref[...], g_ref[...], sc_ref[...], dxo_ref[...])
        dx_ref[...] = dx
        _acc_rows(dsh_ref, dsh, m % per == 0)
        _acc_rows(dsc_ref, dsc, m % per == 0)
        _acc_rows(dg_ref, dg, m == 0)

    row = pl.BlockSpec((TM, d), lambda m: (m, 0))
    blk = pl.BlockSpec((N_CHIPS, TM, fs), lambda m: (0, m, 0))
    ex = pl.BlockSpec((None, 1, d), lambda m: (m // per, 0, 0))
    vec = pl.BlockSpec((1, d), lambda m: (0, 0))
    return _call(
        body, "ffn_bwd_dh", (t // TM,),
        [blk, blk, _whole(wgu), row, vec, ex, row],
        [row, ex, ex, vec],
        [_sds((t, d), F32), _sds((nb, 1, d), F32), _sds((nb, 1, d), F32), _sds((1, d), F32)],
        (da, du, wgu, x, g, sc, dxo), carry=carry)


def _qkv_proj(x, g, sc, sh, w_in, seq, carry=None):
    t, d = x.shape
    wc = w_in.shape[-1]
    per = seq // TM

    dils = [dil for _, dil in DIL_CONFIGS if dil > 1]

    def body(x_ref, g_ref, sc_ref, sh_ref, w_ref, h_ref, o_ref, *rest):
        res_refs, buf = rest[:len(dils)], rest[len(dils)]
        hv = _modnorm_tile(x_ref, g_ref, sc_ref, sh_ref)
        h_ref[...] = hv
        for j in range(N_CHIPS):
            rf = _nn(hv, w_ref[j, 0])
            r = rf.astype(BF)
            for a, lc, off, width in _col_pieces(j, wc):
                o_ref[a, :, lc:lc + width] = r[:, off:off + width]
                if a < 3:
                    continue
                for c0 in range(0, width, 128):
                    cg = (lc + c0) // 128
                    buf[...] = rf[:, off + c0:off + c0 + 128]
                    for ref, dil in zip(res_refs, dils):
                        for rr in range(dil):
                            ref[a - 3, :, rr * D_GRP + cg * 128:rr * D_GRP + (cg + 1) * 128] = (
                                buf[pl.ds(rr, TM // dil, stride=dil), :].astype(BF))

    row = pl.BlockSpec((TM, d), lambda m: (m, 0))
    ex = pl.BlockSpec((None, 1, d), lambda m: (m // per, 0, 0))
    return _call(
        body, "qkv_proj", (t // TM,),
        [row, pl.BlockSpec((1, d), lambda m: (0, 0)), ex, ex, _whole(w_in)],
        [row, pl.BlockSpec((6, TM, D_GRP), lambda m: (0, m, 0))]
        + [pl.BlockSpec((3, TM // dil, dil * D_GRP), lambda m: (0, m, 0)) for dil in dils],
        [_sds((t, d), BF), _sds((6, t, D_GRP), BF)] + [_sds((3, t // dil, dil * D_GRP), BF) for dil in dils],
        (x, g, sc, sh, w_in), scratch=[pltpu.VMEM((TM, 128), F32)], carry=carry)


def _col_pieces(j, wc):
    out, off = [], 0
    while off < wc:
        a, lc = divmod(j * wc + off, D_GRP)
        width = min(D_GRP - lc, wc - off)
        out.append((a, lc, off, width))
        off += width
    return out


def _chip_cols(g6_ref, j, wc):
    return jnp.concatenate([g6_ref[a, :, lc:lc + width] for a, lc, _, width in _col_pieces(j, wc)], axis=1)


def _mix_out(on_sb, on_dil, w_out, x, gt, seq):
    t, d = x.shape
    per = seq // TM

    def body(a_ref, b_ref, w_ref, x_ref, gt_ref, t_ref, xo_ref):
        tv = _nn(a_ref[...], w_ref[0:D_GRP, :]) + _nn(b_ref[...], w_ref[D_GRP:2 * D_GRP, :])
        t_ref[...] = tv
        xo_ref[...] = x_ref[...] + gt_ref[...] * tv

    row = pl.BlockSpec((TM, d), lambda m: (m, 0))
    half = pl.BlockSpec((TM, D_GRP), lambda m: (m, 0))
    return pl.pallas_call(
        body, name="mix_out", grid=(t // TM,),
        in_specs=[half, half, pl.BlockSpec((2 * D_GRP, d), lambda m: (0, 0)), row,
                  pl.BlockSpec((None, 1, d), lambda m: (m // per, 0, 0))],
        out_specs=[row, row],
        out_shape=[_sds((t, d), F32), _sds((t, d), F32)],
        compiler_params=_cp(1))(on_sb, on_dil, w_out, x, gt)


def _sb_masks():
    lane = lax.broadcasted_iota(jnp.int32, (1, 2 * HEAD_DIM), 1)
    hm0 = lane < HEAD_DIM
    rel = lax.broadcasted_iota(jnp.int32, (TQ, KB), 0) - lax.broadcasted_iota(jnp.int32, (TQ, KB), 1)
    kr = lax.broadcasted_iota(jnp.int32, (KB, KB), 0)
    kc = lax.broadcasted_iota(jnp.int32, (KB, KB), 1)
    return hm0, rel, kr, kc


def _headnorm_pair(o, gv, hm0):
    o2 = o * o
    ms0 = jnp.sum(jnp.where(hm0, o2, 0.0), axis=-1, keepdims=True) * (1.0 / HEAD_DIM)
    ms1 = jnp.sum(jnp.where(hm0, 0.0, o2), axis=-1, keepdims=True) * (1.0 / HEAD_DIM)
    r = jnp.where(hm0, lax.rsqrt(ms0 + EPS), lax.rsqrt(ms1 + EPS))
    return (o * r) * gv


SB_DEAD = -104.0


def _alive(c_l):
    return (jnp.max(c_l) > SB_DEAD).astype(jnp.int32)


def _sb_fwd(qkv6, g_sb, nb, seq, carry=None):
    nq = seq // TQ

    def body(q_ref, k_ref, v_ref, g_ref, o_ref, on_ref):
        qi = pl.program_id(2)
        hm0, rel, kr, kc = _sb_masks()
        upper = (kr > kc).astype(BF)
        qv = q_ref[...]
        qhs = [jnp.where(hm0, qv, jnp.zeros_like(qv)), jnp.where(hm0, jnp.zeros_like(qv), qv)]

        def block(kj, causal, c_ls, accs):
            ks = pl.multiple_of(kj * KB, KB)
            kb = k_ref[pl.ds(ks, KB), :]
            vb = v_ref[pl.ds(ks, KB), :]
            new_c, new_acc = [], []
            for qh, c_l, acc in zip(qhs, c_ls, accs):
                z = _nt(qh, kb) * SCALE
                sp = _softplus(z)
                ln = -sp if causal is None else jnp.where(causal, -sp, 0.0)
                suf = _nn2(ln, upper)
                w = jnp.exp((z - sp) + (suf + c_l))
                if causal is not None:
                    w = jnp.where(causal, w, 0.0)
                new_acc.append(acc + _nn(w.astype(BF), vb))
                new_c.append(c_l + (suf[:, 0:1] + ln[:, 0:1]))
            return new_c, new_acc

        zc = jnp.zeros((TQ, 1), F32)
        za = jnp.zeros((TQ, 2 * HEAD_DIM), F32)
        c_ls, accs = block(qi, rel > 0, [zc, zc], [za, za])

        def cond(carry):
            return jnp.logical_and(carry[0] <= qi, carry[1] > 0)

        def kbody(carry):
            it, _, c0, c1, a0, a1 = carry
            (c0, c1), (a0, a1) = block(qi - it, None, [c0, c1], [a0, a1])
            return it + 1, jnp.maximum(_alive(c0), _alive(c1)), c0, c1, a0, a1

        init = (jnp.int32(1), jnp.maximum(_alive(c_ls[0]), _alive(c_ls[1])), c_ls[0], c_ls[1], accs[0], accs[1])
        outs = lax.while_loop(cond, kbody, init)[4:]
        o = jnp.where(hm0, outs[0], outs[1])
        o_ref[...] = o
        on_ref[...] = _headnorm_pair(o, g_ref[...], hm0).astype(BF)

    w = 2 * HEAD_DIM
    full = lambda i: pl.BlockSpec((None, None, seq, w), lambda b, hp, q: (i, b, 0, hp))
    qblk = pl.BlockSpec((None, None, TQ, w), lambda b, hp, q: (0, b, q, hp))
    oblk = pl.BlockSpec((None, TQ, w), lambda b, hp, q: (b, q, hp))
    return _call(
        body, "sb_fwd", (nb, N_HEADS // 2, nq),
        [qblk, full(1), full(2), pl.BlockSpec((1, w), lambda b, hp, q: (0, hp))],
        [oblk, oblk],
        [_sds((nb, seq, D_GRP), F32), _sds((nb, seq, D_GRP), BF)],
        (qkv6, qkv6, qkv6, g_sb), carry=carry)


def _sb_bwd(qkv6, do, nb, seq, carry=None):
    nq = seq // TQ
    nk = seq // KB

    def body(q_ref, k_ref, v_ref, do_ref, out_ref, dk_acc, dv_acc, g_st, s_st):
        qi = pl.program_id(2)
        hm0, rel, kr, kc = _sb_masks()
        upper = (kr > kc).astype(BF)
        lower = (kr < kc).astype(BF)

        @pl.when(qi == 0)
        def _():
            dk_acc[...] = jnp.zeros_like(dk_acc)
            dv_acc[...] = jnp.zeros_like(dv_acc)

        qv = q_ref[...]
        dov = do_ref[...]
        qhs = [jnp.where(hm0, qv, jnp.zeros_like(qv)), jnp.where(hm0, jnp.zeros_like(qv), qv)]
        dohs = [jnp.where(hm0, dov, 0.0).astype(BF), jnp.where(hm0, 0.0, dov).astype(BF)]

        def weights(kj, causal, c_ls):
            ks = pl.multiple_of(kj * KB, KB)
            kb = k_ref[pl.ds(ks, KB), :]
            vb = v_ref[pl.ds(ks, KB), :]
            new_c, dv = [], None
            for hh, (qh, doh, c_l) in enumerate(zip(qhs, dohs, c_ls)):
                z = _nt(qh, kb) * SCALE
                sp = _softplus(z)
                ln = -sp if causal is None else jnp.where(causal, -sp, 0.0)
                suf = _nn2(ln, upper)
                lsz = z - sp
                w = jnp.exp(lsz + (suf + c_l))
                if causal is not None:
                    w = jnp.where(causal, w, 0.0)
                g_st[hh, kj] = w * _nt(doh, vb)
                s_st[hh, kj] = jnp.exp(lsz)
                part = _tn(w.astype(BF), doh)
                dv = part if dv is None else dv + part
                new_c.append(c_l + (suf[:, 0:1] + ln[:, 0:1]))
            dv_acc[pl.ds(ks, KB), :] += dv
            return new_c

        zc = jnp.zeros((TQ, 1), F32)
        c_ls = weights(qi, rel > 0, [zc, zc])

        def acond(carry):
            return jnp.logical_and(carry[0] <= qi, carry[1] > 0)

        def abody(carry):
            it, _, c0, c1 = carry
            c0, c1 = weights(qi - it, None, [c0, c1])
            return it + 1, jnp.maximum(_alive(c0), _alive(c1)), c0, c1

        n_used = lax.while_loop(
            acond, abody, (jnp.int32(1), jnp.maximum(_alive(c_ls[0]), _alive(c_ls[1])), c_ls[0], c_ls[1]))[0]

        def grads(kj, causal, c_gs, dqs):
            ks = pl.multiple_of(kj * KB, KB)
            kb = k_ref[pl.ds(ks, KB), :]
            new_c, new_dq, dk = [], [], None
            for hh, (qh, c_g, dq) in enumerate(zip(qhs, c_gs, dqs)):
                g = g_st[hh, kj]
                sig = s_st[hh, kj]
                pre = _nn(g.astype(BF), lower)
                dz = g * (1.0 - sig) - sig * (pre + c_g)
                if causal is not None:
                    dz = jnp.where(causal, dz, 0.0)
                dzb = (dz * SCALE).astype(BF)
                part = _tn(dzb, qh)
                dk = part if dk is None else dk + part
                new_dq.append(dq + _nn(dzb, kb))
                new_c.append(c_g + (pre[:, KB - 1:KB] + g[:, KB - 1:KB]))
            dk_acc[pl.ds(ks, KB), :] += dk
            return new_c, new_dq

        za = jnp.zeros((TQ, 2 * HEAD_DIM), F32)

        def bbody(kj, carry):
            (c0, c1), (d0, d1) = grads(kj, None, carry[:2], carry[2:])
            return c0, c1, d0, d1

        c0, c1, d0, d1 = lax.fori_loop(qi - n_used + 1, qi, bbody, (zc, zc, za, za))
        _, dqs = grads(qi, rel > 0, [c0, c1], [d0, d1])
        dq = jnp.where(hm0, dqs[0], dqs[1])
        out_ref[0, pl.ds(pl.multiple_of(qi * TQ, TQ), TQ), :] = dq.astype(BF)

        @pl.when(qi == nq - 1)
        def _():
            out_ref[1] = dk_acc[...].astype(BF)
            out_ref[2] = dv_acc[...].astype(BF)

    w = 2 * HEAD_DIM
    full = lambda i: pl.BlockSpec((None, None, seq, w), lambda b, hp, q: (i, b, 0, hp))
    qblk = pl.BlockSpec((None, None, TQ, w), lambda b, hp, q: (0, b, q, hp))
    oblk = pl.BlockSpec((None, TQ, w), lambda b, hp, q: (b, q, hp))
    return _call(
        body, "sb_bwd", (nb, N_HEADS // 2, nq),
        [qblk, full(1), full(2), oblk],
        [pl.BlockSpec((3, None, seq, w), lambda b, hp, q: (0, b, 0, hp))],
        [_sds((6, nb, seq, D_GRP), BF)], (qkv6, qkv6, qkv6, do),
        scratch=[pltpu.VMEM((seq, w), F32), pltpu.VMEM((seq, w), F32),
                 pltpu.VMEM((2, nk, TQ, KB), F32), pltpu.VMEM((2, nk, TQ, KB), F32)],
        carry=carry)


def _t5_bucket(n):
    max_exact = N_BUCKETS // 2
    nf = np.maximum(n, 1).astype(np.float32)
    large = max_exact + (np.log(nf / max_exact) / math.log(MAX_DISTANCE / max_exact)
                         * (N_BUCKETS - max_exact)).astype(np.int32)
    large = np.minimum(large, N_BUCKETS - 1)
    return np.where(n < max_exact, n, large).astype(np.int32)


def _bucket_map(dilation):
    step = BLOCK + np.arange(BLOCK)[:, None] - np.arange(2 * BLOCK)[None, :]
    return _t5_bucket(np.clip(step, 0, N_STEPS) * dilation)


GRP_HEADS = 4
GRP_W = GRP_HEADS * HEAD_DIM


def _dil_masks():
    lane = lax.broadcasted_iota(jnp.int32, (1, GRP_W), 1)
    heads = [jnp.logical_and(lane >= HEAD_DIM * i, lane < HEAD_DIM * (i + 1)) for i in range(GRP_HEADS)]
    iq = lax.broadcasted_iota(jnp.int32, (BLOCK, BLOCK), 0)
    ik = lax.broadcasted_iota(jnp.int32, (BLOCK, BLOCK), 1)
    return heads, ik <= iq, ik >= iq


def _dil_rows(n):
    rs = pl.multiple_of(n * BLOCK, BLOCK)
    ps = pl.multiple_of(jnp.maximum(n - 1, 0) * BLOCK, BLOCK)
    return pl.ds(rs, BLOCK), pl.ds(ps, BLOCK)


def _dil_probs(qh, kc, kp, b_ref, hh, valid_c, valid_p):
    zc = _nt(qh, kc) * SCALE + b_ref[hh, :, BLOCK:2 * BLOCK]
    zp = _nt(qh, kp) * SCALE + b_ref[hh, :, 0:BLOCK]
    zc = jnp.where(valid_c, zc, NEG_INF)
    zp = jnp.where(valid_p, zp, NEG_INF)
    m = jnp.maximum(jnp.max(zc, axis=-1, keepdims=True), jnp.max(zp, axis=-1, keepdims=True))
    ec = jnp.exp(zc - m)
    ep = jnp.exp(zp - m)
    den = jnp.sum(ec, axis=-1, keepdims=True) + jnp.sum(ep, axis=-1, keepdims=True)
    return ec, ep, den, m


def _dil_fwd(qkv6r, base, bias, nb, sub_len, dilation):
    n_blk = sub_len // BLOCK

    def body(q_ref, k_ref, v_ref, b_ref, o_ref, l_ref):
        heads, valid_c, valid_p0 = _dil_masks()

        def nbody(n, carry):
            cur, prev = _dil_rows(n)
            valid_p = jnp.logical_and(valid_p0, n > 0)
            for gi in range(N_HEADS // GRP_HEADS):
                lanes = slice(gi * GRP_W, (gi + 1) * GRP_W)
                qv, kc, kp = q_ref[cur, lanes], k_ref[cur, lanes], k_ref[prev, lanes]
                vc, vp = v_ref[cur, lanes], v_ref[prev, lanes]
                o = jnp.zeros((BLOCK, GRP_W), F32)
                lse = jnp.zeros((BLOCK, GRP_W), F32)
                for i, hm in enumerate(heads):
                    qh = jnp.where(hm, qv, jnp.zeros_like(qv))
                    ec, ep, den, m = _dil_probs(qh, kc, kp, b_ref, gi * GRP_HEADS + i, valid_c, valid_p)
                    o = jnp.where(hm, (_nn(ec.astype(BF), vc) + _nn(ep.astype(BF), vp)) / den, o)
                    lse = jnp.where(hm, m + jnp.log(den), lse)
                o_ref[cur, lanes] = o
                l_ref[cur, lanes] = lse
            return carry

        lax.fori_loop(0, n_blk, nbody, 0)

    seqblk = lambda i: pl.BlockSpec((None, None, sub_len, D_GRP), lambda b, r: (i, b, 0, r))
    oblk = pl.BlockSpec((None, sub_len, D_GRP), lambda b, r: (b, 0, r))
    shp = _sds((nb, sub_len, dilation * D_GRP), F32)
    return pl.pallas_call(
        body, name="dil_fwd_%d" % dilation, grid=(nb, dilation),
        in_specs=[seqblk(base), seqblk(base + 1), seqblk(base + 2), _whole(bias)],
        out_specs=[oblk, oblk], out_shape=[shp, shp],
        compiler_params=_cp(2))(qkv6r, qkv6r, qkv6r, bias)


def _dil_bwd(qkv6r, base, bias, do_c, dd_c, nb, sub_len, dilation, carry=None):
    n_blk = sub_len // BLOCK

    def body(q_ref, k_ref, v_ref, b_ref, do_ref, dd_ref, out_ref, a_ref):
        heads, valid_c, valid_p0 = _dil_masks()
        first = jnp.logical_and(pl.program_id(0) == 0, pl.program_id(1) == 0)

        @pl.when(first)
        def _():
            a_ref[...] = jnp.zeros_like(a_ref)

        out_ref[1] = jnp.zeros((sub_len, D_GRP), F32)
        out_ref[2] = jnp.zeros((sub_len, D_GRP), F32)

        def nbody(n, carry):
            cur, prev = _dil_rows(n)
            valid_p = jnp.logical_and(valid_p0, n > 0)
            for gi in range(N_HEADS // GRP_HEADS):
                lanes = slice(gi * GRP_W, (gi + 1) * GRP_W)
                qv, kc, kp = q_ref[cur, lanes], k_ref[cur, lanes], k_ref[prev, lanes]
                vc, vp = v_ref[cur, lanes], v_ref[prev, lanes]
                dov, ddv = do_ref[cur, lanes], dd_ref[cur, lanes]
                dq = jnp.zeros((BLOCK, GRP_W), F32)
                dkc = jnp.zeros((BLOCK, GRP_W), F32)
                dkp = jnp.zeros((BLOCK, GRP_W), F32)
                dvc = jnp.zeros((BLOCK, GRP_W), F32)
                dvp = jnp.zeros((BLOCK, GRP_W), F32)
                for i, hm in enumerate(heads):
                    h = gi * GRP_HEADS + i
                    qh = jnp.where(hm, qv, jnp.zeros_like(qv))
                    doh = jnp.where(hm, dov, 0.0).astype(BF)
                    ddh = jnp.sum(jnp.where(hm, ddv, 0.0), axis=-1, keepdims=True) * (1.0 / HEAD_DIM)
                    ec, ep, den, _ = _dil_probs(qh, kc, kp, b_ref, h, valid_c, valid_p)
                    inv = 1.0 / den
                    pc = ec * inv
                    pp = ep * inv
                    dzc = pc * (_nt(doh, vc) + ddh)
                    dzp = pp * (_nt(doh, vp) + ddh)
                    a_ref[h, :, BLOCK:2 * BLOCK] += dzc
                    a_ref[h, :, 0:BLOCK] += dzp
                    dzcb = (dzc * SCALE).astype(BF)
                    dzpb = (dzp * SCALE).astype(BF)
                    dq = jnp.where(hm, _nn(dzcb, kc) + _nn(dzpb, kp), dq)
                    dkc = dkc + _tn(dzcb, qh)
                    dkp = dkp + _tn(dzpb, qh)
                    dvc = dvc + _tn(pc.astype(BF), doh)
                    dvp = dvp + _tn(pp.astype(BF), doh)
                out_ref[0, cur, lanes] = dq
                out_ref[1, cur, lanes] += dkc
                out_ref[1, prev, lanes] += dkp
                out_ref[2, cur, lanes] += dvc
                out_ref[2, prev, lanes] += dvp
            return carry

        lax.fori_loop(0, n_blk, nbody, 0)

    seqblk = lambda i: pl.BlockSpec((None, None, sub_len, D_GRP), lambda b, r: (i, b, 0, r))
    oblk = pl.BlockSpec((None, sub_len, D_GRP), lambda b, r: (b, 0, r))
    return _call(
        body, "dil_bwd_%d" % dilation, (nb, dilation),
        [seqblk(base), seqblk(base + 1), seqblk(base + 2), _whole(bias), oblk, oblk],
        [pl.BlockSpec((3, None, sub_len, D_GRP), lambda b, r: (0, b, 0, r)),
         pl.BlockSpec((N_HEADS, BLOCK, 2 * BLOCK), lambda b, r: (0, 0, 0))],
        [_sds((3, nb, sub_len, dilation * D_GRP), F32), _sds((N_HEADS, BLOCK, 2 * BLOCK), F32)],
        (qkv6r, qkv6r, qkv6r, bias, do_c, dd_c), carry=carry)


def _group_ones():
    idx = np.arange(D_GRP) // HEAD_DIM
    return jnp.asarray((idx[:, None] == idx[None, :]).astype(np.float32), dtype=BF)


def _dil_alphas(l1, l4, l16):
    mx = jnp.maximum(jnp.maximum(l1, l4), l16)
    e1 = jnp.exp(l1 - mx)
    e4 = jnp.exp(l4 - mx)
    e16 = jnp.exp(l16 - mx)
    den = e1 + e4 + e16
    return e1 / den, e4 / den, e16 / den


def _residue_spec(dil):
    return pl.BlockSpec((TM // dil, dil * D_GRP), lambda m: (m, 0))


def _from_residue(src, dil, cg, buf):
    if dil == 1:
        return src[:, cg * 128:(cg + 1) * 128]
    for r in range(dil):
        buf[pl.ds(r, TM // dil, stride=dil), :] = src[:, r * D_GRP + cg * 128:r * D_GRP + (cg + 1) * 128]
    return buf[...]


def _to_residue(dst, dil, cg, buf, val):
    if dil == 1:
        dst[:, cg * 128:(cg + 1) * 128] = val
        return
    buf[...] = val
    for r in range(dil):
        dst[:, r * D_GRP + cg * 128:r * D_GRP + (cg + 1) * 128] = buf[pl.ds(r, TM // dil, stride=dil), :]


def _pair_sum(x, hm0):
    s0 = jnp.sum(jnp.where(hm0, x, 0.0), axis=-1, keepdims=True)
    s1 = jnp.sum(jnp.where(hm0, 0.0, x), axis=-1, keepdims=True)
    return jnp.where(hm0, s0, s1)


def _dil_comb(os, ls, g_dil):
    t = os[0].shape[0]
    dils = [dil for _, dil in DIL_CONFIGS]

    def body(o1, l1, o4, l4, o16, l16, g_ref, o_ref, on_ref, b0, b1, b2, b3):
        hm0 = lax.broadcasted_iota(jnp.int32, (1, 128), 1) < HEAD_DIM
        for cg in range(D_GRP // 128):
            lanes = slice(cg * 128, (cg + 1) * 128)
            ov = [_from_residue(src, dil, cg, buf) for src, dil, buf in zip((o1, o4, o16), dils, (None, b0, b1))]
            lv = [_from_residue(src, dil, cg, buf) for src, dil, buf in zip((l1, l4, l16), dils, (None, b2, b3))]
            a1, a4, a16 = _dil_alphas(*lv)
            o = a1 * ov[0] + a4 * ov[1] + a16 * ov[2]
            o_ref[:, lanes] = o
            on_ref[:, lanes] = _headnorm_pair(o, g_ref[:, lanes], hm0).astype(BF)

    blk = pl.BlockSpec((TM, D_GRP), lambda m: (m, 0))
    specs = [_residue_spec(dil) for dil in dils for _ in range(2)]
    return pl.pallas_call(
        body, name="dil_comb", grid=(t // TM,),
        in_specs=specs + [pl.BlockSpec((1, D_GRP), lambda m: (0, 0))],
        out_specs=[blk, blk],
        out_shape=[_sds((t, D_GRP), F32), _sds((t, D_GRP), BF)],
        scratch_shapes=[pltpu.VMEM((TM, 128), F32)] * 4,
        compiler_params=_cp(1))(os[0], ls[0], os[1], ls[1], os[2], ls[2], g_dil)


def _dil_comb_bwd(do, os, ls):
    t = do.shape[0]
    dils = [dil for _, dil in DIL_CONFIGS]

    def body(do_ref, o1, l1, o4, l4, o16, l16, d1, d4, d16, e1, e4, e16, b0, b1, b2, b3):
        hm0 = lax.broadcasted_iota(jnp.int32, (1, 128), 1) < HEAD_DIM
        for cg in range(D_GRP // 128):
            dov = do_ref[:, cg * 128:(cg + 1) * 128]
            ov = [_from_residue(src, dil, cg, buf) for src, dil, buf in zip((o1, o4, o16), dils, (None, b0, b1))]
            lv = [_from_residue(src, dil, cg, buf) for src, dil, buf in zip((l1, l4, l16), dils, (None, b2, b3))]
            al = _dil_alphas(*lv)
            sbar = al[0] * _pair_sum(dov * ov[0], hm0)
            for a_c, o_c in zip(al[1:], ov[1:]):
                sbar = sbar + a_c * _pair_sum(dov * o_c, hm0)
            for a_c, dil, dref, eref in zip(al, dils, (d1, d4, d16), (e1, e4, e16)):
                _to_residue(dref, dil, cg, b0, a_c * dov)
                _to_residue(eref, dil, cg, b1, -a_c * sbar)

    specs = [_residue_spec(dil) for dil in dils]
    return pl.pallas_call(
        body, name="dil_comb_bwd", grid=(t // TM,),
        in_specs=[pl.BlockSpec((TM, D_GRP), lambda m: (m, 0))] + [sp for sp in specs for _ in range(2)],
        out_specs=specs + specs,
        out_shape=[_sds((t // dil, dil * D_GRP), F32) for dil in dils] * 2,
        scratch_shapes=[pltpu.VMEM((TM, 128), F32)] * 4,
        compiler_params=_cp(1))(do, os[0], ls[0], os[1], ls[1], os[2], ls[2])


def _dqkv_dil_sum(ds, dqkv6):
    t = dqkv6.shape[1]
    dils = [dil for _, dil in DIL_CONFIGS]

    def body(*refs):
        srcs, o_ref, acc = refs[:len(dils)], refs[len(dils) + 1], refs[len(dils) + 2]
        for a in range(3):
            for cg in range(D_GRP // 128):
                for src, dil in zip(srcs, dils):
                    for r in range(dil):
                        part = src[a, :, r * D_GRP + cg * 128:r * D_GRP + (cg + 1) * 128]
                        rows = pl.ds(r, TM // dil, stride=dil) if dil > 1 else slice(None)
                        if dil == dils[0]:
                            acc[rows, :] = part
                        else:
                            acc[rows, :] += part
                o_ref[a, :, cg * 128:(cg + 1) * 128] = acc[...].astype(BF)

    return pl.pallas_call(
        body, name="dqkv_dil_sum", grid=(t // TM,),
        in_specs=[pl.BlockSpec((3, TM // dil, dil * D_GRP), lambda m: (0, m, 0)) for dil in dils]
        + [pl.BlockSpec(memory_space=pl.ANY)],
        out_specs=pl.BlockSpec((3, TM, D_GRP), lambda m: (1, m, 0)),
        out_shape=_sds((6, t, D_GRP), BF), input_output_aliases={len(dils): 0},
        scratch_shapes=[pltpu.VMEM((TM, 128), F32)],
        compiler_params=_cp(1))(*ds, dqkv6)


def _relbias_grad(a_all, onehot):
    def body(a_ref, oh_ref, o_ref):
        acc = jnp.zeros((N_HEADS, N_BUCKETS), F32)
        for c in range(len(DIL_CONFIGS)):
            av = a_ref[c]
            hi = av.astype(BF)
            lo = (av - hi.astype(F32)).astype(BF)
            acc = acc + _nt(hi, oh_ref[c]) + _nt(lo, oh_ref[c])
        o_ref[...] = acc

    return pl.pallas_call(body, name="relbias_grad", out_shape=_sds((N_HEADS, N_BUCKETS), F32),
                          compiler_params=_cp())(a_all, onehot)


def _headnorm_bwd(dn, o, gv, mv):
    ms = _nn2(o * o, mv) * (1.0 / HEAD_DIM)
    r = lax.rsqrt(ms + EPS)
    nrm = o * r
    dg = jnp.sum(dn * nrm, axis=0, keepdims=True)
    dnn = dn * gv
    do = r * (dnn - nrm * (_nn2(dnn * nrm, mv) * (1.0 / HEAD_DIM)))
    return do, dg


def _mix_bwd_out(dx, gt, tv, w_out, o_sb, o_dil, on_sb, on_dil, g_sb, g_dil, ones_g, seq, carry=None):
    t, d = dx.shape
    per = seq // TM
    nb = t // seq

    def body(dx_ref, gt_ref, t_ref, w_ref, osb, odl, onsb, ondl, gsb, gdl, m_ref,
             dosb, dodl, dgt_ref, dgsb, dgdl, dw_ref):
        m = pl.program_id(0)
        dxv = dx_ref[...]
        dt = (gt_ref[...] * dxv).astype(BF)
        _acc_rows(dgt_ref, jnp.sum(dxv * t_ref[...], axis=0, keepdims=True), m % per == 0)
        mv = m_ref[...]
        don_sb = _nt(dt, w_ref[0:D_GRP, :])
        don_dl = _nt(dt, w_ref[D_GRP:2 * D_GRP, :])
        do1, dg1 = _headnorm_bwd(don_sb, osb[...], gsb[...], mv)
        do2, dg2 = _headnorm_bwd(don_dl, odl[...], gdl[...], mv)
        dosb[...] = do1
        dodl[...] = do2
        _acc_rows(dgsb, dg1, m == 0)
        _acc_rows(dgdl, dg2, m == 0)
        p1 = _tn(onsb[...], dt)
        p2 = _tn(ondl[...], dt)

        @pl.when(m == 0)
        def _():
            dw_ref[0:D_GRP, :] = p1
            dw_ref[D_GRP:2 * D_GRP, :] = p2

        @pl.when(m != 0)
        def _():
            dw_ref[0:D_GRP, :] += p1
            dw_ref[D_GRP:2 * D_GRP, :] += p2

    row = pl.BlockSpec((TM, d), lambda m: (m, 0))
    half = pl.BlockSpec((TM, D_GRP), lambda m: (m, 0))
    ex = pl.BlockSpec((None, 1, d), lambda m: (m // per, 0, 0))
    gvec = pl.BlockSpec((1, D_GRP), lambda m: (0, 0))
    wblk = pl.BlockSpec((2 * D_GRP, d), lambda m: (0, 0))
    return _call(
        body, "mix_bwd_out", (t // TM,),
        [row, ex, row, wblk, half, half, half, half, gvec, gvec, pl.BlockSpec((D_GRP, D_GRP), lambda m: (0, 0))],
        [half, half, ex, gvec, gvec, wblk],
        [_sds((t, D_GRP), F32), _sds((t, D_GRP), F32), _sds((nb, 1, d), F32),
         _sds((1, D_GRP), F32), _sds((1, D_GRP), F32), _sds((2 * D_GRP, d), F32)],
        (dx, gt, tv, w_out, o_sb, o_dil, on_sb, on_dil, g_sb, g_dil, ones_g), carry=carry)


def _dw_in(h, dqkv6, carry=None):
    t, d = h.shape
    wc = 6 * D_GRP // N_CHIPS

    def body(h_ref, g_ref, o_ref):
        kt = pl.program_id(0)
        hv = h_ref[...]
        for j in range(N_CHIPS):
            p = _tn(hv, _chip_cols(g_ref, j, wc))

            @pl.when(kt == 0)
            def _(p=p, j=j):
                o_ref[j, 0] = p

            @pl.when(kt != 0)
            def _(p=p, j=j):
                o_ref[j, 0] += p

    return _call(
        body, "dw_in", (t // TM,),
        [pl.BlockSpec((TM, d), lambda kt: (kt, 0)), pl.BlockSpec((6, TM, D_GRP), lambda kt: (0, kt, 0))],
        [pl.BlockSpec((N_CHIPS, 1, d, wc), lambda kt: (0, 0, 0, 0))],
        [_sds((N_CHIPS, 1, d, wc), F32)], (h, dqkv6), carry=carry)


def _mix_bwd_dh(dqkv6, w_in, x, g, sc, dxo, seq, carry=None):
    _, t, _ = dqkv6.shape
    d = x.shape[-1]
    wc = w_in.shape[-1]
    per = seq // TM
    nb = t // seq

    def body(g6_ref, w_ref, x_ref, g_ref, sc_ref, dxo_ref, dx_ref, dsh_ref, dsc_ref, dg_ref):
        m = pl.program_id(0)
        dh = _nt(_chip_cols(g6_ref, 0, wc), w_ref[0, 0])
        for j in range(1, N_CHIPS):
            dh = dh + _nt(_chip_cols(g6_ref, j, wc), w_ref[j, 0])
        dx, dsh, dsc, dg = _modnorm_bwd_tile(dh, x_ref[...], g_ref[...], sc_ref[...], dxo_ref[...])
        dx_ref[...] = dx
        _acc_rows(dsh_ref, dsh, m % per == 0)
        _acc_rows(dsc_ref, dsc, m % per == 0)
        _acc_rows(dg_ref, dg, m == 0)

    row = pl.BlockSpec((TM, d), lambda m: (m, 0))
    ex = pl.BlockSpec((None, 1, d), lambda m: (m // per, 0, 0))
    vec = pl.BlockSpec((1, d), lambda m: (0, 0))
    return _call(
        body, "mix_bwd_dh", (t // TM,),
        [pl.BlockSpec((6, TM, D_GRP), lambda m: (0, m, 0)), _whole(w_in), row, vec, ex, row],
        [row, ex, ex, vec],
        [_sds((t, d), F32), _sds((nb, 1, d), F32), _sds((nb, 1, d), F32), _sds((1, d), F32)],
        (dqkv6, w_in, x, g, sc, dxo), carry=carry)


def _ffn_down_loss(s, wd, x, gt, seq, coef, g, target):
    _, t, fs = s.shape
    d = x.shape[-1]
    per = seq // TM
    steps = t // TM

    def body(s_ref, w_ref, x_ref, gt_ref, g_ref, t_ref, f_ref, dx_ref, dg_ref, loss_ref, lacc):
        m = pl.program_id(0)
        f = _nn(s_ref[0], w_ref[0, 0])
        for j in range(1, N_CHIPS):
            f = f + _nn(s_ref[j], w_ref[j, 0])
        f_ref[...] = f
        xv = x_ref[...] + (coef * gt_ref[...]) * f
        gv = g_ref[...]
        r = lax.rsqrt(jnp.mean(xv * xv, axis=-1, keepdims=True) + EPS)
        n = xv * r
        err = n * gv - t_ref[...]
        dy = err * (1.0 / d)
        _acc_rows(dg_ref, jnp.sum(dy * n, axis=0, keepdims=True), m == 0)
        dn = dy * gv
        dx_ref[...] = r * (dn - n * jnp.mean(dn * n, axis=-1, keepdims=True))
        _acc_rows(lacc, jnp.sum(err * err, axis=0, keepdims=True), m == 0)

        @pl.when(m == steps - 1)
        def _():
            tot = jnp.sum(lacc[...], axis=-1, keepdims=True) * (0.5 / d)
            loss_ref[...] = jnp.broadcast_to(tot, (1, 128))

    row = pl.BlockSpec((TM, d), lambda m: (m, 0))
    vec = pl.BlockSpec((1, d), lambda m: (0, 0))
    return pl.pallas_call(
        body, name="ffn_down_loss", grid=(steps,),
        in_specs=[pl.BlockSpec((N_CHIPS, TM, fs), lambda m: (0, m, 0)), _whole(wd), row,
                  pl.BlockSpec((None, 1, d), lambda m: (m // per, 0, 0)), vec, row],
        out_specs=[row, row, vec, pl.BlockSpec((1, 128), lambda m: (0, 0))],
        out_shape=[_sds((t, d), F32), _sds((t, d), F32), _sds((1, d), F32), _sds((1, 128), F32)],
        scratch_shapes=[pltpu.VMEM((1, d), F32)],
        compiler_params=_cp(1))(s, wd, x, gt, g, target)


def _row_tile(rows, cols):
    best = rows
    for tr in range(8, rows + 1, 8):
        if rows % tr == 0 and tr * cols * 4 <= (1 << 20):
            best = tr
    if best * cols * 4 > (1 << 21):
        best = 8
    return best


def _adamw(w, g_arr, g_sel, m, v):
    rows, cols = w.shape
    tr = _row_tile(rows, cols)
    b1c = 1.0 - ADAM_B1 ** ADAM_STEP
    b2c = 1.0 - ADAM_B2 ** ADAM_STEP

    def body(w_ref, g_ref, m_ref, v_ref, go_ref, d_ref, mo_ref, vo_ref):
        gv = g_ref[...]
        mn = ADAM_B1 * m_ref[...] + (1.0 - ADAM_B1) * gv
        vn = ADAM_B2 * v_ref[...] + (1.0 - ADAM_B2) * (gv * gv)
        go_ref[...] = gv
        mo_ref[...] = mn
        vo_ref[...] = vn
        d_ref[...] = -ADAM_LR * ((mn / b1c) / (jnp.sqrt(vn / b2c) + ADAM_EPS) + ADAM_WD * w_ref[...])

    blk = pl.BlockSpec((tr, cols), lambda i: (i, 0))
    shp = _sds((rows, cols), F32)
    return pl.pallas_call(
        body, name="adamw", grid=(rows // tr,),
        in_specs=[blk, pl.BlockSpec((None, tr, cols), lambda i: (g_sel, i, 0)), blk, blk],
        out_specs=[blk] * 4, out_shape=[shp] * 4,
        compiler_params=_cp(1))(w, g_arr, m, v)


def _flip(v, bit):
    return 1 - v if bit else v


def _my_place():
    x, y, c = lax.axis_index("x"), lax.axis_index("y"), lax.axis_index("c")
    return x, y, c


class _Exchange:
    def __init__(self, operands, out_shape, aliases, sems, start, finish):
        self.operands, self.out_shape, self.aliases, self.sems = list(operands), list(out_shape), dict(aliases), list(sems)
        self.start, self.finish = start, finish


def _join(exchanges):
    exchanges = [e for e in exchanges if e is not None]
    if not exchanges:
        return None
    ops, outs, sems, aliases, spans = [], [], [], {}, []
    for e in exchanges:
        spans.append((len(ops), len(outs), len(sems), e))
        for i, j in e.aliases.items():
            aliases[len(ops) + i] = len(outs) + j
        ops += e.operands
        outs += e.out_shape
        sems += e.sems

    def run(which):
        def go(ins, res, sm):
            for io, oo, so, e in spans:
                getattr(e, which)(ins[io:io + len(e.operands)], res[oo:oo + len(e.out_shape)], sm[so:so + len(e.sems)])
        return go

    return _Exchange(ops, outs, aliases, sems, run("start"), run("finish"))


def _call(body, name, grid, in_specs, out_specs, out_shape, args, scratch=(), carry=None):
    in_specs, out_specs, out_shape, scratch = list(in_specs), list(out_specs), list(out_shape), list(scratch)
    if carry is None:
        return pl.pallas_call(body, name=name, grid=grid, in_specs=in_specs, out_specs=out_specs,
                              out_shape=out_shape, scratch_shapes=scratch,
                              compiler_params=_cp(len(grid)))(*args)
    n_in, n_out, n_s = len(in_specs), len(out_specs), len(scratch)
    c_in, c_out = len(carry.operands), len(carry.out_shape)
    any_spec = pl.BlockSpec(memory_space=pl.ANY)

    def wrapped(*refs):
        ins, cins = refs[:n_in], refs[n_in:n_in + c_in]
        o0 = n_in + c_in
        outs, couts = refs[o0:o0 + n_out], refs[o0 + n_out:o0 + n_out + c_out]
        s0 = o0 + n_out + c_out
        scr, sems = refs[s0:s0 + n_s], refs[s0 + n_s:]
        first = pl.program_id(0) == 0
        last = pl.program_id(0) == grid[0] - 1
        for ax in range(1, len(grid)):
            first = jnp.logical_and(first, pl.program_id(ax) == 0)
            last = jnp.logical_and(last, pl.program_id(ax) == grid[ax] - 1)

        @pl.when(first)
        def _():
            carry.start(cins, couts, sems)

        body(*ins, *outs, *scr)

        @pl.when(last)
        def _():
            carry.finish(cins, couts, sems)

    return pl.pallas_call(
        wrapped, name=name, grid=grid, in_specs=in_specs + [any_spec] * c_in,
        out_specs=out_specs + [any_spec] * c_out, out_shape=out_shape + carry.out_shape,
        scratch_shapes=scratch + carry.sems,
        input_output_aliases={n_in + i: n_out + j for i, j in carry.aliases.items()},
        compiler_params=_cp(len(grid)))(*args, *carry.operands)


def _whole_call(body, name, args, out_shape, scratch, carry=None):
    vm = pl.BlockSpec(memory_space=pltpu.VMEM)
    any_spec = pl.BlockSpec(memory_space=pl.ANY)
    out_shape, scratch = list(out_shape), list(scratch)
    n_in, n_out, n_s = len(args), len(out_shape), len(scratch)
    if carry is None:
        return pl.pallas_call(body, name=name, in_specs=[vm] * n_in, out_specs=[vm] * n_out, out_shape=out_shape,
                              scratch_shapes=scratch, compiler_params=_cp())(*args)
    c_in, c_out = len(carry.operands), len(carry.out_shape)

    def wrapped(*refs):
        ins, cins = refs[:n_in], refs[n_in:n_in + c_in]
        o0 = n_in + c_in
        outs, couts = refs[o0:o0 + n_out], refs[o0 + n_out:o0 + n_out + c_out]
        s0 = o0 + n_out + c_out
        scr, sems = refs[s0:s0 + n_s], refs[s0 + n_s:]
        carry.start(cins, couts, sems)
        body(*ins, *outs, *scr)
        carry.finish(cins, couts, sems)

    return pl.pallas_call(
        wrapped, name=name, in_specs=[vm] * n_in + [any_spec] * c_in, out_specs=[vm] * n_out + [any_spec] * c_out,
        out_shape=out_shape + carry.out_shape, scratch_shapes=scratch + carry.sems,
        input_output_aliases={n_in + i: n_out + j for i, j in carry.aliases.items()},
        compiler_params=_cp())(*args, *carry.operands)


def _alone(name, ex):
    any_spec = pl.BlockSpec(memory_space=pl.ANY)
    c_in, c_out = len(ex.operands), len(ex.out_shape)

    def body(*refs):
        ins, outs, sems = refs[:c_in], refs[c_in:c_in + c_out], refs[c_in + c_out:]
        ex.start(ins, outs, sems)
        ex.finish(ins, outs, sems)

    return pl.pallas_call(
        body, name=name, in_specs=[any_spec] * c_in, out_specs=[any_spec] * c_out, out_shape=ex.out_shape,
        scratch_shapes=ex.sems, input_output_aliases=ex.aliases, compiler_params=_cp())(*ex.operands)


def _ada_fwd(c_pad, w_ada, b_shard, carry=None):
    d = c_pad.shape[-1]
    cols = w_ada.shape[-1]
    chunk = 384

    def body(c_ref, w_ref, b_ref, call_ref, mod_ref, part, s1, r1, s2, r2):
        x, y, c = _my_place()
        dev = 4 * x + 2 * y + c
        chip = 2 * x + y
        call_ref[dev] = c_ref[...]

        def c_copy(k):
            px, py, pc = _flip(x, (k >> 2) & 1), _flip(y, (k >> 1) & 1), _flip(c, k & 1)
            return px, py, pc

        sends = []
        for k in range(1, N_DEV):
            px, py, pc = c_copy(k)
            cp = pltpu.make_async_remote_copy(src_ref=c_ref, dst_ref=call_ref.at[dev], send_sem=s1.at[k - 1],
                                              recv_sem=r1.at[k - 1], device_id=(px, py, pc), device_id_type=MESH)
            cp.start()
            sends.append(cp)
        for k in range(1, N_DEV):
            px, py, pc = c_copy(k)
            pltpu.make_async_remote_copy(src_ref=c_ref, dst_ref=call_ref.at[4 * px + 2 * py + pc],
                                         send_sem=s1.at[k - 1], recv_sem=r1.at[k - 1],
                                         device_id=(px, py, pc), device_id_type=MESH).wait_recv()
        for cp in sends:
            cp.wait_send()

        cs = call_ref[...].reshape(N_DEV * 8, d)
        sc = (cs * jax.nn.sigmoid(cs)).astype(BF)
        for n0 in range(0, cols, chunk):
            blk = _nn(sc, w_ref[:, n0:n0 + chunk].astype(BF)) + b_ref[:, n0:n0 + chunk]
            part[:, :, n0:n0 + chunk] = blk.reshape(N_DEV, 8, chunk)

        mod_ref[chip] = part[dev]
        sends = []
        for kk in range(1, N_CHIPS):
            px, py = _flip(x, (kk >> 1) & 1), _flip(y, kk & 1)
            cp = pltpu.make_async_remote_copy(src_ref=part.at[4 * px + 2 * py + c], dst_ref=mod_ref.at[chip],
                                              send_sem=s2.at[kk - 1], recv_sem=r2.at[kk - 1],
                                              device_id=(px, py, c), device_id_type=MESH)
            cp.start()
            sends.append(cp)
        for kk in range(1, N_CHIPS):
            px, py = _flip(x, (kk >> 1) & 1), _flip(y, kk & 1)
            pltpu.make_async_remote_copy(src_ref=part.at[dev], dst_ref=mod_ref.at[2 * px + py],
                                         send_sem=s2.at[kk - 1], recv_sem=r2.at[kk - 1],
                                         device_id=(px, py, c), device_id_type=MESH).wait_recv()
        for cp in sends:
            cp.wait_send()

    return _whole_call(
        body, "ada_fwd", (c_pad, w_ada, b_shard),
        [_sds((N_DEV, 8, d), F32), _sds((N_CHIPS, 8, cols), F32)],
        [pltpu.VMEM((N_DEV, 8, cols), F32),
         pltpu.SemaphoreType.DMA((N_DEV - 1,)), pltpu.SemaphoreType.DMA((N_DEV - 1,)),
         pltpu.SemaphoreType.DMA((N_CHIPS - 1,)), pltpu.SemaphoreType.DMA((N_CHIPS - 1,))], carry=carry)


def _ag_weights(bufs):
    n = len(bufs)

    def place():
        x, y, c = _my_place()
        others = [(_flip(x, (kk >> 1) & 1), _flip(y, kk & 1)) for kk in range(1, N_CHIPS)]
        return x, y, c, 2 * x + y, others

    def half(b, which):
        hr = bufs[b].shape[2] // 2
        return pl.ds(pl.multiple_of(which * hr, 16), hr)

    def ici(outs, sems, b, i, slot, x, y, c, px, py):
        rows = outs[b].at[slot, :, half(b, c), :]
        return pltpu.make_async_remote_copy(
            src_ref=rows, dst_ref=rows, send_sem=sems[0].at[3 * b + i], recv_sem=sems[1].at[3 * b + i],
            device_id=(px, py, c), device_id_type=MESH)

    def d2d(outs, sems, b, i, slot, x, y, c, which):
        rows = outs[b].at[slot, :, half(b, which), :]
        return pltpu.make_async_remote_copy(
            src_ref=rows, dst_ref=rows, send_sem=sems[2].at[3 * b + i], recv_sem=sems[3].at[3 * b + i],
            device_id=(x, y, 1 - c), device_id_type=MESH)

    def start(ins, outs, sems):
        x, y, c, chip, others = place()
        for b in range(n):
            for i, (px, py) in enumerate(others):
                ici(outs, sems, b, i, chip, x, y, c, px, py).start()

    def finish(ins, outs, sems):
        x, y, c, chip, others = place()
        for b in range(n):
            for i, (px, py) in enumerate(others):
                ici(outs, sems, b, i, 2 * px + py, x, y, c, px, py).wait_recv()
                d2d(outs, sems, b, i, 2 * px + py, x, y, c, c).start()
        for b in range(n):
            for i, (px, py) in enumerate(others):
                d2d(outs, sems, b, i, 2 * px + py, x, y, c, 1 - c).wait_recv()
        for b in range(n):
            for i, (px, py) in enumerate(others):
                ici(outs, sems, b, i, chip, x, y, c, px, py).wait_send()
                d2d(outs, sems, b, i, 2 * px + py, x, y, c, c).wait_send()

    return _Exchange(bufs, [_sds(s.shape, s.dtype) for s in bufs], {i: i for i in range(n)},
                     [pltpu.SemaphoreType.DMA((3 * n,))] * 4, start, finish)


def _rs_d2d(grads):
    n = len(grads)

    def copy(ins, outs, sems, b):
        x, y, c = _my_place()
        hr = grads[b].shape[2] // 2
        theirs = pl.ds(pl.multiple_of((1 - c) * hr, 8), hr)
        return pltpu.make_async_remote_copy(
            src_ref=ins[b].at[:, :, theirs, :], dst_ref=outs[b], send_sem=sems[0].at[b], recv_sem=sems[1].at[b],
            device_id=(x, y, 1 - c), device_id_type=MESH)

    def start(ins, outs, sems):
        for b in range(n):
            copy(ins, outs, sems, b).start()

    def finish(ins, outs, sems):
        for b in range(n):
            copy(ins, outs, sems, b).wait()

    return _Exchange(grads, [_sds(g.shape[:2] + (g.shape[2] // 2, g.shape[3]), F32) for g in grads], {},
                     [pltpu.SemaphoreType.DMA((n,))] * 2, start, finish)


def _add_halves(core, g, land):
    nchip, ng, rows, cols = g.shape
    hr = rows // 2
    tr = _row_tile(hr, cols)
    steps = hr // tr

    def body(core_ref, g_ref, l_ref, o_ref):
        del core_ref
        o_ref[...] = (g_ref[...] + l_ref[...]).astype(BF)

    return pl.pallas_call(
        body, name="add_halves",
        grid_spec=pltpu.PrefetchScalarGridSpec(
            num_scalar_prefetch=1, grid=(nchip, ng, steps),
            in_specs=[pl.BlockSpec((None, None, tr, cols), lambda j, a, i, cr: (j, a, cr[0] * steps + i, 0)),
                      pl.BlockSpec((None, None, tr, cols), lambda j, a, i, cr: (j, a, i, 0))],
            out_specs=pl.BlockSpec((None, None, tr, cols), lambda j, a, i, cr: (j, a, i, 0))),
        out_shape=_sds((nchip, ng, hr, cols), BF),
        compiler_params=_cp(3))(core, g, land)


def _rs_ici(parts):
    n = len(parts)

    def copies(ins, outs, sems):
        x, y, c = _my_place()
        chip = 2 * x + y
        for b in range(n):
            for kk in range(1, N_CHIPS):
                px, py = _flip(x, (kk >> 1) & 1), _flip(y, kk & 1)
                k = 3 * b + kk - 1
                send = pltpu.make_async_remote_copy(
                    src_ref=ins[b].at[2 * px + py], dst_ref=outs[b].at[chip],
                    send_sem=sems[0].at[k], recv_sem=sems[1].at[k], device_id=(px, py, c), device_id_type=MESH)
                slot = outs[b].at[2 * px + py]
                recv = pltpu.make_async_remote_copy(
                    src_ref=slot, dst_ref=slot, send_sem=sems[0].at[k], recv_sem=sems[1].at[k],
                    device_id=(px, py, c), device_id_type=MESH)
                yield send, recv

    def start(ins, outs, sems):
        for send, _ in copies(ins, outs, sems):
            send.start()

    def finish(ins, outs, sems):
        for send, recv in copies(ins, outs, sems):
            recv.wait_recv()
            send.wait_send()

    return _Exchange(parts, [_sds(p.shape, p.dtype) for p in parts], {},
                     [pltpu.SemaphoreType.DMA((3 * n,))] * 2, start, finish)


def _sum_chips(place, part, land):
    nchip, ng, hr, cols = land.shape
    tr = _row_tile(hr, cols)
    steps = hr // tr

    def body(place_ref, p_ref, l1, l2, l3, o_ref):
        del place_ref
        o_ref[...] = ((p_ref[...].astype(F32) + l1[...].astype(F32)) + l2[...].astype(F32)) + l3[...].astype(F32)

    def slot(k):
        return pl.BlockSpec((None, None, tr, cols), lambda a, i, pr: (jnp.bitwise_xor(pr[1], k), a, i, 0))

    return pl.pallas_call(
        body, name="sum_chips",
        grid_spec=pltpu.PrefetchScalarGridSpec(
            num_scalar_prefetch=1, grid=(ng, steps),
            in_specs=[slot(0), slot(1), slot(2), slot(3)],
            out_specs=pl.BlockSpec((None, tr, cols), lambda a, i, pr: (a, pr[0] * steps + i, 0))),
        out_shape=_sds((ng, 2 * hr, cols), F32),
        compiler_params=_cp(2))(place, part, land, land, land)


def _rs_final(bufs):
    n = len(bufs)

    def copy(outs, sems, b, which):
        x, y, c = _my_place()
        hr = bufs[b].shape[1] // 2
        rows = outs[b].at[:, pl.ds(pl.multiple_of((c if which == 0 else 1 - c) * hr, 8), hr), :]
        return pltpu.make_async_remote_copy(
            src_ref=rows, dst_ref=rows, send_sem=sems[0].at[b], recv_sem=sems[1].at[b],
            device_id=(x, y, 1 - c), device_id_type=MESH)

    def start(ins, outs, sems):
        for b in range(n):
            copy(outs, sems, b, 0).start()

    def finish(ins, outs, sems):
        for b in range(n):
            copy(outs, sems, b, 0).wait_send()
            copy(outs, sems, b, 1).wait_recv()

    return _Exchange(bufs, [_sds(h.shape, F32) for h in bufs], {i: i for i in range(n)},
                     [pltpu.SemaphoreType.DMA((n,))] * 2, start, finish)


def _small_sync(smalls, dmod_blk, c_all, carry=None):
    d = c_all.shape[-1]
    cols = dmod_blk.shape[-1]
    chunk = 384

    def body(sm_ref, dm_ref, c_ref, sum_ref, gw_ref, sm_all, dm_all, ssem, rsem):
        x, y, c = _my_place()
        dev = 4 * x + 2 * y + c
        chip = 2 * x + y
        sm_all[dev] = sm_ref[...]
        dm_all[dev] = dm_ref[chip]
        sends = []
        for k in range(1, N_DEV):
            px, py, pc = _flip(x, (k >> 2) & 1), _flip(y, (k >> 1) & 1), _flip(c, k & 1)
            a = pltpu.make_async_remote_copy(src_ref=sm_ref, dst_ref=sm_all.at[dev], send_sem=ssem.at[2 * (k - 1)],
                                             recv_sem=rsem.at[2 * (k - 1)], device_id=(px, py, pc),
                                             device_id_type=MESH)
            b = pltpu.make_async_remote_copy(src_ref=dm_ref.at[2 * px + py], dst_ref=dm_all.at[dev],
                                             send_sem=ssem.at[2 * (k - 1) + 1], recv_sem=rsem.at[2 * (k - 1) + 1],
                                             device_id=(px, py, pc), device_id_type=MESH)
            a.start()
            b.start()
            sends += [a, b]
        for k in range(1, N_DEV):
            px, py, pc = _flip(x, (k >> 2) & 1), _flip(y, (k >> 1) & 1), _flip(c, k & 1)
            pdev = 4 * px + 2 * py + pc
            pltpu.make_async_remote_copy(src_ref=sm_ref, dst_ref=sm_all.at[pdev], send_sem=ssem.at[2 * (k - 1)],
                                         recv_sem=rsem.at[2 * (k - 1)], device_id=(px, py, pc),
                                         device_id_type=MESH).wait_recv()
            pltpu.make_async_remote_copy(src_ref=dm_ref.at[chip], dst_ref=dm_all.at[pdev],
                                         send_sem=ssem.at[2 * (k - 1) + 1], recv_sem=rsem.at[2 * (k - 1) + 1],
                                         device_id=(px, py, pc), device_id_type=MESH).wait_recv()
        for cp in sends:
            cp.wait_send()

        tot = sm_all[0]
        for q in range(1, N_DEV):
            tot = tot + sm_all[q]
        sum_ref[...] = tot

        cs = c_ref[...].reshape(N_DEV * 8, d)
        sc = (cs * jax.nn.sigmoid(cs)).astype(BF)
        for n0 in range(0, cols, chunk):
            dmv = dm_all[:, :, n0:n0 + chunk].reshape(N_DEV * 8, chunk).astype(BF)
            gw_ref[:, n0:n0 + chunk] = _tn(sc, dmv)

    return _whole_call(
        body, "small_sync", (smalls, dmod_blk, c_all),
        [_sds(smalls.shape, F32), _sds((d, cols), F32)],
        [pltpu.VMEM((N_DEV,) + smalls.shape, F32), pltpu.VMEM((N_DEV, 8, cols), F32),
         pltpu.SemaphoreType.DMA((2 * (N_DEV - 1),)), pltpu.SemaphoreType.DMA((2 * (N_DEV - 1),))], carry=carry)


def _bucket_onehot():
    maps = np.stack([_bucket_map(dil).reshape(-1) for _, dil in DIL_CONFIGS])
    return (jnp.asarray(maps)[:, None, :] == jnp.arange(N_BUCKETS, dtype=jnp.int32)[None, :, None]).astype(BF)


def _dil_bias(rel_t, onehot):
    def body(r_ref, oh_ref, o_ref):
        rv = r_ref[...]
        hi = rv.astype(BF)
        lo = (rv - hi.astype(F32)).astype(BF)
        for c in range(len(DIL_CONFIGS)):
            o_ref[c] = _nn(hi, oh_ref[c]) + _nn(lo, oh_ref[c])

    return pl.pallas_call(body, name="dil_bias",
                          out_shape=_sds((len(DIL_CONFIGS), N_HEADS, BLOCK * 2 * BLOCK), F32),
                          compiler_params=_cp())(rel_t, onehot)


def _rowsum8(a):
    def body(a_ref, o_ref):
        o_ref[...] = jnp.sum(a_ref[...], axis=0, keepdims=True)

    return pl.pallas_call(body, name="rowsum8", out_shape=_sds((1, a.shape[1]), F32), compiler_params=_cp())(a)


def _local_step(x, mod, target, w, gains, rel_bias, place=None):
    nb, seq, d = x.shape
    t = nb * seq
    dist = place is not None
    core = place[0:1] if dist else None
    x0 = x.reshape(t, d)
    tgt = target.reshape(t, d)
    md = [mod[:, i:i + 1, :] for i in range(N_MOD)]
    sh1, sc1, gt1, sh2, sc2, gt2, sh3, sc3, gt3 = md
    g1, g2, g3 = gains["g_ffn1"], gains["g_mix"], gains["g_ffn2"]
    ones_g = _group_ones()

    def partial_sums(grads, lands):
        return [_add_halves(core, g, l) for g, l in zip(grads, lands)]

    def chip_sums(parts, lands):
        return [_sum_chips(place, p, l) for p, l in zip(parts, lands)]

    res = _ffn_up(x0, g1, sc1, sh1, w["gu1"], seq,
                  carry=_ag_weights([w["d1"], w["win"], w["wout"]]) if dist else None)
    h1, a1, u1, s1 = res[:4]
    wd1, w_in, w_out = res[4:] if dist else (w["d1"], w["win"], w["wout"])
    w_out2 = w_out.reshape(2 * D_GRP, d)
    f1, x1 = _ffn_down(s1, wd1, x0, gt1, seq, 0.5)

    h2, qkv6, qkv_r4, qkv_r16 = _qkv_proj(x1, g2, sc2, sh2, w_in, seq)
    qkv6b = qkv6.reshape(6, nb, seq, D_GRP)
    res = _sb_fwd(qkv6b, gains["g_sb_out"], nb, seq, carry=_ag_weights([w["gu2"], w["d2"]]) if dist else None)
    o_sb, on_sb = res[:2]
    wgu2, wd2 = res[2:] if dist else (w["gu2"], w["d2"])
    onehot = _bucket_onehot()
    bias = _dil_bias(rel_bias.T, onehot).reshape(len(DIL_CONFIGS), N_HEADS, BLOCK, 2 * BLOCK)
    o_cs, l_cs = [], []
    qkv_rs = [(qkv6b, 3), (qkv_r4, 0), (qkv_r16, 0)]
    for ci, (_, dil) in enumerate(DIL_CONFIGS):
        sub = seq // dil
        arr, base = qkv_rs[ci]
        arr = arr.reshape(base + 3, nb, sub, dil * D_GRP)
        qkv_rs[ci] = (arr, base)
        o_c, l_c = _dil_fwd(arr, base, bias[ci], nb, sub, dil)
        o_cs.append(o_c.reshape(t // dil, dil * D_GRP))
        l_cs.append(l_c.reshape(t // dil, dil * D_GRP))
    o_dil, on_dil = _dil_comb(o_cs, l_cs, gains["g_dil_out"])
    tmix, x2 = _mix_out(on_sb.reshape(t, D_GRP), on_dil, w_out2, x1, gt2, seq)

    h3, a3, u3, s3 = _ffn_up(x2, g3, sc3, sh3, wgu2, seq)
    f3, dx3, dg_final, loss = _ffn_down_loss(s3, wd2, x2, gt3, seq, 0.5, gains["g_final"], tgt)

    da3, du3, df3, dgt3, dx2, dsh3, dsc3, dg3 = _ffn_bwd_x(dx3, gt3, f3, wd2, a3, u3, wgu2, x2, g3, sc3, seq, 0.5)
    grads2 = [_ffn_bwd_w(h3, da3, du3, s3, df3)]

    res = _mix_bwd_out(
        dx2, gt2, tmix, w_out2, o_sb.reshape(t, D_GRP), o_dil, on_sb.reshape(t, D_GRP), on_dil,
        gains["g_sb_out"], gains["g_dil_out"], ones_g, seq, carry=_rs_d2d(grads2) if dist else None)
    do_sb, do_dil, dgt2, dg_sb, dg_dil, dw_out = res[:6]
    parts2 = partial_sums(grads2, res[6:]) if dist else None
    dw_out = dw_out.reshape(N_CHIPS, 1, 2 * D_GRP // N_CHIPS, d)
    res = _sb_bwd(qkv6b, do_sb.reshape(nb, seq, D_GRP), nb, seq, carry=_rs_ici(parts2) if dist else None)
    dqkv6 = res[0]
    halves2 = chip_sums(parts2, res[1:]) if dist else None
    dcs = _dil_comb_bwd(do_dil, o_cs, l_cs)
    dsum, a_tiles = [], []
    for ci, (_, dil) in enumerate(DIL_CONFIGS):
        sub = seq // dil
        do_c = dcs[ci].reshape(nb, sub, dil * D_GRP)
        dd_c = dcs[3 + ci].reshape(nb, sub, dil * D_GRP)
        res = _dil_bwd(qkv_rs[ci][0], qkv_rs[ci][1], bias[ci], do_c, dd_c, nb, sub, dil,
                       carry=_rs_final(halves2) if dist and ci == 0 else None)
        if dist and ci == 0:
            grads2 = res[2:]
        dsum.append(res[0].reshape(3, t // dil, dil * D_GRP))
        a_tiles.append(res[1].reshape(N_HEADS, BLOCK * 2 * BLOCK))
    dqkv6 = _dqkv_dil_sum(dsum, dqkv6.reshape(6, t, D_GRP))
    drel = _relbias_grad(jnp.stack(a_tiles), onehot)
    dx1, dsh2, dsc2, dg2 = _mix_bwd_dh(dqkv6, w_in, x1, g2, sc2, dx2, seq)

    da1, du1, df1, dgt1 = _ffn_bwd_ds(dx1, gt1, f1, wd1, a1, u1, seq, 0.5)
    grads1 = [_ffn_bwd_w(h1, da1, du1, s1, df1)]
    res = _dw_in(h2, dqkv6, carry=_rs_d2d(grads1) if dist else None)
    grads_m = [res[0], dw_out]
    parts1 = partial_sums(grads1, res[1:]) if dist else None
    res = _ffn_bwd_dh(da1, du1, w["gu1"], x0, g1, sc1, dx1, seq,
                      carry=_join([_rs_ici(parts1), _rs_d2d(grads_m)]) if dist else None)
    dx0, dsh1, dsc1, dg1 = res[:4]
    pending = None
    if dist:
        pending = (chip_sums(parts1, res[4:5]), partial_sums(grads_m, res[5:7]))

    dmod = jnp.concatenate([dsh1, dsc1, dgt1, dsh2, dsc2, dgt2, dsh3, dsc3, dgt3], axis=1)
    return dict(grad_x=dx0.reshape(nb, seq, d), loss=loss[0, 0], dmod=dmod.reshape(nb, N_MOD * d),
                dffn1=grads1[0], dffn2=grads2[0], dwin=grads_m[0], dwout=grads_m[1], pending=pending,
                dg_ffn1=dg1, dg_mix=dg2, dg_ffn2=dg3, dg_final=dg_final, dg_sb=dg_sb, dg_dil=dg_dil,
                drel=drel.T)


_SMALL_ORDER = (("b_ada", N_MOD * 1024), ("g_ffn1", 1024), ("g_mix", 1024), ("g_ffn2", 1024), ("g_final", 1024),
                ("g_sb_out", D_GRP), ("g_dil_out", D_GRP), ("rel_bias", N_BUCKETS * N_HEADS))


def _pack_small(parts, extra=None):
    flat = [parts[name].reshape(-1).astype(F32) for name, _ in _SMALL_ORDER]
    used = sum(sz for _, sz in _SMALL_ORDER)
    pad = SMALL_ROWS * 128 - used
    tail = jnp.zeros((pad,), F32)
    if extra is not None:
        tail = tail.at[0].set(extra)
    return jnp.concatenate(flat + [tail]).reshape(SMALL_ROWS, 128)


def _unpack_small(packed, shapes):
    flat = packed.reshape(-1)
    out, off = {}, 0
    for name, sz in _SMALL_ORDER:
        out[name] = flat[off:off + sz].reshape(shapes[name])
        off += sz
    return out, flat[off]


def kernel(x, c, w_ada, b_ada, g_ffn1, w1_gate, w1_up, w1_down, g_mix, w_in, g_sb_out, g_dil_out, w_out, rel_bias, g_ffn2, w2_gate, w2_up, w2_down, g_final, loss_target, m_w_ada, m_b_ada, m_g_ffn1, m_w1_gate, m_w1_up, m_w1_down, m_g_mix, m_w_in, m_g_sb_out, m_g_dil_out, m_w_out, m_rel_bias, m_g_ffn2, m_w2_gate, m_w2_up, m_w2_down, m_g_final, v_w_ada, v_b_ada, v_g_ffn1, v_w1_gate, v_w1_up, v_w1_down, v_g_mix, v_w_in, v_g_sb_out, v_g_dil_out, v_w_out, v_rel_bias, v_g_ffn2, v_w2_gate, v_w2_up, v_w2_down, v_g_final):
    nb, seq, d = x.shape
    xi, yi, ci = lax.axis_index("x"), lax.axis_index("y"), lax.axis_index("c")
    chip = 2 * xi + yi
    ada_cols = w_ada.shape[-1]

    c_pad = jnp.zeros((8, d), F32).at[:nb].set(c)
    b_shard = lax.dynamic_slice(b_ada, (0, chip * ada_cols), (1, ada_cols))
    shards = dict(gu1=jnp.stack([w1_gate[0], w1_up[0]]), d1=w1_down, win=w_in, wout=w_out,
                  gu2=jnp.stack([w2_gate[0], w2_up[0]]), d2=w2_down)
    bufs = {k: lax.dynamic_update_slice(lax.empty((N_CHIPS,) + s.shape, BF), s.astype(BF)[None], (chip, 0, 0, 0))
            for k, s in shards.items()}
    c_all, mod_blk, bufs["gu1"] = _ada_fwd(c_pad, w_ada[0], b_shard, carry=_ag_weights([bufs["gu1"]]))
    mod = jnp.transpose(mod_blk[:, :nb, :], (1, 0, 2)).reshape(nb, N_MOD, d)

    gains = dict(g_ffn1=g_ffn1, g_mix=g_mix, g_ffn2=g_ffn2, g_final=g_final.reshape(1, d),
                 g_sb_out=g_sb_out.reshape(1, D_GRP), g_dil_out=g_dil_out.reshape(1, D_GRP))
    place = jnp.stack([ci, chip]).astype(jnp.int32)
    r = _local_step(x, mod, loss_target, bufs, gains, rel_bias, place)

    dmod = r["dmod"]
    dmod_pad = jnp.zeros((8, N_MOD * d), F32).at[:nb].set(dmod)
    dmod_blk = jnp.transpose(dmod_pad.reshape(8, N_CHIPS, ada_cols), (1, 0, 2))
    small_parts = dict(b_ada=_rowsum8(dmod_pad), g_ffn1=r["dg_ffn1"], g_mix=r["dg_mix"], g_ffn2=r["dg_ffn2"],
                       g_final=r["dg_final"], g_sb_out=r["dg_sb"], g_dil_out=r["dg_dil"], rel_bias=r["drel"])
    halves1, parts_m = r["pending"]
    res = _small_sync(_pack_small(small_parts, r["loss"]), dmod_blk, c_all,
                      carry=_join([_rs_final(halves1), _rs_ici(parts_m)]))
    small_sum, g_wada, gffn1 = res[:3]
    halves_m = [_sum_chips(place, p, l) for p, l in zip(parts_m, res[3:5])]
    gwin, gwout = _alone("rs_last", _rs_final(halves_m))
    gffn2 = r["dffn2"]

    small_w = dict(b_ada=b_ada, g_ffn1=g_ffn1, g_mix=g_mix, g_ffn2=g_ffn2, g_final=g_final,
                   g_sb_out=g_sb_out, g_dil_out=g_dil_out, rel_bias=rel_bias)
    small_m = dict(b_ada=m_b_ada, g_ffn1=m_g_ffn1, g_mix=m_g_mix, g_ffn2=m_g_ffn2, g_final=m_g_final,
                   g_sb_out=m_g_sb_out, g_dil_out=m_g_dil_out, rel_bias=m_rel_bias)
    small_v = dict(b_ada=v_b_ada, g_ffn1=v_g_ffn1, g_mix=v_g_mix, g_ffn2=v_g_ffn2, g_final=v_g_final,
                   g_sb_out=v_g_sb_out, g_dil_out=v_g_dil_out, rel_bias=v_rel_bias)
    shapes = {k: v.shape for k, v in small_w.items()}
    sg, sd, sm, sv = _adamw(_pack_small(small_w), small_sum.reshape(1, SMALL_ROWS, 128), 0,
                            _pack_small(small_m), _pack_small(small_v))
    sg, loss = _unpack_small(sg, shapes)
    sd, _ = _unpack_small(sd, shapes)
    sm, _ = _unpack_small(sm, shapes)
    sv, _ = _unpack_small(sv, shapes)

    big = {}

    def upd(name, w, g_arr, sel, m, v, transposed=False):
        swap = (lambda a: jnp.swapaxes(a, -1, -2)) if transposed else (lambda a: a)
        w2, m2, v2 = [swap(a)[0] for a in (w, m, v)]
        big[name] = [swap(a[None]) for a in _adamw(w2, g_arr, sel, m2, v2)]

    upd("w_ada", w_ada, g_wada.reshape(1, d, ada_cols), 0, m_w_ada, v_w_ada)
    upd("w1_gate", w1_gate, gffn1, 0, m_w1_gate, v_w1_gate, transposed=True)
    upd("w1_up", w1_up, gffn1, 1, m_w1_up, v_w1_up, transposed=True)
    upd("w1_down", w1_down, gffn1, 2, m_w1_down, v_w1_down)
    upd("w_in", w_in, gwin, 0, m_w_in, v_w_in)
    upd("w_out", w_out, gwout, 0, m_w_out, v_w_out)
    upd("w2_gate", w2_gate, gffn2, 0, m_w2_gate, v_w2_gate, transposed=True)
    upd("w2_up", w2_up, gffn2, 1, m_w2_up, v_w2_up, transposed=True)
    upd("w2_down", w2_down, gffn2, 2, m_w2_down, v_w2_down)

    names = ["w_ada", "b_ada", "g_ffn1", "w1_gate", "w1_up", "w1_down", "g_mix", "w_in", "g_sb_out", "g_dil_out",
             "w_out", "rel_bias", "g_ffn2", "w2_gate", "w2_up", "w2_down", "g_final"]
    outs = [loss, r["grad_x"]]
    for k, small in enumerate((sg, sd, sm, sv)):
        for name in names:
            outs.append(big[name][k] if name in big else small[name])
    return tuple(outs)
```

```python
import functools
import math

import numpy as np
import jax
import jax.numpy as jnp
from jax import lax
from jax.experimental import pallas as pl
from jax.experimental.pallas import tpu as pltpu

F32 = jnp.float32
BF = jnp.bfloat16
MESH = pl.DeviceIdType.MESH

HEAD_DIM = 64
N_HEADS = 8
D_GRP = N_HEADS * HEAD_DIM
DIL_CONFIGS = ((128, 1), (512, 4), (2048, 16))
N_STEPS = 128
BLOCK = 128
N_BUCKETS = 32
MAX_DISTANCE = 2048
N_MOD = 9
EPS = 1e-6
NEG_INF = -1e30
SCALE = HEAD_DIM ** -0.5

ADAM_LR = 0.001
ADAM_B1 = 0.9
ADAM_B2 = 0.999
ADAM_EPS = 1e-08
ADAM_WD = 0.01
ADAM_STEP = 10

N_CHIPS = 4
N_DEV = 8
VMEM_LIMIT = 56 * 1024 * 1024
TM = 512
TQ = 256
KB = 256
SMALL_ROWS = 120


def _cp(n_axes=0, **kw):
    sem = ("arbitrary",) * n_axes if n_axes else None
    return pltpu.CompilerParams(dimension_semantics=sem, vmem_limit_bytes=VMEM_LIMIT, **kw)


def _nn(a, b):
    return jnp.dot(a, b, preferred_element_type=F32)


def _nt(a, b):
    return lax.dot_general(a, b, (((1,), (1,)), ((), ())), preferred_element_type=F32)


def _tn(a, b):
    return lax.dot_general(a, b, (((0,), (0,)), ((), ())), preferred_element_type=F32)


def _nn2(x, m):
    hi = x.astype(BF)
    lo = (x - hi.astype(F32)).astype(BF)
    r = _nn(jnp.concatenate([hi, lo], axis=0), m)
    return r[:x.shape[0]] + r[x.shape[0]:]


def _softplus(z):
    return jnp.maximum(z, 0.0) + jnp.log1p(jnp.exp(-jnp.abs(z)))


def _sds(shape, dtype):
    return jax.ShapeDtypeStruct(shape, dtype)


def _whole(a):
    nd = a.ndim
    return pl.BlockSpec(a.shape, lambda *_: (0,) * nd, pipeline_mode=pl.Buffered(1))


def _modnorm_bwd_tile(dh, xv, gv, scv, dxo):
    r = lax.rsqrt(jnp.mean(xv * xv, axis=-1, keepdims=True) + EPS)
    n = xv * r
    ng = n * gv
    dsh = jnp.sum(dh, axis=0, keepdims=True)
    dsc = jnp.sum(dh * ng, axis=0, keepdims=True)
    dy = dh * (1.0 + scv)
    dg = jnp.sum(dy * n, axis=0, keepdims=True)
    dn = dy * gv
    dx = dxo + r * (dn - n * jnp.mean(dn * n, axis=-1, keepdims=True))
    return dx, dsh, dsc, dg


def _acc_rows(ref, val, first):
    @pl.when(first)
    def _():
        ref[...] = val

    @pl.when(jnp.logical_not(first))
    def _():
        ref[...] += val


def _modnorm_tile(x_ref, g_ref, sc_ref, sh_ref):
    xv = x_ref[...]
    r = lax.rsqrt(jnp.mean(xv * xv, axis=-1, keepdims=True) + EPS)
    return (((xv * r) * g_ref[...]) * (1.0 + sc_ref[...]) + sh_ref[...]).astype(BF)


def _ffn_up(x, g, sc, sh, wgu, seq, carry=None):
    t, d = x.shape
    fs = wgu.shape[-1]
    per = seq // TM

    def body(x_ref, g_ref, sc_ref, sh_ref, w_ref, h_ref, p_ref, q_ref, s_ref):
        hv = _modnorm_tile(x_ref, g_ref, sc_ref, sh_ref)
        h_ref[...] = hv
        for j in range(N_CHIPS):
            a = _nn(hv, w_ref[j, 0])
            u = _nn(hv, w_ref[j, 1])
            sig = jax.nn.sigmoid(a)
            q = a * sig
            p_ref[j] = (u * (sig * (1.0 + a * (1.0 - sig)))).astype(BF)
            q_ref[j] = q.astype(BF)
            s_ref[j] = (q * u).astype(BF)

    row = pl.BlockSpec((TM, d), lambda m: (m, 0))
    ex = pl.BlockSpec((None, 1, d), lambda m: (m // per, 0, 0))
    blk = pl.BlockSpec((N_CHIPS, TM, fs), lambda m: (0, m, 0))
    return _call(
        body, "ffn_up", (t // TM,),
        [row, pl.BlockSpec((1, d), lambda m: (0, 0)), ex, ex, _whole(wgu)],
        [row, blk, blk, blk],
        [_sds((t, d), BF)] + [_sds((N_CHIPS, t, fs), BF)] * 3,
        (x, g, sc, sh, wgu), carry=carry)


def _ffn_down(s, wd, x, gt, seq, coef):
    _, t, fs = s.shape
    d = x.shape[-1]
    per = seq // TM

    def body(s_ref, w_ref, x_ref, gt_ref, f_ref, xo_ref):
        f = _nn(s_ref[0], w_ref[0, 0])
        for j in range(1, N_CHIPS):
            f = f + _nn(s_ref[j], w_ref[j, 0])
        f_ref[...] = f
        xo_ref[...] = x_ref[...] + (coef * gt_ref[...]) * f

    row = pl.BlockSpec((TM, d), lambda m: (m, 0))
    return pl.pallas_call(
        body, name="ffn_down", grid=(t // TM,),
        in_specs=[pl.BlockSpec((N_CHIPS, TM, fs), lambda m: (0, m, 0)),
                  _whole(wd), row,
                  pl.BlockSpec((None, 1, d), lambda m: (m // per, 0, 0))],
        out_specs=[row, row],
        out_shape=[_sds((t, d), F32), _sds((t, d), F32)],
        compiler_params=_cp(1))(s, wd, x, gt)


def _ffn_bwd_ds(dxo, gt, f, wd, p, q, seq, coef, carry=None):
    t, d = dxo.shape
    fs = p.shape[-1]
    per = seq // TM
    nb = t // seq

    def body(dxo_ref, gt_ref, f_ref, w_ref, p_ref, q_ref, da_ref, du_ref, df_ref, dgt_ref):
        m = pl.program_id(0)
        dxv = dxo_ref[...]
        df = ((coef * gt_ref[...]) * dxv).astype(BF)
        df_ref[...] = df
        _acc_rows(dgt_ref, coef * jnp.sum(dxv * f_ref[...], axis=0, keepdims=True), m % per == 0)
        for j in range(N_CHIPS):
            ds = _nt(df, w_ref[j, 0])
            da_ref[j] = (ds * p_ref[j].astype(F32)).astype(BF)
            du_ref[j] = (ds * q_ref[j].astype(F32)).astype(BF)

    row = pl.BlockSpec((TM, d), lambda m: (m, 0))
    blk = pl.BlockSpec((N_CHIPS, TM, fs), lambda m: (0, m, 0))
    ex = pl.BlockSpec((None, 1, d), lambda m: (m // per, 0, 0))
    return _call(
        body, "ffn_bwd_ds", (t // TM,),
        [row, ex, row, _whole(wd), blk, blk],
        [blk, blk, row, ex],
        [_sds((N_CHIPS, t, fs), BF), _sds((N_CHIPS, t, fs), BF), _sds((t, d), BF), _sds((nb, 1, d), F32)],
        (dxo, gt, f, wd, p, q), carry=carry)


TM_X = 256


def _ffn_bwd_x(dxo, gt, f, wd, p, q, wgu, x, g, sc, seq, coef):
    t, d = dxo.shape
    fs = p.shape[-1]
    per = seq // TM_X
    nb = t // seq

    def body(dxo_ref, gt_ref, f_ref, wd_ref, p_ref, q_ref, w_ref, x_ref, g_ref, sc_ref,
             da_ref, du_ref, df_ref, dgt_ref, dx_ref, dsh_ref, dsc_ref, dg_ref):
        m = pl.program_id(0)
        dxv = dxo_ref[...]
        df = ((coef * gt_ref[...]) * dxv).astype(BF)
        df_ref[...] = df
        _acc_rows(dgt_ref, coef * jnp.sum(dxv * f_ref[...], axis=0, keepdims=True), m % per == 0)
        dh = None
        for j in range(N_CHIPS):
            ds = _nt(df, wd_ref[j, 0])
            da = (ds * p_ref[j].astype(F32)).astype(BF)
            du = (ds * q_ref[j].astype(F32)).astype(BF)
            da_ref[j] = da
            du_ref[j] = du
            part = _nt(da, w_ref[j, 0]) + _nt(du, w_ref[j, 1])
            dh = part if dh is None else dh + part
        dx, dsh, dsc, dg = _modnorm_bwd_tile(dh, x_ref[...], g_ref[...], sc_ref[...], dxv)
        dx_ref[...] = dx
        _acc_rows(dsh_ref, dsh, m % per == 0)
        _acc_rows(dsc_ref, dsc, m % per == 0)
        _acc_rows(dg_ref, dg, m == 0)

    row = pl.BlockSpec((TM_X, d), lambda m: (m, 0))
    blk = pl.BlockSpec((N_CHIPS, TM_X, fs), lambda m: (0, m, 0))
    ex = pl.BlockSpec((None, 1, d), lambda m: (m // per, 0, 0))
    vec = pl.BlockSpec((1, d), lambda m: (0, 0))
    exs = _sds((nb, 1, d), F32)
    return pl.pallas_call(
        body, name="ffn_bwd_x", grid=(t // TM_X,),
        in_specs=[row, ex, row, _whole(wd), blk, blk, _whole(wgu), row, vec, ex],
        out_specs=[blk, blk, row, ex, row, ex, ex, vec],
        out_shape=[_sds((N_CHIPS, t, fs), BF), _sds((N_CHIPS, t, fs), BF), _sds((t, d), BF), exs,
                   _sds((t, d), F32), exs, exs, _sds((1, d), F32)],
        compiler_params=_cp(1))(dxo, gt, f, wd, p, q, wgu, x, g, sc)


TK_W = 1024


def _ffn_bwd_w(h, da, du, s, df):
    t, d = h.shape
    fs = da.shape[-1]

    def body(h_ref, da_ref, du_ref, s_ref, df_ref, o_ref):
        kt = pl.program_id(1)
        hv = h_ref[...]
        parts = (_tn(da_ref[...], hv), _tn(du_ref[...], hv), _tn(s_ref[...], df_ref[...]))

        @pl.when(kt == 0)
        def _():
            for i, p in enumerate(parts):
                o_ref[i] = p

        @pl.when(kt != 0)
        def _():
            for i, p in enumerate(parts):
                o_ref[i] += p

    row = pl.BlockSpec((TK_W, d), lambda j, kt: (kt, 0))
    blk = pl.BlockSpec((None, TK_W, fs), lambda j, kt: (j, kt, 0))
    return pl.pallas_call(
        body, name="ffn_bwd_w", grid=(N_CHIPS, t // TK_W),
        in_specs=[row, blk, blk, blk, row],
        out_specs=pl.BlockSpec((None, 3, fs, d), lambda j, kt: (j, 0, 0, 0)),
        out_shape=_sds((N_CHIPS, 3, fs, d), F32),
        compiler_params=_cp(2))(h, da, du, s, df)


def _ffn_bwd_dh(da, du, wgu, x, g, sc, dxo, seq, carry=None):
    _, t, fs = da.shape
    d = x.shape[-1]
    per = seq // TM
    nb = t // seq

    def body(da_ref, du_ref, w_ref, x_ref, g_ref, sc_ref, dxo_ref, dx_ref, dsh_ref, dsc_ref, dg_ref):
        m = pl.program_id(0)
        dh = _nt(da_ref[0], w_ref[0, 0]) + _nt(du_ref[0], w_ref[0, 1])
        for j in range(1, N_CHIPS):
            dh = dh + _nt(da_ref[j], w_ref[j, 0]) + _nt(du_ref[j], w_ref[j, 1])
        dx, dsh, dsc, dg = _modnorm_bwd_tile(dh, x_ref[...], g_ref[...], sc_ref[...], dxo_ref[...])
        dx_ref[...] = dx
        _acc_rows(dsh_ref, dsh, m % per == 0)
        _acc_rows(dsc_ref, dsc, m % per == 0)
        _acc_rows(dg_ref, dg, m == 0)

    row = pl.BlockSpec((TM, d), lambda m: (m, 0))
    blk = pl.BlockSpec((N_CHIPS, TM, fs), lambda m: (0, m, 0))
    ex = pl.BlockSpec((None, 1, d), lambda m: (m // per, 0, 0))
    vec = pl.BlockSpec((1, d), lambda m: (0, 0))
    return _call(
        body, "ffn_bwd_dh", (t // TM,),
        [blk, blk, _whole(wgu), row, vec, ex, row],
        [row, ex, ex, vec],
        [_sds((t, d), F32), _sds((nb, 1, d), F32), _sds((nb, 1, d), F32), _sds((1, d), F32)],
        (da, du, wgu, x, g, sc, dxo), carry=carry)


def _qkv_proj(x, g, sc, sh, w_in, seq, carry=None):
    t, d = x.shape
    wc = w_in.shape[-1]
    per = seq // TM

    dils = [dil for _, dil in DIL_CONFIGS if dil > 1]

    def body(x_ref, g_ref, sc_ref, sh_ref, w_ref, h_ref, o_ref, *rest):
        res_refs, buf = rest[:len(dils)], rest[len(dils)]
        hv = _modnorm_tile(x_ref, g_ref, sc_ref, sh_ref)
        h_ref[...] = hv
        for j in range(N_CHIPS):
            rf = _nn(hv, w_ref[j, 0])
            r = rf.astype(BF)
            for a, lc, off, width in _col_pieces(j, wc):
                o_ref[a, :, lc:lc + width] = r[:, off:off + width]
                if a < 3:
                    continue
                for c0 in range(0, width, 128):
                    cg = (lc + c0) // 128
                    buf[...] = rf[:, off + c0:off + c0 + 128]
                    for ref, dil in zip(res_refs, dils):
                        for rr in range(dil):
                            ref[a - 3, :, rr * D_GRP + cg * 128:rr * D_GRP + (cg + 1) * 128] = (
                                buf[pl.ds(rr, TM // dil, stride=dil), :].astype(BF))

    row = pl.BlockSpec((TM, d), lambda m: (m, 0))
    ex = pl.BlockSpec((None, 1, d), lambda m: (m // per, 0, 0))
    return _call(
        body, "qkv_proj", (t // TM,),
        [row, pl.BlockSpec((1, d), lambda m: (0, 0)), ex, ex, _whole(w_in)],
        [row, pl.BlockSpec((6, TM, D_GRP), lambda m: (0, m, 0))]
        + [pl.BlockSpec((3, TM // dil, dil * D_GRP), lambda m: (0, m, 0)) for dil in dils],
        [_sds((t, d), BF), _sds((6, t, D_GRP), BF)] + [_sds((3, t // dil, dil * D_GRP), BF) for dil in dils],
        (x, g, sc, sh, w_in), scratch=[pltpu.VMEM((TM, 128), F32)], carry=carry)


def _col_pieces(j, wc):
    out, off = [], 0
    while off < wc:
        a, lc = divmod(j * wc + off, D_GRP)
        width = min(D_GRP - lc, wc - off)
        out.append((a, lc, off, width))
        off += width
    return out


def _chip_cols(g6_ref, j, wc):
    return jnp.concatenate([g6_ref[a, :, lc:lc + width] for a, lc, _, width in _col_pieces(j, wc)], axis=1)


def _mix_out(on_sb, on_dil, w_out, x, gt, seq):
    t, d = x.shape
    per = seq // TM

    def body(a_ref, b_ref, w_ref, x_ref, gt_ref, t_ref, xo_ref):
        tv = _nn(a_ref[...], w_ref[0:D_GRP, :]) + _nn(b_ref[...], w_ref[D_GRP:2 * D_GRP, :])
        t_ref[...] = tv
        xo_ref[...] = x_ref[...] + gt_ref[...] * tv

    row = pl.BlockSpec((TM, d), lambda m: (m, 0))
    half = pl.BlockSpec((TM, D_GRP), lambda m: (m, 0))
    return pl.pallas_call(
        body, name="mix_out", grid=(t // TM,),
        in_specs=[half, half, pl.BlockSpec((2 * D_GRP, d), lambda m: (0, 0)), row,
                  pl.BlockSpec((None, 1, d), lambda m: (m // per, 0, 0))],
        out_specs=[row, row],
        out_shape=[_sds((t, d), F32), _sds((t, d), F32)],
        compiler_params=_cp(1))(on_sb, on_dil, w_out, x, gt)


def _sb_masks():
    lane = lax.broadcasted_iota(jnp.int32, (1, 2 * HEAD_DIM), 1)
    hm0 = lane < HEAD_DIM
    rel = lax.broadcasted_iota(jnp.int32, (TQ, KB), 0) - lax.broadcasted_iota(jnp.int32, (TQ, KB), 1)
    kr = lax.broadcasted_iota(jnp.int32, (KB, KB), 0)
    kc = lax.broadcasted_iota(jnp.int32, (KB, KB), 1)
    return hm0, rel, kr, kc


def _stack_pair(x, hm0):
    zero = jnp.zeros_like(x)
    return jnp.concatenate([jnp.where(hm0, x, zero), jnp.where(hm0, zero, x)], axis=0)


def _headnorm_pair(o, gv, hm0):
    o2 = o * o
    ms0 = jnp.sum(jnp.where(hm0, o2, 0.0), axis=-1, keepdims=True) * (1.0 / HEAD_DIM)
    ms1 = jnp.sum(jnp.where(hm0, 0.0, o2), axis=-1, keepdims=True) * (1.0 / HEAD_DIM)
    r = jnp.where(hm0, lax.rsqrt(ms0 + EPS), lax.rsqrt(ms1 + EPS))
    return (o * r) * gv


SB_DEAD = -104.0


def _alive(c_l):
    return (jnp.max(c_l) > SB_DEAD).astype(jnp.int32)


def _sb_fwd(qkv6, g_sb, nb, seq, carry=None):
    nq = seq // TQ

    def body(q_ref, k_ref, v_ref, g_ref, o_ref, on_ref):
        qi = pl.program_id(2)
        hm0, rel, kr, kc = _sb_masks()
        upper = (kr > kc).astype(BF)
        qs = _stack_pair(q_ref[...], hm0)
        causal2 = jnp.concatenate([rel, rel], axis=0) > 0

        def block(kj, causal, c_l, acc):
            ks = pl.multiple_of(kj * KB, KB)
            z = _nt(qs, k_ref[pl.ds(ks, KB), :]) * SCALE
            sp = _softplus(z)
            ln = -sp if causal is None else jnp.where(causal, -sp, 0.0)
            suf = _nn2(ln, upper)
            w = jnp.exp((z - sp) + (suf + c_l))
            if causal is not None:
                w = jnp.where(causal, w, 0.0)
            return c_l + (suf[:, 0:1] + ln[:, 0:1]), acc + _nn(w.astype(BF), v_ref[pl.ds(ks, KB), :])

        c_l, acc = block(qi, causal2, jnp.zeros((2 * TQ, 1), F32), jnp.zeros((2 * TQ, 2 * HEAD_DIM), F32))

        def cond(carry):
            return jnp.logical_and(carry[0] <= qi, carry[1] > 0)

        def kbody(carry):
            it, _, c_l, acc = carry
            c_l, acc = block(qi - it, None, c_l, acc)
            return it + 1, _alive(c_l), c_l, acc

        acc = lax.while_loop(cond, kbody, (jnp.int32(1), _alive(c_l), c_l, acc))[3]
        o = jnp.where(hm0, acc[:TQ], acc[TQ:])
        o_ref[...] = o
        on_ref[...] = _headnorm_pair(o, g_ref[...], hm0).astype(BF)

    w = 2 * HEAD_DIM
    full = lambda i: pl.BlockSpec((None, None, seq, w), lambda b, hp, q: (i, b, 0, hp))
    qblk = pl.BlockSpec((None, None, TQ, w), lambda b, hp, q: (0, b, q, hp))
    oblk = pl.BlockSpec((None, TQ, w), lambda b, hp, q: (b, q, hp))
    return _call(
        body, "sb_fwd", (nb, N_HEADS // 2, nq),
        [qblk, full(1), full(2), pl.BlockSpec((1, w), lambda b, hp, q: (0, hp))],
        [oblk, oblk],
        [_sds((nb, seq, D_GRP), F32), _sds((nb, seq, D_GRP), BF)],
        (qkv6, qkv6, qkv6, g_sb), carry=carry)


def _sb_bwd(qkv6, do, nb, seq, carry=None):
    nq = seq // TQ
    nk = seq // KB

    def body(q_ref, k_ref, v_ref, do_ref, out_ref, dk_acc, dv_acc, g_st, s_st):
        qi = pl.program_id(2)
        hm0, rel, kr, kc = _sb_masks()
        upper = (kr > kc).astype(BF)
        lower = (kr < kc).astype(BF)

        @pl.when(qi == 0)
        def _():
            dk_acc[...] = jnp.zeros_like(dk_acc)
            dv_acc[...] = jnp.zeros_like(dv_acc)

        qs = _stack_pair(q_ref[...], hm0)
        dos = _stack_pair(do_ref[...], hm0).astype(BF)
        causal2 = jnp.concatenate([rel, rel], axis=0) > 0

        def weights(kj, causal, c_l):
            ks = pl.multiple_of(kj * KB, KB)
            vb = v_ref[pl.ds(ks, KB), :]
            z = _nt(qs, k_ref[pl.ds(ks, KB), :]) * SCALE
            sp = _softplus(z)
            ln = -sp if causal is None else jnp.where(causal, -sp, 0.0)
            suf = _nn2(ln, upper)
            lsz = z - sp
            w = jnp.exp(lsz + (suf + c_l))
            if causal is not None:
                w = jnp.where(causal, w, 0.0)
            g_st[kj] = w * _nt(dos, vb)
            s_st[kj] = jnp.exp(lsz)
            dv_acc[pl.ds(ks, KB), :] += _tn(w.astype(BF), dos)
            return c_l + (suf[:, 0:1] + ln[:, 0:1])

        zc = jnp.zeros((2 * TQ, 1), F32)
        c_l = weights(qi, causal2, zc)

        def acond(carry):
            return jnp.logical_and(carry[0] <= qi, carry[1] > 0)

        def abody(carry):
            c_l = weights(qi - carry[0], None, carry[2])
            return carry[0] + 1, _alive(c_l), c_l

        n_used = lax.while_loop(acond, abody, (jnp.int32(1), _alive(c_l), c_l))[0]

        def grads(kj, causal, c_g, dq):
            ks = pl.multiple_of(kj * KB, KB)
            kb = k_ref[pl.ds(ks, KB), :]
            g = g_st[kj]
            sig = s_st[kj]
            pre = _nn(g.astype(BF), lower)
            dz = g * (1.0 - sig) - sig * (pre + c_g)
            if causal is not None:
                dz = jnp.where(causal, dz, 0.0)
            dzb = (dz * SCALE).astype(BF)
            dk_acc[pl.ds(ks, KB), :] += _tn(dzb, qs)
            return c_g + (pre[:, KB - 1:KB] + g[:, KB - 1:KB]), dq + _nn(dzb, kb)

        c_g, dq = lax.fori_loop(qi - n_used + 1, qi, lambda kj, cr: grads(kj, None, *cr),
                                (zc, jnp.zeros((2 * TQ, 2 * HEAD_DIM), F32)))
        _, dq = grads(qi, causal2, c_g, dq)
        dq = jnp.where(hm0, dq[:TQ], dq[TQ:])
        out_ref[0, pl.ds(pl.multiple_of(qi * TQ, TQ), TQ), :] = dq.astype(BF)

        @pl.when(qi == nq - 1)
        def _():
            out_ref[1] = dk_acc[...].astype(BF)
            out_ref[2] = dv_acc[...].astype(BF)

    w = 2 * HEAD_DIM
    full = lambda i: pl.BlockSpec((None, None, seq, w), lambda b, hp, q: (i, b, 0, hp))
    qblk = pl.BlockSpec((None, None, TQ, w), lambda b, hp, q: (0, b, q, hp))
    oblk = pl.BlockSpec((None, TQ, w), lambda b, hp, q: (b, q, hp))
    return _call(
        body, "sb_bwd", (nb, N_HEADS // 2, nq),
        [qblk, full(1), full(2), oblk],
        [pl.BlockSpec((3, None, seq, w), lambda b, hp, q: (0, b, 0, hp))],
        [_sds((6, nb, seq, D_GRP), BF)], (qkv6, qkv6, qkv6, do),
        scratch=[pltpu.VMEM((seq, w), F32), pltpu.VMEM((seq, w), F32),
                 pltpu.VMEM((nk, 2 * TQ, KB), F32), pltpu.VMEM((nk, 2 * TQ, KB), F32)],
        carry=carry)


def _t5_bucket(n):
    max_exact = N_BUCKETS // 2
    nf = np.maximum(n, 1).astype(np.float32)
    large = max_exact + (np.log(nf / max_exact) / math.log(MAX_DISTANCE / max_exact)
                         * (N_BUCKETS - max_exact)).astype(np.int32)
    large = np.minimum(large, N_BUCKETS - 1)
    return np.where(n < max_exact, n, large).astype(np.int32)


def _bucket_map(dilation):
    step = BLOCK + np.arange(BLOCK)[:, None] - np.arange(2 * BLOCK)[None, :]
    return _t5_bucket(np.clip(step, 0, N_STEPS) * dilation)


GRP_HEADS = 4
GRP_W = GRP_HEADS * HEAD_DIM


def _dil_masks():
    lane = lax.broadcasted_iota(jnp.int32, (1, GRP_W), 1)
    heads = [jnp.logical_and(lane >= HEAD_DIM * i, lane < HEAD_DIM * (i + 1)) for i in range(GRP_HEADS)]
    iq = lax.broadcasted_iota(jnp.int32, (BLOCK, BLOCK), 0)
    ik = lax.broadcasted_iota(jnp.int32, (BLOCK, BLOCK), 1)
    return heads, ik <= iq, ik >= iq


def _dil_rows(n):
    rs = pl.multiple_of(n * BLOCK, BLOCK)
    ps = pl.multiple_of(jnp.maximum(n - 1, 0) * BLOCK, BLOCK)
    return pl.ds(rs, BLOCK), pl.ds(ps, BLOCK)


def _dil_probs(qh, kc, kp, b_ref, hh, valid_c, valid_p):
    zc = _nt(qh, kc) * SCALE + b_ref[hh, :, BLOCK:2 * BLOCK]
    zp = _nt(qh, kp) * SCALE + b_ref[hh, :, 0:BLOCK]
    zc = jnp.where(valid_c, zc, NEG_INF)
    zp = jnp.where(valid_p, zp, NEG_INF)
    m = jnp.maximum(jnp.max(zc, axis=-1, keepdims=True), jnp.max(zp, axis=-1, keepdims=True))
    ec = jnp.exp(zc - m)
    ep = jnp.exp(zp - m)
    den = jnp.sum(ec, axis=-1, keepdims=True) + jnp.sum(ep, axis=-1, keepdims=True)
    return ec, ep, den, m


def _dil_fwd(qkv6r, base, bias, nb, sub_len, dilation):
    n_blk = sub_len // BLOCK

    def body(q_ref, k_ref, v_ref, b_ref, o_ref, l_ref):
        heads, valid_c, valid_p0 = _dil_masks()

        def nbody(n, carry):
            cur, prev = _dil_rows(n)
            valid_p = jnp.logical_and(valid_p0, n > 0)
            for gi in range(N_HEADS // GRP_HEADS):
                lanes = slice(gi * GRP_W, (gi + 1) * GRP_W)
                qv, kc, kp = q_ref[cur, lanes], k_ref[cur, lanes], k_ref[prev, lanes]
                vc, vp = v_ref[cur, lanes], v_ref[prev, lanes]
                o = jnp.zeros((BLOCK, GRP_W), F32)
                lse = jnp.zeros((BLOCK, GRP_W), F32)
                for i, hm in enumerate(heads):
                    qh = jnp.where(hm, qv, jnp.zeros_like(qv))
                    ec, ep, den, m = _dil_probs(qh, kc, kp, b_ref, gi * GRP_HEADS + i, valid_c, valid_p)
                    o = jnp.where(hm, (_nn(ec.astype(BF), vc) + _nn(ep.astype(BF), vp)) / den, o)
                    lse = jnp.where(hm, m + jnp.log(den), lse)
                o_ref[cur, lanes] = o
                l_ref[cur, lanes] = lse
            return carry

        lax.fori_loop(0, n_blk, nbody, 0)

    seqblk = lambda i: pl.BlockSpec((None, None, sub_len, D_GRP), lambda b, r: (i, b, 0, r))
    oblk = pl.BlockSpec((None, sub_len, D_GRP), lambda b, r: (b, 0, r))
    shp = _sds((nb, sub_len, dilation * D_GRP), F32)
    return pl.pallas_call(
        body, name="dil_fwd_%d" % dilation, grid=(nb, dilation),
        in_specs=[seqblk(base), seqblk(base + 1), seqblk(base + 2), _whole(bias)],
        out_specs=[oblk, oblk], out_shape=[shp, shp],
        compiler_params=_cp(2))(qkv6r, qkv6r, qkv6r, bias)


def _dil_bwd(qkv6r, base, bias, do_c, dd_c, nb, sub_len, dilation, carry=None):
    n_blk = sub_len // BLOCK

    def body(q_ref, k_ref, v_ref, b_ref, do_ref, dd_ref, out_ref, a_ref):
        heads, valid_c, valid_p0 = _dil_masks()
        first = jnp.logical_and(pl.program_id(0) == 0, pl.program_id(1) == 0)

        @pl.when(first)
        def _():
            a_ref[...] = jnp.zeros_like(a_ref)

        out_ref[1] = jnp.zeros((sub_len, D_GRP), F32)
        out_ref[2] = jnp.zeros((sub_len, D_GRP), F32)

        def nbody(n, carry):
            cur, prev = _dil_rows(n)
            valid_p = jnp.logical_and(valid_p0, n > 0)
            for gi in range(N_HEADS // GRP_HEADS):
                lanes = slice(gi * GRP_W, (gi + 1) * GRP_W)
                qv, kc, kp = q_ref[cur, lanes], k_ref[cur, lanes], k_ref[prev, lanes]
                vc, vp = v_ref[cur, lanes], v_ref[prev, lanes]
                dov, ddv = do_ref[cur, lanes], dd_ref[cur, lanes]
                dq = jnp.zeros((BLOCK, GRP_W), F32)
                dkc = jnp.zeros((BLOCK, GRP_W), F32)
                dkp = jnp.zeros((BLOCK, GRP_W), F32)
                dvc = jnp.zeros((BLOCK, GRP_W), F32)
                dvp = jnp.zeros((BLOCK, GRP_W), F32)
                for i, hm in enumerate(heads):
                    h = gi * GRP_HEADS + i
                    qh = jnp.where(hm, qv, jnp.zeros_like(qv))
                    doh = jnp.where(hm, dov, 0.0).astype(BF)
                    ddh = jnp.sum(jnp.where(hm, ddv, 0.0), axis=-1, keepdims=True) * (1.0 / HEAD_DIM)
                    ec, ep, den, _ = _dil_probs(qh, kc, kp, b_ref, h, valid_c, valid_p)
                    inv = 1.0 / den
                    pc = ec * inv
                    pp = ep * inv
                    dzc = pc * (_nt(doh, vc) + ddh)
                    dzp = pp * (_nt(doh, vp) + ddh)
                    a_ref[h, :, BLOCK:2 * BLOCK] += dzc
                    a_ref[h, :, 0:BLOCK] += dzp
                    dzcb = (dzc * SCALE).astype(BF)
                    dzpb = (dzp * SCALE).astype(BF)
                    dq = jnp.where(hm, _nn(dzcb, kc) + _nn(dzpb, kp), dq)
                    dkc = dkc + _tn(dzcb, qh)
                    dkp = dkp + _tn(dzpb, qh)
                    dvc = dvc + _tn(pc.astype(BF), doh)
                    dvp = dvp + _tn(pp.astype(BF), doh)
                out_ref[0, cur, lanes] = dq
                out_ref[1, cur, lanes] += dkc
                out_ref[1, prev, lanes] += dkp
                out_ref[2, cur, lanes] += dvc
                out_ref[2, prev, lanes] += dvp
            return carry

        lax.fori_loop(0, n_blk, nbody, 0)

    seqblk = lambda i: pl.BlockSpec((None, None, sub_len, D_GRP), lambda b, r: (i, b, 0, r))
    oblk = pl.BlockSpec((None, sub_len, D_GRP), lambda b, r: (b, 0, r))
    return _call(
        body, "dil_bwd_%d" % dilation, (nb, dilation),
        [seqblk(base), seqblk(base + 1), seqblk(base + 2), _whole(bias), oblk, oblk],
        [pl.BlockSpec((3, None, sub_len, D_GRP), lambda b, r: (0, b, 0, r)),
         pl.BlockSpec((N_HEADS, BLOCK, 2 * BLOCK), lambda b, r: (0, 0, 0))],
        [_sds((3, nb, sub_len, dilation * D_GRP), F32), _sds((N_HEADS, BLOCK, 2 * BLOCK), F32)],
        (qkv6r, qkv6r, qkv6r, bias, do_c, dd_c), carry=carry)


def _group_ones():
    idx = np.arange(D_GRP) // HEAD_DIM
    return jnp.asarray((idx[:, None] == idx[None, :]).astype(np.float32), dtype=BF)


def _dil_alphas(l1, l4, l16):
    mx = jnp.maximum(jnp.maximum(l1, l4), l16)
    e1 = jnp.exp(l1 - mx)
    e4 = jnp.exp(l4 - mx)
    e16 = jnp.exp(l16 - mx)
    den = e1 + e4 + e16
    return e1 / den, e4 / den, e16 / den


def _residue_spec(dil):
    return pl.BlockSpec((TM // dil, dil * D_GRP), lambda m: (m, 0))


def _from_residue(src, dil, cg, buf):
    if dil == 1:
        return src[:, cg * 128:(cg + 1) * 128]
    for r in range(dil):
        buf[pl.ds(r, TM // dil, stride=dil), :] = src[:, r * D_GRP + cg * 128:r * D_GRP + (cg + 1) * 128]
    return buf[...]


def _to_residue(dst, dil, cg, buf, val):
    if dil == 1:
        dst[:, cg * 128:(cg + 1) * 128] = val
        return
    buf[...] = val
    for r in range(dil):
        dst[:, r * D_GRP + cg * 128:r * D_GRP + (cg + 1) * 128] = buf[pl.ds(r, TM // dil, stride=dil), :]


def _pair_sum(x, hm0):
    s0 = jnp.sum(jnp.where(hm0, x, 0.0), axis=-1, keepdims=True)
    s1 = jnp.sum(jnp.where(hm0, 0.0, x), axis=-1, keepdims=True)
    return jnp.where(hm0, s0, s1)


def _dil_comb(os, ls, g_dil):
    t = os[0].shape[0]
    dils = [dil for _, dil in DIL_CONFIGS]

    def body(o1, l1, o4, l4, o16, l16, g_ref, o_ref, on_ref, b0, b1, b2, b3):
        hm0 = lax.broadcasted_iota(jnp.int32, (1, 128), 1) < HEAD_DIM
        for cg in range(D_GRP // 128):
            lanes = slice(cg * 128, (cg + 1) * 128)
            ov = [_from_residue(src, dil, cg, buf) for src, dil, buf in zip((o1, o4, o16), dils, (None, b0, b1))]
            lv = [_from_residue(src, dil, cg, buf) for src, dil, buf in zip((l1, l4, l16), dils, (None, b2, b3))]
            a1, a4, a16 = _dil_alphas(*lv)
            o = a1 * ov[0] + a4 * ov[1] + a16 * ov[2]
            o_ref[:, lanes] = o
            on_ref[:, lanes] = _headnorm_pair(o, g_ref[:, lanes], hm0).astype(BF)

    blk = pl.BlockSpec((TM, D_GRP), lambda m: (m, 0))
    specs = [_residue_spec(dil) for dil in dils for _ in range(2)]
    return pl.pallas_call(
        body, name="dil_comb", grid=(t // TM,),
        in_specs=specs + [pl.BlockSpec((1, D_GRP), lambda m: (0, 0))],
        out_specs=[blk, blk],
        out_shape=[_sds((t, D_GRP), F32), _sds((t, D_GRP), BF)],
        scratch_shapes=[pltpu.VMEM((TM, 128), F32)] * 4,
        compiler_params=_cp(1))(os[0], ls[0], os[1], ls[1], os[2], ls[2], g_dil)


def _dil_comb_bwd(do, os, ls):
    t = do.shape[0]
    dils = [dil for _, dil in DIL_CONFIGS]

    def body(do_ref, o1, l1, o4, l4, o16, l16, d1, d4, d16, e1, e4, e16, b0, b1, b2, b3):
        hm0 = lax.broadcasted_iota(jnp.int32, (1, 128), 1) < HEAD_DIM
        for cg in range(D_GRP // 128):
            dov = do_ref[:, cg * 128:(cg + 1) * 128]
            ov = [_from_residue(src, dil, cg, buf) for src, dil, buf in zip((o1, o4, o16), dils, (None, b0, b1))]
            lv = [_from_residue(src, dil, cg, buf) for src, dil, buf in zip((l1, l4, l16), dils, (None, b2, b3))]
            al = _dil_alphas(*lv)
            sbar = al[0] * _pair_sum(dov * ov[0], hm0)
            for a_c, o_c in zip(al[1:], ov[1:]):
                sbar = sbar + a_c * _pair_sum(dov * o_c, hm0)
            for a_c, dil, dref, eref in zip(al, dils, (d1, d4, d16), (e1, e4, e16)):
                _to_residue(dref, dil, cg, b0, a_c * dov)
                _to_residue(eref, dil, cg, b1, -a_c * sbar)

    specs = [_residue_spec(dil) for dil in dils]
    return pl.pallas_call(
        body, name="dil_comb_bwd", grid=(t // TM,),
        in_specs=[pl.BlockSpec((TM, D_GRP), lambda m: (m, 0))] + [sp for sp in specs for _ in range(2)],
        out_specs=specs + specs,
        out_shape=[_sds((t // dil, dil * D_GRP), F32) for dil in dils] * 2,
        scratch_shapes=[pltpu.VMEM((TM, 128), F32)] * 4,
        compiler_params=_cp(1))(do, os[0], ls[0], os[1], ls[1], os[2], ls[2])


def _dqkv_dil_sum(ds, dqkv6):
    t = dqkv6.shape[1]
    dils = [dil for _, dil in DIL_CONFIGS]

    def body(*refs):
        srcs, o_ref, acc = refs[:len(dils)], refs[len(dils) + 1], refs[len(dils) + 2]
        for a in range(3):
            for cg in range(D_GRP // 128):
                for src, dil in zip(srcs, dils):
                    for r in range(dil):
                        part = src[a, :, r * D_GRP + cg * 128:r * D_GRP + (cg + 1) * 128]
                        rows = pl.ds(r, TM // dil, stride=dil) if dil > 1 else slice(None)
                        if dil == dils[0]:
                            acc[rows, :] = part
                        else:
                            acc[rows, :] += part
                o_ref[a, :, cg * 128:(cg + 1) * 128] = acc[...].astype(BF)

    return pl.pallas_call(
        body, name="dqkv_dil_sum", grid=(t // TM,),
        in_specs=[pl.BlockSpec((3, TM // dil, dil * D_GRP), lambda m: (0, m, 0)) for dil in dils]
        + [pl.BlockSpec(memory_space=pl.ANY)],
        out_specs=pl.BlockSpec((3, TM, D_GRP), lambda m: (1, m, 0)),
        out_shape=_sds((6, t, D_GRP), BF), input_output_aliases={len(dils): 0},
        scratch_shapes=[pltpu.VMEM((TM, 128), F32)],
        compiler_params=_cp(1))(*ds, dqkv6)


def _relbias_grad(a_all, onehot):
    def body(a_ref, oh_ref, o_ref):
        acc = jnp.zeros((N_HEADS, N_BUCKETS), F32)
        for c in range(len(DIL_CONFIGS)):
            av = a_ref[c]
            hi = av.astype(BF)
            lo = (av - hi.astype(F32)).astype(BF)
            acc = acc + _nt(hi, oh_ref[c]) + _nt(lo, oh_ref[c])
        o_ref[...] = acc

    return pl.pallas_call(body, name="relbias_grad", out_shape=_sds((N_HEADS, N_BUCKETS), F32),
                          compiler_params=_cp())(a_all, onehot)


def _headnorm_bwd(dn, o, gv, mv):
    ms = _nn2(o * o, mv) * (1.0 / HEAD_DIM)
    r = lax.rsqrt(ms + EPS)
    nrm = o * r
    dg = jnp.sum(dn * nrm, axis=0, keepdims=True)
    dnn = dn * gv
    do = r * (dnn - nrm * (_nn2(dnn * nrm, mv) * (1.0 / HEAD_DIM)))
    return do, dg


def _mix_bwd_out(dx, gt, tv, w_out, o_sb, o_dil, on_sb, on_dil, g_sb, g_dil, ones_g, seq, carry=None):
    t, d = dx.shape
    per = seq // TM
    nb = t // seq

    def body(dx_ref, gt_ref, t_ref, w_ref, osb, odl, onsb, ondl, gsb, gdl, m_ref,
             dosb, dodl, dgt_ref, dgsb, dgdl, dw_ref):
        m = pl.program_id(0)
        dxv = dx_ref[...]
        dt = (gt_ref[...] * dxv).astype(BF)
        _acc_rows(dgt_ref, jnp.sum(dxv * t_ref[...], axis=0, keepdims=True), m % per == 0)
        mv = m_ref[...]
        don_sb = _nt(dt, w_ref[0:D_GRP, :])
        don_dl = _nt(dt, w_ref[D_GRP:2 * D_GRP, :])
        do1, dg1 = _headnorm_bwd(don_sb, osb[...], gsb[...], mv)
        do2, dg2 = _headnorm_bwd(don_dl, odl[...], gdl[...], mv)
        dosb[...] = do1
        dodl[...] = do2
        _acc_rows(dgsb, dg1, m == 0)
        _acc_rows(dgdl, dg2, m == 0)
        p1 = _tn(onsb[...], dt)
        p2 = _tn(ondl[...], dt)

        @pl.when(m == 0)
        def _():
            dw_ref[0:D_GRP, :] = p1
            dw_ref[D_GRP:2 * D_GRP, :] = p2

        @pl.when(m != 0)
        def _():
            dw_ref[0:D_GRP, :] += p1
            dw_ref[D_GRP:2 * D_GRP, :] += p2

    row = pl.BlockSpec((TM, d), lambda m: (m, 0))
    half = pl.BlockSpec((TM, D_GRP), lambda m: (m, 0))
    ex = pl.BlockSpec((None, 1, d), lambda m: (m // per, 0, 0))
    gvec = pl.BlockSpec((1, D_GRP), lambda m: (0, 0))
    wblk = pl.BlockSpec((2 * D_GRP, d), lambda m: (0, 0))
    return _call(
        body, "mix_bwd_out", (t // TM,),
        [row, ex, row, wblk, half, half, half, half, gvec, gvec, pl.BlockSpec((D_GRP, D_GRP), lambda m: (0, 0))],
        [half, half, ex, gvec, gvec, wblk],
        [_sds((t, D_GRP), F32), _sds((t, D_GRP), F32), _sds((nb, 1, d), F32),
         _sds((1, D_GRP), F32), _sds((1, D_GRP), F32), _sds((2 * D_GRP, d), F32)],
        (dx, gt, tv, w_out, o_sb, o_dil, on_sb, on_dil, g_sb, g_dil, ones_g), carry=carry)


def _dw_in(h, dqkv6, carry=None):
    t, d = h.shape
    wc = 6 * D_GRP // N_CHIPS

    def body(h_ref, g_ref, o_ref):
        kt = pl.program_id(0)
        hv = h_ref[...]
        for j in range(N_CHIPS):
            p = _tn(hv, _chip_cols(g_ref, j, wc))

            @pl.when(kt == 0)
            def _(p=p, j=j):
                o_ref[j, 0] = p

            @pl.when(kt != 0)
            def _(p=p, j=j):
                o_ref[j, 0] += p

    return _call(
        body, "dw_in", (t // TM,),
        [pl.BlockSpec((TM, d), lambda kt: (kt, 0)), pl.BlockSpec((6, TM, D_GRP), lambda kt: (0, kt, 0))],
        [pl.BlockSpec((N_CHIPS, 1, d, wc), lambda kt: (0, 0, 0, 0))],
        [_sds((N_CHIPS, 1, d, wc), F32)], (h, dqkv6), carry=carry)


def _mix_bwd_dh(dqkv6, w_in, x, g, sc, dxo, seq, carry=None):
    _, t, _ = dqkv6.shape
    d = x.shape[-1]
    wc = w_in.shape[-1]
    per = seq // TM
    nb = t // seq

    def body(g6_ref, w_ref, x_ref, g_ref, sc_ref, dxo_ref, dx_ref, dsh_ref, dsc_ref, dg_ref):
        m = pl.program_id(0)
        dh = _nt(_chip_cols(g6_ref, 0, wc), w_ref[0, 0])
        for j in range(1, N_CHIPS):
            dh = dh + _nt(_chip_cols(g6_ref, j, wc), w_ref[j, 0])
        dx, dsh, dsc, dg = _modnorm_bwd_tile(dh, x_ref[...], g_ref[...], sc_ref[...], dxo_ref[...])
        dx_ref[...] = dx
        _acc_rows(dsh_ref, dsh, m % per == 0)
        _acc_rows(dsc_ref, dsc, m % per == 0)
        _acc_rows(dg_ref, dg, m == 0)

    row = pl.BlockSpec((TM, d), lambda m: (m, 0))
    ex = pl.BlockSpec((None, 1, d), lambda m: (m // per, 0, 0))
    vec = pl.BlockSpec((1, d), lambda m: (0, 0))
    return _call(
        body, "mix_bwd_dh", (t // TM,),
        [pl.BlockSpec((6, TM, D_GRP), lambda m: (0, m, 0)), _whole(w_in), row, vec, ex, row],
        [row, ex, ex, vec],
        [_sds((t, d), F32), _sds((nb, 1, d), F32), _sds((nb, 1, d), F32), _sds((1, d), F32)],
        (dqkv6, w_in, x, g, sc, dxo), carry=carry)


def _ffn_down_loss(s, wd, x, gt, seq, coef, g, target):
    _, t, fs = s.shape
    d = x.shape[-1]
    per = seq // TM
    steps = t // TM

    def body(s_ref, w_ref, x_ref, gt_ref, g_ref, t_ref, f_ref, dx_ref, dg_ref, loss_ref, lacc):
        m = pl.program_id(0)
        f = _nn(s_ref[0], w_ref[0, 0])
        for j in range(1, N_CHIPS):
            f = f + _nn(s_ref[j], w_ref[j, 0])
        f_ref[...] = f
        xv = x_ref[...] + (coef * gt_ref[...]) * f
        gv = g_ref[...]
        r = lax.rsqrt(jnp.mean(xv * xv, axis=-1, keepdims=True) + EPS)
        n = xv * r
        err = n * gv - t_ref[...]
        dy = err * (1.0 / d)
        _acc_rows(dg_ref, jnp.sum(dy * n, axis=0, keepdims=True), m == 0)
        dn = dy * gv
        dx_ref[...] = r * (dn - n * jnp.mean(dn * n, axis=-1, keepdims=True))
        _acc_rows(lacc, jnp.sum(err * err, axis=0, keepdims=True), m == 0)

        @pl.when(m == steps - 1)
        def _():
            tot = jnp.sum(lacc[...], axis=-1, keepdims=True) * (0.5 / d)
            loss_ref[...] = jnp.broadcast_to(tot, (1, 128))

    row = pl.BlockSpec((TM, d), lambda m: (m, 0))
    vec = pl.BlockSpec((1, d), lambda m: (0, 0))
    return pl.pallas_call(
        body, name="ffn_down_loss", grid=(steps,),
        in_specs=[pl.BlockSpec((N_CHIPS, TM, fs), lambda m: (0, m, 0)), _whole(wd), row,
                  pl.BlockSpec((None, 1, d), lambda m: (m // per, 0, 0)), vec, row],
        out_specs=[row, row, vec, pl.BlockSpec((1, 128), lambda m: (0, 0))],
        out_shape=[_sds((t, d), F32), _sds((t, d), F32), _sds((1, d), F32), _sds((1, 128), F32)],
        scratch_shapes=[pltpu.VMEM((1, d), F32)],
        compiler_params=_cp(1))(s, wd, x, gt, g, target)


def _row_tile(rows, cols):
    best = rows
    for tr in range(8, rows + 1, 8):
        if rows % tr == 0 and tr * cols * 4 <= (1 << 20):
            best = tr
    if best * cols * 4 > (1 << 21):
        best = 8
    return best


def _adamw(w, g_arr, g_sel, m, v):
    rows, cols = w.shape
    tr = _row_tile(rows, cols)
    b1c = 1.0 - ADAM_B1 ** ADAM_STEP
    b2c = 1.0 - ADAM_B2 ** ADAM_STEP

    def body(w_ref, g_ref, m_ref, v_ref, go_ref, d_ref, mo_ref, vo_ref):
        gv = g_ref[...]
        mn = ADAM_B1 * m_ref[...] + (1.0 - ADAM_B1) * gv
        vn = ADAM_B2 * v_ref[...] + (1.0 - ADAM_B2) * (gv * gv)
        go_ref[...] = gv
        mo_ref[...] = mn
        vo_ref[...] = vn
        d_ref[...] = -ADAM_LR * ((mn / b1c) / (jnp.sqrt(vn / b2c) + ADAM_EPS) + ADAM_WD * w_ref[...])

    blk = pl.BlockSpec((tr, cols), lambda i: (i, 0))
    shp = _sds((rows, cols), F32)
    return pl.pallas_call(
        body, name="adamw", grid=(rows // tr,),
        in_specs=[blk, pl.BlockSpec((None, tr, cols), lambda i: (g_sel, i, 0)), blk, blk],
        out_specs=[blk] * 4, out_shape=[shp] * 4,
        compiler_params=_cp(1))(w, g_arr, m, v)


def _flip(v, bit):
    return 1 - v if bit else v


def _my_place():
    x, y, c = lax.axis_index("x"), lax.axis_index("y"), lax.axis_index("c")
    return x, y, c


class _Exchange:
    def __init__(self, operands, out_shape, aliases, sems, start, finish):
        self.operands, self.out_shape, self.aliases, self.sems = list(operands), list(out_shape), dict(aliases), list(sems)
        self.start, self.finish = start, finish


def _join(exchanges):
    exchanges = [e for e in exchanges if e is not None]
    if not exchanges:
        return None
    ops, outs, sems, aliases, spans = [], [], [], {}, []
    for e in exchanges:
        spans.append((len(ops), len(outs), len(sems), e))
        for i, j in e.aliases.items():
            aliases[len(ops) + i] = len(outs) + j
        ops += e.operands
        outs += e.out_shape
        sems += e.sems

    def run(which):
        def go(ins, res, sm):
            for io, oo, so, e in spans:
                getattr(e, which)(ins[io:io + len(e.operands)], res[oo:oo + len(e.out_shape)], sm[so:so + len(e.sems)])
        return go

    return _Exchange(ops, outs, aliases, sems, run("start"), run("finish"))


def _call(body, name, grid, in_specs, out_specs, out_shape, args, scratch=(), carry=None):
    in_specs, out_specs, out_shape, scratch = list(in_specs), list(out_specs), list(out_shape), list(scratch)
    if carry is None:
        return pl.pallas_call(body, name=name, grid=grid, in_specs=in_specs, out_specs=out_specs,
                              out_shape=out_shape, scratch_shapes=scratch,
                              compiler_params=_cp(len(grid)))(*args)
    n_in, n_out, n_s = len(in_specs), len(out_specs), len(scratch)
    c_in, c_out = len(carry.operands), len(carry.out_shape)
    any_spec = pl.BlockSpec(memory_space=pl.ANY)

    def wrapped(*refs):
        ins, cins = refs[:n_in], refs[n_in:n_in + c_in]
        o0 = n_in + c_in
        outs, couts = refs[o0:o0 + n_out], refs[o0 + n_out:o0 + n_out + c_out]
        s0 = o0 + n_out + c_out
        scr, sems = refs[s0:s0 + n_s], refs[s0 + n_s:]
        first = pl.program_id(0) == 0
        last = pl.program_id(0) == grid[0] - 1
        for ax in range(1, len(grid)):
            first = jnp.logical_and(first, pl.program_id(ax) == 0)
            last = jnp.logical_and(last, pl.program_id(ax) == grid[ax] - 1)

        @pl.when(first)
        def _():
            carry.start(cins, couts, sems)

        body(*ins, *outs, *scr)

        @pl.when(last)
        def _():
            carry.finish(cins, couts, sems)

    return pl.pallas_call(
        wrapped, name=name, grid=grid, in_specs=in_specs + [any_spec] * c_in,
        out_specs=out_specs + [any_spec] * c_out, out_shape=out_shape + carry.out_shape,
        scratch_shapes=scratch + carry.sems,
        input_output_aliases={n_in + i: n_out + j for i, j in carry.aliases.items()},
        compiler_params=_cp(len(grid)))(*args, *carry.operands)


def _whole_call(body, name, args, out_shape, scratch, carry=None):
    vm = pl.BlockSpec(memory_space=pltpu.VMEM)
    any_spec = pl.BlockSpec(memory_space=pl.ANY)
    out_shape, scratch = list(out_shape), list(scratch)
    n_in, n_out, n_s = len(args), len(out_shape), len(scratch)
    if carry is None:
        return pl.pallas_call(body, name=name, in_specs=[vm] * n_in, out_specs=[vm] * n_out, out_shape=out_shape,
                              scratch_shapes=scratch, compiler_params=_cp())(*args)
    c_in, c_out = len(carry.operands), len(carry.out_shape)

    def wrapped(*refs):
        ins, cins = refs[:n_in], refs[n_in:n_in + c_in]
        o0 = n_in + c_in
        outs, couts = refs[o0:o0 + n_out], refs[o0 + n_out:o0 + n_out + c_out]
        s0 = o0 + n_out + c_out
        scr, sems = refs[s0:s0 + n_s], refs[s0 + n_s:]
        carry.start(cins, couts, sems)
        body(*ins, *outs, *scr)
        carry.finish(cins, couts, sems)

    return pl.pallas_call(
        wrapped, name=name, in_specs=[vm] * n_in + [any_spec] * c_in, out_specs=[vm] * n_out + [any_spec] * c_out,
        out_shape=out_shape + carry.out_shape, scratch_shapes=scratch + carry.sems,
        input_output_aliases={n_in + i: n_out + j for i, j in carry.aliases.items()},
        compiler_params=_cp())(*args, *carry.operands)


def _alone(name, ex):
    any_spec = pl.BlockSpec(memory_space=pl.ANY)
    c_in, c_out = len(ex.operands), len(ex.out_shape)

    def body(*refs):
        ins, outs, sems = refs[:c_in], refs[c_in:c_in + c_out], refs[c_in + c_out:]
        ex.start(ins, outs, sems)
        ex.finish(ins, outs, sems)

    return pl.pallas_call(
        body, name=name, in_specs=[any_spec] * c_in, out_specs=[any_spec] * c_out, out_shape=ex.out_shape,
        scratch_shapes=ex.sems, input_output_aliases=ex.aliases, compiler_params=_cp())(*ex.operands)


def _ada_fwd(c_pad, w_ada, b_shard, carry=None):
    d = c_pad.shape[-1]
    cols = w_ada.shape[-1]
    chunk = 384

    def body(c_ref, w_ref, b_ref, call_ref, mod_ref, part, s1, r1, s2, r2):
        x, y, c = _my_place()
        dev = 4 * x + 2 * y + c
        chip = 2 * x + y
        call_ref[dev] = c_ref[...]

        def c_copy(k):
            px, py, pc = _flip(x, (k >> 2) & 1), _flip(y, (k >> 1) & 1), _flip(c, k & 1)
            return px, py, pc

        sends = []
        for k in range(1, N_DEV):
            px, py, pc = c_copy(k)
            cp = pltpu.make_async_remote_copy(src_ref=c_ref, dst_ref=call_ref.at[dev], send_sem=s1.at[k - 1],
                                              recv_sem=r1.at[k - 1], device_id=(px, py, pc), device_id_type=MESH)
            cp.start()
            sends.append(cp)
        for k in range(1, N_DEV):
            px, py, pc = c_copy(k)
            pltpu.make_async_remote_copy(src_ref=c_ref, dst_ref=call_ref.at[4 * px + 2 * py + pc],
                                         send_sem=s1.at[k - 1], recv_sem=r1.at[k - 1],
                                         device_id=(px, py, pc), device_id_type=MESH).wait_recv()
        for cp in sends:
            cp.wait_send()

        cs = call_ref[...].reshape(N_DEV * 8, d)
        sc = (cs * jax.nn.sigmoid(cs)).astype(BF)
        for n0 in range(0, cols, chunk):
            blk = _nn(sc, w_ref[:, n0:n0 + chunk].astype(BF)) + b_ref[:, n0:n0 + chunk]
            part[:, :, n0:n0 + chunk] = blk.reshape(N_DEV, 8, chunk)

        mod_ref[chip] = part[dev]
        sends = []
        for kk in range(1, N_CHIPS):
            px, py = _flip(x, (kk >> 1) & 1), _flip(y, kk & 1)
            cp = pltpu.make_async_remote_copy(src_ref=part.at[4 * px + 2 * py + c], dst_ref=mod_ref.at[chip],
                                              send_sem=s2.at[kk - 1], recv_sem=r2.at[kk - 1],
                                              device_id=(px, py, c), device_id_type=MESH)
            cp.start()
            sends.append(cp)
        for kk in range(1, N_CHIPS):
            px, py = _flip(x, (kk >> 1) & 1), _flip(y, kk & 1)
            pltpu.make_async_remote_copy(src_ref=part.at[dev], dst_ref=mod_ref.at[2 * px + py],
                                         send_sem=s2.at[kk - 1], recv_sem=r2.at[kk - 1],
                                         device_id=(px, py, c), device_id_type=MESH).wait_recv()
        for cp in sends:
            cp.wait_send()

    return _whole_call(
        body, "ada_fwd", (c_pad, w_ada, b_shard),
        [_sds((N_DEV, 8, d), F32), _sds((N_CHIPS, 8, cols), F32)],
        [pltpu.VMEM((N_DEV, 8, cols), F32),
         pltpu.SemaphoreType.DMA((N_DEV - 1,)), pltpu.SemaphoreType.DMA((N_DEV - 1,)),
         pltpu.SemaphoreType.DMA((N_CHIPS - 1,)), pltpu.SemaphoreType.DMA((N_CHIPS - 1,))], carry=carry)


def _ag_weights(bufs):
    n = len(bufs)

    def place():
        x, y, c = _my_place()
        others = [(_flip(x, (kk >> 1) & 1), _flip(y, kk & 1)) for kk in range(1, N_CHIPS)]
        return x, y, c, 2 * x + y, others

    def half(b, which):
        hr = bufs[b].shape[2] // 2
        return pl.ds(pl.multiple_of(which * hr, 16), hr)

    def ici(outs, sems, b, i, slot, x, y, c, px, py):
        rows = outs[b].at[slot, :, half(b, c), :]
        return pltpu.make_async_remote_copy(
            src_ref=rows, dst_ref=rows, send_sem=sems[0].at[3 * b + i], recv_sem=sems[1].at[3 * b + i],
            device_id=(px, py, c), device_id_type=MESH)

    def d2d(outs, sems, b, i, slot, x, y, c, which):
        rows = outs[b].at[slot, :, half(b, which), :]
        return pltpu.make_async_remote_copy(
            src_ref=rows, dst_ref=rows, send_sem=sems[2].at[3 * b + i], recv_sem=sems[3].at[3 * b + i],
            device_id=(x, y, 1 - c), device_id_type=MESH)

    def start(ins, outs, sems):
        x, y, c, chip, others = place()
        for b in range(n):
            for i, (px, py) in enumerate(others):
                ici(outs, sems, b, i, chip, x, y, c, px, py).start()

    def finish(ins, outs, sems):
        x, y, c, chip, others = place()
        for b in range(n):
            for i, (px, py) in enumerate(others):
                ici(outs, sems, b, i, 2 * px + py, x, y, c, px, py).wait_recv()
                d2d(outs, sems, b, i, 2 * px + py, x, y, c, c).start()
        for b in range(n):
            for i, (px, py) in enumerate(others):
                d2d(outs, sems, b, i, 2 * px + py, x, y, c, 1 - c).wait_recv()
        for b in range(n):
            for i, (px, py) in enumerate(others):
                ici(outs, sems, b, i, chip, x, y, c, px, py).wait_send()
                d2d(outs, sems, b, i, 2 * px + py, x, y, c, c).wait_send()

    return _Exchange(bufs, [_sds(s.shape, s.dtype) for s in bufs], {i: i for i in range(n)},
                     [pltpu.SemaphoreType.DMA((3 * n,))] * 4, start, finish)


def _rs_d2d(grads):
    n = len(grads)

    def copy(ins, outs, sems, b):
        x, y, c = _my_place()
        hr = grads[b].shape[2] // 2
        theirs = pl.ds(pl.multiple_of((1 - c) * hr, 8), hr)
        return pltpu.make_async_remote_copy(
            src_ref=ins[b].at[:, :, theirs, :], dst_ref=outs[b], send_sem=sems[0].at[b], recv_sem=sems[1].at[b],
            device_id=(x, y, 1 - c), device_id_type=MESH)

    def start(ins, outs, sems):
        for b in range(n):
            copy(ins, outs, sems, b).start()

    def finish(ins, outs, sems):
        for b in range(n):
            copy(ins, outs, sems, b).wait()

    return _Exchange(grads, [_sds(g.shape[:2] + (g.shape[2] // 2, g.shape[3]), F32) for g in grads], {},
                     [pltpu.SemaphoreType.DMA((n,))] * 2, start, finish)


def _add_halves(core, g, land):
    nchip, ng, rows, cols = g.shape
    hr = rows // 2
    tr = _row_tile(hr, cols)
    steps = hr // tr

    def body(core_ref, g_ref, l_ref, o_ref):
        del core_ref
        o_ref[...] = (g_ref[...] + l_ref[...]).astype(BF)

    return pl.pallas_call(
        body, name="add_halves",
        grid_spec=pltpu.PrefetchScalarGridSpec(
            num_scalar_prefetch=1, grid=(nchip, ng, steps),
            in_specs=[pl.BlockSpec((None, None, tr, cols), lambda j, a, i, cr: (j, a, cr[0] * steps + i, 0)),
                      pl.BlockSpec((None, None, tr, cols), lambda j, a, i, cr: (j, a, i, 0))],
            out_specs=pl.BlockSpec((None, None, tr, cols), lambda j, a, i, cr: (j, a, i, 0))),
        out_shape=_sds((nchip, ng, hr, cols), BF),
        compiler_params=_cp(3))(core, g, land)


def _rs_ici(parts):
    n = len(parts)

    def copies(ins, outs, sems):
        x, y, c = _my_place()
        chip = 2 * x + y
        for b in range(n):
            for kk in range(1, N_CHIPS):
                px, py = _flip(x, (kk >> 1) & 1), _flip(y, kk & 1)
                k = 3 * b + kk - 1
                send = pltpu.make_async_remote_copy(
                    src_ref=ins[b].at[2 * px + py], dst_ref=outs[b].at[chip],
                    send_sem=sems[0].at[k], recv_sem=sems[1].at[k], device_id=(px, py, c), device_id_type=MESH)
                slot = outs[b].at[2 * px + py]
                recv = pltpu.make_async_remote_copy(
                    src_ref=slot, dst_ref=slot, send_sem=sems[0].at[k], recv_sem=sems[1].at[k],
                    device_id=(px, py, c), device_id_type=MESH)
                yield send, recv

    def start(ins, outs, sems):
        for send, _ in copies(ins, outs, sems):
            send.start()

    def finish(ins, outs, sems):
        for send, recv in copies(ins, outs, sems):
            recv.wait_recv()
            send.wait_send()

    return _Exchange(parts, [_sds(p.shape, p.dtype) for p in parts], {},
                     [pltpu.SemaphoreType.DMA((3 * n,))] * 2, start, finish)


def _sum_chips(place, part, land):
    nchip, ng, hr, cols = land.shape
    tr = _row_tile(hr, cols)
    steps = hr // tr

    def body(place_ref, p_ref, l1, l2, l3, o_ref):
        del place_ref
        o_ref[...] = ((p_ref[...].astype(F32) + l1[...].astype(F32)) + l2[...].astype(F32)) + l3[...].astype(F32)

    def slot(k):
        return pl.BlockSpec((None, None, tr, cols), lambda a, i, pr: (jnp.bitwise_xor(pr[1], k), a, i, 0))

    return pl.pallas_call(
        body, name="sum_chips",
        grid_spec=pltpu.PrefetchScalarGridSpec(
            num_scalar_prefetch=1, grid=(ng, steps),
            in_specs=[slot(0), slot(1), slot(2), slot(3)],
            out_specs=pl.BlockSpec((None, tr, cols), lambda a, i, pr: (a, pr[0] * steps + i, 0))),
        out_shape=_sds((ng, 2 * hr, cols), F32),
        compiler_params=_cp(2))(place, part, land, land, land)


def _rs_final(bufs):
    n = len(bufs)

    def copy(outs, sems, b, which):
        x, y, c = _my_place()
        hr = bufs[b].shape[1] // 2
        rows = outs[b].at[:, pl.ds(pl.multiple_of((c if which == 0 else 1 - c) * hr, 8), hr), :]
        return pltpu.make_async_remote_copy(
            src_ref=rows, dst_ref=rows, send_sem=sems[0].at[b], recv_sem=sems[1].at[b],
            device_id=(x, y, 1 - c), device_id_type=MESH)

    def start(ins, outs, sems):
        for b in range(n):
            copy(outs, sems, b, 0).start()

    def finish(ins, outs, sems):
        for b in range(n):
            copy(outs, sems, b, 0).wait_send()
            copy(outs, sems, b, 1).wait_recv()

    return _Exchange(bufs, [_sds(h.shape, F32) for h in bufs], {i: i for i in range(n)},
                     [pltpu.SemaphoreType.DMA((n,))] * 2, start, finish)


def _small_sync(smalls, dmod_blk, c_all, carry=None):
    d = c_all.shape[-1]
    cols = dmod_blk.shape[-1]
    chunk = 384

    def body(sm_ref, dm_ref, c_ref, sum_ref, gw_ref, sm_all, dm_all, ssem, rsem):
        x, y, c = _my_place()
        dev = 4 * x + 2 * y + c
        chip = 2 * x + y
        sm_all[dev] = sm_ref[...]
        dm_all[dev] = dm_ref[chip]
        sends = []
        for k in range(1, N_DEV):
            px, py, pc = _flip(x, (k >> 2) & 1), _flip(y, (k >> 1) & 1), _flip(c, k & 1)
            a = pltpu.make_async_remote_copy(src_ref=sm_ref, dst_ref=sm_all.at[dev], send_sem=ssem.at[2 * (k - 1)],
                                             recv_sem=rsem.at[2 * (k - 1)], device_id=(px, py, pc),
                                             device_id_type=MESH)
            b = pltpu.make_async_remote_copy(src_ref=dm_ref.at[2 * px + py], dst_ref=dm_all.at[dev],
                                             send_sem=ssem.at[2 * (k - 1) + 1], recv_sem=rsem.at[2 * (k - 1) + 1],
                                             device_id=(px, py, pc), device_id_type=MESH)
            a.start()
            b.start()
            sends += [a, b]
        for k in range(1, N_DEV):
            px, py, pc = _flip(x, (k >> 2) & 1), _flip(y, (k >> 1) & 1), _flip(c, k & 1)
            pdev = 4 * px + 2 * py + pc
            pltpu.make_async_remote_copy(src_ref=sm_ref, dst_ref=sm_all.at[pdev], send_sem=ssem.at[2 * (k - 1)],
                                         recv_sem=rsem.at[2 * (k - 1)], device_id=(px, py, pc),
                                         device_id_type=MESH).wait_recv()
            pltpu.make_async_remote_copy(src_ref=dm_ref.at[chip], dst_ref=dm_all.at[pdev],
                                         send_sem=ssem.at[2 * (k - 1) + 1], recv_sem=rsem.at[2 * (k - 1) + 1],
                                         device_id=(px, py, pc), device_id_type=MESH).wait_recv()
        for cp in sends:
            cp.wait_send()

        tot = sm_all[0]
        for q in range(1, N_DEV):
            tot = tot + sm_all[q]
        sum_ref[...] = tot

        cs = c_ref[...].reshape(N_DEV * 8, d)
        sc = (cs * jax.nn.sigmoid(cs)).astype(BF)
        for n0 in range(0, cols, chunk):
            dmv = dm_all[:, :, n0:n0 + chunk].reshape(N_DEV * 8, chunk).astype(BF)
            gw_ref[:, n0:n0 + chunk] = _tn(sc, dmv)

    return _whole_call(
        body, "small_sync", (smalls, dmod_blk, c_all),
        [_sds(smalls.shape, F32), _sds((d, cols), F32)],
        [pltpu.VMEM((N_DEV,) + smalls.shape, F32), pltpu.VMEM((N_DEV, 8, cols), F32),
         pltpu.SemaphoreType.DMA((2 * (N_DEV - 1),)), pltpu.SemaphoreType.DMA((2 * (N_DEV - 1),))], carry=carry)


def _bucket_onehot():
    maps = np.stack([_bucket_map(dil).reshape(-1) for _, dil in DIL_CONFIGS])
    return (jnp.asarray(maps)[:, None, :] == jnp.arange(N_BUCKETS, dtype=jnp.int32)[None, :, None]).astype(BF)


def _dil_bias(rel_t, onehot):
    def body(r_ref, oh_ref, o_ref):
        rv = r_ref[...]
        hi = rv.astype(BF)
        lo = (rv - hi.astype(F32)).astype(BF)
        for c in range(len(DIL_CONFIGS)):
            o_ref[c] = _nn(hi, oh_ref[c]) + _nn(lo, oh_ref[c])

    return pl.pallas_call(body, name="dil_bias",
                          out_shape=_sds((len(DIL_CONFIGS), N_HEADS, BLOCK * 2 * BLOCK), F32),
                          compiler_params=_cp())(rel_t, onehot)


def _rowsum8(a):
    def body(a_ref, o_ref):
        o_ref[...] = jnp.sum(a_ref[...], axis=0, keepdims=True)

    return pl.pallas_call(body, name="rowsum8", out_shape=_sds((1, a.shape[1]), F32), compiler_params=_cp())(a)


def _local_step(x, mod, target, w, gains, rel_bias, place=None):
    nb, seq, d = x.shape
    t = nb * seq
    dist = place is not None
    core = place[0:1] if dist else None
    x0 = x.reshape(t, d)
    tgt = target.reshape(t, d)
    md = [mod[:, i:i + 1, :] for i in range(N_MOD)]
    sh1, sc1, gt1, sh2, sc2, gt2, sh3, sc3, gt3 = md
    g1, g2, g3 = gains["g_ffn1"], gains["g_mix"], gains["g_ffn2"]
    ones_g = _group_ones()

    def partial_sums(grads, lands):
        return [_add_halves(core, g, l) for g, l in zip(grads, lands)]

    def chip_sums(parts, lands):
        return [_sum_chips(place, p, l) for p, l in zip(parts, lands)]

    res = _ffn_up(x0, g1, sc1, sh1, w["gu1"], seq,
                  carry=_ag_weights([w["d1"], w["win"], w["wout"]]) if dist else None)
    h1, a1, u1, s1 = res[:4]
    wd1, w_in, w_out = res[4:] if dist else (w["d1"], w["win"], w["wout"])
    w_out2 = w_out.reshape(2 * D_GRP, d)
    f1, x1 = _ffn_down(s1, wd1, x0, gt1, seq, 0.5)

    h2, qkv6, qkv_r4, qkv_r16 = _qkv_proj(x1, g2, sc2, sh2, w_in, seq)
    qkv6b = qkv6.reshape(6, nb, seq, D_GRP)
    res = _sb_fwd(qkv6b, gains["g_sb_out"], nb, seq, carry=_ag_weights([w["gu2"], w["d2"]]) if dist else None)
    o_sb, on_sb = res[:2]
    wgu2, wd2 = res[2:] if dist else (w["gu2"], w["d2"])
    onehot = _bucket_onehot()
    bias = _dil_bias(rel_bias.T, onehot).reshape(len(DIL_CONFIGS), N_HEADS, BLOCK, 2 * BLOCK)
    o_cs, l_cs = [], []
    qkv_rs = [(qkv6b, 3), (qkv_r4, 0), (qkv_r16, 0)]
    for ci, (_, dil) in enumerate(DIL_CONFIGS):
        sub = seq // dil
        arr, base = qkv_rs[ci]
        arr = arr.reshape(base + 3, nb, sub, dil * D_GRP)
        qkv_rs[ci] = (arr, base)
        o_c, l_c = _dil_fwd(arr, base, bias[ci], nb, sub, dil)
        o_cs.append(o_c.reshape(t // dil, dil * D_GRP))
        l_cs.append(l_c.reshape(t // dil, dil * D_GRP))
    o_dil, on_dil = _dil_comb(o_cs, l_cs, gains["g_dil_out"])
    tmix, x2 = _mix_out(on_sb.reshape(t, D_GRP), on_dil, w_out2, x1, gt2, seq)

    h3, a3, u3, s3 = _ffn_up(x2, g3, sc3, sh3, wgu2, seq)
    f3, dx3, dg_final, loss = _ffn_down_loss(s3, wd2, x2, gt3, seq, 0.5, gains["g_final"], tgt)

    da3, du3, df3, dgt3, dx2, dsh3, dsc3, dg3 = _ffn_bwd_x(dx3, gt3, f3, wd2, a3, u3, wgu2, x2, g3, sc3, seq, 0.5)
    grads2 = [_ffn_bwd_w(h3, da3, du3, s3, df3)]

    res = _mix_bwd_out(
        dx2, gt2, tmix, w_out2, o_sb.reshape(t, D_GRP), o_dil, on_sb.reshape(t, D_GRP), on_dil,
        gains["g_sb_out"], gains["g_dil_out"], ones_g, seq, carry=_rs_d2d(grads2) if dist else None)
    do_sb, do_dil, dgt2, dg_sb, dg_dil, dw_out = res[:6]
    parts2 = partial_sums(grads2, res[6:]) if dist else None
    dw_out = dw_out.reshape(N_CHIPS, 1, 2 * D_GRP // N_CHIPS, d)
    res = _sb_bwd(qkv6b, do_sb.reshape(nb, seq, D_GRP), nb, seq, carry=_rs_ici(parts2) if dist else None)
    dqkv6 = res[0]
    halves2 = chip_sums(parts2, res[1:]) if dist else None
    dcs = _dil_comb_bwd(do_dil, o_cs, l_cs)
    dsum, a_tiles = [], []
    for ci, (_, dil) in enumerate(DIL_CONFIGS):
        sub = seq // dil
        do_c = dcs[ci].reshape(nb, sub, dil * D_GRP)
        dd_c = dcs[3 + ci].reshape(nb, sub, dil * D_GRP)
        res = _dil_bwd(qkv_rs[ci][0], qkv_rs[ci][1], bias[ci], do_c, dd_c, nb, sub, dil,
                       carry=_rs_final(halves2) if dist and ci == 0 else None)
        if dist and ci == 0:
            grads2 = res[2:]
        dsum.append(res[0].reshape(3, t // dil, dil * D_GRP))
        a_tiles.append(res[1].reshape(N_HEADS, BLOCK * 2 * BLOCK))
    dqkv6 = _dqkv_dil_sum(dsum, dqkv6.reshape(6, t, D_GRP))
    drel = _relbias_grad(jnp.stack(a_tiles), onehot)
    dx1, dsh2, dsc2, dg2 = _mix_bwd_dh(dqkv6, w_in, x1, g2, sc2, dx2, seq)

    da1, du1, df1, dgt1 = _ffn_bwd_ds(dx1, gt1, f1, wd1, a1, u1, seq, 0.5)
    grads1 = [_ffn_bwd_w(h1, da1, du1, s1, df1)]
    res = _dw_in(h2, dqkv6, carry=_rs_d2d(grads1) if dist else None)
    grads_m = [res[0], dw_out]
    parts1 = partial_sums(grads1, res[1:]) if dist else None
    res = _ffn_bwd_dh(da1, du1, w["gu1"], x0, g1, sc1, dx1, seq,
                      carry=_join([_rs_ici(parts1), _rs_d2d(grads_m)]) if dist else None)
    dx0, dsh1, dsc1, dg1 = res[:4]
    pending = None
    if dist:
        pending = (chip_sums(parts1, res[4:5]), partial_sums(grads_m, res[5:7]))

    dmod = jnp.concatenate([dsh1, dsc1, dgt1, dsh2, dsc2, dgt2, dsh3, dsc3, dgt3], axis=1)
    return dict(grad_x=dx0.reshape(nb, seq, d), loss=loss[0, 0], dmod=dmod.reshape(nb, N_MOD * d),
                dffn1=grads1[0], dffn2=grads2[0], dwin=grads_m[0], dwout=grads_m[1], pending=pending,
                dg_ffn1=dg1, dg_mix=dg2, dg_ffn2=dg3, dg_final=dg_final, dg_sb=dg_sb, dg_dil=dg_dil,
                drel=drel.T)


_SMALL_ORDER = (("b_ada", N_MOD * 1024), ("g_ffn1", 1024), ("g_mix", 1024), ("g_ffn2", 1024), ("g_final", 1024),
                ("g_sb_out", D_GRP), ("g_dil_out", D_GRP), ("rel_bias", N_BUCKETS * N_HEADS))


def _pack_small(parts, extra=None):
    flat = [parts[name].reshape(-1).astype(F32) for name, _ in _SMALL_ORDER]
    used = sum(sz for _, sz in _SMALL_ORDER)
    pad = SMALL_ROWS * 128 - used
    tail = jnp.zeros((pad,), F32)
    if extra is not None:
        tail = tail.at[0].set(extra)
    return jnp.concatenate(flat + [tail]).reshape(SMALL_ROWS, 128)


def _unpack_small(packed, shapes):
    flat = packed.reshape(-1)
    out, off = {}, 0
    for name, sz in _SMALL_ORDER:
        out[name] = flat[off:off + sz].reshape(shapes[name])
        off += sz
    return out, flat[off]


def kernel(x, c, w_ada, b_ada, g_ffn1, w1_gate, w1_up, w1_down, g_mix, w_in, g_sb_out, g_dil_out, w_out, rel_bias, g_ffn2, w2_gate, w2_up, w2_down, g_final, loss_target, m_w_ada, m_b_ada, m_g_ffn1, m_w1_gate, m_w1_up, m_w1_down, m_g_mix, m_w_in, m_g_sb_out, m_g_dil_out, m_w_out, m_rel_bias, m_g_ffn2, m_w2_gate, m_w2_up, m_w2_down, m_g_final, v_w_ada, v_b_ada, v_g_ffn1, v_w1_gate, v_w1_up, v_w1_down, v_g_mix, v_w_in, v_g_sb_out, v_g_dil_out, v_w_out, v_rel_bias, v_g_ffn2, v_w2_gate, v_w2_up, v_w2_down, v_g_final):
    nb, seq, d = x.shape
    xi, yi, ci = lax.axis_index("x"), lax.axis_index("y"), lax.axis_index("c")
    chip = 2 * xi + yi
    ada_cols = w_ada.shape[-1]

    c_pad = jnp.zeros((8, d), F32).at[:nb].set(c)
    b_shard = lax.dynamic_slice(b_ada, (0, chip * ada_cols), (1, ada_cols))
    shards = dict(gu1=jnp.stack([w1_gate[0], w1_up[0]]), d1=w1_down, win=w_in, wout=w_out,
                  gu2=jnp.stack([w2_gate[0], w2_up[0]]), d2=w2_down)
    bufs = {k: lax.dynamic_update_slice(lax.empty((N_CHIPS,) + s.shape, BF), s.astype(BF)[None], (chip, 0, 0, 0))
            for k, s in shards.items()}
    c_all, mod_blk, bufs["gu1"] = _ada_fwd(c_pad, w_ada[0], b_shard, carry=_ag_weights([bufs["gu1"]]))
    mod = jnp.transpose(mod_blk[:, :nb, :], (1, 0, 2)).reshape(nb, N_MOD, d)

    gains = dict(g_ffn1=g_ffn1, g_mix=g_mix, g_ffn2=g_ffn2, g_final=g_final.reshape(1, d),
                 g_sb_out=g_sb_out.reshape(1, D_GRP), g_dil_out=g_dil_out.reshape(1, D_GRP))
    place = jnp.stack([ci, chip]).astype(jnp.int32)
    r = _local_step(x, mod, loss_target, bufs, gains, rel_bias, place)

    dmod = r["dmod"]
    dmod_pad = jnp.zeros((8, N_MOD * d), F32).at[:nb].set(dmod)
    dmod_blk = jnp.transpose(dmod_pad.reshape(8, N_CHIPS, ada_cols), (1, 0, 2))
    small_parts = dict(b_ada=_rowsum8(dmod_pad), g_ffn1=r["dg_ffn1"], g_mix=r["dg_mix"], g_ffn2=r["dg_ffn2"],
                       g_final=r["dg_final"], g_sb_out=r["dg_sb"], g_dil_out=r["dg_dil"], rel_bias=r["drel"])
    halves1, parts_m = r["pending"]
    res = _small_sync(_pack_small(small_parts, r["loss"]), dmod_blk, c_all,
                      carry=_join([_rs_final(halves1), _rs_ici(parts_m)]))
    small_sum, g_wada, gffn1 = res[:3]
    halves_m = [_sum_chips(place, p, l) for p, l in zip(parts_m, res[3:5])]
    gwin, gwout = _alone("rs_last", _rs_final(halves_m))
    gffn2 = r["dffn2"]

    small_w = dict(b_ada=b_ada, g_ffn1=g_ffn1, g_mix=g_mix, g_ffn2=g_ffn2, g_final=g_final,
                   g_sb_out=g_sb_out, g_dil_out=g_dil_out, rel_bias=rel_bias)
    small_m = dict(b_ada=m_b_ada, g_ffn1=m_g_ffn1, g_mix=m_g_mix, g_ffn2=m_g_ffn2, g_final=m_g_final,
                   g_sb_out=m_g_sb_out, g_dil_out=m_g_dil_out, rel_bias=m_rel_bias)
    small_v = dict(b_ada=v_b_ada, g_ffn1=v_g_ffn1, g_mix=v_g_mix, g_ffn2=v_g_ffn2, g_final=v_g_final,
                   g_sb_out=v_g_sb_out, g_dil_out=v_g_dil_out, rel_bias=v_rel_bias)
    shapes = {k: v.shape for k, v in small_w.items()}
    sg, sd, sm, sv = _adamw(_pack_small(small_w), small_sum.reshape(1, SMALL_ROWS, 128), 0,
                            _pack_small(small_m), _pack_small(small_v))
    sg, loss = _unpack_small(sg, shapes)
    sd, _ = _unpack_small(sd, shapes)
    sm, _ = _unpack_small(sm, shapes)
    sv, _ = _unpack_small(sv, shapes)

    big = {}

    def upd(name, w, g_arr, sel, m, v, transposed=False):
        swap = (lambda a: jnp.swapaxes(a, -1, -2)) if transposed else (lambda a: a)
        w2, m2, v2 = [swap(a)[0] for a in (w, m, v)]
        big[name] = [swap(a[None]) for a in _adamw(w2, g_arr, sel, m2, v2)]

    upd("w_ada", w_ada, g_wada.reshape(1, d, ada_cols), 0, m_w_ada, v_w_ada)
    upd("w1_gate", w1_gate, gffn1, 0, m_w1_gate, v_w1_gate, transposed=True)
    upd("w1_up", w1_up, gffn1, 1, m_w1_up, v_w1_up, transposed=True)
    upd("w1_down", w1_down, gffn1, 2, m_w1_down, v_w1_down)
    upd("w_in", w_in, gwin, 0, m_w_in, v_w_in)
    upd("w_out", w_out, gwout, 0, m_w_out, v_w_out)
    upd("w2_gate", w2_gate, gffn2, 0, m_w2_gate, v_w2_gate, transposed=True)
    upd("w2_up", w2_up, gffn2, 1, m_w2_up, v_w2_up, transposed=True)
    upd("w2_down", w2_down, gffn2, 2, m_w2_down, v_w2_down)

    names = ["w_ada", "b_ada", "g_ffn1", "w1_gate", "w1_up", "w1_down", "g_mix", "w_in", "g_sb_out", "g_dil_out",
             "w_out", "rel_bias", "g_ffn2", "w2_gate", "w2_up", "w2_down", "g_final"]
    outs = [loss, r["grad_x"]]
    for k, small in enumerate((sg, sd, sm, sv)):
        for name in names:
            outs.append(big[name][k] if name in big else small[name])
    return tuple(outs)
```

```python
import functools
import math

import numpy as np
import jax
import jax.numpy as jnp
from jax import lax
from jax.experimental import pallas as pl
from jax.experimental.pallas import tpu as pltpu

F32 = jnp.float32
BF = jnp.bfloat16
MESH = pl.DeviceIdType.MESH

HEAD_DIM = 64
N_HEADS = 8
D_GRP = N_HEADS * HEAD_DIM
DIL_CONFIGS = ((128, 1), (512, 4), (2048, 16))
N_STEPS = 128
BLOCK = 128
N_BUCKETS = 32
MAX_DISTANCE = 2048
N_MOD = 9
EPS = 1e-6
NEG_INF = -1e30
SCALE = HEAD_DIM ** -0.5

ADAM_LR = 0.001
ADAM_B1 = 0.9
ADAM_B2 = 0.999
ADAM_EPS = 1e-08
ADAM_WD = 0.01
ADAM_STEP = 10

N_CHIPS = 4
N_DEV = 8
VMEM_LIMIT = 56 * 1024 * 1024
TM = 512
TQ = 256
KB = 256
SMALL_ROWS = 120


def _cp(n_axes=0, **kw):
    sem = ("arbitrary",) * n_axes if n_axes else None
    return pltpu.CompilerParams(dimension_semantics=sem, vmem_limit_bytes=VMEM_LIMIT, **kw)


def _nn(a, b):
    return jnp.dot(a, b, preferred_element_type=F32)


def _nt(a, b):
    return lax.dot_general(a, b, (((1,), (1,)), ((), ())), preferred_element_type=F32)


def _tn(a, b):
    return lax.dot_general(a, b, (((0,), (0,)), ((), ())), preferred_element_type=F32)


def _nn2(x, m):
    hi = x.astype(BF)
    lo = (x - hi.astype(F32)).astype(BF)
    r = _nn(jnp.concatenate([hi, lo], axis=0), m)
    return r[:x.shape[0]] + r[x.shape[0]:]


def _softplus(z):
    return jnp.maximum(z, 0.0) + jnp.log1p(jnp.exp(-jnp.abs(z)))


def _sds(shape, dtype):
    return jax.ShapeDtypeStruct(shape, dtype)


def _whole(a):
    nd = a.ndim
    return pl.BlockSpec(a.shape, lambda *_: (0,) * nd, pipeline_mode=pl.Buffered(1))


def _modnorm_bwd_tile(dh, xv, gv, scv, dxo):
    r = lax.rsqrt(jnp.mean(xv * xv, axis=-1, keepdims=True) + EPS)
    n = xv * r
    ng = n * gv
    dsh = jnp.sum(dh, axis=0, keepdims=True)
    dsc = jnp.sum(dh * ng, axis=0, keepdims=True)
    dy = dh * (1.0 + scv)
    dg = jnp.sum(dy * n, axis=0, keepdims=True)
    dn = dy * gv
    dx = dxo + r * (dn - n * jnp.mean(dn * n, axis=-1, keepdims=True))
    return dx, dsh, dsc, dg


def _acc_rows(ref, val, first):
    @pl.when(first)
    def _():
        ref[...] = val

    @pl.when(jnp.logical_not(first))
    def _():
        ref[...] += val


def _modnorm_tile(x_ref, g_ref, sc_ref, sh_ref):
    xv = x_ref[...]
    r = lax.rsqrt(jnp.mean(xv * xv, axis=-1, keepdims=True) + EPS)
    return (((xv * r) * g_ref[...]) * (1.0 + sc_ref[...]) + sh_ref[...]).astype(BF)


def _ffn_up(x, g, sc, sh, wgu, seq, carry=None):
    t, d = x.shape
    fs = wgu.shape[-1]
    per = seq // TM

    def body(x_ref, g_ref, sc_ref, sh_ref, w_ref, h_ref, p_ref, q_ref, s_ref):
        hv = _modnorm_tile(x_ref, g_ref, sc_ref, sh_ref)
        h_ref[...] = hv
        for j in range(N_CHIPS):
            a = _nn(hv, w_ref[j, 0])
            u = _nn(hv, w_ref[j, 1])
            sig = jax.nn.sigmoid(a)
            q = a * sig
            p_ref[j] = (u * (sig * (1.0 + a * (1.0 - sig)))).astype(BF)
            q_ref[j] = q.astype(BF)
            s_ref[j] = (q * u).astype(BF)

    row = pl.BlockSpec((TM, d), lambda m: (m, 0))
    ex = pl.BlockSpec((None, 1, d), lambda m: (m // per, 0, 0))
    blk = pl.BlockSpec((N_CHIPS, TM, fs), lambda m: (0, m, 0))
    return _call(
        body, "ffn_up", (t // TM,),
        [row, pl.BlockSpec((1, d), lambda m: (0, 0)), ex, ex, _whole(wgu)],
        [row, blk, blk, blk],
        [_sds((t, d), BF)] + [_sds((N_CHIPS, t, fs), BF)] * 3,
        (x, g, sc, sh, wgu), carry=carry)


def _ffn_down(s, wd, x, gt, seq, coef):
    _, t, fs = s.shape
    d = x.shape[-1]
    per = seq // TM

    def body(s_ref, w_ref, x_ref, gt_ref, f_ref, xo_ref):
        f = _nn(s_ref[0], w_ref[0, 0])
        for j in range(1, N_CHIPS):
            f = f + _nn(s_ref[j], w_ref[j, 0])
        f_ref[...] = f
        xo_ref[...] = x_ref[...] + (coef * gt_ref[...]) * f

    row = pl.BlockSpec((TM, d), lambda m: (m, 0))
    return pl.pallas_call(
        body, name="ffn_down", grid=(t // TM,),
        in_specs=[pl.BlockSpec((N_CHIPS, TM, fs), lambda m: (0, m, 0)),
                  _whole(wd), row,
                  pl.BlockSpec((None, 1, d), lambda m: (m // per, 0, 0))],
        out_specs=[row, row],
        out_shape=[_sds((t, d), F32), _sds((t, d), F32)],
        compiler_params=_cp(1))(s, wd, x, gt)


def _ffn_bwd_ds(dxo, gt, f, wd, p, q, seq, coef, carry=None):
    t, d = dxo.shape
    fs = p.shape[-1]
    per = seq // TM
    nb = t // seq

    def body(dxo_ref, gt_ref, f_ref, w_ref, p_ref, q_ref, da_ref, du_ref, df_ref, dgt_ref):
        m = pl.program_id(0)
        dxv = dxo_ref[...]
        df = ((coef * gt_ref[...]) * dxv).astype(BF)
        df_ref[...] = df
        _acc_rows(dgt_ref, coef * jnp.sum(dxv * f_ref[...], axis=0, keepdims=True), m % per == 0)
        for j in range(N_CHIPS):
            ds = _nt(df, w_ref[j, 0])
            da_ref[j] = (ds * p_ref[j].astype(F32)).astype(BF)
            du_ref[j] = (ds * q_ref[j].astype(F32)).astype(BF)

    row = pl.BlockSpec((TM, d), lambda m: (m, 0))
    blk = pl.BlockSpec((N_CHIPS, TM, fs), lambda m: (0, m, 0))
    ex = pl.BlockSpec((None, 1, d), lambda m: (m // per, 0, 0))
    return _call(
        body, "ffn_bwd_ds", (t // TM,),
        [row, ex, row, _whole(wd), blk, blk],
        [blk, blk, row, ex],
        [_sds((N_CHIPS, t, fs), BF), _sds((N_CHIPS, t, fs), BF), _sds((t, d), BF), _sds((nb, 1, d), F32)],
        (dxo, gt, f, wd, p, q), carry=carry)


TM_X = 256


def _ffn_bwd_x(dxo, gt, f, wd, p, q, wgu, x, g, sc, seq, coef):
    t, d = dxo.shape
    fs = p.shape[-1]
    per = seq // TM_X
    nb = t // seq

    def body(dxo_ref, gt_ref, f_ref, wd_ref, p_ref, q_ref, w_ref, x_ref, g_ref, sc_ref,
             da_ref, du_ref, df_ref, dgt_ref, dx_ref, dsh_ref, dsc_ref, dg_ref):
        m = pl.program_id(0)
        dxv = dxo_ref[...]
        df = ((coef * gt_ref[...]) * dxv).astype(BF)
        df_ref[...] = df
        _acc_rows(dgt_ref, coef * jnp.sum(dxv * f_ref[...], axis=0, keepdims=True), m % per == 0)
        dh = None
        for j in range(N_CHIPS):
            ds = _nt(df, wd_ref[j, 0])
            da = (ds * p_ref[j].astype(F32)).astype(BF)
            du = (ds * q_ref[j].astype(F32)).astype(BF)
            da_ref[j] = da
            du_ref[j] = du
            part = _nt(da, w_ref[j, 0]) + _nt(du, w_ref[j, 1])
            dh = part if dh is None else dh + part
        dx, dsh, dsc, dg = _modnorm_bwd_tile(dh, x_ref[...], g_ref[...], sc_ref[...], dxv)
        dx_ref[...] = dx
        _acc_rows(dsh_ref, dsh, m % per == 0)
        _acc_rows(dsc_ref, dsc, m % per == 0)
        _acc_rows(dg_ref, dg, m == 0)

    row = pl.BlockSpec((TM_X, d), lambda m: (m, 0))
    blk = pl.BlockSpec((N_CHIPS, TM_X, fs), lambda m: (0, m, 0))
    ex = pl.BlockSpec((None, 1, d), lambda m: (m // per, 0, 0))
    vec = pl.BlockSpec((1, d), lambda m: (0, 0))
    exs = _sds((nb, 1, d), F32)
    return pl.pallas_call(
        body, name="ffn_bwd_x", grid=(t // TM_X,),
        in_specs=[row, ex, row, _whole(wd), blk, blk, _whole(wgu), row, vec, ex],
        out_specs=[blk, blk, row, ex, row, ex, ex, vec],
        out_shape=[_sds((N_CHIPS, t, fs), BF), _sds((N_CHIPS, t, fs), BF), _sds((t, d), BF), exs,
                   _sds((t, d), F32), exs, exs, _sds((1, d), F32)],
        compiler_params=_cp(1))(dxo, gt, f, wd, p, q, wgu, x, g, sc)


TK_W = 1024


def _ffn_bwd_w(h, da, du, s, df):
    t, d = h.shape
    fs = da.shape[-1]

    def body(h_ref, da_ref, du_ref, s_ref, df_ref, o_ref):
        kt = pl.program_id(1)
        hv = h_ref[...]
        parts = (_tn(da_ref[...], hv), _tn(du_ref[...], hv), _tn(s_ref[...], df_ref[...]))

        @pl.when(kt == 0)
        def _():
            for i, p in enumerate(parts):
                o_ref[i] = p

        @pl.when(kt != 0)
        def _():
            for i, p in enumerate(parts):
                o_ref[i] += p

    row = pl.BlockSpec((TK_W, d), lambda j, kt: (kt, 0))
    blk = pl.BlockSpec((None, TK_W, fs), lambda j, kt: (j, kt, 0))
    return pl.pallas_call(
        body, name="ffn_bwd_w", grid=(N_CHIPS, t // TK_W),
        in_specs=[row, blk, blk, blk, row],
        out_specs=pl.BlockSpec((None, 3, fs, d), lambda j, kt: (j, 0, 0, 0)),
        out_shape=_sds((N_CHIPS, 3, fs, d), F32),
        compiler_params=_cp(2))(h, da, du, s, df)


def _ffn_bwd_dh(da, du, wgu, x, g, sc, dxo, seq, carry=None):
    _, t, fs = da.shape
    d = x.shape[-1]
    per = seq // TM
    nb = t // seq

    def body(da_ref, du_ref, w_ref, x_ref, g_ref, sc_ref, dxo_ref, dx_ref, dsh_ref, dsc_ref, dg_ref):
        m = pl.program_id(0)
        dh = _nt(da_ref[0], w_ref[0, 0]) + _nt(du_ref[0], w_ref[0, 1])
        for j in range(1, N_CHIPS):
            dh = dh + _nt(da_ref[j], w_ref[j, 0]) + _nt(du_ref[j], w_ref[j, 1])
        dx, dsh, dsc, dg = _modnorm_bwd_tile(dh, x_ref[...], g_ref[...], sc_ref[...], dxo_ref[...])
        dx_ref[...] = dx
        _acc_rows(dsh_ref, dsh, m % per == 0)
        _acc_rows(dsc_ref, dsc, m % per == 0)
        _acc_rows(dg_ref, dg, m == 0)

    row = pl.BlockSpec((TM, d), lambda m: (m, 0))
    blk = pl.BlockSpec((N_CHIPS, TM, fs), lambda m: (0, m, 0))
    ex = pl.BlockSpec((None, 1, d), lambda m: (m // per, 0, 0))
    vec = pl.BlockSpec((1, d), lambda m: (0, 0))
    return _call(
        body, "ffn_bwd_dh", (t // TM,),
        [blk, blk, _whole(wgu), row, vec, ex, row],
        [row, ex, ex, vec],
        [_sds((t, d), F32), _sds((nb, 1, d), F32), _sds((nb, 1, d), F32), _sds((1, d), F32)],
        (da, du, wgu, x, g, sc, dxo), carry=carry)


def _qkv_proj(x, g, sc, sh, w_in, seq, carry=None):
    t, d = x.shape
    wc = w_in.shape[-1]
    per = seq // TM

    dils = [dil for _, dil in DIL_CONFIGS if dil > 1]

    def body(x_ref, g_ref, sc_ref, sh_ref, w_ref, h_ref, o_ref, *rest):
        res_refs, buf = rest[:len(dils)], rest[len(dils)]
        hv = _modnorm_tile(x_ref, g_ref, sc_ref, sh_ref)
        h_ref[...] = hv
        for j in range(N_CHIPS):
            rf = _nn(hv, w_ref[j, 0])
            r = rf.astype(BF)
            for a, lc, off, width in _col_pieces(j, wc):
                o_ref[a, :, lc:lc + width] = r[:, off:off + width]
                if a < 3:
                    continue
                for c0 in range(0, width, 128):
                    cg = (lc + c0) // 128
                    buf[...] = rf[:, off + c0:off + c0 + 128]
                    for ref, dil in zip(res_refs, dils):
                        for rr in range(dil):
                            ref[a - 3, :, rr * D_GRP + cg * 128:rr * D_GRP + (cg + 1) * 128] = (
                                buf[pl.ds(rr, TM // dil, stride=dil), :].astype(BF))

    row = pl.BlockSpec((TM, d), lambda m: (m, 0))
    ex = pl.BlockSpec((None, 1, d), lambda m: (m // per, 0, 0))
    return _call(
        body, "qkv_proj", (t // TM,),
        [row, pl.BlockSpec((1, d), lambda m: (0, 0)), ex, ex, _whole(w_in)],
        [row, pl.BlockSpec((6, TM, D_GRP), lambda m: (0, m, 0))]
        + [pl.BlockSpec((3, TM // dil, dil * D_GRP), lambda m: (0, m, 0)) for dil in dils],
        [_sds((t, d), BF), _sds((6, t, D_GRP), BF)] + [_sds((3, t // dil, dil * D_GRP), BF) for dil in dils],
        (x, g, sc, sh, w_in), scratch=[pltpu.VMEM((TM, 128), F32)], carry=carry)


def _col_pieces(j, wc):
    out, off = [], 0
    while off < wc:
        a, lc = divmod(j * wc + off, D_GRP)
        width = min(D_GRP - lc, wc - off)
        out.append((a, lc, off, width))
        off += width
    return out


def _chip_cols(g6_ref, j, wc):
    return jnp.concatenate([g6_ref[a, :, lc:lc + width] for a, lc, _, width in _col_pieces(j, wc)], axis=1)


def _mix_out(on_sb, on_dil, w_out, x, gt, seq):
    t, d = x.shape
    per = seq // TM

    def body(a_ref, b_ref, w_ref, x_ref, gt_ref, t_ref, xo_ref):
        tv = _nn(a_ref[...], w_ref[0:D_GRP, :]) + _nn(b_ref[...], w_ref[D_GRP:2 * D_GRP, :])
        t_ref[...] = tv
        xo_ref[...] = x_ref[...] + gt_ref[...] * tv

    row = pl.BlockSpec((TM, d), lambda m: (m, 0))
    half = pl.BlockSpec((TM, D_GRP), lambda m: (m, 0))
    return pl.pallas_call(
        body, name="mix_out", grid=(t // TM,),
        in_specs=[half, half, pl.BlockSpec((2 * D_GRP, d), lambda m: (0, 0)), row,
                  pl.BlockSpec((None, 1, d), lambda m: (m // per, 0, 0))],
        out_specs=[row, row],
        out_shape=[_sds((t, d), F32), _sds((t, d), F32)],
        compiler_params=_cp(1))(on_sb, on_dil, w_out, x, gt)


def _sb_masks():
    lane = lax.broadcasted_iota(jnp.int32, (1, 2 * HEAD_DIM), 1)
    hm0 = lane < HEAD_DIM
    rel = lax.broadcasted_iota(jnp.int32, (TQ, KB), 0) - lax.broadcasted_iota(jnp.int32, (TQ, KB), 1)
    kr = lax.broadcasted_iota(jnp.int32, (KB, KB), 0)
    kc = lax.broadcasted_iota(jnp.int32, (KB, KB), 1)
    return hm0, rel, kr, kc


def _stack_pair(x, hm0):
    zero = jnp.zeros_like(x)
    return jnp.concatenate([jnp.where(hm0, x, zero), jnp.where(hm0, zero, x)], axis=0)


def _headnorm_pair(o, gv, hm0):
    o2 = o * o
    ms0 = jnp.sum(jnp.where(hm0, o2, 0.0), axis=-1, keepdims=True) * (1.0 / HEAD_DIM)
    ms1 = jnp.sum(jnp.where(hm0, 0.0, o2), axis=-1, keepdims=True) * (1.0 / HEAD_DIM)
    r = jnp.where(hm0, lax.rsqrt(ms0 + EPS), lax.rsqrt(ms1 + EPS))
    return (o * r) * gv


SB_DEAD = -104.0


def _alive(c_l):
    return (jnp.max(c_l) > SB_DEAD).astype(jnp.int32)


def _sb_fwd(qkv6, g_sb, nb, seq, carry=None):
    nq = seq // TQ

    def body(q_ref, k_ref, v_ref, g_ref, o_ref, on_ref):
        qi = pl.program_id(2)
        hm0, rel, kr, kc = _sb_masks()
        upper = (kr > kc).astype(BF)
        qs = _stack_pair(q_ref[...], hm0)
        causal2 = jnp.concatenate([rel, rel], axis=0) > 0

        def block(kj, causal, c_l, acc):
            ks = pl.multiple_of(kj * KB, KB)
            z = _nt(qs, k_ref[pl.ds(ks, KB), :]) * SCALE
            sp = _softplus(z)
            ln = -sp if causal is None else jnp.where(causal, -sp, 0.0)
            suf = _nn2(ln, upper)
            w = jnp.exp((z - sp) + (suf + c_l))
            if causal is not None:
                w = jnp.where(causal, w, 0.0)
            return c_l + (suf[:, 0:1] + ln[:, 0:1]), acc + _nn(w.astype(BF), v_ref[pl.ds(ks, KB), :])

        c_l, acc = block(qi, causal2, jnp.zeros((2 * TQ, 1), F32), jnp.zeros((2 * TQ, 2 * HEAD_DIM), F32))

        def cond(carry):
            return jnp.logical_and(carry[0] <= qi, carry[1] > 0)

        def kbody(carry):
            it, _, c_l, acc = carry
            c_l, acc = block(qi - it, None, c_l, acc)
            return it + 1, _alive(c_l), c_l, acc

        acc = lax.while_loop(cond, kbody, (jnp.int32(1), _alive(c_l), c_l, acc))[3]
        o = jnp.where(hm0, acc[:TQ], acc[TQ:])
        o_ref[...] = o
        on_ref[...] = _headnorm_pair(o, g_ref[...], hm0).astype(BF)

    w = 2 * HEAD_DIM
    full = lambda i: pl.BlockSpec((None, None, seq, w), lambda b, hp, q: (i, b, 0, hp))
    qblk = pl.BlockSpec((None, None, TQ, w), lambda b, hp, q: (0, b, q, hp))
    oblk = pl.BlockSpec((None, TQ, w), lambda b, hp, q: (b, q, hp))
    return _call(
        body, "sb_fwd", (nb, N_HEADS // 2, nq),
        [qblk, full(1), full(2), pl.BlockSpec((1, w), lambda b, hp, q: (0, hp))],
        [oblk, oblk],
        [_sds((nb, seq, D_GRP), F32), _sds((nb, seq, D_GRP), BF)],
        (qkv6, qkv6, qkv6, g_sb), carry=carry)


def _sb_bwd(qkv6, do, nb, seq, carry=None):
    nq = seq // TQ
    nk = seq // KB

    def body(q_ref, k_ref, v_ref, do_ref, out_ref, dk_acc, dv_acc, g_st, s_st):
        qi = pl.program_id(2)
        hm0, rel, kr, kc = _sb_masks()
        upper = (kr > kc).astype(BF)
        lower = (kr < kc).astype(BF)

        @pl.when(qi == 0)
        def _():
            dk_acc[...] = jnp.zeros_like(dk_acc)
            dv_acc[...] = jnp.zeros_like(dv_acc)

        qs = _stack_pair(q_ref[...], hm0)
        dos = _stack_pair(do_ref[...], hm0).astype(BF)
        causal2 = jnp.concatenate([rel, rel], axis=0) > 0

        def weights(kj, causal, c_l):
            ks = pl.multiple_of(kj * KB, KB)
            vb = v_ref[pl.ds(ks, KB), :]
            z = _nt(qs, k_ref[pl.ds(ks, KB), :]) * SCALE
            sp = _softplus(z)
            ln = -sp if causal is None else jnp.where(causal, -sp, 0.0)
            suf = _nn2(ln, upper)
            lsz = z - sp
            w = jnp.exp(lsz + (suf + c_l))
            if causal is not None:
                w = jnp.where(causal, w, 0.0)
            g_st[kj] = w * _nt(dos, vb)
            s_st[kj] = jnp.exp(lsz)
            dv_acc[pl.ds(ks, KB), :] += _tn(w.astype(BF), dos)
            return c_l + (suf[:, 0:1] + ln[:, 0:1])

        zc = jnp.zeros((2 * TQ, 1), F32)
        c_l = weights(qi, causal2, zc)

        def acond(carry):
            return jnp.logical_and(carry[0] <= qi, carry[1] > 0)

        def abody(carry):
            c_l = weights(qi - carry[0], None, carry[2])
            return carry[0] + 1, _alive(c_l), c_l

        n_used = lax.while_loop(acond, abody, (jnp.int32(1), _alive(c_l), c_l))[0]

        def grads(kj, causal, c_g, dq):
            ks = pl.multiple_of(kj * KB, KB)
            kb = k_ref[pl.ds(ks, KB), :]
            g = g_st[kj]
            sig = s_st[kj]
            pre = _nn(g.astype(BF), lower)
            dz = g * (1.0 - sig) - sig * (pre + c_g)
            if causal is not None:
                dz = jnp.where(causal, dz, 0.0)
            dzb = (dz * SCALE).astype(BF)
            dk_acc[pl.ds(ks, KB), :] += _tn(dzb, qs)
            return c_g + (pre[:, KB - 1:KB] + g[:, KB - 1:KB]), dq + _nn(dzb, kb)

        c_g, dq = lax.fori_loop(qi - n_used + 1, qi, lambda kj, cr: grads(kj, None, *cr),
                                (zc, jnp.zeros((2 * TQ, 2 * HEAD_DIM), F32)))
        _, dq = grads(qi, causal2, c_g, dq)
        dq = jnp.where(hm0, dq[:TQ], dq[TQ:])
        out_ref[0, pl.ds(pl.multiple_of(qi * TQ, TQ), TQ), :] = dq.astype(BF)

        @pl.when(qi == nq - 1)
        def _():
            out_ref[1] = dk_acc[...].astype(BF)
            out_ref[2] = dv_acc[...].astype(BF)

    w = 2 * HEAD_DIM
    full = lambda i: pl.BlockSpec((None, None, seq, w), lambda b, hp, q: (i, b, 0, hp))
    qblk = pl.BlockSpec((None, None, TQ, w), lambda b, hp, q: (0, b, q, hp))
    oblk = pl.BlockSpec((None, TQ, w), lambda b, hp, q: (b, q, hp))
    return _call(
        body, "sb_bwd", (nb, N_HEADS // 2, nq),
        [qblk, full(1), full(2), oblk],
        [pl.BlockSpec((3, None, seq, w), lambda b, hp, q: (0, b, 0, hp))],
        [_sds((6, nb, seq, D_GRP), BF)], (qkv6, qkv6, qkv6, do),
        scratch=[pltpu.VMEM((seq, w), F32), pltpu.VMEM((seq, w), F32),
                 pltpu.VMEM((nk, 2 * TQ, KB), F32), pltpu.VMEM((nk, 2 * TQ, KB), F32)],
        carry=carry)


def _t5_bucket(n):
    max_exact = N_BUCKETS // 2
    nf = np.maximum(n, 1).astype(np.float32)
    large = max_exact + (np.log(nf / max_exact) / math.log(MAX_DISTANCE / max_exact)
                         * (N_BUCKETS - max_exact)).astype(np.int32)
    large = np.minimum(large, N_BUCKETS - 1)
    return np.where(n < max_exact, n, large).astype(np.int32)


def _bucket_map(dilation):
    step = BLOCK + np.arange(BLOCK)[:, None] - np.arange(2 * BLOCK)[None, :]
    return _t5_bucket(np.clip(step, 0, N_STEPS) * dilation)


GRP_HEADS = 4
GRP_W = GRP_HEADS * HEAD_DIM


def _dil_masks():
    lane = lax.broadcasted_iota(jnp.int32, (1, GRP_W), 1)
    heads = [jnp.logical_and(lane >= HEAD_DIM * i, lane < HEAD_DIM * (i + 1)) for i in range(GRP_HEADS)]
    iq = jnp.bitwise_and(lax.broadcasted_iota(jnp.int32, (GRP_HEADS * BLOCK, BLOCK), 0), BLOCK - 1)
    ik = lax.broadcasted_iota(jnp.int32, (GRP_HEADS * BLOCK, BLOCK), 1)
    return heads, ik <= iq, ik >= iq


def _stack_heads(x, heads):
    zero = jnp.zeros_like(x)
    return jnp.concatenate([jnp.where(hm, x, zero) for hm in heads], axis=0)


def _unstack_heads(xs, heads):
    out = xs[0:BLOCK]
    for i in range(1, GRP_HEADS):
        out = jnp.where(heads[i], xs[i * BLOCK:(i + 1) * BLOCK], out)
    return out


def _dil_rows(n):
    rs = pl.multiple_of(n * BLOCK, BLOCK)
    ps = pl.multiple_of(jnp.maximum(n - 1, 0) * BLOCK, BLOCK)
    return pl.ds(rs, BLOCK), pl.ds(ps, BLOCK)


def _dil_probs(qs, kc, kp, b_ref, gi, valid_c, valid_p):
    rows = slice(gi * GRP_HEADS * BLOCK, (gi + 1) * GRP_HEADS * BLOCK)
    zc = _nt(qs, kc) * SCALE + b_ref[rows, BLOCK:2 * BLOCK]
    zp = _nt(qs, kp) * SCALE + b_ref[rows, 0:BLOCK]
    zc = jnp.where(valid_c, zc, NEG_INF)
    zp = jnp.where(valid_p, zp, NEG_INF)
    m = jnp.maximum(jnp.max(zc, axis=-1, keepdims=True), jnp.max(zp, axis=-1, keepdims=True))
    ec = jnp.exp(zc - m)
    ep = jnp.exp(zp - m)
    den = jnp.sum(ec, axis=-1, keepdims=True) + jnp.sum(ep, axis=-1, keepdims=True)
    return ec, ep, den, m


def _dil_fwd(qkv6r, base, bias, nb, sub_len, dilation):
    n_blk = sub_len // BLOCK

    def body(q_ref, k_ref, v_ref, b_ref, o_ref, l_ref):
        heads, valid_c, valid_p0 = _dil_masks()

        def nbody(n, carry):
            cur, prev = _dil_rows(n)
            valid_p = jnp.logical_and(valid_p0, n > 0)
            for gi in range(N_HEADS // GRP_HEADS):
                lanes = slice(gi * GRP_W, (gi + 1) * GRP_W)
                qs = _stack_heads(q_ref[cur, lanes], heads)
                ec, ep, den, m = _dil_probs(qs, k_ref[cur, lanes], k_ref[prev, lanes], b_ref, gi, valid_c, valid_p)
                o = (_nn(ec.astype(BF), v_ref[cur, lanes]) + _nn(ep.astype(BF), v_ref[prev, lanes])) / den
                o_ref[cur, lanes] = _unstack_heads(o, heads)
                l_ref[cur, lanes] = _unstack_heads(jnp.broadcast_to(m + jnp.log(den), o.shape), heads)
            return carry

        lax.fori_loop(0, n_blk, nbody, 0)

    seqblk = lambda i: pl.BlockSpec((None, None, sub_len, D_GRP), lambda b, r: (i, b, 0, r))
    oblk = pl.BlockSpec((None, sub_len, D_GRP), lambda b, r: (b, 0, r))
    shp = _sds((nb, sub_len, dilation * D_GRP), F32)
    return pl.pallas_call(
        body, name="dil_fwd_%d" % dilation, grid=(nb, dilation),
        in_specs=[seqblk(base), seqblk(base + 1), seqblk(base + 2), _whole(bias)],
        out_specs=[oblk, oblk], out_shape=[shp, shp],
        compiler_params=_cp(2))(qkv6r, qkv6r, qkv6r, bias)


def _dil_bwd(qkv6r, base, bias, do_c, dd_c, nb, sub_len, dilation, carry=None):
    n_blk = sub_len // BLOCK

    def body(q_ref, k_ref, v_ref, b_ref, do_ref, dd_ref, out_ref, a_ref):
        heads, valid_c, valid_p0 = _dil_masks()
        first = jnp.logical_and(pl.program_id(0) == 0, pl.program_id(1) == 0)

        @pl.when(first)
        def _():
            a_ref[...] = jnp.zeros_like(a_ref)

        out_ref[1] = jnp.zeros((sub_len, D_GRP), F32)
        out_ref[2] = jnp.zeros((sub_len, D_GRP), F32)

        def nbody(n, carry):
            cur, prev = _dil_rows(n)
            valid_p = jnp.logical_and(valid_p0, n > 0)
            for gi in range(N_HEADS // GRP_HEADS):
                lanes = slice(gi * GRP_W, (gi + 1) * GRP_W)
                kc, kp = k_ref[cur, lanes], k_ref[prev, lanes]
                vc, vp = v_ref[cur, lanes], v_ref[prev, lanes]
                qs = _stack_heads(q_ref[cur, lanes], heads)
                dos = _stack_heads(do_ref[cur, lanes], heads).astype(BF)
                dds = jnp.sum(_stack_heads(dd_ref[cur, lanes], heads), axis=-1, keepdims=True) * (1.0 / HEAD_DIM)
                ec, ep, den, _ = _dil_probs(qs, kc, kp, b_ref, gi, valid_c, valid_p)
                inv = 1.0 / den
                pc = ec * inv
                pp = ep * inv
                dzc = pc * (_nt(dos, vc) + dds)
                dzp = pp * (_nt(dos, vp) + dds)
                rows = slice(gi * GRP_HEADS * BLOCK, (gi + 1) * GRP_HEADS * BLOCK)
                a_ref[rows, BLOCK:2 * BLOCK] += dzc
                a_ref[rows, 0:BLOCK] += dzp
                dzcb = (dzc * SCALE).astype(BF)
                dzpb = (dzp * SCALE).astype(BF)
                out_ref[0, cur, lanes] = _unstack_heads(_nn(dzcb, kc) + _nn(dzpb, kp), heads)
                out_ref[1, cur, lanes] += _tn(dzcb, qs)
                out_ref[1, prev, lanes] += _tn(dzpb, qs)
                out_ref[2, cur, lanes] += _tn(pc.astype(BF), dos)
                out_ref[2, prev, lanes] += _tn(pp.astype(BF), dos)
            return carry

        lax.fori_loop(0, n_blk, nbody, 0)

    seqblk = lambda i: pl.BlockSpec((None, None, sub_len, D_GRP), lambda b, r: (i, b, 0, r))
    oblk = pl.BlockSpec((None, sub_len, D_GRP), lambda b, r: (b, 0, r))
    return _call(
        body, "dil_bwd_%d" % dilation, (nb, dilation),
        [seqblk(base), seqblk(base + 1), seqblk(base + 2), _whole(bias), oblk, oblk],
        [pl.BlockSpec((3, None, sub_len, D_GRP), lambda b, r: (0, b, 0, r)),
         pl.BlockSpec((N_HEADS * BLOCK, 2 * BLOCK), lambda b, r: (0, 0))],
        [_sds((3, nb, sub_len, dilation * D_GRP), F32), _sds((N_HEADS * BLOCK, 2 * BLOCK), F32)],
        (qkv6r, qkv6r, qkv6r, bias, do_c, dd_c), carry=carry)


def _group_ones():
    idx = np.arange(D_GRP) // HEAD_DIM
    return jnp.asarray((idx[:, None] == idx[None, :]).astype(np.float32), dtype=BF)


def _dil_alphas(l1, l4, l16):
    mx = jnp.maximum(jnp.maximum(l1, l4), l16)
    e1 = jnp.exp(l1 - mx)
    e4 = jnp.exp(l4 - mx)
    e16 = jnp.exp(l16 - mx)
    den = e1 + e4 + e16
    return e1 / den, e4 / den, e16 / den


def _residue_spec(dil):
    return pl.BlockSpec((TM // dil, dil * D_GRP), lambda m: (m, 0))


def _from_residue(src, dil, cg, buf):
    if dil == 1:
        return src[:, cg * 128:(cg + 1) * 128]
    for r in range(dil):
        buf[pl.ds(r, TM // dil, stride=dil), :] = src[:, r * D_GRP + cg * 128:r * D_GRP + (cg + 1) * 128]
    return buf[...]


def _to_residue(dst, dil, cg, buf, val):
    if dil == 1:
        dst[:, cg * 128:(cg + 1) * 128] = val
        return
    buf[...] = val
    for r in range(dil):
        dst[:, r * D_GRP + cg * 128:r * D_GRP + (cg + 1) * 128] = buf[pl.ds(r, TM // dil, stride=dil), :]


def _pair_sum(x, hm0):
    s0 = jnp.sum(jnp.where(hm0, x, 0.0), axis=-1, keepdims=True)
    s1 = jnp.sum(jnp.where(hm0, 0.0, x), axis=-1, keepdims=True)
    return jnp.where(hm0, s0, s1)


def _dil_comb(os, ls, g_dil):
    t = os[0].shape[0]
    dils = [dil for _, dil in DIL_CONFIGS]

    def body(o1, l1, o4, l4, o16, l16, g_ref, o_ref, on_ref, b0, b1, b2, b3):
        hm0 = lax.broadcasted_iota(jnp.int32, (1, 128), 1) < HEAD_DIM
        for cg in range(D_GRP // 128):
            lanes = slice(cg * 128, (cg + 1) * 128)
            ov = [_from_residue(src, dil, cg, buf) for src, dil, buf in zip((o1, o4, o16), dils, (None, b0, b1))]
            lv = [_from_residue(src, dil, cg, buf) for src, dil, buf in zip((l1, l4, l16), dils, (None, b2, b3))]
            a1, a4, a16 = _dil_alphas(*lv)
            o = a1 * ov[0] + a4 * ov[1] + a16 * ov[2]
            o_ref[:, lanes] = o
            on_ref[:, lanes] = _headnorm_pair(o, g_ref[:, lanes], hm0).astype(BF)

    blk = pl.BlockSpec((TM, D_GRP), lambda m: (m, 0))
    specs = [_residue_spec(dil) for dil in dils for _ in range(2)]
    return pl.pallas_call(
        body, name="dil_comb", grid=(t // TM,),
        in_specs=specs + [pl.BlockSpec((1, D_GRP), lambda m: (0, 0))],
        out_specs=[blk, blk],
        out_shape=[_sds((t, D_GRP), F32), _sds((t, D_GRP), BF)],
        scratch_shapes=[pltpu.VMEM((TM, 128), F32)] * 4,
        compiler_params=_cp(1))(os[0], ls[0], os[1], ls[1], os[2], ls[2], g_dil)


def _dil_comb_bwd(do, os, ls):
    t = do.shape[0]
    dils = [dil for _, dil in DIL_CONFIGS]

    def body(do_ref, o1, l1, o4, l4, o16, l16, d1, d4, d16, e1, e4, e16, b0, b1, b2, b3):
        hm0 = lax.broadcasted_iota(jnp.int32, (1, 128), 1) < HEAD_DIM
        for cg in range(D_GRP // 128):
            dov = do_ref[:, cg * 128:(cg + 1) * 128]
            ov = [_from_residue(src, dil, cg, buf) for src, dil, buf in zip((o1, o4, o16), dils, (None, b0, b1))]
            lv = [_from_residue(src, dil, cg, buf) for src, dil, buf in zip((l1, l4, l16), dils, (None, b2, b3))]
            al = _dil_alphas(*lv)
            sbar = al[0] * _pair_sum(dov * ov[0], hm0)
            for a_c, o_c in zip(al[1:], ov[1:]):
                sbar = sbar + a_c * _pair_sum(dov * o_c, hm0)
            for a_c, dil, dref, eref in zip(al, dils, (d1, d4, d16), (e1, e4, e16)):
                _to_residue(dref, dil, cg, b0, a_c * dov)
                _to_residue(eref, dil, cg, b1, -a_c * sbar)

    specs = [_residue_spec(dil) for dil in dils]
    return pl.pallas_call(
        body, name="dil_comb_bwd", grid=(t // TM,),
        in_specs=[pl.BlockSpec((TM, D_GRP), lambda m: (m, 0))] + [sp for sp in specs for _ in range(2)],
        out_specs=specs + specs,
        out_shape=[_sds((t // dil, dil * D_GRP), F32) for dil in dils] * 2,
        scratch_shapes=[pltpu.VMEM((TM, 128), F32)] * 4,
        compiler_params=_cp(1))(do, os[0], ls[0], os[1], ls[1], os[2], ls[2])


def _dqkv_dil_sum(ds, dqkv6):
    t = dqkv6.shape[1]
    dils = [dil for _, dil in DIL_CONFIGS]

    def body(*refs):
        srcs, o_ref, acc = refs[:len(dils)], refs[len(dils) + 1], refs[len(dils) + 2]
        for a in range(3):
            for cg in range(D_GRP // 128):
                for src, dil in zip(srcs, dils):
                    for r in range(dil):
                        part = src[a, :, r * D_GRP + cg * 128:r * D_GRP + (cg + 1) * 128]
                        rows = pl.ds(r, TM // dil, stride=dil) if dil > 1 else slice(None)
                        if dil == dils[0]:
                            acc[rows, :] = part
                        else:
                            acc[rows, :] += part
                o_ref[a, :, cg * 128:(cg + 1) * 128] = acc[...].astype(BF)

    return pl.pallas_call(
        body, name="dqkv_dil_sum", grid=(t // TM,),
        in_specs=[pl.BlockSpec((3, TM // dil, dil * D_GRP), lambda m: (0, m, 0)) for dil in dils]
        + [pl.BlockSpec(memory_space=pl.ANY)],
        out_specs=pl.BlockSpec((3, TM, D_GRP), lambda m: (1, m, 0)),
        out_shape=_sds((6, t, D_GRP), BF), input_output_aliases={len(dils): 0},
        scratch_shapes=[pltpu.VMEM((TM, 128), F32)],
        compiler_params=_cp(1))(*ds, dqkv6)


def _relbias_grad(a_all, onehot):
    def body(a_ref, oh_ref, o_ref):
        acc = jnp.zeros((N_HEADS, N_BUCKETS), F32)
        for c in range(len(DIL_CONFIGS)):
            av = a_ref[c]
            hi = av.astype(BF)
            lo = (av - hi.astype(F32)).astype(BF)
            acc = acc + _nt(hi, oh_ref[c]) + _nt(lo, oh_ref[c])
        o_ref[...] = acc

    return pl.pallas_call(body, name="relbias_grad", out_shape=_sds((N_HEADS, N_BUCKETS), F32),
                          compiler_params=_cp())(a_all, onehot)


def _headnorm_bwd(dn, o, gv, mv):
    ms = _nn2(o * o, mv) * (1.0 / HEAD_DIM)
    r = lax.rsqrt(ms + EPS)
    nrm = o * r
    dg = jnp.sum(dn * nrm, axis=0, keepdims=True)
    dnn = dn * gv
    do = r * (dnn - nrm * (_nn2(dnn * nrm, mv) * (1.0 / HEAD_DIM)))
    return do, dg


def _mix_bwd_out(dx, gt, tv, w_out, o_sb, o_dil, on_sb, on_dil, g_sb, g_dil, ones_g, seq, carry=None):
    t, d = dx.shape
    per = seq // TM
    nb = t // seq

    def body(dx_ref, gt_ref, t_ref, w_ref, osb, odl, onsb, ondl, gsb, gdl, m_ref,
             dosb, dodl, dgt_ref, dgsb, dgdl, dw_ref):
        m = pl.program_id(0)
        dxv = dx_ref[...]
        dt = (gt_ref[...] * dxv).astype(BF)
        _acc_rows(dgt_ref, jnp.sum(dxv * t_ref[...], axis=0, keepdims=True), m % per == 0)
        mv = m_ref[...]
        don_sb = _nt(dt, w_ref[0:D_GRP, :])
        don_dl = _nt(dt, w_ref[D_GRP:2 * D_GRP, :])
        do1, dg1 = _headnorm_bwd(don_sb, osb[...], gsb[...], mv)
        do2, dg2 = _headnorm_bwd(don_dl, odl[...], gdl[...], mv)
        dosb[...] = do1
        dodl[...] = do2
        _acc_rows(dgsb, dg1, m == 0)
        _acc_rows(dgdl, dg2, m == 0)
        p1 = _tn(onsb[...], dt)
        p2 = _tn(ondl[...], dt)

        @pl.when(m == 0)
        def _():
            dw_ref[0:D_GRP, :] = p1
            dw_ref[D_GRP:2 * D_GRP, :] = p2

        @pl.when(m != 0)
        def _():
            dw_ref[0:D_GRP, :] += p1
            dw_ref[D_GRP:2 * D_GRP, :] += p2

    row = pl.BlockSpec((TM, d), lambda m: (m, 0))
    half = pl.BlockSpec((TM, D_GRP), lambda m: (m, 0))
    ex = pl.BlockSpec((None, 1, d), lambda m: (m // per, 0, 0))
    gvec = pl.BlockSpec((1, D_GRP), lambda m: (0, 0))
    wblk = pl.BlockSpec((2 * D_GRP, d), lambda m: (0, 0))
    return _call(
        body, "mix_bwd_out", (t // TM,),
        [row, ex, row, wblk, half, half, half, half, gvec, gvec, pl.BlockSpec((D_GRP, D_GRP), lambda m: (0, 0))],
        [half, half, ex, gvec, gvec, wblk],
        [_sds((t, D_GRP), F32), _sds((t, D_GRP), F32), _sds((nb, 1, d), F32),
         _sds((1, D_GRP), F32), _sds((1, D_GRP), F32), _sds((2 * D_GRP, d), F32)],
        (dx, gt, tv, w_out, o_sb, o_dil, on_sb, on_dil, g_sb, g_dil, ones_g), carry=carry)


def _dw_in(h, dqkv6, carry=None):
    t, d = h.shape
    wc = 6 * D_GRP // N_CHIPS

    def body(h_ref, g_ref, o_ref):
        kt = pl.program_id(0)
        hv = h_ref[...]
        for j in range(N_CHIPS):
            p = _tn(hv, _chip_cols(g_ref, j, wc))

            @pl.when(kt == 0)
            def _(p=p, j=j):
                o_ref[j, 0] = p

            @pl.when(kt != 0)
            def _(p=p, j=j):
                o_ref[j, 0] += p

    return _call(
        body, "dw_in", (t // TM,),
        [pl.BlockSpec((TM, d), lambda kt: (kt, 0)), pl.BlockSpec((6, TM, D_GRP), lambda kt: (0, kt, 0))],
        [pl.BlockSpec((N_CHIPS, 1, d, wc), lambda kt: (0, 0, 0, 0))],
        [_sds((N_CHIPS, 1, d, wc), F32)], (h, dqkv6), carry=carry)


def _mix_bwd_dh(dqkv6, w_in, x, g, sc, dxo, seq, carry=None):
    _, t, _ = dqkv6.shape
    d = x.shape[-1]
    wc = w_in.shape[-1]
    per = seq // TM
    nb = t // seq

    def body(g6_ref, w_ref, x_ref, g_ref, sc_ref, dxo_ref, dx_ref, dsh_ref, dsc_ref, dg_ref):
        m = pl.program_id(0)
        dh = _nt(_chip_cols(g6_ref, 0, wc), w_ref[0, 0])
        for j in range(1, N_CHIPS):
            dh = dh + _nt(_chip_cols(g6_ref, j, wc), w_ref[j, 0])
        dx, dsh, dsc, dg = _modnorm_bwd_tile(dh, x_ref[...], g_ref[...], sc_ref[...], dxo_ref[...])
        dx_ref[...] = dx
        _acc_rows(dsh_ref, dsh, m % per == 0)
        _acc_rows(dsc_ref, dsc, m % per == 0)
        _acc_rows(dg_ref, dg, m == 0)

    row = pl.BlockSpec((TM, d), lambda m: (m, 0))
    ex = pl.BlockSpec((None, 1, d), lambda m: (m // per, 0, 0))
    vec = pl.BlockSpec((1, d), lambda m: (0, 0))
    return _call(
        body, "mix_bwd_dh", (t // TM,),
        [pl.BlockSpec((6, TM, D_GRP), lambda m: (0, m, 0)), _whole(w_in), row, vec, ex, row],
        [row, ex, ex, vec],
        [_sds((t, d), F32), _sds((nb, 1, d), F32), _sds((nb, 1, d), F32), _sds((1, d), F32)],
        (dqkv6, w_in, x, g, sc, dxo), carry=carry)


def _ffn_down_loss(s, wd, x, gt, seq, coef, g, target):
    _, t, fs = s.shape
    d = x.shape[-1]
    per = seq // TM
    steps = t // TM

    def body(s_ref, w_ref, x_ref, gt_ref, g_ref, t_ref, f_ref, dx_ref, dg_ref, loss_ref, lacc):
        m = pl.program_id(0)
        f = _nn(s_ref[0], w_ref[0, 0])
        for j in range(1, N_CHIPS):
            f = f + _nn(s_ref[j], w_ref[j, 0])
        f_ref[...] = f
        xv = x_ref[...] + (coef * gt_ref[...]) * f
        gv = g_ref[...]
        r = lax.rsqrt(jnp.mean(xv * xv, axis=-1, keepdims=True) + EPS)
        n = xv * r
        err = n * gv - t_ref[...]
        dy = err * (1.0 / d)
        _acc_rows(dg_ref, jnp.sum(dy * n, axis=0, keepdims=True), m == 0)
        dn = dy * gv
        dx_ref[...] = r * (dn - n * jnp.mean(dn * n, axis=-1, keepdims=True))
        _acc_rows(lacc, jnp.sum(err * err, axis=0, keepdims=True), m == 0)

        @pl.when(m == steps - 1)
        def _():
            tot = jnp.sum(lacc[...], axis=-1, keepdims=True) * (0.5 / d)
            loss_ref[...] = jnp.broadcast_to(tot, (1, 128))

    row = pl.BlockSpec((TM, d), lambda m: (m, 0))
    vec = pl.BlockSpec((1, d), lambda m: (0, 0))
    return pl.pallas_call(
        body, name="ffn_down_loss", grid=(steps,),
        in_specs=[pl.BlockSpec((N_CHIPS, TM, fs), lambda m: (0, m, 0)), _whole(wd), row,
                  pl.BlockSpec((None, 1, d), lambda m: (m // per, 0, 0)), vec, row],
        out_specs=[row, row, vec, pl.BlockSpec((1, 128), lambda m: (0, 0))],
        out_shape=[_sds((t, d), F32), _sds((t, d), F32), _sds((1, d), F32), _sds((1, 128), F32)],
        scratch_shapes=[pltpu.VMEM((1, d), F32)],
        compiler_params=_cp(1))(s, wd, x, gt, g, target)


def _row_tile(rows, cols):
    best = rows
    for tr in range(8, rows + 1, 8):
        if rows % tr == 0 and tr * cols * 4 <= (1 << 20):
            best = tr
    if best * cols * 4 > (1 << 21):
        best = 8
    return best


def _adamw(w, g_arr, g_sel, m, v):
    rows, cols = w.shape
    tr = _row_tile(rows, cols)
    b1c = 1.0 - ADAM_B1 ** ADAM_STEP
    b2c = 1.0 - ADAM_B2 ** ADAM_STEP

    def body(w_ref, g_ref, m_ref, v_ref, go_ref, d_ref, mo_ref, vo_ref):
        gv = g_ref[...]
        mn = ADAM_B1 * m_ref[...] + (1.0 - ADAM_B1) * gv
        vn = ADAM_B2 * v_ref[...] + (1.0 - ADAM_B2) * (gv * gv)
        go_ref[...] = gv
        mo_ref[...] = mn
        vo_ref[...] = vn
        d_ref[...] = -ADAM_LR * ((mn / b1c) / (jnp.sqrt(vn / b2c) + ADAM_EPS) + ADAM_WD * w_ref[...])

    blk = pl.BlockSpec((tr, cols), lambda i: (i, 0))
    shp = _sds((rows, cols), F32)
    return pl.pallas_call(
        body, name="adamw", grid=(rows // tr,),
        in_specs=[blk, pl.BlockSpec((None, tr, cols), lambda i: (g_sel, i, 0)), blk, blk],
        out_specs=[blk] * 4, out_shape=[shp] * 4,
        compiler_params=_cp(1))(w, g_arr, m, v)


def _flip(v, bit):
    return 1 - v if bit else v


def _my_place():
    x, y, c = lax.axis_index("x"), lax.axis_index("y"), lax.axis_index("c")
    return x, y, c


class _Exchange:
    def __init__(self, operands, out_shape, aliases, sems, start, finish):
        self.operands, self.out_shape, self.aliases, self.sems = list(operands), list(out_shape), dict(aliases), list(sems)
        self.start, self.finish = start, finish


def _join(exchanges):
    exchanges = [e for e in exchanges if e is not None]
    if not exchanges:
        return None
    ops, outs, sems, aliases, spans = [], [], [], {}, []
    for e in exchanges:
        spans.append((len(ops), len(outs), len(sems), e))
        for i, j in e.aliases.items():
            aliases[len(ops) + i] = len(outs) + j
        ops += e.operands
        outs += e.out_shape
        sems += e.sems

    def run(which):
        def go(ins, res, sm):
            for io, oo, so, e in spans:
                getattr(e, which)(ins[io:io + len(e.operands)], res[oo:oo + len(e.out_shape)], sm[so:so + len(e.sems)])
        return go

    return _Exchange(ops, outs, aliases, sems, run("start"), run("finish"))


def _call(body, name, grid, in_specs, out_specs, out_shape, args, scratch=(), carry=None):
    in_specs, out_specs, out_shape, scratch = list(in_specs), list(out_specs), list(out_shape), list(scratch)
    if carry is None:
        return pl.pallas_call(body, name=name, grid=grid, in_specs=in_specs, out_specs=out_specs,
                              out_shape=out_shape, scratch_shapes=scratch,
                              compiler_params=_cp(len(grid)))(*args)
    n_in, n_out, n_s = len(in_specs), len(out_specs), len(scratch)
    c_in, c_out = len(carry.operands), len(carry.out_shape)
    any_spec = pl.BlockSpec(memory_space=pl.ANY)

    def wrapped(*refs):
        ins, cins = refs[:n_in], refs[n_in:n_in + c_in]
        o0 = n_in + c_in
        outs, couts = refs[o0:o0 + n_out], refs[o0 + n_out:o0 + n_out + c_out]
        s0 = o0 + n_out + c_out
        scr, sems = refs[s0:s0 + n_s], refs[s0 + n_s:]
        first = pl.program_id(0) == 0
        last = pl.program_id(0) == grid[0] - 1
        for ax in range(1, len(grid)):
            first = jnp.logical_and(first, pl.program_id(ax) == 0)
            last = jnp.logical_and(last, pl.program_id(ax) == grid[ax] - 1)

        @pl.when(first)
        def _():
            carry.start(cins, couts, sems)

        body(*ins, *outs, *scr)

        @pl.when(last)
        def _():
            carry.finish(cins, couts, sems)

    return pl.pallas_call(
        wrapped, name=name, grid=grid, in_specs=in_specs + [any_spec] * c_in,
        out_specs=out_specs + [any_spec] * c_out, out_shape=out_shape + carry.out_shape,
        scratch_shapes=scratch + carry.sems,
        input_output_aliases={n_in + i: n_out + j for i, j in carry.aliases.items()},
        compiler_params=_cp(len(grid)))(*args, *carry.operands)


def _whole_call(body, name, args, out_shape, scratch, carry=None):
    vm = pl.BlockSpec(memory_space=pltpu.VMEM)
    any_spec = pl.BlockSpec(memory_space=pl.ANY)
    out_shape, scratch = list(out_shape), list(scratch)
    n_in, n_out, n_s = len(args), len(out_shape), len(scratch)
    if carry is None:
        return pl.pallas_call(body, name=name, in_specs=[vm] * n_in, out_specs=[vm] * n_out, out_shape=out_shape,
                              scratch_shapes=scratch, compiler_params=_cp())(*args)
    c_in, c_out = len(carry.operands), len(carry.out_shape)

    def wrapped(*refs):
        ins, cins = refs[:n_in], refs[n_in:n_in + c_in]
        o0 = n_in + c_in
        outs, couts = refs[o0:o0 + n_out], refs[o0 + n_out:o0 + n_out + c_out]
        s0 = o0 + n_out + c_out
        scr, sems = refs[s0:s0 + n_s], refs[s0 + n_s:]
        carry.start(cins, couts, sems)
        body(*ins, *outs, *scr)
        carry.finish(cins, couts, sems)

    return pl.pallas_call(
        wrapped, name=name, in_specs=[vm] * n_in + [any_spec] * c_in, out_specs=[vm] * n_out + [any_spec] * c_out,
        out_shape=out_shape + carry.out_shape, scratch_shapes=scratch + carry.sems,
        input_output_aliases={n_in + i: n_out + j for i, j in carry.aliases.items()},
        compiler_params=_cp())(*args, *carry.operands)


def _alone(name, ex):
    any_spec = pl.BlockSpec(memory_space=pl.ANY)
    c_in, c_out = len(ex.operands), len(ex.out_shape)

    def body(*refs):
        ins, outs, sems = refs[:c_in], refs[c_in:c_in + c_out], refs[c_in + c_out:]
        ex.start(ins, outs, sems)
        ex.finish(ins, outs, sems)

    return pl.pallas_call(
        body, name=name, in_specs=[any_spec] * c_in, out_specs=[any_spec] * c_out, out_shape=ex.out_shape,
        scratch_shapes=ex.sems, input_output_aliases=ex.aliases, compiler_params=_cp())(*ex.operands)


def _ada_fwd(c_pad, w_ada, b_shard, carry=None):
    d = c_pad.shape[-1]
    cols = w_ada.shape[-1]
    chunk = 384

    def body(c_ref, w_ref, b_ref, call_ref, mod_ref, part, s1, r1, s2, r2):
        x, y, c = _my_place()
        dev = 4 * x + 2 * y + c
        chip = 2 * x + y
        call_ref[dev] = c_ref[...]

        def c_copy(k):
            px, py, pc = _flip(x, (k >> 2) & 1), _flip(y, (k >> 1) & 1), _flip(c, k & 1)
            return px, py, pc

        sends = []
        for k in range(1, N_DEV):
            px, py, pc = c_copy(k)
            cp = pltpu.make_async_remote_copy(src_ref=c_ref, dst_ref=call_ref.at[dev], send_sem=s1.at[k - 1],
                                              recv_sem=r1.at[k - 1], device_id=(px, py, pc), device_id_type=MESH)
            cp.start()
            sends.append(cp)
        for k in range(1, N_DEV):
            px, py, pc = c_copy(k)
            pltpu.make_async_remote_copy(src_ref=c_ref, dst_ref=call_ref.at[4 * px + 2 * py + pc],
                                         send_sem=s1.at[k - 1], recv_sem=r1.at[k - 1],
                                         device_id=(px, py, pc), device_id_type=MESH).wait_recv()
        for cp in sends:
            cp.wait_send()

        cs = call_ref[...].reshape(N_DEV * 8, d)
        sc = (cs * jax.nn.sigmoid(cs)).astype(BF)
        for n0 in range(0, cols, chunk):
            blk = _nn(sc, w_ref[:, n0:n0 + chunk].astype(BF)) + b_ref[:, n0:n0 + chunk]
            part[:, :, n0:n0 + chunk] = blk.reshape(N_DEV, 8, chunk)

        mod_ref[chip] = part[dev]
        sends = []
        for kk in range(1, N_CHIPS):
            px, py = _flip(x, (kk >> 1) & 1), _flip(y, kk & 1)
            cp = pltpu.make_async_remote_copy(src_ref=part.at[4 * px + 2 * py + c], dst_ref=mod_ref.at[chip],
                                              send_sem=s2.at[kk - 1], recv_sem=r2.at[kk - 1],
                                              device_id=(px, py, c), device_id_type=MESH)
            cp.start()
            sends.append(cp)
        for kk in range(1, N_CHIPS):
            px, py = _flip(x, (kk >> 1) & 1), _flip(y, kk & 1)
            pltpu.make_async_remote_copy(src_ref=part.at[dev], dst_ref=mod_ref.at[2 * px + py],
                                         send_sem=s2.at[kk - 1], recv_sem=r2.at[kk - 1],
                                         device_id=(px, py, c), device_id_type=MESH).wait_recv()
        for cp in sends:
            cp.wait_send()

    return _whole_call(
        body, "ada_fwd", (c_pad, w_ada, b_shard),
        [_sds((N_DEV, 8, d), F32), _sds((N_CHIPS, 8, cols), F32)],
        [pltpu.VMEM((N_DEV, 8, cols), F32),
         pltpu.SemaphoreType.DMA((N_DEV - 1,)), pltpu.SemaphoreType.DMA((N_DEV - 1,)),
         pltpu.SemaphoreType.DMA((N_CHIPS - 1,)), pltpu.SemaphoreType.DMA((N_CHIPS - 1,))], carry=carry)


def _ag_weights(bufs):
    n = len(bufs)

    def place():
        x, y, c = _my_place()
        others = [(_flip(x, (kk >> 1) & 1), _flip(y, kk & 1)) for kk in range(1, N_CHIPS)]
        return x, y, c, 2 * x + y, others

    def half(b, which):
        hr = bufs[b].shape[2] // 2
        return pl.ds(pl.multiple_of(which * hr, 16), hr)

    def ici(outs, sems, b, i, slot, x, y, c, px, py):
        rows = outs[b].at[slot, :, half(b, c), :]
        return pltpu.make_async_remote_copy(
            src_ref=rows, dst_ref=rows, send_sem=sems[0].at[3 * b + i], recv_sem=sems[1].at[3 * b + i],
            device_id=(px, py, c), device_id_type=MESH)

    def d2d(outs, sems, b, i, slot, x, y, c, which):
        rows = outs[b].at[slot, :, half(b, which), :]
        return pltpu.make_async_remote_copy(
            src_ref=rows, dst_ref=rows, send_sem=sems[2].at[3 * b + i], recv_sem=sems[3].at[3 * b + i],
            device_id=(x, y, 1 - c), device_id_type=MESH)

    def start(ins, outs, sems):
        x, y, c, chip, others = place()
        for b in range(n):
            for i, (px, py) in enumerate(others):
                ici(outs, sems, b, i, chip, x, y, c, px, py).start()

    def finish(ins, outs, sems):
        x, y, c, chip, others = place()
        for b in range(n):
            for i, (px, py) in enumerate(others):
                ici(outs, sems, b, i, 2 * px + py, x, y, c, px, py).wait_recv()
                d2d(outs, sems, b, i, 2 * px + py, x, y, c, c).start()
        for b in range(n):
            for i, (px, py) in enumerate(others):
                d2d(outs, sems, b, i, 2 * px + py, x, y, c, 1 - c).wait_recv()
        for b in range(n):
            for i, (px, py) in enumerate(others):
                ici(outs, sems, b, i, chip, x, y, c, px, py).wait_send()
                d2d(outs, sems, b, i, 2 * px + py, x, y, c, c).wait_send()

    return _Exchange(bufs, [_sds(s.shape, s.dtype) for s in bufs], {i: i for i in range(n)},
                     [pltpu.SemaphoreType.DMA((3 * n,))] * 4, start, finish)


def _rs_d2d(grads):
    n = len(grads)

    def copy(ins, outs, sems, b):
        x, y, c = _my_place()
        hr = grads[b].shape[2] // 2
        theirs = pl.ds(pl.multiple_of((1 - c) * hr, 8), hr)
        return pltpu.make_async_remote_copy(
            src_ref=ins[b].at[:, :, theirs, :], dst_ref=outs[b], send_sem=sems[0].at[b], recv_sem=sems[1].at[b],
            device_id=(x, y, 1 - c), device_id_type=MESH)

    def start(ins, outs, sems):
        for b in range(n):
            copy(ins, outs, sems, b).start()

    def finish(ins, outs, sems):
        for b in range(n):
            copy(ins, outs, sems, b).wait()

    return _Exchange(grads, [_sds(g.shape[:2] + (g.shape[2] // 2, g.shape[3]), F32) for g in grads], {},
                     [pltpu.SemaphoreType.DMA((n,))] * 2, start, finish)


def _add_halves(core, g, land):
    nchip, ng, rows, cols = g.shape
    hr = rows // 2
    tr = _row_tile(hr, cols)
    steps = hr // tr

    def body(core_ref, g_ref, l_ref, o_ref):
        del core_ref
        o_ref[...] = (g_ref[...] + l_ref[...]).astype(BF)

    return pl.pallas_call(
        body, name="add_halves",
        grid_spec=pltpu.PrefetchScalarGridSpec(
            num_scalar_prefetch=1, grid=(nchip, ng, steps),
            in_specs=[pl.BlockSpec((None, None, tr, cols), lambda j, a, i, cr: (j, a, cr[0] * steps + i, 0)),
                      pl.BlockSpec((None, None, tr, cols), lambda j, a, i, cr: (j, a, i, 0))],
            out_specs=pl.BlockSpec((None, None, tr, cols), lambda j, a, i, cr: (j, a, i, 0))),
        out_shape=_sds((nchip, ng, hr, cols), BF),
        compiler_params=_cp(3))(core, g, land)


def _rs_ici(parts):
    n = len(parts)

    def copies(ins, outs, sems):
        x, y, c = _my_place()
        chip = 2 * x + y
        for b in range(n):
            for kk in range(1, N_CHIPS):
                px, py = _flip(x, (kk >> 1) & 1), _flip(y, kk & 1)
                k = 3 * b + kk - 1
                send = pltpu.make_async_remote_copy(
                    src_ref=ins[b].at[2 * px + py], dst_ref=outs[b].at[chip],
                    send_sem=sems[0].at[k], recv_sem=sems[1].at[k], device_id=(px, py, c), device_id_type=MESH)
                slot = outs[b].at[2 * px + py]
                recv = pltpu.make_async_remote_copy(
                    src_ref=slot, dst_ref=slot, send_sem=sems[0].at[k], recv_sem=sems[1].at[k],
                    device_id=(px, py, c), device_id_type=MESH)
                yield send, recv

    def start(ins, outs, sems):
        for send, _ in copies(ins, outs, sems):
            send.start()

    def finish(ins, outs, sems):
        for send, recv in copies(ins, outs, sems):
            recv.wait_recv()
            send.wait_send()

    return _Exchange(parts, [_sds(p.shape, p.dtype) for p in parts], {},
                     [pltpu.SemaphoreType.DMA((3 * n,))] * 2, start, finish)


def _sum_chips(place, part, land):
    nchip, ng, hr, cols = land.shape
    tr = _row_tile(hr, cols)
    steps = hr // tr

    def body(place_ref, p_ref, l1, l2, l3, o_ref):
        del place_ref
        o_ref[...] = ((p_ref[...].astype(F32) + l1[...].astype(F32)) + l2[...].astype(F32)) + l3[...].astype(F32)

    def slot(k):
        return pl.BlockSpec((None, None, tr, cols), lambda a, i, pr: (jnp.bitwise_xor(pr[1], k), a, i, 0))

    return pl.pallas_call(
        body, name="sum_chips",
        grid_spec=pltpu.PrefetchScalarGridSpec(
            num_scalar_prefetch=1, grid=(ng, steps),
            in_specs=[slot(0), slot(1), slot(2), slot(3)],
            out_specs=pl.BlockSpec((None, tr, cols), lambda a, i, pr: (a, pr[0] * steps + i, 0))),
        out_shape=_sds((ng, 2 * hr, cols), F32),
        compiler_params=_cp(2))(place, part, land, land, land)


def _rs_final(bufs):
    n = len(bufs)

    def copy(outs, sems, b, which):
        x, y, c = _my_place()
        hr = bufs[b].shape[1] // 2
        rows = outs[b].at[:, pl.ds(pl.multiple_of((c if which == 0 else 1 - c) * hr, 8), hr), :]
        return pltpu.make_async_remote_copy(
            src_ref=rows, dst_ref=rows, send_sem=sems[0].at[b], recv_sem=sems[1].at[b],
            device_id=(x, y, 1 - c), device_id_type=MESH)

    def start(ins, outs, sems):
        for b in range(n):
            copy(outs, sems, b, 0).start()

    def finish(ins, outs, sems):
        for b in range(n):
            copy(outs, sems, b, 0).wait_send()
            copy(outs, sems, b, 1).wait_recv()

    return _Exchange(bufs, [_sds(h.shape, F32) for h in bufs], {i: i for i in range(n)},
                     [pltpu.SemaphoreType.DMA((n,))] * 2, start, finish)


def _small_sync(smalls, dmod_blk, c_all, carry=None):
    d = c_all.shape[-1]
    cols = dmod_blk.shape[-1]
    chunk = 384

    def body(sm_ref, dm_ref, c_ref, sum_ref, gw_ref, sm_all, dm_all, ssem, rsem):
        x, y, c = _my_place()
        dev = 4 * x + 2 * y + c
        chip = 2 * x + y
        sm_all[dev] = sm_ref[...]
        dm_all[dev] = dm_ref[chip]
        sends = []
        for k in range(1, N_DEV):
            px, py, pc = _flip(x, (k >> 2) & 1), _flip(y, (k >> 1) & 1), _flip(c, k & 1)
            a = pltpu.make_async_remote_copy(src_ref=sm_ref, dst_ref=sm_all.at[dev], send_sem=ssem.at[2 * (k - 1)],
                                             recv_sem=rsem.at[2 * (k - 1)], device_id=(px, py, pc),
                                             device_id_type=MESH)
            b = pltpu.make_async_remote_copy(src_ref=dm_ref.at[2 * px + py], dst_ref=dm_all.at[dev],
                                             send_sem=ssem.at[2 * (k - 1) + 1], recv_sem=rsem.at[2 * (k - 1) + 1],
                                             device_id=(px, py, pc), device_id_type=MESH)
            a.start()
            b.start()
            sends += [a, b]
        for k in range(1, N_DEV):
            px, py, pc = _flip(x, (k >> 2) & 1), _flip(y, (k >> 1) & 1), _flip(c, k & 1)
            pdev = 4 * px + 2 * py + pc
            pltpu.make_async_remote_copy(src_ref=sm_ref, dst_ref=sm_all.at[pdev], send_sem=ssem.at[2 * (k - 1)],
                                         recv_sem=rsem.at[2 * (k - 1)], device_id=(px, py, pc),
                                         device_id_type=MESH).wait_recv()
            pltpu.make_async_remote_copy(src_ref=dm_ref.at[chip], dst_ref=dm_all.at[pdev],
                                         send_sem=ssem.at[2 * (k - 1) + 1], recv_sem=rsem.at[2 * (k - 1) + 1],
                                         device_id=(px, py, pc), device_id_type=MESH).wait_recv()
        for cp in sends:
            cp.wait_send()

        tot = sm_all[0]
        for q in range(1, N_DEV):
            tot = tot + sm_all[q]
        sum_ref[...] = tot

        cs = c_ref[...].reshape(N_DEV * 8, d)
        sc = (cs * jax.nn.sigmoid(cs)).astype(BF)
        for n0 in range(0, cols, chunk):
            dmv = dm_all[:, :, n0:n0 + chunk].reshape(N_DEV * 8, chunk).astype(BF)
            gw_ref[:, n0:n0 + chunk] = _tn(sc, dmv)

    return _whole_call(
        body, "small_sync", (smalls, dmod_blk, c_all),
        [_sds(smalls.shape, F32), _sds((d, cols), F32)],
        [pltpu.VMEM((N_DEV,) + smalls.shape, F32), pltpu.VMEM((N_DEV, 8, cols), F32),
         pltpu.SemaphoreType.DMA((2 * (N_DEV - 1),)), pltpu.SemaphoreType.DMA((2 * (N_DEV - 1),))], carry=carry)


def _bucket_onehot():
    maps = np.stack([_bucket_map(dil).reshape(-1) for _, dil in DIL_CONFIGS])
    return (jnp.asarray(maps)[:, None, :] == jnp.arange(N_BUCKETS, dtype=jnp.int32)[None, :, None]).astype(BF)


def _dil_bias(rel_t, onehot):
    def body(r_ref, oh_ref, o_ref):
        rv = r_ref[...]
        hi = rv.astype(BF)
        lo = (rv - hi.astype(F32)).astype(BF)
        for c in range(len(DIL_CONFIGS)):
            o_ref[c] = _nn(hi, oh_ref[c]) + _nn(lo, oh_ref[c])

    return pl.pallas_call(body, name="dil_bias",
                          out_shape=_sds((len(DIL_CONFIGS), N_HEADS, BLOCK * 2 * BLOCK), F32),
                          compiler_params=_cp())(rel_t, onehot)


def _rowsum8(a):
    def body(a_ref, o_ref):
        o_ref[...] = jnp.sum(a_ref[...], axis=0, keepdims=True)

    return pl.pallas_call(body, name="rowsum8", out_shape=_sds((1, a.shape[1]), F32), compiler_params=_cp())(a)


def _local_step(x, mod, target, w, gains, rel_bias, place=None):
    nb, seq, d = x.shape
    t = nb * seq
    dist = place is not None
    core = place[0:1] if dist else None
    x0 = x.reshape(t, d)
    tgt = target.reshape(t, d)
    md = [mod[:, i:i + 1, :] for i in range(N_MOD)]
    sh1, sc1, gt1, sh2, sc2, gt2, sh3, sc3, gt3 = md
    g1, g2, g3 = gains["g_ffn1"], gains["g_mix"], gains["g_ffn2"]
    ones_g = _group_ones()

    def partial_sums(grads, lands):
        return [_add_halves(core, g, l) for g, l in zip(grads, lands)]

    def chip_sums(parts, lands):
        return [_sum_chips(place, p, l) for p, l in zip(parts, lands)]

    res = _ffn_up(x0, g1, sc1, sh1, w["gu1"], seq,
                  carry=_ag_weights([w["d1"], w["win"], w["wout"]]) if dist else None)
    h1, a1, u1, s1 = res[:4]
    wd1, w_in, w_out = res[4:] if dist else (w["d1"], w["win"], w["wout"])
    w_out2 = w_out.reshape(2 * D_GRP, d)
    f1, x1 = _ffn_down(s1, wd1, x0, gt1, seq, 0.5)

    h2, qkv6, qkv_r4, qkv_r16 = _qkv_proj(x1, g2, sc2, sh2, w_in, seq)
    qkv6b = qkv6.reshape(6, nb, seq, D_GRP)
    res = _sb_fwd(qkv6b, gains["g_sb_out"], nb, seq, carry=_ag_weights([w["gu2"], w["d2"]]) if dist else None)
    o_sb, on_sb = res[:2]
    wgu2, wd2 = res[2:] if dist else (w["gu2"], w["d2"])
    onehot = _bucket_onehot()
    bias = _dil_bias(rel_bias.T, onehot).reshape(len(DIL_CONFIGS), N_HEADS * BLOCK, 2 * BLOCK)
    o_cs, l_cs = [], []
    qkv_rs = [(qkv6b, 3), (qkv_r4, 0), (qkv_r16, 0)]
    for ci, (_, dil) in enumerate(DIL_CONFIGS):
        sub = seq // dil
        arr, base = qkv_rs[ci]
        arr = arr.reshape(base + 3, nb, sub, dil * D_GRP)
        qkv_rs[ci] = (arr, base)
        o_c, l_c = _dil_fwd(arr, base, bias[ci], nb, sub, dil)
        o_cs.append(o_c.reshape(t // dil, dil * D_GRP))
        l_cs.append(l_c.reshape(t // dil, dil * D_GRP))
    o_dil, on_dil = _dil_comb(o_cs, l_cs, gains["g_dil_out"])
    tmix, x2 = _mix_out(on_sb.reshape(t, D_GRP), on_dil, w_out2, x1, gt2, seq)

    h3, a3, u3, s3 = _ffn_up(x2, g3, sc3, sh3, wgu2, seq)
    f3, dx3, dg_final, loss = _ffn_down_loss(s3, wd2, x2, gt3, seq, 0.5, gains["g_final"], tgt)

    da3, du3, df3, dgt3, dx2, dsh3, dsc3, dg3 = _ffn_bwd_x(dx3, gt3, f3, wd2, a3, u3, wgu2, x2, g3, sc3, seq, 0.5)
    grads2 = [_ffn_bwd_w(h3, da3, du3, s3, df3)]

    res = _mix_bwd_out(
        dx2, gt2, tmix, w_out2, o_sb.reshape(t, D_GRP), o_dil, on_sb.reshape(t, D_GRP), on_dil,
        gains["g_sb_out"], gains["g_dil_out"], ones_g, seq, carry=_rs_d2d(grads2) if dist else None)
    do_sb, do_dil, dgt2, dg_sb, dg_dil, dw_out = res[:6]
    parts2 = partial_sums(grads2, res[6:]) if dist else None
    dw_out = dw_out.reshape(N_CHIPS, 1, 2 * D_GRP // N_CHIPS, d)
    res = _sb_bwd(qkv6b, do_sb.reshape(nb, seq, D_GRP), nb, seq, carry=_rs_ici(parts2) if dist else None)
    dqkv6 = res[0]
    halves2 = chip_sums(parts2, res[1:]) if dist else None
    dcs = _dil_comb_bwd(do_dil, o_cs, l_cs)
    dsum, a_tiles = [], []
    for ci, (_, dil) in enumerate(DIL_CONFIGS):
        sub = seq // dil
        do_c = dcs[ci].reshape(nb, sub, dil * D_GRP)
        dd_c = dcs[3 + ci].reshape(nb, sub, dil * D_GRP)
        res = _dil_bwd(qkv_rs[ci][0], qkv_rs[ci][1], bias[ci], do_c, dd_c, nb, sub, dil,
                       carry=_rs_final(halves2) if dist and ci == 0 else None)
        if dist and ci == 0:
            grads2 = res[2:]
        dsum.append(res[0].reshape(3, t // dil, dil * D_GRP))
        a_tiles.append(res[1].reshape(N_HEADS, BLOCK * 2 * BLOCK))
    dqkv6 = _dqkv_dil_sum(dsum, dqkv6.reshape(6, t, D_GRP))
    drel = _relbias_grad(jnp.stack(a_tiles), onehot)
    dx1, dsh2, dsc2, dg2 = _mix_bwd_dh(dqkv6, w_in, x1, g2, sc2, dx2, seq)

    da1, du1, df1, dgt1 = _ffn_bwd_ds(dx1, gt1, f1, wd1, a1, u1, seq, 0.5)
    grads1 = [_ffn_bwd_w(h1, da1, du1, s1, df1)]
    res = _dw_in(h2, dqkv6, carry=_rs_d2d(grads1) if dist else None)
    grads_m = [res[0], dw_out]
    parts1 = partial_sums(grads1, res[1:]) if dist else None
    res = _ffn_bwd_dh(da1, du1, w["gu1"], x0, g1, sc1, dx1, seq,
                      carry=_join([_rs_ici(parts1), _rs_d2d(grads_m)]) if dist else None)
    dx0, dsh1, dsc1, dg1 = res[:4]
    pending = None
    if dist:
        pending = (chip_sums(parts1, res[4:5]), partial_sums(grads_m, res[5:7]))

    dmod = jnp.concatenate([dsh1, dsc1, dgt1, dsh2, dsc2, dgt2, dsh3, dsc3, dgt3], axis=1)
    return dict(grad_x=dx0.reshape(nb, seq, d), loss=loss[0, 0], dmod=dmod.reshape(nb, N_MOD * d),
                dffn1=grads1[0], dffn2=grads2[0], dwin=grads_m[0], dwout=grads_m[1], pending=pending,
                dg_ffn1=dg1, dg_mix=dg2, dg_ffn2=dg3, dg_final=dg_final, dg_sb=dg_sb, dg_dil=dg_dil,
                drel=drel.T)


_SMALL_ORDER = (("b_ada", N_MOD * 1024), ("g_ffn1", 1024), ("g_mix", 1024), ("g_ffn2", 1024), ("g_final", 1024),
                ("g_sb_out", D_GRP), ("g_dil_out", D_GRP), ("rel_bias", N_BUCKETS * N_HEADS))


def _pack_small(parts, extra=None):
    flat = [parts[name].reshape(-1).astype(F32) for name, _ in _SMALL_ORDER]
    used = sum(sz for _, sz in _SMALL_ORDER)
    pad = SMALL_ROWS * 128 - used
    tail = jnp.zeros((pad,), F32)
    if extra is not None:
        tail = tail.at[0].set(extra)
    return jnp.concatenate(flat + [tail]).reshape(SMALL_ROWS, 128)


def _unpack_small(packed, shapes):
    flat = packed.reshape(-1)
    out, off = {}, 0
    for name, sz in _SMALL_ORDER:
        out[name] = flat[off:off + sz].reshape(shapes[name])
        off += sz
    return out, flat[off]


def kernel(x, c, w_ada, b_ada, g_ffn1, w1_gate, w1_up, w1_down, g_mix, w_in, g_sb_out, g_dil_out, w_out, rel_bias, g_ffn2, w2_gate, w2_up, w2_down, g_final, loss_target, m_w_ada, m_b_ada, m_g_ffn1, m_w1_gate, m_w1_up, m_w1_down, m_g_mix, m_w_in, m_g_sb_out, m_g_dil_out, m_w_out, m_rel_bias, m_g_ffn2, m_w2_gate, m_w2_up, m_w2_down, m_g_final, v_w_ada, v_b_ada, v_g_ffn1, v_w1_gate, v_w1_up, v_w1_down, v_g_mix, v_w_in, v_g_sb_out, v_g_dil_out, v_w_out, v_rel_bias, v_g_ffn2, v_w2_gate, v_w2_up, v_w2_down, v_g_final):
    nb, seq, d = x.shape
    xi, yi, ci = lax.axis_index("x"), lax.axis_index("y"), lax.axis_index("c")
    chip = 2 * xi + yi
    ada_cols = w_ada.shape[-1]

    c_pad = jnp.zeros((8, d), F32).at[:nb].set(c)
    b_shard = lax.dynamic_slice(b_ada, (0, chip * ada_cols), (1, ada_cols))
    shards = dict(gu1=jnp.stack([w1_gate[0], w1_up[0]]), d1=w1_down, win=w_in, wout=w_out,
                  gu2=jnp.stack([w2_gate[0], w2_up[0]]), d2=w2_down)
    bufs = {k: lax.dynamic_update_slice(lax.empty((N_CHIPS,) + s.shape, BF), s.astype(BF)[None], (chip, 0, 0, 0))
            for k, s in shards.items()}
    c_all, mod_blk, bufs["gu1"] = _ada_fwd(c_pad, w_ada[0], b_shard, carry=_ag_weights([bufs["gu1"]]))
    mod = jnp.transpose(mod_blk[:, :nb, :], (1, 0, 2)).reshape(nb, N_MOD, d)

    gains = dict(g_ffn1=g_ffn1, g_mix=g_mix, g_ffn2=g_ffn2, g_final=g_final.reshape(1, d),
                 g_sb_out=g_sb_out.reshape(1, D_GRP), g_dil_out=g_dil_out.reshape(1, D_GRP))
    place = jnp.stack([ci, chip]).astype(jnp.int32)
    r = _local_step(x, mod, loss_target, bufs, gains, rel_bias, place)

    dmod = r["dmod"]
    dmod_pad = jnp.zeros((8, N_MOD * d), F32).at[:nb].set(dmod)
    dmod_blk = jnp.transpose(dmod_pad.reshape(8, N_CHIPS, ada_cols), (1, 0, 2))
    small_parts = dict(b_ada=_rowsum8(dmod_pad), g_ffn1=r["dg_ffn1"], g_mix=r["dg_mix"], g_ffn2=r["dg_ffn2"],
                       g_final=r["dg_final"], g_sb_out=r["dg_sb"], g_dil_out=r["dg_dil"], rel_bias=r["drel"])
    halves1, parts_m = r["pending"]
    res = _small_sync(_pack_small(small_parts, r["loss"]), dmod_blk, c_all,
                      carry=_join([_rs_final(halves1), _rs_ici(parts_m)]))
    small_sum, g_wada, gffn1 = res[:3]
    halves_m = [_sum_chips(place, p, l) for p, l in zip(parts_m, res[3:5])]
    gwin, gwout = _alone("rs_last", _rs_final(halves_m))
    gffn2 = r["dffn2"]

    small_w = dict(b_ada=b_ada, g_ffn1=g_ffn1, g_mix=g_mix, g_ffn2=g_ffn2, g_final=g_final,
                   g_sb_out=g_sb_out, g_dil_out=g_dil_out, rel_bias=rel_bias)
    small_m = dict(b_ada=m_b_ada, g_ffn1=m_g_ffn1, g_mix=m_g_mix, g_ffn2=m_g_ffn2, g_final=m_g_final,
                   g_sb_out=m_g_sb_out, g_dil_out=m_g_dil_out, rel_bias=m_rel_bias)
    small_v = dict(b_ada=v_b_ada, g_ffn1=v_g_ffn1, g_mix=v_g_mix, g_ffn2=v_g_ffn2, g_final=v_g_final,
                   g_sb_out=v_g_sb_out, g_dil_out=v_g_dil_out, rel_bias=v_rel_bias)
    shapes = {k: v.shape for k, v in small_w.items()}
    sg, sd, sm, sv = _adamw(_pack_small(small_w), small_sum.reshape(1, SMALL_ROWS, 128), 0,
                            _pack_small(small_m), _pack_small(small_v))
    sg, loss = _unpack_small(sg, shapes)
    sd, _ = _unpack_small(sd, shapes)
    sm, _ = _unpack_small(sm, shapes)
    sv, _ = _unpack_small(sv, shapes)

    big = {}

    def upd(name, w, g_arr, sel, m, v, transposed=False):
        swap = (lambda a: jnp.swapaxes(a, -1, -2)) if transposed else (lambda a: a)
        w2, m2, v2 = [swap(a)[0] for a in (w, m, v)]
        big[name] = [swap(a[None]) for a in _adamw(w2, g_arr, sel, m2, v2)]

    upd("w_ada", w_ada, g_wada.reshape(1, d, ada_cols), 0, m_w_ada, v_w_ada)
    upd("w1_gate", w1_gate, gffn1, 0, m_w1_gate, v_w1_gate, transposed=True)
    upd("w1_up", w1_up, gffn1, 1, m_w1_up, v_w1_up, transposed=True)
    upd("w1_down", w1_down, gffn1, 2, m_w1_down, v_w1_down)
    upd("w_in", w_in, gwin, 0, m_w_in, v_w_in)
    upd("w_out", w_out, gwout, 0, m_w_out, v_w_out)
    upd("w2_gate", w2_gate, gffn2, 0, m_w2_gate, v_w2_gate, transposed=True)
    upd("w2_up", w2_up, gffn2, 1, m_w2_up, v_w2_up, transposed=True)
    upd("w2_down", w2_down, gffn2, 2, m_w2_down, v_w2_down)

    names = ["w_ada", "b_ada", "g_ffn1", "w1_gate", "w1_up", "w1_down", "g_mix", "w_in", "g_sb_out", "g_dil_out",
             "w_out", "rel_bias", "g_ffn2", "w2_gate", "w2_up", "w2_down", "g_final"]
    outs = [loss, r["grad_x"]]
    for k, small in enumerate((sg, sd, sm, sv)):
        for name in names:
            outs.append(big[name][k] if name in big else small[name])
    return tuple(outs)
```

```python
import functools
import math

import numpy as np
import jax
import jax.numpy as jnp
from jax import lax
from jax.experimental import pallas as pl
from jax.experimental.pallas import tpu as pltpu

F32 = jnp.float32
BF = jnp.bfloat16
MESH = pl.DeviceIdType.MESH

HEAD_DIM = 64
N_HEADS = 8
D_GRP = N_HEADS * HEAD_DIM
DIL_CONFIGS = ((128, 1), (512, 4), (2048, 16))
N_STEPS = 128
BLOCK = 128
N_BUCKETS = 32
MAX_DISTANCE = 2048
N_MOD = 9
EPS = 1e-6
NEG_INF = -1e30
SCALE = HEAD_DIM ** -0.5

ADAM_LR = 0.001
ADAM_B1 = 0.9
ADAM_B2 = 0.999
ADAM_EPS = 1e-08
ADAM_WD = 0.01
ADAM_STEP = 10

N_CHIPS = 4
N_DEV = 8
VMEM_LIMIT = 56 * 1024 * 1024
TM = 512
TQ = 256
KB = 256
SMALL_ROWS = 120


def _cp(n_axes=0, **kw):
    sem = ("arbitrary",) * n_axes if n_axes else None
    return pltpu.CompilerParams(dimension_semantics=sem, vmem_limit_bytes=VMEM_LIMIT, **kw)


def _nn(a, b):
    return jnp.dot(a, b, preferred_element_type=F32)


def _nt(a, b):
    return lax.dot_general(a, b, (((1,), (1,)), ((), ())), preferred_element_type=F32)


def _tn(a, b):
    return lax.dot_general(a, b, (((0,), (0,)), ((), ())), preferred_element_type=F32)


def _nn2(x, m):
    hi = x.astype(BF)
    lo = (x - hi.astype(F32)).astype(BF)
    r = _nn(jnp.concatenate([hi, lo], axis=0), m)
    return r[:x.shape[0]] + r[x.shape[0]:]


def _softplus(z):
    return jnp.maximum(z, 0.0) + jnp.log1p(jnp.exp(-jnp.abs(z)))


def _sds(shape, dtype):
    return jax.ShapeDtypeStruct(shape, dtype)


def _whole(a):
    nd = a.ndim
    return pl.BlockSpec(a.shape, lambda *_: (0,) * nd, pipeline_mode=pl.Buffered(1))


def _modnorm_bwd_tile(dh, xv, gv, scv, dxo):
    r = lax.rsqrt(jnp.mean(xv * xv, axis=-1, keepdims=True) + EPS)
    n = xv * r
    ng = n * gv
    dsh = jnp.sum(dh, axis=0, keepdims=True)
    dsc = jnp.sum(dh * ng, axis=0, keepdims=True)
    dy = dh * (1.0 + scv)
    dg = jnp.sum(dy * n, axis=0, keepdims=True)
    dn = dy * gv
    dx = dxo + r * (dn - n * jnp.mean(dn * n, axis=-1, keepdims=True))
    return dx, dsh, dsc, dg


def _acc_rows(ref, val, first):
    @pl.when(first)
    def _():
        ref[...] = val

    @pl.when(jnp.logical_not(first))
    def _():
        ref[...] += val


def _modnorm_tile(x_ref, g_ref, sc_ref, sh_ref):
    xv = x_ref[...]
    r = lax.rsqrt(jnp.mean(xv * xv, axis=-1, keepdims=True) + EPS)
    return (((xv * r) * g_ref[...]) * (1.0 + sc_ref[...]) + sh_ref[...]).astype(BF)


def _ffn_up(x, g, sc, sh, wgu, seq, carry=None):
    t, d = x.shape
    fs = wgu.shape[-1]
    per = seq // TM

    def body(x_ref, g_ref, sc_ref, sh_ref, w_ref, h_ref, p_ref, q_ref, s_ref):
        hv = _modnorm_tile(x_ref, g_ref, sc_ref, sh_ref)
        h_ref[...] = hv
        for j in range(N_CHIPS):
            a = _nn(hv, w_ref[j, 0])
            u = _nn(hv, w_ref[j, 1])
            sig = jax.nn.sigmoid(a)
            q = a * sig
            p_ref[j] = (u * (sig * (1.0 + a * (1.0 - sig)))).astype(BF)
            q_ref[j] = q.astype(BF)
            s_ref[j] = (q * u).astype(BF)

    row = pl.BlockSpec((TM, d), lambda m: (m, 0))
    ex = pl.BlockSpec((None, 1, d), lambda m: (m // per, 0, 0))
    blk = pl.BlockSpec((N_CHIPS, TM, fs), lambda m: (0, m, 0))
    return _call(
        body, "ffn_up", (t // TM,),
        [row, pl.BlockSpec((1, d), lambda m: (0, 0)), ex, ex, _whole(wgu)],
        [row, blk, blk, blk],
        [_sds((t, d), BF)] + [_sds((N_CHIPS, t, fs), BF)] * 3,
        (x, g, sc, sh, wgu), carry=carry)


def _silu_parts(hv, w_ref):
    a = _nn(hv, w_ref[0])
    u = _nn(hv, w_ref[1])
    sig = jax.nn.sigmoid(a)
    q = a * sig
    return (u * (sig * (1.0 + a * (1.0 - sig)))).astype(BF), q.astype(BF), (q * u).astype(BF)


def _ffn_up_first(x, g, sc, sh, wgu, seq, carry=None):
    t, d = x.shape
    fs = wgu.shape[-1]
    per = seq // TM

    def body(x_ref, g_ref, sc_ref, sh_ref, w_ref, h_ref, p_ref, q_ref, s_ref):
        hv = _modnorm_tile(x_ref, g_ref, sc_ref, sh_ref)
        h_ref[...] = hv
        p_ref[...], q_ref[...], s_ref[...] = _silu_parts(hv, w_ref)

    row = pl.BlockSpec((TM, d), lambda m: (m, 0))
    ex = pl.BlockSpec((None, 1, d), lambda m: (m // per, 0, 0))
    blk = pl.BlockSpec((None, TM, fs), lambda m: (0, m, 0))
    return _call(
        body, "ffn_up_first", (t // TM,),
        [row, pl.BlockSpec((1, d), lambda m: (0, 0)), ex, ex, pl.BlockSpec((None, 2, d, fs), lambda m: (0, 0, 0, 0))],
        [row, blk, blk, blk],
        [_sds((t, d), BF)] + [_sds((N_CHIPS, t, fs), BF)] * 3,
        (x, g, sc, sh, wgu), carry=carry)


def _ffn_up_next(k, h, wgu, pqs, carry=None):
    t, d = h.shape
    fs = wgu.shape[-1]

    def body(h_ref, w_ref, p_in, q_in, s_in, p_ref, q_ref, s_ref):
        del p_in, q_in, s_in
        p_ref[...], q_ref[...], s_ref[...] = _silu_parts(h_ref[...], w_ref)

    blk = pl.BlockSpec((None, TM, fs), lambda m: (k, m, 0))
    any_spec = pl.BlockSpec(memory_space=pl.ANY)
    return _call(
        body, "ffn_up_next", (t // TM,),
        [pl.BlockSpec((TM, d), lambda m: (m, 0)), pl.BlockSpec((None, 2, d, fs), lambda m: (k, 0, 0, 0))]
        + [any_spec] * 3,
        [blk, blk, blk], [_sds((N_CHIPS, t, fs), BF)] * 3,
        (h, wgu, *pqs), carry=carry, io_alias={2: 0, 3: 1, 4: 2})


def _ffn_down(s, wd, x, gt, seq, coef, carry=None):
    _, t, fs = s.shape
    d = x.shape[-1]
    per = seq // TM

    def body(s_ref, w_ref, x_ref, gt_ref, f_ref, xo_ref):
        f = _nn(s_ref[0], w_ref[0, 0])
        for j in range(1, N_CHIPS):
            f = f + _nn(s_ref[j], w_ref[j, 0])
        f_ref[...] = f
        xo_ref[...] = x_ref[...] + (coef * gt_ref[...]) * f

    row = pl.BlockSpec((TM, d), lambda m: (m, 0))
    return _call(
        body, "ffn_down", (t // TM,),
        [pl.BlockSpec((N_CHIPS, TM, fs), lambda m: (0, m, 0)), _whole(wd), row,
         pl.BlockSpec((None, 1, d), lambda m: (m // per, 0, 0))],
        [row, row], [_sds((t, d), F32), _sds((t, d), F32)], (s, wd, x, gt), carry=carry)


def _ffn_bwd_ds(dxo, gt, f, wd, p, q, seq, coef, carry=None):
    t, d = dxo.shape
    fs = p.shape[-1]
    per = seq // TM
    nb = t // seq

    def body(dxo_ref, gt_ref, f_ref, w_ref, p_ref, q_ref, da_ref, du_ref, df_ref, dgt_ref):
        m = pl.program_id(0)
        dxv = dxo_ref[...]
        df = ((coef * gt_ref[...]) * dxv).astype(BF)
        df_ref[...] = df
        _acc_rows(dgt_ref, coef * jnp.sum(dxv * f_ref[...], axis=0, keepdims=True), m % per == 0)
        for j in range(N_CHIPS):
            ds = _nt(df, w_ref[j, 0])
            da_ref[j] = (ds * p_ref[j].astype(F32)).astype(BF)
            du_ref[j] = (ds * q_ref[j].astype(F32)).astype(BF)

    row = pl.BlockSpec((TM, d), lambda m: (m, 0))
    blk = pl.BlockSpec((N_CHIPS, TM, fs), lambda m: (0, m, 0))
    ex = pl.BlockSpec((None, 1, d), lambda m: (m // per, 0, 0))
    return _call(
        body, "ffn_bwd_ds", (t // TM,),
        [row, ex, row, _whole(wd), blk, blk],
        [blk, blk, row, ex],
        [_sds((N_CHIPS, t, fs), BF), _sds((N_CHIPS, t, fs), BF), _sds((t, d), BF), _sds((nb, 1, d), F32)],
        (dxo, gt, f, wd, p, q), carry=carry)


TM_X = 256


def _ffn_bwd_x(dxo, gt, f, wd, p, q, wgu, x, g, sc, seq, coef):
    t, d = dxo.shape
    fs = p.shape[-1]
    per = seq // TM_X
    nb = t // seq

    def body(dxo_ref, gt_ref, f_ref, wd_ref, p_ref, q_ref, w_ref, x_ref, g_ref, sc_ref,
             da_ref, du_ref, df_ref, dgt_ref, dx_ref, dsh_ref, dsc_ref, dg_ref):
        m = pl.program_id(0)
        dxv = dxo_ref[...]
        df = ((coef * gt_ref[...]) * dxv).astype(BF)
        df_ref[...] = df
        _acc_rows(dgt_ref, coef * jnp.sum(dxv * f_ref[...], axis=0, keepdims=True), m % per == 0)
        dh = None
        for j in range(N_CHIPS):
            ds = _nt(df, wd_ref[j, 0])
            da = (ds * p_ref[j].astype(F32)).astype(BF)
            du = (ds * q_ref[j].astype(F32)).astype(BF)
            da_ref[j] = da
            du_ref[j] = du
            part = _nt(da, w_ref[j, 0]) + _nt(du, w_ref[j, 1])
            dh = part if dh is None else dh + part
        dx, dsh, dsc, dg = _modnorm_bwd_tile(dh, x_ref[...], g_ref[...], sc_ref[...], dxv)
        dx_ref[...] = dx
        _acc_rows(dsh_ref, dsh, m % per == 0)
        _acc_rows(dsc_ref, dsc, m % per == 0)
        _acc_rows(dg_ref, dg, m == 0)

    row = pl.BlockSpec((TM_X, d), lambda m: (m, 0))
    blk = pl.BlockSpec((N_CHIPS, TM_X, fs), lambda m: (0, m, 0))
    ex = pl.BlockSpec((None, 1, d), lambda m: (m // per, 0, 0))
    vec = pl.BlockSpec((1, d), lambda m: (0, 0))
    exs = _sds((nb, 1, d), F32)
    return pl.pallas_call(
        body, name="ffn_bwd_x", grid=(t // TM_X,),
        in_specs=[row, ex, row, _whole(wd), blk, blk, _whole(wgu), row, vec, ex],
        out_specs=[blk, blk, row, ex, row, ex, ex, vec],
        out_shape=[_sds((N_CHIPS, t, fs), BF), _sds((N_CHIPS, t, fs), BF), _sds((t, d), BF), exs,
                   _sds((t, d), F32), exs, exs, _sds((1, d), F32)],
        compiler_params=_cp(1))(dxo, gt, f, wd, p, q, wgu, x, g, sc)


TK_W = 1024


def _ffn_bwd_w(h, da, du, s, df):
    t, d = h.shape
    fs = da.shape[-1]

    def body(h_ref, da_ref, du_ref, s_ref, df_ref, o_ref):
        kt = pl.program_id(1)
        hv = h_ref[...]
        parts = (_tn(da_ref[...], hv), _tn(du_ref[...], hv), _tn(s_ref[...], df_ref[...]))

        @pl.when(kt == 0)
        def _():
            for i, p in enumerate(parts):
                o_ref[i] = p

        @pl.when(kt != 0)
        def _():
            for i, p in enumerate(parts):
                o_ref[i] += p

    row = pl.BlockSpec((TK_W, d), lambda j, kt: (kt, 0))
    blk = pl.BlockSpec((None, TK_W, fs), lambda j, kt: (j, kt, 0))
    return pl.pallas_call(
        body, name="ffn_bwd_w", grid=(N_CHIPS, t // TK_W),
        in_specs=[row, blk, blk, blk, row],
        out_specs=pl.BlockSpec((None, 3, fs, d), lambda j, kt: (j, 0, 0, 0)),
        out_shape=_sds((N_CHIPS, 3, fs, d), F32),
        compiler_params=_cp(2))(h, da, du, s, df)


def _ffn_bwd_dh(da, du, wgu, x, g, sc, dxo, seq, carry=None):
    _, t, fs = da.shape
    d = x.shape[-1]
    per = seq // TM
    nb = t // seq

    def body(da_ref, du_ref, w_ref, x_ref, g_ref, sc_ref, dxo_ref, dx_ref, dsh_ref, dsc_ref, dg_ref):
        m = pl.program_id(0)
        dh = _nt(da_ref[0], w_ref[0, 0]) + _nt(du_ref[0], w_ref[0, 1])
        for j in range(1, N_CHIPS):
            dh = dh + _nt(da_ref[j], w_ref[j, 0]) + _nt(du_ref[j], w_ref[j, 1])
        dx, dsh, dsc, dg = _modnorm_bwd_tile(dh, x_ref[...], g_ref[...], sc_ref[...], dxo_ref[...])
        dx_ref[...] = dx
        _acc_rows(dsh_ref, dsh, m % per == 0)
        _acc_rows(dsc_ref, dsc, m % per == 0)
        _acc_rows(dg_ref, dg, m == 0)

    row = pl.BlockSpec((TM, d), lambda m: (m, 0))
    blk = pl.BlockSpec((N_CHIPS, TM, fs), lambda m: (0, m, 0))
    ex = pl.BlockSpec((None, 1, d), lambda m: (m // per, 0, 0))
    vec = pl.BlockSpec((1, d), lambda m: (0, 0))
    return _call(
        body, "ffn_bwd_dh", (t // TM,),
        [blk, blk, _whole(wgu), row, vec, ex, row],
        [row, ex, ex, vec],
        [_sds((t, d), F32), _sds((nb, 1, d), F32), _sds((nb, 1, d), F32), _sds((1, d), F32)],
        (da, du, wgu, x, g, sc, dxo), carry=carry)


def _qkv_proj(x, g, sc, sh, w_in, seq, carry=None):
    t, d = x.shape
    wc = w_in.shape[-1]
    per = seq // TM

    dils = [dil for _, dil in DIL_CONFIGS if dil > 1]

    def body(x_ref, g_ref, sc_ref, sh_ref, w_ref, h_ref, o_ref, *rest):
        res_refs, buf = rest[:len(dils)], rest[len(dils)]
        hv = _modnorm_tile(x_ref, g_ref, sc_ref, sh_ref)
        h_ref[...] = hv
        for j in range(N_CHIPS):
            rf = _nn(hv, w_ref[j, 0])
            r = rf.astype(BF)
            for a, lc, off, width in _col_pieces(j, wc):
                o_ref[a, :, lc:lc + width] = r[:, off:off + width]
                if a < 3:
                    continue
                for c0 in range(0, width, 128):
                    cg = (lc + c0) // 128
                    buf[...] = rf[:, off + c0:off + c0 + 128]
                    for ref, dil in zip(res_refs, dils):
                        for rr in range(dil):
                            ref[a - 3, :, rr * D_GRP + cg * 128:rr * D_GRP + (cg + 1) * 128] = (
                                buf[pl.ds(rr, TM // dil, stride=dil), :].astype(BF))

    row = pl.BlockSpec((TM, d), lambda m: (m, 0))
    ex = pl.BlockSpec((None, 1, d), lambda m: (m // per, 0, 0))
    return _call(
        body, "qkv_proj", (t // TM,),
        [row, pl.BlockSpec((1, d), lambda m: (0, 0)), ex, ex, _whole(w_in)],
        [row, pl.BlockSpec((6, TM, D_GRP), lambda m: (0, m, 0))]
        + [pl.BlockSpec((3, TM // dil, dil * D_GRP), lambda m: (0, m, 0)) for dil in dils],
        [_sds((t, d), BF), _sds((6, t, D_GRP), BF)] + [_sds((3, t // dil, dil * D_GRP), BF) for dil in dils],
        (x, g, sc, sh, w_in), scratch=[pltpu.VMEM((TM, 128), F32)], carry=carry)


def _col_pieces(j, wc):
    out, off = [], 0
    while off < wc:
        a, lc = divmod(j * wc + off, D_GRP)
        width = min(D_GRP - lc, wc - off)
        out.append((a, lc, off, width))
        off += width
    return out


def _chip_cols(g6_ref, j, wc):
    return jnp.concatenate([g6_ref[a, :, lc:lc + width] for a, lc, _, width in _col_pieces(j, wc)], axis=1)


def _mix_out(on_sb, on_dil, w_out, x, gt, seq):
    t, d = x.shape
    per = seq // TM

    def body(a_ref, b_ref, w_ref, x_ref, gt_ref, t_ref, xo_ref):
        tv = _nn(a_ref[...], w_ref[0:D_GRP, :]) + _nn(b_ref[...], w_ref[D_GRP:2 * D_GRP, :])
        t_ref[...] = tv
        xo_ref[...] = x_ref[...] + gt_ref[...] * tv

    row = pl.BlockSpec((TM, d), lambda m: (m, 0))
    half = pl.BlockSpec((TM, D_GRP), lambda m: (m, 0))
    return pl.pallas_call(
        body, name="mix_out", grid=(t // TM,),
        in_specs=[half, half, pl.BlockSpec((2 * D_GRP, d), lambda m: (0, 0)), row,
                  pl.BlockSpec((None, 1, d), lambda m: (m // per, 0, 0))],
        out_specs=[row, row],
        out_shape=[_sds((t, d), F32), _sds((t, d), F32)],
        compiler_params=_cp(1))(on_sb, on_dil, w_out, x, gt)


def _sb_masks():
    lane = lax.broadcasted_iota(jnp.int32, (1, 2 * HEAD_DIM), 1)
    hm0 = lane < HEAD_DIM
    rel = lax.broadcasted_iota(jnp.int32, (TQ, KB), 0) - lax.broadcasted_iota(jnp.int32, (TQ, KB), 1)
    kr = lax.broadcasted_iota(jnp.int32, (KB, KB), 0)
    kc = lax.broadcasted_iota(jnp.int32, (KB, KB), 1)
    return hm0, rel, kr, kc


def _stack_pair(x, hm0):
    zero = jnp.zeros_like(x)
    return jnp.concatenate([jnp.where(hm0, x, zero), jnp.where(hm0, zero, x)], axis=0)


def _headnorm_pair(o, gv, hm0):
    o2 = o * o
    ms0 = jnp.sum(jnp.where(hm0, o2, 0.0), axis=-1, keepdims=True) * (1.0 / HEAD_DIM)
    ms1 = jnp.sum(jnp.where(hm0, 0.0, o2), axis=-1, keepdims=True) * (1.0 / HEAD_DIM)
    r = jnp.where(hm0, lax.rsqrt(ms0 + EPS), lax.rsqrt(ms1 + EPS))
    return (o * r) * gv


SB_DEAD = -104.0


def _alive(c_l):
    return (jnp.max(c_l) > SB_DEAD).astype(jnp.int32)


def _sb_fwd(qkv6, g_sb, nb, seq, carry=None):
    nq = seq // TQ

    def body(q_ref, k_ref, v_ref, g_ref, o_ref, on_ref):
        qi = pl.program_id(2)
        hm0, rel, kr, kc = _sb_masks()
        upper = (kr > kc).astype(BF)
        qs = _stack_pair(q_ref[...], hm0)
        causal2 = jnp.concatenate([rel, rel], axis=0) > 0

        def block(kj, causal, c_l, acc):
            ks = pl.multiple_of(kj * KB, KB)
            z = _nt(qs, k_ref[pl.ds(ks, KB), :]) * SCALE
            sp = _softplus(z)
            ln = -sp if causal is None else jnp.where(causal, -sp, 0.0)
            suf = _nn2(ln, upper)
            w = jnp.exp((z - sp) + (suf + c_l))
            if causal is not None:
                w = jnp.where(causal, w, 0.0)
            return c_l + (suf[:, 0:1] + ln[:, 0:1]), acc + _nn(w.astype(BF), v_ref[pl.ds(ks, KB), :])

        c_l, acc = block(qi, causal2, jnp.zeros((2 * TQ, 1), F32), jnp.zeros((2 * TQ, 2 * HEAD_DIM), F32))

        def cond(carry):
            return jnp.logical_and(carry[0] <= qi, carry[1] > 0)

        def kbody(carry):
            it, _, c_l, acc = carry
            c_l, acc = block(qi - it, None, c_l, acc)
            return it + 1, _alive(c_l), c_l, acc

        acc = lax.while_loop(cond, kbody, (jnp.int32(1), _alive(c_l), c_l, acc))[3]
        o = jnp.where(hm0, acc[:TQ], acc[TQ:])
        o_ref[...] = o
        on_ref[...] = _headnorm_pair(o, g_ref[...], hm0).astype(BF)

    w = 2 * HEAD_DIM
    full = lambda i: pl.BlockSpec((None, None, seq, w), lambda b, hp, q: (i, b, 0, hp))
    qblk = pl.BlockSpec((None, None, TQ, w), lambda b, hp, q: (0, b, q, hp))
    oblk = pl.BlockSpec((None, TQ, w), lambda b, hp, q: (b, q, hp))
    return _call(
        body, "sb_fwd", (nb, N_HEADS // 2, nq),
        [qblk, full(1), full(2), pl.BlockSpec((1, w), lambda b, hp, q: (0, hp))],
        [oblk, oblk],
        [_sds((nb, seq, D_GRP), F32), _sds((nb, seq, D_GRP), BF)],
        (qkv6, qkv6, qkv6, g_sb), carry=carry)


def _sb_bwd(qkv6, do, nb, seq, carry=None):
    nq = seq // TQ
    nk = seq // KB

    def body(q_ref, k_ref, v_ref, do_ref, out_ref, dk_acc, dv_acc, g_st, s_st):
        qi = pl.program_id(2)
        hm0, rel, kr, kc = _sb_masks()
        upper = (kr > kc).astype(BF)
        lower = (kr < kc).astype(BF)

        @pl.when(qi == 0)
        def _():
            dk_acc[...] = jnp.zeros_like(dk_acc)
            dv_acc[...] = jnp.zeros_like(dv_acc)

        qs = _stack_pair(q_ref[...], hm0)
        dos = _stack_pair(do_ref[...], hm0).astype(BF)
        causal2 = jnp.concatenate([rel, rel], axis=0) > 0

        def weights(kj, causal, c_l):
            ks = pl.multiple_of(kj * KB, KB)
            vb = v_ref[pl.ds(ks, KB), :]
            z = _nt(qs, k_ref[pl.ds(ks, KB), :]) * SCALE
            sp = _softplus(z)
            ln = -sp if causal is None else jnp.where(causal, -sp, 0.0)
            suf = _nn2(ln, upper)
            lsz = z - sp
            w = jnp.exp(lsz + (suf + c_l))
            if causal is not None:
                w = jnp.where(causal, w, 0.0)
            g_st[kj] = w * _nt(dos, vb)
            s_st[kj] = jnp.exp(lsz)
            dv_acc[pl.ds(ks, KB), :] += _tn(w.astype(BF), dos)
            return c_l + (suf[:, 0:1] + ln[:, 0:1])

        zc = jnp.zeros((2 * TQ, 1), F32)
        c_l = weights(qi, causal2, zc)

        def acond(carry):
            return jnp.logical_and(carry[0] <= qi, carry[1] > 0)

        def abody(carry):
            c_l = weights(qi - carry[0], None, carry[2])
            return carry[0] + 1, _alive(c_l), c_l

        n_used = lax.while_loop(acond, abody, (jnp.int32(1), _alive(c_l), c_l))[0]

        def grads(kj, causal, c_g, dq):
            ks = pl.multiple_of(kj * KB, KB)
            kb = k_ref[pl.ds(ks, KB), :]
            g = g_st[kj]
            sig = s_st[kj]
            pre = _nn(g.astype(BF), lower)
            dz = g * (1.0 - sig) - sig * (pre + c_g)
            if causal is not None:
                dz = jnp.where(causal, dz, 0.0)
            dzb = (dz * SCALE).astype(BF)
            dk_acc[pl.ds(ks, KB), :] += _tn(dzb, qs)
            return c_g + (pre[:, KB - 1:KB] + g[:, KB - 1:KB]), dq + _nn(dzb, kb)

        c_g, dq = lax.fori_loop(qi - n_used + 1, qi, lambda kj, cr: grads(kj, None, *cr),
                                (zc, jnp.zeros((2 * TQ, 2 * HEAD_DIM), F32)))
        _, dq = grads(qi, causal2, c_g, dq)
        dq = jnp.where(hm0, dq[:TQ], dq[TQ:])
        out_ref[0, pl.ds(pl.multiple_of(qi * TQ, TQ), TQ), :] = dq.astype(BF)

        @pl.when(qi == nq - 1)
        def _():
            out_ref[1] = dk_acc[...].astype(BF)
            out_ref[2] = dv_acc[...].astype(BF)

    w = 2 * HEAD_DIM
    full = lambda i: pl.BlockSpec((None, None, seq, w), lambda b, hp, q: (i, b, 0, hp))
    qblk = pl.BlockSpec((None, None, TQ, w), lambda b, hp, q: (0, b, q, hp))
    oblk = pl.BlockSpec((None, TQ, w), lambda b, hp, q: (b, q, hp))
    return _call(
        body, "sb_bwd", (nb, N_HEADS // 2, nq),
        [qblk, full(1), full(2), oblk],
        [pl.BlockSpec((3, None, seq, w), lambda b, hp, q: (0, b, 0, hp))],
        [_sds((6, nb, seq, D_GRP), BF)], (qkv6, qkv6, qkv6, do),
        scratch=[pltpu.VMEM((seq, w), F32), pltpu.VMEM((seq, w), F32),
                 pltpu.VMEM((nk, 2 * TQ, KB), F32), pltpu.VMEM((nk, 2 * TQ, KB), F32)],
        carry=carry)


def _t5_bucket(n):
    max_exact = N_BUCKETS // 2
    nf = np.maximum(n, 1).astype(np.float32)
    large = max_exact + (np.log(nf / max_exact) / math.log(MAX_DISTANCE / max_exact)
                         * (N_BUCKETS - max_exact)).astype(np.int32)
    large = np.minimum(large, N_BUCKETS - 1)
    return np.where(n < max_exact, n, large).astype(np.int32)


def _bucket_map(dilation):
    step = BLOCK + np.arange(BLOCK)[:, None] - np.arange(2 * BLOCK)[None, :]
    return _t5_bucket(np.clip(step, 0, N_STEPS) * dilation)


GRP_HEADS = 4
GRP_W = GRP_HEADS * HEAD_DIM


def _dil_masks():
    lane = lax.broadcasted_iota(jnp.int32, (1, GRP_W), 1)
    heads = [jnp.logical_and(lane >= HEAD_DIM * i, lane < HEAD_DIM * (i + 1)) for i in range(GRP_HEADS)]
    iq = jnp.bitwise_and(lax.broadcasted_iota(jnp.int32, (GRP_HEADS * BLOCK, BLOCK), 0), BLOCK - 1)
    ik = lax.broadcasted_iota(jnp.int32, (GRP_HEADS * BLOCK, BLOCK), 1)
    return heads, ik <= iq, ik >= iq


def _stack_heads(x, heads):
    zero = jnp.zeros_like(x)
    return jnp.concatenate([jnp.where(hm, x, zero) for hm in heads], axis=0)


def _unstack_heads(xs, heads):
    out = xs[0:BLOCK]
    for i in range(1, GRP_HEADS):
        out = jnp.where(heads[i], xs[i * BLOCK:(i + 1) * BLOCK], out)
    return out


def _dil_rows(n):
    rs = pl.multiple_of(n * BLOCK, BLOCK)
    ps = pl.multiple_of(jnp.maximum(n - 1, 0) * BLOCK, BLOCK)
    return pl.ds(rs, BLOCK), pl.ds(ps, BLOCK)


def _dil_probs(qs, kc, kp, b_ref, gi, valid_c, valid_p):
    rows = slice(gi * GRP_HEADS * BLOCK, (gi + 1) * GRP_HEADS * BLOCK)
    zc = _nt(qs, kc) * SCALE + b_ref[rows, BLOCK:2 * BLOCK]
    zp = _nt(qs, kp) * SCALE + b_ref[rows, 0:BLOCK]
    zc = jnp.where(valid_c, zc, NEG_INF)
    zp = jnp.where(valid_p, zp, NEG_INF)
    m = jnp.maximum(jnp.max(zc, axis=-1, keepdims=True), jnp.max(zp, axis=-1, keepdims=True))
    ec = jnp.exp(zc - m)
    ep = jnp.exp(zp - m)
    den = jnp.sum(ec, axis=-1, keepdims=True) + jnp.sum(ep, axis=-1, keepdims=True)
    return ec, ep, den, m


def _dil_fwd(qkv6r, base, bias, nb, sub_len, dilation):
    n_blk = sub_len // BLOCK

    def body(q_ref, k_ref, v_ref, b_ref, o_ref, l_ref):
        heads, valid_c, valid_p0 = _dil_masks()

        def nbody(n, carry):
            cur, prev = _dil_rows(n)
            valid_p = jnp.logical_and(valid_p0, n > 0)
            for gi in range(N_HEADS // GRP_HEADS):
                lanes = slice(gi * GRP_W, (gi + 1) * GRP_W)
                qs = _stack_heads(q_ref[cur, lanes], heads)
                ec, ep, den, m = _dil_probs(qs, k_ref[cur, lanes], k_ref[prev, lanes], b_ref, gi, valid_c, valid_p)
                o = (_nn(ec.astype(BF), v_ref[cur, lanes]) + _nn(ep.astype(BF), v_ref[prev, lanes])) / den
                o_ref[cur, lanes] = _unstack_heads(o, heads)
                l_ref[cur, lanes] = _unstack_heads(jnp.broadcast_to(m + jnp.log(den), o.shape), heads)
            return carry

        lax.fori_loop(0, n_blk, nbody, 0)

    seqblk = lambda i: pl.BlockSpec((None, None, sub_len, D_GRP), lambda b, r: (i, b, 0, r))
    oblk = pl.BlockSpec((None, sub_len, D_GRP), lambda b, r: (b, 0, r))
    shp = _sds((nb, sub_len, dilation * D_GRP), F32)
    return pl.pallas_call(
        body, name="dil_fwd_%d" % dilation, grid=(nb, dilation),
        in_specs=[seqblk(base), seqblk(base + 1), seqblk(base + 2), _whole(bias)],
        out_specs=[oblk, oblk], out_shape=[shp, shp],
        compiler_params=_cp(2))(qkv6r, qkv6r, qkv6r, bias)


def _dil_bwd(qkv6r, base, bias, do_c, dd_c, nb, sub_len, dilation, carry=None):
    n_blk = sub_len // BLOCK

    def body(q_ref, k_ref, v_ref, b_ref, do_ref, dd_ref, out_ref, a_ref):
        heads, valid_c, valid_p0 = _dil_masks()
        first = jnp.logical_and(pl.program_id(0) == 0, pl.program_id(1) == 0)

        @pl.when(first)
        def _():
            a_ref[...] = jnp.zeros_like(a_ref)

        out_ref[1] = jnp.zeros((sub_len, D_GRP), F32)
        out_ref[2] = jnp.zeros((sub_len, D_GRP), F32)

        def nbody(n, carry):
            cur, prev = _dil_rows(n)
            valid_p = jnp.logical_and(valid_p0, n > 0)
            for gi in range(N_HEADS // GRP_HEADS):
                lanes = slice(gi * GRP_W, (gi + 1) * GRP_W)
                kc, kp = k_ref[cur, lanes], k_ref[prev, lanes]
                vc, vp = v_ref[cur, lanes], v_ref[prev, lanes]
                qs = _stack_heads(q_ref[cur, lanes], heads)
                dos = _stack_heads(do_ref[cur, lanes], heads).astype(BF)
                dds = jnp.sum(_stack_heads(dd_ref[cur, lanes], heads), axis=-1, keepdims=True) * (1.0 / HEAD_DIM)
                ec, ep, den, _ = _dil_probs(qs, kc, kp, b_ref, gi, valid_c, valid_p)
                inv = 1.0 / den
                pc = ec * inv
                pp = ep * inv
                dzc = pc * (_nt(dos, vc) + dds)
                dzp = pp * (_nt(dos, vp) + dds)
                rows = slice(gi * GRP_HEADS * BLOCK, (gi + 1) * GRP_HEADS * BLOCK)
                a_ref[rows, BLOCK:2 * BLOCK] += dzc
                a_ref[rows, 0:BLOCK] += dzp
                dzcb = (dzc * SCALE).astype(BF)
                dzpb = (dzp * SCALE).astype(BF)
                out_ref[0, cur, lanes] = _unstack_heads(_nn(dzcb, kc) + _nn(dzpb, kp), heads)
                out_ref[1, cur, lanes] += _tn(dzcb, qs)
                out_ref[1, prev, lanes] += _tn(dzpb, qs)
                out_ref[2, cur, lanes] += _tn(pc.astype(BF), dos)
                out_ref[2, prev, lanes] += _tn(pp.astype(BF), dos)
            return carry

        lax.fori_loop(0, n_blk, nbody, 0)

    seqblk = lambda i: pl.BlockSpec((None, None, sub_len, D_GRP), lambda b, r: (i, b, 0, r))
    oblk = pl.BlockSpec((None, sub_len, D_GRP), lambda b, r: (b, 0, r))
    return _call(
        body, "dil_bwd_%d" % dilation, (nb, dilation),
        [seqblk(base), seqblk(base + 1), seqblk(base + 2), _whole(bias), oblk, oblk],
        [pl.BlockSpec((3, None, sub_len, D_GRP), lambda b, r: (0, b, 0, r)),
         pl.BlockSpec((N_HEADS * BLOCK, 2 * BLOCK), lambda b, r: (0, 0))],
        [_sds((3, nb, sub_len, dilation * D_GRP), F32), _sds((N_HEADS * BLOCK, 2 * BLOCK), F32)],
        (qkv6r, qkv6r, qkv6r, bias, do_c, dd_c), carry=carry)


def _group_ones():
    idx = np.arange(D_GRP) // HEAD_DIM
    return jnp.asarray((idx[:, None] == idx[None, :]).astype(np.float32), dtype=BF)


def _dil_alphas(l1, l4, l16):
    mx = jnp.maximum(jnp.maximum(l1, l4), l16)
    e1 = jnp.exp(l1 - mx)
    e4 = jnp.exp(l4 - mx)
    e16 = jnp.exp(l16 - mx)
    den = e1 + e4 + e16
    return e1 / den, e4 / den, e16 / den


def _residue_spec(dil):
    return pl.BlockSpec((TM // dil, dil * D_GRP), lambda m: (m, 0))


def _from_residue(src, dil, cg, buf):
    if dil == 1:
        return src[:, cg * 128:(cg + 1) * 128]
    for r in range(dil):
        buf[pl.ds(r, TM // dil, stride=dil), :] = src[:, r * D_GRP + cg * 128:r * D_GRP + (cg + 1) * 128]
    return buf[...]


def _to_residue(dst, dil, cg, buf, val):
    if dil == 1:
        dst[:, cg * 128:(cg + 1) * 128] = val
        return
    buf[...] = val
    for r in range(dil):
        dst[:, r * D_GRP + cg * 128:r * D_GRP + (cg + 1) * 128] = buf[pl.ds(r, TM // dil, stride=dil), :]


def _pair_sum(x, hm0):
    s0 = jnp.sum(jnp.where(hm0, x, 0.0), axis=-1, keepdims=True)
    s1 = jnp.sum(jnp.where(hm0, 0.0, x), axis=-1, keepdims=True)
    return jnp.where(hm0, s0, s1)


def _dil_comb(os, ls, g_dil):
    t = os[0].shape[0]
    dils = [dil for _, dil in DIL_CONFIGS]

    def body(o1, l1, o4, l4, o16, l16, g_ref, o_ref, on_ref, b0, b1, b2, b3):
        hm0 = lax.broadcasted_iota(jnp.int32, (1, 128), 1) < HEAD_DIM
        for cg in range(D_GRP // 128):
            lanes = slice(cg * 128, (cg + 1) * 128)
            ov = [_from_residue(src, dil, cg, buf) for src, dil, buf in zip((o1, o4, o16), dils, (None, b0, b1))]
            lv = [_from_residue(src, dil, cg, buf) for src, dil, buf in zip((l1, l4, l16), dils, (None, b2, b3))]
            a1, a4, a16 = _dil_alphas(*lv)
            o = a1 * ov[0] + a4 * ov[1] + a16 * ov[2]
            o_ref[:, lanes] = o
            on_ref[:, lanes] = _headnorm_pair(o, g_ref[:, lanes], hm0).astype(BF)

    blk = pl.BlockSpec((TM, D_GRP), lambda m: (m, 0))
    specs = [_residue_spec(dil) for dil in dils for _ in range(2)]
    return pl.pallas_call(
        body, name="dil_comb", grid=(t // TM,),
        in_specs=specs + [pl.BlockSpec((1, D_GRP), lambda m: (0, 0))],
        out_specs=[blk, blk],
        out_shape=[_sds((t, D_GRP), F32), _sds((t, D_GRP), BF)],
        scratch_shapes=[pltpu.VMEM((TM, 128), F32)] * 4,
        compiler_params=_cp(1))(os[0], ls[0], os[1], ls[1], os[2], ls[2], g_dil)


def _dil_comb_bwd(do, os, ls):
    t = do.shape[0]
    dils = [dil for _, dil in DIL_CONFIGS]

    def body(do_ref, o1, l1, o4, l4, o16, l16, d1, d4, d16, e1, e4, e16, b0, b1, b2, b3):
        hm0 = lax.broadcasted_iota(jnp.int32, (1, 128), 1) < HEAD_DIM
        for cg in range(D_GRP // 128):
            dov = do_ref[:, cg * 128:(cg + 1) * 128]
            ov = [_from_residue(src, dil, cg, buf) for src, dil, buf in zip((o1, o4, o16), dils, (None, b0, b1))]
            lv = [_from_residue(src, dil, cg, buf) for src, dil, buf in zip((l1, l4, l16), dils, (None, b2, b3))]
            al = _dil_alphas(*lv)
            sbar = al[0] * _pair_sum(dov * ov[0], hm0)
            for a_c, o_c in zip(al[1:], ov[1:]):
                sbar = sbar + a_c * _pair_sum(dov * o_c, hm0)
            for a_c, dil, dref, eref in zip(al, dils, (d1, d4, d16), (e1, e4, e16)):
                _to_residue(dref, dil, cg, b0, a_c * dov)
                _to_residue(eref, dil, cg, b1, -a_c * sbar)

    specs = [_residue_spec(dil) for dil in dils]
    return pl.pallas_call(
        body, name="dil_comb_bwd", grid=(t // TM,),
        in_specs=[pl.BlockSpec((TM, D_GRP), lambda m: (m, 0))] + [sp for sp in specs for _ in range(2)],
        out_specs=specs + specs,
        out_shape=[_sds((t // dil, dil * D_GRP), F32) for dil in dils] * 2,
        scratch_shapes=[pltpu.VMEM((TM, 128), F32)] * 4,
        compiler_params=_cp(1))(do, os[0], ls[0], os[1], ls[1], os[2], ls[2])


def _dqkv_dil_sum(ds, dqkv6):
    t = dqkv6.shape[1]
    dils = [dil for _, dil in DIL_CONFIGS]

    def body(*refs):
        srcs, o_ref, acc = refs[:len(dils)], refs[len(dils) + 1], refs[len(dils) + 2]
        for a in range(3):
            for cg in range(D_GRP // 128):
                for src, dil in zip(srcs, dils):
                    for r in range(dil):
                        part = src[a, :, r * D_GRP + cg * 128:r * D_GRP + (cg + 1) * 128]
                        rows = pl.ds(r, TM // dil, stride=dil) if dil > 1 else slice(None)
                        if dil == dils[0]:
                            acc[rows, :] = part
                        else:
                            acc[rows, :] += part
                o_ref[a, :, cg * 128:(cg + 1) * 128] = acc[...].astype(BF)

    return pl.pallas_call(
        body, name="dqkv_dil_sum", grid=(t // TM,),
        in_specs=[pl.BlockSpec((3, TM // dil, dil * D_GRP), lambda m: (0, m, 0)) for dil in dils]
        + [pl.BlockSpec(memory_space=pl.ANY)],
        out_specs=pl.BlockSpec((3, TM, D_GRP), lambda m: (1, m, 0)),
        out_shape=_sds((6, t, D_GRP), BF), input_output_aliases={len(dils): 0},
        scratch_shapes=[pltpu.VMEM((TM, 128), F32)],
        compiler_params=_cp(1))(*ds, dqkv6)


def _relbias_grad(a_all, onehot):
    def body(a_ref, oh_ref, o_ref):
        acc = jnp.zeros((N_HEADS, N_BUCKETS), F32)
        for c in range(len(DIL_CONFIGS)):
            av = a_ref[c]
            hi = av.astype(BF)
            lo = (av - hi.astype(F32)).astype(BF)
            acc = acc + _nt(hi, oh_ref[c]) + _nt(lo, oh_ref[c])
        o_ref[...] = acc

    return pl.pallas_call(body, name="relbias_grad", out_shape=_sds((N_HEADS, N_BUCKETS), F32),
                          compiler_params=_cp())(a_all, onehot)


def _headnorm_bwd(dn, o, gv, mv):
    ms = _nn2(o * o, mv) * (1.0 / HEAD_DIM)
    r = lax.rsqrt(ms + EPS)
    nrm = o * r
    dg = jnp.sum(dn * nrm, axis=0, keepdims=True)
    dnn = dn * gv
    do = r * (dnn - nrm * (_nn2(dnn * nrm, mv) * (1.0 / HEAD_DIM)))
    return do, dg


def _mix_bwd_out(dx, gt, tv, w_out, o_sb, o_dil, on_sb, on_dil, g_sb, g_dil, ones_g, seq, carry=None):
    t, d = dx.shape
    per = seq // TM
    nb = t // seq

    def body(dx_ref, gt_ref, t_ref, w_ref, osb, odl, onsb, ondl, gsb, gdl, m_ref,
             dosb, dodl, dgt_ref, dgsb, dgdl, dw_ref):
        m = pl.program_id(0)
        dxv = dx_ref[...]
        dt = (gt_ref[...] * dxv).astype(BF)
        _acc_rows(dgt_ref, jnp.sum(dxv * t_ref[...], axis=0, keepdims=True), m % per == 0)
        mv = m_ref[...]
        don_sb = _nt(dt, w_ref[0:D_GRP, :])
        don_dl = _nt(dt, w_ref[D_GRP:2 * D_GRP, :])
        do1, dg1 = _headnorm_bwd(don_sb, osb[...], gsb[...], mv)
        do2, dg2 = _headnorm_bwd(don_dl, odl[...], gdl[...], mv)
        dosb[...] = do1
        dodl[...] = do2
        _acc_rows(dgsb, dg1, m == 0)
        _acc_rows(dgdl, dg2, m == 0)
        p1 = _tn(onsb[...], dt)
        p2 = _tn(ondl[...], dt)

        @pl.when(m == 0)
        def _():
            dw_ref[0:D_GRP, :] = p1
            dw_ref[D_GRP:2 * D_GRP, :] = p2

        @pl.when(m != 0)
        def _():
            dw_ref[0:D_GRP, :] += p1
            dw_ref[D_GRP:2 * D_GRP, :] += p2

    row = pl.BlockSpec((TM, d), lambda m: (m, 0))
    half = pl.BlockSpec((TM, D_GRP), lambda m: (m, 0))
    ex = pl.BlockSpec((None, 1, d), lambda m: (m // per, 0, 0))
    gvec = pl.BlockSpec((1, D_GRP), lambda m: (0, 0))
    wblk = pl.BlockSpec((2 * D_GRP, d), lambda m: (0, 0))
    return _call(
        body, "mix_bwd_out", (t // TM,),
        [row, ex, row, wblk, half, half, half, half, gvec, gvec, pl.BlockSpec((D_GRP, D_GRP), lambda m: (0, 0))],
        [half, half, ex, gvec, gvec, wblk],
        [_sds((t, D_GRP), F32), _sds((t, D_GRP), F32), _sds((nb, 1, d), F32),
         _sds((1, D_GRP), F32), _sds((1, D_GRP), F32), _sds((2 * D_GRP, d), F32)],
        (dx, gt, tv, w_out, o_sb, o_dil, on_sb, on_dil, g_sb, g_dil, ones_g), carry=carry)


def _dw_in(h, dqkv6, carry=None):
    t, d = h.shape
    wc = 6 * D_GRP // N_CHIPS

    def body(h_ref, g_ref, o_ref):
        kt = pl.program_id(0)
        hv = h_ref[...]
        for j in range(N_CHIPS):
            p = _tn(hv, _chip_cols(g_ref, j, wc))

            @pl.when(kt == 0)
            def _(p=p, j=j):
                o_ref[j, 0] = p

            @pl.when(kt != 0)
            def _(p=p, j=j):
                o_ref[j, 0] += p

    return _call(
        body, "dw_in", (t // TM,),
        [pl.BlockSpec((TM, d), lambda kt: (kt, 0)), pl.BlockSpec((6, TM, D_GRP), lambda kt: (0, kt, 0))],
        [pl.BlockSpec((N_CHIPS, 1, d, wc), lambda kt: (0, 0, 0, 0))],
        [_sds((N_CHIPS, 1, d, wc), F32)], (h, dqkv6), carry=carry)


def _mix_bwd_dh(dqkv6, w_in, x, g, sc, dxo, seq, carry=None):
    _, t, _ = dqkv6.shape
    d = x.shape[-1]
    wc = w_in.shape[-1]
    per = seq // TM
    nb = t // seq

    def body(g6_ref, w_ref, x_ref, g_ref, sc_ref, dxo_ref, dx_ref, dsh_ref, dsc_ref, dg_ref):
        m = pl.program_id(0)
        dh = _nt(_chip_cols(g6_ref, 0, wc), w_ref[0, 0])
        for j in range(1, N_CHIPS):
            dh = dh + _nt(_chip_cols(g6_ref, j, wc), w_ref[j, 0])
        dx, dsh, dsc, dg = _modnorm_bwd_tile(dh, x_ref[...], g_ref[...], sc_ref[...], dxo_ref[...])
        dx_ref[...] = dx
        _acc_rows(dsh_ref, dsh, m % per == 0)
        _acc_rows(dsc_ref, dsc, m % per == 0)
        _acc_rows(dg_ref, dg, m == 0)

    row = pl.BlockSpec((TM, d), lambda m: (m, 0))
    ex = pl.BlockSpec((None, 1, d), lambda m: (m // per, 0, 0))
    vec = pl.BlockSpec((1, d), lambda m: (0, 0))
    return _call(
        body, "mix_bwd_dh", (t // TM,),
        [pl.BlockSpec((6, TM, D_GRP), lambda m: (0, m, 0)), _whole(w_in), row, vec, ex, row],
        [row, ex, ex, vec],
        [_sds((t, d), F32), _sds((nb, 1, d), F32), _sds((nb, 1, d), F32), _sds((1, d), F32)],
        (dqkv6, w_in, x, g, sc, dxo), carry=carry)


def _ffn_down_loss(s, wd, x, gt, seq, coef, g, target):
    _, t, fs = s.shape
    d = x.shape[-1]
    per = seq // TM
    steps = t // TM

    def body(s_ref, w_ref, x_ref, gt_ref, g_ref, t_ref, f_ref, dx_ref, dg_ref, loss_ref, lacc):
        m = pl.program_id(0)
        f = _nn(s_ref[0], w_ref[0, 0])
        for j in range(1, N_CHIPS):
            f = f + _nn(s_ref[j], w_ref[j, 0])
        f_ref[...] = f
        xv = x_ref[...] + (coef * gt_ref[...]) * f
        gv = g_ref[...]
        r = lax.rsqrt(jnp.mean(xv * xv, axis=-1, keepdims=True) + EPS)
        n = xv * r
        err = n * gv - t_ref[...]
        dy = err * (1.0 / d)
        _acc_rows(dg_ref, jnp.sum(dy * n, axis=0, keepdims=True), m == 0)
        dn = dy * gv
        dx_ref[...] = r * (dn - n * jnp.mean(dn * n, axis=-1, keepdims=True))
        _acc_rows(lacc, jnp.sum(err * err, axis=0, keepdims=True), m == 0)

        @pl.when(m == steps - 1)
        def _():
            tot = jnp.sum(lacc[...], axis=-1, keepdims=True) * (0.5 / d)
            loss_ref[...] = jnp.broadcast_to(tot, (1, 128))

    row = pl.BlockSpec((TM, d), lambda m: (m, 0))
    vec = pl.BlockSpec((1, d), lambda m: (0, 0))
    return pl.pallas_call(
        body, name="ffn_down_loss", grid=(steps,),
        in_specs=[pl.BlockSpec((N_CHIPS, TM, fs), lambda m: (0, m, 0)), _whole(wd), row,
                  pl.BlockSpec((None, 1, d), lambda m: (m // per, 0, 0)), vec, row],
        out_specs=[row, row, vec, pl.BlockSpec((1, 128), lambda m: (0, 0))],
        out_shape=[_sds((t, d), F32), _sds((t, d), F32), _sds((1, d), F32), _sds((1, 128), F32)],
        scratch_shapes=[pltpu.VMEM((1, d), F32)],
        compiler_params=_cp(1))(s, wd, x, gt, g, target)


def _row_tile(rows, cols):
    best = rows
    for tr in range(8, rows + 1, 8):
        if rows % tr == 0 and tr * cols * 4 <= (1 << 20):
            best = tr
    if best * cols * 4 > (1 << 21):
        best = 8
    return best


def _adamw(w, g_arr, g_sel, m, v):
    rows, cols = w.shape
    tr = _row_tile(rows, cols)
    b1c = 1.0 - ADAM_B1 ** ADAM_STEP
    b2c = 1.0 - ADAM_B2 ** ADAM_STEP

    def body(w_ref, g_ref, m_ref, v_ref, go_ref, d_ref, mo_ref, vo_ref):
        gv = g_ref[...]
        mn = ADAM_B1 * m_ref[...] + (1.0 - ADAM_B1) * gv
        vn = ADAM_B2 * v_ref[...] + (1.0 - ADAM_B2) * (gv * gv)
        go_ref[...] = gv
        mo_ref[...] = mn
        vo_ref[...] = vn
        d_ref[...] = -ADAM_LR * ((mn / b1c) / (jnp.sqrt(vn / b2c) + ADAM_EPS) + ADAM_WD * w_ref[...])

    blk = pl.BlockSpec((tr, cols), lambda i: (i, 0))
    shp = _sds((rows, cols), F32)
    return pl.pallas_call(
        body, name="adamw", grid=(rows // tr,),
        in_specs=[blk, pl.BlockSpec((None, tr, cols), lambda i: (g_sel, i, 0)), blk, blk],
        out_specs=[blk] * 4, out_shape=[shp] * 4,
        compiler_params=_cp(1))(w, g_arr, m, v)


def _flip(v, bit):
    return 1 - v if bit else v


def _my_place():
    x, y, c = lax.axis_index("x"), lax.axis_index("y"), lax.axis_index("c")
    return x, y, c


class _Exchange:
    def __init__(self, operands, out_shape, aliases, sems, start, finish):
        self.operands, self.out_shape, self.aliases, self.sems = list(operands), list(out_shape), dict(aliases), list(sems)
        self.start, self.finish = start, finish


def _join(exchanges):
    exchanges = [e for e in exchanges if e is not None]
    if not exchanges:
        return None
    ops, outs, sems, aliases, spans = [], [], [], {}, []
    for e in exchanges:
        spans.append((len(ops), len(outs), len(sems), e))
        for i, j in e.aliases.items():
            aliases[len(ops) + i] = len(outs) + j
        ops += e.operands
        outs += e.out_shape
        sems += e.sems

    def run(which):
        def go(ins, res, sm):
            for io, oo, so, e in spans:
                getattr(e, which)(ins[io:io + len(e.operands)], res[oo:oo + len(e.out_shape)], sm[so:so + len(e.sems)])
        return go

    return _Exchange(ops, outs, aliases, sems, run("start"), run("finish"))


def _call(body, name, grid, in_specs, out_specs, out_shape, args, scratch=(), carry=None, io_alias=None):
    in_specs, out_specs, out_shape, scratch = list(in_specs), list(out_specs), list(out_shape), list(scratch)
    io_alias = dict(io_alias or {})
    if carry is None:
        return pl.pallas_call(body, name=name, grid=grid, in_specs=in_specs, out_specs=out_specs,
                              out_shape=out_shape, scratch_shapes=scratch, input_output_aliases=io_alias,
                              compiler_params=_cp(len(grid)))(*args)
    n_in, n_out, n_s = len(in_specs), len(out_specs), len(scratch)
    c_in, c_out = len(carry.operands), len(carry.out_shape)
    any_spec = pl.BlockSpec(memory_space=pl.ANY)

    def wrapped(*refs):
        ins, cins = refs[:n_in], refs[n_in:n_in + c_in]
        o0 = n_in + c_in
        outs, couts = refs[o0:o0 + n_out], refs[o0 + n_out:o0 + n_out + c_out]
        s0 = o0 + n_out + c_out
        scr, sems = refs[s0:s0 + n_s], refs[s0 + n_s:]
        first = pl.program_id(0) == 0
        last = pl.program_id(0) == grid[0] - 1
        for ax in range(1, len(grid)):
            first = jnp.logical_and(first, pl.program_id(ax) == 0)
            last = jnp.logical_and(last, pl.program_id(ax) == grid[ax] - 1)

        @pl.when(first)
        def _():
            carry.start(cins, couts, sems)

        body(*ins, *outs, *scr)

        @pl.when(last)
        def _():
            carry.finish(cins, couts, sems)

    return pl.pallas_call(
        wrapped, name=name, grid=grid, in_specs=in_specs + [any_spec] * c_in,
        out_specs=out_specs + [any_spec] * c_out, out_shape=out_shape + carry.out_shape,
        scratch_shapes=scratch + carry.sems,
        input_output_aliases={**io_alias, **{n_in + i: n_out + j for i, j in carry.aliases.items()}},
        compiler_params=_cp(len(grid)))(*args, *carry.operands)


def _whole_call(body, name, args, out_shape, scratch, carry=None):
    vm = pl.BlockSpec(memory_space=pltpu.VMEM)
    any_spec = pl.BlockSpec(memory_space=pl.ANY)
    out_shape, scratch = list(out_shape), list(scratch)
    n_in, n_out, n_s = len(args), len(out_shape), len(scratch)
    if carry is None:
        return pl.pallas_call(body, name=name, in_specs=[vm] * n_in, out_specs=[vm] * n_out, out_shape=out_shape,
                              scratch_shapes=scratch, compiler_params=_cp())(*args)
    c_in, c_out = len(carry.operands), len(carry.out_shape)

    def wrapped(*refs):
        ins, cins = refs[:n_in], refs[n_in:n_in + c_in]
        o0 = n_in + c_in
        outs, couts = refs[o0:o0 + n_out], refs[o0 + n_out:o0 + n_out + c_out]
        s0 = o0 + n_out + c_out
        scr, sems = refs[s0:s0 + n_s], refs[s0 + n_s:]
        carry.start(cins, couts, sems)
        body(*ins, *outs, *scr)
        carry.finish(cins, couts, sems)

    return pl.pallas_call(
        wrapped, name=name, in_specs=[vm] * n_in + [any_spec] * c_in, out_specs=[vm] * n_out + [any_spec] * c_out,
        out_shape=out_shape + carry.out_shape, scratch_shapes=scratch + carry.sems,
        input_output_aliases={n_in + i: n_out + j for i, j in carry.aliases.items()},
        compiler_params=_cp())(*args, *carry.operands)


def _alone(name, ex):
    any_spec = pl.BlockSpec(memory_space=pl.ANY)
    c_in, c_out = len(ex.operands), len(ex.out_shape)

    def body(*refs):
        ins, outs, sems = refs[:c_in], refs[c_in:c_in + c_out], refs[c_in + c_out:]
        ex.start(ins, outs, sems)
        ex.finish(ins, outs, sems)

    return pl.pallas_call(
        body, name=name, in_specs=[any_spec] * c_in, out_specs=[any_spec] * c_out, out_shape=ex.out_shape,
        scratch_shapes=ex.sems, input_output_aliases=ex.aliases, compiler_params=_cp())(*ex.operands)


def _ada_fwd(c_pad, w_ada, b_shard, carry=None):
    d = c_pad.shape[-1]
    cols = w_ada.shape[-1]
    chunk = 384

    def body(c_ref, w_ref, b_ref, call_ref, mod_ref, part, s1, r1, s2, r2):
        x, y, c = _my_place()
        dev = 4 * x + 2 * y + c
        chip = 2 * x + y
        call_ref[dev] = c_ref[...]

        def c_copy(k):
            px, py, pc = _flip(x, (k >> 2) & 1), _flip(y, (k >> 1) & 1), _flip(c, k & 1)
            return px, py, pc

        sends = []
        for k in range(1, N_DEV):
            px, py, pc = c_copy(k)
            cp = pltpu.make_async_remote_copy(src_ref=c_ref, dst_ref=call_ref.at[dev], send_sem=s1.at[k - 1],
                                              recv_sem=r1.at[k - 1], device_id=(px, py, pc), device_id_type=MESH)
            cp.start()
            sends.append(cp)
        for k in range(1, N_DEV):
            px, py, pc = c_copy(k)
            pltpu.make_async_remote_copy(src_ref=c_ref, dst_ref=call_ref.at[4 * px + 2 * py + pc],
                                         send_sem=s1.at[k - 1], recv_sem=r1.at[k - 1],
                                         device_id=(px, py, pc), device_id_type=MESH).wait_recv()
        for cp in sends:
            cp.wait_send()

        cs = call_ref[...].reshape(N_DEV * 8, d)
        sc = (cs * jax.nn.sigmoid(cs)).astype(BF)
        for n0 in range(0, cols, chunk):
            blk = _nn(sc, w_ref[:, n0:n0 + chunk].astype(BF)) + b_ref[:, n0:n0 + chunk]
            part[:, :, n0:n0 + chunk] = blk.reshape(N_DEV, 8, chunk)

        mod_ref[chip] = part[dev]
        sends = []
        for kk in range(1, N_CHIPS):
            px, py = _flip(x, (kk >> 1) & 1), _flip(y, kk & 1)
            cp = pltpu.make_async_remote_copy(src_ref=part.at[4 * px + 2 * py + c], dst_ref=mod_ref.at[chip],
                                              send_sem=s2.at[kk - 1], recv_sem=r2.at[kk - 1],
                                              device_id=(px, py, c), device_id_type=MESH)
            cp.start()
            sends.append(cp)
        for kk in range(1, N_CHIPS):
            px, py = _flip(x, (kk >> 1) & 1), _flip(y, kk & 1)
            pltpu.make_async_remote_copy(src_ref=part.at[dev], dst_ref=mod_ref.at[2 * px + py],
                                         send_sem=s2.at[kk - 1], recv_sem=r2.at[kk - 1],
                                         device_id=(px, py, c), device_id_type=MESH).wait_recv()
        for cp in sends:
            cp.wait_send()

    return _whole_call(
        body, "ada_fwd", (c_pad, w_ada, b_shard),
        [_sds((N_DEV, 8, d), F32), _sds((N_CHIPS, 8, cols), F32)],
        [pltpu.VMEM((N_DEV, 8, cols), F32),
         pltpu.SemaphoreType.DMA((N_DEV - 1,)), pltpu.SemaphoreType.DMA((N_DEV - 1,)),
         pltpu.SemaphoreType.DMA((N_CHIPS - 1,)), pltpu.SemaphoreType.DMA((N_CHIPS - 1,))], carry=carry)


def _ag_weights(bufs, kks=(1, 2, 3), relative=False):
    n, nk = len(bufs), len(kks)

    def half(b, which):
        hr = bufs[b].shape[2] // 2
        return pl.ds(pl.multiple_of(which * hr, 16), hr)

    def copies(outs, sems, b, i, kk):
        x, y, c = _my_place()
        chip = 2 * x + y
        px, py = _flip(x, (kk >> 1) & 1), _flip(y, kk & 1)
        mine, theirs = (0, kk) if relative else (chip, 2 * px + py)
        landing = kk if relative else chip
        k = nk * b + i
        send = pltpu.make_async_remote_copy(
            src_ref=outs[b].at[mine, :, half(b, c), :], dst_ref=outs[b].at[landing, :, half(b, c), :],
            send_sem=sems[0].at[k], recv_sem=sems[1].at[k], device_id=(px, py, c), device_id_type=MESH)
        got = outs[b].at[theirs, :, half(b, c), :]
        recv = pltpu.make_async_remote_copy(
            src_ref=got, dst_ref=got, send_sem=sems[0].at[k], recv_sem=sems[1].at[k],
            device_id=(px, py, c), device_id_type=MESH)
        fwd = pltpu.make_async_remote_copy(
            src_ref=got, dst_ref=got, send_sem=sems[2].at[k], recv_sem=sems[3].at[k],
            device_id=(x, y, 1 - c), device_id_type=MESH)
        other = outs[b].at[theirs, :, half(b, 1 - c), :]
        back = pltpu.make_async_remote_copy(
            src_ref=other, dst_ref=other, send_sem=sems[2].at[k], recv_sem=sems[3].at[k],
            device_id=(x, y, 1 - c), device_id_type=MESH)
        return send, recv, fwd, back

    def each(outs, sems):
        for b in range(n):
            for i, kk in enumerate(kks):
                yield copies(outs, sems, b, i, kk)

    def start(ins, outs, sems):
        for send, _, _, _ in each(outs, sems):
            send.start()

    def finish(ins, outs, sems):
        for _, recv, fwd, _ in each(outs, sems):
            recv.wait_recv()
            fwd.start()
        for send, _, fwd, back in each(outs, sems):
            back.wait_recv()
            send.wait_send()
            fwd.wait_send()

    return _Exchange(bufs, [_sds(s.shape, s.dtype) for s in bufs], {i: i for i in range(n)},
                     [pltpu.SemaphoreType.DMA((nk * n,))] * 4, start, finish)


def _rs_d2d(grads):
    n = len(grads)

    def copy(ins, outs, sems, b):
        x, y, c = _my_place()
        hr = grads[b].shape[2] // 2
        theirs = pl.ds(pl.multiple_of((1 - c) * hr, 8), hr)
        return pltpu.make_async_remote_copy(
            src_ref=ins[b].at[:, :, theirs, :], dst_ref=outs[b], send_sem=sems[0].at[b], recv_sem=sems[1].at[b],
            device_id=(x, y, 1 - c), device_id_type=MESH)

    def start(ins, outs, sems):
        for b in range(n):
            copy(ins, outs, sems, b).start()

    def finish(ins, outs, sems):
        for b in range(n):
            copy(ins, outs, sems, b).wait()

    return _Exchange(grads, [_sds(g.shape[:2] + (g.shape[2] // 2, g.shape[3]), F32) for g in grads], {},
                     [pltpu.SemaphoreType.DMA((n,))] * 2, start, finish)


def _add_halves(core, g, land):
    nchip, ng, rows, cols = g.shape
    hr = rows // 2
    tr = _row_tile(hr, cols)
    steps = hr // tr

    def body(core_ref, g_ref, l_ref, o_ref):
        del core_ref
        o_ref[...] = (g_ref[...] + l_ref[...]).astype(BF)

    return pl.pallas_call(
        body, name="add_halves",
        grid_spec=pltpu.PrefetchScalarGridSpec(
            num_scalar_prefetch=1, grid=(nchip, ng, steps),
            in_specs=[pl.BlockSpec((None, None, tr, cols), lambda j, a, i, cr: (j, a, cr[0] * steps + i, 0)),
                      pl.BlockSpec((None, None, tr, cols), lambda j, a, i, cr: (j, a, i, 0))],
            out_specs=pl.BlockSpec((None, None, tr, cols), lambda j, a, i, cr: (j, a, i, 0))),
        out_shape=_sds((nchip, ng, hr, cols), BF),
        compiler_params=_cp(3))(core, g, land)


def _rs_ici(parts, relative=False):
    n = len(parts)

    def copies(ins, outs, sems):
        x, y, c = _my_place()
        chip = 2 * x + y
        for b in range(n):
            for kk in range(1, N_CHIPS):
                px, py = _flip(x, (kk >> 1) & 1), _flip(y, kk & 1)
                k = 3 * b + kk - 1
                theirs, landing = (kk, kk) if relative else (2 * px + py, chip)
                send = pltpu.make_async_remote_copy(
                    src_ref=ins[b].at[theirs], dst_ref=outs[b].at[landing],
                    send_sem=sems[0].at[k], recv_sem=sems[1].at[k], device_id=(px, py, c), device_id_type=MESH)
                slot = outs[b].at[theirs]
                recv = pltpu.make_async_remote_copy(
                    src_ref=slot, dst_ref=slot, send_sem=sems[0].at[k], recv_sem=sems[1].at[k],
                    device_id=(px, py, c), device_id_type=MESH)
                yield send, recv

    def start(ins, outs, sems):
        for send, _ in copies(ins, outs, sems):
            send.start()

    def finish(ins, outs, sems):
        for send, recv in copies(ins, outs, sems):
            recv.wait_recv()
            send.wait_send()

    return _Exchange(parts, [_sds(p.shape, p.dtype) for p in parts], {},
                     [pltpu.SemaphoreType.DMA((3 * n,))] * 2, start, finish)


def _sum_chips(place, part, land, relative=False):
    nchip, ng, hr, cols = land.shape
    tr = _row_tile(hr, cols)
    steps = hr // tr

    def body(place_ref, p_ref, l1, l2, l3, o_ref):
        del place_ref
        o_ref[...] = ((p_ref[...].astype(F32) + l1[...].astype(F32)) + l2[...].astype(F32)) + l3[...].astype(F32)

    def slot(k):
        if relative:
            return pl.BlockSpec((None, None, tr, cols), lambda a, i, pr: (k, a, i, 0))
        return pl.BlockSpec((None, None, tr, cols), lambda a, i, pr: (jnp.bitwise_xor(pr[1], k), a, i, 0))

    return pl.pallas_call(
        body, name="sum_chips",
        grid_spec=pltpu.PrefetchScalarGridSpec(
            num_scalar_prefetch=1, grid=(ng, steps),
            in_specs=[slot(0), slot(1), slot(2), slot(3)],
            out_specs=pl.BlockSpec((None, tr, cols), lambda a, i, pr: (a, pr[0] * steps + i, 0))),
        out_shape=_sds((ng, 2 * hr, cols), F32),
        compiler_params=_cp(2))(place, part, land, land, land)


def _rs_final(bufs):
    n = len(bufs)

    def copy(outs, sems, b, which):
        x, y, c = _my_place()
        hr = bufs[b].shape[1] // 2
        rows = outs[b].at[:, pl.ds(pl.multiple_of((c if which == 0 else 1 - c) * hr, 8), hr), :]
        return pltpu.make_async_remote_copy(
            src_ref=rows, dst_ref=rows, send_sem=sems[0].at[b], recv_sem=sems[1].at[b],
            device_id=(x, y, 1 - c), device_id_type=MESH)

    def start(ins, outs, sems):
        for b in range(n):
            copy(outs, sems, b, 0).start()

    def finish(ins, outs, sems):
        for b in range(n):
            copy(outs, sems, b, 0).wait_send()
            copy(outs, sems, b, 1).wait_recv()

    return _Exchange(bufs, [_sds(h.shape, F32) for h in bufs], {i: i for i in range(n)},
                     [pltpu.SemaphoreType.DMA((n,))] * 2, start, finish)


def _small_sync(smalls, dmod_blk, c_all, carry=None):
    d = c_all.shape[-1]
    cols = dmod_blk.shape[-1]
    chunk = 384

    def body(sm_ref, dm_ref, c_ref, sum_ref, gw_ref, sm_all, dm_all, ssem, rsem):
        x, y, c = _my_place()
        dev = 4 * x + 2 * y + c
        chip = 2 * x + y
        sm_all[dev] = sm_ref[...]
        dm_all[dev] = dm_ref[chip]
        sends = []
        for k in range(1, N_DEV):
            px, py, pc = _flip(x, (k >> 2) & 1), _flip(y, (k >> 1) & 1), _flip(c, k & 1)
            a = pltpu.make_async_remote_copy(src_ref=sm_ref, dst_ref=sm_all.at[dev], send_sem=ssem.at[2 * (k - 1)],
                                             recv_sem=rsem.at[2 * (k - 1)], device_id=(px, py, pc),
                                             device_id_type=MESH)
            b = pltpu.make_async_remote_copy(src_ref=dm_ref.at[2 * px + py], dst_ref=dm_all.at[dev],
                                             send_sem=ssem.at[2 * (k - 1) + 1], recv_sem=rsem.at[2 * (k - 1) + 1],
                                             device_id=(px, py, pc), device_id_type=MESH)
            a.start()
            b.start()
            sends += [a, b]
        for k in range(1, N_DEV):
            px, py, pc = _flip(x, (k >> 2) & 1), _flip(y, (k >> 1) & 1), _flip(c, k & 1)
            pdev = 4 * px + 2 * py + pc
            pltpu.make_async_remote_copy(src_ref=sm_ref, dst_ref=sm_all.at[pdev], send_sem=ssem.at[2 * (k - 1)],
                                         recv_sem=rsem.at[2 * (k - 1)], device_id=(px, py, pc),
                                         device_id_type=MESH).wait_recv()
            pltpu.make_async_remote_copy(src_ref=dm_ref.at[chip], dst_ref=dm_all.at[pdev],
                                         send_sem=ssem.at[2 * (k - 1) + 1], recv_sem=rsem.at[2 * (k - 1) + 1],
                                         device_id=(px, py, pc), device_id_type=MESH).wait_recv()
        for cp in sends:
            cp.wait_send()

        tot = sm_all[0]
        for q in range(1, N_DEV):
            tot = tot + sm_all[q]
        sum_ref[...] = tot

        cs = c_ref[...].reshape(N_DEV * 8, d)
        sc = (cs * jax.nn.sigmoid(cs)).astype(BF)
        for n0 in range(0, cols, chunk):
            dmv = dm_all[:, :, n0:n0 + chunk].reshape(N_DEV * 8, chunk).astype(BF)
            gw_ref[:, n0:n0 + chunk] = _tn(sc, dmv)

    return _whole_call(
        body, "small_sync", (smalls, dmod_blk, c_all),
        [_sds(smalls.shape, F32), _sds((d, cols), F32)],
        [pltpu.VMEM((N_DEV,) + smalls.shape, F32), pltpu.VMEM((N_DEV, 8, cols), F32),
         pltpu.SemaphoreType.DMA((2 * (N_DEV - 1),)), pltpu.SemaphoreType.DMA((2 * (N_DEV - 1),))], carry=carry)


def _bucket_onehot():
    maps = np.stack([_bucket_map(dil).reshape(-1) for _, dil in DIL_CONFIGS])
    return (jnp.asarray(maps)[:, None, :] == jnp.arange(N_BUCKETS, dtype=jnp.int32)[None, :, None]).astype(BF)


def _dil_bias(rel_t, onehot):
    def body(r_ref, oh_ref, o_ref):
        rv = r_ref[...]
        hi = rv.astype(BF)
        lo = (rv - hi.astype(F32)).astype(BF)
        for c in range(len(DIL_CONFIGS)):
            o_ref[c] = _nn(hi, oh_ref[c]) + _nn(lo, oh_ref[c])

    return pl.pallas_call(body, name="dil_bias",
                          out_shape=_sds((len(DIL_CONFIGS), N_HEADS, BLOCK * 2 * BLOCK), F32),
                          compiler_params=_cp())(rel_t, onehot)


def _rowsum8(a):
    def body(a_ref, o_ref):
        o_ref[...] = jnp.sum(a_ref[...], axis=0, keepdims=True)

    return pl.pallas_call(body, name="rowsum8", out_shape=_sds((1, a.shape[1]), F32), compiler_params=_cp())(a)


def _local_step(x, mod, target, w, gains, rel_bias, place=None):
    nb, seq, d = x.shape
    t = nb * seq
    dist = place is not None
    core = place[0:1] if dist else None
    x0 = x.reshape(t, d)
    tgt = target.reshape(t, d)
    md = [mod[:, i:i + 1, :] for i in range(N_MOD)]
    sh1, sc1, gt1, sh2, sc2, gt2, sh3, sc3, gt3 = md
    g1, g2, g3 = gains["g_ffn1"], gains["g_mix"], gains["g_ffn2"]
    ones_g = _group_ones()

    def partial_sums(grads, lands):
        return [_add_halves(core, g, l) for g, l in zip(grads, lands)]

    def chip_sums(parts, lands):
        return [_sum_chips(place, p, l, relative=True) for p, l in zip(parts, lands)]

    if dist:
        res = _ffn_up_first(x0, g1, sc1, sh1, w["gu1"], seq,
                            carry=_ag_weights([w["gu1"]], kks=(3,), relative=True))
        h1, pqs, gu1 = res[0], res[1:4], res[4]
        res = _ffn_up_next(1, h1, gu1, pqs, carry=_ag_weights([w["d1"]], kks=(1, 2), relative=True))
        pqs, wd1 = res[:3], res[3]
        res = _ffn_up_next(2, h1, gu1, pqs, carry=_ag_weights([wd1], kks=(3,), relative=True))
        pqs, wd1 = res[:3], res[3]
        res = _ffn_up_next(3, h1, gu1, pqs, carry=_ag_weights([w["win"], w["wout"]], kks=(1, 2)))
        (a1, u1, s1), w_in, w_out = res[:3], res[3], res[4]
        f1, x1, w_in, w_out = _ffn_down(s1, wd1, x0, gt1, seq, 0.5, carry=_ag_weights([w_in, w_out], kks=(3,)))
    else:
        gu1, wd1, w_in, w_out = w["gu1"], w["d1"], w["win"], w["wout"]
        h1, a1, u1, s1 = _ffn_up(x0, g1, sc1, sh1, gu1, seq)
        f1, x1 = _ffn_down(s1, wd1, x0, gt1, seq, 0.5)
    w_out2 = w_out.reshape(2 * D_GRP, d)

    h2, qkv6, qkv_r4, qkv_r16 = _qkv_proj(x1, g2, sc2, sh2, w_in, seq)
    qkv6b = qkv6.reshape(6, nb, seq, D_GRP)
    res = _sb_fwd(qkv6b, gains["g_sb_out"], nb, seq,
                  carry=_ag_weights([w["gu2"], w["d2"]], relative=True) if dist else None)
    o_sb, on_sb = res[:2]
    wgu2, wd2 = res[2:] if dist else (w["gu2"], w["d2"])
    onehot = _bucket_onehot()
    bias = _dil_bias(rel_bias.T, onehot).reshape(len(DIL_CONFIGS), N_HEADS * BLOCK, 2 * BLOCK)
    o_cs, l_cs = [], []
    qkv_rs = [(qkv6b, 3), (qkv_r4, 0), (qkv_r16, 0)]
    for ci, (_, dil) in enumerate(DIL_CONFIGS):
        sub = seq // dil
        arr, base = qkv_rs[ci]
        arr = arr.reshape(base + 3, nb, sub, dil * D_GRP)
        qkv_rs[ci] = (arr, base)
        o_c, l_c = _dil_fwd(arr, base, bias[ci], nb, sub, dil)
        o_cs.append(o_c.reshape(t // dil, dil * D_GRP))
        l_cs.append(l_c.reshape(t // dil, dil * D_GRP))
    o_dil, on_dil = _dil_comb(o_cs, l_cs, gains["g_dil_out"])
    tmix, x2 = _mix_out(on_sb.reshape(t, D_GRP), on_dil, w_out2, x1, gt2, seq)

    h3, a3, u3, s3 = _ffn_up(x2, g3, sc3, sh3, wgu2, seq)
    f3, dx3, dg_final, loss = _ffn_down_loss(s3, wd2, x2, gt3, seq, 0.5, gains["g_final"], tgt)

    da3, du3, df3, dgt3, dx2, dsh3, dsc3, dg3 = _ffn_bwd_x(dx3, gt3, f3, wd2, a3, u3, wgu2, x2, g3, sc3, seq, 0.5)
    grads2 = [_ffn_bwd_w(h3, da3, du3, s3, df3)]

    res = _mix_bwd_out(
        dx2, gt2, tmix, w_out2, o_sb.reshape(t, D_GRP), o_dil, on_sb.reshape(t, D_GRP), on_dil,
        gains["g_sb_out"], gains["g_dil_out"], ones_g, seq, carry=_rs_d2d(grads2) if dist else None)
    do_sb, do_dil, dgt2, dg_sb, dg_dil, dw_out = res[:6]
    parts2 = partial_sums(grads2, res[6:]) if dist else None
    dw_out = dw_out.reshape(N_CHIPS, 1, 2 * D_GRP // N_CHIPS, d)
    res = _sb_bwd(qkv6b, do_sb.reshape(nb, seq, D_GRP), nb, seq,
                  carry=_rs_ici(parts2, relative=True) if dist else None)
    dqkv6 = res[0]
    halves2 = chip_sums(parts2, res[1:]) if dist else None
    dcs = _dil_comb_bwd(do_dil, o_cs, l_cs)
    dsum, a_tiles = [], []
    for ci, (_, dil) in enumerate(DIL_CONFIGS):
        sub = seq // dil
        do_c = dcs[ci].reshape(nb, sub, dil * D_GRP)
        dd_c = dcs[3 + ci].reshape(nb, sub, dil * D_GRP)
        res = _dil_bwd(qkv_rs[ci][0], qkv_rs[ci][1], bias[ci], do_c, dd_c, nb, sub, dil,
                       carry=_rs_final(halves2) if dist and ci == 0 else None)
        if dist and ci == 0:
            grads2 = res[2:]
        dsum.append(res[0].reshape(3, t // dil, dil * D_GRP))
        a_tiles.append(res[1].reshape(N_HEADS, BLOCK * 2 * BLOCK))
    dqkv6 = _dqkv_dil_sum(dsum, dqkv6.reshape(6, t, D_GRP))
    drel = _relbias_grad(jnp.stack(a_tiles), onehot)
    dx1, dsh2, dsc2, dg2 = _mix_bwd_dh(dqkv6, w_in, x1, g2, sc2, dx2, seq)

    da1, du1, df1, dgt1 = _ffn_bwd_ds(dx1, gt1, f1, wd1, a1, u1, seq, 0.5)
    grads1 = [_ffn_bwd_w(h1, da1, du1, s1, df1)]
    res = _dw_in(h2, dqkv6, carry=_rs_d2d(grads1) if dist else None)
    grads_m = [res[0], dw_out]
    parts1 = partial_sums(grads1, res[1:]) if dist else None
    res = _ffn_bwd_dh(da1, du1, gu1, x0, g1, sc1, dx1, seq,
                      carry=_join([_rs_ici(parts1, relative=True), _rs_d2d(grads_m)]) if dist else None)
    dx0, dsh1, dsc1, dg1 = res[:4]
    pending = None
    if dist:
        pending = (chip_sums(parts1, res[4:5]), partial_sums(grads_m, res[5:7]))

    dmod = jnp.concatenate([dsh1, dsc1, dgt1, dsh2, dsc2, dgt2, dsh3, dsc3, dgt3], axis=1)
    return dict(grad_x=dx0.reshape(nb, seq, d), loss=loss[0, 0], dmod=dmod.reshape(nb, N_MOD * d),
                dffn1=grads1[0], dffn2=grads2[0], dwin=grads_m[0], dwout=grads_m[1], pending=pending,
                dg_ffn1=dg1, dg_mix=dg2, dg_ffn2=dg3, dg_final=dg_final, dg_sb=dg_sb, dg_dil=dg_dil,
                drel=drel.T)


_SMALL_ORDER = (("b_ada", N_MOD * 1024), ("g_ffn1", 1024), ("g_mix", 1024), ("g_ffn2", 1024), ("g_final", 1024),
                ("g_sb_out", D_GRP), ("g_dil_out", D_GRP), ("rel_bias", N_BUCKETS * N_HEADS))


def _pack_small(parts, extra=None):
    flat = [parts[name].reshape(-1).astype(F32) for name, _ in _SMALL_ORDER]
    used = sum(sz for _, sz in _SMALL_ORDER)
    pad = SMALL_ROWS * 128 - used
    tail = jnp.zeros((pad,), F32)
    if extra is not None:
        tail = tail.at[0].set(extra)
    return jnp.concatenate(flat + [tail]).reshape(SMALL_ROWS, 128)


def _unpack_small(packed, shapes):
    flat = packed.reshape(-1)
    out, off = {}, 0
    for name, sz in _SMALL_ORDER:
        out[name] = flat[off:off + sz].reshape(shapes[name])
        off += sz
    return out, flat[off]


def kernel(x, c, w_ada, b_ada, g_ffn1, w1_gate, w1_up, w1_down, g_mix, w_in, g_sb_out, g_dil_out, w_out, rel_bias, g_ffn2, w2_gate, w2_up, w2_down, g_final, loss_target, m_w_ada, m_b_ada, m_g_ffn1, m_w1_gate, m_w1_up, m_w1_down, m_g_mix, m_w_in, m_g_sb_out, m_g_dil_out, m_w_out, m_rel_bias, m_g_ffn2, m_w2_gate, m_w2_up, m_w2_down, m_g_final, v_w_ada, v_b_ada, v_g_ffn1, v_w1_gate, v_w1_up, v_w1_down, v_g_mix, v_w_in, v_g_sb_out, v_g_dil_out, v_w_out, v_rel_bias, v_g_ffn2, v_w2_gate, v_w2_up, v_w2_down, v_g_final):
    nb, seq, d = x.shape
    xi, yi, ci = lax.axis_index("x"), lax.axis_index("y"), lax.axis_index("c")
    chip = 2 * xi + yi
    ada_cols = w_ada.shape[-1]

    c_pad = jnp.zeros((8, d), F32).at[:nb].set(c)
    b_shard = lax.dynamic_slice(b_ada, (0, chip * ada_cols), (1, ada_cols))
    shards = dict(gu1=jnp.stack([w1_gate[0], w1_up[0]]), d1=w1_down, win=w_in, wout=w_out,
                  gu2=jnp.stack([w2_gate[0], w2_up[0]]), d2=w2_down)
    bufs = {k: lax.dynamic_update_slice(lax.empty((N_CHIPS,) + s.shape, BF), s.astype(BF)[None],
                                        (chip if k in ("win", "wout") else 0, 0, 0, 0))
            for k, s in shards.items()}
    c_all, mod_blk, bufs["gu1"] = _ada_fwd(c_pad, w_ada[0], b_shard,
                                           carry=_ag_weights([bufs["gu1"]], kks=(1, 2), relative=True))
    mod = jnp.transpose(mod_blk[:, :nb, :], (1, 0, 2)).reshape(nb, N_MOD, d)

    gains = dict(g_ffn1=g_ffn1, g_mix=g_mix, g_ffn2=g_ffn2, g_final=g_final.reshape(1, d),
                 g_sb_out=g_sb_out.reshape(1, D_GRP), g_dil_out=g_dil_out.reshape(1, D_GRP))
    place = jnp.stack([ci, chip]).astype(jnp.int32)
    r = _local_step(x, mod, loss_target, bufs, gains, rel_bias, place)

    dmod = r["dmod"]
    dmod_pad = jnp.zeros((8, N_MOD * d), F32).at[:nb].set(dmod)
    dmod_blk = jnp.transpose(dmod_pad.reshape(8, N_CHIPS, ada_cols), (1, 0, 2))
    small_parts = dict(b_ada=_rowsum8(dmod_pad), g_ffn1=r["dg_ffn1"], g_mix=r["dg_mix"], g_ffn2=r["dg_ffn2"],
                       g_final=r["dg_final"], g_sb_out=r["dg_sb"], g_dil_out=r["dg_dil"], rel_bias=r["drel"])
    halves1, parts_m = r["pending"]
    res = _small_sync(_pack_small(small_parts, r["loss"]), dmod_blk, c_all,
                      carry=_join([_rs_final(halves1), _rs_ici(parts_m)]))
    small_sum, g_wada, gffn1 = res[:3]
    halves_m = [_sum_chips(place, p, l) for p, l in zip(parts_m, res[3:5])]
    gwin, gwout = _alone("rs_last", _rs_final(halves_m))
    gffn2 = r["dffn2"]

    small_w = dict(b_ada=b_ada, g_ffn1=g_ffn1, g_mix=g_mix, g_ffn2=g_ffn2, g_final=g_final,
                   g_sb_out=g_sb_out, g_dil_out=g_dil_out, rel_bias=rel_bias)
    small_m = dict(b_ada=m_b_ada, g_ffn1=m_g_ffn1, g_mix=m_g_mix, g_ffn2=m_g_ffn2, g_final=m_g_final,
                   g_sb_out=m_g_sb_out, g_dil_out=m_g_dil_out, rel_bias=m_rel_bias)
    small_v = dict(b_ada=v_b_ada, g_ffn1=v_g_ffn1, g_mix=v_g_mix, g_ffn2=v_g_ffn2, g_final=v_g_final,
                   g_sb_out=v_g_sb_out, g_dil_out=v_g_dil_out, rel_bias=v_rel_bias)
    shapes = {k: v.shape for k, v in small_w.items()}
    sg, sd, sm, sv = _adamw(_pack_small(small_w), small_sum.reshape(1, SMALL_ROWS, 128), 0,
                            _pack_small(small_m), _pack_small(small_v))
    sg, loss = _unpack_small(sg, shapes)
    sd, _ = _unpack_small(sd, shapes)
    sm, _ = _unpack_small(sm, shapes)
    sv, _ = _unpack_small(sv, shapes)

    big = {}

    def upd(name, w, g_arr, sel, m, v, transposed=False):
        swap = (lambda a: jnp.swapaxes(a, -1, -2)) if transposed else (lambda a: a)
        w2, m2, v2 = [swap(a)[0] for a in (w, m, v)]
        big[name] = [swap(a[None]) for a in _adamw(w2, g_arr, sel, m2, v2)]

    upd("w_ada", w_ada, g_wada.reshape(1, d, ada_cols), 0, m_w_ada, v_w_ada)
    upd("w1_gate", w1_gate, gffn1, 0, m_w1_gate, v_w1_gate, transposed=True)
    upd("w1_up", w1_up, gffn1, 1, m_w1_up, v_w1_up, transposed=True)
    upd("w1_down", w1_down, gffn1, 2, m_w1_down, v_w1_down)
    upd("w_in", w_in, gwin, 0, m_w_in, v_w_in)
    upd("w_out", w_out, gwout, 0, m_w_out, v_w_out)
    upd("w2_gate", w2_gate, gffn2, 0, m_w2_gate, v_w2_gate, transposed=True)
    upd("w2_up", w2_up, gffn2, 1, m_w2_up, v_w2_up, transposed=True)
    upd("w2_down", w2_down, gffn2, 2, m_w2_down, v_w2_down)

    names = ["w_ada", "b_ada", "g_ffn1", "w1_gate", "w1_up", "w1_down", "g_mix", "w_in", "g_sb_out", "g_dil_out",
             "w_out", "rel_bias", "g_ffn2", "w2_gate", "w2_up", "w2_down", "g_final"]
    outs = [loss, r["grad_x"]]
    for k, small in enumerate((sg, sd, sm, sv)):
        for name in names:
            outs.append(big[name][k] if name in big else small[name])
    return tuple(outs)
```

```python
import functools
import math

import numpy as np
import jax
import jax.numpy as jnp
from jax import lax
from jax.experimental import pallas as pl
from jax.experimental.pallas import tpu as pltpu

F32 = jnp.float32
BF = jnp.bfloat16
MESH = pl.DeviceIdType.MESH

HEAD_DIM = 64
N_HEADS = 8
D_GRP = N_HEADS * HEAD_DIM
DIL_CONFIGS = ((128, 1), (512, 4), (2048, 16))
N_STEPS = 128
BLOCK = 128
N_BUCKETS = 32
MAX_DISTANCE = 2048
N_MOD = 9
EPS = 1e-6
NEG_INF = -1e30
SCALE = HEAD_DIM ** -0.5

ADAM_LR = 0.001
ADAM_B1 = 0.9
ADAM_B2 = 0.999
ADAM_EPS = 1e-08
ADAM_WD = 0.01
ADAM_STEP = 10

N_CHIPS = 4
N_DEV = 8
VMEM_LIMIT = 56 * 1024 * 1024
TM = 512
TQ = 256
KB = 256
SMALL_ROWS = 120


def _cp(n_axes=0, **kw):
    sem = ("arbitrary",) * n_axes if n_axes else None
    return pltpu.CompilerParams(dimension_semantics=sem, vmem_limit_bytes=VMEM_LIMIT, **kw)


def _nn(a, b):
    return jnp.dot(a, b, preferred_element_type=F32)


def _nt(a, b):
    return lax.dot_general(a, b, (((1,), (1,)), ((), ())), preferred_element_type=F32)


def _tn(a, b):
    return lax.dot_general(a, b, (((0,), (0,)), ((), ())), preferred_element_type=F32)


def _nn2(x, m):
    hi = x.astype(BF)
    lo = (x - hi.astype(F32)).astype(BF)
    r = _nn(jnp.concatenate([hi, lo], axis=0), m)
    return r[:x.shape[0]] + r[x.shape[0]:]


def _softplus(z):
    return jnp.maximum(z, 0.0) + jnp.log1p(jnp.exp(-jnp.abs(z)))


def _sds(shape, dtype):
    return jax.ShapeDtypeStruct(shape, dtype)


def _whole(a):
    nd = a.ndim
    return pl.BlockSpec(a.shape, lambda *_: (0,) * nd, pipeline_mode=pl.Buffered(1))


def _modnorm_bwd_tile(dh, xv, gv, scv, dxo):
    r = lax.rsqrt(jnp.mean(xv * xv, axis=-1, keepdims=True) + EPS)
    n = xv * r
    ng = n * gv
    dsh = jnp.sum(dh, axis=0, keepdims=True)
    dsc = jnp.sum(dh * ng, axis=0, keepdims=True)
    dy = dh * (1.0 + scv)
    dg = jnp.sum(dy * n, axis=0, keepdims=True)
    dn = dy * gv
    dx = dxo + r * (dn - n * jnp.mean(dn * n, axis=-1, keepdims=True))
    return dx, dsh, dsc, dg


def _acc_rows(ref, val, first):
    @pl.when(first)
    def _():
        ref[...] = val

    @pl.when(jnp.logical_not(first))
    def _():
        ref[...] += val


def _modnorm_tile(x_ref, g_ref, sc_ref, sh_ref):
    xv = x_ref[...]
    r = lax.rsqrt(jnp.mean(xv * xv, axis=-1, keepdims=True) + EPS)
    return (((xv * r) * g_ref[...]) * (1.0 + sc_ref[...]) + sh_ref[...]).astype(BF)


def _ffn_up(x, g, sc, sh, wgu, seq, carry=None):
    t, d = x.shape
    fs = wgu.shape[-1]
    per = seq // TM

    def body(x_ref, g_ref, sc_ref, sh_ref, w_ref, h_ref, p_ref, q_ref, s_ref):
        hv = _modnorm_tile(x_ref, g_ref, sc_ref, sh_ref)
        h_ref[...] = hv
        for j in range(N_CHIPS):
            a = _nn(hv, w_ref[j, 0])
            u = _nn(hv, w_ref[j, 1])
            sig = jax.nn.sigmoid(a)
            q = a * sig
            p_ref[j] = (u * (sig * (1.0 + a * (1.0 - sig)))).astype(BF)
            q_ref[j] = q.astype(BF)
            s_ref[j] = (q * u).astype(BF)

    row = pl.BlockSpec((TM, d), lambda m: (m, 0))
    ex = pl.BlockSpec((None, 1, d), lambda m: (m // per, 0, 0))
    blk = pl.BlockSpec((N_CHIPS, TM, fs), lambda m: (0, m, 0))
    return _call(
        body, "ffn_up", (t // TM,),
        [row, pl.BlockSpec((1, d), lambda m: (0, 0)), ex, ex, _whole(wgu)],
        [row, blk, blk, blk],
        [_sds((t, d), BF)] + [_sds((N_CHIPS, t, fs), BF)] * 3,
        (x, g, sc, sh, wgu), carry=carry)


def _ffn_down(s, wd, x, gt, seq, coef, carry=None):
    _, t, fs = s.shape
    d = x.shape[-1]
    per = seq // TM

    def body(s_ref, w_ref, x_ref, gt_ref, f_ref, xo_ref):
        f = _nn(s_ref[0], w_ref[0, 0])
        for j in range(1, N_CHIPS):
            f = f + _nn(s_ref[j], w_ref[j, 0])
        f_ref[...] = f
        xo_ref[...] = x_ref[...] + (coef * gt_ref[...]) * f

    row = pl.BlockSpec((TM, d), lambda m: (m, 0))
    return _call(
        body, "ffn_down", (t // TM,),
        [pl.BlockSpec((N_CHIPS, TM, fs), lambda m: (0, m, 0)), _whole(wd), row,
         pl.BlockSpec((None, 1, d), lambda m: (m // per, 0, 0))],
        [row, row], [_sds((t, d), F32), _sds((t, d), F32)], (s, wd, x, gt), carry=carry)


def _ffn_bwd_ds(dxo, gt, f, wd, p, q, seq, coef, carry=None):
    t, d = dxo.shape
    fs = p.shape[-1]
    per = seq // TM
    nb = t // seq

    def body(dxo_ref, gt_ref, f_ref, w_ref, p_ref, q_ref, da_ref, du_ref, df_ref, dgt_ref):
        m = pl.program_id(0)
        dxv = dxo_ref[...]
        df = ((coef * gt_ref[...]) * dxv).astype(BF)
        df_ref[...] = df
        _acc_rows(dgt_ref, coef * jnp.sum(dxv * f_ref[...], axis=0, keepdims=True), m % per == 0)
        for j in range(N_CHIPS):
            ds = _nt(df, w_ref[j, 0])
            da_ref[j] = (ds * p_ref[j].astype(F32)).astype(BF)
            du_ref[j] = (ds * q_ref[j].astype(F32)).astype(BF)

    row = pl.BlockSpec((TM, d), lambda m: (m, 0))
    blk = pl.BlockSpec((N_CHIPS, TM, fs), lambda m: (0, m, 0))
    ex = pl.BlockSpec((None, 1, d), lambda m: (m // per, 0, 0))
    return _call(
        body, "ffn_bwd_ds", (t // TM,),
        [row, ex, row, _whole(wd), blk, blk],
        [blk, blk, row, ex],
        [_sds((N_CHIPS, t, fs), BF), _sds((N_CHIPS, t, fs), BF), _sds((t, d), BF), _sds((nb, 1, d), F32)],
        (dxo, gt, f, wd, p, q), carry=carry)


TM_X = 256


def _ffn_bwd_x(dxo, gt, f, wd, p, q, wgu, x, g, sc, seq, coef):
    t, d = dxo.shape
    fs = p.shape[-1]
    per = seq // TM_X
    nb = t // seq

    def body(dxo_ref, gt_ref, f_ref, wd_ref, p_ref, q_ref, w_ref, x_ref, g_ref, sc_ref,
             da_ref, du_ref, df_ref, dgt_ref, dx_ref, dsh_ref, dsc_ref, dg_ref):
        m = pl.program_id(0)
        dxv = dxo_ref[...]
        df = ((coef * gt_ref[...]) * dxv).astype(BF)
        df_ref[...] = df
        _acc_rows(dgt_ref, coef * jnp.sum(dxv * f_ref[...], axis=0, keepdims=True), m % per == 0)
        dh = None
        for j in range(N_CHIPS):
            ds = _nt(df, wd_ref[j, 0])
            da = (ds * p_ref[j].astype(F32)).astype(BF)
            du = (ds * q_ref[j].astype(F32)).astype(BF)
            da_ref[j] = da
            du_ref[j] = du
            part = _nt(da, w_ref[j, 0]) + _nt(du, w_ref[j, 1])
            dh = part if dh is None else dh + part
        dx, dsh, dsc, dg = _modnorm_bwd_tile(dh, x_ref[...], g_ref[...], sc_ref[...], dxv)
        dx_ref[...] = dx
        _acc_rows(dsh_ref, dsh, m % per == 0)
        _acc_rows(dsc_ref, dsc, m % per == 0)
        _acc_rows(dg_ref, dg, m == 0)

    row = pl.BlockSpec((TM_X, d), lambda m: (m, 0))
    blk = pl.BlockSpec((N_CHIPS, TM_X, fs), lambda m: (0, m, 0))
    ex = pl.BlockSpec((None, 1, d), lambda m: (m // per, 0, 0))
    vec = pl.BlockSpec((1, d), lambda m: (0, 0))
    exs = _sds((nb, 1, d), F32)
    return pl.pallas_call(
        body, name="ffn_bwd_x", grid=(t // TM_X,),
        in_specs=[row, ex, row, _whole(wd), blk, blk, _whole(wgu), row, vec, ex],
        out_specs=[blk, blk, row, ex, row, ex, ex, vec],
        out_shape=[_sds((N_CHIPS, t, fs), BF), _sds((N_CHIPS, t, fs), BF), _sds((t, d), BF), exs,
                   _sds((t, d), F32), exs, exs, _sds((1, d), F32)],
        compiler_params=_cp(1))(dxo, gt, f, wd, p, q, wgu, x, g, sc)


TK_W = 1024


def _ffn_bwd_w(h, da, du, s, df):
    t, d = h.shape
    fs = da.shape[-1]

    def body(h_ref, da_ref, du_ref, s_ref, df_ref, o_ref):
        kt = pl.program_id(1)
        hv = h_ref[...]
        parts = (_tn(da_ref[...], hv), _tn(du_ref[...], hv), _tn(s_ref[...], df_ref[...]))

        @pl.when(kt == 0)
        def _():
            for i, p in enumerate(parts):
                o_ref[i] = p

        @pl.when(kt != 0)
        def _():
            for i, p in enumerate(parts):
                o_ref[i] += p

    row = pl.BlockSpec((TK_W, d), lambda j, kt: (kt, 0))
    blk = pl.BlockSpec((None, TK_W, fs), lambda j, kt: (j, kt, 0))
    return pl.pallas_call(
        body, name="ffn_bwd_w", grid=(N_CHIPS, t // TK_W),
        in_specs=[row, blk, blk, blk, row],
        out_specs=pl.BlockSpec((None, 3, fs, d), lambda j, kt: (j, 0, 0, 0)),
        out_shape=_sds((N_CHIPS, 3, fs, d), F32),
        compiler_params=_cp(2))(h, da, du, s, df)


def _ffn_bwd_dh(da, du, wgu, x, g, sc, dxo, seq, carry=None):
    _, t, fs = da.shape
    d = x.shape[-1]
    per = seq // TM
    nb = t // seq

    def body(da_ref, du_ref, w_ref, x_ref, g_ref, sc_ref, dxo_ref, dx_ref, dsh_ref, dsc_ref, dg_ref):
        m = pl.program_id(0)
        dh = _nt(da_ref[0], w_ref[0, 0]) + _nt(du_ref[0], w_ref[0, 1])
        for j in range(1, N_CHIPS):
            dh = dh + _nt(da_ref[j], w_ref[j, 0]) + _nt(du_ref[j], w_ref[j, 1])
        dx, dsh, dsc, dg = _modnorm_bwd_tile(dh, x_ref[...], g_ref[...], sc_ref[...], dxo_ref[...])
        dx_ref[...] = dx
        _acc_rows(dsh_ref, dsh, m % per == 0)
        _acc_rows(dsc_ref, dsc, m % per == 0)
        _acc_rows(dg_ref, dg, m == 0)

    row = pl.BlockSpec((TM, d), lambda m: (m, 0))
    blk = pl.BlockSpec((N_CHIPS, TM, fs), lambda m: (0, m, 0))
    ex = pl.BlockSpec((None, 1, d), lambda m: (m // per, 0, 0))
    vec = pl.BlockSpec((1, d), lambda m: (0, 0))
    return _call(
        body, "ffn_bwd_dh", (t // TM,),
        [blk, blk, _whole(wgu), row, vec, ex, row],
        [row, ex, ex, vec],
        [_sds((t, d), F32), _sds((nb, 1, d), F32), _sds((nb, 1, d), F32), _sds((1, d), F32)],
        (da, du, wgu, x, g, sc, dxo), carry=carry)


def _qkv_proj(x, g, sc, sh, w_in, seq, carry=None):
    t, d = x.shape
    wc = w_in.shape[-1]
    per = seq // TM

    dils = [dil for _, dil in DIL_CONFIGS if dil > 1]

    def body(x_ref, g_ref, sc_ref, sh_ref, w_ref, h_ref, o_ref, *rest):
        res_refs, buf = rest[:len(dils)], rest[len(dils)]
        hv = _modnorm_tile(x_ref, g_ref, sc_ref, sh_ref)
        h_ref[...] = hv
        for j in range(N_CHIPS):
            rf = _nn(hv, w_ref[j, 0])
            r = rf.astype(BF)
            for a, lc, off, width in _col_pieces(j, wc):
                o_ref[a, :, lc:lc + width] = r[:, off:off + width]
                if a < 3:
                    continue
                for c0 in range(0, width, 128):
                    cg = (lc + c0) // 128
                    buf[...] = rf[:, off + c0:off + c0 + 128]
                    for ref, dil in zip(res_refs, dils):
                        for rr in range(dil):
                            ref[a - 3, :, rr * D_GRP + cg * 128:rr * D_GRP + (cg + 1) * 128] = (
                                buf[pl.ds(rr, TM // dil, stride=dil), :].astype(BF))

    row = pl.BlockSpec((TM, d), lambda m: (m, 0))
    ex = pl.BlockSpec((None, 1, d), lambda m: (m // per, 0, 0))
    return _call(
        body, "qkv_proj", (t // TM,),
        [row, pl.BlockSpec((1, d), lambda m: (0, 0)), ex, ex, _whole(w_in)],
        [row, pl.BlockSpec((6, TM, D_GRP), lambda m: (0, m, 0))]
        + [pl.BlockSpec((3, TM // dil, dil * D_GRP), lambda m: (0, m, 0)) for dil in dils],
        [_sds((t, d), BF), _sds((6, t, D_GRP), BF)] + [_sds((3, t // dil, dil * D_GRP), BF) for dil in dils],
        (x, g, sc, sh, w_in), scratch=[pltpu.VMEM((TM, 128), F32)], carry=carry)


def _col_pieces(j, wc):
    out, off = [], 0
    while off < wc:
        a, lc = divmod(j * wc + off, D_GRP)
        width = min(D_GRP - lc, wc - off)
        out.append((a, lc, off, width))
        off += width
    return out


def _chip_cols(g6_ref, j, wc):
    return jnp.concatenate([g6_ref[a, :, lc:lc + width] for a, lc, _, width in _col_pieces(j, wc)], axis=1)


def _mix_out(on_sb, on_dil, w_out, x, gt, seq):
    t, d = x.shape
    per = seq // TM

    def body(a_ref, b_ref, w_ref, x_ref, gt_ref, t_ref, xo_ref):
        tv = _nn(a_ref[...], w_ref[0:D_GRP, :]) + _nn(b_ref[...], w_ref[D_GRP:2 * D_GRP, :])
        t_ref[...] = tv
        xo_ref[...] = x_ref[...] + gt_ref[...] * tv

    row = pl.BlockSpec((TM, d), lambda m: (m, 0))
    half = pl.BlockSpec((TM, D_GRP), lambda m: (m, 0))
    return pl.pallas_call(
        body, name="mix_out", grid=(t // TM,),
        in_specs=[half, half, pl.BlockSpec((2 * D_GRP, d), lambda m: (0, 0)), row,
                  pl.BlockSpec((None, 1, d), lambda m: (m // per, 0, 0))],
        out_specs=[row, row],
        out_shape=[_sds((t, d), F32), _sds((t, d), F32)],
        compiler_params=_cp(1))(on_sb, on_dil, w_out, x, gt)


def _sb_masks():
    lane = lax.broadcasted_iota(jnp.int32, (1, 2 * HEAD_DIM), 1)
    hm0 = lane < HEAD_DIM
    rel = lax.broadcasted_iota(jnp.int32, (TQ, KB), 0) - lax.broadcasted_iota(jnp.int32, (TQ, KB), 1)
    kr = lax.broadcasted_iota(jnp.int32, (KB, KB), 0)
    kc = lax.broadcasted_iota(jnp.int32, (KB, KB), 1)
    return hm0, rel, kr, kc


def _stack_pair(x, hm0):
    zero = jnp.zeros_like(x)
    return jnp.concatenate([jnp.where(hm0, x, zero), jnp.where(hm0, zero, x)], axis=0)


def _headnorm_pair(o, gv, hm0):
    o2 = o * o
    ms0 = jnp.sum(jnp.where(hm0, o2, 0.0), axis=-1, keepdims=True) * (1.0 / HEAD_DIM)
    ms1 = jnp.sum(jnp.where(hm0, 0.0, o2), axis=-1, keepdims=True) * (1.0 / HEAD_DIM)
    r = jnp.where(hm0, lax.rsqrt(ms0 + EPS), lax.rsqrt(ms1 + EPS))
    return (o * r) * gv


SB_DEAD = -104.0


def _alive(c_l):
    return (jnp.max(c_l) > SB_DEAD).astype(jnp.int32)


def _sb_fwd(qkv6, g_sb, nb, seq, carry=None):
    nq = seq // TQ

    def body(q_ref, k_ref, v_ref, g_ref, o_ref, on_ref):
        qi = pl.program_id(2)
        hm0, rel, kr, kc = _sb_masks()
        upper = (kr > kc).astype(BF)
        qs = _stack_pair(q_ref[...], hm0)
        causal2 = jnp.concatenate([rel, rel], axis=0) > 0

        def block(kj, causal, c_l, acc):
            ks = pl.multiple_of(kj * KB, KB)
            z = _nt(qs, k_ref[pl.ds(ks, KB), :]) * SCALE
            sp = _softplus(z)
            ln = -sp if causal is None else jnp.where(causal, -sp, 0.0)
            suf = _nn2(ln, upper)
            w = jnp.exp((z - sp) + (suf + c_l))
            if causal is not None:
                w = jnp.where(causal, w, 0.0)
            return c_l + (suf[:, 0:1] + ln[:, 0:1]), acc + _nn(w.astype(BF), v_ref[pl.ds(ks, KB), :])

        c_l, acc = block(qi, causal2, jnp.zeros((2 * TQ, 1), F32), jnp.zeros((2 * TQ, 2 * HEAD_DIM), F32))

        def cond(carry):
            return jnp.logical_and(carry[0] <= qi, carry[1] > 0)

        def kbody(carry):
            it, _, c_l, acc = carry
            c_l, acc = block(qi - it, None, c_l, acc)
            return it + 1, _alive(c_l), c_l, acc

        acc = lax.while_loop(cond, kbody, (jnp.int32(1), _alive(c_l), c_l, acc))[3]
        o = jnp.where(hm0, acc[:TQ], acc[TQ:])
        o_ref[...] = o
        on_ref[...] = _headnorm_pair(o, g_ref[...], hm0).astype(BF)

    w = 2 * HEAD_DIM
    full = lambda i: pl.BlockSpec((None, None, seq, w), lambda b, hp, q: (i, b, 0, hp))
    qblk = pl.BlockSpec((None, None, TQ, w), lambda b, hp, q: (0, b, q, hp))
    oblk = pl.BlockSpec((None, TQ, w), lambda b, hp, q: (b, q, hp))
    return _call(
        body, "sb_fwd", (nb, N_HEADS // 2, nq),
        [qblk, full(1), full(2), pl.BlockSpec((1, w), lambda b, hp, q: (0, hp))],
        [oblk, oblk],
        [_sds((nb, seq, D_GRP), F32), _sds((nb, seq, D_GRP), BF)],
        (qkv6, qkv6, qkv6, g_sb), carry=carry)


def _sb_bwd(qkv6, do, nb, seq, carry=None):
    nq = seq // TQ
    nk = seq // KB

    def body(q_ref, k_ref, v_ref, do_ref, out_ref, dk_acc, dv_acc, g_st, s_st):
        qi = pl.program_id(2)
        hm0, rel, kr, kc = _sb_masks()
        upper = (kr > kc).astype(BF)
        lower = (kr < kc).astype(BF)

        @pl.when(qi == 0)
        def _():
            dk_acc[...] = jnp.zeros_like(dk_acc)
            dv_acc[...] = jnp.zeros_like(dv_acc)

        qs = _stack_pair(q_ref[...], hm0)
        dos = _stack_pair(do_ref[...], hm0).astype(BF)
        causal2 = jnp.concatenate([rel, rel], axis=0) > 0

        def weights(kj, causal, c_l):
            ks = pl.multiple_of(kj * KB, KB)
            vb = v_ref[pl.ds(ks, KB), :]
            z = _nt(qs, k_ref[pl.ds(ks, KB), :]) * SCALE
            sp = _softplus(z)
            ln = -sp if causal is None else jnp.where(causal, -sp, 0.0)
            suf = _nn2(ln, upper)
            lsz = z - sp
            w = jnp.exp(lsz + (suf + c_l))
            if causal is not None:
                w = jnp.where(causal, w, 0.0)
            g_st[kj] = w * _nt(dos, vb)
            s_st[kj] = jnp.exp(lsz)
            dv_acc[pl.ds(ks, KB), :] += _tn(w.astype(BF), dos)
            return c_l + (suf[:, 0:1] + ln[:, 0:1])

        zc = jnp.zeros((2 * TQ, 1), F32)
        c_l = weights(qi, causal2, zc)

        def acond(carry):
            return jnp.logical_and(carry[0] <= qi, carry[1] > 0)

        def abody(carry):
            c_l = weights(qi - carry[0], None, carry[2])
            return carry[0] + 1, _alive(c_l), c_l

        n_used = lax.while_loop(acond, abody, (jnp.int32(1), _alive(c_l), c_l))[0]

        def grads(kj, causal, c_g, dq):
            ks = pl.multiple_of(kj * KB, KB)
            kb = k_ref[pl.ds(ks, KB), :]
            g = g_st[kj]
            sig = s_st[kj]
            pre = _nn(g.astype(BF), lower)
            dz = g * (1.0 - sig) - sig * (pre + c_g)
            if causal is not None:
                dz = jnp.where(causal, dz, 0.0)
            dzb = (dz * SCALE).astype(BF)
            dk_acc[pl.ds(ks, KB), :] += _tn(dzb, qs)
            return c_g + (pre[:, KB - 1:KB] + g[:, KB - 1:KB]), dq + _nn(dzb, kb)

        c_g, dq = lax.fori_loop(qi - n_used + 1, qi, lambda kj, cr: grads(kj, None, *cr),
                                (zc, jnp.zeros((2 * TQ, 2 * HEAD_DIM), F32)))
        _, dq = grads(qi, causal2, c_g, dq)
        dq = jnp.where(hm0, dq[:TQ], dq[TQ:])
        out_ref[0, pl.ds(pl.multiple_of(qi * TQ, TQ), TQ), :] = dq.astype(BF)

        @pl.when(qi == nq - 1)
        def _():
            out_ref[1] = dk_acc[...].astype(BF)
            out_ref[2] = dv_acc[...].astype(BF)

    w = 2 * HEAD_DIM
    full = lambda i: pl.BlockSpec((None, None, seq, w), lambda b, hp, q: (i, b, 0, hp))
    qblk = pl.BlockSpec((None, None, TQ, w), lambda b, hp, q: (0, b, q, hp))
    oblk = pl.BlockSpec((None, TQ, w), lambda b, hp, q: (b, q, hp))
    return _call(
        body, "sb_bwd", (nb, N_HEADS // 2, nq),
        [qblk, full(1), full(2), oblk],
        [pl.BlockSpec((3, None, seq, w), lambda b, hp, q: (0, b, 0, hp))],
        [_sds((6, nb, seq, D_GRP), BF)], (qkv6, qkv6, qkv6, do),
        scratch=[pltpu.VMEM((seq, w), F32), pltpu.VMEM((seq, w), F32),
                 pltpu.VMEM((nk, 2 * TQ, KB), F32), pltpu.VMEM((nk, 2 * TQ, KB), F32)],
        carry=carry)


def _t5_bucket(n):
    max_exact = N_BUCKETS // 2
    nf = np.maximum(n, 1).astype(np.float32)
    large = max_exact + (np.log(nf / max_exact) / math.log(MAX_DISTANCE / max_exact)
                         * (N_BUCKETS - max_exact)).astype(np.int32)
    large = np.minimum(large, N_BUCKETS - 1)
    return np.where(n < max_exact, n, large).astype(np.int32)


def _bucket_map(dilation):
    step = BLOCK + np.arange(BLOCK)[:, None] - np.arange(2 * BLOCK)[None, :]
    return _t5_bucket(np.clip(step, 0, N_STEPS) * dilation)


GRP_HEADS = 4
GRP_W = GRP_HEADS * HEAD_DIM


def _dil_masks():
    lane = lax.broadcasted_iota(jnp.int32, (1, GRP_W), 1)
    heads = [jnp.logical_and(lane >= HEAD_DIM * i, lane < HEAD_DIM * (i + 1)) for i in range(GRP_HEADS)]
    iq = jnp.bitwise_and(lax.broadcasted_iota(jnp.int32, (GRP_HEADS * BLOCK, BLOCK), 0), BLOCK - 1)
    ik = lax.broadcasted_iota(jnp.int32, (GRP_HEADS * BLOCK, BLOCK), 1)
    return heads, ik <= iq, ik >= iq


def _stack_heads(x, heads):
    zero = jnp.zeros_like(x)
    return jnp.concatenate([jnp.where(hm, x, zero) for hm in heads], axis=0)


def _unstack_heads(xs, heads):
    out = xs[0:BLOCK]
    for i in range(1, GRP_HEADS):
        out = jnp.where(heads[i], xs[i * BLOCK:(i + 1) * BLOCK], out)
    return out


def _dil_rows(n):
    rs = pl.multiple_of(n * BLOCK, BLOCK)
    ps = pl.multiple_of(jnp.maximum(n - 1, 0) * BLOCK, BLOCK)
    return pl.ds(rs, BLOCK), pl.ds(ps, BLOCK)


def _dil_probs(qs, kc, kp, b_ref, gi, valid_c, valid_p):
    rows = slice(gi * GRP_HEADS * BLOCK, (gi + 1) * GRP_HEADS * BLOCK)
    zc = _nt(qs, kc) * SCALE + b_ref[rows, BLOCK:2 * BLOCK]
    zp = _nt(qs, kp) * SCALE + b_ref[rows, 0:BLOCK]
    zc = jnp.where(valid_c, zc, NEG_INF)
    zp = jnp.where(valid_p, zp, NEG_INF)
    m = jnp.maximum(jnp.max(zc, axis=-1, keepdims=True), jnp.max(zp, axis=-1, keepdims=True))
    ec = jnp.exp(zc - m)
    ep = jnp.exp(zp - m)
    den = jnp.sum(ec, axis=-1, keepdims=True) + jnp.sum(ep, axis=-1, keepdims=True)
    return ec, ep, den, m


def _dil_fwd(qkv6r, base, bias, nb, sub_len, dilation):
    n_blk = sub_len // BLOCK

    def body(q_ref, k_ref, v_ref, b_ref, o_ref, l_ref):
        heads, valid_c, valid_p0 = _dil_masks()

        def nbody(n, carry):
            cur, prev = _dil_rows(n)
            valid_p = jnp.logical_and(valid_p0, n > 0)
            for gi in range(N_HEADS // GRP_HEADS):
                lanes = slice(gi * GRP_W, (gi + 1) * GRP_W)
                qs = _stack_heads(q_ref[cur, lanes], heads)
                ec, ep, den, m = _dil_probs(qs, k_ref[cur, lanes], k_ref[prev, lanes], b_ref, gi, valid_c, valid_p)
                o = (_nn(ec.astype(BF), v_ref[cur, lanes]) + _nn(ep.astype(BF), v_ref[prev, lanes])) / den
                o_ref[cur, lanes] = _unstack_heads(o, heads)
                l_ref[cur, lanes] = _unstack_heads(jnp.broadcast_to(m + jnp.log(den), o.shape), heads)
            return carry

        lax.fori_loop(0, n_blk, nbody, 0)

    seqblk = lambda i: pl.BlockSpec((None, None, sub_len, D_GRP), lambda b, r: (i, b, 0, r))
    oblk = pl.BlockSpec((None, sub_len, D_GRP), lambda b, r: (b, 0, r))
    shp = _sds((nb, sub_len, dilation * D_GRP), F32)
    return pl.pallas_call(
        body, name="dil_fwd_%d" % dilation, grid=(nb, dilation),
        in_specs=[seqblk(base), seqblk(base + 1), seqblk(base + 2), _whole(bias)],
        out_specs=[oblk, oblk], out_shape=[shp, shp],
        compiler_params=_cp(2))(qkv6r, qkv6r, qkv6r, bias)


def _dil_bwd(qkv6r, base, bias, do_c, dd_c, nb, sub_len, dilation, carry=None):
    n_blk = sub_len // BLOCK

    def body(q_ref, k_ref, v_ref, b_ref, do_ref, dd_ref, out_ref, a_ref, dk_acc, dv_acc):
        heads, valid_c, valid_p0 = _dil_masks()
        first = jnp.logical_and(pl.program_id(0) == 0, pl.program_id(1) == 0)

        @pl.when(first)
        def _():
            a_ref[...] = jnp.zeros_like(a_ref)

        dk_acc[...] = jnp.zeros_like(dk_acc)
        dv_acc[...] = jnp.zeros_like(dv_acc)

        def nbody(n, carry):
            cur, prev = _dil_rows(n)
            valid_p = jnp.logical_and(valid_p0, n > 0)
            for gi in range(N_HEADS // GRP_HEADS):
                lanes = slice(gi * GRP_W, (gi + 1) * GRP_W)
                kc, kp = k_ref[cur, lanes], k_ref[prev, lanes]
                vc, vp = v_ref[cur, lanes], v_ref[prev, lanes]
                qs = _stack_heads(q_ref[cur, lanes], heads)
                dos = _stack_heads(do_ref[cur, lanes], heads).astype(BF)
                dds = jnp.sum(_stack_heads(dd_ref[cur, lanes], heads), axis=-1, keepdims=True) * (1.0 / HEAD_DIM)
                ec, ep, den, _ = _dil_probs(qs, kc, kp, b_ref, gi, valid_c, valid_p)
                inv = 1.0 / den
                pc = ec * inv
                pp = ep * inv
                dzc = pc * (_nt(dos, vc) + dds)
                dzp = pp * (_nt(dos, vp) + dds)
                rows = slice(gi * GRP_HEADS * BLOCK, (gi + 1) * GRP_HEADS * BLOCK)
                a_ref[rows, BLOCK:2 * BLOCK] += dzc
                a_ref[rows, 0:BLOCK] += dzp
                dzcb = (dzc * SCALE).astype(BF)
                dzpb = (dzp * SCALE).astype(BF)
                out_ref[0, cur, lanes] = _unstack_heads(_nn(dzcb, kc) + _nn(dzpb, kp), heads).astype(BF)
                dk_acc[cur, lanes] += _tn(dzcb, qs)
                dk_acc[prev, lanes] += _tn(dzpb, qs)
                dv_acc[cur, lanes] += _tn(pc.astype(BF), dos)
                dv_acc[prev, lanes] += _tn(pp.astype(BF), dos)
            return carry

        lax.fori_loop(0, n_blk, nbody, 0)
        out_ref[1] = dk_acc[...].astype(BF)
        out_ref[2] = dv_acc[...].astype(BF)

    seqblk = lambda i: pl.BlockSpec((None, None, sub_len, D_GRP), lambda b, r: (i, b, 0, r))
    oblk = pl.BlockSpec((None, sub_len, D_GRP), lambda b, r: (b, 0, r))
    return _call(
        body, "dil_bwd_%d" % dilation, (nb, dilation),
        [seqblk(base), seqblk(base + 1), seqblk(base + 2), _whole(bias), oblk, oblk],
        [pl.BlockSpec((3, None, sub_len, D_GRP), lambda b, r: (0, b, 0, r)),
         pl.BlockSpec((N_HEADS * BLOCK, 2 * BLOCK), lambda b, r: (0, 0))],
        [_sds((3, nb, sub_len, dilation * D_GRP), BF), _sds((N_HEADS * BLOCK, 2 * BLOCK), F32)],
        (qkv6r, qkv6r, qkv6r, bias, do_c, dd_c),
        scratch=[pltpu.VMEM((sub_len, D_GRP), F32)] * 2, carry=carry)


def _group_ones():
    idx = np.arange(D_GRP) // HEAD_DIM
    return jnp.asarray((idx[:, None] == idx[None, :]).astype(np.float32), dtype=BF)


def _dil_alphas(l1, l4, l16):
    mx = jnp.maximum(jnp.maximum(l1, l4), l16)
    e1 = jnp.exp(l1 - mx)
    e4 = jnp.exp(l4 - mx)
    e16 = jnp.exp(l16 - mx)
    den = e1 + e4 + e16
    return e1 / den, e4 / den, e16 / den


def _residue_spec(dil):
    return pl.BlockSpec((TM // dil, dil * D_GRP), lambda m: (m, 0))


def _from_residue(src, dil, cg, buf):
    if dil == 1:
        return src[:, cg * 128:(cg + 1) * 128]
    for r in range(dil):
        buf[pl.ds(r, TM // dil, stride=dil), :] = src[:, r * D_GRP + cg * 128:r * D_GRP + (cg + 1) * 128]
    return buf[...]


def _to_residue(dst, dil, cg, buf, val):
    if dil == 1:
        dst[:, cg * 128:(cg + 1) * 128] = val.astype(dst.dtype)
        return
    buf[...] = val
    for r in range(dil):
        dst[:, r * D_GRP + cg * 128:r * D_GRP + (cg + 1) * 128] = (
            buf[pl.ds(r, TM // dil, stride=dil), :].astype(dst.dtype))


def _pair_sum(x, hm0):
    s0 = jnp.sum(jnp.where(hm0, x, 0.0), axis=-1, keepdims=True)
    s1 = jnp.sum(jnp.where(hm0, 0.0, x), axis=-1, keepdims=True)
    return jnp.where(hm0, s0, s1)


def _dil_comb(os, ls, g_dil):
    t = os[0].shape[0]
    dils = [dil for _, dil in DIL_CONFIGS]

    def body(o1, l1, o4, l4, o16, l16, g_ref, o_ref, on_ref, b0, b1, b2, b3):
        hm0 = lax.broadcasted_iota(jnp.int32, (1, 128), 1) < HEAD_DIM
        for cg in range(D_GRP // 128):
            lanes = slice(cg * 128, (cg + 1) * 128)
            ov = [_from_residue(src, dil, cg, buf) for src, dil, buf in zip((o1, o4, o16), dils, (None, b0, b1))]
            lv = [_from_residue(src, dil, cg, buf) for src, dil, buf in zip((l1, l4, l16), dils, (None, b2, b3))]
            a1, a4, a16 = _dil_alphas(*lv)
            o = a1 * ov[0] + a4 * ov[1] + a16 * ov[2]
            o_ref[:, lanes] = o
            on_ref[:, lanes] = _headnorm_pair(o, g_ref[:, lanes], hm0).astype(BF)

    blk = pl.BlockSpec((TM, D_GRP), lambda m: (m, 0))
    specs = [_residue_spec(dil) for dil in dils for _ in range(2)]
    return pl.pallas_call(
        body, name="dil_comb", grid=(t // TM,),
        in_specs=specs + [pl.BlockSpec((1, D_GRP), lambda m: (0, 0))],
        out_specs=[blk, blk],
        out_shape=[_sds((t, D_GRP), F32), _sds((t, D_GRP), BF)],
        scratch_shapes=[pltpu.VMEM((TM, 128), F32)] * 4,
        compiler_params=_cp(1))(os[0], ls[0], os[1], ls[1], os[2], ls[2], g_dil)


def _dil_comb_bwd(do, os, ls):
    t = do.shape[0]
    dils = [dil for _, dil in DIL_CONFIGS]

    def body(do_ref, o1, l1, o4, l4, o16, l16, d1, d4, d16, e1, e4, e16, b0, b1, b2, b3):
        hm0 = lax.broadcasted_iota(jnp.int32, (1, 128), 1) < HEAD_DIM
        for cg in range(D_GRP // 128):
            dov = do_ref[:, cg * 128:(cg + 1) * 128]
            ov = [_from_residue(src, dil, cg, buf) for src, dil, buf in zip((o1, o4, o16), dils, (None, b0, b1))]
            lv = [_from_residue(src, dil, cg, buf) for src, dil, buf in zip((l1, l4, l16), dils, (None, b2, b3))]
            al = _dil_alphas(*lv)
            sbar = al[0] * _pair_sum(dov * ov[0], hm0)
            for a_c, o_c in zip(al[1:], ov[1:]):
                sbar = sbar + a_c * _pair_sum(dov * o_c, hm0)
            for a_c, dil, dref, eref in zip(al, dils, (d1, d4, d16), (e1, e4, e16)):
                _to_residue(dref, dil, cg, b0, a_c * dov)
                _to_residue(eref, dil, cg, b1, -a_c * sbar)

    specs = [_residue_spec(dil) for dil in dils]
    return pl.pallas_call(
        body, name="dil_comb_bwd", grid=(t // TM,),
        in_specs=[pl.BlockSpec((TM, D_GRP), lambda m: (m, 0))] + [sp for sp in specs for _ in range(2)],
        out_specs=specs + specs,
        out_shape=[_sds((t // dil, dil * D_GRP), BF) for dil in dils]
        + [_sds((t // dil, dil * D_GRP), F32) for dil in dils],
        scratch_shapes=[pltpu.VMEM((TM, 128), F32)] * 4,
        compiler_params=_cp(1))(do, os[0], ls[0], os[1], ls[1], os[2], ls[2])


def _dqkv_dil_sum(ds, dqkv6):
    t = dqkv6.shape[1]
    dils = [dil for _, dil in DIL_CONFIGS]

    def body(*refs):
        srcs, o_ref, acc = refs[:len(dils)], refs[len(dils) + 1], refs[len(dils) + 2]
        for a in range(3):
            for cg in range(D_GRP // 128):
                for src, dil in zip(srcs, dils):
                    for r in range(dil):
                        part = src[a, :, r * D_GRP + cg * 128:r * D_GRP + (cg + 1) * 128].astype(F32)
                        rows = pl.ds(r, TM // dil, stride=dil) if dil > 1 else slice(None)
                        if dil == dils[0]:
                            acc[rows, :] = part
                        else:
                            acc[rows, :] += part
                o_ref[a, :, cg * 128:(cg + 1) * 128] = acc[...].astype(BF)

    return pl.pallas_call(
        body, name="dqkv_dil_sum", grid=(t // TM,),
        in_specs=[pl.BlockSpec((3, TM // dil, dil * D_GRP), lambda m: (0, m, 0)) for dil in dils]
        + [pl.BlockSpec(memory_space=pl.ANY)],
        out_specs=pl.BlockSpec((3, TM, D_GRP), lambda m: (1, m, 0)),
        out_shape=_sds((6, t, D_GRP), BF), input_output_aliases={len(dils): 0},
        scratch_shapes=[pltpu.VMEM((TM, 128), F32)],
        compiler_params=_cp(1))(*ds, dqkv6)


def _relbias_grad(a_all, onehot):
    def body(a_ref, oh_ref, o_ref):
        acc = jnp.zeros((N_HEADS, N_BUCKETS), F32)
        for c in range(len(DIL_CONFIGS)):
            av = a_ref[c]
            hi = av.astype(BF)
            lo = (av - hi.astype(F32)).astype(BF)
            acc = acc + _nt(hi, oh_ref[c]) + _nt(lo, oh_ref[c])
        o_ref[...] = acc

    return pl.pallas_call(body, name="relbias_grad", out_shape=_sds((N_HEADS, N_BUCKETS), F32),
                          compiler_params=_cp())(a_all, onehot)


def _headnorm_bwd(dn, o, gv, mv):
    ms = _nn2(o * o, mv) * (1.0 / HEAD_DIM)
    r = lax.rsqrt(ms + EPS)
    nrm = o * r
    dg = jnp.sum(dn * nrm, axis=0, keepdims=True)
    dnn = dn * gv
    do = r * (dnn - nrm * (_nn2(dnn * nrm, mv) * (1.0 / HEAD_DIM)))
    return do, dg


def _mix_bwd_out(dx, gt, tv, w_out, o_sb, o_dil, on_sb, on_dil, g_sb, g_dil, ones_g, seq, carry=None):
    t, d = dx.shape
    per = seq // TM
    nb = t // seq

    def body(dx_ref, gt_ref, t_ref, w_ref, osb, odl, onsb, ondl, gsb, gdl, m_ref,
             dosb, dodl, dgt_ref, dgsb, dgdl, dw_ref):
        m = pl.program_id(0)
        dxv = dx_ref[...]
        dt = (gt_ref[...] * dxv).astype(BF)
        _acc_rows(dgt_ref, jnp.sum(dxv * t_ref[...], axis=0, keepdims=True), m % per == 0)
        mv = m_ref[...]
        don_sb = _nt(dt, w_ref[0:D_GRP, :])
        don_dl = _nt(dt, w_ref[D_GRP:2 * D_GRP, :])
        do1, dg1 = _headnorm_bwd(don_sb, osb[...], gsb[...], mv)
        do2, dg2 = _headnorm_bwd(don_dl, odl[...], gdl[...], mv)
        dosb[...] = do1
        dodl[...] = do2
        _acc_rows(dgsb, dg1, m == 0)
        _acc_rows(dgdl, dg2, m == 0)
        p1 = _tn(onsb[...], dt)
        p2 = _tn(ondl[...], dt)

        @pl.when(m == 0)
        def _():
            dw_ref[0:D_GRP, :] = p1
            dw_ref[D_GRP:2 * D_GRP, :] = p2

        @pl.when(m != 0)
        def _():
            dw_ref[0:D_GRP, :] += p1
            dw_ref[D_GRP:2 * D_GRP, :] += p2

    row = pl.BlockSpec((TM, d), lambda m: (m, 0))
    half = pl.BlockSpec((TM, D_GRP), lambda m: (m, 0))
    ex = pl.BlockSpec((None, 1, d), lambda m: (m // per, 0, 0))
    gvec = pl.BlockSpec((1, D_GRP), lambda m: (0, 0))
    wblk = pl.BlockSpec((2 * D_GRP, d), lambda m: (0, 0))
    return _call(
        body, "mix_bwd_out", (t // TM,),
        [row, ex, row, wblk, half, half, half, half, gvec, gvec, pl.BlockSpec((D_GRP, D_GRP), lambda m: (0, 0))],
        [half, half, ex, gvec, gvec, wblk],
        [_sds((t, D_GRP), F32), _sds((t, D_GRP), F32), _sds((nb, 1, d), F32),
         _sds((1, D_GRP), F32), _sds((1, D_GRP), F32), _sds((2 * D_GRP, d), F32)],
        (dx, gt, tv, w_out, o_sb, o_dil, on_sb, on_dil, g_sb, g_dil, ones_g), carry=carry)


def _dw_in(h, dqkv6, carry=None):
    t, d = h.shape
    wc = 6 * D_GRP // N_CHIPS

    def body(h_ref, g_ref, o_ref):
        kt = pl.program_id(0)
        hv = h_ref[...]
        for j in range(N_CHIPS):
            p = _tn(hv, _chip_cols(g_ref, j, wc))

            @pl.when(kt == 0)
            def _(p=p, j=j):
                o_ref[j, 0] = p

            @pl.when(kt != 0)
            def _(p=p, j=j):
                o_ref[j, 0] += p

    return _call(
        body, "dw_in", (t // TK_W,),
        [pl.BlockSpec((TK_W, d), lambda kt: (kt, 0)), pl.BlockSpec((6, TK_W, D_GRP), lambda kt: (0, kt, 0))],
        [pl.BlockSpec((N_CHIPS, 1, d, wc), lambda kt: (0, 0, 0, 0))],
        [_sds((N_CHIPS, 1, d, wc), F32)], (h, dqkv6), carry=carry)


def _mix_bwd_dh(dqkv6, w_in, x, g, sc, dxo, seq, carry=None):
    _, t, _ = dqkv6.shape
    d = x.shape[-1]
    wc = w_in.shape[-1]
    per = seq // TM
    nb = t // seq

    def body(g6_ref, w_ref, x_ref, g_ref, sc_ref, dxo_ref, dx_ref, dsh_ref, dsc_ref, dg_ref):
        m = pl.program_id(0)
        dh = _nt(_chip_cols(g6_ref, 0, wc), w_ref[0, 0])
        for j in range(1, N_CHIPS):
            dh = dh + _nt(_chip_cols(g6_ref, j, wc), w_ref[j, 0])
        dx, dsh, dsc, dg = _modnorm_bwd_tile(dh, x_ref[...], g_ref[...], sc_ref[...], dxo_ref[...])
        dx_ref[...] = dx
        _acc_rows(dsh_ref, dsh, m % per == 0)
        _acc_rows(dsc_ref, dsc, m % per == 0)
        _acc_rows(dg_ref, dg, m == 0)

    row = pl.BlockSpec((TM, d), lambda m: (m, 0))
    ex = pl.BlockSpec((None, 1, d), lambda m: (m // per, 0, 0))
    vec = pl.BlockSpec((1, d), lambda m: (0, 0))
    return _call(
        body, "mix_bwd_dh", (t // TM,),
        [pl.BlockSpec((6, TM, D_GRP), lambda m: (0, m, 0)), _whole(w_in), row, vec, ex, row],
        [row, ex, ex, vec],
        [_sds((t, d), F32), _sds((nb, 1, d), F32), _sds((nb, 1, d), F32), _sds((1, d), F32)],
        (dqkv6, w_in, x, g, sc, dxo), carry=carry)


def _ffn_down_loss(s, wd, x, gt, seq, coef, g, target):
    _, t, fs = s.shape
    d = x.shape[-1]
    per = seq // TM
    steps = t // TM

    def body(s_ref, w_ref, x_ref, gt_ref, g_ref, t_ref, f_ref, dx_ref, dg_ref, loss_ref, lacc):
        m = pl.program_id(0)
        f = _nn(s_ref[0], w_ref[0, 0])
        for j in range(1, N_CHIPS):
            f = f + _nn(s_ref[j], w_ref[j, 0])
        f_ref[...] = f
        xv = x_ref[...] + (coef * gt_ref[...]) * f
        gv = g_ref[...]
        r = lax.rsqrt(jnp.mean(xv * xv, axis=-1, keepdims=True) + EPS)
        n = xv * r
        err = n * gv - t_ref[...]
        dy = err * (1.0 / d)
        _acc_rows(dg_ref, jnp.sum(dy * n, axis=0, keepdims=True), m == 0)
        dn = dy * gv
        dx_ref[...] = r * (dn - n * jnp.mean(dn * n, axis=-1, keepdims=True))
        _acc_rows(lacc, jnp.sum(err * err, axis=0, keepdims=True), m == 0)

        @pl.when(m == steps - 1)
        def _():
            tot = jnp.sum(lacc[...], axis=-1, keepdims=True) * (0.5 / d)
            loss_ref[...] = jnp.broadcast_to(tot, (1, 128))

    row = pl.BlockSpec((TM, d), lambda m: (m, 0))
    vec = pl.BlockSpec((1, d), lambda m: (0, 0))
    return pl.pallas_call(
        body, name="ffn_down_loss", grid=(steps,),
        in_specs=[pl.BlockSpec((N_CHIPS, TM, fs), lambda m: (0, m, 0)), _whole(wd), row,
                  pl.BlockSpec((None, 1, d), lambda m: (m // per, 0, 0)), vec, row],
        out_specs=[row, row, vec, pl.BlockSpec((1, 128), lambda m: (0, 0))],
        out_shape=[_sds((t, d), F32), _sds((t, d), F32), _sds((1, d), F32), _sds((1, 128), F32)],
        scratch_shapes=[pltpu.VMEM((1, d), F32)],
        compiler_params=_cp(1))(s, wd, x, gt, g, target)


def _row_tile(rows, cols):
    best = rows
    for tr in range(8, rows + 1, 8):
        if rows % tr == 0 and tr * cols * 4 <= (1 << 20):
            best = tr
    if best * cols * 4 > (1 << 21):
        best = 8
    return best


def _adamw(w, g_arr, g_sel, m, v):
    rows, cols = w.shape
    tr = _row_tile(rows, cols)
    b1c = 1.0 - ADAM_B1 ** ADAM_STEP
    b2c = 1.0 - ADAM_B2 ** ADAM_STEP

    def body(w_ref, g_ref, m_ref, v_ref, go_ref, d_ref, mo_ref, vo_ref):
        gv = g_ref[...]
        mn = ADAM_B1 * m_ref[...] + (1.0 - ADAM_B1) * gv
        vn = ADAM_B2 * v_ref[...] + (1.0 - ADAM_B2) * (gv * gv)
        go_ref[...] = gv
        mo_ref[...] = mn
        vo_ref[...] = vn
        d_ref[...] = -ADAM_LR * ((mn / b1c) / (jnp.sqrt(vn / b2c) + ADAM_EPS) + ADAM_WD * w_ref[...])

    blk = pl.BlockSpec((tr, cols), lambda i: (i, 0))
    shp = _sds((rows, cols), F32)
    return pl.pallas_call(
        body, name="adamw", grid=(rows // tr,),
        in_specs=[blk, pl.BlockSpec((None, tr, cols), lambda i: (g_sel, i, 0)), blk, blk],
        out_specs=[blk] * 4, out_shape=[shp] * 4,
        compiler_params=_cp(1))(w, g_arr, m, v)


def _flip(v, bit):
    return 1 - v if bit else v


def _my_place():
    x, y, c = lax.axis_index("x"), lax.axis_index("y"), lax.axis_index("c")
    return x, y, c


class _Exchange:
    def __init__(self, operands, out_shape, aliases, sems, start, finish):
        self.operands, self.out_shape, self.aliases, self.sems = list(operands), list(out_shape), dict(aliases), list(sems)
        self.start, self.finish = start, finish


def _join(exchanges):
    exchanges = [e for e in exchanges if e is not None]
    if not exchanges:
        return None
    ops, outs, sems, aliases, spans = [], [], [], {}, []
    for e in exchanges:
        spans.append((len(ops), len(outs), len(sems), e))
        for i, j in e.aliases.items():
            aliases[len(ops) + i] = len(outs) + j
        ops += e.operands
        outs += e.out_shape
        sems += e.sems

    def run(which):
        def go(ins, res, sm):
            for io, oo, so, e in spans:
                getattr(e, which)(ins[io:io + len(e.operands)], res[oo:oo + len(e.out_shape)], sm[so:so + len(e.sems)])
        return go

    return _Exchange(ops, outs, aliases, sems, run("start"), run("finish"))


def _call(body, name, grid, in_specs, out_specs, out_shape, args, scratch=(), carry=None, io_alias=None):
    in_specs, out_specs, out_shape, scratch = list(in_specs), list(out_specs), list(out_shape), list(scratch)
    io_alias = dict(io_alias or {})
    if carry is None:
        return pl.pallas_call(body, name=name, grid=grid, in_specs=in_specs, out_specs=out_specs,
                              out_shape=out_shape, scratch_shapes=scratch, input_output_aliases=io_alias,
                              compiler_params=_cp(len(grid)))(*args)
    n_in, n_out, n_s = len(in_specs), len(out_specs), len(scratch)
    c_in, c_out = len(carry.operands), len(carry.out_shape)
    any_spec = pl.BlockSpec(memory_space=pl.ANY)

    def wrapped(*refs):
        ins, cins = refs[:n_in], refs[n_in:n_in + c_in]
        o0 = n_in + c_in
        outs, couts = refs[o0:o0 + n_out], refs[o0 + n_out:o0 + n_out + c_out]
        s0 = o0 + n_out + c_out
        scr, sems = refs[s0:s0 + n_s], refs[s0 + n_s:]
        first = pl.program_id(0) == 0
        last = pl.program_id(0) == grid[0] - 1
        for ax in range(1, len(grid)):
            first = jnp.logical_and(first, pl.program_id(ax) == 0)
            last = jnp.logical_and(last, pl.program_id(ax) == grid[ax] - 1)

        @pl.when(first)
        def _():
            carry.start(cins, couts, sems)

        body(*ins, *outs, *scr)

        @pl.when(last)
        def _():
            carry.finish(cins, couts, sems)

    return pl.pallas_call(
        wrapped, name=name, grid=grid, in_specs=in_specs + [any_spec] * c_in,
        out_specs=out_specs + [any_spec] * c_out, out_shape=out_shape + carry.out_shape,
        scratch_shapes=scratch + carry.sems,
        input_output_aliases={**io_alias, **{n_in + i: n_out + j for i, j in carry.aliases.items()}},
        compiler_params=_cp(len(grid)))(*args, *carry.operands)


def _whole_call(body, name, args, out_shape, scratch, carry=None):
    vm = pl.BlockSpec(memory_space=pltpu.VMEM)
    any_spec = pl.BlockSpec(memory_space=pl.ANY)
    out_shape, scratch = list(out_shape), list(scratch)
    n_in, n_out, n_s = len(args), len(out_shape), len(scratch)
    if carry is None:
        return pl.pallas_call(body, name=name, in_specs=[vm] * n_in, out_specs=[vm] * n_out, out_shape=out_shape,
                              scratch_shapes=scratch, compiler_params=_cp())(*args)
    c_in, c_out = len(carry.operands), len(carry.out_shape)

    def wrapped(*refs):
        ins, cins = refs[:n_in], refs[n_in:n_in + c_in]
        o0 = n_in + c_in
        outs, couts = refs[o0:o0 + n_out], refs[o0 + n_out:o0 + n_out + c_out]
        s0 = o0 + n_out + c_out
        scr, sems = refs[s0:s0 + n_s], refs[s0 + n_s:]
        carry.start(cins, couts, sems)
        body(*ins, *outs, *scr)
        carry.finish(cins, couts, sems)

    return pl.pallas_call(
        wrapped, name=name, in_specs=[vm] * n_in + [any_spec] * c_in, out_specs=[vm] * n_out + [any_spec] * c_out,
        out_shape=out_shape + carry.out_shape, scratch_shapes=scratch + carry.sems,
        input_output_aliases={n_in + i: n_out + j for i, j in carry.aliases.items()},
        compiler_params=_cp())(*args, *carry.operands)


def _alone(name, ex):
    any_spec = pl.BlockSpec(memory_space=pl.ANY)
    c_in, c_out = len(ex.operands), len(ex.out_shape)

    def body(*refs):
        ins, outs, sems = refs[:c_in], refs[c_in:c_in + c_out], refs[c_in + c_out:]
        ex.start(ins, outs, sems)
        ex.finish(ins, outs, sems)

    return pl.pallas_call(
        body, name=name, in_specs=[any_spec] * c_in, out_specs=[any_spec] * c_out, out_shape=ex.out_shape,
        scratch_shapes=ex.sems, input_output_aliases=ex.aliases, compiler_params=_cp())(*ex.operands)


def _ada_fwd(c_pad, w_ada, b_shard, carry=None):
    d = c_pad.shape[-1]
    cols = w_ada.shape[-1]
    chunk = 384

    def body(c_ref, w_ref, b_ref, call_ref, mod_ref, part, s1, r1, s2, r2):
        x, y, c = _my_place()
        dev = 4 * x + 2 * y + c
        chip = 2 * x + y
        call_ref[dev] = c_ref[...]

        def c_copy(k):
            px, py, pc = _flip(x, (k >> 2) & 1), _flip(y, (k >> 1) & 1), _flip(c, k & 1)
            return px, py, pc

        sends = []
        for k in range(1, N_DEV):
            px, py, pc = c_copy(k)
            cp = pltpu.make_async_remote_copy(src_ref=c_ref, dst_ref=call_ref.at[dev], send_sem=s1.at[k - 1],
                                              recv_sem=r1.at[k - 1], device_id=(px, py, pc), device_id_type=MESH)
            cp.start()
            sends.append(cp)
        for k in range(1, N_DEV):
            px, py, pc = c_copy(k)
            pltpu.make_async_remote_copy(src_ref=c_ref, dst_ref=call_ref.at[4 * px + 2 * py + pc],
                                         send_sem=s1.at[k - 1], recv_sem=r1.at[k - 1],
                                         device_id=(px, py, pc), device_id_type=MESH).wait_recv()
        for cp in sends:
            cp.wait_send()

        cs = call_ref[...].reshape(N_DEV * 8, d)
        sc = (cs * jax.nn.sigmoid(cs)).astype(BF)
        for n0 in range(0, cols, chunk):
            blk = _nn(sc, w_ref[:, n0:n0 + chunk].astype(BF)) + b_ref[:, n0:n0 + chunk]
            part[:, :, n0:n0 + chunk] = blk.reshape(N_DEV, 8, chunk)

        mod_ref[chip] = part[dev]
        sends = []
        for kk in range(1, N_CHIPS):
            px, py = _flip(x, (kk >> 1) & 1), _flip(y, kk & 1)
            cp = pltpu.make_async_remote_copy(src_ref=part.at[4 * px + 2 * py + c], dst_ref=mod_ref.at[chip],
                                              send_sem=s2.at[kk - 1], recv_sem=r2.at[kk - 1],
                                              device_id=(px, py, c), device_id_type=MESH)
            cp.start()
            sends.append(cp)
        for kk in range(1, N_CHIPS):
            px, py = _flip(x, (kk >> 1) & 1), _flip(y, kk & 1)
            pltpu.make_async_remote_copy(src_ref=part.at[dev], dst_ref=mod_ref.at[2 * px + py],
                                         send_sem=s2.at[kk - 1], recv_sem=r2.at[kk - 1],
                                         device_id=(px, py, c), device_id_type=MESH).wait_recv()
        for cp in sends:
            cp.wait_send()

    return _whole_call(
        body, "ada_fwd", (c_pad, w_ada, b_shard),
        [_sds((N_DEV, 8, d), F32), _sds((N_CHIPS, 8, cols), F32)],
        [pltpu.VMEM((N_DEV, 8, cols), F32),
         pltpu.SemaphoreType.DMA((N_DEV - 1,)), pltpu.SemaphoreType.DMA((N_DEV - 1,)),
         pltpu.SemaphoreType.DMA((N_CHIPS - 1,)), pltpu.SemaphoreType.DMA((N_CHIPS - 1,))], carry=carry)


def _ag_weights(bufs, kks=(1, 2, 3), relative=False):
    n, nk = len(bufs), len(kks)

    def half(b, which):
        hr = bufs[b].shape[2] // 2
        return pl.ds(pl.multiple_of(which * hr, 16), hr)

    def copies(outs, sems, b, i, kk):
        x, y, c = _my_place()
        chip = 2 * x + y
        px, py = _flip(x, (kk >> 1) & 1), _flip(y, kk & 1)
        mine, theirs = (0, kk) if relative else (chip, 2 * px + py)
        landing = kk if relative else chip
        k = nk * b + i
        send = pltpu.make_async_remote_copy(
            src_ref=outs[b].at[mine, :, half(b, c), :], dst_ref=outs[b].at[landing, :, half(b, c), :],
            send_sem=sems[0].at[k], recv_sem=sems[1].at[k], device_id=(px, py, c), device_id_type=MESH)
        got = outs[b].at[theirs, :, half(b, c), :]
        recv = pltpu.make_async_remote_copy(
            src_ref=got, dst_ref=got, send_sem=sems[0].at[k], recv_sem=sems[1].at[k],
            device_id=(px, py, c), device_id_type=MESH)
        fwd = pltpu.make_async_remote_copy(
            src_ref=got, dst_ref=got, send_sem=sems[2].at[k], recv_sem=sems[3].at[k],
            device_id=(x, y, 1 - c), device_id_type=MESH)
        other = outs[b].at[theirs, :, half(b, 1 - c), :]
        back = pltpu.make_async_remote_copy(
            src_ref=other, dst_ref=other, send_sem=sems[2].at[k], recv_sem=sems[3].at[k],
            device_id=(x, y, 1 - c), device_id_type=MESH)
        return send, recv, fwd, back

    def each(outs, sems):
        for b in range(n):
            for i, kk in enumerate(kks):
                yield copies(outs, sems, b, i, kk)

    def start(ins, outs, sems):
        for send, _, _, _ in each(outs, sems):
            send.start()

    def finish(ins, outs, sems):
        for _, recv, fwd, _ in each(outs, sems):
            recv.wait_recv()
            fwd.start()
        for send, _, fwd, back in each(outs, sems):
            back.wait_recv()
            send.wait_send()
            fwd.wait_send()

    return _Exchange(bufs, [_sds(s.shape, s.dtype) for s in bufs], {i: i for i in range(n)},
                     [pltpu.SemaphoreType.DMA((nk * n,))] * 4, start, finish)


def _rs_d2d(grads):
    n = len(grads)

    def copy(ins, outs, sems, b):
        x, y, c = _my_place()
        hr = grads[b].shape[2] // 2
        theirs = pl.ds(pl.multiple_of((1 - c) * hr, 8), hr)
        return pltpu.make_async_remote_copy(
            src_ref=ins[b].at[:, :, theirs, :], dst_ref=outs[b], send_sem=sems[0].at[b], recv_sem=sems[1].at[b],
            device_id=(x, y, 1 - c), device_id_type=MESH)

    def start(ins, outs, sems):
        for b in range(n):
            copy(ins, outs, sems, b).start()

    def finish(ins, outs, sems):
        for b in range(n):
            copy(ins, outs, sems, b).wait()

    return _Exchange(grads, [_sds(g.shape[:2] + (g.shape[2] // 2, g.shape[3]), F32) for g in grads], {},
                     [pltpu.SemaphoreType.DMA((n,))] * 2, start, finish)


def _add_halves(core, g, land):
    nchip, ng, rows, cols = g.shape
    hr = rows // 2
    tr = _row_tile(hr, cols)
    steps = hr // tr

    def body(core_ref, g_ref, l_ref, o_ref):
        del core_ref
        o_ref[...] = (g_ref[...] + l_ref[...]).astype(BF)

    return pl.pallas_call(
        body, name="add_halves",
        grid_spec=pltpu.PrefetchScalarGridSpec(
            num_scalar_prefetch=1, grid=(nchip, ng, steps),
            in_specs=[pl.BlockSpec((None, None, tr, cols), lambda j, a, i, cr: (j, a, cr[0] * steps + i, 0)),
                      pl.BlockSpec((None, None, tr, cols), lambda j, a, i, cr: (j, a, i, 0))],
            out_specs=pl.BlockSpec((None, None, tr, cols), lambda j, a, i, cr: (j, a, i, 0))),
        out_shape=_sds((nchip, ng, hr, cols), BF),
        compiler_params=_cp(3))(core, g, land)


def _rs_ici(parts, relative=False):
    n = len(parts)

    def copies(ins, outs, sems):
        x, y, c = _my_place()
        chip = 2 * x + y
        for b in range(n):
            for kk in range(1, N_CHIPS):
                px, py = _flip(x, (kk >> 1) & 1), _flip(y, kk & 1)
                k = 3 * b + kk - 1
                theirs, landing = (kk, kk) if relative else (2 * px + py, chip)
                send = pltpu.make_async_remote_copy(
                    src_ref=ins[b].at[theirs], dst_ref=outs[b].at[landing],
                    send_sem=sems[0].at[k], recv_sem=sems[1].at[k], device_id=(px, py, c), device_id_type=MESH)
                slot = outs[b].at[theirs]
                recv = pltpu.make_async_remote_copy(
                    src_ref=slot, dst_ref=slot, send_sem=sems[0].at[k], recv_sem=sems[1].at[k],
                    device_id=(px, py, c), device_id_type=MESH)
                yield send, recv

    def start(ins, outs, sems):
        for send, _ in copies(ins, outs, sems):
            send.start()

    def finish(ins, outs, sems):
        for send, recv in copies(ins, outs, sems):
            recv.wait_recv()
            send.wait_send()

    return _Exchange(parts, [_sds(p.shape, p.dtype) for p in parts], {},
                     [pltpu.SemaphoreType.DMA((3 * n,))] * 2, start, finish)


def _sum_chips(place, part, land, relative=False):
    nchip, ng, hr, cols = land.shape
    tr = _row_tile(hr, cols)
    steps = hr // tr

    def body(place_ref, p_ref, l1, l2, l3, o_ref):
        del place_ref
        o_ref[...] = ((p_ref[...].astype(F32) + l1[...].astype(F32)) + l2[...].astype(F32)) + l3[...].astype(F32)

    def slot(k):
        if relative:
            return pl.BlockSpec((None, None, tr, cols), lambda a, i, pr: (k, a, i, 0))
        return pl.BlockSpec((None, None, tr, cols), lambda a, i, pr: (jnp.bitwise_xor(pr[1], k), a, i, 0))

    return pl.pallas_call(
        body, name="sum_chips",
        grid_spec=pltpu.PrefetchScalarGridSpec(
            num_scalar_prefetch=1, grid=(ng, steps),
            in_specs=[slot(0), slot(1), slot(2), slot(3)],
            out_specs=pl.BlockSpec((None, tr, cols), lambda a, i, pr: (a, pr[0] * steps + i, 0))),
        out_shape=_sds((ng, 2 * hr, cols), F32),
        compiler_params=_cp(2))(place, part, land, land, land)


def _rs_final(bufs):
    n = len(bufs)

    def copy(outs, sems, b, which):
        x, y, c = _my_place()
        hr = bufs[b].shape[1] // 2
        rows = outs[b].at[:, pl.ds(pl.multiple_of((c if which == 0 else 1 - c) * hr, 8), hr), :]
        return pltpu.make_async_remote_copy(
            src_ref=rows, dst_ref=rows, send_sem=sems[0].at[b], recv_sem=sems[1].at[b],
            device_id=(x, y, 1 - c), device_id_type=MESH)

    def start(ins, outs, sems):
        for b in range(n):
            copy(outs, sems, b, 0).start()

    def finish(ins, outs, sems):
        for b in range(n):
            copy(outs, sems, b, 0).wait_send()
            copy(outs, sems, b, 1).wait_recv()

    return _Exchange(bufs, [_sds(h.shape, F32) for h in bufs], {i: i for i in range(n)},
                     [pltpu.SemaphoreType.DMA((n,))] * 2, start, finish)


def _small_sync(smalls, dmod_blk, c_all, carry=None):
    d = c_all.shape[-1]
    cols = dmod_blk.shape[-1]
    chunk = 384

    def body(sm_ref, dm_ref, c_ref, sum_ref, gw_ref, sm_all, dm_all, ssem, rsem):
        x, y, c = _my_place()
        dev = 4 * x + 2 * y + c
        chip = 2 * x + y
        sm_all[dev] = sm_ref[...]
        dm_all[dev] = dm_ref[chip]
        sends = []
        for k in range(1, N_DEV):
            px, py, pc = _flip(x, (k >> 2) & 1), _flip(y, (k >> 1) & 1), _flip(c, k & 1)
            a = pltpu.make_async_remote_copy(src_ref=sm_ref, dst_ref=sm_all.at[dev], send_sem=ssem.at[2 * (k - 1)],
                                             recv_sem=rsem.at[2 * (k - 1)], device_id=(px, py, pc),
                                             device_id_type=MESH)
            b = pltpu.make_async_remote_copy(src_ref=dm_ref.at[2 * px + py], dst_ref=dm_all.at[dev],
                                             send_sem=ssem.at[2 * (k - 1) + 1], recv_sem=rsem.at[2 * (k - 1) + 1],
                                             device_id=(px, py, pc), device_id_type=MESH)
            a.start()
            b.start()
            sends += [a, b]
        for k in range(1, N_DEV):
            px, py, pc = _flip(x, (k >> 2) & 1), _flip(y, (k >> 1) & 1), _flip(c, k & 1)
            pdev = 4 * px + 2 * py + pc
            pltpu.make_async_remote_copy(src_ref=sm_ref, dst_ref=sm_all.at[pdev], send_sem=ssem.at[2 * (k - 1)],
                                         recv_sem=rsem.at[2 * (k - 1)], device_id=(px, py, pc),
                                         device_id_type=MESH).wait_recv()
            pltpu.make_async_remote_copy(src_ref=dm_ref.at[chip], dst_ref=dm_all.at[pdev],
                                         send_sem=ssem.at[2 * (k - 1) + 1], recv_sem=rsem.at[2 * (k - 1) + 1],
                                         device_id=(px, py, pc), device_id_type=MESH).wait_recv()
        for cp in sends:
            cp.wait_send()

        tot = sm_all[0]
        for q in range(1, N_DEV):
            tot = tot + sm_all[q]
        sum_ref[...] = tot

        cs = c_ref[...].reshape(N_DEV * 8, d)
        sc = (cs * jax.nn.sigmoid(cs)).astype(BF)
        for n0 in range(0, cols, chunk):
            dmv = dm_all[:, :, n0:n0 + chunk].reshape(N_DEV * 8, chunk).astype(BF)
            gw_ref[:, n0:n0 + chunk] = _tn(sc, dmv)

    return _whole_call(
        body, "small_sync", (smalls, dmod_blk, c_all),
        [_sds(smalls.shape, F32), _sds((d, cols), F32)],
        [pltpu.VMEM((N_DEV,) + smalls.shape, F32), pltpu.VMEM((N_DEV, 8, cols), F32),
         pltpu.SemaphoreType.DMA((2 * (N_DEV - 1),)), pltpu.SemaphoreType.DMA((2 * (N_DEV - 1),))], carry=carry)


def _bucket_onehot():
    maps = np.stack([_bucket_map(dil).reshape(-1) for _, dil in DIL_CONFIGS])
    return (jnp.asarray(maps)[:, None, :] == jnp.arange(N_BUCKETS, dtype=jnp.int32)[None, :, None]).astype(BF)


def _dil_bias(rel_t, onehot):
    def body(r_ref, oh_ref, o_ref):
        rv = r_ref[...]
        hi = rv.astype(BF)
        lo = (rv - hi.astype(F32)).astype(BF)
        for c in range(len(DIL_CONFIGS)):
            o_ref[c] = _nn(hi, oh_ref[c]) + _nn(lo, oh_ref[c])

    return pl.pallas_call(body, name="dil_bias",
                          out_shape=_sds((len(DIL_CONFIGS), N_HEADS, BLOCK * 2 * BLOCK), F32),
                          compiler_params=_cp())(rel_t, onehot)


def _rowsum8(a):
    def body(a_ref, o_ref):
        o_ref[...] = jnp.sum(a_ref[...], axis=0, keepdims=True)

    return pl.pallas_call(body, name="rowsum8", out_shape=_sds((1, a.shape[1]), F32), compiler_params=_cp())(a)


def _local_step(x, mod, target, w, gains, rel_bias, place=None):
    nb, seq, d = x.shape
    t = nb * seq
    dist = place is not None
    core = place[0:1] if dist else None
    x0 = x.reshape(t, d)
    tgt = target.reshape(t, d)
    md = [mod[:, i:i + 1, :] for i in range(N_MOD)]
    sh1, sc1, gt1, sh2, sc2, gt2, sh3, sc3, gt3 = md
    g1, g2, g3 = gains["g_ffn1"], gains["g_mix"], gains["g_ffn2"]
    ones_g = _group_ones()

    def partial_sums(grads, lands):
        return [_add_halves(core, g, l) for g, l in zip(grads, lands)]

    def chip_sums(parts, lands):
        return [_sum_chips(place, p, l, relative=True) for p, l in zip(parts, lands)]

    gu1 = w["gu1"]
    res = _ffn_up(x0, g1, sc1, sh1, gu1, seq,
                  carry=_join([_ag_weights([w["d1"]], relative=True), _ag_weights([w["win"], w["wout"]])])
                  if dist else None)
    h1, a1, u1, s1 = res[:4]
    wd1, w_in, w_out = res[4:] if dist else (w["d1"], w["win"], w["wout"])
    w_out2 = w_out.reshape(2 * D_GRP, d)
    f1, x1 = _ffn_down(s1, wd1, x0, gt1, seq, 0.5)

    h2, qkv6, qkv_r4, qkv_r16 = _qkv_proj(x1, g2, sc2, sh2, w_in, seq)
    qkv6b = qkv6.reshape(6, nb, seq, D_GRP)
    res = _sb_fwd(qkv6b, gains["g_sb_out"], nb, seq,
                  carry=_ag_weights([w["gu2"], w["d2"]], relative=True) if dist else None)
    o_sb, on_sb = res[:2]
    wgu2, wd2 = res[2:] if dist else (w["gu2"], w["d2"])
    onehot = _bucket_onehot()
    bias = _dil_bias(rel_bias.T, onehot).reshape(len(DIL_CONFIGS), N_HEADS * BLOCK, 2 * BLOCK)
    o_cs, l_cs = [], []
    qkv_rs = [(qkv6b, 3), (qkv_r4, 0), (qkv_r16, 0)]
    for ci, (_, dil) in enumerate(DIL_CONFIGS):
        sub = seq // dil
        arr, base = qkv_rs[ci]
        arr = arr.reshape(base + 3, nb, sub, dil * D_GRP)
        qkv_rs[ci] = (arr, base)
        o_c, l_c = _dil_fwd(arr, base, bias[ci], nb, sub, dil)
        o_cs.append(o_c.reshape(t // dil, dil * D_GRP))
        l_cs.append(l_c.reshape(t // dil, dil * D_GRP))
    o_dil, on_dil = _dil_comb(o_cs, l_cs, gains["g_dil_out"])
    tmix, x2 = _mix_out(on_sb.reshape(t, D_GRP), on_dil, w_out2, x1, gt2, seq)

    h3, a3, u3, s3 = _ffn_up(x2, g3, sc3, sh3, wgu2, seq)
    f3, dx3, dg_final, loss = _ffn_down_loss(s3, wd2, x2, gt3, seq, 0.5, gains["g_final"], tgt)

    da3, du3, df3, dgt3, dx2, dsh3, dsc3, dg3 = _ffn_bwd_x(dx3, gt3, f3, wd2, a3, u3, wgu2, x2, g3, sc3, seq, 0.5)
    grads2 = [_ffn_bwd_w(h3, da3, du3, s3, df3)]

    res = _mix_bwd_out(
        dx2, gt2, tmix, w_out2, o_sb.reshape(t, D_GRP), o_dil, on_sb.reshape(t, D_GRP), on_dil,
        gains["g_sb_out"], gains["g_dil_out"], ones_g, seq, carry=_rs_d2d(grads2) if dist else None)
    do_sb, do_dil, dgt2, dg_sb, dg_dil, dw_out = res[:6]
    parts2 = partial_sums(grads2, res[6:]) if dist else None
    dw_out = dw_out.reshape(N_CHIPS, 1, 2 * D_GRP // N_CHIPS, d)
    res = _sb_bwd(qkv6b, do_sb.reshape(nb, seq, D_GRP), nb, seq,
                  carry=_rs_ici(parts2, relative=True) if dist else None)
    dqkv6 = res[0]
    halves2 = chip_sums(parts2, res[1:]) if dist else None
    dcs = _dil_comb_bwd(do_dil, o_cs, l_cs)
    dsum, a_tiles = [], []
    for ci, (_, dil) in enumerate(DIL_CONFIGS):
        sub = seq // dil
        do_c = dcs[ci].reshape(nb, sub, dil * D_GRP)
        dd_c = dcs[3 + ci].reshape(nb, sub, dil * D_GRP)
        res = _dil_bwd(qkv_rs[ci][0], qkv_rs[ci][1], bias[ci], do_c, dd_c, nb, sub, dil)
        dsum.append(res[0].reshape(3, t // dil, dil * D_GRP))
        a_tiles.append(res[1].reshape(N_HEADS, BLOCK * 2 * BLOCK))
    dqkv6 = _dqkv_dil_sum(dsum, dqkv6.reshape(6, t, D_GRP))
    drel = _relbias_grad(jnp.stack(a_tiles), onehot)
    dx1, dsh2, dsc2, dg2 = _mix_bwd_dh(dqkv6, w_in, x1, g2, sc2, dx2, seq)

    da1, du1, df1, dgt1 = _ffn_bwd_ds(dx1, gt1, f1, wd1, a1, u1, seq, 0.5)
    grads1 = [_ffn_bwd_w(h1, da1, du1, s1, df1)]
    res = _dw_in(h2, dqkv6, carry=_join([_rs_d2d(grads1), _rs_final(halves2)]) if dist else None)
    grads_m = [res[0], dw_out]
    parts1 = partial_sums(grads1, res[1:2]) if dist else None
    if dist:
        grads2 = res[2:3]
    res = _ffn_bwd_dh(da1, du1, gu1, x0, g1, sc1, dx1, seq,
                      carry=_join([_rs_ici(parts1, relative=True), _rs_d2d(grads_m)]) if dist else None)
    dx0, dsh1, dsc1, dg1 = res[:4]
    pending = None
    if dist:
        pending = (chip_sums(parts1, res[4:5]), partial_sums(grads_m, res[5:7]))

    dmod = jnp.concatenate([dsh1, dsc1, dgt1, dsh2, dsc2, dgt2, dsh3, dsc3, dgt3], axis=1)
    return dict(grad_x=dx0.reshape(nb, seq, d), loss=loss[0, 0], dmod=dmod.reshape(nb, N_MOD * d),
                dffn1=grads1[0], dffn2=grads2[0], dwin=grads_m[0], dwout=grads_m[1], pending=pending,
                dg_ffn1=dg1, dg_mix=dg2, dg_ffn2=dg3, dg_final=dg_final, dg_sb=dg_sb, dg_dil=dg_dil,
                drel=drel.T)


_SMALL_ORDER = (("b_ada", N_MOD * 1024), ("g_ffn1", 1024), ("g_mix", 1024), ("g_ffn2", 1024), ("g_final", 1024),
                ("g_sb_out", D_GRP), ("g_dil_out", D_GRP), ("rel_bias", N_BUCKETS * N_HEADS))


def _pack_small(parts, extra=None):
    flat = [parts[name].reshape(-1).astype(F32) for name, _ in _SMALL_ORDER]
    used = sum(sz for _, sz in _SMALL_ORDER)
    pad = SMALL_ROWS * 128 - used
    tail = jnp.zeros((pad,), F32)
    if extra is not None:
        tail = tail.at[0].set(extra)
    return jnp.concatenate(flat + [tail]).reshape(SMALL_ROWS, 128)


def _unpack_small(packed, shapes):
    flat = packed.reshape(-1)
    out, off = {}, 0
    for name, sz in _SMALL_ORDER:
        out[name] = flat[off:off + sz].reshape(shapes[name])
        off += sz
    return out, flat[off]


def kernel(x, c, w_ada, b_ada, g_ffn1, w1_gate, w1_up, w1_down, g_mix, w_in, g_sb_out, g_dil_out, w_out, rel_bias, g_ffn2, w2_gate, w2_up, w2_down, g_final, loss_target, m_w_ada, m_b_ada, m_g_ffn1, m_w1_gate, m_w1_up, m_w1_down, m_g_mix, m_w_in, m_g_sb_out, m_g_dil_out, m_w_out, m_rel_bias, m_g_ffn2, m_w2_gate, m_w2_up, m_w2_down, m_g_final, v_w_ada, v_b_ada, v_g_ffn1, v_w1_gate, v_w1_up, v_w1_down, v_g_mix, v_w_in, v_g_sb_out, v_g_dil_out, v_w_out, v_rel_bias, v_g_ffn2, v_w2_gate, v_w2_up, v_w2_down, v_g_final):
    nb, seq, d = x.shape
    xi, yi, ci = lax.axis_index("x"), lax.axis_index("y"), lax.axis_index("c")
    chip = 2 * xi + yi
    ada_cols = w_ada.shape[-1]

    c_pad = jnp.zeros((8, d), F32).at[:nb].set(c)
    b_shard = lax.dynamic_slice(b_ada, (0, chip * ada_cols), (1, ada_cols))
    shards = dict(gu1=jnp.stack([w1_gate[0], w1_up[0]]), d1=w1_down, win=w_in, wout=w_out,
                  gu2=jnp.stack([w2_gate[0], w2_up[0]]), d2=w2_down)
    bufs = {k: lax.dynamic_update_slice(lax.empty((N_CHIPS,) + s.shape, BF), s.astype(BF)[None],
                                        (chip if k in ("win", "wout") else 0, 0, 0, 0))
            for k, s in shards.items()}
    c_all, mod_blk, bufs["gu1"] = _ada_fwd(c_pad, w_ada[0], b_shard,
                                           carry=_ag_weights([bufs["gu1"]], relative=True))
    mod = jnp.transpose(mod_blk[:, :nb, :], (1, 0, 2)).reshape(nb, N_MOD, d)

    gains = dict(g_ffn1=g_ffn1, g_mix=g_mix, g_ffn2=g_ffn2, g_final=g_final.reshape(1, d),
                 g_sb_out=g_sb_out.reshape(1, D_GRP), g_dil_out=g_dil_out.reshape(1, D_GRP))
    place = jnp.stack([ci, chip]).astype(jnp.int32)
    r = _local_step(x, mod, loss_target, bufs, gains, rel_bias, place)

    dmod = r["dmod"]
    dmod_pad = jnp.zeros((8, N_MOD * d), F32).at[:nb].set(dmod)
    dmod_blk = jnp.transpose(dmod_pad.reshape(8, N_CHIPS, ada_cols), (1, 0, 2))
    small_parts = dict(b_ada=_rowsum8(dmod_pad), g_ffn1=r["dg_ffn1"], g_mix=r["dg_mix"], g_ffn2=r["dg_ffn2"],
                       g_final=r["dg_final"], g_sb_out=r["dg_sb"], g_dil_out=r["dg_dil"], rel_bias=r["drel"])
    halves1, parts_m = r["pending"]
    res = _small_sync(_pack_small(small_parts, r["loss"]), dmod_blk, c_all,
                      carry=_join([_rs_final(halves1), _rs_ici(parts_m)]))
    small_sum, g_wada, gffn1 = res[:3]
    halves_m = [_sum_chips(place, p, l) for p, l in zip(parts_m, res[3:5])]
    gwin, gwout = _alone("rs_last", _rs_final(halves_m))
    gffn2 = r["dffn2"]

    small_w = dict(b_ada=b_ada, g_ffn1=g_ffn1, g_mix=g_mix, g_ffn2=g_ffn2, g_final=g_final,
                   g_sb_out=g_sb_out, g_dil_out=g_dil_out, rel_bias=rel_bias)
    small_m = dict(b_ada=m_b_ada, g_ffn1=m_g_ffn1, g_mix=m_g_mix, g_ffn2=m_g_ffn2, g_final=m_g_final,
                   g_sb_out=m_g_sb_out, g_dil_out=m_g_dil_out, rel_bias=m_rel_bias)
    small_v = dict(b_ada=v_b_ada, g_ffn1=v_g_ffn1, g_mix=v_g_mix, g_ffn2=v_g_ffn2, g_final=v_g_final,
                   g_sb_out=v_g_sb_out, g_dil_out=v_g_dil_out, rel_bias=v_rel_bias)
    shapes = {k: v.shape for k, v in small_w.items()}
    sg, sd, sm, sv = _adamw(_pack_small(small_w), small_sum.reshape(1, SMALL_ROWS, 128), 0,
                            _pack_small(small_m), _pack_small(small_v))
    sg, loss = _unpack_small(sg, shapes)
    sd, _ = _unpack_small(sd, shapes)
    sm, _ = _unpack_small(sm, shapes)
    sv, _ = _unpack_small(sv, shapes)

    big = {}

    def upd(name, w, g_arr, sel, m, v, transposed=False):
        swap = (lambda a: jnp.swapaxes(a, -1, -2)) if transposed else (lambda a: a)
        w2, m2, v2 = [swap(a)[0] for a in (w, m, v)]
        big[name] = [swap(a[None]) for a in _adamw(w2, g_arr, sel, m2, v2)]

    upd("w_ada", w_ada, g_wada.reshape(1, d, ada_cols), 0, m_w_ada, v_w_ada)
    upd("w1_gate", w1_gate, gffn1, 0, m_w1_gate, v_w1_gate, transposed=True)
    upd("w1_up", w1_up, gffn1, 1, m_w1_up, v_w1_up, transposed=True)
    upd("w1_down", w1_down, gffn1, 2, m_w1_down, v_w1_down)
    upd("w_in", w_in, gwin, 0, m_w_in, v_w_in)
    upd("w_out", w_out, gwout, 0, m_w_out, v_w_out)
    upd("w2_gate", w2_gate, gffn2, 0, m_w2_gate, v_w2_gate, transposed=True)
    upd("w2_up", w2_up, gffn2, 1, m_w2_up, v_w2_up, transposed=True)
    upd("w2_down", w2_down, gffn2, 2, m_w2_down, v_w2_down)

    names = ["w_ada", "b_ada", "g_ffn1", "w1_gate", "w1_up", "w1_down", "g_mix", "w_in", "g_sb_out", "g_dil_out",
             "w_out", "rel_bias", "g_ffn2", "w2_gate", "w2_up", "w2_down", "g_final"]
    outs = [loss, r["grad_x"]]
    for k, small in enumerate((sg, sd, sm, sv)):
        for name in names:
            outs.append(big[name][k] if name in big else small[name])
    return tuple(outs)
```

```python
import functools
import math

import numpy as np
import jax
import jax.numpy as jnp
from jax import lax
from jax.experimental import pallas as pl
from jax.experimental.pallas import tpu as pltpu

F32 = jnp.float32
BF = jnp.bfloat16
MESH = pl.DeviceIdType.MESH

HEAD_DIM = 64
N_HEADS = 8
D_GRP = N_HEADS * HEAD_DIM
DIL_CONFIGS = ((128, 1), (512, 4), (2048, 16))
N_STEPS = 128
BLOCK = 128
N_BUCKETS = 32
MAX_DISTANCE = 2048
N_MOD = 9
EPS = 1e-6
NEG_INF = -1e30
SCALE = HEAD_DIM ** -0.5

ADAM_LR = 0.001
ADAM_B1 = 0.9
ADAM_B2 = 0.999
ADAM_EPS = 1e-08
ADAM_WD = 0.01
ADAM_STEP = 10

N_CHIPS = 4
N_DEV = 8
VMEM_LIMIT = 56 * 1024 * 1024
TM = 512
TQ = 128
KB = 256
SMALL_ROWS = 120


def _cp(n_axes=0, **kw):
    sem = ("arbitrary",) * n_axes if n_axes else None
    return pltpu.CompilerParams(dimension_semantics=sem, vmem_limit_bytes=VMEM_LIMIT, **kw)


def _nn(a, b):
    return jnp.dot(a, b, preferred_element_type=F32)


def _nt(a, b):
    return lax.dot_general(a, b, (((1,), (1,)), ((), ())), preferred_element_type=F32)


def _tn(a, b):
    return lax.dot_general(a, b, (((0,), (0,)), ((), ())), preferred_element_type=F32)


def _nn2(x, m):
    hi = x.astype(BF)
    lo = (x - hi.astype(F32)).astype(BF)
    r = _nn(jnp.concatenate([hi, lo], axis=0), m)
    return r[:x.shape[0]] + r[x.shape[0]:]


def _softplus(z):
    return jnp.maximum(z, 0.0) + jnp.log1p(jnp.exp(-jnp.abs(z)))


def _sds(shape, dtype):
    return jax.ShapeDtypeStruct(shape, dtype)


def _whole(a):
    nd = a.ndim
    return pl.BlockSpec(a.shape, lambda *_: (0,) * nd, pipeline_mode=pl.Buffered(1))


def _modnorm_bwd_tile(dh, xv, gv, scv, dxo):
    r = lax.rsqrt(jnp.mean(xv * xv, axis=-1, keepdims=True) + EPS)
    n = xv * r
    ng = n * gv
    dsh = jnp.sum(dh, axis=0, keepdims=True)
    dsc = jnp.sum(dh * ng, axis=0, keepdims=True)
    dy = dh * (1.0 + scv)
    dg = jnp.sum(dy * n, axis=0, keepdims=True)
    dn = dy * gv
    dx = dxo + r * (dn - n * jnp.mean(dn * n, axis=-1, keepdims=True))
    return dx, dsh, dsc, dg


def _acc_rows(ref, val, first):
    @pl.when(first)
    def _():
        ref[...] = val

    @pl.when(jnp.logical_not(first))
    def _():
        ref[...] += val


def _modnorm_tile(x_ref, g_ref, sc_ref, sh_ref):
    xv = x_ref[...]
    r = lax.rsqrt(jnp.mean(xv * xv, axis=-1, keepdims=True) + EPS)
    return (((xv * r) * g_ref[...]) * (1.0 + sc_ref[...]) + sh_ref[...]).astype(BF)


def _ffn_up(x, g, sc, sh, wgu, seq, carry=None):
    t, d = x.shape
    fs = wgu.shape[-1]
    per = seq // TM

    def body(x_ref, g_ref, sc_ref, sh_ref, w_ref, h_ref, p_ref, q_ref, s_ref):
        hv = _modnorm_tile(x_ref, g_ref, sc_ref, sh_ref)
        h_ref[...] = hv
        for j in range(N_CHIPS):
            a = _nn(hv, w_ref[j, 0])
            u = _nn(hv, w_ref[j, 1])
            sig = jax.nn.sigmoid(a)
            q = a * sig
            p_ref[j] = (u * (sig * (1.0 + a * (1.0 - sig)))).astype(BF)
            q_ref[j] = q.astype(BF)
            s_ref[j] = (q * u).astype(BF)

    row = pl.BlockSpec((TM, d), lambda m: (m, 0))
    ex = pl.BlockSpec((None, 1, d), lambda m: (m // per, 0, 0))
    blk = pl.BlockSpec((N_CHIPS, TM, fs), lambda m: (0, m, 0))
    return _call(
        body, "ffn_up", (t // TM,),
        [row, pl.BlockSpec((1, d), lambda m: (0, 0)), ex, ex, _whole(wgu)],
        [row, blk, blk, blk],
        [_sds((t, d), BF)] + [_sds((N_CHIPS, t, fs), BF)] * 3,
        (x, g, sc, sh, wgu), carry=carry)


def _ffn_down(s, wd, x, gt, seq, coef, carry=None):
    _, t, fs = s.shape
    d = x.shape[-1]
    per = seq // TM

    def body(s_ref, w_ref, x_ref, gt_ref, f_ref, xo_ref):
        f = _nn(s_ref[0], w_ref[0, 0])
        for j in range(1, N_CHIPS):
            f = f + _nn(s_ref[j], w_ref[j, 0])
        f_ref[...] = f
        xo_ref[...] = x_ref[...] + (coef * gt_ref[...]) * f

    row = pl.BlockSpec((TM, d), lambda m: (m, 0))
    return _call(
        body, "ffn_down", (t // TM,),
        [pl.BlockSpec((N_CHIPS, TM, fs), lambda m: (0, m, 0)), _whole(wd), row,
         pl.BlockSpec((None, 1, d), lambda m: (m // per, 0, 0))],
        [row, row], [_sds((t, d), F32), _sds((t, d), F32)], (s, wd, x, gt), carry=carry)


def _ffn_bwd_ds(dxo, gt, f, wd, p, q, seq, coef, carry=None):
    t, d = dxo.shape
    fs = p.shape[-1]
    per = seq // TM
    nb = t // seq

    def body(dxo_ref, gt_ref, f_ref, w_ref, p_ref, q_ref, da_ref, du_ref, df_ref, dgt_ref):
        m = pl.program_id(0)
        dxv = dxo_ref[...]
        df = ((coef * gt_ref[...]) * dxv).astype(BF)
        df_ref[...] = df
        _acc_rows(dgt_ref, coef * jnp.sum(dxv * f_ref[...], axis=0, keepdims=True), m % per == 0)
        for j in range(N_CHIPS):
            ds = _nt(df, w_ref[j, 0])
            da_ref[j] = (ds * p_ref[j].astype(F32)).astype(BF)
            du_ref[j] = (ds * q_ref[j].astype(F32)).astype(BF)

    row = pl.BlockSpec((TM, d), lambda m: (m, 0))
    blk = pl.BlockSpec((N_CHIPS, TM, fs), lambda m: (0, m, 0))
    ex = pl.BlockSpec((None, 1, d), lambda m: (m // per, 0, 0))
    return _call(
        body, "ffn_bwd_ds", (t // TM,),
        [row, ex, row, _whole(wd), blk, blk],
        [blk, blk, row, ex],
        [_sds((N_CHIPS, t, fs), BF), _sds((N_CHIPS, t, fs), BF), _sds((t, d), BF), _sds((nb, 1, d), F32)],
        (dxo, gt, f, wd, p, q), carry=carry)


TM_X = 256


def _ffn_bwd_x(dxo, gt, f, wd, p, q, wgu, x, g, sc, seq, coef):
    t, d = dxo.shape
    fs = p.shape[-1]
    per = seq // TM_X
    nb = t // seq

    def body(dxo_ref, gt_ref, f_ref, wd_ref, p_ref, q_ref, w_ref, x_ref, g_ref, sc_ref,
             da_ref, du_ref, df_ref, dgt_ref, dx_ref, dsh_ref, dsc_ref, dg_ref):
        m = pl.program_id(0)
        dxv = dxo_ref[...]
        df = ((coef * gt_ref[...]) * dxv).astype(BF)
        df_ref[...] = df
        _acc_rows(dgt_ref, coef * jnp.sum(dxv * f_ref[...], axis=0, keepdims=True), m % per == 0)
        dh = None
        for j in range(N_CHIPS):
            ds = _nt(df, wd_ref[j, 0])
            da = (ds * p_ref[j].astype(F32)).astype(BF)
            du = (ds * q_ref[j].astype(F32)).astype(BF)
            da_ref[j] = da
            du_ref[j] = du
            part = _nt(da, w_ref[j, 0]) + _nt(du, w_ref[j, 1])
            dh = part if dh is None else dh + part
        dx, dsh, dsc, dg = _modnorm_bwd_tile(dh, x_ref[...], g_ref[...], sc_ref[...], dxv)
        dx_ref[...] = dx
        _acc_rows(dsh_ref, dsh, m % per == 0)
        _acc_rows(dsc_ref, dsc, m % per == 0)
        _acc_rows(dg_ref, dg, m == 0)

    row = pl.BlockSpec((TM_X, d), lambda m: (m, 0))
    blk = pl.BlockSpec((N_CHIPS, TM_X, fs), lambda m: (0, m, 0))
    ex = pl.BlockSpec((None, 1, d), lambda m: (m // per, 0, 0))
    vec = pl.BlockSpec((1, d), lambda m: (0, 0))
    exs = _sds((nb, 1, d), F32)
    return pl.pallas_call(
        body, name="ffn_bwd_x", grid=(t // TM_X,),
        in_specs=[row, ex, row, _whole(wd), blk, blk, _whole(wgu), row, vec, ex],
        out_specs=[blk, blk, row, ex, row, ex, ex, vec],
        out_shape=[_sds((N_CHIPS, t, fs), BF), _sds((N_CHIPS, t, fs), BF), _sds((t, d), BF), exs,
                   _sds((t, d), F32), exs, exs, _sds((1, d), F32)],
        compiler_params=_cp(1))(dxo, gt, f, wd, p, q, wgu, x, g, sc)


TK_W = 1024


def _ffn_bwd_w(h, da, du, s, df):
    t, d = h.shape
    fs = da.shape[-1]

    def body(h_ref, da_ref, du_ref, s_ref, df_ref, o_ref):
        kt = pl.program_id(1)
        hv = h_ref[...]
        parts = (_tn(da_ref[...], hv), _tn(du_ref[...], hv), _tn(s_ref[...], df_ref[...]))

        @pl.when(kt == 0)
        def _():
            for i, p in enumerate(parts):
                o_ref[i] = p

        @pl.when(kt != 0)
        def _():
            for i, p in enumerate(parts):
                o_ref[i] += p

    row = pl.BlockSpec((TK_W, d), lambda j, kt: (kt, 0))
    blk = pl.BlockSpec((None, TK_W, fs), lambda j, kt: (j, kt, 0))
    return pl.pallas_call(
        body, name="ffn_bwd_w", grid=(N_CHIPS, t // TK_W),
        in_specs=[row, blk, blk, blk, row],
        out_specs=pl.BlockSpec((None, 3, fs, d), lambda j, kt: (j, 0, 0, 0)),
        out_shape=_sds((N_CHIPS, 3, fs, d), F32),
        compiler_params=_cp(2))(h, da, du, s, df)


def _ffn_bwd_dh(da, du, wgu, x, g, sc, dxo, seq, carry=None):
    _, t, fs = da.shape
    d = x.shape[-1]
    per = seq // TM
    nb = t // seq

    def body(da_ref, du_ref, w_ref, x_ref, g_ref, sc_ref, dxo_ref, dx_ref, dsh_ref, dsc_ref, dg_ref):
        m = pl.program_id(0)
        dh = _nt(da_ref[0], w_ref[0, 0]) + _nt(du_ref[0], w_ref[0, 1])
        for j in range(1, N_CHIPS):
            dh = dh + _nt(da_ref[j], w_ref[j, 0]) + _nt(du_ref[j], w_ref[j, 1])
        dx, dsh, dsc, dg = _modnorm_bwd_tile(dh, x_ref[...], g_ref[...], sc_ref[...], dxo_ref[...])
        dx_ref[...] = dx
        _acc_rows(dsh_ref, dsh, m % per == 0)
        _acc_rows(dsc_ref, dsc, m % per == 0)
        _acc_rows(dg_ref, dg, m == 0)

    row = pl.BlockSpec((TM, d), lambda m: (m, 0))
    blk = pl.BlockSpec((N_CHIPS, TM, fs), lambda m: (0, m, 0))
    ex = pl.BlockSpec((None, 1, d), lambda m: (m // per, 0, 0))
    vec = pl.BlockSpec((1, d), lambda m: (0, 0))
    return _call(
        body, "ffn_bwd_dh", (t // TM,),
        [blk, blk, _whole(wgu), row, vec, ex, row],
        [row, ex, ex, vec],
        [_sds((t, d), F32), _sds((nb, 1, d), F32), _sds((nb, 1, d), F32), _sds((1, d), F32)],
        (da, du, wgu, x, g, sc, dxo), carry=carry)


def _qkv_proj(x, g, sc, sh, w_in, seq, carry=None):
    t, d = x.shape
    wc = w_in.shape[-1]
    per = seq // TM

    dils = [dil for _, dil in DIL_CONFIGS if dil > 1]

    def body(x_ref, g_ref, sc_ref, sh_ref, w_ref, h_ref, o_ref, *rest):
        res_refs, buf = rest[:len(dils)], rest[len(dils)]
        hv = _modnorm_tile(x_ref, g_ref, sc_ref, sh_ref)
        h_ref[...] = hv
        for j in range(N_CHIPS):
            rf = _nn(hv, w_ref[j, 0])
            r = rf.astype(BF)
            for a, lc, off, width in _col_pieces(j, wc):
                o_ref[a, :, lc:lc + width] = r[:, off:off + width]
                if a < 3:
                    continue
                for c0 in range(0, width, 128):
                    cg = (lc + c0) // 128
                    buf[...] = rf[:, off + c0:off + c0 + 128]
                    for ref, dil in zip(res_refs, dils):
                        for rr in range(dil):
                            ref[a - 3, :, rr * D_GRP + cg * 128:rr * D_GRP + (cg + 1) * 128] = (
                                buf[pl.ds(rr, TM // dil, stride=dil), :].astype(BF))

    row = pl.BlockSpec((TM, d), lambda m: (m, 0))
    ex = pl.BlockSpec((None, 1, d), lambda m: (m // per, 0, 0))
    return _call(
        body, "qkv_proj", (t // TM,),
        [row, pl.BlockSpec((1, d), lambda m: (0, 0)), ex, ex, _whole(w_in)],
        [row, pl.BlockSpec((6, TM, D_GRP), lambda m: (0, m, 0))]
        + [pl.BlockSpec((3, TM // dil, dil * D_GRP), lambda m: (0, m, 0)) for dil in dils],
        [_sds((t, d), BF), _sds((6, t, D_GRP), BF)] + [_sds((3, t // dil, dil * D_GRP), BF) for dil in dils],
        (x, g, sc, sh, w_in), scratch=[pltpu.VMEM((TM, 128), F32)], carry=carry)


def _col_pieces(j, wc):
    out, off = [], 0
    while off < wc:
        a, lc = divmod(j * wc + off, D_GRP)
        width = min(D_GRP - lc, wc - off)
        out.append((a, lc, off, width))
        off += width
    return out


def _chip_cols(g6_ref, j, wc):
    return jnp.concatenate([g6_ref[a, :, lc:lc + width] for a, lc, _, width in _col_pieces(j, wc)], axis=1)


def _mix_out(on_sb, on_dil, w_out, x, gt, seq):
    t, d = x.shape
    per = seq // TM

    def body(a_ref, b_ref, w_ref, x_ref, gt_ref, t_ref, xo_ref):
        tv = _nn(a_ref[...], w_ref[0:D_GRP, :]) + _nn(b_ref[...], w_ref[D_GRP:2 * D_GRP, :])
        t_ref[...] = tv
        xo_ref[...] = x_ref[...] + gt_ref[...] * tv

    row = pl.BlockSpec((TM, d), lambda m: (m, 0))
    half = pl.BlockSpec((TM, D_GRP), lambda m: (m, 0))
    return pl.pallas_call(
        body, name="mix_out", grid=(t // TM,),
        in_specs=[half, half, pl.BlockSpec((2 * D_GRP, d), lambda m: (0, 0)), row,
                  pl.BlockSpec((None, 1, d), lambda m: (m // per, 0, 0))],
        out_specs=[row, row],
        out_shape=[_sds((t, d), F32), _sds((t, d), F32)],
        compiler_params=_cp(1))(on_sb, on_dil, w_out, x, gt)


def _sb_masks():
    lane = lax.broadcasted_iota(jnp.int32, (1, 2 * HEAD_DIM), 1)
    hm0 = lane < HEAD_DIM
    rel = lax.broadcasted_iota(jnp.int32, (TQ, KB), 0) - lax.broadcasted_iota(jnp.int32, (TQ, KB), 1)
    kr = lax.broadcasted_iota(jnp.int32, (KB, KB), 0)
    kc = lax.broadcasted_iota(jnp.int32, (KB, KB), 1)
    return hm0, rel, kr, kc


def _stack_pair(x, hm0):
    zero = jnp.zeros_like(x)
    return jnp.concatenate([jnp.where(hm0, x, zero), jnp.where(hm0, zero, x)], axis=0)


def _headnorm_pair(o, gv, hm0):
    o2 = o * o
    ms0 = jnp.sum(jnp.where(hm0, o2, 0.0), axis=-1, keepdims=True) * (1.0 / HEAD_DIM)
    ms1 = jnp.sum(jnp.where(hm0, 0.0, o2), axis=-1, keepdims=True) * (1.0 / HEAD_DIM)
    r = jnp.where(hm0, lax.rsqrt(ms0 + EPS), lax.rsqrt(ms1 + EPS))
    return (o * r) * gv


SB_DEAD = -104.0


def _alive(c_l):
    return (jnp.max(c_l) > SB_DEAD).astype(jnp.int32)


def _sb_fwd(qkv6, g_sb, nb, seq, carry=None):
    nq = seq // TQ

    def body(q_ref, k_ref, v_ref, g_ref, o_ref, on_ref):
        qi = pl.program_id(2)
        hm0, rel, kr, kc = _sb_masks()
        upper = (kr > kc).astype(BF)
        qs = _stack_pair(q_ref[...], hm0)
        kd = qi // (KB // TQ)
        causal2 = (jnp.concatenate([rel, rel], axis=0) + (qi % (KB // TQ)) * TQ) > 0

        def block(kj, causal, c_l, acc):
            ks = pl.multiple_of(kj * KB, KB)
            z = _nt(qs, k_ref[pl.ds(ks, KB), :]) * SCALE
            sp = _softplus(z)
            ln = -sp if causal is None else jnp.where(causal, -sp, 0.0)
            suf = _nn2(ln, upper)
            w = jnp.exp((z - sp) + (suf + c_l))
            if causal is not None:
                w = jnp.where(causal, w, 0.0)
            return c_l + (suf[:, 0:1] + ln[:, 0:1]), acc + _nn(w.astype(BF), v_ref[pl.ds(ks, KB), :])

        c_l, acc = block(kd, causal2, jnp.zeros((2 * TQ, 1), F32), jnp.zeros((2 * TQ, 2 * HEAD_DIM), F32))

        def cond(carry):
            return jnp.logical_and(carry[0] <= kd, carry[1] > 0)

        def kbody(carry):
            it, _, c_l, acc = carry
            c_l, acc = block(kd - it, None, c_l, acc)
            return it + 1, _alive(c_l), c_l, acc

        acc = lax.while_loop(cond, kbody, (jnp.int32(1), _alive(c_l), c_l, acc))[3]
        o = jnp.where(hm0, acc[:TQ], acc[TQ:])
        o_ref[...] = o
        on_ref[...] = _headnorm_pair(o, g_ref[...], hm0).astype(BF)

    w = 2 * HEAD_DIM
    full = lambda i: pl.BlockSpec((None, None, seq, w), lambda b, hp, q: (i, b, 0, hp))
    qblk = pl.BlockSpec((None, None, TQ, w), lambda b, hp, q: (0, b, q, hp))
    oblk = pl.BlockSpec((None, TQ, w), lambda b, hp, q: (b, q, hp))
    return _call(
        body, "sb_fwd", (nb, N_HEADS // 2, nq),
        [qblk, full(1), full(2), pl.BlockSpec((1, w), lambda b, hp, q: (0, hp))],
        [oblk, oblk],
        [_sds((nb, seq, D_GRP), F32), _sds((nb, seq, D_GRP), BF)],
        (qkv6, qkv6, qkv6, g_sb), carry=carry)


def _sb_bwd(qkv6, do, nb, seq, carry=None):
    nq = seq // TQ
    nk = seq // KB

    def body(q_ref, k_ref, v_ref, do_ref, out_ref, dk_acc, dv_acc, g_st, s_st):
        qi = pl.program_id(2)
        hm0, rel, kr, kc = _sb_masks()
        upper = (kr > kc).astype(BF)
        lower = (kr < kc).astype(BF)

        @pl.when(qi == 0)
        def _():
            dk_acc[...] = jnp.zeros_like(dk_acc)
            dv_acc[...] = jnp.zeros_like(dv_acc)

        qs = _stack_pair(q_ref[...], hm0)
        dos = _stack_pair(do_ref[...], hm0).astype(BF)
        kd = qi // (KB // TQ)
        causal2 = (jnp.concatenate([rel, rel], axis=0) + (qi % (KB // TQ)) * TQ) > 0

        def weights(kj, causal, c_l):
            ks = pl.multiple_of(kj * KB, KB)
            vb = v_ref[pl.ds(ks, KB), :]
            z = _nt(qs, k_ref[pl.ds(ks, KB), :]) * SCALE
            sp = _softplus(z)
            ln = -sp if causal is None else jnp.where(causal, -sp, 0.0)
            suf = _nn2(ln, upper)
            lsz = z - sp
            w = jnp.exp(lsz + (suf + c_l))
            if causal is not None:
                w = jnp.where(causal, w, 0.0)
            g_st[kj] = w * _nt(dos, vb)
            s_st[kj] = jnp.exp(lsz)
            dv_acc[pl.ds(ks, KB), :] += _tn(w.astype(BF), dos)
            return c_l + (suf[:, 0:1] + ln[:, 0:1])

        zc = jnp.zeros((2 * TQ, 1), F32)
        c_l = weights(kd, causal2, zc)

        def acond(carry):
            return jnp.logical_and(carry[0] <= kd, carry[1] > 0)

        def abody(carry):
            c_l = weights(kd - carry[0], None, carry[2])
            return carry[0] + 1, _alive(c_l), c_l

        n_used = lax.while_loop(acond, abody, (jnp.int32(1), _alive(c_l), c_l))[0]

        def grads(kj, causal, c_g, dq):
            ks = pl.multiple_of(kj * KB, KB)
            kb = k_ref[pl.ds(ks, KB), :]
            g = g_st[kj]
            sig = s_st[kj]
            pre = _nn(g.astype(BF), lower)
            dz = g * (1.0 - sig) - sig * (pre + c_g)
            if causal is not None:
                dz = jnp.where(causal, dz, 0.0)
            dzb = (dz * SCALE).astype(BF)
            dk_acc[pl.ds(ks, KB), :] += _tn(dzb, qs)
            return c_g + (pre[:, KB - 1:KB] + g[:, KB - 1:KB]), dq + _nn(dzb, kb)

        c_g, dq = lax.fori_loop(kd - n_used + 1, kd, lambda kj, cr: grads(kj, None, *cr),
                                (zc, jnp.zeros((2 * TQ, 2 * HEAD_DIM), F32)))
        _, dq = grads(kd, causal2, c_g, dq)
        dq = jnp.where(hm0, dq[:TQ], dq[TQ:])
        out_ref[0, pl.ds(pl.multiple_of(qi * TQ, TQ), TQ), :] = dq.astype(BF)

        @pl.when(qi == nq - 1)
        def _():
            out_ref[1] = dk_acc[...].astype(BF)
            out_ref[2] = dv_acc[...].astype(BF)

    w = 2 * HEAD_DIM
    full = lambda i: pl.BlockSpec((None, None, seq, w), lambda b, hp, q: (i, b, 0, hp))
    qblk = pl.BlockSpec((None, None, TQ, w), lambda b, hp, q: (0, b, q, hp))
    oblk = pl.BlockSpec((None, TQ, w), lambda b, hp, q: (b, q, hp))
    return _call(
        body, "sb_bwd", (nb, N_HEADS // 2, nq),
        [qblk, full(1), full(2), oblk],
        [pl.BlockSpec((3, None, seq, w), lambda b, hp, q: (0, b, 0, hp))],
        [_sds((6, nb, seq, D_GRP), BF)], (qkv6, qkv6, qkv6, do),
        scratch=[pltpu.VMEM((seq, w), F32), pltpu.VMEM((seq, w), F32),
                 pltpu.VMEM((nk, 2 * TQ, KB), F32), pltpu.VMEM((nk, 2 * TQ, KB), F32)],
        carry=carry)


def _t5_bucket(n):
    max_exact = N_BUCKETS // 2
    nf = np.maximum(n, 1).astype(np.float32)
    large = max_exact + (np.log(nf / max_exact) / math.log(MAX_DISTANCE / max_exact)
                         * (N_BUCKETS - max_exact)).astype(np.int32)
    large = np.minimum(large, N_BUCKETS - 1)
    return np.where(n < max_exact, n, large).astype(np.int32)


def _bucket_map(dilation):
    step = BLOCK + np.arange(BLOCK)[:, None] - np.arange(2 * BLOCK)[None, :]
    return _t5_bucket(np.clip(step, 0, N_STEPS) * dilation)


GRP_HEADS = 4
GRP_W = GRP_HEADS * HEAD_DIM


def _dil_masks():
    lane = lax.broadcasted_iota(jnp.int32, (1, GRP_W), 1)
    heads = [jnp.logical_and(lane >= HEAD_DIM * i, lane < HEAD_DIM * (i + 1)) for i in range(GRP_HEADS)]
    iq = jnp.bitwise_and(lax.broadcasted_iota(jnp.int32, (GRP_HEADS * BLOCK, BLOCK), 0), BLOCK - 1)
    ik = lax.broadcasted_iota(jnp.int32, (GRP_HEADS * BLOCK, BLOCK), 1)
    return heads, ik <= iq, ik >= iq


def _stack_heads(x, heads):
    zero = jnp.zeros_like(x)
    return jnp.concatenate([jnp.where(hm, x, zero) for hm in heads], axis=0)


def _unstack_heads(xs, heads):
    out = xs[0:BLOCK]
    for i in range(1, GRP_HEADS):
        out = jnp.where(heads[i], xs[i * BLOCK:(i + 1) * BLOCK], out)
    return out


def _dil_rows(n):
    rs = pl.multiple_of(n * BLOCK, BLOCK)
    ps = pl.multiple_of(jnp.maximum(n - 1, 0) * BLOCK, BLOCK)
    return pl.ds(rs, BLOCK), pl.ds(ps, BLOCK)


def _dil_probs(qs, kc, kp, b_ref, gi, valid_c, valid_p):
    rows = slice(gi * GRP_HEADS * BLOCK, (gi + 1) * GRP_HEADS * BLOCK)
    zc = _nt(qs, kc) * SCALE + b_ref[rows, BLOCK:2 * BLOCK]
    zp = _nt(qs, kp) * SCALE + b_ref[rows, 0:BLOCK]
    zc = jnp.where(valid_c, zc, NEG_INF)
    zp = jnp.where(valid_p, zp, NEG_INF)
    m = jnp.maximum(jnp.max(zc, axis=-1, keepdims=True), jnp.max(zp, axis=-1, keepdims=True))
    ec = jnp.exp(zc - m)
    ep = jnp.exp(zp - m)
    den = jnp.sum(ec, axis=-1, keepdims=True) + jnp.sum(ep, axis=-1, keepdims=True)
    return ec, ep, den, m


def _dil_fwd(qkv6r, base, bias, nb, sub_len, dilation):
    n_blk = sub_len // BLOCK

    def body(q_ref, k_ref, v_ref, b_ref, o_ref, l_ref):
        heads, valid_c, valid_p0 = _dil_masks()

        def nbody(n, carry):
            cur, prev = _dil_rows(n)
            valid_p = jnp.logical_and(valid_p0, n > 0)
            for gi in range(N_HEADS // GRP_HEADS):
                lanes = slice(gi * GRP_W, (gi + 1) * GRP_W)
                qs = _stack_heads(q_ref[cur, lanes], heads)
                ec, ep, den, m = _dil_probs(qs, k_ref[cur, lanes], k_ref[prev, lanes], b_ref, gi, valid_c, valid_p)
                o = (_nn(ec.astype(BF), v_ref[cur, lanes]) + _nn(ep.astype(BF), v_ref[prev, lanes])) / den
                o_ref[cur, lanes] = _unstack_heads(o, heads)
                l_ref[cur, lanes] = _unstack_heads(jnp.broadcast_to(m + jnp.log(den), o.shape), heads)
            return carry

        lax.fori_loop(0, n_blk, nbody, 0)

    seqblk = lambda i: pl.BlockSpec((None, None, sub_len, D_GRP), lambda b, r: (i, b, 0, r))
    oblk = pl.BlockSpec((None, sub_len, D_GRP), lambda b, r: (b, 0, r))
    shp = _sds((nb, sub_len, dilation * D_GRP), F32)
    return pl.pallas_call(
        body, name="dil_fwd_%d" % dilation, grid=(nb, dilation),
        in_specs=[seqblk(base), seqblk(base + 1), seqblk(base + 2), _whole(bias)],
        out_specs=[oblk, oblk], out_shape=[shp, shp],
        compiler_params=_cp(2))(qkv6r, qkv6r, qkv6r, bias)


def _dil_bwd(qkv6r, base, bias, do_c, dd_c, nb, sub_len, dilation, carry=None):
    n_blk = sub_len // BLOCK

    def body(q_ref, k_ref, v_ref, b_ref, do_ref, dd_ref, out_ref, a_ref, dk_acc, dv_acc):
        heads, valid_c, valid_p0 = _dil_masks()
        first = jnp.logical_and(pl.program_id(0) == 0, pl.program_id(1) == 0)

        @pl.when(first)
        def _():
            a_ref[...] = jnp.zeros_like(a_ref)

        dk_acc[...] = jnp.zeros_like(dk_acc)
        dv_acc[...] = jnp.zeros_like(dv_acc)

        def nbody(n, carry):
            cur, prev = _dil_rows(n)
            valid_p = jnp.logical_and(valid_p0, n > 0)
            for gi in range(N_HEADS // GRP_HEADS):
                lanes = slice(gi * GRP_W, (gi + 1) * GRP_W)
                kc, kp = k_ref[cur, lanes], k_ref[prev, lanes]
                vc, vp = v_ref[cur, lanes], v_ref[prev, lanes]
                qs = _stack_heads(q_ref[cur, lanes], heads)
                dos = _stack_heads(do_ref[cur, lanes], heads).astype(BF)
                dds = jnp.sum(_stack_heads(dd_ref[cur, lanes], heads), axis=-1, keepdims=True) * (1.0 / HEAD_DIM)
                ec, ep, den, _ = _dil_probs(qs, kc, kp, b_ref, gi, valid_c, valid_p)
                inv = 1.0 / den
                pc = ec * inv
                pp = ep * inv
                dzc = pc * (_nt(dos, vc) + dds)
                dzp = pp * (_nt(dos, vp) + dds)
                rows = slice(gi * GRP_HEADS * BLOCK, (gi + 1) * GRP_HEADS * BLOCK)
                a_ref[rows, BLOCK:2 * BLOCK] += dzc
                a_ref[rows, 0:BLOCK] += dzp
                dzcb = (dzc * SCALE).astype(BF)
                dzpb = (dzp * SCALE).astype(BF)
                out_ref[0, cur, lanes] = _unstack_heads(_nn(dzcb, kc) + _nn(dzpb, kp), heads).astype(BF)
                dk_acc[cur, lanes] += _tn(dzcb, qs)
                dk_acc[prev, lanes] += _tn(dzpb, qs)
                dv_acc[cur, lanes] += _tn(pc.astype(BF), dos)
                dv_acc[prev, lanes] += _tn(pp.astype(BF), dos)
            return carry

        lax.fori_loop(0, n_blk, nbody, 0)
        out_ref[1] = dk_acc[...].astype(BF)
        out_ref[2] = dv_acc[...].astype(BF)

    seqblk = lambda i: pl.BlockSpec((None, None, sub_len, D_GRP), lambda b, r: (i, b, 0, r))
    oblk = pl.BlockSpec((None, sub_len, D_GRP), lambda b, r: (b, 0, r))
    return _call(
        body, "dil_bwd_%d" % dilation, (nb, dilation),
        [seqblk(base), seqblk(base + 1), seqblk(base + 2), _whole(bias), oblk, oblk],
        [pl.BlockSpec((3, None, sub_len, D_GRP), lambda b, r: (0, b, 0, r)),
         pl.BlockSpec((N_HEADS * BLOCK, 2 * BLOCK), lambda b, r: (0, 0))],
        [_sds((3, nb, sub_len, dilation * D_GRP), BF), _sds((N_HEADS * BLOCK, 2 * BLOCK), F32)],
        (qkv6r, qkv6r, qkv6r, bias, do_c, dd_c),
        scratch=[pltpu.VMEM((sub_len, D_GRP), F32)] * 2, carry=carry)


def _group_ones():
    idx = np.arange(D_GRP) // HEAD_DIM
    return jnp.asarray((idx[:, None] == idx[None, :]).astype(np.float32), dtype=BF)


def _dil_alphas(l1, l4, l16):
    mx = jnp.maximum(jnp.maximum(l1, l4), l16)
    e1 = jnp.exp(l1 - mx)
    e4 = jnp.exp(l4 - mx)
    e16 = jnp.exp(l16 - mx)
    den = e1 + e4 + e16
    return e1 / den, e4 / den, e16 / den


def _residue_spec(dil):
    return pl.BlockSpec((TM // dil, dil * D_GRP), lambda m: (m, 0))


def _from_residue(src, dil, cg, buf):
    if dil == 1:
        return src[:, cg * 128:(cg + 1) * 128]
    for r in range(dil):
        buf[pl.ds(r, TM // dil, stride=dil), :] = src[:, r * D_GRP + cg * 128:r * D_GRP + (cg + 1) * 128]
    return buf[...]


def _to_residue(dst, dil, cg, buf, val):
    if dil == 1:
        dst[:, cg * 128:(cg + 1) * 128] = val.astype(dst.dtype)
        return
    buf[...] = val
    for r in range(dil):
        dst[:, r * D_GRP + cg * 128:r * D_GRP + (cg + 1) * 128] = (
            buf[pl.ds(r, TM // dil, stride=dil), :].astype(dst.dtype))


def _pair_sum(x, hm0):
    s0 = jnp.sum(jnp.where(hm0, x, 0.0), axis=-1, keepdims=True)
    s1 = jnp.sum(jnp.where(hm0, 0.0, x), axis=-1, keepdims=True)
    return jnp.where(hm0, s0, s1)


def _dil_comb(os, ls, g_dil):
    t = os[0].shape[0]
    dils = [dil for _, dil in DIL_CONFIGS]

    def body(o1, l1, o4, l4, o16, l16, g_ref, o_ref, on_ref, b0, b1, b2, b3):
        hm0 = lax.broadcasted_iota(jnp.int32, (1, 128), 1) < HEAD_DIM
        for cg in range(D_GRP // 128):
            lanes = slice(cg * 128, (cg + 1) * 128)
            ov = [_from_residue(src, dil, cg, buf) for src, dil, buf in zip((o1, o4, o16), dils, (None, b0, b1))]
            lv = [_from_residue(src, dil, cg, buf) for src, dil, buf in zip((l1, l4, l16), dils, (None, b2, b3))]
            a1, a4, a16 = _dil_alphas(*lv)
            o = a1 * ov[0] + a4 * ov[1] + a16 * ov[2]
            o_ref[:, lanes] = o
            on_ref[:, lanes] = _headnorm_pair(o, g_ref[:, lanes], hm0).astype(BF)

    blk = pl.BlockSpec((TM, D_GRP), lambda m: (m, 0))
    specs = [_residue_spec(dil) for dil in dils for _ in range(2)]
    return pl.pallas_call(
        body, name="dil_comb", grid=(t // TM,),
        in_specs=specs + [pl.BlockSpec((1, D_GRP), lambda m: (0, 0))],
        out_specs=[blk, blk],
        out_shape=[_sds((t, D_GRP), F32), _sds((t, D_GRP), BF)],
        scratch_shapes=[pltpu.VMEM((TM, 128), F32)] * 4,
        compiler_params=_cp(1))(os[0], ls[0], os[1], ls[1], os[2], ls[2], g_dil)


def _dil_comb_bwd(do, os, ls):
    t = do.shape[0]
    dils = [dil for _, dil in DIL_CONFIGS]

    def body(do_ref, o1, l1, o4, l4, o16, l16, d1, d4, d16, e1, e4, e16, b0, b1, b2, b3):
        hm0 = lax.broadcasted_iota(jnp.int32, (1, 128), 1) < HEAD_DIM
        for cg in range(D_GRP // 128):
            dov = do_ref[:, cg * 128:(cg + 1) * 128]
            ov = [_from_residue(src, dil, cg, buf) for src, dil, buf in zip((o1, o4, o16), dils, (None, b0, b1))]
            lv = [_from_residue(src, dil, cg, buf) for src, dil, buf in zip((l1, l4, l16), dils, (None, b2, b3))]
            al = _dil_alphas(*lv)
            sbar = al[0] * _pair_sum(dov * ov[0], hm0)
            for a_c, o_c in zip(al[1:], ov[1:]):
                sbar = sbar + a_c * _pair_sum(dov * o_c, hm0)
            for a_c, dil, dref, eref in zip(al, dils, (d1, d4, d16), (e1, e4, e16)):
                _to_residue(dref, dil, cg, b0, a_c * dov)
                _to_residue(eref, dil, cg, b1, -a_c * sbar)

    specs = [_residue_spec(dil) for dil in dils]
    return pl.pallas_call(
        body, name="dil_comb_bwd", grid=(t // TM,),
        in_specs=[pl.BlockSpec((TM, D_GRP), lambda m: (m, 0))] + [sp for sp in specs for _ in range(2)],
        out_specs=specs + specs,
        out_shape=[_sds((t // dil, dil * D_GRP), BF) for dil in dils]
        + [_sds((t // dil, dil * D_GRP), F32) for dil in dils],
        scratch_shapes=[pltpu.VMEM((TM, 128), F32)] * 4,
        compiler_params=_cp(1))(do, os[0], ls[0], os[1], ls[1], os[2], ls[2])


def _dqkv_dil_sum(ds, dqkv6):
    t = dqkv6.shape[1]
    dils = [dil for _, dil in DIL_CONFIGS]

    def body(*refs):
        srcs, o_ref, acc = refs[:len(dils)], refs[len(dils) + 1], refs[len(dils) + 2]
        for a in range(3):
            for cg in range(D_GRP // 128):
                for src, dil in zip(srcs, dils):
                    for r in range(dil):
                        part = src[a, :, r * D_GRP + cg * 128:r * D_GRP + (cg + 1) * 128].astype(F32)
                        rows = pl.ds(r, TM // dil, stride=dil) if dil > 1 else slice(None)
                        if dil == dils[0]:
                            acc[rows, :] = part
                        else:
                            acc[rows, :] += part
                o_ref[a, :, cg * 128:(cg + 1) * 128] = acc[...].astype(BF)

    return pl.pallas_call(
        body, name="dqkv_dil_sum", grid=(t // TM,),
        in_specs=[pl.BlockSpec((3, TM // dil, dil * D_GRP), lambda m: (0, m, 0)) for dil in dils]
        + [pl.BlockSpec(memory_space=pl.ANY)],
        out_specs=pl.BlockSpec((3, TM, D_GRP), lambda m: (1, m, 0)),
        out_shape=_sds((6, t, D_GRP), BF), input_output_aliases={len(dils): 0},
        scratch_shapes=[pltpu.VMEM((TM, 128), F32)],
        compiler_params=_cp(1))(*ds, dqkv6)


def _relbias_grad(a_all, onehot):
    def body(a_ref, oh_ref, o_ref):
        acc = jnp.zeros((N_HEADS, N_BUCKETS), F32)
        for c in range(len(DIL_CONFIGS)):
            av = a_ref[c]
            hi = av.astype(BF)
            lo = (av - hi.astype(F32)).astype(BF)
            acc = acc + _nt(hi, oh_ref[c]) + _nt(lo, oh_ref[c])
        o_ref[...] = acc

    return pl.pallas_call(body, name="relbias_grad", out_shape=_sds((N_HEADS, N_BUCKETS), F32),
                          compiler_params=_cp())(a_all, onehot)


def _headnorm_bwd(dn, o, gv, mv):
    ms = _nn2(o * o, mv) * (1.0 / HEAD_DIM)
    r = lax.rsqrt(ms + EPS)
    nrm = o * r
    dg = jnp.sum(dn * nrm, axis=0, keepdims=True)
    dnn = dn * gv
    do = r * (dnn - nrm * (_nn2(dnn * nrm, mv) * (1.0 / HEAD_DIM)))
    return do, dg


def _mix_bwd_out(dx, gt, tv, w_out, o_sb, o_dil, on_sb, on_dil, g_sb, g_dil, ones_g, seq, carry=None):
    t, d = dx.shape
    per = seq // TM
    nb = t // seq

    def body(dx_ref, gt_ref, t_ref, w_ref, osb, odl, onsb, ondl, gsb, gdl, m_ref,
             dosb, dodl, dgt_ref, dgsb, dgdl, dw_ref):
        m = pl.program_id(0)
        dxv = dx_ref[...]
        dt = (gt_ref[...] * dxv).astype(BF)
        _acc_rows(dgt_ref, jnp.sum(dxv * t_ref[...], axis=0, keepdims=True), m % per == 0)
        mv = m_ref[...]
        don_sb = _nt(dt, w_ref[0:D_GRP, :])
        don_dl = _nt(dt, w_ref[D_GRP:2 * D_GRP, :])
        do1, dg1 = _headnorm_bwd(don_sb, osb[...], gsb[...], mv)
        do2, dg2 = _headnorm_bwd(don_dl, odl[...], gdl[...], mv)
        dosb[...] = do1
        dodl[...] = do2
        _acc_rows(dgsb, dg1, m == 0)
        _acc_rows(dgdl, dg2, m == 0)
        p1 = _tn(onsb[...], dt)
        p2 = _tn(ondl[...], dt)

        @pl.when(m == 0)
        def _():
            dw_ref[0:D_GRP, :] = p1
            dw_ref[D_GRP:2 * D_GRP, :] = p2

        @pl.when(m != 0)
        def _():
            dw_ref[0:D_GRP, :] += p1
            dw_ref[D_GRP:2 * D_GRP, :] += p2

    row = pl.BlockSpec((TM, d), lambda m: (m, 0))
    half = pl.BlockSpec((TM, D_GRP), lambda m: (m, 0))
    ex = pl.BlockSpec((None, 1, d), lambda m: (m // per, 0, 0))
    gvec = pl.BlockSpec((1, D_GRP), lambda m: (0, 0))
    wblk = pl.BlockSpec((2 * D_GRP, d), lambda m: (0, 0))
    return _call(
        body, "mix_bwd_out", (t // TM,),
        [row, ex, row, wblk, half, half, half, half, gvec, gvec, pl.BlockSpec((D_GRP, D_GRP), lambda m: (0, 0))],
        [half, half, ex, gvec, gvec, wblk],
        [_sds((t, D_GRP), F32), _sds((t, D_GRP), F32), _sds((nb, 1, d), F32),
         _sds((1, D_GRP), F32), _sds((1, D_GRP), F32), _sds((2 * D_GRP, d), F32)],
        (dx, gt, tv, w_out, o_sb, o_dil, on_sb, on_dil, g_sb, g_dil, ones_g), carry=carry)


def _dw_in(h, dqkv6, carry=None):
    t, d = h.shape
    wc = 6 * D_GRP // N_CHIPS

    def body(h_ref, g_ref, o_ref):
        kt = pl.program_id(0)
        hv = h_ref[...]
        for j in range(N_CHIPS):
            p = _tn(hv, _chip_cols(g_ref, j, wc))

            @pl.when(kt == 0)
            def _(p=p, j=j):
                o_ref[j, 0] = p

            @pl.when(kt != 0)
            def _(p=p, j=j):
                o_ref[j, 0] += p

    return _call(
        body, "dw_in", (t // TK_W,),
        [pl.BlockSpec((TK_W, d), lambda kt: (kt, 0)), pl.BlockSpec((6, TK_W, D_GRP), lambda kt: (0, kt, 0))],
        [pl.BlockSpec((N_CHIPS, 1, d, wc), lambda kt: (0, 0, 0, 0))],
        [_sds((N_CHIPS, 1, d, wc), F32)], (h, dqkv6), carry=carry)


def _mix_bwd_dh(dqkv6, w_in, x, g, sc, dxo, seq, carry=None):
    _, t, _ = dqkv6.shape
    d = x.shape[-1]
    wc = w_in.shape[-1]
    per = seq // TM
    nb = t // seq

    def body(g6_ref, w_ref, x_ref, g_ref, sc_ref, dxo_ref, dx_ref, dsh_ref, dsc_ref, dg_ref):
        m = pl.program_id(0)
        dh = _nt(_chip_cols(g6_ref, 0, wc), w_ref[0, 0])
        for j in range(1, N_CHIPS):
            dh = dh + _nt(_chip_cols(g6_ref, j, wc), w_ref[j, 0])
        dx, dsh, dsc, dg = _modnorm_bwd_tile(dh, x_ref[...], g_ref[...], sc_ref[...], dxo_ref[...])
        dx_ref[...] = dx
        _acc_rows(dsh_ref, dsh, m % per == 0)
        _acc_rows(dsc_ref, dsc, m % per == 0)
        _acc_rows(dg_ref, dg, m == 0)

    row = pl.BlockSpec((TM, d), lambda m: (m, 0))
    ex = pl.BlockSpec((None, 1, d), lambda m: (m // per, 0, 0))
    vec = pl.BlockSpec((1, d), lambda m: (0, 0))
    return _call(
        body, "mix_bwd_dh", (t // TM,),
        [pl.BlockSpec((6, TM, D_GRP), lambda m: (0, m, 0)), _whole(w_in), row, vec, ex, row],
        [row, ex, ex, vec],
        [_sds((t, d), F32), _sds((nb, 1, d), F32), _sds((nb, 1, d), F32), _sds((1, d), F32)],
        (dqkv6, w_in, x, g, sc, dxo), carry=carry)


def _ffn_down_loss(s, wd, x, gt, seq, coef, g, target):
    _, t, fs = s.shape
    d = x.shape[-1]
    per = seq // TM
    steps = t // TM

    def body(s_ref, w_ref, x_ref, gt_ref, g_ref, t_ref, f_ref, dx_ref, dg_ref, loss_ref, lacc):
        m = pl.program_id(0)
        f = _nn(s_ref[0], w_ref[0, 0])
        for j in range(1, N_CHIPS):
            f = f + _nn(s_ref[j], w_ref[j, 0])
        f_ref[...] = f
        xv = x_ref[...] + (coef * gt_ref[...]) * f
        gv = g_ref[...]
        r = lax.rsqrt(jnp.mean(xv * xv, axis=-1, keepdims=True) + EPS)
        n = xv * r
        err = n * gv - t_ref[...]
        dy = err * (1.0 / d)
        _acc_rows(dg_ref, jnp.sum(dy * n, axis=0, keepdims=True), m == 0)
        dn = dy * gv
        dx_ref[...] = r * (dn - n * jnp.mean(dn * n, axis=-1, keepdims=True))
        _acc_rows(lacc, jnp.sum(err * err, axis=0, keepdims=True), m == 0)

        @pl.when(m == steps - 1)
        def _():
            tot = jnp.sum(lacc[...], axis=-1, keepdims=True) * (0.5 / d)
            loss_ref[...] = jnp.broadcast_to(tot, (1, 128))

    row = pl.BlockSpec((TM, d), lambda m: (m, 0))
    vec = pl.BlockSpec((1, d), lambda m: (0, 0))
    return pl.pallas_call(
        body, name="ffn_down_loss", grid=(steps,),
        in_specs=[pl.BlockSpec((N_CHIPS, TM, fs), lambda m: (0, m, 0)), _whole(wd), row,
                  pl.BlockSpec((None, 1, d), lambda m: (m // per, 0, 0)), vec, row],
        out_specs=[row, row, vec, pl.BlockSpec((1, 128), lambda m: (0, 0))],
        out_shape=[_sds((t, d), F32), _sds((t, d), F32), _sds((1, d), F32), _sds((1, 128), F32)],
        scratch_shapes=[pltpu.VMEM((1, d), F32)],
        compiler_params=_cp(1))(s, wd, x, gt, g, target)


def _row_tile(rows, cols):
    best = rows
    for tr in range(8, rows + 1, 8):
        if rows % tr == 0 and tr * cols * 4 <= (1 << 20):
            best = tr
    if best * cols * 4 > (1 << 21):
        best = 8
    return best


def _adamw(w, g_arr, g_sel, m, v):
    rows, cols = w.shape
    tr = _row_tile(rows, cols)
    b1c = 1.0 - ADAM_B1 ** ADAM_STEP
    b2c = 1.0 - ADAM_B2 ** ADAM_STEP

    def body(w_ref, g_ref, m_ref, v_ref, go_ref, d_ref, mo_ref, vo_ref):
        gv = g_ref[...]
        mn = ADAM_B1 * m_ref[...] + (1.0 - ADAM_B1) * gv
        vn = ADAM_B2 * v_ref[...] + (1.0 - ADAM_B2) * (gv * gv)
        go_ref[...] = gv
        mo_ref[...] = mn
        vo_ref[...] = vn
        d_ref[...] = -ADAM_LR * ((mn / b1c) / (jnp.sqrt(vn / b2c) + ADAM_EPS) + ADAM_WD * w_ref[...])

    blk = pl.BlockSpec((tr, cols), lambda i: (i, 0))
    shp = _sds((rows, cols), F32)
    return pl.pallas_call(
        body, name="adamw", grid=(rows // tr,),
        in_specs=[blk, pl.BlockSpec((None, tr, cols), lambda i: (g_sel, i, 0)), blk, blk],
        out_specs=[blk] * 4, out_shape=[shp] * 4,
        compiler_params=_cp(1))(w, g_arr, m, v)


def _flip(v, bit):
    return 1 - v if bit else v


def _my_place():
    x, y, c = lax.axis_index("x"), lax.axis_index("y"), lax.axis_index("c")
    return x, y, c


class _Exchange:
    def __init__(self, operands, out_shape, aliases, sems, start, finish):
        self.operands, self.out_shape, self.aliases, self.sems = list(operands), list(out_shape), dict(aliases), list(sems)
        self.start, self.finish = start, finish


def _join(exchanges):
    exchanges = [e for e in exchanges if e is not None]
    if not exchanges:
        return None
    ops, outs, sems, aliases, spans = [], [], [], {}, []
    for e in exchanges:
        spans.append((len(ops), len(outs), len(sems), e))
        for i, j in e.aliases.items():
            aliases[len(ops) + i] = len(outs) + j
        ops += e.operands
        outs += e.out_shape
        sems += e.sems

    def run(which):
        def go(ins, res, sm):
            for io, oo, so, e in spans:
                getattr(e, which)(ins[io:io + len(e.operands)], res[oo:oo + len(e.out_shape)], sm[so:so + len(e.sems)])
        return go

    return _Exchange(ops, outs, aliases, sems, run("start"), run("finish"))


def _call(body, name, grid, in_specs, out_specs, out_shape, args, scratch=(), carry=None, io_alias=None):
    in_specs, out_specs, out_shape, scratch = list(in_specs), list(out_specs), list(out_shape), list(scratch)
    io_alias = dict(io_alias or {})
    if carry is None:
        return pl.pallas_call(body, name=name, grid=grid, in_specs=in_specs, out_specs=out_specs,
                              out_shape=out_shape, scratch_shapes=scratch, input_output_aliases=io_alias,
                              compiler_params=_cp(len(grid)))(*args)
    n_in, n_out, n_s = len(in_specs), len(out_specs), len(scratch)
    c_in, c_out = len(carry.operands), len(carry.out_shape)
    any_spec = pl.BlockSpec(memory_space=pl.ANY)

    def wrapped(*refs):
        ins, cins = refs[:n_in], refs[n_in:n_in + c_in]
        o0 = n_in + c_in
        outs, couts = refs[o0:o0 + n_out], refs[o0 + n_out:o0 + n_out + c_out]
        s0 = o0 + n_out + c_out
        scr, sems = refs[s0:s0 + n_s], refs[s0 + n_s:]
        first = pl.program_id(0) == 0
        last = pl.program_id(0) == grid[0] - 1
        for ax in range(1, len(grid)):
            first = jnp.logical_and(first, pl.program_id(ax) == 0)
            last = jnp.logical_and(last, pl.program_id(ax) == grid[ax] - 1)

        @pl.when(first)
        def _():
            carry.start(cins, couts, sems)

        body(*ins, *outs, *scr)

        @pl.when(last)
        def _():
            carry.finish(cins, couts, sems)

    return pl.pallas_call(
        wrapped, name=name, grid=grid, in_specs=in_specs + [any_spec] * c_in,
        out_specs=out_specs + [any_spec] * c_out, out_shape=out_shape + carry.out_shape,
        scratch_shapes=scratch + carry.sems,
        input_output_aliases={**io_alias, **{n_in + i: n_out + j for i, j in carry.aliases.items()}},
        compiler_params=_cp(len(grid)))(*args, *carry.operands)


def _whole_call(body, name, args, out_shape, scratch, carry=None):
    vm = pl.BlockSpec(memory_space=pltpu.VMEM)
    any_spec = pl.BlockSpec(memory_space=pl.ANY)
    out_shape, scratch = list(out_shape), list(scratch)
    n_in, n_out, n_s = len(args), len(out_shape), len(scratch)
    if carry is None:
        return pl.pallas_call(body, name=name, in_specs=[vm] * n_in, out_specs=[vm] * n_out, out_shape=out_shape,
                              scratch_shapes=scratch, compiler_params=_cp())(*args)
    c_in, c_out = len(carry.operands), len(carry.out_shape)

    def wrapped(*refs):
        ins, cins = refs[:n_in], refs[n_in:n_in + c_in]
        o0 = n_in + c_in
        outs, couts = refs[o0:o0 + n_out], refs[o0 + n_out:o0 + n_out + c_out]
        s0 = o0 + n_out + c_out
        scr, sems = refs[s0:s0 + n_s], refs[s0 + n_s:]
        carry.start(cins, couts, sems)
        body(*ins, *outs, *scr)
        carry.finish(cins, couts, sems)

    return pl.pallas_call(
        wrapped, name=name, in_specs=[vm] * n_in + [any_spec] * c_in, out_specs=[vm] * n_out + [any_spec] * c_out,
        out_shape=out_shape + carry.out_shape, scratch_shapes=scratch + carry.sems,
        input_output_aliases={n_in + i: n_out + j for i, j in carry.aliases.items()},
        compiler_params=_cp())(*args, *carry.operands)


def _alone(name, ex):
    any_spec = pl.BlockSpec(memory_space=pl.ANY)
    c_in, c_out = len(ex.operands), len(ex.out_shape)

    def body(*refs):
        ins, outs, sems = refs[:c_in], refs[c_in:c_in + c_out], refs[c_in + c_out:]
        ex.start(ins, outs, sems)
        ex.finish(ins, outs, sems)

    return pl.pallas_call(
        body, name=name, in_specs=[any_spec] * c_in, out_specs=[any_spec] * c_out, out_shape=ex.out_shape,
        scratch_shapes=ex.sems, input_output_aliases=ex.aliases, compiler_params=_cp())(*ex.operands)


def _ada_fwd(c_pad, w_ada, b_shard, carry=None):
    d = c_pad.shape[-1]
    cols = w_ada.shape[-1]
    chunk = 384

    def body(c_ref, w_ref, b_ref, call_ref, mod_ref, part, s1, r1, s2, r2):
        x, y, c = _my_place()
        dev = 4 * x + 2 * y + c
        chip = 2 * x + y
        call_ref[dev] = c_ref[...]

        def c_copy(k):
            px, py, pc = _flip(x, (k >> 2) & 1), _flip(y, (k >> 1) & 1), _flip(c, k & 1)
            return px, py, pc

        sends = []
        for k in range(1, N_DEV):
            px, py, pc = c_copy(k)
            cp = pltpu.make_async_remote_copy(src_ref=c_ref, dst_ref=call_ref.at[dev], send_sem=s1.at[k - 1],
                                              recv_sem=r1.at[k - 1], device_id=(px, py, pc), device_id_type=MESH)
            cp.start()
            sends.append(cp)
        for k in range(1, N_DEV):
            px, py, pc = c_copy(k)
            pltpu.make_async_remote_copy(src_ref=c_ref, dst_ref=call_ref.at[4 * px + 2 * py + pc],
                                         send_sem=s1.at[k - 1], recv_sem=r1.at[k - 1],
                                         device_id=(px, py, pc), device_id_type=MESH).wait_recv()
        for cp in sends:
            cp.wait_send()

        cs = call_ref[...].reshape(N_DEV * 8, d)
        sc = (cs * jax.nn.sigmoid(cs)).astype(BF)
        for n0 in range(0, cols, chunk):
            blk = _nn(sc, w_ref[:, n0:n0 + chunk].astype(BF)) + b_ref[:, n0:n0 + chunk]
            part[:, :, n0:n0 + chunk] = blk.reshape(N_DEV, 8, chunk)

        mod_ref[chip] = part[dev]
        sends = []
        for kk in range(1, N_CHIPS):
            px, py = _flip(x, (kk >> 1) & 1), _flip(y, kk & 1)
            cp = pltpu.make_async_remote_copy(src_ref=part.at[4 * px + 2 * py + c], dst_ref=mod_ref.at[chip],
                                              send_sem=s2.at[kk - 1], recv_sem=r2.at[kk - 1],
                                              device_id=(px, py, c), device_id_type=MESH)
            cp.start()
            sends.append(cp)
        for kk in range(1, N_CHIPS):
            px, py = _flip(x, (kk >> 1) & 1), _flip(y, kk & 1)
            pltpu.make_async_remote_copy(src_ref=part.at[dev], dst_ref=mod_ref.at[2 * px + py],
                                         send_sem=s2.at[kk - 1], recv_sem=r2.at[kk - 1],
                                         device_id=(px, py, c), device_id_type=MESH).wait_recv()
        for cp in sends:
            cp.wait_send()

    return _whole_call(
        body, "ada_fwd", (c_pad, w_ada, b_shard),
        [_sds((N_DEV, 8, d), F32), _sds((N_CHIPS, 8, cols), F32)],
        [pltpu.VMEM((N_DEV, 8, cols), F32),
         pltpu.SemaphoreType.DMA((N_DEV - 1,)), pltpu.SemaphoreType.DMA((N_DEV - 1,)),
         pltpu.SemaphoreType.DMA((N_CHIPS - 1,)), pltpu.SemaphoreType.DMA((N_CHIPS - 1,))], carry=carry)


def _ag_weights(bufs, kks=(1, 2, 3), relative=False):
    n, nk = len(bufs), len(kks)

    def half(b, which):
        hr = bufs[b].shape[2] // 2
        return pl.ds(pl.multiple_of(which * hr, 16), hr)

    def copies(outs, sems, b, i, kk):
        x, y, c = _my_place()
        chip = 2 * x + y
        px, py = _flip(x, (kk >> 1) & 1), _flip(y, kk & 1)
        mine, theirs = (0, kk) if relative else (chip, 2 * px + py)
        landing = kk if relative else chip
        k = nk * b + i
        send = pltpu.make_async_remote_copy(
            src_ref=outs[b].at[mine, :, half(b, c), :], dst_ref=outs[b].at[landing, :, half(b, c), :],
            send_sem=sems[0].at[k], recv_sem=sems[1].at[k], device_id=(px, py, c), device_id_type=MESH)
        got = outs[b].at[theirs, :, half(b, c), :]
        recv = pltpu.make_async_remote_copy(
            src_ref=got, dst_ref=got, send_sem=sems[0].at[k], recv_sem=sems[1].at[k],
            device_id=(px, py, c), device_id_type=MESH)
        fwd = pltpu.make_async_remote_copy(
            src_ref=got, dst_ref=got, send_sem=sems[2].at[k], recv_sem=sems[3].at[k],
            device_id=(x, y, 1 - c), device_id_type=MESH)
        other = outs[b].at[theirs, :, half(b, 1 - c), :]
        back = pltpu.make_async_remote_copy(
            src_ref=other, dst_ref=other, send_sem=sems[2].at[k], recv_sem=sems[3].at[k],
            device_id=(x, y, 1 - c), device_id_type=MESH)
        return send, recv, fwd, back

    def each(outs, sems):
        for b in range(n):
            for i, kk in enumerate(kks):
                yield copies(outs, sems, b, i, kk)

    def start(ins, outs, sems):
        for send, _, _, _ in each(outs, sems):
            send.start()

    def finish(ins, outs, sems):
        for _, recv, fwd, _ in each(outs, sems):
            recv.wait_recv()
            fwd.start()
        for send, _, fwd, back in each(outs, sems):
            back.wait_recv()
            send.wait_send()
            fwd.wait_send()

    return _Exchange(bufs, [_sds(s.shape, s.dtype) for s in bufs], {i: i for i in range(n)},
                     [pltpu.SemaphoreType.DMA((nk * n,))] * 4, start, finish)


def _rs_d2d(grads):
    n = len(grads)

    def copy(ins, outs, sems, b):
        x, y, c = _my_place()
        hr = grads[b].shape[2] // 2
        theirs = pl.ds(pl.multiple_of((1 - c) * hr, 8), hr)
        return pltpu.make_async_remote_copy(
            src_ref=ins[b].at[:, :, theirs, :], dst_ref=outs[b], send_sem=sems[0].at[b], recv_sem=sems[1].at[b],
            device_id=(x, y, 1 - c), device_id_type=MESH)

    def start(ins, outs, sems):
        for b in range(n):
            copy(ins, outs, sems, b).start()

    def finish(ins, outs, sems):
        for b in range(n):
            copy(ins, outs, sems, b).wait()

    return _Exchange(grads, [_sds(g.shape[:2] + (g.shape[2] // 2, g.shape[3]), F32) for g in grads], {},
                     [pltpu.SemaphoreType.DMA((n,))] * 2, start, finish)


def _add_halves(core, g, land):
    nchip, ng, rows, cols = g.shape
    hr = rows // 2
    tr = _row_tile(hr, cols)
    steps = hr // tr

    def body(core_ref, g_ref, l_ref, o_ref):
        del core_ref
        o_ref[...] = (g_ref[...] + l_ref[...]).astype(BF)

    return pl.pallas_call(
        body, name="add_halves",
        grid_spec=pltpu.PrefetchScalarGridSpec(
            num_scalar_prefetch=1, grid=(nchip, ng, steps),
            in_specs=[pl.BlockSpec((None, None, tr, cols), lambda j, a, i, cr: (j, a, cr[0] * steps + i, 0)),
                      pl.BlockSpec((None, None, tr, cols), lambda j, a, i, cr: (j, a, i, 0))],
            out_specs=pl.BlockSpec((None, None, tr, cols), lambda j, a, i, cr: (j, a, i, 0))),
        out_shape=_sds((nchip, ng, hr, cols), BF),
        compiler_params=_cp(3))(core, g, land)


def _rs_ici(parts, relative=False):
    n = len(parts)

    def copies(ins, outs, sems):
        x, y, c = _my_place()
        chip = 2 * x + y
        for b in range(n):
            for kk in range(1, N_CHIPS):
                px, py = _flip(x, (kk >> 1) & 1), _flip(y, kk & 1)
                k = 3 * b + kk - 1
                theirs, landing = (kk, kk) if relative else (2 * px + py, chip)
                send = pltpu.make_async_remote_copy(
                    src_ref=ins[b].at[theirs], dst_ref=outs[b].at[landing],
                    send_sem=sems[0].at[k], recv_sem=sems[1].at[k], device_id=(px, py, c), device_id_type=MESH)
                slot = outs[b].at[theirs]
                recv = pltpu.make_async_remote_copy(
                    src_ref=slot, dst_ref=slot, send_sem=sems[0].at[k], recv_sem=sems[1].at[k],
                    device_id=(px, py, c), device_id_type=MESH)
                yield send, recv

    def start(ins, outs, sems):
        for send, _ in copies(ins, outs, sems):
            send.start()

    def finish(ins, outs, sems):
        for send, recv in copies(ins, outs, sems):
            recv.wait_recv()
            send.wait_send()

    return _Exchange(parts, [_sds(p.shape, p.dtype) for p in parts], {},
                     [pltpu.SemaphoreType.DMA((3 * n,))] * 2, start, finish)


def _sum_chips(place, part, land, relative=False):
    nchip, ng, hr, cols = land.shape
    tr = _row_tile(hr, cols)
    steps = hr // tr

    def body(place_ref, p_ref, l1, l2, l3, o_ref):
        del place_ref
        o_ref[...] = ((p_ref[...].astype(F32) + l1[...].astype(F32)) + l2[...].astype(F32)) + l3[...].astype(F32)

    def slot(k):
        if relative:
            return pl.BlockSpec((None, None, tr, cols), lambda a, i, pr: (k, a, i, 0))
        return pl.BlockSpec((None, None, tr, cols), lambda a, i, pr: (jnp.bitwise_xor(pr[1], k), a, i, 0))

    return pl.pallas_call(
        body, name="sum_chips",
        grid_spec=pltpu.PrefetchScalarGridSpec(
            num_scalar_prefetch=1, grid=(ng, steps),
            in_specs=[slot(0), slot(1), slot(2), slot(3)],
            out_specs=pl.BlockSpec((None, tr, cols), lambda a, i, pr: (a, pr[0] * steps + i, 0))),
        out_shape=_sds((ng, 2 * hr, cols), F32),
        compiler_params=_cp(2))(place, part, land, land, land)


def _rs_final(bufs):
    n = len(bufs)

    def copy(outs, sems, b, which):
        x, y, c = _my_place()
        hr = bufs[b].shape[1] // 2
        rows = outs[b].at[:, pl.ds(pl.multiple_of((c if which == 0 else 1 - c) * hr, 8), hr), :]
        return pltpu.make_async_remote_copy(
            src_ref=rows, dst_ref=rows, send_sem=sems[0].at[b], recv_sem=sems[1].at[b],
            device_id=(x, y, 1 - c), device_id_type=MESH)

    def start(ins, outs, sems):
        for b in range(n):
            copy(outs, sems, b, 0).start()

    def finish(ins, outs, sems):
        for b in range(n):
            copy(outs, sems, b, 0).wait_send()
            copy(outs, sems, b, 1).wait_recv()

    return _Exchange(bufs, [_sds(h.shape, F32) for h in bufs], {i: i for i in range(n)},
                     [pltpu.SemaphoreType.DMA((n,))] * 2, start, finish)


def _small_sync(smalls, dmod_blk, c_all, carry=None):
    d = c_all.shape[-1]
    cols = dmod_blk.shape[-1]
    chunk = 384

    def body(sm_ref, dm_ref, c_ref, sum_ref, gw_ref, sm_all, dm_all, ssem, rsem):
        x, y, c = _my_place()
        dev = 4 * x + 2 * y + c
        chip = 2 * x + y
        sm_all[dev] = sm_ref[...]
        dm_all[dev] = dm_ref[chip]
        sends = []
        for k in range(1, N_DEV):
            px, py, pc = _flip(x, (k >> 2) & 1), _flip(y, (k >> 1) & 1), _flip(c, k & 1)
            a = pltpu.make_async_remote_copy(src_ref=sm_ref, dst_ref=sm_all.at[dev], send_sem=ssem.at[2 * (k - 1)],
                                             recv_sem=rsem.at[2 * (k - 1)], device_id=(px, py, pc),
                                             device_id_type=MESH)
            b = pltpu.make_async_remote_copy(src_ref=dm_ref.at[2 * px + py], dst_ref=dm_all.at[dev],
                                             send_sem=ssem.at[2 * (k - 1) + 1], recv_sem=rsem.at[2 * (k - 1) + 1],
                                             device_id=(px, py, pc), device_id_type=MESH)
            a.start()
            b.start()
            sends += [a, b]
        for k in range(1, N_DEV):
            px, py, pc = _flip(x, (k >> 2) & 1), _flip(y, (k >> 1) & 1), _flip(c, k & 1)
            pdev = 4 * px + 2 * py + pc
            pltpu.make_async_remote_copy(src_ref=sm_ref, dst_ref=sm_all.at[pdev], send_sem=ssem.at[2 * (k - 1)],
                                         recv_sem=rsem.at[2 * (k - 1)], device_id=(px, py, pc),
                                         device_id_type=MESH).wait_recv()
            pltpu.make_async_remote_copy(src_ref=dm_ref.at[chip], dst_ref=dm_all.at[pdev],
                                         send_sem=ssem.at[2 * (k - 1) + 1], recv_sem=rsem.at[2 * (k - 1) + 1],
                                         device_id=(px, py, pc), device_id_type=MESH).wait_recv()
        for cp in sends:
            cp.wait_send()

        tot = sm_all[0]
        for q in range(1, N_DEV):
            tot = tot + sm_all[q]
        sum_ref[...] = tot

        cs = c_ref[...].reshape(N_DEV * 8, d)
        sc = (cs * jax.nn.sigmoid(cs)).astype(BF)
        for n0 in range(0, cols, chunk):
            dmv = dm_all[:, :, n0:n0 + chunk].reshape(N_DEV * 8, chunk).astype(BF)
            gw_ref[:, n0:n0 + chunk] = _tn(sc, dmv)

    return _whole_call(
        body, "small_sync", (smalls, dmod_blk, c_all),
        [_sds(smalls.shape, F32), _sds((d, cols), F32)],
        [pltpu.VMEM((N_DEV,) + smalls.shape, F32), pltpu.VMEM((N_DEV, 8, cols), F32),
         pltpu.SemaphoreType.DMA((2 * (N_DEV - 1),)), pltpu.SemaphoreType.DMA((2 * (N_DEV - 1),))], carry=carry)


def _bucket_onehot():
    maps = np.stack([_bucket_map(dil).reshape(-1) for _, dil in DIL_CONFIGS])
    return (jnp.asarray(maps)[:, None, :] == jnp.arange(N_BUCKETS, dtype=jnp.int32)[None, :, None]).astype(BF)


def _dil_bias(rel_t, onehot):
    def body(r_ref, oh_ref, o_ref):
        rv = r_ref[...]
        hi = rv.astype(BF)
        lo = (rv - hi.astype(F32)).astype(BF)
        for c in range(len(DIL_CONFIGS)):
            o_ref[c] = _nn(hi, oh_ref[c]) + _nn(lo, oh_ref[c])

    return pl.pallas_call(body, name="dil_bias",
                          out_shape=_sds((len(DIL_CONFIGS), N_HEADS, BLOCK * 2 * BLOCK), F32),
                          compiler_params=_cp())(rel_t, onehot)


def _rowsum8(a):
    def body(a_ref, o_ref):
        o_ref[...] = jnp.sum(a_ref[...], axis=0, keepdims=True)

    return pl.pallas_call(body, name="rowsum8", out_shape=_sds((1, a.shape[1]), F32), compiler_params=_cp())(a)


def _local_step(x, mod, target, w, gains, rel_bias, place=None):
    nb, seq, d = x.shape
    t = nb * seq
    dist = place is not None
    core = place[0:1] if dist else None
    x0 = x.reshape(t, d)
    tgt = target.reshape(t, d)
    md = [mod[:, i:i + 1, :] for i in range(N_MOD)]
    sh1, sc1, gt1, sh2, sc2, gt2, sh3, sc3, gt3 = md
    g1, g2, g3 = gains["g_ffn1"], gains["g_mix"], gains["g_ffn2"]
    ones_g = _group_ones()

    def partial_sums(grads, lands):
        return [_add_halves(core, g, l) for g, l in zip(grads, lands)]

    def chip_sums(parts, lands):
        return [_sum_chips(place, p, l, relative=True) for p, l in zip(parts, lands)]

    gu1 = w["gu1"]
    res = _ffn_up(x0, g1, sc1, sh1, gu1, seq,
                  carry=_join([_ag_weights([w["d1"]], relative=True), _ag_weights([w["win"], w["wout"]])])
                  if dist else None)
    h1, a1, u1, s1 = res[:4]
    wd1, w_in, w_out = res[4:] if dist else (w["d1"], w["win"], w["wout"])
    w_out2 = w_out.reshape(2 * D_GRP, d)
    f1, x1 = _ffn_down(s1, wd1, x0, gt1, seq, 0.5)

    h2, qkv6, qkv_r4, qkv_r16 = _qkv_proj(x1, g2, sc2, sh2, w_in, seq)
    qkv6b = qkv6.reshape(6, nb, seq, D_GRP)
    res = _sb_fwd(qkv6b, gains["g_sb_out"], nb, seq,
                  carry=_ag_weights([w["gu2"], w["d2"]], relative=True) if dist else None)
    o_sb, on_sb = res[:2]
    wgu2, wd2 = res[2:] if dist else (w["gu2"], w["d2"])
    onehot = _bucket_onehot()
    bias = _dil_bias(rel_bias.T, onehot).reshape(len(DIL_CONFIGS), N_HEADS * BLOCK, 2 * BLOCK)
    o_cs, l_cs = [], []
    qkv_rs = [(qkv6b, 3), (qkv_r4, 0), (qkv_r16, 0)]
    for ci, (_, dil) in enumerate(DIL_CONFIGS):
        sub = seq // dil
        arr, base = qkv_rs[ci]
        arr = arr.reshape(base + 3, nb, sub, dil * D_GRP)
        qkv_rs[ci] = (arr, base)
        o_c, l_c = _dil_fwd(arr, base, bias[ci], nb, sub, dil)
        o_cs.append(o_c.reshape(t // dil, dil * D_GRP))
        l_cs.append(l_c.reshape(t // dil, dil * D_GRP))
    o_dil, on_dil = _dil_comb(o_cs, l_cs, gains["g_dil_out"])
    tmix, x2 = _mix_out(on_sb.reshape(t, D_GRP), on_dil, w_out2, x1, gt2, seq)

    h3, a3, u3, s3 = _ffn_up(x2, g3, sc3, sh3, wgu2, seq)
    f3, dx3, dg_final, loss = _ffn_down_loss(s3, wd2, x2, gt3, seq, 0.5, gains["g_final"], tgt)

    da3, du3, df3, dgt3, dx2, dsh3, dsc3, dg3 = _ffn_bwd_x(dx3, gt3, f3, wd2, a3, u3, wgu2, x2, g3, sc3, seq, 0.5)
    grads2 = [_ffn_bwd_w(h3, da3, du3, s3, df3)]

    res = _mix_bwd_out(
        dx2, gt2, tmix, w_out2, o_sb.reshape(t, D_GRP), o_dil, on_sb.reshape(t, D_GRP), on_dil,
        gains["g_sb_out"], gains["g_dil_out"], ones_g, seq, carry=_rs_d2d(grads2) if dist else None)
    do_sb, do_dil, dgt2, dg_sb, dg_dil, dw_out = res[:6]
    parts2 = partial_sums(grads2, res[6:]) if dist else None
    dw_out = dw_out.reshape(N_CHIPS, 1, 2 * D_GRP // N_CHIPS, d)
    res = _sb_bwd(qkv6b, do_sb.reshape(nb, seq, D_GRP), nb, seq,
                  carry=_rs_ici(parts2, relative=True) if dist else None)
    dqkv6 = res[0]
    halves2 = chip_sums(parts2, res[1:]) if dist else None
    dcs = _dil_comb_bwd(do_dil, o_cs, l_cs)
    dsum, a_tiles = [], []
    for ci, (_, dil) in enumerate(DIL_CONFIGS):
        sub = seq // dil
        do_c = dcs[ci].reshape(nb, sub, dil * D_GRP)
        dd_c = dcs[3 + ci].reshape(nb, sub, dil * D_GRP)
        res = _dil_bwd(qkv_rs[ci][0], qkv_rs[ci][1], bias[ci], do_c, dd_c, nb, sub, dil)
        dsum.append(res[0].reshape(3, t // dil, dil * D_GRP))
        a_tiles.append(res[1].reshape(N_HEADS, BLOCK * 2 * BLOCK))
    dqkv6 = _dqkv_dil_sum(dsum, dqkv6.reshape(6, t, D_GRP))
    drel = _relbias_grad(jnp.stack(a_tiles), onehot)
    dx1, dsh2, dsc2, dg2 = _mix_bwd_dh(dqkv6, w_in, x1, g2, sc2, dx2, seq)

    da1, du1, df1, dgt1 = _ffn_bwd_ds(dx1, gt1, f1, wd1, a1, u1, seq, 0.5)
    grads1 = [_ffn_bwd_w(h1, da1, du1, s1, df1)]
    res = _dw_in(h2, dqkv6, carry=_join([_rs_d2d(grads1), _rs_final(halves2)]) if dist else None)
    grads_m = [res[0], dw_out]
    parts1 = partial_sums(grads1, res[1:2]) if dist else None
    if dist:
        grads2 = res[2:3]
    res = _ffn_bwd_dh(da1, du1, gu1, x0, g1, sc1, dx1, seq,
                      carry=_join([_rs_ici(parts1, relative=True), _rs_d2d(grads_m)]) if dist else None)
    dx0, dsh1, dsc1, dg1 = res[:4]
    pending = None
    if dist:
        pending = (chip_sums(parts1, res[4:5]), partial_sums(grads_m, res[5:7]))

    dmod = jnp.concatenate([dsh1, dsc1, dgt1, dsh2, dsc2, dgt2, dsh3, dsc3, dgt3], axis=1)
    return dict(grad_x=dx0.reshape(nb, seq, d), loss=loss[0, 0], dmod=dmod.reshape(nb, N_MOD * d),
                dffn1=grads1[0], dffn2=grads2[0], dwin=grads_m[0], dwout=grads_m[1], pending=pending,
                dg_ffn1=dg1, dg_mix=dg2, dg_ffn2=dg3, dg_final=dg_final, dg_sb=dg_sb, dg_dil=dg_dil,
                drel=drel.T)


_SMALL_ORDER = (("b_ada", N_MOD * 1024), ("g_ffn1", 1024), ("g_mix", 1024), ("g_ffn2", 1024), ("g_final", 1024),
                ("g_sb_out", D_GRP), ("g_dil_out", D_GRP), ("rel_bias", N_BUCKETS * N_HEADS))


def _pack_small(parts, extra=None):
    flat = [parts[name].reshape(-1).astype(F32) for name, _ in _SMALL_ORDER]
    used = sum(sz for _, sz in _SMALL_ORDER)
    pad = SMALL_ROWS * 128 - used
    tail = jnp.zeros((pad,), F32)
    if extra is not None:
        tail = tail.at[0].set(extra)
    return jnp.concatenate(flat + [tail]).reshape(SMALL_ROWS, 128)


def _unpack_small(packed, shapes):
    flat = packed.reshape(-1)
    out, off = {}, 0
    for name, sz in _SMALL_ORDER:
        out[name] = flat[off:off + sz].reshape(shapes[name])
        off += sz
    return out, flat[off]


def kernel(x, c, w_ada, b_ada, g_ffn1, w1_gate, w1_up, w1_down, g_mix, w_in, g_sb_out, g_dil_out, w_out, rel_bias, g_ffn2, w2_gate, w2_up, w2_down, g_final, loss_target, m_w_ada, m_b_ada, m_g_ffn1, m_w1_gate, m_w1_up, m_w1_down, m_g_mix, m_w_in, m_g_sb_out, m_g_dil_out, m_w_out, m_rel_bias, m_g_ffn2, m_w2_gate, m_w2_up, m_w2_down, m_g_final, v_w_ada, v_b_ada, v_g_ffn1, v_w1_gate, v_w1_up, v_w1_down, v_g_mix, v_w_in, v_g_sb_out, v_g_dil_out, v_w_out, v_rel_bias, v_g_ffn2, v_w2_gate, v_w2_up, v_w2_down, v_g_final):
    nb, seq, d = x.shape
    xi, yi, ci = lax.axis_index("x"), lax.axis_index("y"), lax.axis_index("c")
    chip = 2 * xi + yi
    ada_cols = w_ada.shape[-1]

    c_pad = jnp.zeros((8, d), F32).at[:nb].set(c)
    b_shard = lax.dynamic_slice(b_ada, (0, chip * ada_cols), (1, ada_cols))
    shards = dict(gu1=jnp.stack([w1_gate[0], w1_up[0]]), d1=w1_down, win=w_in, wout=w_out,
                  gu2=jnp.stack([w2_gate[0], w2_up[0]]), d2=w2_down)
    bufs = {k: lax.dynamic_update_slice(lax.empty((N_CHIPS,) + s.shape, BF), s.astype(BF)[None],
                                        (chip if k in ("win", "wout") else 0, 0, 0, 0))
            for k, s in shards.items()}
    c_all, mod_blk, bufs["gu1"] = _ada_fwd(c_pad, w_ada[0], b_shard,
                                           carry=_ag_weights([bufs["gu1"]], relative=True))
    mod = jnp.transpose(mod_blk[:, :nb, :], (1, 0, 2)).reshape(nb, N_MOD, d)

    gains = dict(g_ffn1=g_ffn1, g_mix=g_mix, g_ffn2=g_ffn2, g_final=g_final.reshape(1, d),
                 g_sb_out=g_sb_out.reshape(1, D_GRP), g_dil_out=g_dil_out.reshape(1, D_GRP))
    place = jnp.stack([ci, chip]).astype(jnp.int32)
    r = _local_step(x, mod, loss_target, bufs, gains, rel_bias, place)

    dmod = r["dmod"]
    dmod_pad = jnp.zeros((8, N_MOD * d), F32).at[:nb].set(dmod)
    dmod_blk = jnp.transpose(dmod_pad.reshape(8, N_CHIPS, ada_cols), (1, 0, 2))
    small_parts = dict(b_ada=_rowsum8(dmod_pad), g_ffn1=r["dg_ffn1"], g_mix=r["dg_mix"], g_ffn2=r["dg_ffn2"],
                       g_final=r["dg_final"], g_sb_out=r["dg_sb"], g_dil_out=r["dg_dil"], rel_bias=r["drel"])
    halves1, parts_m = r["pending"]
    res = _small_sync(_pack_small(small_parts, r["loss"]), dmod_blk, c_all,
                      carry=_join([_rs_final(halves1), _rs_ici(parts_m)]))
    small_sum, g_wada, gffn1 = res[:3]
    halves_m = [_sum_chips(place, p, l) for p, l in zip(parts_m, res[3:5])]
    gwin, gwout = _alone("rs_last", _rs_final(halves_m))
    gffn2 = r["dffn2"]

    small_w = dict(b_ada=b_ada, g_ffn1=g_ffn1, g_mix=g_mix, g_ffn2=g_ffn2, g_final=g_final,
                   g_sb_out=g_sb_out, g_dil_out=g_dil_out, rel_bias=rel_bias)
    small_m = dict(b_ada=m_b_ada, g_ffn1=m_g_ffn1, g_mix=m_g_mix, g_ffn2=m_g_ffn2, g_final=m_g_final,
                   g_sb_out=m_g_sb_out, g_dil_out=m_g_dil_out, rel_bias=m_rel_bias)
    small_v = dict(b_ada=v_b_ada, g_ffn1=v_g_ffn1, g_mix=v_g_mix, g_ffn2=v_g_ffn2, g_final=v_g_final,
                   g_sb_out=v_g_sb_out, g_dil_out=v_g_dil_out, rel_bias=v_rel_bias)
    shapes = {k: v.shape for k, v in small_w.items()}
    sg, sd, sm, sv = _adamw(_pack_small(small_w), small_sum.reshape(1, SMALL_ROWS, 128), 0,
                            _pack_small(small_m), _pack_small(small_v))
    sg, loss = _unpack_small(sg, shapes)
    sd, _ = _unpack_small(sd, shapes)
    sm, _ = _unpack_small(sm, shapes)
    sv, _ = _unpack_small(sv, shapes)

    big = {}

    def upd(name, w, g_arr, sel, m, v, transposed=False):
        swap = (lambda a: jnp.swapaxes(a, -1, -2)) if transposed else (lambda a: a)
        w2, m2, v2 = [swap(a)[0] for a in (w, m, v)]
        big[name] = [swap(a[None]) for a in _adamw(w2, g_arr, sel, m2, v2)]

    upd("w_ada", w_ada, g_wada.reshape(1, d, ada_cols), 0, m_w_ada, v_w_ada)
    upd("w1_gate", w1_gate, gffn1, 0, m_w1_gate, v_w1_gate, transposed=True)
    upd("w1_up", w1_up, gffn1, 1, m_w1_up, v_w1_up, transposed=True)
    upd("w1_down", w1_down, gffn1, 2, m_w1_down, v_w1_down)
    upd("w_in", w_in, gwin, 0, m_w_in, v_w_in)
    upd("w_out", w_out, gwout, 0, m_w_out, v_w_out)
    upd("w2_gate", w2_gate, gffn2, 0, m_w2_gate, v_w2_gate, transposed=True)
    upd("w2_up", w2_up, gffn2, 1, m_w2_up, v_w2_up, transposed=True)
    upd("w2_down", w2_down, gffn2, 2, m_w2_down, v_w2_down)

    names = ["w_ada", "b_ada", "g_ffn1", "w1_gate", "w1_up", "w1_down", "g_mix", "w_in", "g_sb_out", "g_dil_out",
             "w_out", "rel_bias", "g_ffn2", "w2_gate", "w2_up", "w2_down", "g_final"]
    outs = [loss, r["grad_x"]]
    for k, small in enumerate((sg, sd, sm, sv)):
        for name in names:
            outs.append(big[name][k] if name in big else small[name])
    return tuple(outs)
```

```python
import functools
import math

import numpy as np
import jax
import jax.numpy as jnp
from jax import lax
from jax.experimental import pallas as pl
from jax.experimental.pallas import tpu as pltpu

F32 = jnp.float32
BF = jnp.bfloat16
MESH = pl.DeviceIdType.MESH

HEAD_DIM = 64
N_HEADS = 8
D_GRP = N_HEADS * HEAD_DIM
DIL_CONFIGS = ((128, 1), (512, 4), (2048, 16))
N_STEPS = 128
BLOCK = 128
N_BUCKETS = 32
MAX_DISTANCE = 2048
N_MOD = 9
EPS = 1e-6
NEG_INF = -1e30
SCALE = HEAD_DIM ** -0.5

ADAM_LR = 0.001
ADAM_B1 = 0.9
ADAM_B2 = 0.999
ADAM_EPS = 1e-08
ADAM_WD = 0.01
ADAM_STEP = 10

N_CHIPS = 4
N_DEV = 8
VMEM_LIMIT = 56 * 1024 * 1024
TM = 512
TQ = 256
KB = 256
SMALL_ROWS = 120


def _cp(n_axes=0, **kw):
    sem = ("arbitrary",) * n_axes if n_axes else None
    return pltpu.CompilerParams(dimension_semantics=sem, vmem_limit_bytes=VMEM_LIMIT, **kw)


def _nn(a, b):
    return jnp.dot(a, b, preferred_element_type=F32)


def _nt(a, b):
    return lax.dot_general(a, b, (((1,), (1,)), ((), ())), preferred_element_type=F32)


def _tn(a, b):
    return lax.dot_general(a, b, (((0,), (0,)), ((), ())), preferred_element_type=F32)


def _nn2(x, m):
    top = lax.bitcast_convert_type(
        jnp.bitwise_and(lax.bitcast_convert_type(x, jnp.uint32), jnp.uint32(0xFFFF0000)), F32)
    hi = top.astype(BF)
    lo = (x - top).astype(BF)
    r = _nn(jnp.concatenate([hi, lo], axis=0), m)
    return r[:x.shape[0]] + r[x.shape[0]:]


def _softplus(z):
    return jnp.maximum(z, 0.0) + jnp.log1p(jnp.exp(-jnp.abs(z)))


def _sds(shape, dtype):
    return jax.ShapeDtypeStruct(shape, dtype)


def _whole(a):
    nd = a.ndim
    return pl.BlockSpec(a.shape, lambda *_: (0,) * nd, pipeline_mode=pl.Buffered(1))


def _modnorm_bwd_tile(dh, xv, gv, scv, dxo):
    r = lax.rsqrt(jnp.mean(xv * xv, axis=-1, keepdims=True) + EPS)
    n = xv * r
    ng = n * gv
    dsh = jnp.sum(dh, axis=0, keepdims=True)
    dsc = jnp.sum(dh * ng, axis=0, keepdims=True)
    dy = dh * (1.0 + scv)
    dg = jnp.sum(dy * n, axis=0, keepdims=True)
    dn = dy * gv
    dx = dxo + r * (dn - n * jnp.mean(dn * n, axis=-1, keepdims=True))
    return dx, dsh, dsc, dg


def _acc_rows(ref, val, first):
    @pl.when(first)
    def _():
        ref[...] = val

    @pl.when(jnp.logical_not(first))
    def _():
        ref[...] += val


def _modnorm_tile(x_ref, g_ref, sc_ref, sh_ref):
    xv = x_ref[...]
    r = lax.rsqrt(jnp.mean(xv * xv, axis=-1, keepdims=True) + EPS)
    return (((xv * r) * g_ref[...]) * (1.0 + sc_ref[...]) + sh_ref[...]).astype(BF)


def _ffn_up(x, g, sc, sh, wgu, seq, carry=None):
    t, d = x.shape
    fs = wgu.shape[-1]
    per = seq // TM

    def body(x_ref, g_ref, sc_ref, sh_ref, w_ref, h_ref, p_ref, q_ref, s_ref):
        hv = _modnorm_tile(x_ref, g_ref, sc_ref, sh_ref)
        h_ref[...] = hv
        for j in range(N_CHIPS):
            a = _nn(hv, w_ref[j, 0])
            u = _nn(hv, w_ref[j, 1])
            sig = jax.nn.sigmoid(a)
            q = a * sig
            p_ref[j] = (u * (sig * (1.0 + a * (1.0 - sig)))).astype(BF)
            q_ref[j] = q.astype(BF)
            s_ref[j] = (q * u).astype(BF)

    row = pl.BlockSpec((TM, d), lambda m: (m, 0))
    ex = pl.BlockSpec((None, 1, d), lambda m: (m // per, 0, 0))
    blk = pl.BlockSpec((N_CHIPS, TM, fs), lambda m: (0, m, 0))
    return _call(
        body, "ffn_up", (t // TM,),
        [row, pl.BlockSpec((1, d), lambda m: (0, 0)), ex, ex, _whole(wgu)],
        [row, blk, blk, blk],
        [_sds((t, d), BF)] + [_sds((N_CHIPS, t, fs), BF)] * 3,
        (x, g, sc, sh, wgu), carry=carry)


def _ffn_down(s, wd, x, gt, seq, coef, carry=None):
    _, t, fs = s.shape
    d = x.shape[-1]
    per = seq // TM

    def body(s_ref, w_ref, x_ref, gt_ref, f_ref, xo_ref):
        f = _nn(s_ref[0], w_ref[0, 0])
        for j in range(1, N_CHIPS):
            f = f + _nn(s_ref[j], w_ref[j, 0])
        f_ref[...] = f
        xo_ref[...] = x_ref[...] + (coef * gt_ref[...]) * f

    row = pl.BlockSpec((TM, d), lambda m: (m, 0))
    return _call(
        body, "ffn_down", (t // TM,),
        [pl.BlockSpec((N_CHIPS, TM, fs), lambda m: (0, m, 0)), _whole(wd), row,
         pl.BlockSpec((None, 1, d), lambda m: (m // per, 0, 0))],
        [row, row], [_sds((t, d), F32), _sds((t, d), F32)], (s, wd, x, gt), carry=carry)


def _ffn_bwd_ds(dxo, gt, f, wd, p, q, seq, coef, carry=None):
    t, d = dxo.shape
    fs = p.shape[-1]
    per = seq // TM
    nb = t // seq

    def body(dxo_ref, gt_ref, f_ref, w_ref, p_ref, q_ref, da_ref, du_ref, df_ref, dgt_ref):
        m = pl.program_id(0)
        dxv = dxo_ref[...]
        df = ((coef * gt_ref[...]) * dxv).astype(BF)
        df_ref[...] = df
        _acc_rows(dgt_ref, coef * jnp.sum(dxv * f_ref[...], axis=0, keepdims=True), m % per == 0)
        for j in range(N_CHIPS):
            ds = _nt(df, w_ref[j, 0])
            da_ref[j] = (ds * p_ref[j].astype(F32)).astype(BF)
            du_ref[j] = (ds * q_ref[j].astype(F32)).astype(BF)

    row = pl.BlockSpec((TM, d), lambda m: (m, 0))
    blk = pl.BlockSpec((N_CHIPS, TM, fs), lambda m: (0, m, 0))
    ex = pl.BlockSpec((None, 1, d), lambda m: (m // per, 0, 0))
    return _call(
        body, "ffn_bwd_ds", (t // TM,),
        [row, ex, row, _whole(wd), blk, blk],
        [blk, blk, row, ex],
        [_sds((N_CHIPS, t, fs), BF), _sds((N_CHIPS, t, fs), BF), _sds((t, d), BF), _sds((nb, 1, d), F32)],
        (dxo, gt, f, wd, p, q), carry=carry)


TM_X = 256


def _ffn_bwd_x(dxo, gt, f, wd, p, q, wgu, x, g, sc, seq, coef):
    t, d = dxo.shape
    fs = p.shape[-1]
    per = seq // TM_X
    nb = t // seq

    def body(dxo_ref, gt_ref, f_ref, wd_ref, p_ref, q_ref, w_ref, x_ref, g_ref, sc_ref,
             da_ref, du_ref, df_ref, dgt_ref, dx_ref, dsh_ref, dsc_ref, dg_ref):
        m = pl.program_id(0)
        dxv = dxo_ref[...]
        df = ((coef * gt_ref[...]) * dxv).astype(BF)
        df_ref[...] = df
        _acc_rows(dgt_ref, coef * jnp.sum(dxv * f_ref[...], axis=0, keepdims=True), m % per == 0)
        dh = None
        for j in range(N_CHIPS):
            ds = _nt(df, wd_ref[j, 0])
            da = (ds * p_ref[j].astype(F32)).astype(BF)
            du = (ds * q_ref[j].astype(F32)).astype(BF)
            da_ref[j] = da
            du_ref[j] = du
            part = _nt(da, w_ref[j, 0]) + _nt(du, w_ref[j, 1])
            dh = part if dh is None else dh + part
        dx, dsh, dsc, dg = _modnorm_bwd_tile(dh, x_ref[...], g_ref[...], sc_ref[...], dxv)
        dx_ref[...] = dx
        _acc_rows(dsh_ref, dsh, m % per == 0)
        _acc_rows(dsc_ref, dsc, m % per == 0)
        _acc_rows(dg_ref, dg, m == 0)

    row = pl.BlockSpec((TM_X, d), lambda m: (m, 0))
    blk = pl.BlockSpec((N_CHIPS, TM_X, fs), lambda m: (0, m, 0))
    ex = pl.BlockSpec((None, 1, d), lambda m: (m // per, 0, 0))
    vec = pl.BlockSpec((1, d), lambda m: (0, 0))
    exs = _sds((nb, 1, d), F32)
    return pl.pallas_call(
        body, name="ffn_bwd_x", grid=(t // TM_X,),
        in_specs=[row, ex, row, _whole(wd), blk, blk, _whole(wgu), row, vec, ex],
        out_specs=[blk, blk, row, ex, row, ex, ex, vec],
        out_shape=[_sds((N_CHIPS, t, fs), BF), _sds((N_CHIPS, t, fs), BF), _sds((t, d), BF), exs,
                   _sds((t, d), F32), exs, exs, _sds((1, d), F32)],
        compiler_params=_cp(1))(dxo, gt, f, wd, p, q, wgu, x, g, sc)


TK_W = 1024


def _ffn_bwd_w(h, da, du, s, df):
    t, d = h.shape
    fs = da.shape[-1]

    def body(h_ref, da_ref, du_ref, s_ref, df_ref, o_ref):
        kt = pl.program_id(1)
        hv = h_ref[...]
        parts = (_tn(da_ref[...], hv), _tn(du_ref[...], hv), _tn(s_ref[...], df_ref[...]))

        @pl.when(kt == 0)
        def _():
            for i, p in enumerate(parts):
                o_ref[i] = p

        @pl.when(kt != 0)
        def _():
            for i, p in enumerate(parts):
                o_ref[i] += p

    row = pl.BlockSpec((TK_W, d), lambda j, kt: (kt, 0))
    blk = pl.BlockSpec((None, TK_W, fs), lambda j, kt: (j, kt, 0))
    return pl.pallas_call(
        body, name="ffn_bwd_w", grid=(N_CHIPS, t // TK_W),
        in_specs=[row, blk, blk, blk, row],
        out_specs=pl.BlockSpec((None, 3, fs, d), lambda j, kt: (j, 0, 0, 0)),
        out_shape=_sds((N_CHIPS, 3, fs, d), F32),
        compiler_params=_cp(2))(h, da, du, s, df)


def _ffn_bwd_dh(da, du, wgu, x, g, sc, dxo, seq, carry=None):
    _, t, fs = da.shape
    d = x.shape[-1]
    per = seq // TM
    nb = t // seq

    def body(da_ref, du_ref, w_ref, x_ref, g_ref, sc_ref, dxo_ref, dx_ref, dsh_ref, dsc_ref, dg_ref):
        m = pl.program_id(0)
        dh = _nt(da_ref[0], w_ref[0, 0]) + _nt(du_ref[0], w_ref[0, 1])
        for j in range(1, N_CHIPS):
            dh = dh + _nt(da_ref[j], w_ref[j, 0]) + _nt(du_ref[j], w_ref[j, 1])
        dx, dsh, dsc, dg = _modnorm_bwd_tile(dh, x_ref[...], g_ref[...], sc_ref[...], dxo_ref[...])
        dx_ref[...] = dx
        _acc_rows(dsh_ref, dsh, m % per == 0)
        _acc_rows(dsc_ref, dsc, m % per == 0)
        _acc_rows(dg_ref, dg, m == 0)

    row = pl.BlockSpec((TM, d), lambda m: (m, 0))
    blk = pl.BlockSpec((N_CHIPS, TM, fs), lambda m: (0, m, 0))
    ex = pl.BlockSpec((None, 1, d), lambda m: (m // per, 0, 0))
    vec = pl.BlockSpec((1, d), lambda m: (0, 0))
    return _call(
        body, "ffn_bwd_dh", (t // TM,),
        [blk, blk, _whole(wgu), row, vec, ex, row],
        [row, ex, ex, vec],
        [_sds((t, d), F32), _sds((nb, 1, d), F32), _sds((nb, 1, d), F32), _sds((1, d), F32)],
        (da, du, wgu, x, g, sc, dxo), carry=carry)


def _qkv_proj(x, g, sc, sh, w_in, seq, carry=None):
    t, d = x.shape
    wc = w_in.shape[-1]
    per = seq // TM

    dils = [dil for _, dil in DIL_CONFIGS if dil > 1]

    def body(x_ref, g_ref, sc_ref, sh_ref, w_ref, h_ref, o_ref, *rest):
        res_refs, buf = rest[:len(dils)], rest[len(dils)]
        hv = _modnorm_tile(x_ref, g_ref, sc_ref, sh_ref)
        h_ref[...] = hv
        for j in range(N_CHIPS):
            rf = _nn(hv, w_ref[j, 0])
            r = rf.astype(BF)
            for a, lc, off, width in _col_pieces(j, wc):
                o_ref[a, :, lc:lc + width] = r[:, off:off + width]
                if a < 3:
                    continue
                for c0 in range(0, width, 128):
                    cg = (lc + c0) // 128
                    buf[...] = rf[:, off + c0:off + c0 + 128]
                    for ref, dil in zip(res_refs, dils):
                        for rr in range(dil):
                            ref[a - 3, :, rr * D_GRP + cg * 128:rr * D_GRP + (cg + 1) * 128] = (
                                buf[pl.ds(rr, TM // dil, stride=dil), :].astype(BF))

    row = pl.BlockSpec((TM, d), lambda m: (m, 0))
    ex = pl.BlockSpec((None, 1, d), lambda m: (m // per, 0, 0))
    return _call(
        body, "qkv_proj", (t // TM,),
        [row, pl.BlockSpec((1, d), lambda m: (0, 0)), ex, ex, _whole(w_in)],
        [row, pl.BlockSpec((6, TM, D_GRP), lambda m: (0, m, 0))]
        + [pl.BlockSpec((3, TM // dil, dil * D_GRP), lambda m: (0, m, 0)) for dil in dils],
        [_sds((t, d), BF), _sds((6, t, D_GRP), BF)] + [_sds((3, t // dil, dil * D_GRP), BF) for dil in dils],
        (x, g, sc, sh, w_in), scratch=[pltpu.VMEM((TM, 128), F32)], carry=carry)


def _col_pieces(j, wc):
    out, off = [], 0
    while off < wc:
        a, lc = divmod(j * wc + off, D_GRP)
        width = min(D_GRP - lc, wc - off)
        out.append((a, lc, off, width))
        off += width
    return out


def _chip_cols(g6_ref, j, wc):
    return jnp.concatenate([g6_ref[a, :, lc:lc + width] for a, lc, _, width in _col_pieces(j, wc)], axis=1)


def _mix_out(on_sb, on_dil, w_out, x, gt, seq):
    t, d = x.shape
    per = seq // TM

    def body(a_ref, b_ref, w_ref, x_ref, gt_ref, t_ref, xo_ref):
        tv = _nn(a_ref[...], w_ref[0:D_GRP, :]) + _nn(b_ref[...], w_ref[D_GRP:2 * D_GRP, :])
        t_ref[...] = tv
        xo_ref[...] = x_ref[...] + gt_ref[...] * tv

    row = pl.BlockSpec((TM, d), lambda m: (m, 0))
    half = pl.BlockSpec((TM, D_GRP), lambda m: (m, 0))
    return pl.pallas_call(
        body, name="mix_out", grid=(t // TM,),
        in_specs=[half, half, pl.BlockSpec((2 * D_GRP, d), lambda m: (0, 0)), row,
                  pl.BlockSpec((None, 1, d), lambda m: (m // per, 0, 0))],
        out_specs=[row, row],
        out_shape=[_sds((t, d), F32), _sds((t, d), F32)],
        compiler_params=_cp(1))(on_sb, on_dil, w_out, x, gt)


def _sb_masks():
    lane = lax.broadcasted_iota(jnp.int32, (1, 2 * HEAD_DIM), 1)
    hm0 = lane < HEAD_DIM
    rel = lax.broadcasted_iota(jnp.int32, (TQ, KB), 0) - lax.broadcasted_iota(jnp.int32, (TQ, KB), 1)
    kr = lax.broadcasted_iota(jnp.int32, (KB, KB), 0)
    kc = lax.broadcasted_iota(jnp.int32, (KB, KB), 1)
    return hm0, rel, kr, kc


def _stack_pair(x, hm0):
    zero = jnp.zeros_like(x)
    return jnp.concatenate([jnp.where(hm0, x, zero), jnp.where(hm0, zero, x)], axis=0)


def _headnorm_pair(o, gv, hm0):
    o2 = o * o
    ms0 = jnp.sum(jnp.where(hm0, o2, 0.0), axis=-1, keepdims=True) * (1.0 / HEAD_DIM)
    ms1 = jnp.sum(jnp.where(hm0, 0.0, o2), axis=-1, keepdims=True) * (1.0 / HEAD_DIM)
    r = jnp.where(hm0, lax.rsqrt(ms0 + EPS), lax.rsqrt(ms1 + EPS))
    return (o * r) * gv


SB_DEAD = -104.0


def _alive(c_l):
    return (jnp.max(c_l) > SB_DEAD).astype(jnp.int32)


def _sb_fwd(qkv6, g_sb, nb, seq, carry=None):
    nq = seq // TQ

    def body(q_ref, k_ref, v_ref, g_ref, o_ref, on_ref):
        qi = pl.program_id(2)
        hm0, rel, kr, kc = _sb_masks()
        upper = (kr > kc).astype(BF)
        qs = _stack_pair(q_ref[...], hm0)
        causal2 = jnp.concatenate([rel, rel], axis=0) > 0

        def block(kj, causal, c_l, acc):
            ks = pl.multiple_of(kj * KB, KB)
            z = _nt(qs, k_ref[pl.ds(ks, KB), :]) * SCALE
            sp = _softplus(z)
            ln = -sp if causal is None else jnp.where(causal, -sp, 0.0)
            suf = _nn2(ln, upper)
            w = jnp.exp((z - sp) + (suf + c_l))
            if causal is not None:
                w = jnp.where(causal, w, 0.0)
            return c_l + (suf[:, 0:1] + ln[:, 0:1]), acc + _nn(w.astype(BF), v_ref[pl.ds(ks, KB), :])

        c_l, acc = block(qi, causal2, jnp.zeros((2 * TQ, 1), F32), jnp.zeros((2 * TQ, 2 * HEAD_DIM), F32))

        def cond(carry):
            return jnp.logical_and(carry[0] <= qi, carry[1] > 0)

        def kbody(carry):
            it, _, c_l, acc = carry
            c_l, acc = block(qi - it, None, c_l, acc)
            return it + 1, _alive(c_l), c_l, acc

        acc = lax.while_loop(cond, kbody, (jnp.int32(1), _alive(c_l), c_l, acc))[3]
        o = jnp.where(hm0, acc[:TQ], acc[TQ:])
        o_ref[...] = o
        on_ref[...] = _headnorm_pair(o, g_ref[...], hm0).astype(BF)

    w = 2 * HEAD_DIM
    full = lambda i: pl.BlockSpec((None, None, seq, w), lambda b, hp, q: (i, b, 0, hp))
    qblk = pl.BlockSpec((None, None, TQ, w), lambda b, hp, q: (0, b, q, hp))
    oblk = pl.BlockSpec((None, TQ, w), lambda b, hp, q: (b, q, hp))
    return _call(
        body, "sb_fwd", (nb, N_HEADS // 2, nq),
        [qblk, full(1), full(2), pl.BlockSpec((1, w), lambda b, hp, q: (0, hp))],
        [oblk, oblk],
        [_sds((nb, seq, D_GRP), F32), _sds((nb, seq, D_GRP), BF)],
        (qkv6, qkv6, qkv6, g_sb), carry=carry)


def _sb_bwd(qkv6, do, nb, seq, carry=None):
    nq = seq // TQ
    nk = seq // KB

    def body(q_ref, k_ref, v_ref, do_ref, out_ref, dk_acc, dv_acc, g_st, s_st):
        qi = pl.program_id(2)
        hm0, rel, kr, kc = _sb_masks()
        upper = (kr > kc).astype(BF)
        lower = (kr < kc).astype(BF)

        @pl.when(qi == 0)
        def _():
            dk_acc[...] = jnp.zeros_like(dk_acc)
            dv_acc[...] = jnp.zeros_like(dv_acc)

        qs = _stack_pair(q_ref[...], hm0)
        dos = _stack_pair(do_ref[...], hm0).astype(BF)
        causal2 = jnp.concatenate([rel, rel], axis=0) > 0

        def weights(kj, causal, c_l):
            ks = pl.multiple_of(kj * KB, KB)
            vb = v_ref[pl.ds(ks, KB), :]
            z = _nt(qs, k_ref[pl.ds(ks, KB), :]) * SCALE
            sp = _softplus(z)
            ln = -sp if causal is None else jnp.where(causal, -sp, 0.0)
            suf = _nn2(ln, upper)
            lsz = z - sp
            w = jnp.exp(lsz + (suf + c_l))
            if causal is not None:
                w = jnp.where(causal, w, 0.0)
            g_st[kj] = w * _nt(dos, vb)
            s_st[kj] = jnp.exp(lsz)
            dv_acc[pl.ds(ks, KB), :] += _tn(w.astype(BF), dos)
            return c_l + (suf[:, 0:1] + ln[:, 0:1])

        zc = jnp.zeros((2 * TQ, 1), F32)
        c_l = weights(qi, causal2, zc)

        def acond(carry):
            return jnp.logical_and(carry[0] <= qi, carry[1] > 0)

        def abody(carry):
            c_l = weights(qi - carry[0], None, carry[2])
            return carry[0] + 1, _alive(c_l), c_l

        n_used = lax.while_loop(acond, abody, (jnp.int32(1), _alive(c_l), c_l))[0]

        def grads(kj, causal, c_g, dq):
            ks = pl.multiple_of(kj * KB, KB)
            kb = k_ref[pl.ds(ks, KB), :]
            g = g_st[kj]
            sig = s_st[kj]
            pre = _nn(g.astype(BF), lower)
            dz = g * (1.0 - sig) - sig * (pre + c_g)
            if causal is not None:
                dz = jnp.where(causal, dz, 0.0)
            dzb = (dz * SCALE).astype(BF)
            dk_acc[pl.ds(ks, KB), :] += _tn(dzb, qs)
            return c_g + (pre[:, KB - 1:KB] + g[:, KB - 1:KB]), dq + _nn(dzb, kb)

        c_g, dq = lax.fori_loop(qi - n_used + 1, qi, lambda kj, cr: grads(kj, None, *cr),
                                (zc, jnp.zeros((2 * TQ, 2 * HEAD_DIM), F32)))
        _, dq = grads(qi, causal2, c_g, dq)
        dq = jnp.where(hm0, dq[:TQ], dq[TQ:])
        out_ref[0, pl.ds(pl.multiple_of(qi * TQ, TQ), TQ), :] = dq.astype(BF)

        @pl.when(qi == nq - 1)
        def _():
            out_ref[1] = dk_acc[...].astype(BF)
            out_ref[2] = dv_acc[...].astype(BF)

    w = 2 * HEAD_DIM
    full = lambda i: pl.BlockSpec((None, None, seq, w), lambda b, hp, q: (i, b, 0, hp))
    qblk = pl.BlockSpec((None, None, TQ, w), lambda b, hp, q: (0, b, q, hp))
    oblk = pl.BlockSpec((None, TQ, w), lambda b, hp, q: (b, q, hp))
    return _call(
        body, "sb_bwd", (nb, N_HEADS // 2, nq),
        [qblk, full(1), full(2), oblk],
        [pl.BlockSpec((3, None, seq, w), lambda b, hp, q: (0, b, 0, hp))],
        [_sds((6, nb, seq, D_GRP), BF)], (qkv6, qkv6, qkv6, do),
        scratch=[pltpu.VMEM((seq, w), F32), pltpu.VMEM((seq, w), F32),
                 pltpu.VMEM((nk, 2 * TQ, KB), F32), pltpu.VMEM((nk, 2 * TQ, KB), F32)],
        carry=carry)


def _t5_bucket(n):
    max_exact = N_BUCKETS // 2
    nf = np.maximum(n, 1).astype(np.float32)
    large = max_exact + (np.log(nf / max_exact) / math.log(MAX_DISTANCE / max_exact)
                         * (N_BUCKETS - max_exact)).astype(np.int32)
    large = np.minimum(large, N_BUCKETS - 1)
    return np.where(n < max_exact, n, large).astype(np.int32)


def _bucket_map(dilation):
    step = BLOCK + np.arange(BLOCK)[:, None] - np.arange(2 * BLOCK)[None, :]
    return _t5_bucket(np.clip(step, 0, N_STEPS) * dilation)


GRP_HEADS = 4
GRP_W = GRP_HEADS * HEAD_DIM


def _dil_masks():
    lane = lax.broadcasted_iota(jnp.int32, (1, GRP_W), 1)
    heads = [jnp.logical_and(lane >= HEAD_DIM * i, lane < HEAD_DIM * (i + 1)) for i in range(GRP_HEADS)]
    iq = jnp.bitwise_and(lax.broadcasted_iota(jnp.int32, (GRP_HEADS * BLOCK, BLOCK), 0), BLOCK - 1)
    ik = lax.broadcasted_iota(jnp.int32, (GRP_HEADS * BLOCK, BLOCK), 1)
    return heads, ik <= iq, ik >= iq


def _stack_heads(x, heads):
    zero = jnp.zeros_like(x)
    return jnp.concatenate([jnp.where(hm, x, zero) for hm in heads], axis=0)


def _unstack_heads(xs, heads):
    out = xs[0:BLOCK]
    for i in range(1, GRP_HEADS):
        out = jnp.where(heads[i], xs[i * BLOCK:(i + 1) * BLOCK], out)
    return out


def _dil_rows(n):
    rs = pl.multiple_of(n * BLOCK, BLOCK)
    ps = pl.multiple_of(jnp.maximum(n - 1, 0) * BLOCK, BLOCK)
    return pl.ds(rs, BLOCK), pl.ds(ps, BLOCK)


def _dil_probs(qs, kc, kp, b_ref, gi, valid_c, valid_p):
    rows = slice(gi * GRP_HEADS * BLOCK, (gi + 1) * GRP_HEADS * BLOCK)
    zc = _nt(qs, kc) * SCALE + b_ref[rows, BLOCK:2 * BLOCK]
    zp = _nt(qs, kp) * SCALE + b_ref[rows, 0:BLOCK]
    zc = jnp.where(valid_c, zc, NEG_INF)
    zp = jnp.where(valid_p, zp, NEG_INF)
    m = jnp.maximum(jnp.max(zc, axis=-1, keepdims=True), jnp.max(zp, axis=-1, keepdims=True))
    ec = jnp.exp(zc - m)
    ep = jnp.exp(zp - m)
    den = jnp.sum(ec, axis=-1, keepdims=True) + jnp.sum(ep, axis=-1, keepdims=True)
    return ec, ep, den, m


def _dil_fwd(qkv6r, base, bias, nb, sub_len, dilation):
    n_blk = sub_len // BLOCK

    def body(q_ref, k_ref, v_ref, b_ref, o_ref, l_ref):
        heads, valid_c, valid_p0 = _dil_masks()

        def nbody(n, carry):
            cur, prev = _dil_rows(n)
            valid_p = jnp.logical_and(valid_p0, n > 0)
            for gi in range(N_HEADS // GRP_HEADS):
                lanes = slice(gi * GRP_W, (gi + 1) * GRP_W)
                qs = _stack_heads(q_ref[cur, lanes], heads)
                ec, ep, den, m = _dil_probs(qs, k_ref[cur, lanes], k_ref[prev, lanes], b_ref, gi, valid_c, valid_p)
                o = (_nn(ec.astype(BF), v_ref[cur, lanes]) + _nn(ep.astype(BF), v_ref[prev, lanes])) / den
                o_ref[cur, lanes] = _unstack_heads(o, heads)
                l_ref[cur, lanes] = _unstack_heads(jnp.broadcast_to(m + jnp.log(den), o.shape), heads)
            return carry

        lax.fori_loop(0, n_blk, nbody, 0)

    seqblk = lambda i: pl.BlockSpec((None, None, sub_len, D_GRP), lambda b, r: (i, b, 0, r))
    oblk = pl.BlockSpec((None, sub_len, D_GRP), lambda b, r: (b, 0, r))
    shp = _sds((nb, sub_len, dilation * D_GRP), F32)
    return pl.pallas_call(
        body, name="dil_fwd_%d" % dilation, grid=(nb, dilation),
        in_specs=[seqblk(base), seqblk(base + 1), seqblk(base + 2), _whole(bias)],
        out_specs=[oblk, oblk], out_shape=[shp, shp],
        compiler_params=_cp(2))(qkv6r, qkv6r, qkv6r, bias)


def _dil_bwd(qkv6r, base, bias, do_c, dd_c, nb, sub_len, dilation, carry=None):
    n_blk = sub_len // BLOCK

    def body(q_ref, k_ref, v_ref, b_ref, do_ref, dd_ref, out_ref, a_ref, dk_acc, dv_acc):
        heads, valid_c, valid_p0 = _dil_masks()
        first = jnp.logical_and(pl.program_id(0) == 0, pl.program_id(1) == 0)

        @pl.when(first)
        def _():
            a_ref[...] = jnp.zeros_like(a_ref)

        dk_acc[...] = jnp.zeros_like(dk_acc)
        dv_acc[...] = jnp.zeros_like(dv_acc)

        def nbody(n, carry):
            cur, prev = _dil_rows(n)
            valid_p = jnp.logical_and(valid_p0, n > 0)
            for gi in range(N_HEADS // GRP_HEADS):
                lanes = slice(gi * GRP_W, (gi + 1) * GRP_W)
                kc, kp = k_ref[cur, lanes], k_ref[prev, lanes]
                vc, vp = v_ref[cur, lanes], v_ref[prev, lanes]
                qs = _stack_heads(q_ref[cur, lanes], heads)
                dos = _stack_heads(do_ref[cur, lanes], heads).astype(BF)
                dds = jnp.sum(_stack_heads(dd_ref[cur, lanes], heads), axis=-1, keepdims=True) * (1.0 / HEAD_DIM)
                ec, ep, den, _ = _dil_probs(qs, kc, kp, b_ref, gi, valid_c, valid_p)
                inv = 1.0 / den
                pc = ec * inv
                pp = ep * inv
                dzc = pc * (_nt(dos, vc) + dds)
                dzp = pp * (_nt(dos, vp) + dds)
                rows = slice(gi * GRP_HEADS * BLOCK, (gi + 1) * GRP_HEADS * BLOCK)
                a_ref[rows, BLOCK:2 * BLOCK] += dzc
                a_ref[rows, 0:BLOCK] += dzp
                dzcb = (dzc * SCALE).astype(BF)
                dzpb = (dzp * SCALE).astype(BF)
                out_ref[0, cur, lanes] = _unstack_heads(_nn(dzcb, kc) + _nn(dzpb, kp), heads).astype(BF)
                dk_acc[cur, lanes] += _tn(dzcb, qs)
                dk_acc[prev, lanes] += _tn(dzpb, qs)
                dv_acc[cur, lanes] += _tn(pc.astype(BF), dos)
                dv_acc[prev, lanes] += _tn(pp.astype(BF), dos)
            return carry

        lax.fori_loop(0, n_blk, nbody, 0)
        out_ref[1] = dk_acc[...].astype(BF)
        out_ref[2] = dv_acc[...].astype(BF)

    seqblk = lambda i: pl.BlockSpec((None, None, sub_len, D_GRP), lambda b, r: (i, b, 0, r))
    oblk = pl.BlockSpec((None, sub_len, D_GRP), lambda b, r: (b, 0, r))
    return _call(
        body, "dil_bwd_%d" % dilation, (nb, dilation),
        [seqblk(base), seqblk(base + 1), seqblk(base + 2), _whole(bias), oblk, oblk],
        [pl.BlockSpec((3, None, sub_len, D_GRP), lambda b, r: (0, b, 0, r)),
         pl.BlockSpec((N_HEADS * BLOCK, 2 * BLOCK), lambda b, r: (0, 0))],
        [_sds((3, nb, sub_len, dilation * D_GRP), BF), _sds((N_HEADS * BLOCK, 2 * BLOCK), F32)],
        (qkv6r, qkv6r, qkv6r, bias, do_c, dd_c),
        scratch=[pltpu.VMEM((sub_len, D_GRP), F32)] * 2, carry=carry)


def _group_ones():
    idx = np.arange(D_GRP) // HEAD_DIM
    return jnp.asarray((idx[:, None] == idx[None, :]).astype(np.float32), dtype=BF)


def _dil_alphas(l1, l4, l16):
    mx = jnp.maximum(jnp.maximum(l1, l4), l16)
    e1 = jnp.exp(l1 - mx)
    e4 = jnp.exp(l4 - mx)
    e16 = jnp.exp(l16 - mx)
    den = e1 + e4 + e16
    return e1 / den, e4 / den, e16 / den


def _residue_spec(dil):
    return pl.BlockSpec((TM // dil, dil * D_GRP), lambda m: (m, 0))


def _from_residue(src, dil, cg, buf):
    if dil == 1:
        return src[:, cg * 128:(cg + 1) * 128]
    for r in range(dil):
        buf[pl.ds(r, TM // dil, stride=dil), :] = src[:, r * D_GRP + cg * 128:r * D_GRP + (cg + 1) * 128]
    return buf[...]


def _to_residue(dst, dil, cg, buf, val):
    if dil == 1:
        dst[:, cg * 128:(cg + 1) * 128] = val.astype(dst.dtype)
        return
    buf[...] = val
    for r in range(dil):
        dst[:, r * D_GRP + cg * 128:r * D_GRP + (cg + 1) * 128] = (
            buf[pl.ds(r, TM // dil, stride=dil), :].astype(dst.dtype))


def _pair_sum(x, hm0):
    s0 = jnp.sum(jnp.where(hm0, x, 0.0), axis=-1, keepdims=True)
    s1 = jnp.sum(jnp.where(hm0, 0.0, x), axis=-1, keepdims=True)
    return jnp.where(hm0, s0, s1)


def _dil_comb(os, ls, g_dil):
    t = os[0].shape[0]
    dils = [dil for _, dil in DIL_CONFIGS]

    def body(o1, l1, o4, l4, o16, l16, g_ref, o_ref, on_ref, b0, b1, b2, b3):
        hm0 = lax.broadcasted_iota(jnp.int32, (1, 128), 1) < HEAD_DIM
        for cg in range(D_GRP // 128):
            lanes = slice(cg * 128, (cg + 1) * 128)
            ov = [_from_residue(src, dil, cg, buf) for src, dil, buf in zip((o1, o4, o16), dils, (None, b0, b1))]
            lv = [_from_residue(src, dil, cg, buf) for src, dil, buf in zip((l1, l4, l16), dils, (None, b2, b3))]
            a1, a4, a16 = _dil_alphas(*lv)
            o = a1 * ov[0] + a4 * ov[1] + a16 * ov[2]
            o_ref[:, lanes] = o
            on_ref[:, lanes] = _headnorm_pair(o, g_ref[:, lanes], hm0).astype(BF)

    blk = pl.BlockSpec((TM, D_GRP), lambda m: (m, 0))
    specs = [_residue_spec(dil) for dil in dils for _ in range(2)]
    return pl.pallas_call(
        body, name="dil_comb", grid=(t // TM,),
        in_specs=specs + [pl.BlockSpec((1, D_GRP), lambda m: (0, 0))],
        out_specs=[blk, blk],
        out_shape=[_sds((t, D_GRP), F32), _sds((t, D_GRP), BF)],
        scratch_shapes=[pltpu.VMEM((TM, 128), F32)] * 4,
        compiler_params=_cp(1))(os[0], ls[0], os[1], ls[1], os[2], ls[2], g_dil)


def _dil_comb_bwd(do, os, ls):
    t = do.shape[0]
    dils = [dil for _, dil in DIL_CONFIGS]

    def body(do_ref, o1, l1, o4, l4, o16, l16, d1, d4, d16, e1, e4, e16, b0, b1, b2, b3):
        hm0 = lax.broadcasted_iota(jnp.int32, (1, 128), 1) < HEAD_DIM
        for cg in range(D_GRP // 128):
            dov = do_ref[:, cg * 128:(cg + 1) * 128]
            ov = [_from_residue(src, dil, cg, buf) for src, dil, buf in zip((o1, o4, o16), dils, (None, b0, b1))]
            lv = [_from_residue(src, dil, cg, buf) for src, dil, buf in zip((l1, l4, l16), dils, (None, b2, b3))]
            al = _dil_alphas(*lv)
            sbar = al[0] * _pair_sum(dov * ov[0], hm0)
            for a_c, o_c in zip(al[1:], ov[1:]):
                sbar = sbar + a_c * _pair_sum(dov * o_c, hm0)
            for a_c, dil, dref, eref in zip(al, dils, (d1, d4, d16), (e1, e4, e16)):
                _to_residue(dref, dil, cg, b0, a_c * dov)
                _to_residue(eref, dil, cg, b1, -a_c * sbar)

    specs = [_residue_spec(dil) for dil in dils]
    return pl.pallas_call(
        body, name="dil_comb_bwd", grid=(t // TM,),
        in_specs=[pl.BlockSpec((TM, D_GRP), lambda m: (m, 0))] + [sp for sp in specs for _ in range(2)],
        out_specs=specs + specs,
        out_shape=[_sds((t // dil, dil * D_GRP), BF) for dil in dils]
        + [_sds((t // dil, dil * D_GRP), F32) for dil in dils],
        scratch_shapes=[pltpu.VMEM((TM, 128), F32)] * 4,
        compiler_params=_cp(1))(do, os[0], ls[0], os[1], ls[1], os[2], ls[2])


def _dqkv_dil_sum(ds, dqkv6):
    t = dqkv6.shape[1]
    dils = [dil for _, dil in DIL_CONFIGS]

    def body(*refs):
        srcs, o_ref, acc = refs[:len(dils)], refs[len(dils) + 1], refs[len(dils) + 2]
        for a in range(3):
            for cg in range(D_GRP // 128):
                for src, dil in zip(srcs, dils):
                    for r in range(dil):
                        part = src[a, :, r * D_GRP + cg * 128:r * D_GRP + (cg + 1) * 128].astype(F32)
                        rows = pl.ds(r, TM // dil, stride=dil) if dil > 1 else slice(None)
                        if dil == dils[0]:
                            acc[rows, :] = part
                        else:
                            acc[rows, :] += part
                o_ref[a, :, cg * 128:(cg + 1) * 128] = acc[...].astype(BF)

    return pl.pallas_call(
        body, name="dqkv_dil_sum", grid=(t // TM,),
        in_specs=[pl.BlockSpec((3, TM // dil, dil * D_GRP), lambda m: (0, m, 0)) for dil in dils]
        + [pl.BlockSpec(memory_space=pl.ANY)],
        out_specs=pl.BlockSpec((3, TM, D_GRP), lambda m: (1, m, 0)),
        out_shape=_sds((6, t, D_GRP), BF), input_output_aliases={len(dils): 0},
        scratch_shapes=[pltpu.VMEM((TM, 128), F32)],
        compiler_params=_cp(1))(*ds, dqkv6)


def _relbias_grad(a_all, onehot):
    def body(a_ref, oh_ref, o_ref):
        acc = jnp.zeros((N_HEADS, N_BUCKETS), F32)
        for c in range(len(DIL_CONFIGS)):
            av = a_ref[c]
            hi = av.astype(BF)
            lo = (av - hi.astype(F32)).astype(BF)
            acc = acc + _nt(hi, oh_ref[c]) + _nt(lo, oh_ref[c])
        o_ref[...] = acc

    return pl.pallas_call(body, name="relbias_grad", out_shape=_sds((N_HEADS, N_BUCKETS), F32),
                          compiler_params=_cp())(a_all, onehot)


def _headnorm_bwd(dn, o, gv, mv):
    ms = _nn2(o * o, mv) * (1.0 / HEAD_DIM)
    r = lax.rsqrt(ms + EPS)
    nrm = o * r
    dg = jnp.sum(dn * nrm, axis=0, keepdims=True)
    dnn = dn * gv
    do = r * (dnn - nrm * (_nn2(dnn * nrm, mv) * (1.0 / HEAD_DIM)))
    return do, dg


def _mix_bwd_out(dx, gt, tv, w_out, o_sb, o_dil, on_sb, on_dil, g_sb, g_dil, ones_g, seq, carry=None):
    t, d = dx.shape
    per = seq // TM
    nb = t // seq

    def body(dx_ref, gt_ref, t_ref, w_ref, osb, odl, onsb, ondl, gsb, gdl, m_ref,
             dosb, dodl, dgt_ref, dgsb, dgdl, dw_ref):
        m = pl.program_id(0)
        dxv = dx_ref[...]
        dt = (gt_ref[...] * dxv).astype(BF)
        _acc_rows(dgt_ref, jnp.sum(dxv * t_ref[...], axis=0, keepdims=True), m % per == 0)
        mv = m_ref[...]
        don_sb = _nt(dt, w_ref[0:D_GRP, :])
        don_dl = _nt(dt, w_ref[D_GRP:2 * D_GRP, :])
        do1, dg1 = _headnorm_bwd(don_sb, osb[...], gsb[...], mv)
        do2, dg2 = _headnorm_bwd(don_dl, odl[...], gdl[...], mv)
        dosb[...] = do1
        dodl[...] = do2
        _acc_rows(dgsb, dg1, m == 0)
        _acc_rows(dgdl, dg2, m == 0)
        p1 = _tn(onsb[...], dt)
        p2 = _tn(ondl[...], dt)

        @pl.when(m == 0)
        def _():
            dw_ref[0:D_GRP, :] = p1
            dw_ref[D_GRP:2 * D_GRP, :] = p2

        @pl.when(m != 0)
        def _():
            dw_ref[0:D_GRP, :] += p1
            dw_ref[D_GRP:2 * D_GRP, :] += p2

    row = pl.BlockSpec((TM, d), lambda m: (m, 0))
    half = pl.BlockSpec((TM, D_GRP), lambda m: (m, 0))
    ex = pl.BlockSpec((None, 1, d), lambda m: (m // per, 0, 0))
    gvec = pl.BlockSpec((1, D_GRP), lambda m: (0, 0))
    wblk = pl.BlockSpec((2 * D_GRP, d), lambda m: (0, 0))
    return _call(
        body, "mix_bwd_out", (t // TM,),
        [row, ex, row, wblk, half, half, half, half, gvec, gvec, pl.BlockSpec((D_GRP, D_GRP), lambda m: (0, 0))],
        [half, half, ex, gvec, gvec, wblk],
        [_sds((t, D_GRP), F32), _sds((t, D_GRP), F32), _sds((nb, 1, d), F32),
         _sds((1, D_GRP), F32), _sds((1, D_GRP), F32), _sds((2 * D_GRP, d), F32)],
        (dx, gt, tv, w_out, o_sb, o_dil, on_sb, on_dil, g_sb, g_dil, ones_g), carry=carry)


def _dw_in(h, dqkv6, carry=None):
    t, d = h.shape
    wc = 6 * D_GRP // N_CHIPS

    def body(h_ref, g_ref, o_ref):
        kt = pl.program_id(0)
        hv = h_ref[...]
        for j in range(N_CHIPS):
            p = _tn(hv, _chip_cols(g_ref, j, wc))

            @pl.when(kt == 0)
            def _(p=p, j=j):
                o_ref[j, 0] = p

            @pl.when(kt != 0)
            def _(p=p, j=j):
                o_ref[j, 0] += p

    return _call(
        body, "dw_in", (t // TK_W,),
        [pl.BlockSpec((TK_W, d), lambda kt: (kt, 0)), pl.BlockSpec((6, TK_W, D_GRP), lambda kt: (0, kt, 0))],
        [pl.BlockSpec((N_CHIPS, 1, d, wc), lambda kt: (0, 0, 0, 0))],
        [_sds((N_CHIPS, 1, d, wc), F32)], (h, dqkv6), carry=carry)


def _mix_bwd_dh(dqkv6, w_in, x, g, sc, dxo, seq, carry=None):
    _, t, _ = dqkv6.shape
    d = x.shape[-1]
    wc = w_in.shape[-1]
    per = seq // TM
    nb = t // seq

    def body(g6_ref, w_ref, x_ref, g_ref, sc_ref, dxo_ref, dx_ref, dsh_ref, dsc_ref, dg_ref):
        m = pl.program_id(0)
        dh = _nt(_chip_cols(g6_ref, 0, wc), w_ref[0, 0])
        for j in range(1, N_CHIPS):
            dh = dh + _nt(_chip_cols(g6_ref, j, wc), w_ref[j, 0])
        dx, dsh, dsc, dg = _modnorm_bwd_tile(dh, x_ref[...], g_ref[...], sc_ref[...], dxo_ref[...])
        dx_ref[...] = dx
        _acc_rows(dsh_ref, dsh, m % per == 0)
        _acc_rows(dsc_ref, dsc, m % per == 0)
        _acc_rows(dg_ref, dg, m == 0)

    row = pl.BlockSpec((TM, d), lambda m: (m, 0))
    ex = pl.BlockSpec((None, 1, d), lambda m: (m // per, 0, 0))
    vec = pl.BlockSpec((1, d), lambda m: (0, 0))
    return _call(
        body, "mix_bwd_dh", (t // TM,),
        [pl.BlockSpec((6, TM, D_GRP), lambda m: (0, m, 0)), _whole(w_in), row, vec, ex, row],
        [row, ex, ex, vec],
        [_sds((t, d), F32), _sds((nb, 1, d), F32), _sds((nb, 1, d), F32), _sds((1, d), F32)],
        (dqkv6, w_in, x, g, sc, dxo), carry=carry)


def _ffn_down_loss(s, wd, x, gt, seq, coef, g, target):
    _, t, fs = s.shape
    d = x.shape[-1]
    per = seq // TM
    steps = t // TM

    def body(s_ref, w_ref, x_ref, gt_ref, g_ref, t_ref, f_ref, dx_ref, dg_ref, loss_ref, lacc):
        m = pl.program_id(0)
        f = _nn(s_ref[0], w_ref[0, 0])
        for j in range(1, N_CHIPS):
            f = f + _nn(s_ref[j], w_ref[j, 0])
        f_ref[...] = f
        xv = x_ref[...] + (coef * gt_ref[...]) * f
        gv = g_ref[...]
        r = lax.rsqrt(jnp.mean(xv * xv, axis=-1, keepdims=True) + EPS)
        n = xv * r
        err = n * gv - t_ref[...]
        dy = err * (1.0 / d)
        _acc_rows(dg_ref, jnp.sum(dy * n, axis=0, keepdims=True), m == 0)
        dn = dy * gv
        dx_ref[...] = r * (dn - n * jnp.mean(dn * n, axis=-1, keepdims=True))
        _acc_rows(lacc, jnp.sum(err * err, axis=0, keepdims=True), m == 0)

        @pl.when(m == steps - 1)
        def _():
            tot = jnp.sum(lacc[...], axis=-1, keepdims=True) * (0.5 / d)
            loss_ref[...] = jnp.broadcast_to(tot, (1, 128))

    row = pl.BlockSpec((TM, d), lambda m: (m, 0))
    vec = pl.BlockSpec((1, d), lambda m: (0, 0))
    return pl.pallas_call(
        body, name="ffn_down_loss", grid=(steps,),
        in_specs=[pl.BlockSpec((N_CHIPS, TM, fs), lambda m: (0, m, 0)), _whole(wd), row,
                  pl.BlockSpec((None, 1, d), lambda m: (m // per, 0, 0)), vec, row],
        out_specs=[row, row, vec, pl.BlockSpec((1, 128), lambda m: (0, 0))],
        out_shape=[_sds((t, d), F32), _sds((t, d), F32), _sds((1, d), F32), _sds((1, 128), F32)],
        scratch_shapes=[pltpu.VMEM((1, d), F32)],
        compiler_params=_cp(1))(s, wd, x, gt, g, target)


def _row_tile(rows, cols):
    best = rows
    for tr in range(8, rows + 1, 8):
        if rows % tr == 0 and tr * cols * 4 <= (1 << 20):
            best = tr
    if best * cols * 4 > (1 << 21):
        best = 8
    return best


def _adamw(w, g_arr, g_sel, m, v):
    rows, cols = w.shape
    tr = _row_tile(rows, cols)
    b1c = 1.0 - ADAM_B1 ** ADAM_STEP
    b2c = 1.0 - ADAM_B2 ** ADAM_STEP

    def body(w_ref, g_ref, m_ref, v_ref, go_ref, d_ref, mo_ref, vo_ref):
        gv = g_ref[...]
        mn = ADAM_B1 * m_ref[...] + (1.0 - ADAM_B1) * gv
        vn = ADAM_B2 * v_ref[...] + (1.0 - ADAM_B2) * (gv * gv)
        go_ref[...] = gv
        mo_ref[...] = mn
        vo_ref[...] = vn
        d_ref[...] = -ADAM_LR * ((mn / b1c) / (jnp.sqrt(vn / b2c) + ADAM_EPS) + ADAM_WD * w_ref[...])

    blk = pl.BlockSpec((tr, cols), lambda i: (i, 0))
    shp = _sds((rows, cols), F32)
    return pl.pallas_call(
        body, name="adamw", grid=(rows // tr,),
        in_specs=[blk, pl.BlockSpec((None, tr, cols), lambda i: (g_sel, i, 0)), blk, blk],
        out_specs=[blk] * 4, out_shape=[shp] * 4,
        compiler_params=_cp(1))(w, g_arr, m, v)


def _flip(v, bit):
    return 1 - v if bit else v


def _my_place():
    x, y, c = lax.axis_index("x"), lax.axis_index("y"), lax.axis_index("c")
    return x, y, c


class _Exchange:
    def __init__(self, operands, out_shape, aliases, sems, start, finish):
        self.operands, self.out_shape, self.aliases, self.sems = list(operands), list(out_shape), dict(aliases), list(sems)
        self.start, self.finish = start, finish


def _join(exchanges):
    exchanges = [e for e in exchanges if e is not None]
    if not exchanges:
        return None
    ops, outs, sems, aliases, spans = [], [], [], {}, []
    for e in exchanges:
        spans.append((len(ops), len(outs), len(sems), e))
        for i, j in e.aliases.items():
            aliases[len(ops) + i] = len(outs) + j
        ops += e.operands
        outs += e.out_shape
        sems += e.sems

    def run(which):
        def go(ins, res, sm):
            for io, oo, so, e in spans:
                getattr(e, which)(ins[io:io + len(e.operands)], res[oo:oo + len(e.out_shape)], sm[so:so + len(e.sems)])
        return go

    return _Exchange(ops, outs, aliases, sems, run("start"), run("finish"))


def _call(body, name, grid, in_specs, out_specs, out_shape, args, scratch=(), carry=None, io_alias=None):
    in_specs, out_specs, out_shape, scratch = list(in_specs), list(out_specs), list(out_shape), list(scratch)
    io_alias = dict(io_alias or {})
    if carry is None:
        return pl.pallas_call(body, name=name, grid=grid, in_specs=in_specs, out_specs=out_specs,
                              out_shape=out_shape, scratch_shapes=scratch, input_output_aliases=io_alias,
                              compiler_params=_cp(len(grid)))(*args)
    n_in, n_out, n_s = len(in_specs), len(out_specs), len(scratch)
    c_in, c_out = len(carry.operands), len(carry.out_shape)
    any_spec = pl.BlockSpec(memory_space=pl.ANY)

    def wrapped(*refs):
        ins, cins = refs[:n_in], refs[n_in:n_in + c_in]
        o0 = n_in + c_in
        outs, couts = refs[o0:o0 + n_out], refs[o0 + n_out:o0 + n_out + c_out]
        s0 = o0 + n_out + c_out
        scr, sems = refs[s0:s0 + n_s], refs[s0 + n_s:]
        first = pl.program_id(0) == 0
        last = pl.program_id(0) == grid[0] - 1
        for ax in range(1, len(grid)):
            first = jnp.logical_and(first, pl.program_id(ax) == 0)
            last = jnp.logical_and(last, pl.program_id(ax) == grid[ax] - 1)

        @pl.when(first)
        def _():
            carry.start(cins, couts, sems)

        body(*ins, *outs, *scr)

        @pl.when(last)
        def _():
            carry.finish(cins, couts, sems)

    return pl.pallas_call(
        wrapped, name=name, grid=grid, in_specs=in_specs + [any_spec] * c_in,
        out_specs=out_specs + [any_spec] * c_out, out_shape=out_shape + carry.out_shape,
        scratch_shapes=scratch + carry.sems,
        input_output_aliases={**io_alias, **{n_in + i: n_out + j for i, j in carry.aliases.items()}},
        compiler_params=_cp(len(grid)))(*args, *carry.operands)


def _whole_call(body, name, args, out_shape, scratch, carry=None):
    vm = pl.BlockSpec(memory_space=pltpu.VMEM)
    any_spec = pl.BlockSpec(memory_space=pl.ANY)
    out_shape, scratch = list(out_shape), list(scratch)
    n_in, n_out, n_s = len(args), len(out_shape), len(scratch)
    if carry is None:
        return pl.pallas_call(body, name=name, in_specs=[vm] * n_in, out_specs=[vm] * n_out, out_shape=out_shape,
                              scratch_shapes=scratch, compiler_params=_cp())(*args)
    c_in, c_out = len(carry.operands), len(carry.out_shape)

    def wrapped(*refs):
        ins, cins = refs[:n_in], refs[n_in:n_in + c_in]
        o0 = n_in + c_in
        outs, couts = refs[o0:o0 + n_out], refs[o0 + n_out:o0 + n_out + c_out]
        s0 = o0 + n_out + c_out
        scr, sems = refs[s0:s0 + n_s], refs[s0 + n_s:]
        carry.start(cins, couts, sems)
        body(*ins, *outs, *scr)
        carry.finish(cins, couts, sems)

    return pl.pallas_call(
        wrapped, name=name, in_specs=[vm] * n_in + [any_spec] * c_in, out_specs=[vm] * n_out + [any_spec] * c_out,
        out_shape=out_shape + carry.out_shape, scratch_shapes=scratch + carry.sems,
        input_output_aliases={n_in + i: n_out + j for i, j in carry.aliases.items()},
        compiler_params=_cp())(*args, *carry.operands)


def _alone(name, ex):
    any_spec = pl.BlockSpec(memory_space=pl.ANY)
    c_in, c_out = len(ex.operands), len(ex.out_shape)

    def body(*refs):
        ins, outs, sems = refs[:c_in], refs[c_in:c_in + c_out], refs[c_in + c_out:]
        ex.start(ins, outs, sems)
        ex.finish(ins, outs, sems)

    return pl.pallas_call(
        body, name=name, in_specs=[any_spec] * c_in, out_specs=[any_spec] * c_out, out_shape=ex.out_shape,
        scratch_shapes=ex.sems, input_output_aliases=ex.aliases, compiler_params=_cp())(*ex.operands)


def _ada_fwd(c_pad, w_ada, b_shard, carry=None):
    d = c_pad.shape[-1]
    cols = w_ada.shape[-1]
    chunk = 384

    def body(c_ref, w_ref, b_ref, call_ref, mod_ref, part, s1, r1, s2, r2):
        x, y, c = _my_place()
        dev = 4 * x + 2 * y + c
        chip = 2 * x + y
        call_ref[dev] = c_ref[...]

        def c_copy(k):
            px, py, pc = _flip(x, (k >> 2) & 1), _flip(y, (k >> 1) & 1), _flip(c, k & 1)
            return px, py, pc

        sends = []
        for k in range(1, N_DEV):
            px, py, pc = c_copy(k)
            cp = pltpu.make_async_remote_copy(src_ref=c_ref, dst_ref=call_ref.at[dev], send_sem=s1.at[k - 1],
                                              recv_sem=r1.at[k - 1], device_id=(px, py, pc), device_id_type=MESH)
            cp.start()
            sends.append(cp)
        for k in range(1, N_DEV):
            px, py, pc = c_copy(k)
            pltpu.make_async_remote_copy(src_ref=c_ref, dst_ref=call_ref.at[4 * px + 2 * py + pc],
                                         send_sem=s1.at[k - 1], recv_sem=r1.at[k - 1],
                                         device_id=(px, py, pc), device_id_type=MESH).wait_recv()
        for cp in sends:
            cp.wait_send()

        cs = call_ref[...].reshape(N_DEV * 8, d)
        sc = (cs * jax.nn.sigmoid(cs)).astype(BF)
        for n0 in range(0, cols, chunk):
            blk = _nn(sc, w_ref[:, n0:n0 + chunk].astype(BF)) + b_ref[:, n0:n0 + chunk]
            part[:, :, n0:n0 + chunk] = blk.reshape(N_DEV, 8, chunk)

        mod_ref[chip] = part[dev]
        sends = []
        for kk in range(1, N_CHIPS):
            px, py = _flip(x, (kk >> 1) & 1), _flip(y, kk & 1)
            cp = pltpu.make_async_remote_copy(src_ref=part.at[4 * px + 2 * py + c], dst_ref=mod_ref.at[chip],
                                              send_sem=s2.at[kk - 1], recv_sem=r2.at[kk - 1],
                                              device_id=(px, py, c), device_id_type=MESH)
            cp.start()
            sends.append(cp)
        for kk in range(1, N_CHIPS):
            px, py = _flip(x, (kk >> 1) & 1), _flip(y, kk & 1)
            pltpu.make_async_remote_copy(src_ref=part.at[dev], dst_ref=mod_ref.at[2 * px + py],
                                         send_sem=s2.at[kk - 1], recv_sem=r2.at[kk - 1],
                                         device_id=(px, py, c), device_id_type=MESH).wait_recv()
        for cp in sends:
            cp.wait_send()

    return _whole_call(
        body, "ada_fwd", (c_pad, w_ada, b_shard),
        [_sds((N_DEV, 8, d), F32), _sds((N_CHIPS, 8, cols), F32)],
        [pltpu.VMEM((N_DEV, 8, cols), F32),
         pltpu.SemaphoreType.DMA((N_DEV - 1,)), pltpu.SemaphoreType.DMA((N_DEV - 1,)),
         pltpu.SemaphoreType.DMA((N_CHIPS - 1,)), pltpu.SemaphoreType.DMA((N_CHIPS - 1,))], carry=carry)


def _ag_weights(bufs, kks=(1, 2, 3), relative=False):
    n, nk = len(bufs), len(kks)

    def half(b, which):
        hr = bufs[b].shape[2] // 2
        return pl.ds(pl.multiple_of(which * hr, 16), hr)

    def copies(outs, sems, b, i, kk):
        x, y, c = _my_place()
        chip = 2 * x + y
        px, py = _flip(x, (kk >> 1) & 1), _flip(y, kk & 1)
        mine, theirs = (0, kk) if relative else (chip, 2 * px + py)
        landing = kk if relative else chip
        k = nk * b + i
        send = pltpu.make_async_remote_copy(
            src_ref=outs[b].at[mine, :, half(b, c), :], dst_ref=outs[b].at[landing, :, half(b, c), :],
            send_sem=sems[0].at[k], recv_sem=sems[1].at[k], device_id=(px, py, c), device_id_type=MESH)
        got = outs[b].at[theirs, :, half(b, c), :]
        recv = pltpu.make_async_remote_copy(
            src_ref=got, dst_ref=got, send_sem=sems[0].at[k], recv_sem=sems[1].at[k],
            device_id=(px, py, c), device_id_type=MESH)
        fwd = pltpu.make_async_remote_copy(
            src_ref=got, dst_ref=got, send_sem=sems[2].at[k], recv_sem=sems[3].at[k],
            device_id=(x, y, 1 - c), device_id_type=MESH)
        other = outs[b].at[theirs, :, half(b, 1 - c), :]
        back = pltpu.make_async_remote_copy(
            src_ref=other, dst_ref=other, send_sem=sems[2].at[k], recv_sem=sems[3].at[k],
            device_id=(x, y, 1 - c), device_id_type=MESH)
        return send, recv, fwd, back

    def each(outs, sems):
        for b in range(n):
            for i, kk in enumerate(kks):
                yield copies(outs, sems, b, i, kk)

    def start(ins, outs, sems):
        for send, _, _, _ in each(outs, sems):
            send.start()

    def finish(ins, outs, sems):
        for _, recv, fwd, _ in each(outs, sems):
            recv.wait_recv()
            fwd.start()
        for send, _, fwd, back in each(outs, sems):
            back.wait_recv()
            send.wait_send()
            fwd.wait_send()

    return _Exchange(bufs, [_sds(s.shape, s.dtype) for s in bufs], {i: i for i in range(n)},
                     [pltpu.SemaphoreType.DMA((nk * n,))] * 4, start, finish)


def _rs_d2d(grads):
    n = len(grads)

    def copy(ins, outs, sems, b):
        x, y, c = _my_place()
        hr = grads[b].shape[2] // 2
        theirs = pl.ds(pl.multiple_of((1 - c) * hr, 8), hr)
        return pltpu.make_async_remote_copy(
            src_ref=ins[b].at[:, :, theirs, :], dst_ref=outs[b], send_sem=sems[0].at[b], recv_sem=sems[1].at[b],
            device_id=(x, y, 1 - c), device_id_type=MESH)

    def start(ins, outs, sems):
        for b in range(n):
            copy(ins, outs, sems, b).start()

    def finish(ins, outs, sems):
        for b in range(n):
            copy(ins, outs, sems, b).wait()

    return _Exchange(grads, [_sds(g.shape[:2] + (g.shape[2] // 2, g.shape[3]), F32) for g in grads], {},
                     [pltpu.SemaphoreType.DMA((n,))] * 2, start, finish)


def _add_halves(core, g, land):
    nchip, ng, rows, cols = g.shape
    hr = rows // 2
    tr = _row_tile(hr, cols)
    steps = hr // tr

    def body(core_ref, g_ref, l_ref, o_ref):
        del core_ref
        o_ref[...] = (g_ref[...] + l_ref[...]).astype(BF)

    return pl.pallas_call(
        body, name="add_halves",
        grid_spec=pltpu.PrefetchScalarGridSpec(
            num_scalar_prefetch=1, grid=(nchip, ng, steps),
            in_specs=[pl.BlockSpec((None, None, tr, cols), lambda j, a, i, cr: (j, a, cr[0] * steps + i, 0)),
                      pl.BlockSpec((None, None, tr, cols), lambda j, a, i, cr: (j, a, i, 0))],
            out_specs=pl.BlockSpec((None, None, tr, cols), lambda j, a, i, cr: (j, a, i, 0))),
        out_shape=_sds((nchip, ng, hr, cols), BF),
        compiler_params=_cp(3))(core, g, land)


def _rs_ici(parts, relative=False):
    n = len(parts)

    def copies(ins, outs, sems):
        x, y, c = _my_place()
        chip = 2 * x + y
        for b in range(n):
            for kk in range(1, N_CHIPS):
                px, py = _flip(x, (kk >> 1) & 1), _flip(y, kk & 1)
                k = 3 * b + kk - 1
                theirs, landing = (kk, kk) if relative else (2 * px + py, chip)
                send = pltpu.make_async_remote_copy(
                    src_ref=ins[b].at[theirs], dst_ref=outs[b].at[landing],
                    send_sem=sems[0].at[k], recv_sem=sems[1].at[k], device_id=(px, py, c), device_id_type=MESH)
                slot = outs[b].at[theirs]
                recv = pltpu.make_async_remote_copy(
                    src_ref=slot, dst_ref=slot, send_sem=sems[0].at[k], recv_sem=sems[1].at[k],
                    device_id=(px, py, c), device_id_type=MESH)
                yield send, recv

    def start(ins, outs, sems):
        for send, _ in copies(ins, outs, sems):
            send.start()

    def finish(ins, outs, sems):
        for send, recv in copies(ins, outs, sems):
            recv.wait_recv()
            send.wait_send()

    return _Exchange(parts, [_sds(p.shape, p.dtype) for p in parts], {},
                     [pltpu.SemaphoreType.DMA((3 * n,))] * 2, start, finish)


def _sum_chips(place, part, land, relative=False):
    nchip, ng, hr, cols = land.shape
    tr = _row_tile(hr, cols)
    steps = hr // tr

    def body(place_ref, p_ref, l1, l2, l3, o_ref):
        del place_ref
        o_ref[...] = ((p_ref[...].astype(F32) + l1[...].astype(F32)) + l2[...].astype(F32)) + l3[...].astype(F32)

    def slot(k):
        if relative:
            return pl.BlockSpec((None, None, tr, cols), lambda a, i, pr: (k, a, i, 0))
        return pl.BlockSpec((None, None, tr, cols), lambda a, i, pr: (jnp.bitwise_xor(pr[1], k), a, i, 0))

    return pl.pallas_call(
        body, name="sum_chips",
        grid_spec=pltpu.PrefetchScalarGridSpec(
            num_scalar_prefetch=1, grid=(ng, steps),
            in_specs=[slot(0), slot(1), slot(2), slot(3)],
            out_specs=pl.BlockSpec((None, tr, cols), lambda a, i, pr: (a, pr[0] * steps + i, 0))),
        out_shape=_sds((ng, 2 * hr, cols), F32),
        compiler_params=_cp(2))(place, part, land, land, land)


def _rs_final(bufs):
    n = len(bufs)

    def copy(outs, sems, b, which):
        x, y, c = _my_place()
        hr = bufs[b].shape[1] // 2
        rows = outs[b].at[:, pl.ds(pl.multiple_of((c if which == 0 else 1 - c) * hr, 8), hr), :]
        return pltpu.make_async_remote_copy(
            src_ref=rows, dst_ref=rows, send_sem=sems[0].at[b], recv_sem=sems[1].at[b],
            device_id=(x, y, 1 - c), device_id_type=MESH)

    def start(ins, outs, sems):
        for b in range(n):
            copy(outs, sems, b, 0).start()

    def finish(ins, outs, sems):
        for b in range(n):
            copy(outs, sems, b, 0).wait_send()
            copy(outs, sems, b, 1).wait_recv()

    return _Exchange(bufs, [_sds(h.shape, F32) for h in bufs], {i: i for i in range(n)},
                     [pltpu.SemaphoreType.DMA((n,))] * 2, start, finish)


def _small_sync(smalls, dmod_blk, c_all, carry=None):
    d = c_all.shape[-1]
    cols = dmod_blk.shape[-1]
    chunk = 384

    def body(sm_ref, dm_ref, c_ref, sum_ref, gw_ref, sm_all, dm_all, ssem, rsem):
        x, y, c = _my_place()
        dev = 4 * x + 2 * y + c
        chip = 2 * x + y
        sm_all[dev] = sm_ref[...]
        dm_all[dev] = dm_ref[chip]
        sends = []
        for k in range(1, N_DEV):
            px, py, pc = _flip(x, (k >> 2) & 1), _flip(y, (k >> 1) & 1), _flip(c, k & 1)
            a = pltpu.make_async_remote_copy(src_ref=sm_ref, dst_ref=sm_all.at[dev], send_sem=ssem.at[2 * (k - 1)],
                                             recv_sem=rsem.at[2 * (k - 1)], device_id=(px, py, pc),
                                             device_id_type=MESH)
            b = pltpu.make_async_remote_copy(src_ref=dm_ref.at[2 * px + py], dst_ref=dm_all.at[dev],
                                             send_sem=ssem.at[2 * (k - 1) + 1], recv_sem=rsem.at[2 * (k - 1) + 1],
                                             device_id=(px, py, pc), device_id_type=MESH)
            a.start()
            b.start()
            sends += [a, b]
        for k in range(1, N_DEV):
            px, py, pc = _flip(x, (k >> 2) & 1), _flip(y, (k >> 1) & 1), _flip(c, k & 1)
            pdev = 4 * px + 2 * py + pc
            pltpu.make_async_remote_copy(src_ref=sm_ref, dst_ref=sm_all.at[pdev], send_sem=ssem.at[2 * (k - 1)],
                                         recv_sem=rsem.at[2 * (k - 1)], device_id=(px, py, pc),
                                         device_id_type=MESH).wait_recv()
            pltpu.make_async_remote_copy(src_ref=dm_ref.at[chip], dst_ref=dm_all.at[pdev],
                                         send_sem=ssem.at[2 * (k - 1) + 1], recv_sem=rsem.at[2 * (k - 1) + 1],
                                         device_id=(px, py, pc), device_id_type=MESH).wait_recv()
        for cp in sends:
            cp.wait_send()

        tot = sm_all[0]
        for q in range(1, N_DEV):
            tot = tot + sm_all[q]
        sum_ref[...] = tot

        cs = c_ref[...].reshape(N_DEV * 8, d)
        sc = (cs * jax.nn.sigmoid(cs)).astype(BF)
        for n0 in range(0, cols, chunk):
            dmv = dm_all[:, :, n0:n0 + chunk].reshape(N_DEV * 8, chunk).astype(BF)
            gw_ref[:, n0:n0 + chunk] = _tn(sc, dmv)

    return _whole_call(
        body, "small_sync", (smalls, dmod_blk, c_all),
        [_sds(smalls.shape, F32), _sds((d, cols), F32)],
        [pltpu.VMEM((N_DEV,) + smalls.shape, F32), pltpu.VMEM((N_DEV, 8, cols), F32),
         pltpu.SemaphoreType.DMA((2 * (N_DEV - 1),)), pltpu.SemaphoreType.DMA((2 * (N_DEV - 1),))], carry=carry)


def _bucket_onehot():
    maps = np.stack([_bucket_map(dil).reshape(-1) for _, dil in DIL_CONFIGS])
    return (jnp.asarray(maps)[:, None, :] == jnp.arange(N_BUCKETS, dtype=jnp.int32)[None, :, None]).astype(BF)


def _dil_bias(rel_t, onehot):
    def body(r_ref, oh_ref, o_ref):
        rv = r_ref[...]
        hi = rv.astype(BF)
        lo = (rv - hi.astype(F32)).astype(BF)
        for c in range(len(DIL_CONFIGS)):
            o_ref[c] = _nn(hi, oh_ref[c]) + _nn(lo, oh_ref[c])

    return pl.pallas_call(body, name="dil_bias",
                          out_shape=_sds((len(DIL_CONFIGS), N_HEADS, BLOCK * 2 * BLOCK), F32),
                          compiler_params=_cp())(rel_t, onehot)


def _rowsum8(a):
    def body(a_ref, o_ref):
        o_ref[...] = jnp.sum(a_ref[...], axis=0, keepdims=True)

    return pl.pallas_call(body, name="rowsum8", out_shape=_sds((1, a.shape[1]), F32), compiler_params=_cp())(a)


def _local_step(x, mod, target, w, gains, rel_bias, place=None):
    nb, seq, d = x.shape
    t = nb * seq
    dist = place is not None
    core = place[0:1] if dist else None
    x0 = x.reshape(t, d)
    tgt = target.reshape(t, d)
    md = [mod[:, i:i + 1, :] for i in range(N_MOD)]
    sh1, sc1, gt1, sh2, sc2, gt2, sh3, sc3, gt3 = md
    g1, g2, g3 = gains["g_ffn1"], gains["g_mix"], gains["g_ffn2"]
    ones_g = _group_ones()

    def partial_sums(grads, lands):
        return [_add_halves(core, g, l) for g, l in zip(grads, lands)]

    def chip_sums(parts, lands):
        return [_sum_chips(place, p, l, relative=True) for p, l in zip(parts, lands)]

    gu1 = w["gu1"]
    res = _ffn_up(x0, g1, sc1, sh1, gu1, seq,
                  carry=_join([_ag_weights([w["d1"]], relative=True), _ag_weights([w["win"], w["wout"]])])
                  if dist else None)
    h1, a1, u1, s1 = res[:4]
    wd1, w_in, w_out = res[4:] if dist else (w["d1"], w["win"], w["wout"])
    w_out2 = w_out.reshape(2 * D_GRP, d)
    f1, x1 = _ffn_down(s1, wd1, x0, gt1, seq, 0.5)

    h2, qkv6, qkv_r4, qkv_r16 = _qkv_proj(x1, g2, sc2, sh2, w_in, seq)
    qkv6b = qkv6.reshape(6, nb, seq, D_GRP)
    res = _sb_fwd(qkv6b, gains["g_sb_out"], nb, seq,
                  carry=_ag_weights([w["gu2"], w["d2"]], relative=True) if dist else None)
    o_sb, on_sb = res[:2]
    wgu2, wd2 = res[2:] if dist else (w["gu2"], w["d2"])
    onehot = _bucket_onehot()
    bias = _dil_bias(rel_bias.T, onehot).reshape(len(DIL_CONFIGS), N_HEADS * BLOCK, 2 * BLOCK)
    o_cs, l_cs = [], []
    qkv_rs = [(qkv6b, 3), (qkv_r4, 0), (qkv_r16, 0)]
    for ci, (_, dil) in enumerate(DIL_CONFIGS):
        sub = seq // dil
        arr, base = qkv_rs[ci]
        arr = arr.reshape(base + 3, nb, sub, dil * D_GRP)
        qkv_rs[ci] = (arr, base)
        o_c, l_c = _dil_fwd(arr, base, bias[ci], nb, sub, dil)
        o_cs.append(o_c.reshape(t // dil, dil * D_GRP))
        l_cs.append(l_c.reshape(t // dil, dil * D_GRP))
    o_dil, on_dil = _dil_comb(o_cs, l_cs, gains["g_dil_out"])
    tmix, x2 = _mix_out(on_sb.reshape(t, D_GRP), on_dil, w_out2, x1, gt2, seq)

    h3, a3, u3, s3 = _ffn_up(x2, g3, sc3, sh3, wgu2, seq)
    f3, dx3, dg_final, loss = _ffn_down_loss(s3, wd2, x2, gt3, seq, 0.5, gains["g_final"], tgt)

    da3, du3, df3, dgt3, dx2, dsh3, dsc3, dg3 = _ffn_bwd_x(dx3, gt3, f3, wd2, a3, u3, wgu2, x2, g3, sc3, seq, 0.5)
    grads2 = [_ffn_bwd_w(h3, da3, du3, s3, df3)]

    res = _mix_bwd_out(
        dx2, gt2, tmix, w_out2, o_sb.reshape(t, D_GRP), o_dil, on_sb.reshape(t, D_GRP), on_dil,
        gains["g_sb_out"], gains["g_dil_out"], ones_g, seq, carry=_rs_d2d(grads2) if dist else None)
    do_sb, do_dil, dgt2, dg_sb, dg_dil, dw_out = res[:6]
    parts2 = partial_sums(grads2, res[6:]) if dist else None
    dw_out = dw_out.reshape(N_CHIPS, 1, 2 * D_GRP // N_CHIPS, d)
    res = _sb_bwd(qkv6b, do_sb.reshape(nb, seq, D_GRP), nb, seq,
                  carry=_rs_ici(parts2, relative=True) if dist else None)
    dqkv6 = res[0]
    halves2 = chip_sums(parts2, res[1:]) if dist else None
    dcs = _dil_comb_bwd(do_dil, o_cs, l_cs)
    dsum, a_tiles = [], []
    for ci, (_, dil) in enumerate(DIL_CONFIGS):
        sub = seq // dil
        do_c = dcs[ci].reshape(nb, sub, dil * D_GRP)
        dd_c = dcs[3 + ci].reshape(nb, sub, dil * D_GRP)
        res = _dil_bwd(qkv_rs[ci][0], qkv_rs[ci][1], bias[ci], do_c, dd_c, nb, sub, dil)
        dsum.append(res[0].reshape(3, t // dil, dil * D_GRP))
        a_tiles.append(res[1].reshape(N_HEADS, BLOCK * 2 * BLOCK))
    dqkv6 = _dqkv_dil_sum(dsum, dqkv6.reshape(6, t, D_GRP))
    drel = _relbias_grad(jnp.stack(a_tiles), onehot)
    dx1, dsh2, dsc2, dg2 = _mix_bwd_dh(dqkv6, w_in, x1, g2, sc2, dx2, seq)

    da1, du1, df1, dgt1 = _ffn_bwd_ds(dx1, gt1, f1, wd1, a1, u1, seq, 0.5)
    grads1 = [_ffn_bwd_w(h1, da1, du1, s1, df1)]
    res = _dw_in(h2, dqkv6, carry=_join([_rs_d2d(grads1), _rs_final(halves2)]) if dist else None)
    grads_m = [res[0], dw_out]
    parts1 = partial_sums(grads1, res[1:2]) if dist else None
    if dist:
        grads2 = res[2:3]
    res = _ffn_bwd_dh(da1, du1, gu1, x0, g1, sc1, dx1, seq,
                      carry=_join([_rs_ici(parts1, relative=True), _rs_d2d(grads_m)]) if dist else None)
    dx0, dsh1, dsc1, dg1 = res[:4]
    pending = None
    if dist:
        pending = (chip_sums(parts1, res[4:5]), partial_sums(grads_m, res[5:7]))

    dmod = jnp.concatenate([dsh1, dsc1, dgt1, dsh2, dsc2, dgt2, dsh3, dsc3, dgt3], axis=1)
    return dict(grad_x=dx0.reshape(nb, seq, d), loss=loss[0, 0], dmod=dmod.reshape(nb, N_MOD * d),
                dffn1=grads1[0], dffn2=grads2[0], dwin=grads_m[0], dwout=grads_m[1], pending=pending,
                dg_ffn1=dg1, dg_mix=dg2, dg_ffn2=dg3, dg_final=dg_final, dg_sb=dg_sb, dg_dil=dg_dil,
                drel=drel.T)


_SMALL_ORDER = (("b_ada", N_MOD * 1024), ("g_ffn1", 1024), ("g_mix", 1024), ("g_ffn2", 1024), ("g_final", 1024),
                ("g_sb_out", D_GRP), ("g_dil_out", D_GRP), ("rel_bias", N_BUCKETS * N_HEADS))


def _pack_small(parts, extra=None):
    flat = [parts[name].reshape(-1).astype(F32) for name, _ in _SMALL_ORDER]
    used = sum(sz for _, sz in _SMALL_ORDER)
    pad = SMALL_ROWS * 128 - used
    tail = jnp.zeros((pad,), F32)
    if extra is not None:
        tail = tail.at[0].set(extra)
    return jnp.concatenate(flat + [tail]).reshape(SMALL_ROWS, 128)


def _unpack_small(packed, shapes):
    flat = packed.reshape(-1)
    out, off = {}, 0
    for name, sz in _SMALL_ORDER:
        out[name] = flat[off:off + sz].reshape(shapes[name])
        off += sz
    return out, flat[off]


def kernel(x, c, w_ada, b_ada, g_ffn1, w1_gate, w1_up, w1_down, g_mix, w_in, g_sb_out, g_dil_out, w_out, rel_bias, g_ffn2, w2_gate, w2_up, w2_down, g_final, loss_target, m_w_ada, m_b_ada, m_g_ffn1, m_w1_gate, m_w1_up, m_w1_down, m_g_mix, m_w_in, m_g_sb_out, m_g_dil_out, m_w_out, m_rel_bias, m_g_ffn2, m_w2_gate, m_w2_up, m_w2_down, m_g_final, v_w_ada, v_b_ada, v_g_ffn1, v_w1_gate, v_w1_up, v_w1_down, v_g_mix, v_w_in, v_g_sb_out, v_g_dil_out, v_w_out, v_rel_bias, v_g_ffn2, v_w2_gate, v_w2_up, v_w2_down, v_g_final):
    nb, seq, d = x.shape
    xi, yi, ci = lax.axis_index("x"), lax.axis_index("y"), lax.axis_index("c")
    chip = 2 * xi + yi
    ada_cols = w_ada.shape[-1]

    c_pad = jnp.zeros((8, d), F32).at[:nb].set(c)
    b_shard = lax.dynamic_slice(b_ada, (0, chip * ada_cols), (1, ada_cols))
    shards = dict(gu1=jnp.stack([w1_gate[0], w1_up[0]]), d1=w1_down, win=w_in, wout=w_out,
                  gu2=jnp.stack([w2_gate[0], w2_up[0]]), d2=w2_down)
    bufs = {k: lax.dynamic_update_slice(lax.empty((N_CHIPS,) + s.shape, BF), s.astype(BF)[None],
                                        (chip if k in ("win", "wout") else 0, 0, 0, 0))
            for k, s in shards.items()}
    c_all, mod_blk, bufs["gu1"] = _ada_fwd(c_pad, w_ada[0], b_shard,
                                           carry=_ag_weights([bufs["gu1"]], relative=True))
    mod = jnp.transpose(mod_blk[:, :nb, :], (1, 0, 2)).reshape(nb, N_MOD, d)

    gains = dict(g_ffn1=g_ffn1, g_mix=g_mix, g_ffn2=g_ffn2, g_final=g_final.reshape(1, d),
                 g_sb_out=g_sb_out.reshape(1, D_GRP), g_dil_out=g_dil_out.reshape(1, D_GRP))
    place = jnp.stack([ci, chip]).astype(jnp.int32)
    r = _local_step(x, mod, loss_target, bufs, gains, rel_bias, place)

    dmod = r["dmod"]
    dmod_pad = jnp.zeros((8, N_MOD * d), F32).at[:nb].set(dmod)
    dmod_blk = jnp.transpose(dmod_pad.reshape(8, N_CHIPS, ada_cols), (1, 0, 2))
    small_parts = dict(b_ada=_rowsum8(dmod_pad), g_ffn1=r["dg_ffn1"], g_mix=r["dg_mix"], g_ffn2=r["dg_ffn2"],
                       g_final=r["dg_final"], g_sb_out=r["dg_sb"], g_dil_out=r["dg_dil"], rel_bias=r["drel"])
    halves1, parts_m = r["pending"]
    res = _small_sync(_pack_small(small_parts, r["loss"]), dmod_blk, c_all,
                      carry=_join([_rs_final(halves1), _rs_ici(parts_m)]))
    small_sum, g_wada, gffn1 = res[:3]
    halves_m = [_sum_chips(place, p, l) for p, l in zip(parts_m, res[3:5])]
    gwin, gwout = _alone("rs_last", _rs_final(halves_m))
    gffn2 = r["dffn2"]

    small_w = dict(b_ada=b_ada, g_ffn1=g_ffn1, g_mix=g_mix, g_ffn2=g_ffn2, g_final=g_final,
                   g_sb_out=g_sb_out, g_dil_out=g_dil_out, rel_bias=rel_bias)
    small_m = dict(b_ada=m_b_ada, g_ffn1=m_g_ffn1, g_mix=m_g_mix, g_ffn2=m_g_ffn2, g_final=m_g_final,
                   g_sb_out=m_g_sb_out, g_dil_out=m_g_dil_out, rel_bias=m_rel_bias)
    small_v = dict(b_ada=v_b_ada, g_ffn1=v_g_ffn1, g_mix=v_g_mix, g_ffn2=v_g_ffn2, g_final=v_g_final,
                   g_sb_out=v_g_sb_out, g_dil_out=v_g_dil_out, rel_bias=v_rel_bias)
    shapes = {k: v.shape for k, v in small_w.items()}
    sg, sd, sm, sv = _adamw(_pack_small(small_w), small_sum.reshape(1, SMALL_ROWS, 128), 0,
                            _pack_small(small_m), _pack_small(small_v))
    sg, loss = _unpack_small(sg, shapes)
    sd, _ = _unpack_small(sd, shapes)
    sm, _ = _unpack_small(sm, shapes)
    sv, _ = _unpack_small(sv, shapes)

    big = {}

    def upd(name, w, g_arr, sel, m, v, transposed=False):
        swap = (lambda a: jnp.swapaxes(a, -1, -2)) if transposed else (lambda a: a)
        w2, m2, v2 = [swap(a)[0] for a in (w, m, v)]
        big[name] = [swap(a[None]) for a in _adamw(w2, g_arr, sel, m2, v2)]

    upd("w_ada", w_ada, g_wada.reshape(1, d, ada_cols), 0, m_w_ada, v_w_ada)
    upd("w1_gate", w1_gate, gffn1, 0, m_w1_gate, v_w1_gate, transposed=True)
    upd("w1_up", w1_up, gffn1, 1, m_w1_up, v_w1_up, transposed=True)
    upd("w1_down", w1_down, gffn1, 2, m_w1_down, v_w1_down)
    upd("w_in", w_in, gwin, 0, m_w_in, v_w_in)
    upd("w_out", w_out, gwout, 0, m_w_out, v_w_out)
    upd("w2_gate", w2_gate, gffn2, 0, m_w2_gate, v_w2_gate, transposed=True)
    upd("w2_up", w2_up, gffn2, 1, m_w2_up, v_w2_up, transposed=True)
    upd("w2_down", w2_down, gffn2, 2, m_w2_down, v_w2_down)

    names = ["w_ada", "b_ada", "g_ffn1", "w1_gate", "w1_up", "w1_down", "g_mix", "w_in", "g_sb_out", "g_dil_out",
             "w_out", "rel_bias", "g_ffn2", "w2_gate", "w2_up", "w2_down", "g_final"]
    outs = [loss, r["grad_x"]]
    for k, small in enumerate((sg, sd, sm, sv)):
        for name in names:
            outs.append(big[name][k] if name in big else small[name])
    return tuple(outs)
```

```python
import functools
import math

import numpy as np
import jax
import jax.numpy as jnp
from jax import lax
from jax.experimental import pallas as pl
from jax.experimental.pallas import tpu as pltpu

F32 = jnp.float32
BF = jnp.bfloat16
MESH = pl.DeviceIdType.MESH

HEAD_DIM = 64
N_HEADS = 8
D_GRP = N_HEADS * HEAD_DIM
DIL_CONFIGS = ((128, 1), (512, 4), (2048, 16))
N_STEPS = 128
BLOCK = 128
N_BUCKETS = 32
MAX_DISTANCE = 2048
N_MOD = 9
EPS = 1e-6
NEG_INF = -1e30
SCALE = HEAD_DIM ** -0.5

ADAM_LR = 0.001
ADAM_B1 = 0.9
ADAM_B2 = 0.999
ADAM_EPS = 1e-08
ADAM_WD = 0.01
ADAM_STEP = 10

N_CHIPS = 4
N_DEV = 8
VMEM_LIMIT = 56 * 1024 * 1024
TM = 512
TQ = 256
KB = 256
SMALL_ROWS = 120


def _cp(n_axes=0, **kw):
    sem = ("arbitrary",) * n_axes if n_axes else None
    return pltpu.CompilerParams(dimension_semantics=sem, vmem_limit_bytes=VMEM_LIMIT, **kw)


def _nn(a, b):
    return jnp.dot(a, b, preferred_element_type=F32)


def _nt(a, b):
    return lax.dot_general(a, b, (((1,), (1,)), ((), ())), preferred_element_type=F32)


def _tn(a, b):
    return lax.dot_general(a, b, (((0,), (0,)), ((), ())), preferred_element_type=F32)


def _nn2(x, m):
    hi = x.astype(BF)
    lo = (x - hi.astype(F32)).astype(BF)
    r = _nn(jnp.concatenate([hi, lo], axis=0), m)
    return r[:x.shape[0]] + r[x.shape[0]:]


def _softplus(z):
    return jnp.maximum(z, 0.0) + jnp.log1p(jnp.exp(-jnp.abs(z)))


def _sds(shape, dtype):
    return jax.ShapeDtypeStruct(shape, dtype)


def _whole(a):
    nd = a.ndim
    return pl.BlockSpec(a.shape, lambda *_: (0,) * nd, pipeline_mode=pl.Buffered(1))


def _modnorm_bwd_tile(dh, xv, gv, scv, dxo):
    r = lax.rsqrt(jnp.mean(xv * xv, axis=-1, keepdims=True) + EPS)
    n = xv * r
    ng = n * gv
    dsh = jnp.sum(dh, axis=0, keepdims=True)
    dsc = jnp.sum(dh * ng, axis=0, keepdims=True)
    dy = dh * (1.0 + scv)
    dg = jnp.sum(dy * n, axis=0, keepdims=True)
    dn = dy * gv
    dx = dxo + r * (dn - n * jnp.mean(dn * n, axis=-1, keepdims=True))
    return dx, dsh, dsc, dg


def _acc_rows(ref, val, first):
    @pl.when(first)
    def _():
        ref[...] = val

    @pl.when(jnp.logical_not(first))
    def _():
        ref[...] += val


def _modnorm_tile(x_ref, g_ref, sc_ref, sh_ref):
    xv = x_ref[...]
    r = lax.rsqrt(jnp.mean(xv * xv, axis=-1, keepdims=True) + EPS)
    return (((xv * r) * g_ref[...]) * (1.0 + sc_ref[...]) + sh_ref[...]).astype(BF)


def _ffn_up(x, g, sc, sh, wgu, seq, carry=None):
    t, d = x.shape
    fs = wgu.shape[-1]
    per = seq // TM

    def body(x_ref, g_ref, sc_ref, sh_ref, w_ref, h_ref, p_ref, q_ref, s_ref):
        hv = _modnorm_tile(x_ref, g_ref, sc_ref, sh_ref)
        h_ref[...] = hv
        for j in range(N_CHIPS):
            a = _nn(hv, w_ref[j, 0])
            u = _nn(hv, w_ref[j, 1])
            sig = jax.nn.sigmoid(a)
            q = a * sig
            p_ref[j] = (u * (sig * (1.0 + a * (1.0 - sig)))).astype(BF)
            q_ref[j] = q.astype(BF)
            s_ref[j] = (q * u).astype(BF)

    row = pl.BlockSpec((TM, d), lambda m: (m, 0))
    ex = pl.BlockSpec((None, 1, d), lambda m: (m // per, 0, 0))
    blk = pl.BlockSpec((N_CHIPS, TM, fs), lambda m: (0, m, 0))
    return _call(
        body, "ffn_up", (t // TM,),
        [row, pl.BlockSpec((1, d), lambda m: (0, 0)), ex, ex, _whole(wgu)],
        [row, blk, blk, blk],
        [_sds((t, d), BF)] + [_sds((N_CHIPS, t, fs), BF)] * 3,
        (x, g, sc, sh, wgu), carry=carry)


def _ffn_down(s, wd, x, gt, seq, coef, carry=None):
    _, t, fs = s.shape
    d = x.shape[-1]
    per = seq // TM

    def body(s_ref, w_ref, x_ref, gt_ref, f_ref, xo_ref):
        f = _nn(s_ref[0], w_ref[0, 0])
        for j in range(1, N_CHIPS):
            f = f + _nn(s_ref[j], w_ref[j, 0])
        f_ref[...] = f
        xo_ref[...] = x_ref[...] + (coef * gt_ref[...]) * f

    row = pl.BlockSpec((TM, d), lambda m: (m, 0))
    return _call(
        body, "ffn_down", (t // TM,),
        [pl.BlockSpec((N_CHIPS, TM, fs), lambda m: (0, m, 0)), _whole(wd), row,
         pl.BlockSpec((None, 1, d), lambda m: (m // per, 0, 0))],
        [row, row], [_sds((t, d), F32), _sds((t, d), F32)], (s, wd, x, gt), carry=carry)


def _ffn_bwd_ds(dxo, gt, f, wd, p, q, seq, coef, carry=None):
    t, d = dxo.shape
    fs = p.shape[-1]
    per = seq // TM
    nb = t // seq

    def body(dxo_ref, gt_ref, f_ref, w_ref, p_ref, q_ref, da_ref, du_ref, df_ref, dgt_ref):
        m = pl.program_id(0)
        dxv = dxo_ref[...]
        df = ((coef * gt_ref[...]) * dxv).astype(BF)
        df_ref[...] = df
        _acc_rows(dgt_ref, coef * jnp.sum(dxv * f_ref[...], axis=0, keepdims=True), m % per == 0)
        for j in range(N_CHIPS):
            ds = _nt(df, w_ref[j, 0])
            da_ref[j] = (ds * p_ref[j].astype(F32)).astype(BF)
            du_ref[j] = (ds * q_ref[j].astype(F32)).astype(BF)

    row = pl.BlockSpec((TM, d), lambda m: (m, 0))
    blk = pl.BlockSpec((N_CHIPS, TM, fs), lambda m: (0, m, 0))
    ex = pl.BlockSpec((None, 1, d), lambda m: (m // per, 0, 0))
    return _call(
        body, "ffn_bwd_ds", (t // TM,),
        [row, ex, row, _whole(wd), blk, blk],
        [blk, blk, row, ex],
        [_sds((N_CHIPS, t, fs), BF), _sds((N_CHIPS, t, fs), BF), _sds((t, d), BF), _sds((nb, 1, d), F32)],
        (dxo, gt, f, wd, p, q), carry=carry)


TM_X = 256


def _ffn_bwd_x(dxo, gt, f, wd, p, q, wgu, x, g, sc, seq, coef):
    t, d = dxo.shape
    fs = p.shape[-1]
    per = seq // TM_X
    nb = t // seq

    def body(dxo_ref, gt_ref, f_ref, wd_ref, p_ref, q_ref, w_ref, x_ref, g_ref, sc_ref,
             da_ref, du_ref, df_ref, dgt_ref, dx_ref, dsh_ref, dsc_ref, dg_ref):
        m = pl.program_id(0)
        dxv = dxo_ref[...]
        df = ((coef * gt_ref[...]) * dxv).astype(BF)
        df_ref[...] = df
        _acc_rows(dgt_ref, coef * jnp.sum(dxv * f_ref[...], axis=0, keepdims=True), m % per == 0)
        dh = None
        for j in range(N_CHIPS):
            ds = _nt(df, wd_ref[j, 0])
            da = (ds * p_ref[j].astype(F32)).astype(BF)
            du = (ds * q_ref[j].astype(F32)).astype(BF)
            da_ref[j] = da
            du_ref[j] = du
            part = _nt(da, w_ref[j, 0]) + _nt(du, w_ref[j, 1])
            dh = part if dh is None else dh + part
        dx, dsh, dsc, dg = _modnorm_bwd_tile(dh, x_ref[...], g_ref[...], sc_ref[...], dxv)
        dx_ref[...] = dx
        _acc_rows(dsh_ref, dsh, m % per == 0)
        _acc_rows(dsc_ref, dsc, m % per == 0)
        _acc_rows(dg_ref, dg, m == 0)

    row = pl.BlockSpec((TM_X, d), lambda m: (m, 0))
    blk = pl.BlockSpec((N_CHIPS, TM_X, fs), lambda m: (0, m, 0))
    ex = pl.BlockSpec((None, 1, d), lambda m: (m // per, 0, 0))
    vec = pl.BlockSpec((1, d), lambda m: (0, 0))
    exs = _sds((nb, 1, d), F32)
    return pl.pallas_call(
        body, name="ffn_bwd_x", grid=(t // TM_X,),
        in_specs=[row, ex, row, _whole(wd), blk, blk, _whole(wgu), row, vec, ex],
        out_specs=[blk, blk, row, ex, row, ex, ex, vec],
        out_shape=[_sds((N_CHIPS, t, fs), BF), _sds((N_CHIPS, t, fs), BF), _sds((t, d), BF), exs,
                   _sds((t, d), F32), exs, exs, _sds((1, d), F32)],
        compiler_params=_cp(1))(dxo, gt, f, wd, p, q, wgu, x, g, sc)


TK_W = 1024


def _ffn_bwd_w(h, da, du, s, df):
    t, d = h.shape
    fs = da.shape[-1]

    def body(h_ref, da_ref, du_ref, s_ref, df_ref, o_ref):
        kt = pl.program_id(1)
        hv = h_ref[...]
        parts = (_tn(da_ref[...], hv), _tn(du_ref[...], hv), _tn(s_ref[...], df_ref[...]))

        @pl.when(kt == 0)
        def _():
            for i, p in enumerate(parts):
                o_ref[i] = p

        @pl.when(kt != 0)
        def _():
            for i, p in enumerate(parts):
                o_ref[i] += p

    row = pl.BlockSpec((TK_W, d), lambda j, kt: (kt, 0))
    blk = pl.BlockSpec((None, TK_W, fs), lambda j, kt: (j, kt, 0))
    return pl.pallas_call(
        body, name="ffn_bwd_w", grid=(N_CHIPS, t // TK_W),
        in_specs=[row, blk, blk, blk, row],
        out_specs=pl.BlockSpec((None, 3, fs, d), lambda j, kt: (j, 0, 0, 0)),
        out_shape=_sds((N_CHIPS, 3, fs, d), F32),
        compiler_params=_cp(2))(h, da, du, s, df)


def _ffn_bwd_dh(da, du, wgu, x, g, sc, dxo, seq, carry=None):
    _, t, fs = da.shape
    d = x.shape[-1]
    per = seq // TM
    nb = t // seq

    def body(da_ref, du_ref, w_ref, x_ref, g_ref, sc_ref, dxo_ref, dx_ref, dsh_ref, dsc_ref, dg_ref):
        m = pl.program_id(0)
        dh = _nt(da_ref[0], w_ref[0, 0]) + _nt(du_ref[0], w_ref[0, 1])
        for j in range(1, N_CHIPS):
            dh = dh + _nt(da_ref[j], w_ref[j, 0]) + _nt(du_ref[j], w_ref[j, 1])
        dx, dsh, dsc, dg = _modnorm_bwd_tile(dh, x_ref[...], g_ref[...], sc_ref[...], dxo_ref[...])
        dx_ref[...] = dx
        _acc_rows(dsh_ref, dsh, m % per == 0)
        _acc_rows(dsc_ref, dsc, m % per == 0)
        _acc_rows(dg_ref, dg, m == 0)

    row = pl.BlockSpec((TM, d), lambda m: (m, 0))
    blk = pl.BlockSpec((N_CHIPS, TM, fs), lambda m: (0, m, 0))
    ex = pl.BlockSpec((None, 1, d), lambda m: (m // per, 0, 0))
    vec = pl.BlockSpec((1, d), lambda m: (0, 0))
    return _call(
        body, "ffn_bwd_dh", (t // TM,),
        [blk, blk, _whole(wgu), row, vec, ex, row],
        [row, ex, ex, vec],
        [_sds((t, d), F32), _sds((nb, 1, d), F32), _sds((nb, 1, d), F32), _sds((1, d), F32)],
        (da, du, wgu, x, g, sc, dxo), carry=carry)


def _qkv_proj(x, g, sc, sh, w_in, seq, carry=None):
    t, d = x.shape
    wc = w_in.shape[-1]
    per = seq // TM

    dils = [dil for _, dil in DIL_CONFIGS if dil > 1]

    def body(x_ref, g_ref, sc_ref, sh_ref, w_ref, h_ref, o_ref, *rest):
        res_refs, buf = rest[:len(dils)], rest[len(dils)]
        hv = _modnorm_tile(x_ref, g_ref, sc_ref, sh_ref)
        h_ref[...] = hv
        for j in range(N_CHIPS):
            rf = _nn(hv, w_ref[j, 0])
            r = rf.astype(BF)
            for a, lc, off, width in _col_pieces(j, wc):
                o_ref[a, :, lc:lc + width] = r[:, off:off + width]
                if a < 3:
                    continue
                for c0 in range(0, width, 128):
                    cg = (lc + c0) // 128
                    buf[...] = rf[:, off + c0:off + c0 + 128]
                    for ref, dil in zip(res_refs, dils):
                        for rr in range(dil):
                            ref[a - 3, :, rr * D_GRP + cg * 128:rr * D_GRP + (cg + 1) * 128] = (
                                buf[pl.ds(rr, TM // dil, stride=dil), :].astype(BF))

    row = pl.BlockSpec((TM, d), lambda m: (m, 0))
    ex = pl.BlockSpec((None, 1, d), lambda m: (m // per, 0, 0))
    return _call(
        body, "qkv_proj", (t // TM,),
        [row, pl.BlockSpec((1, d), lambda m: (0, 0)), ex, ex, _whole(w_in)],
        [row, pl.BlockSpec((6, TM, D_GRP), lambda m: (0, m, 0))]
        + [pl.BlockSpec((3, TM // dil, dil * D_GRP), lambda m: (0, m, 0)) for dil in dils],
        [_sds((t, d), BF), _sds((6, t, D_GRP), BF)] + [_sds((3, t // dil, dil * D_GRP), BF) for dil in dils],
        (x, g, sc, sh, w_in), scratch=[pltpu.VMEM((TM, 128), F32)], carry=carry)


def _col_pieces(j, wc):
    out, off = [], 0
    while off < wc:
        a, lc = divmod(j * wc + off, D_GRP)
        width = min(D_GRP - lc, wc - off)
        out.append((a, lc, off, width))
        off += width
    return out


def _chip_cols(g6_ref, j, wc):
    return jnp.concatenate([g6_ref[a, :, lc:lc + width] for a, lc, _, width in _col_pieces(j, wc)], axis=1)


def _mix_out(on_sb, on_dil, w_out, x, gt, seq):
    t, d = x.shape
    per = seq // TM

    def body(a_ref, b_ref, w_ref, x_ref, gt_ref, t_ref, xo_ref):
        tv = _nn(a_ref[...], w_ref[0:D_GRP, :]) + _nn(b_ref[...], w_ref[D_GRP:2 * D_GRP, :])
        t_ref[...] = tv
        xo_ref[...] = x_ref[...] + gt_ref[...] * tv

    row = pl.BlockSpec((TM, d), lambda m: (m, 0))
    half = pl.BlockSpec((TM, D_GRP), lambda m: (m, 0))
    return pl.pallas_call(
        body, name="mix_out", grid=(t // TM,),
        in_specs=[half, half, pl.BlockSpec((2 * D_GRP, d), lambda m: (0, 0)), row,
                  pl.BlockSpec((None, 1, d), lambda m: (m // per, 0, 0))],
        out_specs=[row, row],
        out_shape=[_sds((t, d), F32), _sds((t, d), F32)],
        compiler_params=_cp(1))(on_sb, on_dil, w_out, x, gt)


def _sb_masks():
    lane = lax.broadcasted_iota(jnp.int32, (1, 2 * HEAD_DIM), 1)
    hm0 = lane < HEAD_DIM
    rel = lax.broadcasted_iota(jnp.int32, (TQ, KB), 0) - lax.broadcasted_iota(jnp.int32, (TQ, KB), 1)
    kr = lax.broadcasted_iota(jnp.int32, (KB, KB), 0)
    kc = lax.broadcasted_iota(jnp.int32, (KB, KB), 1)
    return hm0, rel, kr, kc


def _stack_pair(x, hm0):
    zero = jnp.zeros_like(x)
    return jnp.concatenate([jnp.where(hm0, x, zero), jnp.where(hm0, zero, x)], axis=0)


def _headnorm_pair(o, gv, hm0):
    o2 = o * o
    ms0 = jnp.sum(jnp.where(hm0, o2, 0.0), axis=-1, keepdims=True) * (1.0 / HEAD_DIM)
    ms1 = jnp.sum(jnp.where(hm0, 0.0, o2), axis=-1, keepdims=True) * (1.0 / HEAD_DIM)
    r = jnp.where(hm0, lax.rsqrt(ms0 + EPS), lax.rsqrt(ms1 + EPS))
    return (o * r) * gv


SB_DEAD = -104.0


def _alive(c_l):
    return (jnp.max(c_l) > SB_DEAD).astype(jnp.int32)


def _sb_fwd(qkv6, g_sb, nb, seq, carry=None):
    nq = seq // TQ

    def body(q_ref, k_ref, v_ref, g_ref, o_ref, on_ref):
        qi = pl.program_id(2)
        hm0, rel, kr, kc = _sb_masks()
        upper = (kr > kc).astype(BF)
        qs = _stack_pair(q_ref[...], hm0)
        causal2 = jnp.concatenate([rel, rel], axis=0) > 0

        def block(kj, causal, c_l, acc):
            ks = pl.multiple_of(kj * KB, KB)
            z = _nt(qs, k_ref[pl.ds(ks, KB), :]) * SCALE
            sp = _softplus(z)
            ln = -sp if causal is None else jnp.where(causal, -sp, 0.0)
            suf = _nn2(ln, upper)
            w = jnp.exp((z - sp) + (suf + c_l))
            if causal is not None:
                w = jnp.where(causal, w, 0.0)
            return c_l + (suf[:, 0:1] + ln[:, 0:1]), acc + _nn(w.astype(BF), v_ref[pl.ds(ks, KB), :])

        c_l, acc = block(qi, causal2, jnp.zeros((2 * TQ, 1), F32), jnp.zeros((2 * TQ, 2 * HEAD_DIM), F32))

        def cond(carry):
            return jnp.logical_and(carry[0] <= qi, carry[1] > 0)

        def kbody(carry):
            it, _, c_l, acc = carry
            c_l, acc = block(qi - it, None, c_l, acc)
            return it + 1, _alive(c_l), c_l, acc

        acc = lax.while_loop(cond, kbody, (jnp.int32(1), _alive(c_l), c_l, acc))[3]
        o = jnp.where(hm0, acc[:TQ], acc[TQ:])
        o_ref[...] = o
        on_ref[...] = _headnorm_pair(o, g_ref[...], hm0).astype(BF)

    w = 2 * HEAD_DIM
    full = lambda i: pl.BlockSpec((None, None, seq, w), lambda b, hp, q: (i, b, 0, hp))
    qblk = pl.BlockSpec((None, None, TQ, w), lambda b, hp, q: (0, b, q, hp))
    oblk = pl.BlockSpec((None, TQ, w), lambda b, hp, q: (b, q, hp))
    return _call(
        body, "sb_fwd", (nb, N_HEADS // 2, nq),
        [qblk, full(1), full(2), pl.BlockSpec((1, w), lambda b, hp, q: (0, hp))],
        [oblk, oblk],
        [_sds((nb, seq, D_GRP), F32), _sds((nb, seq, D_GRP), BF)],
        (qkv6, qkv6, qkv6, g_sb), carry=carry)


def _sb_bwd(qkv6, do, nb, seq, carry=None):
    nq = seq // TQ
    nk = seq // KB

    def body(q_ref, k_ref, v_ref, do_ref, out_ref, dk_acc, dv_acc, g_st, s_st):
        qi = pl.program_id(2)
        hm0, rel, kr, kc = _sb_masks()
        upper = (kr > kc).astype(BF)
        lower = (kr < kc).astype(BF)

        @pl.when(qi == 0)
        def _():
            dk_acc[...] = jnp.zeros_like(dk_acc)
            dv_acc[...] = jnp.zeros_like(dv_acc)

        qs = _stack_pair(q_ref[...], hm0)
        dos = _stack_pair(do_ref[...], hm0).astype(BF)
        causal2 = jnp.concatenate([rel, rel], axis=0) > 0

        def weights(kj, causal, c_l):
            ks = pl.multiple_of(kj * KB, KB)
            vb = v_ref[pl.ds(ks, KB), :]
            z = _nt(qs, k_ref[pl.ds(ks, KB), :]) * SCALE
            sp = _softplus(z)
            ln = -sp if causal is None else jnp.where(causal, -sp, 0.0)
            suf = _nn2(ln, upper)
            lsz = z - sp
            w = jnp.exp(lsz + (suf + c_l))
            if causal is not None:
                w = jnp.where(causal, w, 0.0)
            g_st[kj] = w * _nt(dos, vb)
            s_st[kj] = jnp.exp(lsz)
            dv_acc[pl.ds(ks, KB), :] += _tn(w.astype(BF), dos)
            return c_l + (suf[:, 0:1] + ln[:, 0:1])

        zc = jnp.zeros((2 * TQ, 1), F32)
        c_l = weights(qi, causal2, zc)

        def acond(carry):
            return jnp.logical_and(carry[0] <= qi, carry[1] > 0)

        def abody(carry):
            c_l = weights(qi - carry[0], None, carry[2])
            return carry[0] + 1, _alive(c_l), c_l

        n_used = lax.while_loop(acond, abody, (jnp.int32(1), _alive(c_l), c_l))[0]

        def grads(kj, causal, c_g, dq):
            ks = pl.multiple_of(kj * KB, KB)
            kb = k_ref[pl.ds(ks, KB), :]
            g = g_st[kj]
            sig = s_st[kj]
            pre = _nn(g.astype(BF), lower)
            dz = g * (1.0 - sig) - sig * (pre + c_g)
            if causal is not None:
                dz = jnp.where(causal, dz, 0.0)
            dzb = (dz * SCALE).astype(BF)
            dk_acc[pl.ds(ks, KB), :] += _tn(dzb, qs)
            return c_g + (pre[:, KB - 1:KB] + g[:, KB - 1:KB]), dq + _nn(dzb, kb)

        c_g, dq = lax.fori_loop(qi - n_used + 1, qi, lambda kj, cr: grads(kj, None, *cr),
                                (zc, jnp.zeros((2 * TQ, 2 * HEAD_DIM), F32)))
        _, dq = grads(qi, causal2, c_g, dq)
        dq = jnp.where(hm0, dq[:TQ], dq[TQ:])
        out_ref[0, pl.ds(pl.multiple_of(qi * TQ, TQ), TQ), :] = dq.astype(BF)

        @pl.when(qi == nq - 1)
        def _():
            out_ref[1] = dk_acc[...].astype(BF)
            out_ref[2] = dv_acc[...].astype(BF)

    w = 2 * HEAD_DIM
    full = lambda i: pl.BlockSpec((None, None, seq, w), lambda b, hp, q: (i, b, 0, hp))
    qblk = pl.BlockSpec((None, None, TQ, w), lambda b, hp, q: (0, b, q, hp))
    oblk = pl.BlockSpec((None, TQ, w), lambda b, hp, q: (b, q, hp))
    return _call(
        body, "sb_bwd", (nb, N_HEADS // 2, nq),
        [qblk, full(1), full(2), oblk],
        [pl.BlockSpec((3, None, seq, w), lambda b, hp, q: (0, b, 0, hp))],
        [_sds((6, nb, seq, D_GRP), BF)], (qkv6, qkv6, qkv6, do),
        scratch=[pltpu.VMEM((seq, w), F32), pltpu.VMEM((seq, w), F32),
                 pltpu.VMEM((nk, 2 * TQ, KB), F32), pltpu.VMEM((nk, 2 * TQ, KB), F32)],
        carry=carry)


def _t5_bucket(n):
    max_exact = N_BUCKETS // 2
    nf = np.maximum(n, 1).astype(np.float32)
    large = max_exact + (np.log(nf / max_exact) / math.log(MAX_DISTANCE / max_exact)
                         * (N_BUCKETS - max_exact)).astype(np.int32)
    large = np.minimum(large, N_BUCKETS - 1)
    return np.where(n < max_exact, n, large).astype(np.int32)


def _bucket_map(dilation):
    step = BLOCK + np.arange(BLOCK)[:, None] - np.arange(2 * BLOCK)[None, :]
    return _t5_bucket(np.clip(step, 0, N_STEPS) * dilation)


GRP_HEADS = 4
GRP_W = GRP_HEADS * HEAD_DIM


def _dil_masks():
    lane = lax.broadcasted_iota(jnp.int32, (1, GRP_W), 1)
    heads = [jnp.logical_and(lane >= HEAD_DIM * i, lane < HEAD_DIM * (i + 1)) for i in range(GRP_HEADS)]
    iq = jnp.bitwise_and(lax.broadcasted_iota(jnp.int32, (GRP_HEADS * BLOCK, BLOCK), 0), BLOCK - 1)
    ik = lax.broadcasted_iota(jnp.int32, (GRP_HEADS * BLOCK, BLOCK), 1)
    return heads, ik <= iq, ik >= iq


def _stack_heads(x, heads):
    zero = jnp.zeros_like(x)
    return jnp.concatenate([jnp.where(hm, x, zero) for hm in heads], axis=0)


def _unstack_heads(xs, heads):
    out = xs[0:BLOCK]
    for i in range(1, GRP_HEADS):
        out = jnp.where(heads[i], xs[i * BLOCK:(i + 1) * BLOCK], out)
    return out


def _dil_rows(n):
    rs = pl.multiple_of(n * BLOCK, BLOCK)
    ps = pl.multiple_of(jnp.maximum(n - 1, 0) * BLOCK, BLOCK)
    return pl.ds(rs, BLOCK), pl.ds(ps, BLOCK)


def _dil_probs(qs, kc, kp, b_ref, gi, valid_c, valid_p):
    rows = slice(gi * GRP_HEADS * BLOCK, (gi + 1) * GRP_HEADS * BLOCK)
    zc = _nt(qs, kc) * SCALE + b_ref[rows, BLOCK:2 * BLOCK]
    zp = _nt(qs, kp) * SCALE + b_ref[rows, 0:BLOCK]
    zc = jnp.where(valid_c, zc, NEG_INF)
    zp = jnp.where(valid_p, zp, NEG_INF)
    m = jnp.maximum(jnp.max(zc, axis=-1, keepdims=True), jnp.max(zp, axis=-1, keepdims=True))
    ec = jnp.exp(zc - m)
    ep = jnp.exp(zp - m)
    den = jnp.sum(ec, axis=-1, keepdims=True) + jnp.sum(ep, axis=-1, keepdims=True)
    return ec, ep, den, m


def _dil_fwd(qkv6r, base, bias, nb, sub_len, dilation):
    n_blk = sub_len // BLOCK

    def body(q_ref, k_ref, v_ref, b_ref, o_ref, l_ref):
        heads, valid_c, valid_p0 = _dil_masks()

        def nbody(n, carry):
            cur, prev = _dil_rows(n)
            valid_p = jnp.logical_and(valid_p0, n > 0)
            for gi in range(N_HEADS // GRP_HEADS):
                lanes = slice(gi * GRP_W, (gi + 1) * GRP_W)
                qs = _stack_heads(q_ref[cur, lanes], heads)
                ec, ep, den, m = _dil_probs(qs, k_ref[cur, lanes], k_ref[prev, lanes], b_ref, gi, valid_c, valid_p)
                o = (_nn(ec.astype(BF), v_ref[cur, lanes]) + _nn(ep.astype(BF), v_ref[prev, lanes])) / den
                o_ref[cur, lanes] = _unstack_heads(o, heads)
                l_ref[cur, lanes] = _unstack_heads(jnp.broadcast_to(m + jnp.log(den), o.shape), heads)
            return carry

        lax.fori_loop(0, n_blk, nbody, 0)

    seqblk = lambda i: pl.BlockSpec((None, None, sub_len, D_GRP), lambda b, r: (i, b, 0, r))
    oblk = pl.BlockSpec((None, sub_len, D_GRP), lambda b, r: (b, 0, r))
    shp = _sds((nb, sub_len, dilation * D_GRP), F32)
    return pl.pallas_call(
        body, name="dil_fwd_%d" % dilation, grid=(nb, dilation),
        in_specs=[seqblk(base), seqblk(base + 1), seqblk(base + 2), _whole(bias)],
        out_specs=[oblk, oblk], out_shape=[shp, shp],
        compiler_params=_cp(2))(qkv6r, qkv6r, qkv6r, bias)


def _dil_bwd(qkv6r, base, bias, do_c, dd_c, nb, sub_len, dilation, carry=None):
    n_blk = sub_len // BLOCK

    def body(q_ref, k_ref, v_ref, b_ref, do_ref, dd_ref, out_ref, a_ref, dk_acc, dv_acc):
        heads, valid_c, valid_p0 = _dil_masks()
        first = jnp.logical_and(pl.program_id(0) == 0, pl.program_id(1) == 0)

        @pl.when(first)
        def _():
            a_ref[...] = jnp.zeros_like(a_ref)

        dk_acc[...] = jnp.zeros_like(dk_acc)
        dv_acc[...] = jnp.zeros_like(dv_acc)

        def nbody(n, carry):
            cur, prev = _dil_rows(n)
            valid_p = jnp.logical_and(valid_p0, n > 0)
            for gi in range(N_HEADS // GRP_HEADS):
                lanes = slice(gi * GRP_W, (gi + 1) * GRP_W)
                kc, kp = k_ref[cur, lanes], k_ref[prev, lanes]
                vc, vp = v_ref[cur, lanes], v_ref[prev, lanes]
                qs = _stack_heads(q_ref[cur, lanes], heads)
                dos = _stack_heads(do_ref[cur, lanes], heads).astype(BF)
                dds = jnp.sum(_stack_heads(dd_ref[cur, lanes], heads), axis=-1, keepdims=True) * (1.0 / HEAD_DIM)
                ec, ep, den, _ = _dil_probs(qs, kc, kp, b_ref, gi, valid_c, valid_p)
                inv = 1.0 / den
                pc = ec * inv
                pp = ep * inv
                dzc = pc * (_nt(dos, vc) + dds)
                dzp = pp * (_nt(dos, vp) + dds)
                rows = slice(gi * GRP_HEADS * BLOCK, (gi + 1) * GRP_HEADS * BLOCK)
                a_ref[rows, BLOCK:2 * BLOCK] += dzc
                a_ref[rows, 0:BLOCK] += dzp
                dzcb = (dzc * SCALE).astype(BF)
                dzpb = (dzp * SCALE).astype(BF)
                out_ref[0, cur, lanes] = _unstack_heads(_nn(dzcb, kc) + _nn(dzpb, kp), heads).astype(BF)
                dk_acc[cur, lanes] += _tn(dzcb, qs)
                dk_acc[prev, lanes] += _tn(dzpb, qs)
                dv_acc[cur, lanes] += _tn(pc.astype(BF), dos)
                dv_acc[prev, lanes] += _tn(pp.astype(BF), dos)
            return carry

        lax.fori_loop(0, n_blk, nbody, 0)
        out_ref[1] = dk_acc[...].astype(BF)
        out_ref[2] = dv_acc[...].astype(BF)

    seqblk = lambda i: pl.BlockSpec((None, None, sub_len, D_GRP), lambda b, r: (i, b, 0, r))
    oblk = pl.BlockSpec((None, sub_len, D_GRP), lambda b, r: (b, 0, r))
    return _call(
        body, "dil_bwd_%d" % dilation, (nb, dilation),
        [seqblk(base), seqblk(base + 1), seqblk(base + 2), _whole(bias), oblk, oblk],
        [pl.BlockSpec((3, None, sub_len, D_GRP), lambda b, r: (0, b, 0, r)),
         pl.BlockSpec((N_HEADS * BLOCK, 2 * BLOCK), lambda b, r: (0, 0))],
        [_sds((3, nb, sub_len, dilation * D_GRP), BF), _sds((N_HEADS * BLOCK, 2 * BLOCK), F32)],
        (qkv6r, qkv6r, qkv6r, bias, do_c, dd_c),
        scratch=[pltpu.VMEM((sub_len, D_GRP), F32)] * 2, carry=carry)


def _group_ones():
    idx = np.arange(D_GRP) // HEAD_DIM
    return jnp.asarray((idx[:, None] == idx[None, :]).astype(np.float32), dtype=BF)


def _dil_alphas(l1, l4, l16):
    mx = jnp.maximum(jnp.maximum(l1, l4), l16)
    e1 = jnp.exp(l1 - mx)
    e4 = jnp.exp(l4 - mx)
    e16 = jnp.exp(l16 - mx)
    den = e1 + e4 + e16
    return e1 / den, e4 / den, e16 / den


def _residue_spec(dil):
    return pl.BlockSpec((TM // dil, dil * D_GRP), lambda m: (m, 0))


def _from_residue(src, dil, cg, buf):
    if dil == 1:
        return src[:, cg * 128:(cg + 1) * 128]
    for r in range(dil):
        buf[pl.ds(r, TM // dil, stride=dil), :] = src[:, r * D_GRP + cg * 128:r * D_GRP + (cg + 1) * 128]
    return buf[...]


def _to_residue(dst, dil, cg, buf, val):
    if dil == 1:
        dst[:, cg * 128:(cg + 1) * 128] = val.astype(dst.dtype)
        return
    buf[...] = val
    for r in range(dil):
        dst[:, r * D_GRP + cg * 128:r * D_GRP + (cg + 1) * 128] = (
            buf[pl.ds(r, TM // dil, stride=dil), :].astype(dst.dtype))


def _pair_sum(x, hm0):
    s0 = jnp.sum(jnp.where(hm0, x, 0.0), axis=-1, keepdims=True)
    s1 = jnp.sum(jnp.where(hm0, 0.0, x), axis=-1, keepdims=True)
    return jnp.where(hm0, s0, s1)


def _dil_comb(os, ls, g_dil):
    t = os[0].shape[0]
    dils = [dil for _, dil in DIL_CONFIGS]

    def body(o1, l1, o4, l4, o16, l16, g_ref, o_ref, on_ref, b0, b1, b2, b3):
        hm0 = lax.broadcasted_iota(jnp.int32, (1, 128), 1) < HEAD_DIM
        for cg in range(D_GRP // 128):
            lanes = slice(cg * 128, (cg + 1) * 128)
            ov = [_from_residue(src, dil, cg, buf) for src, dil, buf in zip((o1, o4, o16), dils, (None, b0, b1))]
            lv = [_from_residue(src, dil, cg, buf) for src, dil, buf in zip((l1, l4, l16), dils, (None, b2, b3))]
            a1, a4, a16 = _dil_alphas(*lv)
            o = a1 * ov[0] + a4 * ov[1] + a16 * ov[2]
            o_ref[:, lanes] = o
            on_ref[:, lanes] = _headnorm_pair(o, g_ref[:, lanes], hm0).astype(BF)

    blk = pl.BlockSpec((TM, D_GRP), lambda m: (m, 0))
    specs = [_residue_spec(dil) for dil in dils for _ in range(2)]
    return pl.pallas_call(
        body, name="dil_comb", grid=(t // TM,),
        in_specs=specs + [pl.BlockSpec((1, D_GRP), lambda m: (0, 0))],
        out_specs=[blk, blk],
        out_shape=[_sds((t, D_GRP), F32), _sds((t, D_GRP), BF)],
        scratch_shapes=[pltpu.VMEM((TM, 128), F32)] * 4,
        compiler_params=_cp(1))(os[0], ls[0], os[1], ls[1], os[2], ls[2], g_dil)


def _dil_comb_bwd(do, os, ls):
    t = do.shape[0]
    dils = [dil for _, dil in DIL_CONFIGS]

    def body(do_ref, o1, l1, o4, l4, o16, l16, d1, d4, d16, e1, e4, e16, b0, b1, b2, b3):
        hm0 = lax.broadcasted_iota(jnp.int32, (1, 128), 1) < HEAD_DIM
        for cg in range(D_GRP // 128):
            dov = do_ref[:, cg * 128:(cg + 1) * 128]
            ov = [_from_residue(src, dil, cg, buf) for src, dil, buf in zip((o1, o4, o16), dils, (None, b0, b1))]
            lv = [_from_residue(src, dil, cg, buf) for src, dil, buf in zip((l1, l4, l16), dils, (None, b2, b3))]
            al = _dil_alphas(*lv)
            sbar = al[0] * _pair_sum(dov * ov[0], hm0)
            for a_c, o_c in zip(al[1:], ov[1:]):
                sbar = sbar + a_c * _pair_sum(dov * o_c, hm0)
            for a_c, dil, dref, eref in zip(al, dils, (d1, d4, d16), (e1, e4, e16)):
                _to_residue(dref, dil, cg, b0, a_c * dov)
                _to_residue(eref, dil, cg, b1, -a_c * sbar)

    specs = [_residue_spec(dil) for dil in dils]
    return pl.pallas_call(
        body, name="dil_comb_bwd", grid=(t // TM,),
        in_specs=[pl.BlockSpec((TM, D_GRP), lambda m: (m, 0))] + [sp for sp in specs for _ in range(2)],
        out_specs=specs + specs,
        out_shape=[_sds((t // dil, dil * D_GRP), BF) for dil in dils]
        + [_sds((t // dil, dil * D_GRP), F32) for dil in dils],
        scratch_shapes=[pltpu.VMEM((TM, 128), F32)] * 4,
        compiler_params=_cp(1))(do, os[0], ls[0], os[1], ls[1], os[2], ls[2])


def _dqkv_dil_sum(ds, dqkv6):
    t = dqkv6.shape[1]
    dils = [dil for _, dil in DIL_CONFIGS]

    def body(*refs):
        srcs, o_ref, acc = refs[:len(dils)], refs[len(dils) + 1], refs[len(dils) + 2]
        for a in range(3):
            for cg in range(D_GRP // 128):
                for src, dil in zip(srcs, dils):
                    for r in range(dil):
                        part = src[a, :, r * D_GRP + cg * 128:r * D_GRP + (cg + 1) * 128].astype(F32)
                        rows = pl.ds(r, TM // dil, stride=dil) if dil > 1 else slice(None)
                        if dil == dils[0]:
                            acc[rows, :] = part
                        else:
                            acc[rows, :] += part
                o_ref[a, :, cg * 128:(cg + 1) * 128] = acc[...].astype(BF)

    return pl.pallas_call(
        body, name="dqkv_dil_sum", grid=(t // TM,),
        in_specs=[pl.BlockSpec((3, TM // dil, dil * D_GRP), lambda m: (0, m, 0)) for dil in dils]
        + [pl.BlockSpec(memory_space=pl.ANY)],
        out_specs=pl.BlockSpec((3, TM, D_GRP), lambda m: (1, m, 0)),
        out_shape=_sds((6, t, D_GRP), BF), input_output_aliases={len(dils): 0},
        scratch_shapes=[pltpu.VMEM((TM, 128), F32)],
        compiler_params=_cp(1))(*ds, dqkv6)


def _relbias_grad(a_all, onehot):
    def body(a_ref, oh_ref, o_ref):
        acc = jnp.zeros((N_HEADS, N_BUCKETS), F32)
        for c in range(len(DIL_CONFIGS)):
            av = a_ref[c]
            hi = av.astype(BF)
            lo = (av - hi.astype(F32)).astype(BF)
            acc = acc + _nt(hi, oh_ref[c]) + _nt(lo, oh_ref[c])
        o_ref[...] = acc

    return pl.pallas_call(body, name="relbias_grad", out_shape=_sds((N_HEADS, N_BUCKETS), F32),
                          compiler_params=_cp())(a_all, onehot)


def _headnorm_bwd(dn, o, gv, mv):
    ms = _nn2(o * o, mv) * (1.0 / HEAD_DIM)
    r = lax.rsqrt(ms + EPS)
    nrm = o * r
    dg = jnp.sum(dn * nrm, axis=0, keepdims=True)
    dnn = dn * gv
    do = r * (dnn - nrm * (_nn2(dnn * nrm, mv) * (1.0 / HEAD_DIM)))
    return do, dg


def _mix_bwd_out(dx, gt, tv, w_out, o_sb, o_dil, on_sb, on_dil, g_sb, g_dil, ones_g, seq, carry=None):
    t, d = dx.shape
    per = seq // TM
    nb = t // seq

    def body(dx_ref, gt_ref, t_ref, w_ref, osb, odl, onsb, ondl, gsb, gdl, m_ref,
             dosb, dodl, dgt_ref, dgsb, dgdl, dw_ref):
        m = pl.program_id(0)
        dxv = dx_ref[...]
        dt = (gt_ref[...] * dxv).astype(BF)
        _acc_rows(dgt_ref, jnp.sum(dxv * t_ref[...], axis=0, keepdims=True), m % per == 0)
        mv = m_ref[...]
        don_sb = _nt(dt, w_ref[0:D_GRP, :])
        don_dl = _nt(dt, w_ref[D_GRP:2 * D_GRP, :])
        do1, dg1 = _headnorm_bwd(don_sb, osb[...], gsb[...], mv)
        do2, dg2 = _headnorm_bwd(don_dl, odl[...], gdl[...], mv)
        dosb[...] = do1
        dodl[...] = do2
        _acc_rows(dgsb, dg1, m == 0)
        _acc_rows(dgdl, dg2, m == 0)
        p1 = _tn(onsb[...], dt)
        p2 = _tn(ondl[...], dt)

        @pl.when(m == 0)
        def _():
            dw_ref[0:D_GRP, :] = p1
            dw_ref[D_GRP:2 * D_GRP, :] = p2

        @pl.when(m != 0)
        def _():
            dw_ref[0:D_GRP, :] += p1
            dw_ref[D_GRP:2 * D_GRP, :] += p2

    row = pl.BlockSpec((TM, d), lambda m: (m, 0))
    half = pl.BlockSpec((TM, D_GRP), lambda m: (m, 0))
    ex = pl.BlockSpec((None, 1, d), lambda m: (m // per, 0, 0))
    gvec = pl.BlockSpec((1, D_GRP), lambda m: (0, 0))
    wblk = pl.BlockSpec((2 * D_GRP, d), lambda m: (0, 0))
    return _call(
        body, "mix_bwd_out", (t // TM,),
        [row, ex, row, wblk, half, half, half, half, gvec, gvec, pl.BlockSpec((D_GRP, D_GRP), lambda m: (0, 0))],
        [half, half, ex, gvec, gvec, wblk],
        [_sds((t, D_GRP), F32), _sds((t, D_GRP), F32), _sds((nb, 1, d), F32),
         _sds((1, D_GRP), F32), _sds((1, D_GRP), F32), _sds((2 * D_GRP, d), F32)],
        (dx, gt, tv, w_out, o_sb, o_dil, on_sb, on_dil, g_sb, g_dil, ones_g), carry=carry)


def _dw_in(h, dqkv6, carry=None):
    t, d = h.shape
    wc = 6 * D_GRP // N_CHIPS

    def body(h_ref, g_ref, o_ref):
        kt = pl.program_id(0)
        hv = h_ref[...]
        for j in range(N_CHIPS):
            p = _tn(hv, _chip_cols(g_ref, j, wc))

            @pl.when(kt == 0)
            def _(p=p, j=j):
                o_ref[j, 0] = p

            @pl.when(kt != 0)
            def _(p=p, j=j):
                o_ref[j, 0] += p

    return _call(
        body, "dw_in", (t // TK_W,),
        [pl.BlockSpec((TK_W, d), lambda kt: (kt, 0)), pl.BlockSpec((6, TK_W, D_GRP), lambda kt: (0, kt, 0))],
        [pl.BlockSpec((N_CHIPS, 1, d, wc), lambda kt: (0, 0, 0, 0))],
        [_sds((N_CHIPS, 1, d, wc), F32)], (h, dqkv6), carry=carry)


def _mix_bwd_dh(dqkv6, w_in, x, g, sc, dxo, seq, carry=None):
    _, t, _ = dqkv6.shape
    d = x.shape[-1]
    wc = w_in.shape[-1]
    per = seq // TM
    nb = t // seq

    def body(g6_ref, w_ref, x_ref, g_ref, sc_ref, dxo_ref, dx_ref, dsh_ref, dsc_ref, dg_ref):
        m = pl.program_id(0)
        dh = _nt(_chip_cols(g6_ref, 0, wc), w_ref[0, 0])
        for j in range(1, N_CHIPS):
            dh = dh + _nt(_chip_cols(g6_ref, j, wc), w_ref[j, 0])
        dx, dsh, dsc, dg = _modnorm_bwd_tile(dh, x_ref[...], g_ref[...], sc_ref[...], dxo_ref[...])
        dx_ref[...] = dx
        _acc_rows(dsh_ref, dsh, m % per == 0)
        _acc_rows(dsc_ref, dsc, m % per == 0)
        _acc_rows(dg_ref, dg, m == 0)

    row = pl.BlockSpec((TM, d), lambda m: (m, 0))
    ex = pl.BlockSpec((None, 1, d), lambda m: (m // per, 0, 0))
    vec = pl.BlockSpec((1, d), lambda m: (0, 0))
    return _call(
        body, "mix_bwd_dh", (t // TM,),
        [pl.BlockSpec((6, TM, D_GRP), lambda m: (0, m, 0)), _whole(w_in), row, vec, ex, row],
        [row, ex, ex, vec],
        [_sds((t, d), F32), _sds((nb, 1, d), F32), _sds((nb, 1, d), F32), _sds((1, d), F32)],
        (dqkv6, w_in, x, g, sc, dxo), carry=carry)


def _ffn_down_loss(s, wd, x, gt, seq, coef, g, target):
    _, t, fs = s.shape
    d = x.shape[-1]
    per = seq // TM
    steps = t // TM

    def body(s_ref, w_ref, x_ref, gt_ref, g_ref, t_ref, f_ref, dx_ref, dg_ref, loss_ref, lacc):
        m = pl.program_id(0)
        f = _nn(s_ref[0], w_ref[0, 0])
        for j in range(1, N_CHIPS):
            f = f + _nn(s_ref[j], w_ref[j, 0])
        f_ref[...] = f
        xv = x_ref[...] + (coef * gt_ref[...]) * f
        gv = g_ref[...]
        r = lax.rsqrt(jnp.mean(xv * xv, axis=-1, keepdims=True) + EPS)
        n = xv * r
        err = n * gv - t_ref[...]
        dy = err * (1.0 / d)
        _acc_rows(dg_ref, jnp.sum(dy * n, axis=0, keepdims=True), m == 0)
        dn = dy * gv
        dx_ref[...] = r * (dn - n * jnp.mean(dn * n, axis=-1, keepdims=True))
        _acc_rows(lacc, jnp.sum(err * err, axis=0, keepdims=True), m == 0)

        @pl.when(m == steps - 1)
        def _():
            tot = jnp.sum(lacc[...], axis=-1, keepdims=True) * (0.5 / d)
            loss_ref[...] = jnp.broadcast_to(tot, (1, 128))

    row = pl.BlockSpec((TM, d), lambda m: (m, 0))
    vec = pl.BlockSpec((1, d), lambda m: (0, 0))
    return pl.pallas_call(
        body, name="ffn_down_loss", grid=(steps,),
        in_specs=[pl.BlockSpec((N_CHIPS, TM, fs), lambda m: (0, m, 0)), _whole(wd), row,
                  pl.BlockSpec((None, 1, d), lambda m: (m // per, 0, 0)), vec, row],
        out_specs=[row, row, vec, pl.BlockSpec((1, 128), lambda m: (0, 0))],
        out_shape=[_sds((t, d), F32), _sds((t, d), F32), _sds((1, d), F32), _sds((1, 128), F32)],
        scratch_shapes=[pltpu.VMEM((1, d), F32)],
        compiler_params=_cp(1))(s, wd, x, gt, g, target)


def _row_tile(rows, cols):
    best = rows
    for tr in range(8, rows + 1, 8):
        if rows % tr == 0 and tr * cols * 4 <= (1 << 20):
            best = tr
    if best * cols * 4 > (1 << 21):
        best = 8
    return best


def _adamw(w, g_arr, g_sel, m, v):
    rows, cols = w.shape
    tr = _row_tile(rows, cols)
    b1c = 1.0 - ADAM_B1 ** ADAM_STEP
    b2c = 1.0 - ADAM_B2 ** ADAM_STEP

    def body(w_ref, g_ref, m_ref, v_ref, go_ref, d_ref, mo_ref, vo_ref):
        gv = g_ref[...]
        mn = ADAM_B1 * m_ref[...] + (1.0 - ADAM_B1) * gv
        vn = ADAM_B2 * v_ref[...] + (1.0 - ADAM_B2) * (gv * gv)
        go_ref[...] = gv
        mo_ref[...] = mn
        vo_ref[...] = vn
        d_ref[...] = -ADAM_LR * ((mn / b1c) / (jnp.sqrt(vn / b2c) + ADAM_EPS) + ADAM_WD * w_ref[...])

    blk = pl.BlockSpec((tr, cols), lambda i: (i, 0))
    shp = _sds((rows, cols), F32)
    return pl.pallas_call(
        body, name="adamw", grid=(rows // tr,),
        in_specs=[blk, pl.BlockSpec((None, tr, cols), lambda i: (g_sel, i, 0)), blk, blk],
        out_specs=[blk] * 4, out_shape=[shp] * 4,
        compiler_params=_cp(1))(w, g_arr, m, v)


def _flip(v, bit):
    return 1 - v if bit else v


def _my_place():
    x, y, c = lax.axis_index("x"), lax.axis_index("y"), lax.axis_index("c")
    return x, y, c


class _Exchange:
    def __init__(self, operands, out_shape, aliases, sems, start, finish, late=None):
        self.operands, self.out_shape, self.aliases, self.sems = list(operands), list(out_shape), dict(aliases), list(sems)
        self.start, self.finish = start, finish
        self.late = late if late is not None else (lambda ins, outs, sems: None)


def _join(exchanges):
    exchanges = [e for e in exchanges if e is not None]
    if not exchanges:
        return None
    ops, outs, sems, aliases, spans = [], [], [], {}, []
    for e in exchanges:
        spans.append((len(ops), len(outs), len(sems), e))
        for i, j in e.aliases.items():
            aliases[len(ops) + i] = len(outs) + j
        ops += e.operands
        outs += e.out_shape
        sems += e.sems

    def run(which):
        def go(ins, res, sm):
            for io, oo, so, e in spans:
                getattr(e, which)(ins[io:io + len(e.operands)], res[oo:oo + len(e.out_shape)], sm[so:so + len(e.sems)])
        return go

    return _Exchange(ops, outs, aliases, sems, run("start"), run("finish"), run("late"))


def _call(body, name, grid, in_specs, out_specs, out_shape, args, scratch=(), carry=None, io_alias=None):
    in_specs, out_specs, out_shape, scratch = list(in_specs), list(out_specs), list(out_shape), list(scratch)
    io_alias = dict(io_alias or {})
    if carry is None:
        return pl.pallas_call(body, name=name, grid=grid, in_specs=in_specs, out_specs=out_specs,
                              out_shape=out_shape, scratch_shapes=scratch, input_output_aliases=io_alias,
                              compiler_params=_cp(len(grid)))(*args)
    n_in, n_out, n_s = len(in_specs), len(out_specs), len(scratch)
    c_in, c_out = len(carry.operands), len(carry.out_shape)
    any_spec = pl.BlockSpec(memory_space=pl.ANY)

    def wrapped(*refs):
        ins, cins = refs[:n_in], refs[n_in:n_in + c_in]
        o0 = n_in + c_in
        outs, couts = refs[o0:o0 + n_out], refs[o0 + n_out:o0 + n_out + c_out]
        s0 = o0 + n_out + c_out
        scr, sems = refs[s0:s0 + n_s], refs[s0 + n_s:]
        first = pl.program_id(0) == 0
        last = pl.program_id(0) == grid[0] - 1
        step = pl.program_id(0)
        for ax in range(1, len(grid)):
            first = jnp.logical_and(first, pl.program_id(ax) == 0)
            last = jnp.logical_and(last, pl.program_id(ax) == grid[ax] - 1)
            step = step * grid[ax] + pl.program_id(ax)

        @pl.when(first)
        def _():
            carry.start(cins, couts, sems)

        total = math.prod(grid)

        @pl.when(step == max(total - max(2, total // 8), 0))
        def _():
            carry.late(cins, couts, sems)

        body(*ins, *outs, *scr)

        @pl.when(last)
        def _():
            carry.finish(cins, couts, sems)

    return pl.pallas_call(
        wrapped, name=name, grid=grid, in_specs=in_specs + [any_spec] * c_in,
        out_specs=out_specs + [any_spec] * c_out, out_shape=out_shape + carry.out_shape,
        scratch_shapes=scratch + carry.sems,
        input_output_aliases={**io_alias, **{n_in + i: n_out + j for i, j in carry.aliases.items()}},
        compiler_params=_cp(len(grid)))(*args, *carry.operands)


def _whole_call(body, name, args, out_shape, scratch, carry=None):
    vm = pl.BlockSpec(memory_space=pltpu.VMEM)
    any_spec = pl.BlockSpec(memory_space=pl.ANY)
    out_shape, scratch = list(out_shape), list(scratch)
    n_in, n_out, n_s = len(args), len(out_shape), len(scratch)
    if carry is None:
        return pl.pallas_call(body, name=name, in_specs=[vm] * n_in, out_specs=[vm] * n_out, out_shape=out_shape,
                              scratch_shapes=scratch, compiler_params=_cp())(*args)
    c_in, c_out = len(carry.operands), len(carry.out_shape)

    def wrapped(*refs):
        ins, cins = refs[:n_in], refs[n_in:n_in + c_in]
        o0 = n_in + c_in
        outs, couts = refs[o0:o0 + n_out], refs[o0 + n_out:o0 + n_out + c_out]
        s0 = o0 + n_out + c_out
        scr, sems = refs[s0:s0 + n_s], refs[s0 + n_s:]
        carry.start(cins, couts, sems)
        body(*ins, *outs, *scr)
        carry.late(cins, couts, sems)
        carry.finish(cins, couts, sems)

    return pl.pallas_call(
        wrapped, name=name, in_specs=[vm] * n_in + [any_spec] * c_in, out_specs=[vm] * n_out + [any_spec] * c_out,
        out_shape=out_shape + carry.out_shape, scratch_shapes=scratch + carry.sems,
        input_output_aliases={n_in + i: n_out + j for i, j in carry.aliases.items()},
        compiler_params=_cp())(*args, *carry.operands)


def _alone(name, ex):
    any_spec = pl.BlockSpec(memory_space=pl.ANY)
    c_in, c_out = len(ex.operands), len(ex.out_shape)

    def body(*refs):
        ins, outs, sems = refs[:c_in], refs[c_in:c_in + c_out], refs[c_in + c_out:]
        ex.start(ins, outs, sems)
        ex.late(ins, outs, sems)
        ex.finish(ins, outs, sems)

    return pl.pallas_call(
        body, name=name, in_specs=[any_spec] * c_in, out_specs=[any_spec] * c_out, out_shape=ex.out_shape,
        scratch_shapes=ex.sems, input_output_aliases=ex.aliases, compiler_params=_cp())(*ex.operands)


def _ada_fwd(c_pad, w_ada, b_shard, carry=None):
    d = c_pad.shape[-1]
    cols = w_ada.shape[-1]
    chunk = 384

    def body(c_ref, w_ref, b_ref, call_ref, mod_ref, part, s1, r1, s2, r2):
        x, y, c = _my_place()
        dev = 4 * x + 2 * y + c
        chip = 2 * x + y
        call_ref[dev] = c_ref[...]

        def c_copy(k):
            px, py, pc = _flip(x, (k >> 2) & 1), _flip(y, (k >> 1) & 1), _flip(c, k & 1)
            return px, py, pc

        sends = []
        for k in range(1, N_DEV):
            px, py, pc = c_copy(k)
            cp = pltpu.make_async_remote_copy(src_ref=c_ref, dst_ref=call_ref.at[dev], send_sem=s1.at[k - 1],
                                              recv_sem=r1.at[k - 1], device_id=(px, py, pc), device_id_type=MESH)
            cp.start()
            sends.append(cp)
        for k in range(1, N_DEV):
            px, py, pc = c_copy(k)
            pltpu.make_async_remote_copy(src_ref=c_ref, dst_ref=call_ref.at[4 * px + 2 * py + pc],
                                         send_sem=s1.at[k - 1], recv_sem=r1.at[k - 1],
                                         device_id=(px, py, pc), device_id_type=MESH).wait_recv()
        for cp in sends:
            cp.wait_send()

        cs = call_ref[...].reshape(N_DEV * 8, d)
        sc = (cs * jax.nn.sigmoid(cs)).astype(BF)
        for n0 in range(0, cols, chunk):
            blk = _nn(sc, w_ref[:, n0:n0 + chunk].astype(BF)) + b_ref[:, n0:n0 + chunk]
            part[:, :, n0:n0 + chunk] = blk.reshape(N_DEV, 8, chunk)

        mod_ref[chip] = part[dev]
        sends = []
        for kk in range(1, N_CHIPS):
            px, py = _flip(x, (kk >> 1) & 1), _flip(y, kk & 1)
            cp = pltpu.make_async_remote_copy(src_ref=part.at[4 * px + 2 * py + c], dst_ref=mod_ref.at[chip],
                                              send_sem=s2.at[kk - 1], recv_sem=r2.at[kk - 1],
                                              device_id=(px, py, c), device_id_type=MESH)
            cp.start()
            sends.append(cp)
        for kk in range(1, N_CHIPS):
            px, py = _flip(x, (kk >> 1) & 1), _flip(y, kk & 1)
            pltpu.make_async_remote_copy(src_ref=part.at[dev], dst_ref=mod_ref.at[2 * px + py],
                                         send_sem=s2.at[kk - 1], recv_sem=r2.at[kk - 1],
                                         device_id=(px, py, c), device_id_type=MESH).wait_recv()
        for cp in sends:
            cp.wait_send()

    return _whole_call(
        body, "ada_fwd", (c_pad, w_ada, b_shard),
        [_sds((N_DEV, 8, d), F32), _sds((N_CHIPS, 8, cols), F32)],
        [pltpu.VMEM((N_DEV, 8, cols), F32),
         pltpu.SemaphoreType.DMA((N_DEV - 1,)), pltpu.SemaphoreType.DMA((N_DEV - 1,)),
         pltpu.SemaphoreType.DMA((N_CHIPS - 1,)), pltpu.SemaphoreType.DMA((N_CHIPS - 1,))], carry=carry)


def _ag_weights(bufs, kks=(1, 2, 3), relative=False):
    n, nk = len(bufs), len(kks)

    def half(b, which):
        hr = bufs[b].shape[2] // 2
        return pl.ds(pl.multiple_of(which * hr, 16), hr)

    def copies(outs, sems, b, i, kk):
        x, y, c = _my_place()
        chip = 2 * x + y
        px, py = _flip(x, (kk >> 1) & 1), _flip(y, kk & 1)
        mine, theirs = (0, kk) if relative else (chip, 2 * px + py)
        landing = kk if relative else chip
        k = nk * b + i
        send = pltpu.make_async_remote_copy(
            src_ref=outs[b].at[mine, :, half(b, c), :], dst_ref=outs[b].at[landing, :, half(b, c), :],
            send_sem=sems[0].at[k], recv_sem=sems[1].at[k], device_id=(px, py, c), device_id_type=MESH)
        got = outs[b].at[theirs, :, half(b, c), :]
        recv = pltpu.make_async_remote_copy(
            src_ref=got, dst_ref=got, send_sem=sems[0].at[k], recv_sem=sems[1].at[k],
            device_id=(px, py, c), device_id_type=MESH)
        fwd = pltpu.make_async_remote_copy(
            src_ref=got, dst_ref=got, send_sem=sems[2].at[k], recv_sem=sems[3].at[k],
            device_id=(x, y, 1 - c), device_id_type=MESH)
        other = outs[b].at[theirs, :, half(b, 1 - c), :]
        back = pltpu.make_async_remote_copy(
            src_ref=other, dst_ref=other, send_sem=sems[2].at[k], recv_sem=sems[3].at[k],
            device_id=(x, y, 1 - c), device_id_type=MESH)
        return send, recv, fwd, back

    def each(outs, sems):
        for b in range(n):
            for i, kk in enumerate(kks):
                yield copies(outs, sems, b, i, kk)

    def start(ins, outs, sems):
        for send, _, _, _ in each(outs, sems):
            send.start()

    def late(ins, outs, sems):
        for _, recv, fwd, _ in each(outs, sems):
            recv.wait_recv()
            fwd.start()

    def finish(ins, outs, sems):
        for send, _, fwd, back in each(outs, sems):
            back.wait_recv()
            send.wait_send()
            fwd.wait_send()

    return _Exchange(bufs, [_sds(s.shape, s.dtype) for s in bufs], {i: i for i in range(n)},
                     [pltpu.SemaphoreType.DMA((nk * n,))] * 4, start, finish, late)


def _rs_d2d(grads):
    n = len(grads)

    def copy(ins, outs, sems, b):
        x, y, c = _my_place()
        hr = grads[b].shape[2] // 2
        theirs = pl.ds(pl.multiple_of((1 - c) * hr, 8), hr)
        return pltpu.make_async_remote_copy(
            src_ref=ins[b].at[:, :, theirs, :], dst_ref=outs[b], send_sem=sems[0].at[b], recv_sem=sems[1].at[b],
            device_id=(x, y, 1 - c), device_id_type=MESH)

    def start(ins, outs, sems):
        for b in range(n):
            copy(ins, outs, sems, b).start()

    def finish(ins, outs, sems):
        for b in range(n):
            copy(ins, outs, sems, b).wait()

    return _Exchange(grads, [_sds(g.shape[:2] + (g.shape[2] // 2, g.shape[3]), F32) for g in grads], {},
                     [pltpu.SemaphoreType.DMA((n,))] * 2, start, finish)


def _add_halves(core, g, land):
    nchip, ng, rows, cols = g.shape
    hr = rows // 2
    tr = _row_tile(hr, cols)
    steps = hr // tr

    def body(core_ref, g_ref, l_ref, o_ref):
        del core_ref
        o_ref[...] = (g_ref[...] + l_ref[...]).astype(BF)

    return pl.pallas_call(
        body, name="add_halves",
        grid_spec=pltpu.PrefetchScalarGridSpec(
            num_scalar_prefetch=1, grid=(nchip, ng, steps),
            in_specs=[pl.BlockSpec((None, None, tr, cols), lambda j, a, i, cr: (j, a, cr[0] * steps + i, 0)),
                      pl.BlockSpec((None, None, tr, cols), lambda j, a, i, cr: (j, a, i, 0))],
            out_specs=pl.BlockSpec((None, None, tr, cols), lambda j, a, i, cr: (j, a, i, 0))),
        out_shape=_sds((nchip, ng, hr, cols), BF),
        compiler_params=_cp(3))(core, g, land)


def _rs_ici(parts, relative=False):
    n = len(parts)

    def copies(ins, outs, sems):
        x, y, c = _my_place()
        chip = 2 * x + y
        for b in range(n):
            for kk in range(1, N_CHIPS):
                px, py = _flip(x, (kk >> 1) & 1), _flip(y, kk & 1)
                k = 3 * b + kk - 1
                theirs, landing = (kk, kk) if relative else (2 * px + py, chip)
                send = pltpu.make_async_remote_copy(
                    src_ref=ins[b].at[theirs], dst_ref=outs[b].at[landing],
                    send_sem=sems[0].at[k], recv_sem=sems[1].at[k], device_id=(px, py, c), device_id_type=MESH)
                slot = outs[b].at[theirs]
                recv = pltpu.make_async_remote_copy(
                    src_ref=slot, dst_ref=slot, send_sem=sems[0].at[k], recv_sem=sems[1].at[k],
                    device_id=(px, py, c), device_id_type=MESH)
                yield send, recv

    def start(ins, outs, sems):
        for send, _ in copies(ins, outs, sems):
            send.start()

    def finish(ins, outs, sems):
        for send, recv in copies(ins, outs, sems):
            recv.wait_recv()
            send.wait_send()

    return _Exchange(parts, [_sds(p.shape, p.dtype) for p in parts], {},
                     [pltpu.SemaphoreType.DMA((3 * n,))] * 2, start, finish)


def _sum_chips(place, part, land, relative=False):
    nchip, ng, hr, cols = land.shape
    tr = _row_tile(hr, cols)
    steps = hr // tr

    def body(place_ref, p_ref, l1, l2, l3, o_ref):
        del place_ref
        o_ref[...] = ((p_ref[...].astype(F32) + l1[...].astype(F32)) + l2[...].astype(F32)) + l3[...].astype(F32)

    def slot(k):
        if relative:
            return pl.BlockSpec((None, None, tr, cols), lambda a, i, pr: (k, a, i, 0))
        return pl.BlockSpec((None, None, tr, cols), lambda a, i, pr: (jnp.bitwise_xor(pr[1], k), a, i, 0))

    return pl.pallas_call(
        body, name="sum_chips",
        grid_spec=pltpu.PrefetchScalarGridSpec(
            num_scalar_prefetch=1, grid=(ng, steps),
            in_specs=[slot(0), slot(1), slot(2), slot(3)],
            out_specs=pl.BlockSpec((None, tr, cols), lambda a, i, pr: (a, pr[0] * steps + i, 0))),
        out_shape=_sds((ng, 2 * hr, cols), F32),
        compiler_params=_cp(2))(place, part, land, land, land)


def _rs_final(bufs):
    n = len(bufs)

    def copy(outs, sems, b, which):
        x, y, c = _my_place()
        hr = bufs[b].shape[1] // 2
        rows = outs[b].at[:, pl.ds(pl.multiple_of((c if which == 0 else 1 - c) * hr, 8), hr), :]
        return pltpu.make_async_remote_copy(
            src_ref=rows, dst_ref=rows, send_sem=sems[0].at[b], recv_sem=sems[1].at[b],
            device_id=(x, y, 1 - c), device_id_type=MESH)

    def start(ins, outs, sems):
        for b in range(n):
            copy(outs, sems, b, 0).start()

    def finish(ins, outs, sems):
        for b in range(n):
            copy(outs, sems, b, 0).wait_send()
            copy(outs, sems, b, 1).wait_recv()

    return _Exchange(bufs, [_sds(h.shape, F32) for h in bufs], {i: i for i in range(n)},
                     [pltpu.SemaphoreType.DMA((n,))] * 2, start, finish)


def _small_sync(smalls, dmod_blk, c_all, carry=None):
    d = c_all.shape[-1]
    cols = dmod_blk.shape[-1]
    chunk = 384

    def body(sm_ref, dm_ref, c_ref, sum_ref, gw_ref, sm_all, dm_all, ssem, rsem):
        x, y, c = _my_place()
        dev = 4 * x + 2 * y + c
        chip = 2 * x + y
        sm_all[dev] = sm_ref[...]
        dm_all[dev] = dm_ref[chip]
        sends = []
        for k in range(1, N_DEV):
            px, py, pc = _flip(x, (k >> 2) & 1), _flip(y, (k >> 1) & 1), _flip(c, k & 1)
            a = pltpu.make_async_remote_copy(src_ref=sm_ref, dst_ref=sm_all.at[dev], send_sem=ssem.at[2 * (k - 1)],
                                             recv_sem=rsem.at[2 * (k - 1)], device_id=(px, py, pc),
                                             device_id_type=MESH)
            b = pltpu.make_async_remote_copy(src_ref=dm_ref.at[2 * px + py], dst_ref=dm_all.at[dev],
                                             send_sem=ssem.at[2 * (k - 1) + 1], recv_sem=rsem.at[2 * (k - 1) + 1],
                                             device_id=(px, py, pc), device_id_type=MESH)
            a.start()
            b.start()
            sends += [a, b]
        for k in range(1, N_DEV):
            px, py, pc = _flip(x, (k >> 2) & 1), _flip(y, (k >> 1) & 1), _flip(c, k & 1)
            pdev = 4 * px + 2 * py + pc
            pltpu.make_async_remote_copy(src_ref=sm_ref, dst_ref=sm_all.at[pdev], send_sem=ssem.at[2 * (k - 1)],
                                         recv_sem=rsem.at[2 * (k - 1)], device_id=(px, py, pc),
                                         device_id_type=MESH).wait_recv()
            pltpu.make_async_remote_copy(src_ref=dm_ref.at[chip], dst_ref=dm_all.at[pdev],
                                         send_sem=ssem.at[2 * (k - 1) + 1], recv_sem=rsem.at[2 * (k - 1) + 1],
                                         device_id=(px, py, pc), device_id_type=MESH).wait_recv()
        for cp in sends:
            cp.wait_send()

        tot = sm_all[0]
        for q in range(1, N_DEV):
            tot = tot + sm_all[q]
        sum_ref[...] = tot

        cs = c_ref[...].reshape(N_DEV * 8, d)
        sc = (cs * jax.nn.sigmoid(cs)).astype(BF)
        for n0 in range(0, cols, chunk):
            dmv = dm_all[:, :, n0:n0 + chunk].reshape(N_DEV * 8, chunk).astype(BF)
            gw_ref[:, n0:n0 + chunk] = _tn(sc, dmv)

    return _whole_call(
        body, "small_sync", (smalls, dmod_blk, c_all),
        [_sds(smalls.shape, F32), _sds((d, cols), F32)],
        [pltpu.VMEM((N_DEV,) + smalls.shape, F32), pltpu.VMEM((N_DEV, 8, cols), F32),
         pltpu.SemaphoreType.DMA((2 * (N_DEV - 1),)), pltpu.SemaphoreType.DMA((2 * (N_DEV - 1),))], carry=carry)


def _bucket_onehot():
    maps = np.stack([_bucket_map(dil).reshape(-1) for _, dil in DIL_CONFIGS])
    return (jnp.asarray(maps)[:, None, :] == jnp.arange(N_BUCKETS, dtype=jnp.int32)[None, :, None]).astype(BF)


def _dil_bias(rel_t, onehot):
    def body(r_ref, oh_ref, o_ref):
        rv = r_ref[...]
        hi = rv.astype(BF)
        lo = (rv - hi.astype(F32)).astype(BF)
        for c in range(len(DIL_CONFIGS)):
            o_ref[c] = _nn(hi, oh_ref[c]) + _nn(lo, oh_ref[c])

    return pl.pallas_call(body, name="dil_bias",
                          out_shape=_sds((len(DIL_CONFIGS), N_HEADS, BLOCK * 2 * BLOCK), F32),
                          compiler_params=_cp())(rel_t, onehot)


def _rowsum8(a):
    def body(a_ref, o_ref):
        o_ref[...] = jnp.sum(a_ref[...], axis=0, keepdims=True)

    return pl.pallas_call(body, name="rowsum8", out_shape=_sds((1, a.shape[1]), F32), compiler_params=_cp())(a)


def _local_step(x, mod, target, w, gains, rel_bias, place=None):
    nb, seq, d = x.shape
    t = nb * seq
    dist = place is not None
    core = place[0:1] if dist else None
    x0 = x.reshape(t, d)
    tgt = target.reshape(t, d)
    md = [mod[:, i:i + 1, :] for i in range(N_MOD)]
    sh1, sc1, gt1, sh2, sc2, gt2, sh3, sc3, gt3 = md
    g1, g2, g3 = gains["g_ffn1"], gains["g_mix"], gains["g_ffn2"]
    ones_g = _group_ones()

    def partial_sums(grads, lands):
        return [_add_halves(core, g, l) for g, l in zip(grads, lands)]

    def chip_sums(parts, lands):
        return [_sum_chips(place, p, l, relative=True) for p, l in zip(parts, lands)]

    gu1 = w["gu1"]
    res = _ffn_up(x0, g1, sc1, sh1, gu1, seq,
                  carry=_join([_ag_weights([w["d1"]], relative=True), _ag_weights([w["win"], w["wout"]])])
                  if dist else None)
    h1, a1, u1, s1 = res[:4]
    wd1, w_in, w_out = res[4:] if dist else (w["d1"], w["win"], w["wout"])
    w_out2 = w_out.reshape(2 * D_GRP, d)
    f1, x1 = _ffn_down(s1, wd1, x0, gt1, seq, 0.5)

    h2, qkv6, qkv_r4, qkv_r16 = _qkv_proj(x1, g2, sc2, sh2, w_in, seq)
    qkv6b = qkv6.reshape(6, nb, seq, D_GRP)
    res = _sb_fwd(qkv6b, gains["g_sb_out"], nb, seq,
                  carry=_ag_weights([w["gu2"], w["d2"]], relative=True) if dist else None)
    o_sb, on_sb = res[:2]
    wgu2, wd2 = res[2:] if dist else (w["gu2"], w["d2"])
    onehot = _bucket_onehot()
    bias = _dil_bias(rel_bias.T, onehot).reshape(len(DIL_CONFIGS), N_HEADS * BLOCK, 2 * BLOCK)
    o_cs, l_cs = [], []
    qkv_rs = [(qkv6b, 3), (qkv_r4, 0), (qkv_r16, 0)]
    for ci, (_, dil) in enumerate(DIL_CONFIGS):
        sub = seq // dil
        arr, base = qkv_rs[ci]
        arr = arr.reshape(base + 3, nb, sub, dil * D_GRP)
        qkv_rs[ci] = (arr, base)
        o_c, l_c = _dil_fwd(arr, base, bias[ci], nb, sub, dil)
        o_cs.append(o_c.reshape(t // dil, dil * D_GRP))
        l_cs.append(l_c.reshape(t // dil, dil * D_GRP))
    o_dil, on_dil = _dil_comb(o_cs, l_cs, gains["g_dil_out"])
    tmix, x2 = _mix_out(on_sb.reshape(t, D_GRP), on_dil, w_out2, x1, gt2, seq)

    h3, a3, u3, s3 = _ffn_up(x2, g3, sc3, sh3, wgu2, seq)
    f3, dx3, dg_final, loss = _ffn_down_loss(s3, wd2, x2, gt3, seq, 0.5, gains["g_final"], tgt)

    da3, du3, df3, dgt3, dx2, dsh3, dsc3, dg3 = _ffn_bwd_x(dx3, gt3, f3, wd2, a3, u3, wgu2, x2, g3, sc3, seq, 0.5)
    grads2 = [_ffn_bwd_w(h3, da3, du3, s3, df3)]

    res = _mix_bwd_out(
        dx2, gt2, tmix, w_out2, o_sb.reshape(t, D_GRP), o_dil, on_sb.reshape(t, D_GRP), on_dil,
        gains["g_sb_out"], gains["g_dil_out"], ones_g, seq, carry=_rs_d2d(grads2) if dist else None)
    do_sb, do_dil, dgt2, dg_sb, dg_dil, dw_out = res[:6]
    parts2 = partial_sums(grads2, res[6:]) if dist else None
    dw_out = dw_out.reshape(N_CHIPS, 1, 2 * D_GRP // N_CHIPS, d)
    res = _sb_bwd(qkv6b, do_sb.reshape(nb, seq, D_GRP), nb, seq,
                  carry=_rs_ici(parts2, relative=True) if dist else None)
    dqkv6 = res[0]
    halves2 = chip_sums(parts2, res[1:]) if dist else None
    dcs = _dil_comb_bwd(do_dil, o_cs, l_cs)
    dsum, a_tiles = [], []
    for ci, (_, dil) in enumerate(DIL_CONFIGS):
        sub = seq // dil
        do_c = dcs[ci].reshape(nb, sub, dil * D_GRP)
        dd_c = dcs[3 + ci].reshape(nb, sub, dil * D_GRP)
        res = _dil_bwd(qkv_rs[ci][0], qkv_rs[ci][1], bias[ci], do_c, dd_c, nb, sub, dil)
        dsum.append(res[0].reshape(3, t // dil, dil * D_GRP))
        a_tiles.append(res[1].reshape(N_HEADS, BLOCK * 2 * BLOCK))
    dqkv6 = _dqkv_dil_sum(dsum, dqkv6.reshape(6, t, D_GRP))
    drel = _relbias_grad(jnp.stack(a_tiles), onehot)
    dx1, dsh2, dsc2, dg2 = _mix_bwd_dh(dqkv6, w_in, x1, g2, sc2, dx2, seq)

    da1, du1, df1, dgt1 = _ffn_bwd_ds(dx1, gt1, f1, wd1, a1, u1, seq, 0.5)
    grads1 = [_ffn_bwd_w(h1, da1, du1, s1, df1)]
    res = _dw_in(h2, dqkv6, carry=_join([_rs_d2d(grads1), _rs_final(halves2)]) if dist else None)
    grads_m = [res[0], dw_out]
    parts1 = partial_sums(grads1, res[1:2]) if dist else None
    if dist:
        grads2 = res[2:3]
    res = _ffn_bwd_dh(da1, du1, gu1, x0, g1, sc1, dx1, seq,
                      carry=_join([_rs_ici(parts1, relative=True), _rs_d2d(grads_m)]) if dist else None)
    dx0, dsh1, dsc1, dg1 = res[:4]
    pending = None
    if dist:
        pending = (chip_sums(parts1, res[4:5]), partial_sums(grads_m, res[5:7]))

    dmod = jnp.concatenate([dsh1, dsc1, dgt1, dsh2, dsc2, dgt2, dsh3, dsc3, dgt3], axis=1)
    return dict(grad_x=dx0.reshape(nb, seq, d), loss=loss[0, 0], dmod=dmod.reshape(nb, N_MOD * d),
                dffn1=grads1[0], dffn2=grads2[0], dwin=grads_m[0], dwout=grads_m[1], pending=pending,
                dg_ffn1=dg1, dg_mix=dg2, dg_ffn2=dg3, dg_final=dg_final, dg_sb=dg_sb, dg_dil=dg_dil,
                drel=drel.T)


_SMALL_ORDER = (("b_ada", N_MOD * 1024), ("g_ffn1", 1024), ("g_mix", 1024), ("g_ffn2", 1024), ("g_final", 1024),
                ("g_sb_out", D_GRP), ("g_dil_out", D_GRP), ("rel_bias", N_BUCKETS * N_HEADS))


def _pack_small(parts, extra=None):
    flat = [parts[name].reshape(-1).astype(F32) for name, _ in _SMALL_ORDER]
    used = sum(sz for _, sz in _SMALL_ORDER)
    pad = SMALL_ROWS * 128 - used
    tail = jnp.zeros((pad,), F32)
    if extra is not None:
        tail = tail.at[0].set(extra)
    return jnp.concatenate(flat + [tail]).reshape(SMALL_ROWS, 128)


def _unpack_small(packed, shapes):
    flat = packed.reshape(-1)
    out, off = {}, 0
    for name, sz in _SMALL_ORDER:
        out[name] = flat[off:off + sz].reshape(shapes[name])
        off += sz
    return out, flat[off]


def kernel(x, c, w_ada, b_ada, g_ffn1, w1_gate, w1_up, w1_down, g_mix, w_in, g_sb_out, g_dil_out, w_out, rel_bias, g_ffn2, w2_gate, w2_up, w2_down, g_final, loss_target, m_w_ada, m_b_ada, m_g_ffn1, m_w1_gate, m_w1_up, m_w1_down, m_g_mix, m_w_in, m_g_sb_out, m_g_dil_out, m_w_out, m_rel_bias, m_g_ffn2, m_w2_gate, m_w2_up, m_w2_down, m_g_final, v_w_ada, v_b_ada, v_g_ffn1, v_w1_gate, v_w1_up, v_w1_down, v_g_mix, v_w_in, v_g_sb_out, v_g_dil_out, v_w_out, v_rel_bias, v_g_ffn2, v_w2_gate, v_w2_up, v_w2_down, v_g_final):
    nb, seq, d = x.shape
    xi, yi, ci = lax.axis_index("x"), lax.axis_index("y"), lax.axis_index("c")
    chip = 2 * xi + yi
    ada_cols = w_ada.shape[-1]

    c_pad = jnp.zeros((8, d), F32).at[:nb].set(c)
    b_shard = lax.dynamic_slice(b_ada, (0, chip * ada_cols), (1, ada_cols))
    shards = dict(gu1=jnp.stack([w1_gate[0], w1_up[0]]), d1=w1_down, win=w_in, wout=w_out,
                  gu2=jnp.stack([w2_gate[0], w2_up[0]]), d2=w2_down)
    bufs = {k: lax.dynamic_update_slice(lax.empty((N_CHIPS,) + s.shape, BF), s.astype(BF)[None],
                                        (chip if k in ("win", "wout") else 0, 0, 0, 0))
            for k, s in shards.items()}
    c_all, mod_blk, bufs["gu1"] = _ada_fwd(c_pad, w_ada[0], b_shard,
                                           carry=_ag_weights([bufs["gu1"]], relative=True))
    mod = jnp.transpose(mod_blk[:, :nb, :], (1, 0, 2)).reshape(nb, N_MOD, d)

    gains = dict(g_ffn1=g_ffn1, g_mix=g_mix, g_ffn2=g_ffn2, g_final=g_final.reshape(1, d),
                 g_sb_out=g_sb_out.reshape(1, D_GRP), g_dil_out=g_dil_out.reshape(1, D_GRP))
    place = jnp.stack([ci, chip]).astype(jnp.int32)
    r = _local_step(x, mod, loss_target, bufs, gains, rel_bias, place)

    dmod = r["dmod"]
    dmod_pad = jnp.zeros((8, N_MOD * d), F32).at[:nb].set(dmod)
    dmod_blk = jnp.transpose(dmod_pad.reshape(8, N_CHIPS, ada_cols), (1, 0, 2))
    small_parts = dict(b_ada=_rowsum8(dmod_pad), g_ffn1=r["dg_ffn1"], g_mix=r["dg_mix"], g_ffn2=r["dg_ffn2"],
                       g_final=r["dg_final"], g_sb_out=r["dg_sb"], g_dil_out=r["dg_dil"], rel_bias=r["drel"])
    halves1, parts_m = r["pending"]
    res = _small_sync(_pack_small(small_parts, r["loss"]), dmod_blk, c_all,
                      carry=_join([_rs_final(halves1), _rs_ici(parts_m)]))
    small_sum, g_wada, gffn1 = res[:3]
    halves_m = [_sum_chips(place, p, l) for p, l in zip(parts_m, res[3:5])]
    gwin, gwout = _alone("rs_last", _rs_final(halves_m))
    gffn2 = r["dffn2"]

    small_w = dict(b_ada=b_ada, g_ffn1=g_ffn1, g_mix=g_mix, g_ffn2=g_ffn2, g_final=g_final,
                   g_sb_out=g_sb_out, g_dil_out=g_dil_out, rel_bias=rel_bias)
    small_m = dict(b_ada=m_b_ada, g_ffn1=m_g_ffn1, g_mix=m_g_mix, g_ffn2=m_g_ffn2, g_final=m_g_final,
                   g_sb_out=m_g_sb_out, g_dil_out=m_g_dil_out, rel_bias=m_rel_bias)
    small_v = dict(b_ada=v_b_ada, g_ffn1=v_g_ffn1, g_mix=v_g_mix, g_ffn2=v_g_ffn2, g_final=v_g_final,
                   g_sb_out=v_g_sb_out, g_dil_out=v_g_dil_out, rel_bias=v_rel_bias)
    shapes = {k: v.shape for k, v in small_w.items()}
    sg, sd, sm, sv = _adamw(_pack_small(small_w), small_sum.reshape(1, SMALL_ROWS, 128), 0,
                            _pack_small(small_m), _pack_small(small_v))
    sg, loss = _unpack_small(sg, shapes)
    sd, _ = _unpack_small(sd, shapes)
    sm, _ = _unpack_small(sm, shapes)
    sv, _ = _unpack_small(sv, shapes)

    big = {}

    def upd(name, w, g_arr, sel, m, v, transposed=False):
        swap = (lambda a: jnp.swapaxes(a, -1, -2)) if transposed else (lambda a: a)
        w2, m2, v2 = [swap(a)[0] for a in (w, m, v)]
        big[name] = [swap(a[None]) for a in _adamw(w2, g_arr, sel, m2, v2)]

    upd("w_ada", w_ada, g_wada.reshape(1, d, ada_cols), 0, m_w_ada, v_w_ada)
    upd("w1_gate", w1_gate, gffn1, 0, m_w1_gate, v_w1_gate, transposed=True)
    upd("w1_up", w1_up, gffn1, 1, m_w1_up, v_w1_up, transposed=True)
    upd("w1_down", w1_down, gffn1, 2, m_w1_down, v_w1_down)
    upd("w_in", w_in, gwin, 0, m_w_in, v_w_in)
    upd("w_out", w_out, gwout, 0, m_w_out, v_w_out)
    upd("w2_gate", w2_gate, gffn2, 0, m_w2_gate, v_w2_gate, transposed=True)
    upd("w2_up", w2_up, gffn2, 1, m_w2_up, v_w2_up, transposed=True)
    upd("w2_down", w2_down, gffn2, 2, m_w2_down, v_w2_down)

    names = ["w_ada", "b_ada", "g_ffn1", "w1_gate", "w1_up", "w1_down", "g_mix", "w_in", "g_sb_out", "g_dil_out",
             "w_out", "rel_bias", "g_ffn2", "w2_gate", "w2_up", "w2_down", "g_final"]
    outs = [loss, r["grad_x"]]
    for k, small in enumerate((sg, sd, sm, sv)):
        for name in names:
            outs.append(big[name][k] if name in big else small[name])
    return tuple(outs)
```

```python
import functools
import math

import numpy as np
import jax
import jax.numpy as jnp
from jax import lax
from jax.experimental import pallas as pl
from jax.experimental.pallas import tpu as pltpu

F32 = jnp.float32
BF = jnp.bfloat16
MESH = pl.DeviceIdType.MESH

HEAD_DIM = 64
N_HEADS = 8
D_GRP = N_HEADS * HEAD_DIM
DIL_CONFIGS = ((128, 1), (512, 4), (2048, 16))
N_STEPS = 128
BLOCK = 128
N_BUCKETS = 32
MAX_DISTANCE = 2048
N_MOD = 9
EPS = 1e-6
NEG_INF = -1e30
SCALE = HEAD_DIM ** -0.5

ADAM_LR = 0.001
ADAM_B1 = 0.9
ADAM_B2 = 0.999
ADAM_EPS = 1e-08
ADAM_WD = 0.01
ADAM_STEP = 10

N_CHIPS = 4
N_DEV = 8
VMEM_LIMIT = 56 * 1024 * 1024
TM = 512
TQ = 256
KB = 256
SMALL_ROWS = 120


def _cp(n_axes=0, **kw):
    sem = ("arbitrary",) * n_axes if n_axes else None
    return pltpu.CompilerParams(dimension_semantics=sem, vmem_limit_bytes=VMEM_LIMIT, **kw)


def _nn(a, b):
    return jnp.dot(a, b, preferred_element_type=F32)


def _nt(a, b):
    return lax.dot_general(a, b, (((1,), (1,)), ((), ())), preferred_element_type=F32)


def _tn(a, b):
    return lax.dot_general(a, b, (((0,), (0,)), ((), ())), preferred_element_type=F32)


def _nn2(x, m):
    hi = x.astype(BF)
    lo = (x - hi.astype(F32)).astype(BF)
    r = _nn(jnp.concatenate([hi, lo], axis=0), m)
    return r[:x.shape[0]] + r[x.shape[0]:]


def _softplus(z):
    return jnp.maximum(z, 0.0) + jnp.log1p(jnp.exp(-jnp.abs(z)))


def _sds(shape, dtype):
    return jax.ShapeDtypeStruct(shape, dtype)


def _whole(a):
    nd = a.ndim
    return pl.BlockSpec(a.shape, lambda *_: (0,) * nd, pipeline_mode=pl.Buffered(1))


def _modnorm_bwd_tile(dh, xv, gv, scv, dxo):
    r = lax.rsqrt(jnp.mean(xv * xv, axis=-1, keepdims=True) + EPS)
    n = xv * r
    ng = n * gv
    dsh = jnp.sum(dh, axis=0, keepdims=True)
    dsc = jnp.sum(dh * ng, axis=0, keepdims=True)
    dy = dh * (1.0 + scv)
    dg = jnp.sum(dy * n, axis=0, keepdims=True)
    dn = dy * gv
    dx = dxo + r * (dn - n * jnp.mean(dn * n, axis=-1, keepdims=True))
    return dx, dsh, dsc, dg


def _acc_rows(ref, val, first):
    @pl.when(first)
    def _():
        ref[...] = val

    @pl.when(jnp.logical_not(first))
    def _():
        ref[...] += val


def _modnorm_tile(x_ref, g_ref, sc_ref, sh_ref):
    xv = x_ref[...]
    r = lax.rsqrt(jnp.mean(xv * xv, axis=-1, keepdims=True) + EPS)
    return (((xv * r) * g_ref[...]) * (1.0 + sc_ref[...]) + sh_ref[...]).astype(BF)


def _ffn_up(x, g, sc, sh, wgu, seq, carry=None):
    t, d = x.shape
    fs = wgu.shape[-1]
    per = seq // TM

    def body(x_ref, g_ref, sc_ref, sh_ref, w_ref, h_ref, p_ref, q_ref, s_ref):
        hv = _modnorm_tile(x_ref, g_ref, sc_ref, sh_ref)
        h_ref[...] = hv
        for j in range(N_CHIPS):
            a = _nn(hv, w_ref[j, 0])
            u = _nn(hv, w_ref[j, 1])
            sig = jax.nn.sigmoid(a)
            q = a * sig
            p_ref[j] = (u * (sig * (1.0 + a * (1.0 - sig)))).astype(BF)
            q_ref[j] = q.astype(BF)
            s_ref[j] = (q * u).astype(BF)

    row = pl.BlockSpec((TM, d), lambda m: (m, 0))
    ex = pl.BlockSpec((None, 1, d), lambda m: (m // per, 0, 0))
    blk = pl.BlockSpec((N_CHIPS, TM, fs), lambda m: (0, m, 0))
    return _call(
        body, "ffn_up", (t // TM,),
        [row, pl.BlockSpec((1, d), lambda m: (0, 0)), ex, ex, _whole(wgu)],
        [row, blk, blk, blk],
        [_sds((t, d), BF)] + [_sds((N_CHIPS, t, fs), BF)] * 3,
        (x, g, sc, sh, wgu), carry=carry)


def _ffn_down(s, wd, x, gt, seq, coef, carry=None):
    _, t, fs = s.shape
    d = x.shape[-1]
    per = seq // TM

    def body(s_ref, w_ref, x_ref, gt_ref, f_ref, xo_ref):
        f = _nn(s_ref[0], w_ref[0, 0])
        for j in range(1, N_CHIPS):
            f = f + _nn(s_ref[j], w_ref[j, 0])
        f_ref[...] = f.astype(BF)
        xo_ref[...] = x_ref[...] + (coef * gt_ref[...]) * f

    row = pl.BlockSpec((TM, d), lambda m: (m, 0))
    return _call(
        body, "ffn_down", (t // TM,),
        [pl.BlockSpec((N_CHIPS, TM, fs), lambda m: (0, m, 0)), _whole(wd), row,
         pl.BlockSpec((None, 1, d), lambda m: (m // per, 0, 0))],
        [row, row], [_sds((t, d), BF), _sds((t, d), F32)], (s, wd, x, gt), carry=carry)


def _ffn_bwd_ds(dxo, gt, f, wd, p, q, seq, coef, carry=None):
    t, d = dxo.shape
    fs = p.shape[-1]
    per = seq // TM
    nb = t // seq

    def body(dxo_ref, gt_ref, f_ref, w_ref, p_ref, q_ref, da_ref, du_ref, df_ref, dgt_ref):
        m = pl.program_id(0)
        dxv = dxo_ref[...]
        df = ((coef * gt_ref[...]) * dxv).astype(BF)
        df_ref[...] = df
        _acc_rows(dgt_ref, coef * jnp.sum(dxv * f_ref[...].astype(F32), axis=0, keepdims=True), m % per == 0)
        for j in range(N_CHIPS):
            ds = _nt(df, w_ref[j, 0])
            da_ref[j] = (ds * p_ref[j].astype(F32)).astype(BF)
            du_ref[j] = (ds * q_ref[j].astype(F32)).astype(BF)

    row = pl.BlockSpec((TM, d), lambda m: (m, 0))
    blk = pl.BlockSpec((N_CHIPS, TM, fs), lambda m: (0, m, 0))
    ex = pl.BlockSpec((None, 1, d), lambda m: (m // per, 0, 0))
    return _call(
        body, "ffn_bwd_ds", (t // TM,),
        [row, ex, row, _whole(wd), blk, blk],
        [blk, blk, row, ex],
        [_sds((N_CHIPS, t, fs), BF), _sds((N_CHIPS, t, fs), BF), _sds((t, d), BF), _sds((nb, 1, d), F32)],
        (dxo, gt, f, wd, p, q), carry=carry)


TM_X = 256


def _ffn_bwd_x(dxo, gt, f, wd, p, q, wgu, x, g, sc, seq, coef):
    t, d = dxo.shape
    fs = p.shape[-1]
    per = seq // TM_X
    nb = t // seq

    def body(dxo_ref, gt_ref, f_ref, wd_ref, p_ref, q_ref, w_ref, x_ref, g_ref, sc_ref,
             da_ref, du_ref, df_ref, dgt_ref, dx_ref, dsh_ref, dsc_ref, dg_ref):
        m = pl.program_id(0)
        dxv = dxo_ref[...]
        df = ((coef * gt_ref[...]) * dxv).astype(BF)
        df_ref[...] = df
        _acc_rows(dgt_ref, coef * jnp.sum(dxv * f_ref[...].astype(F32), axis=0, keepdims=True), m % per == 0)
        dh = None
        for j in range(N_CHIPS):
            ds = _nt(df, wd_ref[j, 0])
            da = (ds * p_ref[j].astype(F32)).astype(BF)
            du = (ds * q_ref[j].astype(F32)).astype(BF)
            da_ref[j] = da
            du_ref[j] = du
            part = _nt(da, w_ref[j, 0]) + _nt(du, w_ref[j, 1])
            dh = part if dh is None else dh + part
        dx, dsh, dsc, dg = _modnorm_bwd_tile(dh, x_ref[...], g_ref[...], sc_ref[...], dxv)
        dx_ref[...] = dx
        _acc_rows(dsh_ref, dsh, m % per == 0)
        _acc_rows(dsc_ref, dsc, m % per == 0)
        _acc_rows(dg_ref, dg, m == 0)

    row = pl.BlockSpec((TM_X, d), lambda m: (m, 0))
    blk = pl.BlockSpec((N_CHIPS, TM_X, fs), lambda m: (0, m, 0))
    ex = pl.BlockSpec((None, 1, d), lambda m: (m // per, 0, 0))
    vec = pl.BlockSpec((1, d), lambda m: (0, 0))
    exs = _sds((nb, 1, d), F32)
    return pl.pallas_call(
        body, name="ffn_bwd_x", grid=(t // TM_X,),
        in_specs=[row, ex, row, _whole(wd), blk, blk, _whole(wgu), row, vec, ex],
        out_specs=[blk, blk, row, ex, row, ex, ex, vec],
        out_shape=[_sds((N_CHIPS, t, fs), BF), _sds((N_CHIPS, t, fs), BF), _sds((t, d), BF), exs,
                   _sds((t, d), F32), exs, exs, _sds((1, d), F32)],
        compiler_params=_cp(1))(dxo, gt, f, wd, p, q, wgu, x, g, sc)


TK_W = 1024


def _ffn_bwd_w(h, da, du, s, df):
    t, d = h.shape
    fs = da.shape[-1]

    def body(h_ref, da_ref, du_ref, s_ref, df_ref, o_ref):
        kt = pl.program_id(1)
        hv = h_ref[...]
        parts = (_tn(da_ref[...], hv), _tn(du_ref[...], hv), _tn(s_ref[...], df_ref[...]))

        @pl.when(kt == 0)
        def _():
            for i, p in enumerate(parts):
                o_ref[i] = p

        @pl.when(kt != 0)
        def _():
            for i, p in enumerate(parts):
                o_ref[i] += p

    row = pl.BlockSpec((TK_W, d), lambda j, kt: (kt, 0))
    blk = pl.BlockSpec((None, TK_W, fs), lambda j, kt: (j, kt, 0))
    return pl.pallas_call(
        body, name="ffn_bwd_w", grid=(N_CHIPS, t // TK_W),
        in_specs=[row, blk, blk, blk, row],
        out_specs=pl.BlockSpec((None, 3, fs, d), lambda j, kt: (j, 0, 0, 0)),
        out_shape=_sds((N_CHIPS, 3, fs, d), F32),
        compiler_params=_cp(2))(h, da, du, s, df)


def _ffn_bwd_dh(da, du, wgu, x, g, sc, dxo, seq, carry=None):
    _, t, fs = da.shape
    d = x.shape[-1]
    per = seq // TM
    nb = t // seq

    def body(da_ref, du_ref, w_ref, x_ref, g_ref, sc_ref, dxo_ref, dx_ref, dsh_ref, dsc_ref, dg_ref):
        m = pl.program_id(0)
        dh = _nt(da_ref[0], w_ref[0, 0]) + _nt(du_ref[0], w_ref[0, 1])
        for j in range(1, N_CHIPS):
            dh = dh + _nt(da_ref[j], w_ref[j, 0]) + _nt(du_ref[j], w_ref[j, 1])
        dx, dsh, dsc, dg = _modnorm_bwd_tile(dh, x_ref[...], g_ref[...], sc_ref[...], dxo_ref[...])
        dx_ref[...] = dx
        _acc_rows(dsh_ref, dsh, m % per == 0)
        _acc_rows(dsc_ref, dsc, m % per == 0)
        _acc_rows(dg_ref, dg, m == 0)

    row = pl.BlockSpec((TM, d), lambda m: (m, 0))
    blk = pl.BlockSpec((N_CHIPS, TM, fs), lambda m: (0, m, 0))
    ex = pl.BlockSpec((None, 1, d), lambda m: (m // per, 0, 0))
    vec = pl.BlockSpec((1, d), lambda m: (0, 0))
    return _call(
        body, "ffn_bwd_dh", (t // TM,),
        [blk, blk, _whole(wgu), row, vec, ex, row],
        [row, ex, ex, vec],
        [_sds((t, d), F32), _sds((nb, 1, d), F32), _sds((nb, 1, d), F32), _sds((1, d), F32)],
        (da, du, wgu, x, g, sc, dxo), carry=carry)


def _qkv_proj(x, g, sc, sh, w_in, seq, carry=None):
    t, d = x.shape
    wc = w_in.shape[-1]
    per = seq // TM

    dils = [dil for _, dil in DIL_CONFIGS if dil > 1]

    def body(x_ref, g_ref, sc_ref, sh_ref, w_ref, h_ref, o_ref, *rest):
        res_refs, buf = rest[:len(dils)], rest[len(dils)]
        hv = _modnorm_tile(x_ref, g_ref, sc_ref, sh_ref)
        h_ref[...] = hv
        for j in range(N_CHIPS):
            rf = _nn(hv, w_ref[j, 0])
            r = rf.astype(BF)
            for a, lc, off, width in _col_pieces(j, wc):
                o_ref[a, :, lc:lc + width] = r[:, off:off + width]
                if a < 3:
                    continue
                for c0 in range(0, width, 128):
                    cg = (lc + c0) // 128
                    buf[...] = rf[:, off + c0:off + c0 + 128]
                    for ref, dil in zip(res_refs, dils):
                        for rr in range(dil):
                            ref[a - 3, :, rr * D_GRP + cg * 128:rr * D_GRP + (cg + 1) * 128] = (
                                buf[pl.ds(rr, TM // dil, stride=dil), :].astype(BF))

    row = pl.BlockSpec((TM, d), lambda m: (m, 0))
    ex = pl.BlockSpec((None, 1, d), lambda m: (m // per, 0, 0))
    return _call(
        body, "qkv_proj", (t // TM,),
        [row, pl.BlockSpec((1, d), lambda m: (0, 0)), ex, ex, _whole(w_in)],
        [row, pl.BlockSpec((6, TM, D_GRP), lambda m: (0, m, 0))]
        + [pl.BlockSpec((3, TM // dil, dil * D_GRP), lambda m: (0, m, 0)) for dil in dils],
        [_sds((t, d), BF), _sds((6, t, D_GRP), BF)] + [_sds((3, t // dil, dil * D_GRP), BF) for dil in dils],
        (x, g, sc, sh, w_in), scratch=[pltpu.VMEM((TM, 128), F32)], carry=carry)


def _col_pieces(j, wc):
    out, off = [], 0
    while off < wc:
        a, lc = divmod(j * wc + off, D_GRP)
        width = min(D_GRP - lc, wc - off)
        out.append((a, lc, off, width))
        off += width
    return out


def _chip_cols(g6_ref, j, wc):
    return jnp.concatenate([g6_ref[a, :, lc:lc + width] for a, lc, _, width in _col_pieces(j, wc)], axis=1)


def _mix_out(on_sb, on_dil, w_out, x, gt, seq):
    t, d = x.shape
    per = seq // TM

    def body(a_ref, b_ref, w_ref, x_ref, gt_ref, t_ref, xo_ref):
        tv = _nn(a_ref[...], w_ref[0:D_GRP, :]) + _nn(b_ref[...], w_ref[D_GRP:2 * D_GRP, :])
        t_ref[...] = tv.astype(BF)
        xo_ref[...] = x_ref[...] + gt_ref[...] * tv

    row = pl.BlockSpec((TM, d), lambda m: (m, 0))
    half = pl.BlockSpec((TM, D_GRP), lambda m: (m, 0))
    return pl.pallas_call(
        body, name="mix_out", grid=(t // TM,),
        in_specs=[half, half, pl.BlockSpec((2 * D_GRP, d), lambda m: (0, 0)), row,
                  pl.BlockSpec((None, 1, d), lambda m: (m // per, 0, 0))],
        out_specs=[row, row],
        out_shape=[_sds((t, d), BF), _sds((t, d), F32)],
        compiler_params=_cp(1))(on_sb, on_dil, w_out, x, gt)


def _sb_masks():
    lane = lax.broadcasted_iota(jnp.int32, (1, 2 * HEAD_DIM), 1)
    hm0 = lane < HEAD_DIM
    rel = lax.broadcasted_iota(jnp.int32, (TQ, KB), 0) - lax.broadcasted_iota(jnp.int32, (TQ, KB), 1)
    kr = lax.broadcasted_iota(jnp.int32, (KB, KB), 0)
    kc = lax.broadcasted_iota(jnp.int32, (KB, KB), 1)
    return hm0, rel, kr, kc


def _stack_pair(x, hm0):
    zero = jnp.zeros_like(x)
    return jnp.concatenate([jnp.where(hm0, x, zero), jnp.where(hm0, zero, x)], axis=0)


def _headnorm_pair(o, gv, hm0):
    o2 = o * o
    ms0 = jnp.sum(jnp.where(hm0, o2, 0.0), axis=-1, keepdims=True) * (1.0 / HEAD_DIM)
    ms1 = jnp.sum(jnp.where(hm0, 0.0, o2), axis=-1, keepdims=True) * (1.0 / HEAD_DIM)
    r = jnp.where(hm0, lax.rsqrt(ms0 + EPS), lax.rsqrt(ms1 + EPS))
    return (o * r) * gv


SB_DEAD = -104.0


def _alive(c_l):
    return (jnp.max(c_l) > SB_DEAD).astype(jnp.int32)


def _sb_fwd(qkv6, g_sb, nb, seq, carry=None):
    nq = seq // TQ

    def body(q_ref, k_ref, v_ref, g_ref, o_ref, on_ref):
        qi = pl.program_id(2)
        hm0, rel, kr, kc = _sb_masks()
        upper = (kr > kc).astype(BF)
        qs = _stack_pair(q_ref[...], hm0)
        causal2 = jnp.concatenate([rel, rel], axis=0) > 0

        def block(kj, causal, c_l, acc):
            ks = pl.multiple_of(kj * KB, KB)
            z = _nt(qs, k_ref[pl.ds(ks, KB), :]) * SCALE
            sp = _softplus(z)
            ln = -sp if causal is None else jnp.where(causal, -sp, 0.0)
            suf = _nn2(ln, upper)
            w = jnp.exp((z - sp) + (suf + c_l))
            if causal is not None:
                w = jnp.where(causal, w, 0.0)
            return c_l + (suf[:, 0:1] + ln[:, 0:1]), acc + _nn(w.astype(BF), v_ref[pl.ds(ks, KB), :])

        c_l, acc = block(qi, causal2, jnp.zeros((2 * TQ, 1), F32), jnp.zeros((2 * TQ, 2 * HEAD_DIM), F32))

        def cond(carry):
            return jnp.logical_and(carry[0] <= qi, carry[1] > 0)

        def kbody(carry):
            it, _, c_l, acc = carry
            c_l, acc = block(qi - it, None, c_l, acc)
            return it + 1, _alive(c_l), c_l, acc

        acc = lax.while_loop(cond, kbody, (jnp.int32(1), _alive(c_l), c_l, acc))[3]
        o = jnp.where(hm0, acc[:TQ], acc[TQ:])
        o_ref[...] = o
        on_ref[...] = _headnorm_pair(o, g_ref[...], hm0).astype(BF)

    w = 2 * HEAD_DIM
    full = lambda i: pl.BlockSpec((None, None, seq, w), lambda b, hp, q: (i, b, 0, hp))
    qblk = pl.BlockSpec((None, None, TQ, w), lambda b, hp, q: (0, b, q, hp))
    oblk = pl.BlockSpec((None, TQ, w), lambda b, hp, q: (b, q, hp))
    return _call(
        body, "sb_fwd", (nb, N_HEADS // 2, nq),
        [qblk, full(1), full(2), pl.BlockSpec((1, w), lambda b, hp, q: (0, hp))],
        [oblk, oblk],
        [_sds((nb, seq, D_GRP), F32), _sds((nb, seq, D_GRP), BF)],
        (qkv6, qkv6, qkv6, g_sb), carry=carry)


def _sb_bwd(qkv6, do, nb, seq, carry=None):
    nq = seq // TQ
    nk = seq // KB

    def body(q_ref, k_ref, v_ref, do_ref, out_ref, dk_acc, dv_acc, g_st, s_st):
        qi = pl.program_id(2)
        hm0, rel, kr, kc = _sb_masks()
        upper = (kr > kc).astype(BF)
        lower = (kr < kc).astype(BF)

        @pl.when(qi == 0)
        def _():
            dk_acc[...] = jnp.zeros_like(dk_acc)
            dv_acc[...] = jnp.zeros_like(dv_acc)

        qs = _stack_pair(q_ref[...], hm0)
        dos = _stack_pair(do_ref[...], hm0).astype(BF)
        causal2 = jnp.concatenate([rel, rel], axis=0) > 0

        def weights(kj, causal, c_l):
            ks = pl.multiple_of(kj * KB, KB)
            vb = v_ref[pl.ds(ks, KB), :]
            z = _nt(qs, k_ref[pl.ds(ks, KB), :]) * SCALE
            sp = _softplus(z)
            ln = -sp if causal is None else jnp.where(causal, -sp, 0.0)
            suf = _nn2(ln, upper)
            lsz = z - sp
            w = jnp.exp(lsz + (suf + c_l))
            if causal is not None:
                w = jnp.where(causal, w, 0.0)
            g_st[kj] = w * _nt(dos, vb)
            s_st[kj] = jnp.exp(lsz)
            dv_acc[pl.ds(ks, KB), :] += _tn(w.astype(BF), dos)
            return c_l + (suf[:, 0:1] + ln[:, 0:1])

        zc = jnp.zeros((2 * TQ, 1), F32)
        c_l = weights(qi, causal2, zc)

        def acond(carry):
            return jnp.logical_and(carry[0] <= qi, carry[1] > 0)

        def abody(carry):
            c_l = weights(qi - carry[0], None, carry[2])
            return carry[0] + 1, _alive(c_l), c_l

        n_used = lax.while_loop(acond, abody, (jnp.int32(1), _alive(c_l), c_l))[0]

        def grads(kj, causal, c_g, dq):
            ks = pl.multiple_of(kj * KB, KB)
            kb = k_ref[pl.ds(ks, KB), :]
            g = g_st[kj]
            sig = s_st[kj]
            pre = _nn(g.astype(BF), lower)
            dz = g * (1.0 - sig) - sig * (pre + c_g)
            if causal is not None:
                dz = jnp.where(causal, dz, 0.0)
            dzb = (dz * SCALE).astype(BF)
            dk_acc[pl.ds(ks, KB), :] += _tn(dzb, qs)
            return c_g + (pre[:, KB - 1:KB] + g[:, KB - 1:KB]), dq + _nn(dzb, kb)

        c_g, dq = lax.fori_loop(qi - n_used + 1, qi, lambda kj, cr: grads(kj, None, *cr),
                                (zc, jnp.zeros((2 * TQ, 2 * HEAD_DIM), F32)))
        _, dq = grads(qi, causal2, c_g, dq)
        dq = jnp.where(hm0, dq[:TQ], dq[TQ:])
        out_ref[0, pl.ds(pl.multiple_of(qi * TQ, TQ), TQ), :] = dq.astype(BF)

        @pl.when(qi == nq - 1)
        def _():
            out_ref[1] = dk_acc[...].astype(BF)
            out_ref[2] = dv_acc[...].astype(BF)

    w = 2 * HEAD_DIM
    full = lambda i: pl.BlockSpec((None, None, seq, w), lambda b, hp, q: (i, b, 0, hp))
    qblk = pl.BlockSpec((None, None, TQ, w), lambda b, hp, q: (0, b, q, hp))
    oblk = pl.BlockSpec((None, TQ, w), lambda b, hp, q: (b, q, hp))
    return _call(
        body, "sb_bwd", (nb, N_HEADS // 2, nq),
        [qblk, full(1), full(2), oblk],
        [pl.BlockSpec((3, None, seq, w), lambda b, hp, q: (0, b, 0, hp))],
        [_sds((6, nb, seq, D_GRP), BF)], (qkv6, qkv6, qkv6, do),
        scratch=[pltpu.VMEM((seq, w), F32), pltpu.VMEM((seq, w), F32),
                 pltpu.VMEM((nk, 2 * TQ, KB), F32), pltpu.VMEM((nk, 2 * TQ, KB), F32)],
        carry=carry)


def _t5_bucket(n):
    max_exact = N_BUCKETS // 2
    nf = np.maximum(n, 1).astype(np.float32)
    large = max_exact + (np.log(nf / max_exact) / math.log(MAX_DISTANCE / max_exact)
                         * (N_BUCKETS - max_exact)).astype(np.int32)
    large = np.minimum(large, N_BUCKETS - 1)
    return np.where(n < max_exact, n, large).astype(np.int32)


def _bucket_map(dilation):
    step = BLOCK + np.arange(BLOCK)[:, None] - np.arange(2 * BLOCK)[None, :]
    return _t5_bucket(np.clip(step, 0, N_STEPS) * dilation)


GRP_HEADS = 4
GRP_W = GRP_HEADS * HEAD_DIM


def _dil_masks():
    lane = lax.broadcasted_iota(jnp.int32, (1, GRP_W), 1)
    heads = [jnp.logical_and(lane >= HEAD_DIM * i, lane < HEAD_DIM * (i + 1)) for i in range(GRP_HEADS)]
    iq = jnp.bitwise_and(lax.broadcasted_iota(jnp.int32, (GRP_HEADS * BLOCK, BLOCK), 0), BLOCK - 1)
    ik = lax.broadcasted_iota(jnp.int32, (GRP_HEADS * BLOCK, BLOCK), 1)
    return heads, ik <= iq, ik >= iq


def _stack_heads(x, heads):
    zero = jnp.zeros_like(x)
    return jnp.concatenate([jnp.where(hm, x, zero) for hm in heads], axis=0)


def _unstack_heads(xs, heads):
    out = xs[0:BLOCK]
    for i in range(1, GRP_HEADS):
        out = jnp.where(heads[i], xs[i * BLOCK:(i + 1) * BLOCK], out)
    return out


def _dil_rows(n):
    rs = pl.multiple_of(n * BLOCK, BLOCK)
    ps = pl.multiple_of(jnp.maximum(n - 1, 0) * BLOCK, BLOCK)
    return pl.ds(rs, BLOCK), pl.ds(ps, BLOCK)


def _dil_probs(qs, kc, kp, b_ref, gi, valid_c, valid_p):
    rows = slice(gi * GRP_HEADS * BLOCK, (gi + 1) * GRP_HEADS * BLOCK)
    zc = _nt(qs, kc) * SCALE + b_ref[rows, BLOCK:2 * BLOCK]
    zp = _nt(qs, kp) * SCALE + b_ref[rows, 0:BLOCK]
    zc = jnp.where(valid_c, zc, NEG_INF)
    zp = jnp.where(valid_p, zp, NEG_INF)
    m = jnp.maximum(jnp.max(zc, axis=-1, keepdims=True), jnp.max(zp, axis=-1, keepdims=True))
    ec = jnp.exp(zc - m)
    ep = jnp.exp(zp - m)
    den = jnp.sum(ec, axis=-1, keepdims=True) + jnp.sum(ep, axis=-1, keepdims=True)
    return ec, ep, den, m


def _dil_fwd(qkv6r, base, bias, nb, sub_len, dilation):
    n_blk = sub_len // BLOCK

    def body(q_ref, k_ref, v_ref, b_ref, o_ref, l_ref):
        heads, valid_c, valid_p0 = _dil_masks()

        def nbody(n, carry):
            cur, prev = _dil_rows(n)
            valid_p = jnp.logical_and(valid_p0, n > 0)
            for gi in range(N_HEADS // GRP_HEADS):
                lanes = slice(gi * GRP_W, (gi + 1) * GRP_W)
                qs = _stack_heads(q_ref[cur, lanes], heads)
                ec, ep, den, m = _dil_probs(qs, k_ref[cur, lanes], k_ref[prev, lanes], b_ref, gi, valid_c, valid_p)
                o = (_nn(ec.astype(BF), v_ref[cur, lanes]) + _nn(ep.astype(BF), v_ref[prev, lanes])) / den
                o_ref[cur, lanes] = _unstack_heads(o, heads).astype(BF)
                l_ref[cur, lanes] = _unstack_heads(jnp.broadcast_to(m + jnp.log(den), o.shape), heads)
            return carry

        lax.fori_loop(0, n_blk, nbody, 0)

    seqblk = lambda i: pl.BlockSpec((None, None, sub_len, D_GRP), lambda b, r: (i, b, 0, r))
    oblk = pl.BlockSpec((None, sub_len, D_GRP), lambda b, r: (b, 0, r))
    shp = _sds((nb, sub_len, dilation * D_GRP), F32)
    return pl.pallas_call(
        body, name="dil_fwd_%d" % dilation, grid=(nb, dilation),
        in_specs=[seqblk(base), seqblk(base + 1), seqblk(base + 2), _whole(bias)],
        out_specs=[oblk, oblk], out_shape=[_sds(shp.shape, BF), shp],
        compiler_params=_cp(2))(qkv6r, qkv6r, qkv6r, bias)


def _dil_bwd(qkv6r, base, bias, do_c, dd_c, nb, sub_len, dilation, carry=None):
    n_blk = sub_len // BLOCK

    def body(q_ref, k_ref, v_ref, b_ref, do_ref, dd_ref, out_ref, a_ref, dk_acc, dv_acc):
        heads, valid_c, valid_p0 = _dil_masks()
        first = jnp.logical_and(pl.program_id(0) == 0, pl.program_id(1) == 0)

        @pl.when(first)
        def _():
            a_ref[...] = jnp.zeros_like(a_ref)

        dk_acc[...] = jnp.zeros_like(dk_acc)
        dv_acc[...] = jnp.zeros_like(dv_acc)

        def nbody(n, carry):
            cur, prev = _dil_rows(n)
            valid_p = jnp.logical_and(valid_p0, n > 0)
            for gi in range(N_HEADS // GRP_HEADS):
                lanes = slice(gi * GRP_W, (gi + 1) * GRP_W)
                kc, kp = k_ref[cur, lanes], k_ref[prev, lanes]
                vc, vp = v_ref[cur, lanes], v_ref[prev, lanes]
                qs = _stack_heads(q_ref[cur, lanes], heads)
                dos = _stack_heads(do_ref[cur, lanes], heads).astype(BF)
                dds = jnp.sum(_stack_heads(dd_ref[cur, lanes], heads), axis=-1, keepdims=True) * (1.0 / HEAD_DIM)
                ec, ep, den, _ = _dil_probs(qs, kc, kp, b_ref, gi, valid_c, valid_p)
                inv = 1.0 / den
                pc = ec * inv
                pp = ep * inv
                dzc = pc * (_nt(dos, vc) + dds)
                dzp = pp * (_nt(dos, vp) + dds)
                rows = slice(gi * GRP_HEADS * BLOCK, (gi + 1) * GRP_HEADS * BLOCK)
                a_ref[rows, BLOCK:2 * BLOCK] += dzc
                a_ref[rows, 0:BLOCK] += dzp
                dzcb = (dzc * SCALE).astype(BF)
                dzpb = (dzp * SCALE).astype(BF)
                out_ref[0, cur, lanes] = _unstack_heads(_nn(dzcb, kc) + _nn(dzpb, kp), heads).astype(BF)
                dk_acc[cur, lanes] += _tn(dzcb, qs)
                dk_acc[prev, lanes] += _tn(dzpb, qs)
                dv_acc[cur, lanes] += _tn(pc.astype(BF), dos)
                dv_acc[prev, lanes] += _tn(pp.astype(BF), dos)
            return carry

        lax.fori_loop(0, n_blk, nbody, 0)
        out_ref[1] = dk_acc[...].astype(BF)
        out_ref[2] = dv_acc[...].astype(BF)

    seqblk = lambda i: pl.BlockSpec((None, None, sub_len, D_GRP), lambda b, r: (i, b, 0, r))
    oblk = pl.BlockSpec((None, sub_len, D_GRP), lambda b, r: (b, 0, r))
    return _call(
        body, "dil_bwd_%d" % dilation, (nb, dilation),
        [seqblk(base), seqblk(base + 1), seqblk(base + 2), _whole(bias), oblk, oblk],
        [pl.BlockSpec((3, None, sub_len, D_GRP), lambda b, r: (0, b, 0, r)),
         pl.BlockSpec((N_HEADS * BLOCK, 2 * BLOCK), lambda b, r: (0, 0))],
        [_sds((3, nb, sub_len, dilation * D_GRP), BF), _sds((N_HEADS * BLOCK, 2 * BLOCK), F32)],
        (qkv6r, qkv6r, qkv6r, bias, do_c, dd_c),
        scratch=[pltpu.VMEM((sub_len, D_GRP), F32)] * 2, carry=carry)


def _group_ones():
    idx = np.arange(D_GRP) // HEAD_DIM
    return jnp.asarray((idx[:, None] == idx[None, :]).astype(np.float32), dtype=BF)


def _dil_alphas(l1, l4, l16):
    mx = jnp.maximum(jnp.maximum(l1, l4), l16)
    e1 = jnp.exp(l1 - mx)
    e4 = jnp.exp(l4 - mx)
    e16 = jnp.exp(l16 - mx)
    den = e1 + e4 + e16
    return e1 / den, e4 / den, e16 / den


def _residue_spec(dil):
    return pl.BlockSpec((TM // dil, dil * D_GRP), lambda m: (m, 0))


def _from_residue(src, dil, cg, buf):
    if dil == 1:
        return src[:, cg * 128:(cg + 1) * 128].astype(F32)
    for r in range(dil):
        buf[pl.ds(r, TM // dil, stride=dil), :] = (
            src[:, r * D_GRP + cg * 128:r * D_GRP + (cg + 1) * 128].astype(F32))
    return buf[...]


def _to_residue(dst, dil, cg, buf, val):
    if dil == 1:
        dst[:, cg * 128:(cg + 1) * 128] = val.astype(dst.dtype)
        return
    buf[...] = val
    for r in range(dil):
        dst[:, r * D_GRP + cg * 128:r * D_GRP + (cg + 1) * 128] = (
            buf[pl.ds(r, TM // dil, stride=dil), :].astype(dst.dtype))


def _pair_sum(x, hm0):
    s0 = jnp.sum(jnp.where(hm0, x, 0.0), axis=-1, keepdims=True)
    s1 = jnp.sum(jnp.where(hm0, 0.0, x), axis=-1, keepdims=True)
    return jnp.where(hm0, s0, s1)


def _dil_comb(os, ls, g_dil):
    t = os[0].shape[0]
    dils = [dil for _, dil in DIL_CONFIGS]

    def body(o1, l1, o4, l4, o16, l16, g_ref, o_ref, on_ref, b0, b1, b2, b3):
        hm0 = lax.broadcasted_iota(jnp.int32, (1, 128), 1) < HEAD_DIM
        for cg in range(D_GRP // 128):
            lanes = slice(cg * 128, (cg + 1) * 128)
            ov = [_from_residue(src, dil, cg, buf) for src, dil, buf in zip((o1, o4, o16), dils, (None, b0, b1))]
            lv = [_from_residue(src, dil, cg, buf) for src, dil, buf in zip((l1, l4, l16), dils, (None, b2, b3))]
            a1, a4, a16 = _dil_alphas(*lv)
            o = a1 * ov[0] + a4 * ov[1] + a16 * ov[2]
            o_ref[:, lanes] = o
            on_ref[:, lanes] = _headnorm_pair(o, g_ref[:, lanes], hm0).astype(BF)

    blk = pl.BlockSpec((TM, D_GRP), lambda m: (m, 0))
    specs = [_residue_spec(dil) for dil in dils for _ in range(2)]
    return pl.pallas_call(
        body, name="dil_comb", grid=(t // TM,),
        in_specs=specs + [pl.BlockSpec((1, D_GRP), lambda m: (0, 0))],
        out_specs=[blk, blk],
        out_shape=[_sds((t, D_GRP), F32), _sds((t, D_GRP), BF)],
        scratch_shapes=[pltpu.VMEM((TM, 128), F32)] * 4,
        compiler_params=_cp(1))(os[0], ls[0], os[1], ls[1], os[2], ls[2], g_dil)


def _dil_comb_bwd(do, os, ls):
    t = do.shape[0]
    dils = [dil for _, dil in DIL_CONFIGS]

    def body(do_ref, o1, l1, o4, l4, o16, l16, d1, d4, d16, e1, e4, e16, b0, b1, b2, b3):
        hm0 = lax.broadcasted_iota(jnp.int32, (1, 128), 1) < HEAD_DIM
        for cg in range(D_GRP // 128):
            dov = do_ref[:, cg * 128:(cg + 1) * 128]
            ov = [_from_residue(src, dil, cg, buf) for src, dil, buf in zip((o1, o4, o16), dils, (None, b0, b1))]
            lv = [_from_residue(src, dil, cg, buf) for src, dil, buf in zip((l1, l4, l16), dils, (None, b2, b3))]
            al = _dil_alphas(*lv)
            sbar = al[0] * _pair_sum(dov * ov[0], hm0)
            for a_c, o_c in zip(al[1:], ov[1:]):
                sbar = sbar + a_c * _pair_sum(dov * o_c, hm0)
            for a_c, dil, dref, eref in zip(al, dils, (d1, d4, d16), (e1, e4, e16)):
                _to_residue(dref, dil, cg, b0, a_c * dov)
                _to_residue(eref, dil, cg, b1, -a_c * sbar)

    specs = [_residue_spec(dil) for dil in dils]
    return pl.pallas_call(
        body, name="dil_comb_bwd", grid=(t // TM,),
        in_specs=[pl.BlockSpec((TM, D_GRP), lambda m: (m, 0))] + [sp for sp in specs for _ in range(2)],
        out_specs=specs + specs,
        out_shape=[_sds((t // dil, dil * D_GRP), BF) for dil in dils]
        + [_sds((t // dil, dil * D_GRP), F32) for dil in dils],
        scratch_shapes=[pltpu.VMEM((TM, 128), F32)] * 4,
        compiler_params=_cp(1))(do, os[0], ls[0], os[1], ls[1], os[2], ls[2])


def _dqkv_dil_sum(ds, dqkv6):
    t = dqkv6.shape[1]
    dils = [dil for _, dil in DIL_CONFIGS]

    def body(*refs):
        srcs, o_ref, acc = refs[:len(dils)], refs[len(dils) + 1], refs[len(dils) + 2]
        for a in range(3):
            for cg in range(D_GRP // 128):
                for src, dil in zip(srcs, dils):
                    for r in range(dil):
                        part = src[a, :, r * D_GRP + cg * 128:r * D_GRP + (cg + 1) * 128].astype(F32)
                        rows = pl.ds(r, TM // dil, stride=dil) if dil > 1 else slice(None)
                        if dil == dils[0]:
                            acc[rows, :] = part
                        else:
                            acc[rows, :] += part
                o_ref[a, :, cg * 128:(cg + 1) * 128] = acc[...].astype(BF)

    return pl.pallas_call(
        body, name="dqkv_dil_sum", grid=(t // TM,),
        in_specs=[pl.BlockSpec((3, TM // dil, dil * D_GRP), lambda m: (0, m, 0)) for dil in dils]
        + [pl.BlockSpec(memory_space=pl.ANY)],
        out_specs=pl.BlockSpec((3, TM, D_GRP), lambda m: (1, m, 0)),
        out_shape=_sds((6, t, D_GRP), BF), input_output_aliases={len(dils): 0},
        scratch_shapes=[pltpu.VMEM((TM, 128), F32)],
        compiler_params=_cp(1))(*ds, dqkv6)


def _relbias_grad(a_all, onehot):
    def body(a_ref, oh_ref, o_ref):
        acc = jnp.zeros((N_HEADS, N_BUCKETS), F32)
        for c in range(len(DIL_CONFIGS)):
            av = a_ref[c]
            hi = av.astype(BF)
            lo = (av - hi.astype(F32)).astype(BF)
            acc = acc + _nt(hi, oh_ref[c]) + _nt(lo, oh_ref[c])
        o_ref[...] = acc

    return pl.pallas_call(body, name="relbias_grad", out_shape=_sds((N_HEADS, N_BUCKETS), F32),
                          compiler_params=_cp())(a_all, onehot)


def _headnorm_bwd(dn, o, gv, mv):
    ms = _nn2(o * o, mv) * (1.0 / HEAD_DIM)
    r = lax.rsqrt(ms + EPS)
    nrm = o * r
    dg = jnp.sum(dn * nrm, axis=0, keepdims=True)
    dnn = dn * gv
    do = r * (dnn - nrm * (_nn2(dnn * nrm, mv) * (1.0 / HEAD_DIM)))
    return do, dg


def _mix_bwd_out(dx, gt, tv, w_out, o_sb, o_dil, on_sb, on_dil, g_sb, g_dil, ones_g, seq, carry=None):
    t, d = dx.shape
    per = seq // TM
    nb = t // seq

    def body(dx_ref, gt_ref, t_ref, w_ref, osb, odl, onsb, ondl, gsb, gdl, m_ref,
             dosb, dodl, dgt_ref, dgsb, dgdl, dw_ref):
        m = pl.program_id(0)
        dxv = dx_ref[...]
        dt = (gt_ref[...] * dxv).astype(BF)
        _acc_rows(dgt_ref, jnp.sum(dxv * t_ref[...].astype(F32), axis=0, keepdims=True), m % per == 0)
        mv = m_ref[...]
        don_sb = _nt(dt, w_ref[0:D_GRP, :])
        don_dl = _nt(dt, w_ref[D_GRP:2 * D_GRP, :])
        do1, dg1 = _headnorm_bwd(don_sb, osb[...], gsb[...], mv)
        do2, dg2 = _headnorm_bwd(don_dl, odl[...], gdl[...], mv)
        dosb[...] = do1
        dodl[...] = do2
        _acc_rows(dgsb, dg1, m == 0)
        _acc_rows(dgdl, dg2, m == 0)
        p1 = _tn(onsb[...], dt)
        p2 = _tn(ondl[...], dt)

        @pl.when(m == 0)
        def _():
            dw_ref[0:D_GRP, :] = p1
            dw_ref[D_GRP:2 * D_GRP, :] = p2

        @pl.when(m != 0)
        def _():
            dw_ref[0:D_GRP, :] += p1
            dw_ref[D_GRP:2 * D_GRP, :] += p2

    row = pl.BlockSpec((TM, d), lambda m: (m, 0))
    half = pl.BlockSpec((TM, D_GRP), lambda m: (m, 0))
    ex = pl.BlockSpec((None, 1, d), lambda m: (m // per, 0, 0))
    gvec = pl.BlockSpec((1, D_GRP), lambda m: (0, 0))
    wblk = pl.BlockSpec((2 * D_GRP, d), lambda m: (0, 0))
    return _call(
        body, "mix_bwd_out", (t // TM,),
        [row, ex, row, wblk, half, half, half, half, gvec, gvec, pl.BlockSpec((D_GRP, D_GRP), lambda m: (0, 0))],
        [half, half, ex, gvec, gvec, wblk],
        [_sds((t, D_GRP), F32), _sds((t, D_GRP), F32), _sds((nb, 1, d), F32),
         _sds((1, D_GRP), F32), _sds((1, D_GRP), F32), _sds((2 * D_GRP, d), F32)],
        (dx, gt, tv, w_out, o_sb, o_dil, on_sb, on_dil, g_sb, g_dil, ones_g), carry=carry)


def _dw_in(h, dqkv6, carry=None):
    t, d = h.shape
    wc = 6 * D_GRP // N_CHIPS

    def body(h_ref, g_ref, o_ref):
        kt = pl.program_id(0)
        hv = h_ref[...]
        for j in range(N_CHIPS):
            p = _tn(hv, _chip_cols(g_ref, j, wc))

            @pl.when(kt == 0)
            def _(p=p, j=j):
                o_ref[j, 0] = p

            @pl.when(kt != 0)
            def _(p=p, j=j):
                o_ref[j, 0] += p

    return _call(
        body, "dw_in", (t // TK_W,),
        [pl.BlockSpec((TK_W, d), lambda kt: (kt, 0)), pl.BlockSpec((6, TK_W, D_GRP), lambda kt: (0, kt, 0))],
        [pl.BlockSpec((N_CHIPS, 1, d, wc), lambda kt: (0, 0, 0, 0))],
        [_sds((N_CHIPS, 1, d, wc), F32)], (h, dqkv6), carry=carry)


def _mix_bwd_dh(dqkv6, w_in, x, g, sc, dxo, seq, carry=None):
    _, t, _ = dqkv6.shape
    d = x.shape[-1]
    wc = w_in.shape[-1]
    per = seq // TM
    nb = t // seq

    def body(g6_ref, w_ref, x_ref, g_ref, sc_ref, dxo_ref, dx_ref, dsh_ref, dsc_ref, dg_ref):
        m = pl.program_id(0)
        dh = _nt(_chip_cols(g6_ref, 0, wc), w_ref[0, 0])
        for j in range(1, N_CHIPS):
            dh = dh + _nt(_chip_cols(g6_ref, j, wc), w_ref[j, 0])
        dx, dsh, dsc, dg = _modnorm_bwd_tile(dh, x_ref[...], g_ref[...], sc_ref[...], dxo_ref[...])
        dx_ref[...] = dx
        _acc_rows(dsh_ref, dsh, m % per == 0)
        _acc_rows(dsc_ref, dsc, m % per == 0)
        _acc_rows(dg_ref, dg, m == 0)

    row = pl.BlockSpec((TM, d), lambda m: (m, 0))
    ex = pl.BlockSpec((None, 1, d), lambda m: (m // per, 0, 0))
    vec = pl.BlockSpec((1, d), lambda m: (0, 0))
    return _call(
        body, "mix_bwd_dh", (t // TM,),
        [pl.BlockSpec((6, TM, D_GRP), lambda m: (0, m, 0)), _whole(w_in), row, vec, ex, row],
        [row, ex, ex, vec],
        [_sds((t, d), F32), _sds((nb, 1, d), F32), _sds((nb, 1, d), F32), _sds((1, d), F32)],
        (dqkv6, w_in, x, g, sc, dxo), carry=carry)


def _ffn_down_loss(s, wd, x, gt, seq, coef, g, target):
    _, t, fs = s.shape
    d = x.shape[-1]
    per = seq // TM
    steps = t // TM

    def body(s_ref, w_ref, x_ref, gt_ref, g_ref, t_ref, f_ref, dx_ref, dg_ref, loss_ref, lacc):
        m = pl.program_id(0)
        f = _nn(s_ref[0], w_ref[0, 0])
        for j in range(1, N_CHIPS):
            f = f + _nn(s_ref[j], w_ref[j, 0])
        f_ref[...] = f.astype(BF)
        xv = x_ref[...] + (coef * gt_ref[...]) * f
        gv = g_ref[...]
        r = lax.rsqrt(jnp.mean(xv * xv, axis=-1, keepdims=True) + EPS)
        n = xv * r
        err = n * gv - t_ref[...]
        dy = err * (1.0 / d)
        _acc_rows(dg_ref, jnp.sum(dy * n, axis=0, keepdims=True), m == 0)
        dn = dy * gv
        dx_ref[...] = r * (dn - n * jnp.mean(dn * n, axis=-1, keepdims=True))
        _acc_rows(lacc, jnp.sum(err * err, axis=0, keepdims=True), m == 0)

        @pl.when(m == steps - 1)
        def _():
            tot = jnp.sum(lacc[...], axis=-1, keepdims=True) * (0.5 / d)
            loss_ref[...] = jnp.broadcast_to(tot, (1, 128))

    row = pl.BlockSpec((TM, d), lambda m: (m, 0))
    vec = pl.BlockSpec((1, d), lambda m: (0, 0))
    return pl.pallas_call(
        body, name="ffn_down_loss", grid=(steps,),
        in_specs=[pl.BlockSpec((N_CHIPS, TM, fs), lambda m: (0, m, 0)), _whole(wd), row,
                  pl.BlockSpec((None, 1, d), lambda m: (m // per, 0, 0)), vec, row],
        out_specs=[row, row, vec, pl.BlockSpec((1, 128), lambda m: (0, 0))],
        out_shape=[_sds((t, d), BF), _sds((t, d), F32), _sds((1, d), F32), _sds((1, 128), F32)],
        scratch_shapes=[pltpu.VMEM((1, d), F32)],
        compiler_params=_cp(1))(s, wd, x, gt, g, target)


def _row_tile(rows, cols):
    best = rows
    for tr in range(8, rows + 1, 8):
        if rows % tr == 0 and tr * cols * 4 <= (1 << 20):
            best = tr
    if best * cols * 4 > (1 << 21):
        best = 8
    return best


def _adamw(w, g_arr, g_sel, m, v):
    rows, cols = w.shape
    tr = _row_tile(rows, cols)
    b1c = 1.0 - ADAM_B1 ** ADAM_STEP
    b2c = 1.0 - ADAM_B2 ** ADAM_STEP

    def body(w_ref, g_ref, m_ref, v_ref, go_ref, d_ref, mo_ref, vo_ref):
        gv = g_ref[...]
        mn = ADAM_B1 * m_ref[...] + (1.0 - ADAM_B1) * gv
        vn = ADAM_B2 * v_ref[...] + (1.0 - ADAM_B2) * (gv * gv)
        go_ref[...] = gv
        mo_ref[...] = mn
        vo_ref[...] = vn
        d_ref[...] = -ADAM_LR * ((mn / b1c) / (jnp.sqrt(vn / b2c) + ADAM_EPS) + ADAM_WD * w_ref[...])

    blk = pl.BlockSpec((tr, cols), lambda i: (i, 0))
    shp = _sds((rows, cols), F32)
    return pl.pallas_call(
        body, name="adamw", grid=(rows // tr,),
        in_specs=[blk, pl.BlockSpec((None, tr, cols), lambda i: (g_sel, i, 0)), blk, blk],
        out_specs=[blk] * 4, out_shape=[shp] * 4,
        compiler_params=_cp(1))(w, g_arr, m, v)


def _flip(v, bit):
    return 1 - v if bit else v


def _my_place():
    x, y, c = lax.axis_index("x"), lax.axis_index("y"), lax.axis_index("c")
    return x, y, c


class _Exchange:
    def __init__(self, operands, out_shape, aliases, sems, start, finish):
        self.operands, self.out_shape, self.aliases, self.sems = list(operands), list(out_shape), dict(aliases), list(sems)
        self.start, self.finish = start, finish


def _join(exchanges):
    exchanges = [e for e in exchanges if e is not None]
    if not exchanges:
        return None
    ops, outs, sems, aliases, spans = [], [], [], {}, []
    for e in exchanges:
        spans.append((len(ops), len(outs), len(sems), e))
        for i, j in e.aliases.items():
            aliases[len(ops) + i] = len(outs) + j
        ops += e.operands
        outs += e.out_shape
        sems += e.sems

    def run(which):
        def go(ins, res, sm):
            for io, oo, so, e in spans:
                getattr(e, which)(ins[io:io + len(e.operands)], res[oo:oo + len(e.out_shape)], sm[so:so + len(e.sems)])
        return go

    return _Exchange(ops, outs, aliases, sems, run("start"), run("finish"))


def _call(body, name, grid, in_specs, out_specs, out_shape, args, scratch=(), carry=None, io_alias=None):
    in_specs, out_specs, out_shape, scratch = list(in_specs), list(out_specs), list(out_shape), list(scratch)
    io_alias = dict(io_alias or {})
    if carry is None:
        return pl.pallas_call(body, name=name, grid=grid, in_specs=in_specs, out_specs=out_specs,
                              out_shape=out_shape, scratch_shapes=scratch, input_output_aliases=io_alias,
                              compiler_params=_cp(len(grid)))(*args)
    n_in, n_out, n_s = len(in_specs), len(out_specs), len(scratch)
    c_in, c_out = len(carry.operands), len(carry.out_shape)
    any_spec = pl.BlockSpec(memory_space=pl.ANY)

    def wrapped(*refs):
        ins, cins = refs[:n_in], refs[n_in:n_in + c_in]
        o0 = n_in + c_in
        outs, couts = refs[o0:o0 + n_out], refs[o0 + n_out:o0 + n_out + c_out]
        s0 = o0 + n_out + c_out
        scr, sems = refs[s0:s0 + n_s], refs[s0 + n_s:]
        first = pl.program_id(0) == 0
        last = pl.program_id(0) == grid[0] - 1
        for ax in range(1, len(grid)):
            first = jnp.logical_and(first, pl.program_id(ax) == 0)
            last = jnp.logical_and(last, pl.program_id(ax) == grid[ax] - 1)

        @pl.when(first)
        def _():
            carry.start(cins, couts, sems)

        body(*ins, *outs, *scr)

        @pl.when(last)
        def _():
            carry.finish(cins, couts, sems)

    return pl.pallas_call(
        wrapped, name=name, grid=grid, in_specs=in_specs + [any_spec] * c_in,
        out_specs=out_specs + [any_spec] * c_out, out_shape=out_shape + carry.out_shape,
        scratch_shapes=scratch + carry.sems,
        input_output_aliases={**io_alias, **{n_in + i: n_out + j for i, j in carry.aliases.items()}},
        compiler_params=_cp(len(grid)))(*args, *carry.operands)


def _whole_call(body, name, args, out_shape, scratch, carry=None):
    vm = pl.BlockSpec(memory_space=pltpu.VMEM)
    any_spec = pl.BlockSpec(memory_space=pl.ANY)
    out_shape, scratch = list(out_shape), list(scratch)
    n_in, n_out, n_s = len(args), len(out_shape), len(scratch)
    if carry is None:
        return pl.pallas_call(body, name=name, in_specs=[vm] * n_in, out_specs=[vm] * n_out, out_shape=out_shape,
                              scratch_shapes=scratch, compiler_params=_cp())(*args)
    c_in, c_out = len(carry.operands), len(carry.out_shape)

    def wrapped(*refs):
        ins, cins = refs[:n_in], refs[n_in:n_in + c_in]
        o0 = n_in + c_in
        outs, couts = refs[o0:o0 + n_out], refs[o0 + n_out:o0 + n_out + c_out]
        s0 = o0 + n_out + c_out
        scr, sems = refs[s0:s0 + n_s], refs[s0 + n_s:]
        carry.start(cins, couts, sems)
        body(*ins, *outs, *scr)
        carry.finish(cins, couts, sems)

    return pl.pallas_call(
        wrapped, name=name, in_specs=[vm] * n_in + [any_spec] * c_in, out_specs=[vm] * n_out + [any_spec] * c_out,
        out_shape=out_shape + carry.out_shape, scratch_shapes=scratch + carry.sems,
        input_output_aliases={n_in + i: n_out + j for i, j in carry.aliases.items()},
        compiler_params=_cp())(*args, *carry.operands)


def _alone(name, ex):
    any_spec = pl.BlockSpec(memory_space=pl.ANY)
    c_in, c_out = len(ex.operands), len(ex.out_shape)

    def body(*refs):
        ins, outs, sems = refs[:c_in], refs[c_in:c_in + c_out], refs[c_in + c_out:]
        ex.start(ins, outs, sems)
        ex.finish(ins, outs, sems)

    return pl.pallas_call(
        body, name=name, in_specs=[any_spec] * c_in, out_specs=[any_spec] * c_out, out_shape=ex.out_shape,
        scratch_shapes=ex.sems, input_output_aliases=ex.aliases, compiler_params=_cp())(*ex.operands)


def _ada_fwd(c_pad, w_ada, b_shard, carry=None):
    d = c_pad.shape[-1]
    cols = w_ada.shape[-1]
    chunk = 384

    def body(c_ref, w_ref, b_ref, call_ref, mod_ref, part, s1, r1, s2, r2):
        x, y, c = _my_place()
        dev = 4 * x + 2 * y + c
        chip = 2 * x + y
        call_ref[dev] = c_ref[...]

        def c_copy(k):
            px, py, pc = _flip(x, (k >> 2) & 1), _flip(y, (k >> 1) & 1), _flip(c, k & 1)
            return px, py, pc

        sends = []
        for k in range(1, N_DEV):
            px, py, pc = c_copy(k)
            cp = pltpu.make_async_remote_copy(src_ref=c_ref, dst_ref=call_ref.at[dev], send_sem=s1.at[k - 1],
                                              recv_sem=r1.at[k - 1], device_id=(px, py, pc), device_id_type=MESH)
            cp.start()
            sends.append(cp)
        for k in range(1, N_DEV):
            px, py, pc = c_copy(k)
            pltpu.make_async_remote_copy(src_ref=c_ref, dst_ref=call_ref.at[4 * px + 2 * py + pc],
                                         send_sem=s1.at[k - 1], recv_sem=r1.at[k - 1],
                                         device_id=(px, py, pc), device_id_type=MESH).wait_recv()
        for cp in sends:
            cp.wait_send()

        cs = call_ref[...].reshape(N_DEV * 8, d)
        sc = (cs * jax.nn.sigmoid(cs)).astype(BF)
        for n0 in range(0, cols, chunk):
            blk = _nn(sc, w_ref[:, n0:n0 + chunk].astype(BF)) + b_ref[:, n0:n0 + chunk]
            part[:, :, n0:n0 + chunk] = blk.reshape(N_DEV, 8, chunk)

        mod_ref[chip] = part[dev]
        sends = []
        for kk in range(1, N_CHIPS):
            px, py = _flip(x, (kk >> 1) & 1), _flip(y, kk & 1)
            cp = pltpu.make_async_remote_copy(src_ref=part.at[4 * px + 2 * py + c], dst_ref=mod_ref.at[chip],
                                              send_sem=s2.at[kk - 1], recv_sem=r2.at[kk - 1],
                                              device_id=(px, py, c), device_id_type=MESH)
            cp.start()
            sends.append(cp)
        for kk in range(1, N_CHIPS):
            px, py = _flip(x, (kk >> 1) & 1), _flip(y, kk & 1)
            pltpu.make_async_remote_copy(src_ref=part.at[dev], dst_ref=mod_ref.at[2 * px + py],
                                         send_sem=s2.at[kk - 1], recv_sem=r2.at[kk - 1],
                                         device_id=(px, py, c), device_id_type=MESH).wait_recv()
        for cp in sends:
            cp.wait_send()

    return _whole_call(
        body, "ada_fwd", (c_pad, w_ada, b_shard),
        [_sds((N_DEV, 8, d), F32), _sds((N_CHIPS, 8, cols), F32)],
        [pltpu.VMEM((N_DEV, 8, cols), F32),
         pltpu.SemaphoreType.DMA((N_DEV - 1,)), pltpu.SemaphoreType.DMA((N_DEV - 1,)),
         pltpu.SemaphoreType.DMA((N_CHIPS - 1,)), pltpu.SemaphoreType.DMA((N_CHIPS - 1,))], carry=carry)


def _ag_weights(bufs, kks=(1, 2, 3), relative=False):
    n, nk = len(bufs), len(kks)

    def half(b, which):
        hr = bufs[b].shape[2] // 2
        return pl.ds(pl.multiple_of(which * hr, 16), hr)

    def copies(outs, sems, b, i, kk):
        x, y, c = _my_place()
        chip = 2 * x + y
        px, py = _flip(x, (kk >> 1) & 1), _flip(y, kk & 1)
        mine, theirs = (0, kk) if relative else (chip, 2 * px + py)
        landing = kk if relative else chip
        k = nk * b + i
        send = pltpu.make_async_remote_copy(
            src_ref=outs[b].at[mine, :, half(b, c), :], dst_ref=outs[b].at[landing, :, half(b, c), :],
            send_sem=sems[0].at[k], recv_sem=sems[1].at[k], device_id=(px, py, c), device_id_type=MESH)
        got = outs[b].at[theirs, :, half(b, c), :]
        recv = pltpu.make_async_remote_copy(
            src_ref=got, dst_ref=got, send_sem=sems[0].at[k], recv_sem=sems[1].at[k],
            device_id=(px, py, c), device_id_type=MESH)
        fwd = pltpu.make_async_remote_copy(
            src_ref=got, dst_ref=got, send_sem=sems[2].at[k], recv_sem=sems[3].at[k],
            device_id=(x, y, 1 - c), device_id_type=MESH)
        other = outs[b].at[theirs, :, half(b, 1 - c), :]
        back = pltpu.make_async_remote_copy(
            src_ref=other, dst_ref=other, send_sem=sems[2].at[k], recv_sem=sems[3].at[k],
            device_id=(x, y, 1 - c), device_id_type=MESH)
        return send, recv, fwd, back

    def each(outs, sems):
        for b in range(n):
            for i, kk in enumerate(kks):
                yield copies(outs, sems, b, i, kk)

    def start(ins, outs, sems):
        for send, _, _, _ in each(outs, sems):
            send.start()

    def finish(ins, outs, sems):
        for _, recv, fwd, _ in each(outs, sems):
            recv.wait_recv()
            fwd.start()
        for send, _, fwd, back in each(outs, sems):
            back.wait_recv()
            send.wait_send()
            fwd.wait_send()

    return _Exchange(bufs, [_sds(s.shape, s.dtype) for s in bufs], {i: i for i in range(n)},
                     [pltpu.SemaphoreType.DMA((nk * n,))] * 4, start, finish)


def _rs_d2d(grads):
    n = len(grads)

    def copy(ins, outs, sems, b):
        x, y, c = _my_place()
        hr = grads[b].shape[2] // 2
        theirs = pl.ds(pl.multiple_of((1 - c) * hr, 8), hr)
        return pltpu.make_async_remote_copy(
            src_ref=ins[b].at[:, :, theirs, :], dst_ref=outs[b], send_sem=sems[0].at[b], recv_sem=sems[1].at[b],
            device_id=(x, y, 1 - c), device_id_type=MESH)

    def start(ins, outs, sems):
        for b in range(n):
            copy(ins, outs, sems, b).start()

    def finish(ins, outs, sems):
        for b in range(n):
            copy(ins, outs, sems, b).wait()

    return _Exchange(grads, [_sds(g.shape[:2] + (g.shape[2] // 2, g.shape[3]), F32) for g in grads], {},
                     [pltpu.SemaphoreType.DMA((n,))] * 2, start, finish)


def _add_halves(core, g, land):
    nchip, ng, rows, cols = g.shape
    hr = rows // 2
    tr = _row_tile(hr, cols)
    steps = hr // tr

    def body(core_ref, g_ref, l_ref, o_ref):
        del core_ref
        o_ref[...] = (g_ref[...] + l_ref[...]).astype(BF)

    return pl.pallas_call(
        body, name="add_halves",
        grid_spec=pltpu.PrefetchScalarGridSpec(
            num_scalar_prefetch=1, grid=(nchip, ng, steps),
            in_specs=[pl.BlockSpec((None, None, tr, cols), lambda j, a, i, cr: (j, a, cr[0] * steps + i, 0)),
                      pl.BlockSpec((None, None, tr, cols), lambda j, a, i, cr: (j, a, i, 0))],
            out_specs=pl.BlockSpec((None, None, tr, cols), lambda j, a, i, cr: (j, a, i, 0))),
        out_shape=_sds((nchip, ng, hr, cols), BF),
        compiler_params=_cp(3))(core, g, land)


def _rs_ici(parts, relative=False):
    n = len(parts)

    def copies(ins, outs, sems):
        x, y, c = _my_place()
        chip = 2 * x + y
        for b in range(n):
            for kk in range(1, N_CHIPS):
                px, py = _flip(x, (kk >> 1) & 1), _flip(y, kk & 1)
                k = 3 * b + kk - 1
                theirs, landing = (kk, kk) if relative else (2 * px + py, chip)
                send = pltpu.make_async_remote_copy(
                    src_ref=ins[b].at[theirs], dst_ref=outs[b].at[landing],
                    send_sem=sems[0].at[k], recv_sem=sems[1].at[k], device_id=(px, py, c), device_id_type=MESH)
                slot = outs[b].at[theirs]
                recv = pltpu.make_async_remote_copy(
                    src_ref=slot, dst_ref=slot, send_sem=sems[0].at[k], recv_sem=sems[1].at[k],
                    device_id=(px, py, c), device_id_type=MESH)
                yield send, recv

    def start(ins, outs, sems):
        for send, _ in copies(ins, outs, sems):
            send.start()

    def finish(ins, outs, sems):
        for send, recv in copies(ins, outs, sems):
            recv.wait_recv()
            send.wait_send()

    return _Exchange(parts, [_sds(p.shape, p.dtype) for p in parts], {},
                     [pltpu.SemaphoreType.DMA((3 * n,))] * 2, start, finish)


def _sum_chips(place, part, land, relative=False):
    nchip, ng, hr, cols = land.shape
    tr = _row_tile(hr, cols)
    steps = hr // tr

    def body(place_ref, p_ref, l1, l2, l3, o_ref):
        del place_ref
        o_ref[...] = ((p_ref[...].astype(F32) + l1[...].astype(F32)) + l2[...].astype(F32)) + l3[...].astype(F32)

    def slot(k):
        if relative:
            return pl.BlockSpec((None, None, tr, cols), lambda a, i, pr: (k, a, i, 0))
        return pl.BlockSpec((None, None, tr, cols), lambda a, i, pr: (jnp.bitwise_xor(pr[1], k), a, i, 0))

    return pl.pallas_call(
        body, name="sum_chips",
        grid_spec=pltpu.PrefetchScalarGridSpec(
            num_scalar_prefetch=1, grid=(ng, steps),
            in_specs=[slot(0), slot(1), slot(2), slot(3)],
            out_specs=pl.BlockSpec((None, tr, cols), lambda a, i, pr: (a, pr[0] * steps + i, 0))),
        out_shape=_sds((ng, 2 * hr, cols), F32),
        compiler_params=_cp(2))(place, part, land, land, land)


def _rs_final(bufs):
    n = len(bufs)

    def copy(outs, sems, b, which):
        x, y, c = _my_place()
        hr = bufs[b].shape[1] // 2
        rows = outs[b].at[:, pl.ds(pl.multiple_of((c if which == 0 else 1 - c) * hr, 8), hr), :]
        return pltpu.make_async_remote_copy(
            src_ref=rows, dst_ref=rows, send_sem=sems[0].at[b], recv_sem=sems[1].at[b],
            device_id=(x, y, 1 - c), device_id_type=MESH)

    def start(ins, outs, sems):
        for b in range(n):
            copy(outs, sems, b, 0).start()

    def finish(ins, outs, sems):
        for b in range(n):
            copy(outs, sems, b, 0).wait_send()
            copy(outs, sems, b, 1).wait_recv()

    return _Exchange(bufs, [_sds(h.shape, F32) for h in bufs], {i: i for i in range(n)},
                     [pltpu.SemaphoreType.DMA((n,))] * 2, start, finish)


def _small_sync(smalls, dmod_blk, c_all, carry=None):
    d = c_all.shape[-1]
    cols = dmod_blk.shape[-1]
    chunk = 384

    def body(sm_ref, dm_ref, c_ref, sum_ref, gw_ref, sm_all, dm_all, ssem, rsem):
        x, y, c = _my_place()
        dev = 4 * x + 2 * y + c
        chip = 2 * x + y
        sm_all[dev] = sm_ref[...]
        dm_all[dev] = dm_ref[chip]
        sends = []
        for k in range(1, N_DEV):
            px, py, pc = _flip(x, (k >> 2) & 1), _flip(y, (k >> 1) & 1), _flip(c, k & 1)
            a = pltpu.make_async_remote_copy(src_ref=sm_ref, dst_ref=sm_all.at[dev], send_sem=ssem.at[2 * (k - 1)],
                                             recv_sem=rsem.at[2 * (k - 1)], device_id=(px, py, pc),
                                             device_id_type=MESH)
            b = pltpu.make_async_remote_copy(src_ref=dm_ref.at[2 * px + py], dst_ref=dm_all.at[dev],
                                             send_sem=ssem.at[2 * (k - 1) + 1], recv_sem=rsem.at[2 * (k - 1) + 1],
                                             device_id=(px, py, pc), device_id_type=MESH)
            a.start()
            b.start()
            sends += [a, b]
        for k in range(1, N_DEV):
            px, py, pc = _flip(x, (k >> 2) & 1), _flip(y, (k >> 1) & 1), _flip(c, k & 1)
            pdev = 4 * px + 2 * py + pc
            pltpu.make_async_remote_copy(src_ref=sm_ref, dst_ref=sm_all.at[pdev], send_sem=ssem.at[2 * (k - 1)],
                                         recv_sem=rsem.at[2 * (k - 1)], device_id=(px, py, pc),
                                         device_id_type=MESH).wait_recv()
            pltpu.make_async_remote_copy(src_ref=dm_ref.at[chip], dst_ref=dm_all.at[pdev],
                                         send_sem=ssem.at[2 * (k - 1) + 1], recv_sem=rsem.at[2 * (k - 1) + 1],
                                         device_id=(px, py, pc), device_id_type=MESH).wait_recv()
        for cp in sends:
            cp.wait_send()

        tot = sm_all[0]
        for q in range(1, N_DEV):
            tot = tot + sm_all[q]
        sum_ref[...] = tot

        cs = c_ref[...].reshape(N_DEV * 8, d)
        sc = (cs * jax.nn.sigmoid(cs)).astype(BF)
        for n0 in range(0, cols, chunk):
            dmv = dm_all[:, :, n0:n0 + chunk].reshape(N_DEV * 8, chunk).astype(BF)
            gw_ref[:, n0:n0 + chunk] = _tn(sc, dmv)

    return _whole_call(
        body, "small_sync", (smalls, dmod_blk, c_all),
        [_sds(smalls.shape, F32), _sds((d, cols), F32)],
        [pltpu.VMEM((N_DEV,) + smalls.shape, F32), pltpu.VMEM((N_DEV, 8, cols), F32),
         pltpu.SemaphoreType.DMA((2 * (N_DEV - 1),)), pltpu.SemaphoreType.DMA((2 * (N_DEV - 1),))], carry=carry)


def _bucket_onehot():
    maps = np.stack([_bucket_map(dil).reshape(-1) for _, dil in DIL_CONFIGS])
    return (jnp.asarray(maps)[:, None, :] == jnp.arange(N_BUCKETS, dtype=jnp.int32)[None, :, None]).astype(BF)


def _dil_bias(rel_t, onehot):
    def body(r_ref, oh_ref, o_ref):
        rv = r_ref[...]
        hi = rv.astype(BF)
        lo = (rv - hi.astype(F32)).astype(BF)
        for c in range(len(DIL_CONFIGS)):
            o_ref[c] = _nn(hi, oh_ref[c]) + _nn(lo, oh_ref[c])

    return pl.pallas_call(body, name="dil_bias",
                          out_shape=_sds((len(DIL_CONFIGS), N_HEADS, BLOCK * 2 * BLOCK), F32),
                          compiler_params=_cp())(rel_t, onehot)


def _rowsum8(a):
    def body(a_ref, o_ref):
        o_ref[...] = jnp.sum(a_ref[...], axis=0, keepdims=True)

    return pl.pallas_call(body, name="rowsum8", out_shape=_sds((1, a.shape[1]), F32), compiler_params=_cp())(a)


def _local_step(x, mod, target, w, gains, rel_bias, place=None):
    nb, seq, d = x.shape
    t = nb * seq
    dist = place is not None
    core = place[0:1] if dist else None
    x0 = x.reshape(t, d)
    tgt = target.reshape(t, d)
    md = [mod[:, i:i + 1, :] for i in range(N_MOD)]
    sh1, sc1, gt1, sh2, sc2, gt2, sh3, sc3, gt3 = md
    g1, g2, g3 = gains["g_ffn1"], gains["g_mix"], gains["g_ffn2"]
    ones_g = _group_ones()

    def partial_sums(grads, lands):
        return [_add_halves(core, g, l) for g, l in zip(grads, lands)]

    def chip_sums(parts, lands):
        return [_sum_chips(place, p, l, relative=True) for p, l in zip(parts, lands)]

    gu1 = w["gu1"]
    res = _ffn_up(x0, g1, sc1, sh1, gu1, seq,
                  carry=_join([_ag_weights([w["d1"]], relative=True), _ag_weights([w["win"]])]) if dist else None)
    h1, a1, u1, s1 = res[:4]
    wd1, w_in = res[4:] if dist else (w["d1"], w["win"])
    f1, x1 = _ffn_down(s1, wd1, x0, gt1, seq, 0.5)

    h2, qkv6, qkv_r4, qkv_r16 = _qkv_proj(x1, g2, sc2, sh2, w_in, seq)
    qkv6b = qkv6.reshape(6, nb, seq, D_GRP)
    res = _sb_fwd(qkv6b, gains["g_sb_out"], nb, seq,
                  carry=_join([_ag_weights([w["gu2"], w["d2"]], relative=True), _ag_weights([w["wout"]])])
                  if dist else None)
    o_sb, on_sb = res[:2]
    wgu2, wd2, w_out = res[2:] if dist else (w["gu2"], w["d2"], w["wout"])
    w_out2 = w_out.reshape(2 * D_GRP, d)
    onehot = _bucket_onehot()
    bias = _dil_bias(rel_bias.T, onehot).reshape(len(DIL_CONFIGS), N_HEADS * BLOCK, 2 * BLOCK)
    o_cs, l_cs = [], []
    qkv_rs = [(qkv6b, 3), (qkv_r4, 0), (qkv_r16, 0)]
    for ci, (_, dil) in enumerate(DIL_CONFIGS):
        sub = seq // dil
        arr, base = qkv_rs[ci]
        arr = arr.reshape(base + 3, nb, sub, dil * D_GRP)
        qkv_rs[ci] = (arr, base)
        o_c, l_c = _dil_fwd(arr, base, bias[ci], nb, sub, dil)
        o_cs.append(o_c.reshape(t // dil, dil * D_GRP))
        l_cs.append(l_c.reshape(t // dil, dil * D_GRP))
    o_dil, on_dil = _dil_comb(o_cs, l_cs, gains["g_dil_out"])
    tmix, x2 = _mix_out(on_sb.reshape(t, D_GRP), on_dil, w_out2, x1, gt2, seq)

    h3, a3, u3, s3 = _ffn_up(x2, g3, sc3, sh3, wgu2, seq)
    f3, dx3, dg_final, loss = _ffn_down_loss(s3, wd2, x2, gt3, seq, 0.5, gains["g_final"], tgt)

    da3, du3, df3, dgt3, dx2, dsh3, dsc3, dg3 = _ffn_bwd_x(dx3, gt3, f3, wd2, a3, u3, wgu2, x2, g3, sc3, seq, 0.5)
    grads2 = [_ffn_bwd_w(h3, da3, du3, s3, df3)]

    res = _mix_bwd_out(
        dx2, gt2, tmix, w_out2, o_sb.reshape(t, D_GRP), o_dil, on_sb.reshape(t, D_GRP), on_dil,
        gains["g_sb_out"], gains["g_dil_out"], ones_g, seq, carry=_rs_d2d(grads2) if dist else None)
    do_sb, do_dil, dgt2, dg_sb, dg_dil, dw_out = res[:6]
    parts2 = partial_sums(grads2, res[6:]) if dist else None
    dw_out = dw_out.reshape(N_CHIPS, 1, 2 * D_GRP // N_CHIPS, d)
    res = _sb_bwd(qkv6b, do_sb.reshape(nb, seq, D_GRP), nb, seq,
                  carry=_rs_ici(parts2, relative=True) if dist else None)
    dqkv6 = res[0]
    halves2 = chip_sums(parts2, res[1:]) if dist else None
    dcs = _dil_comb_bwd(do_dil, o_cs, l_cs)
    dsum, a_tiles = [], []
    for ci, (_, dil) in enumerate(DIL_CONFIGS):
        sub = seq // dil
        do_c = dcs[ci].reshape(nb, sub, dil * D_GRP)
        dd_c = dcs[3 + ci].reshape(nb, sub, dil * D_GRP)
        res = _dil_bwd(qkv_rs[ci][0], qkv_rs[ci][1], bias[ci], do_c, dd_c, nb, sub, dil)
        dsum.append(res[0].reshape(3, t // dil, dil * D_GRP))
        a_tiles.append(res[1].reshape(N_HEADS, BLOCK * 2 * BLOCK))
    dqkv6 = _dqkv_dil_sum(dsum, dqkv6.reshape(6, t, D_GRP))
    drel = _relbias_grad(jnp.stack(a_tiles), onehot)
    dx1, dsh2, dsc2, dg2 = _mix_bwd_dh(dqkv6, w_in, x1, g2, sc2, dx2, seq)

    da1, du1, df1, dgt1 = _ffn_bwd_ds(dx1, gt1, f1, wd1, a1, u1, seq, 0.5)
    grads1 = [_ffn_bwd_w(h1, da1, du1, s1, df1)]
    res = _dw_in(h2, dqkv6, carry=_join([_rs_d2d(grads1), _rs_final(halves2)]) if dist else None)
    grads_m = [res[0], dw_out]
    parts1 = partial_sums(grads1, res[1:2]) if dist else None
    if dist:
        grads2 = res[2:3]
    res = _ffn_bwd_dh(da1, du1, gu1, x0, g1, sc1, dx1, seq,
                      carry=_join([_rs_ici(parts1, relative=True), _rs_d2d(grads_m)]) if dist else None)
    dx0, dsh1, dsc1, dg1 = res[:4]
    pending = None
    if dist:
        pending = (chip_sums(parts1, res[4:5]), partial_sums(grads_m, res[5:7]))

    dmod = jnp.concatenate([dsh1, dsc1, dgt1, dsh2, dsc2, dgt2, dsh3, dsc3, dgt3], axis=1)
    return dict(grad_x=dx0.reshape(nb, seq, d), loss=loss[0, 0], dmod=dmod.reshape(nb, N_MOD * d),
                dffn1=grads1[0], dffn2=grads2[0], dwin=grads_m[0], dwout=grads_m[1], pending=pending,
                dg_ffn1=dg1, dg_mix=dg2, dg_ffn2=dg3, dg_final=dg_final, dg_sb=dg_sb, dg_dil=dg_dil,
                drel=drel.T)


_SMALL_ORDER = (("b_ada", N_MOD * 1024), ("g_ffn1", 1024), ("g_mix", 1024), ("g_ffn2", 1024), ("g_final", 1024),
                ("g_sb_out", D_GRP), ("g_dil_out", D_GRP), ("rel_bias", N_BUCKETS * N_HEADS))


def _pack_small(parts, extra=None):
    flat = [parts[name].reshape(-1).astype(F32) for name, _ in _SMALL_ORDER]
    used = sum(sz for _, sz in _SMALL_ORDER)
    pad = SMALL_ROWS * 128 - used
    tail = jnp.zeros((pad,), F32)
    if extra is not None:
        tail = tail.at[0].set(extra)
    return jnp.concatenate(flat + [tail]).reshape(SMALL_ROWS, 128)


def _unpack_small(packed, shapes):
    flat = packed.reshape(-1)
    out, off = {}, 0
    for name, sz in _SMALL_ORDER:
        out[name] = flat[off:off + sz].reshape(shapes[name])
        off += sz
    return out, flat[off]


def kernel(x, c, w_ada, b_ada, g_ffn1, w1_gate, w1_up, w1_down, g_mix, w_in, g_sb_out, g_dil_out, w_out, rel_bias, g_ffn2, w2_gate, w2_up, w2_down, g_final, loss_target, m_w_ada, m_b_ada, m_g_ffn1, m_w1_gate, m_w1_up, m_w1_down, m_g_mix, m_w_in, m_g_sb_out, m_g_dil_out, m_w_out, m_rel_bias, m_g_ffn2, m_w2_gate, m_w2_up, m_w2_down, m_g_final, v_w_ada, v_b_ada, v_g_ffn1, v_w1_gate, v_w1_up, v_w1_down, v_g_mix, v_w_in, v_g_sb_out, v_g_dil_out, v_w_out, v_rel_bias, v_g_ffn2, v_w2_gate, v_w2_up, v_w2_down, v_g_final):
    nb, seq, d = x.shape
    xi, yi, ci = lax.axis_index("x"), lax.axis_index("y"), lax.axis_index("c")
    chip = 2 * xi + yi
    ada_cols = w_ada.shape[-1]

    c_pad = jnp.zeros((8, d), F32).at[:nb].set(c)
    b_shard = lax.dynamic_slice(b_ada, (0, chip * ada_cols), (1, ada_cols))
    shards = dict(gu1=jnp.stack([w1_gate[0], w1_up[0]]), d1=w1_down, win=w_in, wout=w_out,
                  gu2=jnp.stack([w2_gate[0], w2_up[0]]), d2=w2_down)
    bufs = {k: lax.dynamic_update_slice(lax.empty((N_CHIPS,) + s.shape, BF), s.astype(BF)[None],
                                        (chip if k in ("win", "wout") else 0, 0, 0, 0))
            for k, s in shards.items()}
    c_all, mod_blk, bufs["gu1"] = _ada_fwd(c_pad, w_ada[0], b_shard,
                                           carry=_ag_weights([bufs["gu1"]], relative=True))
    mod = jnp.transpose(mod_blk[:, :nb, :], (1, 0, 2)).reshape(nb, N_MOD, d)

    gains = dict(g_ffn1=g_ffn1, g_mix=g_mix, g_ffn2=g_ffn2, g_final=g_final.reshape(1, d),
                 g_sb_out=g_sb_out.reshape(1, D_GRP), g_dil_out=g_dil_out.reshape(1, D_GRP))
    place = jnp.stack([ci, chip]).astype(jnp.int32)
    r = _local_step(x, mod, loss_target, bufs, gains, rel_bias, place)

    dmod = r["dmod"]
    dmod_pad = jnp.zeros((8, N_MOD * d), F32).at[:nb].set(dmod)
    dmod_blk = jnp.transpose(dmod_pad.reshape(8, N_CHIPS, ada_cols), (1, 0, 2))
    small_parts = dict(b_ada=_rowsum8(dmod_pad), g_ffn1=r["dg_ffn1"], g_mix=r["dg_mix"], g_ffn2=r["dg_ffn2"],
                       g_final=r["dg_final"], g_sb_out=r["dg_sb"], g_dil_out=r["dg_dil"], rel_bias=r["drel"])
    halves1, parts_m = r["pending"]
    res = _small_sync(_pack_small(small_parts, r["loss"]), dmod_blk, c_all,
                      carry=_join([_rs_final(halves1), _rs_ici(parts_m)]))
    small_sum, g_wada, gffn1 = res[:3]
    halves_m = [_sum_chips(place, p, l) for p, l in zip(parts_m, res[3:5])]
    gwin, gwout = _alone("rs_last", _rs_final(halves_m))
    gffn2 = r["dffn2"]

    small_w = dict(b_ada=b_ada, g_ffn1=g_ffn1, g_mix=g_mix, g_ffn2=g_ffn2, g_final=g_final,
                   g_sb_out=g_sb_out, g_dil_out=g_dil_out, rel_bias=rel_bias)
    small_m = dict(b_ada=m_b_ada, g_ffn1=m_g_ffn1, g_mix=m_g_mix, g_ffn2=m_g_ffn2, g_final=m_g_final,
                   g_sb_out=m_g_sb_out, g_dil_out=m_g_dil_out, rel_bias=m_rel_bias)
    small_v = dict(b_ada=v_b_ada, g_ffn1=v_g_ffn1, g_mix=v_g_mix, g_ffn2=v_g_ffn2, g_final=v_g_final,
                   g_sb_out=v_g_sb_out, g_dil_out=v_g_dil_out, rel_bias=v_rel_bias)
    shapes = {k: v.shape for k, v in small_w.items()}
    sg, sd, sm, sv = _adamw(_pack_small(small_w), small_sum.reshape(1, SMALL_ROWS, 128), 0,
                            _pack_small(small_m), _pack_small(small_v))
    sg, loss = _unpack_small(sg, shapes)
    sd, _ = _unpack_small(sd, shapes)
    sm, _ = _unpack_small(sm, shapes)
    sv, _ = _unpack_small(sv, shapes)

    big = {}

    def upd(name, w, g_arr, sel, m, v, transposed=False):
        swap = (lambda a: jnp.swapaxes(a, -1, -2)) if transposed else (lambda a: a)
        w2, m2, v2 = [swap(a)[0] for a in (w, m, v)]
        big[name] = [swap(a[None]) for a in _adamw(w2, g_arr, sel, m2, v2)]

    upd("w_ada", w_ada, g_wada.reshape(1, d, ada_cols), 0, m_w_ada, v_w_ada)
    upd("w1_gate", w1_gate, gffn1, 0, m_w1_gate, v_w1_gate, transposed=True)
    upd("w1_up", w1_up, gffn1, 1, m_w1_up, v_w1_up, transposed=True)
    upd("w1_down", w1_down, gffn1, 2, m_w1_down, v_w1_down)
    upd("w_in", w_in, gwin, 0, m_w_in, v_w_in)
    upd("w_out", w_out, gwout, 0, m_w_out, v_w_out)
    upd("w2_gate", w2_gate, gffn2, 0, m_w2_gate, v_w2_gate, transposed=True)
    upd("w2_up", w2_up, gffn2, 1, m_w2_up, v_w2_up, transposed=True)
    upd("w2_down", w2_down, gffn2, 2, m_w2_down, v_w2_down)

    names = ["w_ada", "b_ada", "g_ffn1", "w1_gate", "w1_up", "w1_down", "g_mix", "w_in", "g_sb_out", "g_dil_out",
             "w_out", "rel_bias", "g_ffn2", "w2_gate", "w2_up", "w2_down", "g_final"]
    outs = [loss, r["grad_x"]]
    for k, small in enumerate((sg, sd, sm, sv)):
        for name in names:
            outs.append(big[name][k] if name in big else small[name])
    return tuple(outs)
```

```python
import functools
import math

import numpy as np
import jax
import jax.numpy as jnp
from jax import lax
from jax.experimental import pallas as pl
from jax.experimental.pallas import tpu as pltpu

F32 = jnp.float32
BF = jnp.bfloat16
MESH = pl.DeviceIdType.MESH

HEAD_DIM = 64
N_HEADS = 8
D_GRP = N_HEADS * HEAD_DIM
DIL_CONFIGS = ((128, 1), (512, 4), (2048, 16))
N_STEPS = 128
BLOCK = 128
N_BUCKETS = 32
MAX_DISTANCE = 2048
N_MOD = 9
EPS = 1e-6
NEG_INF = -1e30
SCALE = HEAD_DIM ** -0.5

ADAM_LR = 0.001
ADAM_B1 = 0.9
ADAM_B2 = 0.999
ADAM_EPS = 1e-08
ADAM_WD = 0.01
ADAM_STEP = 10

N_CHIPS = 4
N_DEV = 8
VMEM_LIMIT = 56 * 1024 * 1024
TM = 512
TQ = 256
KB = 256
SMALL_ROWS = 120


def _cp(n_axes=0, **kw):
    sem = ("arbitrary",) * n_axes if n_axes else None
    return pltpu.CompilerParams(dimension_semantics=sem, vmem_limit_bytes=VMEM_LIMIT, **kw)


def _nn(a, b):
    return jnp.dot(a, b, preferred_element_type=F32)


def _nt(a, b):
    return lax.dot_general(a, b, (((1,), (1,)), ((), ())), preferred_element_type=F32)


def _tn(a, b):
    return lax.dot_general(a, b, (((0,), (0,)), ((), ())), preferred_element_type=F32)


def _nn2(x, m):
    hi = x.astype(BF)
    lo = (x - hi.astype(F32)).astype(BF)
    r = _nn(jnp.concatenate([hi, lo], axis=0), m)
    return r[:x.shape[0]] + r[x.shape[0]:]


def _softplus(z):
    return jnp.maximum(z, 0.0) + jnp.log1p(jnp.exp(-jnp.abs(z)))


def _sds(shape, dtype):
    return jax.ShapeDtypeStruct(shape, dtype)


def _whole(a):
    nd = a.ndim
    return pl.BlockSpec(a.shape, lambda *_: (0,) * nd, pipeline_mode=pl.Buffered(1))


def _modnorm_bwd_tile(dh, xv, gv, scv, dxo):
    r = lax.rsqrt(jnp.mean(xv * xv, axis=-1, keepdims=True) + EPS)
    n = xv * r
    ng = n * gv
    dsh = jnp.sum(dh, axis=0, keepdims=True)
    dsc = jnp.sum(dh * ng, axis=0, keepdims=True)
    dy = dh * (1.0 + scv)
    dg = jnp.sum(dy * n, axis=0, keepdims=True)
    dn = dy * gv
    dx = dxo + r * (dn - n * jnp.mean(dn * n, axis=-1, keepdims=True))
    return dx, dsh, dsc, dg


def _acc_rows(ref, val, first):
    @pl.when(first)
    def _():
        ref[...] = val

    @pl.when(jnp.logical_not(first))
    def _():
        ref[...] += val


def _modnorm_tile(x_ref, g_ref, sc_ref, sh_ref):
    xv = x_ref[...]
    r = lax.rsqrt(jnp.mean(xv * xv, axis=-1, keepdims=True) + EPS)
    return (((xv * r) * g_ref[...]) * (1.0 + sc_ref[...]) + sh_ref[...]).astype(BF)


def _ffn_up(x, g, sc, sh, wgu, seq, carry=None):
    t, d = x.shape
    fs = wgu.shape[-1]
    per = seq // TM

    def body(x_ref, g_ref, sc_ref, sh_ref, w_ref, h_ref, p_ref, q_ref, s_ref):
        hv = _modnorm_tile(x_ref, g_ref, sc_ref, sh_ref)
        h_ref[...] = hv
        for j in range(N_CHIPS):
            a = _nn(hv, w_ref[j, 0])
            u = _nn(hv, w_ref[j, 1])
            sig = jax.nn.sigmoid(a)
            q = a * sig
            p_ref[j] = (u * (sig * (1.0 + a * (1.0 - sig)))).astype(BF)
            q_ref[j] = q.astype(BF)
            s_ref[j] = (q * u).astype(BF)

    row = pl.BlockSpec((TM, d), lambda m: (m, 0))
    ex = pl.BlockSpec((None, 1, d), lambda m: (m // per, 0, 0))
    blk = pl.BlockSpec((N_CHIPS, TM, fs), lambda m: (0, m, 0))
    return _call(
        body, "ffn_up", (t // TM,),
        [row, pl.BlockSpec((1, d), lambda m: (0, 0)), ex, ex, _whole(wgu)],
        [row, blk, blk, blk],
        [_sds((t, d), BF)] + [_sds((N_CHIPS, t, fs), BF)] * 3,
        (x, g, sc, sh, wgu), carry=carry)


def _ffn_down(s, wd, x, gt, seq, coef, carry=None):
    _, t, fs = s.shape
    d = x.shape[-1]
    per = seq // TM

    def body(s_ref, w_ref, x_ref, gt_ref, f_ref, xo_ref):
        f = _nn(s_ref[0], w_ref[0, 0])
        for j in range(1, N_CHIPS):
            f = f + _nn(s_ref[j], w_ref[j, 0])
        f_ref[...] = f.astype(BF)
        xo_ref[...] = x_ref[...] + (coef * gt_ref[...]) * f

    row = pl.BlockSpec((TM, d), lambda m: (m, 0))
    return _call(
        body, "ffn_down", (t // TM,),
        [pl.BlockSpec((N_CHIPS, TM, fs), lambda m: (0, m, 0)), _whole(wd), row,
         pl.BlockSpec((None, 1, d), lambda m: (m // per, 0, 0))],
        [row, row], [_sds((t, d), BF), _sds((t, d), F32)], (s, wd, x, gt), carry=carry)


def _ffn_bwd_ds(dxo, gt, f, wd, p, q, seq, coef, carry=None):
    t, d = dxo.shape
    fs = p.shape[-1]
    per = seq // TM
    nb = t // seq

    def body(dxo_ref, gt_ref, f_ref, w_ref, p_ref, q_ref, da_ref, du_ref, df_ref, dgt_ref):
        m = pl.program_id(0)
        dxv = dxo_ref[...]
        df = ((coef * gt_ref[...]) * dxv).astype(BF)
        df_ref[...] = df
        _acc_rows(dgt_ref, coef * jnp.sum(dxv * f_ref[...].astype(F32), axis=0, keepdims=True), m % per == 0)
        for j in range(N_CHIPS):
            ds = _nt(df, w_ref[j, 0])
            da_ref[j] = (ds * p_ref[j].astype(F32)).astype(BF)
            du_ref[j] = (ds * q_ref[j].astype(F32)).astype(BF)

    row = pl.BlockSpec((TM, d), lambda m: (m, 0))
    blk = pl.BlockSpec((N_CHIPS, TM, fs), lambda m: (0, m, 0))
    ex = pl.BlockSpec((None, 1, d), lambda m: (m // per, 0, 0))
    return _call(
        body, "ffn_bwd_ds", (t // TM,),
        [row, ex, row, _whole(wd), blk, blk],
        [blk, blk, row, ex],
        [_sds((N_CHIPS, t, fs), BF), _sds((N_CHIPS, t, fs), BF), _sds((t, d), BF), _sds((nb, 1, d), F32)],
        (dxo, gt, f, wd, p, q), carry=carry)


TM_X = 256


def _ffn_bwd_x(dxo, gt, f, wd, p, q, wgu, x, g, sc, seq, coef):
    t, d = dxo.shape
    fs = p.shape[-1]
    per = seq // TM_X
    nb = t // seq

    def body(dxo_ref, gt_ref, f_ref, wd_ref, p_ref, q_ref, w_ref, x_ref, g_ref, sc_ref,
             da_ref, du_ref, df_ref, dgt_ref, dx_ref, dsh_ref, dsc_ref, dg_ref):
        m = pl.program_id(0)
        dxv = dxo_ref[...]
        df = ((coef * gt_ref[...]) * dxv).astype(BF)
        df_ref[...] = df
        _acc_rows(dgt_ref, coef * jnp.sum(dxv * f_ref[...].astype(F32), axis=0, keepdims=True), m % per == 0)
        dh = None
        for j in range(N_CHIPS):
            ds = _nt(df, wd_ref[j, 0])
            da = (ds * p_ref[j].astype(F32)).astype(BF)
            du = (ds * q_ref[j].astype(F32)).astype(BF)
            da_ref[j] = da
            du_ref[j] = du
            part = _nt(da, w_ref[j, 0]) + _nt(du, w_ref[j, 1])
            dh = part if dh is None else dh + part
        dx, dsh, dsc, dg = _modnorm_bwd_tile(dh, x_ref[...], g_ref[...], sc_ref[...], dxv)
        dx_ref[...] = dx
        _acc_rows(dsh_ref, dsh, m % per == 0)
        _acc_rows(dsc_ref, dsc, m % per == 0)
        _acc_rows(dg_ref, dg, m == 0)

    row = pl.BlockSpec((TM_X, d), lambda m: (m, 0))
    blk = pl.BlockSpec((N_CHIPS, TM_X, fs), lambda m: (0, m, 0))
    ex = pl.BlockSpec((None, 1, d), lambda m: (m // per, 0, 0))
    vec = pl.BlockSpec((1, d), lambda m: (0, 0))
    exs = _sds((nb, 1, d), F32)
    return pl.pallas_call(
        body, name="ffn_bwd_x", grid=(t // TM_X,),
        in_specs=[row, ex, row, _whole(wd), blk, blk, _whole(wgu), row, vec, ex],
        out_specs=[blk, blk, row, ex, row, ex, ex, vec],
        out_shape=[_sds((N_CHIPS, t, fs), BF), _sds((N_CHIPS, t, fs), BF), _sds((t, d), BF), exs,
                   _sds((t, d), F32), exs, exs, _sds((1, d), F32)],
        compiler_params=_cp(1))(dxo, gt, f, wd, p, q, wgu, x, g, sc)


TK_W = 1024


def _ffn_bwd_w(h, da, du, s, df):
    t, d = h.shape
    fs = da.shape[-1]

    def body(h_ref, da_ref, du_ref, s_ref, df_ref, o_ref):
        kt = pl.program_id(1)
        hv = h_ref[...]
        parts = (_tn(da_ref[...], hv), _tn(du_ref[...], hv), _tn(s_ref[...], df_ref[...]))

        @pl.when(kt == 0)
        def _():
            for i, p in enumerate(parts):
                o_ref[i] = p

        @pl.when(kt != 0)
        def _():
            for i, p in enumerate(parts):
                o_ref[i] += p

    row = pl.BlockSpec((TK_W, d), lambda j, kt: (kt, 0))
    blk = pl.BlockSpec((None, TK_W, fs), lambda j, kt: (j, kt, 0))
    return pl.pallas_call(
        body, name="ffn_bwd_w", grid=(N_CHIPS, t // TK_W),
        in_specs=[row, blk, blk, blk, row],
        out_specs=pl.BlockSpec((None, 3, fs, d), lambda j, kt: (j, 0, 0, 0)),
        out_shape=_sds((N_CHIPS, 3, fs, d), F32),
        compiler_params=_cp(2))(h, da, du, s, df)


def _ffn_bwd_dh(da, du, wgu, x, g, sc, dxo, seq, carry=None):
    _, t, fs = da.shape
    d = x.shape[-1]
    per = seq // TM
    nb = t // seq

    def body(da_ref, du_ref, w_ref, x_ref, g_ref, sc_ref, dxo_ref, dx_ref, dsh_ref, dsc_ref, dg_ref):
        m = pl.program_id(0)
        dh = _nt(da_ref[0], w_ref[0, 0]) + _nt(du_ref[0], w_ref[0, 1])
        for j in range(1, N_CHIPS):
            dh = dh + _nt(da_ref[j], w_ref[j, 0]) + _nt(du_ref[j], w_ref[j, 1])
        dx, dsh, dsc, dg = _modnorm_bwd_tile(dh, x_ref[...], g_ref[...], sc_ref[...], dxo_ref[...])
        dx_ref[...] = dx
        _acc_rows(dsh_ref, dsh, m % per == 0)
        _acc_rows(dsc_ref, dsc, m % per == 0)
        _acc_rows(dg_ref, dg, m == 0)

    row = pl.BlockSpec((TM, d), lambda m: (m, 0))
    blk = pl.BlockSpec((N_CHIPS, TM, fs), lambda m: (0, m, 0))
    ex = pl.BlockSpec((None, 1, d), lambda m: (m // per, 0, 0))
    vec = pl.BlockSpec((1, d), lambda m: (0, 0))
    return _call(
        body, "ffn_bwd_dh", (t // TM,),
        [blk, blk, _whole(wgu), row, vec, ex, row],
        [row, ex, ex, vec],
        [_sds((t, d), F32), _sds((nb, 1, d), F32), _sds((nb, 1, d), F32), _sds((1, d), F32)],
        (da, du, wgu, x, g, sc, dxo), carry=carry)


def _qkv_proj(x, g, sc, sh, w_in, seq, carry=None):
    t, d = x.shape
    wc = w_in.shape[-1]
    per = seq // TM

    dils = [dil for _, dil in DIL_CONFIGS if dil > 1]

    def body(x_ref, g_ref, sc_ref, sh_ref, w_ref, h_ref, o_ref, *rest):
        res_refs, buf = rest[:len(dils)], rest[len(dils)]
        hv = _modnorm_tile(x_ref, g_ref, sc_ref, sh_ref)
        h_ref[...] = hv
        for j in range(N_CHIPS):
            rf = _nn(hv, w_ref[j, 0])
            r = rf.astype(BF)
            for a, lc, off, width in _col_pieces(j, wc):
                o_ref[a, :, lc:lc + width] = r[:, off:off + width]
                if a < 3:
                    continue
                for c0 in range(0, width, 128):
                    cg = (lc + c0) // 128
                    buf[...] = rf[:, off + c0:off + c0 + 128]
                    for ref, dil in zip(res_refs, dils):
                        for rr in range(dil):
                            ref[a - 3, :, rr * D_GRP + cg * 128:rr * D_GRP + (cg + 1) * 128] = (
                                buf[pl.ds(rr, TM // dil, stride=dil), :].astype(BF))

    row = pl.BlockSpec((TM, d), lambda m: (m, 0))
    ex = pl.BlockSpec((None, 1, d), lambda m: (m // per, 0, 0))
    return _call(
        body, "qkv_proj", (t // TM,),
        [row, pl.BlockSpec((1, d), lambda m: (0, 0)), ex, ex, _whole(w_in)],
        [row, pl.BlockSpec((6, TM, D_GRP), lambda m: (0, m, 0))]
        + [pl.BlockSpec((3, TM // dil, dil * D_GRP), lambda m: (0, m, 0)) for dil in dils],
        [_sds((t, d), BF), _sds((6, t, D_GRP), BF)] + [_sds((3, t // dil, dil * D_GRP), BF) for dil in dils],
        (x, g, sc, sh, w_in), scratch=[pltpu.VMEM((TM, 128), F32)], carry=carry)


def _col_pieces(j, wc):
    out, off = [], 0
    while off < wc:
        a, lc = divmod(j * wc + off, D_GRP)
        width = min(D_GRP - lc, wc - off)
        out.append((a, lc, off, width))
        off += width
    return out


def _chip_cols(g6_ref, j, wc):
    return jnp.concatenate([g6_ref[a, :, lc:lc + width] for a, lc, _, width in _col_pieces(j, wc)], axis=1)


def _mix_out(on_sb, on_dil, w_out, x, gt, seq):
    t, d = x.shape
    per = seq // TM

    def body(a_ref, b_ref, w_ref, x_ref, gt_ref, t_ref, xo_ref):
        tv = _nn(a_ref[...], w_ref[0:D_GRP, :]) + _nn(b_ref[...], w_ref[D_GRP:2 * D_GRP, :])
        t_ref[...] = tv.astype(BF)
        xo_ref[...] = x_ref[...] + gt_ref[...] * tv

    row = pl.BlockSpec((TM, d), lambda m: (m, 0))
    half = pl.BlockSpec((TM, D_GRP), lambda m: (m, 0))
    return pl.pallas_call(
        body, name="mix_out", grid=(t // TM,),
        in_specs=[half, half, pl.BlockSpec((2 * D_GRP, d), lambda m: (0, 0)), row,
                  pl.BlockSpec((None, 1, d), lambda m: (m // per, 0, 0))],
        out_specs=[row, row],
        out_shape=[_sds((t, d), BF), _sds((t, d), F32)],
        compiler_params=_cp(1))(on_sb, on_dil, w_out, x, gt)


def _sb_masks():
    lane = lax.broadcasted_iota(jnp.int32, (1, 2 * HEAD_DIM), 1)
    hm0 = lane < HEAD_DIM
    rel = lax.broadcasted_iota(jnp.int32, (TQ, KB), 0) - lax.broadcasted_iota(jnp.int32, (TQ, KB), 1)
    kr = lax.broadcasted_iota(jnp.int32, (KB, KB), 0)
    kc = lax.broadcasted_iota(jnp.int32, (KB, KB), 1)
    return hm0, rel, kr, kc


def _stack_pair(x, hm0):
    zero = jnp.zeros_like(x)
    return jnp.concatenate([jnp.where(hm0, x, zero), jnp.where(hm0, zero, x)], axis=0)


def _headnorm_pair(o, gv, hm0):
    o2 = o * o
    ms0 = jnp.sum(jnp.where(hm0, o2, 0.0), axis=-1, keepdims=True) * (1.0 / HEAD_DIM)
    ms1 = jnp.sum(jnp.where(hm0, 0.0, o2), axis=-1, keepdims=True) * (1.0 / HEAD_DIM)
    r = jnp.where(hm0, lax.rsqrt(ms0 + EPS), lax.rsqrt(ms1 + EPS))
    return (o * r) * gv


SB_DEAD = -104.0


def _alive(c_l):
    return (jnp.max(c_l) > SB_DEAD).astype(jnp.int32)


def _sb_fwd(qkv6, g_sb, nb, seq, carry=None):
    nq = seq // TQ

    def body(q_ref, k_ref, v_ref, g_ref, o_ref, on_ref):
        qi = pl.program_id(2)
        hm0, rel, kr, kc = _sb_masks()
        upper = (kr > kc).astype(BF)
        qs = _stack_pair(q_ref[...], hm0)
        causal2 = jnp.concatenate([rel, rel], axis=0) > 0

        def block(kj, causal, c_l, acc):
            ks = pl.multiple_of(kj * KB, KB)
            z = _nt(qs, k_ref[pl.ds(ks, KB), :]) * SCALE
            sp = _softplus(z)
            ln = -sp if causal is None else jnp.where(causal, -sp, 0.0)
            suf = _nn2(ln, upper)
            w = jnp.exp((z - sp) + (suf + c_l))
            if causal is not None:
                w = jnp.where(causal, w, 0.0)
            return c_l + (suf[:, 0:1] + ln[:, 0:1]), acc + _nn(w.astype(BF), v_ref[pl.ds(ks, KB), :])

        c_l, acc = block(qi, causal2, jnp.zeros((2 * TQ, 1), F32), jnp.zeros((2 * TQ, 2 * HEAD_DIM), F32))

        def cond(carry):
            return jnp.logical_and(carry[0] <= qi, carry[1] > 0)

        def kbody(carry):
            it, _, c_l, acc = carry
            c_l, acc = block(qi - it, None, c_l, acc)
            return it + 1, _alive(c_l), c_l, acc

        acc = lax.while_loop(cond, kbody, (jnp.int32(1), _alive(c_l), c_l, acc))[3]
        o = jnp.where(hm0, acc[:TQ], acc[TQ:])
        o_ref[...] = o.astype(BF)
        on_ref[...] = _headnorm_pair(o, g_ref[...], hm0).astype(BF)

    w = 2 * HEAD_DIM
    full = lambda i: pl.BlockSpec((None, None, seq, w), lambda b, hp, q: (i, b, 0, hp))
    qblk = pl.BlockSpec((None, None, TQ, w), lambda b, hp, q: (0, b, q, hp))
    oblk = pl.BlockSpec((None, TQ, w), lambda b, hp, q: (b, q, hp))
    return _call(
        body, "sb_fwd", (nb, N_HEADS // 2, nq),
        [qblk, full(1), full(2), pl.BlockSpec((1, w), lambda b, hp, q: (0, hp))],
        [oblk, oblk],
        [_sds((nb, seq, D_GRP), BF), _sds((nb, seq, D_GRP), BF)],
        (qkv6, qkv6, qkv6, g_sb), carry=carry)


def _sb_bwd(qkv6, do, nb, seq, carry=None):
    nq = seq // TQ
    nk = seq // KB

    def body(q_ref, k_ref, v_ref, do_ref, out_ref, dk_acc, dv_acc, g_st, s_st):
        qi = pl.program_id(2)
        hm0, rel, kr, kc = _sb_masks()
        upper = (kr > kc).astype(BF)
        lower = (kr < kc).astype(BF)

        @pl.when(qi == 0)
        def _():
            dk_acc[...] = jnp.zeros_like(dk_acc)
            dv_acc[...] = jnp.zeros_like(dv_acc)

        qs = _stack_pair(q_ref[...], hm0)
        dos = _stack_pair(do_ref[...], hm0).astype(BF)
        causal2 = jnp.concatenate([rel, rel], axis=0) > 0

        def weights(kj, causal, c_l):
            ks = pl.multiple_of(kj * KB, KB)
            vb = v_ref[pl.ds(ks, KB), :]
            z = _nt(qs, k_ref[pl.ds(ks, KB), :]) * SCALE
            sp = _softplus(z)
            ln = -sp if causal is None else jnp.where(causal, -sp, 0.0)
            suf = _nn2(ln, upper)
            lsz = z - sp
            w = jnp.exp(lsz + (suf + c_l))
            if causal is not None:
                w = jnp.where(causal, w, 0.0)
            g_st[kj] = w * _nt(dos, vb)
            s_st[kj] = jnp.exp(lsz)
            dv_acc[pl.ds(ks, KB), :] += _tn(w.astype(BF), dos)
            return c_l + (suf[:, 0:1] + ln[:, 0:1])

        zc = jnp.zeros((2 * TQ, 1), F32)
        c_l = weights(qi, causal2, zc)

        def acond(carry):
            return jnp.logical_and(carry[0] <= qi, carry[1] > 0)

        def abody(carry):
            c_l = weights(qi - carry[0], None, carry[2])
            return carry[0] + 1, _alive(c_l), c_l

        n_used = lax.while_loop(acond, abody, (jnp.int32(1), _alive(c_l), c_l))[0]

        def grads(kj, causal, c_g, dq):
            ks = pl.multiple_of(kj * KB, KB)
            kb = k_ref[pl.ds(ks, KB), :]
            g = g_st[kj]
            sig = s_st[kj]
            pre = _nn(g.astype(BF), lower)
            dz = g * (1.0 - sig) - sig * (pre + c_g)
            if causal is not None:
                dz = jnp.where(causal, dz, 0.0)
            dzb = (dz * SCALE).astype(BF)
            dk_acc[pl.ds(ks, KB), :] += _tn(dzb, qs)
            return c_g + (pre[:, KB - 1:KB] + g[:, KB - 1:KB]), dq + _nn(dzb, kb)

        c_g, dq = lax.fori_loop(qi - n_used + 1, qi, lambda kj, cr: grads(kj, None, *cr),
                                (zc, jnp.zeros((2 * TQ, 2 * HEAD_DIM), F32)))
        _, dq = grads(qi, causal2, c_g, dq)
        dq = jnp.where(hm0, dq[:TQ], dq[TQ:])
        out_ref[0, pl.ds(pl.multiple_of(qi * TQ, TQ), TQ), :] = dq.astype(BF)

        @pl.when(qi == nq - 1)
        def _():
            out_ref[1] = dk_acc[...].astype(BF)
            out_ref[2] = dv_acc[...].astype(BF)

    w = 2 * HEAD_DIM
    full = lambda i: pl.BlockSpec((None, None, seq, w), lambda b, hp, q: (i, b, 0, hp))
    qblk = pl.BlockSpec((None, None, TQ, w), lambda b, hp, q: (0, b, q, hp))
    oblk = pl.BlockSpec((None, TQ, w), lambda b, hp, q: (b, q, hp))
    return _call(
        body, "sb_bwd", (nb, N_HEADS // 2, nq),
        [qblk, full(1), full(2), oblk],
        [pl.BlockSpec((3, None, seq, w), lambda b, hp, q: (0, b, 0, hp))],
        [_sds((6, nb, seq, D_GRP), BF)], (qkv6, qkv6, qkv6, do),
        scratch=[pltpu.VMEM((seq, w), F32), pltpu.VMEM((seq, w), F32),
                 pltpu.VMEM((nk, 2 * TQ, KB), F32), pltpu.VMEM((nk, 2 * TQ, KB), F32)],
        carry=carry)


def _t5_bucket(n):
    max_exact = N_BUCKETS // 2
    nf = np.maximum(n, 1).astype(np.float32)
    large = max_exact + (np.log(nf / max_exact) / math.log(MAX_DISTANCE / max_exact)
                         * (N_BUCKETS - max_exact)).astype(np.int32)
    large = np.minimum(large, N_BUCKETS - 1)
    return np.where(n < max_exact, n, large).astype(np.int32)


def _bucket_map(dilation):
    step = BLOCK + np.arange(BLOCK)[:, None] - np.arange(2 * BLOCK)[None, :]
    return _t5_bucket(np.clip(step, 0, N_STEPS) * dilation)


GRP_HEADS = 4
GRP_W = GRP_HEADS * HEAD_DIM


def _dil_masks():
    lane = lax.broadcasted_iota(jnp.int32, (1, GRP_W), 1)
    heads = [jnp.logical_and(lane >= HEAD_DIM * i, lane < HEAD_DIM * (i + 1)) for i in range(GRP_HEADS)]
    iq = jnp.bitwise_and(lax.broadcasted_iota(jnp.int32, (GRP_HEADS * BLOCK, BLOCK), 0), BLOCK - 1)
    ik = lax.broadcasted_iota(jnp.int32, (GRP_HEADS * BLOCK, BLOCK), 1)
    return heads, ik <= iq, ik >= iq


def _stack_heads(x, heads):
    zero = jnp.zeros_like(x)
    return jnp.concatenate([jnp.where(hm, x, zero) for hm in heads], axis=0)


def _unstack_heads(xs, heads):
    out = xs[0:BLOCK]
    for i in range(1, GRP_HEADS):
        out = jnp.where(heads[i], xs[i * BLOCK:(i + 1) * BLOCK], out)
    return out


def _dil_rows(n):
    rs = pl.multiple_of(n * BLOCK, BLOCK)
    ps = pl.multiple_of(jnp.maximum(n - 1, 0) * BLOCK, BLOCK)
    return pl.ds(rs, BLOCK), pl.ds(ps, BLOCK)


def _dil_probs(qs, kc, kp, b_ref, gi, valid_c, valid_p):
    rows = slice(gi * GRP_HEADS * BLOCK, (gi + 1) * GRP_HEADS * BLOCK)
    zc = _nt(qs, kc) * SCALE + b_ref[rows, BLOCK:2 * BLOCK]
    zp = _nt(qs, kp) * SCALE + b_ref[rows, 0:BLOCK]
    zc = jnp.where(valid_c, zc, NEG_INF)
    zp = jnp.where(valid_p, zp, NEG_INF)
    m = jnp.maximum(jnp.max(zc, axis=-1, keepdims=True), jnp.max(zp, axis=-1, keepdims=True))
    ec = jnp.exp(zc - m)
    ep = jnp.exp(zp - m)
    den = jnp.sum(ec, axis=-1, keepdims=True) + jnp.sum(ep, axis=-1, keepdims=True)
    return ec, ep, den, m


def _dil_fwd(qkv6r, base, bias, nb, sub_len, dilation):
    n_blk = sub_len // BLOCK

    def body(q_ref, k_ref, v_ref, b_ref, o_ref, l_ref):
        heads, valid_c, valid_p0 = _dil_masks()

        def nbody(n, carry):
            cur, prev = _dil_rows(n)
            valid_p = jnp.logical_and(valid_p0, n > 0)
            for gi in range(N_HEADS // GRP_HEADS):
                lanes = slice(gi * GRP_W, (gi + 1) * GRP_W)
                qs = _stack_heads(q_ref[cur, lanes], heads)
                ec, ep, den, m = _dil_probs(qs, k_ref[cur, lanes], k_ref[prev, lanes], b_ref, gi, valid_c, valid_p)
                o = (_nn(ec.astype(BF), v_ref[cur, lanes]) + _nn(ep.astype(BF), v_ref[prev, lanes])) / den
                o_ref[cur, lanes] = _unstack_heads(o, heads).astype(BF)
                l_ref[cur, lanes] = _unstack_heads(jnp.broadcast_to(m + jnp.log(den), o.shape), heads)
            return carry

        lax.fori_loop(0, n_blk, nbody, 0)

    seqblk = lambda i: pl.BlockSpec((None, None, sub_len, D_GRP), lambda b, r: (i, b, 0, r))
    oblk = pl.BlockSpec((None, sub_len, D_GRP), lambda b, r: (b, 0, r))
    shp = _sds((nb, sub_len, dilation * D_GRP), F32)
    return pl.pallas_call(
        body, name="dil_fwd_%d" % dilation, grid=(nb, dilation),
        in_specs=[seqblk(base), seqblk(base + 1), seqblk(base + 2), _whole(bias)],
        out_specs=[oblk, oblk], out_shape=[_sds(shp.shape, BF), shp],
        compiler_params=_cp(2))(qkv6r, qkv6r, qkv6r, bias)


def _dil_bwd(qkv6r, base, bias, do_c, dd_c, nb, sub_len, dilation, carry=None):
    n_blk = sub_len // BLOCK

    def body(q_ref, k_ref, v_ref, b_ref, do_ref, dd_ref, out_ref, a_ref, dk_acc, dv_acc):
        heads, valid_c, valid_p0 = _dil_masks()
        first = jnp.logical_and(pl.program_id(0) == 0, pl.program_id(1) == 0)

        @pl.when(first)
        def _():
            a_ref[...] = jnp.zeros_like(a_ref)

        dk_acc[...] = jnp.zeros_like(dk_acc)
        dv_acc[...] = jnp.zeros_like(dv_acc)

        def nbody(n, carry):
            cur, prev = _dil_rows(n)
            valid_p = jnp.logical_and(valid_p0, n > 0)
            for gi in range(N_HEADS // GRP_HEADS):
                lanes = slice(gi * GRP_W, (gi + 1) * GRP_W)
                kc, kp = k_ref[cur, lanes], k_ref[prev, lanes]
                vc, vp = v_ref[cur, lanes], v_ref[prev, lanes]
                qs = _stack_heads(q_ref[cur, lanes], heads)
                dos = _stack_heads(do_ref[cur, lanes], heads).astype(BF)
                dds = jnp.sum(_stack_heads(dd_ref[cur, lanes], heads), axis=-1, keepdims=True) * (1.0 / HEAD_DIM)
                ec, ep, den, _ = _dil_probs(qs, kc, kp, b_ref, gi, valid_c, valid_p)
                inv = 1.0 / den
                pc = ec * inv
                pp = ep * inv
                dzc = pc * (_nt(dos, vc) + dds)
                dzp = pp * (_nt(dos, vp) + dds)
                rows = slice(gi * GRP_HEADS * BLOCK, (gi + 1) * GRP_HEADS * BLOCK)
                a_ref[rows, BLOCK:2 * BLOCK] += dzc
                a_ref[rows, 0:BLOCK] += dzp
                dzcb = (dzc * SCALE).astype(BF)
                dzpb = (dzp * SCALE).astype(BF)
                out_ref[0, cur, lanes] = _unstack_heads(_nn(dzcb, kc) + _nn(dzpb, kp), heads).astype(BF)
                dk_acc[cur, lanes] += _tn(dzcb, qs)
                dk_acc[prev, lanes] += _tn(dzpb, qs)
                dv_acc[cur, lanes] += _tn(pc.astype(BF), dos)
                dv_acc[prev, lanes] += _tn(pp.astype(BF), dos)
            return carry

        lax.fori_loop(0, n_blk, nbody, 0)
        out_ref[1] = dk_acc[...].astype(BF)
        out_ref[2] = dv_acc[...].astype(BF)

    seqblk = lambda i: pl.BlockSpec((None, None, sub_len, D_GRP), lambda b, r: (i, b, 0, r))
    oblk = pl.BlockSpec((None, sub_len, D_GRP), lambda b, r: (b, 0, r))
    return _call(
        body, "dil_bwd_%d" % dilation, (nb, dilation),
        [seqblk(base), seqblk(base + 1), seqblk(base + 2), _whole(bias), oblk, oblk],
        [pl.BlockSpec((3, None, sub_len, D_GRP), lambda b, r: (0, b, 0, r)),
         pl.BlockSpec((N_HEADS * BLOCK, 2 * BLOCK), lambda b, r: (0, 0))],
        [_sds((3, nb, sub_len, dilation * D_GRP), BF), _sds((N_HEADS * BLOCK, 2 * BLOCK), F32)],
        (qkv6r, qkv6r, qkv6r, bias, do_c, dd_c),
        scratch=[pltpu.VMEM((sub_len, D_GRP), F32)] * 2, carry=carry)


def _group_ones():
    idx = np.arange(D_GRP) // HEAD_DIM
    return jnp.asarray((idx[:, None] == idx[None, :]).astype(np.float32), dtype=BF)


def _dil_alphas(l1, l4, l16):
    mx = jnp.maximum(jnp.maximum(l1, l4), l16)
    e1 = jnp.exp(l1 - mx)
    e4 = jnp.exp(l4 - mx)
    e16 = jnp.exp(l16 - mx)
    den = e1 + e4 + e16
    return e1 / den, e4 / den, e16 / den


def _residue_spec(dil):
    return pl.BlockSpec((TM // dil, dil * D_GRP), lambda m: (m, 0))


def _from_residue(src, dil, cg, buf):
    if dil == 1:
        return src[:, cg * 128:(cg + 1) * 128].astype(F32)
    for r in range(dil):
        buf[pl.ds(r, TM // dil, stride=dil), :] = (
            src[:, r * D_GRP + cg * 128:r * D_GRP + (cg + 1) * 128].astype(F32))
    return buf[...]


def _to_residue(dst, dil, cg, buf, val):
    if dil == 1:
        dst[:, cg * 128:(cg + 1) * 128] = val.astype(dst.dtype)
        return
    buf[...] = val
    for r in range(dil):
        dst[:, r * D_GRP + cg * 128:r * D_GRP + (cg + 1) * 128] = (
            buf[pl.ds(r, TM // dil, stride=dil), :].astype(dst.dtype))


def _pair_sum(x, hm0):
    s0 = jnp.sum(jnp.where(hm0, x, 0.0), axis=-1, keepdims=True)
    s1 = jnp.sum(jnp.where(hm0, 0.0, x), axis=-1, keepdims=True)
    return jnp.where(hm0, s0, s1)


def _dil_comb(os, ls, g_dil):
    t = os[0].shape[0]
    dils = [dil for _, dil in DIL_CONFIGS]

    def body(o1, l1, o4, l4, o16, l16, g_ref, o_ref, on_ref, b0, b1, b2, b3):
        hm0 = lax.broadcasted_iota(jnp.int32, (1, 128), 1) < HEAD_DIM
        for cg in range(D_GRP // 128):
            lanes = slice(cg * 128, (cg + 1) * 128)
            ov = [_from_residue(src, dil, cg, buf) for src, dil, buf in zip((o1, o4, o16), dils, (None, b0, b1))]
            lv = [_from_residue(src, dil, cg, buf) for src, dil, buf in zip((l1, l4, l16), dils, (None, b2, b3))]
            a1, a4, a16 = _dil_alphas(*lv)
            o = a1 * ov[0] + a4 * ov[1] + a16 * ov[2]
            o_ref[:, lanes] = o.astype(BF)
            on_ref[:, lanes] = _headnorm_pair(o, g_ref[:, lanes], hm0).astype(BF)

    blk = pl.BlockSpec((TM, D_GRP), lambda m: (m, 0))
    specs = [_residue_spec(dil) for dil in dils for _ in range(2)]
    return pl.pallas_call(
        body, name="dil_comb", grid=(t // TM,),
        in_specs=specs + [pl.BlockSpec((1, D_GRP), lambda m: (0, 0))],
        out_specs=[blk, blk],
        out_shape=[_sds((t, D_GRP), BF), _sds((t, D_GRP), BF)],
        scratch_shapes=[pltpu.VMEM((TM, 128), F32)] * 4,
        compiler_params=_cp(1))(os[0], ls[0], os[1], ls[1], os[2], ls[2], g_dil)


def _dil_comb_bwd(do, os, ls):
    t = do.shape[0]
    dils = [dil for _, dil in DIL_CONFIGS]

    def body(do_ref, o1, l1, o4, l4, o16, l16, d1, d4, d16, e1, e4, e16, b0, b1, b2, b3):
        hm0 = lax.broadcasted_iota(jnp.int32, (1, 128), 1) < HEAD_DIM
        for cg in range(D_GRP // 128):
            dov = do_ref[:, cg * 128:(cg + 1) * 128].astype(F32)
            ov = [_from_residue(src, dil, cg, buf) for src, dil, buf in zip((o1, o4, o16), dils, (None, b0, b1))]
            lv = [_from_residue(src, dil, cg, buf) for src, dil, buf in zip((l1, l4, l16), dils, (None, b2, b3))]
            al = _dil_alphas(*lv)
            sbar = al[0] * _pair_sum(dov * ov[0], hm0)
            for a_c, o_c in zip(al[1:], ov[1:]):
                sbar = sbar + a_c * _pair_sum(dov * o_c, hm0)
            for a_c, dil, dref, eref in zip(al, dils, (d1, d4, d16), (e1, e4, e16)):
                _to_residue(dref, dil, cg, b0, a_c * dov)
                _to_residue(eref, dil, cg, b1, -a_c * sbar)

    specs = [_residue_spec(dil) for dil in dils]
    return pl.pallas_call(
        body, name="dil_comb_bwd", grid=(t // TM,),
        in_specs=[pl.BlockSpec((TM, D_GRP), lambda m: (m, 0))] + [sp for sp in specs for _ in range(2)],
        out_specs=specs + specs,
        out_shape=[_sds((t // dil, dil * D_GRP), BF) for dil in dils]
        + [_sds((t // dil, dil * D_GRP), F32) for dil in dils],
        scratch_shapes=[pltpu.VMEM((TM, 128), F32)] * 4,
        compiler_params=_cp(1))(do, os[0], ls[0], os[1], ls[1], os[2], ls[2])


def _dqkv_dil_sum(ds, dqkv6):
    t = dqkv6.shape[1]
    dils = [dil for _, dil in DIL_CONFIGS]

    def body(*refs):
        srcs, o_ref, acc = refs[:len(dils)], refs[len(dils) + 1], refs[len(dils) + 2]
        for a in range(3):
            for cg in range(D_GRP // 128):
                for src, dil in zip(srcs, dils):
                    for r in range(dil):
                        part = src[a, :, r * D_GRP + cg * 128:r * D_GRP + (cg + 1) * 128].astype(F32)
                        rows = pl.ds(r, TM // dil, stride=dil) if dil > 1 else slice(None)
                        if dil == dils[0]:
                            acc[rows, :] = part
                        else:
                            acc[rows, :] += part
                o_ref[a, :, cg * 128:(cg + 1) * 128] = acc[...].astype(BF)

    return pl.pallas_call(
        body, name="dqkv_dil_sum", grid=(t // TM,),
        in_specs=[pl.BlockSpec((3, TM // dil, dil * D_GRP), lambda m: (0, m, 0)) for dil in dils]
        + [pl.BlockSpec(memory_space=pl.ANY)],
        out_specs=pl.BlockSpec((3, TM, D_GRP), lambda m: (1, m, 0)),
        out_shape=_sds((6, t, D_GRP), BF), input_output_aliases={len(dils): 0},
        scratch_shapes=[pltpu.VMEM((TM, 128), F32)],
        compiler_params=_cp(1))(*ds, dqkv6)


def _relbias_grad(a_all, onehot):
    def body(a_ref, oh_ref, o_ref):
        acc = jnp.zeros((N_HEADS, N_BUCKETS), F32)
        for c in range(len(DIL_CONFIGS)):
            av = a_ref[c]
            hi = av.astype(BF)
            lo = (av - hi.astype(F32)).astype(BF)
            acc = acc + _nt(hi, oh_ref[c]) + _nt(lo, oh_ref[c])
        o_ref[...] = acc

    return pl.pallas_call(body, name="relbias_grad", out_shape=_sds((N_HEADS, N_BUCKETS), F32),
                          compiler_params=_cp())(a_all, onehot)


def _headnorm_bwd(dn, o, gv, mv):
    ms = _nn2(o * o, mv) * (1.0 / HEAD_DIM)
    r = lax.rsqrt(ms + EPS)
    nrm = o * r
    dg = jnp.sum(dn * nrm, axis=0, keepdims=True)
    dnn = dn * gv
    do = r * (dnn - nrm * (_nn2(dnn * nrm, mv) * (1.0 / HEAD_DIM)))
    return do, dg


def _mix_bwd_out(dx, gt, tv, w_out, o_sb, o_dil, on_sb, on_dil, g_sb, g_dil, ones_g, seq, carry=None):
    t, d = dx.shape
    per = seq // TM
    nb = t // seq

    def body(dx_ref, gt_ref, t_ref, w_ref, osb, odl, onsb, ondl, gsb, gdl, m_ref,
             dosb, dodl, dgt_ref, dgsb, dgdl, dw_ref):
        m = pl.program_id(0)
        dxv = dx_ref[...]
        dt = (gt_ref[...] * dxv).astype(BF)
        _acc_rows(dgt_ref, jnp.sum(dxv * t_ref[...].astype(F32), axis=0, keepdims=True), m % per == 0)
        mv = m_ref[...]
        don_sb = _nt(dt, w_ref[0:D_GRP, :])
        don_dl = _nt(dt, w_ref[D_GRP:2 * D_GRP, :])
        do1, dg1 = _headnorm_bwd(don_sb, osb[...].astype(F32), gsb[...], mv)
        do2, dg2 = _headnorm_bwd(don_dl, odl[...].astype(F32), gdl[...], mv)
        dosb[...] = do1.astype(BF)
        dodl[...] = do2.astype(BF)
        _acc_rows(dgsb, dg1, m == 0)
        _acc_rows(dgdl, dg2, m == 0)
        p1 = _tn(onsb[...], dt)
        p2 = _tn(ondl[...], dt)

        @pl.when(m == 0)
        def _():
            dw_ref[0:D_GRP, :] = p1
            dw_ref[D_GRP:2 * D_GRP, :] = p2

        @pl.when(m != 0)
        def _():
            dw_ref[0:D_GRP, :] += p1
            dw_ref[D_GRP:2 * D_GRP, :] += p2

    row = pl.BlockSpec((TM, d), lambda m: (m, 0))
    half = pl.BlockSpec((TM, D_GRP), lambda m: (m, 0))
    ex = pl.BlockSpec((None, 1, d), lambda m: (m // per, 0, 0))
    gvec = pl.BlockSpec((1, D_GRP), lambda m: (0, 0))
    wblk = pl.BlockSpec((2 * D_GRP, d), lambda m: (0, 0))
    return _call(
        body, "mix_bwd_out", (t // TM,),
        [row, ex, row, wblk, half, half, half, half, gvec, gvec, pl.BlockSpec((D_GRP, D_GRP), lambda m: (0, 0))],
        [half, half, ex, gvec, gvec, wblk],
        [_sds((t, D_GRP), BF), _sds((t, D_GRP), BF), _sds((nb, 1, d), F32),
         _sds((1, D_GRP), F32), _sds((1, D_GRP), F32), _sds((2 * D_GRP, d), F32)],
        (dx, gt, tv, w_out, o_sb, o_dil, on_sb, on_dil, g_sb, g_dil, ones_g), carry=carry)


def _dw_in(h, dqkv6, carry=None):
    t, d = h.shape
    wc = 6 * D_GRP // N_CHIPS

    def body(h_ref, g_ref, o_ref):
        kt = pl.program_id(0)
        hv = h_ref[...]
        for j in range(N_CHIPS):
            p = _tn(hv, _chip_cols(g_ref, j, wc))

            @pl.when(kt == 0)
            def _(p=p, j=j):
                o_ref[j, 0] = p

            @pl.when(kt != 0)
            def _(p=p, j=j):
                o_ref[j, 0] += p

    return _call(
        body, "dw_in", (t // TK_W,),
        [pl.BlockSpec((TK_W, d), lambda kt: (kt, 0)), pl.BlockSpec((6, TK_W, D_GRP), lambda kt: (0, kt, 0))],
        [pl.BlockSpec((N_CHIPS, 1, d, wc), lambda kt: (0, 0, 0, 0))],
        [_sds((N_CHIPS, 1, d, wc), F32)], (h, dqkv6), carry=carry)


def _mix_bwd_dh(dqkv6, w_in, x, g, sc, dxo, seq, carry=None):
    _, t, _ = dqkv6.shape
    d = x.shape[-1]
    wc = w_in.shape[-1]
    per = seq // TM
    nb = t // seq

    def body(g6_ref, w_ref, x_ref, g_ref, sc_ref, dxo_ref, dx_ref, dsh_ref, dsc_ref, dg_ref):
        m = pl.program_id(0)
        dh = _nt(_chip_cols(g6_ref, 0, wc), w_ref[0, 0])
        for j in range(1, N_CHIPS):
            dh = dh + _nt(_chip_cols(g6_ref, j, wc), w_ref[j, 0])
        dx, dsh, dsc, dg = _modnorm_bwd_tile(dh, x_ref[...], g_ref[...], sc_ref[...], dxo_ref[...])
        dx_ref[...] = dx
        _acc_rows(dsh_ref, dsh, m % per == 0)
        _acc_rows(dsc_ref, dsc, m % per == 0)
        _acc_rows(dg_ref, dg, m == 0)

    row = pl.BlockSpec((TM, d), lambda m: (m, 0))
    ex = pl.BlockSpec((None, 1, d), lambda m: (m // per, 0, 0))
    vec = pl.BlockSpec((1, d), lambda m: (0, 0))
    return _call(
        body, "mix_bwd_dh", (t // TM,),
        [pl.BlockSpec((6, TM, D_GRP), lambda m: (0, m, 0)), _whole(w_in), row, vec, ex, row],
        [row, ex, ex, vec],
        [_sds((t, d), F32), _sds((nb, 1, d), F32), _sds((nb, 1, d), F32), _sds((1, d), F32)],
        (dqkv6, w_in, x, g, sc, dxo), carry=carry)


def _ffn_down_loss(s, wd, x, gt, seq, coef, g, target):
    _, t, fs = s.shape
    d = x.shape[-1]
    per = seq // TM
    steps = t // TM

    def body(s_ref, w_ref, x_ref, gt_ref, g_ref, t_ref, f_ref, dx_ref, dg_ref, loss_ref, lacc):
        m = pl.program_id(0)
        f = _nn(s_ref[0], w_ref[0, 0])
        for j in range(1, N_CHIPS):
            f = f + _nn(s_ref[j], w_ref[j, 0])
        f_ref[...] = f.astype(BF)
        xv = x_ref[...] + (coef * gt_ref[...]) * f
        gv = g_ref[...]
        r = lax.rsqrt(jnp.mean(xv * xv, axis=-1, keepdims=True) + EPS)
        n = xv * r
        err = n * gv - t_ref[...]
        dy = err * (1.0 / d)
        _acc_rows(dg_ref, jnp.sum(dy * n, axis=0, keepdims=True), m == 0)
        dn = dy * gv
        dx_ref[...] = r * (dn - n * jnp.mean(dn * n, axis=-1, keepdims=True))
        _acc_rows(lacc, jnp.sum(err * err, axis=0, keepdims=True), m == 0)

        @pl.when(m == steps - 1)
        def _():
            tot = jnp.sum(lacc[...], axis=-1, keepdims=True) * (0.5 / d)
            loss_ref[...] = jnp.broadcast_to(tot, (1, 128))

    row = pl.BlockSpec((TM, d), lambda m: (m, 0))
    vec = pl.BlockSpec((1, d), lambda m: (0, 0))
    return pl.pallas_call(
        body, name="ffn_down_loss", grid=(steps,),
        in_specs=[pl.BlockSpec((N_CHIPS, TM, fs), lambda m: (0, m, 0)), _whole(wd), row,
                  pl.BlockSpec((None, 1, d), lambda m: (m // per, 0, 0)), vec, row],
        out_specs=[row, row, vec, pl.BlockSpec((1, 128), lambda m: (0, 0))],
        out_shape=[_sds((t, d), BF), _sds((t, d), F32), _sds((1, d), F32), _sds((1, 128), F32)],
        scratch_shapes=[pltpu.VMEM((1, d), F32)],
        compiler_params=_cp(1))(s, wd, x, gt, g, target)


def _row_tile(rows, cols):
    best = rows
    for tr in range(8, rows + 1, 8):
        if rows % tr == 0 and tr * cols * 4 <= (1 << 20):
            best = tr
    if best * cols * 4 > (1 << 21):
        best = 8
    return best


def _adamw(w, g_arr, g_sel, m, v):
    rows, cols = w.shape
    tr = _row_tile(rows, cols)
    b1c = 1.0 - ADAM_B1 ** ADAM_STEP
    b2c = 1.0 - ADAM_B2 ** ADAM_STEP

    def body(w_ref, g_ref, m_ref, v_ref, go_ref, d_ref, mo_ref, vo_ref):
        gv = g_ref[...]
        mn = ADAM_B1 * m_ref[...] + (1.0 - ADAM_B1) * gv
        vn = ADAM_B2 * v_ref[...] + (1.0 - ADAM_B2) * (gv * gv)
        go_ref[...] = gv
        mo_ref[...] = mn
        vo_ref[...] = vn
        d_ref[...] = -ADAM_LR * ((mn / b1c) / (jnp.sqrt(vn / b2c) + ADAM_EPS) + ADAM_WD * w_ref[...])

    blk = pl.BlockSpec((tr, cols), lambda i: (i, 0))
    shp = _sds((rows, cols), F32)
    return pl.pallas_call(
        body, name="adamw", grid=(rows // tr,),
        in_specs=[blk, pl.BlockSpec((None, tr, cols), lambda i: (g_sel, i, 0)), blk, blk],
        out_specs=[blk] * 4, out_shape=[shp] * 4,
        compiler_params=_cp(1))(w, g_arr, m, v)


def _flip(v, bit):
    return 1 - v if bit else v


def _my_place():
    x, y, c = lax.axis_index("x"), lax.axis_index("y"), lax.axis_index("c")
    return x, y, c


class _Exchange:
    def __init__(self, operands, out_shape, aliases, sems, start, finish):
        self.operands, self.out_shape, self.aliases, self.sems = list(operands), list(out_shape), dict(aliases), list(sems)
        self.start, self.finish = start, finish


def _join(exchanges):
    exchanges = [e for e in exchanges if e is not None]
    if not exchanges:
        return None
    ops, outs, sems, aliases, spans = [], [], [], {}, []
    for e in exchanges:
        spans.append((len(ops), len(outs), len(sems), e))
        for i, j in e.aliases.items():
            aliases[len(ops) + i] = len(outs) + j
        ops += e.operands
        outs += e.out_shape
        sems += e.sems

    def run(which):
        def go(ins, res, sm):
            for io, oo, so, e in spans:
                getattr(e, which)(ins[io:io + len(e.operands)], res[oo:oo + len(e.out_shape)], sm[so:so + len(e.sems)])
        return go

    return _Exchange(ops, outs, aliases, sems, run("start"), run("finish"))


def _call(body, name, grid, in_specs, out_specs, out_shape, args, scratch=(), carry=None, io_alias=None):
    in_specs, out_specs, out_shape, scratch = list(in_specs), list(out_specs), list(out_shape), list(scratch)
    io_alias = dict(io_alias or {})
    if carry is None:
        return pl.pallas_call(body, name=name, grid=grid, in_specs=in_specs, out_specs=out_specs,
                              out_shape=out_shape, scratch_shapes=scratch, input_output_aliases=io_alias,
                              compiler_params=_cp(len(grid)))(*args)
    n_in, n_out, n_s = len(in_specs), len(out_specs), len(scratch)
    c_in, c_out = len(carry.operands), len(carry.out_shape)
    any_spec = pl.BlockSpec(memory_space=pl.ANY)

    def wrapped(*refs):
        ins, cins = refs[:n_in], refs[n_in:n_in + c_in]
        o0 = n_in + c_in
        outs, couts = refs[o0:o0 + n_out], refs[o0 + n_out:o0 + n_out + c_out]
        s0 = o0 + n_out + c_out
        scr, sems = refs[s0:s0 + n_s], refs[s0 + n_s:]
        first = pl.program_id(0) == 0
        last = pl.program_id(0) == grid[0] - 1
        for ax in range(1, len(grid)):
            first = jnp.logical_and(first, pl.program_id(ax) == 0)
            last = jnp.logical_and(last, pl.program_id(ax) == grid[ax] - 1)

        @pl.when(first)
        def _():
            carry.start(cins, couts, sems)

        body(*ins, *outs, *scr)

        @pl.when(last)
        def _():
            carry.finish(cins, couts, sems)

    return pl.pallas_call(
        wrapped, name=name, grid=grid, in_specs=in_specs + [any_spec] * c_in,
        out_specs=out_specs + [any_spec] * c_out, out_shape=out_shape + carry.out_shape,
        scratch_shapes=scratch + carry.sems,
        input_output_aliases={**io_alias, **{n_in + i: n_out + j for i, j in carry.aliases.items()}},
        compiler_params=_cp(len(grid)))(*args, *carry.operands)


def _whole_call(body, name, args, out_shape, scratch, carry=None):
    vm = pl.BlockSpec(memory_space=pltpu.VMEM)
    any_spec = pl.BlockSpec(memory_space=pl.ANY)
    out_shape, scratch = list(out_shape), list(scratch)
    n_in, n_out, n_s = len(args), len(out_shape), len(scratch)
    if carry is None:
        return pl.pallas_call(body, name=name, in_specs=[vm] * n_in, out_specs=[vm] * n_out, out_shape=out_shape,
                              scratch_shapes=scratch, compiler_params=_cp())(*args)
    c_in, c_out = len(carry.operands), len(carry.out_shape)

    def wrapped(*refs):
        ins, cins = refs[:n_in], refs[n_in:n_in + c_in]
        o0 = n_in + c_in
        outs, couts = refs[o0:o0 + n_out], refs[o0 + n_out:o0 + n_out + c_out]
        s0 = o0 + n_out + c_out
        scr, sems = refs[s0:s0 + n_s], refs[s0 + n_s:]
        carry.start(cins, couts, sems)
        body(*ins, *outs, *scr)
        carry.finish(cins, couts, sems)

    return pl.pallas_call(
        wrapped, name=name, in_specs=[vm] * n_in + [any_spec] * c_in, out_specs=[vm] * n_out + [any_spec] * c_out,
        out_shape=out_shape + carry.out_shape, scratch_shapes=scratch + carry.sems,
        input_output_aliases={n_in + i: n_out + j for i, j in carry.aliases.items()},
        compiler_params=_cp())(*args, *carry.operands)


def _alone(name, ex):
    any_spec = pl.BlockSpec(memory_space=pl.ANY)
    c_in, c_out = len(ex.operands), len(ex.out_shape)

    def body(*refs):
        ins, outs, sems = refs[:c_in], refs[c_in:c_in + c_out], refs[c_in + c_out:]
        ex.start(ins, outs, sems)
        ex.finish(ins, outs, sems)

    return pl.pallas_call(
        body, name=name, in_specs=[any_spec] * c_in, out_specs=[any_spec] * c_out, out_shape=ex.out_shape,
        scratch_shapes=ex.sems, input_output_aliases=ex.aliases, compiler_params=_cp())(*ex.operands)


def _ada_fwd(c_pad, w_ada, b_shard, carry=None):
    d = c_pad.shape[-1]
    cols = w_ada.shape[-1]
    chunk = 384

    def body(c_ref, w_ref, b_ref, call_ref, mod_ref, part, s1, r1, s2, r2):
        x, y, c = _my_place()
        dev = 4 * x + 2 * y + c
        chip = 2 * x + y
        call_ref[dev] = c_ref[...]

        def c_copy(k):
            px, py, pc = _flip(x, (k >> 2) & 1), _flip(y, (k >> 1) & 1), _flip(c, k & 1)
            return px, py, pc

        sends = []
        for k in range(1, N_DEV):
            px, py, pc = c_copy(k)
            cp = pltpu.make_async_remote_copy(src_ref=c_ref, dst_ref=call_ref.at[dev], send_sem=s1.at[k - 1],
                                              recv_sem=r1.at[k - 1], device_id=(px, py, pc), device_id_type=MESH)
            cp.start()
            sends.append(cp)
        for k in range(1, N_DEV):
            px, py, pc = c_copy(k)
            pltpu.make_async_remote_copy(src_ref=c_ref, dst_ref=call_ref.at[4 * px + 2 * py + pc],
                                         send_sem=s1.at[k - 1], recv_sem=r1.at[k - 1],
                                         device_id=(px, py, pc), device_id_type=MESH).wait_recv()
        for cp in sends:
            cp.wait_send()

        cs = call_ref[...].reshape(N_DEV * 8, d)
        sc = (cs * jax.nn.sigmoid(cs)).astype(BF)
        for n0 in range(0, cols, chunk):
            blk = _nn(sc, w_ref[:, n0:n0 + chunk].astype(BF)) + b_ref[:, n0:n0 + chunk]
            part[:, :, n0:n0 + chunk] = blk.reshape(N_DEV, 8, chunk)

        mod_ref[chip] = part[dev]
        sends = []
        for kk in range(1, N_CHIPS):
            px, py = _flip(x, (kk >> 1) & 1), _flip(y, kk & 1)
            cp = pltpu.make_async_remote_copy(src_ref=part.at[4 * px + 2 * py + c], dst_ref=mod_ref.at[chip],
                                              send_sem=s2.at[kk - 1], recv_sem=r2.at[kk - 1],
                                              device_id=(px, py, c), device_id_type=MESH)
            cp.start()
            sends.append(cp)
        for kk in range(1, N_CHIPS):
            px, py = _flip(x, (kk >> 1) & 1), _flip(y, kk & 1)
            pltpu.make_async_remote_copy(src_ref=part.at[dev], dst_ref=mod_ref.at[2 * px + py],
                                         send_sem=s2.at[kk - 1], recv_sem=r2.at[kk - 1],
                                         device_id=(px, py, c), device_id_type=MESH).wait_recv()
        for cp in sends:
            cp.wait_send()

    return _whole_call(
        body, "ada_fwd", (c_pad, w_ada, b_shard),
        [_sds((N_DEV, 8, d), F32), _sds((N_CHIPS, 8, cols), F32)],
        [pltpu.VMEM((N_DEV, 8, cols), F32),
         pltpu.SemaphoreType.DMA((N_DEV - 1,)), pltpu.SemaphoreType.DMA((N_DEV - 1,)),
         pltpu.SemaphoreType.DMA((N_CHIPS - 1,)), pltpu.SemaphoreType.DMA((N_CHIPS - 1,))], carry=carry)


def _ag_weights(bufs, kks=(1, 2, 3), relative=False):
    n, nk = len(bufs), len(kks)

    def half(b, which):
        hr = bufs[b].shape[2] // 2
        return pl.ds(pl.multiple_of(which * hr, 16), hr)

    def copies(outs, sems, b, i, kk):
        x, y, c = _my_place()
        chip = 2 * x + y
        px, py = _flip(x, (kk >> 1) & 1), _flip(y, kk & 1)
        mine, theirs = (0, kk) if relative else (chip, 2 * px + py)
        landing = kk if relative else chip
        k = nk * b + i
        send = pltpu.make_async_remote_copy(
            src_ref=outs[b].at[mine, :, half(b, c), :], dst_ref=outs[b].at[landing, :, half(b, c), :],
            send_sem=sems[0].at[k], recv_sem=sems[1].at[k], device_id=(px, py, c), device_id_type=MESH)
        got = outs[b].at[theirs, :, half(b, c), :]
        recv = pltpu.make_async_remote_copy(
            src_ref=got, dst_ref=got, send_sem=sems[0].at[k], recv_sem=sems[1].at[k],
            device_id=(px, py, c), device_id_type=MESH)
        fwd = pltpu.make_async_remote_copy(
            src_ref=got, dst_ref=got, send_sem=sems[2].at[k], recv_sem=sems[3].at[k],
            device_id=(x, y, 1 - c), device_id_type=MESH)
        other = outs[b].at[theirs, :, half(b, 1 - c), :]
        back = pltpu.make_async_remote_copy(
            src_ref=other, dst_ref=other, send_sem=sems[2].at[k], recv_sem=sems[3].at[k],
            device_id=(x, y, 1 - c), device_id_type=MESH)
        return send, recv, fwd, back

    def each(outs, sems):
        for b in range(n):
            for i, kk in enumerate(kks):
                yield copies(outs, sems, b, i, kk)

    def start(ins, outs, sems):
        for send, _, _, _ in each(outs, sems):
            send.start()

    def finish(ins, outs, sems):
        for _, recv, fwd, _ in each(outs, sems):
            recv.wait_recv()
            fwd.start()
        for send, _, fwd, back in each(outs, sems):
            back.wait_recv()
            send.wait_send()
            fwd.wait_send()

    return _Exchange(bufs, [_sds(s.shape, s.dtype) for s in bufs], {i: i for i in range(n)},
                     [pltpu.SemaphoreType.DMA((nk * n,))] * 4, start, finish)


def _rs_d2d(grads):
    n = len(grads)

    def copy(ins, outs, sems, b):
        x, y, c = _my_place()
        hr = grads[b].shape[2] // 2
        theirs = pl.ds(pl.multiple_of((1 - c) * hr, 8), hr)
        return pltpu.make_async_remote_copy(
            src_ref=ins[b].at[:, :, theirs, :], dst_ref=outs[b], send_sem=sems[0].at[b], recv_sem=sems[1].at[b],
            device_id=(x, y, 1 - c), device_id_type=MESH)

    def start(ins, outs, sems):
        for b in range(n):
            copy(ins, outs, sems, b).start()

    def finish(ins, outs, sems):
        for b in range(n):
            copy(ins, outs, sems, b).wait()

    return _Exchange(grads, [_sds(g.shape[:2] + (g.shape[2] // 2, g.shape[3]), F32) for g in grads], {},
                     [pltpu.SemaphoreType.DMA((n,))] * 2, start, finish)


def _add_halves(core, g, land):
    nchip, ng, rows, cols = g.shape
    hr = rows // 2
    tr = _row_tile(hr, cols)
    steps = hr // tr

    def body(core_ref, g_ref, l_ref, o_ref):
        del core_ref
        o_ref[...] = (g_ref[...] + l_ref[...]).astype(BF)

    return pl.pallas_call(
        body, name="add_halves",
        grid_spec=pltpu.PrefetchScalarGridSpec(
            num_scalar_prefetch=1, grid=(nchip, ng, steps),
            in_specs=[pl.BlockSpec((None, None, tr, cols), lambda j, a, i, cr: (j, a, cr[0] * steps + i, 0)),
                      pl.BlockSpec((None, None, tr, cols), lambda j, a, i, cr: (j, a, i, 0))],
            out_specs=pl.BlockSpec((None, None, tr, cols), lambda j, a, i, cr: (j, a, i, 0))),
        out_shape=_sds((nchip, ng, hr, cols), BF),
        compiler_params=_cp(3))(core, g, land)


def _rs_ici(parts, relative=False):
    n = len(parts)

    def copies(ins, outs, sems):
        x, y, c = _my_place()
        chip = 2 * x + y
        for b in range(n):
            for kk in range(1, N_CHIPS):
                px, py = _flip(x, (kk >> 1) & 1), _flip(y, kk & 1)
                k = 3 * b + kk - 1
                theirs, landing = (kk, kk) if relative else (2 * px + py, chip)
                send = pltpu.make_async_remote_copy(
                    src_ref=ins[b].at[theirs], dst_ref=outs[b].at[landing],
                    send_sem=sems[0].at[k], recv_sem=sems[1].at[k], device_id=(px, py, c), device_id_type=MESH)
                slot = outs[b].at[theirs]
                recv = pltpu.make_async_remote_copy(
                    src_ref=slot, dst_ref=slot, send_sem=sems[0].at[k], recv_sem=sems[1].at[k],
                    device_id=(px, py, c), device_id_type=MESH)
                yield send, recv

    def start(ins, outs, sems):
        for send, _ in copies(ins, outs, sems):
            send.start()

    def finish(ins, outs, sems):
        for send, recv in copies(ins, outs, sems):
            recv.wait_recv()
            send.wait_send()

    return _Exchange(parts, [_sds(p.shape, p.dtype) for p in parts], {},
                     [pltpu.SemaphoreType.DMA((3 * n,))] * 2, start, finish)


def _sum_chips(place, part, land, relative=False):
    nchip, ng, hr, cols = land.shape
    tr = _row_tile(hr, cols)
    steps = hr // tr

    def body(place_ref, p_ref, l1, l2, l3, o_ref):
        del place_ref
        o_ref[...] = ((p_ref[...].astype(F32) + l1[...].astype(F32)) + l2[...].astype(F32)) + l3[...].astype(F32)

    def slot(k):
        if relative:
            return pl.BlockSpec((None, None, tr, cols), lambda a, i, pr: (k, a, i, 0))
        return pl.BlockSpec((None, None, tr, cols), lambda a, i, pr: (jnp.bitwise_xor(pr[1], k), a, i, 0))

    return pl.pallas_call(
        body, name="sum_chips",
        grid_spec=pltpu.PrefetchScalarGridSpec(
            num_scalar_prefetch=1, grid=(ng, steps),
            in_specs=[slot(0), slot(1), slot(2), slot(3)],
            out_specs=pl.BlockSpec((None, tr, cols), lambda a, i, pr: (a, pr[0] * steps + i, 0))),
        out_shape=_sds((ng, 2 * hr, cols), F32),
        compiler_params=_cp(2))(place, part, land, land, land)


def _rs_final(bufs):
    n = len(bufs)

    def copy(outs, sems, b, which):
        x, y, c = _my_place()
        hr = bufs[b].shape[1] // 2
        rows = outs[b].at[:, pl.ds(pl.multiple_of((c if which == 0 else 1 - c) * hr, 8), hr), :]
        return pltpu.make_async_remote_copy(
            src_ref=rows, dst_ref=rows, send_sem=sems[0].at[b], recv_sem=sems[1].at[b],
            device_id=(x, y, 1 - c), device_id_type=MESH)

    def start(ins, outs, sems):
        for b in range(n):
            copy(outs, sems, b, 0).start()

    def finish(ins, outs, sems):
        for b in range(n):
            copy(outs, sems, b, 0).wait_send()
            copy(outs, sems, b, 1).wait_recv()

    return _Exchange(bufs, [_sds(h.shape, F32) for h in bufs], {i: i for i in range(n)},
                     [pltpu.SemaphoreType.DMA((n,))] * 2, start, finish)


def _small_sync(smalls, dmod_blk, c_all, carry=None):
    d = c_all.shape[-1]
    cols = dmod_blk.shape[-1]
    chunk = 384

    def body(sm_ref, dm_ref, c_ref, sum_ref, gw_ref, sm_all, dm_all, ssem, rsem):
        x, y, c = _my_place()
        dev = 4 * x + 2 * y + c
        chip = 2 * x + y
        sm_all[dev] = sm_ref[...]
        dm_all[dev] = dm_ref[chip]
        sends = []
        for k in range(1, N_DEV):
            px, py, pc = _flip(x, (k >> 2) & 1), _flip(y, (k >> 1) & 1), _flip(c, k & 1)
            a = pltpu.make_async_remote_copy(src_ref=sm_ref, dst_ref=sm_all.at[dev], send_sem=ssem.at[2 * (k - 1)],
                                             recv_sem=rsem.at[2 * (k - 1)], device_id=(px, py, pc),
                                             device_id_type=MESH)
            b = pltpu.make_async_remote_copy(src_ref=dm_ref.at[2 * px + py], dst_ref=dm_all.at[dev],
                                             send_sem=ssem.at[2 * (k - 1) + 1], recv_sem=rsem.at[2 * (k - 1) + 1],
                                             device_id=(px, py, pc), device_id_type=MESH)
            a.start()
            b.start()
            sends += [a, b]
        for k in range(1, N_DEV):
            px, py, pc = _flip(x, (k >> 2) & 1), _flip(y, (k >> 1) & 1), _flip(c, k & 1)
            pdev = 4 * px + 2 * py + pc
            pltpu.make_async_remote_copy(src_ref=sm_ref, dst_ref=sm_all.at[pdev], send_sem=ssem.at[2 * (k - 1)],
                                         recv_sem=rsem.at[2 * (k - 1)], device_id=(px, py, pc),
                                         device_id_type=MESH).wait_recv()
            pltpu.make_async_remote_copy(src_ref=dm_ref.at[chip], dst_ref=dm_all.at[pdev],
                                         send_sem=ssem.at[2 * (k - 1) + 1], recv_sem=rsem.at[2 * (k - 1) + 1],
                                         device_id=(px, py, pc), device_id_type=MESH).wait_recv()
        for cp in sends:
            cp.wait_send()

        tot = sm_all[0]
        for q in range(1, N_DEV):
            tot = tot + sm_all[q]
        sum_ref[...] = tot

        cs = c_ref[...].reshape(N_DEV * 8, d)
        sc = (cs * jax.nn.sigmoid(cs)).astype(BF)
        for n0 in range(0, cols, chunk):
            dmv = dm_all[:, :, n0:n0 + chunk].reshape(N_DEV * 8, chunk).astype(BF)
            gw_ref[:, n0:n0 + chunk] = _tn(sc, dmv)

    return _whole_call(
        body, "small_sync", (smalls, dmod_blk, c_all),
        [_sds(smalls.shape, F32), _sds((d, cols), F32)],
        [pltpu.VMEM((N_DEV,) + smalls.shape, F32), pltpu.VMEM((N_DEV, 8, cols), F32),
         pltpu.SemaphoreType.DMA((2 * (N_DEV - 1),)), pltpu.SemaphoreType.DMA((2 * (N_DEV - 1),))], carry=carry)


def _bucket_onehot():
    maps = np.stack([_bucket_map(dil).reshape(-1) for _, dil in DIL_CONFIGS])
    return (jnp.asarray(maps)[:, None, :] == jnp.arange(N_BUCKETS, dtype=jnp.int32)[None, :, None]).astype(BF)


def _dil_bias(rel_t, onehot):
    def body(r_ref, oh_ref, o_ref):
        rv = r_ref[...]
        hi = rv.astype(BF)
        lo = (rv - hi.astype(F32)).astype(BF)
        for c in range(len(DIL_CONFIGS)):
            o_ref[c] = _nn(hi, oh_ref[c]) + _nn(lo, oh_ref[c])

    return pl.pallas_call(body, name="dil_bias",
                          out_shape=_sds((len(DIL_CONFIGS), N_HEADS, BLOCK * 2 * BLOCK), F32),
                          compiler_params=_cp())(rel_t, onehot)


def _rowsum8(a):
    def body(a_ref, o_ref):
        o_ref[...] = jnp.sum(a_ref[...], axis=0, keepdims=True)

    return pl.pallas_call(body, name="rowsum8", out_shape=_sds((1, a.shape[1]), F32), compiler_params=_cp())(a)


def _local_step(x, mod, target, w, gains, rel_bias, place=None):
    nb, seq, d = x.shape
    t = nb * seq
    dist = place is not None
    core = place[0:1] if dist else None
    x0 = x.reshape(t, d)
    tgt = target.reshape(t, d)
    md = [mod[:, i:i + 1, :] for i in range(N_MOD)]
    sh1, sc1, gt1, sh2, sc2, gt2, sh3, sc3, gt3 = md
    g1, g2, g3 = gains["g_ffn1"], gains["g_mix"], gains["g_ffn2"]
    ones_g = _group_ones()

    def partial_sums(grads, lands):
        return [_add_halves(core, g, l) for g, l in zip(grads, lands)]

    def chip_sums(parts, lands):
        return [_sum_chips(place, p, l, relative=True) for p, l in zip(parts, lands)]

    gu1 = w["gu1"]
    res = _ffn_up(x0, g1, sc1, sh1, gu1, seq,
                  carry=_join([_ag_weights([w["d1"]], relative=True), _ag_weights([w["win"]])]) if dist else None)
    h1, a1, u1, s1 = res[:4]
    wd1, w_in = res[4:] if dist else (w["d1"], w["win"])
    f1, x1 = _ffn_down(s1, wd1, x0, gt1, seq, 0.5)

    h2, qkv6, qkv_r4, qkv_r16 = _qkv_proj(x1, g2, sc2, sh2, w_in, seq)
    qkv6b = qkv6.reshape(6, nb, seq, D_GRP)
    res = _sb_fwd(qkv6b, gains["g_sb_out"], nb, seq,
                  carry=_join([_ag_weights([w["gu2"], w["d2"]], relative=True), _ag_weights([w["wout"]])])
                  if dist else None)
    o_sb, on_sb = res[:2]
    wgu2, wd2, w_out = res[2:] if dist else (w["gu2"], w["d2"], w["wout"])
    w_out2 = w_out.reshape(2 * D_GRP, d)
    onehot = _bucket_onehot()
    bias = _dil_bias(rel_bias.T, onehot).reshape(len(DIL_CONFIGS), N_HEADS * BLOCK, 2 * BLOCK)
    o_cs, l_cs = [], []
    qkv_rs = [(qkv6b, 3), (qkv_r4, 0), (qkv_r16, 0)]
    for ci, (_, dil) in enumerate(DIL_CONFIGS):
        sub = seq // dil
        arr, base = qkv_rs[ci]
        arr = arr.reshape(base + 3, nb, sub, dil * D_GRP)
        qkv_rs[ci] = (arr, base)
        o_c, l_c = _dil_fwd(arr, base, bias[ci], nb, sub, dil)
        o_cs.append(o_c.reshape(t // dil, dil * D_GRP))
        l_cs.append(l_c.reshape(t // dil, dil * D_GRP))
    o_dil, on_dil = _dil_comb(o_cs, l_cs, gains["g_dil_out"])
    tmix, x2 = _mix_out(on_sb.reshape(t, D_GRP), on_dil, w_out2, x1, gt2, seq)

    h3, a3, u3, s3 = _ffn_up(x2, g3, sc3, sh3, wgu2, seq)
    f3, dx3, dg_final, loss = _ffn_down_loss(s3, wd2, x2, gt3, seq, 0.5, gains["g_final"], tgt)

    da3, du3, df3, dgt3, dx2, dsh3, dsc3, dg3 = _ffn_bwd_x(dx3, gt3, f3, wd2, a3, u3, wgu2, x2, g3, sc3, seq, 0.5)
    grads2 = [_ffn_bwd_w(h3, da3, du3, s3, df3)]

    res = _mix_bwd_out(
        dx2, gt2, tmix, w_out2, o_sb.reshape(t, D_GRP), o_dil, on_sb.reshape(t, D_GRP), on_dil,
        gains["g_sb_out"], gains["g_dil_out"], ones_g, seq, carry=_rs_d2d(grads2) if dist else None)
    do_sb, do_dil, dgt2, dg_sb, dg_dil, dw_out = res[:6]
    parts2 = partial_sums(grads2, res[6:]) if dist else None
    dw_out = dw_out.reshape(N_CHIPS, 1, 2 * D_GRP // N_CHIPS, d)
    res = _sb_bwd(qkv6b, do_sb.reshape(nb, seq, D_GRP), nb, seq,
                  carry=_rs_ici(parts2, relative=True) if dist else None)
    dqkv6 = res[0]
    halves2 = chip_sums(parts2, res[1:]) if dist else None
    dcs = _dil_comb_bwd(do_dil, o_cs, l_cs)
    dsum, a_tiles = [], []
    for ci, (_, dil) in enumerate(DIL_CONFIGS):
        sub = seq // dil
        do_c = dcs[ci].reshape(nb, sub, dil * D_GRP)
        dd_c = dcs[3 + ci].reshape(nb, sub, dil * D_GRP)
        res = _dil_bwd(qkv_rs[ci][0], qkv_rs[ci][1], bias[ci], do_c, dd_c, nb, sub, dil)
        dsum.append(res[0].reshape(3, t // dil, dil * D_GRP))
        a_tiles.append(res[1].reshape(N_HEADS, BLOCK * 2 * BLOCK))
    dqkv6 = _dqkv_dil_sum(dsum, dqkv6.reshape(6, t, D_GRP))
    drel = _relbias_grad(jnp.stack(a_tiles), onehot)
    dx1, dsh2, dsc2, dg2 = _mix_bwd_dh(dqkv6, w_in, x1, g2, sc2, dx2, seq)

    da1, du1, df1, dgt1 = _ffn_bwd_ds(dx1, gt1, f1, wd1, a1, u1, seq, 0.5)
    grads1 = [_ffn_bwd_w(h1, da1, du1, s1, df1)]
    res = _dw_in(h2, dqkv6, carry=_join([_rs_d2d(grads1), _rs_final(halves2)]) if dist else None)
    grads_m = [res[0], dw_out]
    parts1 = partial_sums(grads1, res[1:2]) if dist else None
    if dist:
        grads2 = res[2:3]
    res = _ffn_bwd_dh(da1, du1, gu1, x0, g1, sc1, dx1, seq,
                      carry=_join([_rs_ici(parts1, relative=True), _rs_d2d(grads_m)]) if dist else None)
    dx0, dsh1, dsc1, dg1 = res[:4]
    pending = None
    if dist:
        pending = (chip_sums(parts1, res[4:5]), partial_sums(grads_m, res[5:7]))

    dmod = jnp.concatenate([dsh1, dsc1, dgt1, dsh2, dsc2, dgt2, dsh3, dsc3, dgt3], axis=1)
    return dict(grad_x=dx0.reshape(nb, seq, d), loss=loss[0, 0], dmod=dmod.reshape(nb, N_MOD * d),
                dffn1=grads1[0], dffn2=grads2[0], dwin=grads_m[0], dwout=grads_m[1], pending=pending,
                dg_ffn1=dg1, dg_mix=dg2, dg_ffn2=dg3, dg_final=dg_final, dg_sb=dg_sb, dg_dil=dg_dil,
                drel=drel.T)


_SMALL_ORDER = (("b_ada", N_MOD * 1024), ("g_ffn1", 1024), ("g_mix", 1024), ("g_ffn2", 1024), ("g_final", 1024),
                ("g_sb_out", D_GRP), ("g_dil_out", D_GRP), ("rel_bias", N_BUCKETS * N_HEADS))


def _pack_small(parts, extra=None):
    flat = [parts[name].reshape(-1).astype(F32) for name, _ in _SMALL_ORDER]
    used = sum(sz for _, sz in _SMALL_ORDER)
    pad = SMALL_ROWS * 128 - used
    tail = jnp.zeros((pad,), F32)
    if extra is not None:
        tail = tail.at[0].set(extra)
    return jnp.concatenate(flat + [tail]).reshape(SMALL_ROWS, 128)


def _unpack_small(packed, shapes):
    flat = packed.reshape(-1)
    out, off = {}, 0
    for name, sz in _SMALL_ORDER:
        out[name] = flat[off:off + sz].reshape(shapes[name])
        off += sz
    return out, flat[off]


def kernel(x, c, w_ada, b_ada, g_ffn1, w1_gate, w1_up, w1_down, g_mix, w_in, g_sb_out, g_dil_out, w_out, rel_bias, g_ffn2, w2_gate, w2_up, w2_down, g_final, loss_target, m_w_ada, m_b_ada, m_g_ffn1, m_w1_gate, m_w1_up, m_w1_down, m_g_mix, m_w_in, m_g_sb_out, m_g_dil_out, m_w_out, m_rel_bias, m_g_ffn2, m_w2_gate, m_w2_up, m_w2_down, m_g_final, v_w_ada, v_b_ada, v_g_ffn1, v_w1_gate, v_w1_up, v_w1_down, v_g_mix, v_w_in, v_g_sb_out, v_g_dil_out, v_w_out, v_rel_bias, v_g_ffn2, v_w2_gate, v_w2_up, v_w2_down, v_g_final):
    nb, seq, d = x.shape
    xi, yi, ci = lax.axis_index("x"), lax.axis_index("y"), lax.axis_index("c")
    chip = 2 * xi + yi
    ada_cols = w_ada.shape[-1]

    c_pad = jnp.zeros((8, d), F32).at[:nb].set(c)
    b_shard = lax.dynamic_slice(b_ada, (0, chip * ada_cols), (1, ada_cols))
    shards = dict(gu1=jnp.stack([w1_gate[0], w1_up[0]]), d1=w1_down, win=w_in, wout=w_out,
                  gu2=jnp.stack([w2_gate[0], w2_up[0]]), d2=w2_down)
    bufs = {k: lax.dynamic_update_slice(lax.empty((N_CHIPS,) + s.shape, BF), s.astype(BF)[None],
                                        (chip if k in ("win", "wout") else 0, 0, 0, 0))
            for k, s in shards.items()}
    c_all, mod_blk, bufs["gu1"] = _ada_fwd(c_pad, w_ada[0], b_shard,
                                           carry=_ag_weights([bufs["gu1"]], relative=True))
    mod = jnp.transpose(mod_blk[:, :nb, :], (1, 0, 2)).reshape(nb, N_MOD, d)

    gains = dict(g_ffn1=g_ffn1, g_mix=g_mix, g_ffn2=g_ffn2, g_final=g_final.reshape(1, d),
                 g_sb_out=g_sb_out.reshape(1, D_GRP), g_dil_out=g_dil_out.reshape(1, D_GRP))
    place = jnp.stack([ci, chip]).astype(jnp.int32)
    r = _local_step(x, mod, loss_target, bufs, gains, rel_bias, place)

    dmod = r["dmod"]
    dmod_pad = jnp.zeros((8, N_MOD * d), F32).at[:nb].set(dmod)
    dmod_blk = jnp.transpose(dmod_pad.reshape(8, N_CHIPS, ada_cols), (1, 0, 2))
    small_parts = dict(b_ada=_rowsum8(dmod_pad), g_ffn1=r["dg_ffn1"], g_mix=r["dg_mix"], g_ffn2=r["dg_ffn2"],
                       g_final=r["dg_final"], g_sb_out=r["dg_sb"], g_dil_out=r["dg_dil"], rel_bias=r["drel"])
    halves1, parts_m = r["pending"]
    res = _small_sync(_pack_small(small_parts, r["loss"]), dmod_blk, c_all,
                      carry=_join([_rs_final(halves1), _rs_ici(parts_m)]))
    small_sum, g_wada, gffn1 = res[:3]
    halves_m = [_sum_chips(place, p, l) for p, l in zip(parts_m, res[3:5])]
    gwin, gwout = _alone("rs_last", _rs_final(halves_m))
    gffn2 = r["dffn2"]

    small_w = dict(b_ada=b_ada, g_ffn1=g_ffn1, g_mix=g_mix, g_ffn2=g_ffn2, g_final=g_final,
                   g_sb_out=g_sb_out, g_dil_out=g_dil_out, rel_bias=rel_bias)
    small_m = dict(b_ada=m_b_ada, g_ffn1=m_g_ffn1, g_mix=m_g_mix, g_ffn2=m_g_ffn2, g_final=m_g_final,
                   g_sb_out=m_g_sb_out, g_dil_out=m_g_dil_out, rel_bias=m_rel_bias)
    small_v = dict(b_ada=v_b_ada, g_ffn1=v_g_ffn1, g_mix=v_g_mix, g_ffn2=v_g_ffn2, g_final=v_g_final,
                   g_sb_out=v_g_sb_out, g_dil_out=v_g_dil_out, rel_bias=v_rel_bias)
    shapes = {k: v.shape for k, v in small_w.items()}
    sg, sd, sm, sv = _adamw(_pack_small(small_w), small_sum.reshape(1, SMALL_ROWS, 128), 0,
                            _pack_small(small_m), _pack_small(small_v))
    sg, loss = _unpack_small(sg, shapes)
    sd, _ = _unpack_small(sd, shapes)
    sm, _ = _unpack_small(sm, shapes)
    sv, _ = _unpack_small(sv, shapes)

    big = {}

    def upd(name, w, g_arr, sel, m, v, transposed=False):
        swap = (lambda a: jnp.swapaxes(a, -1, -2)) if transposed else (lambda a: a)
        w2, m2, v2 = [swap(a)[0] for a in (w, m, v)]
        big[name] = [swap(a[None]) for a in _adamw(w2, g_arr, sel, m2, v2)]

    upd("w_ada", w_ada, g_wada.reshape(1, d, ada_cols), 0, m_w_ada, v_w_ada)
    upd("w1_gate", w1_gate, gffn1, 0, m_w1_gate, v_w1_gate, transposed=True)
    upd("w1_up", w1_up, gffn1, 1, m_w1_up, v_w1_up, transposed=True)
    upd("w1_down", w1_down, gffn1, 2, m_w1_down, v_w1_down)
    upd("w_in", w_in, gwin, 0, m_w_in, v_w_in)
    upd("w_out", w_out, gwout, 0, m_w_out, v_w_out)
    upd("w2_gate", w2_gate, gffn2, 0, m_w2_gate, v_w2_gate, transposed=True)
    upd("w2_up", w2_up, gffn2, 1, m_w2_up, v_w2_up, transposed=True)
    upd("w2_down", w2_down, gffn2, 2, m_w2_down, v_w2_down)

    names = ["w_ada", "b_ada", "g_ffn1", "w1_gate", "w1_up", "w1_down", "g_mix", "w_in", "g_sb_out", "g_dil_out",
             "w_out", "rel_bias", "g_ffn2", "w2_gate", "w2_up", "w2_down", "g_final"]
    outs = [loss, r["grad_x"]]
    for k, small in enumerate((sg, sd, sm, sv)):
        for name in names:
            outs.append(big[name][k] if name in big else small[name])
    return tuple(outs)
```

```python
import math

import numpy as np
import jax
import jax.numpy as jnp
from jax import lax
from jax.experimental import pallas as pl
from jax.experimental.pallas import tpu as pltpu

F32 = jnp.float32
BF = jnp.bfloat16
MESH = pl.DeviceIdType.MESH

HEAD_DIM = 64
N_HEADS = 8
D_GRP = N_HEADS * HEAD_DIM
DIL_CONFIGS = ((128, 1), (512, 4), (2048, 16))
N_STEPS = 128
BLOCK = 128
N_BUCKETS = 32
MAX_DISTANCE = 2048
N_MOD = 9
EPS = 1e-6
NEG_INF = -1e30
SCALE = HEAD_DIM ** -0.5

ADAM_LR = 0.001
ADAM_B1 = 0.9
ADAM_B2 = 0.999
ADAM_EPS = 1e-08
ADAM_WD = 0.01
ADAM_STEP = 10

N_CHIPS = 4
N_DEV = 8
VMEM_LIMIT = 56 * 1024 * 1024
TM = 512
TQ = 256
KB = 256
SMALL_ROWS = 120


def _cp(n_axes=0, **kw):
    sem = ("arbitrary",) * n_axes if n_axes else None
    return pltpu.CompilerParams(dimension_semantics=sem, vmem_limit_bytes=VMEM_LIMIT, **kw)


def _nn(a, b):
    return jnp.dot(a, b, preferred_element_type=F32)


def _nt(a, b):
    return lax.dot_general(a, b, (((1,), (1,)), ((), ())), preferred_element_type=F32)


def _tn(a, b):
    return lax.dot_general(a, b, (((0,), (0,)), ((), ())), preferred_element_type=F32)


def _nn2(x, m):
    hi = x.astype(BF)
    lo = (x - hi.astype(F32)).astype(BF)
    r = _nn(jnp.concatenate([hi, lo], axis=0), m)
    return r[:x.shape[0]] + r[x.shape[0]:]


def _softplus(z):
    return jnp.maximum(z, 0.0) + jnp.log1p(jnp.exp(-jnp.abs(z)))


def _sds(shape, dtype):
    return jax.ShapeDtypeStruct(shape, dtype)


def _whole(a):
    nd = a.ndim
    return pl.BlockSpec(a.shape, lambda *_: (0,) * nd, pipeline_mode=pl.Buffered(1))


def _modnorm_bwd_tile(dh, xv, gv, scv, dxo):
    r = lax.rsqrt(jnp.mean(xv * xv, axis=-1, keepdims=True) + EPS)
    n = xv * r
    ng = n * gv
    dsh = jnp.sum(dh, axis=0, keepdims=True)
    dsc = jnp.sum(dh * ng, axis=0, keepdims=True)
    dy = dh * (1.0 + scv)
    dg = jnp.sum(dy * n, axis=0, keepdims=True)
    dn = dy * gv
    dx = dxo + r * (dn - n * jnp.mean(dn * n, axis=-1, keepdims=True))
    return dx, dsh, dsc, dg


def _acc_rows(ref, val, first):
    @pl.when(first)
    def _():
        ref[...] = val

    @pl.when(jnp.logical_not(first))
    def _():
        ref[...] += val


def _modnorm_tile(x_ref, g_ref, sc_ref, sh_ref):
    xv = x_ref[...]
    r = lax.rsqrt(jnp.mean(xv * xv, axis=-1, keepdims=True) + EPS)
    return (((xv * r) * g_ref[...]) * (1.0 + sc_ref[...]) + sh_ref[...]).astype(BF)


def _ffn_up(x, g, sc, sh, wgu, seq, carry=None):
    t, d = x.shape
    fs = wgu.shape[-1]
    per = seq // TM

    def body(x_ref, g_ref, sc_ref, sh_ref, w_ref, h_ref, p_ref, q_ref, s_ref):
        hv = _modnorm_tile(x_ref, g_ref, sc_ref, sh_ref)
        h_ref[...] = hv
        for j in range(N_CHIPS):
            a = _nn(hv, w_ref[j, 0])
            u = _nn(hv, w_ref[j, 1])
            sig = jax.nn.sigmoid(a)
            q = a * sig
            p_ref[j] = (u * (sig * (1.0 + a * (1.0 - sig)))).astype(BF)
            q_ref[j] = q.astype(BF)
            s_ref[j] = (q * u).astype(BF)

    row = pl.BlockSpec((TM, d), lambda m: (m, 0))
    ex = pl.BlockSpec((None, 1, d), lambda m: (m // per, 0, 0))
    blk = pl.BlockSpec((N_CHIPS, TM, fs), lambda m: (0, m, 0))
    return _call(
        body, "ffn_up", (t // TM,),
        [row, pl.BlockSpec((1, d), lambda m: (0, 0)), ex, ex, _whole(wgu)],
        [row, blk, blk, blk],
        [_sds((t, d), BF)] + [_sds((N_CHIPS, t, fs), BF)] * 3,
        (x, g, sc, sh, wgu), carry=carry)


def _ffn_down(s, wd, x, gt, seq, coef, carry=None):
    _, t, fs = s.shape
    d = x.shape[-1]
    per = seq // TM

    def body(s_ref, w_ref, x_ref, gt_ref, f_ref, xo_ref):
        f = _nn(s_ref[0], w_ref[0, 0])
        for j in range(1, N_CHIPS):
            f = f + _nn(s_ref[j], w_ref[j, 0])
        f_ref[...] = f.astype(BF)
        xo_ref[...] = x_ref[...] + (coef * gt_ref[...]) * f

    row = pl.BlockSpec((TM, d), lambda m: (m, 0))
    return _call(
        body, "ffn_down", (t // TM,),
        [pl.BlockSpec((N_CHIPS, TM, fs), lambda m: (0, m, 0)), _whole(wd), row,
         pl.BlockSpec((None, 1, d), lambda m: (m // per, 0, 0))],
        [row, row], [_sds((t, d), BF), _sds((t, d), F32)], (s, wd, x, gt), carry=carry)


def _ffn_bwd_ds(dxo, gt, f, wd, p, q, seq, coef, carry=None):
    t, d = dxo.shape
    fs = p.shape[-1]
    per = seq // TM
    nb = t // seq

    def body(dxo_ref, gt_ref, f_ref, w_ref, p_ref, q_ref, da_ref, du_ref, df_ref, dgt_ref):
        m = pl.program_id(0)
        dxv = dxo_ref[...]
        df = ((coef * gt_ref[...]) * dxv).astype(BF)
        df_ref[...] = df
        _acc_rows(dgt_ref, coef * jnp.sum(dxv * f_ref[...].astype(F32), axis=0, keepdims=True), m % per == 0)
        for j in range(N_CHIPS):
            ds = _nt(df, w_ref[j, 0])
            da_ref[j] = (ds * p_ref[j].astype(F32)).astype(BF)
            du_ref[j] = (ds * q_ref[j].astype(F32)).astype(BF)

    row = pl.BlockSpec((TM, d), lambda m: (m, 0))
    blk = pl.BlockSpec((N_CHIPS, TM, fs), lambda m: (0, m, 0))
    ex = pl.BlockSpec((None, 1, d), lambda m: (m // per, 0, 0))
    return _call(
        body, "ffn_bwd_ds", (t // TM,),
        [row, ex, row, _whole(wd), blk, blk],
        [blk, blk, row, ex],
        [_sds((N_CHIPS, t, fs), BF), _sds((N_CHIPS, t, fs), BF), _sds((t, d), BF), _sds((nb, 1, d), F32)],
        (dxo, gt, f, wd, p, q), carry=carry)


TM_X = 256


def _ffn_bwd_x(dxo, gt, f, wd, p, q, wgu, x, g, sc, seq, coef):
    t, d = dxo.shape
    fs = p.shape[-1]
    per = seq // TM_X
    nb = t // seq

    def body(dxo_ref, gt_ref, f_ref, wd_ref, p_ref, q_ref, w_ref, x_ref, g_ref, sc_ref,
             da_ref, du_ref, df_ref, dgt_ref, dx_ref, dsh_ref, dsc_ref, dg_ref):
        m = pl.program_id(0)
        dxv = dxo_ref[...]
        df = ((coef * gt_ref[...]) * dxv).astype(BF)
        df_ref[...] = df
        _acc_rows(dgt_ref, coef * jnp.sum(dxv * f_ref[...].astype(F32), axis=0, keepdims=True), m % per == 0)
        dh = None
        for j in range(N_CHIPS):
            ds = _nt(df, wd_ref[j, 0])
            da = (ds * p_ref[j].astype(F32)).astype(BF)
            du = (ds * q_ref[j].astype(F32)).astype(BF)
            da_ref[j] = da
            du_ref[j] = du
            part = _nt(da, w_ref[j, 0]) + _nt(du, w_ref[j, 1])
            dh = part if dh is None else dh + part
        dx, dsh, dsc, dg = _modnorm_bwd_tile(dh, x_ref[...], g_ref[...], sc_ref[...], dxv)
        dx_ref[...] = dx
        _acc_rows(dsh_ref, dsh, m % per == 0)
        _acc_rows(dsc_ref, dsc, m % per == 0)
        _acc_rows(dg_ref, dg, m == 0)

    row = pl.BlockSpec((TM_X, d), lambda m: (m, 0))
    blk = pl.BlockSpec((N_CHIPS, TM_X, fs), lambda m: (0, m, 0))
    ex = pl.BlockSpec((None, 1, d), lambda m: (m // per, 0, 0))
    vec = pl.BlockSpec((1, d), lambda m: (0, 0))
    exs = _sds((nb, 1, d), F32)
    return pl.pallas_call(
        body, name="ffn_bwd_x", grid=(t // TM_X,),
        in_specs=[row, ex, row, _whole(wd), blk, blk, _whole(wgu), row, vec, ex],
        out_specs=[blk, blk, row, ex, row, ex, ex, vec],
        out_shape=[_sds((N_CHIPS, t, fs), BF), _sds((N_CHIPS, t, fs), BF), _sds((t, d), BF), exs,
                   _sds((t, d), F32), exs, exs, _sds((1, d), F32)],
        compiler_params=_cp(1))(dxo, gt, f, wd, p, q, wgu, x, g, sc)


TK_W = 1024


def _ffn_bwd_w(h, da, du, s, df):
    t, d = h.shape
    fs = da.shape[-1]

    def body(h_ref, da_ref, du_ref, s_ref, df_ref, o_ref):
        kt = pl.program_id(1)
        hv = h_ref[...]
        parts = (_tn(da_ref[...], hv), _tn(du_ref[...], hv), _tn(s_ref[...], df_ref[...]))

        @pl.when(kt == 0)
        def _():
            for i, p in enumerate(parts):
                o_ref[i] = p

        @pl.when(kt != 0)
        def _():
            for i, p in enumerate(parts):
                o_ref[i] += p

    row = pl.BlockSpec((TK_W, d), lambda j, kt: (kt, 0))
    blk = pl.BlockSpec((None, TK_W, fs), lambda j, kt: (j, kt, 0))
    return pl.pallas_call(
        body, name="ffn_bwd_w", grid=(N_CHIPS, t // TK_W),
        in_specs=[row, blk, blk, blk, row],
        out_specs=pl.BlockSpec((None, 3, fs, d), lambda j, kt: (j, 0, 0, 0)),
        out_shape=_sds((N_CHIPS, 3, fs, d), F32),
        compiler_params=_cp(2))(h, da, du, s, df)


def _ffn_bwd_dh(da, du, wgu, x, g, sc, dxo, seq, carry=None):
    _, t, fs = da.shape
    d = x.shape[-1]
    per = seq // TM
    nb = t // seq

    def body(da_ref, du_ref, w_ref, x_ref, g_ref, sc_ref, dxo_ref, dx_ref, dsh_ref, dsc_ref, dg_ref):
        m = pl.program_id(0)
        dh = _nt(da_ref[0], w_ref[0, 0]) + _nt(du_ref[0], w_ref[0, 1])
        for j in range(1, N_CHIPS):
            dh = dh + _nt(da_ref[j], w_ref[j, 0]) + _nt(du_ref[j], w_ref[j, 1])
        dx, dsh, dsc, dg = _modnorm_bwd_tile(dh, x_ref[...], g_ref[...], sc_ref[...], dxo_ref[...])
        dx_ref[...] = dx
        _acc_rows(dsh_ref, dsh, m % per == 0)
        _acc_rows(dsc_ref, dsc, m % per == 0)
        _acc_rows(dg_ref, dg, m == 0)

    row = pl.BlockSpec((TM, d), lambda m: (m, 0))
    blk = pl.BlockSpec((N_CHIPS, TM, fs), lambda m: (0, m, 0))
    ex = pl.BlockSpec((None, 1, d), lambda m: (m // per, 0, 0))
    vec = pl.BlockSpec((1, d), lambda m: (0, 0))
    return _call(
        body, "ffn_bwd_dh", (t // TM,),
        [blk, blk, _whole(wgu), row, vec, ex, row],
        [row, ex, ex, vec],
        [_sds((t, d), F32), _sds((nb, 1, d), F32), _sds((nb, 1, d), F32), _sds((1, d), F32)],
        (da, du, wgu, x, g, sc, dxo), carry=carry)


def _qkv_proj(x, g, sc, sh, w_in, seq, carry=None):
    t, d = x.shape
    wc = w_in.shape[-1]
    per = seq // TM

    dils = [dil for _, dil in DIL_CONFIGS if dil > 1]

    def body(x_ref, g_ref, sc_ref, sh_ref, w_ref, h_ref, o_ref, *rest):
        res_refs, buf = rest[:len(dils)], rest[len(dils)]
        hv = _modnorm_tile(x_ref, g_ref, sc_ref, sh_ref)
        h_ref[...] = hv
        for j in range(N_CHIPS):
            rf = _nn(hv, w_ref[j, 0])
            r = rf.astype(BF)
            for a, lc, off, width in _col_pieces(j, wc):
                o_ref[a, :, lc:lc + width] = r[:, off:off + width]
                if a < 3:
                    continue
                for c0 in range(0, width, 128):
                    cg = (lc + c0) // 128
                    buf[...] = rf[:, off + c0:off + c0 + 128]
                    for ref, dil in zip(res_refs, dils):
                        for rr in range(dil):
                            ref[a - 3, :, rr * D_GRP + cg * 128:rr * D_GRP + (cg + 1) * 128] = (
                                buf[pl.ds(rr, TM // dil, stride=dil), :].astype(BF))

    row = pl.BlockSpec((TM, d), lambda m: (m, 0))
    ex = pl.BlockSpec((None, 1, d), lambda m: (m // per, 0, 0))
    return _call(
        body, "qkv_proj", (t // TM,),
        [row, pl.BlockSpec((1, d), lambda m: (0, 0)), ex, ex, _whole(w_in)],
        [row, pl.BlockSpec((6, TM, D_GRP), lambda m: (0, m, 0))]
        + [pl.BlockSpec((3, TM // dil, dil * D_GRP), lambda m: (0, m, 0)) for dil in dils],
        [_sds((t, d), BF), _sds((6, t, D_GRP), BF)] + [_sds((3, t // dil, dil * D_GRP), BF) for dil in dils],
        (x, g, sc, sh, w_in), scratch=[pltpu.VMEM((TM, 128), F32)], carry=carry)


def _col_pieces(j, wc):
    out, off = [], 0
    while off < wc:
        a, lc = divmod(j * wc + off, D_GRP)
        width = min(D_GRP - lc, wc - off)
        out.append((a, lc, off, width))
        off += width
    return out


def _chip_cols(g6_ref, j, wc):
    return jnp.concatenate([g6_ref[a, :, lc:lc + width] for a, lc, _, width in _col_pieces(j, wc)], axis=1)


def _mix_out(on_sb, on_dil, w_out, x, gt, seq):
    t, d = x.shape
    per = seq // TM

    def body(a_ref, b_ref, w_ref, x_ref, gt_ref, t_ref, xo_ref):
        tv = _nn(a_ref[...], w_ref[0:D_GRP, :]) + _nn(b_ref[...], w_ref[D_GRP:2 * D_GRP, :])
        t_ref[...] = tv.astype(BF)
        xo_ref[...] = x_ref[...] + gt_ref[...] * tv

    row = pl.BlockSpec((TM, d), lambda m: (m, 0))
    half = pl.BlockSpec((TM, D_GRP), lambda m: (m, 0))
    return pl.pallas_call(
        body, name="mix_out", grid=(t // TM,),
        in_specs=[half, half, pl.BlockSpec((2 * D_GRP, d), lambda m: (0, 0)), row,
                  pl.BlockSpec((None, 1, d), lambda m: (m // per, 0, 0))],
        out_specs=[row, row],
        out_shape=[_sds((t, d), BF), _sds((t, d), F32)],
        compiler_params=_cp(1))(on_sb, on_dil, w_out, x, gt)


def _sb_masks():
    lane = lax.broadcasted_iota(jnp.int32, (1, 2 * HEAD_DIM), 1)
    hm0 = lane < HEAD_DIM
    rel = lax.broadcasted_iota(jnp.int32, (TQ, KB), 0) - lax.broadcasted_iota(jnp.int32, (TQ, KB), 1)
    kr = lax.broadcasted_iota(jnp.int32, (KB, KB), 0)
    kc = lax.broadcasted_iota(jnp.int32, (KB, KB), 1)
    return hm0, rel, kr, kc


def _stack_pair(x, hm0):
    zero = jnp.zeros_like(x)
    return jnp.concatenate([jnp.where(hm0, x, zero), jnp.where(hm0, zero, x)], axis=0)


def _headnorm_pair(o, gv, hm0):
    o2 = o * o
    ms0 = jnp.sum(jnp.where(hm0, o2, 0.0), axis=-1, keepdims=True) * (1.0 / HEAD_DIM)
    ms1 = jnp.sum(jnp.where(hm0, 0.0, o2), axis=-1, keepdims=True) * (1.0 / HEAD_DIM)
    r = jnp.where(hm0, lax.rsqrt(ms0 + EPS), lax.rsqrt(ms1 + EPS))
    return (o * r) * gv


SB_DEAD = -104.0


def _alive(c_l):
    return (jnp.max(c_l) > SB_DEAD).astype(jnp.int32)


def _sb_fwd(qkv6, g_sb, nb, seq, carry=None):
    nq = seq // TQ

    def body(q_ref, k_ref, v_ref, g_ref, o_ref, on_ref):
        qi = pl.program_id(2)
        hm0, rel, kr, kc = _sb_masks()
        upper = (kr > kc).astype(BF)
        heads = _dil_masks()[0]
        qs = _stack_heads(q_ref[...], heads)
        causal2 = jnp.concatenate([rel] * GRP_HEADS, axis=0) > 0

        def block(kj, causal, c_l, acc):
            ks = pl.multiple_of(kj * KB, KB)
            z = _nt(qs, k_ref[pl.ds(ks, KB), :]) * SCALE
            sp = _softplus(z)
            ln = -sp if causal is None else jnp.where(causal, -sp, 0.0)
            suf = _nn2(ln, upper)
            w = jnp.exp((z - sp) + (suf + c_l))
            if causal is not None:
                w = jnp.where(causal, w, 0.0)
            return c_l + (suf[:, 0:1] + ln[:, 0:1]), acc + _nn(w.astype(BF), v_ref[pl.ds(ks, KB), :])

        c_l, acc = block(qi, causal2, jnp.zeros((GRP_HEADS * TQ, 1), F32), jnp.zeros((GRP_HEADS * TQ, GRP_W), F32))

        def cond(carry):
            return jnp.logical_and(carry[0] <= qi, carry[1] > 0)

        def kbody(carry):
            it, _, c_l, acc = carry
            c_l, acc = block(qi - it, None, c_l, acc)
            return it + 1, _alive(c_l), c_l, acc

        acc = lax.while_loop(cond, kbody, (jnp.int32(1), _alive(c_l), c_l, acc))[3]
        o = _unstack_heads(acc, heads, TQ)
        o_ref[...] = o.astype(BF)
        gv = g_ref[...]
        for half in range(GRP_W // 128):
            lanes = slice(half * 128, (half + 1) * 128)
            on_ref[:, lanes] = _headnorm_pair(o[:, lanes], gv[:, lanes], hm0).astype(BF)

    w = GRP_W
    full = lambda i: pl.BlockSpec((None, None, seq, w), lambda b, hp, q: (i, b, 0, hp))
    qblk = pl.BlockSpec((None, None, TQ, w), lambda b, hp, q: (0, b, q, hp))
    oblk = pl.BlockSpec((None, TQ, w), lambda b, hp, q: (b, q, hp))
    return _call(
        body, "sb_fwd", (nb, N_HEADS // GRP_HEADS, nq),
        [qblk, full(1), full(2), pl.BlockSpec((1, w), lambda b, hp, q: (0, hp))],
        [oblk, oblk],
        [_sds((nb, seq, D_GRP), BF), _sds((nb, seq, D_GRP), BF)],
        (qkv6, qkv6, qkv6, g_sb), carry=carry)


def _sb_bwd(qkv6, do, nb, seq, carry=None):
    nq = seq // TQ
    nk = seq // KB

    def body(q_ref, k_ref, v_ref, do_ref, out_ref, dk_acc, dv_acc, g_st, s_st):
        qi = pl.program_id(2)
        hm0, rel, kr, kc = _sb_masks()
        upper = (kr > kc).astype(BF)
        lower = (kr < kc).astype(BF)

        @pl.when(qi == 0)
        def _():
            dk_acc[...] = jnp.zeros_like(dk_acc)
            dv_acc[...] = jnp.zeros_like(dv_acc)

        heads = _dil_masks()[0]
        qs = _stack_heads(q_ref[...], heads)
        dos = _stack_heads(do_ref[...], heads)
        causal2 = jnp.concatenate([rel] * GRP_HEADS, axis=0) > 0

        def weights(kj, causal, c_l):
            ks = pl.multiple_of(kj * KB, KB)
            vb = v_ref[pl.ds(ks, KB), :]
            z = _nt(qs, k_ref[pl.ds(ks, KB), :]) * SCALE
            sp = _softplus(z)
            ln = -sp if causal is None else jnp.where(causal, -sp, 0.0)
            suf = _nn2(ln, upper)
            lsz = z - sp
            w = jnp.exp(lsz + (suf + c_l))
            if causal is not None:
                w = jnp.where(causal, w, 0.0)
            g_st[kj] = w * _nt(dos, vb)
            s_st[kj] = jnp.exp(lsz)
            dv_acc[pl.ds(ks, KB), :] += _tn(w.astype(BF), dos)
            return c_l + (suf[:, 0:1] + ln[:, 0:1])

        zc = jnp.zeros((GRP_HEADS * TQ, 1), F32)
        c_l = weights(qi, causal2, zc)

        def acond(carry):
            return jnp.logical_and(carry[0] <= qi, carry[1] > 0)

        def abody(carry):
            c_l = weights(qi - carry[0], None, carry[2])
            return carry[0] + 1, _alive(c_l), c_l

        n_used = lax.while_loop(acond, abody, (jnp.int32(1), _alive(c_l), c_l))[0]

        def grads(kj, causal, c_g, dq):
            ks = pl.multiple_of(kj * KB, KB)
            kb = k_ref[pl.ds(ks, KB), :]
            g = g_st[kj]
            sig = s_st[kj]
            pre = _nn(g.astype(BF), lower)
            dz = g * (1.0 - sig) - sig * (pre + c_g)
            if causal is not None:
                dz = jnp.where(causal, dz, 0.0)
            dzb = (dz * SCALE).astype(BF)
            dk_acc[pl.ds(ks, KB), :] += _tn(dzb, qs)
            return c_g + (pre[:, KB - 1:KB] + g[:, KB - 1:KB]), dq + _nn(dzb, kb)

        c_g, dq = lax.fori_loop(qi - n_used + 1, qi, lambda kj, cr: grads(kj, None, *cr),
                                (zc, jnp.zeros((GRP_HEADS * TQ, GRP_W), F32)))
        _, dq = grads(qi, causal2, c_g, dq)
        dq = _unstack_heads(dq, heads, TQ)
        out_ref[0, pl.ds(pl.multiple_of(qi * TQ, TQ), TQ), :] = dq.astype(BF)

        @pl.when(qi == nq - 1)
        def _():
            out_ref[1] = dk_acc[...].astype(BF)
            out_ref[2] = dv_acc[...].astype(BF)

    w = GRP_W
    full = lambda i: pl.BlockSpec((None, None, seq, w), lambda b, hp, q: (i, b, 0, hp))
    qblk = pl.BlockSpec((None, None, TQ, w), lambda b, hp, q: (0, b, q, hp))
    oblk = pl.BlockSpec((None, TQ, w), lambda b, hp, q: (b, q, hp))
    return _call(
        body, "sb_bwd", (nb, N_HEADS // GRP_HEADS, nq),
        [qblk, full(1), full(2), oblk],
        [pl.BlockSpec((3, None, seq, w), lambda b, hp, q: (0, b, 0, hp))],
        [_sds((6, nb, seq, D_GRP), BF)], (qkv6, qkv6, qkv6, do),
        scratch=[pltpu.VMEM((seq, w), F32), pltpu.VMEM((seq, w), F32),
                 pltpu.VMEM((nk, GRP_HEADS * TQ, KB), F32), pltpu.VMEM((nk, GRP_HEADS * TQ, KB), F32)],
        carry=carry)


def _t5_bucket(n):
    max_exact = N_BUCKETS // 2
    nf = np.maximum(n, 1).astype(np.float32)
    large = max_exact + (np.log(nf / max_exact) / math.log(MAX_DISTANCE / max_exact)
                         * (N_BUCKETS - max_exact)).astype(np.int32)
    large = np.minimum(large, N_BUCKETS - 1)
    return np.where(n < max_exact, n, large).astype(np.int32)


def _bucket_map(dilation):
    step = BLOCK + np.arange(BLOCK)[:, None] - np.arange(2 * BLOCK)[None, :]
    return _t5_bucket(np.clip(step, 0, N_STEPS) * dilation)


GRP_HEADS = 4
GRP_W = GRP_HEADS * HEAD_DIM


def _dil_masks():
    lane = lax.broadcasted_iota(jnp.int32, (1, GRP_W), 1)
    heads = [jnp.logical_and(lane >= HEAD_DIM * i, lane < HEAD_DIM * (i + 1)) for i in range(GRP_HEADS)]
    iq = jnp.bitwise_and(lax.broadcasted_iota(jnp.int32, (GRP_HEADS * BLOCK, BLOCK), 0), BLOCK - 1)
    ik = lax.broadcasted_iota(jnp.int32, (GRP_HEADS * BLOCK, BLOCK), 1)
    return heads, ik <= iq, ik >= iq


def _stack_heads(x, heads):
    zero = jnp.zeros_like(x)
    return jnp.concatenate([jnp.where(hm, x, zero) for hm in heads], axis=0)


def _unstack_heads(xs, heads, rows=BLOCK):
    out = xs[0:rows]
    for i in range(1, GRP_HEADS):
        out = jnp.where(heads[i], xs[i * rows:(i + 1) * rows], out)
    return out


def _dil_rows(n):
    rs = pl.multiple_of(n * BLOCK, BLOCK)
    ps = pl.multiple_of(jnp.maximum(n - 1, 0) * BLOCK, BLOCK)
    return pl.ds(rs, BLOCK), pl.ds(ps, BLOCK)


def _dil_probs(qs, kc, kp, b_ref, gi, valid_c, valid_p):
    rows = slice(gi * GRP_HEADS * BLOCK, (gi + 1) * GRP_HEADS * BLOCK)
    zc = _nt(qs, kc) * SCALE + b_ref[rows, BLOCK:2 * BLOCK]
    zp = _nt(qs, kp) * SCALE + b_ref[rows, 0:BLOCK]
    zc = jnp.where(valid_c, zc, NEG_INF)
    zp = jnp.where(valid_p, zp, NEG_INF)
    m = jnp.maximum(jnp.max(zc, axis=-1, keepdims=True), jnp.max(zp, axis=-1, keepdims=True))
    ec = jnp.exp(zc - m)
    ep = jnp.exp(zp - m)
    den = jnp.sum(ec, axis=-1, keepdims=True) + jnp.sum(ep, axis=-1, keepdims=True)
    return ec, ep, den, m


def _dil_fwd(qkv6r, base, bias, nb, sub_len, dilation):
    n_blk = sub_len // BLOCK

    def body(q_ref, k_ref, v_ref, b_ref, o_ref, l_ref):
        heads, valid_c, valid_p0 = _dil_masks()

        def nbody(n, carry):
            cur, prev = _dil_rows(n)
            valid_p = jnp.logical_and(valid_p0, n > 0)
            for gi in range(N_HEADS // GRP_HEADS):
                lanes = slice(gi * GRP_W, (gi + 1) * GRP_W)
                qs = _stack_heads(q_ref[cur, lanes], heads)
                ec, ep, den, m = _dil_probs(qs, k_ref[cur, lanes], k_ref[prev, lanes], b_ref, gi, valid_c, valid_p)
                o = (_nn(ec.astype(BF), v_ref[cur, lanes]) + _nn(ep.astype(BF), v_ref[prev, lanes])) / den
                o_ref[cur, lanes] = _unstack_heads(o, heads).astype(BF)
                l_ref[cur, lanes] = _unstack_heads(jnp.broadcast_to(m + jnp.log(den), o.shape), heads)
            return carry

        lax.fori_loop(0, n_blk, nbody, 0)

    seqblk = lambda i: pl.BlockSpec((None, None, sub_len, D_GRP), lambda b, r: (i, b, 0, r))
    oblk = pl.BlockSpec((None, sub_len, D_GRP), lambda b, r: (b, 0, r))
    shp = _sds((nb, sub_len, dilation * D_GRP), F32)
    return pl.pallas_call(
        body, name="dil_fwd_%d" % dilation, grid=(nb, dilation),
        in_specs=[seqblk(base), seqblk(base + 1), seqblk(base + 2), _whole(bias)],
        out_specs=[oblk, oblk], out_shape=[_sds(shp.shape, BF), shp],
        compiler_params=_cp(2))(qkv6r, qkv6r, qkv6r, bias)


def _dil_bwd(qkv6r, base, bias, do_c, dd_c, nb, sub_len, dilation, carry=None):
    n_blk = sub_len // BLOCK

    def body(q_ref, k_ref, v_ref, b_ref, do_ref, dd_ref, out_ref, a_ref, dk_acc, dv_acc):
        heads, valid_c, valid_p0 = _dil_masks()
        first = jnp.logical_and(pl.program_id(0) == 0, pl.program_id(1) == 0)

        @pl.when(first)
        def _():
            a_ref[...] = jnp.zeros_like(a_ref)

        dk_acc[...] = jnp.zeros_like(dk_acc)
        dv_acc[...] = jnp.zeros_like(dv_acc)

        def nbody(n, carry):
            cur, prev = _dil_rows(n)
            valid_p = jnp.logical_and(valid_p0, n > 0)
            for gi in range(N_HEADS // GRP_HEADS):
                lanes = slice(gi * GRP_W, (gi + 1) * GRP_W)
                kc, kp = k_ref[cur, lanes], k_ref[prev, lanes]
                vc, vp = v_ref[cur, lanes], v_ref[prev, lanes]
                qs = _stack_heads(q_ref[cur, lanes], heads)
                dos = _stack_heads(do_ref[cur, lanes], heads).astype(BF)
                dds = jnp.sum(_stack_heads(dd_ref[cur, lanes], heads), axis=-1, keepdims=True) * (1.0 / HEAD_DIM)
                ec, ep, den, _ = _dil_probs(qs, kc, kp, b_ref, gi, valid_c, valid_p)
                inv = 1.0 / den
                pc = ec * inv
                pp = ep * inv
                dzc = pc * (_nt(dos, vc) + dds)
                dzp = pp * (_nt(dos, vp) + dds)
                rows = slice(gi * GRP_HEADS * BLOCK, (gi + 1) * GRP_HEADS * BLOCK)
                a_ref[rows, BLOCK:2 * BLOCK] += dzc
                a_ref[rows, 0:BLOCK] += dzp
                dzcb = (dzc * SCALE).astype(BF)
                dzpb = (dzp * SCALE).astype(BF)
                out_ref[0, cur, lanes] = _unstack_heads(_nn(dzcb, kc) + _nn(dzpb, kp), heads).astype(BF)
                dk_acc[cur, lanes] += _tn(dzcb, qs)
                dk_acc[prev, lanes] += _tn(dzpb, qs)
                dv_acc[cur, lanes] += _tn(pc.astype(BF), dos)
                dv_acc[prev, lanes] += _tn(pp.astype(BF), dos)
            return carry

        lax.fori_loop(0, n_blk, nbody, 0)
        out_ref[1] = dk_acc[...].astype(BF)
        out_ref[2] = dv_acc[...].astype(BF)

    seqblk = lambda i: pl.BlockSpec((None, None, sub_len, D_GRP), lambda b, r: (i, b, 0, r))
    oblk = pl.BlockSpec((None, sub_len, D_GRP), lambda b, r: (b, 0, r))
    return _call(
        body, "dil_bwd_%d" % dilation, (nb, dilation),
        [seqblk(base), seqblk(base + 1), seqblk(base + 2), _whole(bias), oblk, oblk],
        [pl.BlockSpec((3, None, sub_len, D_GRP), lambda b, r: (0, b, 0, r)),
         pl.BlockSpec((N_HEADS * BLOCK, 2 * BLOCK), lambda b, r: (0, 0))],
        [_sds((3, nb, sub_len, dilation * D_GRP), BF), _sds((N_HEADS * BLOCK, 2 * BLOCK), F32)],
        (qkv6r, qkv6r, qkv6r, bias, do_c, dd_c),
        scratch=[pltpu.VMEM((sub_len, D_GRP), F32)] * 2, carry=carry)


def _group_ones():
    idx = np.arange(D_GRP) // HEAD_DIM
    return jnp.asarray((idx[:, None] == idx[None, :]).astype(np.float32), dtype=BF)


def _dil_alphas(l1, l4, l16):
    mx = jnp.maximum(jnp.maximum(l1, l4), l16)
    e1 = jnp.exp(l1 - mx)
    e4 = jnp.exp(l4 - mx)
    e16 = jnp.exp(l16 - mx)
    den = e1 + e4 + e16
    return e1 / den, e4 / den, e16 / den


def _residue_spec(dil):
    return pl.BlockSpec((TM // dil, dil * D_GRP), lambda m: (m, 0))


def _from_residue(src, dil, cg, buf):
    if dil == 1:
        return src[:, cg * 128:(cg + 1) * 128].astype(F32)
    for r in range(dil):
        buf[pl.ds(r, TM // dil, stride=dil), :] = (
            src[:, r * D_GRP + cg * 128:r * D_GRP + (cg + 1) * 128].astype(F32))
    return buf[...]


def _to_residue(dst, dil, cg, buf, val):
    if dil == 1:
        dst[:, cg * 128:(cg + 1) * 128] = val.astype(dst.dtype)
        return
    buf[...] = val
    for r in range(dil):
        dst[:, r * D_GRP + cg * 128:r * D_GRP + (cg + 1) * 128] = (
            buf[pl.ds(r, TM // dil, stride=dil), :].astype(dst.dtype))


def _pair_sum(x, hm0):
    s0 = jnp.sum(jnp.where(hm0, x, 0.0), axis=-1, keepdims=True)
    s1 = jnp.sum(jnp.where(hm0, 0.0, x), axis=-1, keepdims=True)
    return jnp.where(hm0, s0, s1)


def _dil_comb(os, ls, g_dil):
    t = os[0].shape[0]
    dils = [dil for _, dil in DIL_CONFIGS]

    def body(o1, l1, o4, l4, o16, l16, g_ref, o_ref, on_ref, b0, b1, b2, b3):
        hm0 = lax.broadcasted_iota(jnp.int32, (1, 128), 1) < HEAD_DIM
        for cg in range(D_GRP // 128):
            lanes = slice(cg * 128, (cg + 1) * 128)
            ov = [_from_residue(src, dil, cg, buf) for src, dil, buf in zip((o1, o4, o16), dils, (None, b0, b1))]
            lv = [_from_residue(src, dil, cg, buf) for src, dil, buf in zip((l1, l4, l16), dils, (None, b2, b3))]
            a1, a4, a16 = _dil_alphas(*lv)
            o = a1 * ov[0] + a4 * ov[1] + a16 * ov[2]
            o_ref[:, lanes] = o.astype(BF)
            on_ref[:, lanes] = _headnorm_pair(o, g_ref[:, lanes], hm0).astype(BF)

    blk = pl.BlockSpec((TM, D_GRP), lambda m: (m, 0))
    specs = [_residue_spec(dil) for dil in dils for _ in range(2)]
    return pl.pallas_call(
        body, name="dil_comb", grid=(t // TM,),
        in_specs=specs + [pl.BlockSpec((1, D_GRP), lambda m: (0, 0))],
        out_specs=[blk, blk],
        out_shape=[_sds((t, D_GRP), BF), _sds((t, D_GRP), BF)],
        scratch_shapes=[pltpu.VMEM((TM, 128), F32)] * 4,
        compiler_params=_cp(1))(os[0], ls[0], os[1], ls[1], os[2], ls[2], g_dil)


def _dil_comb_bwd(do, os, ls):
    t = do.shape[0]
    dils = [dil for _, dil in DIL_CONFIGS]

    def body(do_ref, o1, l1, o4, l4, o16, l16, d1, d4, d16, e1, e4, e16, b0, b1, b2, b3):
        hm0 = lax.broadcasted_iota(jnp.int32, (1, 128), 1) < HEAD_DIM
        for cg in range(D_GRP // 128):
            dov = do_ref[:, cg * 128:(cg + 1) * 128].astype(F32)
            ov = [_from_residue(src, dil, cg, buf) for src, dil, buf in zip((o1, o4, o16), dils, (None, b0, b1))]
            lv = [_from_residue(src, dil, cg, buf) for src, dil, buf in zip((l1, l4, l16), dils, (None, b2, b3))]
            al = _dil_alphas(*lv)
            sbar = al[0] * _pair_sum(dov * ov[0], hm0)
            for a_c, o_c in zip(al[1:], ov[1:]):
                sbar = sbar + a_c * _pair_sum(dov * o_c, hm0)
            for a_c, dil, dref, eref in zip(al, dils, (d1, d4, d16), (e1, e4, e16)):
                _to_residue(dref, dil, cg, b0, a_c * dov)
                _to_residue(eref, dil, cg, b1, -a_c * sbar)

    specs = [_residue_spec(dil) for dil in dils]
    return pl.pallas_call(
        body, name="dil_comb_bwd", grid=(t // TM,),
        in_specs=[pl.BlockSpec((TM, D_GRP), lambda m: (m, 0))] + [sp for sp in specs for _ in range(2)],
        out_specs=specs + specs,
        out_shape=[_sds((t // dil, dil * D_GRP), BF) for dil in dils]
        + [_sds((t // dil, dil * D_GRP), F32) for dil in dils],
        scratch_shapes=[pltpu.VMEM((TM, 128), F32)] * 4,
        compiler_params=_cp(1))(do, os[0], ls[0], os[1], ls[1], os[2], ls[2])


def _dqkv_dil_sum(ds, dqkv6):
    t = dqkv6.shape[1]
    dils = [dil for _, dil in DIL_CONFIGS]

    def body(*refs):
        srcs, o_ref, acc = refs[:len(dils)], refs[len(dils) + 1], refs[len(dils) + 2]
        for a in range(3):
            for cg in range(D_GRP // 128):
                for src, dil in zip(srcs, dils):
                    for r in range(dil):
                        part = src[a, :, r * D_GRP + cg * 128:r * D_GRP + (cg + 1) * 128].astype(F32)
                        rows = pl.ds(r, TM // dil, stride=dil) if dil > 1 else slice(None)
                        if dil == dils[0]:
                            acc[rows, :] = part
                        else:
                            acc[rows, :] += part
                o_ref[a, :, cg * 128:(cg + 1) * 128] = acc[...].astype(BF)

    return pl.pallas_call(
        body, name="dqkv_dil_sum", grid=(t // TM,),
        in_specs=[pl.BlockSpec((3, TM // dil, dil * D_GRP), lambda m: (0, m, 0)) for dil in dils]
        + [pl.BlockSpec(memory_space=pl.ANY)],
        out_specs=pl.BlockSpec((3, TM, D_GRP), lambda m: (1, m, 0)),
        out_shape=_sds((6, t, D_GRP), BF), input_output_aliases={len(dils): 0},
        scratch_shapes=[pltpu.VMEM((TM, 128), F32)],
        compiler_params=_cp(1))(*ds, dqkv6)


def _relbias_grad(a_all, onehot):
    def body(a_ref, oh_ref, o_ref):
        acc = jnp.zeros((N_HEADS, N_BUCKETS), F32)
        for c in range(len(DIL_CONFIGS)):
            av = a_ref[c]
            hi = av.astype(BF)
            lo = (av - hi.astype(F32)).astype(BF)
            acc = acc + _nt(hi, oh_ref[c]) + _nt(lo, oh_ref[c])
        o_ref[...] = acc

    return pl.pallas_call(body, name="relbias_grad", out_shape=_sds((N_HEADS, N_BUCKETS), F32),
                          compiler_params=_cp())(a_all, onehot)


def _headnorm_bwd(dn, o, gv, mv):
    ms = _nn2(o * o, mv) * (1.0 / HEAD_DIM)
    r = lax.rsqrt(ms + EPS)
    nrm = o * r
    dg = jnp.sum(dn * nrm, axis=0, keepdims=True)
    dnn = dn * gv
    do = r * (dnn - nrm * (_nn2(dnn * nrm, mv) * (1.0 / HEAD_DIM)))
    return do, dg


def _mix_bwd_out(dx, gt, tv, w_out, o_sb, o_dil, on_sb, on_dil, g_sb, g_dil, ones_g, seq, carry=None):
    t, d = dx.shape
    per = seq // TM
    nb = t // seq

    def body(dx_ref, gt_ref, t_ref, w_ref, osb, odl, onsb, ondl, gsb, gdl, m_ref,
             dosb, dodl, dgt_ref, dgsb, dgdl, dw_ref):
        m = pl.program_id(0)
        dxv = dx_ref[...]
        dt = (gt_ref[...] * dxv).astype(BF)
        _acc_rows(dgt_ref, jnp.sum(dxv * t_ref[...].astype(F32), axis=0, keepdims=True), m % per == 0)
        mv = m_ref[...]
        don_sb = _nt(dt, w_ref[0:D_GRP, :])
        don_dl = _nt(dt, w_ref[D_GRP:2 * D_GRP, :])
        do1, dg1 = _headnorm_bwd(don_sb, osb[...].astype(F32), gsb[...], mv)
        do2, dg2 = _headnorm_bwd(don_dl, odl[...].astype(F32), gdl[...], mv)
        dosb[...] = do1.astype(BF)
        dodl[...] = do2.astype(BF)
        _acc_rows(dgsb, dg1, m == 0)
        _acc_rows(dgdl, dg2, m == 0)
        p1 = _tn(onsb[...], dt)
        p2 = _tn(ondl[...], dt)

        @pl.when(m == 0)
        def _():
            dw_ref[0:D_GRP, :] = p1
            dw_ref[D_GRP:2 * D_GRP, :] = p2

        @pl.when(m != 0)
        def _():
            dw_ref[0:D_GRP, :] += p1
            dw_ref[D_GRP:2 * D_GRP, :] += p2

    row = pl.BlockSpec((TM, d), lambda m: (m, 0))
    half = pl.BlockSpec((TM, D_GRP), lambda m: (m, 0))
    ex = pl.BlockSpec((None, 1, d), lambda m: (m // per, 0, 0))
    gvec = pl.BlockSpec((1, D_GRP), lambda m: (0, 0))
    wblk = pl.BlockSpec((2 * D_GRP, d), lambda m: (0, 0))
    return _call(
        body, "mix_bwd_out", (t // TM,),
        [row, ex, row, wblk, half, half, half, half, gvec, gvec, pl.BlockSpec((D_GRP, D_GRP), lambda m: (0, 0))],
        [half, half, ex, gvec, gvec, wblk],
        [_sds((t, D_GRP), BF), _sds((t, D_GRP), BF), _sds((nb, 1, d), F32),
         _sds((1, D_GRP), F32), _sds((1, D_GRP), F32), _sds((2 * D_GRP, d), F32)],
        (dx, gt, tv, w_out, o_sb, o_dil, on_sb, on_dil, g_sb, g_dil, ones_g), carry=carry)


def _dw_in(h, dqkv6, carry=None):
    t, d = h.shape
    wc = 6 * D_GRP // N_CHIPS

    def body(h_ref, g_ref, o_ref):
        kt = pl.program_id(0)
        hv = h_ref[...]
        for j in range(N_CHIPS):
            p = _tn(hv, _chip_cols(g_ref, j, wc))

            @pl.when(kt == 0)
            def _(p=p, j=j):
                o_ref[j, 0] = p

            @pl.when(kt != 0)
            def _(p=p, j=j):
                o_ref[j, 0] += p

    return _call(
        body, "dw_in", (t // TK_W,),
        [pl.BlockSpec((TK_W, d), lambda kt: (kt, 0)), pl.BlockSpec((6, TK_W, D_GRP), lambda kt: (0, kt, 0))],
        [pl.BlockSpec((N_CHIPS, 1, d, wc), lambda kt: (0, 0, 0, 0))],
        [_sds((N_CHIPS, 1, d, wc), F32)], (h, dqkv6), carry=carry)


def _mix_bwd_dh(dqkv6, w_in, x, g, sc, dxo, seq, carry=None):
    _, t, _ = dqkv6.shape
    d = x.shape[-1]
    wc = w_in.shape[-1]
    per = seq // TM
    nb = t // seq

    def body(g6_ref, w_ref, x_ref, g_ref, sc_ref, dxo_ref, dx_ref, dsh_ref, dsc_ref, dg_ref):
        m = pl.program_id(0)
        dh = _nt(_chip_cols(g6_ref, 0, wc), w_ref[0, 0])
        for j in range(1, N_CHIPS):
            dh = dh + _nt(_chip_cols(g6_ref, j, wc), w_ref[j, 0])
        dx, dsh, dsc, dg = _modnorm_bwd_tile(dh, x_ref[...], g_ref[...], sc_ref[...], dxo_ref[...])
        dx_ref[...] = dx
        _acc_rows(dsh_ref, dsh, m % per == 0)
        _acc_rows(dsc_ref, dsc, m % per == 0)
        _acc_rows(dg_ref, dg, m == 0)

    row = pl.BlockSpec((TM, d), lambda m: (m, 0))
    ex = pl.BlockSpec((None, 1, d), lambda m: (m // per, 0, 0))
    vec = pl.BlockSpec((1, d), lambda m: (0, 0))
    return _call(
        body, "mix_bwd_dh", (t // TM,),
        [pl.BlockSpec((6, TM, D_GRP), lambda m: (0, m, 0)), _whole(w_in), row, vec, ex, row],
        [row, ex, ex, vec],
        [_sds((t, d), F32), _sds((nb, 1, d), F32), _sds((nb, 1, d), F32), _sds((1, d), F32)],
        (dqkv6, w_in, x, g, sc, dxo), carry=carry)


def _ffn_down_loss(s, wd, x, gt, seq, coef, g, target):
    _, t, fs = s.shape
    d = x.shape[-1]
    per = seq // TM
    steps = t // TM

    def body(s_ref, w_ref, x_ref, gt_ref, g_ref, t_ref, f_ref, dx_ref, dg_ref, loss_ref, lacc):
        m = pl.program_id(0)
        f = _nn(s_ref[0], w_ref[0, 0])
        for j in range(1, N_CHIPS):
            f = f + _nn(s_ref[j], w_ref[j, 0])
        f_ref[...] = f.astype(BF)
        xv = x_ref[...] + (coef * gt_ref[...]) * f
        gv = g_ref[...]
        r = lax.rsqrt(jnp.mean(xv * xv, axis=-1, keepdims=True) + EPS)
        n = xv * r
        err = n * gv - t_ref[...]
        dy = err * (1.0 / d)
        _acc_rows(dg_ref, jnp.sum(dy * n, axis=0, keepdims=True), m == 0)
        dn = dy * gv
        dx_ref[...] = r * (dn - n * jnp.mean(dn * n, axis=-1, keepdims=True))
        _acc_rows(lacc, jnp.sum(err * err, axis=0, keepdims=True), m == 0)

        @pl.when(m == steps - 1)
        def _():
            tot = jnp.sum(lacc[...], axis=-1, keepdims=True) * (0.5 / d)
            loss_ref[...] = jnp.broadcast_to(tot, (1, 128))

    row = pl.BlockSpec((TM, d), lambda m: (m, 0))
    vec = pl.BlockSpec((1, d), lambda m: (0, 0))
    return pl.pallas_call(
        body, name="ffn_down_loss", grid=(steps,),
        in_specs=[pl.BlockSpec((N_CHIPS, TM, fs), lambda m: (0, m, 0)), _whole(wd), row,
                  pl.BlockSpec((None, 1, d), lambda m: (m // per, 0, 0)), vec, row],
        out_specs=[row, row, vec, pl.BlockSpec((1, 128), lambda m: (0, 0))],
        out_shape=[_sds((t, d), BF), _sds((t, d), F32), _sds((1, d), F32), _sds((1, 128), F32)],
        scratch_shapes=[pltpu.VMEM((1, d), F32)],
        compiler_params=_cp(1))(s, wd, x, gt, g, target)


def _row_tile(rows, cols):
    best = rows
    for tr in range(8, rows + 1, 8):
        if rows % tr == 0 and tr * cols * 4 <= (1 << 20):
            best = tr
    if best * cols * 4 > (1 << 21):
        best = 8
    return best


def _adamw(w, g_arr, g_sel, m, v):
    rows, cols = w.shape
    tr = _row_tile(rows, cols)
    b1c = 1.0 - ADAM_B1 ** ADAM_STEP
    b2c = 1.0 - ADAM_B2 ** ADAM_STEP

    def body(w_ref, g_ref, m_ref, v_ref, go_ref, d_ref, mo_ref, vo_ref):
        gv = g_ref[...]
        mn = ADAM_B1 * m_ref[...] + (1.0 - ADAM_B1) * gv
        vn = ADAM_B2 * v_ref[...] + (1.0 - ADAM_B2) * (gv * gv)
        go_ref[...] = gv
        mo_ref[...] = mn
        vo_ref[...] = vn
        d_ref[...] = -ADAM_LR * ((mn / b1c) / (jnp.sqrt(vn / b2c) + ADAM_EPS) + ADAM_WD * w_ref[...])

    blk = pl.BlockSpec((tr, cols), lambda i: (i, 0))
    shp = _sds((rows, cols), F32)
    return pl.pallas_call(
        body, name="adamw", grid=(rows // tr,),
        in_specs=[blk, pl.BlockSpec((None, tr, cols), lambda i: (g_sel, i, 0)), blk, blk],
        out_specs=[blk] * 4, out_shape=[shp] * 4,
        compiler_params=_cp(1))(w, g_arr, m, v)


def _flip(v, bit):
    return 1 - v if bit else v


def _my_place():
    x, y, c = lax.axis_index("x"), lax.axis_index("y"), lax.axis_index("c")
    return x, y, c


class _Exchange:
    def __init__(self, operands, out_shape, aliases, sems, start, finish):
        self.operands, self.out_shape, self.aliases, self.sems = list(operands), list(out_shape), dict(aliases), list(sems)
        self.start, self.finish = start, finish


def _join(exchanges):
    exchanges = [e for e in exchanges if e is not None]
    if not exchanges:
        return None
    ops, outs, sems, aliases, spans = [], [], [], {}, []
    for e in exchanges:
        spans.append((len(ops), len(outs), len(sems), e))
        for i, j in e.aliases.items():
            aliases[len(ops) + i] = len(outs) + j
        ops += e.operands
        outs += e.out_shape
        sems += e.sems

    def run(which):
        def go(ins, res, sm):
            for io, oo, so, e in spans:
                getattr(e, which)(ins[io:io + len(e.operands)], res[oo:oo + len(e.out_shape)], sm[so:so + len(e.sems)])
        return go

    return _Exchange(ops, outs, aliases, sems, run("start"), run("finish"))


def _call(body, name, grid, in_specs, out_specs, out_shape, args, scratch=(), carry=None, io_alias=None):
    in_specs, out_specs, out_shape, scratch = list(in_specs), list(out_specs), list(out_shape), list(scratch)
    io_alias = dict(io_alias or {})
    if carry is None:
        return pl.pallas_call(body, name=name, grid=grid, in_specs=in_specs, out_specs=out_specs,
                              out_shape=out_shape, scratch_shapes=scratch, input_output_aliases=io_alias,
                              compiler_params=_cp(len(grid)))(*args)
    n_in, n_out, n_s = len(in_specs), len(out_specs), len(scratch)
    c_in, c_out = len(carry.operands), len(carry.out_shape)
    any_spec = pl.BlockSpec(memory_space=pl.ANY)

    def wrapped(*refs):
        ins, cins = refs[:n_in], refs[n_in:n_in + c_in]
        o0 = n_in + c_in
        outs, couts = refs[o0:o0 + n_out], refs[o0 + n_out:o0 + n_out + c_out]
        s0 = o0 + n_out + c_out
        scr, sems = refs[s0:s0 + n_s], refs[s0 + n_s:]
        first = pl.program_id(0) == 0
        last = pl.program_id(0) == grid[0] - 1
        for ax in range(1, len(grid)):
            first = jnp.logical_and(first, pl.program_id(ax) == 0)
            last = jnp.logical_and(last, pl.program_id(ax) == grid[ax] - 1)

        @pl.when(first)
        def _():
            carry.start(cins, couts, sems)

        body(*ins, *outs, *scr)

        @pl.when(last)
        def _():
            carry.finish(cins, couts, sems)

    return pl.pallas_call(
        wrapped, name=name, grid=grid, in_specs=in_specs + [any_spec] * c_in,
        out_specs=out_specs + [any_spec] * c_out, out_shape=out_shape + carry.out_shape,
        scratch_shapes=scratch + carry.sems,
        input_output_aliases={**io_alias, **{n_in + i: n_out + j for i, j in carry.aliases.items()}},
        compiler_params=_cp(len(grid)))(*args, *carry.operands)


def _whole_call(body, name, args, out_shape, scratch, carry=None):
    vm = pl.BlockSpec(memory_space=pltpu.VMEM)
    any_spec = pl.BlockSpec(memory_space=pl.ANY)
    out_shape, scratch = list(out_shape), list(scratch)
    n_in, n_out, n_s = len(args), len(out_shape), len(scratch)
    if carry is None:
        return pl.pallas_call(body, name=name, in_specs=[vm] * n_in, out_specs=[vm] * n_out, out_shape=out_shape,
                              scratch_shapes=scratch, compiler_params=_cp())(*args)
    c_in, c_out = len(carry.operands), len(carry.out_shape)

    def wrapped(*refs):
        ins, cins = refs[:n_in], refs[n_in:n_in + c_in]
        o0 = n_in + c_in
        outs, couts = refs[o0:o0 + n_out], refs[o0 + n_out:o0 + n_out + c_out]
        s0 = o0 + n_out + c_out
        scr, sems = refs[s0:s0 + n_s], refs[s0 + n_s:]
        carry.start(cins, couts, sems)
        body(*ins, *outs, *scr)
        carry.finish(cins, couts, sems)

    return pl.pallas_call(
        wrapped, name=name, in_specs=[vm] * n_in + [any_spec] * c_in, out_specs=[vm] * n_out + [any_spec] * c_out,
        out_shape=out_shape + carry.out_shape, scratch_shapes=scratch + carry.sems,
        input_output_aliases={n_in + i: n_out + j for i, j in carry.aliases.items()},
        compiler_params=_cp())(*args, *carry.operands)


def _alone(name, ex):
    any_spec = pl.BlockSpec(memory_space=pl.ANY)
    c_in, c_out = len(ex.operands), len(ex.out_shape)

    def body(*refs):
        ins, outs, sems = refs[:c_in], refs[c_in:c_in + c_out], refs[c_in + c_out:]
        ex.start(ins, outs, sems)
        ex.finish(ins, outs, sems)

    return pl.pallas_call(
        body, name=name, in_specs=[any_spec] * c_in, out_specs=[any_spec] * c_out, out_shape=ex.out_shape,
        scratch_shapes=ex.sems, input_output_aliases=ex.aliases, compiler_params=_cp())(*ex.operands)


def _ada_fwd(c_pad, w_ada, b_shard, carry=None):
    d = c_pad.shape[-1]
    cols = w_ada.shape[-1]
    chunk = 384

    def body(c_ref, w_ref, b_ref, call_ref, mod_ref, part, s1, r1, s2, r2):
        x, y, c = _my_place()
        dev = 4 * x + 2 * y + c
        chip = 2 * x + y
        call_ref[dev] = c_ref[...]

        def c_copy(k):
            px, py, pc = _flip(x, (k >> 2) & 1), _flip(y, (k >> 1) & 1), _flip(c, k & 1)
            return px, py, pc

        sends = []
        for k in range(1, N_DEV):
            px, py, pc = c_copy(k)
            cp = pltpu.make_async_remote_copy(src_ref=c_ref, dst_ref=call_ref.at[dev], send_sem=s1.at[k - 1],
                                              recv_sem=r1.at[k - 1], device_id=(px, py, pc), device_id_type=MESH)
            cp.start()
            sends.append(cp)
        for k in range(1, N_DEV):
            px, py, pc = c_copy(k)
            pltpu.make_async_remote_copy(src_ref=c_ref, dst_ref=call_ref.at[4 * px + 2 * py + pc],
                                         send_sem=s1.at[k - 1], recv_sem=r1.at[k - 1],
                                         device_id=(px, py, pc), device_id_type=MESH).wait_recv()
        for cp in sends:
            cp.wait_send()

        cs = call_ref[...].reshape(N_DEV * 8, d)
        sc = (cs * jax.nn.sigmoid(cs)).astype(BF)
        for n0 in range(0, cols, chunk):
            blk = _nn(sc, w_ref[:, n0:n0 + chunk].astype(BF)) + b_ref[:, n0:n0 + chunk]
            part[:, :, n0:n0 + chunk] = blk.reshape(N_DEV, 8, chunk)

        mod_ref[chip] = part[dev]
        sends = []
        for kk in range(1, N_CHIPS):
            px, py = _flip(x, (kk >> 1) & 1), _flip(y, kk & 1)
            cp = pltpu.make_async_remote_copy(src_ref=part.at[4 * px + 2 * py + c], dst_ref=mod_ref.at[chip],
                                              send_sem=s2.at[kk - 1], recv_sem=r2.at[kk - 1],
                                              device_id=(px, py, c), device_id_type=MESH)
            cp.start()
            sends.append(cp)
        for kk in range(1, N_CHIPS):
            px, py = _flip(x, (kk >> 1) & 1), _flip(y, kk & 1)
            pltpu.make_async_remote_copy(src_ref=part.at[dev], dst_ref=mod_ref.at[2 * px + py],
                                         send_sem=s2.at[kk - 1], recv_sem=r2.at[kk - 1],
                                         device_id=(px, py, c), device_id_type=MESH).wait_recv()
        for cp in sends:
            cp.wait_send()

    return _whole_call(
        body, "ada_fwd", (c_pad, w_ada, b_shard),
        [_sds((N_DEV, 8, d), F32), _sds((N_CHIPS, 8, cols), F32)],
        [pltpu.VMEM((N_DEV, 8, cols), F32),
         pltpu.SemaphoreType.DMA((N_DEV - 1,)), pltpu.SemaphoreType.DMA((N_DEV - 1,)),
         pltpu.SemaphoreType.DMA((N_CHIPS - 1,)), pltpu.SemaphoreType.DMA((N_CHIPS - 1,))], carry=carry)


def _ag_weights(bufs, kks=(1, 2, 3), relative=False):
    n, nk = len(bufs), len(kks)

    def half(b, which):
        hr = bufs[b].shape[2] // 2
        return pl.ds(pl.multiple_of(which * hr, 16), hr)

    def copies(outs, sems, b, i, kk):
        x, y, c = _my_place()
        chip = 2 * x + y
        px, py = _flip(x, (kk >> 1) & 1), _flip(y, kk & 1)
        mine, theirs = (0, kk) if relative else (chip, 2 * px + py)
        landing = kk if relative else chip
        k = nk * b + i
        send = pltpu.make_async_remote_copy(
            src_ref=outs[b].at[mine, :, half(b, c), :], dst_ref=outs[b].at[landing, :, half(b, c), :],
            send_sem=sems[0].at[k], recv_sem=sems[1].at[k], device_id=(px, py, c), device_id_type=MESH)
        got = outs[b].at[theirs, :, half(b, c), :]
        recv = pltpu.make_async_remote_copy(
            src_ref=got, dst_ref=got, send_sem=sems[0].at[k], recv_sem=sems[1].at[k],
            device_id=(px, py, c), device_id_type=MESH)
        fwd = pltpu.make_async_remote_copy(
            src_ref=got, dst_ref=got, send_sem=sems[2].at[k], recv_sem=sems[3].at[k],
            device_id=(x, y, 1 - c), device_id_type=MESH)
        other = outs[b].at[theirs, :, half(b, 1 - c), :]
        back = pltpu.make_async_remote_copy(
            src_ref=other, dst_ref=other, send_sem=sems[2].at[k], recv_sem=sems[3].at[k],
            device_id=(x, y, 1 - c), device_id_type=MESH)
        return send, recv, fwd, back

    def each(outs, sems):
        for b in range(n):
            for i, kk in enumerate(kks):
                yield copies(outs, sems, b, i, kk)

    def start(ins, outs, sems):
        for send, _, _, _ in each(outs, sems):
            send.start()

    def finish(ins, outs, sems):
        for _, recv, fwd, _ in each(outs, sems):
            recv.wait_recv()
            fwd.start()
        for send, _, fwd, back in each(outs, sems):
            back.wait_recv()
            send.wait_send()
            fwd.wait_send()

    return _Exchange(bufs, [_sds(s.shape, s.dtype) for s in bufs], {i: i for i in range(n)},
                     [pltpu.SemaphoreType.DMA((nk * n,))] * 4, start, finish)


def _rs_d2d(grads):
    n = len(grads)

    def copy(ins, outs, sems, b):
        x, y, c = _my_place()
        hr = grads[b].shape[2] // 2
        theirs = pl.ds(pl.multiple_of((1 - c) * hr, 8), hr)
        return pltpu.make_async_remote_copy(
            src_ref=ins[b].at[:, :, theirs, :], dst_ref=outs[b], send_sem=sems[0].at[b], recv_sem=sems[1].at[b],
            device_id=(x, y, 1 - c), device_id_type=MESH)

    def start(ins, outs, sems):
        for b in range(n):
            copy(ins, outs, sems, b).start()

    def finish(ins, outs, sems):
        for b in range(n):
            copy(ins, outs, sems, b).wait()

    return _Exchange(grads, [_sds(g.shape[:2] + (g.shape[2] // 2, g.shape[3]), F32) for g in grads], {},
                     [pltpu.SemaphoreType.DMA((n,))] * 2, start, finish)


def _add_halves(core, g, land):
    nchip, ng, rows, cols = g.shape
    hr = rows // 2
    tr = _row_tile(hr, cols)
    steps = hr // tr

    def body(core_ref, g_ref, l_ref, o_ref):
        del core_ref
        o_ref[...] = (g_ref[...] + l_ref[...]).astype(BF)

    return pl.pallas_call(
        body, name="add_halves",
        grid_spec=pltpu.PrefetchScalarGridSpec(
            num_scalar_prefetch=1, grid=(nchip, ng, steps),
            in_specs=[pl.BlockSpec((None, None, tr, cols), lambda j, a, i, cr: (j, a, cr[0] * steps + i, 0)),
                      pl.BlockSpec((None, None, tr, cols), lambda j, a, i, cr: (j, a, i, 0))],
            out_specs=pl.BlockSpec((None, None, tr, cols), lambda j, a, i, cr: (j, a, i, 0))),
        out_shape=_sds((nchip, ng, hr, cols), BF),
        compiler_params=_cp(3))(core, g, land)


def _rs_ici(parts, relative=False):
    n = len(parts)

    def copies(ins, outs, sems):
        x, y, c = _my_place()
        chip = 2 * x + y
        for b in range(n):
            for kk in range(1, N_CHIPS):
                px, py = _flip(x, (kk >> 1) & 1), _flip(y, kk & 1)
                k = 3 * b + kk - 1
                theirs, landing = (kk, kk) if relative else (2 * px + py, chip)
                send = pltpu.make_async_remote_copy(
                    src_ref=ins[b].at[theirs], dst_ref=outs[b].at[landing],
                    send_sem=sems[0].at[k], recv_sem=sems[1].at[k], device_id=(px, py, c), device_id_type=MESH)
                slot = outs[b].at[theirs]
                recv = pltpu.make_async_remote_copy(
                    src_ref=slot, dst_ref=slot, send_sem=sems[0].at[k], recv_sem=sems[1].at[k],
                    device_id=(px, py, c), device_id_type=MESH)
                yield send, recv

    def start(ins, outs, sems):
        for send, _ in copies(ins, outs, sems):
            send.start()

    def finish(ins, outs, sems):
        for send, recv in copies(ins, outs, sems):
            recv.wait_recv()
            send.wait_send()

    return _Exchange(parts, [_sds(p.shape, p.dtype) for p in parts], {},
                     [pltpu.SemaphoreType.DMA((3 * n,))] * 2, start, finish)


def _sum_chips(place, part, land, relative=False):
    nchip, ng, hr, cols = land.shape
    tr = _row_tile(hr, cols)
    steps = hr // tr

    def body(place_ref, p_ref, l1, l2, l3, o_ref):
        del place_ref
        o_ref[...] = ((p_ref[...].astype(F32) + l1[...].astype(F32)) + l2[...].astype(F32)) + l3[...].astype(F32)

    def slot(k):
        if relative:
            return pl.BlockSpec((None, None, tr, cols), lambda a, i, pr: (k, a, i, 0))
        return pl.BlockSpec((None, None, tr, cols), lambda a, i, pr: (jnp.bitwise_xor(pr[1], k), a, i, 0))

    return pl.pallas_call(
        body, name="sum_chips",
        grid_spec=pltpu.PrefetchScalarGridSpec(
            num_scalar_prefetch=1, grid=(ng, steps),
            in_specs=[slot(0), slot(1), slot(2), slot(3)],
            out_specs=pl.BlockSpec((None, tr, cols), lambda a, i, pr: (a, pr[0] * steps + i, 0))),
        out_shape=_sds((ng, 2 * hr, cols), F32),
        compiler_params=_cp(2))(place, part, land, land, land)


def _rs_final(bufs):
    n = len(bufs)

    def copy(outs, sems, b, which):
        x, y, c = _my_place()
        hr = bufs[b].shape[1] // 2
        rows = outs[b].at[:, pl.ds(pl.multiple_of((c if which == 0 else 1 - c) * hr, 8), hr), :]
        return pltpu.make_async_remote_copy(
            src_ref=rows, dst_ref=rows, send_sem=sems[0].at[b], recv_sem=sems[1].at[b],
            device_id=(x, y, 1 - c), device_id_type=MESH)

    def start(ins, outs, sems):
        for b in range(n):
            copy(outs, sems, b, 0).start()

    def finish(ins, outs, sems):
        for b in range(n):
            copy(outs, sems, b, 0).wait_send()
            copy(outs, sems, b, 1).wait_recv()

    return _Exchange(bufs, [_sds(h.shape, F32) for h in bufs], {i: i for i in range(n)},
                     [pltpu.SemaphoreType.DMA((n,))] * 2, start, finish)


def _small_sync(smalls, dmod_blk, c_all, carry=None):
    d = c_all.shape[-1]
    cols = dmod_blk.shape[-1]
    chunk = 384

    def body(sm_ref, dm_ref, c_ref, sum_ref, gw_ref, sm_all, dm_all, ssem, rsem):
        x, y, c = _my_place()
        dev = 4 * x + 2 * y + c
        chip = 2 * x + y
        sm_all[dev] = sm_ref[...]
        dm_all[dev] = dm_ref[chip]
        sends = []
        for k in range(1, N_DEV):
            px, py, pc = _flip(x, (k >> 2) & 1), _flip(y, (k >> 1) & 1), _flip(c, k & 1)
            a = pltpu.make_async_remote_copy(src_ref=sm_ref, dst_ref=sm_all.at[dev], send_sem=ssem.at[2 * (k - 1)],
                                             recv_sem=rsem.at[2 * (k - 1)], device_id=(px, py, pc),
                                             device_id_type=MESH)
            b = pltpu.make_async_remote_copy(src_ref=dm_ref.at[2 * px + py], dst_ref=dm_all.at[dev],
                                             send_sem=ssem.at[2 * (k - 1) + 1], recv_sem=rsem.at[2 * (k - 1) + 1],
                                             device_id=(px, py, pc), device_id_type=MESH)
            a.start()
            b.start()
            sends += [a, b]
        for k in range(1, N_DEV):
            px, py, pc = _flip(x, (k >> 2) & 1), _flip(y, (k >> 1) & 1), _flip(c, k & 1)
            pdev = 4 * px + 2 * py + pc
            pltpu.make_async_remote_copy(src_ref=sm_ref, dst_ref=sm_all.at[pdev], send_sem=ssem.at[2 * (k - 1)],
                                         recv_sem=rsem.at[2 * (k - 1)], device_id=(px, py, pc),
                                         device_id_type=MESH).wait_recv()
            pltpu.make_async_remote_copy(src_ref=dm_ref.at[chip], dst_ref=dm_all.at[pdev],
                                         send_sem=ssem.at[2 * (k - 1) + 1], recv_sem=rsem.at[2 * (k - 1) + 1],
                                         device_id=(px, py, pc), device_id_type=MESH).wait_recv()
        for cp in sends:
            cp.wait_send()

        tot = sm_all[0]
        for q in range(1, N_DEV):
            tot = tot + sm_all[q]
        sum_ref[...] = tot

        cs = c_ref[...].reshape(N_DEV * 8, d)
        sc = (cs * jax.nn.sigmoid(cs)).astype(BF)
        for n0 in range(0, cols, chunk):
            dmv = dm_all[:, :, n0:n0 + chunk].reshape(N_DEV * 8, chunk).astype(BF)
            gw_ref[:, n0:n0 + chunk] = _tn(sc, dmv)

    return _whole_call(
        body, "small_sync", (smalls, dmod_blk, c_all),
        [_sds(smalls.shape, F32), _sds((d, cols), F32)],
        [pltpu.VMEM((N_DEV,) + smalls.shape, F32), pltpu.VMEM((N_DEV, 8, cols), F32),
         pltpu.SemaphoreType.DMA((2 * (N_DEV - 1),)), pltpu.SemaphoreType.DMA((2 * (N_DEV - 1),))], carry=carry)


def _bucket_onehot():
    maps = np.stack([_bucket_map(dil).reshape(-1) for _, dil in DIL_CONFIGS])
    return (jnp.asarray(maps)[:, None, :] == jnp.arange(N_BUCKETS, dtype=jnp.int32)[None, :, None]).astype(BF)


def _dil_bias(rel_t, onehot):
    def body(r_ref, oh_ref, o_ref):
        rv = r_ref[...]
        hi = rv.astype(BF)
        lo = (rv - hi.astype(F32)).astype(BF)
        for c in range(len(DIL_CONFIGS)):
            o_ref[c] = _nn(hi, oh_ref[c]) + _nn(lo, oh_ref[c])

    return pl.pallas_call(body, name="dil_bias",
                          out_shape=_sds((len(DIL_CONFIGS), N_HEADS, BLOCK * 2 * BLOCK), F32),
                          compiler_params=_cp())(rel_t, onehot)


def _rowsum8(a):
    def body(a_ref, o_ref):
        o_ref[...] = jnp.sum(a_ref[...], axis=0, keepdims=True)

    return pl.pallas_call(body, name="rowsum8", out_shape=_sds((1, a.shape[1]), F32), compiler_params=_cp())(a)


def _local_step(x, mod, target, w, gains, rel_bias, place=None):
    nb, seq, d = x.shape
    t = nb * seq
    dist = place is not None
    core = place[0:1] if dist else None
    x0 = x.reshape(t, d)
    tgt = target.reshape(t, d)
    md = [mod[:, i:i + 1, :] for i in range(N_MOD)]
    sh1, sc1, gt1, sh2, sc2, gt2, sh3, sc3, gt3 = md
    g1, g2, g3 = gains["g_ffn1"], gains["g_mix"], gains["g_ffn2"]
    ones_g = _group_ones()

    def partial_sums(grads, lands):
        return [_add_halves(core, g, l) for g, l in zip(grads, lands)]

    def chip_sums(parts, lands):
        return [_sum_chips(place, p, l, relative=True) for p, l in zip(parts, lands)]

    gu1 = w["gu1"]
    res = _ffn_up(x0, g1, sc1, sh1, gu1, seq,
                  carry=_join([_ag_weights([w["d1"]], relative=True), _ag_weights([w["win"]])]) if dist else None)
    h1, a1, u1, s1 = res[:4]
    wd1, w_in = res[4:] if dist else (w["d1"], w["win"])
    f1, x1 = _ffn_down(s1, wd1, x0, gt1, seq, 0.5)

    h2, qkv6, qkv_r4, qkv_r16 = _qkv_proj(x1, g2, sc2, sh2, w_in, seq)
    qkv6b = qkv6.reshape(6, nb, seq, D_GRP)
    res = _sb_fwd(qkv6b, gains["g_sb_out"], nb, seq,
                  carry=_join([_ag_weights([w["gu2"], w["d2"]], relative=True), _ag_weights([w["wout"]])])
                  if dist else None)
    o_sb, on_sb = res[:2]
    wgu2, wd2, w_out = res[2:] if dist else (w["gu2"], w["d2"], w["wout"])
    w_out2 = w_out.reshape(2 * D_GRP, d)
    onehot = _bucket_onehot()
    bias = _dil_bias(rel_bias.T, onehot).reshape(len(DIL_CONFIGS), N_HEADS * BLOCK, 2 * BLOCK)
    o_cs, l_cs = [], []
    qkv_rs = [(qkv6b, 3), (qkv_r4, 0), (qkv_r16, 0)]
    for ci, (_, dil) in enumerate(DIL_CONFIGS):
        sub = seq // dil
        arr, base = qkv_rs[ci]
        arr = arr.reshape(base + 3, nb, sub, dil * D_GRP)
        qkv_rs[ci] = (arr, base)
        o_c, l_c = _dil_fwd(arr, base, bias[ci], nb, sub, dil)
        o_cs.append(o_c.reshape(t // dil, dil * D_GRP))
        l_cs.append(l_c.reshape(t // dil, dil * D_GRP))
    o_dil, on_dil = _dil_comb(o_cs, l_cs, gains["g_dil_out"])
    tmix, x2 = _mix_out(on_sb.reshape(t, D_GRP), on_dil, w_out2, x1, gt2, seq)

    h3, a3, u3, s3 = _ffn_up(x2, g3, sc3, sh3, wgu2, seq)
    f3, dx3, dg_final, loss = _ffn_down_loss(s3, wd2, x2, gt3, seq, 0.5, gains["g_final"], tgt)

    da3, du3, df3, dgt3, dx2, dsh3, dsc3, dg3 = _ffn_bwd_x(dx3, gt3, f3, wd2, a3, u3, wgu2, x2, g3, sc3, seq, 0.5)
    grads2 = [_ffn_bwd_w(h3, da3, du3, s3, df3)]

    res = _mix_bwd_out(
        dx2, gt2, tmix, w_out2, o_sb.reshape(t, D_GRP), o_dil, on_sb.reshape(t, D_GRP), on_dil,
        gains["g_sb_out"], gains["g_dil_out"], ones_g, seq, carry=_rs_d2d(grads2) if dist else None)
    do_sb, do_dil, dgt2, dg_sb, dg_dil, dw_out = res[:6]
    parts2 = partial_sums(grads2, res[6:]) if dist else None
    dw_out = dw_out.reshape(N_CHIPS, 1, 2 * D_GRP // N_CHIPS, d)
    res = _sb_bwd(qkv6b, do_sb.reshape(nb, seq, D_GRP), nb, seq,
                  carry=_rs_ici(parts2, relative=True) if dist else None)
    dqkv6 = res[0]
    halves2 = chip_sums(parts2, res[1:]) if dist else None
    dcs = _dil_comb_bwd(do_dil, o_cs, l_cs)
    dsum, a_tiles = [], []
    for ci, (_, dil) in enumerate(DIL_CONFIGS):
        sub = seq // dil
        do_c = dcs[ci].reshape(nb, sub, dil * D_GRP)
        dd_c = dcs[3 + ci].reshape(nb, sub, dil * D_GRP)
        res = _dil_bwd(qkv_rs[ci][0], qkv_rs[ci][1], bias[ci], do_c, dd_c, nb, sub, dil)
        dsum.append(res[0].reshape(3, t // dil, dil * D_GRP))
        a_tiles.append(res[1].reshape(N_HEADS, BLOCK * 2 * BLOCK))
    dqkv6 = _dqkv_dil_sum(dsum, dqkv6.reshape(6, t, D_GRP))
    drel = _relbias_grad(jnp.stack(a_tiles), onehot)
    dx1, dsh2, dsc2, dg2 = _mix_bwd_dh(dqkv6, w_in, x1, g2, sc2, dx2, seq)

    da1, du1, df1, dgt1 = _ffn_bwd_ds(dx1, gt1, f1, wd1, a1, u1, seq, 0.5)
    grads1 = [_ffn_bwd_w(h1, da1, du1, s1, df1)]
    res = _dw_in(h2, dqkv6, carry=_join([_rs_d2d(grads1), _rs_final(halves2)]) if dist else None)
    grads_m = [res[0], dw_out]
    parts1 = partial_sums(grads1, res[1:2]) if dist else None
    if dist:
        grads2 = res[2:3]
    res = _ffn_bwd_dh(da1, du1, gu1, x0, g1, sc1, dx1, seq,
                      carry=_join([_rs_ici(parts1, relative=True), _rs_d2d(grads_m)]) if dist else None)
    dx0, dsh1, dsc1, dg1 = res[:4]
    pending = None
    if dist:
        pending = (chip_sums(parts1, res[4:5]), partial_sums(grads_m, res[5:7]))

    dmod = jnp.concatenate([dsh1, dsc1, dgt1, dsh2, dsc2, dgt2, dsh3, dsc3, dgt3], axis=1)
    return dict(grad_x=dx0.reshape(nb, seq, d), loss=loss[0, 0], dmod=dmod.reshape(nb, N_MOD * d),
                dffn1=grads1[0], dffn2=grads2[0], dwin=grads_m[0], dwout=grads_m[1], pending=pending,
                dg_ffn1=dg1, dg_mix=dg2, dg_ffn2=dg3, dg_final=dg_final, dg_sb=dg_sb, dg_dil=dg_dil,
                drel=drel.T)


_SMALL_ORDER = (("b_ada", N_MOD * 1024), ("g_ffn1", 1024), ("g_mix", 1024), ("g_ffn2", 1024), ("g_final", 1024),
                ("g_sb_out", D_GRP), ("g_dil_out", D_GRP), ("rel_bias", N_BUCKETS * N_HEADS))


def _pack_small(parts, extra=None):
    flat = [parts[name].reshape(-1).astype(F32) for name, _ in _SMALL_ORDER]
    used = sum(sz for _, sz in _SMALL_ORDER)
    pad = SMALL_ROWS * 128 - used
    tail = jnp.zeros((pad,), F32)
    if extra is not None:
        tail = tail.at[0].set(extra)
    return jnp.concatenate(flat + [tail]).reshape(SMALL_ROWS, 128)


def _unpack_small(packed, shapes):
    flat = packed.reshape(-1)
    out, off = {}, 0
    for name, sz in _SMALL_ORDER:
        out[name] = flat[off:off + sz].reshape(shapes[name])
        off += sz
    return out, flat[off]


def kernel(x, c, w_ada, b_ada, g_ffn1, w1_gate, w1_up, w1_down, g_mix, w_in, g_sb_out, g_dil_out, w_out, rel_bias, g_ffn2, w2_gate, w2_up, w2_down, g_final, loss_target, m_w_ada, m_b_ada, m_g_ffn1, m_w1_gate, m_w1_up, m_w1_down, m_g_mix, m_w_in, m_g_sb_out, m_g_dil_out, m_w_out, m_rel_bias, m_g_ffn2, m_w2_gate, m_w2_up, m_w2_down, m_g_final, v_w_ada, v_b_ada, v_g_ffn1, v_w1_gate, v_w1_up, v_w1_down, v_g_mix, v_w_in, v_g_sb_out, v_g_dil_out, v_w_out, v_rel_bias, v_g_ffn2, v_w2_gate, v_w2_up, v_w2_down, v_g_final):
    nb, seq, d = x.shape
    xi, yi, ci = lax.axis_index("x"), lax.axis_index("y"), lax.axis_index("c")
    chip = 2 * xi + yi
    ada_cols = w_ada.shape[-1]

    c_pad = jnp.zeros((8, d), F32).at[:nb].set(c)
    b_shard = lax.dynamic_slice(b_ada, (0, chip * ada_cols), (1, ada_cols))
    shards = dict(gu1=jnp.stack([w1_gate[0], w1_up[0]]), d1=w1_down, win=w_in, wout=w_out,
                  gu2=jnp.stack([w2_gate[0], w2_up[0]]), d2=w2_down)
    bufs = {k: lax.dynamic_update_slice(lax.empty((N_CHIPS,) + s.shape, BF), s.astype(BF)[None],
                                        (chip if k in ("win", "wout") else 0, 0, 0, 0))
            for k, s in shards.items()}
    c_all, mod_blk, bufs["gu1"] = _ada_fwd(c_pad, w_ada[0], b_shard,
                                           carry=_ag_weights([bufs["gu1"]], relative=True))
    mod = jnp.transpose(mod_blk[:, :nb, :], (1, 0, 2)).reshape(nb, N_MOD, d)

    gains = dict(g_ffn1=g_ffn1, g_mix=g_mix, g_ffn2=g_ffn2, g_final=g_final.reshape(1, d),
                 g_sb_out=g_sb_out.reshape(1, D_GRP), g_dil_out=g_dil_out.reshape(1, D_GRP))
    place = jnp.stack([ci, chip]).astype(jnp.int32)
    r = _local_step(x, mod, loss_target, bufs, gains, rel_bias, place)

    dmod = r["dmod"]
    dmod_pad = jnp.zeros((8, N_MOD * d), F32).at[:nb].set(dmod)
    dmod_blk = jnp.transpose(dmod_pad.reshape(8, N_CHIPS, ada_cols), (1, 0, 2))
    small_parts = dict(b_ada=_rowsum8(dmod_pad), g_ffn1=r["dg_ffn1"], g_mix=r["dg_mix"], g_ffn2=r["dg_ffn2"],
                       g_final=r["dg_final"], g_sb_out=r["dg_sb"], g_dil_out=r["dg_dil"], rel_bias=r["drel"])
    halves1, parts_m = r["pending"]
    res = _small_sync(_pack_small(small_parts, r["loss"]), dmod_blk, c_all,
                      carry=_join([_rs_final(halves1), _rs_ici(parts_m)]))
    small_sum, g_wada, gffn1 = res[:3]
    halves_m = [_sum_chips(place, p, l) for p, l in zip(parts_m, res[3:5])]
    gwin, gwout = _alone("rs_last", _rs_final(halves_m))
    gffn2 = r["dffn2"]

    small_w = dict(b_ada=b_ada, g_ffn1=g_ffn1, g_mix=g_mix, g_ffn2=g_ffn2, g_final=g_final,
                   g_sb_out=g_sb_out, g_dil_out=g_dil_out, rel_bias=rel_bias)
    small_m = dict(b_ada=m_b_ada, g_ffn1=m_g_ffn1, g_mix=m_g_mix, g_ffn2=m_g_ffn2, g_final=m_g_final,
                   g_sb_out=m_g_sb_out, g_dil_out=m_g_dil_out, rel_bias=m_rel_bias)
    small_v = dict(b_ada=v_b_ada, g_ffn1=v_g_ffn1, g_mix=v_g_mix, g_ffn2=v_g_ffn2, g_final=v_g_final,
                   g_sb_out=v_g_sb_out, g_dil_out=v_g_dil_out, rel_bias=v_rel_bias)
    shapes = {k: v.shape for k, v in small_w.items()}
    sg, sd, sm, sv = _adamw(_pack_small(small_w), small_sum.reshape(1, SMALL_ROWS, 128), 0,
                            _pack_small(small_m), _pack_small(small_v))
    sg, loss = _unpack_small(sg, shapes)
    sd, _ = _unpack_small(sd, shapes)
    sm, _ = _unpack_small(sm, shapes)
    sv, _ = _unpack_small(sv, shapes)

    big = {}

    def upd(name, w, g_arr, sel, m, v, transposed=False):
        swap = (lambda a: jnp.swapaxes(a, -1, -2)) if transposed else (lambda a: a)
        w2, m2, v2 = [swap(a)[0] for a in (w, m, v)]
        big[name] = [swap(a[None]) for a in _adamw(w2, g_arr, sel, m2, v2)]

    upd("w_ada", w_ada, g_wada.reshape(1, d, ada_cols), 0, m_w_ada, v_w_ada)
    upd("w1_gate", w1_gate, gffn1, 0, m_w1_gate, v_w1_gate, transposed=True)
    upd("w1_up", w1_up, gffn1, 1, m_w1_up, v_w1_up, transposed=True)
    upd("w1_down", w1_down, gffn1, 2, m_w1_down, v_w1_down)
    upd("w_in", w_in, gwin, 0, m_w_in, v_w_in)
    upd("w_out", w_out, gwout, 0, m_w_out, v_w_out)
    upd("w2_gate", w2_gate, gffn2, 0, m_w2_gate, v_w2_gate, transposed=True)
    upd("w2_up", w2_up, gffn2, 1, m_w2_up, v_w2_up, transposed=True)
    upd("w2_down", w2_down, gffn2, 2, m_w2_down, v_w2_down)

    names = ["w_ada", "b_ada", "g_ffn1", "w1_gate", "w1_up", "w1_down", "g_mix", "w_in", "g_sb_out", "g_dil_out",
             "w_out", "rel_bias", "g_ffn2", "w2_gate", "w2_up", "w2_down", "g_final"]
    outs = [loss, r["grad_x"]]
    for k, small in enumerate((sg, sd, sm, sv)):
        for name in names:
            outs.append(big[name][k] if name in big else small[name])
    return tuple(outs)
```

```python
import math

import numpy as np
import jax
import jax.numpy as jnp
from jax import lax
from jax.experimental import pallas as pl
from jax.experimental.pallas import tpu as pltpu

F32 = jnp.float32
BF = jnp.bfloat16
MESH = pl.DeviceIdType.MESH

HEAD_DIM = 64
N_HEADS = 8
D_GRP = N_HEADS * HEAD_DIM
DIL_CONFIGS = ((128, 1), (512, 4), (2048, 16))
N_STEPS = 128
BLOCK = 128
N_BUCKETS = 32
MAX_DISTANCE = 2048
N_MOD = 9
EPS = 1e-6
NEG_INF = -1e30
SCALE = HEAD_DIM ** -0.5

ADAM_LR = 0.001
ADAM_B1 = 0.9
ADAM_B2 = 0.999
ADAM_EPS = 1e-08
ADAM_WD = 0.01
ADAM_STEP = 10

N_CHIPS = 4
N_DEV = 8
VMEM_LIMIT = 56 * 1024 * 1024
TM = 512
TQ = 256
KB = 256
SMALL_ROWS = 120


def _cp(n_axes=0, **kw):
    sem = ("arbitrary",) * n_axes if n_axes else None
    return pltpu.CompilerParams(dimension_semantics=sem, vmem_limit_bytes=VMEM_LIMIT, **kw)


def _nn(a, b):
    return jnp.dot(a, b, preferred_element_type=F32)


def _nt(a, b):
    return lax.dot_general(a, b, (((1,), (1,)), ((), ())), preferred_element_type=F32)


def _tn(a, b):
    return lax.dot_general(a, b, (((0,), (0,)), ((), ())), preferred_element_type=F32)


def _nn2(x, m):
    hi = x.astype(BF)
    lo = (x - hi.astype(F32)).astype(BF)
    r = _nn(jnp.concatenate([hi, lo], axis=0), m)
    return r[:x.shape[0]] + r[x.shape[0]:]


def _softplus(z):
    return jnp.maximum(z, 0.0) + jnp.log1p(jnp.exp(-jnp.abs(z)))


def _sds(shape, dtype):
    return jax.ShapeDtypeStruct(shape, dtype)


def _whole(a):
    nd = a.ndim
    return pl.BlockSpec(a.shape, lambda *_: (0,) * nd, pipeline_mode=pl.Buffered(1))


def _modnorm_bwd_tile(dh, xv, gv, scv, dxo):
    r = lax.rsqrt(jnp.mean(xv * xv, axis=-1, keepdims=True) + EPS)
    n = xv * r
    ng = n * gv
    dsh = jnp.sum(dh, axis=0, keepdims=True)
    dsc = jnp.sum(dh * ng, axis=0, keepdims=True)
    dy = dh * (1.0 + scv)
    dg = jnp.sum(dy * n, axis=0, keepdims=True)
    dn = dy * gv
    dx = dxo + r * (dn - n * jnp.mean(dn * n, axis=-1, keepdims=True))
    return dx, dsh, dsc, dg


def _acc_rows(ref, val, first):
    @pl.when(first)
    def _():
        ref[...] = val

    @pl.when(jnp.logical_not(first))
    def _():
        ref[...] += val


def _modnorm_tile(x_ref, g_ref, sc_ref, sh_ref):
    xv = x_ref[...]
    r = lax.rsqrt(jnp.mean(xv * xv, axis=-1, keepdims=True) + EPS)
    return (((xv * r) * g_ref[...]) * (1.0 + sc_ref[...]) + sh_ref[...]).astype(BF)


def _ffn_up(x, g, sc, sh, wgu, seq, carry=None):
    t, d = x.shape
    fs = wgu.shape[-1]
    per = seq // TM

    def body(x_ref, g_ref, sc_ref, sh_ref, w_ref, h_ref, p_ref, q_ref, s_ref):
        hv = _modnorm_tile(x_ref, g_ref, sc_ref, sh_ref)
        h_ref[...] = hv
        for j in range(N_CHIPS):
            a = _nn(hv, w_ref[j, 0])
            u = _nn(hv, w_ref[j, 1])
            sig = jax.nn.sigmoid(a)
            q = a * sig
            p_ref[j] = (u * (sig * (1.0 + a * (1.0 - sig)))).astype(BF)
            q_ref[j] = q.astype(BF)
            s_ref[j] = (q * u).astype(BF)

    row = pl.BlockSpec((TM, d), lambda m: (m, 0))
    ex = pl.BlockSpec((None, 1, d), lambda m: (m // per, 0, 0))
    blk = pl.BlockSpec((N_CHIPS, TM, fs), lambda m: (0, m, 0))
    return _call(
        body, "ffn_up", (t // TM,),
        [row, pl.BlockSpec((1, d), lambda m: (0, 0)), ex, ex, _whole(wgu)],
        [row, blk, blk, blk],
        [_sds((t, d), BF)] + [_sds((N_CHIPS, t, fs), BF)] * 3,
        (x, g, sc, sh, wgu), carry=carry)


def _ffn_down(s, wd, x, gt, seq, coef, carry=None):
    _, t, fs = s.shape
    d = x.shape[-1]
    per = seq // TM

    def body(s_ref, w_ref, x_ref, gt_ref, f_ref, xo_ref):
        f = _nn(s_ref[0], w_ref[0, 0])
        for j in range(1, N_CHIPS):
            f = f + _nn(s_ref[j], w_ref[j, 0])
        f_ref[...] = f.astype(BF)
        xo_ref[...] = x_ref[...] + (coef * gt_ref[...]) * f

    row = pl.BlockSpec((TM, d), lambda m: (m, 0))
    return _call(
        body, "ffn_down", (t // TM,),
        [pl.BlockSpec((N_CHIPS, TM, fs), lambda m: (0, m, 0)), _whole(wd), row,
         pl.BlockSpec((None, 1, d), lambda m: (m // per, 0, 0))],
        [row, row], [_sds((t, d), BF), _sds((t, d), F32)], (s, wd, x, gt), carry=carry)


def _ffn_bwd_ds(dxo, gt, f, wd, p, q, seq, coef, carry=None):
    t, d = dxo.shape
    fs = p.shape[-1]
    per = seq // TM
    nb = t // seq

    def body(dxo_ref, gt_ref, f_ref, w_ref, p_ref, q_ref, da_ref, du_ref, df_ref, dgt_ref):
        m = pl.program_id(0)
        dxv = dxo_ref[...]
        df = ((coef * gt_ref[...]) * dxv).astype(BF)
        df_ref[...] = df
        _acc_rows(dgt_ref, coef * jnp.sum(dxv * f_ref[...].astype(F32), axis=0, keepdims=True), m % per == 0)
        for j in range(N_CHIPS):
            ds = _nt(df, w_ref[j, 0])
            da_ref[j] = (ds * p_ref[j].astype(F32)).astype(BF)
            du_ref[j] = (ds * q_ref[j].astype(F32)).astype(BF)

    row = pl.BlockSpec((TM, d), lambda m: (m, 0))
    blk = pl.BlockSpec((N_CHIPS, TM, fs), lambda m: (0, m, 0))
    ex = pl.BlockSpec((None, 1, d), lambda m: (m // per, 0, 0))
    return _call(
        body, "ffn_bwd_ds", (t // TM,),
        [row, ex, row, _whole(wd), blk, blk],
        [blk, blk, row, ex],
        [_sds((N_CHIPS, t, fs), BF), _sds((N_CHIPS, t, fs), BF), _sds((t, d), BF), _sds((nb, 1, d), F32)],
        (dxo, gt, f, wd, p, q), carry=carry)


TM_X = 256


def _ffn_bwd_x(dxo, gt, f, wd, p, q, wgu, x, g, sc, seq, coef):
    t, d = dxo.shape
    fs = p.shape[-1]
    per = seq // TM_X
    nb = t // seq

    def body(dxo_ref, gt_ref, f_ref, wd_ref, p_ref, q_ref, w_ref, x_ref, g_ref, sc_ref,
             da_ref, du_ref, df_ref, dgt_ref, dx_ref, dsh_ref, dsc_ref, dg_ref):
        m = pl.program_id(0)
        dxv = dxo_ref[...]
        df = ((coef * gt_ref[...]) * dxv).astype(BF)
        df_ref[...] = df
        _acc_rows(dgt_ref, coef * jnp.sum(dxv * f_ref[...].astype(F32), axis=0, keepdims=True), m % per == 0)
        dh = None
        for j in range(N_CHIPS):
            ds = _nt(df, wd_ref[j, 0])
            da = (ds * p_ref[j].astype(F32)).astype(BF)
            du = (ds * q_ref[j].astype(F32)).astype(BF)
            da_ref[j] = da
            du_ref[j] = du
            part = _nt(da, w_ref[j, 0]) + _nt(du, w_ref[j, 1])
            dh = part if dh is None else dh + part
        dx, dsh, dsc, dg = _modnorm_bwd_tile(dh, x_ref[...], g_ref[...], sc_ref[...], dxv)
        dx_ref[...] = dx
        _acc_rows(dsh_ref, dsh, m % per == 0)
        _acc_rows(dsc_ref, dsc, m % per == 0)
        _acc_rows(dg_ref, dg, m == 0)

    row = pl.BlockSpec((TM_X, d), lambda m: (m, 0))
    blk = pl.BlockSpec((N_CHIPS, TM_X, fs), lambda m: (0, m, 0))
    ex = pl.BlockSpec((None, 1, d), lambda m: (m // per, 0, 0))
    vec = pl.BlockSpec((1, d), lambda m: (0, 0))
    exs = _sds((nb, 1, d), F32)
    return pl.pallas_call(
        body, name="ffn_bwd_x", grid=(t // TM_X,),
        in_specs=[row, ex, row, _whole(wd), blk, blk, _whole(wgu), row, vec, ex],
        out_specs=[blk, blk, row, ex, row, ex, ex, vec],
        out_shape=[_sds((N_CHIPS, t, fs), BF), _sds((N_CHIPS, t, fs), BF), _sds((t, d), BF), exs,
                   _sds((t, d), F32), exs, exs, _sds((1, d), F32)],
        compiler_params=_cp(1))(dxo, gt, f, wd, p, q, wgu, x, g, sc)


TK_W = 1024


def _ffn_bwd_w(h, da, du, s, df):
    t, d = h.shape
    fs = da.shape[-1]

    def body(h_ref, da_ref, du_ref, s_ref, df_ref, o_ref):
        kt = pl.program_id(1)
        hv = h_ref[...]
        parts = (_tn(da_ref[...], hv), _tn(du_ref[...], hv), _tn(s_ref[...], df_ref[...]))

        @pl.when(kt == 0)
        def _():
            for i, p in enumerate(parts):
                o_ref[i] = p

        @pl.when(kt != 0)
        def _():
            for i, p in enumerate(parts):
                o_ref[i] += p

    row = pl.BlockSpec((TK_W, d), lambda j, kt: (kt, 0))
    blk = pl.BlockSpec((None, TK_W, fs), lambda j, kt: (j, kt, 0))
    return pl.pallas_call(
        body, name="ffn_bwd_w", grid=(N_CHIPS, t // TK_W),
        in_specs=[row, blk, blk, blk, row],
        out_specs=pl.BlockSpec((None, 3, fs, d), lambda j, kt: (j, 0, 0, 0)),
        out_shape=_sds((N_CHIPS, 3, fs, d), F32),
        compiler_params=_cp(2))(h, da, du, s, df)


def _ffn_bwd_dh(da, du, wgu, x, g, sc, dxo, seq, carry=None):
    _, t, fs = da.shape
    d = x.shape[-1]
    per = seq // TM
    nb = t // seq

    def body(da_ref, du_ref, w_ref, x_ref, g_ref, sc_ref, dxo_ref, dx_ref, dsh_ref, dsc_ref, dg_ref):
        m = pl.program_id(0)
        dh = _nt(da_ref[0], w_ref[0, 0]) + _nt(du_ref[0], w_ref[0, 1])
        for j in range(1, N_CHIPS):
            dh = dh + _nt(da_ref[j], w_ref[j, 0]) + _nt(du_ref[j], w_ref[j, 1])
        dx, dsh, dsc, dg = _modnorm_bwd_tile(dh, x_ref[...], g_ref[...], sc_ref[...], dxo_ref[...])
        dx_ref[...] = dx
        _acc_rows(dsh_ref, dsh, m % per == 0)
        _acc_rows(dsc_ref, dsc, m % per == 0)
        _acc_rows(dg_ref, dg, m == 0)

    row = pl.BlockSpec((TM, d), lambda m: (m, 0))
    blk = pl.BlockSpec((N_CHIPS, TM, fs), lambda m: (0, m, 0))
    ex = pl.BlockSpec((None, 1, d), lambda m: (m // per, 0, 0))
    vec = pl.BlockSpec((1, d), lambda m: (0, 0))
    return _call(
        body, "ffn_bwd_dh", (t // TM,),
        [blk, blk, _whole(wgu), row, vec, ex, row],
        [row, ex, ex, vec],
        [_sds((t, d), F32), _sds((nb, 1, d), F32), _sds((nb, 1, d), F32), _sds((1, d), F32)],
        (da, du, wgu, x, g, sc, dxo), carry=carry)


def _qkv_proj(x, g, sc, sh, w_in, seq, carry=None):
    t, d = x.shape
    wc = w_in.shape[-1]
    per = seq // TM

    dils = [dil for _, dil in DIL_CONFIGS if dil > 1]

    def body(x_ref, g_ref, sc_ref, sh_ref, w_ref, h_ref, o_ref, *rest):
        res_refs, buf = rest[:len(dils)], rest[len(dils)]
        hv = _modnorm_tile(x_ref, g_ref, sc_ref, sh_ref)
        h_ref[...] = hv
        for j in range(N_CHIPS):
            rf = _nn(hv, w_ref[j, 0])
            r = rf.astype(BF)
            for a, lc, off, width in _col_pieces(j, wc):
                o_ref[a, :, lc:lc + width] = r[:, off:off + width]
                if a < 3:
                    continue
                for c0 in range(0, width, 128):
                    cg = (lc + c0) // 128
                    buf[...] = rf[:, off + c0:off + c0 + 128]
                    for ref, dil in zip(res_refs, dils):
                        for rr in range(dil):
                            ref[a - 3, :, rr * D_GRP + cg * 128:rr * D_GRP + (cg + 1) * 128] = (
                                buf[pl.ds(rr, TM // dil, stride=dil), :].astype(BF))

    row = pl.BlockSpec((TM, d), lambda m: (m, 0))
    ex = pl.BlockSpec((None, 1, d), lambda m: (m // per, 0, 0))
    return _call(
        body, "qkv_proj", (t // TM,),
        [row, pl.BlockSpec((1, d), lambda m: (0, 0)), ex, ex, _whole(w_in)],
        [row, pl.BlockSpec((6, TM, D_GRP), lambda m: (0, m, 0))]
        + [pl.BlockSpec((3, TM // dil, dil * D_GRP), lambda m: (0, m, 0)) for dil in dils],
        [_sds((t, d), BF), _sds((6, t, D_GRP), BF)] + [_sds((3, t // dil, dil * D_GRP), BF) for dil in dils],
        (x, g, sc, sh, w_in), scratch=[pltpu.VMEM((TM, 128), F32)], carry=carry)


def _col_pieces(j, wc):
    out, off = [], 0
    while off < wc:
        a, lc = divmod(j * wc + off, D_GRP)
        width = min(D_GRP - lc, wc - off)
        out.append((a, lc, off, width))
        off += width
    return out


def _chip_cols(g6_ref, j, wc):
    return jnp.concatenate([g6_ref[a, :, lc:lc + width] for a, lc, _, width in _col_pieces(j, wc)], axis=1)


def _mix_out(on_sb, on_dil, w_out, x, gt, seq):
    t, d = x.shape
    per = seq // TM

    def body(a_ref, b_ref, w_ref, x_ref, gt_ref, t_ref, xo_ref):
        tv = _nn(a_ref[...], w_ref[0:D_GRP, :]) + _nn(b_ref[...], w_ref[D_GRP:2 * D_GRP, :])
        t_ref[...] = tv.astype(BF)
        xo_ref[...] = x_ref[...] + gt_ref[...] * tv

    row = pl.BlockSpec((TM, d), lambda m: (m, 0))
    half = pl.BlockSpec((TM, D_GRP), lambda m: (m, 0))
    return pl.pallas_call(
        body, name="mix_out", grid=(t // TM,),
        in_specs=[half, half, pl.BlockSpec((2 * D_GRP, d), lambda m: (0, 0)), row,
                  pl.BlockSpec((None, 1, d), lambda m: (m // per, 0, 0))],
        out_specs=[row, row],
        out_shape=[_sds((t, d), BF), _sds((t, d), F32)],
        compiler_params=_cp(1))(on_sb, on_dil, w_out, x, gt)


def _sb_masks():
    lane = lax.broadcasted_iota(jnp.int32, (1, 2 * HEAD_DIM), 1)
    hm0 = lane < HEAD_DIM
    rel = lax.broadcasted_iota(jnp.int32, (TQ, KB), 0) - lax.broadcasted_iota(jnp.int32, (TQ, KB), 1)
    kr = lax.broadcasted_iota(jnp.int32, (KB, KB), 0)
    kc = lax.broadcasted_iota(jnp.int32, (KB, KB), 1)
    return hm0, rel, kr, kc


def _headnorm_pair(o, gv, hm0):
    o2 = o * o
    ms0 = jnp.sum(jnp.where(hm0, o2, 0.0), axis=-1, keepdims=True) * (1.0 / HEAD_DIM)
    ms1 = jnp.sum(jnp.where(hm0, 0.0, o2), axis=-1, keepdims=True) * (1.0 / HEAD_DIM)
    r = jnp.where(hm0, lax.rsqrt(ms0 + EPS), lax.rsqrt(ms1 + EPS))
    return (o * r) * gv


SB_DEAD = -104.0


def _alive(c_l):
    return (jnp.max(c_l) > SB_DEAD).astype(jnp.int32)


def _sb_fwd(qkv6, g_sb, nb, seq, carry=None):
    nq = seq // TQ

    def body(q_ref, k_ref, v_ref, g_ref, o_ref, on_ref):
        qi = pl.program_id(2)
        hm0, rel, kr, kc = _sb_masks()
        upper = (kr > kc).astype(BF)
        heads = _dil_masks()[0]
        qs = _stack_heads(q_ref[...], heads)
        causal2 = jnp.concatenate([rel] * GRP_HEADS, axis=0) > 0

        def block(kj, causal, c_l, acc):
            ks = pl.multiple_of(kj * KB, KB)
            z = _nt(qs, k_ref[pl.ds(ks, KB), :]) * SCALE
            sp = _softplus(z)
            ln = -sp if causal is None else jnp.where(causal, -sp, 0.0)
            suf = _nn2(ln, upper)
            w = jnp.exp((z - sp) + (suf + c_l))
            if causal is not None:
                w = jnp.where(causal, w, 0.0)
            return c_l + (suf[:, 0:1] + ln[:, 0:1]), acc + _nn(w.astype(BF), v_ref[pl.ds(ks, KB), :])

        c_l, acc = block(qi, causal2, jnp.zeros((GRP_HEADS * TQ, 1), F32), jnp.zeros((GRP_HEADS * TQ, GRP_W), F32))

        def cond(carry):
            return jnp.logical_and(carry[0] <= qi, carry[1] > 0)

        def kbody(carry):
            it, _, c_l, acc = carry
            c_l, acc = block(qi - it, None, c_l, acc)
            return it + 1, _alive(c_l), c_l, acc

        acc = lax.while_loop(cond, kbody, (jnp.int32(1), _alive(c_l), c_l, acc))[3]
        o = _unstack_heads(acc, heads, TQ)
        o_ref[...] = o.astype(BF)
        gv = g_ref[...]
        for half in range(GRP_W // 128):
            lanes = slice(half * 128, (half + 1) * 128)
            on_ref[:, lanes] = _headnorm_pair(o[:, lanes], gv[:, lanes], hm0).astype(BF)

    w = GRP_W
    full = lambda i: pl.BlockSpec((None, None, seq, w), lambda b, hp, q: (i, b, 0, hp))
    qblk = pl.BlockSpec((None, None, TQ, w), lambda b, hp, q: (0, b, q, hp))
    oblk = pl.BlockSpec((None, TQ, w), lambda b, hp, q: (b, q, hp))
    return _call(
        body, "sb_fwd", (nb, N_HEADS // GRP_HEADS, nq),
        [qblk, full(1), full(2), pl.BlockSpec((1, w), lambda b, hp, q: (0, hp))],
        [oblk, oblk],
        [_sds((nb, seq, D_GRP), BF), _sds((nb, seq, D_GRP), BF)],
        (qkv6, qkv6, qkv6, g_sb), carry=carry)


def _sb_bwd(qkv6, do, nb, seq, carry=None):
    nq = seq // TQ
    nk = seq // KB

    def body(q_ref, k_ref, v_ref, do_ref, out_ref, dk_acc, dv_acc, g_st, s_st):
        qi = pl.program_id(2)
        hm0, rel, kr, kc = _sb_masks()
        upper = (kr > kc).astype(BF)
        lower = (kr < kc).astype(BF)

        @pl.when(qi == 0)
        def _():
            dk_acc[...] = jnp.zeros_like(dk_acc)
            dv_acc[...] = jnp.zeros_like(dv_acc)

        heads = _dil_masks()[0]
        qs = _stack_heads(q_ref[...], heads)
        dos = _stack_heads(do_ref[...], heads)
        causal2 = jnp.concatenate([rel] * GRP_HEADS, axis=0) > 0

        def weights(kj, causal, c_l):
            ks = pl.multiple_of(kj * KB, KB)
            vb = v_ref[pl.ds(ks, KB), :]
            z = _nt(qs, k_ref[pl.ds(ks, KB), :]) * SCALE
            sp = _softplus(z)
            ln = -sp if causal is None else jnp.where(causal, -sp, 0.0)
            suf = _nn2(ln, upper)
            lsz = z - sp
            w = jnp.exp(lsz + (suf + c_l))
            if causal is not None:
                w = jnp.where(causal, w, 0.0)
            g_st[kj] = w * _nt(dos, vb)
            s_st[kj] = jnp.exp(lsz)
            dv_acc[pl.ds(ks, KB), :] += _tn(w.astype(BF), dos)
            return c_l + (suf[:, 0:1] + ln[:, 0:1])

        zc = jnp.zeros((GRP_HEADS * TQ, 1), F32)
        c_l = weights(qi, causal2, zc)

        def acond(carry):
            return jnp.logical_and(carry[0] <= qi, carry[1] > 0)

        def abody(carry):
            c_l = weights(qi - carry[0], None, carry[2])
            return carry[0] + 1, _alive(c_l), c_l

        n_used = lax.while_loop(acond, abody, (jnp.int32(1), _alive(c_l), c_l))[0]

        def grads(kj, causal, c_g, dq):
            ks = pl.multiple_of(kj * KB, KB)
            kb = k_ref[pl.ds(ks, KB), :]
            g = g_st[kj]
            sig = s_st[kj]
            pre = _nn(g.astype(BF), lower)
            dz = g * (1.0 - sig) - sig * (pre + c_g)
            if causal is not None:
                dz = jnp.where(causal, dz, 0.0)
            dzb = (dz * SCALE).astype(BF)
            dk_acc[pl.ds(ks, KB), :] += _tn(dzb, qs)
            return c_g + (pre[:, KB - 1:KB] + g[:, KB - 1:KB]), dq + _nn(dzb, kb)

        c_g, dq = lax.fori_loop(qi - n_used + 1, qi, lambda kj, cr: grads(kj, None, *cr),
                                (zc, jnp.zeros((GRP_HEADS * TQ, GRP_W), F32)))
        _, dq = grads(qi, causal2, c_g, dq)
        dq = _unstack_heads(dq, heads, TQ)
        out_ref[0, pl.ds(pl.multiple_of(qi * TQ, TQ), TQ), :] = dq.astype(BF)

        @pl.when(qi == nq - 1)
        def _():
            out_ref[1] = dk_acc[...].astype(BF)
            out_ref[2] = dv_acc[...].astype(BF)

    w = GRP_W
    full = lambda i: pl.BlockSpec((None, None, seq, w), lambda b, hp, q: (i, b, 0, hp))
    qblk = pl.BlockSpec((None, None, TQ, w), lambda b, hp, q: (0, b, q, hp))
    oblk = pl.BlockSpec((None, TQ, w), lambda b, hp, q: (b, q, hp))
    return _call(
        body, "sb_bwd", (nb, N_HEADS // GRP_HEADS, nq),
        [qblk, full(1), full(2), oblk],
        [pl.BlockSpec((3, None, seq, w), lambda b, hp, q: (0, b, 0, hp))],
        [_sds((6, nb, seq, D_GRP), BF)], (qkv6, qkv6, qkv6, do),
        scratch=[pltpu.VMEM((seq, w), F32), pltpu.VMEM((seq, w), F32),
                 pltpu.VMEM((nk, GRP_HEADS * TQ, KB), F32), pltpu.VMEM((nk, GRP_HEADS * TQ, KB), F32)],
        carry=carry)


def _t5_bucket(n):
    max_exact = N_BUCKETS // 2
    nf = np.maximum(n, 1).astype(np.float32)
    large = max_exact + (np.log(nf / max_exact) / math.log(MAX_DISTANCE / max_exact)
                         * (N_BUCKETS - max_exact)).astype(np.int32)
    large = np.minimum(large, N_BUCKETS - 1)
    return np.where(n < max_exact, n, large).astype(np.int32)


def _bucket_map(dilation):
    step = BLOCK + np.arange(BLOCK)[:, None] - np.arange(2 * BLOCK)[None, :]
    return _t5_bucket(np.clip(step, 0, N_STEPS) * dilation)


GRP_HEADS = 4
GRP_W = GRP_HEADS * HEAD_DIM


def _dil_masks():
    lane = lax.broadcasted_iota(jnp.int32, (1, GRP_W), 1)
    heads = [jnp.logical_and(lane >= HEAD_DIM * i, lane < HEAD_DIM * (i + 1)) for i in range(GRP_HEADS)]
    iq = jnp.bitwise_and(lax.broadcasted_iota(jnp.int32, (GRP_HEADS * BLOCK, BLOCK), 0), BLOCK - 1)
    ik = lax.broadcasted_iota(jnp.int32, (GRP_HEADS * BLOCK, BLOCK), 1)
    return heads, ik <= iq, ik >= iq


def _stack_heads(x, heads):
    zero = jnp.zeros_like(x)
    return jnp.concatenate([jnp.where(hm, x, zero) for hm in heads], axis=0)


def _unstack_heads(xs, heads, rows=BLOCK):
    out = xs[0:rows]
    for i in range(1, GRP_HEADS):
        out = jnp.where(heads[i], xs[i * rows:(i + 1) * rows], out)
    return out


def _dil_rows(n):
    rs = pl.multiple_of(n * BLOCK, BLOCK)
    ps = pl.multiple_of(jnp.maximum(n - 1, 0) * BLOCK, BLOCK)
    return pl.ds(rs, BLOCK), pl.ds(ps, BLOCK)


def _dil_probs(qs, kc, kp, b_ref, gi, valid_c, valid_p):
    rows = slice(gi * GRP_HEADS * BLOCK, (gi + 1) * GRP_HEADS * BLOCK)
    zc = _nt(qs, kc) * SCALE + b_ref[rows, BLOCK:2 * BLOCK]
    zp = _nt(qs, kp) * SCALE + b_ref[rows, 0:BLOCK]
    zc = jnp.where(valid_c, zc, NEG_INF)
    zp = jnp.where(valid_p, zp, NEG_INF)
    m = jnp.maximum(jnp.max(zc, axis=-1, keepdims=True), jnp.max(zp, axis=-1, keepdims=True))
    ec = jnp.exp(zc - m)
    ep = jnp.exp(zp - m)
    den = jnp.sum(ec, axis=-1, keepdims=True) + jnp.sum(ep, axis=-1, keepdims=True)
    return ec, ep, den, m


def _dil_fwd(qkv6r, base, bias, nb, sub_len, dilation):
    n_blk = sub_len // BLOCK

    def body(q_ref, k_ref, v_ref, b_ref, o_ref, l_ref):
        heads, valid_c, valid_p0 = _dil_masks()

        def nbody(n, carry):
            cur, prev = _dil_rows(n)
            valid_p = jnp.logical_and(valid_p0, n > 0)
            for gi in range(N_HEADS // GRP_HEADS):
                lanes = slice(gi * GRP_W, (gi + 1) * GRP_W)
                qs = _stack_heads(q_ref[cur, lanes], heads)
                ec, ep, den, m = _dil_probs(qs, k_ref[cur, lanes], k_ref[prev, lanes], b_ref, gi, valid_c, valid_p)
                o = (_nn(ec.astype(BF), v_ref[cur, lanes]) + _nn(ep.astype(BF), v_ref[prev, lanes])) / den
                o_ref[cur, lanes] = _unstack_heads(o, heads).astype(BF)
                l_ref[cur, lanes] = _unstack_heads(jnp.broadcast_to(m + jnp.log(den), o.shape), heads)
            return carry

        lax.fori_loop(0, n_blk, nbody, 0)

    seqblk = lambda i: pl.BlockSpec((None, None, sub_len, D_GRP), lambda b, r: (i, b, 0, r))
    oblk = pl.BlockSpec((None, sub_len, D_GRP), lambda b, r: (b, 0, r))
    shp = _sds((nb, sub_len, dilation * D_GRP), F32)
    return pl.pallas_call(
        body, name="dil_fwd_%d" % dilation, grid=(nb, dilation),
        in_specs=[seqblk(base), seqblk(base + 1), seqblk(base + 2), _whole(bias)],
        out_specs=[oblk, oblk], out_shape=[_sds(shp.shape, BF), shp],
        compiler_params=_cp(2))(qkv6r, qkv6r, qkv6r, bias)


def _dil_bwd(qkv6r, base, bias, do_c, dd_c, nb, sub_len, dilation, carry=None):
    n_blk = sub_len // BLOCK

    def body(q_ref, k_ref, v_ref, b_ref, do_ref, dd_ref, out_ref, a_ref, dk_acc, dv_acc):
        heads, valid_c, valid_p0 = _dil_masks()
        first = jnp.logical_and(pl.program_id(0) == 0, pl.program_id(1) == 0)

        @pl.when(first)
        def _():
            a_ref[...] = jnp.zeros_like(a_ref)

        dk_acc[...] = jnp.zeros_like(dk_acc)
        dv_acc[...] = jnp.zeros_like(dv_acc)

        def nbody(n, carry):
            cur, prev = _dil_rows(n)
            valid_p = jnp.logical_and(valid_p0, n > 0)
            for gi in range(N_HEADS // GRP_HEADS):
                lanes = slice(gi * GRP_W, (gi + 1) * GRP_W)
                kc, kp = k_ref[cur, lanes], k_ref[prev, lanes]
                vc, vp = v_ref[cur, lanes], v_ref[prev, lanes]
                qs = _stack_heads(q_ref[cur, lanes], heads)
                dos = _stack_heads(do_ref[cur, lanes], heads).astype(BF)
                dds = jnp.sum(_stack_heads(dd_ref[cur, lanes], heads), axis=-1, keepdims=True) * (1.0 / HEAD_DIM)
                ec, ep, den, _ = _dil_probs(qs, kc, kp, b_ref, gi, valid_c, valid_p)
                inv = 1.0 / den
                pc = ec * inv
                pp = ep * inv
                dzc = pc * (_nt(dos, vc) + dds)
                dzp = pp * (_nt(dos, vp) + dds)
                rows = slice(gi * GRP_HEADS * BLOCK, (gi + 1) * GRP_HEADS * BLOCK)
                a_ref[rows, BLOCK:2 * BLOCK] += dzc
                a_ref[rows, 0:BLOCK] += dzp
                dzcb = (dzc * SCALE).astype(BF)
                dzpb = (dzp * SCALE).astype(BF)
                out_ref[0, cur, lanes] = _unstack_heads(_nn(dzcb, kc) + _nn(dzpb, kp), heads).astype(BF)
                dk_acc[cur, lanes] += _tn(dzcb, qs)
                dk_acc[prev, lanes] += _tn(dzpb, qs)
                dv_acc[cur, lanes] += _tn(pc.astype(BF), dos)
                dv_acc[prev, lanes] += _tn(pp.astype(BF), dos)
            return carry

        lax.fori_loop(0, n_blk, nbody, 0)
        out_ref[1] = dk_acc[...].astype(BF)
        out_ref[2] = dv_acc[...].astype(BF)

    seqblk = lambda i: pl.BlockSpec((None, None, sub_len, D_GRP), lambda b, r: (i, b, 0, r))
    oblk = pl.BlockSpec((None, sub_len, D_GRP), lambda b, r: (b, 0, r))
    return _call(
        body, "dil_bwd_%d" % dilation, (nb, dilation),
        [seqblk(base), seqblk(base + 1), seqblk(base + 2), _whole(bias), oblk, oblk],
        [pl.BlockSpec((3, None, sub_len, D_GRP), lambda b, r: (0, b, 0, r)),
         pl.BlockSpec((N_HEADS * BLOCK, 2 * BLOCK), lambda b, r: (0, 0))],
        [_sds((3, nb, sub_len, dilation * D_GRP), BF), _sds((N_HEADS * BLOCK, 2 * BLOCK), F32)],
        (qkv6r, qkv6r, qkv6r, bias, do_c, dd_c),
        scratch=[pltpu.VMEM((sub_len, D_GRP), F32)] * 2, carry=carry)


def _group_ones():
    idx = np.arange(D_GRP) // HEAD_DIM
    return jnp.asarray((idx[:, None] == idx[None, :]).astype(np.float32), dtype=BF)


def _dil_alphas(l1, l4, l16):
    mx = jnp.maximum(jnp.maximum(l1, l4), l16)
    e1 = jnp.exp(l1 - mx)
    e4 = jnp.exp(l4 - mx)
    e16 = jnp.exp(l16 - mx)
    den = e1 + e4 + e16
    return e1 / den, e4 / den, e16 / den


def _residue_spec(dil):
    return pl.BlockSpec((TM // dil, dil * D_GRP), lambda m: (m, 0))


def _from_residue(src, dil, cg, buf):
    if dil == 1:
        return src[:, cg * 128:(cg + 1) * 128].astype(F32)
    for r in range(dil):
        buf[pl.ds(r, TM // dil, stride=dil), :] = (
            src[:, r * D_GRP + cg * 128:r * D_GRP + (cg + 1) * 128].astype(F32))
    return buf[...]


def _to_residue(dst, dil, cg, buf, val):
    if dil == 1:
        dst[:, cg * 128:(cg + 1) * 128] = val.astype(dst.dtype)
        return
    buf[...] = val
    for r in range(dil):
        dst[:, r * D_GRP + cg * 128:r * D_GRP + (cg + 1) * 128] = (
            buf[pl.ds(r, TM // dil, stride=dil), :].astype(dst.dtype))


def _pair_sum(x, hm0):
    s0 = jnp.sum(jnp.where(hm0, x, 0.0), axis=-1, keepdims=True)
    s1 = jnp.sum(jnp.where(hm0, 0.0, x), axis=-1, keepdims=True)
    return jnp.where(hm0, s0, s1)


def _dil_comb(os, ls, g_dil):
    t = os[0].shape[0]
    dils = [dil for _, dil in DIL_CONFIGS]

    def body(o1, l1, o4, l4, o16, l16, g_ref, o_ref, on_ref, b0, b1, b2, b3):
        hm0 = lax.broadcasted_iota(jnp.int32, (1, 128), 1) < HEAD_DIM
        for cg in range(D_GRP // 128):
            lanes = slice(cg * 128, (cg + 1) * 128)
            ov = [_from_residue(src, dil, cg, buf) for src, dil, buf in zip((o1, o4, o16), dils, (None, b0, b1))]
            lv = [_from_residue(src, dil, cg, buf) for src, dil, buf in zip((l1, l4, l16), dils, (None, b2, b3))]
            a1, a4, a16 = _dil_alphas(*lv)
            o = a1 * ov[0] + a4 * ov[1] + a16 * ov[2]
            o_ref[:, lanes] = o.astype(BF)
            on_ref[:, lanes] = _headnorm_pair(o, g_ref[:, lanes], hm0).astype(BF)

    blk = pl.BlockSpec((TM, D_GRP), lambda m: (m, 0))
    specs = [_residue_spec(dil) for dil in dils for _ in range(2)]
    return pl.pallas_call(
        body, name="dil_comb", grid=(t // TM,),
        in_specs=specs + [pl.BlockSpec((1, D_GRP), lambda m: (0, 0))],
        out_specs=[blk, blk],
        out_shape=[_sds((t, D_GRP), BF), _sds((t, D_GRP), BF)],
        scratch_shapes=[pltpu.VMEM((TM, 128), F32)] * 4,
        compiler_params=_cp(1))(os[0], ls[0], os[1], ls[1], os[2], ls[2], g_dil)


def _dil_comb_bwd(do, os, ls):
    t = do.shape[0]
    dils = [dil for _, dil in DIL_CONFIGS]

    def body(do_ref, o1, l1, o4, l4, o16, l16, d1, d4, d16, e1, e4, e16, b0, b1, b2, b3):
        hm0 = lax.broadcasted_iota(jnp.int32, (1, 128), 1) < HEAD_DIM
        for cg in range(D_GRP // 128):
            dov = do_ref[:, cg * 128:(cg + 1) * 128].astype(F32)
            ov = [_from_residue(src, dil, cg, buf) for src, dil, buf in zip((o1, o4, o16), dils, (None, b0, b1))]
            lv = [_from_residue(src, dil, cg, buf) for src, dil, buf in zip((l1, l4, l16), dils, (None, b2, b3))]
            al = _dil_alphas(*lv)
            sbar = al[0] * _pair_sum(dov * ov[0], hm0)
            for a_c, o_c in zip(al[1:], ov[1:]):
                sbar = sbar + a_c * _pair_sum(dov * o_c, hm0)
            for a_c, dil, dref, eref in zip(al, dils, (d1, d4, d16), (e1, e4, e16)):
                _to_residue(dref, dil, cg, b0, a_c * dov)
                _to_residue(eref, dil, cg, b1, -a_c * sbar)

    specs = [_residue_spec(dil) for dil in dils]
    return pl.pallas_call(
        body, name="dil_comb_bwd", grid=(t // TM,),
        in_specs=[pl.BlockSpec((TM, D_GRP), lambda m: (m, 0))] + [sp for sp in specs for _ in range(2)],
        out_specs=specs + specs,
        out_shape=[_sds((t // dil, dil * D_GRP), BF) for dil in dils]
        + [_sds((t // dil, dil * D_GRP), F32) for dil in dils],
        scratch_shapes=[pltpu.VMEM((TM, 128), F32)] * 4,
        compiler_params=_cp(1))(do, os[0], ls[0], os[1], ls[1], os[2], ls[2])


def _dqkv_dil_sum(ds, dqkv6):
    t = dqkv6.shape[1]
    dils = [dil for _, dil in DIL_CONFIGS]

    def body(*refs):
        srcs, o_ref, acc = refs[:len(dils)], refs[len(dils) + 1], refs[len(dils) + 2]
        for a in range(3):
            for cg in range(D_GRP // 128):
                for src, dil in zip(srcs, dils):
                    for r in range(dil):
                        part = src[a, :, r * D_GRP + cg * 128:r * D_GRP + (cg + 1) * 128].astype(F32)
                        rows = pl.ds(r, TM // dil, stride=dil) if dil > 1 else slice(None)
                        if dil == dils[0]:
                            acc[rows, :] = part
                        else:
                            acc[rows, :] += part
                o_ref[a, :, cg * 128:(cg + 1) * 128] = acc[...].astype(BF)

    return pl.pallas_call(
        body, name="dqkv_dil_sum", grid=(t // TM,),
        in_specs=[pl.BlockSpec((3, TM // dil, dil * D_GRP), lambda m: (0, m, 0)) for dil in dils]
        + [pl.BlockSpec(memory_space=pl.ANY)],
        out_specs=pl.BlockSpec((3, TM, D_GRP), lambda m: (1, m, 0)),
        out_shape=_sds((6, t, D_GRP), BF), input_output_aliases={len(dils): 0},
        scratch_shapes=[pltpu.VMEM((TM, 128), F32)],
        compiler_params=_cp(1))(*ds, dqkv6)


def _relbias_grad(a_all, onehot):
    def body(a_ref, oh_ref, o_ref):
        acc = jnp.zeros((N_HEADS, N_BUCKETS), F32)
        for c in range(len(DIL_CONFIGS)):
            av = a_ref[c]
            hi = av.astype(BF)
            lo = (av - hi.astype(F32)).astype(BF)
            acc = acc + _nt(hi, oh_ref[c]) + _nt(lo, oh_ref[c])
        o_ref[...] = acc

    return pl.pallas_call(body, name="relbias_grad", out_shape=_sds((N_HEADS, N_BUCKETS), F32),
                          compiler_params=_cp())(a_all, onehot)


def _headnorm_bwd(dn, o, gv, mv):
    ms = _nn2(o * o, mv) * (1.0 / HEAD_DIM)
    r = lax.rsqrt(ms + EPS)
    nrm = o * r
    dg = jnp.sum(dn * nrm, axis=0, keepdims=True)
    dnn = dn * gv
    do = r * (dnn - nrm * (_nn2(dnn * nrm, mv) * (1.0 / HEAD_DIM)))
    return do, dg


def _mix_bwd_out(dx, gt, tv, w_out, o_sb, o_dil, on_sb, on_dil, g_sb, g_dil, ones_g, seq, carry=None):
    t, d = dx.shape
    per = seq // TM
    nb = t // seq

    def body(dx_ref, gt_ref, t_ref, w_ref, osb, odl, onsb, ondl, gsb, gdl, m_ref,
             dosb, dodl, dgt_ref, dgsb, dgdl, dw_ref):
        m = pl.program_id(0)
        dxv = dx_ref[...]
        dt = (gt_ref[...] * dxv).astype(BF)
        _acc_rows(dgt_ref, jnp.sum(dxv * t_ref[...].astype(F32), axis=0, keepdims=True), m % per == 0)
        mv = m_ref[...]
        don_sb = _nt(dt, w_ref[0:D_GRP, :])
        don_dl = _nt(dt, w_ref[D_GRP:2 * D_GRP, :])
        do1, dg1 = _headnorm_bwd(don_sb, osb[...].astype(F32), gsb[...], mv)
        do2, dg2 = _headnorm_bwd(don_dl, odl[...].astype(F32), gdl[...], mv)
        dosb[...] = do1.astype(BF)
        dodl[...] = do2.astype(BF)
        _acc_rows(dgsb, dg1, m == 0)
        _acc_rows(dgdl, dg2, m == 0)
        p1 = _tn(onsb[...], dt)
        p2 = _tn(ondl[...], dt)

        @pl.when(m == 0)
        def _():
            dw_ref[0:D_GRP, :] = p1
            dw_ref[D_GRP:2 * D_GRP, :] = p2

        @pl.when(m != 0)
        def _():
            dw_ref[0:D_GRP, :] += p1
            dw_ref[D_GRP:2 * D_GRP, :] += p2

    row = pl.BlockSpec((TM, d), lambda m: (m, 0))
    half = pl.BlockSpec((TM, D_GRP), lambda m: (m, 0))
    ex = pl.BlockSpec((None, 1, d), lambda m: (m // per, 0, 0))
    gvec = pl.BlockSpec((1, D_GRP), lambda m: (0, 0))
    wblk = pl.BlockSpec((2 * D_GRP, d), lambda m: (0, 0))
    return _call(
        body, "mix_bwd_out", (t // TM,),
        [row, ex, row, wblk, half, half, half, half, gvec, gvec, pl.BlockSpec((D_GRP, D_GRP), lambda m: (0, 0))],
        [half, half, ex, gvec, gvec, wblk],
        [_sds((t, D_GRP), BF), _sds((t, D_GRP), BF), _sds((nb, 1, d), F32),
         _sds((1, D_GRP), F32), _sds((1, D_GRP), F32), _sds((2 * D_GRP, d), F32)],
        (dx, gt, tv, w_out, o_sb, o_dil, on_sb, on_dil, g_sb, g_dil, ones_g), carry=carry)


def _dw_in(h, dqkv6, carry=None):
    t, d = h.shape
    wc = 6 * D_GRP // N_CHIPS

    def body(h_ref, g_ref, o_ref):
        kt = pl.program_id(0)
        hv = h_ref[...]
        for j in range(N_CHIPS):
            p = _tn(hv, _chip_cols(g_ref, j, wc))

            @pl.when(kt == 0)
            def _(p=p, j=j):
                o_ref[j, 0] = p

            @pl.when(kt != 0)
            def _(p=p, j=j):
                o_ref[j, 0] += p

    return _call(
        body, "dw_in", (t // TK_W,),
        [pl.BlockSpec((TK_W, d), lambda kt: (kt, 0)), pl.BlockSpec((6, TK_W, D_GRP), lambda kt: (0, kt, 0))],
        [pl.BlockSpec((N_CHIPS, 1, d, wc), lambda kt: (0, 0, 0, 0))],
        [_sds((N_CHIPS, 1, d, wc), F32)], (h, dqkv6), carry=carry)


def _mix_bwd_dh(dqkv6, w_in, x, g, sc, dxo, seq, carry=None):
    _, t, _ = dqkv6.shape
    d = x.shape[-1]
    wc = w_in.shape[-1]
    per = seq // TM
    nb = t // seq

    def body(g6_ref, w_ref, x_ref, g_ref, sc_ref, dxo_ref, dx_ref, dsh_ref, dsc_ref, dg_ref):
        m = pl.program_id(0)
        dh = _nt(_chip_cols(g6_ref, 0, wc), w_ref[0, 0])
        for j in range(1, N_CHIPS):
            dh = dh + _nt(_chip_cols(g6_ref, j, wc), w_ref[j, 0])
        dx, dsh, dsc, dg = _modnorm_bwd_tile(dh, x_ref[...], g_ref[...], sc_ref[...], dxo_ref[...])
        dx_ref[...] = dx
        _acc_rows(dsh_ref, dsh, m % per == 0)
        _acc_rows(dsc_ref, dsc, m % per == 0)
        _acc_rows(dg_ref, dg, m == 0)

    row = pl.BlockSpec((TM, d), lambda m: (m, 0))
    ex = pl.BlockSpec((None, 1, d), lambda m: (m // per, 0, 0))
    vec = pl.BlockSpec((1, d), lambda m: (0, 0))
    return _call(
        body, "mix_bwd_dh", (t // TM,),
        [pl.BlockSpec((6, TM, D_GRP), lambda m: (0, m, 0)), _whole(w_in), row, vec, ex, row],
        [row, ex, ex, vec],
        [_sds((t, d), F32), _sds((nb, 1, d), F32), _sds((nb, 1, d), F32), _sds((1, d), F32)],
        (dqkv6, w_in, x, g, sc, dxo), carry=carry)


def _ffn_down_loss(s, wd, x, gt, seq, coef, g, target):
    _, t, fs = s.shape
    d = x.shape[-1]
    per = seq // TM
    steps = t // TM

    def body(s_ref, w_ref, x_ref, gt_ref, g_ref, t_ref, f_ref, dx_ref, dg_ref, loss_ref, lacc):
        m = pl.program_id(0)
        f = _nn(s_ref[0], w_ref[0, 0])
        for j in range(1, N_CHIPS):
            f = f + _nn(s_ref[j], w_ref[j, 0])
        f_ref[...] = f.astype(BF)
        xv = x_ref[...] + (coef * gt_ref[...]) * f
        gv = g_ref[...]
        r = lax.rsqrt(jnp.mean(xv * xv, axis=-1, keepdims=True) + EPS)
        n = xv * r
        err = n * gv - t_ref[...]
        dy = err * (1.0 / d)
        _acc_rows(dg_ref, jnp.sum(dy * n, axis=0, keepdims=True), m == 0)
        dn = dy * gv
        dx_ref[...] = r * (dn - n * jnp.mean(dn * n, axis=-1, keepdims=True))
        _acc_rows(lacc, jnp.sum(err * err, axis=0, keepdims=True), m == 0)

        @pl.when(m == steps - 1)
        def _():
            tot = jnp.sum(lacc[...], axis=-1, keepdims=True) * (0.5 / d)
            loss_ref[...] = jnp.broadcast_to(tot, (1, 128))

    row = pl.BlockSpec((TM, d), lambda m: (m, 0))
    vec = pl.BlockSpec((1, d), lambda m: (0, 0))
    return pl.pallas_call(
        body, name="ffn_down_loss", grid=(steps,),
        in_specs=[pl.BlockSpec((N_CHIPS, TM, fs), lambda m: (0, m, 0)), _whole(wd), row,
                  pl.BlockSpec((None, 1, d), lambda m: (m // per, 0, 0)), vec, row],
        out_specs=[row, row, vec, pl.BlockSpec((1, 128), lambda m: (0, 0))],
        out_shape=[_sds((t, d), BF), _sds((t, d), F32), _sds((1, d), F32), _sds((1, 128), F32)],
        scratch_shapes=[pltpu.VMEM((1, d), F32)],
        compiler_params=_cp(1))(s, wd, x, gt, g, target)


def _row_tile(rows, cols):
    best = rows
    for tr in range(8, rows + 1, 8):
        if rows % tr == 0 and tr * cols * 4 <= (1 << 20):
            best = tr
    if best * cols * 4 > (1 << 21):
        best = 8
    return best


def _adamw(w, g_arr, g_sel, m, v):
    rows, cols = w.shape
    tr = _row_tile(rows, cols)
    b1c = 1.0 - ADAM_B1 ** ADAM_STEP
    b2c = 1.0 - ADAM_B2 ** ADAM_STEP

    def body(w_ref, g_ref, m_ref, v_ref, go_ref, d_ref, mo_ref, vo_ref):
        gv = g_ref[...]
        mn = ADAM_B1 * m_ref[...] + (1.0 - ADAM_B1) * gv
        vn = ADAM_B2 * v_ref[...] + (1.0 - ADAM_B2) * (gv * gv)
        go_ref[...] = gv
        mo_ref[...] = mn
        vo_ref[...] = vn
        d_ref[...] = -ADAM_LR * ((mn / b1c) / (jnp.sqrt(vn / b2c) + ADAM_EPS) + ADAM_WD * w_ref[...])

    blk = pl.BlockSpec((tr, cols), lambda i: (i, 0))
    shp = _sds((rows, cols), F32)
    return pl.pallas_call(
        body, name="adamw", grid=(rows // tr,),
        in_specs=[blk, pl.BlockSpec((None, tr, cols), lambda i: (g_sel, i, 0)), blk, blk],
        out_specs=[blk] * 4, out_shape=[shp] * 4,
        compiler_params=_cp(1))(w, g_arr, m, v)


def _flip(v, bit):
    return 1 - v if bit else v


def _my_place():
    x, y, c = lax.axis_index("x"), lax.axis_index("y"), lax.axis_index("c")
    return x, y, c


class _Exchange:
    def __init__(self, operands, out_shape, aliases, sems, start, finish):
        self.operands, self.out_shape, self.aliases, self.sems = list(operands), list(out_shape), dict(aliases), list(sems)
        self.start, self.finish = start, finish


def _join(exchanges):
    exchanges = [e for e in exchanges if e is not None]
    if not exchanges:
        return None
    ops, outs, sems, aliases, spans = [], [], [], {}, []
    for e in exchanges:
        spans.append((len(ops), len(outs), len(sems), e))
        for i, j in e.aliases.items():
            aliases[len(ops) + i] = len(outs) + j
        ops += e.operands
        outs += e.out_shape
        sems += e.sems

    def run(which):
        def go(ins, res, sm):
            for io, oo, so, e in spans:
                getattr(e, which)(ins[io:io + len(e.operands)], res[oo:oo + len(e.out_shape)], sm[so:so + len(e.sems)])
        return go

    return _Exchange(ops, outs, aliases, sems, run("start"), run("finish"))


def _call(body, name, grid, in_specs, out_specs, out_shape, args, scratch=(), carry=None, io_alias=None):
    in_specs, out_specs, out_shape, scratch = list(in_specs), list(out_specs), list(out_shape), list(scratch)
    io_alias = dict(io_alias or {})
    if carry is None:
        return pl.pallas_call(body, name=name, grid=grid, in_specs=in_specs, out_specs=out_specs,
                              out_shape=out_shape, scratch_shapes=scratch, input_output_aliases=io_alias,
                              compiler_params=_cp(len(grid)))(*args)
    n_in, n_out, n_s = len(in_specs), len(out_specs), len(scratch)
    c_in, c_out = len(carry.operands), len(carry.out_shape)
    any_spec = pl.BlockSpec(memory_space=pl.ANY)

    def wrapped(*refs):
        ins, cins = refs[:n_in], refs[n_in:n_in + c_in]
        o0 = n_in + c_in
        outs, couts = refs[o0:o0 + n_out], refs[o0 + n_out:o0 + n_out + c_out]
        s0 = o0 + n_out + c_out
        scr, sems = refs[s0:s0 + n_s], refs[s0 + n_s:]
        first = pl.program_id(0) == 0
        last = pl.program_id(0) == grid[0] - 1
        for ax in range(1, len(grid)):
            first = jnp.logical_and(first, pl.program_id(ax) == 0)
            last = jnp.logical_and(last, pl.program_id(ax) == grid[ax] - 1)

        @pl.when(first)
        def _():
            carry.start(cins, couts, sems)

        body(*ins, *outs, *scr)

        @pl.when(last)
        def _():
            carry.finish(cins, couts, sems)

    return pl.pallas_call(
        wrapped, name=name, grid=grid, in_specs=in_specs + [any_spec] * c_in,
        out_specs=out_specs + [any_spec] * c_out, out_shape=out_shape + carry.out_shape,
        scratch_shapes=scratch + carry.sems,
        input_output_aliases={**io_alias, **{n_in + i: n_out + j for i, j in carry.aliases.items()}},
        compiler_params=_cp(len(grid)))(*args, *carry.operands)


def _whole_call(body, name, args, out_shape, scratch, carry=None):
    vm = pl.BlockSpec(memory_space=pltpu.VMEM)
    any_spec = pl.BlockSpec(memory_space=pl.ANY)
    out_shape, scratch = list(out_shape), list(scratch)
    n_in, n_out, n_s = len(args), len(out_shape), len(scratch)
    if carry is None:
        return pl.pallas_call(body, name=name, in_specs=[vm] * n_in, out_specs=[vm] * n_out, out_shape=out_shape,
                              scratch_shapes=scratch, compiler_params=_cp())(*args)
    c_in, c_out = len(carry.operands), len(carry.out_shape)

    def wrapped(*refs):
        ins, cins = refs[:n_in], refs[n_in:n_in + c_in]
        o0 = n_in + c_in
        outs, couts = refs[o0:o0 + n_out], refs[o0 + n_out:o0 + n_out + c_out]
        s0 = o0 + n_out + c_out
        scr, sems = refs[s0:s0 + n_s], refs[s0 + n_s:]
        carry.start(cins, couts, sems)
        body(*ins, *outs, *scr)
        carry.finish(cins, couts, sems)

    return pl.pallas_call(
        wrapped, name=name, in_specs=[vm] * n_in + [any_spec] * c_in, out_specs=[vm] * n_out + [any_spec] * c_out,
        out_shape=out_shape + carry.out_shape, scratch_shapes=scratch + carry.sems,
        input_output_aliases={n_in + i: n_out + j for i, j in carry.aliases.items()},
        compiler_params=_cp())(*args, *carry.operands)


def _alone(name, ex):
    any_spec = pl.BlockSpec(memory_space=pl.ANY)
    c_in, c_out = len(ex.operands), len(ex.out_shape)

    def body(*refs):
        ins, outs, sems = refs[:c_in], refs[c_in:c_in + c_out], refs[c_in + c_out:]
        ex.start(ins, outs, sems)
        ex.finish(ins, outs, sems)

    return pl.pallas_call(
        body, name=name, in_specs=[any_spec] * c_in, out_specs=[any_spec] * c_out, out_shape=ex.out_shape,
        scratch_shapes=ex.sems, input_output_aliases=ex.aliases, compiler_params=_cp())(*ex.operands)


def _ada_fwd(c_pad, w_ada, b_shard, carry=None):
    d = c_pad.shape[-1]
    cols = w_ada.shape[-1]
    chunk = 384

    def body(c_ref, w_ref, b_ref, call_ref, mod_ref, part, s1, r1, s2, r2):
        x, y, c = _my_place()
        dev = 4 * x + 2 * y + c
        chip = 2 * x + y
        call_ref[dev] = c_ref[...]

        def c_copy(k):
            px, py, pc = _flip(x, (k >> 2) & 1), _flip(y, (k >> 1) & 1), _flip(c, k & 1)
            return px, py, pc

        sends = []
        for k in range(1, N_DEV):
            px, py, pc = c_copy(k)
            cp = pltpu.make_async_remote_copy(src_ref=c_ref, dst_ref=call_ref.at[dev], send_sem=s1.at[k - 1],
                                              recv_sem=r1.at[k - 1], device_id=(px, py, pc), device_id_type=MESH)
            cp.start()
            sends.append(cp)
        for k in range(1, N_DEV):
            px, py, pc = c_copy(k)
            pltpu.make_async_remote_copy(src_ref=c_ref, dst_ref=call_ref.at[4 * px + 2 * py + pc],
                                         send_sem=s1.at[k - 1], recv_sem=r1.at[k - 1],
                                         device_id=(px, py, pc), device_id_type=MESH).wait_recv()
        for cp in sends:
            cp.wait_send()

        cs = call_ref[...].reshape(N_DEV * 8, d)
        sc = (cs * jax.nn.sigmoid(cs)).astype(BF)
        for n0 in range(0, cols, chunk):
            blk = _nn(sc, w_ref[:, n0:n0 + chunk].astype(BF)) + b_ref[:, n0:n0 + chunk]
            part[:, :, n0:n0 + chunk] = blk.reshape(N_DEV, 8, chunk)

        mod_ref[chip] = part[dev]
        sends = []
        for kk in range(1, N_CHIPS):
            px, py = _flip(x, (kk >> 1) & 1), _flip(y, kk & 1)
            cp = pltpu.make_async_remote_copy(src_ref=part.at[4 * px + 2 * py + c], dst_ref=mod_ref.at[chip],
                                              send_sem=s2.at[kk - 1], recv_sem=r2.at[kk - 1],
                                              device_id=(px, py, c), device_id_type=MESH)
            cp.start()
            sends.append(cp)
        for kk in range(1, N_CHIPS):
            px, py = _flip(x, (kk >> 1) & 1), _flip(y, kk & 1)
            pltpu.make_async_remote_copy(src_ref=part.at[dev], dst_ref=mod_ref.at[2 * px + py],
                                         send_sem=s2.at[kk - 1], recv_sem=r2.at[kk - 1],
                                         device_id=(px, py, c), device_id_type=MESH).wait_recv()
        for cp in sends:
            cp.wait_send()

    return _whole_call(
        body, "ada_fwd", (c_pad, w_ada, b_shard),
        [_sds((N_DEV, 8, d), F32), _sds((N_CHIPS, 8, cols), F32)],
        [pltpu.VMEM((N_DEV, 8, cols), F32),
         pltpu.SemaphoreType.DMA((N_DEV - 1,)), pltpu.SemaphoreType.DMA((N_DEV - 1,)),
         pltpu.SemaphoreType.DMA((N_CHIPS - 1,)), pltpu.SemaphoreType.DMA((N_CHIPS - 1,))], carry=carry)


def _ag_weights(bufs, kks=(1, 2, 3), relative=False):
    n, nk = len(bufs), len(kks)

    def half(b, which):
        hr = bufs[b].shape[2] // 2
        return pl.ds(pl.multiple_of(which * hr, 16), hr)

    def copies(outs, sems, b, i, kk):
        x, y, c = _my_place()
        chip = 2 * x + y
        px, py = _flip(x, (kk >> 1) & 1), _flip(y, kk & 1)
        mine, theirs = (0, kk) if relative else (chip, 2 * px + py)
        landing = kk if relative else chip
        k = nk * b + i
        send = pltpu.make_async_remote_copy(
            src_ref=outs[b].at[mine, :, half(b, c), :], dst_ref=outs[b].at[landing, :, half(b, c), :],
            send_sem=sems[0].at[k], recv_sem=sems[1].at[k], device_id=(px, py, c), device_id_type=MESH)
        got = outs[b].at[theirs, :, half(b, c), :]
        recv = pltpu.make_async_remote_copy(
            src_ref=got, dst_ref=got, send_sem=sems[0].at[k], recv_sem=sems[1].at[k],
            device_id=(px, py, c), device_id_type=MESH)
        fwd = pltpu.make_async_remote_copy(
            src_ref=got, dst_ref=got, send_sem=sems[2].at[k], recv_sem=sems[3].at[k],
            device_id=(x, y, 1 - c), device_id_type=MESH)
        other = outs[b].at[theirs, :, half(b, 1 - c), :]
        back = pltpu.make_async_remote_copy(
            src_ref=other, dst_ref=other, send_sem=sems[2].at[k], recv_sem=sems[3].at[k],
            device_id=(x, y, 1 - c), device_id_type=MESH)
        return send, recv, fwd, back

    def each(outs, sems):
        for b in range(n):
            for i, kk in enumerate(kks):
                yield copies(outs, sems, b, i, kk)

    def start(ins, outs, sems):
        for send, _, _, _ in each(outs, sems):
            send.start()

    def finish(ins, outs, sems):
        for _, recv, fwd, _ in each(outs, sems):
            recv.wait_recv()
            fwd.start()
        for send, _, fwd, back in each(outs, sems):
            back.wait_recv()
            send.wait_send()
            fwd.wait_send()

    return _Exchange(bufs, [_sds(s.shape, s.dtype) for s in bufs], {i: i for i in range(n)},
                     [pltpu.SemaphoreType.DMA((nk * n,))] * 4, start, finish)


def _rs_d2d(grads):
    n = len(grads)

    def copy(ins, outs, sems, b):
        x, y, c = _my_place()
        hr = grads[b].shape[2] // 2
        theirs = pl.ds(pl.multiple_of((1 - c) * hr, 8), hr)
        return pltpu.make_async_remote_copy(
            src_ref=ins[b].at[:, :, theirs, :], dst_ref=outs[b], send_sem=sems[0].at[b], recv_sem=sems[1].at[b],
            device_id=(x, y, 1 - c), device_id_type=MESH)

    def start(ins, outs, sems):
        for b in range(n):
            copy(ins, outs, sems, b).start()

    def finish(ins, outs, sems):
        for b in range(n):
            copy(ins, outs, sems, b).wait()

    return _Exchange(grads, [_sds(g.shape[:2] + (g.shape[2] // 2, g.shape[3]), F32) for g in grads], {},
                     [pltpu.SemaphoreType.DMA((n,))] * 2, start, finish)


def _add_halves(core, g, land):
    nchip, ng, rows, cols = g.shape
    hr = rows // 2
    tr = _row_tile(hr, cols)
    steps = hr // tr

    def body(core_ref, g_ref, l_ref, o_ref):
        del core_ref
        o_ref[...] = (g_ref[...] + l_ref[...]).astype(BF)

    return pl.pallas_call(
        body, name="add_halves",
        grid_spec=pltpu.PrefetchScalarGridSpec(
            num_scalar_prefetch=1, grid=(nchip, ng, steps),
            in_specs=[pl.BlockSpec((None, None, tr, cols), lambda j, a, i, cr: (j, a, cr[0] * steps + i, 0)),
                      pl.BlockSpec((None, None, tr, cols), lambda j, a, i, cr: (j, a, i, 0))],
            out_specs=pl.BlockSpec((None, None, tr, cols), lambda j, a, i, cr: (j, a, i, 0))),
        out_shape=_sds((nchip, ng, hr, cols), BF),
        compiler_params=_cp(3))(core, g, land)


def _rs_ici(parts, relative=False):
    n = len(parts)

    def copies(ins, outs, sems):
        x, y, c = _my_place()
        chip = 2 * x + y
        for b in range(n):
            for kk in range(1, N_CHIPS):
                px, py = _flip(x, (kk >> 1) & 1), _flip(y, kk & 1)
                k = 3 * b + kk - 1
                theirs, landing = (kk, kk) if relative else (2 * px + py, chip)
                send = pltpu.make_async_remote_copy(
                    src_ref=ins[b].at[theirs], dst_ref=outs[b].at[landing],
                    send_sem=sems[0].at[k], recv_sem=sems[1].at[k], device_id=(px, py, c), device_id_type=MESH)
                slot = outs[b].at[theirs]
                recv = pltpu.make_async_remote_copy(
                    src_ref=slot, dst_ref=slot, send_sem=sems[0].at[k], recv_sem=sems[1].at[k],
                    device_id=(px, py, c), device_id_type=MESH)
                yield send, recv

    def start(ins, outs, sems):
        for send, _ in copies(ins, outs, sems):
            send.start()

    def finish(ins, outs, sems):
        for send, recv in copies(ins, outs, sems):
            recv.wait_recv()
            send.wait_send()

    return _Exchange(parts, [_sds(p.shape, p.dtype) for p in parts], {},
                     [pltpu.SemaphoreType.DMA((3 * n,))] * 2, start, finish)


def _sum_chips(place, part, land, relative=False):
    nchip, ng, hr, cols = land.shape
    tr = _row_tile(hr, cols)
    steps = hr // tr

    def body(place_ref, p_ref, l1, l2, l3, o_ref):
        del place_ref
        o_ref[...] = ((p_ref[...].astype(F32) + l1[...].astype(F32)) + l2[...].astype(F32)) + l3[...].astype(F32)

    def slot(k):
        if relative:
            return pl.BlockSpec((None, None, tr, cols), lambda a, i, pr: (k, a, i, 0))
        return pl.BlockSpec((None, None, tr, cols), lambda a, i, pr: (jnp.bitwise_xor(pr[1], k), a, i, 0))

    return pl.pallas_call(
        body, name="sum_chips",
        grid_spec=pltpu.PrefetchScalarGridSpec(
            num_scalar_prefetch=1, grid=(ng, steps),
            in_specs=[slot(0), slot(1), slot(2), slot(3)],
            out_specs=pl.BlockSpec((None, tr, cols), lambda a, i, pr: (a, pr[0] * steps + i, 0))),
        out_shape=_sds((ng, 2 * hr, cols), F32),
        compiler_params=_cp(2))(place, part, land, land, land)


def _rs_final(bufs):
    n = len(bufs)

    def copy(outs, sems, b, which):
        x, y, c = _my_place()
        hr = bufs[b].shape[1] // 2
        rows = outs[b].at[:, pl.ds(pl.multiple_of((c if which == 0 else 1 - c) * hr, 8), hr), :]
        return pltpu.make_async_remote_copy(
            src_ref=rows, dst_ref=rows, send_sem=sems[0].at[b], recv_sem=sems[1].at[b],
            device_id=(x, y, 1 - c), device_id_type=MESH)

    def start(ins, outs, sems):
        for b in range(n):
            copy(outs, sems, b, 0).start()

    def finish(ins, outs, sems):
        for b in range(n):
            copy(outs, sems, b, 0).wait_send()
            copy(outs, sems, b, 1).wait_recv()

    return _Exchange(bufs, [_sds(h.shape, F32) for h in bufs], {i: i for i in range(n)},
                     [pltpu.SemaphoreType.DMA((n,))] * 2, start, finish)


def _small_sync(smalls, dmod_blk, c_all, carry=None):
    d = c_all.shape[-1]
    cols = dmod_blk.shape[-1]
    chunk = 384

    def body(sm_ref, dm_ref, c_ref, sum_ref, gw_ref, sm_all, dm_all, ssem, rsem):
        x, y, c = _my_place()
        dev = 4 * x + 2 * y + c
        chip = 2 * x + y
        sm_all[dev] = sm_ref[...]
        dm_all[dev] = dm_ref[chip]
        sends = []
        for k in range(1, N_DEV):
            px, py, pc = _flip(x, (k >> 2) & 1), _flip(y, (k >> 1) & 1), _flip(c, k & 1)
            a = pltpu.make_async_remote_copy(src_ref=sm_ref, dst_ref=sm_all.at[dev], send_sem=ssem.at[2 * (k - 1)],
                                             recv_sem=rsem.at[2 * (k - 1)], device_id=(px, py, pc),
                                             device_id_type=MESH)
            b = pltpu.make_async_remote_copy(src_ref=dm_ref.at[2 * px + py], dst_ref=dm_all.at[dev],
                                             send_sem=ssem.at[2 * (k - 1) + 1], recv_sem=rsem.at[2 * (k - 1) + 1],
                                             device_id=(px, py, pc), device_id_type=MESH)
            a.start()
            b.start()
            sends += [a, b]
        for k in range(1, N_DEV):
            px, py, pc = _flip(x, (k >> 2) & 1), _flip(y, (k >> 1) & 1), _flip(c, k & 1)
            pdev = 4 * px + 2 * py + pc
            pltpu.make_async_remote_copy(src_ref=sm_ref, dst_ref=sm_all.at[pdev], send_sem=ssem.at[2 * (k - 1)],
                                         recv_sem=rsem.at[2 * (k - 1)], device_id=(px, py, pc),
                                         device_id_type=MESH).wait_recv()
            pltpu.make_async_remote_copy(src_ref=dm_ref.at[chip], dst_ref=dm_all.at[pdev],
                                         send_sem=ssem.at[2 * (k - 1) + 1], recv_sem=rsem.at[2 * (k - 1) + 1],
                                         device_id=(px, py, pc), device_id_type=MESH).wait_recv()
        for cp in sends:
            cp.wait_send()

        tot = sm_all[0]
        for q in range(1, N_DEV):
            tot = tot + sm_all[q]
        sum_ref[...] = tot

        cs = c_ref[...].reshape(N_DEV * 8, d)
        sc = (cs * jax.nn.sigmoid(cs)).astype(BF)
        for n0 in range(0, cols, chunk):
            dmv = dm_all[:, :, n0:n0 + chunk].reshape(N_DEV * 8, chunk).astype(BF)
            gw_ref[:, n0:n0 + chunk] = _tn(sc, dmv)

    return _whole_call(
        body, "small_sync", (smalls, dmod_blk, c_all),
        [_sds(smalls.shape, F32), _sds((d, cols), F32)],
        [pltpu.VMEM((N_DEV,) + smalls.shape, F32), pltpu.VMEM((N_DEV, 8, cols), F32),
         pltpu.SemaphoreType.DMA((2 * (N_DEV - 1),)), pltpu.SemaphoreType.DMA((2 * (N_DEV - 1),))], carry=carry)


def _bucket_onehot():
    maps = np.stack([_bucket_map(dil).reshape(-1) for _, dil in DIL_CONFIGS])
    return (jnp.asarray(maps)[:, None, :] == jnp.arange(N_BUCKETS, dtype=jnp.int32)[None, :, None]).astype(BF)


def _dil_bias(rel_t, onehot):
    def body(r_ref, oh_ref, o_ref):
        rv = r_ref[...]
        hi = rv.astype(BF)
        lo = (rv - hi.astype(F32)).astype(BF)
        for c in range(len(DIL_CONFIGS)):
            o_ref[c] = _nn(hi, oh_ref[c]) + _nn(lo, oh_ref[c])

    return pl.pallas_call(body, name="dil_bias",
                          out_shape=_sds((len(DIL_CONFIGS), N_HEADS, BLOCK * 2 * BLOCK), F32),
                          compiler_params=_cp())(rel_t, onehot)


def _rowsum8(a):
    def body(a_ref, o_ref):
        o_ref[...] = jnp.sum(a_ref[...], axis=0, keepdims=True)

    return pl.pallas_call(body, name="rowsum8", out_shape=_sds((1, a.shape[1]), F32), compiler_params=_cp())(a)


def _local_step(x, mod, target, w, gains, rel_bias, place=None):
    nb, seq, d = x.shape
    t = nb * seq
    dist = place is not None
    core = place[0:1] if dist else None
    x0 = x.reshape(t, d)
    tgt = target.reshape(t, d)
    md = [mod[:, i:i + 1, :] for i in range(N_MOD)]
    sh1, sc1, gt1, sh2, sc2, gt2, sh3, sc3, gt3 = md
    g1, g2, g3 = gains["g_ffn1"], gains["g_mix"], gains["g_ffn2"]
    ones_g = _group_ones()

    def partial_sums(grads, lands):
        return [_add_halves(core, g, l) for g, l in zip(grads, lands)]

    def chip_sums(parts, lands):
        return [_sum_chips(place, p, l, relative=True) for p, l in zip(parts, lands)]

    gu1 = w["gu1"]
    res = _ffn_up(x0, g1, sc1, sh1, gu1, seq,
                  carry=_join([_ag_weights([w["d1"]], relative=True), _ag_weights([w["win"]])]) if dist else None)
    h1, a1, u1, s1 = res[:4]
    wd1, w_in = res[4:] if dist else (w["d1"], w["win"])
    f1, x1 = _ffn_down(s1, wd1, x0, gt1, seq, 0.5)

    res = _qkv_proj(x1, g2, sc2, sh2, w_in, seq,
                    carry=_join([_ag_weights([w["d2"]], relative=True), _ag_weights([w["wout"]])]) if dist else None)
    h2, qkv6, qkv_r4, qkv_r16 = res[:4]
    wd2, w_out = res[4:] if dist else (w["d2"], w["wout"])
    w_out2 = w_out.reshape(2 * D_GRP, d)
    qkv6b = qkv6.reshape(6, nb, seq, D_GRP)
    res = _sb_fwd(qkv6b, gains["g_sb_out"], nb, seq, carry=_ag_weights([w["gu2"]], relative=True) if dist else None)
    o_sb, on_sb = res[:2]
    wgu2 = res[2] if dist else w["gu2"]
    onehot = _bucket_onehot()
    bias = _dil_bias(rel_bias.T, onehot).reshape(len(DIL_CONFIGS), N_HEADS * BLOCK, 2 * BLOCK)
    o_cs, l_cs = [], []
    qkv_rs = [(qkv6b, 3), (qkv_r4, 0), (qkv_r16, 0)]
    for ci, (_, dil) in enumerate(DIL_CONFIGS):
        sub = seq // dil
        arr, base = qkv_rs[ci]
        arr = arr.reshape(base + 3, nb, sub, dil * D_GRP)
        qkv_rs[ci] = (arr, base)
        o_c, l_c = _dil_fwd(arr, base, bias[ci], nb, sub, dil)
        o_cs.append(o_c.reshape(t // dil, dil * D_GRP))
        l_cs.append(l_c.reshape(t // dil, dil * D_GRP))
    o_dil, on_dil = _dil_comb(o_cs, l_cs, gains["g_dil_out"])
    tmix, x2 = _mix_out(on_sb.reshape(t, D_GRP), on_dil, w_out2, x1, gt2, seq)

    h3, a3, u3, s3 = _ffn_up(x2, g3, sc3, sh3, wgu2, seq)
    f3, dx3, dg_final, loss = _ffn_down_loss(s3, wd2, x2, gt3, seq, 0.5, gains["g_final"], tgt)

    da3, du3, df3, dgt3, dx2, dsh3, dsc3, dg3 = _ffn_bwd_x(dx3, gt3, f3, wd2, a3, u3, wgu2, x2, g3, sc3, seq, 0.5)
    grads2 = [_ffn_bwd_w(h3, da3, du3, s3, df3)]

    res = _mix_bwd_out(
        dx2, gt2, tmix, w_out2, o_sb.reshape(t, D_GRP), o_dil, on_sb.reshape(t, D_GRP), on_dil,
        gains["g_sb_out"], gains["g_dil_out"], ones_g, seq, carry=_rs_d2d(grads2) if dist else None)
    do_sb, do_dil, dgt2, dg_sb, dg_dil, dw_out = res[:6]
    parts2 = partial_sums(grads2, res[6:]) if dist else None
    dw_out = dw_out.reshape(N_CHIPS, 1, 2 * D_GRP // N_CHIPS, d)
    res = _sb_bwd(qkv6b, do_sb.reshape(nb, seq, D_GRP), nb, seq,
                  carry=_rs_ici(parts2, relative=True) if dist else None)
    dqkv6 = res[0]
    halves2 = chip_sums(parts2, res[1:]) if dist else None
    dcs = _dil_comb_bwd(do_dil, o_cs, l_cs)
    dsum, a_tiles = [], []
    for ci, (_, dil) in enumerate(DIL_CONFIGS):
        sub = seq // dil
        do_c = dcs[ci].reshape(nb, sub, dil * D_GRP)
        dd_c = dcs[3 + ci].reshape(nb, sub, dil * D_GRP)
        res = _dil_bwd(qkv_rs[ci][0], qkv_rs[ci][1], bias[ci], do_c, dd_c, nb, sub, dil)
        dsum.append(res[0].reshape(3, t // dil, dil * D_GRP))
        a_tiles.append(res[1].reshape(N_HEADS, BLOCK * 2 * BLOCK))
    dqkv6 = _dqkv_dil_sum(dsum, dqkv6.reshape(6, t, D_GRP))
    drel = _relbias_grad(jnp.stack(a_tiles), onehot)
    dx1, dsh2, dsc2, dg2 = _mix_bwd_dh(dqkv6, w_in, x1, g2, sc2, dx2, seq)

    da1, du1, df1, dgt1 = _ffn_bwd_ds(dx1, gt1, f1, wd1, a1, u1, seq, 0.5)
    grads1 = [_ffn_bwd_w(h1, da1, du1, s1, df1)]
    res = _dw_in(h2, dqkv6, carry=_join([_rs_d2d(grads1), _rs_final(halves2)]) if dist else None)
    grads_m = [res[0], dw_out]
    parts1 = partial_sums(grads1, res[1:2]) if dist else None
    if dist:
        grads2 = res[2:3]
    res = _ffn_bwd_dh(da1, du1, gu1, x0, g1, sc1, dx1, seq,
                      carry=_join([_rs_ici(parts1, relative=True), _rs_d2d(grads_m)]) if dist else None)
    dx0, dsh1, dsc1, dg1 = res[:4]
    pending = None
    if dist:
        pending = (chip_sums(parts1, res[4:5]), partial_sums(grads_m, res[5:7]))

    dmod = jnp.concatenate([dsh1, dsc1, dgt1, dsh2, dsc2, dgt2, dsh3, dsc3, dgt3], axis=1)
    return dict(grad_x=dx0.reshape(nb, seq, d), loss=loss[0, 0], dmod=dmod.reshape(nb, N_MOD * d),
                dffn1=grads1[0], dffn2=grads2[0], dwin=grads_m[0], dwout=grads_m[1], pending=pending,
                dg_ffn1=dg1, dg_mix=dg2, dg_ffn2=dg3, dg_final=dg_final, dg_sb=dg_sb, dg_dil=dg_dil,
                drel=drel.T)


_SMALL_ORDER = (("b_ada", N_MOD * 1024), ("g_ffn1", 1024), ("g_mix", 1024), ("g_ffn2", 1024), ("g_final", 1024),
                ("g_sb_out", D_GRP), ("g_dil_out", D_GRP), ("rel_bias", N_BUCKETS * N_HEADS))


def _pack_small(parts, extra=None):
    flat = [parts[name].reshape(-1).astype(F32) for name, _ in _SMALL_ORDER]
    used = sum(sz for _, sz in _SMALL_ORDER)
    pad = SMALL_ROWS * 128 - used
    tail = jnp.zeros((pad,), F32)
    if extra is not None:
        tail = tail.at[0].set(extra)
    return jnp.concatenate(flat + [tail]).reshape(SMALL_ROWS, 128)


def _unpack_small(packed, shapes):
    flat = packed.reshape(-1)
    out, off = {}, 0
    for name, sz in _SMALL_ORDER:
        out[name] = flat[off:off + sz].reshape(shapes[name])
        off += sz
    return out, flat[off]


def kernel(x, c, w_ada, b_ada, g_ffn1, w1_gate, w1_up, w1_down, g_mix, w_in, g_sb_out, g_dil_out, w_out, rel_bias, g_ffn2, w2_gate, w2_up, w2_down, g_final, loss_target, m_w_ada, m_b_ada, m_g_ffn1, m_w1_gate, m_w1_up, m_w1_down, m_g_mix, m_w_in, m_g_sb_out, m_g_dil_out, m_w_out, m_rel_bias, m_g_ffn2, m_w2_gate, m_w2_up, m_w2_down, m_g_final, v_w_ada, v_b_ada, v_g_ffn1, v_w1_gate, v_w1_up, v_w1_down, v_g_mix, v_w_in, v_g_sb_out, v_g_dil_out, v_w_out, v_rel_bias, v_g_ffn2, v_w2_gate, v_w2_up, v_w2_down, v_g_final):
    nb, seq, d = x.shape
    xi, yi, ci = lax.axis_index("x"), lax.axis_index("y"), lax.axis_index("c")
    chip = 2 * xi + yi
    ada_cols = w_ada.shape[-1]

    c_pad = jnp.zeros((8, d), F32).at[:nb].set(c)
    b_shard = lax.dynamic_slice(b_ada, (0, chip * ada_cols), (1, ada_cols))
    shards = dict(gu1=jnp.stack([w1_gate[0], w1_up[0]]), d1=w1_down, win=w_in, wout=w_out,
                  gu2=jnp.stack([w2_gate[0], w2_up[0]]), d2=w2_down)
    bufs = {k: lax.dynamic_update_slice(lax.empty((N_CHIPS,) + s.shape, BF), s.astype(BF)[None],
                                        (chip if k in ("win", "wout") else 0, 0, 0, 0))
            for k, s in shards.items()}
    c_all, mod_blk, bufs["gu1"] = _ada_fwd(c_pad, w_ada[0], b_shard,
                                           carry=_ag_weights([bufs["gu1"]], relative=True))
    mod = jnp.transpose(mod_blk[:, :nb, :], (1, 0, 2)).reshape(nb, N_MOD, d)

    gains = dict(g_ffn1=g_ffn1, g_mix=g_mix, g_ffn2=g_ffn2, g_final=g_final.reshape(1, d),
                 g_sb_out=g_sb_out.reshape(1, D_GRP), g_dil_out=g_dil_out.reshape(1, D_GRP))
    place = jnp.stack([ci, chip]).astype(jnp.int32)
    r = _local_step(x, mod, loss_target, bufs, gains, rel_bias, place)

    dmod = r["dmod"]
    dmod_pad = jnp.zeros((8, N_MOD * d), F32).at[:nb].set(dmod)
    dmod_blk = jnp.transpose(dmod_pad.reshape(8, N_CHIPS, ada_cols), (1, 0, 2))
    small_parts = dict(b_ada=_rowsum8(dmod_pad), g_ffn1=r["dg_ffn1"], g_mix=r["dg_mix"], g_ffn2=r["dg_ffn2"],
                       g_final=r["dg_final"], g_sb_out=r["dg_sb"], g_dil_out=r["dg_dil"], rel_bias=r["drel"])
    halves1, parts_m = r["pending"]
    res = _small_sync(_pack_small(small_parts, r["loss"]), dmod_blk, c_all,
                      carry=_join([_rs_final(halves1), _rs_ici(parts_m)]))
    small_sum, g_wada, gffn1 = res[:3]
    halves_m = [_sum_chips(place, p, l) for p, l in zip(parts_m, res[3:5])]
    gwin, gwout = _alone("rs_last", _rs_final(halves_m))
    gffn2 = r["dffn2"]

    small_w = dict(b_ada=b_ada, g_ffn1=g_ffn1, g_mix=g_mix, g_ffn2=g_ffn2, g_final=g_final,
                   g_sb_out=g_sb_out, g_dil_out=g_dil_out, rel_bias=rel_bias)
    small_m = dict(b_ada=m_b_ada, g_ffn1=m_g_ffn1, g_mix=m_g_mix, g_ffn2=m_g_ffn2, g_final=m_g_final,
                   g_sb_out=m_g_sb_out, g_dil_out=m_g_dil_out, rel_bias=m_rel_bias)
    small_v = dict(b_ada=v_b_ada, g_ffn1=v_g_ffn1, g_mix=v_g_mix, g_ffn2=v_g_ffn2, g_final=v_g_final,
                   g_sb_out=v_g_sb_out, g_dil_out=v_g_dil_out, rel_bias=v_rel_bias)
    shapes = {k: v.shape for k, v in small_w.items()}
    sg, sd, sm, sv = _adamw(_pack_small(small_w), small_sum.reshape(1, SMALL_ROWS, 128), 0,
                            _pack_small(small_m), _pack_small(small_v))
    sg, loss = _unpack_small(sg, shapes)
    sd, _ = _unpack_small(sd, shapes)
    sm, _ = _unpack_small(sm, shapes)
    sv, _ = _unpack_small(sv, shapes)

    big = {}

    def upd(name, w, g_arr, sel, m, v, transposed=False):
        swap = (lambda a: jnp.swapaxes(a, -1, -2)) if transposed else (lambda a: a)
        w2, m2, v2 = [swap(a)[0] for a in (w, m, v)]
        big[name] = [swap(a[None]) for a in _adamw(w2, g_arr, sel, m2, v2)]

    upd("w_ada", w_ada, g_wada.reshape(1, d, ada_cols), 0, m_w_ada, v_w_ada)
    upd("w1_gate", w1_gate, gffn1, 0, m_w1_gate, v_w1_gate, transposed=True)
    upd("w1_up", w1_up, gffn1, 1, m_w1_up, v_w1_up, transposed=True)
    upd("w1_down", w1_down, gffn1, 2, m_w1_down, v_w1_down)
    upd("w_in", w_in, gwin, 0, m_w_in, v_w_in)
    upd("w_out", w_out, gwout, 0, m_w_out, v_w_out)
    upd("w2_gate", w2_gate, gffn2, 0, m_w2_gate, v_w2_gate, transposed=True)
    upd("w2_up", w2_up, gffn2, 1, m_w2_up, v_w2_up, transposed=True)
    upd("w2_down", w2_down, gffn2, 2, m_w2_down, v_w2_down)

    names = ["w_ada", "b_ada", "g_ffn1", "w1_gate", "w1_up", "w1_down", "g_mix", "w_in", "g_sb_out", "g_dil_out",
             "w_out", "rel_bias", "g_ffn2", "w2_gate", "w2_up", "w2_down", "g_final"]
    outs = [loss, r["grad_x"]]
    for k, small in enumerate((sg, sd, sm, sv)):
        for name in names:
            outs.append(big[name][k] if name in big else small[name])
    return tuple(outs)
```

```python
import math

import numpy as np
import jax
import jax.numpy as jnp
from jax import lax
from jax.experimental import pallas as pl
from jax.experimental.pallas import tpu as pltpu

F32 = jnp.float32
BF = jnp.bfloat16
MESH = pl.DeviceIdType.MESH

HEAD_DIM = 64
N_HEADS = 8
D_GRP = N_HEADS * HEAD_DIM
DIL_CONFIGS = ((128, 1), (512, 4), (2048, 16))
N_STEPS = 128
BLOCK = 128
N_BUCKETS = 32
MAX_DISTANCE = 2048
N_MOD = 9
EPS = 1e-6
NEG_INF = -1e30
SCALE = HEAD_DIM ** -0.5

ADAM_LR = 0.001
ADAM_B1 = 0.9
ADAM_B2 = 0.999
ADAM_EPS = 1e-08
ADAM_WD = 0.01
ADAM_STEP = 10

N_CHIPS = 4
N_DEV = 8
VMEM_LIMIT = 56 * 1024 * 1024
TM = 512
TQ = 256
KB = 256
SMALL_ROWS = 120


def _cp(n_axes=0, **kw):
    sem = ("arbitrary",) * n_axes if n_axes else None
    return pltpu.CompilerParams(dimension_semantics=sem, vmem_limit_bytes=VMEM_LIMIT, **kw)


def _nn(a, b):
    return jnp.dot(a, b, preferred_element_type=F32)


def _nt(a, b):
    return lax.dot_general(a, b, (((1,), (1,)), ((), ())), preferred_element_type=F32)


def _tn(a, b):
    return lax.dot_general(a, b, (((0,), (0,)), ((), ())), preferred_element_type=F32)


def _nn2(x, m):
    hi = x.astype(BF)
    lo = (x - hi.astype(F32)).astype(BF)
    r = _nn(jnp.concatenate([hi, lo], axis=0), m)
    return r[:x.shape[0]] + r[x.shape[0]:]


def _softplus(z):
    return jnp.maximum(z, 0.0) + jnp.log(1.0 + jnp.exp(-jnp.abs(z)))


def _sds(shape, dtype):
    return jax.ShapeDtypeStruct(shape, dtype)


def _whole(a):
    nd = a.ndim
    return pl.BlockSpec(a.shape, lambda *_: (0,) * nd, pipeline_mode=pl.Buffered(1))


def _modnorm_bwd_tile(dh, xv, gv, scv, dxo):
    r = lax.rsqrt(jnp.mean(xv * xv, axis=-1, keepdims=True) + EPS)
    n = xv * r
    ng = n * gv
    dsh = jnp.sum(dh, axis=0, keepdims=True)
    dsc = jnp.sum(dh * ng, axis=0, keepdims=True)
    dy = dh * (1.0 + scv)
    dg = jnp.sum(dy * n, axis=0, keepdims=True)
    dn = dy * gv
    dx = dxo + r * (dn - n * jnp.mean(dn * n, axis=-1, keepdims=True))
    return dx, dsh, dsc, dg


def _acc_rows(ref, val, first):
    @pl.when(first)
    def _():
        ref[...] = val

    @pl.when(jnp.logical_not(first))
    def _():
        ref[...] += val


def _modnorm_tile(x_ref, g_ref, sc_ref, sh_ref):
    xv = x_ref[...]
    r = lax.rsqrt(jnp.mean(xv * xv, axis=-1, keepdims=True) + EPS)
    return (((xv * r) * g_ref[...]) * (1.0 + sc_ref[...]) + sh_ref[...]).astype(BF)


def _ffn_up(x, g, sc, sh, wgu, seq, carry=None):
    t, d = x.shape
    fs = wgu.shape[-1]
    per = seq // TM

    def body(x_ref, g_ref, sc_ref, sh_ref, w_ref, h_ref, p_ref, q_ref, s_ref):
        hv = _modnorm_tile(x_ref, g_ref, sc_ref, sh_ref)
        h_ref[...] = hv
        for j in range(N_CHIPS):
            a = _nn(hv, w_ref[j, 0])
            u = _nn(hv, w_ref[j, 1])
            sig = jax.nn.sigmoid(a)
            q = a * sig
            p_ref[j] = (u * (sig * (1.0 + a * (1.0 - sig)))).astype(BF)
            q_ref[j] = q.astype(BF)
            s_ref[j] = (q * u).astype(BF)

    row = pl.BlockSpec((TM, d), lambda m: (m, 0))
    ex = pl.BlockSpec((None, 1, d), lambda m: (m // per, 0, 0))
    blk = pl.BlockSpec((N_CHIPS, TM, fs), lambda m: (0, m, 0))
    return _call(
        body, "ffn_up", (t // TM,),
        [row, pl.BlockSpec((1, d), lambda m: (0, 0)), ex, ex, _whole(wgu)],
        [row, blk, blk, blk],
        [_sds((t, d), BF)] + [_sds((N_CHIPS, t, fs), BF)] * 3,
        (x, g, sc, sh, wgu), carry=carry)


def _ffn_down(s, wd, x, gt, seq, coef, carry=None):
    _, t, fs = s.shape
    d = x.shape[-1]
    per = seq // TM

    def body(s_ref, w_ref, x_ref, gt_ref, f_ref, xo_ref):
        f = _nn(s_ref[0], w_ref[0, 0])
        for j in range(1, N_CHIPS):
            f = f + _nn(s_ref[j], w_ref[j, 0])
        f_ref[...] = f.astype(BF)
        xo_ref[...] = x_ref[...] + (coef * gt_ref[...]) * f

    row = pl.BlockSpec((TM, d), lambda m: (m, 0))
    return _call(
        body, "ffn_down", (t // TM,),
        [pl.BlockSpec((N_CHIPS, TM, fs), lambda m: (0, m, 0)), _whole(wd), row,
         pl.BlockSpec((None, 1, d), lambda m: (m // per, 0, 0))],
        [row, row], [_sds((t, d), BF), _sds((t, d), F32)], (s, wd, x, gt), carry=carry)


def _ffn_bwd_ds(dxo, gt, f, wd, p, q, seq, coef, carry=None):
    t, d = dxo.shape
    fs = p.shape[-1]
    per = seq // TM
    nb = t // seq

    def body(dxo_ref, gt_ref, f_ref, w_ref, p_ref, q_ref, da_ref, du_ref, df_ref, dgt_ref):
        m = pl.program_id(0)
        dxv = dxo_ref[...]
        df = ((coef * gt_ref[...]) * dxv).astype(BF)
        df_ref[...] = df
        _acc_rows(dgt_ref, coef * jnp.sum(dxv * f_ref[...].astype(F32), axis=0, keepdims=True), m % per == 0)
        for j in range(N_CHIPS):
            ds = _nt(df, w_ref[j, 0])
            da_ref[j] = (ds * p_ref[j].astype(F32)).astype(BF)
            du_ref[j] = (ds * q_ref[j].astype(F32)).astype(BF)

    row = pl.BlockSpec((TM, d), lambda m: (m, 0))
    blk = pl.BlockSpec((N_CHIPS, TM, fs), lambda m: (0, m, 0))
    ex = pl.BlockSpec((None, 1, d), lambda m: (m // per, 0, 0))
    return _call(
        body, "ffn_bwd_ds", (t // TM,),
        [row, ex, row, _whole(wd), blk, blk],
        [blk, blk, row, ex],
        [_sds((N_CHIPS, t, fs), BF), _sds((N_CHIPS, t, fs), BF), _sds((t, d), BF), _sds((nb, 1, d), F32)],
        (dxo, gt, f, wd, p, q), carry=carry)


TM_X = 256


def _ffn_bwd_x(dxo, gt, f, wd, p, q, wgu, x, g, sc, seq, coef):
    t, d = dxo.shape
    fs = p.shape[-1]
    per = seq // TM_X
    nb = t // seq

    def body(dxo_ref, gt_ref, f_ref, wd_ref, p_ref, q_ref, w_ref, x_ref, g_ref, sc_ref,
             da_ref, du_ref, df_ref, dgt_ref, dx_ref, dsh_ref, dsc_ref, dg_ref):
        m = pl.program_id(0)
        dxv = dxo_ref[...]
        df = ((coef * gt_ref[...]) * dxv).astype(BF)
        df_ref[...] = df
        _acc_rows(dgt_ref, coef * jnp.sum(dxv * f_ref[...].astype(F32), axis=0, keepdims=True), m % per == 0)
        dh = None
        for j in range(N_CHIPS):
            ds = _nt(df, wd_ref[j, 0])
            da = (ds * p_ref[j].astype(F32)).astype(BF)
            du = (ds * q_ref[j].astype(F32)).astype(BF)
            da_ref[j] = da
            du_ref[j] = du
            part = _nt(da, w_ref[j, 0]) + _nt(du, w_ref[j, 1])
            dh = part if dh is None else dh + part
        dx, dsh, dsc, dg = _modnorm_bwd_tile(dh, x_ref[...], g_ref[...], sc_ref[...], dxv)
        dx_ref[...] = dx
        _acc_rows(dsh_ref, dsh, m % per == 0)
        _acc_rows(dsc_ref, dsc, m % per == 0)
        _acc_rows(dg_ref, dg, m == 0)

    row = pl.BlockSpec((TM_X, d), lambda m: (m, 0))
    blk = pl.BlockSpec((N_CHIPS, TM_X, fs), lambda m: (0, m, 0))
    ex = pl.BlockSpec((None, 1, d), lambda m: (m // per, 0, 0))
    vec = pl.BlockSpec((1, d), lambda m: (0, 0))
    exs = _sds((nb, 1, d), F32)
    return pl.pallas_call(
        body, name="ffn_bwd_x", grid=(t // TM_X,),
        in_specs=[row, ex, row, _whole(wd), blk, blk, _whole(wgu), row, vec, ex],
        out_specs=[blk, blk, row, ex, row, ex, ex, vec],
        out_shape=[_sds((N_CHIPS, t, fs), BF), _sds((N_CHIPS, t, fs), BF), _sds((t, d), BF), exs,
                   _sds((t, d), F32), exs, exs, _sds((1, d), F32)],
        compiler_params=_cp(1))(dxo, gt, f, wd, p, q, wgu, x, g, sc)


TK_W = 1024


def _ffn_bwd_w(h, da, du, s, df):
    t, d = h.shape
    fs = da.shape[-1]

    def body(h_ref, da_ref, du_ref, s_ref, df_ref, o_ref):
        kt = pl.program_id(1)
        hv = h_ref[...]
        parts = (_tn(da_ref[...], hv), _tn(du_ref[...], hv), _tn(s_ref[...], df_ref[...]))

        @pl.when(kt == 0)
        def _():
            for i, p in enumerate(parts):
                o_ref[i] = p

        @pl.when(kt != 0)
        def _():
            for i, p in enumerate(parts):
                o_ref[i] += p

    row = pl.BlockSpec((TK_W, d), lambda j, kt: (kt, 0))
    blk = pl.BlockSpec((None, TK_W, fs), lambda j, kt: (j, kt, 0))
    return pl.pallas_call(
        body, name="ffn_bwd_w", grid=(N_CHIPS, t // TK_W),
        in_specs=[row, blk, blk, blk, row],
        out_specs=pl.BlockSpec((None, 3, fs, d), lambda j, kt: (j, 0, 0, 0)),
        out_shape=_sds((N_CHIPS, 3, fs, d), F32),
        compiler_params=_cp(2))(h, da, du, s, df)


def _ffn_bwd_dh(da, du, wgu, x, g, sc, dxo, seq, carry=None):
    _, t, fs = da.shape
    d = x.shape[-1]
    per = seq // TM
    nb = t // seq

    def body(da_ref, du_ref, w_ref, x_ref, g_ref, sc_ref, dxo_ref, dx_ref, dsh_ref, dsc_ref, dg_ref):
        m = pl.program_id(0)
        dh = _nt(da_ref[0], w_ref[0, 0]) + _nt(du_ref[0], w_ref[0, 1])
        for j in range(1, N_CHIPS):
            dh = dh + _nt(da_ref[j], w_ref[j, 0]) + _nt(du_ref[j], w_ref[j, 1])
        dx, dsh, dsc, dg = _modnorm_bwd_tile(dh, x_ref[...], g_ref[...], sc_ref[...], dxo_ref[...])
        dx_ref[...] = dx
        _acc_rows(dsh_ref, dsh, m % per == 0)
        _acc_rows(dsc_ref, dsc, m % per == 0)
        _acc_rows(dg_ref, dg, m == 0)

    row = pl.BlockSpec((TM, d), lambda m: (m, 0))
    blk = pl.BlockSpec((N_CHIPS, TM, fs), lambda m: (0, m, 0))
    ex = pl.BlockSpec((None, 1, d), lambda m: (m // per, 0, 0))
    vec = pl.BlockSpec((1, d), lambda m: (0, 0))
    return _call(
        body, "ffn_bwd_dh", (t // TM,),
        [blk, blk, _whole(wgu), row, vec, ex, row],
        [row, ex, ex, vec],
        [_sds((t, d), F32), _sds((nb, 1, d), F32), _sds((nb, 1, d), F32), _sds((1, d), F32)],
        (da, du, wgu, x, g, sc, dxo), carry=carry)


def _qkv_proj(x, g, sc, sh, w_in, seq, carry=None):
    t, d = x.shape
    wc = w_in.shape[-1]
    per = seq // TM

    dils = [dil for _, dil in DIL_CONFIGS if dil > 1]

    def body(x_ref, g_ref, sc_ref, sh_ref, w_ref, h_ref, o_ref, *rest):
        res_refs, buf = rest[:len(dils)], rest[len(dils)]
        hv = _modnorm_tile(x_ref, g_ref, sc_ref, sh_ref)
        h_ref[...] = hv
        for j in range(N_CHIPS):
            rf = _nn(hv, w_ref[j, 0])
            r = rf.astype(BF)
            for a, lc, off, width in _col_pieces(j, wc):
                o_ref[a, :, lc:lc + width] = r[:, off:off + width]
                if a < 3:
                    continue
                for c0 in range(0, width, 128):
                    cg = (lc + c0) // 128
                    buf[...] = rf[:, off + c0:off + c0 + 128]
                    for ref, dil in zip(res_refs, dils):
                        for rr in range(dil):
                            ref[a - 3, :, rr * D_GRP + cg * 128:rr * D_GRP + (cg + 1) * 128] = (
                                buf[pl.ds(rr, TM // dil, stride=dil), :].astype(BF))

    row = pl.BlockSpec((TM, d), lambda m: (m, 0))
    ex = pl.BlockSpec((None, 1, d), lambda m: (m // per, 0, 0))
    return _call(
        body, "qkv_proj", (t // TM,),
        [row, pl.BlockSpec((1, d), lambda m: (0, 0)), ex, ex, _whole(w_in)],
        [row, pl.BlockSpec((6, TM, D_GRP), lambda m: (0, m, 0))]
        + [pl.BlockSpec((3, TM // dil, dil * D_GRP), lambda m: (0, m, 0)) for dil in dils],
        [_sds((t, d), BF), _sds((6, t, D_GRP), BF)] + [_sds((3, t // dil, dil * D_GRP), BF) for dil in dils],
        (x, g, sc, sh, w_in), scratch=[pltpu.VMEM((TM, 128), F32)], carry=carry)


def _col_pieces(j, wc):
    out, off = [], 0
    while off < wc:
        a, lc = divmod(j * wc + off, D_GRP)
        width = min(D_GRP - lc, wc - off)
        out.append((a, lc, off, width))
        off += width
    return out


def _chip_cols(g6_ref, j, wc):
    return jnp.concatenate([g6_ref[a, :, lc:lc + width] for a, lc, _, width in _col_pieces(j, wc)], axis=1)


def _mix_out(on_sb, on_dil, w_out, x, gt, seq):
    t, d = x.shape
    per = seq // TM

    def body(a_ref, b_ref, w_ref, x_ref, gt_ref, t_ref, xo_ref):
        tv = _nn(a_ref[...], w_ref[0:D_GRP, :]) + _nn(b_ref[...], w_ref[D_GRP:2 * D_GRP, :])
        t_ref[...] = tv.astype(BF)
        xo_ref[...] = x_ref[...] + gt_ref[...] * tv

    row = pl.BlockSpec((TM, d), lambda m: (m, 0))
    half = pl.BlockSpec((TM, D_GRP), lambda m: (m, 0))
    return pl.pallas_call(
        body, name="mix_out", grid=(t // TM,),
        in_specs=[half, half, pl.BlockSpec((2 * D_GRP, d), lambda m: (0, 0)), row,
                  pl.BlockSpec((None, 1, d), lambda m: (m // per, 0, 0))],
        out_specs=[row, row],
        out_shape=[_sds((t, d), BF), _sds((t, d), F32)],
        compiler_params=_cp(1))(on_sb, on_dil, w_out, x, gt)


def _sb_masks():
    lane = lax.broadcasted_iota(jnp.int32, (1, 2 * HEAD_DIM), 1)
    hm0 = lane < HEAD_DIM
    rel = lax.broadcasted_iota(jnp.int32, (TQ, KB), 0) - lax.broadcasted_iota(jnp.int32, (TQ, KB), 1)
    kr = lax.broadcasted_iota(jnp.int32, (KB, KB), 0)
    kc = lax.broadcasted_iota(jnp.int32, (KB, KB), 1)
    return hm0, rel, kr, kc


def _headnorm_pair(o, gv, hm0):
    o2 = o * o
    ms0 = jnp.sum(jnp.where(hm0, o2, 0.0), axis=-1, keepdims=True) * (1.0 / HEAD_DIM)
    ms1 = jnp.sum(jnp.where(hm0, 0.0, o2), axis=-1, keepdims=True) * (1.0 / HEAD_DIM)
    r = jnp.where(hm0, lax.rsqrt(ms0 + EPS), lax.rsqrt(ms1 + EPS))
    return (o * r) * gv


SB_DEAD = -104.0


def _alive(c_l):
    return (jnp.max(c_l) > SB_DEAD).astype(jnp.int32)


def _sb_fwd(qkv6, g_sb, nb, seq, carry=None):
    nq = seq // TQ

    def body(q_ref, k_ref, v_ref, g_ref, o_ref, on_ref):
        qi = pl.program_id(2)
        hm0, rel, kr, kc = _sb_masks()
        upper = (kr > kc).astype(BF)
        heads = _dil_masks()[0]
        qs = _stack_heads(q_ref[...], heads)
        causal2 = jnp.concatenate([rel] * GRP_HEADS, axis=0) > 0

        def block(kj, causal, c_l, acc):
            ks = pl.multiple_of(kj * KB, KB)
            z = _nt(qs, k_ref[pl.ds(ks, KB), :]) * SCALE
            sp = _softplus(z)
            ln = -sp if causal is None else jnp.where(causal, -sp, 0.0)
            suf = _nn2(ln, upper)
            w = jnp.exp((z - sp) + (suf + c_l))
            if causal is not None:
                w = jnp.where(causal, w, 0.0)
            return c_l + (suf[:, 0:1] + ln[:, 0:1]), acc + _nn(w.astype(BF), v_ref[pl.ds(ks, KB), :])

        c_l, acc = block(qi, causal2, jnp.zeros((GRP_HEADS * TQ, 1), F32), jnp.zeros((GRP_HEADS * TQ, GRP_W), F32))

        def cond(carry):
            return jnp.logical_and(carry[0] <= qi, carry[1] > 0)

        def kbody(carry):
            it, _, c_l, acc = carry
            c_l, acc = block(qi - it, None, c_l, acc)
            return it + 1, _alive(c_l), c_l, acc

        acc = lax.while_loop(cond, kbody, (jnp.int32(1), _alive(c_l), c_l, acc))[3]
        o = _unstack_heads(acc, heads, TQ)
        o_ref[...] = o.astype(BF)
        gv = g_ref[...]
        for half in range(GRP_W // 128):
            lanes = slice(half * 128, (half + 1) * 128)
            on_ref[:, lanes] = _headnorm_pair(o[:, lanes], gv[:, lanes], hm0).astype(BF)

    w = GRP_W
    full = lambda i: pl.BlockSpec((None, None, seq, w), lambda b, hp, q: (i, b, 0, hp))
    qblk = pl.BlockSpec((None, None, TQ, w), lambda b, hp, q: (0, b, q, hp))
    oblk = pl.BlockSpec((None, TQ, w), lambda b, hp, q: (b, q, hp))
    return _call(
        body, "sb_fwd", (nb, N_HEADS // GRP_HEADS, nq),
        [qblk, full(1), full(2), pl.BlockSpec((1, w), lambda b, hp, q: (0, hp))],
        [oblk, oblk],
        [_sds((nb, seq, D_GRP), BF), _sds((nb, seq, D_GRP), BF)],
        (qkv6, qkv6, qkv6, g_sb), carry=carry)


def _sb_bwd(qkv6, do, nb, seq, carry=None):
    nq = seq // TQ
    nk = seq // KB

    def body(q_ref, k_ref, v_ref, do_ref, out_ref, dk_acc, dv_acc, g_st, s_st):
        qi = pl.program_id(2)
        hm0, rel, kr, kc = _sb_masks()
        upper = (kr > kc).astype(BF)
        lower = (kr < kc).astype(BF)

        @pl.when(qi == 0)
        def _():
            dk_acc[...] = jnp.zeros_like(dk_acc)
            dv_acc[...] = jnp.zeros_like(dv_acc)

        heads = _dil_masks()[0]
        qs = _stack_heads(q_ref[...], heads)
        dos = _stack_heads(do_ref[...], heads)
        causal2 = jnp.concatenate([rel] * GRP_HEADS, axis=0) > 0

        def weights(kj, causal, c_l):
            ks = pl.multiple_of(kj * KB, KB)
            vb = v_ref[pl.ds(ks, KB), :]
            z = _nt(qs, k_ref[pl.ds(ks, KB), :]) * SCALE
            sp = _softplus(z)
            ln = -sp if causal is None else jnp.where(causal, -sp, 0.0)
            suf = _nn2(ln, upper)
            lsz = z - sp
            w = jnp.exp(lsz + (suf + c_l))
            if causal is not None:
                w = jnp.where(causal, w, 0.0)
            g_st[kj] = w * _nt(dos, vb)
            s_st[kj] = jnp.exp(lsz)
            dv_acc[pl.ds(ks, KB), :] += _tn(w.astype(BF), dos)
            return c_l + (suf[:, 0:1] + ln[:, 0:1])

        zc = jnp.zeros((GRP_HEADS * TQ, 1), F32)
        c_l = weights(qi, causal2, zc)

        def acond(carry):
            return jnp.logical_and(carry[0] <= qi, carry[1] > 0)

        def abody(carry):
            c_l = weights(qi - carry[0], None, carry[2])
            return carry[0] + 1, _alive(c_l), c_l

        n_used = lax.while_loop(acond, abody, (jnp.int32(1), _alive(c_l), c_l))[0]

        def grads(kj, causal, c_g, dq):
            ks = pl.multiple_of(kj * KB, KB)
            kb = k_ref[pl.ds(ks, KB), :]
            g = g_st[kj]
            sig = s_st[kj]
            pre = _nn(g.astype(BF), lower)
            dz = g * (1.0 - sig) - sig * (pre + c_g)
            if causal is not None:
                dz = jnp.where(causal, dz, 0.0)
            dzb = (dz * SCALE).astype(BF)
            dk_acc[pl.ds(ks, KB), :] += _tn(dzb, qs)
            return c_g + (pre[:, KB - 1:KB] + g[:, KB - 1:KB]), dq + _nn(dzb, kb)

        c_g, dq = lax.fori_loop(qi - n_used + 1, qi, lambda kj, cr: grads(kj, None, *cr),
                                (zc, jnp.zeros((GRP_HEADS * TQ, GRP_W), F32)))
        _, dq = grads(qi, causal2, c_g, dq)
        dq = _unstack_heads(dq, heads, TQ)
        out_ref[0, pl.ds(pl.multiple_of(qi * TQ, TQ), TQ), :] = dq.astype(BF)

        @pl.when(qi == nq - 1)
        def _():
            out_ref[1] = dk_acc[...].astype(BF)
            out_ref[2] = dv_acc[...].astype(BF)

    w = GRP_W
    full = lambda i: pl.BlockSpec((None, None, seq, w), lambda b, hp, q: (i, b, 0, hp))
    qblk = pl.BlockSpec((None, None, TQ, w), lambda b, hp, q: (0, b, q, hp))
    oblk = pl.BlockSpec((None, TQ, w), lambda b, hp, q: (b, q, hp))
    return _call(
        body, "sb_bwd", (nb, N_HEADS // GRP_HEADS, nq),
        [qblk, full(1), full(2), oblk],
        [pl.BlockSpec((3, None, seq, w), lambda b, hp, q: (0, b, 0, hp))],
        [_sds((6, nb, seq, D_GRP), BF)], (qkv6, qkv6, qkv6, do),
        scratch=[pltpu.VMEM((seq, w), F32), pltpu.VMEM((seq, w), F32),
                 pltpu.VMEM((nk, GRP_HEADS * TQ, KB), F32), pltpu.VMEM((nk, GRP_HEADS * TQ, KB), F32)],
        carry=carry)


def _t5_bucket(n):
    max_exact = N_BUCKETS // 2
    nf = np.maximum(n, 1).astype(np.float32)
    large = max_exact + (np.log(nf / max_exact) / math.log(MAX_DISTANCE / max_exact)
                         * (N_BUCKETS - max_exact)).astype(np.int32)
    large = np.minimum(large, N_BUCKETS - 1)
    return np.where(n < max_exact, n, large).astype(np.int32)


def _bucket_map(dilation):
    step = BLOCK + np.arange(BLOCK)[:, None] - np.arange(2 * BLOCK)[None, :]
    return _t5_bucket(np.clip(step, 0, N_STEPS) * dilation)


GRP_HEADS = 4
GRP_W = GRP_HEADS * HEAD_DIM


def _dil_masks():
    lane = lax.broadcasted_iota(jnp.int32, (1, GRP_W), 1)
    heads = [jnp.logical_and(lane >= HEAD_DIM * i, lane < HEAD_DIM * (i + 1)) for i in range(GRP_HEADS)]
    iq = jnp.bitwise_and(lax.broadcasted_iota(jnp.int32, (GRP_HEADS * BLOCK, BLOCK), 0), BLOCK - 1)
    ik = lax.broadcasted_iota(jnp.int32, (GRP_HEADS * BLOCK, BLOCK), 1)
    return heads, ik <= iq, ik >= iq


def _stack_heads(x, heads):
    zero = jnp.zeros_like(x)
    return jnp.concatenate([jnp.where(hm, x, zero) for hm in heads], axis=0)


def _unstack_heads(xs, heads, rows=BLOCK):
    out = xs[0:rows]
    for i in range(1, GRP_HEADS):
        out = jnp.where(heads[i], xs[i * rows:(i + 1) * rows], out)
    return out


def _dil_rows(n):
    rs = pl.multiple_of(n * BLOCK, BLOCK)
    ps = pl.multiple_of(jnp.maximum(n - 1, 0) * BLOCK, BLOCK)
    return pl.ds(rs, BLOCK), pl.ds(ps, BLOCK)


def _dil_probs(qs, kc, kp, b_ref, gi, valid_c, valid_p):
    rows = slice(gi * GRP_HEADS * BLOCK, (gi + 1) * GRP_HEADS * BLOCK)
    zc = _nt(qs, kc) * SCALE + b_ref[rows, BLOCK:2 * BLOCK]
    zp = _nt(qs, kp) * SCALE + b_ref[rows, 0:BLOCK]
    zc = jnp.where(valid_c, zc, NEG_INF)
    zp = jnp.where(valid_p, zp, NEG_INF)
    m = jnp.maximum(jnp.max(zc, axis=-1, keepdims=True), jnp.max(zp, axis=-1, keepdims=True))
    ec = jnp.exp(zc - m)
    ep = jnp.exp(zp - m)
    den = jnp.sum(ec, axis=-1, keepdims=True) + jnp.sum(ep, axis=-1, keepdims=True)
    return ec, ep, den, m


def _dil_fwd(qkv6r, base, bias, nb, sub_len, dilation):
    n_blk = sub_len // BLOCK

    def body(q_ref, k_ref, v_ref, b_ref, o_ref, l_ref):
        heads, valid_c, valid_p0 = _dil_masks()

        def nbody(n, carry):
            cur, prev = _dil_rows(n)
            valid_p = jnp.logical_and(valid_p0, n > 0)
            for gi in range(N_HEADS // GRP_HEADS):
                lanes = slice(gi * GRP_W, (gi + 1) * GRP_W)
                qs = _stack_heads(q_ref[cur, lanes], heads)
                ec, ep, den, m = _dil_probs(qs, k_ref[cur, lanes], k_ref[prev, lanes], b_ref, gi, valid_c, valid_p)
                o = (_nn(ec.astype(BF), v_ref[cur, lanes]) + _nn(ep.astype(BF), v_ref[prev, lanes])) / den
                o_ref[cur, lanes] = _unstack_heads(o, heads).astype(BF)
                l_ref[cur, lanes] = _unstack_heads(jnp.broadcast_to(m + jnp.log(den), o.shape), heads)
            return carry

        lax.fori_loop(0, n_blk, nbody, 0)

    seqblk = lambda i: pl.BlockSpec((None, None, sub_len, D_GRP), lambda b, r: (i, b, 0, r))
    oblk = pl.BlockSpec((None, sub_len, D_GRP), lambda b, r: (b, 0, r))
    shp = _sds((nb, sub_len, dilation * D_GRP), F32)
    return pl.pallas_call(
        body, name="dil_fwd_%d" % dilation, grid=(nb, dilation),
        in_specs=[seqblk(base), seqblk(base + 1), seqblk(base + 2), _whole(bias)],
        out_specs=[oblk, oblk], out_shape=[_sds(shp.shape, BF), shp],
        compiler_params=_cp(2))(qkv6r, qkv6r, qkv6r, bias)


def _dil_bwd(qkv6r, base, bias, do_c, dd_c, nb, sub_len, dilation, carry=None):
    n_blk = sub_len // BLOCK

    def body(q_ref, k_ref, v_ref, b_ref, do_ref, dd_ref, out_ref, a_ref, dk_acc, dv_acc):
        heads, valid_c, valid_p0 = _dil_masks()
        first = jnp.logical_and(pl.program_id(0) == 0, pl.program_id(1) == 0)

        @pl.when(first)
        def _():
            a_ref[...] = jnp.zeros_like(a_ref)

        dk_acc[...] = jnp.zeros_like(dk_acc)
        dv_acc[...] = jnp.zeros_like(dv_acc)

        def nbody(n, carry):
            cur, prev = _dil_rows(n)
            valid_p = jnp.logical_and(valid_p0, n > 0)
            for gi in range(N_HEADS // GRP_HEADS):
                lanes = slice(gi * GRP_W, (gi + 1) * GRP_W)
                kc, kp = k_ref[cur, lanes], k_ref[prev, lanes]
                vc, vp = v_ref[cur, lanes], v_ref[prev, lanes]
                qs = _stack_heads(q_ref[cur, lanes], heads)
                dos = _stack_heads(do_ref[cur, lanes], heads).astype(BF)
                dds = jnp.sum(_stack_heads(dd_ref[cur, lanes], heads), axis=-1, keepdims=True) * (1.0 / HEAD_DIM)
                ec, ep, den, _ = _dil_probs(qs, kc, kp, b_ref, gi, valid_c, valid_p)
                inv = 1.0 / den
                pc = ec * inv
                pp = ep * inv
                dzc = pc * (_nt(dos, vc) + dds)
                dzp = pp * (_nt(dos, vp) + dds)
                rows = slice(gi * GRP_HEADS * BLOCK, (gi + 1) * GRP_HEADS * BLOCK)
                a_ref[rows, BLOCK:2 * BLOCK] += dzc
                a_ref[rows, 0:BLOCK] += dzp
                dzcb = (dzc * SCALE).astype(BF)
                dzpb = (dzp * SCALE).astype(BF)
                out_ref[0, cur, lanes] = _unstack_heads(_nn(dzcb, kc) + _nn(dzpb, kp), heads).astype(BF)
                dk_acc[cur, lanes] += _tn(dzcb, qs)
                dk_acc[prev, lanes] += _tn(dzpb, qs)
                dv_acc[cur, lanes] += _tn(pc.astype(BF), dos)
                dv_acc[prev, lanes] += _tn(pp.astype(BF), dos)
            return carry

        lax.fori_loop(0, n_blk, nbody, 0)
        out_ref[1] = dk_acc[...].astype(BF)
        out_ref[2] = dv_acc[...].astype(BF)

    seqblk = lambda i: pl.BlockSpec((None, None, sub_len, D_GRP), lambda b, r: (i, b, 0, r))
    oblk = pl.BlockSpec((None, sub_len, D_GRP), lambda b, r: (b, 0, r))
    return _call(
        body, "dil_bwd_%d" % dilation, (nb, dilation),
        [seqblk(base), seqblk(base + 1), seqblk(base + 2), _whole(bias), oblk, oblk],
        [pl.BlockSpec((3, None, sub_len, D_GRP), lambda b, r: (0, b, 0, r)),
         pl.BlockSpec((N_HEADS * BLOCK, 2 * BLOCK), lambda b, r: (0, 0))],
        [_sds((3, nb, sub_len, dilation * D_GRP), BF), _sds((N_HEADS * BLOCK, 2 * BLOCK), F32)],
        (qkv6r, qkv6r, qkv6r, bias, do_c, dd_c),
        scratch=[pltpu.VMEM((sub_len, D_GRP), F32)] * 2, carry=carry)


def _group_ones():
    idx = np.arange(D_GRP) // HEAD_DIM
    return jnp.asarray((idx[:, None] == idx[None, :]).astype(np.float32), dtype=BF)


def _dil_alphas(l1, l4, l16):
    mx = jnp.maximum(jnp.maximum(l1, l4), l16)
    e1 = jnp.exp(l1 - mx)
    e4 = jnp.exp(l4 - mx)
    e16 = jnp.exp(l16 - mx)
    den = e1 + e4 + e16
    return e1 / den, e4 / den, e16 / den


def _residue_spec(dil):
    return pl.BlockSpec((TM // dil, dil * D_GRP), lambda m: (m, 0))


def _from_residue(src, dil, cg, buf):
    if dil == 1:
        return src[:, cg * 128:(cg + 1) * 128].astype(F32)
    for r in range(dil):
        buf[pl.ds(r, TM // dil, stride=dil), :] = (
            src[:, r * D_GRP + cg * 128:r * D_GRP + (cg + 1) * 128].astype(F32))
    return buf[...]


def _to_residue(dst, dil, cg, buf, val):
    if dil == 1:
        dst[:, cg * 128:(cg + 1) * 128] = val.astype(dst.dtype)
        return
    buf[...] = val
    for r in range(dil):
        dst[:, r * D_GRP + cg * 128:r * D_GRP + (cg + 1) * 128] = (
            buf[pl.ds(r, TM // dil, stride=dil), :].astype(dst.dtype))


def _pair_sum(x, hm0):
    s0 = jnp.sum(jnp.where(hm0, x, 0.0), axis=-1, keepdims=True)
    s1 = jnp.sum(jnp.where(hm0, 0.0, x), axis=-1, keepdims=True)
    return jnp.where(hm0, s0, s1)


def _dil_comb(os, ls, g_dil):
    t = os[0].shape[0]
    dils = [dil for _, dil in DIL_CONFIGS]

    def body(o1, l1, o4, l4, o16, l16, g_ref, o_ref, on_ref, b0, b1, b2, b3):
        hm0 = lax.broadcasted_iota(jnp.int32, (1, 128), 1) < HEAD_DIM
        for cg in range(D_GRP // 128):
            lanes = slice(cg * 128, (cg + 1) * 128)
            ov = [_from_residue(src, dil, cg, buf) for src, dil, buf in zip((o1, o4, o16), dils, (None, b0, b1))]
            lv = [_from_residue(src, dil, cg, buf) for src, dil, buf in zip((l1, l4, l16), dils, (None, b2, b3))]
            a1, a4, a16 = _dil_alphas(*lv)
            o = a1 * ov[0] + a4 * ov[1] + a16 * ov[2]
            o_ref[:, lanes] = o.astype(BF)
            on_ref[:, lanes] = _headnorm_pair(o, g_ref[:, lanes], hm0).astype(BF)

    blk = pl.BlockSpec((TM, D_GRP), lambda m: (m, 0))
    specs = [_residue_spec(dil) for dil in dils for _ in range(2)]
    return pl.pallas_call(
        body, name="dil_comb", grid=(t // TM,),
        in_specs=specs + [pl.BlockSpec((1, D_GRP), lambda m: (0, 0))],
        out_specs=[blk, blk],
        out_shape=[_sds((t, D_GRP), BF), _sds((t, D_GRP), BF)],
        scratch_shapes=[pltpu.VMEM((TM, 128), F32)] * 4,
        compiler_params=_cp(1))(os[0], ls[0], os[1], ls[1], os[2], ls[2], g_dil)


def _dil_comb_bwd(do, os, ls):
    t = do.shape[0]
    dils = [dil for _, dil in DIL_CONFIGS]

    def body(do_ref, o1, l1, o4, l4, o16, l16, d1, d4, d16, e1, e4, e16, b0, b1, b2, b3):
        hm0 = lax.broadcasted_iota(jnp.int32, (1, 128), 1) < HEAD_DIM
        for cg in range(D_GRP // 128):
            dov = do_ref[:, cg * 128:(cg + 1) * 128].astype(F32)
            ov = [_from_residue(src, dil, cg, buf) for src, dil, buf in zip((o1, o4, o16), dils, (None, b0, b1))]
            lv = [_from_residue(src, dil, cg, buf) for src, dil, buf in zip((l1, l4, l16), dils, (None, b2, b3))]
            al = _dil_alphas(*lv)
            sbar = al[0] * _pair_sum(dov * ov[0], hm0)
            for a_c, o_c in zip(al[1:], ov[1:]):
                sbar = sbar + a_c * _pair_sum(dov * o_c, hm0)
            for a_c, dil, dref, eref in zip(al, dils, (d1, d4, d16), (e1, e4, e16)):
                _to_residue(dref, dil, cg, b0, a_c * dov)
                _to_residue(eref, dil, cg, b1, -a_c * sbar)

    specs = [_residue_spec(dil) for dil in dils]
    return pl.pallas_call(
        body, name="dil_comb_bwd", grid=(t // TM,),
        in_specs=[pl.BlockSpec((TM, D_GRP), lambda m: (m, 0))] + [sp for sp in specs for _ in range(2)],
        out_specs=specs + specs,
        out_shape=[_sds((t // dil, dil * D_GRP), BF) for dil in dils]
        + [_sds((t // dil, dil * D_GRP), F32) for dil in dils],
        scratch_shapes=[pltpu.VMEM((TM, 128), F32)] * 4,
        compiler_params=_cp(1))(do, os[0], ls[0], os[1], ls[1], os[2], ls[2])


def _dqkv_dil_sum(ds, dqkv6):
    t = dqkv6.shape[1]
    dils = [dil for _, dil in DIL_CONFIGS]

    def body(*refs):
        srcs, o_ref, acc = refs[:len(dils)], refs[len(dils) + 1], refs[len(dils) + 2]
        for a in range(3):
            for cg in range(D_GRP // 128):
                for src, dil in zip(srcs, dils):
                    for r in range(dil):
                        part = src[a, :, r * D_GRP + cg * 128:r * D_GRP + (cg + 1) * 128].astype(F32)
                        rows = pl.ds(r, TM // dil, stride=dil) if dil > 1 else slice(None)
                        if dil == dils[0]:
                            acc[rows, :] = part
                        else:
                            acc[rows, :] += part
                o_ref[a, :, cg * 128:(cg + 1) * 128] = acc[...].astype(BF)

    return pl.pallas_call(
        body, name="dqkv_dil_sum", grid=(t // TM,),
        in_specs=[pl.BlockSpec((3, TM // dil, dil * D_GRP), lambda m: (0, m, 0)) for dil in dils]
        + [pl.BlockSpec(memory_space=pl.ANY)],
        out_specs=pl.BlockSpec((3, TM, D_GRP), lambda m: (1, m, 0)),
        out_shape=_sds((6, t, D_GRP), BF), input_output_aliases={len(dils): 0},
        scratch_shapes=[pltpu.VMEM((TM, 128), F32)],
        compiler_params=_cp(1))(*ds, dqkv6)


def _relbias_grad(a_all, onehot):
    def body(a_ref, oh_ref, o_ref):
        acc = jnp.zeros((N_HEADS, N_BUCKETS), F32)
        for c in range(len(DIL_CONFIGS)):
            av = a_ref[c]
            hi = av.astype(BF)
            lo = (av - hi.astype(F32)).astype(BF)
            acc = acc + _nt(hi, oh_ref[c]) + _nt(lo, oh_ref[c])
        o_ref[...] = acc

    return pl.pallas_call(body, name="relbias_grad", out_shape=_sds((N_HEADS, N_BUCKETS), F32),
                          compiler_params=_cp())(a_all, onehot)


def _headnorm_bwd(dn, o, gv, mv):
    ms = _nn2(o * o, mv) * (1.0 / HEAD_DIM)
    r = lax.rsqrt(ms + EPS)
    nrm = o * r
    dg = jnp.sum(dn * nrm, axis=0, keepdims=True)
    dnn = dn * gv
    do = r * (dnn - nrm * (_nn2(dnn * nrm, mv) * (1.0 / HEAD_DIM)))
    return do, dg


def _mix_bwd_out(dx, gt, tv, w_out, o_sb, o_dil, on_sb, on_dil, g_sb, g_dil, ones_g, seq, carry=None):
    t, d = dx.shape
    per = seq // TM
    nb = t // seq

    def body(dx_ref, gt_ref, t_ref, w_ref, osb, odl, onsb, ondl, gsb, gdl, m_ref,
             dosb, dodl, dgt_ref, dgsb, dgdl, dw_ref):
        m = pl.program_id(0)
        dxv = dx_ref[...]
        dt = (gt_ref[...] * dxv).astype(BF)
        _acc_rows(dgt_ref, jnp.sum(dxv * t_ref[...].astype(F32), axis=0, keepdims=True), m % per == 0)
        mv = m_ref[...]
        don_sb = _nt(dt, w_ref[0:D_GRP, :])
        don_dl = _nt(dt, w_ref[D_GRP:2 * D_GRP, :])
        do1, dg1 = _headnorm_bwd(don_sb, osb[...].astype(F32), gsb[...], mv)
        do2, dg2 = _headnorm_bwd(don_dl, odl[...].astype(F32), gdl[...], mv)
        dosb[...] = do1.astype(BF)
        dodl[...] = do2.astype(BF)
        _acc_rows(dgsb, dg1, m == 0)
        _acc_rows(dgdl, dg2, m == 0)
        p1 = _tn(onsb[...], dt)
        p2 = _tn(ondl[...], dt)

        @pl.when(m == 0)
        def _():
            dw_ref[0:D_GRP, :] = p1
            dw_ref[D_GRP:2 * D_GRP, :] = p2

        @pl.when(m != 0)
        def _():
            dw_ref[0:D_GRP, :] += p1
            dw_ref[D_GRP:2 * D_GRP, :] += p2

    row = pl.BlockSpec((TM, d), lambda m: (m, 0))
    half = pl.BlockSpec((TM, D_GRP), lambda m: (m, 0))
    ex = pl.BlockSpec((None, 1, d), lambda m: (m // per, 0, 0))
    gvec = pl.BlockSpec((1, D_GRP), lambda m: (0, 0))
    wblk = pl.BlockSpec((2 * D_GRP, d), lambda m: (0, 0))
    return _call(
        body, "mix_bwd_out", (t // TM,),
        [row, ex, row, wblk, half, half, half, half, gvec, gvec, pl.BlockSpec((D_GRP, D_GRP), lambda m: (0, 0))],
        [half, half, ex, gvec, gvec, wblk],
        [_sds((t, D_GRP), BF), _sds((t, D_GRP), BF), _sds((nb, 1, d), F32),
         _sds((1, D_GRP), F32), _sds((1, D_GRP), F32), _sds((2 * D_GRP, d), F32)],
        (dx, gt, tv, w_out, o_sb, o_dil, on_sb, on_dil, g_sb, g_dil, ones_g), carry=carry)


def _dw_in(h, dqkv6, carry=None):
    t, d = h.shape
    wc = 6 * D_GRP // N_CHIPS

    def body(h_ref, g_ref, o_ref):
        kt = pl.program_id(0)
        hv = h_ref[...]
        for j in range(N_CHIPS):
            p = _tn(hv, _chip_cols(g_ref, j, wc))

            @pl.when(kt == 0)
            def _(p=p, j=j):
                o_ref[j, 0] = p

            @pl.when(kt != 0)
            def _(p=p, j=j):
                o_ref[j, 0] += p

    return _call(
        body, "dw_in", (t // TK_W,),
        [pl.BlockSpec((TK_W, d), lambda kt: (kt, 0)), pl.BlockSpec((6, TK_W, D_GRP), lambda kt: (0, kt, 0))],
        [pl.BlockSpec((N_CHIPS, 1, d, wc), lambda kt: (0, 0, 0, 0))],
        [_sds((N_CHIPS, 1, d, wc), F32)], (h, dqkv6), carry=carry)


def _mix_bwd_dh(dqkv6, w_in, x, g, sc, dxo, seq, carry=None):
    _, t, _ = dqkv6.shape
    d = x.shape[-1]
    wc = w_in.shape[-1]
    per = seq // TM
    nb = t // seq

    def body(g6_ref, w_ref, x_ref, g_ref, sc_ref, dxo_ref, dx_ref, dsh_ref, dsc_ref, dg_ref):
        m = pl.program_id(0)
        dh = _nt(_chip_cols(g6_ref, 0, wc), w_ref[0, 0])
        for j in range(1, N_CHIPS):
            dh = dh + _nt(_chip_cols(g6_ref, j, wc), w_ref[j, 0])
        dx, dsh, dsc, dg = _modnorm_bwd_tile(dh, x_ref[...], g_ref[...], sc_ref[...], dxo_ref[...])
        dx_ref[...] = dx
        _acc_rows(dsh_ref, dsh, m % per == 0)
        _acc_rows(dsc_ref, dsc, m % per == 0)
        _acc_rows(dg_ref, dg, m == 0)

    row = pl.BlockSpec((TM, d), lambda m: (m, 0))
    ex = pl.BlockSpec((None, 1, d), lambda m: (m // per, 0, 0))
    vec = pl.BlockSpec((1, d), lambda m: (0, 0))
    return _call(
        body, "mix_bwd_dh", (t // TM,),
        [pl.BlockSpec((6, TM, D_GRP), lambda m: (0, m, 0)), _whole(w_in), row, vec, ex, row],
        [row, ex, ex, vec],
        [_sds((t, d), F32), _sds((nb, 1, d), F32), _sds((nb, 1, d), F32), _sds((1, d), F32)],
        (dqkv6, w_in, x, g, sc, dxo), carry=carry)


def _ffn_down_loss(s, wd, x, gt, seq, coef, g, target):
    _, t, fs = s.shape
    d = x.shape[-1]
    per = seq // TM
    steps = t // TM

    def body(s_ref, w_ref, x_ref, gt_ref, g_ref, t_ref, f_ref, dx_ref, dg_ref, loss_ref, lacc):
        m = pl.program_id(0)
        f = _nn(s_ref[0], w_ref[0, 0])
        for j in range(1, N_CHIPS):
            f = f + _nn(s_ref[j], w_ref[j, 0])
        f_ref[...] = f.astype(BF)
        xv = x_ref[...] + (coef * gt_ref[...]) * f
        gv = g_ref[...]
        r = lax.rsqrt(jnp.mean(xv * xv, axis=-1, keepdims=True) + EPS)
        n = xv * r
        err = n * gv - t_ref[...]
        dy = err * (1.0 / d)
        _acc_rows(dg_ref, jnp.sum(dy * n, axis=0, keepdims=True), m == 0)
        dn = dy * gv
        dx_ref[...] = r * (dn - n * jnp.mean(dn * n, axis=-1, keepdims=True))
        _acc_rows(lacc, jnp.sum(err * err, axis=0, keepdims=True), m == 0)

        @pl.when(m == steps - 1)
        def _():
            tot = jnp.sum(lacc[...], axis=-1, keepdims=True) * (0.5 / d)
            loss_ref[...] = jnp.broadcast_to(tot, (1, 128))

    row = pl.BlockSpec((TM, d), lambda m: (m, 0))
    vec = pl.BlockSpec((1, d), lambda m: (0, 0))
    return pl.pallas_call(
        body, name="ffn_down_loss", grid=(steps,),
        in_specs=[pl.BlockSpec((N_CHIPS, TM, fs), lambda m: (0, m, 0)), _whole(wd), row,
                  pl.BlockSpec((None, 1, d), lambda m: (m // per, 0, 0)), vec, row],
        out_specs=[row, row, vec, pl.BlockSpec((1, 128), lambda m: (0, 0))],
        out_shape=[_sds((t, d), BF), _sds((t, d), F32), _sds((1, d), F32), _sds((1, 128), F32)],
        scratch_shapes=[pltpu.VMEM((1, d), F32)],
        compiler_params=_cp(1))(s, wd, x, gt, g, target)


def _row_tile(rows, cols):
    best = rows
    for tr in range(8, rows + 1, 8):
        if rows % tr == 0 and tr * cols * 4 <= (1 << 20):
            best = tr
    if best * cols * 4 > (1 << 21):
        best = 8
    return best


def _adamw(w, g_arr, g_sel, m, v):
    rows, cols = w.shape
    tr = _row_tile(rows, cols)
    b1c = 1.0 - ADAM_B1 ** ADAM_STEP
    b2c = 1.0 - ADAM_B2 ** ADAM_STEP

    def body(w_ref, g_ref, m_ref, v_ref, go_ref, d_ref, mo_ref, vo_ref):
        gv = g_ref[...]
        mn = ADAM_B1 * m_ref[...] + (1.0 - ADAM_B1) * gv
        vn = ADAM_B2 * v_ref[...] + (1.0 - ADAM_B2) * (gv * gv)
        go_ref[...] = gv
        mo_ref[...] = mn
        vo_ref[...] = vn
        d_ref[...] = -ADAM_LR * ((mn / b1c) / (jnp.sqrt(vn / b2c) + ADAM_EPS) + ADAM_WD * w_ref[...])

    blk = pl.BlockSpec((tr, cols), lambda i: (i, 0))
    shp = _sds((rows, cols), F32)
    return pl.pallas_call(
        body, name="adamw", grid=(rows // tr,),
        in_specs=[blk, pl.BlockSpec((None, tr, cols), lambda i: (g_sel, i, 0)), blk, blk],
        out_specs=[blk] * 4, out_shape=[shp] * 4,
        compiler_params=_cp(1))(w, g_arr, m, v)


def _flip(v, bit):
    return 1 - v if bit else v


def _my_place():
    x, y, c = lax.axis_index("x"), lax.axis_index("y"), lax.axis_index("c")
    return x, y, c


class _Exchange:
    def __init__(self, operands, out_shape, aliases, sems, start, finish):
        self.operands, self.out_shape, self.aliases, self.sems = list(operands), list(out_shape), dict(aliases), list(sems)
        self.start, self.finish = start, finish


def _join(exchanges):
    exchanges = [e for e in exchanges if e is not None]
    if not exchanges:
        return None
    ops, outs, sems, aliases, spans = [], [], [], {}, []
    for e in exchanges:
        spans.append((len(ops), len(outs), len(sems), e))
        for i, j in e.aliases.items():
            aliases[len(ops) + i] = len(outs) + j
        ops += e.operands
        outs += e.out_shape
        sems += e.sems

    def run(which):
        def go(ins, res, sm):
            for io, oo, so, e in spans:
                getattr(e, which)(ins[io:io + len(e.operands)], res[oo:oo + len(e.out_shape)], sm[so:so + len(e.sems)])
        return go

    return _Exchange(ops, outs, aliases, sems, run("start"), run("finish"))


def _call(body, name, grid, in_specs, out_specs, out_shape, args, scratch=(), carry=None, io_alias=None):
    in_specs, out_specs, out_shape, scratch = list(in_specs), list(out_specs), list(out_shape), list(scratch)
    io_alias = dict(io_alias or {})
    if carry is None:
        return pl.pallas_call(body, name=name, grid=grid, in_specs=in_specs, out_specs=out_specs,
                              out_shape=out_shape, scratch_shapes=scratch, input_output_aliases=io_alias,
                              compiler_params=_cp(len(grid)))(*args)
    n_in, n_out, n_s = len(in_specs), len(out_specs), len(scratch)
    c_in, c_out = len(carry.operands), len(carry.out_shape)
    any_spec = pl.BlockSpec(memory_space=pl.ANY)

    def wrapped(*refs):
        ins, cins = refs[:n_in], refs[n_in:n_in + c_in]
        o0 = n_in + c_in
        outs, couts = refs[o0:o0 + n_out], refs[o0 + n_out:o0 + n_out + c_out]
        s0 = o0 + n_out + c_out
        scr, sems = refs[s0:s0 + n_s], refs[s0 + n_s:]
        first = pl.program_id(0) == 0
        last = pl.program_id(0) == grid[0] - 1
        for ax in range(1, len(grid)):
            first = jnp.logical_and(first, pl.program_id(ax) == 0)
            last = jnp.logical_and(last, pl.program_id(ax) == grid[ax] - 1)

        @pl.when(first)
        def _():
            carry.start(cins, couts, sems)

        body(*ins, *outs, *scr)

        @pl.when(last)
        def _():
            carry.finish(cins, couts, sems)

    return pl.pallas_call(
        wrapped, name=name, grid=grid, in_specs=in_specs + [any_spec] * c_in,
        out_specs=out_specs + [any_spec] * c_out, out_shape=out_shape + carry.out_shape,
        scratch_shapes=scratch + carry.sems,
        input_output_aliases={**io_alias, **{n_in + i: n_out + j for i, j in carry.aliases.items()}},
        compiler_params=_cp(len(grid)))(*args, *carry.operands)


def _whole_call(body, name, args, out_shape, scratch, carry=None):
    vm = pl.BlockSpec(memory_space=pltpu.VMEM)
    any_spec = pl.BlockSpec(memory_space=pl.ANY)
    out_shape, scratch = list(out_shape), list(scratch)
    n_in, n_out, n_s = len(args), len(out_shape), len(scratch)
    if carry is None:
        return pl.pallas_call(body, name=name, in_specs=[vm] * n_in, out_specs=[vm] * n_out, out_shape=out_shape,
                              scratch_shapes=scratch, compiler_params=_cp())(*args)
    c_in, c_out = len(carry.operands), len(carry.out_shape)

    def wrapped(*refs):
        ins, cins = refs[:n_in], refs[n_in:n_in + c_in]
        o0 = n_in + c_in
        outs, couts = refs[o0:o0 + n_out], refs[o0 + n_out:o0 + n_out + c_out]
        s0 = o0 + n_out + c_out
        scr, sems = refs[s0:s0 + n_s], refs[s0 + n_s:]
        carry.start(cins, couts, sems)
        body(*ins, *outs, *scr)
        carry.finish(cins, couts, sems)

    return pl.pallas_call(
        wrapped, name=name, in_specs=[vm] * n_in + [any_spec] * c_in, out_specs=[vm] * n_out + [any_spec] * c_out,
        out_shape=out_shape + carry.out_shape, scratch_shapes=scratch + carry.sems,
        input_output_aliases={n_in + i: n_out + j for i, j in carry.aliases.items()},
        compiler_params=_cp())(*args, *carry.operands)


def _alone(name, ex):
    any_spec = pl.BlockSpec(memory_space=pl.ANY)
    c_in, c_out = len(ex.operands), len(ex.out_shape)

    def body(*refs):
        ins, outs, sems = refs[:c_in], refs[c_in:c_in + c_out], refs[c_in + c_out:]
        ex.start(ins, outs, sems)
        ex.finish(ins, outs, sems)

    return pl.pallas_call(
        body, name=name, in_specs=[any_spec] * c_in, out_specs=[any_spec] * c_out, out_shape=ex.out_shape,
        scratch_shapes=ex.sems, input_output_aliases=ex.aliases, compiler_params=_cp())(*ex.operands)


def _ada_fwd(c_pad, w_ada, b_shard, carry=None):
    d = c_pad.shape[-1]
    cols = w_ada.shape[-1]
    chunk = 384

    def body(c_ref, w_ref, b_ref, call_ref, mod_ref, part, s1, r1, s2, r2):
        x, y, c = _my_place()
        dev = 4 * x + 2 * y + c
        chip = 2 * x + y
        call_ref[dev] = c_ref[...]

        def c_copy(k):
            px, py, pc = _flip(x, (k >> 2) & 1), _flip(y, (k >> 1) & 1), _flip(c, k & 1)
            return px, py, pc

        sends = []
        for k in range(1, N_DEV):
            px, py, pc = c_copy(k)
            cp = pltpu.make_async_remote_copy(src_ref=c_ref, dst_ref=call_ref.at[dev], send_sem=s1.at[k - 1],
                                              recv_sem=r1.at[k - 1], device_id=(px, py, pc), device_id_type=MESH)
            cp.start()
            sends.append(cp)
        for k in range(1, N_DEV):
            px, py, pc = c_copy(k)
            pltpu.make_async_remote_copy(src_ref=c_ref, dst_ref=call_ref.at[4 * px + 2 * py + pc],
                                         send_sem=s1.at[k - 1], recv_sem=r1.at[k - 1],
                                         device_id=(px, py, pc), device_id_type=MESH).wait_recv()
        for cp in sends:
            cp.wait_send()

        cs = call_ref[...].reshape(N_DEV * 8, d)
        sc = (cs * jax.nn.sigmoid(cs)).astype(BF)
        for n0 in range(0, cols, chunk):
            blk = _nn(sc, w_ref[:, n0:n0 + chunk].astype(BF)) + b_ref[:, n0:n0 + chunk]
            part[:, :, n0:n0 + chunk] = blk.reshape(N_DEV, 8, chunk)

        mod_ref[chip] = part[dev]
        sends = []
        for kk in range(1, N_CHIPS):
            px, py = _flip(x, (kk >> 1) & 1), _flip(y, kk & 1)
            cp = pltpu.make_async_remote_copy(src_ref=part.at[4 * px + 2 * py + c], dst_ref=mod_ref.at[chip],
                                              send_sem=s2.at[kk - 1], recv_sem=r2.at[kk - 1],
                                              device_id=(px, py, c), device_id_type=MESH)
            cp.start()
            sends.append(cp)
        for kk in range(1, N_CHIPS):
            px, py = _flip(x, (kk >> 1) & 1), _flip(y, kk & 1)
            pltpu.make_async_remote_copy(src_ref=part.at[dev], dst_ref=mod_ref.at[2 * px + py],
                                         send_sem=s2.at[kk - 1], recv_sem=r2.at[kk - 1],
                                         device_id=(px, py, c), device_id_type=MESH).wait_recv()
        for cp in sends:
            cp.wait_send()

    return _whole_call(
        body, "ada_fwd", (c_pad, w_ada, b_shard),
        [_sds((N_DEV, 8, d), F32), _sds((N_CHIPS, 8, cols), F32)],
        [pltpu.VMEM((N_DEV, 8, cols), F32),
         pltpu.SemaphoreType.DMA((N_DEV - 1,)), pltpu.SemaphoreType.DMA((N_DEV - 1,)),
         pltpu.SemaphoreType.DMA((N_CHIPS - 1,)), pltpu.SemaphoreType.DMA((N_CHIPS - 1,))], carry=carry)


def _ag_weights(bufs, kks=(1, 2, 3), relative=False):
    n, nk = len(bufs), len(kks)

    def half(b, which):
        hr = bufs[b].shape[2] // 2
        return pl.ds(pl.multiple_of(which * hr, 16), hr)

    def copies(outs, sems, b, i, kk):
        x, y, c = _my_place()
        chip = 2 * x + y
        px, py = _flip(x, (kk >> 1) & 1), _flip(y, kk & 1)
        mine, theirs = (0, kk) if relative else (chip, 2 * px + py)
        landing = kk if relative else chip
        k = nk * b + i
        send = pltpu.make_async_remote_copy(
            src_ref=outs[b].at[mine, :, half(b, c), :], dst_ref=outs[b].at[landing, :, half(b, c), :],
            send_sem=sems[0].at[k], recv_sem=sems[1].at[k], device_id=(px, py, c), device_id_type=MESH)
        got = outs[b].at[theirs, :, half(b, c), :]
        recv = pltpu.make_async_remote_copy(
            src_ref=got, dst_ref=got, send_sem=sems[0].at[k], recv_sem=sems[1].at[k],
            device_id=(px, py, c), device_id_type=MESH)
        fwd = pltpu.make_async_remote_copy(
            src_ref=got, dst_ref=got, send_sem=sems[2].at[k], recv_sem=sems[3].at[k],
            device_id=(x, y, 1 - c), device_id_type=MESH)
        other = outs[b].at[theirs, :, half(b, 1 - c), :]
        back = pltpu.make_async_remote_copy(
            src_ref=other, dst_ref=other, send_sem=sems[2].at[k], recv_sem=sems[3].at[k],
            device_id=(x, y, 1 - c), device_id_type=MESH)
        return send, recv, fwd, back

    def each(outs, sems):
        for b in range(n):
            for i, kk in enumerate(kks):
                yield copies(outs, sems, b, i, kk)

    def start(ins, outs, sems):
        for send, _, _, _ in each(outs, sems):
            send.start()

    def finish(ins, outs, sems):
        for _, recv, fwd, _ in each(outs, sems):
            recv.wait_recv()
            fwd.start()
        for send, _, fwd, back in each(outs, sems):
            back.wait_recv()
            send.wait_send()
            fwd.wait_send()

    return _Exchange(bufs, [_sds(s.shape, s.dtype) for s in bufs], {i: i for i in range(n)},
                     [pltpu.SemaphoreType.DMA((nk * n,))] * 4, start, finish)


def _rs_d2d(grads):
    n = len(grads)

    def copy(ins, outs, sems, b):
        x, y, c = _my_place()
        hr = grads[b].shape[2] // 2
        theirs = pl.ds(pl.multiple_of((1 - c) * hr, 8), hr)
        return pltpu.make_async_remote_copy(
            src_ref=ins[b].at[:, :, theirs, :], dst_ref=outs[b], send_sem=sems[0].at[b], recv_sem=sems[1].at[b],
            device_id=(x, y, 1 - c), device_id_type=MESH)

    def start(ins, outs, sems):
        for b in range(n):
            copy(ins, outs, sems, b).start()

    def finish(ins, outs, sems):
        for b in range(n):
            copy(ins, outs, sems, b).wait()

    return _Exchange(grads, [_sds(g.shape[:2] + (g.shape[2] // 2, g.shape[3]), F32) for g in grads], {},
                     [pltpu.SemaphoreType.DMA((n,))] * 2, start, finish)


def _add_halves(core, g, land):
    nchip, ng, rows, cols = g.shape
    hr = rows // 2
    tr = _row_tile(hr, cols)
    steps = hr // tr

    def body(core_ref, g_ref, l_ref, o_ref):
        del core_ref
        o_ref[...] = (g_ref[...] + l_ref[...]).astype(BF)

    return pl.pallas_call(
        body, name="add_halves",
        grid_spec=pltpu.PrefetchScalarGridSpec(
            num_scalar_prefetch=1, grid=(nchip, ng, steps),
            in_specs=[pl.BlockSpec((None, None, tr, cols), lambda j, a, i, cr: (j, a, cr[0] * steps + i, 0)),
                      pl.BlockSpec((None, None, tr, cols), lambda j, a, i, cr: (j, a, i, 0))],
            out_specs=pl.BlockSpec((None, None, tr, cols), lambda j, a, i, cr: (j, a, i, 0))),
        out_shape=_sds((nchip, ng, hr, cols), BF),
        compiler_params=_cp(3))(core, g, land)


def _rs_ici(parts, relative=False):
    n = len(parts)

    def copies(ins, outs, sems):
        x, y, c = _my_place()
        chip = 2 * x + y
        for b in range(n):
            for kk in range(1, N_CHIPS):
                px, py = _flip(x, (kk >> 1) & 1), _flip(y, kk & 1)
                k = 3 * b + kk - 1
                theirs, landing = (kk, kk) if relative else (2 * px + py, chip)
                send = pltpu.make_async_remote_copy(
                    src_ref=ins[b].at[theirs], dst_ref=outs[b].at[landing],
                    send_sem=sems[0].at[k], recv_sem=sems[1].at[k], device_id=(px, py, c), device_id_type=MESH)
                slot = outs[b].at[theirs]
                recv = pltpu.make_async_remote_copy(
                    src_ref=slot, dst_ref=slot, send_sem=sems[0].at[k], recv_sem=sems[1].at[k],
                    device_id=(px, py, c), device_id_type=MESH)
                yield send, recv

    def start(ins, outs, sems):
        for send, _ in copies(ins, outs, sems):
            send.start()

    def finish(ins, outs, sems):
        for send, recv in copies(ins, outs, sems):
            recv.wait_recv()
            send.wait_send()

    return _Exchange(parts, [_sds(p.shape, p.dtype) for p in parts], {},
                     [pltpu.SemaphoreType.DMA((3 * n,))] * 2, start, finish)


def _sum_chips(place, part, land, relative=False):
    nchip, ng, hr, cols = land.shape
    tr = _row_tile(hr, cols)
    steps = hr // tr

    def body(place_ref, p_ref, l1, l2, l3, o_ref):
        del place_ref
        o_ref[...] = ((p_ref[...].astype(F32) + l1[...].astype(F32)) + l2[...].astype(F32)) + l3[...].astype(F32)

    def slot(k):
        if relative:
            return pl.BlockSpec((None, None, tr, cols), lambda a, i, pr: (k, a, i, 0))
        return pl.BlockSpec((None, None, tr, cols), lambda a, i, pr: (jnp.bitwise_xor(pr[1], k), a, i, 0))

    return pl.pallas_call(
        body, name="sum_chips",
        grid_spec=pltpu.PrefetchScalarGridSpec(
            num_scalar_prefetch=1, grid=(ng, steps),
            in_specs=[slot(0), slot(1), slot(2), slot(3)],
            out_specs=pl.BlockSpec((None, tr, cols), lambda a, i, pr: (a, pr[0] * steps + i, 0))),
        out_shape=_sds((ng, 2 * hr, cols), F32),
        compiler_params=_cp(2))(place, part, land, land, land)


def _rs_final(bufs):
    n = len(bufs)

    def copy(outs, sems, b, which):
        x, y, c = _my_place()
        hr = bufs[b].shape[1] // 2
        rows = outs[b].at[:, pl.ds(pl.multiple_of((c if which == 0 else 1 - c) * hr, 8), hr), :]
        return pltpu.make_async_remote_copy(
            src_ref=rows, dst_ref=rows, send_sem=sems[0].at[b], recv_sem=sems[1].at[b],
            device_id=(x, y, 1 - c), device_id_type=MESH)

    def start(ins, outs, sems):
        for b in range(n):
            copy(outs, sems, b, 0).start()

    def finish(ins, outs, sems):
        for b in range(n):
            copy(outs, sems, b, 0).wait_send()
            copy(outs, sems, b, 1).wait_recv()

    return _Exchange(bufs, [_sds(h.shape, F32) for h in bufs], {i: i for i in range(n)},
                     [pltpu.SemaphoreType.DMA((n,))] * 2, start, finish)


def _small_sync(smalls, dmod_blk, c_all, carry=None):
    d = c_all.shape[-1]
    cols = dmod_blk.shape[-1]
    chunk = 384

    def body(sm_ref, dm_ref, c_ref, sum_ref, gw_ref, sm_all, dm_all, ssem, rsem):
        x, y, c = _my_place()
        dev = 4 * x + 2 * y + c
        chip = 2 * x + y
        sm_all[dev] = sm_ref[...]
        dm_all[dev] = dm_ref[chip]
        sends = []
        for k in range(1, N_DEV):
            px, py, pc = _flip(x, (k >> 2) & 1), _flip(y, (k >> 1) & 1), _flip(c, k & 1)
            a = pltpu.make_async_remote_copy(src_ref=sm_ref, dst_ref=sm_all.at[dev], send_sem=ssem.at[2 * (k - 1)],
                                             recv_sem=rsem.at[2 * (k - 1)], device_id=(px, py, pc),
                                             device_id_type=MESH)
            b = pltpu.make_async_remote_copy(src_ref=dm_ref.at[2 * px + py], dst_ref=dm_all.at[dev],
                                             send_sem=ssem.at[2 * (k - 1) + 1], recv_sem=rsem.at[2 * (k - 1) + 1],
                                             device_id=(px, py, pc), device_id_type=MESH)
            a.start()
            b.start()
            sends += [a, b]
        for k in range(1, N_DEV):
            px, py, pc = _flip(x, (k >> 2) & 1), _flip(y, (k >> 1) & 1), _flip(c, k & 1)
            pdev = 4 * px + 2 * py + pc
            pltpu.make_async_remote_copy(src_ref=sm_ref, dst_ref=sm_all.at[pdev], send_sem=ssem.at[2 * (k - 1)],
                                         recv_sem=rsem.at[2 * (k - 1)], device_id=(px, py, pc),
                                         device_id_type=MESH).wait_recv()
            pltpu.make_async_remote_copy(src_ref=dm_ref.at[chip], dst_ref=dm_all.at[pdev],
                                         send_sem=ssem.at[2 * (k - 1) + 1], recv_sem=rsem.at[2 * (k - 1) + 1],
                                         device_id=(px, py, pc), device_id_type=MESH).wait_recv()
        for cp in sends:
            cp.wait_send()

        tot = sm_all[0]
        for q in range(1, N_DEV):
            tot = tot + sm_all[q]
        sum_ref[...] = tot

        cs = c_ref[...].reshape(N_DEV * 8, d)
        sc = (cs * jax.nn.sigmoid(cs)).astype(BF)
        for n0 in range(0, cols, chunk):
            dmv = dm_all[:, :, n0:n0 + chunk].reshape(N_DEV * 8, chunk).astype(BF)
            gw_ref[:, n0:n0 + chunk] = _tn(sc, dmv)

    return _whole_call(
        body, "small_sync", (smalls, dmod_blk, c_all),
        [_sds(smalls.shape, F32), _sds((d, cols), F32)],
        [pltpu.VMEM((N_DEV,) + smalls.shape, F32), pltpu.VMEM((N_DEV, 8, cols), F32),
         pltpu.SemaphoreType.DMA((2 * (N_DEV - 1),)), pltpu.SemaphoreType.DMA((2 * (N_DEV - 1),))], carry=carry)


def _bucket_onehot():
    maps = np.stack([_bucket_map(dil).reshape(-1) for _, dil in DIL_CONFIGS])
    return (jnp.asarray(maps)[:, None, :] == jnp.arange(N_BUCKETS, dtype=jnp.int32)[None, :, None]).astype(BF)


def _dil_bias(rel_t, onehot):
    def body(r_ref, oh_ref, o_ref):
        rv = r_ref[...]
        hi = rv.astype(BF)
        lo = (rv - hi.astype(F32)).astype(BF)
        for c in range(len(DIL_CONFIGS)):
            o_ref[c] = _nn(hi, oh_ref[c]) + _nn(lo, oh_ref[c])

    return pl.pallas_call(body, name="dil_bias",
                          out_shape=_sds((len(DIL_CONFIGS), N_HEADS, BLOCK * 2 * BLOCK), F32),
                          compiler_params=_cp())(rel_t, onehot)


def _rowsum8(a):
    def body(a_ref, o_ref):
        o_ref[...] = jnp.sum(a_ref[...], axis=0, keepdims=True)

    return pl.pallas_call(body, name="rowsum8", out_shape=_sds((1, a.shape[1]), F32), compiler_params=_cp())(a)


def _local_step(x, mod, target, w, gains, rel_bias, place=None):
    nb, seq, d = x.shape
    t = nb * seq
    dist = place is not None
    core = place[0:1] if dist else None
    x0 = x.reshape(t, d)
    tgt = target.reshape(t, d)
    md = [mod[:, i:i + 1, :] for i in range(N_MOD)]
    sh1, sc1, gt1, sh2, sc2, gt2, sh3, sc3, gt3 = md
    g1, g2, g3 = gains["g_ffn1"], gains["g_mix"], gains["g_ffn2"]
    ones_g = _group_ones()

    def partial_sums(grads, lands):
        return [_add_halves(core, g, l) for g, l in zip(grads, lands)]

    def chip_sums(parts, lands):
        return [_sum_chips(place, p, l, relative=True) for p, l in zip(parts, lands)]

    gu1 = w["gu1"]
    res = _ffn_up(x0, g1, sc1, sh1, gu1, seq,
                  carry=_join([_ag_weights([w["d1"]], relative=True), _ag_weights([w["win"]])]) if dist else None)
    h1, a1, u1, s1 = res[:4]
    wd1, w_in = res[4:] if dist else (w["d1"], w["win"])
    f1, x1 = _ffn_down(s1, wd1, x0, gt1, seq, 0.5)

    h2, qkv6, qkv_r4, qkv_r16 = _qkv_proj(x1, g2, sc2, sh2, w_in, seq)
    qkv6b = qkv6.reshape(6, nb, seq, D_GRP)
    res = _sb_fwd(qkv6b, gains["g_sb_out"], nb, seq,
                  carry=_join([_ag_weights([w["gu2"], w["d2"]], relative=True), _ag_weights([w["wout"]])])
                  if dist else None)
    o_sb, on_sb = res[:2]
    wgu2, wd2, w_out = res[2:] if dist else (w["gu2"], w["d2"], w["wout"])
    w_out2 = w_out.reshape(2 * D_GRP, d)
    onehot = _bucket_onehot()
    bias = _dil_bias(rel_bias.T, onehot).reshape(len(DIL_CONFIGS), N_HEADS * BLOCK, 2 * BLOCK)
    o_cs, l_cs = [], []
    qkv_rs = [(qkv6b, 3), (qkv_r4, 0), (qkv_r16, 0)]
    for ci, (_, dil) in enumerate(DIL_CONFIGS):
        sub = seq // dil
        arr, base = qkv_rs[ci]
        arr = arr.reshape(base + 3, nb, sub, dil * D_GRP)
        qkv_rs[ci] = (arr, base)
        o_c, l_c = _dil_fwd(arr, base, bias[ci], nb, sub, dil)
        o_cs.append(o_c.reshape(t // dil, dil * D_GRP))
        l_cs.append(l_c.reshape(t // dil, dil * D_GRP))
    o_dil, on_dil = _dil_comb(o_cs, l_cs, gains["g_dil_out"])
    tmix, x2 = _mix_out(on_sb.reshape(t, D_GRP), on_dil, w_out2, x1, gt2, seq)

    h3, a3, u3, s3 = _ffn_up(x2, g3, sc3, sh3, wgu2, seq)
    f3, dx3, dg_final, loss = _ffn_down_loss(s3, wd2, x2, gt3, seq, 0.5, gains["g_final"], tgt)

    da3, du3, df3, dgt3, dx2, dsh3, dsc3, dg3 = _ffn_bwd_x(dx3, gt3, f3, wd2, a3, u3, wgu2, x2, g3, sc3, seq, 0.5)
    grads2 = [_ffn_bwd_w(h3, da3, du3, s3, df3)]

    res = _mix_bwd_out(
        dx2, gt2, tmix, w_out2, o_sb.reshape(t, D_GRP), o_dil, on_sb.reshape(t, D_GRP), on_dil,
        gains["g_sb_out"], gains["g_dil_out"], ones_g, seq, carry=_rs_d2d(grads2) if dist else None)
    do_sb, do_dil, dgt2, dg_sb, dg_dil, dw_out = res[:6]
    parts2 = partial_sums(grads2, res[6:]) if dist else None
    dw_out = dw_out.reshape(N_CHIPS, 1, 2 * D_GRP // N_CHIPS, d)
    res = _sb_bwd(qkv6b, do_sb.reshape(nb, seq, D_GRP), nb, seq,
                  carry=_rs_ici(parts2, relative=True) if dist else None)
    dqkv6 = res[0]
    halves2 = chip_sums(parts2, res[1:]) if dist else None
    dcs = _dil_comb_bwd(do_dil, o_cs, l_cs)
    dsum, a_tiles = [], []
    for ci, (_, dil) in enumerate(DIL_CONFIGS):
        sub = seq // dil
        do_c = dcs[ci].reshape(nb, sub, dil * D_GRP)
        dd_c = dcs[3 + ci].reshape(nb, sub, dil * D_GRP)
        res = _dil_bwd(qkv_rs[ci][0], qkv_rs[ci][1], bias[ci], do_c, dd_c, nb, sub, dil)
        dsum.append(res[0].reshape(3, t // dil, dil * D_GRP))
        a_tiles.append(res[1].reshape(N_HEADS, BLOCK * 2 * BLOCK))
    dqkv6 = _dqkv_dil_sum(dsum, dqkv6.reshape(6, t, D_GRP))
    drel = _relbias_grad(jnp.stack(a_tiles), onehot)
    dx1, dsh2, dsc2, dg2 = _mix_bwd_dh(dqkv6, w_in, x1, g2, sc2, dx2, seq)

    da1, du1, df1, dgt1 = _ffn_bwd_ds(dx1, gt1, f1, wd1, a1, u1, seq, 0.5)
    grads1 = [_ffn_bwd_w(h1, da1, du1, s1, df1)]
    res = _dw_in(h2, dqkv6, carry=_join([_rs_d2d(grads1), _rs_final(halves2)]) if dist else None)
    grads_m = [res[0], dw_out]
    parts1 = partial_sums(grads1, res[1:2]) if dist else None
    if dist:
        grads2 = res[2:3]
    res = _ffn_bwd_dh(da1, du1, gu1, x0, g1, sc1, dx1, seq,
                      carry=_join([_rs_ici(parts1, relative=True), _rs_d2d(grads_m)]) if dist else None)
    dx0, dsh1, dsc1, dg1 = res[:4]
    pending = None
    if dist:
        pending = (chip_sums(parts1, res[4:5]), partial_sums(grads_m, res[5:7]))

    dmod = jnp.concatenate([dsh1, dsc1, dgt1, dsh2, dsc2, dgt2, dsh3, dsc3, dgt3], axis=1)
    return dict(grad_x=dx0.reshape(nb, seq, d), loss=loss[0, 0], dmod=dmod.reshape(nb, N_MOD * d),
                dffn1=grads1[0], dffn2=grads2[0], dwin=grads_m[0], dwout=grads_m[1], pending=pending,
                dg_ffn1=dg1, dg_mix=dg2, dg_ffn2=dg3, dg_final=dg_final, dg_sb=dg_sb, dg_dil=dg_dil,
                drel=drel.T)


_SMALL_ORDER = (("b_ada", N_MOD * 1024), ("g_ffn1", 1024), ("g_mix", 1024), ("g_ffn2", 1024), ("g_final", 1024),
                ("g_sb_out", D_GRP), ("g_dil_out", D_GRP), ("rel_bias", N_BUCKETS * N_HEADS))


def _pack_small(parts, extra=None):
    flat = [parts[name].reshape(-1).astype(F32) for name, _ in _SMALL_ORDER]
    used = sum(sz for _, sz in _SMALL_ORDER)
    pad = SMALL_ROWS * 128 - used
    tail = jnp.zeros((pad,), F32)
    if extra is not None:
        tail = tail.at[0].set(extra)
    return jnp.concatenate(flat + [tail]).reshape(SMALL_ROWS, 128)


def _unpack_small(packed, shapes):
    flat = packed.reshape(-1)
    out, off = {}, 0
    for name, sz in _SMALL_ORDER:
        out[name] = flat[off:off + sz].reshape(shapes[name])
        off += sz
    return out, flat[off]


def kernel(x, c, w_ada, b_ada, g_ffn1, w1_gate, w1_up, w1_down, g_mix, w_in, g_sb_out, g_dil_out, w_out, rel_bias, g_ffn2, w2_gate, w2_up, w2_down, g_final, loss_target, m_w_ada, m_b_ada, m_g_ffn1, m_w1_gate, m_w1_up, m_w1_down, m_g_mix, m_w_in, m_g_sb_out, m_g_dil_out, m_w_out, m_rel_bias, m_g_ffn2, m_w2_gate, m_w2_up, m_w2_down, m_g_final, v_w_ada, v_b_ada, v_g_ffn1, v_w1_gate, v_w1_up, v_w1_down, v_g_mix, v_w_in, v_g_sb_out, v_g_dil_out, v_w_out, v_rel_bias, v_g_ffn2, v_w2_gate, v_w2_up, v_w2_down, v_g_final):
    nb, seq, d = x.shape
    xi, yi, ci = lax.axis_index("x"), lax.axis_index("y"), lax.axis_index("c")
    chip = 2 * xi + yi
    ada_cols = w_ada.shape[-1]

    c_pad = jnp.zeros((8, d), F32).at[:nb].set(c)
    b_shard = lax.dynamic_slice(b_ada, (0, chip * ada_cols), (1, ada_cols))
    shards = dict(gu1=jnp.stack([w1_gate[0], w1_up[0]]), d1=w1_down, win=w_in, wout=w_out,
                  gu2=jnp.stack([w2_gate[0], w2_up[0]]), d2=w2_down)
    bufs = {k: lax.dynamic_update_slice(lax.empty((N_CHIPS,) + s.shape, BF), s.astype(BF)[None],
                                        (chip if k in ("win", "wout") else 0, 0, 0, 0))
            for k, s in shards.items()}
    c_all, mod_blk, bufs["gu1"] = _ada_fwd(c_pad, w_ada[0], b_shard,
                                           carry=_ag_weights([bufs["gu1"]], relative=True))
    mod = jnp.transpose(mod_blk[:, :nb, :], (1, 0, 2)).reshape(nb, N_MOD, d)

    gains = dict(g_ffn1=g_ffn1, g_mix=g_mix, g_ffn2=g_ffn2, g_final=g_final.reshape(1, d),
                 g_sb_out=g_sb_out.reshape(1, D_GRP), g_dil_out=g_dil_out.reshape(1, D_GRP))
    place = jnp.stack([ci, chip]).astype(jnp.int32)
    r = _local_step(x, mod, loss_target, bufs, gains, rel_bias, place)

    dmod = r["dmod"]
    dmod_pad = jnp.zeros((8, N_MOD * d), F32).at[:nb].set(dmod)
    dmod_blk = jnp.transpose(dmod_pad.reshape(8, N_CHIPS, ada_cols), (1, 0, 2))
    small_parts = dict(b_ada=_rowsum8(dmod_pad), g_ffn1=r["dg_ffn1"], g_mix=r["dg_mix"], g_ffn2=r["dg_ffn2"],
                       g_final=r["dg_final"], g_sb_out=r["dg_sb"], g_dil_out=r["dg_dil"], rel_bias=r["drel"])
    halves1, parts_m = r["pending"]
    res = _small_sync(_pack_small(small_parts, r["loss"]), dmod_blk, c_all,
                      carry=_join([_rs_final(halves1), _rs_ici(parts_m)]))
    small_sum, g_wada, gffn1 = res[:3]
    halves_m = [_sum_chips(place, p, l) for p, l in zip(parts_m, res[3:5])]
    gwin, gwout = _alone("rs_last", _rs_final(halves_m))
    gffn2 = r["dffn2"]

    small_w = dict(b_ada=b_ada, g_ffn1=g_ffn1, g_mix=g_mix, g_ffn2=g_ffn2, g_final=g_final,
                   g_sb_out=g_sb_out, g_dil_out=g_dil_out, rel_bias=rel_bias)
    small_m = dict(b_ada=m_b_ada, g_ffn1=m_g_ffn1, g_mix=m_g_mix, g_ffn2=m_g_ffn2, g_final=m_g_final,
                   g_sb_out=m_g_sb_out, g_dil_out=m_g_dil_out, rel_bias=m_rel_bias)
    small_v = dict(b_ada=v_b_ada, g_ffn1=v_g_ffn1, g_mix=v_g_mix, g_ffn2=v_g_ffn2, g_final=v_g_final,
                   g_sb_out=v_g_sb_out, g_dil_out=v_g_dil_out, rel_bias=v_rel_bias)
    shapes = {k: v.shape for k, v in small_w.items()}
    sg, sd, sm, sv = _adamw(_pack_small(small_w), small_sum.reshape(1, SMALL_ROWS, 128), 0,
                            _pack_small(small_m), _pack_small(small_v))
    sg, loss = _unpack_small(sg, shapes)
    sd, _ = _unpack_small(sd, shapes)
    sm, _ = _unpack_small(sm, shapes)
    sv, _ = _unpack_small(sv, shapes)

    big = {}

    def upd(name, w, g_arr, sel, m, v, transposed=False):
        swap = (lambda a: jnp.swapaxes(a, -1, -2)) if transposed else (lambda a: a)
        w2, m2, v2 = [swap(a)[0] for a in (w, m, v)]
        big[name] = [swap(a[None]) for a in _adamw(w2, g_arr, sel, m2, v2)]

    upd("w_ada", w_ada, g_wada.reshape(1, d, ada_cols), 0, m_w_ada, v_w_ada)
    upd("w1_gate", w1_gate, gffn1, 0, m_w1_gate, v_w1_gate, transposed=True)
    upd("w1_up", w1_up, gffn1, 1, m_w1_up, v_w1_up, transposed=True)
    upd("w1_down", w1_down, gffn1, 2, m_w1_down, v_w1_down)
    upd("w_in", w_in, gwin, 0, m_w_in, v_w_in)
    upd("w_out", w_out, gwout, 0, m_w_out, v_w_out)
    upd("w2_gate", w2_gate, gffn2, 0, m_w2_gate, v_w2_gate, transposed=True)
    upd("w2_up", w2_up, gffn2, 1, m_w2_up, v_w2_up, transposed=True)
    upd("w2_down", w2_down, gffn2, 2, m_w2_down, v_w2_down)

    names = ["w_ada", "b_ada", "g_ffn1", "w1_gate", "w1_up", "w1_down", "g_mix", "w_in", "g_sb_out", "g_dil_out",
             "w_out", "rel_bias", "g_ffn2", "w2_gate", "w2_up", "w2_down", "g_final"]
    outs = [loss, r["grad_x"]]
    for k, small in enumerate((sg, sd, sm, sv)):
        for name in names:
            outs.append(big[name][k] if name in big else small[name])
    return tuple(outs)
```

```python
import math

import numpy as np
import jax
import jax.numpy as jnp
from jax import lax
from jax.experimental import pallas as pl
from jax.experimental.pallas import tpu as pltpu

F32 = jnp.float32
BF = jnp.bfloat16
MESH = pl.DeviceIdType.MESH

HEAD_DIM = 64
N_HEADS = 8
D_GRP = N_HEADS * HEAD_DIM
DIL_CONFIGS = ((128, 1), (512, 4), (2048, 16))
N_STEPS = 128
BLOCK = 128
N_BUCKETS = 32
MAX_DISTANCE = 2048
N_MOD = 9
EPS = 1e-6
NEG_INF = -1e30
SCALE = HEAD_DIM ** -0.5

ADAM_LR = 0.001
ADAM_B1 = 0.9
ADAM_B2 = 0.999
ADAM_EPS = 1e-08
ADAM_WD = 0.01
ADAM_STEP = 10

N_CHIPS = 4
N_DEV = 8
VMEM_LIMIT = 56 * 1024 * 1024
TM = 512
TQ = 256
KB = 256
SMALL_ROWS = 120


def _cp(n_axes=0, **kw):
    sem = ("arbitrary",) * n_axes if n_axes else None
    return pltpu.CompilerParams(dimension_semantics=sem, vmem_limit_bytes=VMEM_LIMIT, **kw)


def _nn(a, b):
    return jnp.dot(a, b, preferred_element_type=F32)


def _nt(a, b):
    return lax.dot_general(a, b, (((1,), (1,)), ((), ())), preferred_element_type=F32)


def _tn(a, b):
    return lax.dot_general(a, b, (((0,), (0,)), ((), ())), preferred_element_type=F32)


def _nn2(x, m):
    hi = x.astype(BF)
    lo = (x - hi.astype(F32)).astype(BF)
    r = _nn(jnp.concatenate([hi, lo], axis=0), m)
    return r[:x.shape[0]] + r[x.shape[0]:]


def _softplus(z):
    return jnp.maximum(z, 0.0) + jnp.log(1.0 + jnp.exp(-jnp.abs(z)))


def _sds(shape, dtype):
    return jax.ShapeDtypeStruct(shape, dtype)


def _whole(a):
    nd = a.ndim
    return pl.BlockSpec(a.shape, lambda *_: (0,) * nd, pipeline_mode=pl.Buffered(1))


def _modnorm_bwd_tile(dh, xv, gv, scv, dxo):
    r = lax.rsqrt(jnp.mean(xv * xv, axis=-1, keepdims=True) + EPS)
    n = xv * r
    ng = n * gv
    dsh = jnp.sum(dh, axis=0, keepdims=True)
    dsc = jnp.sum(dh * ng, axis=0, keepdims=True)
    dy = dh * (1.0 + scv)
    dg = jnp.sum(dy * n, axis=0, keepdims=True)
    dn = dy * gv
    dx = dxo + r * (dn - n * jnp.mean(dn * n, axis=-1, keepdims=True))
    return dx, dsh, dsc, dg


def _acc_rows(ref, val, first):
    @pl.when(first)
    def _():
        ref[...] = val

    @pl.when(jnp.logical_not(first))
    def _():
        ref[...] += val


def _modnorm_tile(x_ref, g_ref, sc_ref, sh_ref):
    xv = x_ref[...]
    r = lax.rsqrt(jnp.mean(xv * xv, axis=-1, keepdims=True) + EPS)
    return (((xv * r) * g_ref[...]) * (1.0 + sc_ref[...]) + sh_ref[...]).astype(BF)


def _ffn_up(x, g, sc, sh, wgu, seq, carry=None):
    t, d = x.shape
    fs = wgu.shape[-1]
    per = seq // TM

    def body(x_ref, g_ref, sc_ref, sh_ref, w_ref, h_ref, p_ref, q_ref, s_ref):
        hv = _modnorm_tile(x_ref, g_ref, sc_ref, sh_ref)
        h_ref[...] = hv
        for j in range(N_CHIPS):
            a = _nn(hv, w_ref[j, 0])
            u = _nn(hv, w_ref[j, 1])
            sig = jax.nn.sigmoid(a)
            q = a * sig
            p_ref[j] = (u * (sig * (1.0 + a * (1.0 - sig)))).astype(BF)
            q_ref[j] = q.astype(BF)
            s_ref[j] = (q * u).astype(BF)

    row = pl.BlockSpec((TM, d), lambda m: (m, 0))
    ex = pl.BlockSpec((None, 1, d), lambda m: (m // per, 0, 0))
    blk = pl.BlockSpec((N_CHIPS, TM, fs), lambda m: (0, m, 0))
    return _call(
        body, "ffn_up", (t // TM,),
        [row, pl.BlockSpec((1, d), lambda m: (0, 0)), ex, ex, _whole(wgu)],
        [row, blk, blk, blk],
        [_sds((t, d), BF)] + [_sds((N_CHIPS, t, fs), BF)] * 3,
        (x, g, sc, sh, wgu), carry=carry)


def _ffn_down(s, wd, x, gt, seq, coef, carry=None):
    _, t, fs = s.shape
    d = x.shape[-1]
    per = seq // TM

    def body(s_ref, w_ref, x_ref, gt_ref, f_ref, xo_ref):
        f = _nn(s_ref[0], w_ref[0, 0])
        for j in range(1, N_CHIPS):
            f = f + _nn(s_ref[j], w_ref[j, 0])
        f_ref[...] = f.astype(BF)
        xo_ref[...] = x_ref[...] + (coef * gt_ref[...]) * f

    row = pl.BlockSpec((TM, d), lambda m: (m, 0))
    return _call(
        body, "ffn_down", (t // TM,),
        [pl.BlockSpec((N_CHIPS, TM, fs), lambda m: (0, m, 0)), _whole(wd), row,
         pl.BlockSpec((None, 1, d), lambda m: (m // per, 0, 0))],
        [row, row], [_sds((t, d), BF), _sds((t, d), F32)], (s, wd, x, gt), carry=carry)


def _ffn_bwd_ds(dxo, gt, f, wd, p, q, seq, coef, carry=None):
    t, d = dxo.shape
    fs = p.shape[-1]
    per = seq // TM
    nb = t // seq

    def body(dxo_ref, gt_ref, f_ref, w_ref, p_ref, q_ref, da_ref, du_ref, df_ref, dgt_ref):
        m = pl.program_id(0)
        dxv = dxo_ref[...]
        df = ((coef * gt_ref[...]) * dxv).astype(BF)
        df_ref[...] = df
        _acc_rows(dgt_ref, coef * jnp.sum(dxv * f_ref[...].astype(F32), axis=0, keepdims=True), m % per == 0)
        for j in range(N_CHIPS):
            ds = _nt(df, w_ref[j, 0])
            da_ref[j] = (ds * p_ref[j].astype(F32)).astype(BF)
            du_ref[j] = (ds * q_ref[j].astype(F32)).astype(BF)

    row = pl.BlockSpec((TM, d), lambda m: (m, 0))
    blk = pl.BlockSpec((N_CHIPS, TM, fs), lambda m: (0, m, 0))
    ex = pl.BlockSpec((None, 1, d), lambda m: (m // per, 0, 0))
    return _call(
        body, "ffn_bwd_ds", (t // TM,),
        [row, ex, row, _whole(wd), blk, blk],
        [blk, blk, row, ex],
        [_sds((N_CHIPS, t, fs), BF), _sds((N_CHIPS, t, fs), BF), _sds((t, d), BF), _sds((nb, 1, d), F32)],
        (dxo, gt, f, wd, p, q), carry=carry)


TM_X = 256


def _ffn_bwd_x(dxo, gt, f, wd, p, q, wgu, x, g, sc, seq, coef):
    t, d = dxo.shape
    fs = p.shape[-1]
    per = seq // TM_X
    nb = t // seq

    def body(dxo_ref, gt_ref, f_ref, wd_ref, p_ref, q_ref, w_ref, x_ref, g_ref, sc_ref,
             da_ref, du_ref, df_ref, dgt_ref, dx_ref, dsh_ref, dsc_ref, dg_ref):
        m = pl.program_id(0)
        dxv = dxo_ref[...]
        df = ((coef * gt_ref[...]) * dxv).astype(BF)
        df_ref[...] = df
        _acc_rows(dgt_ref, coef * jnp.sum(dxv * f_ref[...].astype(F32), axis=0, keepdims=True), m % per == 0)
        dh = None
        for j in range(N_CHIPS):
            ds = _nt(df, wd_ref[j, 0])
            da = (ds * p_ref[j].astype(F32)).astype(BF)
            du = (ds * q_ref[j].astype(F32)).astype(BF)
            da_ref[j] = da
            du_ref[j] = du
            part = _nt(da, w_ref[j, 0]) + _nt(du, w_ref[j, 1])
            dh = part if dh is None else dh + part
        dx, dsh, dsc, dg = _modnorm_bwd_tile(dh, x_ref[...], g_ref[...], sc_ref[...], dxv)
        dx_ref[...] = dx
        _acc_rows(dsh_ref, dsh, m % per == 0)
        _acc_rows(dsc_ref, dsc, m % per == 0)
        _acc_rows(dg_ref, dg, m == 0)

    row = pl.BlockSpec((TM_X, d), lambda m: (m, 0))
    blk = pl.BlockSpec((N_CHIPS, TM_X, fs), lambda m: (0, m, 0))
    ex = pl.BlockSpec((None, 1, d), lambda m: (m // per, 0, 0))
    vec = pl.BlockSpec((1, d), lambda m: (0, 0))
    exs = _sds((nb, 1, d), F32)
    return pl.pallas_call(
        body, name="ffn_bwd_x", grid=(t // TM_X,),
        in_specs=[row, ex, row, _whole(wd), blk, blk, _whole(wgu), row, vec, ex],
        out_specs=[blk, blk, row, ex, row, ex, ex, vec],
        out_shape=[_sds((N_CHIPS, t, fs), BF), _sds((N_CHIPS, t, fs), BF), _sds((t, d), BF), exs,
                   _sds((t, d), F32), exs, exs, _sds((1, d), F32)],
        compiler_params=_cp(1))(dxo, gt, f, wd, p, q, wgu, x, g, sc)


TK_W = 1024


def _ffn_bwd_w(h, da, du, s, df):
    t, d = h.shape
    fs = da.shape[-1]

    def body(h_ref, da_ref, du_ref, s_ref, df_ref, o_ref):
        kt = pl.program_id(1)
        hv = h_ref[...]
        parts = (_tn(da_ref[...], hv), _tn(du_ref[...], hv), _tn(s_ref[...], df_ref[...]))

        @pl.when(kt == 0)
        def _():
            for i, p in enumerate(parts):
                o_ref[i] = p

        @pl.when(kt != 0)
        def _():
            for i, p in enumerate(parts):
                o_ref[i] += p

    row = pl.BlockSpec((TK_W, d), lambda j, kt: (kt, 0))
    blk = pl.BlockSpec((None, TK_W, fs), lambda j, kt: (j, kt, 0))
    return pl.pallas_call(
        body, name="ffn_bwd_w", grid=(N_CHIPS, t // TK_W),
        in_specs=[row, blk, blk, blk, row],
        out_specs=pl.BlockSpec((None, 3, fs, d), lambda j, kt: (j, 0, 0, 0)),
        out_shape=_sds((N_CHIPS, 3, fs, d), F32),
        compiler_params=_cp(2))(h, da, du, s, df)


def _ffn_bwd_dh(da, du, wgu, x, g, sc, dxo, seq, carry=None):
    _, t, fs = da.shape
    d = x.shape[-1]
    per = seq // TM
    nb = t // seq

    def body(da_ref, du_ref, w_ref, x_ref, g_ref, sc_ref, dxo_ref, dx_ref, dsh_ref, dsc_ref, dg_ref):
        m = pl.program_id(0)
        dh = _nt(da_ref[0], w_ref[0, 0]) + _nt(du_ref[0], w_ref[0, 1])
        for j in range(1, N_CHIPS):
            dh = dh + _nt(da_ref[j], w_ref[j, 0]) + _nt(du_ref[j], w_ref[j, 1])
        dx, dsh, dsc, dg = _modnorm_bwd_tile(dh, x_ref[...], g_ref[...], sc_ref[...], dxo_ref[...])
        dx_ref[...] = dx
        _acc_rows(dsh_ref, dsh, m % per == 0)
        _acc_rows(dsc_ref, dsc, m % per == 0)
        _acc_rows(dg_ref, dg, m == 0)

    row = pl.BlockSpec((TM, d), lambda m: (m, 0))
    blk = pl.BlockSpec((N_CHIPS, TM, fs), lambda m: (0, m, 0))
    ex = pl.BlockSpec((None, 1, d), lambda m: (m // per, 0, 0))
    vec = pl.BlockSpec((1, d), lambda m: (0, 0))
    return _call(
        body, "ffn_bwd_dh", (t // TM,),
        [blk, blk, _whole(wgu), row, vec, ex, row],
        [row, ex, ex, vec],
        [_sds((t, d), F32), _sds((nb, 1, d), F32), _sds((nb, 1, d), F32), _sds((1, d), F32)],
        (da, du, wgu, x, g, sc, dxo), carry=carry)


def _qkv_proj(x, g, sc, sh, w_in, seq, carry=None):
    t, d = x.shape
    wc = w_in.shape[-1]
    per = seq // TM

    dils = [dil for _, dil in DIL_CONFIGS if dil > 1]

    def body(x_ref, g_ref, sc_ref, sh_ref, w_ref, h_ref, o_ref, *rest):
        res_refs, buf = rest[:len(dils)], rest[len(dils)]
        hv = _modnorm_tile(x_ref, g_ref, sc_ref, sh_ref)
        h_ref[...] = hv
        for j in range(N_CHIPS):
            rf = _nn(hv, w_ref[j, 0])
            r = rf.astype(BF)
            for a, lc, off, width in _col_pieces(j, wc):
                o_ref[a, :, lc:lc + width] = r[:, off:off + width]
                if a < 3:
                    continue
                for c0 in range(0, width, 128):
                    cg = (lc + c0) // 128
                    buf[...] = rf[:, off + c0:off + c0 + 128]
                    for ref, dil in zip(res_refs, dils):
                        for rr in range(dil):
                            ref[a - 3, :, rr * D_GRP + cg * 128:rr * D_GRP + (cg + 1) * 128] = (
                                buf[pl.ds(rr, TM // dil, stride=dil), :].astype(BF))

    row = pl.BlockSpec((TM, d), lambda m: (m, 0))
    ex = pl.BlockSpec((None, 1, d), lambda m: (m // per, 0, 0))
    return _call(
        body, "qkv_proj", (t // TM,),
        [row, pl.BlockSpec((1, d), lambda m: (0, 0)), ex, ex, _whole(w_in)],
        [row, pl.BlockSpec((6, TM, D_GRP), lambda m: (0, m, 0))]
        + [pl.BlockSpec((3, TM // dil, dil * D_GRP), lambda m: (0, m, 0)) for dil in dils],
        [_sds((t, d), BF), _sds((6, t, D_GRP), BF)] + [_sds((3, t // dil, dil * D_GRP), BF) for dil in dils],
        (x, g, sc, sh, w_in), scratch=[pltpu.VMEM((TM, 128), F32)], carry=carry)


def _col_pieces(j, wc):
    out, off = [], 0
    while off < wc:
        a, lc = divmod(j * wc + off, D_GRP)
        width = min(D_GRP - lc, wc - off)
        out.append((a, lc, off, width))
        off += width
    return out


def _chip_cols(g6_ref, j, wc):
    return jnp.concatenate([g6_ref[a, :, lc:lc + width] for a, lc, _, width in _col_pieces(j, wc)], axis=1)


def _mix_out(on_sb, on_dil, w_out, x, gt, seq):
    t, d = x.shape
    per = seq // TM

    def body(a_ref, b_ref, w_ref, x_ref, gt_ref, t_ref, xo_ref):
        tv = _nn(a_ref[...], w_ref[0:D_GRP, :]) + _nn(b_ref[...], w_ref[D_GRP:2 * D_GRP, :])
        t_ref[...] = tv.astype(BF)
        xo_ref[...] = x_ref[...] + gt_ref[...] * tv

    row = pl.BlockSpec((TM, d), lambda m: (m, 0))
    half = pl.BlockSpec((TM, D_GRP), lambda m: (m, 0))
    return pl.pallas_call(
        body, name="mix_out", grid=(t // TM,),
        in_specs=[half, half, pl.BlockSpec((2 * D_GRP, d), lambda m: (0, 0)), row,
                  pl.BlockSpec((None, 1, d), lambda m: (m // per, 0, 0))],
        out_specs=[row, row],
        out_shape=[_sds((t, d), BF), _sds((t, d), F32)],
        compiler_params=_cp(1))(on_sb, on_dil, w_out, x, gt)


def _sb_masks():
    lane = lax.broadcasted_iota(jnp.int32, (1, 2 * HEAD_DIM), 1)
    hm0 = lane < HEAD_DIM
    rel = lax.broadcasted_iota(jnp.int32, (TQ, KB), 0) - lax.broadcasted_iota(jnp.int32, (TQ, KB), 1)
    kr = lax.broadcasted_iota(jnp.int32, (KB, KB), 0)
    kc = lax.broadcasted_iota(jnp.int32, (KB, KB), 1)
    return hm0, rel, kr, kc


def _headnorm_pair(o, gv, hm0):
    o2 = o * o
    ms0 = jnp.sum(jnp.where(hm0, o2, 0.0), axis=-1, keepdims=True) * (1.0 / HEAD_DIM)
    ms1 = jnp.sum(jnp.where(hm0, 0.0, o2), axis=-1, keepdims=True) * (1.0 / HEAD_DIM)
    r = jnp.where(hm0, lax.rsqrt(ms0 + EPS), lax.rsqrt(ms1 + EPS))
    return (o * r) * gv


SB_DEAD = -104.0


def _alive(c_l):
    return (jnp.max(c_l) > SB_DEAD).astype(jnp.int32)


def _sb_fwd(qkv6, g_sb, nb, seq, carry=None):
    nq = seq // TQ

    def body(q_ref, k_ref, v_ref, g_ref, o_ref, on_ref):
        qi = pl.program_id(2)
        hm0, rel, kr, kc = _sb_masks()
        upper = (kr > kc).astype(BF)
        heads = _dil_masks()[0]
        qs = _stack_heads(q_ref[...], heads)
        causal2 = jnp.concatenate([rel] * GRP_HEADS, axis=0) > 0

        def block(kj, causal, c_l, acc):
            ks = pl.multiple_of(kj * KB, KB)
            z = _nt(qs, k_ref[pl.ds(ks, KB), :]) * SCALE
            sp = _softplus(z)
            ln = -sp if causal is None else jnp.where(causal, -sp, 0.0)
            suf = _nn2(ln, upper)
            w = jnp.exp((z - sp) + (suf + c_l))
            if causal is not None:
                w = jnp.where(causal, w, 0.0)
            return c_l + (suf[:, 0:1] + ln[:, 0:1]), acc + _nn(w.astype(BF), v_ref[pl.ds(ks, KB), :])

        c_l, acc = block(qi, causal2, jnp.zeros((GRP_HEADS * TQ, 1), F32), jnp.zeros((GRP_HEADS * TQ, GRP_W), F32))

        def cond(carry):
            return jnp.logical_and(carry[0] <= qi, carry[1] > 0)

        def kbody(carry):
            it, _, c_l, acc = carry
            c_l, acc = block(qi - it, None, c_l, acc)
            return it + 1, _alive(c_l), c_l, acc

        acc = lax.while_loop(cond, kbody, (jnp.int32(1), _alive(c_l), c_l, acc))[3]
        o = _unstack_heads(acc, heads, TQ)
        o_ref[...] = o.astype(BF)
        gv = g_ref[...]
        for half in range(GRP_W // 128):
            lanes = slice(half * 128, (half + 1) * 128)
            on_ref[:, lanes] = _headnorm_pair(o[:, lanes], gv[:, lanes], hm0).astype(BF)

    w = GRP_W
    full = lambda i: pl.BlockSpec((None, None, seq, w), lambda b, hp, q: (i, b, 0, hp))
    qblk = pl.BlockSpec((None, None, TQ, w), lambda b, hp, q: (0, b, q, hp))
    oblk = pl.BlockSpec((None, TQ, w), lambda b, hp, q: (b, q, hp))
    return _call(
        body, "sb_fwd", (nb, N_HEADS // GRP_HEADS, nq),
        [qblk, full(1), full(2), pl.BlockSpec((1, w), lambda b, hp, q: (0, hp))],
        [oblk, oblk],
        [_sds((nb, seq, D_GRP), BF), _sds((nb, seq, D_GRP), BF)],
        (qkv6, qkv6, qkv6, g_sb), carry=carry)


def _sb_bwd(qkv6, do, nb, seq, carry=None):
    nq = seq // TQ
    nk = seq // KB

    def body(q_ref, k_ref, v_ref, do_ref, out_ref, dk_acc, dv_acc, g_st, s_st):
        qi = pl.program_id(2)
        hm0, rel, kr, kc = _sb_masks()
        upper = (kr > kc).astype(BF)
        lower = (kr < kc).astype(BF)

        @pl.when(qi == 0)
        def _():
            dk_acc[...] = jnp.zeros_like(dk_acc)
            dv_acc[...] = jnp.zeros_like(dv_acc)

        heads = _dil_masks()[0]
        qs = _stack_heads(q_ref[...], heads)
        dos = _stack_heads(do_ref[...], heads)
        causal2 = jnp.concatenate([rel] * GRP_HEADS, axis=0) > 0

        def weights(kj, causal, c_l):
            ks = pl.multiple_of(kj * KB, KB)
            vb = v_ref[pl.ds(ks, KB), :]
            z = _nt(qs, k_ref[pl.ds(ks, KB), :]) * SCALE
            sp = _softplus(z)
            ln = -sp if causal is None else jnp.where(causal, -sp, 0.0)
            suf = _nn2(ln, upper)
            lsz = z - sp
            w = jnp.exp(lsz + (suf + c_l))
            if causal is not None:
                w = jnp.where(causal, w, 0.0)
            g_st[kj] = w * _nt(dos, vb)
            s_st[kj] = jnp.exp(lsz)
            dv_acc[pl.ds(ks, KB), :] += _tn(w.astype(BF), dos)
            return c_l + (suf[:, 0:1] + ln[:, 0:1])

        zc = jnp.zeros((GRP_HEADS * TQ, 1), F32)
        c_l = weights(qi, causal2, zc)

        def acond(carry):
            return jnp.logical_and(carry[0] <= qi, carry[1] > 0)

        def abody(carry):
            c_l = weights(qi - carry[0], None, carry[2])
            return carry[0] + 1, _alive(c_l), c_l

        n_used = lax.while_loop(acond, abody, (jnp.int32(1), _alive(c_l), c_l))[0]

        def grads(kj, causal, c_g, dq):
            ks = pl.multiple_of(kj * KB, KB)
            kb = k_ref[pl.ds(ks, KB), :]
            g = g_st[kj]
            sig = s_st[kj]
            pre = _nn(g.astype(BF), lower)
            dz = g * (1.0 - sig) - sig * (pre + c_g)
            if causal is not None:
                dz = jnp.where(causal, dz, 0.0)
            dzb = (dz * SCALE).astype(BF)
            dk_acc[pl.ds(ks, KB), :] += _tn(dzb, qs)
            return c_g + (pre[:, KB - 1:KB] + g[:, KB - 1:KB]), dq + _nn(dzb, kb)

        c_g, dq = lax.fori_loop(qi - n_used + 1, qi, lambda kj, cr: grads(kj, None, *cr),
                                (zc, jnp.zeros((GRP_HEADS * TQ, GRP_W), F32)))
        _, dq = grads(qi, causal2, c_g, dq)
        dq = _unstack_heads(dq, heads, TQ)
        out_ref[0, pl.ds(pl.multiple_of(qi * TQ, TQ), TQ), :] = dq.astype(BF)

        @pl.when(qi == nq - 1)
        def _():
            out_ref[1] = dk_acc[...].astype(BF)
            out_ref[2] = dv_acc[...].astype(BF)

    w = GRP_W
    full = lambda i: pl.BlockSpec((None, None, seq, w), lambda b, hp, q: (i, b, 0, hp))
    qblk = pl.BlockSpec((None, None, TQ, w), lambda b, hp, q: (0, b, q, hp))
    oblk = pl.BlockSpec((None, TQ, w), lambda b, hp, q: (b, q, hp))
    return _call(
        body, "sb_bwd", (nb, N_HEADS // GRP_HEADS, nq),
        [qblk, full(1), full(2), oblk],
        [pl.BlockSpec((3, None, seq, w), lambda b, hp, q: (0, b, 0, hp))],
        [_sds((6, nb, seq, D_GRP), BF)], (qkv6, qkv6, qkv6, do),
        scratch=[pltpu.VMEM((seq, w), F32), pltpu.VMEM((seq, w), F32),
                 pltpu.VMEM((nk, GRP_HEADS * TQ, KB), F32), pltpu.VMEM((nk, GRP_HEADS * TQ, KB), F32)],
        carry=carry)


def _t5_bucket(n):
    max_exact = N_BUCKETS // 2
    nf = np.maximum(n, 1).astype(np.float32)
    large = max_exact + (np.log(nf / max_exact) / math.log(MAX_DISTANCE / max_exact)
                         * (N_BUCKETS - max_exact)).astype(np.int32)
    large = np.minimum(large, N_BUCKETS - 1)
    return np.where(n < max_exact, n, large).astype(np.int32)


def _bucket_map(dilation):
    step = BLOCK + np.arange(BLOCK)[:, None] - np.arange(2 * BLOCK)[None, :]
    return _t5_bucket(np.clip(step, 0, N_STEPS) * dilation)


GRP_HEADS = 4
GRP_W = GRP_HEADS * HEAD_DIM


def _dil_masks():
    lane = lax.broadcasted_iota(jnp.int32, (1, GRP_W), 1)
    heads = [jnp.logical_and(lane >= HEAD_DIM * i, lane < HEAD_DIM * (i + 1)) for i in range(GRP_HEADS)]
    iq = jnp.bitwise_and(lax.broadcasted_iota(jnp.int32, (GRP_HEADS * BLOCK, BLOCK), 0), BLOCK - 1)
    ik = lax.broadcasted_iota(jnp.int32, (GRP_HEADS * BLOCK, BLOCK), 1)
    return heads, ik <= iq, ik >= iq


def _stack_heads(x, heads):
    zero = jnp.zeros_like(x)
    return jnp.concatenate([jnp.where(hm, x, zero) for hm in heads], axis=0)


def _unstack_heads(xs, heads, rows=BLOCK):
    out = xs[0:rows]
    for i in range(1, GRP_HEADS):
        out = jnp.where(heads[i], xs[i * rows:(i + 1) * rows], out)
    return out


def _dil_rows(n):
    rs = pl.multiple_of(n * BLOCK, BLOCK)
    ps = pl.multiple_of(jnp.maximum(n - 1, 0) * BLOCK, BLOCK)
    return pl.ds(rs, BLOCK), pl.ds(ps, BLOCK)


def _dil_probs(qs, kc, kp, b_ref, gi, valid_c, valid_p):
    rows = slice(gi * GRP_HEADS * BLOCK, (gi + 1) * GRP_HEADS * BLOCK)
    zc = _nt(qs, kc) * SCALE + b_ref[rows, BLOCK:2 * BLOCK]
    zp = _nt(qs, kp) * SCALE + b_ref[rows, 0:BLOCK]
    zc = jnp.where(valid_c, zc, NEG_INF)
    zp = jnp.where(valid_p, zp, NEG_INF)
    m = jnp.maximum(jnp.max(zc, axis=-1, keepdims=True), jnp.max(zp, axis=-1, keepdims=True))
    ec = jnp.exp(zc - m)
    ep = jnp.exp(zp - m)
    den = jnp.sum(ec, axis=-1, keepdims=True) + jnp.sum(ep, axis=-1, keepdims=True)
    return ec, ep, den, m


def _dil_fwd(qkv6r, base, bias, nb, sub_len, dilation):
    n_blk = sub_len // BLOCK

    def body(q_ref, k_ref, v_ref, b_ref, o_ref, l_ref):
        heads, valid_c, valid_p0 = _dil_masks()

        def nbody(n, carry):
            cur, prev = _dil_rows(n)
            valid_p = jnp.logical_and(valid_p0, n > 0)
            for gi in range(N_HEADS // GRP_HEADS):
                lanes = slice(gi * GRP_W, (gi + 1) * GRP_W)
                qs = _stack_heads(q_ref[cur, lanes], heads)
                ec, ep, den, m = _dil_probs(qs, k_ref[cur, lanes], k_ref[prev, lanes], b_ref, gi, valid_c, valid_p)
                o = (_nn(ec.astype(BF), v_ref[cur, lanes]) + _nn(ep.astype(BF), v_ref[prev, lanes])) / den
                o_ref[cur, lanes] = _unstack_heads(o, heads).astype(BF)
                l_ref[cur, lanes] = _unstack_heads(jnp.broadcast_to(m + jnp.log(den), o.shape), heads)
            return carry

        lax.fori_loop(0, n_blk, nbody, 0)

    seqblk = lambda i: pl.BlockSpec((None, None, sub_len, D_GRP), lambda b, r: (i, b, 0, r))
    oblk = pl.BlockSpec((None, sub_len, D_GRP), lambda b, r: (b, 0, r))
    shp = _sds((nb, sub_len, dilation * D_GRP), F32)
    return pl.pallas_call(
        body, name="dil_fwd_%d" % dilation, grid=(nb, dilation),
        in_specs=[seqblk(base), seqblk(base + 1), seqblk(base + 2), _whole(bias)],
        out_specs=[oblk, oblk], out_shape=[_sds(shp.shape, BF), shp],
        compiler_params=_cp(2))(qkv6r, qkv6r, qkv6r, bias)


def _dil_bwd(qkv6r, base, bias, do_c, dd_c, nb, sub_len, dilation, carry=None):
    n_blk = sub_len // BLOCK

    def body(q_ref, k_ref, v_ref, b_ref, do_ref, dd_ref, out_ref, a_ref, dk_acc, dv_acc):
        heads, valid_c, valid_p0 = _dil_masks()
        first = jnp.logical_and(pl.program_id(0) == 0, pl.program_id(1) == 0)

        @pl.when(first)
        def _():
            a_ref[...] = jnp.zeros_like(a_ref)

        dk_acc[...] = jnp.zeros_like(dk_acc)
        dv_acc[...] = jnp.zeros_like(dv_acc)

        def nbody(n, carry):
            cur, prev = _dil_rows(n)
            valid_p = jnp.logical_and(valid_p0, n > 0)
            for gi in range(N_HEADS // GRP_HEADS):
                lanes = slice(gi * GRP_W, (gi + 1) * GRP_W)
                kc, kp = k_ref[cur, lanes], k_ref[prev, lanes]
                vc, vp = v_ref[cur, lanes], v_ref[prev, lanes]
                qs = _stack_heads(q_ref[cur, lanes], heads)
                dos = _stack_heads(do_ref[cur, lanes], heads).astype(BF)
                dds = jnp.sum(_stack_heads(dd_ref[cur, lanes], heads), axis=-1, keepdims=True) * (1.0 / HEAD_DIM)
                ec, ep, den, _ = _dil_probs(qs, kc, kp, b_ref, gi, valid_c, valid_p)
                inv = 1.0 / den
                pc = ec * inv
                pp = ep * inv
                dzc = pc * (_nt(dos, vc) + dds)
                dzp = pp * (_nt(dos, vp) + dds)
                rows = slice(gi * GRP_HEADS * BLOCK, (gi + 1) * GRP_HEADS * BLOCK)
                a_ref[rows, BLOCK:2 * BLOCK] += dzc
                a_ref[rows, 0:BLOCK] += dzp
                dzcb = (dzc * SCALE).astype(BF)
                dzpb = (dzp * SCALE).astype(BF)
                out_ref[0, cur, lanes] = _unstack_heads(_nn(dzcb, kc) + _nn(dzpb, kp), heads).astype(BF)
                dk_acc[cur, lanes] += _tn(dzcb, qs)
                dk_acc[prev, lanes] += _tn(dzpb, qs)
                dv_acc[cur, lanes] += _tn(pc.astype(BF), dos)
                dv_acc[prev, lanes] += _tn(pp.astype(BF), dos)
            return carry

        lax.fori_loop(0, n_blk, nbody, 0)
        out_ref[1] = dk_acc[...].astype(BF)
        out_ref[2] = dv_acc[...].astype(BF)

    seqblk = lambda i: pl.BlockSpec((None, None, sub_len, D_GRP), lambda b, r: (i, b, 0, r))
    oblk = pl.BlockSpec((None, sub_len, D_GRP), lambda b, r: (b, 0, r))
    return _call(
        body, "dil_bwd_%d" % dilation, (nb, dilation),
        [seqblk(base), seqblk(base + 1), seqblk(base + 2), _whole(bias), oblk, oblk],
        [pl.BlockSpec((3, None, sub_len, D_GRP), lambda b, r: (0, b, 0, r)),
         pl.BlockSpec((N_HEADS * BLOCK, 2 * BLOCK), lambda b, r: (0, 0))],
        [_sds((3, nb, sub_len, dilation * D_GRP), BF), _sds((N_HEADS * BLOCK, 2 * BLOCK), F32)],
        (qkv6r, qkv6r, qkv6r, bias, do_c, dd_c),
        scratch=[pltpu.VMEM((sub_len, D_GRP), F32)] * 2, carry=carry)


def _group_ones():
    idx = np.arange(D_GRP) // HEAD_DIM
    return jnp.asarray((idx[:, None] == idx[None, :]).astype(np.float32), dtype=BF)


def _dil_alphas(l1, l4, l16):
    mx = jnp.maximum(jnp.maximum(l1, l4), l16)
    e1 = jnp.exp(l1 - mx)
    e4 = jnp.exp(l4 - mx)
    e16 = jnp.exp(l16 - mx)
    den = e1 + e4 + e16
    return e1 / den, e4 / den, e16 / den


def _residue_spec(dil):
    return pl.BlockSpec((TM // dil, dil * D_GRP), lambda m: (m, 0))


def _from_residue(src, dil, cg, buf):
    if dil == 1:
        return src[:, cg * 128:(cg + 1) * 128].astype(F32)
    for r in range(dil):
        buf[pl.ds(r, TM // dil, stride=dil), :] = (
            src[:, r * D_GRP + cg * 128:r * D_GRP + (cg + 1) * 128].astype(F32))
    return buf[...]


def _to_residue(dst, dil, cg, buf, val):
    if dil == 1:
        dst[:, cg * 128:(cg + 1) * 128] = val.astype(dst.dtype)
        return
    buf[...] = val
    for r in range(dil):
        dst[:, r * D_GRP + cg * 128:r * D_GRP + (cg + 1) * 128] = (
            buf[pl.ds(r, TM // dil, stride=dil), :].astype(dst.dtype))


def _pair_sum(x, hm0):
    s0 = jnp.sum(jnp.where(hm0, x, 0.0), axis=-1, keepdims=True)
    s1 = jnp.sum(jnp.where(hm0, 0.0, x), axis=-1, keepdims=True)
    return jnp.where(hm0, s0, s1)


def _dil_comb(os, ls, g_dil):
    t = os[0].shape[0]
    dils = [dil for _, dil in DIL_CONFIGS]

    def body(o1, l1, o4, l4, o16, l16, g_ref, o_ref, on_ref, b0, b1, b2, b3):
        hm0 = lax.broadcasted_iota(jnp.int32, (1, 128), 1) < HEAD_DIM
        for cg in range(D_GRP // 128):
            lanes = slice(cg * 128, (cg + 1) * 128)
            ov = [_from_residue(src, dil, cg, buf) for src, dil, buf in zip((o1, o4, o16), dils, (None, b0, b1))]
            lv = [_from_residue(src, dil, cg, buf) for src, dil, buf in zip((l1, l4, l16), dils, (None, b2, b3))]
            a1, a4, a16 = _dil_alphas(*lv)
            o = a1 * ov[0] + a4 * ov[1] + a16 * ov[2]
            o_ref[:, lanes] = o.astype(BF)
            on_ref[:, lanes] = _headnorm_pair(o, g_ref[:, lanes], hm0).astype(BF)

    blk = pl.BlockSpec((TM, D_GRP), lambda m: (m, 0))
    specs = [_residue_spec(dil) for dil in dils for _ in range(2)]
    return pl.pallas_call(
        body, name="dil_comb", grid=(t // TM,),
        in_specs=specs + [pl.BlockSpec((1, D_GRP), lambda m: (0, 0))],
        out_specs=[blk, blk],
        out_shape=[_sds((t, D_GRP), BF), _sds((t, D_GRP), BF)],
        scratch_shapes=[pltpu.VMEM((TM, 128), F32)] * 4,
        compiler_params=_cp(1))(os[0], ls[0], os[1], ls[1], os[2], ls[2], g_dil)


def _dil_comb_bwd(do, os, ls):
    t = do.shape[0]
    dils = [dil for _, dil in DIL_CONFIGS]

    def body(do_ref, o1, l1, o4, l4, o16, l16, d1, d4, d16, e1, e4, e16, b0, b1, b2, b3):
        hm0 = lax.broadcasted_iota(jnp.int32, (1, 128), 1) < HEAD_DIM
        for cg in range(D_GRP // 128):
            dov = do_ref[:, cg * 128:(cg + 1) * 128].astype(F32)
            ov = [_from_residue(src, dil, cg, buf) for src, dil, buf in zip((o1, o4, o16), dils, (None, b0, b1))]
            lv = [_from_residue(src, dil, cg, buf) for src, dil, buf in zip((l1, l4, l16), dils, (None, b2, b3))]
            al = _dil_alphas(*lv)
            sbar = al[0] * _pair_sum(dov * ov[0], hm0)
            for a_c, o_c in zip(al[1:], ov[1:]):
                sbar = sbar + a_c * _pair_sum(dov * o_c, hm0)
            for a_c, dil, dref, eref in zip(al, dils, (d1, d4, d16), (e1, e4, e16)):
                _to_residue(dref, dil, cg, b0, a_c * dov)
                _to_residue(eref, dil, cg, b1, -a_c * sbar)

    specs = [_residue_spec(dil) for dil in dils]
    return pl.pallas_call(
        body, name="dil_comb_bwd", grid=(t // TM,),
        in_specs=[pl.BlockSpec((TM, D_GRP), lambda m: (m, 0))] + [sp for sp in specs for _ in range(2)],
        out_specs=specs + specs,
        out_shape=[_sds((t // dil, dil * D_GRP), BF) for dil in dils]
        + [_sds((t // dil, dil * D_GRP), F32) for dil in dils],
        scratch_shapes=[pltpu.VMEM((TM, 128), F32)] * 4,
        compiler_params=_cp(1))(do, os[0], ls[0], os[1], ls[1], os[2], ls[2])


def _dqkv_dil_sum(ds, dqkv6):
    t = dqkv6.shape[1]
    dils = [dil for _, dil in DIL_CONFIGS]

    def body(*refs):
        srcs, o_ref, acc = refs[:len(dils)], refs[len(dils) + 1], refs[len(dils) + 2]
        for a in range(3):
            for cg in range(D_GRP // 128):
                for src, dil in zip(srcs, dils):
                    for r in range(dil):
                        part = src[a, :, r * D_GRP + cg * 128:r * D_GRP + (cg + 1) * 128].astype(F32)
                        rows = pl.ds(r, TM // dil, stride=dil) if dil > 1 else slice(None)
                        if dil == dils[0]:
                            acc[rows, :] = part
                        else:
                            acc[rows, :] += part
                o_ref[a, :, cg * 128:(cg + 1) * 128] = acc[...].astype(BF)

    return pl.pallas_call(
        body, name="dqkv_dil_sum", grid=(t // TM,),
        in_specs=[pl.BlockSpec((3, TM // dil, dil * D_GRP), lambda m: (0, m, 0)) for dil in dils]
        + [pl.BlockSpec(memory_space=pl.ANY)],
        out_specs=pl.BlockSpec((3, TM, D_GRP), lambda m: (1, m, 0)),
        out_shape=_sds((6, t, D_GRP), BF), input_output_aliases={len(dils): 0},
        scratch_shapes=[pltpu.VMEM((TM, 128), F32)],
        compiler_params=_cp(1))(*ds, dqkv6)


def _relbias_grad(a_all, onehot):
    def body(a_ref, oh_ref, o_ref):
        acc = jnp.zeros((N_HEADS, N_BUCKETS), F32)
        for c in range(len(DIL_CONFIGS)):
            av = a_ref[c]
            hi = av.astype(BF)
            lo = (av - hi.astype(F32)).astype(BF)
            acc = acc + _nt(hi, oh_ref[c]) + _nt(lo, oh_ref[c])
        o_ref[...] = acc

    return pl.pallas_call(body, name="relbias_grad", out_shape=_sds((N_HEADS, N_BUCKETS), F32),
                          compiler_params=_cp())(a_all, onehot)


def _headnorm_bwd(dn, o, gv, mv):
    ms = _nn2(o * o, mv) * (1.0 / HEAD_DIM)
    r = lax.rsqrt(ms + EPS)
    nrm = o * r
    dg = jnp.sum(dn * nrm, axis=0, keepdims=True)
    dnn = dn * gv
    do = r * (dnn - nrm * (_nn2(dnn * nrm, mv) * (1.0 / HEAD_DIM)))
    return do, dg


def _mix_bwd_out(dx, gt, tv, w_out, o_sb, o_dil, on_sb, on_dil, g_sb, g_dil, ones_g, seq, carry=None):
    t, d = dx.shape
    per = seq // TM
    nb = t // seq

    def body(dx_ref, gt_ref, t_ref, w_ref, osb, odl, onsb, ondl, gsb, gdl, m_ref,
             dosb, dodl, dgt_ref, dgsb, dgdl, dw_ref):
        m = pl.program_id(0)
        dxv = dx_ref[...]
        dt = (gt_ref[...] * dxv).astype(BF)
        _acc_rows(dgt_ref, jnp.sum(dxv * t_ref[...].astype(F32), axis=0, keepdims=True), m % per == 0)
        mv = m_ref[...]
        don_sb = _nt(dt, w_ref[0:D_GRP, :])
        don_dl = _nt(dt, w_ref[D_GRP:2 * D_GRP, :])
        do1, dg1 = _headnorm_bwd(don_sb, osb[...].astype(F32), gsb[...], mv)
        do2, dg2 = _headnorm_bwd(don_dl, odl[...].astype(F32), gdl[...], mv)
        dosb[...] = do1.astype(BF)
        dodl[...] = do2.astype(BF)
        _acc_rows(dgsb, dg1, m == 0)
        _acc_rows(dgdl, dg2, m == 0)
        p1 = _tn(onsb[...], dt)
        p2 = _tn(ondl[...], dt)

        @pl.when(m == 0)
        def _():
            dw_ref[0:D_GRP, :] = p1
            dw_ref[D_GRP:2 * D_GRP, :] = p2

        @pl.when(m != 0)
        def _():
            dw_ref[0:D_GRP, :] += p1
            dw_ref[D_GRP:2 * D_GRP, :] += p2

    row = pl.BlockSpec((TM, d), lambda m: (m, 0))
    half = pl.BlockSpec((TM, D_GRP), lambda m: (m, 0))
    ex = pl.BlockSpec((None, 1, d), lambda m: (m // per, 0, 0))
    gvec = pl.BlockSpec((1, D_GRP), lambda m: (0, 0))
    wblk = pl.BlockSpec((2 * D_GRP, d), lambda m: (0, 0))
    return _call(
        body, "mix_bwd_out", (t // TM,),
        [row, ex, row, wblk, half, half, half, half, gvec, gvec, pl.BlockSpec((D_GRP, D_GRP), lambda m: (0, 0))],
        [half, half, ex, gvec, gvec, wblk],
        [_sds((t, D_GRP), BF), _sds((t, D_GRP), BF), _sds((nb, 1, d), F32),
         _sds((1, D_GRP), F32), _sds((1, D_GRP), F32), _sds((2 * D_GRP, d), F32)],
        (dx, gt, tv, w_out, o_sb, o_dil, on_sb, on_dil, g_sb, g_dil, ones_g), carry=carry)


def _dw_in(h, dqkv6, carry=None):
    t, d = h.shape
    wc = 6 * D_GRP // N_CHIPS

    def body(h_ref, g_ref, o_ref):
        kt = pl.program_id(0)
        hv = h_ref[...]
        for j in range(N_CHIPS):
            p = _tn(hv, _chip_cols(g_ref, j, wc))

            @pl.when(kt == 0)
            def _(p=p, j=j):
                o_ref[j, 0] = p

            @pl.when(kt != 0)
            def _(p=p, j=j):
                o_ref[j, 0] += p

    return _call(
        body, "dw_in", (t // TK_W,),
        [pl.BlockSpec((TK_W, d), lambda kt: (kt, 0)), pl.BlockSpec((6, TK_W, D_GRP), lambda kt: (0, kt, 0))],
        [pl.BlockSpec((N_CHIPS, 1, d, wc), lambda kt: (0, 0, 0, 0))],
        [_sds((N_CHIPS, 1, d, wc), F32)], (h, dqkv6), carry=carry)


def _mix_bwd_dh(dqkv6, w_in, x, g, sc, dxo, seq, carry=None):
    _, t, _ = dqkv6.shape
    d = x.shape[-1]
    wc = w_in.shape[-1]
    per = seq // TM
    nb = t // seq

    def body(g6_ref, w_ref, x_ref, g_ref, sc_ref, dxo_ref, dx_ref, dsh_ref, dsc_ref, dg_ref):
        m = pl.program_id(0)
        dh = _nt(_chip_cols(g6_ref, 0, wc), w_ref[0, 0])
        for j in range(1, N_CHIPS):
            dh = dh + _nt(_chip_cols(g6_ref, j, wc), w_ref[j, 0])
        dx, dsh, dsc, dg = _modnorm_bwd_tile(dh, x_ref[...], g_ref[...], sc_ref[...], dxo_ref[...])
        dx_ref[...] = dx
        _acc_rows(dsh_ref, dsh, m % per == 0)
        _acc_rows(dsc_ref, dsc, m % per == 0)
        _acc_rows(dg_ref, dg, m == 0)

    row = pl.BlockSpec((TM, d), lambda m: (m, 0))
    ex = pl.BlockSpec((None, 1, d), lambda m: (m // per, 0, 0))
    vec = pl.BlockSpec((1, d), lambda m: (0, 0))
    return _call(
        body, "mix_bwd_dh", (t // TM,),
        [pl.BlockSpec((6, TM, D_GRP), lambda m: (0, m, 0)), _whole(w_in), row, vec, ex, row],
        [row, ex, ex, vec],
        [_sds((t, d), F32), _sds((nb, 1, d), F32), _sds((nb, 1, d), F32), _sds((1, d), F32)],
        (dqkv6, w_in, x, g, sc, dxo), carry=carry)


def _ffn_down_loss(s, wd, x, gt, seq, coef, g, target):
    _, t, fs = s.shape
    d = x.shape[-1]
    per = seq // TM
    steps = t // TM

    def body(s_ref, w_ref, x_ref, gt_ref, g_ref, t_ref, f_ref, dx_ref, dg_ref, loss_ref, lacc):
        m = pl.program_id(0)
        f = _nn(s_ref[0], w_ref[0, 0])
        for j in range(1, N_CHIPS):
            f = f + _nn(s_ref[j], w_ref[j, 0])
        f_ref[...] = f.astype(BF)
        xv = x_ref[...] + (coef * gt_ref[...]) * f
        gv = g_ref[...]
        r = lax.rsqrt(jnp.mean(xv * xv, axis=-1, keepdims=True) + EPS)
        n = xv * r
        err = n * gv - t_ref[...]
        dy = err * (1.0 / d)
        _acc_rows(dg_ref, jnp.sum(dy * n, axis=0, keepdims=True), m == 0)
        dn = dy * gv
        dx_ref[...] = r * (dn - n * jnp.mean(dn * n, axis=-1, keepdims=True))
        _acc_rows(lacc, jnp.sum(err * err, axis=0, keepdims=True), m == 0)

        @pl.when(m == steps - 1)
        def _():
            tot = jnp.sum(lacc[...], axis=-1, keepdims=True) * (0.5 / d)
            loss_ref[...] = jnp.broadcast_to(tot, (1, 128))

    row = pl.BlockSpec((TM, d), lambda m: (m, 0))
    vec = pl.BlockSpec((1, d), lambda m: (0, 0))
    return pl.pallas_call(
        body, name="ffn_down_loss", grid=(steps,),
        in_specs=[pl.BlockSpec((N_CHIPS, TM, fs), lambda m: (0, m, 0)), _whole(wd), row,
                  pl.BlockSpec((None, 1, d), lambda m: (m // per, 0, 0)), vec, row],
        out_specs=[row, row, vec, pl.BlockSpec((1, 128), lambda m: (0, 0))],
        out_shape=[_sds((t, d), BF), _sds((t, d), F32), _sds((1, d), F32), _sds((1, 128), F32)],
        scratch_shapes=[pltpu.VMEM((1, d), F32)],
        compiler_params=_cp(1))(s, wd, x, gt, g, target)


def _row_tile(rows, cols):
    best = rows
    for tr in range(8, rows + 1, 8):
        if rows % tr == 0 and tr * cols * 4 <= (1 << 20):
            best = tr
    if best * cols * 4 > (1 << 21):
        best = 8
    return best


def _adamw(w, g_arr, g_sel, m, v):
    rows, cols = w.shape
    tr = _row_tile(rows, cols)
    b1c = 1.0 - ADAM_B1 ** ADAM_STEP
    b2c = 1.0 - ADAM_B2 ** ADAM_STEP

    def body(w_ref, g_ref, m_ref, v_ref, go_ref, d_ref, mo_ref, vo_ref):
        gv = g_ref[...]
        mn = ADAM_B1 * m_ref[...] + (1.0 - ADAM_B1) * gv
        vn = ADAM_B2 * v_ref[...] + (1.0 - ADAM_B2) * (gv * gv)
        go_ref[...] = gv
        mo_ref[...] = mn
        vo_ref[...] = vn
        d_ref[...] = -ADAM_LR * ((mn / b1c) / (jnp.sqrt(vn / b2c) + ADAM_EPS) + ADAM_WD * w_ref[...])

    blk = pl.BlockSpec((tr, cols), lambda i: (i, 0))
    shp = _sds((rows, cols), F32)
    return pl.pallas_call(
        body, name="adamw", grid=(rows // tr,),
        in_specs=[blk, pl.BlockSpec((None, tr, cols), lambda i: (g_sel, i, 0)), blk, blk],
        out_specs=[blk] * 4, out_shape=[shp] * 4,
        compiler_params=_cp(1))(w, g_arr, m, v)


def _flip(v, bit):
    return 1 - v if bit else v


def _my_place():
    x, y, c = lax.axis_index("x"), lax.axis_index("y"), lax.axis_index("c")
    return x, y, c


class _Exchange:
    def __init__(self, operands, out_shape, aliases, sems, start, finish):
        self.operands, self.out_shape, self.aliases, self.sems = list(operands), list(out_shape), dict(aliases), list(sems)
        self.start, self.finish = start, finish


def _join(exchanges):
    exchanges = [e for e in exchanges if e is not None]
    if not exchanges:
        return None
    ops, outs, sems, aliases, spans = [], [], [], {}, []
    for e in exchanges:
        spans.append((len(ops), len(outs), len(sems), e))
        for i, j in e.aliases.items():
            aliases[len(ops) + i] = len(outs) + j
        ops += e.operands
        outs += e.out_shape
        sems += e.sems

    def run(which):
        def go(ins, res, sm):
            for io, oo, so, e in spans:
                getattr(e, which)(ins[io:io + len(e.operands)], res[oo:oo + len(e.out_shape)], sm[so:so + len(e.sems)])
        return go

    return _Exchange(ops, outs, aliases, sems, run("start"), run("finish"))


def _call(body, name, grid, in_specs, out_specs, out_shape, args, scratch=(), carry=None, io_alias=None):
    in_specs, out_specs, out_shape, scratch = list(in_specs), list(out_specs), list(out_shape), list(scratch)
    io_alias = dict(io_alias or {})
    if carry is None:
        return pl.pallas_call(body, name=name, grid=grid, in_specs=in_specs, out_specs=out_specs,
                              out_shape=out_shape, scratch_shapes=scratch, input_output_aliases=io_alias,
                              compiler_params=_cp(len(grid)))(*args)
    n_in, n_out, n_s = len(in_specs), len(out_specs), len(scratch)
    c_in, c_out = len(carry.operands), len(carry.out_shape)
    any_spec = pl.BlockSpec(memory_space=pl.ANY)

    def wrapped(*refs):
        ins, cins = refs[:n_in], refs[n_in:n_in + c_in]
        o0 = n_in + c_in
        outs, couts = refs[o0:o0 + n_out], refs[o0 + n_out:o0 + n_out + c_out]
        s0 = o0 + n_out + c_out
        scr, sems = refs[s0:s0 + n_s], refs[s0 + n_s:]
        first = pl.program_id(0) == 0
        last = pl.program_id(0) == grid[0] - 1
        for ax in range(1, len(grid)):
            first = jnp.logical_and(first, pl.program_id(ax) == 0)
            last = jnp.logical_and(last, pl.program_id(ax) == grid[ax] - 1)

        @pl.when(first)
        def _():
            carry.start(cins, couts, sems)

        body(*ins, *outs, *scr)

        @pl.when(last)
        def _():
            carry.finish(cins, couts, sems)

    return pl.pallas_call(
        wrapped, name=name, grid=grid, in_specs=in_specs + [any_spec] * c_in,
        out_specs=out_specs + [any_spec] * c_out, out_shape=out_shape + carry.out_shape,
        scratch_shapes=scratch + carry.sems,
        input_output_aliases={**io_alias, **{n_in + i: n_out + j for i, j in carry.aliases.items()}},
        compiler_params=_cp(len(grid)))(*args, *carry.operands)


def _whole_call(body, name, args, out_shape, scratch, carry=None):
    vm = pl.BlockSpec(memory_space=pltpu.VMEM)
    any_spec = pl.BlockSpec(memory_space=pl.ANY)
    out_shape, scratch = list(out_shape), list(scratch)
    n_in, n_out, n_s = len(args), len(out_shape), len(scratch)
    if carry is None:
        return pl.pallas_call(body, name=name, in_specs=[vm] * n_in, out_specs=[vm] * n_out, out_shape=out_shape,
                              scratch_shapes=scratch, compiler_params=_cp())(*args)
    c_in, c_out = len(carry.operands), len(carry.out_shape)

    def wrapped(*refs):
        ins, cins = refs[:n_in], refs[n_in:n_in + c_in]
        o0 = n_in + c_in
        outs, couts = refs[o0:o0 + n_out], refs[o0 + n_out:o0 + n_out + c_out]
        s0 = o0 + n_out + c_out
        scr, sems = refs[s0:s0 + n_s], refs[s0 + n_s:]
        carry.start(cins, couts, sems)
        body(*ins, *outs, *scr)
        carry.finish(cins, couts, sems)

    return pl.pallas_call(
        wrapped, name=name, in_specs=[vm] * n_in + [any_spec] * c_in, out_specs=[vm] * n_out + [any_spec] * c_out,
        out_shape=out_shape + carry.out_shape, scratch_shapes=scratch + carry.sems,
        input_output_aliases={n_in + i: n_out + j for i, j in carry.aliases.items()},
        compiler_params=_cp())(*args, *carry.operands)


def _alone(name, ex):
    any_spec = pl.BlockSpec(memory_space=pl.ANY)
    c_in, c_out = len(ex.operands), len(ex.out_shape)

    def body(*refs):
        ins, outs, sems = refs[:c_in], refs[c_in:c_in + c_out], refs[c_in + c_out:]
        ex.start(ins, outs, sems)
        ex.finish(ins, outs, sems)

    return pl.pallas_call(
        body, name=name, in_specs=[any_spec] * c_in, out_specs=[any_spec] * c_out, out_shape=ex.out_shape,
        scratch_shapes=ex.sems, input_output_aliases=ex.aliases, compiler_params=_cp())(*ex.operands)


def _ada_fwd(c_pad, w_ada, b_shard, carry=None):
    d = c_pad.shape[-1]
    cols = w_ada.shape[-1]
    chunk = 384

    def body(c_ref, w_ref, b_ref, call_ref, mod_ref, part, s1, r1, s2, r2):
        x, y, c = _my_place()
        dev = 4 * x + 2 * y + c
        chip = 2 * x + y
        call_ref[dev] = c_ref[...]

        def c_copy(k):
            px, py, pc = _flip(x, (k >> 2) & 1), _flip(y, (k >> 1) & 1), _flip(c, k & 1)
            return px, py, pc

        sends = []
        for k in range(1, N_DEV):
            px, py, pc = c_copy(k)
            cp = pltpu.make_async_remote_copy(src_ref=c_ref, dst_ref=call_ref.at[dev], send_sem=s1.at[k - 1],
                                              recv_sem=r1.at[k - 1], device_id=(px, py, pc), device_id_type=MESH)
            cp.start()
            sends.append(cp)
        for k in range(1, N_DEV):
            px, py, pc = c_copy(k)
            pltpu.make_async_remote_copy(src_ref=c_ref, dst_ref=call_ref.at[4 * px + 2 * py + pc],
                                         send_sem=s1.at[k - 1], recv_sem=r1.at[k - 1],
                                         device_id=(px, py, pc), device_id_type=MESH).wait_recv()
        for cp in sends:
            cp.wait_send()

        cs = call_ref[...].reshape(N_DEV * 8, d)
        sc = (cs * jax.nn.sigmoid(cs)).astype(BF)
        for n0 in range(0, cols, chunk):
            blk = _nn(sc, w_ref[:, n0:n0 + chunk].astype(BF)) + b_ref[:, n0:n0 + chunk]
            part[:, :, n0:n0 + chunk] = blk.reshape(N_DEV, 8, chunk)

        mod_ref[chip] = part[dev]
        sends = []
        for kk in range(1, N_CHIPS):
            px, py = _flip(x, (kk >> 1) & 1), _flip(y, kk & 1)
            cp = pltpu.make_async_remote_copy(src_ref=part.at[4 * px + 2 * py + c], dst_ref=mod_ref.at[chip],
                                              send_sem=s2.at[kk - 1], recv_sem=r2.at[kk - 1],
                                              device_id=(px, py, c), device_id_type=MESH)
            cp.start()
            sends.append(cp)
        for kk in range(1, N_CHIPS):
            px, py = _flip(x, (kk >> 1) & 1), _flip(y, kk & 1)
            pltpu.make_async_remote_copy(src_ref=part.at[dev], dst_ref=mod_ref.at[2 * px + py],
                                         send_sem=s2.at[kk - 1], recv_sem=r2.at[kk - 1],
                                         device_id=(px, py, c), device_id_type=MESH).wait_recv()
        for cp in sends:
            cp.wait_send()

    return _whole_call(
        body, "ada_fwd", (c_pad, w_ada, b_shard),
        [_sds((N_DEV, 8, d), F32), _sds((N_CHIPS, 8, cols), F32)],
        [pltpu.VMEM((N_DEV, 8, cols), F32),
         pltpu.SemaphoreType.DMA((N_DEV - 1,)), pltpu.SemaphoreType.DMA((N_DEV - 1,)),
         pltpu.SemaphoreType.DMA((N_CHIPS - 1,)), pltpu.SemaphoreType.DMA((N_CHIPS - 1,))], carry=carry)


def _ag_weights(bufs, kks=(1, 2, 3), relative=False):
    n, nk = len(bufs), len(kks)

    def half(b, which):
        hr = bufs[b].shape[2] // 2
        return pl.ds(pl.multiple_of(which * hr, 16), hr)

    def copies(outs, sems, b, i, kk):
        x, y, c = _my_place()
        chip = 2 * x + y
        px, py = _flip(x, (kk >> 1) & 1), _flip(y, kk & 1)
        mine, theirs = (0, kk) if relative else (chip, 2 * px + py)
        landing = kk if relative else chip
        k = nk * b + i
        send = pltpu.make_async_remote_copy(
            src_ref=outs[b].at[mine, :, half(b, c), :], dst_ref=outs[b].at[landing, :, half(b, c), :],
            send_sem=sems[0].at[k], recv_sem=sems[1].at[k], device_id=(px, py, c), device_id_type=MESH)
        got = outs[b].at[theirs, :, half(b, c), :]
        recv = pltpu.make_async_remote_copy(
            src_ref=got, dst_ref=got, send_sem=sems[0].at[k], recv_sem=sems[1].at[k],
            device_id=(px, py, c), device_id_type=MESH)
        fwd = pltpu.make_async_remote_copy(
            src_ref=got, dst_ref=got, send_sem=sems[2].at[k], recv_sem=sems[3].at[k],
            device_id=(x, y, 1 - c), device_id_type=MESH)
        other = outs[b].at[theirs, :, half(b, 1 - c), :]
        back = pltpu.make_async_remote_copy(
            src_ref=other, dst_ref=other, send_sem=sems[2].at[k], recv_sem=sems[3].at[k],
            device_id=(x, y, 1 - c), device_id_type=MESH)
        return send, recv, fwd, back

    def each(outs, sems):
        for b in range(n):
            for i, kk in enumerate(kks):
                yield copies(outs, sems, b, i, kk)

    def start(ins, outs, sems):
        for send, _, _, _ in each(outs, sems):
            send.start()

    def finish(ins, outs, sems):
        for _, recv, fwd, _ in each(outs, sems):
            recv.wait_recv()
            fwd.start()
        for send, _, fwd, back in each(outs, sems):
            back.wait_recv()
            send.wait_send()
            fwd.wait_send()

    return _Exchange(bufs, [_sds(s.shape, s.dtype) for s in bufs], {i: i for i in range(n)},
                     [pltpu.SemaphoreType.DMA((nk * n,))] * 4, start, finish)


def _rs_d2d(grads):
    n = len(grads)

    def copy(ins, outs, sems, b):
        x, y, c = _my_place()
        hr = grads[b].shape[2] // 2
        theirs = pl.ds(pl.multiple_of((1 - c) * hr, 8), hr)
        return pltpu.make_async_remote_copy(
            src_ref=ins[b].at[:, :, theirs, :], dst_ref=outs[b], send_sem=sems[0].at[b], recv_sem=sems[1].at[b],
            device_id=(x, y, 1 - c), device_id_type=MESH)

    def start(ins, outs, sems):
        for b in range(n):
            copy(ins, outs, sems, b).start()

    def finish(ins, outs, sems):
        for b in range(n):
            copy(ins, outs, sems, b).wait()

    return _Exchange(grads, [_sds(g.shape[:2] + (g.shape[2] // 2, g.shape[3]), F32) for g in grads], {},
                     [pltpu.SemaphoreType.DMA((n,))] * 2, start, finish)


def _add_halves(core, g, land):
    nchip, ng, rows, cols = g.shape
    hr = rows // 2
    tr = _row_tile(hr, cols)
    steps = hr // tr

    def body(core_ref, g_ref, l_ref, o_ref):
        del core_ref
        o_ref[...] = (g_ref[...] + l_ref[...]).astype(BF)

    return pl.pallas_call(
        body, name="add_halves",
        grid_spec=pltpu.PrefetchScalarGridSpec(
            num_scalar_prefetch=1, grid=(nchip, ng, steps),
            in_specs=[pl.BlockSpec((None, None, tr, cols), lambda j, a, i, cr: (j, a, cr[0] * steps + i, 0)),
                      pl.BlockSpec((None, None, tr, cols), lambda j, a, i, cr: (j, a, i, 0))],
            out_specs=pl.BlockSpec((None, None, tr, cols), lambda j, a, i, cr: (j, a, i, 0))),
        out_shape=_sds((nchip, ng, hr, cols), BF),
        compiler_params=_cp(3))(core, g, land)


def _rs_ici(parts, relative=False):
    n = len(parts)

    def copies(ins, outs, sems):
        x, y, c = _my_place()
        chip = 2 * x + y
        for b in range(n):
            for kk in range(1, N_CHIPS):
                px, py = _flip(x, (kk >> 1) & 1), _flip(y, kk & 1)
                k = 3 * b + kk - 1
                theirs, landing = (kk, kk) if relative else (2 * px + py, chip)
                send = pltpu.make_async_remote_copy(
                    src_ref=ins[b].at[theirs], dst_ref=outs[b].at[landing],
                    send_sem=sems[0].at[k], recv_sem=sems[1].at[k], device_id=(px, py, c), device_id_type=MESH)
                slot = outs[b].at[theirs]
                recv = pltpu.make_async_remote_copy(
                    src_ref=slot, dst_ref=slot, send_sem=sems[0].at[k], recv_sem=sems[1].at[k],
                    device_id=(px, py, c), device_id_type=MESH)
                yield send, recv

    def start(ins, outs, sems):
        for send, _ in copies(ins, outs, sems):
            send.start()

    def finish(ins, outs, sems):
        for send, recv in copies(ins, outs, sems):
            recv.wait_recv()
            send.wait_send()

    return _Exchange(parts, [_sds(p.shape, p.dtype) for p in parts], {},
                     [pltpu.SemaphoreType.DMA((3 * n,))] * 2, start, finish)


def _sum_chips(place, part, land, relative=False):
    nchip, ng, hr, cols = land.shape
    tr = _row_tile(hr, cols)
    steps = hr // tr

    def body(place_ref, p_ref, l1, l2, l3, o_ref):
        del place_ref
        o_ref[...] = ((p_ref[...].astype(F32) + l1[...].astype(F32)) + l2[...].astype(F32)) + l3[...].astype(F32)

    def slot(k):
        if relative:
            return pl.BlockSpec((None, None, tr, cols), lambda a, i, pr: (k, a, i, 0))
        return pl.BlockSpec((None, None, tr, cols), lambda a, i, pr: (jnp.bitwise_xor(pr[1], k), a, i, 0))

    return pl.pallas_call(
        body, name="sum_chips",
        grid_spec=pltpu.PrefetchScalarGridSpec(
            num_scalar_prefetch=1, grid=(ng, steps),
            in_specs=[slot(0), slot(1), slot(2), slot(3)],
            out_specs=pl.BlockSpec((None, tr, cols), lambda a, i, pr: (a, pr[0] * steps + i, 0))),
        out_shape=_sds((ng, 2 * hr, cols), F32),
        compiler_params=_cp(2))(place, part, land, land, land)


def _rs_final(bufs):
    n = len(bufs)

    def copy(outs, sems, b, which):
        x, y, c = _my_place()
        hr = bufs[b].shape[1] // 2
        rows = outs[b].at[:, pl.ds(pl.multiple_of((c if which == 0 else 1 - c) * hr, 8), hr), :]
        return pltpu.make_async_remote_copy(
            src_ref=rows, dst_ref=rows, send_sem=sems[0].at[b], recv_sem=sems[1].at[b],
            device_id=(x, y, 1 - c), device_id_type=MESH)

    def start(ins, outs, sems):
        for b in range(n):
            copy(outs, sems, b, 0).start()

    def finish(ins, outs, sems):
        for b in range(n):
            copy(outs, sems, b, 0).wait_send()
            copy(outs, sems, b, 1).wait_recv()

    return _Exchange(bufs, [_sds(h.shape, F32) for h in bufs], {i: i for i in range(n)},
                     [pltpu.SemaphoreType.DMA((n,))] * 2, start, finish)


def _small_sync(smalls, dmod_blk, c_all, carry=None):
    d = c_all.shape[-1]
    cols = dmod_blk.shape[-1]
    chunk = 384

    def body(sm_ref, dm_ref, c_ref, sum_ref, gw_ref, sm_all, dm_all, ssem, rsem):
        x, y, c = _my_place()
        dev = 4 * x + 2 * y + c
        chip = 2 * x + y
        sm_all[dev] = sm_ref[...]
        dm_all[dev] = dm_ref[chip]
        sends = []
        for k in range(1, N_DEV):
            px, py, pc = _flip(x, (k >> 2) & 1), _flip(y, (k >> 1) & 1), _flip(c, k & 1)
            a = pltpu.make_async_remote_copy(src_ref=sm_ref, dst_ref=sm_all.at[dev], send_sem=ssem.at[2 * (k - 1)],
                                             recv_sem=rsem.at[2 * (k - 1)], device_id=(px, py, pc),
                                             device_id_type=MESH)
            b = pltpu.make_async_remote_copy(src_ref=dm_ref.at[2 * px + py], dst_ref=dm_all.at[dev],
                                             send_sem=ssem.at[2 * (k - 1) + 1], recv_sem=rsem.at[2 * (k - 1) + 1],
                                             device_id=(px, py, pc), device_id_type=MESH)
            a.start()
            b.start()
            sends += [a, b]
        for k in range(1, N_DEV):
            px, py, pc = _flip(x, (k >> 2) & 1), _flip(y, (k >> 1) & 1), _flip(c, k & 1)
            pdev = 4 * px + 2 * py + pc
            pltpu.make_async_remote_copy(src_ref=sm_ref, dst_ref=sm_all.at[pdev], send_sem=ssem.at[2 * (k - 1)],
                                         recv_sem=rsem.at[2 * (k - 1)], device_id=(px, py, pc),
                                         device_id_type=MESH).wait_recv()
            pltpu.make_async_remote_copy(src_ref=dm_ref.at[chip], dst_ref=dm_all.at[pdev],
                                         send_sem=ssem.at[2 * (k - 1) + 1], recv_sem=rsem.at[2 * (k - 1) + 1],
                                         device_id=(px, py, pc), device_id_type=MESH).wait_recv()
        for cp in sends:
            cp.wait_send()

        tot = sm_all[0]
        for q in range(1, N_DEV):
            tot = tot + sm_all[q]
        sum_ref[...] = tot

        cs = c_ref[...].reshape(N_DEV * 8, d)
        sc = (cs * jax.nn.sigmoid(cs)).astype(BF)
        for n0 in range(0, cols, chunk):
            dmv = dm_all[:, :, n0:n0 + chunk].reshape(N_DEV * 8, chunk).astype(BF)
            gw_ref[:, n0:n0 + chunk] = _tn(sc, dmv)

    return _whole_call(
        body, "small_sync", (smalls, dmod_blk, c_all),
        [_sds(smalls.shape, F32), _sds((d, cols), F32)],
        [pltpu.VMEM((N_DEV,) + smalls.shape, F32), pltpu.VMEM((N_DEV, 8, cols), F32),
         pltpu.SemaphoreType.DMA((2 * (N_DEV - 1),)), pltpu.SemaphoreType.DMA((2 * (N_DEV - 1),))], carry=carry)


def _bucket_onehot():
    maps = np.stack([_bucket_map(dil).reshape(-1) for _, dil in DIL_CONFIGS])
    return (jnp.asarray(maps)[:, None, :] == jnp.arange(N_BUCKETS, dtype=jnp.int32)[None, :, None]).astype(BF)


def _dil_bias(rel_t, onehot):
    def body(r_ref, oh_ref, o_ref):
        rv = r_ref[...]
        hi = rv.astype(BF)
        lo = (rv - hi.astype(F32)).astype(BF)
        for c in range(len(DIL_CONFIGS)):
            o_ref[c] = _nn(hi, oh_ref[c]) + _nn(lo, oh_ref[c])

    return pl.pallas_call(body, name="dil_bias",
                          out_shape=_sds((len(DIL_CONFIGS), N_HEADS, BLOCK * 2 * BLOCK), F32),
                          compiler_params=_cp())(rel_t, onehot)


def _rowsum8(a):
    def body(a_ref, o_ref):
        o_ref[...] = jnp.sum(a_ref[...], axis=0, keepdims=True)

    return pl.pallas_call(body, name="rowsum8", out_shape=_sds((1, a.shape[1]), F32), compiler_params=_cp())(a)


def _local_step(x, mod, target, w, gains, rel_bias, place=None):
    nb, seq, d = x.shape
    t = nb * seq
    dist = place is not None
    core = place[0:1] if dist else None
    x0 = x.reshape(t, d)
    tgt = target.reshape(t, d)
    md = [mod[:, i:i + 1, :] for i in range(N_MOD)]
    sh1, sc1, gt1, sh2, sc2, gt2, sh3, sc3, gt3 = md
    g1, g2, g3 = gains["g_ffn1"], gains["g_mix"], gains["g_ffn2"]
    ones_g = _group_ones()

    def partial_sums(grads, lands):
        return [_add_halves(core, g, l) for g, l in zip(grads, lands)]

    def chip_sums(parts, lands):
        return [_sum_chips(place, p, l, relative=True) for p, l in zip(parts, lands)]

    gu1 = w["gu1"]
    res = _ffn_up(x0, g1, sc1, sh1, gu1, seq,
                  carry=_join([_ag_weights([w["d1"]], relative=True), _ag_weights([w["win"]])]) if dist else None)
    h1, a1, u1, s1 = res[:4]
    wd1, w_in = res[4:] if dist else (w["d1"], w["win"])
    f1, x1 = _ffn_down(s1, wd1, x0, gt1, seq, 0.5)

    res = _qkv_proj(x1, g2, sc2, sh2, w_in, seq, carry=_ag_weights([w["wout"]]) if dist else None)
    h2, qkv6, qkv_r4, qkv_r16 = res[:4]
    w_out2 = (res[4] if dist else w["wout"]).reshape(2 * D_GRP, d)
    qkv6b = qkv6.reshape(6, nb, seq, D_GRP)
    res = _sb_fwd(qkv6b, gains["g_sb_out"], nb, seq,
                  carry=_ag_weights([w["gu2"], w["d2"]], relative=True) if dist else None)
    o_sb, on_sb = res[:2]
    wgu2, wd2 = res[2:] if dist else (w["gu2"], w["d2"])
    onehot = _bucket_onehot()
    bias = _dil_bias(rel_bias.T, onehot).reshape(len(DIL_CONFIGS), N_HEADS * BLOCK, 2 * BLOCK)
    o_cs, l_cs = [], []
    qkv_rs = [(qkv6b, 3), (qkv_r4, 0), (qkv_r16, 0)]
    for ci, (_, dil) in enumerate(DIL_CONFIGS):
        sub = seq // dil
        arr, base = qkv_rs[ci]
        arr = arr.reshape(base + 3, nb, sub, dil * D_GRP)
        qkv_rs[ci] = (arr, base)
        o_c, l_c = _dil_fwd(arr, base, bias[ci], nb, sub, dil)
        o_cs.append(o_c.reshape(t // dil, dil * D_GRP))
        l_cs.append(l_c.reshape(t // dil, dil * D_GRP))
    o_dil, on_dil = _dil_comb(o_cs, l_cs, gains["g_dil_out"])
    tmix, x2 = _mix_out(on_sb.reshape(t, D_GRP), on_dil, w_out2, x1, gt2, seq)

    h3, a3, u3, s3 = _ffn_up(x2, g3, sc3, sh3, wgu2, seq)
    f3, dx3, dg_final, loss = _ffn_down_loss(s3, wd2, x2, gt3, seq, 0.5, gains["g_final"], tgt)

    da3, du3, df3, dgt3, dx2, dsh3, dsc3, dg3 = _ffn_bwd_x(dx3, gt3, f3, wd2, a3, u3, wgu2, x2, g3, sc3, seq, 0.5)
    grads2 = [_ffn_bwd_w(h3, da3, du3, s3, df3)]

    res = _mix_bwd_out(
        dx2, gt2, tmix, w_out2, o_sb.reshape(t, D_GRP), o_dil, on_sb.reshape(t, D_GRP), on_dil,
        gains["g_sb_out"], gains["g_dil_out"], ones_g, seq, carry=_rs_d2d(grads2) if dist else None)
    do_sb, do_dil, dgt2, dg_sb, dg_dil, dw_out = res[:6]
    parts2 = partial_sums(grads2, res[6:]) if dist else None
    dw_out = dw_out.reshape(N_CHIPS, 1, 2 * D_GRP // N_CHIPS, d)
    res = _sb_bwd(qkv6b, do_sb.reshape(nb, seq, D_GRP), nb, seq,
                  carry=_rs_ici(parts2, relative=True) if dist else None)
    dqkv6 = res[0]
    halves2 = chip_sums(parts2, res[1:]) if dist else None
    dcs = _dil_comb_bwd(do_dil, o_cs, l_cs)
    dsum, a_tiles = [], []
    for ci, (_, dil) in enumerate(DIL_CONFIGS):
        sub = seq // dil
        do_c = dcs[ci].reshape(nb, sub, dil * D_GRP)
        dd_c = dcs[3 + ci].reshape(nb, sub, dil * D_GRP)
        res = _dil_bwd(qkv_rs[ci][0], qkv_rs[ci][1], bias[ci], do_c, dd_c, nb, sub, dil)
        dsum.append(res[0].reshape(3, t // dil, dil * D_GRP))
        a_tiles.append(res[1].reshape(N_HEADS, BLOCK * 2 * BLOCK))
    dqkv6 = _dqkv_dil_sum(dsum, dqkv6.reshape(6, t, D_GRP))
    drel = _relbias_grad(jnp.stack(a_tiles), onehot)
    dx1, dsh2, dsc2, dg2 = _mix_bwd_dh(dqkv6, w_in, x1, g2, sc2, dx2, seq)

    da1, du1, df1, dgt1 = _ffn_bwd_ds(dx1, gt1, f1, wd1, a1, u1, seq, 0.5)
    grads1 = [_ffn_bwd_w(h1, da1, du1, s1, df1)]
    res = _dw_in(h2, dqkv6, carry=_join([_rs_d2d(grads1), _rs_final(halves2)]) if dist else None)
    grads_m = [res[0], dw_out]
    parts1 = partial_sums(grads1, res[1:2]) if dist else None
    if dist:
        grads2 = res[2:3]
    res = _ffn_bwd_dh(da1, du1, gu1, x0, g1, sc1, dx1, seq,
                      carry=_join([_rs_ici(parts1, relative=True), _rs_d2d(grads_m)]) if dist else None)
    dx0, dsh1, dsc1, dg1 = res[:4]
    pending = None
    if dist:
        pending = (chip_sums(parts1, res[4:5]), partial_sums(grads_m, res[5:7]))

    dmod = jnp.concatenate([dsh1, dsc1, dgt1, dsh2, dsc2, dgt2, dsh3, dsc3, dgt3], axis=1)
    return dict(grad_x=dx0.reshape(nb, seq, d), loss=loss[0, 0], dmod=dmod.reshape(nb, N_MOD * d),
                dffn1=grads1[0], dffn2=grads2[0], dwin=grads_m[0], dwout=grads_m[1], pending=pending,
                dg_ffn1=dg1, dg_mix=dg2, dg_ffn2=dg3, dg_final=dg_final, dg_sb=dg_sb, dg_dil=dg_dil,
                drel=drel.T)


_SMALL_ORDER = (("b_ada", N_MOD * 1024), ("g_ffn1", 1024), ("g_mix", 1024), ("g_ffn2", 1024), ("g_final", 1024),
                ("g_sb_out", D_GRP), ("g_dil_out", D_GRP), ("rel_bias", N_BUCKETS * N_HEADS))


def _pack_small(parts, extra=None):
    flat = [parts[name].reshape(-1).astype(F32) for name, _ in _SMALL_ORDER]
    used = sum(sz for _, sz in _SMALL_ORDER)
    pad = SMALL_ROWS * 128 - used
    tail = jnp.zeros((pad,), F32)
    if extra is not None:
        tail = tail.at[0].set(extra)
    return jnp.concatenate(flat + [tail]).reshape(SMALL_ROWS, 128)


def _unpack_small(packed, shapes):
    flat = packed.reshape(-1)
    out, off = {}, 0
    for name, sz in _SMALL_ORDER:
        out[name] = flat[off:off + sz].reshape(shapes[name])
        off += sz
    return out, flat[off]


def kernel(x, c, w_ada, b_ada, g_ffn1, w1_gate, w1_up, w1_down, g_mix, w_in, g_sb_out, g_dil_out, w_out, rel_bias, g_ffn2, w2_gate, w2_up, w2_down, g_final, loss_target, m_w_ada, m_b_ada, m_g_ffn1, m_w1_gate, m_w1_up, m_w1_down, m_g_mix, m_w_in, m_g_sb_out, m_g_dil_out, m_w_out, m_rel_bias, m_g_ffn2, m_w2_gate, m_w2_up, m_w2_down, m_g_final, v_w_ada, v_b_ada, v_g_ffn1, v_w1_gate, v_w1_up, v_w1_down, v_g_mix, v_w_in, v_g_sb_out, v_g_dil_out, v_w_out, v_rel_bias, v_g_ffn2, v_w2_gate, v_w2_up, v_w2_down, v_g_final):
    nb, seq, d = x.shape
    xi, yi, ci = lax.axis_index("x"), lax.axis_index("y"), lax.axis_index("c")
    chip = 2 * xi + yi
    ada_cols = w_ada.shape[-1]

    c_pad = jnp.zeros((8, d), F32).at[:nb].set(c)
    b_shard = lax.dynamic_slice(b_ada, (0, chip * ada_cols), (1, ada_cols))
    shards = dict(gu1=jnp.stack([w1_gate[0], w1_up[0]]), d1=w1_down, win=w_in, wout=w_out,
                  gu2=jnp.stack([w2_gate[0], w2_up[0]]), d2=w2_down)
    bufs = {k: lax.dynamic_update_slice(lax.empty((N_CHIPS,) + s.shape, BF), s.astype(BF)[None],
                                        (chip if k in ("win", "wout") else 0, 0, 0, 0))
            for k, s in shards.items()}
    c_all, mod_blk, bufs["gu1"] = _ada_fwd(c_pad, w_ada[0], b_shard,
                                           carry=_ag_weights([bufs["gu1"]], relative=True))
    mod = jnp.transpose(mod_blk[:, :nb, :], (1, 0, 2)).reshape(nb, N_MOD, d)

    gains = dict(g_ffn1=g_ffn1, g_mix=g_mix, g_ffn2=g_ffn2, g_final=g_final.reshape(1, d),
                 g_sb_out=g_sb_out.reshape(1, D_GRP), g_dil_out=g_dil_out.reshape(1, D_GRP))
    place = jnp.stack([ci, chip]).astype(jnp.int32)
    r = _local_step(x, mod, loss_target, bufs, gains, rel_bias, place)

    dmod = r["dmod"]
    dmod_pad = jnp.zeros((8, N_MOD * d), F32).at[:nb].set(dmod)
    dmod_blk = jnp.transpose(dmod_pad.reshape(8, N_CHIPS, ada_cols), (1, 0, 2))
    small_parts = dict(b_ada=_rowsum8(dmod_pad), g_ffn1=r["dg_ffn1"], g_mix=r["dg_mix"], g_ffn2=r["dg_ffn2"],
                       g_final=r["dg_final"], g_sb_out=r["dg_sb"], g_dil_out=r["dg_dil"], rel_bias=r["drel"])
    halves1, parts_m = r["pending"]
    res = _small_sync(_pack_small(small_parts, r["loss"]), dmod_blk, c_all,
                      carry=_join([_rs_final(halves1), _rs_ici(parts_m)]))
    small_sum, g_wada, gffn1 = res[:3]
    halves_m = [_sum_chips(place, p, l) for p, l in zip(parts_m, res[3:5])]
    gwin, gwout = _alone("rs_last", _rs_final(halves_m))
    gffn2 = r["dffn2"]

    small_w = dict(b_ada=b_ada, g_ffn1=g_ffn1, g_mix=g_mix, g_ffn2=g_ffn2, g_final=g_final,
                   g_sb_out=g_sb_out, g_dil_out=g_dil_out, rel_bias=rel_bias)
    small_m = dict(b_ada=m_b_ada, g_ffn1=m_g_ffn1, g_mix=m_g_mix, g_ffn2=m_g_ffn2, g_final=m_g_final,
                   g_sb_out=m_g_sb_out, g_dil_out=m_g_dil_out, rel_bias=m_rel_bias)
    small_v = dict(b_ada=v_b_ada, g_ffn1=v_g_ffn1, g_mix=v_g_mix, g_ffn2=v_g_ffn2, g_final=v_g_final,
                   g_sb_out=v_g_sb_out, g_dil_out=v_g_dil_out, rel_bias=v_rel_bias)
    shapes = {k: v.shape for k, v in small_w.items()}
    sg, sd, sm, sv = _adamw(_pack_small(small_w), small_sum.reshape(1, SMALL_ROWS, 128), 0,
                            _pack_small(small_m), _pack_small(small_v))
    sg, loss = _unpack_small(sg, shapes)
    sd, _ = _unpack_small(sd, shapes)
    sm, _ = _unpack_small(sm, shapes)
    sv, _ = _unpack_small(sv, shapes)

    big = {}

    def upd(name, w, g_arr, sel, m, v, transposed=False):
        swap = (lambda a: jnp.swapaxes(a, -1, -2)) if transposed else (lambda a: a)
        w2, m2, v2 = [swap(a)[0] for a in (w, m, v)]
        big[name] = [swap(a[None]) for a in _adamw(w2, g_arr, sel, m2, v2)]

    upd("w_ada", w_ada, g_wada.reshape(1, d, ada_cols), 0, m_w_ada, v_w_ada)
    upd("w1_gate", w1_gate, gffn1, 0, m_w1_gate, v_w1_gate, transposed=True)
    upd("w1_up", w1_up, gffn1, 1, m_w1_up, v_w1_up, transposed=True)
    upd("w1_down", w1_down, gffn1, 2, m_w1_down, v_w1_down)
    upd("w_in", w_in, gwin, 0, m_w_in, v_w_in)
    upd("w_out", w_out, gwout, 0, m_w_out, v_w_out)
    upd("w2_gate", w2_gate, gffn2, 0, m_w2_gate, v_w2_gate, transposed=True)
    upd("w2_up", w2_up, gffn2, 1, m_w2_up, v_w2_up, transposed=True)
    upd("w2_down", w2_down, gffn2, 2, m_w2_down, v_w2_down)

    names = ["w_ada", "b_ada", "g_ffn1", "w1_gate", "w1_up", "w1_down", "g_mix", "w_in", "g_sb_out", "g_dil_out",
             "w_out", "rel_bias", "g_ffn2", "w2_gate", "w2_up", "w2_down", "g_final"]
    outs = [loss, r["grad_x"]]
    for k, small in enumerate((sg, sd, sm, sv)):
        for name in names:
            outs.append(big[name][k] if name in big else small[name])
    return tuple(outs)
```

```python
import math

import numpy as np
import jax
import jax.numpy as jnp
from jax import lax
from jax.experimental import pallas as pl
from jax.experimental.pallas import tpu as pltpu

F32 = jnp.float32
BF = jnp.bfloat16
MESH = pl.DeviceIdType.MESH

HEAD_DIM = 64
N_HEADS = 8
D_GRP = N_HEADS * HEAD_DIM
DIL_CONFIGS = ((128, 1), (512, 4), (2048, 16))
N_STEPS = 128
BLOCK = 128
N_BUCKETS = 32
MAX_DISTANCE = 2048
N_MOD = 9
EPS = 1e-6
NEG_INF = -1e30
SCALE = HEAD_DIM ** -0.5

ADAM_LR = 0.001
ADAM_B1 = 0.9
ADAM_B2 = 0.999
ADAM_EPS = 1e-08
ADAM_WD = 0.01
ADAM_STEP = 10

N_CHIPS = 4
N_DEV = 8
VMEM_LIMIT = 56 * 1024 * 1024
TM = 512
TQ = 256
KB = 256
SMALL_ROWS = 120


def _cp(n_axes=0, **kw):
    sem = ("arbitrary",) * n_axes if n_axes else None
    return pltpu.CompilerParams(dimension_semantics=sem, vmem_limit_bytes=VMEM_LIMIT, **kw)


def _nn(a, b):
    return jnp.dot(a, b, preferred_element_type=F32)


def _nt(a, b):
    return lax.dot_general(a, b, (((1,), (1,)), ((), ())), preferred_element_type=F32)


def _tn(a, b):
    return lax.dot_general(a, b, (((0,), (0,)), ((), ())), preferred_element_type=F32)


def _twice(m):
    return jnp.concatenate([m, m], axis=0)


def _nn2(x, m2):
    hi = x.astype(BF)
    lo = (x - hi.astype(F32)).astype(BF)
    return _nn(jnp.concatenate([hi, lo], axis=1), m2)


def _softplus(z):
    return jnp.maximum(z, 0.0) + jnp.log(1.0 + jnp.exp(-jnp.abs(z)))


def _sds(shape, dtype):
    return jax.ShapeDtypeStruct(shape, dtype)


def _whole(a):
    nd = a.ndim
    return pl.BlockSpec(a.shape, lambda *_: (0,) * nd, pipeline_mode=pl.Buffered(1))


def _modnorm_bwd_tile(dh, xv, gv, scv, dxo):
    r = lax.rsqrt(jnp.mean(xv * xv, axis=-1, keepdims=True) + EPS)
    n = xv * r
    ng = n * gv
    dsh = jnp.sum(dh, axis=0, keepdims=True)
    dsc = jnp.sum(dh * ng, axis=0, keepdims=True)
    dy = dh * (1.0 + scv)
    dg = jnp.sum(dy * n, axis=0, keepdims=True)
    dn = dy * gv
    dx = dxo + r * (dn - n * jnp.mean(dn * n, axis=-1, keepdims=True))
    return dx, dsh, dsc, dg


def _acc_rows(ref, val, first):
    @pl.when(first)
    def _():
        ref[...] = val

    @pl.when(jnp.logical_not(first))
    def _():
        ref[...] += val


def _modnorm_tile(x_ref, g_ref, sc_ref, sh_ref):
    xv = x_ref[...]
    r = lax.rsqrt(jnp.mean(xv * xv, axis=-1, keepdims=True) + EPS)
    return (((xv * r) * g_ref[...]) * (1.0 + sc_ref[...]) + sh_ref[...]).astype(BF)


def _ffn_up(x, g, sc, sh, wgu, seq, carry=None):
    t, d = x.shape
    fs = wgu.shape[-1]
    per = seq // TM

    def body(x_ref, g_ref, sc_ref, sh_ref, w_ref, h_ref, p_ref, q_ref, s_ref):
        hv = _modnorm_tile(x_ref, g_ref, sc_ref, sh_ref)
        h_ref[...] = hv
        for j in range(N_CHIPS):
            a = _nn(hv, w_ref[j, 0])
            u = _nn(hv, w_ref[j, 1])
            sig = jax.nn.sigmoid(a)
            q = a * sig
            p_ref[j] = (u * (sig * (1.0 + a * (1.0 - sig)))).astype(BF)
            q_ref[j] = q.astype(BF)
            s_ref[j] = (q * u).astype(BF)

    row = pl.BlockSpec((TM, d), lambda m: (m, 0))
    ex = pl.BlockSpec((None, 1, d), lambda m: (m // per, 0, 0))
    blk = pl.BlockSpec((N_CHIPS, TM, fs), lambda m: (0, m, 0))
    return _call(
        body, "ffn_up", (t // TM,),
        [row, pl.BlockSpec((1, d), lambda m: (0, 0)), ex, ex, _whole(wgu)],
        [row, blk, blk, blk],
        [_sds((t, d), BF)] + [_sds((N_CHIPS, t, fs), BF)] * 3,
        (x, g, sc, sh, wgu), carry=carry)


def _ffn_down(s, wd, x, gt, seq, coef, carry=None):
    _, t, fs = s.shape
    d = x.shape[-1]
    per = seq // TM

    def body(s_ref, w_ref, x_ref, gt_ref, f_ref, xo_ref):
        f = _nn(s_ref[0], w_ref[0, 0])
        for j in range(1, N_CHIPS):
            f = f + _nn(s_ref[j], w_ref[j, 0])
        f_ref[...] = f.astype(BF)
        xo_ref[...] = x_ref[...] + (coef * gt_ref[...]) * f

    row = pl.BlockSpec((TM, d), lambda m: (m, 0))
    return _call(
        body, "ffn_down", (t // TM,),
        [pl.BlockSpec((N_CHIPS, TM, fs), lambda m: (0, m, 0)), _whole(wd), row,
         pl.BlockSpec((None, 1, d), lambda m: (m // per, 0, 0))],
        [row, row], [_sds((t, d), BF), _sds((t, d), F32)], (s, wd, x, gt), carry=carry)


def _ffn_bwd_ds(dxo, gt, f, wd, p, q, seq, coef, carry=None):
    t, d = dxo.shape
    fs = p.shape[-1]
    per = seq // TM
    nb = t // seq

    def body(dxo_ref, gt_ref, f_ref, w_ref, p_ref, q_ref, da_ref, du_ref, df_ref, dgt_ref):
        m = pl.program_id(0)
        dxv = dxo_ref[...]
        df = ((coef * gt_ref[...]) * dxv).astype(BF)
        df_ref[...] = df
        _acc_rows(dgt_ref, coef * jnp.sum(dxv * f_ref[...].astype(F32), axis=0, keepdims=True), m % per == 0)
        for j in range(N_CHIPS):
            ds = _nt(df, w_ref[j, 0])
            da_ref[j] = (ds * p_ref[j].astype(F32)).astype(BF)
            du_ref[j] = (ds * q_ref[j].astype(F32)).astype(BF)

    row = pl.BlockSpec((TM, d), lambda m: (m, 0))
    blk = pl.BlockSpec((N_CHIPS, TM, fs), lambda m: (0, m, 0))
    ex = pl.BlockSpec((None, 1, d), lambda m: (m // per, 0, 0))
    return _call(
        body, "ffn_bwd_ds", (t // TM,),
        [row, ex, row, _whole(wd), blk, blk],
        [blk, blk, row, ex],
        [_sds((N_CHIPS, t, fs), BF), _sds((N_CHIPS, t, fs), BF), _sds((t, d), BF), _sds((nb, 1, d), F32)],
        (dxo, gt, f, wd, p, q), carry=carry)


TM_X = 256


def _ffn_bwd_x(dxo, gt, f, wd, p, q, wgu, x, g, sc, seq, coef):
    t, d = dxo.shape
    fs = p.shape[-1]
    per = seq // TM_X
    nb = t // seq

    def body(dxo_ref, gt_ref, f_ref, wd_ref, p_ref, q_ref, w_ref, x_ref, g_ref, sc_ref,
             da_ref, du_ref, df_ref, dgt_ref, dx_ref, dsh_ref, dsc_ref, dg_ref):
        m = pl.program_id(0)
        dxv = dxo_ref[...]
        df = ((coef * gt_ref[...]) * dxv).astype(BF)
        df_ref[...] = df
        _acc_rows(dgt_ref, coef * jnp.sum(dxv * f_ref[...].astype(F32), axis=0, keepdims=True), m % per == 0)
        dh = None
        for j in range(N_CHIPS):
            ds = _nt(df, wd_ref[j, 0])
            da = (ds * p_ref[j].astype(F32)).astype(BF)
            du = (ds * q_ref[j].astype(F32)).astype(BF)
            da_ref[j] = da
            du_ref[j] = du
            part = _nt(da, w_ref[j, 0]) + _nt(du, w_ref[j, 1])
            dh = part if dh is None else dh + part
        dx, dsh, dsc, dg = _modnorm_bwd_tile(dh, x_ref[...], g_ref[...], sc_ref[...], dxv)
        dx_ref[...] = dx
        _acc_rows(dsh_ref, dsh, m % per == 0)
        _acc_rows(dsc_ref, dsc, m % per == 0)
        _acc_rows(dg_ref, dg, m == 0)

    row = pl.BlockSpec((TM_X, d), lambda m: (m, 0))
    blk = pl.BlockSpec((N_CHIPS, TM_X, fs), lambda m: (0, m, 0))
    ex = pl.BlockSpec((None, 1, d), lambda m: (m // per, 0, 0))
    vec = pl.BlockSpec((1, d), lambda m: (0, 0))
    exs = _sds((nb, 1, d), F32)
    return pl.pallas_call(
        body, name="ffn_bwd_x", grid=(t // TM_X,),
        in_specs=[row, ex, row, _whole(wd), blk, blk, _whole(wgu), row, vec, ex],
        out_specs=[blk, blk, row, ex, row, ex, ex, vec],
        out_shape=[_sds((N_CHIPS, t, fs), BF), _sds((N_CHIPS, t, fs), BF), _sds((t, d), BF), exs,
                   _sds((t, d), F32), exs, exs, _sds((1, d), F32)],
        compiler_params=_cp(1))(dxo, gt, f, wd, p, q, wgu, x, g, sc)


TK_W = 1024


def _ffn_bwd_w(h, da, du, s, df):
    t, d = h.shape
    fs = da.shape[-1]

    def body(h_ref, da_ref, du_ref, s_ref, df_ref, o_ref):
        kt = pl.program_id(1)
        hv = h_ref[...]
        parts = (_tn(da_ref[...], hv), _tn(du_ref[...], hv), _tn(s_ref[...], df_ref[...]))

        @pl.when(kt == 0)
        def _():
            for i, p in enumerate(parts):
                o_ref[i] = p

        @pl.when(kt != 0)
        def _():
            for i, p in enumerate(parts):
                o_ref[i] += p

    row = pl.BlockSpec((TK_W, d), lambda j, kt: (kt, 0))
    blk = pl.BlockSpec((None, TK_W, fs), lambda j, kt: (j, kt, 0))
    return pl.pallas_call(
        body, name="ffn_bwd_w", grid=(N_CHIPS, t // TK_W),
        in_specs=[row, blk, blk, blk, row],
        out_specs=pl.BlockSpec((None, 3, fs, d), lambda j, kt: (j, 0, 0, 0)),
        out_shape=_sds((N_CHIPS, 3, fs, d), F32),
        compiler_params=_cp(2))(h, da, du, s, df)


def _ffn_bwd_dh(da, du, wgu, x, g, sc, dxo, seq, carry=None):
    _, t, fs = da.shape
    d = x.shape[-1]
    per = seq // TM
    nb = t // seq

    def body(da_ref, du_ref, w_ref, x_ref, g_ref, sc_ref, dxo_ref, dx_ref, dsh_ref, dsc_ref, dg_ref):
        m = pl.program_id(0)
        dh = _nt(da_ref[0], w_ref[0, 0]) + _nt(du_ref[0], w_ref[0, 1])
        for j in range(1, N_CHIPS):
            dh = dh + _nt(da_ref[j], w_ref[j, 0]) + _nt(du_ref[j], w_ref[j, 1])
        dx, dsh, dsc, dg = _modnorm_bwd_tile(dh, x_ref[...], g_ref[...], sc_ref[...], dxo_ref[...])
        dx_ref[...] = dx
        _acc_rows(dsh_ref, dsh, m % per == 0)
        _acc_rows(dsc_ref, dsc, m % per == 0)
        _acc_rows(dg_ref, dg, m == 0)

    row = pl.BlockSpec((TM, d), lambda m: (m, 0))
    blk = pl.BlockSpec((N_CHIPS, TM, fs), lambda m: (0, m, 0))
    ex = pl.BlockSpec((None, 1, d), lambda m: (m // per, 0, 0))
    vec = pl.BlockSpec((1, d), lambda m: (0, 0))
    return _call(
        body, "ffn_bwd_dh", (t // TM,),
        [blk, blk, _whole(wgu), row, vec, ex, row],
        [row, ex, ex, vec],
        [_sds((t, d), F32), _sds((nb, 1, d), F32), _sds((nb, 1, d), F32), _sds((1, d), F32)],
        (da, du, wgu, x, g, sc, dxo), carry=carry)


def _qkv_proj(x, g, sc, sh, w_in, seq, carry=None):
    t, d = x.shape
    wc = w_in.shape[-1]
    per = seq // TM

    dils = [dil for _, dil in DIL_CONFIGS if dil > 1]

    def body(x_ref, g_ref, sc_ref, sh_ref, w_ref, h_ref, o_ref, *rest):
        res_refs, buf = rest[:len(dils)], rest[len(dils)]
        hv = _modnorm_tile(x_ref, g_ref, sc_ref, sh_ref)
        h_ref[...] = hv
        for j in range(N_CHIPS):
            rf = _nn(hv, w_ref[j, 0])
            r = rf.astype(BF)
            for a, lc, off, width in _col_pieces(j, wc):
                o_ref[a, :, lc:lc + width] = r[:, off:off + width]
                if a < 3:
                    continue
                for c0 in range(0, width, 128):
                    cg = (lc + c0) // 128
                    buf[...] = rf[:, off + c0:off + c0 + 128]
                    for ref, dil in zip(res_refs, dils):
                        for rr in range(dil):
                            ref[a - 3, :, rr * D_GRP + cg * 128:rr * D_GRP + (cg + 1) * 128] = (
                                buf[pl.ds(rr, TM // dil, stride=dil), :].astype(BF))

    row = pl.BlockSpec((TM, d), lambda m: (m, 0))
    ex = pl.BlockSpec((None, 1, d), lambda m: (m // per, 0, 0))
    return _call(
        body, "qkv_proj", (t // TM,),
        [row, pl.BlockSpec((1, d), lambda m: (0, 0)), ex, ex, _whole(w_in)],
        [row, pl.BlockSpec((6, TM, D_GRP), lambda m: (0, m, 0))]
        + [pl.BlockSpec((3, TM // dil, dil * D_GRP), lambda m: (0, m, 0)) for dil in dils],
        [_sds((t, d), BF), _sds((6, t, D_GRP), BF)] + [_sds((3, t // dil, dil * D_GRP), BF) for dil in dils],
        (x, g, sc, sh, w_in), scratch=[pltpu.VMEM((TM, 128), F32)], carry=carry)


def _col_pieces(j, wc):
    out, off = [], 0
    while off < wc:
        a, lc = divmod(j * wc + off, D_GRP)
        width = min(D_GRP - lc, wc - off)
        out.append((a, lc, off, width))
        off += width
    return out


def _chip_cols(g6_ref, j, wc):
    return jnp.concatenate([g6_ref[a, :, lc:lc + width] for a, lc, _, width in _col_pieces(j, wc)], axis=1)


def _mix_out(on_sb, on_dil, w_out, x, gt, seq):
    t, d = x.shape
    per = seq // TM

    def body(a_ref, b_ref, w_ref, x_ref, gt_ref, t_ref, xo_ref):
        tv = _nn(a_ref[...], w_ref[0:D_GRP, :]) + _nn(b_ref[...], w_ref[D_GRP:2 * D_GRP, :])
        t_ref[...] = tv.astype(BF)
        xo_ref[...] = x_ref[...] + gt_ref[...] * tv

    row = pl.BlockSpec((TM, d), lambda m: (m, 0))
    half = pl.BlockSpec((TM, D_GRP), lambda m: (m, 0))
    return pl.pallas_call(
        body, name="mix_out", grid=(t // TM,),
        in_specs=[half, half, pl.BlockSpec((2 * D_GRP, d), lambda m: (0, 0)), row,
                  pl.BlockSpec((None, 1, d), lambda m: (m // per, 0, 0))],
        out_specs=[row, row],
        out_shape=[_sds((t, d), BF), _sds((t, d), F32)],
        compiler_params=_cp(1))(on_sb, on_dil, w_out, x, gt)


def _sb_masks():
    lane = lax.broadcasted_iota(jnp.int32, (1, 2 * HEAD_DIM), 1)
    hm0 = lane < HEAD_DIM
    rel = lax.broadcasted_iota(jnp.int32, (TQ, KB), 0) - lax.broadcasted_iota(jnp.int32, (TQ, KB), 1)
    kr = lax.broadcasted_iota(jnp.int32, (KB, KB), 0)
    kc = lax.broadcasted_iota(jnp.int32, (KB, KB), 1)
    return hm0, rel, kr, kc


def _headnorm_pair(o, gv, hm0):
    o2 = o * o
    ms0 = jnp.sum(jnp.where(hm0, o2, 0.0), axis=-1, keepdims=True) * (1.0 / HEAD_DIM)
    ms1 = jnp.sum(jnp.where(hm0, 0.0, o2), axis=-1, keepdims=True) * (1.0 / HEAD_DIM)
    r = jnp.where(hm0, lax.rsqrt(ms0 + EPS), lax.rsqrt(ms1 + EPS))
    return (o * r) * gv


SB_DEAD = -104.0


def _alive(c_l):
    return (jnp.max(c_l) > SB_DEAD).astype(jnp.int32)


def _sb_fwd(qkv6, g_sb, nb, seq, carry=None):
    nq = seq // TQ

    def body(q_ref, k_ref, v_ref, g_ref, o_ref, on_ref):
        qi = pl.program_id(2)
        hm0, rel, kr, kc = _sb_masks()
        upper = _twice((kr > kc).astype(BF))
        heads = _dil_masks()[0]
        qs = _stack_heads(q_ref[...] * SCALE, heads)
        causal2 = jnp.concatenate([rel] * GRP_HEADS, axis=0) > 0

        def block(kj, causal, c_l, acc):
            ks = pl.multiple_of(kj * KB, KB)
            z = _nt(qs, k_ref[pl.ds(ks, KB), :])
            sp = _softplus(z)
            spm = sp if causal is None else jnp.where(causal, sp, 0.0)
            suf = _nn2(spm, upper)
            w = jnp.exp((z - sp) + (c_l - suf))
            if causal is not None:
                w = jnp.where(causal, w, 0.0)
            return c_l - (suf[:, 0:1] + spm[:, 0:1]), acc + _nn(w.astype(BF), v_ref[pl.ds(ks, KB), :])

        c_l, acc = block(qi, causal2, jnp.zeros((GRP_HEADS * TQ, 1), F32), jnp.zeros((GRP_HEADS * TQ, GRP_W), F32))

        def cond(carry):
            return jnp.logical_and(carry[0] <= qi, carry[1] > 0)

        def kbody(carry):
            it, _, c_l, acc = carry
            c_l, acc = block(qi - it, None, c_l, acc)
            return it + 1, _alive(c_l), c_l, acc

        acc = lax.while_loop(cond, kbody, (jnp.int32(1), _alive(c_l), c_l, acc))[3]
        o = _unstack_heads(acc, heads, TQ)
        o_ref[...] = o.astype(BF)
        gv = g_ref[...]
        for half in range(GRP_W // 128):
            lanes = slice(half * 128, (half + 1) * 128)
            on_ref[:, lanes] = _headnorm_pair(o[:, lanes], gv[:, lanes], hm0).astype(BF)

    w = GRP_W
    full = lambda i: pl.BlockSpec((None, None, seq, w), lambda b, hp, q: (i, b, 0, hp))
    qblk = pl.BlockSpec((None, None, TQ, w), lambda b, hp, q: (0, b, q, hp))
    oblk = pl.BlockSpec((None, TQ, w), lambda b, hp, q: (b, q, hp))
    return _call(
        body, "sb_fwd", (nb, N_HEADS // GRP_HEADS, nq),
        [qblk, full(1), full(2), pl.BlockSpec((1, w), lambda b, hp, q: (0, hp))],
        [oblk, oblk],
        [_sds((nb, seq, D_GRP), BF), _sds((nb, seq, D_GRP), BF)],
        (qkv6, qkv6, qkv6, g_sb), carry=carry)


def _sb_bwd(qkv6, do, nb, seq, carry=None):
    nq = seq // TQ
    nk = seq // KB

    def body(q_ref, k_ref, v_ref, do_ref, out_ref, dk_acc, dv_acc, g_st, s_st):
        qi = pl.program_id(2)
        hm0, rel, kr, kc = _sb_masks()
        upper = _twice((kr > kc).astype(BF))
        lower = (kr < kc).astype(BF)

        @pl.when(qi == 0)
        def _():
            dk_acc[...] = jnp.zeros_like(dk_acc)
            dv_acc[...] = jnp.zeros_like(dv_acc)

        heads = _dil_masks()[0]
        qs = _stack_heads(q_ref[...] * SCALE, heads)
        dos = _stack_heads(do_ref[...], heads)
        causal2 = jnp.concatenate([rel] * GRP_HEADS, axis=0) > 0

        def weights(kj, causal, c_l):
            ks = pl.multiple_of(kj * KB, KB)
            vb = v_ref[pl.ds(ks, KB), :]
            z = _nt(qs, k_ref[pl.ds(ks, KB), :])
            sp = _softplus(z)
            spm = sp if causal is None else jnp.where(causal, sp, 0.0)
            suf = _nn2(spm, upper)
            lsz = z - sp
            w = jnp.exp(lsz + (c_l - suf))
            if causal is not None:
                w = jnp.where(causal, w, 0.0)
            g_st[kj] = w * _nt(dos, vb)
            s_st[kj] = jnp.exp(lsz)
            dv_acc[pl.ds(ks, KB), :] += _tn(w.astype(BF), dos)
            return c_l - (suf[:, 0:1] + spm[:, 0:1])

        zc = jnp.zeros((GRP_HEADS * TQ, 1), F32)
        c_l = weights(qi, causal2, zc)

        def acond(carry):
            return jnp.logical_and(carry[0] <= qi, carry[1] > 0)

        def abody(carry):
            c_l = weights(qi - carry[0], None, carry[2])
            return carry[0] + 1, _alive(c_l), c_l

        n_used = lax.while_loop(acond, abody, (jnp.int32(1), _alive(c_l), c_l))[0]

        def grads(kj, causal, c_g, dq):
            ks = pl.multiple_of(kj * KB, KB)
            kb = k_ref[pl.ds(ks, KB), :]
            g = g_st[kj]
            sig = s_st[kj]
            pre = _nn(g.astype(BF), lower)
            dz = g * (1.0 - sig) - sig * (pre + c_g)
            if causal is not None:
                dz = jnp.where(causal, dz, 0.0)
            dzb = dz.astype(BF)
            dk_acc[pl.ds(ks, KB), :] += _tn(dzb, qs)
            return c_g + (pre[:, KB - 1:KB] + g[:, KB - 1:KB]), dq + _nn(dzb, kb)

        c_g, dq = lax.fori_loop(qi - n_used + 1, qi, lambda kj, cr: grads(kj, None, *cr),
                                (zc, jnp.zeros((GRP_HEADS * TQ, GRP_W), F32)))
        _, dq = grads(qi, causal2, c_g, dq)
        dq = _unstack_heads(dq, heads, TQ) * SCALE
        out_ref[0, pl.ds(pl.multiple_of(qi * TQ, TQ), TQ), :] = dq.astype(BF)

        @pl.when(qi == nq - 1)
        def _():
            out_ref[1] = dk_acc[...].astype(BF)
            out_ref[2] = dv_acc[...].astype(BF)

    w = GRP_W
    full = lambda i: pl.BlockSpec((None, None, seq, w), lambda b, hp, q: (i, b, 0, hp))
    qblk = pl.BlockSpec((None, None, TQ, w), lambda b, hp, q: (0, b, q, hp))
    oblk = pl.BlockSpec((None, TQ, w), lambda b, hp, q: (b, q, hp))
    return _call(
        body, "sb_bwd", (nb, N_HEADS // GRP_HEADS, nq),
        [qblk, full(1), full(2), oblk],
        [pl.BlockSpec((3, None, seq, w), lambda b, hp, q: (0, b, 0, hp))],
        [_sds((6, nb, seq, D_GRP), BF)], (qkv6, qkv6, qkv6, do),
        scratch=[pltpu.VMEM((seq, w), F32), pltpu.VMEM((seq, w), F32),
                 pltpu.VMEM((nk, GRP_HEADS * TQ, KB), F32), pltpu.VMEM((nk, GRP_HEADS * TQ, KB), F32)],
        carry=carry)


def _t5_bucket(n):
    max_exact = N_BUCKETS // 2
    nf = np.maximum(n, 1).astype(np.float32)
    large = max_exact + (np.log(nf / max_exact) / math.log(MAX_DISTANCE / max_exact)
                         * (N_BUCKETS - max_exact)).astype(np.int32)
    large = np.minimum(large, N_BUCKETS - 1)
    return np.where(n < max_exact, n, large).astype(np.int32)


def _bucket_map(dilation):
    step = BLOCK + np.arange(BLOCK)[:, None] - np.arange(2 * BLOCK)[None, :]
    return _t5_bucket(np.clip(step, 0, N_STEPS) * dilation)


GRP_HEADS = 4
GRP_W = GRP_HEADS * HEAD_DIM


def _dil_masks():
    lane = lax.broadcasted_iota(jnp.int32, (1, GRP_W), 1)
    heads = [jnp.logical_and(lane >= HEAD_DIM * i, lane < HEAD_DIM * (i + 1)) for i in range(GRP_HEADS)]
    iq = jnp.bitwise_and(lax.broadcasted_iota(jnp.int32, (GRP_HEADS * BLOCK, BLOCK), 0), BLOCK - 1)
    ik = lax.broadcasted_iota(jnp.int32, (GRP_HEADS * BLOCK, BLOCK), 1)
    return heads, ik <= iq, ik >= iq


def _stack_heads(x, heads):
    zero = jnp.zeros_like(x)
    return jnp.concatenate([jnp.where(hm, x, zero) for hm in heads], axis=0)


def _unstack_heads(xs, heads, rows=BLOCK):
    out = xs[0:rows]
    for i in range(1, GRP_HEADS):
        out = jnp.where(heads[i], xs[i * rows:(i + 1) * rows], out)
    return out


def _dil_rows(n):
    rs = pl.multiple_of(n * BLOCK, BLOCK)
    ps = pl.multiple_of(jnp.maximum(n - 1, 0) * BLOCK, BLOCK)
    return pl.ds(rs, BLOCK), pl.ds(ps, BLOCK)


def _dil_probs(qs, kc, kp, b_ref, gi, valid_c, valid_p):
    rows = slice(gi * GRP_HEADS * BLOCK, (gi + 1) * GRP_HEADS * BLOCK)
    zc = _nt(qs, kc) * SCALE + b_ref[rows, BLOCK:2 * BLOCK]
    zp = _nt(qs, kp) * SCALE + b_ref[rows, 0:BLOCK]
    zc = jnp.where(valid_c, zc, NEG_INF)
    zp = jnp.where(valid_p, zp, NEG_INF)
    m = jnp.maximum(jnp.max(zc, axis=-1, keepdims=True), jnp.max(zp, axis=-1, keepdims=True))
    ec = jnp.exp(zc - m)
    ep = jnp.exp(zp - m)
    den = jnp.sum(ec, axis=-1, keepdims=True) + jnp.sum(ep, axis=-1, keepdims=True)
    return ec, ep, den, m


def _dil_fwd(qkv6r, base, bias, nb, sub_len, dilation):
    n_blk = sub_len // BLOCK

    def body(q_ref, k_ref, v_ref, b_ref, o_ref, l_ref):
        heads, valid_c, valid_p0 = _dil_masks()

        def nbody(n, carry):
            cur, prev = _dil_rows(n)
            valid_p = jnp.logical_and(valid_p0, n > 0)
            for gi in range(N_HEADS // GRP_HEADS):
                lanes = slice(gi * GRP_W, (gi + 1) * GRP_W)
                qs = _stack_heads(q_ref[cur, lanes], heads)
                ec, ep, den, m = _dil_probs(qs, k_ref[cur, lanes], k_ref[prev, lanes], b_ref, gi, valid_c, valid_p)
                o = (_nn(ec.astype(BF), v_ref[cur, lanes]) + _nn(ep.astype(BF), v_ref[prev, lanes])) / den
                o_ref[cur, lanes] = _unstack_heads(o, heads).astype(BF)
                l_ref[cur, lanes] = _unstack_heads(jnp.broadcast_to(m + jnp.log(den), o.shape), heads)
            return carry

        lax.fori_loop(0, n_blk, nbody, 0)

    seqblk = lambda i: pl.BlockSpec((None, None, sub_len, D_GRP), lambda b, r: (i, b, 0, r))
    oblk = pl.BlockSpec((None, sub_len, D_GRP), lambda b, r: (b, 0, r))
    shp = _sds((nb, sub_len, dilation * D_GRP), F32)
    return pl.pallas_call(
        body, name="dil_fwd_%d" % dilation, grid=(nb, dilation),
        in_specs=[seqblk(base), seqblk(base + 1), seqblk(base + 2), _whole(bias)],
        out_specs=[oblk, oblk], out_shape=[_sds(shp.shape, BF), shp],
        compiler_params=_cp(2))(qkv6r, qkv6r, qkv6r, bias)


def _dil_bwd(qkv6r, base, bias, do_c, dd_c, nb, sub_len, dilation, carry=None):
    n_blk = sub_len // BLOCK

    def body(q_ref, k_ref, v_ref, b_ref, do_ref, dd_ref, out_ref, a_ref, dk_acc, dv_acc):
        heads, valid_c, valid_p0 = _dil_masks()
        first = jnp.logical_and(pl.program_id(0) == 0, pl.program_id(1) == 0)

        @pl.when(first)
        def _():
            a_ref[...] = jnp.zeros_like(a_ref)

        dk_acc[...] = jnp.zeros_like(dk_acc)
        dv_acc[...] = jnp.zeros_like(dv_acc)

        def nbody(n, carry):
            cur, prev = _dil_rows(n)
            valid_p = jnp.logical_and(valid_p0, n > 0)
            for gi in range(N_HEADS // GRP_HEADS):
                lanes = slice(gi * GRP_W, (gi + 1) * GRP_W)
                kc, kp = k_ref[cur, lanes], k_ref[prev, lanes]
                vc, vp = v_ref[cur, lanes], v_ref[prev, lanes]
                qs = _stack_heads(q_ref[cur, lanes], heads)
                dos = _stack_heads(do_ref[cur, lanes], heads).astype(BF)
                dds = jnp.sum(_stack_heads(dd_ref[cur, lanes], heads), axis=-1, keepdims=True) * (1.0 / HEAD_DIM)
                ec, ep, den, _ = _dil_probs(qs, kc, kp, b_ref, gi, valid_c, valid_p)
                inv = 1.0 / den
                pc = ec * inv
                pp = ep * inv
                dzc = pc * (_nt(dos, vc) + dds)
                dzp = pp * (_nt(dos, vp) + dds)
                rows = slice(gi * GRP_HEADS * BLOCK, (gi + 1) * GRP_HEADS * BLOCK)
                a_ref[rows, BLOCK:2 * BLOCK] += dzc
                a_ref[rows, 0:BLOCK] += dzp
                dzcb = (dzc * SCALE).astype(BF)
                dzpb = (dzp * SCALE).astype(BF)
                out_ref[0, cur, lanes] = _unstack_heads(_nn(dzcb, kc) + _nn(dzpb, kp), heads).astype(BF)
                dk_acc[cur, lanes] += _tn(dzcb, qs)
                dk_acc[prev, lanes] += _tn(dzpb, qs)
                dv_acc[cur, lanes] += _tn(pc.astype(BF), dos)
                dv_acc[prev, lanes] += _tn(pp.astype(BF), dos)
            return carry

        lax.fori_loop(0, n_blk, nbody, 0)
        out_ref[1] = dk_acc[...].astype(BF)
        out_ref[2] = dv_acc[...].astype(BF)

    seqblk = lambda i: pl.BlockSpec((None, None, sub_len, D_GRP), lambda b, r: (i, b, 0, r))
    oblk = pl.BlockSpec((None, sub_len, D_GRP), lambda b, r: (b, 0, r))
    return _call(
        body, "dil_bwd_%d" % dilation, (nb, dilation),
        [seqblk(base), seqblk(base + 1), seqblk(base + 2), _whole(bias), oblk, oblk],
        [pl.BlockSpec((3, None, sub_len, D_GRP), lambda b, r: (0, b, 0, r)),
         pl.BlockSpec((N_HEADS * BLOCK, 2 * BLOCK), lambda b, r: (0, 0))],
        [_sds((3, nb, sub_len, dilation * D_GRP), BF), _sds((N_HEADS * BLOCK, 2 * BLOCK), F32)],
        (qkv6r, qkv6r, qkv6r, bias, do_c, dd_c),
        scratch=[pltpu.VMEM((sub_len, D_GRP), F32)] * 2, carry=carry)


def _group_ones():
    idx = np.arange(D_GRP) // HEAD_DIM
    return jnp.asarray((idx[:, None] == idx[None, :]).astype(np.float32), dtype=BF)


def _dil_alphas(l1, l4, l16):
    mx = jnp.maximum(jnp.maximum(l1, l4), l16)
    e1 = jnp.exp(l1 - mx)
    e4 = jnp.exp(l4 - mx)
    e16 = jnp.exp(l16 - mx)
    den = e1 + e4 + e16
    return e1 / den, e4 / den, e16 / den


def _residue_spec(dil):
    return pl.BlockSpec((TM // dil, dil * D_GRP), lambda m: (m, 0))


def _from_residue(src, dil, cg, buf):
    if dil == 1:
        return src[:, cg * 128:(cg + 1) * 128].astype(F32)
    for r in range(dil):
        buf[pl.ds(r, TM // dil, stride=dil), :] = (
            src[:, r * D_GRP + cg * 128:r * D_GRP + (cg + 1) * 128].astype(F32))
    return buf[...]


def _to_residue(dst, dil, cg, buf, val):
    if dil == 1:
        dst[:, cg * 128:(cg + 1) * 128] = val.astype(dst.dtype)
        return
    buf[...] = val
    for r in range(dil):
        dst[:, r * D_GRP + cg * 128:r * D_GRP + (cg + 1) * 128] = (
            buf[pl.ds(r, TM // dil, stride=dil), :].astype(dst.dtype))


def _pair_sum(x, hm0):
    s0 = jnp.sum(jnp.where(hm0, x, 0.0), axis=-1, keepdims=True)
    s1 = jnp.sum(jnp.where(hm0, 0.0, x), axis=-1, keepdims=True)
    return jnp.where(hm0, s0, s1)


def _dil_comb(os, ls, g_dil):
    t = os[0].shape[0]
    dils = [dil for _, dil in DIL_CONFIGS]

    def body(o1, l1, o4, l4, o16, l16, g_ref, o_ref, on_ref, b0, b1, b2, b3):
        hm0 = lax.broadcasted_iota(jnp.int32, (1, 128), 1) < HEAD_DIM
        for cg in range(D_GRP // 128):
            lanes = slice(cg * 128, (cg + 1) * 128)
            ov = [_from_residue(src, dil, cg, buf) for src, dil, buf in zip((o1, o4, o16), dils, (None, b0, b1))]
            lv = [_from_residue(src, dil, cg, buf) for src, dil, buf in zip((l1, l4, l16), dils, (None, b2, b3))]
            a1, a4, a16 = _dil_alphas(*lv)
            o = a1 * ov[0] + a4 * ov[1] + a16 * ov[2]
            o_ref[:, lanes] = o.astype(BF)
            on_ref[:, lanes] = _headnorm_pair(o, g_ref[:, lanes], hm0).astype(BF)

    blk = pl.BlockSpec((TM, D_GRP), lambda m: (m, 0))
    specs = [_residue_spec(dil) for dil in dils for _ in range(2)]
    return pl.pallas_call(
        body, name="dil_comb", grid=(t // TM,),
        in_specs=specs + [pl.BlockSpec((1, D_GRP), lambda m: (0, 0))],
        out_specs=[blk, blk],
        out_shape=[_sds((t, D_GRP), BF), _sds((t, D_GRP), BF)],
        scratch_shapes=[pltpu.VMEM((TM, 128), F32)] * 4,
        compiler_params=_cp(1))(os[0], ls[0], os[1], ls[1], os[2], ls[2], g_dil)


def _dil_comb_bwd(do, os, ls):
    t = do.shape[0]
    dils = [dil for _, dil in DIL_CONFIGS]

    def body(do_ref, o1, l1, o4, l4, o16, l16, d1, d4, d16, e1, e4, e16, b0, b1, b2, b3):
        hm0 = lax.broadcasted_iota(jnp.int32, (1, 128), 1) < HEAD_DIM
        for cg in range(D_GRP // 128):
            dov = do_ref[:, cg * 128:(cg + 1) * 128].astype(F32)
            ov = [_from_residue(src, dil, cg, buf) for src, dil, buf in zip((o1, o4, o16), dils, (None, b0, b1))]
            lv = [_from_residue(src, dil, cg, buf) for src, dil, buf in zip((l1, l4, l16), dils, (None, b2, b3))]
            al = _dil_alphas(*lv)
            sbar = al[0] * _pair_sum(dov * ov[0], hm0)
            for a_c, o_c in zip(al[1:], ov[1:]):
                sbar = sbar + a_c * _pair_sum(dov * o_c, hm0)
            for a_c, dil, dref, eref in zip(al, dils, (d1, d4, d16), (e1, e4, e16)):
                _to_residue(dref, dil, cg, b0, a_c * dov)
                _to_residue(eref, dil, cg, b1, -a_c * sbar)

    specs = [_residue_spec(dil) for dil in dils]
    return pl.pallas_call(
        body, name="dil_comb_bwd", grid=(t // TM,),
        in_specs=[pl.BlockSpec((TM, D_GRP), lambda m: (m, 0))] + [sp for sp in specs for _ in range(2)],
        out_specs=specs + specs,
        out_shape=[_sds((t // dil, dil * D_GRP), BF) for dil in dils]
        + [_sds((t // dil, dil * D_GRP), F32) for dil in dils],
        scratch_shapes=[pltpu.VMEM((TM, 128), F32)] * 4,
        compiler_params=_cp(1))(do, os[0], ls[0], os[1], ls[1], os[2], ls[2])


def _dqkv_dil_sum(ds, dqkv6):
    t = dqkv6.shape[1]
    dils = [dil for _, dil in DIL_CONFIGS]

    def body(*refs):
        srcs, o_ref, acc = refs[:len(dils)], refs[len(dils) + 1], refs[len(dils) + 2]
        for a in range(3):
            for cg in range(D_GRP // 128):
                for src, dil in zip(srcs, dils):
                    for r in range(dil):
                        part = src[a, :, r * D_GRP + cg * 128:r * D_GRP + (cg + 1) * 128].astype(F32)
                        rows = pl.ds(r, TM // dil, stride=dil) if dil > 1 else slice(None)
                        if dil == dils[0]:
                            acc[rows, :] = part
                        else:
                            acc[rows, :] += part
                o_ref[a, :, cg * 128:(cg + 1) * 128] = acc[...].astype(BF)

    return pl.pallas_call(
        body, name="dqkv_dil_sum", grid=(t // TM,),
        in_specs=[pl.BlockSpec((3, TM // dil, dil * D_GRP), lambda m: (0, m, 0)) for dil in dils]
        + [pl.BlockSpec(memory_space=pl.ANY)],
        out_specs=pl.BlockSpec((3, TM, D_GRP), lambda m: (1, m, 0)),
        out_shape=_sds((6, t, D_GRP), BF), input_output_aliases={len(dils): 0},
        scratch_shapes=[pltpu.VMEM((TM, 128), F32)],
        compiler_params=_cp(1))(*ds, dqkv6)


def _relbias_grad(a_all, onehot):
    def body(a_ref, oh_ref, o_ref):
        acc = jnp.zeros((N_HEADS, N_BUCKETS), F32)
        for c in range(len(DIL_CONFIGS)):
            av = a_ref[c]
            hi = av.astype(BF)
            lo = (av - hi.astype(F32)).astype(BF)
            acc = acc + _nt(hi, oh_ref[c]) + _nt(lo, oh_ref[c])
        o_ref[...] = acc

    return pl.pallas_call(body, name="relbias_grad", out_shape=_sds((N_HEADS, N_BUCKETS), F32),
                          compiler_params=_cp())(a_all, onehot)


def _headnorm_bwd(dn, o, gv, mv):
    ms = _nn2(o * o, mv) * (1.0 / HEAD_DIM)
    r = lax.rsqrt(ms + EPS)
    nrm = o * r
    dg = jnp.sum(dn * nrm, axis=0, keepdims=True)
    dnn = dn * gv
    do = r * (dnn - nrm * (_nn2(dnn * nrm, mv) * (1.0 / HEAD_DIM)))
    return do, dg


def _mix_bwd_out(dx, gt, tv, w_out, o_sb, o_dil, on_sb, on_dil, g_sb, g_dil, ones_g, seq, carry=None):
    t, d = dx.shape
    per = seq // TM
    nb = t // seq

    def body(dx_ref, gt_ref, t_ref, w_ref, osb, odl, onsb, ondl, gsb, gdl, m_ref,
             dosb, dodl, dgt_ref, dgsb, dgdl, dw_ref):
        m = pl.program_id(0)
        dxv = dx_ref[...]
        dt = (gt_ref[...] * dxv).astype(BF)
        _acc_rows(dgt_ref, jnp.sum(dxv * t_ref[...].astype(F32), axis=0, keepdims=True), m % per == 0)
        mv = _twice(m_ref[...])
        don_sb = _nt(dt, w_ref[0:D_GRP, :])
        don_dl = _nt(dt, w_ref[D_GRP:2 * D_GRP, :])
        do1, dg1 = _headnorm_bwd(don_sb, osb[...].astype(F32), gsb[...], mv)
        do2, dg2 = _headnorm_bwd(don_dl, odl[...].astype(F32), gdl[...], mv)
        dosb[...] = do1.astype(BF)
        dodl[...] = do2.astype(BF)
        _acc_rows(dgsb, dg1, m == 0)
        _acc_rows(dgdl, dg2, m == 0)
        p1 = _tn(onsb[...], dt)
        p2 = _tn(ondl[...], dt)

        @pl.when(m == 0)
        def _():
            dw_ref[0:D_GRP, :] = p1
            dw_ref[D_GRP:2 * D_GRP, :] = p2

        @pl.when(m != 0)
        def _():
            dw_ref[0:D_GRP, :] += p1
            dw_ref[D_GRP:2 * D_GRP, :] += p2

    row = pl.BlockSpec((TM, d), lambda m: (m, 0))
    half = pl.BlockSpec((TM, D_GRP), lambda m: (m, 0))
    ex = pl.BlockSpec((None, 1, d), lambda m: (m // per, 0, 0))
    gvec = pl.BlockSpec((1, D_GRP), lambda m: (0, 0))
    wblk = pl.BlockSpec((2 * D_GRP, d), lambda m: (0, 0))
    return _call(
        body, "mix_bwd_out", (t // TM,),
        [row, ex, row, wblk, half, half, half, half, gvec, gvec, pl.BlockSpec((D_GRP, D_GRP), lambda m: (0, 0))],
        [half, half, ex, gvec, gvec, wblk],
        [_sds((t, D_GRP), BF), _sds((t, D_GRP), BF), _sds((nb, 1, d), F32),
         _sds((1, D_GRP), F32), _sds((1, D_GRP), F32), _sds((2 * D_GRP, d), F32)],
        (dx, gt, tv, w_out, o_sb, o_dil, on_sb, on_dil, g_sb, g_dil, ones_g), carry=carry)


def _dw_in(h, dqkv6, carry=None):
    t, d = h.shape
    wc = 6 * D_GRP // N_CHIPS

    def body(h_ref, g_ref, o_ref):
        kt = pl.program_id(0)
        hv = h_ref[...]
        for j in range(N_CHIPS):
            p = _tn(hv, _chip_cols(g_ref, j, wc))

            @pl.when(kt == 0)
            def _(p=p, j=j):
                o_ref[j, 0] = p

            @pl.when(kt != 0)
            def _(p=p, j=j):
                o_ref[j, 0] += p

    return _call(
        body, "dw_in", (t // TK_W,),
        [pl.BlockSpec((TK_W, d), lambda kt: (kt, 0)), pl.BlockSpec((6, TK_W, D_GRP), lambda kt: (0, kt, 0))],
        [pl.BlockSpec((N_CHIPS, 1, d, wc), lambda kt: (0, 0, 0, 0))],
        [_sds((N_CHIPS, 1, d, wc), F32)], (h, dqkv6), carry=carry)


def _mix_bwd_dh(dqkv6, w_in, x, g, sc, dxo, seq, carry=None):
    _, t, _ = dqkv6.shape
    d = x.shape[-1]
    wc = w_in.shape[-1]
    per = seq // TM
    nb = t // seq

    def body(g6_ref, w_ref, x_ref, g_ref, sc_ref, dxo_ref, dx_ref, dsh_ref, dsc_ref, dg_ref):
        m = pl.program_id(0)
        dh = _nt(_chip_cols(g6_ref, 0, wc), w_ref[0, 0])
        for j in range(1, N_CHIPS):
            dh = dh + _nt(_chip_cols(g6_ref, j, wc), w_ref[j, 0])
        dx, dsh, dsc, dg = _modnorm_bwd_tile(dh, x_ref[...], g_ref[...], sc_ref[...], dxo_ref[...])
        dx_ref[...] = dx
        _acc_rows(dsh_ref, dsh, m % per == 0)
        _acc_rows(dsc_ref, dsc, m % per == 0)
        _acc_rows(dg_ref, dg, m == 0)

    row = pl.BlockSpec((TM, d), lambda m: (m, 0))
    ex = pl.BlockSpec((None, 1, d), lambda m: (m // per, 0, 0))
    vec = pl.BlockSpec((1, d), lambda m: (0, 0))
    return _call(
        body, "mix_bwd_dh", (t // TM,),
        [pl.BlockSpec((6, TM, D_GRP), lambda m: (0, m, 0)), _whole(w_in), row, vec, ex, row],
        [row, ex, ex, vec],
        [_sds((t, d), F32), _sds((nb, 1, d), F32), _sds((nb, 1, d), F32), _sds((1, d), F32)],
        (dqkv6, w_in, x, g, sc, dxo), carry=carry)


def _ffn_down_loss(s, wd, x, gt, seq, coef, g, target):
    _, t, fs = s.shape
    d = x.shape[-1]
    per = seq // TM
    steps = t // TM

    def body(s_ref, w_ref, x_ref, gt_ref, g_ref, t_ref, f_ref, dx_ref, dg_ref, loss_ref, lacc):
        m = pl.program_id(0)
        f = _nn(s_ref[0], w_ref[0, 0])
        for j in range(1, N_CHIPS):
            f = f + _nn(s_ref[j], w_ref[j, 0])
        f_ref[...] = f.astype(BF)
        xv = x_ref[...] + (coef * gt_ref[...]) * f
        gv = g_ref[...]
        r = lax.rsqrt(jnp.mean(xv * xv, axis=-1, keepdims=True) + EPS)
        n = xv * r
        err = n * gv - t_ref[...]
        dy = err * (1.0 / d)
        _acc_rows(dg_ref, jnp.sum(dy * n, axis=0, keepdims=True), m == 0)
        dn = dy * gv
        dx_ref[...] = r * (dn - n * jnp.mean(dn * n, axis=-1, keepdims=True))
        _acc_rows(lacc, jnp.sum(err * err, axis=0, keepdims=True), m == 0)

        @pl.when(m == steps - 1)
        def _():
            tot = jnp.sum(lacc[...], axis=-1, keepdims=True) * (0.5 / d)
            loss_ref[...] = jnp.broadcast_to(tot, (1, 128))

    row = pl.BlockSpec((TM, d), lambda m: (m, 0))
    vec = pl.BlockSpec((1, d), lambda m: (0, 0))
    return pl.pallas_call(
        body, name="ffn_down_loss", grid=(steps,),
        in_specs=[pl.BlockSpec((N_CHIPS, TM, fs), lambda m: (0, m, 0)), _whole(wd), row,
                  pl.BlockSpec((None, 1, d), lambda m: (m // per, 0, 0)), vec, row],
        out_specs=[row, row, vec, pl.BlockSpec((1, 128), lambda m: (0, 0))],
        out_shape=[_sds((t, d), BF), _sds((t, d), F32), _sds((1, d), F32), _sds((1, 128), F32)],
        scratch_shapes=[pltpu.VMEM((1, d), F32)],
        compiler_params=_cp(1))(s, wd, x, gt, g, target)


def _row_tile(rows, cols):
    best = rows
    for tr in range(8, rows + 1, 8):
        if rows % tr == 0 and tr * cols * 4 <= (1 << 20):
            best = tr
    if best * cols * 4 > (1 << 21):
        best = 8
    return best


def _adamw(w, g_arr, g_sel, m, v):
    rows, cols = w.shape
    tr = _row_tile(rows, cols)
    b1c = 1.0 - ADAM_B1 ** ADAM_STEP
    b2c = 1.0 - ADAM_B2 ** ADAM_STEP

    def body(w_ref, g_ref, m_ref, v_ref, go_ref, d_ref, mo_ref, vo_ref):
        gv = g_ref[...]
        mn = ADAM_B1 * m_ref[...] + (1.0 - ADAM_B1) * gv
        vn = ADAM_B2 * v_ref[...] + (1.0 - ADAM_B2) * (gv * gv)
        go_ref[...] = gv
        mo_ref[...] = mn
        vo_ref[...] = vn
        d_ref[...] = -ADAM_LR * ((mn / b1c) / (jnp.sqrt(vn / b2c) + ADAM_EPS) + ADAM_WD * w_ref[...])

    blk = pl.BlockSpec((tr, cols), lambda i: (i, 0))
    shp = _sds((rows, cols), F32)
    return pl.pallas_call(
        body, name="adamw", grid=(rows // tr,),
        in_specs=[blk, pl.BlockSpec((None, tr, cols), lambda i: (g_sel, i, 0)), blk, blk],
        out_specs=[blk] * 4, out_shape=[shp] * 4,
        compiler_params=_cp(1))(w, g_arr, m, v)


def _flip(v, bit):
    return 1 - v if bit else v


def _my_place():
    x, y, c = lax.axis_index("x"), lax.axis_index("y"), lax.axis_index("c")
    return x, y, c


class _Exchange:
    def __init__(self, operands, out_shape, aliases, sems, start, finish):
        self.operands, self.out_shape, self.aliases, self.sems = list(operands), list(out_shape), dict(aliases), list(sems)
        self.start, self.finish = start, finish


def _join(exchanges):
    exchanges = [e for e in exchanges if e is not None]
    if not exchanges:
        return None
    ops, outs, sems, aliases, spans = [], [], [], {}, []
    for e in exchanges:
        spans.append((len(ops), len(outs), len(sems), e))
        for i, j in e.aliases.items():
            aliases[len(ops) + i] = len(outs) + j
        ops += e.operands
        outs += e.out_shape
        sems += e.sems

    def run(which):
        def go(ins, res, sm):
            for io, oo, so, e in spans:
                getattr(e, which)(ins[io:io + len(e.operands)], res[oo:oo + len(e.out_shape)], sm[so:so + len(e.sems)])
        return go

    return _Exchange(ops, outs, aliases, sems, run("start"), run("finish"))


def _call(body, name, grid, in_specs, out_specs, out_shape, args, scratch=(), carry=None, io_alias=None):
    in_specs, out_specs, out_shape, scratch = list(in_specs), list(out_specs), list(out_shape), list(scratch)
    io_alias = dict(io_alias or {})
    if carry is None:
        return pl.pallas_call(body, name=name, grid=grid, in_specs=in_specs, out_specs=out_specs,
                              out_shape=out_shape, scratch_shapes=scratch, input_output_aliases=io_alias,
                              compiler_params=_cp(len(grid)))(*args)
    n_in, n_out, n_s = len(in_specs), len(out_specs), len(scratch)
    c_in, c_out = len(carry.operands), len(carry.out_shape)
    any_spec = pl.BlockSpec(memory_space=pl.ANY)

    def wrapped(*refs):
        ins, cins = refs[:n_in], refs[n_in:n_in + c_in]
        o0 = n_in + c_in
        outs, couts = refs[o0:o0 + n_out], refs[o0 + n_out:o0 + n_out + c_out]
        s0 = o0 + n_out + c_out
        scr, sems = refs[s0:s0 + n_s], refs[s0 + n_s:]
        first = pl.program_id(0) == 0
        last = pl.program_id(0) == grid[0] - 1
        for ax in range(1, len(grid)):
            first = jnp.logical_and(first, pl.program_id(ax) == 0)
            last = jnp.logical_and(last, pl.program_id(ax) == grid[ax] - 1)

        @pl.when(first)
        def _():
            carry.start(cins, couts, sems)

        body(*ins, *outs, *scr)

        @pl.when(last)
        def _():
            carry.finish(cins, couts, sems)

    return pl.pallas_call(
        wrapped, name=name, grid=grid, in_specs=in_specs + [any_spec] * c_in,
        out_specs=out_specs + [any_spec] * c_out, out_shape=out_shape + carry.out_shape,
        scratch_shapes=scratch + carry.sems,
        input_output_aliases={**io_alias, **{n_in + i: n_out + j for i, j in carry.aliases.items()}},
        compiler_params=_cp(len(grid)))(*args, *carry.operands)


def _whole_call(body, name, args, out_shape, scratch, carry=None):
    vm = pl.BlockSpec(memory_space=pltpu.VMEM)
    any_spec = pl.BlockSpec(memory_space=pl.ANY)
    out_shape, scratch = list(out_shape), list(scratch)
    n_in, n_out, n_s = len(args), len(out_shape), len(scratch)
    if carry is None:
        return pl.pallas_call(body, name=name, in_specs=[vm] * n_in, out_specs=[vm] * n_out, out_shape=out_shape,
                              scratch_shapes=scratch, compiler_params=_cp())(*args)
    c_in, c_out = len(carry.operands), len(carry.out_shape)

    def wrapped(*refs):
        ins, cins = refs[:n_in], refs[n_in:n_in + c_in]
        o0 = n_in + c_in
        outs, couts = refs[o0:o0 + n_out], refs[o0 + n_out:o0 + n_out + c_out]
        s0 = o0 + n_out + c_out
        scr, sems = refs[s0:s0 + n_s], refs[s0 + n_s:]
        carry.start(cins, couts, sems)
        body(*ins, *outs, *scr)
        carry.finish(cins, couts, sems)

    return pl.pallas_call(
        wrapped, name=name, in_specs=[vm] * n_in + [any_spec] * c_in, out_specs=[vm] * n_out + [any_spec] * c_out,
        out_shape=out_shape + carry.out_shape, scratch_shapes=scratch + carry.sems,
        input_output_aliases={n_in + i: n_out + j for i, j in carry.aliases.items()},
        compiler_params=_cp())(*args, *carry.operands)


def _alone(name, ex):
    any_spec = pl.BlockSpec(memory_space=pl.ANY)
    c_in, c_out = len(ex.operands), len(ex.out_shape)

    def body(*refs):
        ins, outs, sems = refs[:c_in], refs[c_in:c_in + c_out], refs[c_in + c_out:]
        ex.start(ins, outs, sems)
        ex.finish(ins, outs, sems)

    return pl.pallas_call(
        body, name=name, in_specs=[any_spec] * c_in, out_specs=[any_spec] * c_out, out_shape=ex.out_shape,
        scratch_shapes=ex.sems, input_output_aliases=ex.aliases, compiler_params=_cp())(*ex.operands)


def _ada_fwd(c_pad, w_ada, b_shard, carry=None):
    d = c_pad.shape[-1]
    cols = w_ada.shape[-1]
    chunk = 384

    def body(c_ref, w_ref, b_ref, call_ref, mod_ref, part, s1, r1, s2, r2):
        x, y, c = _my_place()
        dev = 4 * x + 2 * y + c
        chip = 2 * x + y
        call_ref[dev] = c_ref[...]

        def c_copy(k):
            px, py, pc = _flip(x, (k >> 2) & 1), _flip(y, (k >> 1) & 1), _flip(c, k & 1)
            return px, py, pc

        sends = []
        for k in range(1, N_DEV):
            px, py, pc = c_copy(k)
            cp = pltpu.make_async_remote_copy(src_ref=c_ref, dst_ref=call_ref.at[dev], send_sem=s1.at[k - 1],
                                              recv_sem=r1.at[k - 1], device_id=(px, py, pc), device_id_type=MESH)
            cp.start()
            sends.append(cp)
        for k in range(1, N_DEV):
            px, py, pc = c_copy(k)
            pltpu.make_async_remote_copy(src_ref=c_ref, dst_ref=call_ref.at[4 * px + 2 * py + pc],
                                         send_sem=s1.at[k - 1], recv_sem=r1.at[k - 1],
                                         device_id=(px, py, pc), device_id_type=MESH).wait_recv()
        for cp in sends:
            cp.wait_send()

        cs = call_ref[...].reshape(N_DEV * 8, d)
        sc = (cs * jax.nn.sigmoid(cs)).astype(BF)
        for n0 in range(0, cols, chunk):
            blk = _nn(sc, w_ref[:, n0:n0 + chunk].astype(BF)) + b_ref[:, n0:n0 + chunk]
            part[:, :, n0:n0 + chunk] = blk.reshape(N_DEV, 8, chunk)

        mod_ref[chip] = part[dev]
        sends = []
        for kk in range(1, N_CHIPS):
            px, py = _flip(x, (kk >> 1) & 1), _flip(y, kk & 1)
            cp = pltpu.make_async_remote_copy(src_ref=part.at[4 * px + 2 * py + c], dst_ref=mod_ref.at[chip],
                                              send_sem=s2.at[kk - 1], recv_sem=r2.at[kk - 1],
                                              device_id=(px, py, c), device_id_type=MESH)
            cp.start()
            sends.append(cp)
        for kk in range(1, N_CHIPS):
            px, py = _flip(x, (kk >> 1) & 1), _flip(y, kk & 1)
            pltpu.make_async_remote_copy(src_ref=part.at[dev], dst_ref=mod_ref.at[2 * px + py],
                                         send_sem=s2.at[kk - 1], recv_sem=r2.at[kk - 1],
                                         device_id=(px, py, c), device_id_type=MESH).wait_recv()
        for cp in sends:
            cp.wait_send()

    return _whole_call(
        body, "ada_fwd", (c_pad, w_ada, b_shard),
        [_sds((N_DEV, 8, d), F32), _sds((N_CHIPS, 8, cols), F32)],
        [pltpu.VMEM((N_DEV, 8, cols), F32),
         pltpu.SemaphoreType.DMA((N_DEV - 1,)), pltpu.SemaphoreType.DMA((N_DEV - 1,)),
         pltpu.SemaphoreType.DMA((N_CHIPS - 1,)), pltpu.SemaphoreType.DMA((N_CHIPS - 1,))], carry=carry)


def _ag_weights(bufs, kks=(1, 2, 3), relative=False):
    n, nk = len(bufs), len(kks)

    def half(b, which):
        hr = bufs[b].shape[2] // 2
        return pl.ds(pl.multiple_of(which * hr, 16), hr)

    def copies(outs, sems, b, i, kk):
        x, y, c = _my_place()
        chip = 2 * x + y
        px, py = _flip(x, (kk >> 1) & 1), _flip(y, kk & 1)
        mine, theirs = (0, kk) if relative else (chip, 2 * px + py)
        landing = kk if relative else chip
        k = nk * b + i
        send = pltpu.make_async_remote_copy(
            src_ref=outs[b].at[mine, :, half(b, c), :], dst_ref=outs[b].at[landing, :, half(b, c), :],
            send_sem=sems[0].at[k], recv_sem=sems[1].at[k], device_id=(px, py, c), device_id_type=MESH)
        got = outs[b].at[theirs, :, half(b, c), :]
        recv = pltpu.make_async_remote_copy(
            src_ref=got, dst_ref=got, send_sem=sems[0].at[k], recv_sem=sems[1].at[k],
            device_id=(px, py, c), device_id_type=MESH)
        fwd = pltpu.make_async_remote_copy(
            src_ref=got, dst_ref=got, send_sem=sems[2].at[k], recv_sem=sems[3].at[k],
            device_id=(x, y, 1 - c), device_id_type=MESH)
        other = outs[b].at[theirs, :, half(b, 1 - c), :]
        back = pltpu.make_async_remote_copy(
            src_ref=other, dst_ref=other, send_sem=sems[2].at[k], recv_sem=sems[3].at[k],
            device_id=(x, y, 1 - c), device_id_type=MESH)
        return send, recv, fwd, back

    def each(outs, sems):
        for b in range(n):
            for i, kk in enumerate(kks):
                yield copies(outs, sems, b, i, kk)

    def start(ins, outs, sems):
        for send, _, _, _ in each(outs, sems):
            send.start()

    def finish(ins, outs, sems):
        for _, recv, fwd, _ in each(outs, sems):
            recv.wait_recv()
            fwd.start()
        for send, _, fwd, back in each(outs, sems):
            back.wait_recv()
            send.wait_send()
            fwd.wait_send()

    return _Exchange(bufs, [_sds(s.shape, s.dtype) for s in bufs], {i: i for i in range(n)},
                     [pltpu.SemaphoreType.DMA((nk * n,))] * 4, start, finish)


def _rs_d2d(grads):
    n = len(grads)

    def copy(ins, outs, sems, b):
        x, y, c = _my_place()
        hr = grads[b].shape[2] // 2
        theirs = pl.ds(pl.multiple_of((1 - c) * hr, 8), hr)
        return pltpu.make_async_remote_copy(
            src_ref=ins[b].at[:, :, theirs, :], dst_ref=outs[b], send_sem=sems[0].at[b], recv_sem=sems[1].at[b],
            device_id=(x, y, 1 - c), device_id_type=MESH)

    def start(ins, outs, sems):
        for b in range(n):
            copy(ins, outs, sems, b).start()

    def finish(ins, outs, sems):
        for b in range(n):
            copy(ins, outs, sems, b).wait()

    return _Exchange(grads, [_sds(g.shape[:2] + (g.shape[2] // 2, g.shape[3]), F32) for g in grads], {},
                     [pltpu.SemaphoreType.DMA((n,))] * 2, start, finish)


def _add_halves(core, g, land):
    nchip, ng, rows, cols = g.shape
    hr = rows // 2
    tr = _row_tile(hr, cols)
    steps = hr // tr

    def body(core_ref, g_ref, l_ref, o_ref):
        del core_ref
        o_ref[...] = (g_ref[...] + l_ref[...]).astype(BF)

    return pl.pallas_call(
        body, name="add_halves",
        grid_spec=pltpu.PrefetchScalarGridSpec(
            num_scalar_prefetch=1, grid=(nchip, ng, steps),
            in_specs=[pl.BlockSpec((None, None, tr, cols), lambda j, a, i, cr: (j, a, cr[0] * steps + i, 0)),
                      pl.BlockSpec((None, None, tr, cols), lambda j, a, i, cr: (j, a, i, 0))],
            out_specs=pl.BlockSpec((None, None, tr, cols), lambda j, a, i, cr: (j, a, i, 0))),
        out_shape=_sds((nchip, ng, hr, cols), BF),
        compiler_params=_cp(3))(core, g, land)


def _rs_ici(parts, relative=False):
    n = len(parts)

    def copies(ins, outs, sems):
        x, y, c = _my_place()
        chip = 2 * x + y
        for b in range(n):
            for kk in range(1, N_CHIPS):
                px, py = _flip(x, (kk >> 1) & 1), _flip(y, kk & 1)
                k = 3 * b + kk - 1
                theirs, landing = (kk, kk) if relative else (2 * px + py, chip)
                send = pltpu.make_async_remote_copy(
                    src_ref=ins[b].at[theirs], dst_ref=outs[b].at[landing],
                    send_sem=sems[0].at[k], recv_sem=sems[1].at[k], device_id=(px, py, c), device_id_type=MESH)
                slot = outs[b].at[theirs]
                recv = pltpu.make_async_remote_copy(
                    src_ref=slot, dst_ref=slot, send_sem=sems[0].at[k], recv_sem=sems[1].at[k],
                    device_id=(px, py, c), device_id_type=MESH)
                yield send, recv

    def start(ins, outs, sems):
        for send, _ in copies(ins, outs, sems):
            send.start()

    def finish(ins, outs, sems):
        for send, recv in copies(ins, outs, sems):
            recv.wait_recv()
            send.wait_send()

    return _Exchange(parts, [_sds(p.shape, p.dtype) for p in parts], {},
                     [pltpu.SemaphoreType.DMA((3 * n,))] * 2, start, finish)


def _sum_chips(place, part, land, relative=False):
    nchip, ng, hr, cols = land.shape
    tr = _row_tile(hr, cols)
    steps = hr // tr

    def body(place_ref, p_ref, l1, l2, l3, o_ref):
        del place_ref
        o_ref[...] = ((p_ref[...].astype(F32) + l1[...].astype(F32)) + l2[...].astype(F32)) + l3[...].astype(F32)

    def slot(k):
        if relative:
            return pl.BlockSpec((None, None, tr, cols), lambda a, i, pr: (k, a, i, 0))
        return pl.BlockSpec((None, None, tr, cols), lambda a, i, pr: (jnp.bitwise_xor(pr[1], k), a, i, 0))

    return pl.pallas_call(
        body, name="sum_chips",
        grid_spec=pltpu.PrefetchScalarGridSpec(
            num_scalar_prefetch=1, grid=(ng, steps),
            in_specs=[slot(0), slot(1), slot(2), slot(3)],
            out_specs=pl.BlockSpec((None, tr, cols), lambda a, i, pr: (a, pr[0] * steps + i, 0))),
        out_shape=_sds((ng, 2 * hr, cols), F32),
        compiler_params=_cp(2))(place, part, land, land, land)


def _rs_final(bufs):
    n = len(bufs)

    def copy(outs, sems, b, which):
        x, y, c = _my_place()
        hr = bufs[b].shape[1] // 2
        rows = outs[b].at[:, pl.ds(pl.multiple_of((c if which == 0 else 1 - c) * hr, 8), hr), :]
        return pltpu.make_async_remote_copy(
            src_ref=rows, dst_ref=rows, send_sem=sems[0].at[b], recv_sem=sems[1].at[b],
            device_id=(x, y, 1 - c), device_id_type=MESH)

    def start(ins, outs, sems):
        for b in range(n):
            copy(outs, sems, b, 0).start()

    def finish(ins, outs, sems):
        for b in range(n):
            copy(outs, sems, b, 0).wait_send()
            copy(outs, sems, b, 1).wait_recv()

    return _Exchange(bufs, [_sds(h.shape, F32) for h in bufs], {i: i for i in range(n)},
                     [pltpu.SemaphoreType.DMA((n,))] * 2, start, finish)


def _small_sync(smalls, dmod_blk, c_all, carry=None):
    d = c_all.shape[-1]
    cols = dmod_blk.shape[-1]
    chunk = 384

    def body(sm_ref, dm_ref, c_ref, sum_ref, gw_ref, sm_all, dm_all, ssem, rsem):
        x, y, c = _my_place()
        dev = 4 * x + 2 * y + c
        chip = 2 * x + y
        sm_all[dev] = sm_ref[...]
        dm_all[dev] = dm_ref[chip]
        sends = []
        for k in range(1, N_DEV):
            px, py, pc = _flip(x, (k >> 2) & 1), _flip(y, (k >> 1) & 1), _flip(c, k & 1)
            a = pltpu.make_async_remote_copy(src_ref=sm_ref, dst_ref=sm_all.at[dev], send_sem=ssem.at[2 * (k - 1)],
                                             recv_sem=rsem.at[2 * (k - 1)], device_id=(px, py, pc),
                                             device_id_type=MESH)
            b = pltpu.make_async_remote_copy(src_ref=dm_ref.at[2 * px + py], dst_ref=dm_all.at[dev],
                                             send_sem=ssem.at[2 * (k - 1) + 1], recv_sem=rsem.at[2 * (k - 1) + 1],
                                             device_id=(px, py, pc), device_id_type=MESH)
            a.start()
            b.start()
            sends += [a, b]
        for k in range(1, N_DEV):
            px, py, pc = _flip(x, (k >> 2) & 1), _flip(y, (k >> 1) & 1), _flip(c, k & 1)
            pdev = 4 * px + 2 * py + pc
            pltpu.make_async_remote_copy(src_ref=sm_ref, dst_ref=sm_all.at[pdev], send_sem=ssem.at[2 * (k - 1)],
                                         recv_sem=rsem.at[2 * (k - 1)], device_id=(px, py, pc),
                                         device_id_type=MESH).wait_recv()
            pltpu.make_async_remote_copy(src_ref=dm_ref.at[chip], dst_ref=dm_all.at[pdev],
                                         send_sem=ssem.at[2 * (k - 1) + 1], recv_sem=rsem.at[2 * (k - 1) + 1],
                                         device_id=(px, py, pc), device_id_type=MESH).wait_recv()
        for cp in sends:
            cp.wait_send()

        tot = sm_all[0]
        for q in range(1, N_DEV):
            tot = tot + sm_all[q]
        sum_ref[...] = tot

        cs = c_ref[...].reshape(N_DEV * 8, d)
        sc = (cs * jax.nn.sigmoid(cs)).astype(BF)
        for n0 in range(0, cols, chunk):
            dmv = dm_all[:, :, n0:n0 + chunk].reshape(N_DEV * 8, chunk).astype(BF)
            gw_ref[:, n0:n0 + chunk] = _tn(sc, dmv)

    return _whole_call(
        body, "small_sync", (smalls, dmod_blk, c_all),
        [_sds(smalls.shape, F32), _sds((d, cols), F32)],
        [pltpu.VMEM((N_DEV,) + smalls.shape, F32), pltpu.VMEM((N_DEV, 8, cols), F32),
         pltpu.SemaphoreType.DMA((2 * (N_DEV - 1),)), pltpu.SemaphoreType.DMA((2 * (N_DEV - 1),))], carry=carry)


def _bucket_onehot():
    maps = np.stack([_bucket_map(dil).reshape(-1) for _, dil in DIL_CONFIGS])
    return (jnp.asarray(maps)[:, None, :] == jnp.arange(N_BUCKETS, dtype=jnp.int32)[None, :, None]).astype(BF)


def _dil_bias(rel_t, onehot):
    def body(r_ref, oh_ref, o_ref):
        rv = r_ref[...]
        hi = rv.astype(BF)
        lo = (rv - hi.astype(F32)).astype(BF)
        for c in range(len(DIL_CONFIGS)):
            o_ref[c] = _nn(hi, oh_ref[c]) + _nn(lo, oh_ref[c])

    return pl.pallas_call(body, name="dil_bias",
                          out_shape=_sds((len(DIL_CONFIGS), N_HEADS, BLOCK * 2 * BLOCK), F32),
                          compiler_params=_cp())(rel_t, onehot)


def _rowsum8(a):
    def body(a_ref, o_ref):
        o_ref[...] = jnp.sum(a_ref[...], axis=0, keepdims=True)

    return pl.pallas_call(body, name="rowsum8", out_shape=_sds((1, a.shape[1]), F32), compiler_params=_cp())(a)


def _local_step(x, mod, target, w, gains, rel_bias, place=None):
    nb, seq, d = x.shape
    t = nb * seq
    dist = place is not None
    core = place[0:1] if dist else None
    x0 = x.reshape(t, d)
    tgt = target.reshape(t, d)
    md = [mod[:, i:i + 1, :] for i in range(N_MOD)]
    sh1, sc1, gt1, sh2, sc2, gt2, sh3, sc3, gt3 = md
    g1, g2, g3 = gains["g_ffn1"], gains["g_mix"], gains["g_ffn2"]
    ones_g = _group_ones()

    def partial_sums(grads, lands):
        return [_add_halves(core, g, l) for g, l in zip(grads, lands)]

    def chip_sums(parts, lands):
        return [_sum_chips(place, p, l, relative=True) for p, l in zip(parts, lands)]

    gu1 = w["gu1"]
    res = _ffn_up(x0, g1, sc1, sh1, gu1, seq,
                  carry=_join([_ag_weights([w["d1"]], relative=True), _ag_weights([w["win"]])]) if dist else None)
    h1, a1, u1, s1 = res[:4]
    wd1, w_in = res[4:] if dist else (w["d1"], w["win"])
    f1, x1 = _ffn_down(s1, wd1, x0, gt1, seq, 0.5)

    res = _qkv_proj(x1, g2, sc2, sh2, w_in, seq, carry=_ag_weights([w["wout"]]) if dist else None)
    h2, qkv6, qkv_r4, qkv_r16 = res[:4]
    w_out2 = (res[4] if dist else w["wout"]).reshape(2 * D_GRP, d)
    qkv6b = qkv6.reshape(6, nb, seq, D_GRP)
    res = _sb_fwd(qkv6b, gains["g_sb_out"], nb, seq,
                  carry=_ag_weights([w["gu2"], w["d2"]], relative=True) if dist else None)
    o_sb, on_sb = res[:2]
    wgu2, wd2 = res[2:] if dist else (w["gu2"], w["d2"])
    onehot = _bucket_onehot()
    bias = _dil_bias(rel_bias.T, onehot).reshape(len(DIL_CONFIGS), N_HEADS * BLOCK, 2 * BLOCK)
    o_cs, l_cs = [], []
    qkv_rs = [(qkv6b, 3), (qkv_r4, 0), (qkv_r16, 0)]
    for ci, (_, dil) in enumerate(DIL_CONFIGS):
        sub = seq // dil
        arr, base = qkv_rs[ci]
        arr = arr.reshape(base + 3, nb, sub, dil * D_GRP)
        qkv_rs[ci] = (arr, base)
        o_c, l_c = _dil_fwd(arr, base, bias[ci], nb, sub, dil)
        o_cs.append(o_c.reshape(t // dil, dil * D_GRP))
        l_cs.append(l_c.reshape(t // dil, dil * D_GRP))
    o_dil, on_dil = _dil_comb(o_cs, l_cs, gains["g_dil_out"])
    tmix, x2 = _mix_out(on_sb.reshape(t, D_GRP), on_dil, w_out2, x1, gt2, seq)

    h3, a3, u3, s3 = _ffn_up(x2, g3, sc3, sh3, wgu2, seq)
    f3, dx3, dg_final, loss = _ffn_down_loss(s3, wd2, x2, gt3, seq, 0.5, gains["g_final"], tgt)

    da3, du3, df3, dgt3, dx2, dsh3, dsc3, dg3 = _ffn_bwd_x(dx3, gt3, f3, wd2, a3, u3, wgu2, x2, g3, sc3, seq, 0.5)
    grads2 = [_ffn_bwd_w(h3, da3, du3, s3, df3)]

    res = _mix_bwd_out(
        dx2, gt2, tmix, w_out2, o_sb.reshape(t, D_GRP), o_dil, on_sb.reshape(t, D_GRP), on_dil,
        gains["g_sb_out"], gains["g_dil_out"], ones_g, seq, carry=_rs_d2d(grads2) if dist else None)
    do_sb, do_dil, dgt2, dg_sb, dg_dil, dw_out = res[:6]
    parts2 = partial_sums(grads2, res[6:]) if dist else None
    dw_out = dw_out.reshape(N_CHIPS, 1, 2 * D_GRP // N_CHIPS, d)
    res = _sb_bwd(qkv6b, do_sb.reshape(nb, seq, D_GRP), nb, seq,
                  carry=_rs_ici(parts2, relative=True) if dist else None)
    dqkv6 = res[0]
    halves2 = chip_sums(parts2, res[1:]) if dist else None
    dcs = _dil_comb_bwd(do_dil, o_cs, l_cs)
    dsum, a_tiles = [], []
    for ci, (_, dil) in enumerate(DIL_CONFIGS):
        sub = seq // dil
        do_c = dcs[ci].reshape(nb, sub, dil * D_GRP)
        dd_c = dcs[3 + ci].reshape(nb, sub, dil * D_GRP)
        res = _dil_bwd(qkv_rs[ci][0], qkv_rs[ci][1], bias[ci], do_c, dd_c, nb, sub, dil)
        dsum.append(res[0].reshape(3, t // dil, dil * D_GRP))
        a_tiles.append(res[1].reshape(N_HEADS, BLOCK * 2 * BLOCK))
    dqkv6 = _dqkv_dil_sum(dsum, dqkv6.reshape(6, t, D_GRP))
    drel = _relbias_grad(jnp.stack(a_tiles), onehot)
    dx1, dsh2, dsc2, dg2 = _mix_bwd_dh(dqkv6, w_in, x1, g2, sc2, dx2, seq)

    da1, du1, df1, dgt1 = _ffn_bwd_ds(dx1, gt1, f1, wd1, a1, u1, seq, 0.5)
    grads1 = [_ffn_bwd_w(h1, da1, du1, s1, df1)]
    res = _dw_in(h2, dqkv6, carry=_join([_rs_d2d(grads1), _rs_final(halves2)]) if dist else None)
    grads_m = [res[0], dw_out]
    parts1 = partial_sums(grads1, res[1:2]) if dist else None
    if dist:
        grads2 = res[2:3]
    res = _ffn_bwd_dh(da1, du1, gu1, x0, g1, sc1, dx1, seq,
                      carry=_join([_rs_ici(parts1, relative=True), _rs_d2d(grads_m)]) if dist else None)
    dx0, dsh1, dsc1, dg1 = res[:4]
    pending = None
    if dist:
        pending = (chip_sums(parts1, res[4:5]), partial_sums(grads_m, res[5:7]))

    dmod = jnp.concatenate([dsh1, dsc1, dgt1, dsh2, dsc2, dgt2, dsh3, dsc3, dgt3], axis=1)
    return dict(grad_x=dx0.reshape(nb, seq, d), loss=loss[0, 0], dmod=dmod.reshape(nb, N_MOD * d),
                dffn1=grads1[0], dffn2=grads2[0], dwin=grads_m[0], dwout=grads_m[1], pending=pending,
                dg_ffn1=dg1, dg_mix=dg2, dg_ffn2=dg3, dg_final=dg_final, dg_sb=dg_sb, dg_dil=dg_dil,
                drel=drel.T)


_SMALL_ORDER = (("b_ada", N_MOD * 1024), ("g_ffn1", 1024), ("g_mix", 1024), ("g_ffn2", 1024), ("g_final", 1024),
                ("g_sb_out", D_GRP), ("g_dil_out", D_GRP), ("rel_bias", N_BUCKETS * N_HEADS))


def _pack_small(parts, extra=None):
    flat = [parts[name].reshape(-1).astype(F32) for name, _ in _SMALL_ORDER]
    used = sum(sz for _, sz in _SMALL_ORDER)
    pad = SMALL_ROWS * 128 - used
    tail = jnp.zeros((pad,), F32)
    if extra is not None:
        tail = tail.at[0].set(extra)
    return jnp.concatenate(flat + [tail]).reshape(SMALL_ROWS, 128)


def _unpack_small(packed, shapes):
    flat = packed.reshape(-1)
    out, off = {}, 0
    for name, sz in _SMALL_ORDER:
        out[name] = flat[off:off + sz].reshape(shapes[name])
        off += sz
    return out, flat[off]


def kernel(x, c, w_ada, b_ada, g_ffn1, w1_gate, w1_up, w1_down, g_mix, w_in, g_sb_out, g_dil_out, w_out, rel_bias, g_ffn2, w2_gate, w2_up, w2_down, g_final, loss_target, m_w_ada, m_b_ada, m_g_ffn1, m_w1_gate, m_w1_up, m_w1_down, m_g_mix, m_w_in, m_g_sb_out, m_g_dil_out, m_w_out, m_rel_bias, m_g_ffn2, m_w2_gate, m_w2_up, m_w2_down, m_g_final, v_w_ada, v_b_ada, v_g_ffn1, v_w1_gate, v_w1_up, v_w1_down, v_g_mix, v_w_in, v_g_sb_out, v_g_dil_out, v_w_out, v_rel_bias, v_g_ffn2, v_w2_gate, v_w2_up, v_w2_down, v_g_final):
    nb, seq, d = x.shape
    xi, yi, ci = lax.axis_index("x"), lax.axis_index("y"), lax.axis_index("c")
    chip = 2 * xi + yi
    ada_cols = w_ada.shape[-1]

    c_pad = jnp.zeros((8, d), F32).at[:nb].set(c)
    b_shard = lax.dynamic_slice(b_ada, (0, chip * ada_cols), (1, ada_cols))
    shards = dict(gu1=jnp.stack([w1_gate[0], w1_up[0]]), d1=w1_down, win=w_in, wout=w_out,
                  gu2=jnp.stack([w2_gate[0], w2_up[0]]), d2=w2_down)
    bufs = {k: lax.dynamic_update_slice(lax.empty((N_CHIPS,) + s.shape, BF), s.astype(BF)[None],
                                        (chip if k in ("win", "wout") else 0, 0, 0, 0))
            for k, s in shards.items()}
    c_all, mod_blk, bufs["gu1"] = _ada_fwd(c_pad, w_ada[0], b_shard,
                                           carry=_ag_weights([bufs["gu1"]], relative=True))
    mod = jnp.transpose(mod_blk[:, :nb, :], (1, 0, 2)).reshape(nb, N_MOD, d)

    gains = dict(g_ffn1=g_ffn1, g_mix=g_mix, g_ffn2=g_ffn2, g_final=g_final.reshape(1, d),
                 g_sb_out=g_sb_out.reshape(1, D_GRP), g_dil_out=g_dil_out.reshape(1, D_GRP))
    place = jnp.stack([ci, chip]).astype(jnp.int32)
    r = _local_step(x, mod, loss_target, bufs, gains, rel_bias, place)

    dmod = r["dmod"]
    dmod_pad = jnp.zeros((8, N_MOD * d), F32).at[:nb].set(dmod)
    dmod_blk = jnp.transpose(dmod_pad.reshape(8, N_CHIPS, ada_cols), (1, 0, 2))
    small_parts = dict(b_ada=_rowsum8(dmod_pad), g_ffn1=r["dg_ffn1"], g_mix=r["dg_mix"], g_ffn2=r["dg_ffn2"],
                       g_final=r["dg_final"], g_sb_out=r["dg_sb"], g_dil_out=r["dg_dil"], rel_bias=r["drel"])
    halves1, parts_m = r["pending"]
    res = _small_sync(_pack_small(small_parts, r["loss"]), dmod_blk, c_all,
                      carry=_join([_rs_final(halves1), _rs_ici(parts_m)]))
    small_sum, g_wada, gffn1 = res[:3]
    halves_m = [_sum_chips(place, p, l) for p, l in zip(parts_m, res[3:5])]
    gwin, gwout = _alone("rs_last", _rs_final(halves_m))
    gffn2 = r["dffn2"]

    small_w = dict(b_ada=b_ada, g_ffn1=g_ffn1, g_mix=g_mix, g_ffn2=g_ffn2, g_final=g_final,
                   g_sb_out=g_sb_out, g_dil_out=g_dil_out, rel_bias=rel_bias)
    small_m = dict(b_ada=m_b_ada, g_ffn1=m_g_ffn1, g_mix=m_g_mix, g_ffn2=m_g_ffn2, g_final=m_g_final,
                   g_sb_out=m_g_sb_out, g_dil_out=m_g_dil_out, rel_bias=m_rel_bias)
    small_v = dict(b_ada=v_b_ada, g_ffn1=v_g_ffn1, g_mix=v_g_mix, g_ffn2=v_g_ffn2, g_final=v_g_final,
                   g_sb_out=v_g_sb_out, g_dil_out=v_g_dil_out, rel_bias=v_rel_bias)
    shapes = {k: v.shape for k, v in small_w.items()}
    sg, sd, sm, sv = _adamw(_pack_small(small_w), small_sum.reshape(1, SMALL_ROWS, 128), 0,
                            _pack_small(small_m), _pack_small(small_v))
    sg, loss = _unpack_small(sg, shapes)
    sd, _ = _unpack_small(sd, shapes)
    sm, _ = _unpack_small(sm, shapes)
    sv, _ = _unpack_small(sv, shapes)

    big = {}

    def upd(name, w, g_arr, sel, m, v, transposed=False):
        swap = (lambda a: jnp.swapaxes(a, -1, -2)) if transposed else (lambda a: a)
        w2, m2, v2 = [swap(a)[0] for a in (w, m, v)]
        big[name] = [swap(a[None]) for a in _adamw(w2, g_arr, sel, m2, v2)]

    upd("w_ada", w_ada, g_wada.reshape(1, d, ada_cols), 0, m_w_ada, v_w_ada)
    upd("w1_gate", w1_gate, gffn1, 0, m_w1_gate, v_w1_gate, transposed=True)
    upd("w1_up", w1_up, gffn1, 1, m_w1_up, v_w1_up, transposed=True)
    upd("w1_down", w1_down, gffn1, 2, m_w1_down, v_w1_down)
    upd("w_in", w_in, gwin, 0, m_w_in, v_w_in)
    upd("w_out", w_out, gwout, 0, m_w_out, v_w_out)
    upd("w2_gate", w2_gate, gffn2, 0, m_w2_gate, v_w2_gate, transposed=True)
    upd("w2_up", w2_up, gffn2, 1, m_w2_up, v_w2_up, transposed=True)
    upd("w2_down", w2_down, gffn2, 2, m_w2_down, v_w2_down)

    names = ["w_ada", "b_ada", "g_ffn1", "w1_gate", "w1_up", "w1_down", "g_mix", "w_in", "g_sb_out", "g_dil_out",
             "w_out", "rel_bias", "g_ffn2", "w2_gate", "w2_up", "w2_down", "g_final"]
    outs = [loss, r["grad_x"]]
    for k, small in enumerate((sg, sd, sm, sv)):
        for name in names:
            outs.append(big[name][k] if name in big else small[name])
    return tuple(outs)
```

```python
import math

import numpy as np
import jax
import jax.numpy as jnp
from jax import lax
from jax.experimental import pallas as pl
from jax.experimental.pallas import tpu as pltpu

F32 = jnp.float32
BF = jnp.bfloat16
MESH = pl.DeviceIdType.MESH

HEAD_DIM = 64
N_HEADS = 8
D_GRP = N_HEADS * HEAD_DIM
DIL_CONFIGS = ((128, 1), (512, 4), (2048, 16))
N_STEPS = 128
BLOCK = 128
N_BUCKETS = 32
MAX_DISTANCE = 2048
N_MOD = 9
EPS = 1e-6
NEG_INF = -1e30
SCALE = HEAD_DIM ** -0.5

ADAM_LR = 0.001
ADAM_B1 = 0.9
ADAM_B2 = 0.999
ADAM_EPS = 1e-08
ADAM_WD = 0.01
ADAM_STEP = 10

N_CHIPS = 4
N_DEV = 8
VMEM_LIMIT = 56 * 1024 * 1024
TM = 512
TQ = 256
KB = 256
SMALL_ROWS = 120


def _cp(n_axes=0, **kw):
    sem = ("arbitrary",) * n_axes if n_axes else None
    return pltpu.CompilerParams(dimension_semantics=sem, vmem_limit_bytes=VMEM_LIMIT, **kw)


def _nn(a, b):
    return jnp.dot(a, b, preferred_element_type=F32)


def _nt(a, b):
    return lax.dot_general(a, b, (((1,), (1,)), ((), ())), preferred_element_type=F32)


def _tn(a, b):
    return lax.dot_general(a, b, (((0,), (0,)), ((), ())), preferred_element_type=F32)


def _twice(m):
    return jnp.concatenate([m, m], axis=0)


def _nn2(x, m2):
    hi = x.astype(BF)
    lo = (x - hi.astype(F32)).astype(BF)
    return _nn(jnp.concatenate([hi, lo], axis=1), m2)


def _softplus(z):
    return jnp.maximum(z, 0.0) + jnp.log(1.0 + jnp.exp(-jnp.abs(z)))


def _sds(shape, dtype):
    return jax.ShapeDtypeStruct(shape, dtype)


def _whole(a):
    nd = a.ndim
    return pl.BlockSpec(a.shape, lambda *_: (0,) * nd, pipeline_mode=pl.Buffered(1))


def _modnorm_bwd_tile(dh, xv, gv, scv, dxo):
    r = lax.rsqrt(jnp.mean(xv * xv, axis=-1, keepdims=True) + EPS)
    n = xv * r
    ng = n * gv
    dsh = jnp.sum(dh, axis=0, keepdims=True)
    dsc = jnp.sum(dh * ng, axis=0, keepdims=True)
    dy = dh * (1.0 + scv)
    dg = jnp.sum(dy * n, axis=0, keepdims=True)
    dn = dy * gv
    dx = dxo + r * (dn - n * jnp.mean(dn * n, axis=-1, keepdims=True))
    return dx, dsh, dsc, dg


def _acc_rows(ref, val, first):
    @pl.when(first)
    def _():
        ref[...] = val

    @pl.when(jnp.logical_not(first))
    def _():
        ref[...] += val


def _modnorm_tile(x_ref, g_ref, sc_ref, sh_ref):
    xv = x_ref[...]
    r = lax.rsqrt(jnp.mean(xv * xv, axis=-1, keepdims=True) + EPS)
    return (((xv * r) * g_ref[...]) * (1.0 + sc_ref[...]) + sh_ref[...]).astype(BF)


def _ffn_up(x, g, sc, sh, wgu, seq, carry=None):
    t, d = x.shape
    fs = wgu.shape[-1]
    per = seq // TM

    def body(x_ref, g_ref, sc_ref, sh_ref, w_ref, h_ref, p_ref, q_ref, s_ref):
        hv = _modnorm_tile(x_ref, g_ref, sc_ref, sh_ref)
        h_ref[...] = hv
        for j in range(N_CHIPS):
            a = _nn(hv, w_ref[j, 0])
            u = _nn(hv, w_ref[j, 1])
            sig = jax.nn.sigmoid(a)
            q = a * sig
            p_ref[j] = (u * (sig * (1.0 + a * (1.0 - sig)))).astype(BF)
            q_ref[j] = q.astype(BF)
            s_ref[j] = (q * u).astype(BF)

    row = pl.BlockSpec((TM, d), lambda m: (m, 0))
    ex = pl.BlockSpec((None, 1, d), lambda m: (m // per, 0, 0))
    blk = pl.BlockSpec((N_CHIPS, TM, fs), lambda m: (0, m, 0))
    return _call(
        body, "ffn_up", (t // TM,),
        [row, pl.BlockSpec((1, d), lambda m: (0, 0)), ex, ex, _whole(wgu)],
        [row, blk, blk, blk],
        [_sds((t, d), BF)] + [_sds((N_CHIPS, t, fs), BF)] * 3,
        (x, g, sc, sh, wgu), carry=carry)


def _ffn_down(s, wd, x, gt, seq, coef, carry=None):
    _, t, fs = s.shape
    d = x.shape[-1]
    per = seq // TM

    def body(s_ref, w_ref, x_ref, gt_ref, f_ref, xo_ref):
        f = _nn(s_ref[0], w_ref[0, 0])
        for j in range(1, N_CHIPS):
            f = f + _nn(s_ref[j], w_ref[j, 0])
        f_ref[...] = f.astype(BF)
        xo_ref[...] = x_ref[...] + (coef * gt_ref[...]) * f

    row = pl.BlockSpec((TM, d), lambda m: (m, 0))
    return _call(
        body, "ffn_down", (t // TM,),
        [pl.BlockSpec((N_CHIPS, TM, fs), lambda m: (0, m, 0)), _whole(wd), row,
         pl.BlockSpec((None, 1, d), lambda m: (m // per, 0, 0))],
        [row, row], [_sds((t, d), BF), _sds((t, d), F32)], (s, wd, x, gt), carry=carry)


def _ffn_bwd_ds(dxo, gt, f, wd, p, q, seq, coef, carry=None):
    t, d = dxo.shape
    fs = p.shape[-1]
    per = seq // TM
    nb = t // seq

    def body(dxo_ref, gt_ref, f_ref, w_ref, p_ref, q_ref, da_ref, du_ref, df_ref, dgt_ref):
        m = pl.program_id(0)
        dxv = dxo_ref[...]
        df = ((coef * gt_ref[...]) * dxv).astype(BF)
        df_ref[...] = df
        _acc_rows(dgt_ref, coef * jnp.sum(dxv * f_ref[...].astype(F32), axis=0, keepdims=True), m % per == 0)
        for j in range(N_CHIPS):
            ds = _nt(df, w_ref[j, 0])
            da_ref[j] = (ds * p_ref[j].astype(F32)).astype(BF)
            du_ref[j] = (ds * q_ref[j].astype(F32)).astype(BF)

    row = pl.BlockSpec((TM, d), lambda m: (m, 0))
    blk = pl.BlockSpec((N_CHIPS, TM, fs), lambda m: (0, m, 0))
    ex = pl.BlockSpec((None, 1, d), lambda m: (m // per, 0, 0))
    return _call(
        body, "ffn_bwd_ds", (t // TM,),
        [row, ex, row, _whole(wd), blk, blk],
        [blk, blk, row, ex],
        [_sds((N_CHIPS, t, fs), BF), _sds((N_CHIPS, t, fs), BF), _sds((t, d), BF), _sds((nb, 1, d), F32)],
        (dxo, gt, f, wd, p, q), carry=carry)


TM_X = 256


def _ffn_bwd_x(dxo, gt, f, wd, p, q, wgu, x, g, sc, seq, coef):
    t, d = dxo.shape
    fs = p.shape[-1]
    per = seq // TM_X
    nb = t // seq

    def body(dxo_ref, gt_ref, f_ref, wd_ref, p_ref, q_ref, w_ref, x_ref, g_ref, sc_ref,
             da_ref, du_ref, df_ref, dgt_ref, dx_ref, dsh_ref, dsc_ref, dg_ref):
        m = pl.program_id(0)
        dxv = dxo_ref[...]
        df = ((coef * gt_ref[...]) * dxv).astype(BF)
        df_ref[...] = df
        _acc_rows(dgt_ref, coef * jnp.sum(dxv * f_ref[...].astype(F32), axis=0, keepdims=True), m % per == 0)
        dh = None
        for j in range(N_CHIPS):
            ds = _nt(df, wd_ref[j, 0])
            da = (ds * p_ref[j].astype(F32)).astype(BF)
            du = (ds * q_ref[j].astype(F32)).astype(BF)
            da_ref[j] = da
            du_ref[j] = du
            part = _nt(da, w_ref[j, 0]) + _nt(du, w_ref[j, 1])
            dh = part if dh is None else dh + part
        dx, dsh, dsc, dg = _modnorm_bwd_tile(dh, x_ref[...], g_ref[...], sc_ref[...], dxv)
        dx_ref[...] = dx
        _acc_rows(dsh_ref, dsh, m % per == 0)
        _acc_rows(dsc_ref, dsc, m % per == 0)
        _acc_rows(dg_ref, dg, m == 0)

    row = pl.BlockSpec((TM_X, d), lambda m: (m, 0))
    blk = pl.BlockSpec((N_CHIPS, TM_X, fs), lambda m: (0, m, 0))
    ex = pl.BlockSpec((None, 1, d), lambda m: (m // per, 0, 0))
    vec = pl.BlockSpec((1, d), lambda m: (0, 0))
    exs = _sds((nb, 1, d), F32)
    return pl.pallas_call(
        body, name="ffn_bwd_x", grid=(t // TM_X,),
        in_specs=[row, ex, row, _whole(wd), blk, blk, _whole(wgu), row, vec, ex],
        out_specs=[blk, blk, row, ex, row, ex, ex, vec],
        out_shape=[_sds((N_CHIPS, t, fs), BF), _sds((N_CHIPS, t, fs), BF), _sds((t, d), BF), exs,
                   _sds((t, d), F32), exs, exs, _sds((1, d), F32)],
        compiler_params=_cp(1))(dxo, gt, f, wd, p, q, wgu, x, g, sc)


TK_W = 1024


def _ffn_bwd_w(h, da, du, s, df):
    t, d = h.shape
    fs = da.shape[-1]

    def body(h_ref, da_ref, du_ref, s_ref, df_ref, o_ref):
        kt = pl.program_id(1)
        hv = h_ref[...]
        parts = (_tn(da_ref[...], hv), _tn(du_ref[...], hv), _tn(s_ref[...], df_ref[...]))

        @pl.when(kt == 0)
        def _():
            for i, p in enumerate(parts):
                o_ref[i] = p

        @pl.when(kt != 0)
        def _():
            for i, p in enumerate(parts):
                o_ref[i] += p

    row = pl.BlockSpec((TK_W, d), lambda j, kt: (kt, 0))
    blk = pl.BlockSpec((None, TK_W, fs), lambda j, kt: (j, kt, 0))
    return pl.pallas_call(
        body, name="ffn_bwd_w", grid=(N_CHIPS, t // TK_W),
        in_specs=[row, blk, blk, blk, row],
        out_specs=pl.BlockSpec((None, 3, fs, d), lambda j, kt: (j, 0, 0, 0)),
        out_shape=_sds((N_CHIPS, 3, fs, d), F32),
        compiler_params=_cp(2))(h, da, du, s, df)


def _ffn_bwd_dh(da, du, wgu, x, g, sc, dxo, seq, carry=None):
    _, t, fs = da.shape
    d = x.shape[-1]
    per = seq // TM
    nb = t // seq

    def body(da_ref, du_ref, w_ref, x_ref, g_ref, sc_ref, dxo_ref, dx_ref, dsh_ref, dsc_ref, dg_ref):
        m = pl.program_id(0)
        dh = _nt(da_ref[0], w_ref[0, 0]) + _nt(du_ref[0], w_ref[0, 1])
        for j in range(1, N_CHIPS):
            dh = dh + _nt(da_ref[j], w_ref[j, 0]) + _nt(du_ref[j], w_ref[j, 1])
        dx, dsh, dsc, dg = _modnorm_bwd_tile(dh, x_ref[...], g_ref[...], sc_ref[...], dxo_ref[...])
        dx_ref[...] = dx
        _acc_rows(dsh_ref, dsh, m % per == 0)
        _acc_rows(dsc_ref, dsc, m % per == 0)
        _acc_rows(dg_ref, dg, m == 0)

    row = pl.BlockSpec((TM, d), lambda m: (m, 0))
    blk = pl.BlockSpec((N_CHIPS, TM, fs), lambda m: (0, m, 0))
    ex = pl.BlockSpec((None, 1, d), lambda m: (m // per, 0, 0))
    vec = pl.BlockSpec((1, d), lambda m: (0, 0))
    return _call(
        body, "ffn_bwd_dh", (t // TM,),
        [blk, blk, _whole(wgu), row, vec, ex, row],
        [row, ex, ex, vec],
        [_sds((t, d), F32), _sds((nb, 1, d), F32), _sds((nb, 1, d), F32), _sds((1, d), F32)],
        (da, du, wgu, x, g, sc, dxo), carry=carry)


def _qkv_proj(x, g, sc, sh, w_in, seq, carry=None):
    t, d = x.shape
    wc = w_in.shape[-1]
    per = seq // TM

    dils = [dil for _, dil in DIL_CONFIGS if dil > 1]

    def body(x_ref, g_ref, sc_ref, sh_ref, w_ref, h_ref, o_ref, *rest):
        res_refs, buf = rest[:len(dils)], rest[len(dils)]
        hv = _modnorm_tile(x_ref, g_ref, sc_ref, sh_ref)
        h_ref[...] = hv
        for j in range(N_CHIPS):
            rf = _nn(hv, w_ref[j, 0])
            r = rf.astype(BF)
            for a, lc, off, width in _col_pieces(j, wc):
                o_ref[a, :, lc:lc + width] = r[:, off:off + width]
                if a < 3:
                    continue
                for c0 in range(0, width, 128):
                    cg = (lc + c0) // 128
                    buf[...] = rf[:, off + c0:off + c0 + 128]
                    for ref, dil in zip(res_refs, dils):
                        for rr in range(dil):
                            ref[a - 3, :, rr * D_GRP + cg * 128:rr * D_GRP + (cg + 1) * 128] = (
                                buf[pl.ds(rr, TM // dil, stride=dil), :].astype(BF))

    row = pl.BlockSpec((TM, d), lambda m: (m, 0))
    ex = pl.BlockSpec((None, 1, d), lambda m: (m // per, 0, 0))
    return _call(
        body, "qkv_proj", (t // TM,),
        [row, pl.BlockSpec((1, d), lambda m: (0, 0)), ex, ex, _whole(w_in)],
        [row, pl.BlockSpec((6, TM, D_GRP), lambda m: (0, m, 0))]
        + [pl.BlockSpec((3, TM // dil, dil * D_GRP), lambda m: (0, m, 0)) for dil in dils],
        [_sds((t, d), BF), _sds((6, t, D_GRP), BF)] + [_sds((3, t // dil, dil * D_GRP), BF) for dil in dils],
        (x, g, sc, sh, w_in), scratch=[pltpu.VMEM((TM, 128), F32)], carry=carry)


def _col_pieces(j, wc):
    out, off = [], 0
    while off < wc:
        a, lc = divmod(j * wc + off, D_GRP)
        width = min(D_GRP - lc, wc - off)
        out.append((a, lc, off, width))
        off += width
    return out


def _chip_cols(g6_ref, j, wc):
    return jnp.concatenate([g6_ref[a, :, lc:lc + width] for a, lc, _, width in _col_pieces(j, wc)], axis=1)


def _mix_out(on_sb, on_dil, w_out, x, gt, seq):
    t, d = x.shape
    per = seq // TM

    def body(a_ref, b_ref, w_ref, x_ref, gt_ref, t_ref, xo_ref):
        tv = _nn(a_ref[...], w_ref[0:D_GRP, :]) + _nn(b_ref[...], w_ref[D_GRP:2 * D_GRP, :])
        t_ref[...] = tv.astype(BF)
        xo_ref[...] = x_ref[...] + gt_ref[...] * tv

    row = pl.BlockSpec((TM, d), lambda m: (m, 0))
    half = pl.BlockSpec((TM, D_GRP), lambda m: (m, 0))
    return pl.pallas_call(
        body, name="mix_out", grid=(t // TM,),
        in_specs=[half, half, pl.BlockSpec((2 * D_GRP, d), lambda m: (0, 0)), row,
                  pl.BlockSpec((None, 1, d), lambda m: (m // per, 0, 0))],
        out_specs=[row, row],
        out_shape=[_sds((t, d), BF), _sds((t, d), F32)],
        compiler_params=_cp(1))(on_sb, on_dil, w_out, x, gt)


def _sb_masks():
    lane = lax.broadcasted_iota(jnp.int32, (1, 2 * HEAD_DIM), 1)
    hm0 = lane < HEAD_DIM
    rel = lax.broadcasted_iota(jnp.int32, (TQ, KB), 0) - lax.broadcasted_iota(jnp.int32, (TQ, KB), 1)
    kr = lax.broadcasted_iota(jnp.int32, (KB, KB), 0)
    kc = lax.broadcasted_iota(jnp.int32, (KB, KB), 1)
    return hm0, rel, kr, kc


def _headnorm_pair(o, gv, hm0):
    o2 = o * o
    ms0 = jnp.sum(jnp.where(hm0, o2, 0.0), axis=-1, keepdims=True) * (1.0 / HEAD_DIM)
    ms1 = jnp.sum(jnp.where(hm0, 0.0, o2), axis=-1, keepdims=True) * (1.0 / HEAD_DIM)
    r = jnp.where(hm0, lax.rsqrt(ms0 + EPS), lax.rsqrt(ms1 + EPS))
    return (o * r) * gv


SB_DEAD = -104.0


def _alive(c_l):
    return (jnp.max(c_l) > SB_DEAD).astype(jnp.int32)


def _sb_fwd(qkv6, g_sb, nb, seq, carry=None):
    nq = seq // TQ

    def body(q_ref, k_ref, v_ref, g_ref, o_ref, on_ref):
        qi = pl.program_id(2)
        hm0, rel, kr, kc = _sb_masks()
        upper = _twice((kr > kc).astype(BF))
        heads = _dil_masks()[0]
        qs = _stack_heads(q_ref[...] * SCALE, heads)
        causal2 = jnp.concatenate([rel] * GRP_HEADS, axis=0) > 0

        def block(kj, causal, c_l, acc):
            ks = pl.multiple_of(kj * KB, KB)
            z = _nt(qs, k_ref[pl.ds(ks, KB), :])
            sp = _softplus(z)
            spm = sp if causal is None else jnp.where(causal, sp, 0.0)
            suf = _nn2(spm, upper)
            w = jnp.exp((z - sp) + (c_l - suf))
            if causal is not None:
                w = jnp.where(causal, w, 0.0)
            return c_l - (suf[:, 0:1] + spm[:, 0:1]), acc + _nn(w.astype(BF), v_ref[pl.ds(ks, KB), :])

        c_l, acc = block(qi, causal2, jnp.zeros((GRP_HEADS * TQ, 1), F32), jnp.zeros((GRP_HEADS * TQ, GRP_W), F32))

        def cond(carry):
            return jnp.logical_and(carry[0] <= qi, carry[1] > 0)

        def kbody(carry):
            it, _, c_l, acc = carry
            c_l, acc = block(qi - it, None, c_l, acc)
            return it + 1, _alive(c_l), c_l, acc

        acc = lax.while_loop(cond, kbody, (jnp.int32(1), _alive(c_l), c_l, acc))[3]
        o = _unstack_heads(acc, heads, TQ)
        o_ref[...] = o.astype(BF)
        gv = g_ref[...]
        for half in range(GRP_W // 128):
            lanes = slice(half * 128, (half + 1) * 128)
            on_ref[:, lanes] = _headnorm_pair(o[:, lanes], gv[:, lanes], hm0).astype(BF)

    w = GRP_W
    full = lambda i: pl.BlockSpec((None, None, seq, w), lambda b, hp, q: (i, b, 0, hp))
    qblk = pl.BlockSpec((None, None, TQ, w), lambda b, hp, q: (0, b, q, hp))
    oblk = pl.BlockSpec((None, TQ, w), lambda b, hp, q: (b, q, hp))
    return _call(
        body, "sb_fwd", (nb, N_HEADS // GRP_HEADS, nq),
        [qblk, full(1), full(2), pl.BlockSpec((1, w), lambda b, hp, q: (0, hp))],
        [oblk, oblk],
        [_sds((nb, seq, D_GRP), BF), _sds((nb, seq, D_GRP), BF)],
        (qkv6, qkv6, qkv6, g_sb), carry=carry)


def _sb_bwd(qkv6, do, nb, seq, carry=None):
    nq = seq // TQ
    nk = seq // KB

    def body(q_ref, k_ref, v_ref, do_ref, out_ref, dk_acc, dv_acc, g_st, s_st):
        qi = pl.program_id(2)
        hm0, rel, kr, kc = _sb_masks()
        upper = _twice((kr > kc).astype(BF))
        lower = (kr < kc).astype(BF)

        @pl.when(qi == 0)
        def _():
            dk_acc[...] = jnp.zeros_like(dk_acc)
            dv_acc[...] = jnp.zeros_like(dv_acc)

        heads = _dil_masks()[0]
        qs = _stack_heads(q_ref[...] * SCALE, heads)
        dos = _stack_heads(do_ref[...], heads)
        causal2 = jnp.concatenate([rel] * GRP_HEADS, axis=0) > 0

        def weights(kj, causal, c_l):
            ks = pl.multiple_of(kj * KB, KB)
            vb = v_ref[pl.ds(ks, KB), :]
            z = _nt(qs, k_ref[pl.ds(ks, KB), :])
            sp = _softplus(z)
            spm = sp if causal is None else jnp.where(causal, sp, 0.0)
            suf = _nn2(spm, upper)
            lsz = z - sp
            w = jnp.exp(lsz + (c_l - suf))
            if causal is not None:
                w = jnp.where(causal, w, 0.0)
            g_st[kj] = w * _nt(dos, vb)
            s_st[kj] = jnp.exp(lsz)
            dv_acc[pl.ds(ks, KB), :] += _tn(w.astype(BF), dos)
            return c_l - (suf[:, 0:1] + spm[:, 0:1])

        zc = jnp.zeros((GRP_HEADS * TQ, 1), F32)
        c_l = weights(qi, causal2, zc)

        def acond(carry):
            return jnp.logical_and(carry[0] <= qi, carry[1] > 0)

        def abody(carry):
            c_l = weights(qi - carry[0], None, carry[2])
            return carry[0] + 1, _alive(c_l), c_l

        n_used = lax.while_loop(acond, abody, (jnp.int32(1), _alive(c_l), c_l))[0]

        def grads(kj, causal, c_g, dq):
            ks = pl.multiple_of(kj * KB, KB)
            kb = k_ref[pl.ds(ks, KB), :]
            g = g_st[kj]
            sig = s_st[kj]
            pre = _nn(g.astype(BF), lower)
            dz = g * (1.0 - sig) - sig * (pre + c_g)
            if causal is not None:
                dz = jnp.where(causal, dz, 0.0)
            dzb = dz.astype(BF)
            dk_acc[pl.ds(ks, KB), :] += _tn(dzb, qs)
            return c_g + (pre[:, KB - 1:KB] + g[:, KB - 1:KB]), dq + _nn(dzb, kb)

        c_g, dq = lax.fori_loop(qi - n_used + 1, qi, lambda kj, cr: grads(kj, None, *cr),
                                (zc, jnp.zeros((GRP_HEADS * TQ, GRP_W), F32)))
        _, dq = grads(qi, causal2, c_g, dq)
        dq = _unstack_heads(dq, heads, TQ) * SCALE
        out_ref[0, pl.ds(pl.multiple_of(qi * TQ, TQ), TQ), :] = dq.astype(BF)

        @pl.when(qi == nq - 1)
        def _():
            out_ref[1] = dk_acc[...].astype(BF)
            out_ref[2] = dv_acc[...].astype(BF)

    w = GRP_W
    full = lambda i: pl.BlockSpec((None, None, seq, w), lambda b, hp, q: (i, b, 0, hp))
    qblk = pl.BlockSpec((None, None, TQ, w), lambda b, hp, q: (0, b, q, hp))
    oblk = pl.BlockSpec((None, TQ, w), lambda b, hp, q: (b, q, hp))
    return _call(
        body, "sb_bwd", (nb, N_HEADS // GRP_HEADS, nq),
        [qblk, full(1), full(2), oblk],
        [pl.BlockSpec((3, None, seq, w), lambda b, hp, q: (0, b, 0, hp))],
        [_sds((6, nb, seq, D_GRP), BF)], (qkv6, qkv6, qkv6, do),
        scratch=[pltpu.VMEM((seq, w), F32), pltpu.VMEM((seq, w), F32),
                 pltpu.VMEM((nk, GRP_HEADS * TQ, KB), F32), pltpu.VMEM((nk, GRP_HEADS * TQ, KB), F32)],
        carry=carry)


def _t5_bucket(n):
    max_exact = N_BUCKETS // 2
    nf = np.maximum(n, 1).astype(np.float32)
    large = max_exact + (np.log(nf / max_exact) / math.log(MAX_DISTANCE / max_exact)
                         * (N_BUCKETS - max_exact)).astype(np.int32)
    large = np.minimum(large, N_BUCKETS - 1)
    return np.where(n < max_exact, n, large).astype(np.int32)


def _bucket_map(dilation):
    step = BLOCK + np.arange(BLOCK)[:, None] - np.arange(2 * BLOCK)[None, :]
    return _t5_bucket(np.clip(step, 0, N_STEPS) * dilation)


GRP_HEADS = 4
GRP_W = GRP_HEADS * HEAD_DIM


def _dil_masks():
    lane = lax.broadcasted_iota(jnp.int32, (1, GRP_W), 1)
    heads = [jnp.logical_and(lane >= HEAD_DIM * i, lane < HEAD_DIM * (i + 1)) for i in range(GRP_HEADS)]
    iq = jnp.bitwise_and(lax.broadcasted_iota(jnp.int32, (GRP_HEADS * BLOCK, BLOCK), 0), BLOCK - 1)
    ik = lax.broadcasted_iota(jnp.int32, (GRP_HEADS * BLOCK, BLOCK), 1)
    return heads, ik <= iq, ik >= iq


def _stack_heads(x, heads):
    zero = jnp.zeros_like(x)
    return jnp.concatenate([jnp.where(hm, x, zero) for hm in heads], axis=0)


def _unstack_heads(xs, heads, rows=BLOCK):
    out = xs[0:rows]
    for i in range(1, GRP_HEADS):
        out = jnp.where(heads[i], xs[i * rows:(i + 1) * rows], out)
    return out


def _dil_rows(n):
    rs = pl.multiple_of(n * BLOCK, BLOCK)
    ps = pl.multiple_of(jnp.maximum(n - 1, 0) * BLOCK, BLOCK)
    return pl.ds(rs, BLOCK), pl.ds(ps, BLOCK)


def _dil_probs(qs, kc, kp, b_ref, gi, valid_c, valid_p):
    rows = slice(gi * GRP_HEADS * BLOCK, (gi + 1) * GRP_HEADS * BLOCK)
    zc = _nt(qs, kc) * SCALE + b_ref[rows, BLOCK:2 * BLOCK]
    zp = _nt(qs, kp) * SCALE + b_ref[rows, 0:BLOCK]
    zc = jnp.where(valid_c, zc, NEG_INF)
    zp = jnp.where(valid_p, zp, NEG_INF)
    m = jnp.maximum(jnp.max(zc, axis=-1, keepdims=True), jnp.max(zp, axis=-1, keepdims=True))
    ec = jnp.exp(zc - m)
    ep = jnp.exp(zp - m)
    den = jnp.sum(ec, axis=-1, keepdims=True) + jnp.sum(ep, axis=-1, keepdims=True)
    return ec, ep, den, m


def _dil_fwd(qkv6r, base, bias, nb, sub_len, dilation):
    n_blk = sub_len // BLOCK

    def body(q_ref, k_ref, v_ref, b_ref, o_ref, l_ref):
        heads, valid_c, valid_p0 = _dil_masks()

        def nbody(n, carry):
            cur, prev = _dil_rows(n)
            valid_p = jnp.logical_and(valid_p0, n > 0)
            for gi in range(N_HEADS // GRP_HEADS):
                lanes = slice(gi * GRP_W, (gi + 1) * GRP_W)
                qs = _stack_heads(q_ref[cur, lanes], heads)
                ec, ep, den, m = _dil_probs(qs, k_ref[cur, lanes], k_ref[prev, lanes], b_ref, gi, valid_c, valid_p)
                o = (_nn(ec.astype(BF), v_ref[cur, lanes]) + _nn(ep.astype(BF), v_ref[prev, lanes])) / den
                o_ref[cur, lanes] = _unstack_heads(o, heads).astype(BF)
                l_ref[cur, lanes] = _unstack_heads(jnp.broadcast_to(m + jnp.log(den), o.shape), heads)
            return carry

        lax.fori_loop(0, n_blk, nbody, 0)

    seqblk = lambda i: pl.BlockSpec((None, None, sub_len, D_GRP), lambda b, r: (i, b, 0, r))
    oblk = pl.BlockSpec((None, sub_len, D_GRP), lambda b, r: (b, 0, r))
    shp = _sds((nb, sub_len, dilation * D_GRP), F32)
    return pl.pallas_call(
        body, name="dil_fwd_%d" % dilation, grid=(nb, dilation),
        in_specs=[seqblk(base), seqblk(base + 1), seqblk(base + 2), _whole(bias)],
        out_specs=[oblk, oblk], out_shape=[_sds(shp.shape, BF), shp],
        compiler_params=_cp(2))(qkv6r, qkv6r, qkv6r, bias)


def _dil_bwd(qkv6r, base, bias, do_c, dd_c, nb, sub_len, dilation, carry=None):
    n_blk = sub_len // BLOCK

    def body(q_ref, k_ref, v_ref, b_ref, do_ref, dd_ref, out_ref, a_ref, dk_acc, dv_acc):
        heads, valid_c, valid_p0 = _dil_masks()
        first = jnp.logical_and(pl.program_id(0) == 0, pl.program_id(1) == 0)

        @pl.when(first)
        def _():
            a_ref[...] = jnp.zeros_like(a_ref)

        dk_acc[...] = jnp.zeros_like(dk_acc)
        dv_acc[...] = jnp.zeros_like(dv_acc)

        def nbody(n, carry):
            cur, prev = _dil_rows(n)
            valid_p = jnp.logical_and(valid_p0, n > 0)
            for gi in range(N_HEADS // GRP_HEADS):
                lanes = slice(gi * GRP_W, (gi + 1) * GRP_W)
                kc, kp = k_ref[cur, lanes], k_ref[prev, lanes]
                vc, vp = v_ref[cur, lanes], v_ref[prev, lanes]
                qs = _stack_heads(q_ref[cur, lanes], heads)
                dos = _stack_heads(do_ref[cur, lanes], heads).astype(BF)
                dds = jnp.sum(_stack_heads(dd_ref[cur, lanes], heads), axis=-1, keepdims=True) * (1.0 / HEAD_DIM)
                ec, ep, den, _ = _dil_probs(qs, kc, kp, b_ref, gi, valid_c, valid_p)
                inv = 1.0 / den
                pc = ec * inv
                pp = ep * inv
                dzc = pc * (_nt(dos, vc) + dds)
                dzp = pp * (_nt(dos, vp) + dds)
                rows = slice(gi * GRP_HEADS * BLOCK, (gi + 1) * GRP_HEADS * BLOCK)
                a_ref[rows, BLOCK:2 * BLOCK] += dzc
                a_ref[rows, 0:BLOCK] += dzp
                dzcb = (dzc * SCALE).astype(BF)
                dzpb = (dzp * SCALE).astype(BF)
                out_ref[0, cur, lanes] = _unstack_heads(_nn(dzcb, kc) + _nn(dzpb, kp), heads).astype(BF)
                dk_acc[cur, lanes] += _tn(dzcb, qs)
                dk_acc[prev, lanes] += _tn(dzpb, qs)
                dv_acc[cur, lanes] += _tn(pc.astype(BF), dos)
                dv_acc[prev, lanes] += _tn(pp.astype(BF), dos)
            return carry

        lax.fori_loop(0, n_blk, nbody, 0)
        out_ref[1] = dk_acc[...].astype(BF)
        out_ref[2] = dv_acc[...].astype(BF)

    seqblk = lambda i: pl.BlockSpec((None, None, sub_len, D_GRP), lambda b, r: (i, b, 0, r))
    oblk = pl.BlockSpec((None, sub_len, D_GRP), lambda b, r: (b, 0, r))
    return _call(
        body, "dil_bwd_%d" % dilation, (nb, dilation),
        [seqblk(base), seqblk(base + 1), seqblk(base + 2), _whole(bias), oblk, oblk],
        [pl.BlockSpec((3, None, sub_len, D_GRP), lambda b, r: (0, b, 0, r)),
         pl.BlockSpec((N_HEADS * BLOCK, 2 * BLOCK), lambda b, r: (0, 0))],
        [_sds((3, nb, sub_len, dilation * D_GRP), BF), _sds((N_HEADS * BLOCK, 2 * BLOCK), F32)],
        (qkv6r, qkv6r, qkv6r, bias, do_c, dd_c),
        scratch=[pltpu.VMEM((sub_len, D_GRP), F32)] * 2, carry=carry)


def _group_ones():
    idx = np.arange(D_GRP) // HEAD_DIM
    return jnp.asarray((idx[:, None] == idx[None, :]).astype(np.float32), dtype=BF)


def _dil_alphas(l1, l4, l16):
    mx = jnp.maximum(jnp.maximum(l1, l4), l16)
    e1 = jnp.exp(l1 - mx)
    e4 = jnp.exp(l4 - mx)
    e16 = jnp.exp(l16 - mx)
    den = e1 + e4 + e16
    return e1 / den, e4 / den, e16 / den


def _residue_spec(dil):
    return pl.BlockSpec((TM // dil, dil * D_GRP), lambda m: (m, 0))


def _from_residue(src, dil, cg, buf):
    if dil == 1:
        return src[:, cg * 128:(cg + 1) * 128].astype(F32)
    for r in range(dil):
        buf[pl.ds(r, TM // dil, stride=dil), :] = (
            src[:, r * D_GRP + cg * 128:r * D_GRP + (cg + 1) * 128].astype(F32))
    return buf[...]


def _to_residue(dst, dil, cg, buf, val):
    if dil == 1:
        dst[:, cg * 128:(cg + 1) * 128] = val.astype(dst.dtype)
        return
    buf[...] = val
    for r in range(dil):
        dst[:, r * D_GRP + cg * 128:r * D_GRP + (cg + 1) * 128] = (
            buf[pl.ds(r, TM // dil, stride=dil), :].astype(dst.dtype))


def _pair_sum(x, hm0):
    s0 = jnp.sum(jnp.where(hm0, x, 0.0), axis=-1, keepdims=True)
    s1 = jnp.sum(jnp.where(hm0, 0.0, x), axis=-1, keepdims=True)
    return jnp.where(hm0, s0, s1)


def _dil_comb(os, ls, g_dil):
    t = os[0].shape[0]
    dils = [dil for _, dil in DIL_CONFIGS]

    def body(o1, l1, o4, l4, o16, l16, g_ref, o_ref, on_ref, b0, b1, b2, b3):
        hm0 = lax.broadcasted_iota(jnp.int32, (1, 128), 1) < HEAD_DIM
        for cg in range(D_GRP // 128):
            lanes = slice(cg * 128, (cg + 1) * 128)
            ov = [_from_residue(src, dil, cg, buf) for src, dil, buf in zip((o1, o4, o16), dils, (None, b0, b1))]
            lv = [_from_residue(src, dil, cg, buf) for src, dil, buf in zip((l1, l4, l16), dils, (None, b2, b3))]
            a1, a4, a16 = _dil_alphas(*lv)
            o = a1 * ov[0] + a4 * ov[1] + a16 * ov[2]
            o_ref[:, lanes] = o.astype(BF)
            on_ref[:, lanes] = _headnorm_pair(o, g_ref[:, lanes], hm0).astype(BF)

    blk = pl.BlockSpec((TM, D_GRP), lambda m: (m, 0))
    specs = [_residue_spec(dil) for dil in dils for _ in range(2)]
    return pl.pallas_call(
        body, name="dil_comb", grid=(t // TM,),
        in_specs=specs + [pl.BlockSpec((1, D_GRP), lambda m: (0, 0))],
        out_specs=[blk, blk],
        out_shape=[_sds((t, D_GRP), BF), _sds((t, D_GRP), BF)],
        scratch_shapes=[pltpu.VMEM((TM, 128), F32)] * 4,
        compiler_params=_cp(1))(os[0], ls[0], os[1], ls[1], os[2], ls[2], g_dil)


def _dil_comb_bwd(do, os, ls):
    t = do.shape[0]
    dils = [dil for _, dil in DIL_CONFIGS]

    def body(do_ref, o1, l1, o4, l4, o16, l16, d1, d4, d16, e1, e4, e16, b0, b1, b2, b3):
        hm0 = lax.broadcasted_iota(jnp.int32, (1, 128), 1) < HEAD_DIM
        for cg in range(D_GRP // 128):
            dov = do_ref[:, cg * 128:(cg + 1) * 128].astype(F32)
            ov = [_from_residue(src, dil, cg, buf) for src, dil, buf in zip((o1, o4, o16), dils, (None, b0, b1))]
            lv = [_from_residue(src, dil, cg, buf) for src, dil, buf in zip((l1, l4, l16), dils, (None, b2, b3))]
            al = _dil_alphas(*lv)
            sbar = al[0] * _pair_sum(dov * ov[0], hm0)
            for a_c, o_c in zip(al[1:], ov[1:]):
                sbar = sbar + a_c * _pair_sum(dov * o_c, hm0)
            for a_c, dil, dref, eref in zip(al, dils, (d1, d4, d16), (e1, e4, e16)):
                _to_residue(dref, dil, cg, b0, a_c * dov)
                _to_residue(eref, dil, cg, b1, -a_c * sbar)

    specs = [_residue_spec(dil) for dil in dils]
    return pl.pallas_call(
        body, name="dil_comb_bwd", grid=(t // TM,),
        in_specs=[pl.BlockSpec((TM, D_GRP), lambda m: (m, 0))] + [sp for sp in specs for _ in range(2)],
        out_specs=specs + specs,
        out_shape=[_sds((t // dil, dil * D_GRP), BF) for dil in dils]
        + [_sds((t // dil, dil * D_GRP), F32) for dil in dils],
        scratch_shapes=[pltpu.VMEM((TM, 128), F32)] * 4,
        compiler_params=_cp(1))(do, os[0], ls[0], os[1], ls[1], os[2], ls[2])


def _dqkv_dil_sum(ds, dqkv6):
    t = dqkv6.shape[1]
    dils = [dil for _, dil in DIL_CONFIGS]

    def body(*refs):
        srcs, o_ref, acc = refs[:len(dils)], refs[len(dils) + 1], refs[len(dils) + 2]
        for a in range(3):
            for cg in range(D_GRP // 128):
                for src, dil in zip(srcs, dils):
                    for r in range(dil):
                        part = src[a, :, r * D_GRP + cg * 128:r * D_GRP + (cg + 1) * 128].astype(F32)
                        rows = pl.ds(r, TM // dil, stride=dil) if dil > 1 else slice(None)
                        if dil == dils[0]:
                            acc[rows, :] = part
                        else:
                            acc[rows, :] += part
                o_ref[a, :, cg * 128:(cg + 1) * 128] = acc[...].astype(BF)

    return pl.pallas_call(
        body, name="dqkv_dil_sum", grid=(t // TM,),
        in_specs=[pl.BlockSpec((3, TM // dil, dil * D_GRP), lambda m: (0, m, 0)) for dil in dils]
        + [pl.BlockSpec(memory_space=pl.ANY)],
        out_specs=pl.BlockSpec((3, TM, D_GRP), lambda m: (1, m, 0)),
        out_shape=_sds((6, t, D_GRP), BF), input_output_aliases={len(dils): 0},
        scratch_shapes=[pltpu.VMEM((TM, 128), F32)],
        compiler_params=_cp(1))(*ds, dqkv6)


def _relbias_grad(a_all, onehot):
    def body(a_ref, oh_ref, o_ref):
        acc = jnp.zeros((N_HEADS, N_BUCKETS), F32)
        for c in range(len(DIL_CONFIGS)):
            av = a_ref[c]
            hi = av.astype(BF)
            lo = (av - hi.astype(F32)).astype(BF)
            acc = acc + _nt(hi, oh_ref[c]) + _nt(lo, oh_ref[c])
        o_ref[...] = acc

    return pl.pallas_call(body, name="relbias_grad", out_shape=_sds((N_HEADS, N_BUCKETS), F32),
                          compiler_params=_cp())(a_all, onehot)


def _headnorm_bwd(dn, o, gv, mv):
    ms = _nn2(o * o, mv) * (1.0 / HEAD_DIM)
    r = lax.rsqrt(ms + EPS)
    nrm = o * r
    dg = jnp.sum(dn * nrm, axis=0, keepdims=True)
    dnn = dn * gv
    do = r * (dnn - nrm * (_nn2(dnn * nrm, mv) * (1.0 / HEAD_DIM)))
    return do, dg


def _mix_bwd_out(dx, gt, tv, w_out, o_sb, o_dil, on_sb, on_dil, g_sb, g_dil, ones_g, seq, carry=None):
    t, d = dx.shape
    per = seq // TM
    nb = t // seq

    def body(dx_ref, gt_ref, t_ref, w_ref, osb, odl, onsb, ondl, gsb, gdl, m_ref,
             dosb, dodl, dgt_ref, dgsb, dgdl, dw_ref):
        m = pl.program_id(0)
        dxv = dx_ref[...]
        dt = (gt_ref[...] * dxv).astype(BF)
        _acc_rows(dgt_ref, jnp.sum(dxv * t_ref[...].astype(F32), axis=0, keepdims=True), m % per == 0)
        mv = _twice(m_ref[...])
        don_sb = _nt(dt, w_ref[0:D_GRP, :])
        don_dl = _nt(dt, w_ref[D_GRP:2 * D_GRP, :])
        do1, dg1 = _headnorm_bwd(don_sb, osb[...].astype(F32), gsb[...], mv)
        do2, dg2 = _headnorm_bwd(don_dl, odl[...].astype(F32), gdl[...], mv)
        dosb[...] = do1.astype(BF)
        dodl[...] = do2.astype(BF)
        _acc_rows(dgsb, dg1, m == 0)
        _acc_rows(dgdl, dg2, m == 0)
        p1 = _tn(onsb[...], dt)
        p2 = _tn(ondl[...], dt)

        @pl.when(m == 0)
        def _():
            dw_ref[0:D_GRP, :] = p1
            dw_ref[D_GRP:2 * D_GRP, :] = p2

        @pl.when(m != 0)
        def _():
            dw_ref[0:D_GRP, :] += p1
            dw_ref[D_GRP:2 * D_GRP, :] += p2

    row = pl.BlockSpec((TM, d), lambda m: (m, 0))
    half = pl.BlockSpec((TM, D_GRP), lambda m: (m, 0))
    ex = pl.BlockSpec((None, 1, d), lambda m: (m // per, 0, 0))
    gvec = pl.BlockSpec((1, D_GRP), lambda m: (0, 0))
    wblk = pl.BlockSpec((2 * D_GRP, d), lambda m: (0, 0))
    return _call(
        body, "mix_bwd_out", (t // TM,),
        [row, ex, row, wblk, half, half, half, half, gvec, gvec, pl.BlockSpec((D_GRP, D_GRP), lambda m: (0, 0))],
        [half, half, ex, gvec, gvec, wblk],
        [_sds((t, D_GRP), BF), _sds((t, D_GRP), BF), _sds((nb, 1, d), F32),
         _sds((1, D_GRP), F32), _sds((1, D_GRP), F32), _sds((2 * D_GRP, d), F32)],
        (dx, gt, tv, w_out, o_sb, o_dil, on_sb, on_dil, g_sb, g_dil, ones_g), carry=carry)


def _dw_in(h, dqkv6, carry=None):
    t, d = h.shape
    wc = 6 * D_GRP // N_CHIPS

    def body(h_ref, g_ref, o_ref):
        kt = pl.program_id(0)
        hv = h_ref[...]
        for j in range(N_CHIPS):
            p = _tn(hv, _chip_cols(g_ref, j, wc))

            @pl.when(kt == 0)
            def _(p=p, j=j):
                o_ref[j, 0] = p

            @pl.when(kt != 0)
            def _(p=p, j=j):
                o_ref[j, 0] += p

    return _call(
        body, "dw_in", (t // TK_W,),
        [pl.BlockSpec((TK_W, d), lambda kt: (kt, 0)), pl.BlockSpec((6, TK_W, D_GRP), lambda kt: (0, kt, 0))],
        [pl.BlockSpec((N_CHIPS, 1, d, wc), lambda kt: (0, 0, 0, 0))],
        [_sds((N_CHIPS, 1, d, wc), F32)], (h, dqkv6), carry=carry)


def _mix_bwd_dh(dqkv6, w_in, x, g, sc, dxo, seq, carry=None):
    _, t, _ = dqkv6.shape
    d = x.shape[-1]
    wc = w_in.shape[-1]
    per = seq // TM
    nb = t // seq

    def body(g6_ref, w_ref, x_ref, g_ref, sc_ref, dxo_ref, dx_ref, dsh_ref, dsc_ref, dg_ref):
        m = pl.program_id(0)
        dh = _nt(_chip_cols(g6_ref, 0, wc), w_ref[0, 0])
        for j in range(1, N_CHIPS):
            dh = dh + _nt(_chip_cols(g6_ref, j, wc), w_ref[j, 0])
        dx, dsh, dsc, dg = _modnorm_bwd_tile(dh, x_ref[...], g_ref[...], sc_ref[...], dxo_ref[...])
        dx_ref[...] = dx
        _acc_rows(dsh_ref, dsh, m % per == 0)
        _acc_rows(dsc_ref, dsc, m % per == 0)
        _acc_rows(dg_ref, dg, m == 0)

    row = pl.BlockSpec((TM, d), lambda m: (m, 0))
    ex = pl.BlockSpec((None, 1, d), lambda m: (m // per, 0, 0))
    vec = pl.BlockSpec((1, d), lambda m: (0, 0))
    return _call(
        body, "mix_bwd_dh", (t // TM,),
        [pl.BlockSpec((6, TM, D_GRP), lambda m: (0, m, 0)), _whole(w_in), row, vec, ex, row],
        [row, ex, ex, vec],
        [_sds((t, d), F32), _sds((nb, 1, d), F32), _sds((nb, 1, d), F32), _sds((1, d), F32)],
        (dqkv6, w_in, x, g, sc, dxo), carry=carry)


def _ffn_down_loss(s, wd, x, gt, seq, coef, g, target):
    _, t, fs = s.shape
    d = x.shape[-1]
    per = seq // TM
    steps = t // TM

    def body(s_ref, w_ref, x_ref, gt_ref, g_ref, t_ref, f_ref, dx_ref, dg_ref, loss_ref, lacc):
        m = pl.program_id(0)
        f = _nn(s_ref[0], w_ref[0, 0])
        for j in range(1, N_CHIPS):
            f = f + _nn(s_ref[j], w_ref[j, 0])
        f_ref[...] = f.astype(BF)
        xv = x_ref[...] + (coef * gt_ref[...]) * f
        gv = g_ref[...]
        r = lax.rsqrt(jnp.mean(xv * xv, axis=-1, keepdims=True) + EPS)
        n = xv * r
        err = n * gv - t_ref[...]
        dy = err * (1.0 / d)
        _acc_rows(dg_ref, jnp.sum(dy * n, axis=0, keepdims=True), m == 0)
        dn = dy * gv
        dx_ref[...] = r * (dn - n * jnp.mean(dn * n, axis=-1, keepdims=True))
        _acc_rows(lacc, jnp.sum(err * err, axis=0, keepdims=True), m == 0)

        @pl.when(m == steps - 1)
        def _():
            tot = jnp.sum(lacc[...], axis=-1, keepdims=True) * (0.5 / d)
            loss_ref[...] = jnp.broadcast_to(tot, (1, 128))

    row = pl.BlockSpec((TM, d), lambda m: (m, 0))
    vec = pl.BlockSpec((1, d), lambda m: (0, 0))
    return pl.pallas_call(
        body, name="ffn_down_loss", grid=(steps,),
        in_specs=[pl.BlockSpec((N_CHIPS, TM, fs), lambda m: (0, m, 0)), _whole(wd), row,
                  pl.BlockSpec((None, 1, d), lambda m: (m // per, 0, 0)), vec, row],
        out_specs=[row, row, vec, pl.BlockSpec((1, 128), lambda m: (0, 0))],
        out_shape=[_sds((t, d), BF), _sds((t, d), F32), _sds((1, d), F32), _sds((1, 128), F32)],
        scratch_shapes=[pltpu.VMEM((1, d), F32)],
        compiler_params=_cp(1))(s, wd, x, gt, g, target)


def _row_tile(rows, cols):
    best = rows
    for tr in range(8, rows + 1, 8):
        if rows % tr == 0 and tr * cols * 4 <= (1 << 20):
            best = tr
    if best * cols * 4 > (1 << 21):
        best = 8
    return best


def _adamw(w, g_arr, g_sel, m, v):
    rows, cols = w.shape
    tr = _row_tile(rows, cols)
    b1c = 1.0 - ADAM_B1 ** ADAM_STEP
    b2c = 1.0 - ADAM_B2 ** ADAM_STEP

    def body(w_ref, g_ref, m_ref, v_ref, go_ref, d_ref, mo_ref, vo_ref):
        gv = g_ref[...]
        mn = ADAM_B1 * m_ref[...] + (1.0 - ADAM_B1) * gv
        vn = ADAM_B2 * v_ref[...] + (1.0 - ADAM_B2) * (gv * gv)
        go_ref[...] = gv
        mo_ref[...] = mn
        vo_ref[...] = vn
        d_ref[...] = -ADAM_LR * ((mn / b1c) / (jnp.sqrt(vn / b2c) + ADAM_EPS) + ADAM_WD * w_ref[...])

    blk = pl.BlockSpec((tr, cols), lambda i: (i, 0))
    shp = _sds((rows, cols), F32)
    return pl.pallas_call(
        body, name="adamw", grid=(rows // tr,),
        in_specs=[blk, pl.BlockSpec((None, tr, cols), lambda i: (g_sel, i, 0)), blk, blk],
        out_specs=[blk] * 4, out_shape=[shp] * 4,
        compiler_params=_cp(1))(w, g_arr, m, v)


def _flip(v, bit):
    return 1 - v if bit else v


def _my_place():
    x, y, c = lax.axis_index("x"), lax.axis_index("y"), lax.axis_index("c")
    return x, y, c


class _Exchange:
    def __init__(self, operands, out_shape, aliases, sems, start, finish):
        self.operands, self.out_shape, self.aliases, self.sems = list(operands), list(out_shape), dict(aliases), list(sems)
        self.start, self.finish = start, finish


def _join(exchanges):
    exchanges = [e for e in exchanges if e is not None]
    if not exchanges:
        return None
    ops, outs, sems, aliases, spans = [], [], [], {}, []
    for e in exchanges:
        spans.append((len(ops), len(outs), len(sems), e))
        for i, j in e.aliases.items():
            aliases[len(ops) + i] = len(outs) + j
        ops += e.operands
        outs += e.out_shape
        sems += e.sems

    def run(which):
        def go(ins, res, sm):
            for io, oo, so, e in spans:
                getattr(e, which)(ins[io:io + len(e.operands)], res[oo:oo + len(e.out_shape)], sm[so:so + len(e.sems)])
        return go

    return _Exchange(ops, outs, aliases, sems, run("start"), run("finish"))


def _call(body, name, grid, in_specs, out_specs, out_shape, args, scratch=(), carry=None, io_alias=None):
    in_specs, out_specs, out_shape, scratch = list(in_specs), list(out_specs), list(out_shape), list(scratch)
    io_alias = dict(io_alias or {})
    if carry is None:
        return pl.pallas_call(body, name=name, grid=grid, in_specs=in_specs, out_specs=out_specs,
                              out_shape=out_shape, scratch_shapes=scratch, input_output_aliases=io_alias,
                              compiler_params=_cp(len(grid)))(*args)
    n_in, n_out, n_s = len(in_specs), len(out_specs), len(scratch)
    c_in, c_out = len(carry.operands), len(carry.out_shape)
    any_spec = pl.BlockSpec(memory_space=pl.ANY)

    def wrapped(*refs):
        ins, cins = refs[:n_in], refs[n_in:n_in + c_in]
        o0 = n_in + c_in
        outs, couts = refs[o0:o0 + n_out], refs[o0 + n_out:o0 + n_out + c_out]
        s0 = o0 + n_out + c_out
        scr, sems = refs[s0:s0 + n_s], refs[s0 + n_s:]
        first = pl.program_id(0) == 0
        last = pl.program_id(0) == grid[0] - 1
        for ax in range(1, len(grid)):
            first = jnp.logical_and(first, pl.program_id(ax) == 0)
            last = jnp.logical_and(last, pl.program_id(ax) == grid[ax] - 1)

        @pl.when(first)
        def _():
            carry.start(cins, couts, sems)

        body(*ins, *outs, *scr)

        @pl.when(last)
        def _():
            carry.finish(cins, couts, sems)

    return pl.pallas_call(
        wrapped, name=name, grid=grid, in_specs=in_specs + [any_spec] * c_in,
        out_specs=out_specs + [any_spec] * c_out, out_shape=out_shape + carry.out_shape,
        scratch_shapes=scratch + carry.sems,
        input_output_aliases={**io_alias, **{n_in + i: n_out + j for i, j in carry.aliases.items()}},
        compiler_params=_cp(len(grid)))(*args, *carry.operands)


def _whole_call(body, name, args, out_shape, scratch, carry=None):
    vm = pl.BlockSpec(memory_space=pltpu.VMEM)
    any_spec = pl.BlockSpec(memory_space=pl.ANY)
    out_shape, scratch = list(out_shape), list(scratch)
    n_in, n_out, n_s = len(args), len(out_shape), len(scratch)
    if carry is None:
        return pl.pallas_call(body, name=name, in_specs=[vm] * n_in, out_specs=[vm] * n_out, out_shape=out_shape,
                              scratch_shapes=scratch, compiler_params=_cp())(*args)
    c_in, c_out = len(carry.operands), len(carry.out_shape)

    def wrapped(*refs):
        ins, cins = refs[:n_in], refs[n_in:n_in + c_in]
        o0 = n_in + c_in
        outs, couts = refs[o0:o0 + n_out], refs[o0 + n_out:o0 + n_out + c_out]
        s0 = o0 + n_out + c_out
        scr, sems = refs[s0:s0 + n_s], refs[s0 + n_s:]
        carry.start(cins, couts, sems)
        body(*ins, *outs, *scr)
        carry.finish(cins, couts, sems)

    return pl.pallas_call(
        wrapped, name=name, in_specs=[vm] * n_in + [any_spec] * c_in, out_specs=[vm] * n_out + [any_spec] * c_out,
        out_shape=out_shape + carry.out_shape, scratch_shapes=scratch + carry.sems,
        input_output_aliases={n_in + i: n_out + j for i, j in carry.aliases.items()},
        compiler_params=_cp())(*args, *carry.operands)


def _alone(name, ex):
    any_spec = pl.BlockSpec(memory_space=pl.ANY)
    c_in, c_out = len(ex.operands), len(ex.out_shape)

    def body(*refs):
        ins, outs, sems = refs[:c_in], refs[c_in:c_in + c_out], refs[c_in + c_out:]
        ex.start(ins, outs, sems)
        ex.finish(ins, outs, sems)

    return pl.pallas_call(
        body, name=name, in_specs=[any_spec] * c_in, out_specs=[any_spec] * c_out, out_shape=ex.out_shape,
        scratch_shapes=ex.sems, input_output_aliases=ex.aliases, compiler_params=_cp())(*ex.operands)


def _ada_fwd(c_pad, w_ada, b_shard, carry=None):
    d = c_pad.shape[-1]
    cols = w_ada.shape[-1]
    chunk = 384

    def body(c_ref, w_ref, b_ref, call_ref, mod_ref, part, s1, r1, s2, r2):
        x, y, c = _my_place()
        dev = 4 * x + 2 * y + c
        chip = 2 * x + y
        call_ref[dev] = c_ref[...]

        def c_copy(k):
            px, py, pc = _flip(x, (k >> 2) & 1), _flip(y, (k >> 1) & 1), _flip(c, k & 1)
            return px, py, pc

        sends = []
        for k in range(1, N_DEV):
            px, py, pc = c_copy(k)
            cp = pltpu.make_async_remote_copy(src_ref=c_ref, dst_ref=call_ref.at[dev], send_sem=s1.at[k - 1],
                                              recv_sem=r1.at[k - 1], device_id=(px, py, pc), device_id_type=MESH)
            cp.start()
            sends.append(cp)
        for k in range(1, N_DEV):
            px, py, pc = c_copy(k)
            pltpu.make_async_remote_copy(src_ref=c_ref, dst_ref=call_ref.at[4 * px + 2 * py + pc],
                                         send_sem=s1.at[k - 1], recv_sem=r1.at[k - 1],
                                         device_id=(px, py, pc), device_id_type=MESH).wait_recv()
        for cp in sends:
            cp.wait_send()

        cs = call_ref[...].reshape(N_DEV * 8, d)
        sc = (cs * jax.nn.sigmoid(cs)).astype(BF)
        for n0 in range(0, cols, chunk):
            blk = _nn(sc, w_ref[:, n0:n0 + chunk].astype(BF)) + b_ref[:, n0:n0 + chunk]
            part[:, :, n0:n0 + chunk] = blk.reshape(N_DEV, 8, chunk)

        mod_ref[chip] = part[dev]
        sends = []
        for kk in range(1, N_CHIPS):
            px, py = _flip(x, (kk >> 1) & 1), _flip(y, kk & 1)
            cp = pltpu.make_async_remote_copy(src_ref=part.at[4 * px + 2 * py + c], dst_ref=mod_ref.at[chip],
                                              send_sem=s2.at[kk - 1], recv_sem=r2.at[kk - 1],
                                              device_id=(px, py, c), device_id_type=MESH)
            cp.start()
            sends.append(cp)
        for kk in range(1, N_CHIPS):
            px, py = _flip(x, (kk >> 1) & 1), _flip(y, kk & 1)
            pltpu.make_async_remote_copy(src_ref=part.at[dev], dst_ref=mod_ref.at[2 * px + py],
                                         send_sem=s2.at[kk - 1], recv_sem=r2.at[kk - 1],
                                         device_id=(px, py, c), device_id_type=MESH).wait_recv()
        for cp in sends:
            cp.wait_send()

    return _whole_call(
        body, "ada_fwd", (c_pad, w_ada, b_shard),
        [_sds((N_DEV, 8, d), F32), _sds((N_CHIPS, 8, cols), F32)],
        [pltpu.VMEM((N_DEV, 8, cols), F32),
         pltpu.SemaphoreType.DMA((N_DEV - 1,)), pltpu.SemaphoreType.DMA((N_DEV - 1,)),
         pltpu.SemaphoreType.DMA((N_CHIPS - 1,)), pltpu.SemaphoreType.DMA((N_CHIPS - 1,))], carry=carry)


def _ag_weights(bufs, kks=(1, 2, 3), relative=False):
    n, nk = len(bufs), len(kks)

    def half(b, which):
        hr = bufs[b].shape[2] // 2
        return pl.ds(pl.multiple_of(which * hr, 16), hr)

    def copies(outs, sems, b, i, kk):
        x, y, c = _my_place()
        chip = 2 * x + y
        px, py = _flip(x, (kk >> 1) & 1), _flip(y, kk & 1)
        mine, theirs = (0, kk) if relative else (chip, 2 * px + py)
        landing = kk if relative else chip
        k = nk * b + i
        send = pltpu.make_async_remote_copy(
            src_ref=outs[b].at[mine, :, half(b, c), :], dst_ref=outs[b].at[landing, :, half(b, c), :],
            send_sem=sems[0].at[k], recv_sem=sems[1].at[k], device_id=(px, py, c), device_id_type=MESH)
        got = outs[b].at[theirs, :, half(b, c), :]
        recv = pltpu.make_async_remote_copy(
            src_ref=got, dst_ref=got, send_sem=sems[0].at[k], recv_sem=sems[1].at[k],
            device_id=(px, py, c), device_id_type=MESH)
        fwd = pltpu.make_async_remote_copy(
            src_ref=got, dst_ref=got, send_sem=sems[2].at[k], recv_sem=sems[3].at[k],
            device_id=(x, y, 1 - c), device_id_type=MESH)
        other = outs[b].at[theirs, :, half(b, 1 - c), :]
        back = pltpu.make_async_remote_copy(
            src_ref=other, dst_ref=other, send_sem=sems[2].at[k], recv_sem=sems[3].at[k],
            device_id=(x, y, 1 - c), device_id_type=MESH)
        return send, recv, fwd, back

    def each(outs, sems):
        for b in range(n):
            for i, kk in enumerate(kks):
                yield copies(outs, sems, b, i, kk)

    def start(ins, outs, sems):
        for send, _, _, _ in each(outs, sems):
            send.start()

    def finish(ins, outs, sems):
        for _, recv, fwd, _ in each(outs, sems):
            recv.wait_recv()
            fwd.start()
        for send, _, fwd, back in each(outs, sems):
            back.wait_recv()
            send.wait_send()
            fwd.wait_send()

    return _Exchange(bufs, [_sds(s.shape, s.dtype) for s in bufs], {i: i for i in range(n)},
                     [pltpu.SemaphoreType.DMA((nk * n,))] * 4, start, finish)


def _rs_d2d(grads):
    n = len(grads)

    def copy(ins, outs, sems, b):
        x, y, c = _my_place()
        hr = grads[b].shape[2] // 2
        theirs = pl.ds(pl.multiple_of((1 - c) * hr, 8), hr)
        return pltpu.make_async_remote_copy(
            src_ref=ins[b].at[:, :, theirs, :], dst_ref=outs[b], send_sem=sems[0].at[b], recv_sem=sems[1].at[b],
            device_id=(x, y, 1 - c), device_id_type=MESH)

    def start(ins, outs, sems):
        for b in range(n):
            copy(ins, outs, sems, b).start()

    def finish(ins, outs, sems):
        for b in range(n):
            copy(ins, outs, sems, b).wait()

    return _Exchange(grads, [_sds(g.shape[:2] + (g.shape[2] // 2, g.shape[3]), F32) for g in grads], {},
                     [pltpu.SemaphoreType.DMA((n,))] * 2, start, finish)


def _add_halves(core, g, land):
    nchip, ng, rows, cols = g.shape
    hr = rows // 2
    tr = _row_tile(hr, cols)
    steps = hr // tr

    def body(core_ref, g_ref, l_ref, o_ref):
        del core_ref
        o_ref[...] = (g_ref[...] + l_ref[...]).astype(BF)

    return pl.pallas_call(
        body, name="add_halves",
        grid_spec=pltpu.PrefetchScalarGridSpec(
            num_scalar_prefetch=1, grid=(nchip, ng, steps),
            in_specs=[pl.BlockSpec((None, None, tr, cols), lambda j, a, i, cr: (j, a, cr[0] * steps + i, 0)),
                      pl.BlockSpec((None, None, tr, cols), lambda j, a, i, cr: (j, a, i, 0))],
            out_specs=pl.BlockSpec((None, None, tr, cols), lambda j, a, i, cr: (j, a, i, 0))),
        out_shape=_sds((nchip, ng, hr, cols), BF),
        compiler_params=_cp(3))(core, g, land)


def _rs_ici(parts, relative=False):
    n = len(parts)

    def copies(ins, outs, sems):
        x, y, c = _my_place()
        chip = 2 * x + y
        for b in range(n):
            for kk in range(1, N_CHIPS):
                px, py = _flip(x, (kk >> 1) & 1), _flip(y, kk & 1)
                k = 3 * b + kk - 1
                theirs, landing = (kk, kk) if relative else (2 * px + py, chip)
                send = pltpu.make_async_remote_copy(
                    src_ref=ins[b].at[theirs], dst_ref=outs[b].at[landing],
                    send_sem=sems[0].at[k], recv_sem=sems[1].at[k], device_id=(px, py, c), device_id_type=MESH)
                slot = outs[b].at[theirs]
                recv = pltpu.make_async_remote_copy(
                    src_ref=slot, dst_ref=slot, send_sem=sems[0].at[k], recv_sem=sems[1].at[k],
                    device_id=(px, py, c), device_id_type=MESH)
                yield send, recv

    def start(ins, outs, sems):
        for send, _ in copies(ins, outs, sems):
            send.start()

    def finish(ins, outs, sems):
        for send, recv in copies(ins, outs, sems):
            recv.wait_recv()
            send.wait_send()

    return _Exchange(parts, [_sds(p.shape, p.dtype) for p in parts], {},
                     [pltpu.SemaphoreType.DMA((3 * n,))] * 2, start, finish)


def _sum_chips(place, part, land, relative=False):
    nchip, ng, hr, cols = land.shape
    tr = _row_tile(hr, cols)
    steps = hr // tr

    def body(place_ref, p_ref, l1, l2, l3, o_ref):
        del place_ref
        o_ref[...] = ((p_ref[...].astype(F32) + l1[...].astype(F32)) + l2[...].astype(F32)) + l3[...].astype(F32)

    def slot(k):
        if relative:
            return pl.BlockSpec((None, None, tr, cols), lambda a, i, pr: (k, a, i, 0))
        return pl.BlockSpec((None, None, tr, cols), lambda a, i, pr: (jnp.bitwise_xor(pr[1], k), a, i, 0))

    return pl.pallas_call(
        body, name="sum_chips",
        grid_spec=pltpu.PrefetchScalarGridSpec(
            num_scalar_prefetch=1, grid=(ng, steps),
            in_specs=[slot(0), slot(1), slot(2), slot(3)],
            out_specs=pl.BlockSpec((None, tr, cols), lambda a, i, pr: (a, pr[0] * steps + i, 0))),
        out_shape=_sds((ng, 2 * hr, cols), F32),
        compiler_params=_cp(2))(place, part, land, land, land)


def _rs_final(bufs):
    n = len(bufs)

    def copy(outs, sems, b, which):
        x, y, c = _my_place()
        hr = bufs[b].shape[1] // 2
        rows = outs[b].at[:, pl.ds(pl.multiple_of((c if which == 0 else 1 - c) * hr, 8), hr), :]
        return pltpu.make_async_remote_copy(
            src_ref=rows, dst_ref=rows, send_sem=sems[0].at[b], recv_sem=sems[1].at[b],
            device_id=(x, y, 1 - c), device_id_type=MESH)

    def start(ins, outs, sems):
        for b in range(n):
            copy(outs, sems, b, 0).start()

    def finish(ins, outs, sems):
        for b in range(n):
            copy(outs, sems, b, 0).wait_send()
            copy(outs, sems, b, 1).wait_recv()

    return _Exchange(bufs, [_sds(h.shape, F32) for h in bufs], {i: i for i in range(n)},
                     [pltpu.SemaphoreType.DMA((n,))] * 2, start, finish)


def _small_sync(smalls, dmod_blk, c_all, carry=None):
    d = c_all.shape[-1]
    cols = dmod_blk.shape[-1]
    chunk = 384

    def body(sm_ref, dm_ref, c_ref, sum_ref, gw_ref, sm_all, dm_all, ssem, rsem):
        x, y, c = _my_place()
        dev = 4 * x + 2 * y + c
        chip = 2 * x + y
        sm_all[dev] = sm_ref[...]
        dm_all[dev] = dm_ref[chip]
        sends = []
        for k in range(1, N_DEV):
            px, py, pc = _flip(x, (k >> 2) & 1), _flip(y, (k >> 1) & 1), _flip(c, k & 1)
            a = pltpu.make_async_remote_copy(src_ref=sm_ref, dst_ref=sm_all.at[dev], send_sem=ssem.at[2 * (k - 1)],
                                             recv_sem=rsem.at[2 * (k - 1)], device_id=(px, py, pc),
                                             device_id_type=MESH)
            b = pltpu.make_async_remote_copy(src_ref=dm_ref.at[2 * px + py], dst_ref=dm_all.at[dev],
                                             send_sem=ssem.at[2 * (k - 1) + 1], recv_sem=rsem.at[2 * (k - 1) + 1],
                                             device_id=(px, py, pc), device_id_type=MESH)
            a.start()
            b.start()
            sends += [a, b]
        for k in range(1, N_DEV):
            px, py, pc = _flip(x, (k >> 2) & 1), _flip(y, (k >> 1) & 1), _flip(c, k & 1)
            pdev = 4 * px + 2 * py + pc
            pltpu.make_async_remote_copy(src_ref=sm_ref, dst_ref=sm_all.at[pdev], send_sem=ssem.at[2 * (k - 1)],
                                         recv_sem=rsem.at[2 * (k - 1)], device_id=(px, py, pc),
                                         device_id_type=MESH).wait_recv()
            pltpu.make_async_remote_copy(src_ref=dm_ref.at[chip], dst_ref=dm_all.at[pdev],
                                         send_sem=ssem.at[2 * (k - 1) + 1], recv_sem=rsem.at[2 * (k - 1) + 1],
                                         device_id=(px, py, pc), device_id_type=MESH).wait_recv()
        for cp in sends:
            cp.wait_send()

        tot = sm_all[0]
        for q in range(1, N_DEV):
            tot = tot + sm_all[q]
        sum_ref[...] = tot

        cs = c_ref[...].reshape(N_DEV * 8, d)
        sc = (cs * jax.nn.sigmoid(cs)).astype(BF)
        for n0 in range(0, cols, chunk):
            dmv = dm_all[:, :, n0:n0 + chunk].reshape(N_DEV * 8, chunk).astype(BF)
            gw_ref[:, n0:n0 + chunk] = _tn(sc, dmv)

    return _whole_call(
        body, "small_sync", (smalls, dmod_blk, c_all),
        [_sds(smalls.shape, F32), _sds((d, cols), F32)],
        [pltpu.VMEM((N_DEV,) + smalls.shape, F32), pltpu.VMEM((N_DEV, 8, cols), F32),
         pltpu.SemaphoreType.DMA((2 * (N_DEV - 1),)), pltpu.SemaphoreType.DMA((2 * (N_DEV - 1),))], carry=carry)


def _bucket_onehot():
    maps = np.stack([_bucket_map(dil).reshape(-1) for _, dil in DIL_CONFIGS])
    return (jnp.asarray(maps)[:, None, :] == jnp.arange(N_BUCKETS, dtype=jnp.int32)[None, :, None]).astype(BF)


def _dil_bias(rel_t, onehot):
    def body(r_ref, oh_ref, o_ref):
        rv = r_ref[...]
        hi = rv.astype(BF)
        lo = (rv - hi.astype(F32)).astype(BF)
        for c in range(len(DIL_CONFIGS)):
            o_ref[c] = _nn(hi, oh_ref[c]) + _nn(lo, oh_ref[c])

    return pl.pallas_call(body, name="dil_bias",
                          out_shape=_sds((len(DIL_CONFIGS), N_HEADS, BLOCK * 2 * BLOCK), F32),
                          compiler_params=_cp())(rel_t, onehot)


def _rowsum8(a):
    def body(a_ref, o_ref):
        o_ref[...] = jnp.sum(a_ref[...], axis=0, keepdims=True)

    return pl.pallas_call(body, name="rowsum8", out_shape=_sds((1, a.shape[1]), F32), compiler_params=_cp())(a)


def _local_step(x, mod, target, w, gains, rel_bias, place=None):
    nb, seq, d = x.shape
    t = nb * seq
    dist = place is not None
    core = place[0:1] if dist else None
    x0 = x.reshape(t, d)
    tgt = target.reshape(t, d)
    md = [mod[:, i:i + 1, :] for i in range(N_MOD)]
    sh1, sc1, gt1, sh2, sc2, gt2, sh3, sc3, gt3 = md
    g1, g2, g3 = gains["g_ffn1"], gains["g_mix"], gains["g_ffn2"]
    ones_g = _group_ones()

    def partial_sums(grads, lands):
        return [_add_halves(core, g, l) for g, l in zip(grads, lands)]

    def chip_sums(parts, lands):
        return [_sum_chips(place, p, l, relative=True) for p, l in zip(parts, lands)]

    gu1 = w["gu1"]
    res = _ffn_up(x0, g1, sc1, sh1, gu1, seq,
                  carry=_join([_ag_weights([w["d1"]], relative=True), _ag_weights([w["win"]])]) if dist else None)
    h1, a1, u1, s1 = res[:4]
    wd1, w_in = res[4:] if dist else (w["d1"], w["win"])
    f1, x1 = _ffn_down(s1, wd1, x0, gt1, seq, 0.5)

    res = _qkv_proj(x1, g2, sc2, sh2, w_in, seq,
                    carry=_join([_ag_weights([w["d2"]], relative=True), _ag_weights([w["wout"]])]) if dist else None)
    h2, qkv6, qkv_r4, qkv_r16 = res[:4]
    wd2, w_out = res[4:] if dist else (w["d2"], w["wout"])
    w_out2 = w_out.reshape(2 * D_GRP, d)
    qkv6b = qkv6.reshape(6, nb, seq, D_GRP)
    res = _sb_fwd(qkv6b, gains["g_sb_out"], nb, seq, carry=_ag_weights([w["gu2"]], relative=True) if dist else None)
    o_sb, on_sb = res[:2]
    wgu2 = res[2] if dist else w["gu2"]
    onehot = _bucket_onehot()
    bias = _dil_bias(rel_bias.T, onehot).reshape(len(DIL_CONFIGS), N_HEADS * BLOCK, 2 * BLOCK)
    o_cs, l_cs = [], []
    qkv_rs = [(qkv6b, 3), (qkv_r4, 0), (qkv_r16, 0)]
    for ci, (_, dil) in enumerate(DIL_CONFIGS):
        sub = seq // dil
        arr, base = qkv_rs[ci]
        arr = arr.reshape(base + 3, nb, sub, dil * D_GRP)
        qkv_rs[ci] = (arr, base)
        o_c, l_c = _dil_fwd(arr, base, bias[ci], nb, sub, dil)
        o_cs.append(o_c.reshape(t // dil, dil * D_GRP))
        l_cs.append(l_c.reshape(t // dil, dil * D_GRP))
    o_dil, on_dil = _dil_comb(o_cs, l_cs, gains["g_dil_out"])
    tmix, x2 = _mix_out(on_sb.reshape(t, D_GRP), on_dil, w_out2, x1, gt2, seq)

    h3, a3, u3, s3 = _ffn_up(x2, g3, sc3, sh3, wgu2, seq)
    f3, dx3, dg_final, loss = _ffn_down_loss(s3, wd2, x2, gt3, seq, 0.5, gains["g_final"], tgt)

    da3, du3, df3, dgt3, dx2, dsh3, dsc3, dg3 = _ffn_bwd_x(dx3, gt3, f3, wd2, a3, u3, wgu2, x2, g3, sc3, seq, 0.5)
    grads2 = [_ffn_bwd_w(h3, da3, du3, s3, df3)]

    res = _mix_bwd_out(
        dx2, gt2, tmix, w_out2, o_sb.reshape(t, D_GRP), o_dil, on_sb.reshape(t, D_GRP), on_dil,
        gains["g_sb_out"], gains["g_dil_out"], ones_g, seq, carry=_rs_d2d(grads2) if dist else None)
    do_sb, do_dil, dgt2, dg_sb, dg_dil, dw_out = res[:6]
    parts2 = partial_sums(grads2, res[6:]) if dist else None
    dw_out = dw_out.reshape(N_CHIPS, 1, 2 * D_GRP // N_CHIPS, d)
    res = _sb_bwd(qkv6b, do_sb.reshape(nb, seq, D_GRP), nb, seq,
                  carry=_rs_ici(parts2, relative=True) if dist else None)
    dqkv6 = res[0]
    halves2 = chip_sums(parts2, res[1:]) if dist else None
    dcs = _dil_comb_bwd(do_dil, o_cs, l_cs)
    dsum, a_tiles = [], []
    for ci, (_, dil) in enumerate(DIL_CONFIGS):
        sub = seq // dil
        do_c = dcs[ci].reshape(nb, sub, dil * D_GRP)
        dd_c = dcs[3 + ci].reshape(nb, sub, dil * D_GRP)
        res = _dil_bwd(qkv_rs[ci][0], qkv_rs[ci][1], bias[ci], do_c, dd_c, nb, sub, dil)
        dsum.append(res[0].reshape(3, t // dil, dil * D_GRP))
        a_tiles.append(res[1].reshape(N_HEADS, BLOCK * 2 * BLOCK))
    dqkv6 = _dqkv_dil_sum(dsum, dqkv6.reshape(6, t, D_GRP))
    drel = _relbias_grad(jnp.stack(a_tiles), onehot)
    dx1, dsh2, dsc2, dg2 = _mix_bwd_dh(dqkv6, w_in, x1, g2, sc2, dx2, seq)

    da1, du1, df1, dgt1 = _ffn_bwd_ds(dx1, gt1, f1, wd1, a1, u1, seq, 0.5)
    grads1 = [_ffn_bwd_w(h1, da1, du1, s1, df1)]
    res = _dw_in(h2, dqkv6, carry=_join([_rs_d2d(grads1), _rs_final(halves2)]) if dist else None)
    grads_m = [res[0], dw_out]
    parts1 = partial_sums(grads1, res[1:2]) if dist else None
    if dist:
        grads2 = res[2:3]
    res = _ffn_bwd_dh(da1, du1, gu1, x0, g1, sc1, dx1, seq,
                      carry=_join([_rs_ici(parts1, relative=True), _rs_d2d(grads_m)]) if dist else None)
    dx0, dsh1, dsc1, dg1 = res[:4]
    pending = None
    if dist:
        pending = (chip_sums(parts1, res[4:5]), partial_sums(grads_m, res[5:7]))

    dmod = jnp.concatenate([dsh1, dsc1, dgt1, dsh2, dsc2, dgt2, dsh3, dsc3, dgt3], axis=1)
    return dict(grad_x=dx0.reshape(nb, seq, d), loss=loss[0, 0], dmod=dmod.reshape(nb, N_MOD * d),
                dffn1=grads1[0], dffn2=grads2[0], dwin=grads_m[0], dwout=grads_m[1], pending=pending,
                dg_ffn1=dg1, dg_mix=dg2, dg_ffn2=dg3, dg_final=dg_final, dg_sb=dg_sb, dg_dil=dg_dil,
                drel=drel.T)


_SMALL_ORDER = (("b_ada", N_MOD * 1024), ("g_ffn1", 1024), ("g_mix", 1024), ("g_ffn2", 1024), ("g_final", 1024),
                ("g_sb_out", D_GRP), ("g_dil_out", D_GRP), ("rel_bias", N_BUCKETS * N_HEADS))


def _pack_small(parts, extra=None):
    flat = [parts[name].reshape(-1).astype(F32) for name, _ in _SMALL_ORDER]
    used = sum(sz for _, sz in _SMALL_ORDER)
    pad = SMALL_ROWS * 128 - used
    tail = jnp.zeros((pad,), F32)
    if extra is not None:
        tail = tail.at[0].set(extra)
    return jnp.concatenate(flat + [tail]).reshape(SMALL_ROWS, 128)


def _unpack_small(packed, shapes):
    flat = packed.reshape(-1)
    out, off = {}, 0
    for name, sz in _SMALL_ORDER:
        out[name] = flat[off:off + sz].reshape(shapes[name])
        off += sz
    return out, flat[off]


def kernel(x, c, w_ada, b_ada, g_ffn1, w1_gate, w1_up, w1_down, g_mix, w_in, g_sb_out, g_dil_out, w_out, rel_bias, g_ffn2, w2_gate, w2_up, w2_down, g_final, loss_target, m_w_ada, m_b_ada, m_g_ffn1, m_w1_gate, m_w1_up, m_w1_down, m_g_mix, m_w_in, m_g_sb_out, m_g_dil_out, m_w_out, m_rel_bias, m_g_ffn2, m_w2_gate, m_w2_up, m_w2_down, m_g_final, v_w_ada, v_b_ada, v_g_ffn1, v_w1_gate, v_w1_up, v_w1_down, v_g_mix, v_w_in, v_g_sb_out, v_g_dil_out, v_w_out, v_rel_bias, v_g_ffn2, v_w2_gate, v_w2_up, v_w2_down, v_g_final):
    nb, seq, d = x.shape
    xi, yi, ci = lax.axis_index("x"), lax.axis_index("y"), lax.axis_index("c")
    chip = 2 * xi + yi
    ada_cols = w_ada.shape[-1]

    c_pad = jnp.zeros((8, d), F32).at[:nb].set(c)
    b_shard = lax.dynamic_slice(b_ada, (0, chip * ada_cols), (1, ada_cols))
    shards = dict(gu1=jnp.stack([w1_gate[0], w1_up[0]]), d1=w1_down, win=w_in, wout=w_out,
                  gu2=jnp.stack([w2_gate[0], w2_up[0]]), d2=w2_down)
    bufs = {k: lax.dynamic_update_slice(lax.empty((N_CHIPS,) + s.shape, BF), s.astype(BF)[None],
                                        (chip if k in ("win", "wout") else 0, 0, 0, 0))
            for k, s in shards.items()}
    c_all, mod_blk, bufs["gu1"] = _ada_fwd(c_pad, w_ada[0], b_shard,
                                           carry=_ag_weights([bufs["gu1"]], relative=True))
    mod = jnp.transpose(mod_blk[:, :nb, :], (1, 0, 2)).reshape(nb, N_MOD, d)

    gains = dict(g_ffn1=g_ffn1, g_mix=g_mix, g_ffn2=g_ffn2, g_final=g_final.reshape(1, d),
                 g_sb_out=g_sb_out.reshape(1, D_GRP), g_dil_out=g_dil_out.reshape(1, D_GRP))
    place = jnp.stack([ci, chip]).astype(jnp.int32)
    r = _local_step(x, mod, loss_target, bufs, gains, rel_bias, place)

    dmod = r["dmod"]
    dmod_pad = jnp.zeros((8, N_MOD * d), F32).at[:nb].set(dmod)
    dmod_blk = jnp.transpose(dmod_pad.reshape(8, N_CHIPS, ada_cols), (1, 0, 2))
    small_parts = dict(b_ada=_rowsum8(dmod_pad), g_ffn1=r["dg_ffn1"], g_mix=r["dg_mix"], g_ffn2=r["dg_ffn2"],
                       g_final=r["dg_final"], g_sb_out=r["dg_sb"], g_dil_out=r["dg_dil"], rel_bias=r["drel"])
    halves1, parts_m = r["pending"]
    res = _small_sync(_pack_small(small_parts, r["loss"]), dmod_blk, c_all,
                      carry=_join([_rs_final(halves1), _rs_ici(parts_m)]))
    small_sum, g_wada, gffn1 = res[:3]
    halves_m = [_sum_chips(place, p, l) for p, l in zip(parts_m, res[3:5])]
    gwin, gwout = _alone("rs_last", _rs_final(halves_m))
    gffn2 = r["dffn2"]

    small_w = dict(b_ada=b_ada, g_ffn1=g_ffn1, g_mix=g_mix, g_ffn2=g_ffn2, g_final=g_final,
                   g_sb_out=g_sb_out, g_dil_out=g_dil_out, rel_bias=rel_bias)
    small_m = dict(b_ada=m_b_ada, g_ffn1=m_g_ffn1, g_mix=m_g_mix, g_ffn2=m_g_ffn2, g_final=m_g_final,
                   g_sb_out=m_g_sb_out, g_dil_out=m_g_dil_out, rel_bias=m_rel_bias)
    small_v = dict(b_ada=v_b_ada, g_ffn1=v_g_ffn1, g_mix=v_g_mix, g_ffn2=v_g_ffn2, g_final=v_g_final,
                   g_sb_out=v_g_sb_out, g_dil_out=v_g_dil_out, rel_bias=v_rel_bias)
    shapes = {k: v.shape for k, v in small_w.items()}
    sg, sd, sm, sv = _adamw(_pack_small(small_w), small_sum.reshape(1, SMALL_ROWS, 128), 0,
                            _pack_small(small_m), _pack_small(small_v))
    sg, loss = _unpack_small(sg, shapes)
    sd, _ = _unpack_small(sd, shapes)
    sm, _ = _unpack_small(sm, shapes)
    sv, _ = _unpack_small(sv, shapes)

    big = {}

    def upd(name, w, g_arr, sel, m, v, transposed=False):
        swap = (lambda a: jnp.swapaxes(a, -1, -2)) if transposed else (lambda a: a)
        w2, m2, v2 = [swap(a)[0] for a in (w, m, v)]
        big[name] = [swap(a[None]) for a in _adamw(w2, g_arr, sel, m2, v2)]

    upd("w_ada", w_ada, g_wada.reshape(1, d, ada_cols), 0, m_w_ada, v_w_ada)
    upd("w1_gate", w1_gate, gffn1, 0, m_w1_gate, v_w1_gate, transposed=True)
    upd("w1_up", w1_up, gffn1, 1, m_w1_up, v_w1_up, transposed=True)
    upd("w1_down", w1_down, gffn1, 2, m_w1_down, v_w1_down)
    upd("w_in", w_in, gwin, 0, m_w_in, v_w_in)
    upd("w_out", w_out, gwout, 0, m_w_out, v_w_out)
    upd("w2_gate", w2_gate, gffn2, 0, m_w2_gate, v_w2_gate, transposed=True)
    upd("w2_up", w2_up, gffn2, 1, m_w2_up, v_w2_up, transposed=True)
    upd("w2_down", w2_down, gffn2, 2, m_w2_down, v_w2_down)

    names = ["w_ada", "b_ada", "g_ffn1", "w1_gate", "w1_up", "w1_down", "g_mix", "w_in", "g_sb_out", "g_dil_out",
             "w_out", "rel_bias", "g_ffn2", "w2_gate", "w2_up", "w2_down", "g_final"]
    outs = [loss, r["grad_x"]]
    for k, small in enumerate((sg, sd, sm, sv)):
        for name in names:
            outs.append(big[name][k] if name in big else small[name])
    return tuple(outs)
```

```python
import math

import numpy as np
import jax
import jax.numpy as jnp
from jax import lax
from jax.experimental import pallas as pl
from jax.experimental.pallas import tpu as pltpu

F32 = jnp.float32
BF = jnp.bfloat16
MESH = pl.DeviceIdType.MESH

HEAD_DIM = 64
N_HEADS = 8
D_GRP = N_HEADS * HEAD_DIM
DIL_CONFIGS = ((128, 1), (512, 4), (2048, 16))
N_STEPS = 128
BLOCK = 128
N_BUCKETS = 32
MAX_DISTANCE = 2048
N_MOD = 9
EPS = 1e-6
NEG_INF = -1e30
SCALE = HEAD_DIM ** -0.5

ADAM_LR = 0.001
ADAM_B1 = 0.9
ADAM_B2 = 0.999
ADAM_EPS = 1e-08
ADAM_WD = 0.01
ADAM_STEP = 10

N_CHIPS = 4
N_DEV = 8
VMEM_LIMIT = 56 * 1024 * 1024
TM = 512
TQ = 256
KB = 256
SMALL_ROWS = 120


def _cp(n_axes=0, **kw):
    sem = ("arbitrary",) * n_axes if n_axes else None
    return pltpu.CompilerParams(dimension_semantics=sem, vmem_limit_bytes=VMEM_LIMIT, **kw)


def _nn(a, b):
    return jnp.dot(a, b, preferred_element_type=F32)


def _nt(a, b):
    return lax.dot_general(a, b, (((1,), (1,)), ((), ())), preferred_element_type=F32)


def _tn(a, b):
    return lax.dot_general(a, b, (((0,), (0,)), ((), ())), preferred_element_type=F32)


def _twice(m):
    return jnp.concatenate([m, m], axis=0)


def _nn2(x, m2):
    hi = x.astype(BF)
    lo = (x - hi.astype(F32)).astype(BF)
    return _nn(jnp.concatenate([hi, lo], axis=1), m2)


def _softplus(z):
    return jnp.maximum(z, 0.0) + jnp.log(1.0 + jnp.exp(-jnp.abs(z)))


def _sds(shape, dtype):
    return jax.ShapeDtypeStruct(shape, dtype)


def _whole(a):
    nd = a.ndim
    return pl.BlockSpec(a.shape, lambda *_: (0,) * nd, pipeline_mode=pl.Buffered(1))


def _modnorm_bwd_tile(dh, xv, gv, scv, dxo):
    r = lax.rsqrt(jnp.mean(xv * xv, axis=-1, keepdims=True) + EPS)
    n = xv * r
    ng = n * gv
    dsh = jnp.sum(dh, axis=0, keepdims=True)
    dsc = jnp.sum(dh * ng, axis=0, keepdims=True)
    dy = dh * (1.0 + scv)
    dg = jnp.sum(dy * n, axis=0, keepdims=True)
    dn = dy * gv
    dx = dxo + r * (dn - n * jnp.mean(dn * n, axis=-1, keepdims=True))
    return dx, dsh, dsc, dg


def _acc_rows(ref, val, first):
    @pl.when(first)
    def _():
        ref[...] = val

    @pl.when(jnp.logical_not(first))
    def _():
        ref[...] += val


def _modnorm_tile(x_ref, g_ref, sc_ref, sh_ref):
    xv = x_ref[...]
    r = lax.rsqrt(jnp.mean(xv * xv, axis=-1, keepdims=True) + EPS)
    return (((xv * r) * g_ref[...]) * (1.0 + sc_ref[...]) + sh_ref[...]).astype(BF)


def _ffn_up(x, g, sc, sh, wgu, seq, carry=None):
    t, d = x.shape
    fs = wgu.shape[-1]
    per = seq // TM

    def body(x_ref, g_ref, sc_ref, sh_ref, w_ref, h_ref, p_ref, q_ref, s_ref):
        hv = _modnorm_tile(x_ref, g_ref, sc_ref, sh_ref)
        h_ref[...] = hv
        for j in range(N_CHIPS):
            a = _nn(hv, w_ref[j, 0])
            u = _nn(hv, w_ref[j, 1])
            sig = jax.nn.sigmoid(a)
            q = a * sig
            p_ref[j] = (u * (sig * (1.0 + a * (1.0 - sig)))).astype(BF)
            q_ref[j] = q.astype(BF)
            s_ref[j] = (q * u).astype(BF)

    row = pl.BlockSpec((TM, d), lambda m: (m, 0))
    ex = pl.BlockSpec((None, 1, d), lambda m: (m // per, 0, 0))
    blk = pl.BlockSpec((N_CHIPS, TM, fs), lambda m: (0, m, 0))
    return _call(
        body, "ffn_up", (t // TM,),
        [row, pl.BlockSpec((1, d), lambda m: (0, 0)), ex, ex, _whole(wgu)],
        [row, blk, blk, blk],
        [_sds((t, d), BF)] + [_sds((N_CHIPS, t, fs), BF)] * 3,
        (x, g, sc, sh, wgu), carry=carry)


def _ffn_down(s, wd, x, gt, seq, coef, carry=None):
    _, t, fs = s.shape
    d = x.shape[-1]
    per = seq // TM

    def body(s_ref, w_ref, x_ref, gt_ref, f_ref, xo_ref):
        f = _nn(s_ref[0], w_ref[0, 0])
        for j in range(1, N_CHIPS):
            f = f + _nn(s_ref[j], w_ref[j, 0])
        f_ref[...] = f.astype(BF)
        xo_ref[...] = x_ref[...] + (coef * gt_ref[...]) * f

    row = pl.BlockSpec((TM, d), lambda m: (m, 0))
    return _call(
        body, "ffn_down", (t // TM,),
        [pl.BlockSpec((N_CHIPS, TM, fs), lambda m: (0, m, 0)), _whole(wd), row,
         pl.BlockSpec((None, 1, d), lambda m: (m // per, 0, 0))],
        [row, row], [_sds((t, d), BF), _sds((t, d), F32)], (s, wd, x, gt), carry=carry)


def _ffn_bwd_ds(dxo, gt, f, wd, p, q, seq, coef, carry=None):
    t, d = dxo.shape
    fs = p.shape[-1]
    per = seq // TM
    nb = t // seq

    def body(dxo_ref, gt_ref, f_ref, w_ref, p_ref, q_ref, da_ref, du_ref, df_ref, dgt_ref):
        m = pl.program_id(0)
        dxv = dxo_ref[...]
        df = ((coef * gt_ref[...]) * dxv).astype(BF)
        df_ref[...] = df
        _acc_rows(dgt_ref, coef * jnp.sum(dxv * f_ref[...].astype(F32), axis=0, keepdims=True), m % per == 0)
        for j in range(N_CHIPS):
            ds = _nt(df, w_ref[j, 0])
            da_ref[j] = (ds * p_ref[j].astype(F32)).astype(BF)
            du_ref[j] = (ds * q_ref[j].astype(F32)).astype(BF)

    row = pl.BlockSpec((TM, d), lambda m: (m, 0))
    blk = pl.BlockSpec((N_CHIPS, TM, fs), lambda m: (0, m, 0))
    ex = pl.BlockSpec((None, 1, d), lambda m: (m // per, 0, 0))
    return _call(
        body, "ffn_bwd_ds", (t // TM,),
        [row, ex, row, _whole(wd), blk, blk],
        [blk, blk, row, ex],
        [_sds((N_CHIPS, t, fs), BF), _sds((N_CHIPS, t, fs), BF), _sds((t, d), BF), _sds((nb, 1, d), F32)],
        (dxo, gt, f, wd, p, q), carry=carry)


TM_X = 256


def _ffn_bwd_x(dxo, gt, f, wd, p, q, wgu, x, g, sc, seq, coef):
    t, d = dxo.shape
    fs = p.shape[-1]
    per = seq // TM_X
    nb = t // seq

    def body(dxo_ref, gt_ref, f_ref, wd_ref, p_ref, q_ref, w_ref, x_ref, g_ref, sc_ref,
             da_ref, du_ref, df_ref, dgt_ref, dx_ref, dsh_ref, dsc_ref, dg_ref):
        m = pl.program_id(0)
        dxv = dxo_ref[...]
        df = ((coef * gt_ref[...]) * dxv).astype(BF)
        df_ref[...] = df
        _acc_rows(dgt_ref, coef * jnp.sum(dxv * f_ref[...].astype(F32), axis=0, keepdims=True), m % per == 0)
        dh = None
        for j in range(N_CHIPS):
            ds = _nt(df, wd_ref[j, 0])
            da = (ds * p_ref[j].astype(F32)).astype(BF)
            du = (ds * q_ref[j].astype(F32)).astype(BF)
            da_ref[j] = da
            du_ref[j] = du
            part = _nt(da, w_ref[j, 0]) + _nt(du, w_ref[j, 1])
            dh = part if dh is None else dh + part
        dx, dsh, dsc, dg = _modnorm_bwd_tile(dh, x_ref[...], g_ref[...], sc_ref[...], dxv)
        dx_ref[...] = dx
        _acc_rows(dsh_ref, dsh, m % per == 0)
        _acc_rows(dsc_ref, dsc, m % per == 0)
        _acc_rows(dg_ref, dg, m == 0)

    row = pl.BlockSpec((TM_X, d), lambda m: (m, 0))
    blk = pl.BlockSpec((N_CHIPS, TM_X, fs), lambda m: (0, m, 0))
    ex = pl.BlockSpec((None, 1, d), lambda m: (m // per, 0, 0))
    vec = pl.BlockSpec((1, d), lambda m: (0, 0))
    exs = _sds((nb, 1, d), F32)
    return pl.pallas_call(
        body, name="ffn_bwd_x", grid=(t // TM_X,),
        in_specs=[row, ex, row, _whole(wd), blk, blk, _whole(wgu), row, vec, ex],
        out_specs=[blk, blk, row, ex, row, ex, ex, vec],
        out_shape=[_sds((N_CHIPS, t, fs), BF), _sds((N_CHIPS, t, fs), BF), _sds((t, d), BF), exs,
                   _sds((t, d), F32), exs, exs, _sds((1, d), F32)],
        compiler_params=_cp(1))(dxo, gt, f, wd, p, q, wgu, x, g, sc)


TK_W = 1024


def _ffn_bwd_w(h, da, du, s, df, terms=(0, 1, 2), carry=None):
    t, d = h.shape
    fs = da.shape[-1]
    row = pl.BlockSpec((TK_W, d), lambda j, kt: (kt, 0))
    blk = pl.BlockSpec((None, TK_W, fs), lambda j, kt: (j, kt, 0))
    args, specs, idx = [], [], []

    def operand(a, spec):
        for i, o in enumerate(args):
            if o is a:
                return i
        args.append(a)
        specs.append(spec)
        return len(args) - 1

    for term in terms:
        lhs, rhs = ((da, h), (du, h), (s, df))[term]
        idx.append((operand(lhs, blk), operand(rhs, row)))

    def body(*refs):
        o_ref = refs[-1]
        kt = pl.program_id(1)
        parts = [_tn(refs[a][...], refs[b][...]) for a, b in idx]

        @pl.when(kt == 0)
        def _():
            for i, p in enumerate(parts):
                o_ref[i] = p

        @pl.when(kt != 0)
        def _():
            for i, p in enumerate(parts):
                o_ref[i] += p

    return _call(
        body, "ffn_bwd_w", (N_CHIPS, t // TK_W), specs,
        [pl.BlockSpec((None, len(terms), fs, d), lambda j, kt: (j, 0, 0, 0))],
        [_sds((N_CHIPS, len(terms), fs, d), F32)], args, carry=carry)


def _ffn_bwd_dh(da, du, wgu, x, g, sc, dxo, seq, carry=None):
    _, t, fs = da.shape
    d = x.shape[-1]
    per = seq // TM
    nb = t // seq

    def body(da_ref, du_ref, w_ref, x_ref, g_ref, sc_ref, dxo_ref, dx_ref, dsh_ref, dsc_ref, dg_ref):
        m = pl.program_id(0)
        dh = _nt(da_ref[0], w_ref[0, 0]) + _nt(du_ref[0], w_ref[0, 1])
        for j in range(1, N_CHIPS):
            dh = dh + _nt(da_ref[j], w_ref[j, 0]) + _nt(du_ref[j], w_ref[j, 1])
        dx, dsh, dsc, dg = _modnorm_bwd_tile(dh, x_ref[...], g_ref[...], sc_ref[...], dxo_ref[...])
        dx_ref[...] = dx
        _acc_rows(dsh_ref, dsh, m % per == 0)
        _acc_rows(dsc_ref, dsc, m % per == 0)
        _acc_rows(dg_ref, dg, m == 0)

    row = pl.BlockSpec((TM, d), lambda m: (m, 0))
    blk = pl.BlockSpec((N_CHIPS, TM, fs), lambda m: (0, m, 0))
    ex = pl.BlockSpec((None, 1, d), lambda m: (m // per, 0, 0))
    vec = pl.BlockSpec((1, d), lambda m: (0, 0))
    return _call(
        body, "ffn_bwd_dh", (t // TM,),
        [blk, blk, _whole(wgu), row, vec, ex, row],
        [row, ex, ex, vec],
        [_sds((t, d), F32), _sds((nb, 1, d), F32), _sds((nb, 1, d), F32), _sds((1, d), F32)],
        (da, du, wgu, x, g, sc, dxo), carry=carry)


def _qkv_proj(x, g, sc, sh, w_in, seq, carry=None):
    t, d = x.shape
    wc = w_in.shape[-1]
    per = seq // TM

    dils = [dil for _, dil in DIL_CONFIGS if dil > 1]

    def body(x_ref, g_ref, sc_ref, sh_ref, w_ref, h_ref, o_ref, *rest):
        res_refs, buf = rest[:len(dils)], rest[len(dils)]
        hv = _modnorm_tile(x_ref, g_ref, sc_ref, sh_ref)
        h_ref[...] = hv
        for j in range(N_CHIPS):
            rf = _nn(hv, w_ref[j, 0])
            r = rf.astype(BF)
            for a, lc, off, width in _col_pieces(j, wc):
                o_ref[a, :, lc:lc + width] = r[:, off:off + width]
                if a < 3:
                    continue
                for c0 in range(0, width, 128):
                    cg = (lc + c0) // 128
                    buf[...] = rf[:, off + c0:off + c0 + 128]
                    for ref, dil in zip(res_refs, dils):
                        for rr in range(dil):
                            ref[a - 3, :, rr * D_GRP + cg * 128:rr * D_GRP + (cg + 1) * 128] = (
                                buf[pl.ds(rr, TM // dil, stride=dil), :].astype(BF))

    row = pl.BlockSpec((TM, d), lambda m: (m, 0))
    ex = pl.BlockSpec((None, 1, d), lambda m: (m // per, 0, 0))
    return _call(
        body, "qkv_proj", (t // TM,),
        [row, pl.BlockSpec((1, d), lambda m: (0, 0)), ex, ex, _whole(w_in)],
        [row, pl.BlockSpec((6, TM, D_GRP), lambda m: (0, m, 0))]
        + [pl.BlockSpec((3, TM // dil, dil * D_GRP), lambda m: (0, m, 0)) for dil in dils],
        [_sds((t, d), BF), _sds((6, t, D_GRP), BF)] + [_sds((3, t // dil, dil * D_GRP), BF) for dil in dils],
        (x, g, sc, sh, w_in), scratch=[pltpu.VMEM((TM, 128), F32)], carry=carry)


def _col_pieces(j, wc):
    out, off = [], 0
    while off < wc:
        a, lc = divmod(j * wc + off, D_GRP)
        width = min(D_GRP - lc, wc - off)
        out.append((a, lc, off, width))
        off += width
    return out


def _chip_cols(g6_ref, j, wc):
    return jnp.concatenate([g6_ref[a, :, lc:lc + width] for a, lc, _, width in _col_pieces(j, wc)], axis=1)


def _mix_out(on_sb, on_dil, w_out, x, gt, seq):
    t, d = x.shape
    per = seq // TM

    def body(a_ref, b_ref, w_ref, x_ref, gt_ref, t_ref, xo_ref):
        tv = _nn(a_ref[...], w_ref[0:D_GRP, :]) + _nn(b_ref[...], w_ref[D_GRP:2 * D_GRP, :])
        t_ref[...] = tv.astype(BF)
        xo_ref[...] = x_ref[...] + gt_ref[...] * tv

    row = pl.BlockSpec((TM, d), lambda m: (m, 0))
    half = pl.BlockSpec((TM, D_GRP), lambda m: (m, 0))
    return pl.pallas_call(
        body, name="mix_out", grid=(t // TM,),
        in_specs=[half, half, pl.BlockSpec((2 * D_GRP, d), lambda m: (0, 0)), row,
                  pl.BlockSpec((None, 1, d), lambda m: (m // per, 0, 0))],
        out_specs=[row, row],
        out_shape=[_sds((t, d), BF), _sds((t, d), F32)],
        compiler_params=_cp(1))(on_sb, on_dil, w_out, x, gt)


def _sb_masks():
    lane = lax.broadcasted_iota(jnp.int32, (1, 2 * HEAD_DIM), 1)
    hm0 = lane < HEAD_DIM
    rel = lax.broadcasted_iota(jnp.int32, (TQ, KB), 0) - lax.broadcasted_iota(jnp.int32, (TQ, KB), 1)
    kr = lax.broadcasted_iota(jnp.int32, (KB, KB), 0)
    kc = lax.broadcasted_iota(jnp.int32, (KB, KB), 1)
    return hm0, rel, kr, kc


def _headnorm_pair(o, gv, hm0):
    o2 = o * o
    ms0 = jnp.sum(jnp.where(hm0, o2, 0.0), axis=-1, keepdims=True) * (1.0 / HEAD_DIM)
    ms1 = jnp.sum(jnp.where(hm0, 0.0, o2), axis=-1, keepdims=True) * (1.0 / HEAD_DIM)
    r = jnp.where(hm0, lax.rsqrt(ms0 + EPS), lax.rsqrt(ms1 + EPS))
    return (o * r) * gv


SB_DEAD = -104.0


def _alive(c_l):
    return (jnp.max(c_l) > SB_DEAD).astype(jnp.int32)


def _sb_fwd(qkv6, g_sb, nb, seq, carry=None):
    nq = seq // TQ

    def body(q_ref, k_ref, v_ref, g_ref, o_ref, on_ref):
        qi = pl.program_id(2)
        hm0, rel, kr, kc = _sb_masks()
        upper = _twice((kr > kc).astype(BF))
        heads = _dil_masks()[0]
        qs = _stack_heads(q_ref[...] * SCALE, heads)
        causal2 = jnp.concatenate([rel] * GRP_HEADS, axis=0) > 0

        def block(kj, causal, c_l, acc):
            ks = pl.multiple_of(kj * KB, KB)
            z = _nt(qs, k_ref[pl.ds(ks, KB), :])
            sp = _softplus(z)
            spm = sp if causal is None else jnp.where(causal, sp, 0.0)
            suf = _nn2(spm, upper)
            w = jnp.exp((z - sp) + (c_l - suf))
            if causal is not None:
                w = jnp.where(causal, w, 0.0)
            return c_l - (suf[:, 0:1] + spm[:, 0:1]), acc + _nn(w.astype(BF), v_ref[pl.ds(ks, KB), :])

        c_l, acc = block(qi, causal2, jnp.zeros((GRP_HEADS * TQ, 1), F32), jnp.zeros((GRP_HEADS * TQ, GRP_W), F32))

        def cond(carry):
            return jnp.logical_and(carry[0] <= qi, carry[1] > 0)

        def kbody(carry):
            it, _, c_l, acc = carry
            c_l, acc = block(qi - it, None, c_l, acc)
            return it + 1, _alive(c_l), c_l, acc

        acc = lax.while_loop(cond, kbody, (jnp.int32(1), _alive(c_l), c_l, acc))[3]
        o = _unstack_heads(acc, heads, TQ)
        o_ref[...] = o.astype(BF)
        gv = g_ref[...]
        for half in range(GRP_W // 128):
            lanes = slice(half * 128, (half + 1) * 128)
            on_ref[:, lanes] = _headnorm_pair(o[:, lanes], gv[:, lanes], hm0).astype(BF)

    w = GRP_W
    full = lambda i: pl.BlockSpec((None, None, seq, w), lambda b, hp, q: (i, b, 0, hp))
    qblk = pl.BlockSpec((None, None, TQ, w), lambda b, hp, q: (0, b, q, hp))
    oblk = pl.BlockSpec((None, TQ, w), lambda b, hp, q: (b, q, hp))
    return _call(
        body, "sb_fwd", (nb, N_HEADS // GRP_HEADS, nq),
        [qblk, full(1), full(2), pl.BlockSpec((1, w), lambda b, hp, q: (0, hp))],
        [oblk, oblk],
        [_sds((nb, seq, D_GRP), BF), _sds((nb, seq, D_GRP), BF)],
        (qkv6, qkv6, qkv6, g_sb), carry=carry)


def _sb_bwd(qkv6, do, nb, seq, carry=None):
    nq = seq // TQ
    nk = seq // KB

    def body(q_ref, k_ref, v_ref, do_ref, out_ref, dk_acc, dv_acc, g_st, s_st):
        qi = pl.program_id(2)
        hm0, rel, kr, kc = _sb_masks()
        upper = _twice((kr > kc).astype(BF))
        lower = (kr < kc).astype(BF)

        @pl.when(qi == 0)
        def _():
            dk_acc[...] = jnp.zeros_like(dk_acc)
            dv_acc[...] = jnp.zeros_like(dv_acc)

        heads = _dil_masks()[0]
        qs = _stack_heads(q_ref[...] * SCALE, heads)
        dos = _stack_heads(do_ref[...], heads)
        causal2 = jnp.concatenate([rel] * GRP_HEADS, axis=0) > 0

        def weights(kj, causal, c_l):
            ks = pl.multiple_of(kj * KB, KB)
            vb = v_ref[pl.ds(ks, KB), :]
            z = _nt(qs, k_ref[pl.ds(ks, KB), :])
            sp = _softplus(z)
            spm = sp if causal is None else jnp.where(causal, sp, 0.0)
            suf = _nn2(spm, upper)
            lsz = z - sp
            w = jnp.exp(lsz + (c_l - suf))
            if causal is not None:
                w = jnp.where(causal, w, 0.0)
            g_st[kj] = w * _nt(dos, vb)
            s_st[kj] = jnp.exp(lsz)
            dv_acc[pl.ds(ks, KB), :] += _tn(w.astype(BF), dos)
            return c_l - (suf[:, 0:1] + spm[:, 0:1])

        zc = jnp.zeros((GRP_HEADS * TQ, 1), F32)
        c_l = weights(qi, causal2, zc)

        def acond(carry):
            return jnp.logical_and(carry[0] <= qi, carry[1] > 0)

        def abody(carry):
            c_l = weights(qi - carry[0], None, carry[2])
            return carry[0] + 1, _alive(c_l), c_l

        n_used = lax.while_loop(acond, abody, (jnp.int32(1), _alive(c_l), c_l))[0]

        def grads(kj, causal, c_g, dq):
            ks = pl.multiple_of(kj * KB, KB)
            kb = k_ref[pl.ds(ks, KB), :]
            g = g_st[kj]
            sig = s_st[kj]
            pre = _nn(g.astype(BF), lower)
            dz = g * (1.0 - sig) - sig * (pre + c_g)
            if causal is not None:
                dz = jnp.where(causal, dz, 0.0)
            dzb = dz.astype(BF)
            dk_acc[pl.ds(ks, KB), :] += _tn(dzb, qs)
            return c_g + (pre[:, KB - 1:KB] + g[:, KB - 1:KB]), dq + _nn(dzb, kb)

        c_g, dq = lax.fori_loop(qi - n_used + 1, qi, lambda kj, cr: grads(kj, None, *cr),
                                (zc, jnp.zeros((GRP_HEADS * TQ, GRP_W), F32)))
        _, dq = grads(qi, causal2, c_g, dq)
        dq = _unstack_heads(dq, heads, TQ) * SCALE
        out_ref[0, pl.ds(pl.multiple_of(qi * TQ, TQ), TQ), :] = dq.astype(BF)

        @pl.when(qi == nq - 1)
        def _():
            out_ref[1] = dk_acc[...].astype(BF)
            out_ref[2] = dv_acc[...].astype(BF)

    w = GRP_W
    full = lambda i: pl.BlockSpec((None, None, seq, w), lambda b, hp, q: (i, b, 0, hp))
    qblk = pl.BlockSpec((None, None, TQ, w), lambda b, hp, q: (0, b, q, hp))
    oblk = pl.BlockSpec((None, TQ, w), lambda b, hp, q: (b, q, hp))
    return _call(
        body, "sb_bwd", (nb, N_HEADS // GRP_HEADS, nq),
        [qblk, full(1), full(2), oblk],
        [pl.BlockSpec((3, None, seq, w), lambda b, hp, q: (0, b, 0, hp))],
        [_sds((6, nb, seq, D_GRP), BF)], (qkv6, qkv6, qkv6, do),
        scratch=[pltpu.VMEM((seq, w), F32), pltpu.VMEM((seq, w), F32),
                 pltpu.VMEM((nk, GRP_HEADS * TQ, KB), F32), pltpu.VMEM((nk, GRP_HEADS * TQ, KB), F32)],
        carry=carry)


def _t5_bucket(n):
    max_exact = N_BUCKETS // 2
    nf = np.maximum(n, 1).astype(np.float32)
    large = max_exact + (np.log(nf / max_exact) / math.log(MAX_DISTANCE / max_exact)
                         * (N_BUCKETS - max_exact)).astype(np.int32)
    large = np.minimum(large, N_BUCKETS - 1)
    return np.where(n < max_exact, n, large).astype(np.int32)


def _bucket_map(dilation):
    step = BLOCK + np.arange(BLOCK)[:, None] - np.arange(2 * BLOCK)[None, :]
    return _t5_bucket(np.clip(step, 0, N_STEPS) * dilation)


GRP_HEADS = 4
GRP_W = GRP_HEADS * HEAD_DIM


def _dil_masks():
    lane = lax.broadcasted_iota(jnp.int32, (1, GRP_W), 1)
    heads = [jnp.logical_and(lane >= HEAD_DIM * i, lane < HEAD_DIM * (i + 1)) for i in range(GRP_HEADS)]
    iq = jnp.bitwise_and(lax.broadcasted_iota(jnp.int32, (GRP_HEADS * BLOCK, BLOCK), 0), BLOCK - 1)
    ik = lax.broadcasted_iota(jnp.int32, (GRP_HEADS * BLOCK, BLOCK), 1)
    return heads, ik <= iq, ik >= iq


def _stack_heads(x, heads):
    zero = jnp.zeros_like(x)
    return jnp.concatenate([jnp.where(hm, x, zero) for hm in heads], axis=0)


def _unstack_heads(xs, heads, rows=BLOCK):
    out = xs[0:rows]
    for i in range(1, GRP_HEADS):
        out = jnp.where(heads[i], xs[i * rows:(i + 1) * rows], out)
    return out


def _dil_rows(n):
    rs = pl.multiple_of(n * BLOCK, BLOCK)
    ps = pl.multiple_of(jnp.maximum(n - 1, 0) * BLOCK, BLOCK)
    return pl.ds(rs, BLOCK), pl.ds(ps, BLOCK)


def _dil_probs(qs, kc, kp, b_ref, gi, valid_c, valid_p):
    rows = slice(gi * GRP_HEADS * BLOCK, (gi + 1) * GRP_HEADS * BLOCK)
    zc = _nt(qs, kc) * SCALE + b_ref[rows, BLOCK:2 * BLOCK]
    zp = _nt(qs, kp) * SCALE + b_ref[rows, 0:BLOCK]
    zc = jnp.where(valid_c, zc, NEG_INF)
    zp = jnp.where(valid_p, zp, NEG_INF)
    m = jnp.maximum(jnp.max(zc, axis=-1, keepdims=True), jnp.max(zp, axis=-1, keepdims=True))
    ec = jnp.exp(zc - m)
    ep = jnp.exp(zp - m)
    den = jnp.sum(ec, axis=-1, keepdims=True) + jnp.sum(ep, axis=-1, keepdims=True)
    return ec, ep, den, m


def _dil_fwd(qkv6r, base, bias, nb, sub_len, dilation):
    n_blk = sub_len // BLOCK

    def body(q_ref, k_ref, v_ref, b_ref, o_ref, l_ref):
        heads, valid_c, valid_p0 = _dil_masks()

        def nbody(n, carry):
            cur, prev = _dil_rows(n)
            valid_p = jnp.logical_and(valid_p0, n > 0)
            for gi in range(N_HEADS // GRP_HEADS):
                lanes = slice(gi * GRP_W, (gi + 1) * GRP_W)
                qs = _stack_heads(q_ref[cur, lanes], heads)
                ec, ep, den, m = _dil_probs(qs, k_ref[cur, lanes], k_ref[prev, lanes], b_ref, gi, valid_c, valid_p)
                o = (_nn(ec.astype(BF), v_ref[cur, lanes]) + _nn(ep.astype(BF), v_ref[prev, lanes])) / den
                o_ref[cur, lanes] = _unstack_heads(o, heads).astype(BF)
                l_ref[cur, lanes] = _unstack_heads(jnp.broadcast_to(m + jnp.log(den), o.shape), heads)
            return carry

        lax.fori_loop(0, n_blk, nbody, 0)

    seqblk = lambda i: pl.BlockSpec((None, None, sub_len, D_GRP), lambda b, r: (i, b, 0, r))
    oblk = pl.BlockSpec((None, sub_len, D_GRP), lambda b, r: (b, 0, r))
    shp = _sds((nb, sub_len, dilation * D_GRP), F32)
    return pl.pallas_call(
        body, name="dil_fwd_%d" % dilation, grid=(nb, dilation),
        in_specs=[seqblk(base), seqblk(base + 1), seqblk(base + 2), _whole(bias)],
        out_specs=[oblk, oblk], out_shape=[_sds(shp.shape, BF), shp],
        compiler_params=_cp(2))(qkv6r, qkv6r, qkv6r, bias)


def _dil_bwd(qkv6r, base, bias, do_c, dd_c, nb, sub_len, dilation, carry=None):
    n_blk = sub_len // BLOCK

    def body(q_ref, k_ref, v_ref, b_ref, do_ref, dd_ref, out_ref, a_ref, dk_acc, dv_acc):
        heads, valid_c, valid_p0 = _dil_masks()
        first = jnp.logical_and(pl.program_id(0) == 0, pl.program_id(1) == 0)

        @pl.when(first)
        def _():
            a_ref[...] = jnp.zeros_like(a_ref)

        dk_acc[...] = jnp.zeros_like(dk_acc)
        dv_acc[...] = jnp.zeros_like(dv_acc)

        def nbody(n, carry):
            cur, prev = _dil_rows(n)
            valid_p = jnp.logical_and(valid_p0, n > 0)
            for gi in range(N_HEADS // GRP_HEADS):
                lanes = slice(gi * GRP_W, (gi + 1) * GRP_W)
                kc, kp = k_ref[cur, lanes], k_ref[prev, lanes]
                vc, vp = v_ref[cur, lanes], v_ref[prev, lanes]
                qs = _stack_heads(q_ref[cur, lanes], heads)
                dos = _stack_heads(do_ref[cur, lanes], heads).astype(BF)
                dds = jnp.sum(_stack_heads(dd_ref[cur, lanes], heads), axis=-1, keepdims=True) * (1.0 / HEAD_DIM)
                ec, ep, den, _ = _dil_probs(qs, kc, kp, b_ref, gi, valid_c, valid_p)
                inv = 1.0 / den
                pc = ec * inv
                pp = ep * inv
                dzc = pc * (_nt(dos, vc) + dds)
                dzp = pp * (_nt(dos, vp) + dds)
                rows = slice(gi * GRP_HEADS * BLOCK, (gi + 1) * GRP_HEADS * BLOCK)
                a_ref[rows, BLOCK:2 * BLOCK] += dzc
                a_ref[rows, 0:BLOCK] += dzp
                dzcb = (dzc * SCALE).astype(BF)
                dzpb = (dzp * SCALE).astype(BF)
                out_ref[0, cur, lanes] = _unstack_heads(_nn(dzcb, kc) + _nn(dzpb, kp), heads).astype(BF)
                dk_acc[cur, lanes] += _tn(dzcb, qs)
                dk_acc[prev, lanes] += _tn(dzpb, qs)
                dv_acc[cur, lanes] += _tn(pc.astype(BF), dos)
                dv_acc[prev, lanes] += _tn(pp.astype(BF), dos)
            return carry

        lax.fori_loop(0, n_blk, nbody, 0)
        out_ref[1] = dk_acc[...].astype(BF)
        out_ref[2] = dv_acc[...].astype(BF)

    seqblk = lambda i: pl.BlockSpec((None, None, sub_len, D_GRP), lambda b, r: (i, b, 0, r))
    oblk = pl.BlockSpec((None, sub_len, D_GRP), lambda b, r: (b, 0, r))
    return _call(
        body, "dil_bwd_%d" % dilation, (nb, dilation),
        [seqblk(base), seqblk(base + 1), seqblk(base + 2), _whole(bias), oblk, oblk],
        [pl.BlockSpec((3, None, sub_len, D_GRP), lambda b, r: (0, b, 0, r)),
         pl.BlockSpec((N_HEADS * BLOCK, 2 * BLOCK), lambda b, r: (0, 0))],
        [_sds((3, nb, sub_len, dilation * D_GRP), BF), _sds((N_HEADS * BLOCK, 2 * BLOCK), F32)],
        (qkv6r, qkv6r, qkv6r, bias, do_c, dd_c),
        scratch=[pltpu.VMEM((sub_len, D_GRP), F32)] * 2, carry=carry)


def _group_ones():
    idx = np.arange(D_GRP) // HEAD_DIM
    return jnp.asarray((idx[:, None] == idx[None, :]).astype(np.float32), dtype=BF)


def _dil_alphas(l1, l4, l16):
    mx = jnp.maximum(jnp.maximum(l1, l4), l16)
    e1 = jnp.exp(l1 - mx)
    e4 = jnp.exp(l4 - mx)
    e16 = jnp.exp(l16 - mx)
    den = e1 + e4 + e16
    return e1 / den, e4 / den, e16 / den


def _residue_spec(dil):
    return pl.BlockSpec((TM // dil, dil * D_GRP), lambda m: (m, 0))


def _from_residue(src, dil, cg, buf):
    if dil == 1:
        return src[:, cg * 128:(cg + 1) * 128].astype(F32)
    for r in range(dil):
        buf[pl.ds(r, TM // dil, stride=dil), :] = (
            src[:, r * D_GRP + cg * 128:r * D_GRP + (cg + 1) * 128].astype(F32))
    return buf[...]


def _to_residue(dst, dil, cg, buf, val):
    if dil == 1:
        dst[:, cg * 128:(cg + 1) * 128] = val.astype(dst.dtype)
        return
    buf[...] = val
    for r in range(dil):
        dst[:, r * D_GRP + cg * 128:r * D_GRP + (cg + 1) * 128] = (
            buf[pl.ds(r, TM // dil, stride=dil), :].astype(dst.dtype))


def _pair_sum(x, hm0):
    s0 = jnp.sum(jnp.where(hm0, x, 0.0), axis=-1, keepdims=True)
    s1 = jnp.sum(jnp.where(hm0, 0.0, x), axis=-1, keepdims=True)
    return jnp.where(hm0, s0, s1)


def _dil_comb(os, ls, g_dil):
    t = os[0].shape[0]
    dils = [dil for _, dil in DIL_CONFIGS]

    def body(o1, l1, o4, l4, o16, l16, g_ref, o_ref, on_ref, b0, b1, b2, b3):
        hm0 = lax.broadcasted_iota(jnp.int32, (1, 128), 1) < HEAD_DIM
        for cg in range(D_GRP // 128):
            lanes = slice(cg * 128, (cg + 1) * 128)
            ov = [_from_residue(src, dil, cg, buf) for src, dil, buf in zip((o1, o4, o16), dils, (None, b0, b1))]
            lv = [_from_residue(src, dil, cg, buf) for src, dil, buf in zip((l1, l4, l16), dils, (None, b2, b3))]
            a1, a4, a16 = _dil_alphas(*lv)
            o = a1 * ov[0] + a4 * ov[1] + a16 * ov[2]
            o_ref[:, lanes] = o.astype(BF)
            on_ref[:, lanes] = _headnorm_pair(o, g_ref[:, lanes], hm0).astype(BF)

    blk = pl.BlockSpec((TM, D_GRP), lambda m: (m, 0))
    specs = [_residue_spec(dil) for dil in dils for _ in range(2)]
    return pl.pallas_call(
        body, name="dil_comb", grid=(t // TM,),
        in_specs=specs + [pl.BlockSpec((1, D_GRP), lambda m: (0, 0))],
        out_specs=[blk, blk],
        out_shape=[_sds((t, D_GRP), BF), _sds((t, D_GRP), BF)],
        scratch_shapes=[pltpu.VMEM((TM, 128), F32)] * 4,
        compiler_params=_cp(1))(os[0], ls[0], os[1], ls[1], os[2], ls[2], g_dil)


def _dil_comb_bwd(do, os, ls):
    t = do.shape[0]
    dils = [dil for _, dil in DIL_CONFIGS]

    def body(do_ref, o1, l1, o4, l4, o16, l16, d1, d4, d16, e1, e4, e16, b0, b1, b2, b3):
        hm0 = lax.broadcasted_iota(jnp.int32, (1, 128), 1) < HEAD_DIM
        for cg in range(D_GRP // 128):
            dov = do_ref[:, cg * 128:(cg + 1) * 128].astype(F32)
            ov = [_from_residue(src, dil, cg, buf) for src, dil, buf in zip((o1, o4, o16), dils, (None, b0, b1))]
            lv = [_from_residue(src, dil, cg, buf) for src, dil, buf in zip((l1, l4, l16), dils, (None, b2, b3))]
            al = _dil_alphas(*lv)
            sbar = al[0] * _pair_sum(dov * ov[0], hm0)
            for a_c, o_c in zip(al[1:], ov[1:]):
                sbar = sbar + a_c * _pair_sum(dov * o_c, hm0)
            for a_c, dil, dref, eref in zip(al, dils, (d1, d4, d16), (e1, e4, e16)):
                _to_residue(dref, dil, cg, b0, a_c * dov)
                _to_residue(eref, dil, cg, b1, -a_c * sbar)

    specs = [_residue_spec(dil) for dil in dils]
    return pl.pallas_call(
        body, name="dil_comb_bwd", grid=(t // TM,),
        in_specs=[pl.BlockSpec((TM, D_GRP), lambda m: (m, 0))] + [sp for sp in specs for _ in range(2)],
        out_specs=specs + specs,
        out_shape=[_sds((t // dil, dil * D_GRP), BF) for dil in dils]
        + [_sds((t // dil, dil * D_GRP), F32) for dil in dils],
        scratch_shapes=[pltpu.VMEM((TM, 128), F32)] * 4,
        compiler_params=_cp(1))(do, os[0], ls[0], os[1], ls[1], os[2], ls[2])


def _dqkv_dil_sum(ds, dqkv6):
    t = dqkv6.shape[1]
    dils = [dil for _, dil in DIL_CONFIGS]

    def body(*refs):
        srcs, o_ref, acc = refs[:len(dils)], refs[len(dils) + 1], refs[len(dils) + 2]
        for a in range(3):
            for cg in range(D_GRP // 128):
                for src, dil in zip(srcs, dils):
                    for r in range(dil):
                        part = src[a, :, r * D_GRP + cg * 128:r * D_GRP + (cg + 1) * 128].astype(F32)
                        rows = pl.ds(r, TM // dil, stride=dil) if dil > 1 else slice(None)
                        if dil == dils[0]:
                            acc[rows, :] = part
                        else:
                            acc[rows, :] += part
                o_ref[a, :, cg * 128:(cg + 1) * 128] = acc[...].astype(BF)

    return pl.pallas_call(
        body, name="dqkv_dil_sum", grid=(t // TM,),
        in_specs=[pl.BlockSpec((3, TM // dil, dil * D_GRP), lambda m: (0, m, 0)) for dil in dils]
        + [pl.BlockSpec(memory_space=pl.ANY)],
        out_specs=pl.BlockSpec((3, TM, D_GRP), lambda m: (1, m, 0)),
        out_shape=_sds((6, t, D_GRP), BF), input_output_aliases={len(dils): 0},
        scratch_shapes=[pltpu.VMEM((TM, 128), F32)],
        compiler_params=_cp(1))(*ds, dqkv6)


def _relbias_grad(a_all, onehot):
    def body(a_ref, oh_ref, o_ref):
        acc = jnp.zeros((N_HEADS, N_BUCKETS), F32)
        for c in range(len(DIL_CONFIGS)):
            av = a_ref[c]
            hi = av.astype(BF)
            lo = (av - hi.astype(F32)).astype(BF)
            acc = acc + _nt(hi, oh_ref[c]) + _nt(lo, oh_ref[c])
        o_ref[...] = acc

    return pl.pallas_call(body, name="relbias_grad", out_shape=_sds((N_HEADS, N_BUCKETS), F32),
                          compiler_params=_cp())(a_all, onehot)


def _headnorm_bwd(dn, o, gv, mv):
    ms = _nn2(o * o, mv) * (1.0 / HEAD_DIM)
    r = lax.rsqrt(ms + EPS)
    nrm = o * r
    dg = jnp.sum(dn * nrm, axis=0, keepdims=True)
    dnn = dn * gv
    do = r * (dnn - nrm * (_nn2(dnn * nrm, mv) * (1.0 / HEAD_DIM)))
    return do, dg


def _mix_bwd_out(dx, gt, tv, w_out, o_sb, o_dil, on_sb, on_dil, g_sb, g_dil, ones_g, seq, carry=None):
    t, d = dx.shape
    per = seq // TM
    nb = t // seq

    def body(dx_ref, gt_ref, t_ref, w_ref, osb, odl, onsb, ondl, gsb, gdl, m_ref,
             dosb, dodl, dgt_ref, dgsb, dgdl, dw_ref):
        m = pl.program_id(0)
        dxv = dx_ref[...]
        dt = (gt_ref[...] * dxv).astype(BF)
        _acc_rows(dgt_ref, jnp.sum(dxv * t_ref[...].astype(F32), axis=0, keepdims=True), m % per == 0)
        mv = _twice(m_ref[...])
        don_sb = _nt(dt, w_ref[0:D_GRP, :])
        don_dl = _nt(dt, w_ref[D_GRP:2 * D_GRP, :])
        do1, dg1 = _headnorm_bwd(don_sb, osb[...].astype(F32), gsb[...], mv)
        do2, dg2 = _headnorm_bwd(don_dl, odl[...].astype(F32), gdl[...], mv)
        dosb[...] = do1.astype(BF)
        dodl[...] = do2.astype(BF)
        _acc_rows(dgsb, dg1, m == 0)
        _acc_rows(dgdl, dg2, m == 0)
        p1 = _tn(onsb[...], dt)
        p2 = _tn(ondl[...], dt)

        @pl.when(m == 0)
        def _():
            dw_ref[0:D_GRP, :] = p1
            dw_ref[D_GRP:2 * D_GRP, :] = p2

        @pl.when(m != 0)
        def _():
            dw_ref[0:D_GRP, :] += p1
            dw_ref[D_GRP:2 * D_GRP, :] += p2

    row = pl.BlockSpec((TM, d), lambda m: (m, 0))
    half = pl.BlockSpec((TM, D_GRP), lambda m: (m, 0))
    ex = pl.BlockSpec((None, 1, d), lambda m: (m // per, 0, 0))
    gvec = pl.BlockSpec((1, D_GRP), lambda m: (0, 0))
    wblk = pl.BlockSpec((2 * D_GRP, d), lambda m: (0, 0))
    return _call(
        body, "mix_bwd_out", (t // TM,),
        [row, ex, row, wblk, half, half, half, half, gvec, gvec, pl.BlockSpec((D_GRP, D_GRP), lambda m: (0, 0))],
        [half, half, ex, gvec, gvec, wblk],
        [_sds((t, D_GRP), BF), _sds((t, D_GRP), BF), _sds((nb, 1, d), F32),
         _sds((1, D_GRP), F32), _sds((1, D_GRP), F32), _sds((2 * D_GRP, d), F32)],
        (dx, gt, tv, w_out, o_sb, o_dil, on_sb, on_dil, g_sb, g_dil, ones_g), carry=carry)


def _dw_in(h, dqkv6, carry=None):
    t, d = h.shape
    wc = 6 * D_GRP // N_CHIPS

    def body(h_ref, g_ref, o_ref):
        kt = pl.program_id(0)
        hv = h_ref[...]
        for j in range(N_CHIPS):
            p = _tn(hv, _chip_cols(g_ref, j, wc))

            @pl.when(kt == 0)
            def _(p=p, j=j):
                o_ref[j, 0] = p

            @pl.when(kt != 0)
            def _(p=p, j=j):
                o_ref[j, 0] += p

    return _call(
        body, "dw_in", (t // TK_W,),
        [pl.BlockSpec((TK_W, d), lambda kt: (kt, 0)), pl.BlockSpec((6, TK_W, D_GRP), lambda kt: (0, kt, 0))],
        [pl.BlockSpec((N_CHIPS, 1, d, wc), lambda kt: (0, 0, 0, 0))],
        [_sds((N_CHIPS, 1, d, wc), F32)], (h, dqkv6), carry=carry)


def _mix_bwd_dh(dqkv6, w_in, x, g, sc, dxo, seq, carry=None):
    _, t, _ = dqkv6.shape
    d = x.shape[-1]
    wc = w_in.shape[-1]
    per = seq // TM
    nb = t // seq

    def body(g6_ref, w_ref, x_ref, g_ref, sc_ref, dxo_ref, dx_ref, dsh_ref, dsc_ref, dg_ref):
        m = pl.program_id(0)
        dh = _nt(_chip_cols(g6_ref, 0, wc), w_ref[0, 0])
        for j in range(1, N_CHIPS):
            dh = dh + _nt(_chip_cols(g6_ref, j, wc), w_ref[j, 0])
        dx, dsh, dsc, dg = _modnorm_bwd_tile(dh, x_ref[...], g_ref[...], sc_ref[...], dxo_ref[...])
        dx_ref[...] = dx
        _acc_rows(dsh_ref, dsh, m % per == 0)
        _acc_rows(dsc_ref, dsc, m % per == 0)
        _acc_rows(dg_ref, dg, m == 0)

    row = pl.BlockSpec((TM, d), lambda m: (m, 0))
    ex = pl.BlockSpec((None, 1, d), lambda m: (m // per, 0, 0))
    vec = pl.BlockSpec((1, d), lambda m: (0, 0))
    return _call(
        body, "mix_bwd_dh", (t // TM,),
        [pl.BlockSpec((6, TM, D_GRP), lambda m: (0, m, 0)), _whole(w_in), row, vec, ex, row],
        [row, ex, ex, vec],
        [_sds((t, d), F32), _sds((nb, 1, d), F32), _sds((nb, 1, d), F32), _sds((1, d), F32)],
        (dqkv6, w_in, x, g, sc, dxo), carry=carry)


def _ffn_down_loss(s, wd, x, gt, seq, coef, g, target):
    _, t, fs = s.shape
    d = x.shape[-1]
    per = seq // TM
    steps = t // TM

    def body(s_ref, w_ref, x_ref, gt_ref, g_ref, t_ref, f_ref, dx_ref, dg_ref, loss_ref, lacc):
        m = pl.program_id(0)
        f = _nn(s_ref[0], w_ref[0, 0])
        for j in range(1, N_CHIPS):
            f = f + _nn(s_ref[j], w_ref[j, 0])
        f_ref[...] = f.astype(BF)
        xv = x_ref[...] + (coef * gt_ref[...]) * f
        gv = g_ref[...]
        r = lax.rsqrt(jnp.mean(xv * xv, axis=-1, keepdims=True) + EPS)
        n = xv * r
        err = n * gv - t_ref[...]
        dy = err * (1.0 / d)
        _acc_rows(dg_ref, jnp.sum(dy * n, axis=0, keepdims=True), m == 0)
        dn = dy * gv
        dx_ref[...] = r * (dn - n * jnp.mean(dn * n, axis=-1, keepdims=True))
        _acc_rows(lacc, jnp.sum(err * err, axis=0, keepdims=True), m == 0)

        @pl.when(m == steps - 1)
        def _():
            tot = jnp.sum(lacc[...], axis=-1, keepdims=True) * (0.5 / d)
            loss_ref[...] = jnp.broadcast_to(tot, (1, 128))

    row = pl.BlockSpec((TM, d), lambda m: (m, 0))
    vec = pl.BlockSpec((1, d), lambda m: (0, 0))
    return pl.pallas_call(
        body, name="ffn_down_loss", grid=(steps,),
        in_specs=[pl.BlockSpec((N_CHIPS, TM, fs), lambda m: (0, m, 0)), _whole(wd), row,
                  pl.BlockSpec((None, 1, d), lambda m: (m // per, 0, 0)), vec, row],
        out_specs=[row, row, vec, pl.BlockSpec((1, 128), lambda m: (0, 0))],
        out_shape=[_sds((t, d), BF), _sds((t, d), F32), _sds((1, d), F32), _sds((1, 128), F32)],
        scratch_shapes=[pltpu.VMEM((1, d), F32)],
        compiler_params=_cp(1))(s, wd, x, gt, g, target)


def _row_tile(rows, cols):
    best = rows
    for tr in range(8, rows + 1, 8):
        if rows % tr == 0 and tr * cols * 4 <= (1 << 20):
            best = tr
    if best * cols * 4 > (1 << 21):
        best = 8
    return best


def _adamw(w, g_arr, g_sel, m, v):
    rows, cols = w.shape
    tr = _row_tile(rows, cols)
    b1c = 1.0 - ADAM_B1 ** ADAM_STEP
    b2c = 1.0 - ADAM_B2 ** ADAM_STEP

    def body(w_ref, g_ref, m_ref, v_ref, go_ref, d_ref, mo_ref, vo_ref):
        gv = g_ref[...]
        mn = ADAM_B1 * m_ref[...] + (1.0 - ADAM_B1) * gv
        vn = ADAM_B2 * v_ref[...] + (1.0 - ADAM_B2) * (gv * gv)
        go_ref[...] = gv
        mo_ref[...] = mn
        vo_ref[...] = vn
        d_ref[...] = -ADAM_LR * ((mn / b1c) / (jnp.sqrt(vn / b2c) + ADAM_EPS) + ADAM_WD * w_ref[...])

    blk = pl.BlockSpec((tr, cols), lambda i: (i, 0))
    shp = _sds((rows, cols), F32)
    return pl.pallas_call(
        body, name="adamw", grid=(rows // tr,),
        in_specs=[blk, pl.BlockSpec((None, tr, cols), lambda i: (g_sel, i, 0)), blk, blk],
        out_specs=[blk] * 4, out_shape=[shp] * 4,
        compiler_params=_cp(1))(w, g_arr, m, v)


def _flip(v, bit):
    return 1 - v if bit else v


def _my_place():
    x, y, c = lax.axis_index("x"), lax.axis_index("y"), lax.axis_index("c")
    return x, y, c


class _Exchange:
    def __init__(self, operands, out_shape, aliases, sems, start, finish):
        self.operands, self.out_shape, self.aliases, self.sems = list(operands), list(out_shape), dict(aliases), list(sems)
        self.start, self.finish = start, finish


def _join(exchanges):
    exchanges = [e for e in exchanges if e is not None]
    if not exchanges:
        return None
    ops, outs, sems, aliases, spans = [], [], [], {}, []
    for e in exchanges:
        spans.append((len(ops), len(outs), len(sems), e))
        for i, j in e.aliases.items():
            aliases[len(ops) + i] = len(outs) + j
        ops += e.operands
        outs += e.out_shape
        sems += e.sems

    def run(which):
        def go(ins, res, sm):
            for io, oo, so, e in spans:
                getattr(e, which)(ins[io:io + len(e.operands)], res[oo:oo + len(e.out_shape)], sm[so:so + len(e.sems)])
        return go

    return _Exchange(ops, outs, aliases, sems, run("start"), run("finish"))


def _call(body, name, grid, in_specs, out_specs, out_shape, args, scratch=(), carry=None, io_alias=None):
    in_specs, out_specs, out_shape, scratch = list(in_specs), list(out_specs), list(out_shape), list(scratch)
    io_alias = dict(io_alias or {})
    if carry is None:
        return pl.pallas_call(body, name=name, grid=grid, in_specs=in_specs, out_specs=out_specs,
                              out_shape=out_shape, scratch_shapes=scratch, input_output_aliases=io_alias,
                              compiler_params=_cp(len(grid)))(*args)
    n_in, n_out, n_s = len(in_specs), len(out_specs), len(scratch)
    c_in, c_out = len(carry.operands), len(carry.out_shape)
    any_spec = pl.BlockSpec(memory_space=pl.ANY)

    def wrapped(*refs):
        ins, cins = refs[:n_in], refs[n_in:n_in + c_in]
        o0 = n_in + c_in
        outs, couts = refs[o0:o0 + n_out], refs[o0 + n_out:o0 + n_out + c_out]
        s0 = o0 + n_out + c_out
        scr, sems = refs[s0:s0 + n_s], refs[s0 + n_s:]
        first = pl.program_id(0) == 0
        last = pl.program_id(0) == grid[0] - 1
        for ax in range(1, len(grid)):
            first = jnp.logical_and(first, pl.program_id(ax) == 0)
            last = jnp.logical_and(last, pl.program_id(ax) == grid[ax] - 1)

        @pl.when(first)
        def _():
            carry.start(cins, couts, sems)

        body(*ins, *outs, *scr)

        @pl.when(last)
        def _():
            carry.finish(cins, couts, sems)

    return pl.pallas_call(
        wrapped, name=name, grid=grid, in_specs=in_specs + [any_spec] * c_in,
        out_specs=out_specs + [any_spec] * c_out, out_shape=out_shape + carry.out_shape,
        scratch_shapes=scratch + carry.sems,
        input_output_aliases={**io_alias, **{n_in + i: n_out + j for i, j in carry.aliases.items()}},
        compiler_params=_cp(len(grid)))(*args, *carry.operands)


def _whole_call(body, name, args, out_shape, scratch, carry=None):
    vm = pl.BlockSpec(memory_space=pltpu.VMEM)
    any_spec = pl.BlockSpec(memory_space=pl.ANY)
    out_shape, scratch = list(out_shape), list(scratch)
    n_in, n_out, n_s = len(args), len(out_shape), len(scratch)
    if carry is None:
        return pl.pallas_call(body, name=name, in_specs=[vm] * n_in, out_specs=[vm] * n_out, out_shape=out_shape,
                              scratch_shapes=scratch, compiler_params=_cp())(*args)
    c_in, c_out = len(carry.operands), len(carry.out_shape)

    def wrapped(*refs):
        ins, cins = refs[:n_in], refs[n_in:n_in + c_in]
        o0 = n_in + c_in
        outs, couts = refs[o0:o0 + n_out], refs[o0 + n_out:o0 + n_out + c_out]
        s0 = o0 + n_out + c_out
        scr, sems = refs[s0:s0 + n_s], refs[s0 + n_s:]
        carry.start(cins, couts, sems)
        body(*ins, *outs, *scr)
        carry.finish(cins, couts, sems)

    return pl.pallas_call(
        wrapped, name=name, in_specs=[vm] * n_in + [any_spec] * c_in, out_specs=[vm] * n_out + [any_spec] * c_out,
        out_shape=out_shape + carry.out_shape, scratch_shapes=scratch + carry.sems,
        input_output_aliases={n_in + i: n_out + j for i, j in carry.aliases.items()},
        compiler_params=_cp())(*args, *carry.operands)


def _alone(name, ex):
    any_spec = pl.BlockSpec(memory_space=pl.ANY)
    c_in, c_out = len(ex.operands), len(ex.out_shape)

    def body(*refs):
        ins, outs, sems = refs[:c_in], refs[c_in:c_in + c_out], refs[c_in + c_out:]
        ex.start(ins, outs, sems)
        ex.finish(ins, outs, sems)

    return pl.pallas_call(
        body, name=name, in_specs=[any_spec] * c_in, out_specs=[any_spec] * c_out, out_shape=ex.out_shape,
        scratch_shapes=ex.sems, input_output_aliases=ex.aliases, compiler_params=_cp())(*ex.operands)


def _ada_fwd(c_pad, w_ada, b_shard, carry=None):
    d = c_pad.shape[-1]
    cols = w_ada.shape[-1]
    chunk = 384

    def body(c_ref, w_ref, b_ref, call_ref, mod_ref, part, s1, r1, s2, r2):
        x, y, c = _my_place()
        dev = 4 * x + 2 * y + c
        chip = 2 * x + y
        call_ref[dev] = c_ref[...]

        def c_copy(k):
            px, py, pc = _flip(x, (k >> 2) & 1), _flip(y, (k >> 1) & 1), _flip(c, k & 1)
            return px, py, pc

        sends = []
        for k in range(1, N_DEV):
            px, py, pc = c_copy(k)
            cp = pltpu.make_async_remote_copy(src_ref=c_ref, dst_ref=call_ref.at[dev], send_sem=s1.at[k - 1],
                                              recv_sem=r1.at[k - 1], device_id=(px, py, pc), device_id_type=MESH)
            cp.start()
            sends.append(cp)
        for k in range(1, N_DEV):
            px, py, pc = c_copy(k)
            pltpu.make_async_remote_copy(src_ref=c_ref, dst_ref=call_ref.at[4 * px + 2 * py + pc],
                                         send_sem=s1.at[k - 1], recv_sem=r1.at[k - 1],
                                         device_id=(px, py, pc), device_id_type=MESH).wait_recv()
        for cp in sends:
            cp.wait_send()

        cs = call_ref[...].reshape(N_DEV * 8, d)
        sc = (cs * jax.nn.sigmoid(cs)).astype(BF)
        for n0 in range(0, cols, chunk):
            blk = _nn(sc, w_ref[:, n0:n0 + chunk].astype(BF)) + b_ref[:, n0:n0 + chunk]
            part[:, :, n0:n0 + chunk] = blk.reshape(N_DEV, 8, chunk)

        mod_ref[chip] = part[dev]
        sends = []
        for kk in range(1, N_CHIPS):
            px, py = _flip(x, (kk >> 1) & 1), _flip(y, kk & 1)
            cp = pltpu.make_async_remote_copy(src_ref=part.at[4 * px + 2 * py + c], dst_ref=mod_ref.at[chip],
                                              send_sem=s2.at[kk - 1], recv_sem=r2.at[kk - 1],
                                              device_id=(px, py, c), device_id_type=MESH)
            cp.start()
            sends.append(cp)
        for kk in range(1, N_CHIPS):
            px, py = _flip(x, (kk >> 1) & 1), _flip(y, kk & 1)
            pltpu.make_async_remote_copy(src_ref=part.at[dev], dst_ref=mod_ref.at[2 * px + py],
                                         send_sem=s2.at[kk - 1], recv_sem=r2.at[kk - 1],
                                         device_id=(px, py, c), device_id_type=MESH).wait_recv()
        for cp in sends:
            cp.wait_send()

    return _whole_call(
        body, "ada_fwd", (c_pad, w_ada, b_shard),
        [_sds((N_DEV, 8, d), F32), _sds((N_CHIPS, 8, cols), F32)],
        [pltpu.VMEM((N_DEV, 8, cols), F32),
         pltpu.SemaphoreType.DMA((N_DEV - 1,)), pltpu.SemaphoreType.DMA((N_DEV - 1,)),
         pltpu.SemaphoreType.DMA((N_CHIPS - 1,)), pltpu.SemaphoreType.DMA((N_CHIPS - 1,))], carry=carry)


def _ag_weights(bufs, kks=(1, 2, 3), relative=False):
    n, nk = len(bufs), len(kks)

    def half(b, which):
        hr = bufs[b].shape[2] // 2
        return pl.ds(pl.multiple_of(which * hr, 16), hr)

    def copies(outs, sems, b, i, kk):
        x, y, c = _my_place()
        chip = 2 * x + y
        px, py = _flip(x, (kk >> 1) & 1), _flip(y, kk & 1)
        mine, theirs = (0, kk) if relative else (chip, 2 * px + py)
        landing = kk if relative else chip
        k = nk * b + i
        send = pltpu.make_async_remote_copy(
            src_ref=outs[b].at[mine, :, half(b, c), :], dst_ref=outs[b].at[landing, :, half(b, c), :],
            send_sem=sems[0].at[k], recv_sem=sems[1].at[k], device_id=(px, py, c), device_id_type=MESH)
        got = outs[b].at[theirs, :, half(b, c), :]
        recv = pltpu.make_async_remote_copy(
            src_ref=got, dst_ref=got, send_sem=sems[0].at[k], recv_sem=sems[1].at[k],
            device_id=(px, py, c), device_id_type=MESH)
        fwd = pltpu.make_async_remote_copy(
            src_ref=got, dst_ref=got, send_sem=sems[2].at[k], recv_sem=sems[3].at[k],
            device_id=(x, y, 1 - c), device_id_type=MESH)
        other = outs[b].at[theirs, :, half(b, 1 - c), :]
        back = pltpu.make_async_remote_copy(
            src_ref=other, dst_ref=other, send_sem=sems[2].at[k], recv_sem=sems[3].at[k],
            device_id=(x, y, 1 - c), device_id_type=MESH)
        return send, recv, fwd, back

    def each(outs, sems):
        for b in range(n):
            for i, kk in enumerate(kks):
                yield copies(outs, sems, b, i, kk)

    def start(ins, outs, sems):
        for send, _, _, _ in each(outs, sems):
            send.start()

    def finish(ins, outs, sems):
        for _, recv, fwd, _ in each(outs, sems):
            recv.wait_recv()
            fwd.start()
        for send, _, fwd, back in each(outs, sems):
            back.wait_recv()
            send.wait_send()
            fwd.wait_send()

    return _Exchange(bufs, [_sds(s.shape, s.dtype) for s in bufs], {i: i for i in range(n)},
                     [pltpu.SemaphoreType.DMA((nk * n,))] * 4, start, finish)


def _rs_d2d(grads):
    n = len(grads)

    def copy(ins, outs, sems, b):
        x, y, c = _my_place()
        hr = grads[b].shape[2] // 2
        theirs = pl.ds(pl.multiple_of((1 - c) * hr, 8), hr)
        return pltpu.make_async_remote_copy(
            src_ref=ins[b].at[:, :, theirs, :], dst_ref=outs[b], send_sem=sems[0].at[b], recv_sem=sems[1].at[b],
            device_id=(x, y, 1 - c), device_id_type=MESH)

    def start(ins, outs, sems):
        for b in range(n):
            copy(ins, outs, sems, b).start()

    def finish(ins, outs, sems):
        for b in range(n):
            copy(ins, outs, sems, b).wait()

    return _Exchange(grads, [_sds(g.shape[:2] + (g.shape[2] // 2, g.shape[3]), F32) for g in grads], {},
                     [pltpu.SemaphoreType.DMA((n,))] * 2, start, finish)


def _add_halves(core, g, land):
    nchip, ng, rows, cols = g.shape
    hr = rows // 2
    tr = _row_tile(hr, cols)
    steps = hr // tr

    def body(core_ref, g_ref, l_ref, o_ref):
        del core_ref
        o_ref[...] = (g_ref[...] + l_ref[...]).astype(BF)

    return pl.pallas_call(
        body, name="add_halves",
        grid_spec=pltpu.PrefetchScalarGridSpec(
            num_scalar_prefetch=1, grid=(nchip, ng, steps),
            in_specs=[pl.BlockSpec((None, None, tr, cols), lambda j, a, i, cr: (j, a, cr[0] * steps + i, 0)),
                      pl.BlockSpec((None, None, tr, cols), lambda j, a, i, cr: (j, a, i, 0))],
            out_specs=pl.BlockSpec((None, None, tr, cols), lambda j, a, i, cr: (j, a, i, 0))),
        out_shape=_sds((nchip, ng, hr, cols), BF),
        compiler_params=_cp(3))(core, g, land)


def _rs_ici(parts, relative=False):
    n = len(parts)

    def copies(ins, outs, sems):
        x, y, c = _my_place()
        chip = 2 * x + y
        for b in range(n):
            for kk in range(1, N_CHIPS):
                px, py = _flip(x, (kk >> 1) & 1), _flip(y, kk & 1)
                k = 3 * b + kk - 1
                theirs, landing = (kk, kk) if relative else (2 * px + py, chip)
                send = pltpu.make_async_remote_copy(
                    src_ref=ins[b].at[theirs], dst_ref=outs[b].at[landing],
                    send_sem=sems[0].at[k], recv_sem=sems[1].at[k], device_id=(px, py, c), device_id_type=MESH)
                slot = outs[b].at[theirs]
                recv = pltpu.make_async_remote_copy(
                    src_ref=slot, dst_ref=slot, send_sem=sems[0].at[k], recv_sem=sems[1].at[k],
                    device_id=(px, py, c), device_id_type=MESH)
                yield send, recv

    def start(ins, outs, sems):
        for send, _ in copies(ins, outs, sems):
            send.start()

    def finish(ins, outs, sems):
        for send, recv in copies(ins, outs, sems):
            recv.wait_recv()
            send.wait_send()

    return _Exchange(parts, [_sds(p.shape, p.dtype) for p in parts], {},
                     [pltpu.SemaphoreType.DMA((3 * n,))] * 2, start, finish)


def _sum_chips(place, part, land, relative=False):
    nchip, ng, hr, cols = land.shape
    tr = _row_tile(hr, cols)
    steps = hr // tr

    def body(place_ref, p_ref, l1, l2, l3, o_ref):
        del place_ref
        o_ref[...] = ((p_ref[...].astype(F32) + l1[...].astype(F32)) + l2[...].astype(F32)) + l3[...].astype(F32)

    def slot(k):
        if relative:
            return pl.BlockSpec((None, None, tr, cols), lambda a, i, pr: (k, a, i, 0))
        return pl.BlockSpec((None, None, tr, cols), lambda a, i, pr: (jnp.bitwise_xor(pr[1], k), a, i, 0))

    return pl.pallas_call(
        body, name="sum_chips",
        grid_spec=pltpu.PrefetchScalarGridSpec(
            num_scalar_prefetch=1, grid=(ng, steps),
            in_specs=[slot(0), slot(1), slot(2), slot(3)],
            out_specs=pl.BlockSpec((None, tr, cols), lambda a, i, pr: (a, pr[0] * steps + i, 0))),
        out_shape=_sds((ng, 2 * hr, cols), F32),
        compiler_params=_cp(2))(place, part, land, land, land)


def _rs_final(bufs):
    n = len(bufs)

    def copy(outs, sems, b, which):
        x, y, c = _my_place()
        hr = bufs[b].shape[1] // 2
        rows = outs[b].at[:, pl.ds(pl.multiple_of((c if which == 0 else 1 - c) * hr, 8), hr), :]
        return pltpu.make_async_remote_copy(
            src_ref=rows, dst_ref=rows, send_sem=sems[0].at[b], recv_sem=sems[1].at[b],
            device_id=(x, y, 1 - c), device_id_type=MESH)

    def start(ins, outs, sems):
        for b in range(n):
            copy(outs, sems, b, 0).start()

    def finish(ins, outs, sems):
        for b in range(n):
            copy(outs, sems, b, 0).wait_send()
            copy(outs, sems, b, 1).wait_recv()

    return _Exchange(bufs, [_sds(h.shape, F32) for h in bufs], {i: i for i in range(n)},
                     [pltpu.SemaphoreType.DMA((n,))] * 2, start, finish)


def _small_sync(smalls, dmod_blk, c_all, carry=None):
    d = c_all.shape[-1]
    cols = dmod_blk.shape[-1]
    chunk = 384

    def body(sm_ref, dm_ref, c_ref, sum_ref, gw_ref, sm_all, dm_all, ssem, rsem):
        x, y, c = _my_place()
        dev = 4 * x + 2 * y + c
        chip = 2 * x + y
        sm_all[dev] = sm_ref[...]
        dm_all[dev] = dm_ref[chip]
        sends = []
        for k in range(1, N_DEV):
            px, py, pc = _flip(x, (k >> 2) & 1), _flip(y, (k >> 1) & 1), _flip(c, k & 1)
            a = pltpu.make_async_remote_copy(src_ref=sm_ref, dst_ref=sm_all.at[dev], send_sem=ssem.at[2 * (k - 1)],
                                             recv_sem=rsem.at[2 * (k - 1)], device_id=(px, py, pc),
                                             device_id_type=MESH)
            b = pltpu.make_async_remote_copy(src_ref=dm_ref.at[2 * px + py], dst_ref=dm_all.at[dev],
                                             send_sem=ssem.at[2 * (k - 1) + 1], recv_sem=rsem.at[2 * (k - 1) + 1],
                                             device_id=(px, py, pc), device_id_type=MESH)
            a.start()
            b.start()
            sends += [a, b]
        for k in range(1, N_DEV):
            px, py, pc = _flip(x, (k >> 2) & 1), _flip(y, (k >> 1) & 1), _flip(c, k & 1)
            pdev = 4 * px + 2 * py + pc
            pltpu.make_async_remote_copy(src_ref=sm_ref, dst_ref=sm_all.at[pdev], send_sem=ssem.at[2 * (k - 1)],
                                         recv_sem=rsem.at[2 * (k - 1)], device_id=(px, py, pc),
                                         device_id_type=MESH).wait_recv()
            pltpu.make_async_remote_copy(src_ref=dm_ref.at[chip], dst_ref=dm_all.at[pdev],
                                         send_sem=ssem.at[2 * (k - 1) + 1], recv_sem=rsem.at[2 * (k - 1) + 1],
                                         device_id=(px, py, pc), device_id_type=MESH).wait_recv()
        for cp in sends:
            cp.wait_send()

        tot = sm_all[0]
        for q in range(1, N_DEV):
            tot = tot + sm_all[q]
        sum_ref[...] = tot

        cs = c_ref[...].reshape(N_DEV * 8, d)
        sc = (cs * jax.nn.sigmoid(cs)).astype(BF)
        for n0 in range(0, cols, chunk):
            dmv = dm_all[:, :, n0:n0 + chunk].reshape(N_DEV * 8, chunk).astype(BF)
            gw_ref[:, n0:n0 + chunk] = _tn(sc, dmv)

    return _whole_call(
        body, "small_sync", (smalls, dmod_blk, c_all),
        [_sds(smalls.shape, F32), _sds((d, cols), F32)],
        [pltpu.VMEM((N_DEV,) + smalls.shape, F32), pltpu.VMEM((N_DEV, 8, cols), F32),
         pltpu.SemaphoreType.DMA((2 * (N_DEV - 1),)), pltpu.SemaphoreType.DMA((2 * (N_DEV - 1),))], carry=carry)


def _bucket_onehot():
    maps = np.stack([_bucket_map(dil).reshape(-1) for _, dil in DIL_CONFIGS])
    return (jnp.asarray(maps)[:, None, :] == jnp.arange(N_BUCKETS, dtype=jnp.int32)[None, :, None]).astype(BF)


def _dil_bias(rel_t, onehot):
    def body(r_ref, oh_ref, o_ref):
        rv = r_ref[...]
        hi = rv.astype(BF)
        lo = (rv - hi.astype(F32)).astype(BF)
        for c in range(len(DIL_CONFIGS)):
            o_ref[c] = _nn(hi, oh_ref[c]) + _nn(lo, oh_ref[c])

    return pl.pallas_call(body, name="dil_bias",
                          out_shape=_sds((len(DIL_CONFIGS), N_HEADS, BLOCK * 2 * BLOCK), F32),
                          compiler_params=_cp())(rel_t, onehot)


def _rowsum8(a):
    def body(a_ref, o_ref):
        o_ref[...] = jnp.sum(a_ref[...], axis=0, keepdims=True)

    return pl.pallas_call(body, name="rowsum8", out_shape=_sds((1, a.shape[1]), F32), compiler_params=_cp())(a)


def _local_step(x, mod, target, w, gains, rel_bias, place=None):
    nb, seq, d = x.shape
    t = nb * seq
    dist = place is not None
    core = place[0:1] if dist else None
    x0 = x.reshape(t, d)
    tgt = target.reshape(t, d)
    md = [mod[:, i:i + 1, :] for i in range(N_MOD)]
    sh1, sc1, gt1, sh2, sc2, gt2, sh3, sc3, gt3 = md
    g1, g2, g3 = gains["g_ffn1"], gains["g_mix"], gains["g_ffn2"]
    ones_g = _group_ones()

    def partial_sums(grads, lands):
        return [_add_halves(core, g, l) for g, l in zip(grads, lands)]

    def chip_sums(parts, lands):
        return [_sum_chips(place, p, l, relative=True) for p, l in zip(parts, lands)]

    gu1 = w["gu1"]
    res = _ffn_up(x0, g1, sc1, sh1, gu1, seq,
                  carry=_join([_ag_weights([w["d1"]], relative=True), _ag_weights([w["win"]])]) if dist else None)
    h1, a1, u1, s1 = res[:4]
    wd1, w_in = res[4:] if dist else (w["d1"], w["win"])
    f1, x1 = _ffn_down(s1, wd1, x0, gt1, seq, 0.5)

    res = _qkv_proj(x1, g2, sc2, sh2, w_in, seq, carry=_ag_weights([w["wout"]]) if dist else None)
    h2, qkv6, qkv_r4, qkv_r16 = res[:4]
    w_out2 = (res[4] if dist else w["wout"]).reshape(2 * D_GRP, d)
    qkv6b = qkv6.reshape(6, nb, seq, D_GRP)
    res = _sb_fwd(qkv6b, gains["g_sb_out"], nb, seq,
                  carry=_ag_weights([w["gu2"], w["d2"]], relative=True) if dist else None)
    o_sb, on_sb = res[:2]
    wgu2, wd2 = res[2:] if dist else (w["gu2"], w["d2"])
    onehot = _bucket_onehot()
    bias = _dil_bias(rel_bias.T, onehot).reshape(len(DIL_CONFIGS), N_HEADS * BLOCK, 2 * BLOCK)
    o_cs, l_cs = [], []
    qkv_rs = [(qkv6b, 3), (qkv_r4, 0), (qkv_r16, 0)]
    for ci, (_, dil) in enumerate(DIL_CONFIGS):
        sub = seq // dil
        arr, base = qkv_rs[ci]
        arr = arr.reshape(base + 3, nb, sub, dil * D_GRP)
        qkv_rs[ci] = (arr, base)
        o_c, l_c = _dil_fwd(arr, base, bias[ci], nb, sub, dil)
        o_cs.append(o_c.reshape(t // dil, dil * D_GRP))
        l_cs.append(l_c.reshape(t // dil, dil * D_GRP))
    o_dil, on_dil = _dil_comb(o_cs, l_cs, gains["g_dil_out"])
    tmix, x2 = _mix_out(on_sb.reshape(t, D_GRP), on_dil, w_out2, x1, gt2, seq)

    h3, a3, u3, s3 = _ffn_up(x2, g3, sc3, sh3, wgu2, seq)
    f3, dx3, dg_final, loss = _ffn_down_loss(s3, wd2, x2, gt3, seq, 0.5, gains["g_final"], tgt)

    da3, du3, df3, dgt3, dx2, dsh3, dsc3, dg3 = _ffn_bwd_x(dx3, gt3, f3, wd2, a3, u3, wgu2, x2, g3, sc3, seq, 0.5)
    grads2 = _ffn_bwd_w(h3, da3, du3, s3, df3)

    res = _mix_bwd_out(
        dx2, gt2, tmix, w_out2, o_sb.reshape(t, D_GRP), o_dil, on_sb.reshape(t, D_GRP), on_dil,
        gains["g_sb_out"], gains["g_dil_out"], ones_g, seq, carry=_rs_d2d(grads2) if dist else None)
    do_sb, do_dil, dgt2, dg_sb, dg_dil, dw_out = res[:6]
    parts2 = partial_sums(grads2, res[6:]) if dist else None
    dw_out = dw_out.reshape(N_CHIPS, 1, 2 * D_GRP // N_CHIPS, d)
    res = _sb_bwd(qkv6b, do_sb.reshape(nb, seq, D_GRP), nb, seq,
                  carry=_rs_ici(parts2, relative=True) if dist else None)
    dqkv6 = res[0]
    halves2 = chip_sums(parts2, res[1:]) if dist else None
    dcs = _dil_comb_bwd(do_dil, o_cs, l_cs)
    dsum, a_tiles = [], []
    for ci, (_, dil) in enumerate(DIL_CONFIGS):
        sub = seq // dil
        do_c = dcs[ci].reshape(nb, sub, dil * D_GRP)
        dd_c = dcs[3 + ci].reshape(nb, sub, dil * D_GRP)
        res = _dil_bwd(qkv_rs[ci][0], qkv_rs[ci][1], bias[ci], do_c, dd_c, nb, sub, dil)
        dsum.append(res[0].reshape(3, t // dil, dil * D_GRP))
        a_tiles.append(res[1].reshape(N_HEADS, BLOCK * 2 * BLOCK))
    dqkv6 = _dqkv_dil_sum(dsum, dqkv6.reshape(6, t, D_GRP))
    drel = _relbias_grad(jnp.stack(a_tiles), onehot)
    dx1, dsh2, dsc2, dg2 = _mix_bwd_dh(dqkv6, w_in, x1, g2, sc2, dx2, seq)

    da1, du1, df1, dgt1 = _ffn_bwd_ds(dx1, gt1, f1, wd1, a1, u1, seq, 0.5)
    g_gu = _ffn_bwd_w(h1, da1, du1, s1, df1, terms=(0, 1))
    res = _ffn_bwd_w(h1, da1, du1, s1, df1, terms=(2,), carry=_rs_d2d(g_gu) if dist else None)
    grads1 = g_gu + res[:1]
    parts_gu = partial_sums(g_gu, res[1:2]) if dist else None
    res = _dw_in(h2, dqkv6, carry=_join([_rs_ici(parts_gu, relative=True), _rs_d2d(grads1[1:]), _rs_final(halves2)])
                 if dist else None)
    grads_m = [res[0], dw_out]
    if dist:
        sums_gu = chip_sums(parts_gu, res[1:2])
        parts_dn = partial_sums(grads1[1:], res[2:3])
        grads2 = res[3:4]
    res = _ffn_bwd_dh(da1, du1, gu1, x0, g1, sc1, dx1, seq,
                      carry=_join([_rs_ici(parts_dn, relative=True), _rs_d2d(grads_m)]) if dist else None)
    dx0, dsh1, dsc1, dg1 = res[:4]
    pending = None
    if dist:
        pending = (sums_gu + chip_sums(parts_dn, res[4:5]), partial_sums(grads_m, res[5:7]))

    dmod = jnp.concatenate([dsh1, dsc1, dgt1, dsh2, dsc2, dgt2, dsh3, dsc3, dgt3], axis=1)
    return dict(grad_x=dx0.reshape(nb, seq, d), loss=loss[0, 0], dmod=dmod.reshape(nb, N_MOD * d),
                dffn1=grads1, dffn2=grads2[0], dwin=grads_m[0], dwout=grads_m[1], pending=pending,
                dg_ffn1=dg1, dg_mix=dg2, dg_ffn2=dg3, dg_final=dg_final, dg_sb=dg_sb, dg_dil=dg_dil,
                drel=drel.T)


_SMALL_ORDER = (("b_ada", N_MOD * 1024), ("g_ffn1", 1024), ("g_mix", 1024), ("g_ffn2", 1024), ("g_final", 1024),
                ("g_sb_out", D_GRP), ("g_dil_out", D_GRP), ("rel_bias", N_BUCKETS * N_HEADS))


def _pack_small(parts, extra=None):
    flat = [parts[name].reshape(-1).astype(F32) for name, _ in _SMALL_ORDER]
    used = sum(sz for _, sz in _SMALL_ORDER)
    pad = SMALL_ROWS * 128 - used
    tail = jnp.zeros((pad,), F32)
    if extra is not None:
        tail = tail.at[0].set(extra)
    return jnp.concatenate(flat + [tail]).reshape(SMALL_ROWS, 128)


def _unpack_small(packed, shapes):
    flat = packed.reshape(-1)
    out, off = {}, 0
    for name, sz in _SMALL_ORDER:
        out[name] = flat[off:off + sz].reshape(shapes[name])
        off += sz
    return out, flat[off]


def kernel(x, c, w_ada, b_ada, g_ffn1, w1_gate, w1_up, w1_down, g_mix, w_in, g_sb_out, g_dil_out, w_out, rel_bias, g_ffn2, w2_gate, w2_up, w2_down, g_final, loss_target, m_w_ada, m_b_ada, m_g_ffn1, m_w1_gate, m_w1_up, m_w1_down, m_g_mix, m_w_in, m_g_sb_out, m_g_dil_out, m_w_out, m_rel_bias, m_g_ffn2, m_w2_gate, m_w2_up, m_w2_down, m_g_final, v_w_ada, v_b_ada, v_g_ffn1, v_w1_gate, v_w1_up, v_w1_down, v_g_mix, v_w_in, v_g_sb_out, v_g_dil_out, v_w_out, v_rel_bias, v_g_ffn2, v_w2_gate, v_w2_up, v_w2_down, v_g_final):
    nb, seq, d = x.shape
    xi, yi, ci = lax.axis_index("x"), lax.axis_index("y"), lax.axis_index("c")
    chip = 2 * xi + yi
    ada_cols = w_ada.shape[-1]

    c_pad = jnp.zeros((8, d), F32).at[:nb].set(c)
    b_shard = lax.dynamic_slice(b_ada, (0, chip * ada_cols), (1, ada_cols))
    shards = dict(gu1=jnp.stack([w1_gate[0], w1_up[0]]), d1=w1_down, win=w_in, wout=w_out,
                  gu2=jnp.stack([w2_gate[0], w2_up[0]]), d2=w2_down)
    bufs = {k: lax.dynamic_update_slice(lax.empty((N_CHIPS,) + s.shape, BF), s.astype(BF)[None],
                                        (chip if k in ("win", "wout") else 0, 0, 0, 0))
            for k, s in shards.items()}
    c_all, mod_blk, bufs["gu1"] = _ada_fwd(c_pad, w_ada[0], b_shard,
                                           carry=_ag_weights([bufs["gu1"]], relative=True))
    mod = jnp.transpose(mod_blk[:, :nb, :], (1, 0, 2)).reshape(nb, N_MOD, d)

    gains = dict(g_ffn1=g_ffn1, g_mix=g_mix, g_ffn2=g_ffn2, g_final=g_final.reshape(1, d),
                 g_sb_out=g_sb_out.reshape(1, D_GRP), g_dil_out=g_dil_out.reshape(1, D_GRP))
    place = jnp.stack([ci, chip]).astype(jnp.int32)
    r = _local_step(x, mod, loss_target, bufs, gains, rel_bias, place)

    dmod = r["dmod"]
    dmod_pad = jnp.zeros((8, N_MOD * d), F32).at[:nb].set(dmod)
    dmod_blk = jnp.transpose(dmod_pad.reshape(8, N_CHIPS, ada_cols), (1, 0, 2))
    small_parts = dict(b_ada=_rowsum8(dmod_pad), g_ffn1=r["dg_ffn1"], g_mix=r["dg_mix"], g_ffn2=r["dg_ffn2"],
                       g_final=r["dg_final"], g_sb_out=r["dg_sb"], g_dil_out=r["dg_dil"], rel_bias=r["drel"])
    halves1, parts_m = r["pending"]
    res = _small_sync(_pack_small(small_parts, r["loss"]), dmod_blk, c_all,
                      carry=_join([_rs_final(halves1), _rs_ici(parts_m)]))
    small_sum, g_wada, gffn1_gu, gffn1_dn = res[:4]
    halves_m = [_sum_chips(place, p, l) for p, l in zip(parts_m, res[4:6])]
    gwin, gwout = _alone("rs_last", _rs_final(halves_m))
    gffn2 = r["dffn2"]

    small_w = dict(b_ada=b_ada, g_ffn1=g_ffn1, g_mix=g_mix, g_ffn2=g_ffn2, g_final=g_final,
                   g_sb_out=g_sb_out, g_dil_out=g_dil_out, rel_bias=rel_bias)
    small_m = dict(b_ada=m_b_ada, g_ffn1=m_g_ffn1, g_mix=m_g_mix, g_ffn2=m_g_ffn2, g_final=m_g_final,
                   g_sb_out=m_g_sb_out, g_dil_out=m_g_dil_out, rel_bias=m_rel_bias)
    small_v = dict(b_ada=v_b_ada, g_ffn1=v_g_ffn1, g_mix=v_g_mix, g_ffn2=v_g_ffn2, g_final=v_g_final,
                   g_sb_out=v_g_sb_out, g_dil_out=v_g_dil_out, rel_bias=v_rel_bias)
    shapes = {k: v.shape for k, v in small_w.items()}
    sg, sd, sm, sv = _adamw(_pack_small(small_w), small_sum.reshape(1, SMALL_ROWS, 128), 0,
                            _pack_small(small_m), _pack_small(small_v))
    sg, loss = _unpack_small(sg, shapes)
    sd, _ = _unpack_small(sd, shapes)
    sm, _ = _unpack_small(sm, shapes)
    sv, _ = _unpack_small(sv, shapes)

    big = {}

    def upd(name, w, g_arr, sel, m, v, transposed=False):
        swap = (lambda a: jnp.swapaxes(a, -1, -2)) if transposed else (lambda a: a)
        w2, m2, v2 = [swap(a)[0] for a in (w, m, v)]
        big[name] = [swap(a[None]) for a in _adamw(w2, g_arr, sel, m2, v2)]

    upd("w_ada", w_ada, g_wada.reshape(1, d, ada_cols), 0, m_w_ada, v_w_ada)
    upd("w1_gate", w1_gate, gffn1_gu, 0, m_w1_gate, v_w1_gate, transposed=True)
    upd("w1_up", w1_up, gffn1_gu, 1, m_w1_up, v_w1_up, transposed=True)
    upd("w1_down", w1_down, gffn1_dn, 0, m_w1_down, v_w1_down)
    upd("w_in", w_in, gwin, 0, m_w_in, v_w_in)
    upd("w_out", w_out, gwout, 0, m_w_out, v_w_out)
    upd("w2_gate", w2_gate, gffn2, 0, m_w2_gate, v_w2_gate, transposed=True)
    upd("w2_up", w2_up, gffn2, 1, m_w2_up, v_w2_up, transposed=True)
    upd("w2_down", w2_down, gffn2, 2, m_w2_down, v_w2_down)

    names = ["w_ada", "b_ada", "g_ffn1", "w1_gate", "w1_up", "w1_down", "g_mix", "w_in", "g_sb_out", "g_dil_out",
             "w_out", "rel_bias", "g_ffn2", "w2_gate", "w2_up", "w2_down", "g_final"]
    outs = [loss, r["grad_x"]]
    for k, small in enumerate((sg, sd, sm, sv)):
        for name in names:
            outs.append(big[name][k] if name in big else small[name])
    return tuple(outs)
```

```python
import math

import numpy as np
import jax
import jax.numpy as jnp
from jax import lax
from jax.experimental import pallas as pl
from jax.experimental.pallas import tpu as pltpu

F32 = jnp.float32
BF = jnp.bfloat16
MESH = pl.DeviceIdType.MESH

HEAD_DIM = 64
N_HEADS = 8
D_GRP = N_HEADS * HEAD_DIM
DIL_CONFIGS = ((128, 1), (512, 4), (2048, 16))
N_STEPS = 128
BLOCK = 128
N_BUCKETS = 32
MAX_DISTANCE = 2048
N_MOD = 9
EPS = 1e-6
NEG_INF = -1e30
SCALE = HEAD_DIM ** -0.5

ADAM_LR = 0.001
ADAM_B1 = 0.9
ADAM_B2 = 0.999
ADAM_EPS = 1e-08
ADAM_WD = 0.01
ADAM_STEP = 10

N_CHIPS = 4
N_DEV = 8
VMEM_LIMIT = 56 * 1024 * 1024
TM = 512
TQ = 256
KB = 256
SMALL_ROWS = 120


def _cp(n_axes=0, **kw):
    sem = ("arbitrary",) * n_axes if n_axes else None
    return pltpu.CompilerParams(dimension_semantics=sem, vmem_limit_bytes=VMEM_LIMIT, **kw)


def _nn(a, b):
    return jnp.dot(a, b, preferred_element_type=F32)


def _nt(a, b):
    return lax.dot_general(a, b, (((1,), (1,)), ((), ())), preferred_element_type=F32)


def _tn(a, b):
    return lax.dot_general(a, b, (((0,), (0,)), ((), ())), preferred_element_type=F32)


def _twice(m):
    return jnp.concatenate([m, m], axis=0)


def _nn2(x, m2):
    hi = x.astype(BF)
    lo = (x - hi.astype(F32)).astype(BF)
    return _nn(jnp.concatenate([hi, lo], axis=1), m2)


def _softplus(z):
    return jnp.maximum(z, 0.0) + jnp.log(1.0 + jnp.exp(-jnp.abs(z)))


def _sds(shape, dtype):
    return jax.ShapeDtypeStruct(shape, dtype)


def _whole(a):
    nd = a.ndim
    return pl.BlockSpec(a.shape, lambda *_: (0,) * nd, pipeline_mode=pl.Buffered(1))


def _modnorm_bwd_tile(dh, xv, gv, scv, dxo):
    r = lax.rsqrt(jnp.mean(xv * xv, axis=-1, keepdims=True) + EPS)
    n = xv * r
    ng = n * gv
    dsh = jnp.sum(dh, axis=0, keepdims=True)
    dsc = jnp.sum(dh * ng, axis=0, keepdims=True)
    dy = dh * (1.0 + scv)
    dg = jnp.sum(dy * n, axis=0, keepdims=True)
    dn = dy * gv
    dx = dxo + r * (dn - n * jnp.mean(dn * n, axis=-1, keepdims=True))
    return dx, dsh, dsc, dg


def _acc_rows(ref, val, first):
    @pl.when(first)
    def _():
        ref[...] = val

    @pl.when(jnp.logical_not(first))
    def _():
        ref[...] += val


def _modnorm_tile(x_ref, g_ref, sc_ref, sh_ref):
    xv = x_ref[...]
    r = lax.rsqrt(jnp.mean(xv * xv, axis=-1, keepdims=True) + EPS)
    return (((xv * r) * g_ref[...]) * (1.0 + sc_ref[...]) + sh_ref[...]).astype(BF)


def _ffn_up(x, g, sc, sh, wgu, seq, carry=None):
    t, d = x.shape
    fs = wgu.shape[-1]
    per = seq // TM

    def body(x_ref, g_ref, sc_ref, sh_ref, w_ref, h_ref, p_ref, q_ref, s_ref):
        hv = _modnorm_tile(x_ref, g_ref, sc_ref, sh_ref)
        h_ref[...] = hv
        for j in range(N_CHIPS):
            a = _nn(hv, w_ref[j, 0])
            u = _nn(hv, w_ref[j, 1])
            sig = jax.nn.sigmoid(a)
            q = a * sig
            p_ref[j] = (u * (sig * (1.0 + a * (1.0 - sig)))).astype(BF)
            q_ref[j] = q.astype(BF)
            s_ref[j] = (q * u).astype(BF)

    row = pl.BlockSpec((TM, d), lambda m: (m, 0))
    ex = pl.BlockSpec((None, 1, d), lambda m: (m // per, 0, 0))
    blk = pl.BlockSpec((N_CHIPS, TM, fs), lambda m: (0, m, 0))
    return _call(
        body, "ffn_up", (t // TM,),
        [row, pl.BlockSpec((1, d), lambda m: (0, 0)), ex, ex, _whole(wgu)],
        [row, blk, blk, blk],
        [_sds((t, d), BF)] + [_sds((N_CHIPS, t, fs), BF)] * 3,
        (x, g, sc, sh, wgu), carry=carry)


def _ffn_down(s, wd, x, gt, seq, coef, carry=None):
    _, t, fs = s.shape
    d = x.shape[-1]
    per = seq // TM

    def body(s_ref, w_ref, x_ref, gt_ref, f_ref, xo_ref):
        f = _nn(s_ref[0], w_ref[0, 0])
        for j in range(1, N_CHIPS):
            f = f + _nn(s_ref[j], w_ref[j, 0])
        f_ref[...] = f.astype(BF)
        xo_ref[...] = x_ref[...] + (coef * gt_ref[...]) * f

    row = pl.BlockSpec((TM, d), lambda m: (m, 0))
    return _call(
        body, "ffn_down", (t // TM,),
        [pl.BlockSpec((N_CHIPS, TM, fs), lambda m: (0, m, 0)), _whole(wd), row,
         pl.BlockSpec((None, 1, d), lambda m: (m // per, 0, 0))],
        [row, row], [_sds((t, d), BF), _sds((t, d), F32)], (s, wd, x, gt), carry=carry)


def _ffn_bwd_ds(dxo, gt, f, wd, p, q, seq, coef, carry=None):
    t, d = dxo.shape
    fs = p.shape[-1]
    per = seq // TM
    nb = t // seq

    def body(dxo_ref, gt_ref, f_ref, w_ref, p_ref, q_ref, da_ref, du_ref, df_ref, dgt_ref):
        m = pl.program_id(0)
        dxv = dxo_ref[...]
        df = ((coef * gt_ref[...]) * dxv).astype(BF)
        df_ref[...] = df
        _acc_rows(dgt_ref, coef * jnp.sum(dxv * f_ref[...].astype(F32), axis=0, keepdims=True), m % per == 0)
        for j in range(N_CHIPS):
            ds = _nt(df, w_ref[j, 0])
            da_ref[j] = (ds * p_ref[j].astype(F32)).astype(BF)
            du_ref[j] = (ds * q_ref[j].astype(F32)).astype(BF)

    row = pl.BlockSpec((TM, d), lambda m: (m, 0))
    blk = pl.BlockSpec((N_CHIPS, TM, fs), lambda m: (0, m, 0))
    ex = pl.BlockSpec((None, 1, d), lambda m: (m // per, 0, 0))
    return _call(
        body, "ffn_bwd_ds", (t // TM,),
        [row, ex, row, _whole(wd), blk, blk],
        [blk, blk, row, ex],
        [_sds((N_CHIPS, t, fs), BF), _sds((N_CHIPS, t, fs), BF), _sds((t, d), BF), _sds((nb, 1, d), F32)],
        (dxo, gt, f, wd, p, q), carry=carry)


TM_X = 256


def _ffn_bwd_x(dxo, gt, f, wd, p, q, wgu, x, g, sc, seq, coef):
    t, d = dxo.shape
    fs = p.shape[-1]
    per = seq // TM_X
    nb = t // seq

    def body(dxo_ref, gt_ref, f_ref, wd_ref, p_ref, q_ref, w_ref, x_ref, g_ref, sc_ref,
             da_ref, du_ref, df_ref, dgt_ref, dx_ref, dsh_ref, dsc_ref, dg_ref):
        m = pl.program_id(0)
        dxv = dxo_ref[...]
        df = ((coef * gt_ref[...]) * dxv).astype(BF)
        df_ref[...] = df
        _acc_rows(dgt_ref, coef * jnp.sum(dxv * f_ref[...].astype(F32), axis=0, keepdims=True), m % per == 0)
        dh = None
        for j in range(N_CHIPS):
            ds = _nt(df, wd_ref[j, 0])
            da = (ds * p_ref[j].astype(F32)).astype(BF)
            du = (ds * q_ref[j].astype(F32)).astype(BF)
            da_ref[j] = da
            du_ref[j] = du
            part = _nt(da, w_ref[j, 0]) + _nt(du, w_ref[j, 1])
            dh = part if dh is None else dh + part
        dx, dsh, dsc, dg = _modnorm_bwd_tile(dh, x_ref[...], g_ref[...], sc_ref[...], dxv)
        dx_ref[...] = dx
        _acc_rows(dsh_ref, dsh, m % per == 0)
        _acc_rows(dsc_ref, dsc, m % per == 0)
        _acc_rows(dg_ref, dg, m == 0)

    row = pl.BlockSpec((TM_X, d), lambda m: (m, 0))
    blk = pl.BlockSpec((N_CHIPS, TM_X, fs), lambda m: (0, m, 0))
    ex = pl.BlockSpec((None, 1, d), lambda m: (m // per, 0, 0))
    vec = pl.BlockSpec((1, d), lambda m: (0, 0))
    exs = _sds((nb, 1, d), F32)
    return pl.pallas_call(
        body, name="ffn_bwd_x", grid=(t // TM_X,),
        in_specs=[row, ex, row, _whole(wd), blk, blk, _whole(wgu), row, vec, ex],
        out_specs=[blk, blk, row, ex, row, ex, ex, vec],
        out_shape=[_sds((N_CHIPS, t, fs), BF), _sds((N_CHIPS, t, fs), BF), _sds((t, d), BF), exs,
                   _sds((t, d), F32), exs, exs, _sds((1, d), F32)],
        compiler_params=_cp(1))(dxo, gt, f, wd, p, q, wgu, x, g, sc)


TK_W = 1024


def _ffn_bwd_w(h, da, du, s, df, terms=(0, 1, 2), carry=None):
    t, d = h.shape
    fs = da.shape[-1]
    row = pl.BlockSpec((TK_W, d), lambda j, kt: (kt, 0))
    blk = pl.BlockSpec((None, TK_W, fs), lambda j, kt: (j, kt, 0))
    args, specs, idx = [], [], []

    def operand(a, spec):
        for i, o in enumerate(args):
            if o is a:
                return i
        args.append(a)
        specs.append(spec)
        return len(args) - 1

    for term in terms:
        lhs, rhs = ((da, h), (du, h), (s, df))[term]
        idx.append((operand(lhs, blk), operand(rhs, row)))

    def body(*refs):
        o_ref = refs[-1]
        kt = pl.program_id(1)
        parts = [_tn(refs[a][...], refs[b][...]) for a, b in idx]

        @pl.when(kt == 0)
        def _():
            for i, p in enumerate(parts):
                o_ref[i] = p

        @pl.when(kt != 0)
        def _():
            for i, p in enumerate(parts):
                o_ref[i] += p

    return _call(
        body, "ffn_bwd_w", (N_CHIPS, t // TK_W), specs,
        [pl.BlockSpec((None, len(terms), fs, d), lambda j, kt: (j, 0, 0, 0))],
        [_sds((N_CHIPS, len(terms), fs, d), F32)], args, carry=carry)


def _ffn_bwd_dh(da, du, wgu, x, g, sc, dxo, seq, carry=None):
    _, t, fs = da.shape
    d = x.shape[-1]
    per = seq // TM
    nb = t // seq

    def body(da_ref, du_ref, w_ref, x_ref, g_ref, sc_ref, dxo_ref, dx_ref, dsh_ref, dsc_ref, dg_ref):
        m = pl.program_id(0)
        dh = _nt(da_ref[0], w_ref[0, 0]) + _nt(du_ref[0], w_ref[0, 1])
        for j in range(1, N_CHIPS):
            dh = dh + _nt(da_ref[j], w_ref[j, 0]) + _nt(du_ref[j], w_ref[j, 1])
        dx, dsh, dsc, dg = _modnorm_bwd_tile(dh, x_ref[...], g_ref[...], sc_ref[...], dxo_ref[...])
        dx_ref[...] = dx
        _acc_rows(dsh_ref, dsh, m % per == 0)
        _acc_rows(dsc_ref, dsc, m % per == 0)
        _acc_rows(dg_ref, dg, m == 0)

    row = pl.BlockSpec((TM, d), lambda m: (m, 0))
    blk = pl.BlockSpec((N_CHIPS, TM, fs), lambda m: (0, m, 0))
    ex = pl.BlockSpec((None, 1, d), lambda m: (m // per, 0, 0))
    vec = pl.BlockSpec((1, d), lambda m: (0, 0))
    return _call(
        body, "ffn_bwd_dh", (t // TM,),
        [blk, blk, _whole(wgu), row, vec, ex, row],
        [row, ex, ex, vec],
        [_sds((t, d), F32), _sds((nb, 1, d), F32), _sds((nb, 1, d), F32), _sds((1, d), F32)],
        (da, du, wgu, x, g, sc, dxo), carry=carry)


def _qkv_proj(x, g, sc, sh, w_in, seq, carry=None):
    t, d = x.shape
    wc = w_in.shape[-1]
    per = seq // TM

    dils = [dil for _, dil in DIL_CONFIGS if dil > 1]

    def body(x_ref, g_ref, sc_ref, sh_ref, w_ref, h_ref, o_ref, *rest):
        res_refs, buf = rest[:len(dils)], rest[len(dils)]
        hv = _modnorm_tile(x_ref, g_ref, sc_ref, sh_ref)
        h_ref[...] = hv
        for j in range(N_CHIPS):
            rf = _nn(hv, w_ref[j, 0])
            r = rf.astype(BF)
            for a, lc, off, width in _col_pieces(j, wc):
                o_ref[a, :, lc:lc + width] = r[:, off:off + width]
                if a < 3:
                    continue
                for c0 in range(0, width, 128):
                    cg = (lc + c0) // 128
                    buf[...] = rf[:, off + c0:off + c0 + 128]
                    for ref, dil in zip(res_refs, dils):
                        for rr in range(dil):
                            ref[a - 3, :, rr * D_GRP + cg * 128:rr * D_GRP + (cg + 1) * 128] = (
                                buf[pl.ds(rr, TM // dil, stride=dil), :].astype(BF))

    row = pl.BlockSpec((TM, d), lambda m: (m, 0))
    ex = pl.BlockSpec((None, 1, d), lambda m: (m // per, 0, 0))
    return _call(
        body, "qkv_proj", (t // TM,),
        [row, pl.BlockSpec((1, d), lambda m: (0, 0)), ex, ex, _whole(w_in)],
        [row, pl.BlockSpec((6, TM, D_GRP), lambda m: (0, m, 0))]
        + [pl.BlockSpec((3, TM // dil, dil * D_GRP), lambda m: (0, m, 0)) for dil in dils],
        [_sds((t, d), BF), _sds((6, t, D_GRP), BF)] + [_sds((3, t // dil, dil * D_GRP), BF) for dil in dils],
        (x, g, sc, sh, w_in), scratch=[pltpu.VMEM((TM, 128), F32)], carry=carry)


def _col_pieces(j, wc):
    out, off = [], 0
    while off < wc:
        a, lc = divmod(j * wc + off, D_GRP)
        width = min(D_GRP - lc, wc - off)
        out.append((a, lc, off, width))
        off += width
    return out


def _chip_cols(g6_ref, j, wc):
    return jnp.concatenate([g6_ref[a, :, lc:lc + width] for a, lc, _, width in _col_pieces(j, wc)], axis=1)


def _mix_out(on_sb, on_dil, w_out, x, gt, seq):
    t, d = x.shape
    per = seq // TM

    def body(a_ref, b_ref, w_ref, x_ref, gt_ref, t_ref, xo_ref):
        tv = _nn(a_ref[...], w_ref[0:D_GRP, :]) + _nn(b_ref[...], w_ref[D_GRP:2 * D_GRP, :])
        t_ref[...] = tv.astype(BF)
        xo_ref[...] = x_ref[...] + gt_ref[...] * tv

    row = pl.BlockSpec((TM, d), lambda m: (m, 0))
    half = pl.BlockSpec((TM, D_GRP), lambda m: (m, 0))
    return pl.pallas_call(
        body, name="mix_out", grid=(t // TM,),
        in_specs=[half, half, pl.BlockSpec((2 * D_GRP, d), lambda m: (0, 0)), row,
                  pl.BlockSpec((None, 1, d), lambda m: (m // per, 0, 0))],
        out_specs=[row, row],
        out_shape=[_sds((t, d), BF), _sds((t, d), F32)],
        compiler_params=_cp(1))(on_sb, on_dil, w_out, x, gt)


def _sb_masks():
    lane = lax.broadcasted_iota(jnp.int32, (1, 2 * HEAD_DIM), 1)
    hm0 = lane < HEAD_DIM
    rel = lax.broadcasted_iota(jnp.int32, (TQ, KB), 0) - lax.broadcasted_iota(jnp.int32, (TQ, KB), 1)
    kr = lax.broadcasted_iota(jnp.int32, (KB, KB), 0)
    kc = lax.broadcasted_iota(jnp.int32, (KB, KB), 1)
    return hm0, rel, kr, kc


def _headnorm_pair(o, gv, hm0):
    o2 = o * o
    ms0 = jnp.sum(jnp.where(hm0, o2, 0.0), axis=-1, keepdims=True) * (1.0 / HEAD_DIM)
    ms1 = jnp.sum(jnp.where(hm0, 0.0, o2), axis=-1, keepdims=True) * (1.0 / HEAD_DIM)
    r = jnp.where(hm0, lax.rsqrt(ms0 + EPS), lax.rsqrt(ms1 + EPS))
    return (o * r) * gv


SB_DEAD = -104.0


def _alive(c_l):
    return (jnp.max(c_l) > SB_DEAD).astype(jnp.int32)


def _sb_fwd(qkv6, g_sb, nb, seq, carry=None):
    nq = seq // TQ

    def body(q_ref, k_ref, v_ref, g_ref, o_ref, on_ref):
        qi = pl.program_id(2)
        hm0, rel, kr, kc = _sb_masks()
        upper = _twice((kr > kc).astype(BF))
        heads = _dil_masks()[0]
        qs = _stack_heads(q_ref[...] * SCALE, heads)
        causal2 = jnp.concatenate([rel] * GRP_HEADS, axis=0) > 0

        def block(kj, causal, c_l, acc):
            ks = pl.multiple_of(kj * KB, KB)
            z = _nt(qs, k_ref[pl.ds(ks, KB), :])
            sp = _softplus(z)
            spm = sp if causal is None else jnp.where(causal, sp, 0.0)
            suf = _nn2(spm, upper)
            w = jnp.exp((z - sp) + (c_l - suf))
            if causal is not None:
                w = jnp.where(causal, w, 0.0)
            return c_l - (suf[:, 0:1] + spm[:, 0:1]), acc + _nn(w.astype(BF), v_ref[pl.ds(ks, KB), :])

        c_l, acc = block(qi, causal2, jnp.zeros((GRP_HEADS * TQ, 1), F32), jnp.zeros((GRP_HEADS * TQ, GRP_W), F32))

        def cond(carry):
            return jnp.logical_and(carry[0] <= qi, carry[1] > 0)

        def kbody(carry):
            it, _, c_l, acc = carry
            c_l, acc = block(qi - it, None, c_l, acc)
            return it + 1, _alive(c_l), c_l, acc

        acc = lax.while_loop(cond, kbody, (jnp.int32(1), _alive(c_l), c_l, acc))[3]
        o = _unstack_heads(acc, heads, TQ)
        o_ref[...] = o.astype(BF)
        gv = g_ref[...]
        for half in range(GRP_W // 128):
            lanes = slice(half * 128, (half + 1) * 128)
            on_ref[:, lanes] = _headnorm_pair(o[:, lanes], gv[:, lanes], hm0).astype(BF)

    w = GRP_W
    full = lambda i: pl.BlockSpec((None, None, seq, w), lambda b, hp, q: (i, b, 0, hp))
    qblk = pl.BlockSpec((None, None, TQ, w), lambda b, hp, q: (0, b, q, hp))
    oblk = pl.BlockSpec((None, TQ, w), lambda b, hp, q: (b, q, hp))
    return _call(
        body, "sb_fwd", (nb, N_HEADS // GRP_HEADS, nq),
        [qblk, full(1), full(2), pl.BlockSpec((1, w), lambda b, hp, q: (0, hp))],
        [oblk, oblk],
        [_sds((nb, seq, D_GRP), BF), _sds((nb, seq, D_GRP), BF)],
        (qkv6, qkv6, qkv6, g_sb), carry=carry)


def _sb_bwd(qkv6, do, nb, seq, carry=None):
    nq = seq // TQ
    nk = seq // KB

    def body(q_ref, k_ref, v_ref, do_ref, out_ref, dk_acc, dv_acc, g_st, s_st):
        qi = pl.program_id(2)
        hm0, rel, kr, kc = _sb_masks()
        upper = _twice((kr > kc).astype(BF))
        lower = (kr < kc).astype(BF)

        @pl.when(qi == 0)
        def _():
            dk_acc[...] = jnp.zeros_like(dk_acc)
            dv_acc[...] = jnp.zeros_like(dv_acc)

        heads = _dil_masks()[0]
        qs = _stack_heads(q_ref[...] * SCALE, heads)
        dos = _stack_heads(do_ref[...], heads)
        causal2 = jnp.concatenate([rel] * GRP_HEADS, axis=0) > 0

        def weights(kj, causal, c_l):
            ks = pl.multiple_of(kj * KB, KB)
            vb = v_ref[pl.ds(ks, KB), :]
            z = _nt(qs, k_ref[pl.ds(ks, KB), :])
            sp = _softplus(z)
            spm = sp if causal is None else jnp.where(causal, sp, 0.0)
            suf = _nn2(spm, upper)
            lsz = z - sp
            w = jnp.exp(lsz + (c_l - suf))
            if causal is not None:
                w = jnp.where(causal, w, 0.0)
            g_st[kj] = w * _nt(dos, vb)
            s_st[kj] = jnp.exp(lsz)
            dv_acc[pl.ds(ks, KB), :] += _tn(w.astype(BF), dos)
            return c_l - (suf[:, 0:1] + spm[:, 0:1])

        zc = jnp.zeros((GRP_HEADS * TQ, 1), F32)
        c_l = weights(qi, causal2, zc)

        def acond(carry):
            return jnp.logical_and(carry[0] <= qi, carry[1] > 0)

        def abody(carry):
            c_l = weights(qi - carry[0], None, carry[2])
            return carry[0] + 1, _alive(c_l), c_l

        n_used = lax.while_loop(acond, abody, (jnp.int32(1), _alive(c_l), c_l))[0]

        def grads(kj, causal, c_g, dq):
            ks = pl.multiple_of(kj * KB, KB)
            kb = k_ref[pl.ds(ks, KB), :]
            g = g_st[kj]
            sig = s_st[kj]
            pre = _nn(g.astype(BF), lower)
            dz = g * (1.0 - sig) - sig * (pre + c_g)
            if causal is not None:
                dz = jnp.where(causal, dz, 0.0)
            dzb = dz.astype(BF)
            dk_acc[pl.ds(ks, KB), :] += _tn(dzb, qs)
            return c_g + (pre[:, KB - 1:KB] + g[:, KB - 1:KB]), dq + _nn(dzb, kb)

        c_g, dq = lax.fori_loop(qi - n_used + 1, qi, lambda kj, cr: grads(kj, None, *cr),
                                (zc, jnp.zeros((GRP_HEADS * TQ, GRP_W), F32)))
        _, dq = grads(qi, causal2, c_g, dq)
        dq = _unstack_heads(dq, heads, TQ) * SCALE
        out_ref[0, pl.ds(pl.multiple_of(qi * TQ, TQ), TQ), :] = dq.astype(BF)

        @pl.when(qi == nq - 1)
        def _():
            out_ref[1] = dk_acc[...].astype(BF)
            out_ref[2] = dv_acc[...].astype(BF)

    w = GRP_W
    full = lambda i: pl.BlockSpec((None, None, seq, w), lambda b, hp, q: (i, b, 0, hp))
    qblk = pl.BlockSpec((None, None, TQ, w), lambda b, hp, q: (0, b, q, hp))
    oblk = pl.BlockSpec((None, TQ, w), lambda b, hp, q: (b, q, hp))
    return _call(
        body, "sb_bwd", (nb, N_HEADS // GRP_HEADS, nq),
        [qblk, full(1), full(2), oblk],
        [pl.BlockSpec((3, None, seq, w), lambda b, hp, q: (0, b, 0, hp))],
        [_sds((6, nb, seq, D_GRP), BF)], (qkv6, qkv6, qkv6, do),
        scratch=[pltpu.VMEM((seq, w), F32), pltpu.VMEM((seq, w), F32),
                 pltpu.VMEM((nk, GRP_HEADS * TQ, KB), F32), pltpu.VMEM((nk, GRP_HEADS * TQ, KB), F32)],
        carry=carry)


def _t5_bucket(n):
    max_exact = N_BUCKETS // 2
    nf = np.maximum(n, 1).astype(np.float32)
    large = max_exact + (np.log(nf / max_exact) / math.log(MAX_DISTANCE / max_exact)
                         * (N_BUCKETS - max_exact)).astype(np.int32)
    large = np.minimum(large, N_BUCKETS - 1)
    return np.where(n < max_exact, n, large).astype(np.int32)


def _bucket_map(dilation):
    step = BLOCK + np.arange(BLOCK)[:, None] - np.arange(2 * BLOCK)[None, :]
    return _t5_bucket(np.clip(step, 0, N_STEPS) * dilation)


GRP_HEADS = 4
GRP_W = GRP_HEADS * HEAD_DIM


def _dil_masks():
    lane = lax.broadcasted_iota(jnp.int32, (1, GRP_W), 1)
    heads = [jnp.logical_and(lane >= HEAD_DIM * i, lane < HEAD_DIM * (i + 1)) for i in range(GRP_HEADS)]
    iq = jnp.bitwise_and(lax.broadcasted_iota(jnp.int32, (GRP_HEADS * BLOCK, BLOCK), 0), BLOCK - 1)
    ik = lax.broadcasted_iota(jnp.int32, (GRP_HEADS * BLOCK, BLOCK), 1)
    return heads, ik <= iq, ik >= iq


def _stack_heads(x, heads):
    zero = jnp.zeros_like(x)
    return jnp.concatenate([jnp.where(hm, x, zero) for hm in heads], axis=0)


def _unstack_heads(xs, heads, rows=BLOCK):
    out = xs[0:rows]
    for i in range(1, GRP_HEADS):
        out = jnp.where(heads[i], xs[i * rows:(i + 1) * rows], out)
    return out


def _dil_rows(n):
    rs = pl.multiple_of(n * BLOCK, BLOCK)
    ps = pl.multiple_of(jnp.maximum(n - 1, 0) * BLOCK, BLOCK)
    return pl.ds(rs, BLOCK), pl.ds(ps, BLOCK)


def _dil_probs(qs, kc, kp, b_ref, gi, valid_c, valid_p):
    rows = slice(gi * GRP_HEADS * BLOCK, (gi + 1) * GRP_HEADS * BLOCK)
    zc = _nt(qs, kc) * SCALE + b_ref[rows, BLOCK:2 * BLOCK]
    zp = _nt(qs, kp) * SCALE + b_ref[rows, 0:BLOCK]
    zc = jnp.where(valid_c, zc, NEG_INF)
    zp = jnp.where(valid_p, zp, NEG_INF)
    m = jnp.maximum(jnp.max(zc, axis=-1, keepdims=True), jnp.max(zp, axis=-1, keepdims=True))
    ec = jnp.exp(zc - m)
    ep = jnp.exp(zp - m)
    den = jnp.sum(ec, axis=-1, keepdims=True) + jnp.sum(ep, axis=-1, keepdims=True)
    return ec, ep, den, m


def _dil_fwd(qkv6r, base, bias, nb, sub_len, dilation):
    n_blk = sub_len // BLOCK

    def body(q_ref, k_ref, v_ref, b_ref, o_ref, l_ref):
        heads, valid_c, valid_p0 = _dil_masks()

        def nbody(n, carry):
            cur, prev = _dil_rows(n)
            valid_p = jnp.logical_and(valid_p0, n > 0)
            for gi in range(N_HEADS // GRP_HEADS):
                lanes = slice(gi * GRP_W, (gi + 1) * GRP_W)
                qs = _stack_heads(q_ref[cur, lanes], heads)
                ec, ep, den, m = _dil_probs(qs, k_ref[cur, lanes], k_ref[prev, lanes], b_ref, gi, valid_c, valid_p)
                o = (_nn(ec.astype(BF), v_ref[cur, lanes]) + _nn(ep.astype(BF), v_ref[prev, lanes])) / den
                o_ref[cur, lanes] = _unstack_heads(o, heads).astype(BF)
                l_ref[cur, lanes] = _unstack_heads(jnp.broadcast_to(m + jnp.log(den), o.shape), heads)
            return carry

        lax.fori_loop(0, n_blk, nbody, 0)

    seqblk = lambda i: pl.BlockSpec((None, None, sub_len, D_GRP), lambda b, r: (i, b, 0, r))
    oblk = pl.BlockSpec((None, sub_len, D_GRP), lambda b, r: (b, 0, r))
    shp = _sds((nb, sub_len, dilation * D_GRP), F32)
    return pl.pallas_call(
        body, name="dil_fwd_%d" % dilation, grid=(nb, dilation),
        in_specs=[seqblk(base), seqblk(base + 1), seqblk(base + 2), _whole(bias)],
        out_specs=[oblk, oblk], out_shape=[_sds(shp.shape, BF), shp],
        compiler_params=_cp(2))(qkv6r, qkv6r, qkv6r, bias)


def _dil_bwd(qkv6r, base, bias, do_c, dd_c, nb, sub_len, dilation, carry=None):
    n_blk = sub_len // BLOCK

    def body(q_ref, k_ref, v_ref, b_ref, do_ref, dd_ref, out_ref, a_ref, dk_acc, dv_acc):
        heads, valid_c, valid_p0 = _dil_masks()
        first = jnp.logical_and(pl.program_id(0) == 0, pl.program_id(1) == 0)

        @pl.when(first)
        def _():
            a_ref[...] = jnp.zeros_like(a_ref)

        dk_acc[...] = jnp.zeros_like(dk_acc)
        dv_acc[...] = jnp.zeros_like(dv_acc)

        def nbody(n, carry):
            cur, prev = _dil_rows(n)
            valid_p = jnp.logical_and(valid_p0, n > 0)
            for gi in range(N_HEADS // GRP_HEADS):
                lanes = slice(gi * GRP_W, (gi + 1) * GRP_W)
                kc, kp = k_ref[cur, lanes], k_ref[prev, lanes]
                vc, vp = v_ref[cur, lanes], v_ref[prev, lanes]
                qs = _stack_heads(q_ref[cur, lanes], heads)
                dos = _stack_heads(do_ref[cur, lanes], heads).astype(BF)
                dds = jnp.sum(_stack_heads(dd_ref[cur, lanes], heads), axis=-1, keepdims=True) * (1.0 / HEAD_DIM)
                ec, ep, den, _ = _dil_probs(qs, kc, kp, b_ref, gi, valid_c, valid_p)
                inv = 1.0 / den
                pc = ec * inv
                pp = ep * inv
                dzc = pc * (_nt(dos, vc) + dds)
                dzp = pp * (_nt(dos, vp) + dds)
                rows = slice(gi * GRP_HEADS * BLOCK, (gi + 1) * GRP_HEADS * BLOCK)
                a_ref[rows, BLOCK:2 * BLOCK] += dzc
                a_ref[rows, 0:BLOCK] += dzp
                dzcb = (dzc * SCALE).astype(BF)
                dzpb = (dzp * SCALE).astype(BF)
                out_ref[0, cur, lanes] = _unstack_heads(_nn(dzcb, kc) + _nn(dzpb, kp), heads).astype(BF)
                dk_acc[cur, lanes] += _tn(dzcb, qs)
                dk_acc[prev, lanes] += _tn(dzpb, qs)
                dv_acc[cur, lanes] += _tn(pc.astype(BF), dos)
                dv_acc[prev, lanes] += _tn(pp.astype(BF), dos)
            return carry

        lax.fori_loop(0, n_blk, nbody, 0)
        out_ref[1] = dk_acc[...].astype(BF)
        out_ref[2] = dv_acc[...].astype(BF)

    seqblk = lambda i: pl.BlockSpec((None, None, sub_len, D_GRP), lambda b, r: (i, b, 0, r))
    oblk = pl.BlockSpec((None, sub_len, D_GRP), lambda b, r: (b, 0, r))
    return _call(
        body, "dil_bwd_%d" % dilation, (nb, dilation),
        [seqblk(base), seqblk(base + 1), seqblk(base + 2), _whole(bias), oblk, oblk],
        [pl.BlockSpec((3, None, sub_len, D_GRP), lambda b, r: (0, b, 0, r)),
         pl.BlockSpec((N_HEADS * BLOCK, 2 * BLOCK), lambda b, r: (0, 0))],
        [_sds((3, nb, sub_len, dilation * D_GRP), BF), _sds((N_HEADS * BLOCK, 2 * BLOCK), F32)],
        (qkv6r, qkv6r, qkv6r, bias, do_c, dd_c),
        scratch=[pltpu.VMEM((sub_len, D_GRP), F32)] * 2, carry=carry)


def _group_ones():
    idx = np.arange(D_GRP) // HEAD_DIM
    return jnp.asarray((idx[:, None] == idx[None, :]).astype(np.float32), dtype=BF)


def _dil_alphas(l1, l4, l16):
    mx = jnp.maximum(jnp.maximum(l1, l4), l16)
    e1 = jnp.exp(l1 - mx)
    e4 = jnp.exp(l4 - mx)
    e16 = jnp.exp(l16 - mx)
    den = e1 + e4 + e16
    return e1 / den, e4 / den, e16 / den


def _residue_spec(dil):
    return pl.BlockSpec((TM // dil, dil * D_GRP), lambda m: (m, 0))


def _from_residue(src, dil, cg, buf):
    if dil == 1:
        return src[:, cg * 128:(cg + 1) * 128].astype(F32)
    for r in range(dil):
        buf[pl.ds(r, TM // dil, stride=dil), :] = (
            src[:, r * D_GRP + cg * 128:r * D_GRP + (cg + 1) * 128].astype(F32))
    return buf[...]


def _to_residue(dst, dil, cg, buf, val):
    if dil == 1:
        dst[:, cg * 128:(cg + 1) * 128] = val.astype(dst.dtype)
        return
    buf[...] = val
    for r in range(dil):
        dst[:, r * D_GRP + cg * 128:r * D_GRP + (cg + 1) * 128] = (
            buf[pl.ds(r, TM // dil, stride=dil), :].astype(dst.dtype))


def _pair_sum(x, hm0):
    s0 = jnp.sum(jnp.where(hm0, x, 0.0), axis=-1, keepdims=True)
    s1 = jnp.sum(jnp.where(hm0, 0.0, x), axis=-1, keepdims=True)
    return jnp.where(hm0, s0, s1)


def _dil_comb(os, ls, g_dil):
    t = os[0].shape[0]
    dils = [dil for _, dil in DIL_CONFIGS]

    def body(o1, l1, o4, l4, o16, l16, g_ref, o_ref, on_ref, b0, b1, b2, b3):
        hm0 = lax.broadcasted_iota(jnp.int32, (1, 128), 1) < HEAD_DIM
        for cg in range(D_GRP // 128):
            lanes = slice(cg * 128, (cg + 1) * 128)
            ov = [_from_residue(src, dil, cg, buf) for src, dil, buf in zip((o1, o4, o16), dils, (None, b0, b1))]
            lv = [_from_residue(src, dil, cg, buf) for src, dil, buf in zip((l1, l4, l16), dils, (None, b2, b3))]
            a1, a4, a16 = _dil_alphas(*lv)
            o = a1 * ov[0] + a4 * ov[1] + a16 * ov[2]
            o_ref[:, lanes] = o.astype(BF)
            on_ref[:, lanes] = _headnorm_pair(o, g_ref[:, lanes], hm0).astype(BF)

    blk = pl.BlockSpec((TM, D_GRP), lambda m: (m, 0))
    specs = [_residue_spec(dil) for dil in dils for _ in range(2)]
    return pl.pallas_call(
        body, name="dil_comb", grid=(t // TM,),
        in_specs=specs + [pl.BlockSpec((1, D_GRP), lambda m: (0, 0))],
        out_specs=[blk, blk],
        out_shape=[_sds((t, D_GRP), BF), _sds((t, D_GRP), BF)],
        scratch_shapes=[pltpu.VMEM((TM, 128), F32)] * 4,
        compiler_params=_cp(1))(os[0], ls[0], os[1], ls[1], os[2], ls[2], g_dil)


def _dil_comb_bwd(do, os, ls):
    t = do.shape[0]
    dils = [dil for _, dil in DIL_CONFIGS]

    def body(do_ref, o1, l1, o4, l4, o16, l16, d1, d4, d16, e1, e4, e16, b0, b1, b2, b3):
        hm0 = lax.broadcasted_iota(jnp.int32, (1, 128), 1) < HEAD_DIM
        for cg in range(D_GRP // 128):
            dov = do_ref[:, cg * 128:(cg + 1) * 128].astype(F32)
            ov = [_from_residue(src, dil, cg, buf) for src, dil, buf in zip((o1, o4, o16), dils, (None, b0, b1))]
            lv = [_from_residue(src, dil, cg, buf) for src, dil, buf in zip((l1, l4, l16), dils, (None, b2, b3))]
            al = _dil_alphas(*lv)
            sbar = al[0] * _pair_sum(dov * ov[0], hm0)
            for a_c, o_c in zip(al[1:], ov[1:]):
                sbar = sbar + a_c * _pair_sum(dov * o_c, hm0)
            for a_c, dil, dref, eref in zip(al, dils, (d1, d4, d16), (e1, e4, e16)):
                _to_residue(dref, dil, cg, b0, a_c * dov)
                _to_residue(eref, dil, cg, b1, -a_c * sbar)

    specs = [_residue_spec(dil) for dil in dils]
    return pl.pallas_call(
        body, name="dil_comb_bwd", grid=(t // TM,),
        in_specs=[pl.BlockSpec((TM, D_GRP), lambda m: (m, 0))] + [sp for sp in specs for _ in range(2)],
        out_specs=specs + specs,
        out_shape=[_sds((t // dil, dil * D_GRP), BF) for dil in dils]
        + [_sds((t // dil, dil * D_GRP), F32) for dil in dils],
        scratch_shapes=[pltpu.VMEM((TM, 128), F32)] * 4,
        compiler_params=_cp(1))(do, os[0], ls[0], os[1], ls[1], os[2], ls[2])


def _dqkv_dil_sum(ds, dqkv6):
    t = dqkv6.shape[1]
    dils = [dil for _, dil in DIL_CONFIGS]

    def body(*refs):
        srcs, o_ref, acc = refs[:len(dils)], refs[len(dils) + 1], refs[len(dils) + 2]
        for a in range(3):
            for cg in range(D_GRP // 128):
                for src, dil in zip(srcs, dils):
                    for r in range(dil):
                        part = src[a, :, r * D_GRP + cg * 128:r * D_GRP + (cg + 1) * 128].astype(F32)
                        rows = pl.ds(r, TM // dil, stride=dil) if dil > 1 else slice(None)
                        if dil == dils[0]:
                            acc[rows, :] = part
                        else:
                            acc[rows, :] += part
                o_ref[a, :, cg * 128:(cg + 1) * 128] = acc[...].astype(BF)

    return pl.pallas_call(
        body, name="dqkv_dil_sum", grid=(t // TM,),
        in_specs=[pl.BlockSpec((3, TM // dil, dil * D_GRP), lambda m: (0, m, 0)) for dil in dils]
        + [pl.BlockSpec(memory_space=pl.ANY)],
        out_specs=pl.BlockSpec((3, TM, D_GRP), lambda m: (1, m, 0)),
        out_shape=_sds((6, t, D_GRP), BF), input_output_aliases={len(dils): 0},
        scratch_shapes=[pltpu.VMEM((TM, 128), F32)],
        compiler_params=_cp(1))(*ds, dqkv6)


def _relbias_grad(a_all, onehot):
    def body(a_ref, oh_ref, o_ref):
        acc = jnp.zeros((N_HEADS, N_BUCKETS), F32)
        for c in range(len(DIL_CONFIGS)):
            av = a_ref[c]
            hi = av.astype(BF)
            lo = (av - hi.astype(F32)).astype(BF)
            acc = acc + _nt(hi, oh_ref[c]) + _nt(lo, oh_ref[c])
        o_ref[...] = acc

    return pl.pallas_call(body, name="relbias_grad", out_shape=_sds((N_HEADS, N_BUCKETS), F32),
                          compiler_params=_cp())(a_all, onehot)


def _headnorm_bwd(dn, o, gv, mv):
    ms = _nn2(o * o, mv) * (1.0 / HEAD_DIM)
    r = lax.rsqrt(ms + EPS)
    nrm = o * r
    dg = jnp.sum(dn * nrm, axis=0, keepdims=True)
    dnn = dn * gv
    do = r * (dnn - nrm * (_nn2(dnn * nrm, mv) * (1.0 / HEAD_DIM)))
    return do, dg


def _mix_bwd_out(dx, gt, tv, w_out, o_sb, o_dil, on_sb, on_dil, g_sb, g_dil, ones_g, seq, carry=None):
    t, d = dx.shape
    per = seq // TM
    nb = t // seq

    def body(dx_ref, gt_ref, t_ref, w_ref, osb, odl, onsb, ondl, gsb, gdl, m_ref,
             dosb, dodl, dgt_ref, dgsb, dgdl, dw_ref):
        m = pl.program_id(0)
        dxv = dx_ref[...]
        dt = (gt_ref[...] * dxv).astype(BF)
        _acc_rows(dgt_ref, jnp.sum(dxv * t_ref[...].astype(F32), axis=0, keepdims=True), m % per == 0)
        mv = _twice(m_ref[...])
        don_sb = _nt(dt, w_ref[0:D_GRP, :])
        don_dl = _nt(dt, w_ref[D_GRP:2 * D_GRP, :])
        do1, dg1 = _headnorm_bwd(don_sb, osb[...].astype(F32), gsb[...], mv)
        do2, dg2 = _headnorm_bwd(don_dl, odl[...].astype(F32), gdl[...], mv)
        dosb[...] = do1.astype(BF)
        dodl[...] = do2.astype(BF)
        _acc_rows(dgsb, dg1, m == 0)
        _acc_rows(dgdl, dg2, m == 0)
        p1 = _tn(onsb[...], dt)
        p2 = _tn(ondl[...], dt)

        @pl.when(m == 0)
        def _():
            dw_ref[0:D_GRP, :] = p1
            dw_ref[D_GRP:2 * D_GRP, :] = p2

        @pl.when(m != 0)
        def _():
            dw_ref[0:D_GRP, :] += p1
            dw_ref[D_GRP:2 * D_GRP, :] += p2

    row = pl.BlockSpec((TM, d), lambda m: (m, 0))
    half = pl.BlockSpec((TM, D_GRP), lambda m: (m, 0))
    ex = pl.BlockSpec((None, 1, d), lambda m: (m // per, 0, 0))
    gvec = pl.BlockSpec((1, D_GRP), lambda m: (0, 0))
    wblk = pl.BlockSpec((2 * D_GRP, d), lambda m: (0, 0))
    return _call(
        body, "mix_bwd_out", (t // TM,),
        [row, ex, row, wblk, half, half, half, half, gvec, gvec, pl.BlockSpec((D_GRP, D_GRP), lambda m: (0, 0))],
        [half, half, ex, gvec, gvec, wblk],
        [_sds((t, D_GRP), BF), _sds((t, D_GRP), BF), _sds((nb, 1, d), F32),
         _sds((1, D_GRP), F32), _sds((1, D_GRP), F32), _sds((2 * D_GRP, d), F32)],
        (dx, gt, tv, w_out, o_sb, o_dil, on_sb, on_dil, g_sb, g_dil, ones_g), carry=carry)


def _dw_in(h, dqkv6, carry=None):
    t, d = h.shape
    wc = 6 * D_GRP // N_CHIPS

    def body(h_ref, g_ref, o_ref):
        kt = pl.program_id(0)
        hv = h_ref[...]
        for j in range(N_CHIPS):
            p = _tn(hv, _chip_cols(g_ref, j, wc))

            @pl.when(kt == 0)
            def _(p=p, j=j):
                o_ref[j, 0] = p

            @pl.when(kt != 0)
            def _(p=p, j=j):
                o_ref[j, 0] += p

    return _call(
        body, "dw_in", (t // TK_W,),
        [pl.BlockSpec((TK_W, d), lambda kt: (kt, 0)), pl.BlockSpec((6, TK_W, D_GRP), lambda kt: (0, kt, 0))],
        [pl.BlockSpec((N_CHIPS, 1, d, wc), lambda kt: (0, 0, 0, 0))],
        [_sds((N_CHIPS, 1, d, wc), F32)], (h, dqkv6), carry=carry)


def _mix_bwd_dh(dqkv6, w_in, x, g, sc, dxo, seq, carry=None):
    _, t, _ = dqkv6.shape
    d = x.shape[-1]
    wc = w_in.shape[-1]
    per = seq // TM
    nb = t // seq

    def body(g6_ref, w_ref, x_ref, g_ref, sc_ref, dxo_ref, dx_ref, dsh_ref, dsc_ref, dg_ref):
        m = pl.program_id(0)
        dh = _nt(_chip_cols(g6_ref, 0, wc), w_ref[0, 0])
        for j in range(1, N_CHIPS):
            dh = dh + _nt(_chip_cols(g6_ref, j, wc), w_ref[j, 0])
        dx, dsh, dsc, dg = _modnorm_bwd_tile(dh, x_ref[...], g_ref[...], sc_ref[...], dxo_ref[...])
        dx_ref[...] = dx
        _acc_rows(dsh_ref, dsh, m % per == 0)
        _acc_rows(dsc_ref, dsc, m % per == 0)
        _acc_rows(dg_ref, dg, m == 0)

    row = pl.BlockSpec((TM, d), lambda m: (m, 0))
    ex = pl.BlockSpec((None, 1, d), lambda m: (m // per, 0, 0))
    vec = pl.BlockSpec((1, d), lambda m: (0, 0))
    return _call(
        body, "mix_bwd_dh", (t // TM,),
        [pl.BlockSpec((6, TM, D_GRP), lambda m: (0, m, 0)), _whole(w_in), row, vec, ex, row],
        [row, ex, ex, vec],
        [_sds((t, d), F32), _sds((nb, 1, d), F32), _sds((nb, 1, d), F32), _sds((1, d), F32)],
        (dqkv6, w_in, x, g, sc, dxo), carry=carry)


def _ffn_down_loss(s, wd, x, gt, seq, coef, g, target):
    _, t, fs = s.shape
    d = x.shape[-1]
    per = seq // TM
    steps = t // TM

    def body(s_ref, w_ref, x_ref, gt_ref, g_ref, t_ref, f_ref, dx_ref, dg_ref, loss_ref, lacc):
        m = pl.program_id(0)
        f = _nn(s_ref[0], w_ref[0, 0])
        for j in range(1, N_CHIPS):
            f = f + _nn(s_ref[j], w_ref[j, 0])
        f_ref[...] = f.astype(BF)
        xv = x_ref[...] + (coef * gt_ref[...]) * f
        gv = g_ref[...]
        r = lax.rsqrt(jnp.mean(xv * xv, axis=-1, keepdims=True) + EPS)
        n = xv * r
        err = n * gv - t_ref[...]
        dy = err * (1.0 / d)
        _acc_rows(dg_ref, jnp.sum(dy * n, axis=0, keepdims=True), m == 0)
        dn = dy * gv
        dx_ref[...] = r * (dn - n * jnp.mean(dn * n, axis=-1, keepdims=True))
        _acc_rows(lacc, jnp.sum(err * err, axis=0, keepdims=True), m == 0)

        @pl.when(m == steps - 1)
        def _():
            tot = jnp.sum(lacc[...], axis=-1, keepdims=True) * (0.5 / d)
            loss_ref[...] = jnp.broadcast_to(tot, (1, 128))

    row = pl.BlockSpec((TM, d), lambda m: (m, 0))
    vec = pl.BlockSpec((1, d), lambda m: (0, 0))
    return pl.pallas_call(
        body, name="ffn_down_loss", grid=(steps,),
        in_specs=[pl.BlockSpec((N_CHIPS, TM, fs), lambda m: (0, m, 0)), _whole(wd), row,
                  pl.BlockSpec((None, 1, d), lambda m: (m // per, 0, 0)), vec, row],
        out_specs=[row, row, vec, pl.BlockSpec((1, 128), lambda m: (0, 0))],
        out_shape=[_sds((t, d), BF), _sds((t, d), F32), _sds((1, d), F32), _sds((1, 128), F32)],
        scratch_shapes=[pltpu.VMEM((1, d), F32)],
        compiler_params=_cp(1))(s, wd, x, gt, g, target)


def _row_tile(rows, cols):
    best = rows
    for tr in range(8, rows + 1, 8):
        if rows % tr == 0 and tr * cols * 4 <= (1 << 20):
            best = tr
    if best * cols * 4 > (1 << 21):
        best = 8
    return best


def _adamw(w, g_arr, g_sel, m, v):
    rows, cols = w.shape
    tr = _row_tile(rows, cols)
    b1c = 1.0 - ADAM_B1 ** ADAM_STEP
    b2c = 1.0 - ADAM_B2 ** ADAM_STEP

    def body(w_ref, g_ref, m_ref, v_ref, go_ref, d_ref, mo_ref, vo_ref):
        gv = g_ref[...]
        mn = ADAM_B1 * m_ref[...] + (1.0 - ADAM_B1) * gv
        vn = ADAM_B2 * v_ref[...] + (1.0 - ADAM_B2) * (gv * gv)
        go_ref[...] = gv
        mo_ref[...] = mn
        vo_ref[...] = vn
        d_ref[...] = -ADAM_LR * ((mn / b1c) / (jnp.sqrt(vn / b2c) + ADAM_EPS) + ADAM_WD * w_ref[...])

    blk = pl.BlockSpec((tr, cols), lambda i: (i, 0))
    shp = _sds((rows, cols), F32)
    return pl.pallas_call(
        body, name="adamw", grid=(rows // tr,),
        in_specs=[blk, pl.BlockSpec((None, tr, cols), lambda i: (g_sel, i, 0)), blk, blk],
        out_specs=[blk] * 4, out_shape=[shp] * 4,
        compiler_params=_cp(1))(w, g_arr, m, v)


def _flip(v, bit):
    return 1 - v if bit else v


def _my_place():
    x, y, c = lax.axis_index("x"), lax.axis_index("y"), lax.axis_index("c")
    return x, y, c


class _Exchange:
    def __init__(self, operands, out_shape, aliases, sems, start, finish):
        self.operands, self.out_shape, self.aliases, self.sems = list(operands), list(out_shape), dict(aliases), list(sems)
        self.start, self.finish = start, finish


def _join(exchanges):
    exchanges = [e for e in exchanges if e is not None]
    if not exchanges:
        return None
    ops, outs, sems, aliases, spans = [], [], [], {}, []
    for e in exchanges:
        spans.append((len(ops), len(outs), len(sems), e))
        for i, j in e.aliases.items():
            aliases[len(ops) + i] = len(outs) + j
        ops += e.operands
        outs += e.out_shape
        sems += e.sems

    def run(which):
        def go(ins, res, sm):
            for io, oo, so, e in spans:
                getattr(e, which)(ins[io:io + len(e.operands)], res[oo:oo + len(e.out_shape)], sm[so:so + len(e.sems)])
        return go

    return _Exchange(ops, outs, aliases, sems, run("start"), run("finish"))


def _call(body, name, grid, in_specs, out_specs, out_shape, args, scratch=(), carry=None, io_alias=None):
    in_specs, out_specs, out_shape, scratch = list(in_specs), list(out_specs), list(out_shape), list(scratch)
    io_alias = dict(io_alias or {})
    if carry is None:
        return pl.pallas_call(body, name=name, grid=grid, in_specs=in_specs, out_specs=out_specs,
                              out_shape=out_shape, scratch_shapes=scratch, input_output_aliases=io_alias,
                              compiler_params=_cp(len(grid)))(*args)
    n_in, n_out, n_s = len(in_specs), len(out_specs), len(scratch)
    c_in, c_out = len(carry.operands), len(carry.out_shape)
    any_spec = pl.BlockSpec(memory_space=pl.ANY)

    def wrapped(*refs):
        ins, cins = refs[:n_in], refs[n_in:n_in + c_in]
        o0 = n_in + c_in
        outs, couts = refs[o0:o0 + n_out], refs[o0 + n_out:o0 + n_out + c_out]
        s0 = o0 + n_out + c_out
        scr, sems = refs[s0:s0 + n_s], refs[s0 + n_s:]
        first = pl.program_id(0) == 0
        last = pl.program_id(0) == grid[0] - 1
        for ax in range(1, len(grid)):
            first = jnp.logical_and(first, pl.program_id(ax) == 0)
            last = jnp.logical_and(last, pl.program_id(ax) == grid[ax] - 1)

        @pl.when(first)
        def _():
            carry.start(cins, couts, sems)

        body(*ins, *outs, *scr)

        @pl.when(last)
        def _():
            carry.finish(cins, couts, sems)

    return pl.pallas_call(
        wrapped, name=name, grid=grid, in_specs=in_specs + [any_spec] * c_in,
        out_specs=out_specs + [any_spec] * c_out, out_shape=out_shape + carry.out_shape,
        scratch_shapes=scratch + carry.sems,
        input_output_aliases={**io_alias, **{n_in + i: n_out + j for i, j in carry.aliases.items()}},
        compiler_params=_cp(len(grid)))(*args, *carry.operands)


def _whole_call(body, name, args, out_shape, scratch, carry=None):
    vm = pl.BlockSpec(memory_space=pltpu.VMEM)
    any_spec = pl.BlockSpec(memory_space=pl.ANY)
    out_shape, scratch = list(out_shape), list(scratch)
    n_in, n_out, n_s = len(args), len(out_shape), len(scratch)
    if carry is None:
        return pl.pallas_call(body, name=name, in_specs=[vm] * n_in, out_specs=[vm] * n_out, out_shape=out_shape,
                              scratch_shapes=scratch, compiler_params=_cp())(*args)
    c_in, c_out = len(carry.operands), len(carry.out_shape)

    def wrapped(*refs):
        ins, cins = refs[:n_in], refs[n_in:n_in + c_in]
        o0 = n_in + c_in
        outs, couts = refs[o0:o0 + n_out], refs[o0 + n_out:o0 + n_out + c_out]
        s0 = o0 + n_out + c_out
        scr, sems = refs[s0:s0 + n_s], refs[s0 + n_s:]
        carry.start(cins, couts, sems)
        body(*ins, *outs, *scr)
        carry.finish(cins, couts, sems)

    return pl.pallas_call(
        wrapped, name=name, in_specs=[vm] * n_in + [any_spec] * c_in, out_specs=[vm] * n_out + [any_spec] * c_out,
        out_shape=out_shape + carry.out_shape, scratch_shapes=scratch + carry.sems,
        input_output_aliases={n_in + i: n_out + j for i, j in carry.aliases.items()},
        compiler_params=_cp())(*args, *carry.operands)


def _alone(name, ex):
    any_spec = pl.BlockSpec(memory_space=pl.ANY)
    c_in, c_out = len(ex.operands), len(ex.out_shape)

    def body(*refs):
        ins, outs, sems = refs[:c_in], refs[c_in:c_in + c_out], refs[c_in + c_out:]
        ex.start(ins, outs, sems)
        ex.finish(ins, outs, sems)

    return pl.pallas_call(
        body, name=name, in_specs=[any_spec] * c_in, out_specs=[any_spec] * c_out, out_shape=ex.out_shape,
        scratch_shapes=ex.sems, input_output_aliases=ex.aliases, compiler_params=_cp())(*ex.operands)


def _ada_fwd(c_pad, w_ada, b_shard, carry=None):
    d = c_pad.shape[-1]
    cols = w_ada.shape[-1]
    chunk = 384

    def body(c_ref, w_ref, b_ref, call_ref, mod_ref, part, s1, r1, s2, r2):
        x, y, c = _my_place()
        dev = 4 * x + 2 * y + c
        chip = 2 * x + y
        call_ref[dev] = c_ref[...]

        def c_copy(k):
            px, py, pc = _flip(x, (k >> 2) & 1), _flip(y, (k >> 1) & 1), _flip(c, k & 1)
            return px, py, pc

        sends = []
        for k in range(1, N_DEV):
            px, py, pc = c_copy(k)
            cp = pltpu.make_async_remote_copy(src_ref=c_ref, dst_ref=call_ref.at[dev], send_sem=s1.at[k - 1],
                                              recv_sem=r1.at[k - 1], device_id=(px, py, pc), device_id_type=MESH)
            cp.start()
            sends.append(cp)
        for k in range(1, N_DEV):
            px, py, pc = c_copy(k)
            pltpu.make_async_remote_copy(src_ref=c_ref, dst_ref=call_ref.at[4 * px + 2 * py + pc],
                                         send_sem=s1.at[k - 1], recv_sem=r1.at[k - 1],
                                         device_id=(px, py, pc), device_id_type=MESH).wait_recv()
        for cp in sends:
            cp.wait_send()

        cs = call_ref[...].reshape(N_DEV * 8, d)
        sc = (cs * jax.nn.sigmoid(cs)).astype(BF)
        for n0 in range(0, cols, chunk):
            blk = _nn(sc, w_ref[:, n0:n0 + chunk].astype(BF)) + b_ref[:, n0:n0 + chunk]
            part[:, :, n0:n0 + chunk] = blk.reshape(N_DEV, 8, chunk)

        mod_ref[chip] = part[dev]
        sends = []
        for kk in range(1, N_CHIPS):
            px, py = _flip(x, (kk >> 1) & 1), _flip(y, kk & 1)
            cp = pltpu.make_async_remote_copy(src_ref=part.at[4 * px + 2 * py + c], dst_ref=mod_ref.at[chip],
                                              send_sem=s2.at[kk - 1], recv_sem=r2.at[kk - 1],
                                              device_id=(px, py, c), device_id_type=MESH)
            cp.start()
            sends.append(cp)
        for kk in range(1, N_CHIPS):
            px, py = _flip(x, (kk >> 1) & 1), _flip(y, kk & 1)
            pltpu.make_async_remote_copy(src_ref=part.at[dev], dst_ref=mod_ref.at[2 * px + py],
                                         send_sem=s2.at[kk - 1], recv_sem=r2.at[kk - 1],
                                         device_id=(px, py, c), device_id_type=MESH).wait_recv()
        for cp in sends:
            cp.wait_send()

    return _whole_call(
        body, "ada_fwd", (c_pad, w_ada, b_shard),
        [_sds((N_DEV, 8, d), F32), _sds((N_CHIPS, 8, cols), F32)],
        [pltpu.VMEM((N_DEV, 8, cols), F32),
         pltpu.SemaphoreType.DMA((N_DEV - 1,)), pltpu.SemaphoreType.DMA((N_DEV - 1,)),
         pltpu.SemaphoreType.DMA((N_CHIPS - 1,)), pltpu.SemaphoreType.DMA((N_CHIPS - 1,))], carry=carry)


def _ag_weights(bufs, kks=(1, 2, 3), relative=False):
    n, nk = len(bufs), len(kks)

    def half(b, which):
        hr = bufs[b].shape[2] // 2
        return pl.ds(pl.multiple_of(which * hr, 16), hr)

    def copies(outs, sems, b, i, kk):
        x, y, c = _my_place()
        chip = 2 * x + y
        px, py = _flip(x, (kk >> 1) & 1), _flip(y, kk & 1)
        mine, theirs = (0, kk) if relative else (chip, 2 * px + py)
        landing = kk if relative else chip
        k = nk * b + i
        send = pltpu.make_async_remote_copy(
            src_ref=outs[b].at[mine, :, half(b, c), :], dst_ref=outs[b].at[landing, :, half(b, c), :],
            send_sem=sems[0].at[k], recv_sem=sems[1].at[k], device_id=(px, py, c), device_id_type=MESH)
        got = outs[b].at[theirs, :, half(b, c), :]
        recv = pltpu.make_async_remote_copy(
            src_ref=got, dst_ref=got, send_sem=sems[0].at[k], recv_sem=sems[1].at[k],
            device_id=(px, py, c), device_id_type=MESH)
        fwd = pltpu.make_async_remote_copy(
            src_ref=got, dst_ref=got, send_sem=sems[2].at[k], recv_sem=sems[3].at[k],
            device_id=(x, y, 1 - c), device_id_type=MESH)
        other = outs[b].at[theirs, :, half(b, 1 - c), :]
        back = pltpu.make_async_remote_copy(
            src_ref=other, dst_ref=other, send_sem=sems[2].at[k], recv_sem=sems[3].at[k],
            device_id=(x, y, 1 - c), device_id_type=MESH)
        return send, recv, fwd, back

    def each(outs, sems):
        for b in range(n):
            for i, kk in enumerate(kks):
                yield copies(outs, sems, b, i, kk)

    def start(ins, outs, sems):
        for send, _, _, _ in each(outs, sems):
            send.start()

    def finish(ins, outs, sems):
        for _, recv, fwd, _ in each(outs, sems):
            recv.wait_recv()
            fwd.start()
        for send, _, fwd, back in each(outs, sems):
            back.wait_recv()
            send.wait_send()
            fwd.wait_send()

    return _Exchange(bufs, [_sds(s.shape, s.dtype) for s in bufs], {i: i for i in range(n)},
                     [pltpu.SemaphoreType.DMA((nk * n,))] * 4, start, finish)


def _rs_d2d(grads):
    n = len(grads)

    def copy(ins, outs, sems, b):
        x, y, c = _my_place()
        hr = grads[b].shape[2] // 2
        theirs = pl.ds(pl.multiple_of((1 - c) * hr, 8), hr)
        return pltpu.make_async_remote_copy(
            src_ref=ins[b].at[:, :, theirs, :], dst_ref=outs[b], send_sem=sems[0].at[b], recv_sem=sems[1].at[b],
            device_id=(x, y, 1 - c), device_id_type=MESH)

    def start(ins, outs, sems):
        for b in range(n):
            copy(ins, outs, sems, b).start()

    def finish(ins, outs, sems):
        for b in range(n):
            copy(ins, outs, sems, b).wait()

    return _Exchange(grads, [_sds(g.shape[:2] + (g.shape[2] // 2, g.shape[3]), F32) for g in grads], {},
                     [pltpu.SemaphoreType.DMA((n,))] * 2, start, finish)


def _add_halves(core, g, land):
    nchip, ng, rows, cols = g.shape
    hr = rows // 2
    tr = _row_tile(hr, cols)
    steps = hr // tr

    def body(core_ref, g_ref, l_ref, o_ref):
        del core_ref
        o_ref[...] = (g_ref[...] + l_ref[...]).astype(BF)

    return pl.pallas_call(
        body, name="add_halves",
        grid_spec=pltpu.PrefetchScalarGridSpec(
            num_scalar_prefetch=1, grid=(nchip, ng, steps),
            in_specs=[pl.BlockSpec((None, None, tr, cols), lambda j, a, i, cr: (j, a, cr[0] * steps + i, 0)),
                      pl.BlockSpec((None, None, tr, cols), lambda j, a, i, cr: (j, a, i, 0))],
            out_specs=pl.BlockSpec((None, None, tr, cols), lambda j, a, i, cr: (j, a, i, 0))),
        out_shape=_sds((nchip, ng, hr, cols), BF),
        compiler_params=_cp(3))(core, g, land)


def _rs_ici(parts, relative=False):
    n = len(parts)

    def copies(ins, outs, sems):
        x, y, c = _my_place()
        chip = 2 * x + y
        for b in range(n):
            for kk in range(1, N_CHIPS):
                px, py = _flip(x, (kk >> 1) & 1), _flip(y, kk & 1)
                k = 3 * b + kk - 1
                theirs, landing = (kk, kk) if relative else (2 * px + py, chip)
                send = pltpu.make_async_remote_copy(
                    src_ref=ins[b].at[theirs], dst_ref=outs[b].at[landing],
                    send_sem=sems[0].at[k], recv_sem=sems[1].at[k], device_id=(px, py, c), device_id_type=MESH)
                slot = outs[b].at[theirs]
                recv = pltpu.make_async_remote_copy(
                    src_ref=slot, dst_ref=slot, send_sem=sems[0].at[k], recv_sem=sems[1].at[k],
                    device_id=(px, py, c), device_id_type=MESH)
                yield send, recv

    def start(ins, outs, sems):
        for send, _ in copies(ins, outs, sems):
            send.start()

    def finish(ins, outs, sems):
        for send, recv in copies(ins, outs, sems):
            recv.wait_recv()
            send.wait_send()

    return _Exchange(parts, [_sds(p.shape, p.dtype) for p in parts], {},
                     [pltpu.SemaphoreType.DMA((3 * n,))] * 2, start, finish)


def _sum_chips(place, part, land, relative=False):
    nchip, ng, hr, cols = land.shape
    tr = _row_tile(hr, cols)
    steps = hr // tr

    def body(place_ref, p_ref, l1, l2, l3, o_ref):
        del place_ref
        o_ref[...] = ((p_ref[...].astype(F32) + l1[...].astype(F32)) + l2[...].astype(F32)) + l3[...].astype(F32)

    def slot(k):
        if relative:
            return pl.BlockSpec((None, None, tr, cols), lambda a, i, pr: (k, a, i, 0))
        return pl.BlockSpec((None, None, tr, cols), lambda a, i, pr: (jnp.bitwise_xor(pr[1], k), a, i, 0))

    return pl.pallas_call(
        body, name="sum_chips",
        grid_spec=pltpu.PrefetchScalarGridSpec(
            num_scalar_prefetch=1, grid=(ng, steps),
            in_specs=[slot(0), slot(1), slot(2), slot(3)],
            out_specs=pl.BlockSpec((None, tr, cols), lambda a, i, pr: (a, pr[0] * steps + i, 0))),
        out_shape=_sds((ng, 2 * hr, cols), F32),
        compiler_params=_cp(2))(place, part, land, land, land)


def _rs_final(bufs):
    n = len(bufs)

    def copy(outs, sems, b, which):
        x, y, c = _my_place()
        hr = bufs[b].shape[1] // 2
        rows = outs[b].at[:, pl.ds(pl.multiple_of((c if which == 0 else 1 - c) * hr, 8), hr), :]
        return pltpu.make_async_remote_copy(
            src_ref=rows, dst_ref=rows, send_sem=sems[0].at[b], recv_sem=sems[1].at[b],
            device_id=(x, y, 1 - c), device_id_type=MESH)

    def start(ins, outs, sems):
        for b in range(n):
            copy(outs, sems, b, 0).start()

    def finish(ins, outs, sems):
        for b in range(n):
            copy(outs, sems, b, 0).wait_send()
            copy(outs, sems, b, 1).wait_recv()

    return _Exchange(bufs, [_sds(h.shape, F32) for h in bufs], {i: i for i in range(n)},
                     [pltpu.SemaphoreType.DMA((n,))] * 2, start, finish)


def _small_sync(smalls, dmod_blk, c_all, carry=None):
    d = c_all.shape[-1]
    cols = dmod_blk.shape[-1]
    chunk = 384

    def body(sm_ref, dm_ref, c_ref, sum_ref, gw_ref, sm_all, dm_all, ssem, rsem):
        x, y, c = _my_place()
        dev = 4 * x + 2 * y + c
        chip = 2 * x + y
        sm_all[dev] = sm_ref[...]
        dm_all[dev] = dm_ref[chip]
        sends = []
        for k in range(1, N_DEV):
            px, py, pc = _flip(x, (k >> 2) & 1), _flip(y, (k >> 1) & 1), _flip(c, k & 1)
            a = pltpu.make_async_remote_copy(src_ref=sm_ref, dst_ref=sm_all.at[dev], send_sem=ssem.at[2 * (k - 1)],
                                             recv_sem=rsem.at[2 * (k - 1)], device_id=(px, py, pc),
                                             device_id_type=MESH)
            b = pltpu.make_async_remote_copy(src_ref=dm_ref.at[2 * px + py], dst_ref=dm_all.at[dev],
                                             send_sem=ssem.at[2 * (k - 1) + 1], recv_sem=rsem.at[2 * (k - 1) + 1],
                                             device_id=(px, py, pc), device_id_type=MESH)
            a.start()
            b.start()
            sends += [a, b]
        for k in range(1, N_DEV):
            px, py, pc = _flip(x, (k >> 2) & 1), _flip(y, (k >> 1) & 1), _flip(c, k & 1)
            pdev = 4 * px + 2 * py + pc
            pltpu.make_async_remote_copy(src_ref=sm_ref, dst_ref=sm_all.at[pdev], send_sem=ssem.at[2 * (k - 1)],
                                         recv_sem=rsem.at[2 * (k - 1)], device_id=(px, py, pc),
                                         device_id_type=MESH).wait_recv()
            pltpu.make_async_remote_copy(src_ref=dm_ref.at[chip], dst_ref=dm_all.at[pdev],
                                         send_sem=ssem.at[2 * (k - 1) + 1], recv_sem=rsem.at[2 * (k - 1) + 1],
                                         device_id=(px, py, pc), device_id_type=MESH).wait_recv()
        for cp in sends:
            cp.wait_send()

        tot = sm_all[0]
        for q in range(1, N_DEV):
            tot = tot + sm_all[q]
        sum_ref[...] = tot

        cs = c_ref[...].reshape(N_DEV * 8, d)
        sc = (cs * jax.nn.sigmoid(cs)).astype(BF)
        for n0 in range(0, cols, chunk):
            dmv = dm_all[:, :, n0:n0 + chunk].reshape(N_DEV * 8, chunk).astype(BF)
            gw_ref[:, n0:n0 + chunk] = _tn(sc, dmv)

    return _whole_call(
        body, "small_sync", (smalls, dmod_blk, c_all),
        [_sds(smalls.shape, F32), _sds((d, cols), F32)],
        [pltpu.VMEM((N_DEV,) + smalls.shape, F32), pltpu.VMEM((N_DEV, 8, cols), F32),
         pltpu.SemaphoreType.DMA((2 * (N_DEV - 1),)), pltpu.SemaphoreType.DMA((2 * (N_DEV - 1),))], carry=carry)


def _bucket_onehot():
    maps = np.stack([_bucket_map(dil).reshape(-1) for _, dil in DIL_CONFIGS])
    return (jnp.asarray(maps)[:, None, :] == jnp.arange(N_BUCKETS, dtype=jnp.int32)[None, :, None]).astype(BF)


def _dil_bias(rel_t, onehot):
    def body(r_ref, oh_ref, o_ref):
        rv = r_ref[...]
        hi = rv.astype(BF)
        lo = (rv - hi.astype(F32)).astype(BF)
        for c in range(len(DIL_CONFIGS)):
            o_ref[c] = _nn(hi, oh_ref[c]) + _nn(lo, oh_ref[c])

    return pl.pallas_call(body, name="dil_bias",
                          out_shape=_sds((len(DIL_CONFIGS), N_HEADS, BLOCK * 2 * BLOCK), F32),
                          compiler_params=_cp())(rel_t, onehot)


def _rowsum8(a):
    def body(a_ref, o_ref):
        o_ref[...] = jnp.sum(a_ref[...], axis=0, keepdims=True)

    return pl.pallas_call(body, name="rowsum8", out_shape=_sds((1, a.shape[1]), F32), compiler_params=_cp())(a)


def _local_step(x, mod, target, w, gains, rel_bias, place=None):
    nb, seq, d = x.shape
    t = nb * seq
    dist = place is not None
    core = place[0:1] if dist else None
    x0 = x.reshape(t, d)
    tgt = target.reshape(t, d)
    md = [mod[:, i:i + 1, :] for i in range(N_MOD)]
    sh1, sc1, gt1, sh2, sc2, gt2, sh3, sc3, gt3 = md
    g1, g2, g3 = gains["g_ffn1"], gains["g_mix"], gains["g_ffn2"]
    ones_g = _group_ones()

    def partial_sums(grads, lands):
        return [_add_halves(core, g, l) for g, l in zip(grads, lands)]

    def chip_sums(parts, lands):
        return [_sum_chips(place, p, l, relative=True) for p, l in zip(parts, lands)]

    gu1 = w["gu1"]
    res = _ffn_up(x0, g1, sc1, sh1, gu1, seq,
                  carry=_join([_ag_weights([w["d1"]], relative=True), _ag_weights([w["win"]])]) if dist else None)
    h1, a1, u1, s1 = res[:4]
    wd1, w_in = res[4:] if dist else (w["d1"], w["win"])
    f1, x1 = _ffn_down(s1, wd1, x0, gt1, seq, 0.5)

    res = _qkv_proj(x1, g2, sc2, sh2, w_in, seq, carry=_ag_weights([w["wout"]]) if dist else None)
    h2, qkv6, qkv_r4, qkv_r16 = res[:4]
    w_out2 = (res[4] if dist else w["wout"]).reshape(2 * D_GRP, d)
    qkv6b = qkv6.reshape(6, nb, seq, D_GRP)
    res = _sb_fwd(qkv6b, gains["g_sb_out"], nb, seq,
                  carry=_ag_weights([w["gu2"], w["d2"]], relative=True) if dist else None)
    o_sb, on_sb = res[:2]
    wgu2, wd2 = res[2:] if dist else (w["gu2"], w["d2"])
    onehot = _bucket_onehot()
    bias = _dil_bias(rel_bias.T, onehot).reshape(len(DIL_CONFIGS), N_HEADS * BLOCK, 2 * BLOCK)
    o_cs, l_cs = [], []
    qkv_rs = [(qkv6b, 3), (qkv_r4, 0), (qkv_r16, 0)]
    for ci, (_, dil) in enumerate(DIL_CONFIGS):
        sub = seq // dil
        arr, base = qkv_rs[ci]
        arr = arr.reshape(base + 3, nb, sub, dil * D_GRP)
        qkv_rs[ci] = (arr, base)
        o_c, l_c = _dil_fwd(arr, base, bias[ci], nb, sub, dil)
        o_cs.append(o_c.reshape(t // dil, dil * D_GRP))
        l_cs.append(l_c.reshape(t // dil, dil * D_GRP))
    o_dil, on_dil = _dil_comb(o_cs, l_cs, gains["g_dil_out"])
    tmix, x2 = _mix_out(on_sb.reshape(t, D_GRP), on_dil, w_out2, x1, gt2, seq)

    h3, a3, u3, s3 = _ffn_up(x2, g3, sc3, sh3, wgu2, seq)
    f3, dx3, dg_final, loss = _ffn_down_loss(s3, wd2, x2, gt3, seq, 0.5, gains["g_final"], tgt)

    da3, du3, df3, dgt3, dx2, dsh3, dsc3, dg3 = _ffn_bwd_x(dx3, gt3, f3, wd2, a3, u3, wgu2, x2, g3, sc3, seq, 0.5)
    grads2 = _ffn_bwd_w(h3, da3, du3, s3, df3)

    res = _mix_bwd_out(
        dx2, gt2, tmix, w_out2, o_sb.reshape(t, D_GRP), o_dil, on_sb.reshape(t, D_GRP), on_dil,
        gains["g_sb_out"], gains["g_dil_out"], ones_g, seq, carry=_rs_d2d(grads2) if dist else None)
    do_sb, do_dil, dgt2, dg_sb, dg_dil, dw_out = res[:6]
    parts2 = partial_sums(grads2, res[6:]) if dist else None
    dw_out = dw_out.reshape(N_CHIPS, 1, 2 * D_GRP // N_CHIPS, d)
    res = _sb_bwd(qkv6b, do_sb.reshape(nb, seq, D_GRP), nb, seq,
                  carry=_rs_ici(parts2, relative=True) if dist else None)
    dqkv6 = res[0]
    halves2 = chip_sums(parts2, res[1:]) if dist else None
    dcs = _dil_comb_bwd(do_dil, o_cs, l_cs)
    dsum, a_tiles = [], []
    for ci, (_, dil) in enumerate(DIL_CONFIGS):
        sub = seq // dil
        do_c = dcs[ci].reshape(nb, sub, dil * D_GRP)
        dd_c = dcs[3 + ci].reshape(nb, sub, dil * D_GRP)
        res = _dil_bwd(qkv_rs[ci][0], qkv_rs[ci][1], bias[ci], do_c, dd_c, nb, sub, dil)
        dsum.append(res[0].reshape(3, t // dil, dil * D_GRP))
        a_tiles.append(res[1].reshape(N_HEADS, BLOCK * 2 * BLOCK))
    dqkv6 = _dqkv_dil_sum(dsum, dqkv6.reshape(6, t, D_GRP))
    drel = _relbias_grad(jnp.stack(a_tiles), onehot)
    dx1, dsh2, dsc2, dg2 = _mix_bwd_dh(dqkv6, w_in, x1, g2, sc2, dx2, seq)

    da1, du1, df1, dgt1 = _ffn_bwd_ds(dx1, gt1, f1, wd1, a1, u1, seq, 0.5)
    g_dn = _ffn_bwd_w(h1, da1, du1, s1, df1, terms=(2,))
    res = _ffn_bwd_w(h1, da1, du1, s1, df1, terms=(0, 1), carry=_rs_d2d(g_dn) if dist else None)
    g_gu = res[:1]
    grads1 = g_gu + g_dn
    parts_dn = partial_sums(g_dn, res[1:2]) if dist else None
    res = _dw_in(h2, dqkv6, carry=_join([_rs_ici(parts_dn, relative=True), _rs_d2d(g_gu), _rs_final(halves2)])
                 if dist else None)
    grads_m = [res[0], dw_out]
    if dist:
        sums_dn = chip_sums(parts_dn, res[1:2])
        parts_gu = partial_sums(g_gu, res[2:3])
        grads2 = res[3:4]
    res = _ffn_bwd_dh(da1, du1, gu1, x0, g1, sc1, dx1, seq,
                      carry=_join([_rs_ici(parts_gu, relative=True), _rs_d2d(grads_m)]) if dist else None)
    dx0, dsh1, dsc1, dg1 = res[:4]
    pending = None
    if dist:
        pending = (chip_sums(parts_gu, res[4:5]) + sums_dn, partial_sums(grads_m, res[5:7]))

    dmod = jnp.concatenate([dsh1, dsc1, dgt1, dsh2, dsc2, dgt2, dsh3, dsc3, dgt3], axis=1)
    return dict(grad_x=dx0.reshape(nb, seq, d), loss=loss[0, 0], dmod=dmod.reshape(nb, N_MOD * d),
                dffn1=grads1, dffn2=grads2[0], dwin=grads_m[0], dwout=grads_m[1], pending=pending,
                dg_ffn1=dg1, dg_mix=dg2, dg_ffn2=dg3, dg_final=dg_final, dg_sb=dg_sb, dg_dil=dg_dil,
                drel=drel.T)


_SMALL_ORDER = (("b_ada", N_MOD * 1024), ("g_ffn1", 1024), ("g_mix", 1024), ("g_ffn2", 1024), ("g_final", 1024),
                ("g_sb_out", D_GRP), ("g_dil_out", D_GRP), ("rel_bias", N_BUCKETS * N_HEADS))


def _pack_small(parts, extra=None):
    flat = [parts[name].reshape(-1).astype(F32) for name, _ in _SMALL_ORDER]
    used = sum(sz for _, sz in _SMALL_ORDER)
    pad = SMALL_ROWS * 128 - used
    tail = jnp.zeros((pad,), F32)
    if extra is not None:
        tail = tail.at[0].set(extra)
    return jnp.concatenate(flat + [tail]).reshape(SMALL_ROWS, 128)


def _unpack_small(packed, shapes):
    flat = packed.reshape(-1)
    out, off = {}, 0
    for name, sz in _SMALL_ORDER:
        out[name] = flat[off:off + sz].reshape(shapes[name])
        off += sz
    return out, flat[off]


def kernel(x, c, w_ada, b_ada, g_ffn1, w1_gate, w1_up, w1_down, g_mix, w_in, g_sb_out, g_dil_out, w_out, rel_bias, g_ffn2, w2_gate, w2_up, w2_down, g_final, loss_target, m_w_ada, m_b_ada, m_g_ffn1, m_w1_gate, m_w1_up, m_w1_down, m_g_mix, m_w_in, m_g_sb_out, m_g_dil_out, m_w_out, m_rel_bias, m_g_ffn2, m_w2_gate, m_w2_up, m_w2_down, m_g_final, v_w_ada, v_b_ada, v_g_ffn1, v_w1_gate, v_w1_up, v_w1_down, v_g_mix, v_w_in, v_g_sb_out, v_g_dil_out, v_w_out, v_rel_bias, v_g_ffn2, v_w2_gate, v_w2_up, v_w2_down, v_g_final):
    nb, seq, d = x.shape
    xi, yi, ci = lax.axis_index("x"), lax.axis_index("y"), lax.axis_index("c")
    chip = 2 * xi + yi
    ada_cols = w_ada.shape[-1]

    c_pad = jnp.zeros((8, d), F32).at[:nb].set(c)
    b_shard = lax.dynamic_slice(b_ada, (0, chip * ada_cols), (1, ada_cols))
    shards = dict(gu1=jnp.stack([w1_gate[0], w1_up[0]]), d1=w1_down, win=w_in, wout=w_out,
                  gu2=jnp.stack([w2_gate[0], w2_up[0]]), d2=w2_down)
    bufs = {k: lax.dynamic_update_slice(lax.empty((N_CHIPS,) + s.shape, BF), s.astype(BF)[None],
                                        (chip if k in ("win", "wout") else 0, 0, 0, 0))
            for k, s in shards.items()}
    c_all, mod_blk, bufs["gu1"] = _ada_fwd(c_pad, w_ada[0], b_shard,
                                           carry=_ag_weights([bufs["gu1"]], relative=True))
    mod = jnp.transpose(mod_blk[:, :nb, :], (1, 0, 2)).reshape(nb, N_MOD, d)

    gains = dict(g_ffn1=g_ffn1, g_mix=g_mix, g_ffn2=g_ffn2, g_final=g_final.reshape(1, d),
                 g_sb_out=g_sb_out.reshape(1, D_GRP), g_dil_out=g_dil_out.reshape(1, D_GRP))
    place = jnp.stack([ci, chip]).astype(jnp.int32)
    r = _local_step(x, mod, loss_target, bufs, gains, rel_bias, place)

    dmod = r["dmod"]
    dmod_pad = jnp.zeros((8, N_MOD * d), F32).at[:nb].set(dmod)
    dmod_blk = jnp.transpose(dmod_pad.reshape(8, N_CHIPS, ada_cols), (1, 0, 2))
    small_parts = dict(b_ada=_rowsum8(dmod_pad), g_ffn1=r["dg_ffn1"], g_mix=r["dg_mix"], g_ffn2=r["dg_ffn2"],
                       g_final=r["dg_final"], g_sb_out=r["dg_sb"], g_dil_out=r["dg_dil"], rel_bias=r["drel"])
    halves1, parts_m = r["pending"]
    res = _small_sync(_pack_small(small_parts, r["loss"]), dmod_blk, c_all,
                      carry=_join([_rs_final(halves1), _rs_ici(parts_m)]))
    small_sum, g_wada, gffn1_gu, gffn1_dn = res[:4]
    halves_m = [_sum_chips(place, p, l) for p, l in zip(parts_m, res[4:6])]
    gwin, gwout = _alone("rs_last", _rs_final(halves_m))
    gffn2 = r["dffn2"]

    small_w = dict(b_ada=b_ada, g_ffn1=g_ffn1, g_mix=g_mix, g_ffn2=g_ffn2, g_final=g_final,
                   g_sb_out=g_sb_out, g_dil_out=g_dil_out, rel_bias=rel_bias)
    small_m = dict(b_ada=m_b_ada, g_ffn1=m_g_ffn1, g_mix=m_g_mix, g_ffn2=m_g_ffn2, g_final=m_g_final,
                   g_sb_out=m_g_sb_out, g_dil_out=m_g_dil_out, rel_bias=m_rel_bias)
    small_v = dict(b_ada=v_b_ada, g_ffn1=v_g_ffn1, g_mix=v_g_mix, g_ffn2=v_g_ffn2, g_final=v_g_final,
                   g_sb_out=v_g_sb_out, g_dil_out=v_g_dil_out, rel_bias=v_rel_bias)
    shapes = {k: v.shape for k, v in small_w.items()}
    sg, sd, sm, sv = _adamw(_pack_small(small_w), small_sum.reshape(1, SMALL_ROWS, 128), 0,
                            _pack_small(small_m), _pack_small(small_v))
    sg, loss = _unpack_small(sg, shapes)
    sd, _ = _unpack_small(sd, shapes)
    sm, _ = _unpack_small(sm, shapes)
    sv, _ = _unpack_small(sv, shapes)

    big = {}

    def upd(name, w, g_arr, sel, m, v, transposed=False):
        swap = (lambda a: jnp.swapaxes(a, -1, -2)) if transposed else (lambda a: a)
        w2, m2, v2 = [swap(a)[0] for a in (w, m, v)]
        big[name] = [swap(a[None]) for a in _adamw(w2, g_arr, sel, m2, v2)]

    upd("w_ada", w_ada, g_wada.reshape(1, d, ada_cols), 0, m_w_ada, v_w_ada)
    upd("w1_gate", w1_gate, gffn1_gu, 0, m_w1_gate, v_w1_gate, transposed=True)
    upd("w1_up", w1_up, gffn1_gu, 1, m_w1_up, v_w1_up, transposed=True)
    upd("w1_down", w1_down, gffn1_dn, 0, m_w1_down, v_w1_down)
    upd("w_in", w_in, gwin, 0, m_w_in, v_w_in)
    upd("w_out", w_out, gwout, 0, m_w_out, v_w_out)
    upd("w2_gate", w2_gate, gffn2, 0, m_w2_gate, v_w2_gate, transposed=True)
    upd("w2_up", w2_up, gffn2, 1, m_w2_up, v_w2_up, transposed=True)
    upd("w2_down", w2_down, gffn2, 2, m_w2_down, v_w2_down)

    names = ["w_ada", "b_ada", "g_ffn1", "w1_gate", "w1_up", "w1_down", "g_mix", "w_in", "g_sb_out", "g_dil_out",
             "w_out", "rel_bias", "g_ffn2", "w2_gate", "w2_up", "w2_down", "g_final"]
    outs = [loss, r["grad_x"]]
    for k, small in enumerate((sg, sd, sm, sv)):
        for name in names:
            outs.append(big[name][k] if name in big else small[name])
    return tuple(outs)
```

```python
import math

import numpy as np
import jax
import jax.numpy as jnp
from jax import lax
from jax.experimental import pallas as pl
from jax.experimental.pallas import tpu as pltpu

F32 = jnp.float32
BF = jnp.bfloat16
MESH = pl.DeviceIdType.MESH

HEAD_DIM = 64
N_HEADS = 8
D_GRP = N_HEADS * HEAD_DIM
DIL_CONFIGS = ((128, 1), (512, 4), (2048, 16))
N_STEPS = 128
BLOCK = 128
N_BUCKETS = 32
MAX_DISTANCE = 2048
N_MOD = 9
EPS = 1e-6
NEG_INF = -1e30
SCALE = HEAD_DIM ** -0.5

ADAM_LR = 0.001
ADAM_B1 = 0.9
ADAM_B2 = 0.999
ADAM_EPS = 1e-08
ADAM_WD = 0.01
ADAM_STEP = 10

N_CHIPS = 4
N_DEV = 8
VMEM_LIMIT = 56 * 1024 * 1024
TM = 512
TQ = 256
KB = 256
SMALL_ROWS = 120


def _cp(n_axes=0, **kw):
    sem = ("arbitrary",) * n_axes if n_axes else None
    return pltpu.CompilerParams(dimension_semantics=sem, vmem_limit_bytes=VMEM_LIMIT, **kw)


def _nn(a, b):
    return jnp.dot(a, b, preferred_element_type=F32)


def _nt(a, b):
    return lax.dot_general(a, b, (((1,), (1,)), ((), ())), preferred_element_type=F32)


def _tn(a, b):
    return lax.dot_general(a, b, (((0,), (0,)), ((), ())), preferred_element_type=F32)


def _twice(m):
    return jnp.concatenate([m, m], axis=0)


def _nn2(x, m2):
    hi = x.astype(BF)
    lo = (x - hi.astype(F32)).astype(BF)
    return _nn(jnp.concatenate([hi, lo], axis=1), m2)


def _softplus(z):
    return jnp.maximum(z, 0.0) + jnp.log(1.0 + jnp.exp(-jnp.abs(z)))


def _sds(shape, dtype):
    return jax.ShapeDtypeStruct(shape, dtype)


def _whole(a):
    nd = a.ndim
    return pl.BlockSpec(a.shape, lambda *_: (0,) * nd, pipeline_mode=pl.Buffered(1))


def _modnorm_bwd_tile(dh, xv, gv, scv, dxo):
    r = lax.rsqrt(jnp.mean(xv * xv, axis=-1, keepdims=True) + EPS)
    n = xv * r
    ng = n * gv
    dsh = jnp.sum(dh, axis=0, keepdims=True)
    dsc = jnp.sum(dh * ng, axis=0, keepdims=True)
    dy = dh * (1.0 + scv)
    dg = jnp.sum(dy * n, axis=0, keepdims=True)
    dn = dy * gv
    dx = dxo + r * (dn - n * jnp.mean(dn * n, axis=-1, keepdims=True))
    return dx, dsh, dsc, dg


def _acc_rows(ref, val, first):
    @pl.when(first)
    def _():
        ref[...] = val

    @pl.when(jnp.logical_not(first))
    def _():
        ref[...] += val


def _modnorm_tile(x_ref, g_ref, sc_ref, sh_ref):
    xv = x_ref[...]
    r = lax.rsqrt(jnp.mean(xv * xv, axis=-1, keepdims=True) + EPS)
    return (((xv * r) * g_ref[...]) * (1.0 + sc_ref[...]) + sh_ref[...]).astype(BF)


def _ffn_up(x, g, sc, sh, wgu, seq, carry=None):
    t, d = x.shape
    fs = wgu.shape[-1]
    per = seq // TM

    def body(x_ref, g_ref, sc_ref, sh_ref, w_ref, h_ref, p_ref, q_ref, s_ref):
        hv = _modnorm_tile(x_ref, g_ref, sc_ref, sh_ref)
        h_ref[...] = hv
        for j in range(N_CHIPS):
            a = _nn(hv, w_ref[j, 0])
            u = _nn(hv, w_ref[j, 1])
            sig = jax.nn.sigmoid(a)
            q = a * sig
            p_ref[j] = (u * (sig * (1.0 + a * (1.0 - sig)))).astype(BF)
            q_ref[j] = q.astype(BF)
            s_ref[j] = (q * u).astype(BF)

    row = pl.BlockSpec((TM, d), lambda m: (m, 0))
    ex = pl.BlockSpec((None, 1, d), lambda m: (m // per, 0, 0))
    blk = pl.BlockSpec((N_CHIPS, TM, fs), lambda m: (0, m, 0))
    return _call(
        body, "ffn_up", (t // TM,),
        [row, pl.BlockSpec((1, d), lambda m: (0, 0)), ex, ex, _whole(wgu)],
        [row, blk, blk, blk],
        [_sds((t, d), BF)] + [_sds((N_CHIPS, t, fs), BF)] * 3,
        (x, g, sc, sh, wgu), carry=carry)


def _ffn_down(s, wd, x, gt, seq, coef, carry=None):
    _, t, fs = s.shape
    d = x.shape[-1]
    per = seq // TM

    def body(s_ref, w_ref, x_ref, gt_ref, f_ref, xo_ref):
        f = _nn(s_ref[0], w_ref[0, 0])
        for j in range(1, N_CHIPS):
            f = f + _nn(s_ref[j], w_ref[j, 0])
        f_ref[...] = f.astype(BF)
        xo_ref[...] = x_ref[...] + (coef * gt_ref[...]) * f

    row = pl.BlockSpec((TM, d), lambda m: (m, 0))
    return _call(
        body, "ffn_down", (t // TM,),
        [pl.BlockSpec((N_CHIPS, TM, fs), lambda m: (0, m, 0)), _whole(wd), row,
         pl.BlockSpec((None, 1, d), lambda m: (m // per, 0, 0))],
        [row, row], [_sds((t, d), BF), _sds((t, d), F32)], (s, wd, x, gt), carry=carry)


def _ffn_bwd_ds(dxo, gt, f, wd, p, q, seq, coef, carry=None):
    t, d = dxo.shape
    fs = p.shape[-1]
    per = seq // TM
    nb = t // seq

    def body(dxo_ref, gt_ref, f_ref, w_ref, p_ref, q_ref, da_ref, du_ref, df_ref, dgt_ref):
        m = pl.program_id(0)
        dxv = dxo_ref[...]
        df = ((coef * gt_ref[...]) * dxv).astype(BF)
        df_ref[...] = df
        _acc_rows(dgt_ref, coef * jnp.sum(dxv * f_ref[...].astype(F32), axis=0, keepdims=True), m % per == 0)
        for j in range(N_CHIPS):
            ds = _nt(df, w_ref[j, 0])
            da_ref[j] = (ds * p_ref[j].astype(F32)).astype(BF)
            du_ref[j] = (ds * q_ref[j].astype(F32)).astype(BF)

    row = pl.BlockSpec((TM, d), lambda m: (m, 0))
    blk = pl.BlockSpec((N_CHIPS, TM, fs), lambda m: (0, m, 0))
    ex = pl.BlockSpec((None, 1, d), lambda m: (m // per, 0, 0))
    return _call(
        body, "ffn_bwd_ds", (t // TM,),
        [row, ex, row, _whole(wd), blk, blk],
        [blk, blk, row, ex],
        [_sds((N_CHIPS, t, fs), BF), _sds((N_CHIPS, t, fs), BF), _sds((t, d), BF), _sds((nb, 1, d), F32)],
        (dxo, gt, f, wd, p, q), carry=carry)


TM_X = 256


def _ffn_bwd_x(dxo, gt, f, wd, p, q, wgu, x, g, sc, seq, coef):
    t, d = dxo.shape
    fs = p.shape[-1]
    per = seq // TM_X
    nb = t // seq

    def body(dxo_ref, gt_ref, f_ref, wd_ref, p_ref, q_ref, w_ref, x_ref, g_ref, sc_ref,
             da_ref, du_ref, df_ref, dgt_ref, dx_ref, dsh_ref, dsc_ref, dg_ref):
        m = pl.program_id(0)
        dxv = dxo_ref[...]
        df = ((coef * gt_ref[...]) * dxv).astype(BF)
        df_ref[...] = df
        _acc_rows(dgt_ref, coef * jnp.sum(dxv * f_ref[...].astype(F32), axis=0, keepdims=True), m % per == 0)
        dh = None
        for j in range(N_CHIPS):
            ds = _nt(df, wd_ref[j, 0])
            da = (ds * p_ref[j].astype(F32)).astype(BF)
            du = (ds * q_ref[j].astype(F32)).astype(BF)
            da_ref[j] = da
            du_ref[j] = du
            part = _nt(da, w_ref[j, 0]) + _nt(du, w_ref[j, 1])
            dh = part if dh is None else dh + part
        dx, dsh, dsc, dg = _modnorm_bwd_tile(dh, x_ref[...], g_ref[...], sc_ref[...], dxv)
        dx_ref[...] = dx
        _acc_rows(dsh_ref, dsh, m % per == 0)
        _acc_rows(dsc_ref, dsc, m % per == 0)
        _acc_rows(dg_ref, dg, m == 0)

    row = pl.BlockSpec((TM_X, d), lambda m: (m, 0))
    blk = pl.BlockSpec((N_CHIPS, TM_X, fs), lambda m: (0, m, 0))
    ex = pl.BlockSpec((None, 1, d), lambda m: (m // per, 0, 0))
    vec = pl.BlockSpec((1, d), lambda m: (0, 0))
    exs = _sds((nb, 1, d), F32)
    return pl.pallas_call(
        body, name="ffn_bwd_x", grid=(t // TM_X,),
        in_specs=[row, ex, row, _whole(wd), blk, blk, _whole(wgu), row, vec, ex],
        out_specs=[blk, blk, row, ex, row, ex, ex, vec],
        out_shape=[_sds((N_CHIPS, t, fs), BF), _sds((N_CHIPS, t, fs), BF), _sds((t, d), BF), exs,
                   _sds((t, d), F32), exs, exs, _sds((1, d), F32)],
        compiler_params=_cp(1))(dxo, gt, f, wd, p, q, wgu, x, g, sc)


TK_W = 1024


def _ffn_bwd_w(h, da, du, s, df, terms=(0, 1, 2), carry=None):
    t, d = h.shape
    fs = da.shape[-1]
    row = pl.BlockSpec((TK_W, d), lambda j, kt: (kt, 0))
    blk = pl.BlockSpec((None, TK_W, fs), lambda j, kt: (j, kt, 0))
    args, specs, idx = [], [], []

    def operand(a, spec):
        for i, o in enumerate(args):
            if o is a:
                return i
        args.append(a)
        specs.append(spec)
        return len(args) - 1

    for term in terms:
        lhs, rhs = ((da, h), (du, h), (s, df))[term]
        idx.append((operand(lhs, blk), operand(rhs, row)))

    def body(*refs):
        o_ref = refs[-1]
        kt = pl.program_id(1)
        parts = [_tn(refs[a][...], refs[b][...]) for a, b in idx]

        @pl.when(kt == 0)
        def _():
            for i, p in enumerate(parts):
                o_ref[i] = p

        @pl.when(kt != 0)
        def _():
            for i, p in enumerate(parts):
                o_ref[i] += p

    return _call(
        body, "ffn_bwd_w", (N_CHIPS, t // TK_W), specs,
        [pl.BlockSpec((None, len(terms), fs, d), lambda j, kt: (j, 0, 0, 0))],
        [_sds((N_CHIPS, len(terms), fs, d), F32)], args, carry=carry)


def _ffn_bwd_dh(da, du, wgu, x, g, sc, dxo, seq, carry=None):
    _, t, fs = da.shape
    d = x.shape[-1]
    per = seq // TM
    nb = t // seq

    def body(da_ref, du_ref, w_ref, x_ref, g_ref, sc_ref, dxo_ref, dx_ref, dsh_ref, dsc_ref, dg_ref):
        m = pl.program_id(0)
        dh = _nt(da_ref[0], w_ref[0, 0]) + _nt(du_ref[0], w_ref[0, 1])
        for j in range(1, N_CHIPS):
            dh = dh + _nt(da_ref[j], w_ref[j, 0]) + _nt(du_ref[j], w_ref[j, 1])
        dx, dsh, dsc, dg = _modnorm_bwd_tile(dh, x_ref[...], g_ref[...], sc_ref[...], dxo_ref[...])
        dx_ref[...] = dx
        _acc_rows(dsh_ref, dsh, m % per == 0)
        _acc_rows(dsc_ref, dsc, m % per == 0)
        _acc_rows(dg_ref, dg, m == 0)

    row = pl.BlockSpec((TM, d), lambda m: (m, 0))
    blk = pl.BlockSpec((N_CHIPS, TM, fs), lambda m: (0, m, 0))
    ex = pl.BlockSpec((None, 1, d), lambda m: (m // per, 0, 0))
    vec = pl.BlockSpec((1, d), lambda m: (0, 0))
    return _call(
        body, "ffn_bwd_dh", (t // TM,),
        [blk, blk, _whole(wgu), row, vec, ex, row],
        [row, ex, ex, vec],
        [_sds((t, d), F32), _sds((nb, 1, d), F32), _sds((nb, 1, d), F32), _sds((1, d), F32)],
        (da, du, wgu, x, g, sc, dxo), carry=carry)


def _qkv_proj(x, g, sc, sh, w_in, seq, carry=None):
    t, d = x.shape
    wc = w_in.shape[-1]
    per = seq // TM

    dils = [dil for _, dil in DIL_CONFIGS if dil > 1]

    def body(x_ref, g_ref, sc_ref, sh_ref, w_ref, h_ref, o_ref, *rest):
        res_refs, buf = rest[:len(dils)], rest[len(dils)]
        hv = _modnorm_tile(x_ref, g_ref, sc_ref, sh_ref)
        h_ref[...] = hv
        for j in range(N_CHIPS):
            rf = _nn(hv, w_ref[j, 0])
            r = rf.astype(BF)
            for a, lc, off, width in _col_pieces(j, wc):
                o_ref[a, :, lc:lc + width] = r[:, off:off + width]
                if a < 3:
                    continue
                for c0 in range(0, width, 128):
                    cg = (lc + c0) // 128
                    buf[...] = rf[:, off + c0:off + c0 + 128]
                    for ref, dil in zip(res_refs, dils):
                        for rr in range(dil):
                            ref[a - 3, :, rr * D_GRP + cg * 128:rr * D_GRP + (cg + 1) * 128] = (
                                buf[pl.ds(rr, TM // dil, stride=dil), :].astype(BF))

    row = pl.BlockSpec((TM, d), lambda m: (m, 0))
    ex = pl.BlockSpec((None, 1, d), lambda m: (m // per, 0, 0))
    return _call(
        body, "qkv_proj", (t // TM,),
        [row, pl.BlockSpec((1, d), lambda m: (0, 0)), ex, ex, _whole(w_in)],
        [row, pl.BlockSpec((6, TM, D_GRP), lambda m: (0, m, 0))]
        + [pl.BlockSpec((3, TM // dil, dil * D_GRP), lambda m: (0, m, 0)) for dil in dils],
        [_sds((t, d), BF), _sds((6, t, D_GRP), BF)] + [_sds((3, t // dil, dil * D_GRP), BF) for dil in dils],
        (x, g, sc, sh, w_in), scratch=[pltpu.VMEM((TM, 128), F32)], carry=carry)


def _col_pieces(j, wc):
    out, off = [], 0
    while off < wc:
        a, lc = divmod(j * wc + off, D_GRP)
        width = min(D_GRP - lc, wc - off)
        out.append((a, lc, off, width))
        off += width
    return out


def _chip_cols(g6_ref, j, wc):
    return jnp.concatenate([g6_ref[a, :, lc:lc + width] for a, lc, _, width in _col_pieces(j, wc)], axis=1)


def _mix_out(on_sb, on_dil, w_out, x, gt, seq):
    t, d = x.shape
    per = seq // TM

    def body(a_ref, b_ref, w_ref, x_ref, gt_ref, t_ref, xo_ref):
        tv = _nn(a_ref[...], w_ref[0:D_GRP, :]) + _nn(b_ref[...], w_ref[D_GRP:2 * D_GRP, :])
        t_ref[...] = tv.astype(BF)
        xo_ref[...] = x_ref[...] + gt_ref[...] * tv

    row = pl.BlockSpec((TM, d), lambda m: (m, 0))
    half = pl.BlockSpec((TM, D_GRP), lambda m: (m, 0))
    return pl.pallas_call(
        body, name="mix_out", grid=(t // TM,),
        in_specs=[half, half, pl.BlockSpec((2 * D_GRP, d), lambda m: (0, 0)), row,
                  pl.BlockSpec((None, 1, d), lambda m: (m // per, 0, 0))],
        out_specs=[row, row],
        out_shape=[_sds((t, d), BF), _sds((t, d), F32)],
        compiler_params=_cp(1))(on_sb, on_dil, w_out, x, gt)


def _sb_masks():
    lane = lax.broadcasted_iota(jnp.int32, (1, 2 * HEAD_DIM), 1)
    hm0 = lane < HEAD_DIM
    rel = lax.broadcasted_iota(jnp.int32, (TQ, KB), 0) - lax.broadcasted_iota(jnp.int32, (TQ, KB), 1)
    kr = lax.broadcasted_iota(jnp.int32, (KB, KB), 0)
    kc = lax.broadcasted_iota(jnp.int32, (KB, KB), 1)
    return hm0, rel, kr, kc


def _headnorm_pair(o, gv, hm0):
    o2 = o * o
    ms0 = jnp.sum(jnp.where(hm0, o2, 0.0), axis=-1, keepdims=True) * (1.0 / HEAD_DIM)
    ms1 = jnp.sum(jnp.where(hm0, 0.0, o2), axis=-1, keepdims=True) * (1.0 / HEAD_DIM)
    r = jnp.where(hm0, lax.rsqrt(ms0 + EPS), lax.rsqrt(ms1 + EPS))
    return (o * r) * gv


SB_DEAD = -104.0


def _alive(c_l):
    return (jnp.max(c_l) > SB_DEAD).astype(jnp.int32)


def _sb_fwd(qkv6, g_sb, nb, seq, carry=None):
    nq = seq // TQ

    def body(q_ref, k_ref, v_ref, g_ref, o_ref, on_ref):
        qi = pl.program_id(2)
        hm0, rel, kr, kc = _sb_masks()
        upper = _twice((kr > kc).astype(BF))
        heads = _dil_masks()[0]
        qs = _stack_heads(q_ref[...] * SCALE, heads)
        causal2 = jnp.concatenate([rel] * GRP_HEADS, axis=0) > 0

        def block(kj, causal, c_l, acc):
            ks = pl.multiple_of(kj * KB, KB)
            z = _nt(qs, k_ref[pl.ds(ks, KB), :])
            sp = _softplus(z)
            spm = sp if causal is None else jnp.where(causal, sp, 0.0)
            suf = _nn2(spm, upper)
            w = jnp.exp((z - sp) + (c_l - suf))
            if causal is not None:
                w = jnp.where(causal, w, 0.0)
            return c_l - (suf[:, 0:1] + spm[:, 0:1]), acc + _nn(w.astype(BF), v_ref[pl.ds(ks, KB), :])

        c_l, acc = block(qi, causal2, jnp.zeros((GRP_HEADS * TQ, 1), F32), jnp.zeros((GRP_HEADS * TQ, GRP_W), F32))

        def cond(carry):
            return jnp.logical_and(carry[0] <= qi, carry[1] > 0)

        def kbody(carry):
            it, _, c_l, acc = carry
            c_l, acc = block(qi - it, None, c_l, acc)
            return it + 1, _alive(c_l), c_l, acc

        acc = lax.while_loop(cond, kbody, (jnp.int32(1), _alive(c_l), c_l, acc))[3]
        o = _unstack_heads(acc, heads, TQ)
        o_ref[...] = o.astype(BF)
        gv = g_ref[...]
        for half in range(GRP_W // 128):
            lanes = slice(half * 128, (half + 1) * 128)
            on_ref[:, lanes] = _headnorm_pair(o[:, lanes], gv[:, lanes], hm0).astype(BF)

    w = GRP_W
    full = lambda i: pl.BlockSpec((None, None, seq, w), lambda b, hp, q: (i, b, 0, hp))
    qblk = pl.BlockSpec((None, None, TQ, w), lambda b, hp, q: (0, b, q, hp))
    oblk = pl.BlockSpec((None, TQ, w), lambda b, hp, q: (b, q, hp))
    return _call(
        body, "sb_fwd", (nb, N_HEADS // GRP_HEADS, nq),
        [qblk, full(1), full(2), pl.BlockSpec((1, w), lambda b, hp, q: (0, hp))],
        [oblk, oblk],
        [_sds((nb, seq, D_GRP), BF), _sds((nb, seq, D_GRP), BF)],
        (qkv6, qkv6, qkv6, g_sb), carry=carry)


def _sb_bwd(qkv6, do, nb, seq, carry=None):
    nq = seq // TQ
    nk = seq // KB

    def body(q_ref, k_ref, v_ref, do_ref, out_ref, dk_acc, dv_acc, g_st, s_st):
        qi = pl.program_id(2)
        hm0, rel, kr, kc = _sb_masks()
        upper = _twice((kr > kc).astype(BF))
        lower = (kr < kc).astype(BF)

        @pl.when(qi == 0)
        def _():
            dk_acc[...] = jnp.zeros_like(dk_acc)
            dv_acc[...] = jnp.zeros_like(dv_acc)

        heads = _dil_masks()[0]
        qs = _stack_heads(q_ref[...] * SCALE, heads)
        dos = _stack_heads(do_ref[...], heads)
        causal2 = jnp.concatenate([rel] * GRP_HEADS, axis=0) > 0

        def weights(kj, causal, c_l):
            ks = pl.multiple_of(kj * KB, KB)
            vb = v_ref[pl.ds(ks, KB), :]
            z = _nt(qs, k_ref[pl.ds(ks, KB), :])
            sp = _softplus(z)
            spm = sp if causal is None else jnp.where(causal, sp, 0.0)
            suf = _nn2(spm, upper)
            lsz = z - sp
            w = jnp.exp(lsz + (c_l - suf))
            if causal is not None:
                w = jnp.where(causal, w, 0.0)
            g_st[kj] = w * _nt(dos, vb)
            s_st[kj] = jnp.exp(lsz)
            dv_acc[pl.ds(ks, KB), :] += _tn(w.astype(BF), dos)
            return c_l - (suf[:, 0:1] + spm[:, 0:1])

        zc = jnp.zeros((GRP_HEADS * TQ, 1), F32)
        c_l = weights(qi, causal2, zc)

        def acond(carry):
            return jnp.logical_and(carry[0] <= qi, carry[1] > 0)

        def abody(carry):
            c_l = weights(qi - carry[0], None, carry[2])
            return carry[0] + 1, _alive(c_l), c_l

        n_used = lax.while_loop(acond, abody, (jnp.int32(1), _alive(c_l), c_l))[0]

        def grads(kj, causal, c_g, dq):
            ks = pl.multiple_of(kj * KB, KB)
            kb = k_ref[pl.ds(ks, KB), :]
            g = g_st[kj]
            sig = s_st[kj]
            pre = _nn(g.astype(BF), lower)
            dz = g * (1.0 - sig) - sig * (pre + c_g)
            if causal is not None:
                dz = jnp.where(causal, dz, 0.0)
            dzb = dz.astype(BF)
            dk_acc[pl.ds(ks, KB), :] += _tn(dzb, qs)
            return c_g + (pre[:, KB - 1:KB] + g[:, KB - 1:KB]), dq + _nn(dzb, kb)

        c_g, dq = lax.fori_loop(qi - n_used + 1, qi, lambda kj, cr: grads(kj, None, *cr),
                                (zc, jnp.zeros((GRP_HEADS * TQ, GRP_W), F32)))
        _, dq = grads(qi, causal2, c_g, dq)
        dq = _unstack_heads(dq, heads, TQ) * SCALE
        out_ref[0, pl.ds(pl.multiple_of(qi * TQ, TQ), TQ), :] = dq.astype(BF)

        @pl.when(qi == nq - 1)
        def _():
            out_ref[1] = dk_acc[...].astype(BF)
            out_ref[2] = dv_acc[...].astype(BF)

    w = GRP_W
    full = lambda i: pl.BlockSpec((None, None, seq, w), lambda b, hp, q: (i, b, 0, hp))
    qblk = pl.BlockSpec((None, None, TQ, w), lambda b, hp, q: (0, b, q, hp))
    oblk = pl.BlockSpec((None, TQ, w), lambda b, hp, q: (b, q, hp))
    return _call(
        body, "sb_bwd", (nb, N_HEADS // GRP_HEADS, nq),
        [qblk, full(1), full(2), oblk],
        [pl.BlockSpec((3, None, seq, w), lambda b, hp, q: (0, b, 0, hp))],
        [_sds((6, nb, seq, D_GRP), BF)], (qkv6, qkv6, qkv6, do),
        scratch=[pltpu.VMEM((seq, w), F32), pltpu.VMEM((seq, w), F32),
                 pltpu.VMEM((nk, GRP_HEADS * TQ, KB), F32), pltpu.VMEM((nk, GRP_HEADS * TQ, KB), F32)],
        carry=carry)


def _t5_bucket(n):
    max_exact = N_BUCKETS // 2
    nf = np.maximum(n, 1).astype(np.float32)
    large = max_exact + (np.log(nf / max_exact) / math.log(MAX_DISTANCE / max_exact)
                         * (N_BUCKETS - max_exact)).astype(np.int32)
    large = np.minimum(large, N_BUCKETS - 1)
    return np.where(n < max_exact, n, large).astype(np.int32)


def _bucket_map(dilation):
    step = BLOCK + np.arange(BLOCK)[:, None] - np.arange(2 * BLOCK)[None, :]
    return _t5_bucket(np.clip(step, 0, N_STEPS) * dilation)


GRP_HEADS = 4
GRP_W = GRP_HEADS * HEAD_DIM


def _dil_masks():
    lane = lax.broadcasted_iota(jnp.int32, (1, GRP_W), 1)
    heads = [jnp.logical_and(lane >= HEAD_DIM * i, lane < HEAD_DIM * (i + 1)) for i in range(GRP_HEADS)]
    iq = jnp.bitwise_and(lax.broadcasted_iota(jnp.int32, (GRP_HEADS * BLOCK, BLOCK), 0), BLOCK - 1)
    ik = lax.broadcasted_iota(jnp.int32, (GRP_HEADS * BLOCK, BLOCK), 1)
    return heads, ik <= iq, ik >= iq


def _stack_heads(x, heads):
    zero = jnp.zeros_like(x)
    return jnp.concatenate([jnp.where(hm, x, zero) for hm in heads], axis=0)


def _unstack_heads(xs, heads, rows=BLOCK):
    out = xs[0:rows]
    for i in range(1, GRP_HEADS):
        out = jnp.where(heads[i], xs[i * rows:(i + 1) * rows], out)
    return out


def _dil_rows(n):
    rs = pl.multiple_of(n * BLOCK, BLOCK)
    ps = pl.multiple_of(jnp.maximum(n - 1, 0) * BLOCK, BLOCK)
    return pl.ds(rs, BLOCK), pl.ds(ps, BLOCK)


def _dil_probs(qs, kc, kp, b_ref, gi, valid_c, valid_p):
    rows = slice(gi * GRP_HEADS * BLOCK, (gi + 1) * GRP_HEADS * BLOCK)
    zc = _nt(qs, kc) * SCALE + b_ref[rows, BLOCK:2 * BLOCK]
    zp = _nt(qs, kp) * SCALE + b_ref[rows, 0:BLOCK]
    zc = jnp.where(valid_c, zc, NEG_INF)
    zp = jnp.where(valid_p, zp, NEG_INF)
    m = jnp.maximum(jnp.max(zc, axis=-1, keepdims=True), jnp.max(zp, axis=-1, keepdims=True))
    ec = jnp.exp(zc - m)
    ep = jnp.exp(zp - m)
    den = jnp.sum(ec, axis=-1, keepdims=True) + jnp.sum(ep, axis=-1, keepdims=True)
    return ec, ep, den, m


def _dil_fwd(qkv6r, base, bias, nb, sub_len, dilation):
    n_blk = sub_len // BLOCK

    def body(q_ref, k_ref, v_ref, b_ref, o_ref, l_ref):
        heads, valid_c, valid_p0 = _dil_masks()

        def nbody(n, carry):
            cur, prev = _dil_rows(n)
            valid_p = jnp.logical_and(valid_p0, n > 0)
            for gi in range(N_HEADS // GRP_HEADS):
                lanes = slice(gi * GRP_W, (gi + 1) * GRP_W)
                qs = _stack_heads(q_ref[cur, lanes], heads)
                ec, ep, den, m = _dil_probs(qs, k_ref[cur, lanes], k_ref[prev, lanes], b_ref, gi, valid_c, valid_p)
                o = (_nn(ec.astype(BF), v_ref[cur, lanes]) + _nn(ep.astype(BF), v_ref[prev, lanes])) / den
                o_ref[cur, lanes] = _unstack_heads(o, heads).astype(BF)
                l_ref[cur, lanes] = _unstack_heads(jnp.broadcast_to(m + jnp.log(den), o.shape), heads)
            return carry

        lax.fori_loop(0, n_blk, nbody, 0)

    seqblk = lambda i: pl.BlockSpec((None, None, sub_len, D_GRP), lambda b, r: (i, b, 0, r))
    oblk = pl.BlockSpec((None, sub_len, D_GRP), lambda b, r: (b, 0, r))
    shp = _sds((nb, sub_len, dilation * D_GRP), F32)
    return pl.pallas_call(
        body, name="dil_fwd_%d" % dilation, grid=(nb, dilation),
        in_specs=[seqblk(base), seqblk(base + 1), seqblk(base + 2), _whole(bias)],
        out_specs=[oblk, oblk], out_shape=[_sds(shp.shape, BF), shp],
        compiler_params=_cp(2))(qkv6r, qkv6r, qkv6r, bias)


def _dil_bwd(qkv6r, base, bias, do_c, dd_c, nb, sub_len, dilation, carry=None):
    n_blk = sub_len // BLOCK

    def body(q_ref, k_ref, v_ref, b_ref, do_ref, dd_ref, out_ref, a_ref, dk_acc, dv_acc):
        heads, valid_c, valid_p0 = _dil_masks()
        first = jnp.logical_and(pl.program_id(0) == 0, pl.program_id(1) == 0)

        @pl.when(first)
        def _():
            a_ref[...] = jnp.zeros_like(a_ref)

        dk_acc[...] = jnp.zeros_like(dk_acc)
        dv_acc[...] = jnp.zeros_like(dv_acc)

        def nbody(n, carry):
            cur, prev = _dil_rows(n)
            valid_p = jnp.logical_and(valid_p0, n > 0)
            for gi in range(N_HEADS // GRP_HEADS):
                lanes = slice(gi * GRP_W, (gi + 1) * GRP_W)
                kc, kp = k_ref[cur, lanes], k_ref[prev, lanes]
                vc, vp = v_ref[cur, lanes], v_ref[prev, lanes]
                qs = _stack_heads(q_ref[cur, lanes], heads)
                dos = _stack_heads(do_ref[cur, lanes], heads).astype(BF)
                dds = jnp.sum(_stack_heads(dd_ref[cur, lanes], heads), axis=-1, keepdims=True) * (1.0 / HEAD_DIM)
                ec, ep, den, _ = _dil_probs(qs, kc, kp, b_ref, gi, valid_c, valid_p)
                inv = 1.0 / den
                pc = ec * inv
                pp = ep * inv
                dzc = pc * (_nt(dos, vc) + dds)
                dzp = pp * (_nt(dos, vp) + dds)
                rows = slice(gi * GRP_HEADS * BLOCK, (gi + 1) * GRP_HEADS * BLOCK)
                a_ref[rows, BLOCK:2 * BLOCK] += dzc
                a_ref[rows, 0:BLOCK] += dzp
                dzcb = (dzc * SCALE).astype(BF)
                dzpb = (dzp * SCALE).astype(BF)
                out_ref[0, cur, lanes] = _unstack_heads(_nn(dzcb, kc) + _nn(dzpb, kp), heads).astype(BF)
                dk_acc[cur, lanes] += _tn(dzcb, qs)
                dk_acc[prev, lanes] += _tn(dzpb, qs)
                dv_acc[cur, lanes] += _tn(pc.astype(BF), dos)
                dv_acc[prev, lanes] += _tn(pp.astype(BF), dos)
            return carry

        lax.fori_loop(0, n_blk, nbody, 0)
        out_ref[1] = dk_acc[...].astype(BF)
        out_ref[2] = dv_acc[...].astype(BF)

    seqblk = lambda i: pl.BlockSpec((None, None, sub_len, D_GRP), lambda b, r: (i, b, 0, r))
    oblk = pl.BlockSpec((None, sub_len, D_GRP), lambda b, r: (b, 0, r))
    return _call(
        body, "dil_bwd_%d" % dilation, (nb, dilation),
        [seqblk(base), seqblk(base + 1), seqblk(base + 2), _whole(bias), oblk, oblk],
        [pl.BlockSpec((3, None, sub_len, D_GRP), lambda b, r: (0, b, 0, r)),
         pl.BlockSpec((N_HEADS * BLOCK, 2 * BLOCK), lambda b, r: (0, 0))],
        [_sds((3, nb, sub_len, dilation * D_GRP), BF), _sds((N_HEADS * BLOCK, 2 * BLOCK), F32)],
        (qkv6r, qkv6r, qkv6r, bias, do_c, dd_c),
        scratch=[pltpu.VMEM((sub_len, D_GRP), F32)] * 2, carry=carry)


def _group_ones():
    idx = np.arange(D_GRP) // HEAD_DIM
    return jnp.asarray((idx[:, None] == idx[None, :]).astype(np.float32), dtype=BF)


def _dil_alphas(l1, l4, l16):
    mx = jnp.maximum(jnp.maximum(l1, l4), l16)
    e1 = jnp.exp(l1 - mx)
    e4 = jnp.exp(l4 - mx)
    e16 = jnp.exp(l16 - mx)
    den = e1 + e4 + e16
    return e1 / den, e4 / den, e16 / den


def _residue_spec(dil):
    return pl.BlockSpec((TM // dil, dil * D_GRP), lambda m: (m, 0))


def _from_residue(src, dil, cg, buf):
    if dil == 1:
        return src[:, cg * 128:(cg + 1) * 128].astype(F32)
    for r in range(dil):
        buf[pl.ds(r, TM // dil, stride=dil), :] = (
            src[:, r * D_GRP + cg * 128:r * D_GRP + (cg + 1) * 128].astype(F32))
    return buf[...]


def _to_residue(dst, dil, cg, buf, val):
    if dil == 1:
        dst[:, cg * 128:(cg + 1) * 128] = val.astype(dst.dtype)
        return
    buf[...] = val
    for r in range(dil):
        dst[:, r * D_GRP + cg * 128:r * D_GRP + (cg + 1) * 128] = (
            buf[pl.ds(r, TM // dil, stride=dil), :].astype(dst.dtype))


def _pair_sum(x, hm0):
    s0 = jnp.sum(jnp.where(hm0, x, 0.0), axis=-1, keepdims=True)
    s1 = jnp.sum(jnp.where(hm0, 0.0, x), axis=-1, keepdims=True)
    return jnp.where(hm0, s0, s1)


def _dil_comb(os, ls, g_dil):
    t = os[0].shape[0]
    dils = [dil for _, dil in DIL_CONFIGS]

    def body(o1, l1, o4, l4, o16, l16, g_ref, o_ref, on_ref, b0, b1, b2, b3):
        hm0 = lax.broadcasted_iota(jnp.int32, (1, 128), 1) < HEAD_DIM
        for cg in range(D_GRP // 128):
            lanes = slice(cg * 128, (cg + 1) * 128)
            ov = [_from_residue(src, dil, cg, buf) for src, dil, buf in zip((o1, o4, o16), dils, (None, b0, b1))]
            lv = [_from_residue(src, dil, cg, buf) for src, dil, buf in zip((l1, l4, l16), dils, (None, b2, b3))]
            a1, a4, a16 = _dil_alphas(*lv)
            o = a1 * ov[0] + a4 * ov[1] + a16 * ov[2]
            o_ref[:, lanes] = o.astype(BF)
            on_ref[:, lanes] = _headnorm_pair(o, g_ref[:, lanes], hm0).astype(BF)

    blk = pl.BlockSpec((TM, D_GRP), lambda m: (m, 0))
    specs = [_residue_spec(dil) for dil in dils for _ in range(2)]
    return pl.pallas_call(
        body, name="dil_comb", grid=(t // TM,),
        in_specs=specs + [pl.BlockSpec((1, D_GRP), lambda m: (0, 0))],
        out_specs=[blk, blk],
        out_shape=[_sds((t, D_GRP), BF), _sds((t, D_GRP), BF)],
        scratch_shapes=[pltpu.VMEM((TM, 128), F32)] * 4,
        compiler_params=_cp(1))(os[0], ls[0], os[1], ls[1], os[2], ls[2], g_dil)


def _dil_comb_bwd(do, os, ls):
    t = do.shape[0]
    dils = [dil for _, dil in DIL_CONFIGS]

    def body(do_ref, o1, l1, o4, l4, o16, l16, d1, d4, d16, e1, e4, e16, b0, b1, b2, b3):
        hm0 = lax.broadcasted_iota(jnp.int32, (1, 128), 1) < HEAD_DIM
        for cg in range(D_GRP // 128):
            dov = do_ref[:, cg * 128:(cg + 1) * 128].astype(F32)
            ov = [_from_residue(src, dil, cg, buf) for src, dil, buf in zip((o1, o4, o16), dils, (None, b0, b1))]
            lv = [_from_residue(src, dil, cg, buf) for src, dil, buf in zip((l1, l4, l16), dils, (None, b2, b3))]
            al = _dil_alphas(*lv)
            sbar = al[0] * _pair_sum(dov * ov[0], hm0)
            for a_c, o_c in zip(al[1:], ov[1:]):
                sbar = sbar + a_c * _pair_sum(dov * o_c, hm0)
            for a_c, dil, dref, eref in zip(al, dils, (d1, d4, d16), (e1, e4, e16)):
                _to_residue(dref, dil, cg, b0, a_c * dov)
                _to_residue(eref, dil, cg, b1, -a_c * sbar)

    specs = [_residue_spec(dil) for dil in dils]
    return pl.pallas_call(
        body, name="dil_comb_bwd", grid=(t // TM,),
        in_specs=[pl.BlockSpec((TM, D_GRP), lambda m: (m, 0))] + [sp for sp in specs for _ in range(2)],
        out_specs=specs + specs,
        out_shape=[_sds((t // dil, dil * D_GRP), BF) for dil in dils]
        + [_sds((t // dil, dil * D_GRP), F32) for dil in dils],
        scratch_shapes=[pltpu.VMEM((TM, 128), F32)] * 4,
        compiler_params=_cp(1))(do, os[0], ls[0], os[1], ls[1], os[2], ls[2])


def _dqkv_dil_sum(ds, dqkv6):
    t = dqkv6.shape[1]
    dils = [dil for _, dil in DIL_CONFIGS]

    def body(*refs):
        srcs, o_ref, acc = refs[:len(dils)], refs[len(dils) + 1], refs[len(dils) + 2]
        for a in range(3):
            for cg in range(D_GRP // 128):
                for src, dil in zip(srcs, dils):
                    for r in range(dil):
                        part = src[a, :, r * D_GRP + cg * 128:r * D_GRP + (cg + 1) * 128].astype(F32)
                        rows = pl.ds(r, TM // dil, stride=dil) if dil > 1 else slice(None)
                        if dil == dils[0]:
                            acc[rows, :] = part
                        else:
                            acc[rows, :] += part
                o_ref[a, :, cg * 128:(cg + 1) * 128] = acc[...].astype(BF)

    return pl.pallas_call(
        body, name="dqkv_dil_sum", grid=(t // TM,),
        in_specs=[pl.BlockSpec((3, TM // dil, dil * D_GRP), lambda m: (0, m, 0)) for dil in dils]
        + [pl.BlockSpec(memory_space=pl.ANY)],
        out_specs=pl.BlockSpec((3, TM, D_GRP), lambda m: (1, m, 0)),
        out_shape=_sds((6, t, D_GRP), BF), input_output_aliases={len(dils): 0},
        scratch_shapes=[pltpu.VMEM((TM, 128), F32)],
        compiler_params=_cp(1))(*ds, dqkv6)


def _relbias_grad(a_all, onehot):
    def body(a_ref, oh_ref, o_ref):
        acc = jnp.zeros((N_HEADS, N_BUCKETS), F32)
        for c in range(len(DIL_CONFIGS)):
            av = a_ref[c]
            hi = av.astype(BF)
            lo = (av - hi.astype(F32)).astype(BF)
            acc = acc + _nt(hi, oh_ref[c]) + _nt(lo, oh_ref[c])
        o_ref[...] = acc

    return pl.pallas_call(body, name="relbias_grad", out_shape=_sds((N_HEADS, N_BUCKETS), F32),
                          compiler_params=_cp())(a_all, onehot)


def _headnorm_bwd(dn, o, gv, mv):
    ms = _nn2(o * o, mv) * (1.0 / HEAD_DIM)
    r = lax.rsqrt(ms + EPS)
    nrm = o * r
    dg = jnp.sum(dn * nrm, axis=0, keepdims=True)
    dnn = dn * gv
    do = r * (dnn - nrm * (_nn2(dnn * nrm, mv) * (1.0 / HEAD_DIM)))
    return do, dg


def _mix_bwd_out(dx, gt, tv, w_out, o_sb, o_dil, on_sb, on_dil, g_sb, g_dil, ones_g, seq, carry=None):
    t, d = dx.shape
    per = seq // TM
    nb = t // seq

    def body(dx_ref, gt_ref, t_ref, w_ref, osb, odl, onsb, ondl, gsb, gdl, m_ref,
             dosb, dodl, dgt_ref, dgsb, dgdl, dw_ref):
        m = pl.program_id(0)
        dxv = dx_ref[...]
        dt = (gt_ref[...] * dxv).astype(BF)
        _acc_rows(dgt_ref, jnp.sum(dxv * t_ref[...].astype(F32), axis=0, keepdims=True), m % per == 0)
        mv = _twice(m_ref[...])
        don_sb = _nt(dt, w_ref[0:D_GRP, :])
        don_dl = _nt(dt, w_ref[D_GRP:2 * D_GRP, :])
        do1, dg1 = _headnorm_bwd(don_sb, osb[...].astype(F32), gsb[...], mv)
        do2, dg2 = _headnorm_bwd(don_dl, odl[...].astype(F32), gdl[...], mv)
        dosb[...] = do1.astype(BF)
        dodl[...] = do2.astype(BF)
        _acc_rows(dgsb, dg1, m == 0)
        _acc_rows(dgdl, dg2, m == 0)
        p1 = _tn(onsb[...], dt)
        p2 = _tn(ondl[...], dt)

        @pl.when(m == 0)
        def _():
            dw_ref[0:D_GRP, :] = p1
            dw_ref[D_GRP:2 * D_GRP, :] = p2

        @pl.when(m != 0)
        def _():
            dw_ref[0:D_GRP, :] += p1
            dw_ref[D_GRP:2 * D_GRP, :] += p2

    row = pl.BlockSpec((TM, d), lambda m: (m, 0))
    half = pl.BlockSpec((TM, D_GRP), lambda m: (m, 0))
    ex = pl.BlockSpec((None, 1, d), lambda m: (m // per, 0, 0))
    gvec = pl.BlockSpec((1, D_GRP), lambda m: (0, 0))
    wblk = pl.BlockSpec((2 * D_GRP, d), lambda m: (0, 0))
    return _call(
        body, "mix_bwd_out", (t // TM,),
        [row, ex, row, wblk, half, half, half, half, gvec, gvec, pl.BlockSpec((D_GRP, D_GRP), lambda m: (0, 0))],
        [half, half, ex, gvec, gvec, wblk],
        [_sds((t, D_GRP), BF), _sds((t, D_GRP), BF), _sds((nb, 1, d), F32),
         _sds((1, D_GRP), F32), _sds((1, D_GRP), F32), _sds((2 * D_GRP, d), F32)],
        (dx, gt, tv, w_out, o_sb, o_dil, on_sb, on_dil, g_sb, g_dil, ones_g), carry=carry)


def _dw_in(h, dqkv6, carry=None):
    t, d = h.shape
    wc = 6 * D_GRP // N_CHIPS

    def body(h_ref, g_ref, o_ref):
        kt = pl.program_id(0)
        hv = h_ref[...]
        for j in range(N_CHIPS):
            p = _tn(hv, _chip_cols(g_ref, j, wc))

            @pl.when(kt == 0)
            def _(p=p, j=j):
                o_ref[j, 0] = p

            @pl.when(kt != 0)
            def _(p=p, j=j):
                o_ref[j, 0] += p

    return _call(
        body, "dw_in", (t // TK_W,),
        [pl.BlockSpec((TK_W, d), lambda kt: (kt, 0)), pl.BlockSpec((6, TK_W, D_GRP), lambda kt: (0, kt, 0))],
        [pl.BlockSpec((N_CHIPS, 1, d, wc), lambda kt: (0, 0, 0, 0))],
        [_sds((N_CHIPS, 1, d, wc), F32)], (h, dqkv6), carry=carry)


def _mix_bwd_dh(dqkv6, w_in, x, g, sc, dxo, seq, carry=None):
    _, t, _ = dqkv6.shape
    d = x.shape[-1]
    wc = w_in.shape[-1]
    per = seq // TM
    nb = t // seq

    def body(g6_ref, w_ref, x_ref, g_ref, sc_ref, dxo_ref, dx_ref, dsh_ref, dsc_ref, dg_ref):
        m = pl.program_id(0)
        dh = _nt(_chip_cols(g6_ref, 0, wc), w_ref[0, 0])
        for j in range(1, N_CHIPS):
            dh = dh + _nt(_chip_cols(g6_ref, j, wc), w_ref[j, 0])
        dx, dsh, dsc, dg = _modnorm_bwd_tile(dh, x_ref[...], g_ref[...], sc_ref[...], dxo_ref[...])
        dx_ref[...] = dx
        _acc_rows(dsh_ref, dsh, m % per == 0)
        _acc_rows(dsc_ref, dsc, m % per == 0)
        _acc_rows(dg_ref, dg, m == 0)

    row = pl.BlockSpec((TM, d), lambda m: (m, 0))
    ex = pl.BlockSpec((None, 1, d), lambda m: (m // per, 0, 0))
    vec = pl.BlockSpec((1, d), lambda m: (0, 0))
    return _call(
        body, "mix_bwd_dh", (t // TM,),
        [pl.BlockSpec((6, TM, D_GRP), lambda m: (0, m, 0)), _whole(w_in), row, vec, ex, row],
        [row, ex, ex, vec],
        [_sds((t, d), F32), _sds((nb, 1, d), F32), _sds((nb, 1, d), F32), _sds((1, d), F32)],
        (dqkv6, w_in, x, g, sc, dxo), carry=carry)


def _ffn_down_loss(s, wd, x, gt, seq, coef, g, target):
    _, t, fs = s.shape
    d = x.shape[-1]
    per = seq // TM
    steps = t // TM

    def body(s_ref, w_ref, x_ref, gt_ref, g_ref, t_ref, f_ref, dx_ref, dg_ref, loss_ref, lacc):
        m = pl.program_id(0)
        f = _nn(s_ref[0], w_ref[0, 0])
        for j in range(1, N_CHIPS):
            f = f + _nn(s_ref[j], w_ref[j, 0])
        f_ref[...] = f.astype(BF)
        xv = x_ref[...] + (coef * gt_ref[...]) * f
        gv = g_ref[...]
        r = lax.rsqrt(jnp.mean(xv * xv, axis=-1, keepdims=True) + EPS)
        n = xv * r
        err = n * gv - t_ref[...]
        dy = err * (1.0 / d)
        _acc_rows(dg_ref, jnp.sum(dy * n, axis=0, keepdims=True), m == 0)
        dn = dy * gv
        dx_ref[...] = r * (dn - n * jnp.mean(dn * n, axis=-1, keepdims=True))
        _acc_rows(lacc, jnp.sum(err * err, axis=0, keepdims=True), m == 0)

        @pl.when(m == steps - 1)
        def _():
            tot = jnp.sum(lacc[...], axis=-1, keepdims=True) * (0.5 / d)
            loss_ref[...] = jnp.broadcast_to(tot, (1, 128))

    row = pl.BlockSpec((TM, d), lambda m: (m, 0))
    vec = pl.BlockSpec((1, d), lambda m: (0, 0))
    return pl.pallas_call(
        body, name="ffn_down_loss", grid=(steps,),
        in_specs=[pl.BlockSpec((N_CHIPS, TM, fs), lambda m: (0, m, 0)), _whole(wd), row,
                  pl.BlockSpec((None, 1, d), lambda m: (m // per, 0, 0)), vec, row],
        out_specs=[row, row, vec, pl.BlockSpec((1, 128), lambda m: (0, 0))],
        out_shape=[_sds((t, d), BF), _sds((t, d), F32), _sds((1, d), F32), _sds((1, 128), F32)],
        scratch_shapes=[pltpu.VMEM((1, d), F32)],
        compiler_params=_cp(1))(s, wd, x, gt, g, target)


def _row_tile(rows, cols):
    best = rows
    for tr in range(8, rows + 1, 8):
        if rows % tr == 0 and tr * cols * 4 <= (1 << 20):
            best = tr
    if best * cols * 4 > (1 << 21):
        best = 8
    return best


def _adamw(w, g_arr, g_sel, m, v):
    rows, cols = w.shape
    tr = _row_tile(rows, cols)
    b1c = 1.0 - ADAM_B1 ** ADAM_STEP
    b2c = 1.0 - ADAM_B2 ** ADAM_STEP

    def body(w_ref, g_ref, m_ref, v_ref, go_ref, d_ref, mo_ref, vo_ref):
        gv = g_ref[...]
        mn = ADAM_B1 * m_ref[...] + (1.0 - ADAM_B1) * gv
        vn = ADAM_B2 * v_ref[...] + (1.0 - ADAM_B2) * (gv * gv)
        go_ref[...] = gv
        mo_ref[...] = mn
        vo_ref[...] = vn
        d_ref[...] = -ADAM_LR * ((mn / b1c) / (jnp.sqrt(vn / b2c) + ADAM_EPS) + ADAM_WD * w_ref[...])

    blk = pl.BlockSpec((tr, cols), lambda i: (i, 0))
    shp = _sds((rows, cols), F32)
    return pl.pallas_call(
        body, name="adamw", grid=(rows // tr,),
        in_specs=[blk, pl.BlockSpec((None, tr, cols), lambda i: (g_sel, i, 0)), blk, blk],
        out_specs=[blk] * 4, out_shape=[shp] * 4,
        compiler_params=_cp(1))(w, g_arr, m, v)


def _flip(v, bit):
    return 1 - v if bit else v


def _my_place():
    x, y, c = lax.axis_index("x"), lax.axis_index("y"), lax.axis_index("c")
    return x, y, c


class _Exchange:
    def __init__(self, operands, out_shape, aliases, sems, start, finish):
        self.operands, self.out_shape, self.aliases, self.sems = list(operands), list(out_shape), dict(aliases), list(sems)
        self.start, self.finish = start, finish


def _join(exchanges):
    exchanges = [e for e in exchanges if e is not None]
    if not exchanges:
        return None
    ops, outs, sems, aliases, spans = [], [], [], {}, []
    for e in exchanges:
        spans.append((len(ops), len(outs), len(sems), e))
        for i, j in e.aliases.items():
            aliases[len(ops) + i] = len(outs) + j
        ops += e.operands
        outs += e.out_shape
        sems += e.sems

    def run(which):
        def go(ins, res, sm):
            for io, oo, so, e in spans:
                getattr(e, which)(ins[io:io + len(e.operands)], res[oo:oo + len(e.out_shape)], sm[so:so + len(e.sems)])
        return go

    return _Exchange(ops, outs, aliases, sems, run("start"), run("finish"))


def _call(body, name, grid, in_specs, out_specs, out_shape, args, scratch=(), carry=None, io_alias=None):
    in_specs, out_specs, out_shape, scratch = list(in_specs), list(out_specs), list(out_shape), list(scratch)
    io_alias = dict(io_alias or {})
    if carry is None:
        return pl.pallas_call(body, name=name, grid=grid, in_specs=in_specs, out_specs=out_specs,
                              out_shape=out_shape, scratch_shapes=scratch, input_output_aliases=io_alias,
                              compiler_params=_cp(len(grid)))(*args)
    n_in, n_out, n_s = len(in_specs), len(out_specs), len(scratch)
    c_in, c_out = len(carry.operands), len(carry.out_shape)
    any_spec = pl.BlockSpec(memory_space=pl.ANY)

    def wrapped(*refs):
        ins, cins = refs[:n_in], refs[n_in:n_in + c_in]
        o0 = n_in + c_in
        outs, couts = refs[o0:o0 + n_out], refs[o0 + n_out:o0 + n_out + c_out]
        s0 = o0 + n_out + c_out
        scr, sems = refs[s0:s0 + n_s], refs[s0 + n_s:]
        first = pl.program_id(0) == 0
        last = pl.program_id(0) == grid[0] - 1
        for ax in range(1, len(grid)):
            first = jnp.logical_and(first, pl.program_id(ax) == 0)
            last = jnp.logical_and(last, pl.program_id(ax) == grid[ax] - 1)

        @pl.when(first)
        def _():
            carry.start(cins, couts, sems)

        body(*ins, *outs, *scr)

        @pl.when(last)
        def _():
            carry.finish(cins, couts, sems)

    return pl.pallas_call(
        wrapped, name=name, grid=grid, in_specs=in_specs + [any_spec] * c_in,
        out_specs=out_specs + [any_spec] * c_out, out_shape=out_shape + carry.out_shape,
        scratch_shapes=scratch + carry.sems,
        input_output_aliases={**io_alias, **{n_in + i: n_out + j for i, j in carry.aliases.items()}},
        compiler_params=_cp(len(grid)))(*args, *carry.operands)


def _whole_call(body, name, args, out_shape, scratch, carry=None):
    vm = pl.BlockSpec(memory_space=pltpu.VMEM)
    any_spec = pl.BlockSpec(memory_space=pl.ANY)
    out_shape, scratch = list(out_shape), list(scratch)
    n_in, n_out, n_s = len(args), len(out_shape), len(scratch)
    if carry is None:
        return pl.pallas_call(body, name=name, in_specs=[vm] * n_in, out_specs=[vm] * n_out, out_shape=out_shape,
                              scratch_shapes=scratch, compiler_params=_cp())(*args)
    c_in, c_out = len(carry.operands), len(carry.out_shape)

    def wrapped(*refs):
        ins, cins = refs[:n_in], refs[n_in:n_in + c_in]
        o0 = n_in + c_in
        outs, couts = refs[o0:o0 + n_out], refs[o0 + n_out:o0 + n_out + c_out]
        s0 = o0 + n_out + c_out
        scr, sems = refs[s0:s0 + n_s], refs[s0 + n_s:]
        carry.start(cins, couts, sems)
        body(*ins, *outs, *scr)
        carry.finish(cins, couts, sems)

    return pl.pallas_call(
        wrapped, name=name, in_specs=[vm] * n_in + [any_spec] * c_in, out_specs=[vm] * n_out + [any_spec] * c_out,
        out_shape=out_shape + carry.out_shape, scratch_shapes=scratch + carry.sems,
        input_output_aliases={n_in + i: n_out + j for i, j in carry.aliases.items()},
        compiler_params=_cp())(*args, *carry.operands)


def _alone(name, ex):
    any_spec = pl.BlockSpec(memory_space=pl.ANY)
    c_in, c_out = len(ex.operands), len(ex.out_shape)

    def body(*refs):
        ins, outs, sems = refs[:c_in], refs[c_in:c_in + c_out], refs[c_in + c_out:]
        ex.start(ins, outs, sems)
        ex.finish(ins, outs, sems)

    return pl.pallas_call(
        body, name=name, in_specs=[any_spec] * c_in, out_specs=[any_spec] * c_out, out_shape=ex.out_shape,
        scratch_shapes=ex.sems, input_output_aliases=ex.aliases, compiler_params=_cp())(*ex.operands)


def _ada_fwd(c_pad, w_ada, b_shard, carry=None):
    d = c_pad.shape[-1]
    cols = w_ada.shape[-1]
    chunk = 384

    def body(c_ref, w_ref, b_ref, call_ref, mod_ref, part, s1, r1, s2, r2):
        x, y, c = _my_place()
        dev = 4 * x + 2 * y + c
        chip = 2 * x + y
        call_ref[dev] = c_ref[...]

        def c_copy(k):
            px, py, pc = _flip(x, (k >> 2) & 1), _flip(y, (k >> 1) & 1), _flip(c, k & 1)
            return px, py, pc

        sends = []
        for k in range(1, N_DEV):
            px, py, pc = c_copy(k)
            cp = pltpu.make_async_remote_copy(src_ref=c_ref, dst_ref=call_ref.at[dev], send_sem=s1.at[k - 1],
                                              recv_sem=r1.at[k - 1], device_id=(px, py, pc), device_id_type=MESH)
            cp.start()
            sends.append(cp)
        for k in range(1, N_DEV):
            px, py, pc = c_copy(k)
            pltpu.make_async_remote_copy(src_ref=c_ref, dst_ref=call_ref.at[4 * px + 2 * py + pc],
                                         send_sem=s1.at[k - 1], recv_sem=r1.at[k - 1],
                                         device_id=(px, py, pc), device_id_type=MESH).wait_recv()
        for cp in sends:
            cp.wait_send()

        cs = call_ref[...].reshape(N_DEV * 8, d)
        sc = (cs * jax.nn.sigmoid(cs)).astype(BF)
        for n0 in range(0, cols, chunk):
            blk = _nn(sc, w_ref[:, n0:n0 + chunk].astype(BF)) + b_ref[:, n0:n0 + chunk]
            part[:, :, n0:n0 + chunk] = blk.reshape(N_DEV, 8, chunk)

        mod_ref[chip] = part[dev]
        sends = []
        for kk in range(1, N_CHIPS):
            px, py = _flip(x, (kk >> 1) & 1), _flip(y, kk & 1)
            cp = pltpu.make_async_remote_copy(src_ref=part.at[4 * px + 2 * py + c], dst_ref=mod_ref.at[chip],
                                              send_sem=s2.at[kk - 1], recv_sem=r2.at[kk - 1],
                                              device_id=(px, py, c), device_id_type=MESH)
            cp.start()
            sends.append(cp)
        for kk in range(1, N_CHIPS):
            px, py = _flip(x, (kk >> 1) & 1), _flip(y, kk & 1)
            pltpu.make_async_remote_copy(src_ref=part.at[dev], dst_ref=mod_ref.at[2 * px + py],
                                         send_sem=s2.at[kk - 1], recv_sem=r2.at[kk - 1],
                                         device_id=(px, py, c), device_id_type=MESH).wait_recv()
        for cp in sends:
            cp.wait_send()

    return _whole_call(
        body, "ada_fwd", (c_pad, w_ada, b_shard),
        [_sds((N_DEV, 8, d), F32), _sds((N_CHIPS, 8, cols), F32)],
        [pltpu.VMEM((N_DEV, 8, cols), F32),
         pltpu.SemaphoreType.DMA((N_DEV - 1,)), pltpu.SemaphoreType.DMA((N_DEV - 1,)),
         pltpu.SemaphoreType.DMA((N_CHIPS - 1,)), pltpu.SemaphoreType.DMA((N_CHIPS - 1,))], carry=carry)


def _ag_weights(bufs, kks=(1, 2, 3), relative=False):
    n, nk = len(bufs), len(kks)

    def half(b, which):
        hr = bufs[b].shape[2] // 2
        return pl.ds(pl.multiple_of(which * hr, 16), hr)

    def copies(outs, sems, b, i, kk):
        x, y, c = _my_place()
        chip = 2 * x + y
        px, py = _flip(x, (kk >> 1) & 1), _flip(y, kk & 1)
        mine, theirs = (0, kk) if relative else (chip, 2 * px + py)
        landing = kk if relative else chip
        k = nk * b + i
        send = pltpu.make_async_remote_copy(
            src_ref=outs[b].at[mine, :, half(b, c), :], dst_ref=outs[b].at[landing, :, half(b, c), :],
            send_sem=sems[0].at[k], recv_sem=sems[1].at[k], device_id=(px, py, c), device_id_type=MESH)
        got = outs[b].at[theirs, :, half(b, c), :]
        recv = pltpu.make_async_remote_copy(
            src_ref=got, dst_ref=got, send_sem=sems[0].at[k], recv_sem=sems[1].at[k],
            device_id=(px, py, c), device_id_type=MESH)
        fwd = pltpu.make_async_remote_copy(
            src_ref=got, dst_ref=got, send_sem=sems[2].at[k], recv_sem=sems[3].at[k],
            device_id=(x, y, 1 - c), device_id_type=MESH)
        other = outs[b].at[theirs, :, half(b, 1 - c), :]
        back = pltpu.make_async_remote_copy(
            src_ref=other, dst_ref=other, send_sem=sems[2].at[k], recv_sem=sems[3].at[k],
            device_id=(x, y, 1 - c), device_id_type=MESH)
        return send, recv, fwd, back

    def each(outs, sems):
        for b in range(n):
            for i, kk in enumerate(kks):
                yield copies(outs, sems, b, i, kk)

    def start(ins, outs, sems):
        for send, _, _, _ in each(outs, sems):
            send.start()

    def finish(ins, outs, sems):
        for _, recv, fwd, _ in each(outs, sems):
            recv.wait_recv()
            fwd.start()
        for send, _, fwd, back in each(outs, sems):
            back.wait_recv()
            send.wait_send()
            fwd.wait_send()

    return _Exchange(bufs, [_sds(s.shape, s.dtype) for s in bufs], {i: i for i in range(n)},
                     [pltpu.SemaphoreType.DMA((nk * n,))] * 4, start, finish)


def _rs_d2d(grads):
    n = len(grads)

    def copy(ins, outs, sems, b):
        x, y, c = _my_place()
        hr = grads[b].shape[2] // 2
        theirs = pl.ds(pl.multiple_of((1 - c) * hr, 8), hr)
        return pltpu.make_async_remote_copy(
            src_ref=ins[b].at[:, :, theirs, :], dst_ref=outs[b], send_sem=sems[0].at[b], recv_sem=sems[1].at[b],
            device_id=(x, y, 1 - c), device_id_type=MESH)

    def start(ins, outs, sems):
        for b in range(n):
            copy(ins, outs, sems, b).start()

    def finish(ins, outs, sems):
        for b in range(n):
            copy(ins, outs, sems, b).wait()

    return _Exchange(grads, [_sds(g.shape[:2] + (g.shape[2] // 2, g.shape[3]), F32) for g in grads], {},
                     [pltpu.SemaphoreType.DMA((n,))] * 2, start, finish)


def _add_halves(core, g, land):
    nchip, ng, rows, cols = g.shape
    hr = rows // 2
    tr = _row_tile(hr, cols)
    steps = hr // tr

    def body(core_ref, g_ref, l_ref, o_ref):
        del core_ref
        o_ref[...] = (g_ref[...] + l_ref[...]).astype(BF)

    return pl.pallas_call(
        body, name="add_halves",
        grid_spec=pltpu.PrefetchScalarGridSpec(
            num_scalar_prefetch=1, grid=(nchip, ng, steps),
            in_specs=[pl.BlockSpec((None, None, tr, cols), lambda j, a, i, cr: (j, a, cr[0] * steps + i, 0)),
                      pl.BlockSpec((None, None, tr, cols), lambda j, a, i, cr: (j, a, i, 0))],
            out_specs=pl.BlockSpec((None, None, tr, cols), lambda j, a, i, cr: (j, a, i, 0))),
        out_shape=_sds((nchip, ng, hr, cols), BF),
        compiler_params=_cp(3))(core, g, land)


def _rs_ici(parts, relative=False):
    n = len(parts)

    def copies(ins, outs, sems):
        x, y, c = _my_place()
        chip = 2 * x + y
        for b in range(n):
            for kk in range(1, N_CHIPS):
                px, py = _flip(x, (kk >> 1) & 1), _flip(y, kk & 1)
                k = 3 * b + kk - 1
                theirs, landing = (kk, kk) if relative else (2 * px + py, chip)
                send = pltpu.make_async_remote_copy(
                    src_ref=ins[b].at[theirs], dst_ref=outs[b].at[landing],
                    send_sem=sems[0].at[k], recv_sem=sems[1].at[k], device_id=(px, py, c), device_id_type=MESH)
                slot = outs[b].at[theirs]
                recv = pltpu.make_async_remote_copy(
                    src_ref=slot, dst_ref=slot, send_sem=sems[0].at[k], recv_sem=sems[1].at[k],
                    device_id=(px, py, c), device_id_type=MESH)
                yield send, recv

    def start(ins, outs, sems):
        for send, _ in copies(ins, outs, sems):
            send.start()

    def finish(ins, outs, sems):
        for send, recv in copies(ins, outs, sems):
            recv.wait_recv()
            send.wait_send()

    return _Exchange(parts, [_sds(p.shape, p.dtype) for p in parts], {},
                     [pltpu.SemaphoreType.DMA((3 * n,))] * 2, start, finish)


def _sum_chips(place, part, land, relative=False):
    nchip, ng, hr, cols = land.shape
    tr = _row_tile(hr, cols)
    steps = hr // tr

    def body(place_ref, p_ref, l1, l2, l3, o_ref):
        del place_ref
        o_ref[...] = ((p_ref[...].astype(F32) + l1[...].astype(F32)) + l2[...].astype(F32)) + l3[...].astype(F32)

    def slot(k):
        if relative:
            return pl.BlockSpec((None, None, tr, cols), lambda a, i, pr: (k, a, i, 0))
        return pl.BlockSpec((None, None, tr, cols), lambda a, i, pr: (jnp.bitwise_xor(pr[1], k), a, i, 0))

    return pl.pallas_call(
        body, name="sum_chips",
        grid_spec=pltpu.PrefetchScalarGridSpec(
            num_scalar_prefetch=1, grid=(ng, steps),
            in_specs=[slot(0), slot(1), slot(2), slot(3)],
            out_specs=pl.BlockSpec((None, tr, cols), lambda a, i, pr: (a, pr[0] * steps + i, 0))),
        out_shape=_sds((ng, 2 * hr, cols), F32),
        compiler_params=_cp(2))(place, part, land, land, land)


def _rs_final(bufs):
    n = len(bufs)

    def copy(outs, sems, b, which):
        x, y, c = _my_place()
        hr = bufs[b].shape[1] // 2
        rows = outs[b].at[:, pl.ds(pl.multiple_of((c if which == 0 else 1 - c) * hr, 8), hr), :]
        return pltpu.make_async_remote_copy(
            src_ref=rows, dst_ref=rows, send_sem=sems[0].at[b], recv_sem=sems[1].at[b],
            device_id=(x, y, 1 - c), device_id_type=MESH)

    def start(ins, outs, sems):
        for b in range(n):
            copy(outs, sems, b, 0).start()

    def finish(ins, outs, sems):
        for b in range(n):
            copy(outs, sems, b, 0).wait_send()
            copy(outs, sems, b, 1).wait_recv()

    return _Exchange(bufs, [_sds(h.shape, F32) for h in bufs], {i: i for i in range(n)},
                     [pltpu.SemaphoreType.DMA((n,))] * 2, start, finish)


def _small_sync(smalls, dmod_blk, c_all, carry=None):
    d = c_all.shape[-1]
    cols = dmod_blk.shape[-1]
    chunk = 384

    def body(sm_ref, dm_ref, c_ref, sum_ref, gw_ref, sm_all, dm_all, ssem, rsem):
        x, y, c = _my_place()
        dev = 4 * x + 2 * y + c
        chip = 2 * x + y
        sm_all[dev] = sm_ref[...]
        dm_all[dev] = dm_ref[chip]
        sends = []
        for k in range(1, N_DEV):
            px, py, pc = _flip(x, (k >> 2) & 1), _flip(y, (k >> 1) & 1), _flip(c, k & 1)
            a = pltpu.make_async_remote_copy(src_ref=sm_ref, dst_ref=sm_all.at[dev], send_sem=ssem.at[2 * (k - 1)],
                                             recv_sem=rsem.at[2 * (k - 1)], device_id=(px, py, pc),
                                             device_id_type=MESH)
            b = pltpu.make_async_remote_copy(src_ref=dm_ref.at[2 * px + py], dst_ref=dm_all.at[dev],
                                             send_sem=ssem.at[2 * (k - 1) + 1], recv_sem=rsem.at[2 * (k - 1) + 1],
                                             device_id=(px, py, pc), device_id_type=MESH)
            a.start()
            b.start()
            sends += [a, b]
        for k in range(1, N_DEV):
            px, py, pc = _flip(x, (k >> 2) & 1), _flip(y, (k >> 1) & 1), _flip(c, k & 1)
            pdev = 4 * px + 2 * py + pc
            pltpu.make_async_remote_copy(src_ref=sm_ref, dst_ref=sm_all.at[pdev], send_sem=ssem.at[2 * (k - 1)],
                                         recv_sem=rsem.at[2 * (k - 1)], device_id=(px, py, pc),
                                         device_id_type=MESH).wait_recv()
            pltpu.make_async_remote_copy(src_ref=dm_ref.at[chip], dst_ref=dm_all.at[pdev],
                                         send_sem=ssem.at[2 * (k - 1) + 1], recv_sem=rsem.at[2 * (k - 1) + 1],
                                         device_id=(px, py, pc), device_id_type=MESH).wait_recv()
        for cp in sends:
            cp.wait_send()

        tot = sm_all[0]
        for q in range(1, N_DEV):
            tot = tot + sm_all[q]
        sum_ref[...] = tot

        cs = c_ref[...].reshape(N_DEV * 8, d)
        sc = (cs * jax.nn.sigmoid(cs)).astype(BF)
        for n0 in range(0, cols, chunk):
            dmv = dm_all[:, :, n0:n0 + chunk].reshape(N_DEV * 8, chunk).astype(BF)
            gw_ref[:, n0:n0 + chunk] = _tn(sc, dmv)

    return _whole_call(
        body, "small_sync", (smalls, dmod_blk, c_all),
        [_sds(smalls.shape, F32), _sds((d, cols), F32)],
        [pltpu.VMEM((N_DEV,) + smalls.shape, F32), pltpu.VMEM((N_DEV, 8, cols), F32),
         pltpu.SemaphoreType.DMA((2 * (N_DEV - 1),)), pltpu.SemaphoreType.DMA((2 * (N_DEV - 1),))], carry=carry)


def _bucket_onehot():
    maps = np.stack([_bucket_map(dil).reshape(-1) for _, dil in DIL_CONFIGS])
    return (jnp.asarray(maps)[:, None, :] == jnp.arange(N_BUCKETS, dtype=jnp.int32)[None, :, None]).astype(BF)


def _dil_bias(rel_t, onehot):
    def body(r_ref, oh_ref, o_ref):
        rv = r_ref[...]
        hi = rv.astype(BF)
        lo = (rv - hi.astype(F32)).astype(BF)
        for c in range(len(DIL_CONFIGS)):
            o_ref[c] = _nn(hi, oh_ref[c]) + _nn(lo, oh_ref[c])

    return pl.pallas_call(body, name="dil_bias",
                          out_shape=_sds((len(DIL_CONFIGS), N_HEADS, BLOCK * 2 * BLOCK), F32),
                          compiler_params=_cp())(rel_t, onehot)


def _rowsum8(a):
    def body(a_ref, o_ref):
        o_ref[...] = jnp.sum(a_ref[...], axis=0, keepdims=True)

    return pl.pallas_call(body, name="rowsum8", out_shape=_sds((1, a.shape[1]), F32), compiler_params=_cp())(a)


def _local_step(x, mod, target, w, gains, rel_bias, place=None):
    nb, seq, d = x.shape
    t = nb * seq
    dist = place is not None
    core = place[0:1] if dist else None
    x0 = x.reshape(t, d)
    tgt = target.reshape(t, d)
    md = [mod[:, i:i + 1, :] for i in range(N_MOD)]
    sh1, sc1, gt1, sh2, sc2, gt2, sh3, sc3, gt3 = md
    g1, g2, g3 = gains["g_ffn1"], gains["g_mix"], gains["g_ffn2"]
    ones_g = _group_ones()

    def partial_sums(grads, lands):
        return [_add_halves(core, g, l) for g, l in zip(grads, lands)]

    def chip_sums(parts, lands):
        return [_sum_chips(place, p, l, relative=True) for p, l in zip(parts, lands)]

    gu1 = w["gu1"]
    res = _ffn_up(x0, g1, sc1, sh1, gu1, seq,
                  carry=_join([_ag_weights([w["d1"]], relative=True), _ag_weights([w["win"]])]) if dist else None)
    h1, a1, u1, s1 = res[:4]
    wd1, w_in = res[4:] if dist else (w["d1"], w["win"])
    f1, x1 = _ffn_down(s1, wd1, x0, gt1, seq, 0.5)

    res = _qkv_proj(x1, g2, sc2, sh2, w_in, seq, carry=_ag_weights([w["wout"]]) if dist else None)
    h2, qkv6, qkv_r4, qkv_r16 = res[:4]
    w_out2 = (res[4] if dist else w["wout"]).reshape(2 * D_GRP, d)
    qkv6b = qkv6.reshape(6, nb, seq, D_GRP)
    res = _sb_fwd(qkv6b, gains["g_sb_out"], nb, seq,
                  carry=_ag_weights([w["gu2"], w["d2"]], relative=True) if dist else None)
    o_sb, on_sb = res[:2]
    wgu2, wd2 = res[2:] if dist else (w["gu2"], w["d2"])
    onehot = _bucket_onehot()
    bias = _dil_bias(rel_bias.T, onehot).reshape(len(DIL_CONFIGS), N_HEADS * BLOCK, 2 * BLOCK)
    o_cs, l_cs = [], []
    qkv_rs = [(qkv6b, 3), (qkv_r4, 0), (qkv_r16, 0)]
    for ci, (_, dil) in enumerate(DIL_CONFIGS):
        sub = seq // dil
        arr, base = qkv_rs[ci]
        arr = arr.reshape(base + 3, nb, sub, dil * D_GRP)
        qkv_rs[ci] = (arr, base)
        o_c, l_c = _dil_fwd(arr, base, bias[ci], nb, sub, dil)
        o_cs.append(o_c.reshape(t // dil, dil * D_GRP))
        l_cs.append(l_c.reshape(t // dil, dil * D_GRP))
    o_dil, on_dil = _dil_comb(o_cs, l_cs, gains["g_dil_out"])
    tmix, x2 = _mix_out(on_sb.reshape(t, D_GRP), on_dil, w_out2, x1, gt2, seq)

    h3, a3, u3, s3 = _ffn_up(x2, g3, sc3, sh3, wgu2, seq)
    f3, dx3, dg_final, loss = _ffn_down_loss(s3, wd2, x2, gt3, seq, 0.5, gains["g_final"], tgt)

    da3, du3, df3, dgt3, dx2, dsh3, dsc3, dg3 = _ffn_bwd_x(dx3, gt3, f3, wd2, a3, u3, wgu2, x2, g3, sc3, seq, 0.5)
    grads2 = _ffn_bwd_w(h3, da3, du3, s3, df3)

    res = _mix_bwd_out(
        dx2, gt2, tmix, w_out2, o_sb.reshape(t, D_GRP), o_dil, on_sb.reshape(t, D_GRP), on_dil,
        gains["g_sb_out"], gains["g_dil_out"], ones_g, seq, carry=_rs_d2d(grads2) if dist else None)
    do_sb, do_dil, dgt2, dg_sb, dg_dil, dw_out = res[:6]
    parts2 = partial_sums(grads2, res[6:]) if dist else None
    dw_out = dw_out.reshape(N_CHIPS, 1, 2 * D_GRP // N_CHIPS, d)
    res = _sb_bwd(qkv6b, do_sb.reshape(nb, seq, D_GRP), nb, seq,
                  carry=_join([_rs_ici(parts2, relative=True), _rs_d2d([dw_out])]) if dist else None)
    dqkv6 = res[0]
    halves2 = chip_sums(parts2, res[1:2]) if dist else None
    part_wo = partial_sums([dw_out], res[2:3]) if dist else None
    dcs = _dil_comb_bwd(do_dil, o_cs, l_cs)
    dsum, a_tiles = [], []
    for ci, (_, dil) in enumerate(DIL_CONFIGS):
        sub = seq // dil
        do_c = dcs[ci].reshape(nb, sub, dil * D_GRP)
        dd_c = dcs[3 + ci].reshape(nb, sub, dil * D_GRP)
        res = _dil_bwd(qkv_rs[ci][0], qkv_rs[ci][1], bias[ci], do_c, dd_c, nb, sub, dil)
        dsum.append(res[0].reshape(3, t // dil, dil * D_GRP))
        a_tiles.append(res[1].reshape(N_HEADS, BLOCK * 2 * BLOCK))
    dqkv6 = _dqkv_dil_sum(dsum, dqkv6.reshape(6, t, D_GRP))
    drel = _relbias_grad(jnp.stack(a_tiles), onehot)
    dx1, dsh2, dsc2, dg2 = _mix_bwd_dh(dqkv6, w_in, x1, g2, sc2, dx2, seq)

    da1, du1, df1, dgt1 = _ffn_bwd_ds(dx1, gt1, f1, wd1, a1, u1, seq, 0.5)
    g_dn = _ffn_bwd_w(h1, da1, du1, s1, df1, terms=(2,))
    res = _ffn_bwd_w(h1, da1, du1, s1, df1, terms=(0, 1), carry=_rs_d2d(g_dn) if dist else None)
    g_gu = res[:1]
    grads1 = g_gu + g_dn
    parts_dn = partial_sums(g_dn, res[1:2]) if dist else None
    res = _dw_in(h2, dqkv6, carry=_join([_rs_ici(parts_dn, relative=True), _rs_ici(part_wo), _rs_d2d(g_gu),
                                         _rs_final(halves2)]) if dist else None)
    grads_m = [res[0], dw_out]
    if dist:
        sums_dn = chip_sums(parts_dn, res[1:2])
        sum_wo = [_sum_chips(place, p, l) for p, l in zip(part_wo, res[2:3])]
        parts_gu = partial_sums(g_gu, res[3:4])
        grads2 = res[4:5]
    res = _ffn_bwd_dh(da1, du1, gu1, x0, g1, sc1, dx1, seq,
                      carry=_join([_rs_ici(parts_gu, relative=True), _rs_d2d(grads_m[:1])]) if dist else None)
    dx0, dsh1, dsc1, dg1 = res[:4]
    pending = None
    if dist:
        pending = (chip_sums(parts_gu, res[4:5]) + sums_dn + sum_wo, partial_sums(grads_m[:1], res[5:6]))

    dmod = jnp.concatenate([dsh1, dsc1, dgt1, dsh2, dsc2, dgt2, dsh3, dsc3, dgt3], axis=1)
    return dict(grad_x=dx0.reshape(nb, seq, d), loss=loss[0, 0], dmod=dmod.reshape(nb, N_MOD * d),
                dffn1=grads1, dffn2=grads2[0], dwin=grads_m[0], dwout=grads_m[1], pending=pending,
                dg_ffn1=dg1, dg_mix=dg2, dg_ffn2=dg3, dg_final=dg_final, dg_sb=dg_sb, dg_dil=dg_dil,
                drel=drel.T)


_SMALL_ORDER = (("b_ada", N_MOD * 1024), ("g_ffn1", 1024), ("g_mix", 1024), ("g_ffn2", 1024), ("g_final", 1024),
                ("g_sb_out", D_GRP), ("g_dil_out", D_GRP), ("rel_bias", N_BUCKETS * N_HEADS))


def _pack_small(parts, extra=None):
    flat = [parts[name].reshape(-1).astype(F32) for name, _ in _SMALL_ORDER]
    used = sum(sz for _, sz in _SMALL_ORDER)
    pad = SMALL_ROWS * 128 - used
    tail = jnp.zeros((pad,), F32)
    if extra is not None:
        tail = tail.at[0].set(extra)
    return jnp.concatenate(flat + [tail]).reshape(SMALL_ROWS, 128)


def _unpack_small(packed, shapes):
    flat = packed.reshape(-1)
    out, off = {}, 0
    for name, sz in _SMALL_ORDER:
        out[name] = flat[off:off + sz].reshape(shapes[name])
        off += sz
    return out, flat[off]


def kernel(x, c, w_ada, b_ada, g_ffn1, w1_gate, w1_up, w1_down, g_mix, w_in, g_sb_out, g_dil_out, w_out, rel_bias, g_ffn2, w2_gate, w2_up, w2_down, g_final, loss_target, m_w_ada, m_b_ada, m_g_ffn1, m_w1_gate, m_w1_up, m_w1_down, m_g_mix, m_w_in, m_g_sb_out, m_g_dil_out, m_w_out, m_rel_bias, m_g_ffn2, m_w2_gate, m_w2_up, m_w2_down, m_g_final, v_w_ada, v_b_ada, v_g_ffn1, v_w1_gate, v_w1_up, v_w1_down, v_g_mix, v_w_in, v_g_sb_out, v_g_dil_out, v_w_out, v_rel_bias, v_g_ffn2, v_w2_gate, v_w2_up, v_w2_down, v_g_final):
    nb, seq, d = x.shape
    xi, yi, ci = lax.axis_index("x"), lax.axis_index("y"), lax.axis_index("c")
    chip = 2 * xi + yi
    ada_cols = w_ada.shape[-1]

    c_pad = jnp.zeros((8, d), F32).at[:nb].set(c)
    b_shard = lax.dynamic_slice(b_ada, (0, chip * ada_cols), (1, ada_cols))
    shards = dict(gu1=jnp.stack([w1_gate[0], w1_up[0]]), d1=w1_down, win=w_in, wout=w_out,
                  gu2=jnp.stack([w2_gate[0], w2_up[0]]), d2=w2_down)
    bufs = {k: lax.dynamic_update_slice(lax.empty((N_CHIPS,) + s.shape, BF), s.astype(BF)[None],
                                        (chip if k in ("win", "wout") else 0, 0, 0, 0))
            for k, s in shards.items()}
    c_all, mod_blk, bufs["gu1"] = _ada_fwd(c_pad, w_ada[0], b_shard,
                                           carry=_ag_weights([bufs["gu1"]], relative=True))
    mod = jnp.transpose(mod_blk[:, :nb, :], (1, 0, 2)).reshape(nb, N_MOD, d)

    gains = dict(g_ffn1=g_ffn1, g_mix=g_mix, g_ffn2=g_ffn2, g_final=g_final.reshape(1, d),
                 g_sb_out=g_sb_out.reshape(1, D_GRP), g_dil_out=g_dil_out.reshape(1, D_GRP))
    place = jnp.stack([ci, chip]).astype(jnp.int32)
    r = _local_step(x, mod, loss_target, bufs, gains, rel_bias, place)

    dmod = r["dmod"]
    dmod_pad = jnp.zeros((8, N_MOD * d), F32).at[:nb].set(dmod)
    dmod_blk = jnp.transpose(dmod_pad.reshape(8, N_CHIPS, ada_cols), (1, 0, 2))
    small_parts = dict(b_ada=_rowsum8(dmod_pad), g_ffn1=r["dg_ffn1"], g_mix=r["dg_mix"], g_ffn2=r["dg_ffn2"],
                       g_final=r["dg_final"], g_sb_out=r["dg_sb"], g_dil_out=r["dg_dil"], rel_bias=r["drel"])
    halves1, parts_m = r["pending"]
    res = _small_sync(_pack_small(small_parts, r["loss"]), dmod_blk, c_all,
                      carry=_join([_rs_final(halves1), _rs_ici(parts_m)]))
    small_sum, g_wada, gffn1_gu, gffn1_dn, gwout = res[:5]
    halves_m = [_sum_chips(place, p, l) for p, l in zip(parts_m, res[5:6])]
    gwin, = _alone("rs_last", _rs_final(halves_m))
    gffn2 = r["dffn2"]

    small_w = dict(b_ada=b_ada, g_ffn1=g_ffn1, g_mix=g_mix, g_ffn2=g_ffn2, g_final=g_final,
                   g_sb_out=g_sb_out, g_dil_out=g_dil_out, rel_bias=rel_bias)
    small_m = dict(b_ada=m_b_ada, g_ffn1=m_g_ffn1, g_mix=m_g_mix, g_ffn2=m_g_ffn2, g_final=m_g_final,
                   g_sb_out=m_g_sb_out, g_dil_out=m_g_dil_out, rel_bias=m_rel_bias)
    small_v = dict(b_ada=v_b_ada, g_ffn1=v_g_ffn1, g_mix=v_g_mix, g_ffn2=v_g_ffn2, g_final=v_g_final,
                   g_sb_out=v_g_sb_out, g_dil_out=v_g_dil_out, rel_bias=v_rel_bias)
    shapes = {k: v.shape for k, v in small_w.items()}
    sg, sd, sm, sv = _adamw(_pack_small(small_w), small_sum.reshape(1, SMALL_ROWS, 128), 0,
                            _pack_small(small_m), _pack_small(small_v))
    sg, loss = _unpack_small(sg, shapes)
    sd, _ = _unpack_small(sd, shapes)
    sm, _ = _unpack_small(sm, shapes)
    sv, _ = _unpack_small(sv, shapes)

    big = {}

    def upd(name, w, g_arr, sel, m, v, transposed=False):
        swap = (lambda a: jnp.swapaxes(a, -1, -2)) if transposed else (lambda a: a)
        w2, m2, v2 = [swap(a)[0] for a in (w, m, v)]
        big[name] = [swap(a[None]) for a in _adamw(w2, g_arr, sel, m2, v2)]

    upd("w_ada", w_ada, g_wada.reshape(1, d, ada_cols), 0, m_w_ada, v_w_ada)
    upd("w1_gate", w1_gate, gffn1_gu, 0, m_w1_gate, v_w1_gate, transposed=True)
    upd("w1_up", w1_up, gffn1_gu, 1, m_w1_up, v_w1_up, transposed=True)
    upd("w1_down", w1_down, gffn1_dn, 0, m_w1_down, v_w1_down)
    upd("w_in", w_in, gwin, 0, m_w_in, v_w_in)
    upd("w_out", w_out, gwout, 0, m_w_out, v_w_out)
    upd("w2_gate", w2_gate, gffn2, 0, m_w2_gate, v_w2_gate, transposed=True)
    upd("w2_up", w2_up, gffn2, 1, m_w2_up, v_w2_up, transposed=True)
    upd("w2_down", w2_down, gffn2, 2, m_w2_down, v_w2_down)

    names = ["w_ada", "b_ada", "g_ffn1", "w1_gate", "w1_up", "w1_down", "g_mix", "w_in", "g_sb_out", "g_dil_out",
             "w_out", "rel_bias", "g_ffn2", "w2_gate", "w2_up", "w2_down", "g_final"]
    outs = [loss, r["grad_x"]]
    for k, small in enumerate((sg, sd, sm, sv)):
        for name in names:
            outs.append(big[name][k] if name in big else small[name])
    return tuple(outs)
```

```python
import math

import numpy as np
import jax
import jax.numpy as jnp
from jax import lax
from jax.experimental import pallas as pl
from jax.experimental.pallas import tpu as pltpu

F32 = jnp.float32
BF = jnp.bfloat16
MESH = pl.DeviceIdType.MESH

HEAD_DIM = 64
N_HEADS = 8
D_GRP = N_HEADS * HEAD_DIM
DIL_CONFIGS = ((128, 1), (512, 4), (2048, 16))
N_STEPS = 128
BLOCK = 128
N_BUCKETS = 32
MAX_DISTANCE = 2048
N_MOD = 9
EPS = 1e-6
NEG_INF = -1e30
SCALE = HEAD_DIM ** -0.5

ADAM_LR = 0.001
ADAM_B1 = 0.9
ADAM_B2 = 0.999
ADAM_EPS = 1e-08
ADAM_WD = 0.01
ADAM_STEP = 10

N_CHIPS = 4
N_DEV = 8
VMEM_LIMIT = 56 * 1024 * 1024
TM = 512
TQ = 256
KB = 256
SMALL_ROWS = 120


def _cp(n_axes=0, **kw):
    sem = ("arbitrary",) * n_axes if n_axes else None
    return pltpu.CompilerParams(dimension_semantics=sem, vmem_limit_bytes=VMEM_LIMIT, **kw)


def _nn(a, b):
    return jnp.dot(a, b, preferred_element_type=F32)


def _nt(a, b):
    return lax.dot_general(a, b, (((1,), (1,)), ((), ())), preferred_element_type=F32)


def _tn(a, b):
    return lax.dot_general(a, b, (((0,), (0,)), ((), ())), preferred_element_type=F32)


def _twice(m):
    return jnp.concatenate([m, m], axis=0)


def _nn2(x, m2):
    hi = x.astype(BF)
    lo = (x - hi.astype(F32)).astype(BF)
    return _nn(jnp.concatenate([hi, lo], axis=1), m2)


def _softplus(z):
    return jnp.maximum(z, 0.0) + jnp.log(1.0 + jnp.exp(-jnp.abs(z)))


def _sds(shape, dtype):
    return jax.ShapeDtypeStruct(shape, dtype)


def _whole(a):
    nd = a.ndim
    return pl.BlockSpec(a.shape, lambda *_: (0,) * nd, pipeline_mode=pl.Buffered(1))


def _modnorm_bwd_tile(dh, xv, gv, scv, dxo):
    r = lax.rsqrt(jnp.mean(xv * xv, axis=-1, keepdims=True) + EPS)
    n = xv * r
    ng = n * gv
    dsh = jnp.sum(dh, axis=0, keepdims=True)
    dsc = jnp.sum(dh * ng, axis=0, keepdims=True)
    dy = dh * (1.0 + scv)
    dg = jnp.sum(dy * n, axis=0, keepdims=True)
    dn = dy * gv
    dx = dxo + r * (dn - n * jnp.mean(dn * n, axis=-1, keepdims=True))
    return dx, dsh, dsc, dg


def _acc_rows(ref, val, first):
    @pl.when(first)
    def _():
        ref[...] = val

    @pl.when(jnp.logical_not(first))
    def _():
        ref[...] += val


def _modnorm_tile(x_ref, g_ref, sc_ref, sh_ref):
    xv = x_ref[...]
    r = lax.rsqrt(jnp.mean(xv * xv, axis=-1, keepdims=True) + EPS)
    return (((xv * r) * g_ref[...]) * (1.0 + sc_ref[...]) + sh_ref[...]).astype(BF)


def _ffn_up(x, g, sc, sh, wgu, seq, carry=None):
    t, d = x.shape
    fs = wgu.shape[-1]
    per = seq // TM

    def body(x_ref, g_ref, sc_ref, sh_ref, w_ref, h_ref, p_ref, q_ref, s_ref):
        hv = _modnorm_tile(x_ref, g_ref, sc_ref, sh_ref)
        h_ref[...] = hv
        for j in range(N_CHIPS):
            a = _nn(hv, w_ref[j, 0])
            u = _nn(hv, w_ref[j, 1])
            sig = jax.nn.sigmoid(a)
            q = a * sig
            p_ref[j] = (u * (sig * (1.0 + a * (1.0 - sig)))).astype(BF)
            q_ref[j] = q.astype(BF)
            s_ref[j] = (q * u).astype(BF)

    row = pl.BlockSpec((TM, d), lambda m: (m, 0))
    ex = pl.BlockSpec((None, 1, d), lambda m: (m // per, 0, 0))
    blk = pl.BlockSpec((N_CHIPS, TM, fs), lambda m: (0, m, 0))
    return _call(
        body, "ffn_up", (t // TM,),
        [row, pl.BlockSpec((1, d), lambda m: (0, 0)), ex, ex, _whole(wgu)],
        [row, blk, blk, blk],
        [_sds((t, d), BF)] + [_sds((N_CHIPS, t, fs), BF)] * 3,
        (x, g, sc, sh, wgu), carry=carry)


def _ffn_down(s, wd, x, gt, seq, coef, carry=None):
    _, t, fs = s.shape
    d = x.shape[-1]
    per = seq // TM

    def body(s_ref, w_ref, x_ref, gt_ref, f_ref, xo_ref):
        f = _nn(s_ref[0], w_ref[0, 0])
        for j in range(1, N_CHIPS):
            f = f + _nn(s_ref[j], w_ref[j, 0])
        f_ref[...] = f.astype(BF)
        xo_ref[...] = x_ref[...] + (coef * gt_ref[...]) * f

    row = pl.BlockSpec((TM, d), lambda m: (m, 0))
    return _call(
        body, "ffn_down", (t // TM,),
        [pl.BlockSpec((N_CHIPS, TM, fs), lambda m: (0, m, 0)), _whole(wd), row,
         pl.BlockSpec((None, 1, d), lambda m: (m // per, 0, 0))],
        [row, row], [_sds((t, d), BF), _sds((t, d), F32)], (s, wd, x, gt), carry=carry)


def _ffn_bwd_ds(dxo, gt, f, wd, p, q, seq, coef, carry=None):
    t, d = dxo.shape
    fs = p.shape[-1]
    per = seq // TM
    nb = t // seq

    def body(dxo_ref, gt_ref, f_ref, w_ref, p_ref, q_ref, da_ref, du_ref, df_ref, dgt_ref):
        m = pl.program_id(0)
        dxv = dxo_ref[...]
        df = ((coef * gt_ref[...]) * dxv).astype(BF)
        df_ref[...] = df
        _acc_rows(dgt_ref, coef * jnp.sum(dxv * f_ref[...].astype(F32), axis=0, keepdims=True), m % per == 0)
        for j in range(N_CHIPS):
            ds = _nt(df, w_ref[j, 0])
            da_ref[j] = (ds * p_ref[j].astype(F32)).astype(BF)
            du_ref[j] = (ds * q_ref[j].astype(F32)).astype(BF)

    row = pl.BlockSpec((TM, d), lambda m: (m, 0))
    blk = pl.BlockSpec((N_CHIPS, TM, fs), lambda m: (0, m, 0))
    ex = pl.BlockSpec((None, 1, d), lambda m: (m // per, 0, 0))
    return _call(
        body, "ffn_bwd_ds", (t // TM,),
        [row, ex, row, _whole(wd), blk, blk],
        [blk, blk, row, ex],
        [_sds((N_CHIPS, t, fs), BF), _sds((N_CHIPS, t, fs), BF), _sds((t, d), BF), _sds((nb, 1, d), F32)],
        (dxo, gt, f, wd, p, q), carry=carry)


TM_X = 256


def _ffn_bwd_x(dxo, gt, f, wd, p, q, wgu, x, g, sc, seq, coef):
    t, d = dxo.shape
    fs = p.shape[-1]
    per = seq // TM_X
    nb = t // seq

    def body(dxo_ref, gt_ref, f_ref, wd_ref, p_ref, q_ref, w_ref, x_ref, g_ref, sc_ref,
             da_ref, du_ref, df_ref, dgt_ref, dx_ref, dsh_ref, dsc_ref, dg_ref):
        m = pl.program_id(0)
        dxv = dxo_ref[...]
        df = ((coef * gt_ref[...]) * dxv).astype(BF)
        df_ref[...] = df
        _acc_rows(dgt_ref, coef * jnp.sum(dxv * f_ref[...].astype(F32), axis=0, keepdims=True), m % per == 0)
        dh = None
        for j in range(N_CHIPS):
            ds = _nt(df, wd_ref[j, 0])
            da = (ds * p_ref[j].astype(F32)).astype(BF)
            du = (ds * q_ref[j].astype(F32)).astype(BF)
            da_ref[j] = da
            du_ref[j] = du
            part = _nt(da, w_ref[j, 0]) + _nt(du, w_ref[j, 1])
            dh = part if dh is None else dh + part
        dx, dsh, dsc, dg = _modnorm_bwd_tile(dh, x_ref[...], g_ref[...], sc_ref[...], dxv)
        dx_ref[...] = dx
        _acc_rows(dsh_ref, dsh, m % per == 0)
        _acc_rows(dsc_ref, dsc, m % per == 0)
        _acc_rows(dg_ref, dg, m == 0)

    row = pl.BlockSpec((TM_X, d), lambda m: (m, 0))
    blk = pl.BlockSpec((N_CHIPS, TM_X, fs), lambda m: (0, m, 0))
    ex = pl.BlockSpec((None, 1, d), lambda m: (m // per, 0, 0))
    vec = pl.BlockSpec((1, d), lambda m: (0, 0))
    exs = _sds((nb, 1, d), F32)
    return pl.pallas_call(
        body, name="ffn_bwd_x", grid=(t // TM_X,),
        in_specs=[row, ex, row, _whole(wd), blk, blk, _whole(wgu), row, vec, ex],
        out_specs=[blk, blk, row, ex, row, ex, ex, vec],
        out_shape=[_sds((N_CHIPS, t, fs), BF), _sds((N_CHIPS, t, fs), BF), _sds((t, d), BF), exs,
                   _sds((t, d), F32), exs, exs, _sds((1, d), F32)],
        compiler_params=_cp(1))(dxo, gt, f, wd, p, q, wgu, x, g, sc)


TK_W = 1024


def _ffn_bwd_w(h, da, du, s, df, terms=(0, 1, 2), carry=None):
    t, d = h.shape
    fs = da.shape[-1]
    row = pl.BlockSpec((TK_W, d), lambda j, kt: (kt, 0))
    blk = pl.BlockSpec((None, TK_W, fs), lambda j, kt: (j, kt, 0))
    args, specs, idx = [], [], []

    def operand(a, spec):
        for i, o in enumerate(args):
            if o is a:
                return i
        args.append(a)
        specs.append(spec)
        return len(args) - 1

    for term in terms:
        lhs, rhs = ((da, h), (du, h), (s, df))[term]
        idx.append((operand(lhs, blk), operand(rhs, row)))

    def body(*refs):
        o_ref = refs[-1]
        kt = pl.program_id(1)
        parts = [_tn(refs[a][...], refs[b][...]) for a, b in idx]

        @pl.when(kt == 0)
        def _():
            for i, p in enumerate(parts):
                o_ref[i] = p

        @pl.when(kt != 0)
        def _():
            for i, p in enumerate(parts):
                o_ref[i] += p

    return _call(
        body, "ffn_bwd_w", (N_CHIPS, t // TK_W), specs,
        [pl.BlockSpec((None, len(terms), fs, d), lambda j, kt: (j, 0, 0, 0))],
        [_sds((N_CHIPS, len(terms), fs, d), F32)], args, carry=carry)


def _ffn_bwd_dh(da, du, wgu, x, g, sc, dxo, seq, carry=None):
    _, t, fs = da.shape
    d = x.shape[-1]
    per = seq // TM
    nb = t // seq

    def body(da_ref, du_ref, w_ref, x_ref, g_ref, sc_ref, dxo_ref, dx_ref, dsh_ref, dsc_ref, dg_ref):
        m = pl.program_id(0)
        dh = _nt(da_ref[0], w_ref[0, 0]) + _nt(du_ref[0], w_ref[0, 1])
        for j in range(1, N_CHIPS):
            dh = dh + _nt(da_ref[j], w_ref[j, 0]) + _nt(du_ref[j], w_ref[j, 1])
        dx, dsh, dsc, dg = _modnorm_bwd_tile(dh, x_ref[...], g_ref[...], sc_ref[...], dxo_ref[...])
        dx_ref[...] = dx
        _acc_rows(dsh_ref, dsh, m % per == 0)
        _acc_rows(dsc_ref, dsc, m % per == 0)
        _acc_rows(dg_ref, dg, m == 0)

    row = pl.BlockSpec((TM, d), lambda m: (m, 0))
    blk = pl.BlockSpec((N_CHIPS, TM, fs), lambda m: (0, m, 0))
    ex = pl.BlockSpec((None, 1, d), lambda m: (m // per, 0, 0))
    vec = pl.BlockSpec((1, d), lambda m: (0, 0))
    return _call(
        body, "ffn_bwd_dh", (t // TM,),
        [blk, blk, _whole(wgu), row, vec, ex, row],
        [row, ex, ex, vec],
        [_sds((t, d), F32), _sds((nb, 1, d), F32), _sds((nb, 1, d), F32), _sds((1, d), F32)],
        (da, du, wgu, x, g, sc, dxo), carry=carry)


def _qkv_proj(x, g, sc, sh, w_in, seq, carry=None):
    t, d = x.shape
    wc = w_in.shape[-1]
    per = seq // TM

    dils = [dil for _, dil in DIL_CONFIGS if dil > 1]

    def body(x_ref, g_ref, sc_ref, sh_ref, w_ref, h_ref, o_ref, *rest):
        res_refs, buf = rest[:len(dils)], rest[len(dils)]
        hv = _modnorm_tile(x_ref, g_ref, sc_ref, sh_ref)
        h_ref[...] = hv
        for j in range(N_CHIPS):
            rf = _nn(hv, w_ref[j, 0])
            r = rf.astype(BF)
            for a, lc, off, width in _col_pieces(j, wc):
                o_ref[a, :, lc:lc + width] = r[:, off:off + width]
                if a < 3:
                    continue
                for c0 in range(0, width, 128):
                    cg = (lc + c0) // 128
                    buf[...] = rf[:, off + c0:off + c0 + 128]
                    for ref, dil in zip(res_refs, dils):
                        for rr in range(dil):
                            ref[a - 3, :, rr * D_GRP + cg * 128:rr * D_GRP + (cg + 1) * 128] = (
                                buf[pl.ds(rr, TM // dil, stride=dil), :].astype(BF))

    row = pl.BlockSpec((TM, d), lambda m: (m, 0))
    ex = pl.BlockSpec((None, 1, d), lambda m: (m // per, 0, 0))
    return _call(
        body, "qkv_proj", (t // TM,),
        [row, pl.BlockSpec((1, d), lambda m: (0, 0)), ex, ex, _whole(w_in)],
        [row, pl.BlockSpec((6, TM, D_GRP), lambda m: (0, m, 0))]
        + [pl.BlockSpec((3, TM // dil, dil * D_GRP), lambda m: (0, m, 0)) for dil in dils],
        [_sds((t, d), BF), _sds((6, t, D_GRP), BF)] + [_sds((3, t // dil, dil * D_GRP), BF) for dil in dils],
        (x, g, sc, sh, w_in), scratch=[pltpu.VMEM((TM, 128), F32)], carry=carry)


def _col_pieces(j, wc):
    out, off = [], 0
    while off < wc:
        a, lc = divmod(j * wc + off, D_GRP)
        width = min(D_GRP - lc, wc - off)
        out.append((a, lc, off, width))
        off += width
    return out


def _chip_cols(g6_ref, j, wc):
    return jnp.concatenate([g6_ref[a, :, lc:lc + width] for a, lc, _, width in _col_pieces(j, wc)], axis=1)


def _mix_out(on_sb, on_dil, w_out, x, gt, seq):
    t, d = x.shape
    per = seq // TM

    def body(a_ref, b_ref, w_ref, x_ref, gt_ref, t_ref, xo_ref):
        tv = _nn(a_ref[...], w_ref[0:D_GRP, :]) + _nn(b_ref[...], w_ref[D_GRP:2 * D_GRP, :])
        t_ref[...] = tv.astype(BF)
        xo_ref[...] = x_ref[...] + gt_ref[...] * tv

    row = pl.BlockSpec((TM, d), lambda m: (m, 0))
    half = pl.BlockSpec((TM, D_GRP), lambda m: (m, 0))
    return pl.pallas_call(
        body, name="mix_out", grid=(t // TM,),
        in_specs=[half, half, pl.BlockSpec((2 * D_GRP, d), lambda m: (0, 0)), row,
                  pl.BlockSpec((None, 1, d), lambda m: (m // per, 0, 0))],
        out_specs=[row, row],
        out_shape=[_sds((t, d), BF), _sds((t, d), F32)],
        compiler_params=_cp(1))(on_sb, on_dil, w_out, x, gt)


def _sb_masks():
    lane = lax.broadcasted_iota(jnp.int32, (1, 2 * HEAD_DIM), 1)
    hm0 = lane < HEAD_DIM
    rel = lax.broadcasted_iota(jnp.int32, (TQ, KB), 0) - lax.broadcasted_iota(jnp.int32, (TQ, KB), 1)
    kr = lax.broadcasted_iota(jnp.int32, (KB, KB), 0)
    kc = lax.broadcasted_iota(jnp.int32, (KB, KB), 1)
    return hm0, rel, kr, kc


def _headnorm_pair(o, gv, hm0):
    o2 = o * o
    ms0 = jnp.sum(jnp.where(hm0, o2, 0.0), axis=-1, keepdims=True) * (1.0 / HEAD_DIM)
    ms1 = jnp.sum(jnp.where(hm0, 0.0, o2), axis=-1, keepdims=True) * (1.0 / HEAD_DIM)
    r = jnp.where(hm0, lax.rsqrt(ms0 + EPS), lax.rsqrt(ms1 + EPS))
    return (o * r) * gv


SB_DEAD = -104.0


def _alive(c_l):
    return (jnp.max(c_l) > SB_DEAD).astype(jnp.int32)


def _sb_fwd(qkv6, g_sb, nb, seq, carry=None):
    nq = seq // TQ

    def body(q_ref, k_ref, v_ref, g_ref, o_ref, on_ref):
        qi = pl.program_id(2)
        hm0, rel, kr, kc = _sb_masks()
        upper = _twice((kr > kc).astype(BF))
        heads = _dil_masks()[0]
        qs = _stack_heads(q_ref[...] * SCALE, heads)
        causal2 = jnp.concatenate([rel] * GRP_HEADS, axis=0) > 0

        def block(kj, causal, c_l, acc):
            ks = pl.multiple_of(kj * KB, KB)
            z = _nt(qs, k_ref[pl.ds(ks, KB), :])
            sp = _softplus(z)
            spm = sp if causal is None else jnp.where(causal, sp, 0.0)
            suf = _nn2(spm, upper)
            w = jnp.exp((z - sp) + (c_l - suf))
            if causal is not None:
                w = jnp.where(causal, w, 0.0)
            return c_l - (suf[:, 0:1] + spm[:, 0:1]), acc + _nn(w.astype(BF), v_ref[pl.ds(ks, KB), :])

        c_l, acc = block(qi, causal2, jnp.zeros((GRP_HEADS * TQ, 1), F32), jnp.zeros((GRP_HEADS * TQ, GRP_W), F32))

        def cond(carry):
            return jnp.logical_and(carry[0] <= qi, carry[1] > 0)

        def kbody(carry):
            it, _, c_l, acc = carry
            c_l, acc = block(qi - it, None, c_l, acc)
            return it + 1, _alive(c_l), c_l, acc

        acc = lax.while_loop(cond, kbody, (jnp.int32(1), _alive(c_l), c_l, acc))[3]
        o = _unstack_heads(acc, heads, TQ)
        o_ref[...] = o.astype(BF)
        gv = g_ref[...]
        for half in range(GRP_W // 128):
            lanes = slice(half * 128, (half + 1) * 128)
            on_ref[:, lanes] = _headnorm_pair(o[:, lanes], gv[:, lanes], hm0).astype(BF)

    w = GRP_W
    full = lambda i: pl.BlockSpec((None, None, seq, w), lambda b, hp, q: (i, b, 0, hp))
    qblk = pl.BlockSpec((None, None, TQ, w), lambda b, hp, q: (0, b, q, hp))
    oblk = pl.BlockSpec((None, TQ, w), lambda b, hp, q: (b, q, hp))
    return _call(
        body, "sb_fwd", (nb, N_HEADS // GRP_HEADS, nq),
        [qblk, full(1), full(2), pl.BlockSpec((1, w), lambda b, hp, q: (0, hp))],
        [oblk, oblk],
        [_sds((nb, seq, D_GRP), BF), _sds((nb, seq, D_GRP), BF)],
        (qkv6, qkv6, qkv6, g_sb), carry=carry)


def _sb_bwd(qkv6, do, nb, seq, carry=None):
    nq = seq // TQ
    nk = seq // KB

    def body(q_ref, k_ref, v_ref, do_ref, out_ref, dk_acc, dv_acc, g_st, s_st):
        qi = pl.program_id(2)
        hm0, rel, kr, kc = _sb_masks()
        upper = _twice((kr > kc).astype(BF))
        lower = (kr < kc).astype(BF)

        @pl.when(qi == 0)
        def _():
            dk_acc[...] = jnp.zeros_like(dk_acc)
            dv_acc[...] = jnp.zeros_like(dv_acc)

        heads = _dil_masks()[0]
        qs = _stack_heads(q_ref[...] * SCALE, heads)
        dos = _stack_heads(do_ref[...], heads)
        causal2 = jnp.concatenate([rel] * GRP_HEADS, axis=0) > 0

        def weights(kj, causal, c_l):
            ks = pl.multiple_of(kj * KB, KB)
            vb = v_ref[pl.ds(ks, KB), :]
            z = _nt(qs, k_ref[pl.ds(ks, KB), :])
            sp = _softplus(z)
            spm = sp if causal is None else jnp.where(causal, sp, 0.0)
            suf = _nn2(spm, upper)
            lsz = z - sp
            w = jnp.exp(lsz + (c_l - suf))
            if causal is not None:
                w = jnp.where(causal, w, 0.0)
            g_st[kj] = w * _nt(dos, vb)
            s_st[kj] = jnp.exp(lsz)
            dv_acc[pl.ds(ks, KB), :] += _tn(w.astype(BF), dos)
            return c_l - (suf[:, 0:1] + spm[:, 0:1])

        zc = jnp.zeros((GRP_HEADS * TQ, 1), F32)
        c_l = weights(qi, causal2, zc)

        def acond(carry):
            return jnp.logical_and(carry[0] <= qi, carry[1] > 0)

        def abody(carry):
            c_l = weights(qi - carry[0], None, carry[2])
            return carry[0] + 1, _alive(c_l), c_l

        n_used = lax.while_loop(acond, abody, (jnp.int32(1), _alive(c_l), c_l))[0]

        def grads(kj, causal, c_g, dq):
            ks = pl.multiple_of(kj * KB, KB)
            kb = k_ref[pl.ds(ks, KB), :]
            g = g_st[kj]
            sig = s_st[kj]
            pre = _nn(g.astype(BF), lower)
            dz = g * (1.0 - sig) - sig * (pre + c_g)
            if causal is not None:
                dz = jnp.where(causal, dz, 0.0)
            dzb = dz.astype(BF)
            dk_acc[pl.ds(ks, KB), :] += _tn(dzb, qs)
            return c_g + (pre[:, KB - 1:KB] + g[:, KB - 1:KB]), dq + _nn(dzb, kb)

        c_g, dq = lax.fori_loop(qi - n_used + 1, qi, lambda kj, cr: grads(kj, None, *cr),
                                (zc, jnp.zeros((GRP_HEADS * TQ, GRP_W), F32)))
        _, dq = grads(qi, causal2, c_g, dq)
        dq = _unstack_heads(dq, heads, TQ) * SCALE
        out_ref[0, pl.ds(pl.multiple_of(qi * TQ, TQ), TQ), :] = dq.astype(BF)

        @pl.when(qi == nq - 1)
        def _():
            out_ref[1] = dk_acc[...].astype(BF)
            out_ref[2] = dv_acc[...].astype(BF)

    w = GRP_W
    full = lambda i: pl.BlockSpec((None, None, seq, w), lambda b, hp, q: (i, b, 0, hp))
    qblk = pl.BlockSpec((None, None, TQ, w), lambda b, hp, q: (0, b, q, hp))
    oblk = pl.BlockSpec((None, TQ, w), lambda b, hp, q: (b, q, hp))
    return _call(
        body, "sb_bwd", (nb, N_HEADS // GRP_HEADS, nq),
        [qblk, full(1), full(2), oblk],
        [pl.BlockSpec((3, None, seq, w), lambda b, hp, q: (0, b, 0, hp))],
        [_sds((6, nb, seq, D_GRP), BF)], (qkv6, qkv6, qkv6, do),
        scratch=[pltpu.VMEM((seq, w), F32), pltpu.VMEM((seq, w), F32),
                 pltpu.VMEM((nk, GRP_HEADS * TQ, KB), F32), pltpu.VMEM((nk, GRP_HEADS * TQ, KB), F32)],
        carry=carry)


def _t5_bucket(n):
    max_exact = N_BUCKETS // 2
    nf = np.maximum(n, 1).astype(np.float32)
    large = max_exact + (np.log(nf / max_exact) / math.log(MAX_DISTANCE / max_exact)
                         * (N_BUCKETS - max_exact)).astype(np.int32)
    large = np.minimum(large, N_BUCKETS - 1)
    return np.where(n < max_exact, n, large).astype(np.int32)


def _bucket_map(dilation):
    step = BLOCK + np.arange(BLOCK)[:, None] - np.arange(2 * BLOCK)[None, :]
    return _t5_bucket(np.clip(step, 0, N_STEPS) * dilation)


GRP_HEADS = 4
GRP_W = GRP_HEADS * HEAD_DIM


def _dil_masks():
    lane = lax.broadcasted_iota(jnp.int32, (1, GRP_W), 1)
    heads = [jnp.logical_and(lane >= HEAD_DIM * i, lane < HEAD_DIM * (i + 1)) for i in range(GRP_HEADS)]
    iq = jnp.bitwise_and(lax.broadcasted_iota(jnp.int32, (GRP_HEADS * BLOCK, BLOCK), 0), BLOCK - 1)
    ik = lax.broadcasted_iota(jnp.int32, (GRP_HEADS * BLOCK, BLOCK), 1)
    return heads, ik <= iq, ik >= iq


def _stack_heads(x, heads):
    zero = jnp.zeros_like(x)
    return jnp.concatenate([jnp.where(hm, x, zero) for hm in heads], axis=0)


def _unstack_heads(xs, heads, rows=BLOCK):
    out = xs[0:rows]
    for i in range(1, GRP_HEADS):
        out = jnp.where(heads[i], xs[i * rows:(i + 1) * rows], out)
    return out


def _dil_rows(n):
    rs = pl.multiple_of(n * BLOCK, BLOCK)
    ps = pl.multiple_of(jnp.maximum(n - 1, 0) * BLOCK, BLOCK)
    return pl.ds(rs, BLOCK), pl.ds(ps, BLOCK)


def _dil_probs(qs, kc, kp, b_ref, gi, valid_c, valid_p):
    rows = slice(gi * GRP_HEADS * BLOCK, (gi + 1) * GRP_HEADS * BLOCK)
    zc = _nt(qs, kc) * SCALE + b_ref[rows, BLOCK:2 * BLOCK]
    zp = _nt(qs, kp) * SCALE + b_ref[rows, 0:BLOCK]
    zc = jnp.where(valid_c, zc, NEG_INF)
    zp = jnp.where(valid_p, zp, NEG_INF)
    m = jnp.maximum(jnp.max(zc, axis=-1, keepdims=True), jnp.max(zp, axis=-1, keepdims=True))
    ec = jnp.exp(zc - m)
    ep = jnp.exp(zp - m)
    den = jnp.sum(ec, axis=-1, keepdims=True) + jnp.sum(ep, axis=-1, keepdims=True)
    return ec, ep, den, m


def _dil_fwd(qkv6r, base, bias, nb, sub_len, dilation):
    n_blk = sub_len // BLOCK

    def body(q_ref, k_ref, v_ref, b_ref, o_ref, l_ref):
        heads, valid_c, valid_p0 = _dil_masks()

        def nbody(n, carry):
            cur, prev = _dil_rows(n)
            valid_p = jnp.logical_and(valid_p0, n > 0)
            for gi in range(N_HEADS // GRP_HEADS):
                lanes = slice(gi * GRP_W, (gi + 1) * GRP_W)
                qs = _stack_heads(q_ref[cur, lanes], heads)
                ec, ep, den, m = _dil_probs(qs, k_ref[cur, lanes], k_ref[prev, lanes], b_ref, gi, valid_c, valid_p)
                o = (_nn(ec.astype(BF), v_ref[cur, lanes]) + _nn(ep.astype(BF), v_ref[prev, lanes])) / den
                o_ref[cur, lanes] = _unstack_heads(o, heads).astype(BF)
                l_ref[cur, lanes] = _unstack_heads(jnp.broadcast_to(m + jnp.log(den), o.shape), heads)
            return carry

        lax.fori_loop(0, n_blk, nbody, 0)

    seqblk = lambda i: pl.BlockSpec((None, None, sub_len, D_GRP), lambda b, r: (i, b, 0, r))
    oblk = pl.BlockSpec((None, sub_len, D_GRP), lambda b, r: (b, 0, r))
    shp = _sds((nb, sub_len, dilation * D_GRP), F32)
    return pl.pallas_call(
        body, name="dil_fwd_%d" % dilation, grid=(nb, dilation),
        in_specs=[seqblk(base), seqblk(base + 1), seqblk(base + 2), _whole(bias)],
        out_specs=[oblk, oblk], out_shape=[_sds(shp.shape, BF), shp],
        compiler_params=_cp(2))(qkv6r, qkv6r, qkv6r, bias)


def _dil_bwd(qkv6r, base, bias, do_c, dd_c, nb, sub_len, dilation, carry=None):
    n_blk = sub_len // BLOCK

    def body(q_ref, k_ref, v_ref, b_ref, do_ref, dd_ref, out_ref, a_ref, dk_acc, dv_acc):
        heads, valid_c, valid_p0 = _dil_masks()
        first = jnp.logical_and(pl.program_id(0) == 0, pl.program_id(1) == 0)

        @pl.when(first)
        def _():
            a_ref[...] = jnp.zeros_like(a_ref)

        dk_acc[...] = jnp.zeros_like(dk_acc)
        dv_acc[...] = jnp.zeros_like(dv_acc)

        def nbody(n, carry):
            cur, prev = _dil_rows(n)
            valid_p = jnp.logical_and(valid_p0, n > 0)
            for gi in range(N_HEADS // GRP_HEADS):
                lanes = slice(gi * GRP_W, (gi + 1) * GRP_W)
                kc, kp = k_ref[cur, lanes], k_ref[prev, lanes]
                vc, vp = v_ref[cur, lanes], v_ref[prev, lanes]
                qs = _stack_heads(q_ref[cur, lanes], heads)
                dos = _stack_heads(do_ref[cur, lanes], heads).astype(BF)
                dds = jnp.sum(_stack_heads(dd_ref[cur, lanes], heads), axis=-1, keepdims=True) * (1.0 / HEAD_DIM)
                ec, ep, den, _ = _dil_probs(qs, kc, kp, b_ref, gi, valid_c, valid_p)
                inv = 1.0 / den
                pc = ec * inv
                pp = ep * inv
                dzc = pc * (_nt(dos, vc) + dds)
                dzp = pp * (_nt(dos, vp) + dds)
                rows = slice(gi * GRP_HEADS * BLOCK, (gi + 1) * GRP_HEADS * BLOCK)
                a_ref[rows, BLOCK:2 * BLOCK] += dzc
                a_ref[rows, 0:BLOCK] += dzp
                dzcb = (dzc * SCALE).astype(BF)
                dzpb = (dzp * SCALE).astype(BF)
                out_ref[0, cur, lanes] = _unstack_heads(_nn(dzcb, kc) + _nn(dzpb, kp), heads).astype(BF)
                dk_acc[cur, lanes] += _tn(dzcb, qs)
                dk_acc[prev, lanes] += _tn(dzpb, qs)
                dv_acc[cur, lanes] += _tn(pc.astype(BF), dos)
                dv_acc[prev, lanes] += _tn(pp.astype(BF), dos)
            return carry

        lax.fori_loop(0, n_blk, nbody, 0)
        out_ref[1] = dk_acc[...].astype(BF)
        out_ref[2] = dv_acc[...].astype(BF)

    seqblk = lambda i: pl.BlockSpec((None, None, sub_len, D_GRP), lambda b, r: (i, b, 0, r))
    oblk = pl.BlockSpec((None, sub_len, D_GRP), lambda b, r: (b, 0, r))
    return _call(
        body, "dil_bwd_%d" % dilation, (nb, dilation),
        [seqblk(base), seqblk(base + 1), seqblk(base + 2), _whole(bias), oblk, oblk],
        [pl.BlockSpec((3, None, sub_len, D_GRP), lambda b, r: (0, b, 0, r)),
         pl.BlockSpec((N_HEADS * BLOCK, 2 * BLOCK), lambda b, r: (0, 0))],
        [_sds((3, nb, sub_len, dilation * D_GRP), BF), _sds((N_HEADS * BLOCK, 2 * BLOCK), F32)],
        (qkv6r, qkv6r, qkv6r, bias, do_c, dd_c),
        scratch=[pltpu.VMEM((sub_len, D_GRP), F32)] * 2, carry=carry)


def _group_ones():
    idx = np.arange(D_GRP) // HEAD_DIM
    return jnp.asarray((idx[:, None] == idx[None, :]).astype(np.float32), dtype=BF)


def _dil_alphas(l1, l4, l16):
    mx = jnp.maximum(jnp.maximum(l1, l4), l16)
    e1 = jnp.exp(l1 - mx)
    e4 = jnp.exp(l4 - mx)
    e16 = jnp.exp(l16 - mx)
    den = e1 + e4 + e16
    return e1 / den, e4 / den, e16 / den


def _residue_spec(dil):
    return pl.BlockSpec((TM // dil, dil * D_GRP), lambda m: (m, 0))


def _from_residue(src, dil, cg, buf):
    if dil == 1:
        return src[:, cg * 128:(cg + 1) * 128].astype(F32)
    for r in range(dil):
        buf[pl.ds(r, TM // dil, stride=dil), :] = (
            src[:, r * D_GRP + cg * 128:r * D_GRP + (cg + 1) * 128].astype(F32))
    return buf[...]


def _to_residue(dst, dil, cg, buf, val):
    if dil == 1:
        dst[:, cg * 128:(cg + 1) * 128] = val.astype(dst.dtype)
        return
    buf[...] = val
    for r in range(dil):
        dst[:, r * D_GRP + cg * 128:r * D_GRP + (cg + 1) * 128] = (
            buf[pl.ds(r, TM // dil, stride=dil), :].astype(dst.dtype))


def _pair_sum(x, hm0):
    s0 = jnp.sum(jnp.where(hm0, x, 0.0), axis=-1, keepdims=True)
    s1 = jnp.sum(jnp.where(hm0, 0.0, x), axis=-1, keepdims=True)
    return jnp.where(hm0, s0, s1)


def _dil_comb(os, ls, g_dil):
    t = os[0].shape[0]
    dils = [dil for _, dil in DIL_CONFIGS]

    def body(o1, l1, o4, l4, o16, l16, g_ref, o_ref, on_ref, b0, b1, b2, b3):
        hm0 = lax.broadcasted_iota(jnp.int32, (1, 128), 1) < HEAD_DIM
        for cg in range(D_GRP // 128):
            lanes = slice(cg * 128, (cg + 1) * 128)
            ov = [_from_residue(src, dil, cg, buf) for src, dil, buf in zip((o1, o4, o16), dils, (None, b0, b1))]
            lv = [_from_residue(src, dil, cg, buf) for src, dil, buf in zip((l1, l4, l16), dils, (None, b2, b3))]
            a1, a4, a16 = _dil_alphas(*lv)
            o = a1 * ov[0] + a4 * ov[1] + a16 * ov[2]
            o_ref[:, lanes] = o.astype(BF)
            on_ref[:, lanes] = _headnorm_pair(o, g_ref[:, lanes], hm0).astype(BF)

    blk = pl.BlockSpec((TM, D_GRP), lambda m: (m, 0))
    specs = [_residue_spec(dil) for dil in dils for _ in range(2)]
    return pl.pallas_call(
        body, name="dil_comb", grid=(t // TM,),
        in_specs=specs + [pl.BlockSpec((1, D_GRP), lambda m: (0, 0))],
        out_specs=[blk, blk],
        out_shape=[_sds((t, D_GRP), BF), _sds((t, D_GRP), BF)],
        scratch_shapes=[pltpu.VMEM((TM, 128), F32)] * 4,
        compiler_params=_cp(1))(os[0], ls[0], os[1], ls[1], os[2], ls[2], g_dil)


def _dil_comb_bwd(do, os, ls):
    t = do.shape[0]
    dils = [dil for _, dil in DIL_CONFIGS]

    def body(do_ref, o1, l1, o4, l4, o16, l16, d1, d4, d16, e1, e4, e16, b0, b1, b2, b3):
        hm0 = lax.broadcasted_iota(jnp.int32, (1, 128), 1) < HEAD_DIM
        for cg in range(D_GRP // 128):
            dov = do_ref[:, cg * 128:(cg + 1) * 128].astype(F32)
            ov = [_from_residue(src, dil, cg, buf) for src, dil, buf in zip((o1, o4, o16), dils, (None, b0, b1))]
            lv = [_from_residue(src, dil, cg, buf) for src, dil, buf in zip((l1, l4, l16), dils, (None, b2, b3))]
            al = _dil_alphas(*lv)
            sbar = al[0] * _pair_sum(dov * ov[0], hm0)
            for a_c, o_c in zip(al[1:], ov[1:]):
                sbar = sbar + a_c * _pair_sum(dov * o_c, hm0)
            for a_c, dil, dref, eref in zip(al, dils, (d1, d4, d16), (e1, e4, e16)):
                _to_residue(dref, dil, cg, b0, a_c * dov)
                _to_residue(eref, dil, cg, b1, -a_c * sbar)

    specs = [_residue_spec(dil) for dil in dils]
    return pl.pallas_call(
        body, name="dil_comb_bwd", grid=(t // TM,),
        in_specs=[pl.BlockSpec((TM, D_GRP), lambda m: (m, 0))] + [sp for sp in specs for _ in range(2)],
        out_specs=specs + specs,
        out_shape=[_sds((t // dil, dil * D_GRP), BF) for dil in dils]
        + [_sds((t // dil, dil * D_GRP), F32) for dil in dils],
        scratch_shapes=[pltpu.VMEM((TM, 128), F32)] * 4,
        compiler_params=_cp(1))(do, os[0], ls[0], os[1], ls[1], os[2], ls[2])


def _dqkv_dil_sum(ds, dqkv6):
    t = dqkv6.shape[1]
    dils = [dil for _, dil in DIL_CONFIGS]

    def body(*refs):
        srcs, o_ref, acc = refs[:len(dils)], refs[len(dils) + 1], refs[len(dils) + 2]
        for a in range(3):
            for cg in range(D_GRP // 128):
                for src, dil in zip(srcs, dils):
                    for r in range(dil):
                        part = src[a, :, r * D_GRP + cg * 128:r * D_GRP + (cg + 1) * 128].astype(F32)
                        rows = pl.ds(r, TM // dil, stride=dil) if dil > 1 else slice(None)
                        if dil == dils[0]:
                            acc[rows, :] = part
                        else:
                            acc[rows, :] += part
                o_ref[a, :, cg * 128:(cg + 1) * 128] = acc[...].astype(BF)

    return pl.pallas_call(
        body, name="dqkv_dil_sum", grid=(t // TM,),
        in_specs=[pl.BlockSpec((3, TM // dil, dil * D_GRP), lambda m: (0, m, 0)) for dil in dils]
        + [pl.BlockSpec(memory_space=pl.ANY)],
        out_specs=pl.BlockSpec((3, TM, D_GRP), lambda m: (1, m, 0)),
        out_shape=_sds((6, t, D_GRP), BF), input_output_aliases={len(dils): 0},
        scratch_shapes=[pltpu.VMEM((TM, 128), F32)],
        compiler_params=_cp(1))(*ds, dqkv6)


def _relbias_grad(a_all, onehot):
    def body(a_ref, oh_ref, o_ref):
        acc = jnp.zeros((N_HEADS, N_BUCKETS), F32)
        for c in range(len(DIL_CONFIGS)):
            av = a_ref[c]
            hi = av.astype(BF)
            lo = (av - hi.astype(F32)).astype(BF)
            acc = acc + _nt(hi, oh_ref[c]) + _nt(lo, oh_ref[c])
        o_ref[...] = acc

    return pl.pallas_call(body, name="relbias_grad", out_shape=_sds((N_HEADS, N_BUCKETS), F32),
                          compiler_params=_cp())(a_all, onehot)


def _headnorm_bwd(dn, o, gv, mv):
    ms = _nn2(o * o, mv) * (1.0 / HEAD_DIM)
    r = lax.rsqrt(ms + EPS)
    nrm = o * r
    dg = jnp.sum(dn * nrm, axis=0, keepdims=True)
    dnn = dn * gv
    do = r * (dnn - nrm * (_nn2(dnn * nrm, mv) * (1.0 / HEAD_DIM)))
    return do, dg


def _mix_bwd_out(dx, gt, tv, w_out, o_sb, o_dil, on_sb, on_dil, g_sb, g_dil, ones_g, seq, carry=None):
    t, d = dx.shape
    per = seq // TM
    nb = t // seq

    def body(dx_ref, gt_ref, t_ref, w_ref, osb, odl, onsb, ondl, gsb, gdl, m_ref,
             dosb, dodl, dgt_ref, dgsb, dgdl, dw_ref):
        m = pl.program_id(0)
        dxv = dx_ref[...]
        dt = (gt_ref[...] * dxv).astype(BF)
        _acc_rows(dgt_ref, jnp.sum(dxv * t_ref[...].astype(F32), axis=0, keepdims=True), m % per == 0)
        mv = _twice(m_ref[...])
        don_sb = _nt(dt, w_ref[0:D_GRP, :])
        don_dl = _nt(dt, w_ref[D_GRP:2 * D_GRP, :])
        do1, dg1 = _headnorm_bwd(don_sb, osb[...].astype(F32), gsb[...], mv)
        do2, dg2 = _headnorm_bwd(don_dl, odl[...].astype(F32), gdl[...], mv)
        dosb[...] = do1.astype(BF)
        dodl[...] = do2.astype(BF)
        _acc_rows(dgsb, dg1, m == 0)
        _acc_rows(dgdl, dg2, m == 0)
        p1 = _tn(onsb[...], dt)
        p2 = _tn(ondl[...], dt)

        @pl.when(m == 0)
        def _():
            dw_ref[0:D_GRP, :] = p1
            dw_ref[D_GRP:2 * D_GRP, :] = p2

        @pl.when(m != 0)
        def _():
            dw_ref[0:D_GRP, :] += p1
            dw_ref[D_GRP:2 * D_GRP, :] += p2

    row = pl.BlockSpec((TM, d), lambda m: (m, 0))
    half = pl.BlockSpec((TM, D_GRP), lambda m: (m, 0))
    ex = pl.BlockSpec((None, 1, d), lambda m: (m // per, 0, 0))
    gvec = pl.BlockSpec((1, D_GRP), lambda m: (0, 0))
    wblk = pl.BlockSpec((2 * D_GRP, d), lambda m: (0, 0))
    return _call(
        body, "mix_bwd_out", (t // TM,),
        [row, ex, row, wblk, half, half, half, half, gvec, gvec, pl.BlockSpec((D_GRP, D_GRP), lambda m: (0, 0))],
        [half, half, ex, gvec, gvec, wblk],
        [_sds((t, D_GRP), BF), _sds((t, D_GRP), BF), _sds((nb, 1, d), F32),
         _sds((1, D_GRP), F32), _sds((1, D_GRP), F32), _sds((2 * D_GRP, d), F32)],
        (dx, gt, tv, w_out, o_sb, o_dil, on_sb, on_dil, g_sb, g_dil, ones_g), carry=carry)


def _dw_in(h, dqkv6, carry=None):
    t, d = h.shape
    wc = 6 * D_GRP // N_CHIPS

    def body(h_ref, g_ref, o_ref):
        kt = pl.program_id(0)
        hv = h_ref[...]
        for j in range(N_CHIPS):
            p = _tn(hv, _chip_cols(g_ref, j, wc))

            @pl.when(kt == 0)
            def _(p=p, j=j):
                o_ref[j, 0] = p

            @pl.when(kt != 0)
            def _(p=p, j=j):
                o_ref[j, 0] += p

    return _call(
        body, "dw_in", (t // TK_W,),
        [pl.BlockSpec((TK_W, d), lambda kt: (kt, 0)), pl.BlockSpec((6, TK_W, D_GRP), lambda kt: (0, kt, 0))],
        [pl.BlockSpec((N_CHIPS, 1, d, wc), lambda kt: (0, 0, 0, 0))],
        [_sds((N_CHIPS, 1, d, wc), F32)], (h, dqkv6), carry=carry)


def _mix_bwd_dh(dqkv6, w_in, x, g, sc, dxo, seq, carry=None):
    _, t, _ = dqkv6.shape
    d = x.shape[-1]
    wc = w_in.shape[-1]
    per = seq // TM
    nb = t // seq

    def body(g6_ref, w_ref, x_ref, g_ref, sc_ref, dxo_ref, dx_ref, dsh_ref, dsc_ref, dg_ref):
        m = pl.program_id(0)
        dh = _nt(_chip_cols(g6_ref, 0, wc), w_ref[0, 0])
        for j in range(1, N_CHIPS):
            dh = dh + _nt(_chip_cols(g6_ref, j, wc), w_ref[j, 0])
        dx, dsh, dsc, dg = _modnorm_bwd_tile(dh, x_ref[...], g_ref[...], sc_ref[...], dxo_ref[...])
        dx_ref[...] = dx
        _acc_rows(dsh_ref, dsh, m % per == 0)
        _acc_rows(dsc_ref, dsc, m % per == 0)
        _acc_rows(dg_ref, dg, m == 0)

    row = pl.BlockSpec((TM, d), lambda m: (m, 0))
    ex = pl.BlockSpec((None, 1, d), lambda m: (m // per, 0, 0))
    vec = pl.BlockSpec((1, d), lambda m: (0, 0))
    return _call(
        body, "mix_bwd_dh", (t // TM,),
        [pl.BlockSpec((6, TM, D_GRP), lambda m: (0, m, 0)), _whole(w_in), row, vec, ex, row],
        [row, ex, ex, vec],
        [_sds((t, d), F32), _sds((nb, 1, d), F32), _sds((nb, 1, d), F32), _sds((1, d), F32)],
        (dqkv6, w_in, x, g, sc, dxo), carry=carry)


def _ffn_down_loss(s, wd, x, gt, seq, coef, g, target):
    _, t, fs = s.shape
    d = x.shape[-1]
    per = seq // TM
    steps = t // TM

    def body(s_ref, w_ref, x_ref, gt_ref, g_ref, t_ref, f_ref, dx_ref, dg_ref, loss_ref, lacc):
        m = pl.program_id(0)
        f = _nn(s_ref[0], w_ref[0, 0])
        for j in range(1, N_CHIPS):
            f = f + _nn(s_ref[j], w_ref[j, 0])
        f_ref[...] = f.astype(BF)
        xv = x_ref[...] + (coef * gt_ref[...]) * f
        gv = g_ref[...]
        r = lax.rsqrt(jnp.mean(xv * xv, axis=-1, keepdims=True) + EPS)
        n = xv * r
        err = n * gv - t_ref[...]
        dy = err * (1.0 / d)
        _acc_rows(dg_ref, jnp.sum(dy * n, axis=0, keepdims=True), m == 0)
        dn = dy * gv
        dx_ref[...] = r * (dn - n * jnp.mean(dn * n, axis=-1, keepdims=True))
        _acc_rows(lacc, jnp.sum(err * err, axis=0, keepdims=True), m == 0)

        @pl.when(m == steps - 1)
        def _():
            tot = jnp.sum(lacc[...], axis=-1, keepdims=True) * (0.5 / d)
            loss_ref[...] = jnp.broadcast_to(tot, (1, 128))

    row = pl.BlockSpec((TM, d), lambda m: (m, 0))
    vec = pl.BlockSpec((1, d), lambda m: (0, 0))
    return pl.pallas_call(
        body, name="ffn_down_loss", grid=(steps,),
        in_specs=[pl.BlockSpec((N_CHIPS, TM, fs), lambda m: (0, m, 0)), _whole(wd), row,
                  pl.BlockSpec((None, 1, d), lambda m: (m // per, 0, 0)), vec, row],
        out_specs=[row, row, vec, pl.BlockSpec((1, 128), lambda m: (0, 0))],
        out_shape=[_sds((t, d), BF), _sds((t, d), F32), _sds((1, d), F32), _sds((1, 128), F32)],
        scratch_shapes=[pltpu.VMEM((1, d), F32)],
        compiler_params=_cp(1))(s, wd, x, gt, g, target)


def _row_tile(rows, cols):
    best = rows
    for tr in range(8, rows + 1, 8):
        if rows % tr == 0 and tr * cols * 4 <= (1 << 20):
            best = tr
    if best * cols * 4 > (1 << 21):
        best = 8
    return best


def _adamw(w, g_arr, g_sel, m, v):
    rows, cols = w.shape
    tr = _row_tile(rows, cols)
    b1c = 1.0 - ADAM_B1 ** ADAM_STEP
    b2c = 1.0 - ADAM_B2 ** ADAM_STEP

    def body(w_ref, g_ref, m_ref, v_ref, go_ref, d_ref, mo_ref, vo_ref):
        gv = g_ref[...]
        mn = ADAM_B1 * m_ref[...] + (1.0 - ADAM_B1) * gv
        vn = ADAM_B2 * v_ref[...] + (1.0 - ADAM_B2) * (gv * gv)
        go_ref[...] = gv
        mo_ref[...] = mn
        vo_ref[...] = vn
        d_ref[...] = -ADAM_LR * ((mn / b1c) / (jnp.sqrt(vn / b2c) + ADAM_EPS) + ADAM_WD * w_ref[...])

    blk = pl.BlockSpec((tr, cols), lambda i: (i, 0))
    shp = _sds((rows, cols), F32)
    return pl.pallas_call(
        body, name="adamw", grid=(rows // tr,),
        in_specs=[blk, pl.BlockSpec((None, tr, cols), lambda i: (g_sel, i, 0)), blk, blk],
        out_specs=[blk] * 4, out_shape=[shp] * 4,
        compiler_params=_cp(1))(w, g_arr, m, v)


def _flip(v, bit):
    return 1 - v if bit else v


def _my_place():
    x, y, c = lax.axis_index("x"), lax.axis_index("y"), lax.axis_index("c")
    return x, y, c


class _Exchange:
    def __init__(self, operands, out_shape, aliases, sems, start, finish):
        self.operands, self.out_shape, self.aliases, self.sems = list(operands), list(out_shape), dict(aliases), list(sems)
        self.start, self.finish = start, finish


def _join(exchanges):
    exchanges = [e for e in exchanges if e is not None]
    if not exchanges:
        return None
    ops, outs, sems, aliases, spans = [], [], [], {}, []
    for e in exchanges:
        spans.append((len(ops), len(outs), len(sems), e))
        for i, j in e.aliases.items():
            aliases[len(ops) + i] = len(outs) + j
        ops += e.operands
        outs += e.out_shape
        sems += e.sems

    def run(which):
        def go(ins, res, sm):
            for io, oo, so, e in spans:
                getattr(e, which)(ins[io:io + len(e.operands)], res[oo:oo + len(e.out_shape)], sm[so:so + len(e.sems)])
        return go

    return _Exchange(ops, outs, aliases, sems, run("start"), run("finish"))


def _call(body, name, grid, in_specs, out_specs, out_shape, args, scratch=(), carry=None, io_alias=None):
    in_specs, out_specs, out_shape, scratch = list(in_specs), list(out_specs), list(out_shape), list(scratch)
    io_alias = dict(io_alias or {})
    if carry is None:
        return pl.pallas_call(body, name=name, grid=grid, in_specs=in_specs, out_specs=out_specs,
                              out_shape=out_shape, scratch_shapes=scratch, input_output_aliases=io_alias,
                              compiler_params=_cp(len(grid)))(*args)
    n_in, n_out, n_s = len(in_specs), len(out_specs), len(scratch)
    c_in, c_out = len(carry.operands), len(carry.out_shape)
    any_spec = pl.BlockSpec(memory_space=pl.ANY)

    def wrapped(*refs):
        ins, cins = refs[:n_in], refs[n_in:n_in + c_in]
        o0 = n_in + c_in
        outs, couts = refs[o0:o0 + n_out], refs[o0 + n_out:o0 + n_out + c_out]
        s0 = o0 + n_out + c_out
        scr, sems = refs[s0:s0 + n_s], refs[s0 + n_s:]
        first = pl.program_id(0) == 0
        last = pl.program_id(0) == grid[0] - 1
        for ax in range(1, len(grid)):
            first = jnp.logical_and(first, pl.program_id(ax) == 0)
            last = jnp.logical_and(last, pl.program_id(ax) == grid[ax] - 1)

        @pl.when(first)
        def _():
            carry.start(cins, couts, sems)

        body(*ins, *outs, *scr)

        @pl.when(last)
        def _():
            carry.finish(cins, couts, sems)

    return pl.pallas_call(
        wrapped, name=name, grid=grid, in_specs=in_specs + [any_spec] * c_in,
        out_specs=out_specs + [any_spec] * c_out, out_shape=out_shape + carry.out_shape,
        scratch_shapes=scratch + carry.sems,
        input_output_aliases={**io_alias, **{n_in + i: n_out + j for i, j in carry.aliases.items()}},
        compiler_params=_cp(len(grid)))(*args, *carry.operands)


def _whole_call(body, name, args, out_shape, scratch, carry=None):
    vm = pl.BlockSpec(memory_space=pltpu.VMEM)
    any_spec = pl.BlockSpec(memory_space=pl.ANY)
    out_shape, scratch = list(out_shape), list(scratch)
    n_in, n_out, n_s = len(args), len(out_shape), len(scratch)
    if carry is None:
        return pl.pallas_call(body, name=name, in_specs=[vm] * n_in, out_specs=[vm] * n_out, out_shape=out_shape,
                              scratch_shapes=scratch, compiler_params=_cp())(*args)
    c_in, c_out = len(carry.operands), len(carry.out_shape)

    def wrapped(*refs):
        ins, cins = refs[:n_in], refs[n_in:n_in + c_in]
        o0 = n_in + c_in
        outs, couts = refs[o0:o0 + n_out], refs[o0 + n_out:o0 + n_out + c_out]
        s0 = o0 + n_out + c_out
        scr, sems = refs[s0:s0 + n_s], refs[s0 + n_s:]
        carry.start(cins, couts, sems)
        body(*ins, *outs, *scr)
        carry.finish(cins, couts, sems)

    return pl.pallas_call(
        wrapped, name=name, in_specs=[vm] * n_in + [any_spec] * c_in, out_specs=[vm] * n_out + [any_spec] * c_out,
        out_shape=out_shape + carry.out_shape, scratch_shapes=scratch + carry.sems,
        input_output_aliases={n_in + i: n_out + j for i, j in carry.aliases.items()},
        compiler_params=_cp())(*args, *carry.operands)


def _alone(name, ex):
    any_spec = pl.BlockSpec(memory_space=pl.ANY)
    c_in, c_out = len(ex.operands), len(ex.out_shape)

    def body(*refs):
        ins, outs, sems = refs[:c_in], refs[c_in:c_in + c_out], refs[c_in + c_out:]
        ex.start(ins, outs, sems)
        ex.finish(ins, outs, sems)

    return pl.pallas_call(
        body, name=name, in_specs=[any_spec] * c_in, out_specs=[any_spec] * c_out, out_shape=ex.out_shape,
        scratch_shapes=ex.sems, input_output_aliases=ex.aliases, compiler_params=_cp())(*ex.operands)


def _ada_fwd(c_pad, w_ada, b_shard, carry=None):
    d = c_pad.shape[-1]
    cols = w_ada.shape[-1]
    chunk = 384

    def body(c_ref, w_ref, b_ref, call_ref, mod_ref, part, s1, r1, s2, r2):
        x, y, c = _my_place()
        dev = 4 * x + 2 * y + c
        chip = 2 * x + y
        call_ref[dev] = c_ref[...]

        def c_copy(k):
            px, py, pc = _flip(x, (k >> 2) & 1), _flip(y, (k >> 1) & 1), _flip(c, k & 1)
            return px, py, pc

        sends = []
        for k in range(1, N_DEV):
            px, py, pc = c_copy(k)
            cp = pltpu.make_async_remote_copy(src_ref=c_ref, dst_ref=call_ref.at[dev], send_sem=s1.at[k - 1],
                                              recv_sem=r1.at[k - 1], device_id=(px, py, pc), device_id_type=MESH)
            cp.start()
            sends.append(cp)
        for k in range(1, N_DEV):
            px, py, pc = c_copy(k)
            pltpu.make_async_remote_copy(src_ref=c_ref, dst_ref=call_ref.at[4 * px + 2 * py + pc],
                                         send_sem=s1.at[k - 1], recv_sem=r1.at[k - 1],
                                         device_id=(px, py, pc), device_id_type=MESH).wait_recv()
        for cp in sends:
            cp.wait_send()

        cs = call_ref[...].reshape(N_DEV * 8, d)
        sc = (cs * jax.nn.sigmoid(cs)).astype(BF)
        for n0 in range(0, cols, chunk):
            blk = _nn(sc, w_ref[:, n0:n0 + chunk].astype(BF)) + b_ref[:, n0:n0 + chunk]
            part[:, :, n0:n0 + chunk] = blk.reshape(N_DEV, 8, chunk)

        mod_ref[chip] = part[dev]
        sends = []
        for kk in range(1, N_CHIPS):
            px, py = _flip(x, (kk >> 1) & 1), _flip(y, kk & 1)
            cp = pltpu.make_async_remote_copy(src_ref=part.at[4 * px + 2 * py + c], dst_ref=mod_ref.at[chip],
                                              send_sem=s2.at[kk - 1], recv_sem=r2.at[kk - 1],
                                              device_id=(px, py, c), device_id_type=MESH)
            cp.start()
            sends.append(cp)
        for kk in range(1, N_CHIPS):
            px, py = _flip(x, (kk >> 1) & 1), _flip(y, kk & 1)
            pltpu.make_async_remote_copy(src_ref=part.at[dev], dst_ref=mod_ref.at[2 * px + py],
                                         send_sem=s2.at[kk - 1], recv_sem=r2.at[kk - 1],
                                         device_id=(px, py, c), device_id_type=MESH).wait_recv()
        for cp in sends:
            cp.wait_send()

    return _whole_call(
        body, "ada_fwd", (c_pad, w_ada, b_shard),
        [_sds((N_DEV, 8, d), F32), _sds((N_CHIPS, 8, cols), F32)],
        [pltpu.VMEM((N_DEV, 8, cols), F32),
         pltpu.SemaphoreType.DMA((N_DEV - 1,)), pltpu.SemaphoreType.DMA((N_DEV - 1,)),
         pltpu.SemaphoreType.DMA((N_CHIPS - 1,)), pltpu.SemaphoreType.DMA((N_CHIPS - 1,))], carry=carry)


def _ag_weights(bufs, kks=(1, 2, 3), relative=False):
    n, nk = len(bufs), len(kks)

    def half(b, which):
        hr = bufs[b].shape[2] // 2
        return pl.ds(pl.multiple_of(which * hr, 16), hr)

    def copies(outs, sems, b, i, kk):
        x, y, c = _my_place()
        chip = 2 * x + y
        px, py = _flip(x, (kk >> 1) & 1), _flip(y, kk & 1)
        mine, theirs = (0, kk) if relative else (chip, 2 * px + py)
        landing = kk if relative else chip
        k = nk * b + i
        send = pltpu.make_async_remote_copy(
            src_ref=outs[b].at[mine, :, half(b, c), :], dst_ref=outs[b].at[landing, :, half(b, c), :],
            send_sem=sems[0].at[k], recv_sem=sems[1].at[k], device_id=(px, py, c), device_id_type=MESH)
        got = outs[b].at[theirs, :, half(b, c), :]
        recv = pltpu.make_async_remote_copy(
            src_ref=got, dst_ref=got, send_sem=sems[0].at[k], recv_sem=sems[1].at[k],
            device_id=(px, py, c), device_id_type=MESH)
        fwd = pltpu.make_async_remote_copy(
            src_ref=got, dst_ref=got, send_sem=sems[2].at[k], recv_sem=sems[3].at[k],
            device_id=(x, y, 1 - c), device_id_type=MESH)
        other = outs[b].at[theirs, :, half(b, 1 - c), :]
        back = pltpu.make_async_remote_copy(
            src_ref=other, dst_ref=other, send_sem=sems[2].at[k], recv_sem=sems[3].at[k],
            device_id=(x, y, 1 - c), device_id_type=MESH)
        return send, recv, fwd, back

    def each(outs, sems):
        for b in range(n):
            for i, kk in enumerate(kks):
                yield copies(outs, sems, b, i, kk)

    def start(ins, outs, sems):
        for send, _, _, _ in each(outs, sems):
            send.start()

    def finish(ins, outs, sems):
        for _, recv, fwd, _ in each(outs, sems):
            recv.wait_recv()
            fwd.start()
        for send, _, fwd, back in each(outs, sems):
            back.wait_recv()
            send.wait_send()
            fwd.wait_send()

    return _Exchange(bufs, [_sds(s.shape, s.dtype) for s in bufs], {i: i for i in range(n)},
                     [pltpu.SemaphoreType.DMA((nk * n,))] * 4, start, finish)


def _rs_d2d(grads):
    n = len(grads)

    def copy(ins, outs, sems, b):
        x, y, c = _my_place()
        hr = grads[b].shape[2] // 2
        theirs = pl.ds(pl.multiple_of((1 - c) * hr, 8), hr)
        return pltpu.make_async_remote_copy(
            src_ref=ins[b].at[:, :, theirs, :], dst_ref=outs[b], send_sem=sems[0].at[b], recv_sem=sems[1].at[b],
            device_id=(x, y, 1 - c), device_id_type=MESH)

    def start(ins, outs, sems):
        for b in range(n):
            copy(ins, outs, sems, b).start()

    def finish(ins, outs, sems):
        for b in range(n):
            copy(ins, outs, sems, b).wait()

    return _Exchange(grads, [_sds(g.shape[:2] + (g.shape[2] // 2, g.shape[3]), F32) for g in grads], {},
                     [pltpu.SemaphoreType.DMA((n,))] * 2, start, finish)


def _add_halves(core, g, land):
    nchip, ng, rows, cols = g.shape
    hr = rows // 2
    tr = _row_tile(hr, cols)
    steps = hr // tr

    def body(core_ref, g_ref, l_ref, o_ref):
        del core_ref
        o_ref[...] = (g_ref[...] + l_ref[...]).astype(BF)

    return pl.pallas_call(
        body, name="add_halves",
        grid_spec=pltpu.PrefetchScalarGridSpec(
            num_scalar_prefetch=1, grid=(nchip, ng, steps),
            in_specs=[pl.BlockSpec((None, None, tr, cols), lambda j, a, i, cr: (j, a, cr[0] * steps + i, 0)),
                      pl.BlockSpec((None, None, tr, cols), lambda j, a, i, cr: (j, a, i, 0))],
            out_specs=pl.BlockSpec((None, None, tr, cols), lambda j, a, i, cr: (j, a, i, 0))),
        out_shape=_sds((nchip, ng, hr, cols), BF),
        compiler_params=_cp(3))(core, g, land)


def _rs_ici(parts, relative=False):
    n = len(parts)

    def copies(ins, outs, sems):
        x, y, c = _my_place()
        chip = 2 * x + y
        for b in range(n):
            for kk in range(1, N_CHIPS):
                px, py = _flip(x, (kk >> 1) & 1), _flip(y, kk & 1)
                k = 3 * b + kk - 1
                theirs, landing = (kk, kk) if relative else (2 * px + py, chip)
                send = pltpu.make_async_remote_copy(
                    src_ref=ins[b].at[theirs], dst_ref=outs[b].at[landing],
                    send_sem=sems[0].at[k], recv_sem=sems[1].at[k], device_id=(px, py, c), device_id_type=MESH)
                slot = outs[b].at[theirs]
                recv = pltpu.make_async_remote_copy(
                    src_ref=slot, dst_ref=slot, send_sem=sems[0].at[k], recv_sem=sems[1].at[k],
                    device_id=(px, py, c), device_id_type=MESH)
                yield send, recv

    def start(ins, outs, sems):
        for send, _ in copies(ins, outs, sems):
            send.start()

    def finish(ins, outs, sems):
        for send, recv in copies(ins, outs, sems):
            recv.wait_recv()
            send.wait_send()

    return _Exchange(parts, [_sds(p.shape, p.dtype) for p in parts], {},
                     [pltpu.SemaphoreType.DMA((3 * n,))] * 2, start, finish)


def _sum_chips(place, part, land, relative=False):
    nchip, ng, hr, cols = land.shape
    tr = _row_tile(hr, cols)
    steps = hr // tr

    def body(place_ref, p_ref, l1, l2, l3, o_ref):
        del place_ref
        o_ref[...] = ((p_ref[...].astype(F32) + l1[...].astype(F32)) + l2[...].astype(F32)) + l3[...].astype(F32)

    def slot(k):
        if relative:
            return pl.BlockSpec((None, None, tr, cols), lambda a, i, pr: (k, a, i, 0))
        return pl.BlockSpec((None, None, tr, cols), lambda a, i, pr: (jnp.bitwise_xor(pr[1], k), a, i, 0))

    return pl.pallas_call(
        body, name="sum_chips",
        grid_spec=pltpu.PrefetchScalarGridSpec(
            num_scalar_prefetch=1, grid=(ng, steps),
            in_specs=[slot(0), slot(1), slot(2), slot(3)],
            out_specs=pl.BlockSpec((None, tr, cols), lambda a, i, pr: (a, pr[0] * steps + i, 0))),
        out_shape=_sds((ng, 2 * hr, cols), F32),
        compiler_params=_cp(2))(place, part, land, land, land)


def _rs_final(bufs):
    n = len(bufs)

    def copy(outs, sems, b, which):
        x, y, c = _my_place()
        hr = bufs[b].shape[1] // 2
        rows = outs[b].at[:, pl.ds(pl.multiple_of((c if which == 0 else 1 - c) * hr, 8), hr), :]
        return pltpu.make_async_remote_copy(
            src_ref=rows, dst_ref=rows, send_sem=sems[0].at[b], recv_sem=sems[1].at[b],
            device_id=(x, y, 1 - c), device_id_type=MESH)

    def start(ins, outs, sems):
        for b in range(n):
            copy(outs, sems, b, 0).start()

    def finish(ins, outs, sems):
        for b in range(n):
            copy(outs, sems, b, 0).wait_send()
            copy(outs, sems, b, 1).wait_recv()

    return _Exchange(bufs, [_sds(h.shape, F32) for h in bufs], {i: i for i in range(n)},
                     [pltpu.SemaphoreType.DMA((n,))] * 2, start, finish)


def _small_sync(smalls, dmod_blk, c_all, carry=None):
    d = c_all.shape[-1]
    cols = dmod_blk.shape[-1]
    chunk = 384

    def body(sm_ref, dm_ref, c_ref, sum_ref, gw_ref, sm_all, dm_all, ssem, rsem):
        x, y, c = _my_place()
        dev = 4 * x + 2 * y + c
        chip = 2 * x + y
        sm_all[dev] = sm_ref[...]
        dm_all[dev] = dm_ref[chip]
        sends = []
        for k in range(1, N_DEV):
            px, py, pc = _flip(x, (k >> 2) & 1), _flip(y, (k >> 1) & 1), _flip(c, k & 1)
            a = pltpu.make_async_remote_copy(src_ref=sm_ref, dst_ref=sm_all.at[dev], send_sem=ssem.at[2 * (k - 1)],
                                             recv_sem=rsem.at[2 * (k - 1)], device_id=(px, py, pc),
                                             device_id_type=MESH)
            b = pltpu.make_async_remote_copy(src_ref=dm_ref.at[2 * px + py], dst_ref=dm_all.at[dev],
                                             send_sem=ssem.at[2 * (k - 1) + 1], recv_sem=rsem.at[2 * (k - 1) + 1],
                                             device_id=(px, py, pc), device_id_type=MESH)
            a.start()
            b.start()
            sends += [a, b]
        for k in range(1, N_DEV):
            px, py, pc = _flip(x, (k >> 2) & 1), _flip(y, (k >> 1) & 1), _flip(c, k & 1)
            pdev = 4 * px + 2 * py + pc
            pltpu.make_async_remote_copy(src_ref=sm_ref, dst_ref=sm_all.at[pdev], send_sem=ssem.at[2 * (k - 1)],
                                         recv_sem=rsem.at[2 * (k - 1)], device_id=(px, py, pc),
                                         device_id_type=MESH).wait_recv()
            pltpu.make_async_remote_copy(src_ref=dm_ref.at[chip], dst_ref=dm_all.at[pdev],
                                         send_sem=ssem.at[2 * (k - 1) + 1], recv_sem=rsem.at[2 * (k - 1) + 1],
                                         device_id=(px, py, pc), device_id_type=MESH).wait_recv()
        for cp in sends:
            cp.wait_send()

        tot = sm_all[0]
        for q in range(1, N_DEV):
            tot = tot + sm_all[q]
        sum_ref[...] = tot

        cs = c_ref[...].reshape(N_DEV * 8, d)
        sc = (cs * jax.nn.sigmoid(cs)).astype(BF)
        for n0 in range(0, cols, chunk):
            dmv = dm_all[:, :, n0:n0 + chunk].reshape(N_DEV * 8, chunk).astype(BF)
            gw_ref[:, n0:n0 + chunk] = _tn(sc, dmv)

    return _whole_call(
        body, "small_sync", (smalls, dmod_blk, c_all),
        [_sds(smalls.shape, F32), _sds((d, cols), F32)],
        [pltpu.VMEM((N_DEV,) + smalls.shape, F32), pltpu.VMEM((N_DEV, 8, cols), F32),
         pltpu.SemaphoreType.DMA((2 * (N_DEV - 1),)), pltpu.SemaphoreType.DMA((2 * (N_DEV - 1),))], carry=carry)


def _bucket_onehot():
    maps = np.stack([_bucket_map(dil).reshape(-1) for _, dil in DIL_CONFIGS])
    return (jnp.asarray(maps)[:, None, :] == jnp.arange(N_BUCKETS, dtype=jnp.int32)[None, :, None]).astype(BF)


def _dil_bias(rel_t, onehot):
    def body(r_ref, oh_ref, o_ref):
        rv = r_ref[...]
        hi = rv.astype(BF)
        lo = (rv - hi.astype(F32)).astype(BF)
        for c in range(len(DIL_CONFIGS)):
            o_ref[c] = _nn(hi, oh_ref[c]) + _nn(lo, oh_ref[c])

    return pl.pallas_call(body, name="dil_bias",
                          out_shape=_sds((len(DIL_CONFIGS), N_HEADS, BLOCK * 2 * BLOCK), F32),
                          compiler_params=_cp())(rel_t, onehot)


def _rowsum8(a):
    def body(a_ref, o_ref):
        o_ref[...] = jnp.sum(a_ref[...], axis=0, keepdims=True)

    return pl.pallas_call(body, name="rowsum8", out_shape=_sds((1, a.shape[1]), F32), compiler_params=_cp())(a)


def _local_step(x, mod, target, w, gains, rel_bias, place=None):
    nb, seq, d = x.shape
    t = nb * seq
    dist = place is not None
    core = place[0:1] if dist else None
    x0 = x.reshape(t, d)
    tgt = target.reshape(t, d)
    md = [mod[:, i:i + 1, :] for i in range(N_MOD)]
    sh1, sc1, gt1, sh2, sc2, gt2, sh3, sc3, gt3 = md
    g1, g2, g3 = gains["g_ffn1"], gains["g_mix"], gains["g_ffn2"]
    ones_g = _group_ones()

    def partial_sums(grads, lands):
        return [_add_halves(core, g, l) for g, l in zip(grads, lands)]

    def chip_sums(parts, lands):
        return [_sum_chips(place, p, l, relative=True) for p, l in zip(parts, lands)]

    gu1 = w["gu1"]
    res = _ffn_up(x0, g1, sc1, sh1, gu1, seq,
                  carry=_join([_ag_weights([w["d1"]], relative=True), _ag_weights([w["win"]])]) if dist else None)
    h1, a1, u1, s1 = res[:4]
    wd1, w_in = res[4:] if dist else (w["d1"], w["win"])
    f1, x1 = _ffn_down(s1, wd1, x0, gt1, seq, 0.5)

    res = _qkv_proj(x1, g2, sc2, sh2, w_in, seq, carry=_ag_weights([w["wout"]]) if dist else None)
    h2, qkv6, qkv_r4, qkv_r16 = res[:4]
    w_out2 = (res[4] if dist else w["wout"]).reshape(2 * D_GRP, d)
    qkv6b = qkv6.reshape(6, nb, seq, D_GRP)
    res = _sb_fwd(qkv6b, gains["g_sb_out"], nb, seq,
                  carry=_ag_weights([w["gu2"], w["d2"]], relative=True) if dist else None)
    o_sb, on_sb = res[:2]
    wgu2, wd2 = res[2:] if dist else (w["gu2"], w["d2"])
    onehot = _bucket_onehot()
    bias = _dil_bias(rel_bias.T, onehot).reshape(len(DIL_CONFIGS), N_HEADS * BLOCK, 2 * BLOCK)
    o_cs, l_cs = [], []
    qkv_rs = [(qkv6b, 3), (qkv_r4, 0), (qkv_r16, 0)]
    for ci, (_, dil) in enumerate(DIL_CONFIGS):
        sub = seq // dil
        arr, base = qkv_rs[ci]
        arr = arr.reshape(base + 3, nb, sub, dil * D_GRP)
        qkv_rs[ci] = (arr, base)
        o_c, l_c = _dil_fwd(arr, base, bias[ci], nb, sub, dil)
        o_cs.append(o_c.reshape(t // dil, dil * D_GRP))
        l_cs.append(l_c.reshape(t // dil, dil * D_GRP))
    o_dil, on_dil = _dil_comb(o_cs, l_cs, gains["g_dil_out"])
    tmix, x2 = _mix_out(on_sb.reshape(t, D_GRP), on_dil, w_out2, x1, gt2, seq)

    h3, a3, u3, s3 = _ffn_up(x2, g3, sc3, sh3, wgu2, seq)
    f3, dx3, dg_final, loss = _ffn_down_loss(s3, wd2, x2, gt3, seq, 0.5, gains["g_final"], tgt)

    da3, du3, df3, dgt3, dx2, dsh3, dsc3, dg3 = _ffn_bwd_x(dx3, gt3, f3, wd2, a3, u3, wgu2, x2, g3, sc3, seq, 0.5)
    grads2 = _ffn_bwd_w(h3, da3, du3, s3, df3)

    res = _mix_bwd_out(
        dx2, gt2, tmix, w_out2, o_sb.reshape(t, D_GRP), o_dil, on_sb.reshape(t, D_GRP), on_dil,
        gains["g_sb_out"], gains["g_dil_out"], ones_g, seq, carry=_rs_d2d(grads2) if dist else None)
    do_sb, do_dil, dgt2, dg_sb, dg_dil, dw_out = res[:6]
    parts2 = partial_sums(grads2, res[6:]) if dist else None
    dw_out = dw_out.reshape(N_CHIPS, 1, 2 * D_GRP // N_CHIPS, d)
    res = _sb_bwd(qkv6b, do_sb.reshape(nb, seq, D_GRP), nb, seq,
                  carry=_join([_rs_ici(parts2, relative=True), _rs_d2d([dw_out])]) if dist else None)
    dqkv6 = res[0]
    halves2 = chip_sums(parts2, res[1:2]) if dist else None
    part_wo = partial_sums([dw_out], res[2:3]) if dist else None
    dcs = _dil_comb_bwd(do_dil, o_cs, l_cs)
    dsum, a_tiles = [], []
    for ci, (_, dil) in enumerate(DIL_CONFIGS):
        sub = seq // dil
        do_c = dcs[ci].reshape(nb, sub, dil * D_GRP)
        dd_c = dcs[3 + ci].reshape(nb, sub, dil * D_GRP)
        res = _dil_bwd(qkv_rs[ci][0], qkv_rs[ci][1], bias[ci], do_c, dd_c, nb, sub, dil)
        dsum.append(res[0].reshape(3, t // dil, dil * D_GRP))
        a_tiles.append(res[1].reshape(N_HEADS, BLOCK * 2 * BLOCK))
    dqkv6 = _dqkv_dil_sum(dsum, dqkv6.reshape(6, t, D_GRP))
    drel = _relbias_grad(jnp.stack(a_tiles), onehot)
    dx1, dsh2, dsc2, dg2 = _mix_bwd_dh(dqkv6, w_in, x1, g2, sc2, dx2, seq)

    da1, du1, df1, dgt1 = _ffn_bwd_ds(dx1, gt1, f1, wd1, a1, u1, seq, 0.5)
    g_dn = _ffn_bwd_w(h1, da1, du1, s1, df1, terms=(2,))
    res = _ffn_bwd_w(h1, da1, du1, s1, df1, terms=(0, 1), carry=_rs_d2d(g_dn) if dist else None)
    g_gu = res[:1]
    grads1 = g_gu + g_dn
    parts_dn = partial_sums(g_dn, res[1:2]) if dist else None
    res = _dw_in(h2, dqkv6, carry=_join([_rs_ici(parts_dn, relative=True), _rs_ici(part_wo), _rs_d2d(g_gu),
                                         _rs_final(halves2)]) if dist else None)
    grads_m = [res[0], dw_out]
    if dist:
        sums_dn = chip_sums(parts_dn, res[1:2])
        sum_wo = [_sum_chips(place, p, l) for p, l in zip(part_wo, res[2:3])]
        parts_gu = partial_sums(g_gu, res[3:4])
        grads2 = res[4:5]
    res = _ffn_bwd_dh(da1, du1, gu1, x0, g1, sc1, dx1, seq,
                      carry=_join([_rs_ici(parts_gu, relative=True), _rs_d2d(grads_m[:1]),
                                   _rs_final(sums_dn + sum_wo)]) if dist else None)
    dx0, dsh1, dsc1, dg1 = res[:4]
    pending = None
    if dist:
        pending = (chip_sums(parts_gu, res[4:5]), partial_sums(grads_m[:1], res[5:6]))
        grads1, grads_m = [None, res[6]], [None, res[7]]

    dmod = jnp.concatenate([dsh1, dsc1, dgt1, dsh2, dsc2, dgt2, dsh3, dsc3, dgt3], axis=1)
    return dict(grad_x=dx0.reshape(nb, seq, d), loss=loss[0, 0], dmod=dmod.reshape(nb, N_MOD * d),
                dffn1=grads1, dffn2=grads2[0], dwin=grads_m[0], dwout=grads_m[1], pending=pending,
                dg_ffn1=dg1, dg_mix=dg2, dg_ffn2=dg3, dg_final=dg_final, dg_sb=dg_sb, dg_dil=dg_dil,
                drel=drel.T)


_SMALL_ORDER = (("b_ada", N_MOD * 1024), ("g_ffn1", 1024), ("g_mix", 1024), ("g_ffn2", 1024), ("g_final", 1024),
                ("g_sb_out", D_GRP), ("g_dil_out", D_GRP), ("rel_bias", N_BUCKETS * N_HEADS))


def _pack_small(parts, extra=None):
    flat = [parts[name].reshape(-1).astype(F32) for name, _ in _SMALL_ORDER]
    used = sum(sz for _, sz in _SMALL_ORDER)
    pad = SMALL_ROWS * 128 - used
    tail = jnp.zeros((pad,), F32)
    if extra is not None:
        tail = tail.at[0].set(extra)
    return jnp.concatenate(flat + [tail]).reshape(SMALL_ROWS, 128)


def _unpack_small(packed, shapes):
    flat = packed.reshape(-1)
    out, off = {}, 0
    for name, sz in _SMALL_ORDER:
        out[name] = flat[off:off + sz].reshape(shapes[name])
        off += sz
    return out, flat[off]


def kernel(x, c, w_ada, b_ada, g_ffn1, w1_gate, w1_up, w1_down, g_mix, w_in, g_sb_out, g_dil_out, w_out, rel_bias, g_ffn2, w2_gate, w2_up, w2_down, g_final, loss_target, m_w_ada, m_b_ada, m_g_ffn1, m_w1_gate, m_w1_up, m_w1_down, m_g_mix, m_w_in, m_g_sb_out, m_g_dil_out, m_w_out, m_rel_bias, m_g_ffn2, m_w2_gate, m_w2_up, m_w2_down, m_g_final, v_w_ada, v_b_ada, v_g_ffn1, v_w1_gate, v_w1_up, v_w1_down, v_g_mix, v_w_in, v_g_sb_out, v_g_dil_out, v_w_out, v_rel_bias, v_g_ffn2, v_w2_gate, v_w2_up, v_w2_down, v_g_final):
    nb, seq, d = x.shape
    xi, yi, ci = lax.axis_index("x"), lax.axis_index("y"), lax.axis_index("c")
    chip = 2 * xi + yi
    ada_cols = w_ada.shape[-1]

    c_pad = jnp.zeros((8, d), F32).at[:nb].set(c)
    b_shard = lax.dynamic_slice(b_ada, (0, chip * ada_cols), (1, ada_cols))
    shards = dict(gu1=jnp.stack([w1_gate[0], w1_up[0]]), d1=w1_down, win=w_in, wout=w_out,
                  gu2=jnp.stack([w2_gate[0], w2_up[0]]), d2=w2_down)
    bufs = {k: lax.dynamic_update_slice(lax.empty((N_CHIPS,) + s.shape, BF), s.astype(BF)[None],
                                        (chip if k in ("win", "wout") else 0, 0, 0, 0))
            for k, s in shards.items()}
    c_all, mod_blk, bufs["gu1"] = _ada_fwd(c_pad, w_ada[0], b_shard,
                                           carry=_ag_weights([bufs["gu1"]], relative=True))
    mod = jnp.transpose(mod_blk[:, :nb, :], (1, 0, 2)).reshape(nb, N_MOD, d)

    gains = dict(g_ffn1=g_ffn1, g_mix=g_mix, g_ffn2=g_ffn2, g_final=g_final.reshape(1, d),
                 g_sb_out=g_sb_out.reshape(1, D_GRP), g_dil_out=g_dil_out.reshape(1, D_GRP))
    place = jnp.stack([ci, chip]).astype(jnp.int32)
    r = _local_step(x, mod, loss_target, bufs, gains, rel_bias, place)

    dmod = r["dmod"]
    dmod_pad = jnp.zeros((8, N_MOD * d), F32).at[:nb].set(dmod)
    dmod_blk = jnp.transpose(dmod_pad.reshape(8, N_CHIPS, ada_cols), (1, 0, 2))
    small_parts = dict(b_ada=_rowsum8(dmod_pad), g_ffn1=r["dg_ffn1"], g_mix=r["dg_mix"], g_ffn2=r["dg_ffn2"],
                       g_final=r["dg_final"], g_sb_out=r["dg_sb"], g_dil_out=r["dg_dil"], rel_bias=r["drel"])
    halves1, parts_m = r["pending"]
    res = _small_sync(_pack_small(small_parts, r["loss"]), dmod_blk, c_all,
                      carry=_join([_rs_final(halves1), _rs_ici(parts_m)]))
    small_sum, g_wada, gffn1_gu = res[:3]
    halves_m = [_sum_chips(place, p, l) for p, l in zip(parts_m, res[3:4])]
    gwin, = _alone("rs_last", _rs_final(halves_m))
    gffn1_dn, gwout = r["dffn1"][1], r["dwout"]
    gffn2 = r["dffn2"]

    small_w = dict(b_ada=b_ada, g_ffn1=g_ffn1, g_mix=g_mix, g_ffn2=g_ffn2, g_final=g_final,
                   g_sb_out=g_sb_out, g_dil_out=g_dil_out, rel_bias=rel_bias)
    small_m = dict(b_ada=m_b_ada, g_ffn1=m_g_ffn1, g_mix=m_g_mix, g_ffn2=m_g_ffn2, g_final=m_g_final,
                   g_sb_out=m_g_sb_out, g_dil_out=m_g_dil_out, rel_bias=m_rel_bias)
    small_v = dict(b_ada=v_b_ada, g_ffn1=v_g_ffn1, g_mix=v_g_mix, g_ffn2=v_g_ffn2, g_final=v_g_final,
                   g_sb_out=v_g_sb_out, g_dil_out=v_g_dil_out, rel_bias=v_rel_bias)
    shapes = {k: v.shape for k, v in small_w.items()}
    sg, sd, sm, sv = _adamw(_pack_small(small_w), small_sum.reshape(1, SMALL_ROWS, 128), 0,
                            _pack_small(small_m), _pack_small(small_v))
    sg, loss = _unpack_small(sg, shapes)
    sd, _ = _unpack_small(sd, shapes)
    sm, _ = _unpack_small(sm, shapes)
    sv, _ = _unpack_small(sv, shapes)

    big = {}

    def upd(name, w, g_arr, sel, m, v, transposed=False):
        swap = (lambda a: jnp.swapaxes(a, -1, -2)) if transposed else (lambda a: a)
        w2, m2, v2 = [swap(a)[0] for a in (w, m, v)]
        big[name] = [swap(a[None]) for a in _adamw(w2, g_arr, sel, m2, v2)]

    upd("w_ada", w_ada, g_wada.reshape(1, d, ada_cols), 0, m_w_ada, v_w_ada)
    upd("w1_gate", w1_gate, gffn1_gu, 0, m_w1_gate, v_w1_gate, transposed=True)
    upd("w1_up", w1_up, gffn1_gu, 1, m_w1_up, v_w1_up, transposed=True)
    upd("w1_down", w1_down, gffn1_dn, 0, m_w1_down, v_w1_down)
    upd("w_in", w_in, gwin, 0, m_w_in, v_w_in)
    upd("w_out", w_out, gwout, 0, m_w_out, v_w_out)
    upd("w2_gate", w2_gate, gffn2, 0, m_w2_gate, v_w2_gate, transposed=True)
    upd("w2_up", w2_up, gffn2, 1, m_w2_up, v_w2_up, transposed=True)
    upd("w2_down", w2_down, gffn2, 2, m_w2_down, v_w2_down)

    names = ["w_ada", "b_ada", "g_ffn1", "w1_gate", "w1_up", "w1_down", "g_mix", "w_in", "g_sb_out", "g_dil_out",
             "w_out", "rel_bias", "g_ffn2", "w2_gate", "w2_up", "w2_down", "g_final"]
    outs = [loss, r["grad_x"]]
    for k, small in enumerate((sg, sd, sm, sv)):
        for name in names:
            outs.append(big[name][k] if name in big else small[name])
    return tuple(outs)
```

```python
import math

import numpy as np
import jax
import jax.numpy as jnp
from jax import lax
from jax.experimental import pallas as pl
from jax.experimental.pallas import tpu as pltpu

F32 = jnp.float32
BF = jnp.bfloat16
MESH = pl.DeviceIdType.MESH

HEAD_DIM = 64
N_HEADS = 8
D_GRP = N_HEADS * HEAD_DIM
DIL_CONFIGS = ((128, 1), (512, 4), (2048, 16))
N_STEPS = 128
BLOCK = 128
N_BUCKETS = 32
MAX_DISTANCE = 2048
N_MOD = 9
EPS = 1e-6
NEG_INF = -1e30
SCALE = HEAD_DIM ** -0.5

ADAM_LR = 0.001
ADAM_B1 = 0.9
ADAM_B2 = 0.999
ADAM_EPS = 1e-08
ADAM_WD = 0.01
ADAM_STEP = 10

N_CHIPS = 4
N_DEV = 8
VMEM_LIMIT = 56 * 1024 * 1024
TM = 512
TQ = 256
KB = 256
SMALL_ROWS = 120


def _cp(n_axes=0, **kw):
    sem = ("arbitrary",) * n_axes if n_axes else None
    return pltpu.CompilerParams(dimension_semantics=sem, vmem_limit_bytes=VMEM_LIMIT, **kw)


def _nn(a, b):
    return jnp.dot(a, b, preferred_element_type=F32)


def _nt(a, b):
    return lax.dot_general(a, b, (((1,), (1,)), ((), ())), preferred_element_type=F32)


def _tn(a, b):
    return lax.dot_general(a, b, (((0,), (0,)), ((), ())), preferred_element_type=F32)


def _twice(m):
    return jnp.concatenate([m, m], axis=0)


def _nn2(x, m2):
    hi = x.astype(BF)
    lo = (x - hi.astype(F32)).astype(BF)
    return _nn(jnp.concatenate([hi, lo], axis=1), m2)


def _softplus(z):
    return jnp.maximum(z, 0.0) + jnp.log(1.0 + jnp.exp(-jnp.abs(z)))


def _sds(shape, dtype):
    return jax.ShapeDtypeStruct(shape, dtype)


def _whole(a):
    nd = a.ndim
    return pl.BlockSpec(a.shape, lambda *_: (0,) * nd, pipeline_mode=pl.Buffered(1))


def _modnorm_bwd_tile(dh, xv, gv, scv, dxo):
    r = lax.rsqrt(jnp.mean(xv * xv, axis=-1, keepdims=True) + EPS)
    n = xv * r
    ng = n * gv
    dsh = jnp.sum(dh, axis=0, keepdims=True)
    dsc = jnp.sum(dh * ng, axis=0, keepdims=True)
    dy = dh * (1.0 + scv)
    dg = jnp.sum(dy * n, axis=0, keepdims=True)
    dn = dy * gv
    dx = dxo + r * (dn - n * jnp.mean(dn * n, axis=-1, keepdims=True))
    return dx, dsh, dsc, dg


def _acc_rows(ref, val, first):
    @pl.when(first)
    def _():
        ref[...] = val

    @pl.when(jnp.logical_not(first))
    def _():
        ref[...] += val


def _modnorm_tile(x_ref, g_ref, sc_ref, sh_ref):
    xv = x_ref[...]
    r = lax.rsqrt(jnp.mean(xv * xv, axis=-1, keepdims=True) + EPS)
    return (((xv * r) * g_ref[...]) * (1.0 + sc_ref[...]) + sh_ref[...]).astype(BF)


def _ffn_up(x, g, sc, sh, wgu, seq, carry=None):
    t, d = x.shape
    fs = wgu.shape[-1]
    per = seq // TM

    def body(x_ref, g_ref, sc_ref, sh_ref, w_ref, h_ref, p_ref, q_ref, s_ref):
        hv = _modnorm_tile(x_ref, g_ref, sc_ref, sh_ref)
        h_ref[...] = hv
        for j in range(N_CHIPS):
            a = _nn(hv, w_ref[j, 0])
            u = _nn(hv, w_ref[j, 1])
            sig = jax.nn.sigmoid(a)
            q = a * sig
            p_ref[j] = (u * (sig * (1.0 + a * (1.0 - sig)))).astype(BF)
            q_ref[j] = q.astype(BF)
            s_ref[j] = (q * u).astype(BF)

    row = pl.BlockSpec((TM, d), lambda m: (m, 0))
    ex = pl.BlockSpec((None, 1, d), lambda m: (m // per, 0, 0))
    blk = pl.BlockSpec((N_CHIPS, TM, fs), lambda m: (0, m, 0))
    return _call(
        body, "ffn_up", (t // TM,),
        [row, pl.BlockSpec((1, d), lambda m: (0, 0)), ex, ex, _whole(wgu)],
        [row, blk, blk, blk],
        [_sds((t, d), BF)] + [_sds((N_CHIPS, t, fs), BF)] * 3,
        (x, g, sc, sh, wgu), carry=carry)


def _ffn_down(s, wd, x, gt, seq, coef, carry=None):
    _, t, fs = s.shape
    d = x.shape[-1]
    per = seq // TM

    def body(s_ref, w_ref, x_ref, gt_ref, f_ref, xo_ref):
        f = _nn(s_ref[0], w_ref[0, 0])
        for j in range(1, N_CHIPS):
            f = f + _nn(s_ref[j], w_ref[j, 0])
        f_ref[...] = f.astype(BF)
        xo_ref[...] = x_ref[...] + (coef * gt_ref[...]) * f

    row = pl.BlockSpec((TM, d), lambda m: (m, 0))
    return _call(
        body, "ffn_down", (t // TM,),
        [pl.BlockSpec((N_CHIPS, TM, fs), lambda m: (0, m, 0)), _whole(wd), row,
         pl.BlockSpec((None, 1, d), lambda m: (m // per, 0, 0))],
        [row, row], [_sds((t, d), BF), _sds((t, d), F32)], (s, wd, x, gt), carry=carry)


def _ffn_bwd_ds(dxo, gt, f, wd, p, q, seq, coef, carry=None):
    t, d = dxo.shape
    fs = p.shape[-1]
    per = seq // TM
    nb = t // seq

    def body(dxo_ref, gt_ref, f_ref, w_ref, p_ref, q_ref, da_ref, du_ref, df_ref, dgt_ref):
        m = pl.program_id(0)
        dxv = dxo_ref[...]
        df = ((coef * gt_ref[...]) * dxv).astype(BF)
        df_ref[...] = df
        _acc_rows(dgt_ref, coef * jnp.sum(dxv * f_ref[...].astype(F32), axis=0, keepdims=True), m % per == 0)
        for j in range(N_CHIPS):
            ds = _nt(df, w_ref[j, 0])
            da_ref[j] = (ds * p_ref[j].astype(F32)).astype(BF)
            du_ref[j] = (ds * q_ref[j].astype(F32)).astype(BF)

    row = pl.BlockSpec((TM, d), lambda m: (m, 0))
    blk = pl.BlockSpec((N_CHIPS, TM, fs), lambda m: (0, m, 0))
    ex = pl.BlockSpec((None, 1, d), lambda m: (m // per, 0, 0))
    return _call(
        body, "ffn_bwd_ds", (t // TM,),
        [row, ex, row, _whole(wd), blk, blk],
        [blk, blk, row, ex],
        [_sds((N_CHIPS, t, fs), BF), _sds((N_CHIPS, t, fs), BF), _sds((t, d), BF), _sds((nb, 1, d), F32)],
        (dxo, gt, f, wd, p, q), carry=carry)


TM_X = 256


def _ffn_bwd_x(dxo, gt, f, wd, p, q, wgu, x, g, sc, seq, coef):
    t, d = dxo.shape
    fs = p.shape[-1]
    per = seq // TM_X
    nb = t // seq

    def body(dxo_ref, gt_ref, f_ref, wd_ref, p_ref, q_ref, w_ref, x_ref, g_ref, sc_ref,
             da_ref, du_ref, df_ref, dgt_ref, dx_ref, dsh_ref, dsc_ref, dg_ref):
        m = pl.program_id(0)
        dxv = dxo_ref[...]
        df = ((coef * gt_ref[...]) * dxv).astype(BF)
        df_ref[...] = df
        _acc_rows(dgt_ref, coef * jnp.sum(dxv * f_ref[...].astype(F32), axis=0, keepdims=True), m % per == 0)
        dh = None
        for j in range(N_CHIPS):
            ds = _nt(df, wd_ref[j, 0])
            da = (ds * p_ref[j].astype(F32)).astype(BF)
            du = (ds * q_ref[j].astype(F32)).astype(BF)
            da_ref[j] = da
            du_ref[j] = du
            part = _nt(da, w_ref[j, 0]) + _nt(du, w_ref[j, 1])
            dh = part if dh is None else dh + part
        dx, dsh, dsc, dg = _modnorm_bwd_tile(dh, x_ref[...], g_ref[...], sc_ref[...], dxv)
        dx_ref[...] = dx
        _acc_rows(dsh_ref, dsh, m % per == 0)
        _acc_rows(dsc_ref, dsc, m % per == 0)
        _acc_rows(dg_ref, dg, m == 0)

    row = pl.BlockSpec((TM_X, d), lambda m: (m, 0))
    blk = pl.BlockSpec((N_CHIPS, TM_X, fs), lambda m: (0, m, 0))
    ex = pl.BlockSpec((None, 1, d), lambda m: (m // per, 0, 0))
    vec = pl.BlockSpec((1, d), lambda m: (0, 0))
    exs = _sds((nb, 1, d), F32)
    return pl.pallas_call(
        body, name="ffn_bwd_x", grid=(t // TM_X,),
        in_specs=[row, ex, row, _whole(wd), blk, blk, _whole(wgu), row, vec, ex],
        out_specs=[blk, blk, row, ex, row, ex, ex, vec],
        out_shape=[_sds((N_CHIPS, t, fs), BF), _sds((N_CHIPS, t, fs), BF), _sds((t, d), BF), exs,
                   _sds((t, d), F32), exs, exs, _sds((1, d), F32)],
        compiler_params=_cp(1))(dxo, gt, f, wd, p, q, wgu, x, g, sc)


TK_W = 1024


def _ffn_bwd_w(h, da, du, s, df, terms=(0, 1, 2), carry=None):
    t, d = h.shape
    fs = da.shape[-1]
    row = pl.BlockSpec((TK_W, d), lambda j, kt: (kt, 0))
    blk = pl.BlockSpec((None, TK_W, fs), lambda j, kt: (j, kt, 0))
    args, specs, idx = [], [], []

    def operand(a, spec):
        for i, o in enumerate(args):
            if o is a:
                return i
        args.append(a)
        specs.append(spec)
        return len(args) - 1

    for term in terms:
        lhs, rhs = ((da, h), (du, h), (s, df))[term]
        idx.append((operand(lhs, blk), operand(rhs, row)))

    def body(*refs):
        o_ref = refs[-1]
        kt = pl.program_id(1)
        parts = [_tn(refs[a][...], refs[b][...]) for a, b in idx]

        @pl.when(kt == 0)
        def _():
            for i, p in enumerate(parts):
                o_ref[i] = p

        @pl.when(kt != 0)
        def _():
            for i, p in enumerate(parts):
                o_ref[i] += p

    return _call(
        body, "ffn_bwd_w", (N_CHIPS, t // TK_W), specs,
        [pl.BlockSpec((None, len(terms), fs, d), lambda j, kt: (j, 0, 0, 0))],
        [_sds((N_CHIPS, len(terms), fs, d), F32)], args, carry=carry)


def _ffn_bwd_dh(da, du, wgu, x, g, sc, dxo, seq, carry=None):
    _, t, fs = da.shape
    d = x.shape[-1]
    per = seq // TM
    nb = t // seq

    def body(da_ref, du_ref, w_ref, x_ref, g_ref, sc_ref, dxo_ref, dx_ref, dsh_ref, dsc_ref, dg_ref):
        m = pl.program_id(0)
        dh = _nt(da_ref[0], w_ref[0, 0]) + _nt(du_ref[0], w_ref[0, 1])
        for j in range(1, N_CHIPS):
            dh = dh + _nt(da_ref[j], w_ref[j, 0]) + _nt(du_ref[j], w_ref[j, 1])
        dx, dsh, dsc, dg = _modnorm_bwd_tile(dh, x_ref[...], g_ref[...], sc_ref[...], dxo_ref[...])
        dx_ref[...] = dx
        _acc_rows(dsh_ref, dsh, m % per == 0)
        _acc_rows(dsc_ref, dsc, m % per == 0)
        _acc_rows(dg_ref, dg, m == 0)

    row = pl.BlockSpec((TM, d), lambda m: (m, 0))
    blk = pl.BlockSpec((N_CHIPS, TM, fs), lambda m: (0, m, 0))
    ex = pl.BlockSpec((None, 1, d), lambda m: (m // per, 0, 0))
    vec = pl.BlockSpec((1, d), lambda m: (0, 0))
    return _call(
        body, "ffn_bwd_dh", (t // TM,),
        [blk, blk, _whole(wgu), row, vec, ex, row],
        [row, ex, ex, vec],
        [_sds((t, d), F32), _sds((nb, 1, d), F32), _sds((nb, 1, d), F32), _sds((1, d), F32)],
        (da, du, wgu, x, g, sc, dxo), carry=carry)


def _qkv_proj(x, g, sc, sh, w_in, seq, carry=None):
    t, d = x.shape
    wc = w_in.shape[-1]
    per = seq // TM

    dils = [dil for _, dil in DIL_CONFIGS if dil > 1]

    def body(x_ref, g_ref, sc_ref, sh_ref, w_ref, h_ref, o_ref, *rest):
        res_refs, buf = rest[:len(dils)], rest[len(dils)]
        hv = _modnorm_tile(x_ref, g_ref, sc_ref, sh_ref)
        h_ref[...] = hv
        for j in range(N_CHIPS):
            rf = _nn(hv, w_ref[j, 0])
            r = rf.astype(BF)
            for a, lc, off, width in _col_pieces(j, wc):
                o_ref[a, :, lc:lc + width] = r[:, off:off + width]
                if a < 3:
                    continue
                for c0 in range(0, width, 128):
                    cg = (lc + c0) // 128
                    buf[...] = rf[:, off + c0:off + c0 + 128]
                    for ref, dil in zip(res_refs, dils):
                        for rr in range(dil):
                            ref[a - 3, :, rr * D_GRP + cg * 128:rr * D_GRP + (cg + 1) * 128] = (
                                buf[pl.ds(rr, TM // dil, stride=dil), :].astype(BF))

    row = pl.BlockSpec((TM, d), lambda m: (m, 0))
    ex = pl.BlockSpec((None, 1, d), lambda m: (m // per, 0, 0))
    return _call(
        body, "qkv_proj", (t // TM,),
        [row, pl.BlockSpec((1, d), lambda m: (0, 0)), ex, ex, _whole(w_in)],
        [row, pl.BlockSpec((6, TM, D_GRP), lambda m: (0, m, 0))]
        + [pl.BlockSpec((3, TM // dil, dil * D_GRP), lambda m: (0, m, 0)) for dil in dils],
        [_sds((t, d), BF), _sds((6, t, D_GRP), BF)] + [_sds((3, t // dil, dil * D_GRP), BF) for dil in dils],
        (x, g, sc, sh, w_in), scratch=[pltpu.VMEM((TM, 128), F32)], carry=carry)


def _col_pieces(j, wc):
    out, off = [], 0
    while off < wc:
        a, lc = divmod(j * wc + off, D_GRP)
        width = min(D_GRP - lc, wc - off)
        out.append((a, lc, off, width))
        off += width
    return out


def _chip_cols(g6_ref, j, wc):
    return jnp.concatenate([g6_ref[a, :, lc:lc + width] for a, lc, _, width in _col_pieces(j, wc)], axis=1)


def _mix_out(on_sb, on_dil, w_out, x, gt, seq):
    t, d = x.shape
    per = seq // TM

    def body(a_ref, b_ref, w_ref, x_ref, gt_ref, t_ref, xo_ref):
        tv = _nn(a_ref[...], w_ref[0:D_GRP, :]) + _nn(b_ref[...], w_ref[D_GRP:2 * D_GRP, :])
        t_ref[...] = tv.astype(BF)
        xo_ref[...] = x_ref[...] + gt_ref[...] * tv

    row = pl.BlockSpec((TM, d), lambda m: (m, 0))
    half = pl.BlockSpec((TM, D_GRP), lambda m: (m, 0))
    return pl.pallas_call(
        body, name="mix_out", grid=(t // TM,),
        in_specs=[half, half, pl.BlockSpec((2 * D_GRP, d), lambda m: (0, 0)), row,
                  pl.BlockSpec((None, 1, d), lambda m: (m // per, 0, 0))],
        out_specs=[row, row],
        out_shape=[_sds((t, d), BF), _sds((t, d), F32)],
        compiler_params=_cp(1))(on_sb, on_dil, w_out, x, gt)


def _sb_masks():
    lane = lax.broadcasted_iota(jnp.int32, (1, 2 * HEAD_DIM), 1)
    hm0 = lane < HEAD_DIM
    rel = lax.broadcasted_iota(jnp.int32, (TQ, KB), 0) - lax.broadcasted_iota(jnp.int32, (TQ, KB), 1)
    kr = lax.broadcasted_iota(jnp.int32, (KB, KB), 0)
    kc = lax.broadcasted_iota(jnp.int32, (KB, KB), 1)
    return hm0, rel, kr, kc


def _headnorm_pair(o, gv, hm0):
    o2 = o * o
    ms0 = jnp.sum(jnp.where(hm0, o2, 0.0), axis=-1, keepdims=True) * (1.0 / HEAD_DIM)
    ms1 = jnp.sum(jnp.where(hm0, 0.0, o2), axis=-1, keepdims=True) * (1.0 / HEAD_DIM)
    r = jnp.where(hm0, lax.rsqrt(ms0 + EPS), lax.rsqrt(ms1 + EPS))
    return (o * r) * gv


SB_DEAD = -104.0


def _alive(c_l):
    return (jnp.max(c_l) > SB_DEAD).astype(jnp.int32)


def _sb_fwd(qkv6, g_sb, nb, seq, carry=None):
    nq = seq // TQ

    def body(q_ref, k_ref, v_ref, g_ref, o_ref, on_ref):
        qi = pl.program_id(2)
        hm0, rel, kr, kc = _sb_masks()
        upper = _twice((kr > kc).astype(BF))
        heads = _dil_masks()[0]
        qs = _stack_heads(q_ref[...] * SCALE, heads)
        causal2 = jnp.concatenate([rel] * GRP_HEADS, axis=0) > 0

        def logits(kj):
            return _nt(qs, k_ref[pl.ds(pl.multiple_of(kj * KB, KB), KB), :])

        def block(kj, z, causal, c_l, acc):
            ks = pl.multiple_of(kj * KB, KB)
            sp = _softplus(z)
            spm = sp if causal is None else jnp.where(causal, sp, 0.0)
            suf = _nn2(spm, upper)
            w = jnp.exp((z - sp) + (c_l - suf))
            if causal is not None:
                w = jnp.where(causal, w, 0.0)
            return c_l - (suf[:, 0:1] + spm[:, 0:1]), acc + _nn(w.astype(BF), v_ref[pl.ds(ks, KB), :])

        z_next = logits(jnp.maximum(qi - 1, 0))
        c_l, acc = block(qi, logits(qi), causal2, jnp.zeros((GRP_HEADS * TQ, 1), F32),
                         jnp.zeros((GRP_HEADS * TQ, GRP_W), F32))

        def cond(carry):
            return jnp.logical_and(carry[0] <= qi, carry[1] > 0)

        def kbody(carry):
            it, _, c_l, acc, z = carry
            z_next = logits(jnp.maximum(qi - it - 1, 0))
            c_l, acc = block(qi - it, z, None, c_l, acc)
            return it + 1, _alive(c_l), c_l, acc, z_next

        acc = lax.while_loop(cond, kbody, (jnp.int32(1), _alive(c_l), c_l, acc, z_next))[3]
        o = _unstack_heads(acc, heads, TQ)
        o_ref[...] = o.astype(BF)
        gv = g_ref[...]
        for half in range(GRP_W // 128):
            lanes = slice(half * 128, (half + 1) * 128)
            on_ref[:, lanes] = _headnorm_pair(o[:, lanes], gv[:, lanes], hm0).astype(BF)

    w = GRP_W
    full = lambda i: pl.BlockSpec((None, None, seq, w), lambda b, hp, q: (i, b, 0, hp))
    qblk = pl.BlockSpec((None, None, TQ, w), lambda b, hp, q: (0, b, q, hp))
    oblk = pl.BlockSpec((None, TQ, w), lambda b, hp, q: (b, q, hp))
    return _call(
        body, "sb_fwd", (nb, N_HEADS // GRP_HEADS, nq),
        [qblk, full(1), full(2), pl.BlockSpec((1, w), lambda b, hp, q: (0, hp))],
        [oblk, oblk],
        [_sds((nb, seq, D_GRP), BF), _sds((nb, seq, D_GRP), BF)],
        (qkv6, qkv6, qkv6, g_sb), carry=carry)


def _sb_bwd(qkv6, do, nb, seq, carry=None):
    nq = seq // TQ
    nk = seq // KB

    def body(q_ref, k_ref, v_ref, do_ref, out_ref, dk_acc, dv_acc, g_st, s_st):
        qi = pl.program_id(2)
        hm0, rel, kr, kc = _sb_masks()
        upper = _twice((kr > kc).astype(BF))
        lower = (kr < kc).astype(BF)

        @pl.when(qi == 0)
        def _():
            dk_acc[...] = jnp.zeros_like(dk_acc)
            dv_acc[...] = jnp.zeros_like(dv_acc)

        heads = _dil_masks()[0]
        qs = _stack_heads(q_ref[...] * SCALE, heads)
        dos = _stack_heads(do_ref[...], heads)
        causal2 = jnp.concatenate([rel] * GRP_HEADS, axis=0) > 0

        def logits(kj):
            return _nt(qs, k_ref[pl.ds(pl.multiple_of(kj * KB, KB), KB), :])

        def weights(kj, z, causal, c_l):
            ks = pl.multiple_of(kj * KB, KB)
            vb = v_ref[pl.ds(ks, KB), :]
            sp = _softplus(z)
            spm = sp if causal is None else jnp.where(causal, sp, 0.0)
            suf = _nn2(spm, upper)
            lsz = z - sp
            w = jnp.exp(lsz + (c_l - suf))
            if causal is not None:
                w = jnp.where(causal, w, 0.0)
            g_st[kj] = w * _nt(dos, vb)
            s_st[kj] = jnp.exp(lsz)
            dv_acc[pl.ds(ks, KB), :] += _tn(w.astype(BF), dos)
            return c_l - (suf[:, 0:1] + spm[:, 0:1])

        zc = jnp.zeros((GRP_HEADS * TQ, 1), F32)
        z_next = logits(jnp.maximum(qi - 1, 0))
        c_l = weights(qi, logits(qi), causal2, zc)

        def acond(carry):
            return jnp.logical_and(carry[0] <= qi, carry[1] > 0)

        def abody(carry):
            z_next = logits(jnp.maximum(qi - carry[0] - 1, 0))
            c_l = weights(qi - carry[0], carry[3], None, carry[2])
            return carry[0] + 1, _alive(c_l), c_l, z_next

        n_used = lax.while_loop(acond, abody, (jnp.int32(1), _alive(c_l), c_l, z_next))[0]

        def grads(kj, causal, c_g, dq):
            ks = pl.multiple_of(kj * KB, KB)
            kb = k_ref[pl.ds(ks, KB), :]
            g = g_st[kj]
            sig = s_st[kj]
            pre = _nn(g.astype(BF), lower)
            dz = g * (1.0 - sig) - sig * (pre + c_g)
            if causal is not None:
                dz = jnp.where(causal, dz, 0.0)
            dzb = dz.astype(BF)
            dk_acc[pl.ds(ks, KB), :] += _tn(dzb, qs)
            return c_g + (pre[:, KB - 1:KB] + g[:, KB - 1:KB]), dq + _nn(dzb, kb)

        c_g, dq = lax.fori_loop(qi - n_used + 1, qi, lambda kj, cr: grads(kj, None, *cr),
                                (zc, jnp.zeros((GRP_HEADS * TQ, GRP_W), F32)))
        _, dq = grads(qi, causal2, c_g, dq)
        dq = _unstack_heads(dq, heads, TQ) * SCALE
        out_ref[0, pl.ds(pl.multiple_of(qi * TQ, TQ), TQ), :] = dq.astype(BF)

        @pl.when(qi == nq - 1)
        def _():
            out_ref[1] = dk_acc[...].astype(BF)
            out_ref[2] = dv_acc[...].astype(BF)

    w = GRP_W
    full = lambda i: pl.BlockSpec((None, None, seq, w), lambda b, hp, q: (i, b, 0, hp))
    qblk = pl.BlockSpec((None, None, TQ, w), lambda b, hp, q: (0, b, q, hp))
    oblk = pl.BlockSpec((None, TQ, w), lambda b, hp, q: (b, q, hp))
    return _call(
        body, "sb_bwd", (nb, N_HEADS // GRP_HEADS, nq),
        [qblk, full(1), full(2), oblk],
        [pl.BlockSpec((3, None, seq, w), lambda b, hp, q: (0, b, 0, hp))],
        [_sds((6, nb, seq, D_GRP), BF)], (qkv6, qkv6, qkv6, do),
        scratch=[pltpu.VMEM((seq, w), F32), pltpu.VMEM((seq, w), F32),
                 pltpu.VMEM((nk, GRP_HEADS * TQ, KB), F32), pltpu.VMEM((nk, GRP_HEADS * TQ, KB), F32)],
        carry=carry)


def _t5_bucket(n):
    max_exact = N_BUCKETS // 2
    nf = np.maximum(n, 1).astype(np.float32)
    large = max_exact + (np.log(nf / max_exact) / math.log(MAX_DISTANCE / max_exact)
                         * (N_BUCKETS - max_exact)).astype(np.int32)
    large = np.minimum(large, N_BUCKETS - 1)
    return np.where(n < max_exact, n, large).astype(np.int32)


def _bucket_map(dilation):
    step = BLOCK + np.arange(BLOCK)[:, None] - np.arange(2 * BLOCK)[None, :]
    return _t5_bucket(np.clip(step, 0, N_STEPS) * dilation)


GRP_HEADS = 4
GRP_W = GRP_HEADS * HEAD_DIM


def _dil_masks():
    lane = lax.broadcasted_iota(jnp.int32, (1, GRP_W), 1)
    heads = [jnp.logical_and(lane >= HEAD_DIM * i, lane < HEAD_DIM * (i + 1)) for i in range(GRP_HEADS)]
    iq = jnp.bitwise_and(lax.broadcasted_iota(jnp.int32, (GRP_HEADS * BLOCK, BLOCK), 0), BLOCK - 1)
    ik = lax.broadcasted_iota(jnp.int32, (GRP_HEADS * BLOCK, BLOCK), 1)
    return heads, ik <= iq, ik >= iq


def _stack_heads(x, heads):
    zero = jnp.zeros_like(x)
    return jnp.concatenate([jnp.where(hm, x, zero) for hm in heads], axis=0)


def _unstack_heads(xs, heads, rows=BLOCK):
    out = xs[0:rows]
    for i in range(1, GRP_HEADS):
        out = jnp.where(heads[i], xs[i * rows:(i + 1) * rows], out)
    return out


def _dil_rows(n):
    rs = pl.multiple_of(n * BLOCK, BLOCK)
    ps = pl.multiple_of(jnp.maximum(n - 1, 0) * BLOCK, BLOCK)
    return pl.ds(rs, BLOCK), pl.ds(ps, BLOCK)


def _dil_probs(qs, kc, kp, b_ref, gi, valid_c, valid_p):
    rows = slice(gi * GRP_HEADS * BLOCK, (gi + 1) * GRP_HEADS * BLOCK)
    zc = _nt(qs, kc) * SCALE + b_ref[rows, BLOCK:2 * BLOCK]
    zp = _nt(qs, kp) * SCALE + b_ref[rows, 0:BLOCK]
    zc = jnp.where(valid_c, zc, NEG_INF)
    zp = jnp.where(valid_p, zp, NEG_INF)
    m = jnp.maximum(jnp.max(zc, axis=-1, keepdims=True), jnp.max(zp, axis=-1, keepdims=True))
    ec = jnp.exp(zc - m)
    ep = jnp.exp(zp - m)
    den = jnp.sum(ec, axis=-1, keepdims=True) + jnp.sum(ep, axis=-1, keepdims=True)
    return ec, ep, den, m


def _dil_fwd(qkv6r, base, bias, nb, sub_len, dilation):
    n_blk = sub_len // BLOCK

    def body(q_ref, k_ref, v_ref, b_ref, o_ref, l_ref):
        heads, valid_c, valid_p0 = _dil_masks()

        def nbody(n, carry):
            cur, prev = _dil_rows(n)
            valid_p = jnp.logical_and(valid_p0, n > 0)
            for gi in range(N_HEADS // GRP_HEADS):
                lanes = slice(gi * GRP_W, (gi + 1) * GRP_W)
                qs = _stack_heads(q_ref[cur, lanes], heads)
                ec, ep, den, m = _dil_probs(qs, k_ref[cur, lanes], k_ref[prev, lanes], b_ref, gi, valid_c, valid_p)
                o = (_nn(ec.astype(BF), v_ref[cur, lanes]) + _nn(ep.astype(BF), v_ref[prev, lanes])) / den
                o_ref[cur, lanes] = _unstack_heads(o, heads).astype(BF)
                l_ref[cur, lanes] = _unstack_heads(jnp.broadcast_to(m + jnp.log(den), o.shape), heads)
            return carry

        lax.fori_loop(0, n_blk, nbody, 0)

    seqblk = lambda i: pl.BlockSpec((None, None, sub_len, D_GRP), lambda b, r: (i, b, 0, r))
    oblk = pl.BlockSpec((None, sub_len, D_GRP), lambda b, r: (b, 0, r))
    shp = _sds((nb, sub_len, dilation * D_GRP), F32)
    return pl.pallas_call(
        body, name="dil_fwd_%d" % dilation, grid=(nb, dilation),
        in_specs=[seqblk(base), seqblk(base + 1), seqblk(base + 2), _whole(bias)],
        out_specs=[oblk, oblk], out_shape=[_sds(shp.shape, BF), shp],
        compiler_params=_cp(2))(qkv6r, qkv6r, qkv6r, bias)


def _dil_bwd(qkv6r, base, bias, do_c, dd_c, nb, sub_len, dilation, carry=None):
    n_blk = sub_len // BLOCK

    def body(q_ref, k_ref, v_ref, b_ref, do_ref, dd_ref, out_ref, a_ref, dk_acc, dv_acc):
        heads, valid_c, valid_p0 = _dil_masks()
        first = jnp.logical_and(pl.program_id(0) == 0, pl.program_id(1) == 0)

        @pl.when(first)
        def _():
            a_ref[...] = jnp.zeros_like(a_ref)

        dk_acc[...] = jnp.zeros_like(dk_acc)
        dv_acc[...] = jnp.zeros_like(dv_acc)

        def nbody(n, carry):
            cur, prev = _dil_rows(n)
            valid_p = jnp.logical_and(valid_p0, n > 0)
            for gi in range(N_HEADS // GRP_HEADS):
                lanes = slice(gi * GRP_W, (gi + 1) * GRP_W)
                kc, kp = k_ref[cur, lanes], k_ref[prev, lanes]
                vc, vp = v_ref[cur, lanes], v_ref[prev, lanes]
                qs = _stack_heads(q_ref[cur, lanes], heads)
                dos = _stack_heads(do_ref[cur, lanes], heads).astype(BF)
                dds = jnp.sum(_stack_heads(dd_ref[cur, lanes], heads), axis=-1, keepdims=True) * (1.0 / HEAD_DIM)
                ec, ep, den, _ = _dil_probs(qs, kc, kp, b_ref, gi, valid_c, valid_p)
                inv = 1.0 / den
                pc = ec * inv
                pp = ep * inv
                dzc = pc * (_nt(dos, vc) + dds)
                dzp = pp * (_nt(dos, vp) + dds)
                rows = slice(gi * GRP_HEADS * BLOCK, (gi + 1) * GRP_HEADS * BLOCK)
                a_ref[rows, BLOCK:2 * BLOCK] += dzc
                a_ref[rows, 0:BLOCK] += dzp
                dzcb = (dzc * SCALE).astype(BF)
                dzpb = (dzp * SCALE).astype(BF)
                out_ref[0, cur, lanes] = _unstack_heads(_nn(dzcb, kc) + _nn(dzpb, kp), heads).astype(BF)
                dk_acc[cur, lanes] += _tn(dzcb, qs)
                dk_acc[prev, lanes] += _tn(dzpb, qs)
                dv_acc[cur, lanes] += _tn(pc.astype(BF), dos)
                dv_acc[prev, lanes] += _tn(pp.astype(BF), dos)
            return carry

        lax.fori_loop(0, n_blk, nbody, 0)
        out_ref[1] = dk_acc[...].astype(BF)
        out_ref[2] = dv_acc[...].astype(BF)

    seqblk = lambda i: pl.BlockSpec((None, None, sub_len, D_GRP), lambda b, r: (i, b, 0, r))
    oblk = pl.BlockSpec((None, sub_len, D_GRP), lambda b, r: (b, 0, r))
    return _call(
        body, "dil_bwd_%d" % dilation, (nb, dilation),
        [seqblk(base), seqblk(base + 1), seqblk(base + 2), _whole(bias), oblk, oblk],
        [pl.BlockSpec((3, None, sub_len, D_GRP), lambda b, r: (0, b, 0, r)),
         pl.BlockSpec((N_HEADS * BLOCK, 2 * BLOCK), lambda b, r: (0, 0))],
        [_sds((3, nb, sub_len, dilation * D_GRP), BF), _sds((N_HEADS * BLOCK, 2 * BLOCK), F32)],
        (qkv6r, qkv6r, qkv6r, bias, do_c, dd_c),
        scratch=[pltpu.VMEM((sub_len, D_GRP), F32)] * 2, carry=carry)


def _group_ones():
    idx = np.arange(D_GRP) // HEAD_DIM
    return jnp.asarray((idx[:, None] == idx[None, :]).astype(np.float32), dtype=BF)


def _dil_alphas(l1, l4, l16):
    mx = jnp.maximum(jnp.maximum(l1, l4), l16)
    e1 = jnp.exp(l1 - mx)
    e4 = jnp.exp(l4 - mx)
    e16 = jnp.exp(l16 - mx)
    den = e1 + e4 + e16
    return e1 / den, e4 / den, e16 / den


def _residue_spec(dil):
    return pl.BlockSpec((TM // dil, dil * D_GRP), lambda m: (m, 0))


def _from_residue(src, dil, cg, buf):
    if dil == 1:
        return src[:, cg * 128:(cg + 1) * 128].astype(F32)
    for r in range(dil):
        buf[pl.ds(r, TM // dil, stride=dil), :] = (
            src[:, r * D_GRP + cg * 128:r * D_GRP + (cg + 1) * 128].astype(F32))
    return buf[...]


def _to_residue(dst, dil, cg, buf, val):
    if dil == 1:
        dst[:, cg * 128:(cg + 1) * 128] = val.astype(dst.dtype)
        return
    buf[...] = val
    for r in range(dil):
        dst[:, r * D_GRP + cg * 128:r * D_GRP + (cg + 1) * 128] = (
            buf[pl.ds(r, TM // dil, stride=dil), :].astype(dst.dtype))


def _pair_sum(x, hm0):
    s0 = jnp.sum(jnp.where(hm0, x, 0.0), axis=-1, keepdims=True)
    s1 = jnp.sum(jnp.where(hm0, 0.0, x), axis=-1, keepdims=True)
    return jnp.where(hm0, s0, s1)


def _dil_comb(os, ls, g_dil):
    t = os[0].shape[0]
    dils = [dil for _, dil in DIL_CONFIGS]

    def body(o1, l1, o4, l4, o16, l16, g_ref, o_ref, on_ref, b0, b1, b2, b3):
        hm0 = lax.broadcasted_iota(jnp.int32, (1, 128), 1) < HEAD_DIM
        for cg in range(D_GRP // 128):
            lanes = slice(cg * 128, (cg + 1) * 128)
            ov = [_from_residue(src, dil, cg, buf) for src, dil, buf in zip((o1, o4, o16), dils, (None, b0, b1))]
            lv = [_from_residue(src, dil, cg, buf) for src, dil, buf in zip((l1, l4, l16), dils, (None, b2, b3))]
            a1, a4, a16 = _dil_alphas(*lv)
            o = a1 * ov[0] + a4 * ov[1] + a16 * ov[2]
            o_ref[:, lanes] = o.astype(BF)
            on_ref[:, lanes] = _headnorm_pair(o, g_ref[:, lanes], hm0).astype(BF)

    blk = pl.BlockSpec((TM, D_GRP), lambda m: (m, 0))
    specs = [_residue_spec(dil) for dil in dils for _ in range(2)]
    return pl.pallas_call(
        body, name="dil_comb", grid=(t // TM,),
        in_specs=specs + [pl.BlockSpec((1, D_GRP), lambda m: (0, 0))],
        out_specs=[blk, blk],
        out_shape=[_sds((t, D_GRP), BF), _sds((t, D_GRP), BF)],
        scratch_shapes=[pltpu.VMEM((TM, 128), F32)] * 4,
        compiler_params=_cp(1))(os[0], ls[0], os[1], ls[1], os[2], ls[2], g_dil)


def _dil_comb_bwd(do, os, ls):
    t = do.shape[0]
    dils = [dil for _, dil in DIL_CONFIGS]

    def body(do_ref, o1, l1, o4, l4, o16, l16, d1, d4, d16, e1, e4, e16, b0, b1, b2, b3):
        hm0 = lax.broadcasted_iota(jnp.int32, (1, 128), 1) < HEAD_DIM
        for cg in range(D_GRP // 128):
            dov = do_ref[:, cg * 128:(cg + 1) * 128].astype(F32)
            ov = [_from_residue(src, dil, cg, buf) for src, dil, buf in zip((o1, o4, o16), dils, (None, b0, b1))]
            lv = [_from_residue(src, dil, cg, buf) for src, dil, buf in zip((l1, l4, l16), dils, (None, b2, b3))]
            al = _dil_alphas(*lv)
            sbar = al[0] * _pair_sum(dov * ov[0], hm0)
            for a_c, o_c in zip(al[1:], ov[1:]):
                sbar = sbar + a_c * _pair_sum(dov * o_c, hm0)
            for a_c, dil, dref, eref in zip(al, dils, (d1, d4, d16), (e1, e4, e16)):
                _to_residue(dref, dil, cg, b0, a_c * dov)
                _to_residue(eref, dil, cg, b1, -a_c * sbar)

    specs = [_residue_spec(dil) for dil in dils]
    return pl.pallas_call(
        body, name="dil_comb_bwd", grid=(t // TM,),
        in_specs=[pl.BlockSpec((TM, D_GRP), lambda m: (m, 0))] + [sp for sp in specs for _ in range(2)],
        out_specs=specs + specs,
        out_shape=[_sds((t // dil, dil * D_GRP), BF) for dil in dils]
        + [_sds((t // dil, dil * D_GRP), F32) for dil in dils],
        scratch_shapes=[pltpu.VMEM((TM, 128), F32)] * 4,
        compiler_params=_cp(1))(do, os[0], ls[0], os[1], ls[1], os[2], ls[2])


def _dqkv_dil_sum(ds, dqkv6):
    t = dqkv6.shape[1]
    dils = [dil for _, dil in DIL_CONFIGS]

    def body(*refs):
        srcs, o_ref, acc = refs[:len(dils)], refs[len(dils) + 1], refs[len(dils) + 2]
        for a in range(3):
            for cg in range(D_GRP // 128):
                for src, dil in zip(srcs, dils):
                    for r in range(dil):
                        part = src[a, :, r * D_GRP + cg * 128:r * D_GRP + (cg + 1) * 128].astype(F32)
                        rows = pl.ds(r, TM // dil, stride=dil) if dil > 1 else slice(None)
                        if dil == dils[0]:
                            acc[rows, :] = part
                        else:
                            acc[rows, :] += part
                o_ref[a, :, cg * 128:(cg + 1) * 128] = acc[...].astype(BF)

    return pl.pallas_call(
        body, name="dqkv_dil_sum", grid=(t // TM,),
        in_specs=[pl.BlockSpec((3, TM // dil, dil * D_GRP), lambda m: (0, m, 0)) for dil in dils]
        + [pl.BlockSpec(memory_space=pl.ANY)],
        out_specs=pl.BlockSpec((3, TM, D_GRP), lambda m: (1, m, 0)),
        out_shape=_sds((6, t, D_GRP), BF), input_output_aliases={len(dils): 0},
        scratch_shapes=[pltpu.VMEM((TM, 128), F32)],
        compiler_params=_cp(1))(*ds, dqkv6)


def _relbias_grad(a_all, onehot):
    def body(a_ref, oh_ref, o_ref):
        acc = jnp.zeros((N_HEADS, N_BUCKETS), F32)
        for c in range(len(DIL_CONFIGS)):
            av = a_ref[c]
            hi = av.astype(BF)
            lo = (av - hi.astype(F32)).astype(BF)
            acc = acc + _nt(hi, oh_ref[c]) + _nt(lo, oh_ref[c])
        o_ref[...] = acc

    return pl.pallas_call(body, name="relbias_grad", out_shape=_sds((N_HEADS, N_BUCKETS), F32),
                          compiler_params=_cp())(a_all, onehot)


def _headnorm_bwd(dn, o, gv, mv):
    ms = _nn2(o * o, mv) * (1.0 / HEAD_DIM)
    r = lax.rsqrt(ms + EPS)
    nrm = o * r
    dg = jnp.sum(dn * nrm, axis=0, keepdims=True)
    dnn = dn * gv
    do = r * (dnn - nrm * (_nn2(dnn * nrm, mv) * (1.0 / HEAD_DIM)))
    return do, dg


def _mix_bwd_out(dx, gt, tv, w_out, o_sb, o_dil, on_sb, on_dil, g_sb, g_dil, ones_g, seq, carry=None):
    t, d = dx.shape
    per = seq // TM
    nb = t // seq

    def body(dx_ref, gt_ref, t_ref, w_ref, osb, odl, onsb, ondl, gsb, gdl, m_ref,
             dosb, dodl, dgt_ref, dgsb, dgdl, dw_ref):
        m = pl.program_id(0)
        dxv = dx_ref[...]
        dt = (gt_ref[...] * dxv).astype(BF)
        _acc_rows(dgt_ref, jnp.sum(dxv * t_ref[...].astype(F32), axis=0, keepdims=True), m % per == 0)
        mv = _twice(m_ref[...])
        don_sb = _nt(dt, w_ref[0:D_GRP, :])
        don_dl = _nt(dt, w_ref[D_GRP:2 * D_GRP, :])
        do1, dg1 = _headnorm_bwd(don_sb, osb[...].astype(F32), gsb[...], mv)
        do2, dg2 = _headnorm_bwd(don_dl, odl[...].astype(F32), gdl[...], mv)
        dosb[...] = do1.astype(BF)
        dodl[...] = do2.astype(BF)
        _acc_rows(dgsb, dg1, m == 0)
        _acc_rows(dgdl, dg2, m == 0)
        p1 = _tn(onsb[...], dt)
        p2 = _tn(ondl[...], dt)

        @pl.when(m == 0)
        def _():
            dw_ref[0:D_GRP, :] = p1
            dw_ref[D_GRP:2 * D_GRP, :] = p2

        @pl.when(m != 0)
        def _():
            dw_ref[0:D_GRP, :] += p1
            dw_ref[D_GRP:2 * D_GRP, :] += p2

    row = pl.BlockSpec((TM, d), lambda m: (m, 0))
    half = pl.BlockSpec((TM, D_GRP), lambda m: (m, 0))
    ex = pl.BlockSpec((None, 1, d), lambda m: (m // per, 0, 0))
    gvec = pl.BlockSpec((1, D_GRP), lambda m: (0, 0))
    wblk = pl.BlockSpec((2 * D_GRP, d), lambda m: (0, 0))
    return _call(
        body, "mix_bwd_out", (t // TM,),
        [row, ex, row, wblk, half, half, half, half, gvec, gvec, pl.BlockSpec((D_GRP, D_GRP), lambda m: (0, 0))],
        [half, half, ex, gvec, gvec, wblk],
        [_sds((t, D_GRP), BF), _sds((t, D_GRP), BF), _sds((nb, 1, d), F32),
         _sds((1, D_GRP), F32), _sds((1, D_GRP), F32), _sds((2 * D_GRP, d), F32)],
        (dx, gt, tv, w_out, o_sb, o_dil, on_sb, on_dil, g_sb, g_dil, ones_g), carry=carry)


def _dw_in(h, dqkv6, carry=None):
    t, d = h.shape
    wc = 6 * D_GRP // N_CHIPS

    def body(h_ref, g_ref, o_ref):
        kt = pl.program_id(0)
        hv = h_ref[...]
        for j in range(N_CHIPS):
            p = _tn(hv, _chip_cols(g_ref, j, wc))

            @pl.when(kt == 0)
            def _(p=p, j=j):
                o_ref[j, 0] = p

            @pl.when(kt != 0)
            def _(p=p, j=j):
                o_ref[j, 0] += p

    return _call(
        body, "dw_in", (t // TK_W,),
        [pl.BlockSpec((TK_W, d), lambda kt: (kt, 0)), pl.BlockSpec((6, TK_W, D_GRP), lambda kt: (0, kt, 0))],
        [pl.BlockSpec((N_CHIPS, 1, d, wc), lambda kt: (0, 0, 0, 0))],
        [_sds((N_CHIPS, 1, d, wc), F32)], (h, dqkv6), carry=carry)


def _mix_bwd_dh(dqkv6, w_in, x, g, sc, dxo, seq, carry=None):
    _, t, _ = dqkv6.shape
    d = x.shape[-1]
    wc = w_in.shape[-1]
    per = seq // TM
    nb = t // seq

    def body(g6_ref, w_ref, x_ref, g_ref, sc_ref, dxo_ref, dx_ref, dsh_ref, dsc_ref, dg_ref):
        m = pl.program_id(0)
        dh = _nt(_chip_cols(g6_ref, 0, wc), w_ref[0, 0])
        for j in range(1, N_CHIPS):
            dh = dh + _nt(_chip_cols(g6_ref, j, wc), w_ref[j, 0])
        dx, dsh, dsc, dg = _modnorm_bwd_tile(dh, x_ref[...], g_ref[...], sc_ref[...], dxo_ref[...])
        dx_ref[...] = dx
        _acc_rows(dsh_ref, dsh, m % per == 0)
        _acc_rows(dsc_ref, dsc, m % per == 0)
        _acc_rows(dg_ref, dg, m == 0)

    row = pl.BlockSpec((TM, d), lambda m: (m, 0))
    ex = pl.BlockSpec((None, 1, d), lambda m: (m // per, 0, 0))
    vec = pl.BlockSpec((1, d), lambda m: (0, 0))
    return _call(
        body, "mix_bwd_dh", (t // TM,),
        [pl.BlockSpec((6, TM, D_GRP), lambda m: (0, m, 0)), _whole(w_in), row, vec, ex, row],
        [row, ex, ex, vec],
        [_sds((t, d), F32), _sds((nb, 1, d), F32), _sds((nb, 1, d), F32), _sds((1, d), F32)],
        (dqkv6, w_in, x, g, sc, dxo), carry=carry)


def _ffn_down_loss(s, wd, x, gt, seq, coef, g, target):
    _, t, fs = s.shape
    d = x.shape[-1]
    per = seq // TM
    steps = t // TM

    def body(s_ref, w_ref, x_ref, gt_ref, g_ref, t_ref, f_ref, dx_ref, dg_ref, loss_ref, lacc):
        m = pl.program_id(0)
        f = _nn(s_ref[0], w_ref[0, 0])
        for j in range(1, N_CHIPS):
            f = f + _nn(s_ref[j], w_ref[j, 0])
        f_ref[...] = f.astype(BF)
        xv = x_ref[...] + (coef * gt_ref[...]) * f
        gv = g_ref[...]
        r = lax.rsqrt(jnp.mean(xv * xv, axis=-1, keepdims=True) + EPS)
        n = xv * r
        err = n * gv - t_ref[...]
        dy = err * (1.0 / d)
        _acc_rows(dg_ref, jnp.sum(dy * n, axis=0, keepdims=True), m == 0)
        dn = dy * gv
        dx_ref[...] = r * (dn - n * jnp.mean(dn * n, axis=-1, keepdims=True))
        _acc_rows(lacc, jnp.sum(err * err, axis=0, keepdims=True), m == 0)

        @pl.when(m == steps - 1)
        def _():
            tot = jnp.sum(lacc[...], axis=-1, keepdims=True) * (0.5 / d)
            loss_ref[...] = jnp.broadcast_to(tot, (1, 128))

    row = pl.BlockSpec((TM, d), lambda m: (m, 0))
    vec = pl.BlockSpec((1, d), lambda m: (0, 0))
    return pl.pallas_call(
        body, name="ffn_down_loss", grid=(steps,),
        in_specs=[pl.BlockSpec((N_CHIPS, TM, fs), lambda m: (0, m, 0)), _whole(wd), row,
                  pl.BlockSpec((None, 1, d), lambda m: (m // per, 0, 0)), vec, row],
        out_specs=[row, row, vec, pl.BlockSpec((1, 128), lambda m: (0, 0))],
        out_shape=[_sds((t, d), BF), _sds((t, d), F32), _sds((1, d), F32), _sds((1, 128), F32)],
        scratch_shapes=[pltpu.VMEM((1, d), F32)],
        compiler_params=_cp(1))(s, wd, x, gt, g, target)


def _row_tile(rows, cols):
    best = rows
    for tr in range(8, rows + 1, 8):
        if rows % tr == 0 and tr * cols * 4 <= (1 << 20):
            best = tr
    if best * cols * 4 > (1 << 21):
        best = 8
    return best


def _adamw(w, g_arr, g_sel, m, v):
    rows, cols = w.shape
    tr = _row_tile(rows, cols)
    b1c = 1.0 - ADAM_B1 ** ADAM_STEP
    b2c = 1.0 - ADAM_B2 ** ADAM_STEP

    def body(w_ref, g_ref, m_ref, v_ref, go_ref, d_ref, mo_ref, vo_ref):
        gv = g_ref[...]
        mn = ADAM_B1 * m_ref[...] + (1.0 - ADAM_B1) * gv
        vn = ADAM_B2 * v_ref[...] + (1.0 - ADAM_B2) * (gv * gv)
        go_ref[...] = gv
        mo_ref[...] = mn
        vo_ref[...] = vn
        d_ref[...] = -ADAM_LR * ((mn / b1c) / (jnp.sqrt(vn / b2c) + ADAM_EPS) + ADAM_WD * w_ref[...])

    blk = pl.BlockSpec((tr, cols), lambda i: (i, 0))
    shp = _sds((rows, cols), F32)
    return pl.pallas_call(
        body, name="adamw", grid=(rows // tr,),
        in_specs=[blk, pl.BlockSpec((None, tr, cols), lambda i: (g_sel, i, 0)), blk, blk],
        out_specs=[blk] * 4, out_shape=[shp] * 4,
        compiler_params=_cp(1))(w, g_arr, m, v)


def _flip(v, bit):
    return 1 - v if bit else v


def _my_place():
    x, y, c = lax.axis_index("x"), lax.axis_index("y"), lax.axis_index("c")
    return x, y, c


class _Exchange:
    def __init__(self, operands, out_shape, aliases, sems, start, finish):
        self.operands, self.out_shape, self.aliases, self.sems = list(operands), list(out_shape), dict(aliases), list(sems)
        self.start, self.finish = start, finish


def _join(exchanges):
    exchanges = [e for e in exchanges if e is not None]
    if not exchanges:
        return None
    ops, outs, sems, aliases, spans = [], [], [], {}, []
    for e in exchanges:
        spans.append((len(ops), len(outs), len(sems), e))
        for i, j in e.aliases.items():
            aliases[len(ops) + i] = len(outs) + j
        ops += e.operands
        outs += e.out_shape
        sems += e.sems

    def run(which):
        def go(ins, res, sm):
            for io, oo, so, e in spans:
                getattr(e, which)(ins[io:io + len(e.operands)], res[oo:oo + len(e.out_shape)], sm[so:so + len(e.sems)])
        return go

    return _Exchange(ops, outs, aliases, sems, run("start"), run("finish"))


def _call(body, name, grid, in_specs, out_specs, out_shape, args, scratch=(), carry=None, io_alias=None):
    in_specs, out_specs, out_shape, scratch = list(in_specs), list(out_specs), list(out_shape), list(scratch)
    io_alias = dict(io_alias or {})
    if carry is None:
        return pl.pallas_call(body, name=name, grid=grid, in_specs=in_specs, out_specs=out_specs,
                              out_shape=out_shape, scratch_shapes=scratch, input_output_aliases=io_alias,
                              compiler_params=_cp(len(grid)))(*args)
    n_in, n_out, n_s = len(in_specs), len(out_specs), len(scratch)
    c_in, c_out = len(carry.operands), len(carry.out_shape)
    any_spec = pl.BlockSpec(memory_space=pl.ANY)

    def wrapped(*refs):
        ins, cins = refs[:n_in], refs[n_in:n_in + c_in]
        o0 = n_in + c_in
        outs, couts = refs[o0:o0 + n_out], refs[o0 + n_out:o0 + n_out + c_out]
        s0 = o0 + n_out + c_out
        scr, sems = refs[s0:s0 + n_s], refs[s0 + n_s:]
        first = pl.program_id(0) == 0
        last = pl.program_id(0) == grid[0] - 1
        for ax in range(1, len(grid)):
            first = jnp.logical_and(first, pl.program_id(ax) == 0)
            last = jnp.logical_and(last, pl.program_id(ax) == grid[ax] - 1)

        @pl.when(first)
        def _():
            carry.start(cins, couts, sems)

        body(*ins, *outs, *scr)

        @pl.when(last)
        def _():
            carry.finish(cins, couts, sems)

    return pl.pallas_call(
        wrapped, name=name, grid=grid, in_specs=in_specs + [any_spec] * c_in,
        out_specs=out_specs + [any_spec] * c_out, out_shape=out_shape + carry.out_shape,
        scratch_shapes=scratch + carry.sems,
        input_output_aliases={**io_alias, **{n_in + i: n_out + j for i, j in carry.aliases.items()}},
        compiler_params=_cp(len(grid)))(*args, *carry.operands)


def _whole_call(body, name, args, out_shape, scratch, carry=None):
    vm = pl.BlockSpec(memory_space=pltpu.VMEM)
    any_spec = pl.BlockSpec(memory_space=pl.ANY)
    out_shape, scratch = list(out_shape), list(scratch)
    n_in, n_out, n_s = len(args), len(out_shape), len(scratch)
    if carry is None:
        return pl.pallas_call(body, name=name, in_specs=[vm] * n_in, out_specs=[vm] * n_out, out_shape=out_shape,
                              scratch_shapes=scratch, compiler_params=_cp())(*args)
    c_in, c_out = len(carry.operands), len(carry.out_shape)

    def wrapped(*refs):
        ins, cins = refs[:n_in], refs[n_in:n_in + c_in]
        o0 = n_in + c_in
        outs, couts = refs[o0:o0 + n_out], refs[o0 + n_out:o0 + n_out + c_out]
        s0 = o0 + n_out + c_out
        scr, sems = refs[s0:s0 + n_s], refs[s0 + n_s:]
        carry.start(cins, couts, sems)
        body(*ins, *outs, *scr)
        carry.finish(cins, couts, sems)

    return pl.pallas_call(
        wrapped, name=name, in_specs=[vm] * n_in + [any_spec] * c_in, out_specs=[vm] * n_out + [any_spec] * c_out,
        out_shape=out_shape + carry.out_shape, scratch_shapes=scratch + carry.sems,
        input_output_aliases={n_in + i: n_out + j for i, j in carry.aliases.items()},
        compiler_params=_cp())(*args, *carry.operands)


def _alone(name, ex):
    any_spec = pl.BlockSpec(memory_space=pl.ANY)
    c_in, c_out = len(ex.operands), len(ex.out_shape)

    def body(*refs):
        ins, outs, sems = refs[:c_in], refs[c_in:c_in + c_out], refs[c_in + c_out:]
        ex.start(ins, outs, sems)
        ex.finish(ins, outs, sems)

    return pl.pallas_call(
        body, name=name, in_specs=[any_spec] * c_in, out_specs=[any_spec] * c_out, out_shape=ex.out_shape,
        scratch_shapes=ex.sems, input_output_aliases=ex.aliases, compiler_params=_cp())(*ex.operands)


def _ada_fwd(c_pad, w_ada, b_shard, carry=None):
    d = c_pad.shape[-1]
    cols = w_ada.shape[-1]
    chunk = 384

    def body(c_ref, w_ref, b_ref, call_ref, mod_ref, part, s1, r1, s2, r2):
        x, y, c = _my_place()
        dev = 4 * x + 2 * y + c
        chip = 2 * x + y
        call_ref[dev] = c_ref[...]

        def c_copy(k):
            px, py, pc = _flip(x, (k >> 2) & 1), _flip(y, (k >> 1) & 1), _flip(c, k & 1)
            return px, py, pc

        sends = []
        for k in range(1, N_DEV):
            px, py, pc = c_copy(k)
            cp = pltpu.make_async_remote_copy(src_ref=c_ref, dst_ref=call_ref.at[dev], send_sem=s1.at[k - 1],
                                              recv_sem=r1.at[k - 1], device_id=(px, py, pc), device_id_type=MESH)
            cp.start()
            sends.append(cp)
        for k in range(1, N_DEV):
            px, py, pc = c_copy(k)
            pltpu.make_async_remote_copy(src_ref=c_ref, dst_ref=call_ref.at[4 * px + 2 * py + pc],
                                         send_sem=s1.at[k - 1], recv_sem=r1.at[k - 1],
                                         device_id=(px, py, pc), device_id_type=MESH).wait_recv()
        for cp in sends:
            cp.wait_send()

        cs = call_ref[...].reshape(N_DEV * 8, d)
        sc = (cs * jax.nn.sigmoid(cs)).astype(BF)
        for n0 in range(0, cols, chunk):
            blk = _nn(sc, w_ref[:, n0:n0 + chunk].astype(BF)) + b_ref[:, n0:n0 + chunk]
            part[:, :, n0:n0 + chunk] = blk.reshape(N_DEV, 8, chunk)

        mod_ref[chip] = part[dev]
        sends = []
        for kk in range(1, N_CHIPS):
            px, py = _flip(x, (kk >> 1) & 1), _flip(y, kk & 1)
            cp = pltpu.make_async_remote_copy(src_ref=part.at[4 * px + 2 * py + c], dst_ref=mod_ref.at[chip],
                                              send_sem=s2.at[kk - 1], recv_sem=r2.at[kk - 1],
                                              device_id=(px, py, c), device_id_type=MESH)
            cp.start()
            sends.append(cp)
        for kk in range(1, N_CHIPS):
            px, py = _flip(x, (kk >> 1) & 1), _flip(y, kk & 1)
            pltpu.make_async_remote_copy(src_ref=part.at[dev], dst_ref=mod_ref.at[2 * px + py],
                                         send_sem=s2.at[kk - 1], recv_sem=r2.at[kk - 1],
                                         device_id=(px, py, c), device_id_type=MESH).wait_recv()
        for cp in sends:
            cp.wait_send()

    return _whole_call(
        body, "ada_fwd", (c_pad, w_ada, b_shard),
        [_sds((N_DEV, 8, d), F32), _sds((N_CHIPS, 8, cols), F32)],
        [pltpu.VMEM((N_DEV, 8, cols), F32),
         pltpu.SemaphoreType.DMA((N_DEV - 1,)), pltpu.SemaphoreType.DMA((N_DEV - 1,)),
         pltpu.SemaphoreType.DMA((N_CHIPS - 1,)), pltpu.SemaphoreType.DMA((N_CHIPS - 1,))], carry=carry)


def _ag_weights(bufs, kks=(1, 2, 3), relative=False):
    n, nk = len(bufs), len(kks)

    def half(b, which):
        hr = bufs[b].shape[2] // 2
        return pl.ds(pl.multiple_of(which * hr, 16), hr)

    def copies(outs, sems, b, i, kk):
        x, y, c = _my_place()
        chip = 2 * x + y
        px, py = _flip(x, (kk >> 1) & 1), _flip(y, kk & 1)
        mine, theirs = (0, kk) if relative else (chip, 2 * px + py)
        landing = kk if relative else chip
        k = nk * b + i
        send = pltpu.make_async_remote_copy(
            src_ref=outs[b].at[mine, :, half(b, c), :], dst_ref=outs[b].at[landing, :, half(b, c), :],
            send_sem=sems[0].at[k], recv_sem=sems[1].at[k], device_id=(px, py, c), device_id_type=MESH)
        got = outs[b].at[theirs, :, half(b, c), :]
        recv = pltpu.make_async_remote_copy(
            src_ref=got, dst_ref=got, send_sem=sems[0].at[k], recv_sem=sems[1].at[k],
            device_id=(px, py, c), device_id_type=MESH)
        fwd = pltpu.make_async_remote_copy(
            src_ref=got, dst_ref=got, send_sem=sems[2].at[k], recv_sem=sems[3].at[k],
            device_id=(x, y, 1 - c), device_id_type=MESH)
        other = outs[b].at[theirs, :, half(b, 1 - c), :]
        back = pltpu.make_async_remote_copy(
            src_ref=other, dst_ref=other, send_sem=sems[2].at[k], recv_sem=sems[3].at[k],
            device_id=(x, y, 1 - c), device_id_type=MESH)
        return send, recv, fwd, back

    def each(outs, sems):
        for b in range(n):
            for i, kk in enumerate(kks):
                yield copies(outs, sems, b, i, kk)

    def start(ins, outs, sems):
        for send, _, _, _ in each(outs, sems):
            send.start()

    def finish(ins, outs, sems):
        for _, recv, fwd, _ in each(outs, sems):
            recv.wait_recv()
            fwd.start()
        for send, _, fwd, back in each(outs, sems):
            back.wait_recv()
            send.wait_send()
            fwd.wait_send()

    return _Exchange(bufs, [_sds(s.shape, s.dtype) for s in bufs], {i: i for i in range(n)},
                     [pltpu.SemaphoreType.DMA((nk * n,))] * 4, start, finish)


def _rs_d2d(grads):
    n = len(grads)

    def copy(ins, outs, sems, b):
        x, y, c = _my_place()
        hr = grads[b].shape[2] // 2
        theirs = pl.ds(pl.multiple_of((1 - c) * hr, 8), hr)
        return pltpu.make_async_remote_copy(
            src_ref=ins[b].at[:, :, theirs, :], dst_ref=outs[b], send_sem=sems[0].at[b], recv_sem=sems[1].at[b],
            device_id=(x, y, 1 - c), device_id_type=MESH)

    def start(ins, outs, sems):
        for b in range(n):
            copy(ins, outs, sems, b).start()

    def finish(ins, outs, sems):
        for b in range(n):
            copy(ins, outs, sems, b).wait()

    return _Exchange(grads, [_sds(g.shape[:2] + (g.shape[2] // 2, g.shape[3]), F32) for g in grads], {},
                     [pltpu.SemaphoreType.DMA((n,))] * 2, start, finish)


def _add_halves(core, g, land):
    nchip, ng, rows, cols = g.shape
    hr = rows // 2
    tr = _row_tile(hr, cols)
    steps = hr // tr

    def body(core_ref, g_ref, l_ref, o_ref):
        del core_ref
        o_ref[...] = (g_ref[...] + l_ref[...]).astype(BF)

    return pl.pallas_call(
        body, name="add_halves",
        grid_spec=pltpu.PrefetchScalarGridSpec(
            num_scalar_prefetch=1, grid=(nchip, ng, steps),
            in_specs=[pl.BlockSpec((None, None, tr, cols), lambda j, a, i, cr: (j, a, cr[0] * steps + i, 0)),
                      pl.BlockSpec((None, None, tr, cols), lambda j, a, i, cr: (j, a, i, 0))],
            out_specs=pl.BlockSpec((None, None, tr, cols), lambda j, a, i, cr: (j, a, i, 0))),
        out_shape=_sds((nchip, ng, hr, cols), BF),
        compiler_params=_cp(3))(core, g, land)


def _rs_ici(parts, relative=False):
    n = len(parts)

    def copies(ins, outs, sems):
        x, y, c = _my_place()
        chip = 2 * x + y
        for b in range(n):
            for kk in range(1, N_CHIPS):
                px, py = _flip(x, (kk >> 1) & 1), _flip(y, kk & 1)
                k = 3 * b + kk - 1
                theirs, landing = (kk, kk) if relative else (2 * px + py, chip)
                send = pltpu.make_async_remote_copy(
                    src_ref=ins[b].at[theirs], dst_ref=outs[b].at[landing],
                    send_sem=sems[0].at[k], recv_sem=sems[1].at[k], device_id=(px, py, c), device_id_type=MESH)
                slot = outs[b].at[theirs]
                recv = pltpu.make_async_remote_copy(
                    src_ref=slot, dst_ref=slot, send_sem=sems[0].at[k], recv_sem=sems[1].at[k],
                    device_id=(px, py, c), device_id_type=MESH)
                yield send, recv

    def start(ins, outs, sems):
        for send, _ in copies(ins, outs, sems):
            send.start()

    def finish(ins, outs, sems):
        for send, recv in copies(ins, outs, sems):
            recv.wait_recv()
            send.wait_send()

    return _Exchange(parts, [_sds(p.shape, p.dtype) for p in parts], {},
                     [pltpu.SemaphoreType.DMA((3 * n,))] * 2, start, finish)


def _sum_chips(place, part, land, relative=False):
    nchip, ng, hr, cols = land.shape
    tr = _row_tile(hr, cols)
    steps = hr // tr

    def body(place_ref, p_ref, l1, l2, l3, o_ref):
        del place_ref
        o_ref[...] = ((p_ref[...].astype(F32) + l1[...].astype(F32)) + l2[...].astype(F32)) + l3[...].astype(F32)

    def slot(k):
        if relative:
            return pl.BlockSpec((None, None, tr, cols), lambda a, i, pr: (k, a, i, 0))
        return pl.BlockSpec((None, None, tr, cols), lambda a, i, pr: (jnp.bitwise_xor(pr[1], k), a, i, 0))

    return pl.pallas_call(
        body, name="sum_chips",
        grid_spec=pltpu.PrefetchScalarGridSpec(
            num_scalar_prefetch=1, grid=(ng, steps),
            in_specs=[slot(0), slot(1), slot(2), slot(3)],
            out_specs=pl.BlockSpec((None, tr, cols), lambda a, i, pr: (a, pr[0] * steps + i, 0))),
        out_shape=_sds((ng, 2 * hr, cols), F32),
        compiler_params=_cp(2))(place, part, land, land, land)


def _rs_final(bufs):
    n = len(bufs)

    def copy(outs, sems, b, which):
        x, y, c = _my_place()
        hr = bufs[b].shape[1] // 2
        rows = outs[b].at[:, pl.ds(pl.multiple_of((c if which == 0 else 1 - c) * hr, 8), hr), :]
        return pltpu.make_async_remote_copy(
            src_ref=rows, dst_ref=rows, send_sem=sems[0].at[b], recv_sem=sems[1].at[b],
            device_id=(x, y, 1 - c), device_id_type=MESH)

    def start(ins, outs, sems):
        for b in range(n):
            copy(outs, sems, b, 0).start()

    def finish(ins, outs, sems):
        for b in range(n):
            copy(outs, sems, b, 0).wait_send()
            copy(outs, sems, b, 1).wait_recv()

    return _Exchange(bufs, [_sds(h.shape, F32) for h in bufs], {i: i for i in range(n)},
                     [pltpu.SemaphoreType.DMA((n,))] * 2, start, finish)


def _small_sync(smalls, dmod_blk, c_all, carry=None):
    d = c_all.shape[-1]
    cols = dmod_blk.shape[-1]
    chunk = 384

    def body(sm_ref, dm_ref, c_ref, sum_ref, gw_ref, sm_all, dm_all, ssem, rsem):
        x, y, c = _my_place()
        dev = 4 * x + 2 * y + c
        chip = 2 * x + y
        sm_all[dev] = sm_ref[...]
        dm_all[dev] = dm_ref[chip]
        sends = []
        for k in range(1, N_DEV):
            px, py, pc = _flip(x, (k >> 2) & 1), _flip(y, (k >> 1) & 1), _flip(c, k & 1)
            a = pltpu.make_async_remote_copy(src_ref=sm_ref, dst_ref=sm_all.at[dev], send_sem=ssem.at[2 * (k - 1)],
                                             recv_sem=rsem.at[2 * (k - 1)], device_id=(px, py, pc),
                                             device_id_type=MESH)
            b = pltpu.make_async_remote_copy(src_ref=dm_ref.at[2 * px + py], dst_ref=dm_all.at[dev],
                                             send_sem=ssem.at[2 * (k - 1) + 1], recv_sem=rsem.at[2 * (k - 1) + 1],
                                             device_id=(px, py, pc), device_id_type=MESH)
            a.start()
            b.start()
            sends += [a, b]
        for k in range(1, N_DEV):
            px, py, pc = _flip(x, (k >> 2) & 1), _flip(y, (k >> 1) & 1), _flip(c, k & 1)
            pdev = 4 * px + 2 * py + pc
            pltpu.make_async_remote_copy(src_ref=sm_ref, dst_ref=sm_all.at[pdev], send_sem=ssem.at[2 * (k - 1)],
                                         recv_sem=rsem.at[2 * (k - 1)], device_id=(px, py, pc),
                                         device_id_type=MESH).wait_recv()
            pltpu.make_async_remote_copy(src_ref=dm_ref.at[chip], dst_ref=dm_all.at[pdev],
                                         send_sem=ssem.at[2 * (k - 1) + 1], recv_sem=rsem.at[2 * (k - 1) + 1],
                                         device_id=(px, py, pc), device_id_type=MESH).wait_recv()
        for cp in sends:
            cp.wait_send()

        tot = sm_all[0]
        for q in range(1, N_DEV):
            tot = tot + sm_all[q]
        sum_ref[...] = tot

        cs = c_ref[...].reshape(N_DEV * 8, d)
        sc = (cs * jax.nn.sigmoid(cs)).astype(BF)
        for n0 in range(0, cols, chunk):
            dmv = dm_all[:, :, n0:n0 + chunk].reshape(N_DEV * 8, chunk).astype(BF)
            gw_ref[:, n0:n0 + chunk] = _tn(sc, dmv)

    return _whole_call(
        body, "small_sync", (smalls, dmod_blk, c_all),
        [_sds(smalls.shape, F32), _sds((d, cols), F32)],
        [pltpu.VMEM((N_DEV,) + smalls.shape, F32), pltpu.VMEM((N_DEV, 8, cols), F32),
         pltpu.SemaphoreType.DMA((2 * (N_DEV - 1),)), pltpu.SemaphoreType.DMA((2 * (N_DEV - 1),))], carry=carry)


def _bucket_onehot():
    maps = np.stack([_bucket_map(dil).reshape(-1) for _, dil in DIL_CONFIGS])
    return (jnp.asarray(maps)[:, None, :] == jnp.arange(N_BUCKETS, dtype=jnp.int32)[None, :, None]).astype(BF)


def _dil_bias(rel_t, onehot):
    def body(r_ref, oh_ref, o_ref):
        rv = r_ref[...]
        hi = rv.astype(BF)
        lo = (rv - hi.astype(F32)).astype(BF)
        for c in range(len(DIL_CONFIGS)):
            o_ref[c] = _nn(hi, oh_ref[c]) + _nn(lo, oh_ref[c])

    return pl.pallas_call(body, name="dil_bias",
                          out_shape=_sds((len(DIL_CONFIGS), N_HEADS, BLOCK * 2 * BLOCK), F32),
                          compiler_params=_cp())(rel_t, onehot)


def _rowsum8(a):
    def body(a_ref, o_ref):
        o_ref[...] = jnp.sum(a_ref[...], axis=0, keepdims=True)

    return pl.pallas_call(body, name="rowsum8", out_shape=_sds((1, a.shape[1]), F32), compiler_params=_cp())(a)


def _local_step(x, mod, target, w, gains, rel_bias, place=None):
    nb, seq, d = x.shape
    t = nb * seq
    dist = place is not None
    core = place[0:1] if dist else None
    x0 = x.reshape(t, d)
    tgt = target.reshape(t, d)
    md = [mod[:, i:i + 1, :] for i in range(N_MOD)]
    sh1, sc1, gt1, sh2, sc2, gt2, sh3, sc3, gt3 = md
    g1, g2, g3 = gains["g_ffn1"], gains["g_mix"], gains["g_ffn2"]
    ones_g = _group_ones()

    def partial_sums(grads, lands):
        return [_add_halves(core, g, l) for g, l in zip(grads, lands)]

    def chip_sums(parts, lands):
        return [_sum_chips(place, p, l, relative=True) for p, l in zip(parts, lands)]

    gu1 = w["gu1"]
    res = _ffn_up(x0, g1, sc1, sh1, gu1, seq,
                  carry=_join([_ag_weights([w["d1"]], relative=True), _ag_weights([w["win"]])]) if dist else None)
    h1, a1, u1, s1 = res[:4]
    wd1, w_in = res[4:] if dist else (w["d1"], w["win"])
    f1, x1 = _ffn_down(s1, wd1, x0, gt1, seq, 0.5)

    res = _qkv_proj(x1, g2, sc2, sh2, w_in, seq, carry=_ag_weights([w["wout"]]) if dist else None)
    h2, qkv6, qkv_r4, qkv_r16 = res[:4]
    w_out2 = (res[4] if dist else w["wout"]).reshape(2 * D_GRP, d)
    qkv6b = qkv6.reshape(6, nb, seq, D_GRP)
    res = _sb_fwd(qkv6b, gains["g_sb_out"], nb, seq,
                  carry=_ag_weights([w["gu2"], w["d2"]], relative=True) if dist else None)
    o_sb, on_sb = res[:2]
    wgu2, wd2 = res[2:] if dist else (w["gu2"], w["d2"])
    onehot = _bucket_onehot()
    bias = _dil_bias(rel_bias.T, onehot).reshape(len(DIL_CONFIGS), N_HEADS * BLOCK, 2 * BLOCK)
    o_cs, l_cs = [], []
    qkv_rs = [(qkv6b, 3), (qkv_r4, 0), (qkv_r16, 0)]
    for ci, (_, dil) in enumerate(DIL_CONFIGS):
        sub = seq // dil
        arr, base = qkv_rs[ci]
        arr = arr.reshape(base + 3, nb, sub, dil * D_GRP)
        qkv_rs[ci] = (arr, base)
        o_c, l_c = _dil_fwd(arr, base, bias[ci], nb, sub, dil)
        o_cs.append(o_c.reshape(t // dil, dil * D_GRP))
        l_cs.append(l_c.reshape(t // dil, dil * D_GRP))
    o_dil, on_dil = _dil_comb(o_cs, l_cs, gains["g_dil_out"])
    tmix, x2 = _mix_out(on_sb.reshape(t, D_GRP), on_dil, w_out2, x1, gt2, seq)

    h3, a3, u3, s3 = _ffn_up(x2, g3, sc3, sh3, wgu2, seq)
    f3, dx3, dg_final, loss = _ffn_down_loss(s3, wd2, x2, gt3, seq, 0.5, gains["g_final"], tgt)

    da3, du3, df3, dgt3, dx2, dsh3, dsc3, dg3 = _ffn_bwd_x(dx3, gt3, f3, wd2, a3, u3, wgu2, x2, g3, sc3, seq, 0.5)
    grads2 = _ffn_bwd_w(h3, da3, du3, s3, df3)

    res = _mix_bwd_out(
        dx2, gt2, tmix, w_out2, o_sb.reshape(t, D_GRP), o_dil, on_sb.reshape(t, D_GRP), on_dil,
        gains["g_sb_out"], gains["g_dil_out"], ones_g, seq, carry=_rs_d2d(grads2) if dist else None)
    do_sb, do_dil, dgt2, dg_sb, dg_dil, dw_out = res[:6]
    parts2 = partial_sums(grads2, res[6:]) if dist else None
    dw_out = dw_out.reshape(N_CHIPS, 1, 2 * D_GRP // N_CHIPS, d)
    res = _sb_bwd(qkv6b, do_sb.reshape(nb, seq, D_GRP), nb, seq,
                  carry=_join([_rs_ici(parts2, relative=True), _rs_d2d([dw_out])]) if dist else None)
    dqkv6 = res[0]
    halves2 = chip_sums(parts2, res[1:2]) if dist else None
    part_wo = partial_sums([dw_out], res[2:3]) if dist else None
    dcs = _dil_comb_bwd(do_dil, o_cs, l_cs)
    dsum, a_tiles = [], []
    for ci, (_, dil) in enumerate(DIL_CONFIGS):
        sub = seq // dil
        do_c = dcs[ci].reshape(nb, sub, dil * D_GRP)
        dd_c = dcs[3 + ci].reshape(nb, sub, dil * D_GRP)
        res = _dil_bwd(qkv_rs[ci][0], qkv_rs[ci][1], bias[ci], do_c, dd_c, nb, sub, dil)
        dsum.append(res[0].reshape(3, t // dil, dil * D_GRP))
        a_tiles.append(res[1].reshape(N_HEADS, BLOCK * 2 * BLOCK))
    dqkv6 = _dqkv_dil_sum(dsum, dqkv6.reshape(6, t, D_GRP))
    drel = _relbias_grad(jnp.stack(a_tiles), onehot)
    dx1, dsh2, dsc2, dg2 = _mix_bwd_dh(dqkv6, w_in, x1, g2, sc2, dx2, seq)

    da1, du1, df1, dgt1 = _ffn_bwd_ds(dx1, gt1, f1, wd1, a1, u1, seq, 0.5)
    g_dn = _ffn_bwd_w(h1, da1, du1, s1, df1, terms=(2,))
    res = _ffn_bwd_w(h1, da1, du1, s1, df1, terms=(0, 1), carry=_rs_d2d(g_dn) if dist else None)
    g_gu = res[:1]
    grads1 = g_gu + g_dn
    parts_dn = partial_sums(g_dn, res[1:2]) if dist else None
    res = _dw_in(h2, dqkv6, carry=_join([_rs_ici(parts_dn, relative=True), _rs_ici(part_wo), _rs_d2d(g_gu),
                                         _rs_final(halves2)]) if dist else None)
    grads_m = [res[0], dw_out]
    if dist:
        sums_dn = chip_sums(parts_dn, res[1:2])
        sum_wo = [_sum_chips(place, p, l) for p, l in zip(part_wo, res[2:3])]
        parts_gu = partial_sums(g_gu, res[3:4])
        grads2 = res[4:5]
    res = _ffn_bwd_dh(da1, du1, gu1, x0, g1, sc1, dx1, seq,
                      carry=_join([_rs_ici(parts_gu, relative=True), _rs_d2d(grads_m[:1])]) if dist else None)
    dx0, dsh1, dsc1, dg1 = res[:4]
    pending = None
    if dist:
        pending = (chip_sums(parts_gu, res[4:5]) + sums_dn + sum_wo, partial_sums(grads_m[:1], res[5:6]))

    dmod = jnp.concatenate([dsh1, dsc1, dgt1, dsh2, dsc2, dgt2, dsh3, dsc3, dgt3], axis=1)
    return dict(grad_x=dx0.reshape(nb, seq, d), loss=loss[0, 0], dmod=dmod.reshape(nb, N_MOD * d),
                dffn1=grads1, dffn2=grads2[0], dwin=grads_m[0], dwout=grads_m[1], pending=pending,
                dg_ffn1=dg1, dg_mix=dg2, dg_ffn2=dg3, dg_final=dg_final, dg_sb=dg_sb, dg_dil=dg_dil,
                drel=drel.T)


_SMALL_ORDER = (("b_ada", N_MOD * 1024), ("g_ffn1", 1024), ("g_mix", 1024), ("g_ffn2", 1024), ("g_final", 1024),
                ("g_sb_out", D_GRP), ("g_dil_out", D_GRP), ("rel_bias", N_BUCKETS * N_HEADS))


def _pack_small(parts, extra=None):
    flat = [parts[name].reshape(-1).astype(F32) for name, _ in _SMALL_ORDER]
    used = sum(sz for _, sz in _SMALL_ORDER)
    pad = SMALL_ROWS * 128 - used
    tail = jnp.zeros((pad,), F32)
    if extra is not None:
        tail = tail.at[0].set(extra)
    return jnp.concatenate(flat + [tail]).reshape(SMALL_ROWS, 128)


def _unpack_small(packed, shapes):
    flat = packed.reshape(-1)
    out, off = {}, 0
    for name, sz in _SMALL_ORDER:
        out[name] = flat[off:off + sz].reshape(shapes[name])
        off += sz
    return out, flat[off]


def kernel(x, c, w_ada, b_ada, g_ffn1, w1_gate, w1_up, w1_down, g_mix, w_in, g_sb_out, g_dil_out, w_out, rel_bias, g_ffn2, w2_gate, w2_up, w2_down, g_final, loss_target, m_w_ada, m_b_ada, m_g_ffn1, m_w1_gate, m_w1_up, m_w1_down, m_g_mix, m_w_in, m_g_sb_out, m_g_dil_out, m_w_out, m_rel_bias, m_g_ffn2, m_w2_gate, m_w2_up, m_w2_down, m_g_final, v_w_ada, v_b_ada, v_g_ffn1, v_w1_gate, v_w1_up, v_w1_down, v_g_mix, v_w_in, v_g_sb_out, v_g_dil_out, v_w_out, v_rel_bias, v_g_ffn2, v_w2_gate, v_w2_up, v_w2_down, v_g_final):
    nb, seq, d = x.shape
    xi, yi, ci = lax.axis_index("x"), lax.axis_index("y"), lax.axis_index("c")
    chip = 2 * xi + yi
    ada_cols = w_ada.shape[-1]

    c_pad = jnp.zeros((8, d), F32).at[:nb].set(c)
    b_shard = lax.dynamic_slice(b_ada, (0, chip * ada_cols), (1, ada_cols))
    shards = dict(gu1=jnp.stack([w1_gate[0], w1_up[0]]), d1=w1_down, win=w_in, wout=w_out,
                  gu2=jnp.stack([w2_gate[0], w2_up[0]]), d2=w2_down)
    bufs = {k: lax.dynamic_update_slice(lax.empty((N_CHIPS,) + s.shape, BF), s.astype(BF)[None],
                                        (chip if k in ("win", "wout") else 0, 0, 0, 0))
            for k, s in shards.items()}
    c_all, mod_blk, bufs["gu1"] = _ada_fwd(c_pad, w_ada[0], b_shard,
                                           carry=_ag_weights([bufs["gu1"]], relative=True))
    mod = jnp.transpose(mod_blk[:, :nb, :], (1, 0, 2)).reshape(nb, N_MOD, d)

    gains = dict(g_ffn1=g_ffn1, g_mix=g_mix, g_ffn2=g_ffn2, g_final=g_final.reshape(1, d),
                 g_sb_out=g_sb_out.reshape(1, D_GRP), g_dil_out=g_dil_out.reshape(1, D_GRP))
    place = jnp.stack([ci, chip]).astype(jnp.int32)
    r = _local_step(x, mod, loss_target, bufs, gains, rel_bias, place)

    dmod = r["dmod"]
    dmod_pad = jnp.zeros((8, N_MOD * d), F32).at[:nb].set(dmod)
    dmod_blk = jnp.transpose(dmod_pad.reshape(8, N_CHIPS, ada_cols), (1, 0, 2))
    small_parts = dict(b_ada=_rowsum8(dmod_pad), g_ffn1=r["dg_ffn1"], g_mix=r["dg_mix"], g_ffn2=r["dg_ffn2"],
                       g_final=r["dg_final"], g_sb_out=r["dg_sb"], g_dil_out=r["dg_dil"], rel_bias=r["drel"])
    halves1, parts_m = r["pending"]
    res = _small_sync(_pack_small(small_parts, r["loss"]), dmod_blk, c_all,
                      carry=_join([_rs_final(halves1), _rs_ici(parts_m)]))
    small_sum, g_wada, gffn1_gu, gffn1_dn, gwout = res[:5]
    halves_m = [_sum_chips(place, p, l) for p, l in zip(parts_m, res[5:6])]
    gwin, = _alone("rs_last", _rs_final(halves_m))
    gffn2 = r["dffn2"]

    small_w = dict(b_ada=b_ada, g_ffn1=g_ffn1, g_mix=g_mix, g_ffn2=g_ffn2, g_final=g_final,
                   g_sb_out=g_sb_out, g_dil_out=g_dil_out, rel_bias=rel_bias)
    small_m = dict(b_ada=m_b_ada, g_ffn1=m_g_ffn1, g_mix=m_g_mix, g_ffn2=m_g_ffn2, g_final=m_g_final,
                   g_sb_out=m_g_sb_out, g_dil_out=m_g_dil_out, rel_bias=m_rel_bias)
    small_v = dict(b_ada=v_b_ada, g_ffn1=v_g_ffn1, g_mix=v_g_mix, g_ffn2=v_g_ffn2, g_final=v_g_final,
                   g_sb_out=v_g_sb_out, g_dil_out=v_g_dil_out, rel_bias=v_rel_bias)
    shapes = {k: v.shape for k, v in small_w.items()}
    sg, sd, sm, sv = _adamw(_pack_small(small_w), small_sum.reshape(1, SMALL_ROWS, 128), 0,
                            _pack_small(small_m), _pack_small(small_v))
    sg, loss = _unpack_small(sg, shapes)
    sd, _ = _unpack_small(sd, shapes)
    sm, _ = _unpack_small(sm, shapes)
    sv, _ = _unpack_small(sv, shapes)

    big = {}

    def upd(name, w, g_arr, sel, m, v, transposed=False):
        swap = (lambda a: jnp.swapaxes(a, -1, -2)) if transposed else (lambda a: a)
        w2, m2, v2 = [swap(a)[0] for a in (w, m, v)]
        big[name] = [swap(a[None]) for a in _adamw(w2, g_arr, sel, m2, v2)]

    upd("w_ada", w_ada, g_wada.reshape(1, d, ada_cols), 0, m_w_ada, v_w_ada)
    upd("w1_gate", w1_gate, gffn1_gu, 0, m_w1_gate, v_w1_gate, transposed=True)
    upd("w1_up", w1_up, gffn1_gu, 1, m_w1_up, v_w1_up, transposed=True)
    upd("w1_down", w1_down, gffn1_dn, 0, m_w1_down, v_w1_down)
    upd("w_in", w_in, gwin, 0, m_w_in, v_w_in)
    upd("w_out", w_out, gwout, 0, m_w_out, v_w_out)
    upd("w2_gate", w2_gate, gffn2, 0, m_w2_gate, v_w2_gate, transposed=True)
    upd("w2_up", w2_up, gffn2, 1, m_w2_up, v_w2_up, transposed=True)
    upd("w2_down", w2_down, gffn2, 2, m_w2_down, v_w2_down)

    names = ["w_ada", "b_ada", "g_ffn1", "w1_gate", "w1_up", "w1_down", "g_mix", "w_in", "g_sb_out", "g_dil_out",
             "w_out", "rel_bias", "g_ffn2", "w2_gate", "w2_up", "w2_down", "g_final"]
    outs = [loss, r["grad_x"]]
    for k, small in enumerate((sg, sd, sm, sv)):
        for name in names:
            outs.append(big[name][k] if name in big else small[name])
    return tuple(outs)
```

```python
import math

import numpy as np
import jax
import jax.numpy as jnp
from jax import lax
from jax.experimental import pallas as pl
from jax.experimental.pallas import tpu as pltpu

F32 = jnp.float32
BF = jnp.bfloat16
MESH = pl.DeviceIdType.MESH

HEAD_DIM = 64
N_HEADS = 8
D_GRP = N_HEADS * HEAD_DIM
DIL_CONFIGS = ((128, 1), (512, 4), (2048, 16))
N_STEPS = 128
BLOCK = 128
N_BUCKETS = 32
MAX_DISTANCE = 2048
N_MOD = 9
EPS = 1e-6
NEG_INF = -1e30
SCALE = HEAD_DIM ** -0.5

ADAM_LR = 0.001
ADAM_B1 = 0.9
ADAM_B2 = 0.999
ADAM_EPS = 1e-08
ADAM_WD = 0.01
ADAM_STEP = 10

N_CHIPS = 4
N_DEV = 8
VMEM_LIMIT = 56 * 1024 * 1024
TM = 512
TQ = 256
KB = 256
SMALL_ROWS = 120


def _cp(n_axes=0, **kw):
    sem = ("arbitrary",) * n_axes if n_axes else None
    return pltpu.CompilerParams(dimension_semantics=sem, vmem_limit_bytes=VMEM_LIMIT, **kw)


def _nn(a, b):
    return jnp.dot(a, b, preferred_element_type=F32)


def _nt(a, b):
    return lax.dot_general(a, b, (((1,), (1,)), ((), ())), preferred_element_type=F32)


def _tn(a, b):
    return lax.dot_general(a, b, (((0,), (0,)), ((), ())), preferred_element_type=F32)


def _twice(m):
    return jnp.concatenate([m, m], axis=0)


def _nn2(x, m2):
    hi = x.astype(BF)
    lo = (x - hi.astype(F32)).astype(BF)
    return _nn(jnp.concatenate([hi, lo], axis=1), m2)


def _softplus(z):
    return jnp.maximum(z, 0.0) + jnp.log(1.0 + jnp.exp(-jnp.abs(z)))


def _sds(shape, dtype):
    return jax.ShapeDtypeStruct(shape, dtype)


def _whole(a):
    nd = a.ndim
    return pl.BlockSpec(a.shape, lambda *_: (0,) * nd, pipeline_mode=pl.Buffered(1))


def _modnorm_bwd_tile(dh, xv, gv, scv, dxo):
    r = lax.rsqrt(jnp.mean(xv * xv, axis=-1, keepdims=True) + EPS)
    n = xv * r
    ng = n * gv
    dsh = jnp.sum(dh, axis=0, keepdims=True)
    dsc = jnp.sum(dh * ng, axis=0, keepdims=True)
    dy = dh * (1.0 + scv)
    dg = jnp.sum(dy * n, axis=0, keepdims=True)
    dn = dy * gv
    dx = dxo + r * (dn - n * jnp.mean(dn * n, axis=-1, keepdims=True))
    return dx, dsh, dsc, dg


def _acc_rows(ref, val, first):
    @pl.when(first)
    def _():
        ref[...] = val

    @pl.when(jnp.logical_not(first))
    def _():
        ref[...] += val


def _modnorm_tile(x_ref, g_ref, sc_ref, sh_ref):
    xv = x_ref[...]
    r = lax.rsqrt(jnp.mean(xv * xv, axis=-1, keepdims=True) + EPS)
    return (((xv * r) * g_ref[...]) * (1.0 + sc_ref[...]) + sh_ref[...]).astype(BF)


def _ffn_up(x, g, sc, sh, wgu, seq, carry=None):
    t, d = x.shape
    fs = wgu.shape[-1]
    per = seq // TM

    def body(x_ref, g_ref, sc_ref, sh_ref, w_ref, h_ref, p_ref, q_ref, s_ref):
        hv = _modnorm_tile(x_ref, g_ref, sc_ref, sh_ref)
        h_ref[...] = hv
        for j in range(N_CHIPS):
            a = _nn(hv, w_ref[j, 0])
            u = _nn(hv, w_ref[j, 1])
            sig = jax.nn.sigmoid(a)
            q = a * sig
            p_ref[j] = (u * (sig * (1.0 + a * (1.0 - sig)))).astype(BF)
            q_ref[j] = q.astype(BF)
            s_ref[j] = (q * u).astype(BF)

    row = pl.BlockSpec((TM, d), lambda m: (m, 0))
    ex = pl.BlockSpec((None, 1, d), lambda m: (m // per, 0, 0))
    blk = pl.BlockSpec((N_CHIPS, TM, fs), lambda m: (0, m, 0))
    return _call(
        body, "ffn_up", (t // TM,),
        [row, pl.BlockSpec((1, d), lambda m: (0, 0)), ex, ex, _whole(wgu)],
        [row, blk, blk, blk],
        [_sds((t, d), BF)] + [_sds((N_CHIPS, t, fs), BF)] * 3,
        (x, g, sc, sh, wgu), carry=carry)


def _ffn_down(s, wd, x, gt, seq, coef, carry=None):
    _, t, fs = s.shape
    d = x.shape[-1]
    per = seq // TM

    def body(s_ref, w_ref, x_ref, gt_ref, f_ref, xo_ref):
        f = _nn(s_ref[0], w_ref[0, 0])
        for j in range(1, N_CHIPS):
            f = f + _nn(s_ref[j], w_ref[j, 0])
        f_ref[...] = f.astype(BF)
        xo_ref[...] = x_ref[...] + (coef * gt_ref[...]) * f

    row = pl.BlockSpec((TM, d), lambda m: (m, 0))
    return _call(
        body, "ffn_down", (t // TM,),
        [pl.BlockSpec((N_CHIPS, TM, fs), lambda m: (0, m, 0)), _whole(wd), row,
         pl.BlockSpec((None, 1, d), lambda m: (m // per, 0, 0))],
        [row, row], [_sds((t, d), BF), _sds((t, d), F32)], (s, wd, x, gt), carry=carry)


def _ffn_bwd_ds(dxo, gt, f, wd, p, q, seq, coef, carry=None):
    t, d = dxo.shape
    fs = p.shape[-1]
    per = seq // TM
    nb = t // seq

    def body(dxo_ref, gt_ref, f_ref, w_ref, p_ref, q_ref, da_ref, du_ref, df_ref, dgt_ref):
        m = pl.program_id(0)
        dxv = dxo_ref[...]
        df = ((coef * gt_ref[...]) * dxv).astype(BF)
        df_ref[...] = df
        _acc_rows(dgt_ref, coef * jnp.sum(dxv * f_ref[...].astype(F32), axis=0, keepdims=True), m % per == 0)
        for j in range(N_CHIPS):
            ds = _nt(df, w_ref[j, 0])
            da_ref[j] = (ds * p_ref[j].astype(F32)).astype(BF)
            du_ref[j] = (ds * q_ref[j].astype(F32)).astype(BF)

    row = pl.BlockSpec((TM, d), lambda m: (m, 0))
    blk = pl.BlockSpec((N_CHIPS, TM, fs), lambda m: (0, m, 0))
    ex = pl.BlockSpec((None, 1, d), lambda m: (m // per, 0, 0))
    return _call(
        body, "ffn_bwd_ds", (t // TM,),
        [row, ex, row, _whole(wd), blk, blk],
        [blk, blk, row, ex],
        [_sds((N_CHIPS, t, fs), BF), _sds((N_CHIPS, t, fs), BF), _sds((t, d), BF), _sds((nb, 1, d), F32)],
        (dxo, gt, f, wd, p, q), carry=carry)


TM_X = 256


def _ffn_bwd_x(dxo, gt, f, wd, p, q, wgu, x, g, sc, seq, coef):
    t, d = dxo.shape
    fs = p.shape[-1]
    per = seq // TM_X
    nb = t // seq

    def body(dxo_ref, gt_ref, f_ref, wd_ref, p_ref, q_ref, w_ref, x_ref, g_ref, sc_ref,
             da_ref, du_ref, df_ref, dgt_ref, dx_ref, dsh_ref, dsc_ref, dg_ref):
        m = pl.program_id(0)
        dxv = dxo_ref[...]
        df = ((coef * gt_ref[...]) * dxv).astype(BF)
        df_ref[...] = df
        _acc_rows(dgt_ref, coef * jnp.sum(dxv * f_ref[...].astype(F32), axis=0, keepdims=True), m % per == 0)
        dh = None
        for j in range(N_CHIPS):
            ds = _nt(df, wd_ref[j, 0])
            da = (ds * p_ref[j].astype(F32)).astype(BF)
            du = (ds * q_ref[j].astype(F32)).astype(BF)
            da_ref[j] = da
            du_ref[j] = du
            part = _nt(da, w_ref[j, 0]) + _nt(du, w_ref[j, 1])
            dh = part if dh is None else dh + part
        dx, dsh, dsc, dg = _modnorm_bwd_tile(dh, x_ref[...], g_ref[...], sc_ref[...], dxv)
        dx_ref[...] = dx
        _acc_rows(dsh_ref, dsh, m % per == 0)
        _acc_rows(dsc_ref, dsc, m % per == 0)
        _acc_rows(dg_ref, dg, m == 0)

    row = pl.BlockSpec((TM_X, d), lambda m: (m, 0))
    blk = pl.BlockSpec((N_CHIPS, TM_X, fs), lambda m: (0, m, 0))
    ex = pl.BlockSpec((None, 1, d), lambda m: (m // per, 0, 0))
    vec = pl.BlockSpec((1, d), lambda m: (0, 0))
    exs = _sds((nb, 1, d), F32)
    return pl.pallas_call(
        body, name="ffn_bwd_x", grid=(t // TM_X,),
        in_specs=[row, ex, row, _whole(wd), blk, blk, _whole(wgu), row, vec, ex],
        out_specs=[blk, blk, row, ex, row, ex, ex, vec],
        out_shape=[_sds((N_CHIPS, t, fs), BF), _sds((N_CHIPS, t, fs), BF), _sds((t, d), BF), exs,
                   _sds((t, d), F32), exs, exs, _sds((1, d), F32)],
        compiler_params=_cp(1))(dxo, gt, f, wd, p, q, wgu, x, g, sc)


TK_W = 1024


def _ffn_bwd_w(h, da, du, s, df, terms=(0, 1, 2), carry=None):
    t, d = h.shape
    fs = da.shape[-1]
    row = pl.BlockSpec((TK_W, d), lambda j, kt: (kt, 0))
    blk = pl.BlockSpec((None, TK_W, fs), lambda j, kt: (j, kt, 0))
    args, specs, idx = [], [], []

    def operand(a, spec):
        for i, o in enumerate(args):
            if o is a:
                return i
        args.append(a)
        specs.append(spec)
        return len(args) - 1

    for term in terms:
        lhs, rhs = ((da, h), (du, h), (s, df))[term]
        idx.append((operand(lhs, blk), operand(rhs, row)))

    def body(*refs):
        o_ref = refs[-1]
        kt = pl.program_id(1)
        parts = [_tn(refs[a][...], refs[b][...]) for a, b in idx]

        @pl.when(kt == 0)
        def _():
            for i, p in enumerate(parts):
                o_ref[i] = p

        @pl.when(kt != 0)
        def _():
            for i, p in enumerate(parts):
                o_ref[i] += p

    return _call(
        body, "ffn_bwd_w", (N_CHIPS, t // TK_W), specs,
        [pl.BlockSpec((None, len(terms), fs, d), lambda j, kt: (j, 0, 0, 0))],
        [_sds((N_CHIPS, len(terms), fs, d), F32)], args, carry=carry)


def _ffn_bwd_dh(da, du, wgu, x, g, sc, dxo, seq, carry=None):
    _, t, fs = da.shape
    d = x.shape[-1]
    per = seq // TM
    nb = t // seq

    def body(da_ref, du_ref, w_ref, x_ref, g_ref, sc_ref, dxo_ref, dx_ref, dsh_ref, dsc_ref, dg_ref):
        m = pl.program_id(0)
        dh = _nt(da_ref[0], w_ref[0, 0]) + _nt(du_ref[0], w_ref[0, 1])
        for j in range(1, N_CHIPS):
            dh = dh + _nt(da_ref[j], w_ref[j, 0]) + _nt(du_ref[j], w_ref[j, 1])
        dx, dsh, dsc, dg = _modnorm_bwd_tile(dh, x_ref[...], g_ref[...], sc_ref[...], dxo_ref[...])
        dx_ref[...] = dx
        _acc_rows(dsh_ref, dsh, m % per == 0)
        _acc_rows(dsc_ref, dsc, m % per == 0)
        _acc_rows(dg_ref, dg, m == 0)

    row = pl.BlockSpec((TM, d), lambda m: (m, 0))
    blk = pl.BlockSpec((N_CHIPS, TM, fs), lambda m: (0, m, 0))
    ex = pl.BlockSpec((None, 1, d), lambda m: (m // per, 0, 0))
    vec = pl.BlockSpec((1, d), lambda m: (0, 0))
    return _call(
        body, "ffn_bwd_dh", (t // TM,),
        [blk, blk, _whole(wgu), row, vec, ex, row],
        [row, ex, ex, vec],
        [_sds((t, d), F32), _sds((nb, 1, d), F32), _sds((nb, 1, d), F32), _sds((1, d), F32)],
        (da, du, wgu, x, g, sc, dxo), carry=carry, io_alias={6: 0})


def _qkv_proj(x, g, sc, sh, w_in, seq, carry=None):
    t, d = x.shape
    wc = w_in.shape[-1]
    per = seq // TM

    dils = [dil for _, dil in DIL_CONFIGS if dil > 1]

    def body(x_ref, g_ref, sc_ref, sh_ref, w_ref, h_ref, o_ref, *rest):
        res_refs, buf = rest[:len(dils)], rest[len(dils)]
        hv = _modnorm_tile(x_ref, g_ref, sc_ref, sh_ref)
        h_ref[...] = hv
        for j in range(N_CHIPS):
            rf = _nn(hv, w_ref[j, 0])
            r = rf.astype(BF)
            for a, lc, off, width in _col_pieces(j, wc):
                o_ref[a, :, lc:lc + width] = r[:, off:off + width]
                if a < 3:
                    continue
                for c0 in range(0, width, 128):
                    cg = (lc + c0) // 128
                    buf[...] = rf[:, off + c0:off + c0 + 128]
                    for ref, dil in zip(res_refs, dils):
                        for rr in range(dil):
                            ref[a - 3, :, rr * D_GRP + cg * 128:rr * D_GRP + (cg + 1) * 128] = (
                                buf[pl.ds(rr, TM // dil, stride=dil), :].astype(BF))

    row = pl.BlockSpec((TM, d), lambda m: (m, 0))
    ex = pl.BlockSpec((None, 1, d), lambda m: (m // per, 0, 0))
    return _call(
        body, "qkv_proj", (t // TM,),
        [row, pl.BlockSpec((1, d), lambda m: (0, 0)), ex, ex, _whole(w_in)],
        [row, pl.BlockSpec((6, TM, D_GRP), lambda m: (0, m, 0))]
        + [pl.BlockSpec((3, TM // dil, dil * D_GRP), lambda m: (0, m, 0)) for dil in dils],
        [_sds((t, d), BF), _sds((6, t, D_GRP), BF)] + [_sds((3, t // dil, dil * D_GRP), BF) for dil in dils],
        (x, g, sc, sh, w_in), scratch=[pltpu.VMEM((TM, 128), F32)], carry=carry)


def _col_pieces(j, wc):
    out, off = [], 0
    while off < wc:
        a, lc = divmod(j * wc + off, D_GRP)
        width = min(D_GRP - lc, wc - off)
        out.append((a, lc, off, width))
        off += width
    return out


def _chip_cols(g6_ref, j, wc):
    return jnp.concatenate([g6_ref[a, :, lc:lc + width] for a, lc, _, width in _col_pieces(j, wc)], axis=1)


def _mix_out(on_sb, on_dil, w_out, x, gt, seq):
    t, d = x.shape
    per = seq // TM

    def body(a_ref, b_ref, w_ref, x_ref, gt_ref, t_ref, xo_ref):
        tv = _nn(a_ref[...], w_ref[0:D_GRP, :]) + _nn(b_ref[...], w_ref[D_GRP:2 * D_GRP, :])
        t_ref[...] = tv.astype(BF)
        xo_ref[...] = x_ref[...] + gt_ref[...] * tv

    row = pl.BlockSpec((TM, d), lambda m: (m, 0))
    half = pl.BlockSpec((TM, D_GRP), lambda m: (m, 0))
    return pl.pallas_call(
        body, name="mix_out", grid=(t // TM,),
        in_specs=[half, half, pl.BlockSpec((2 * D_GRP, d), lambda m: (0, 0)), row,
                  pl.BlockSpec((None, 1, d), lambda m: (m // per, 0, 0))],
        out_specs=[row, row],
        out_shape=[_sds((t, d), BF), _sds((t, d), F32)],
        compiler_params=_cp(1))(on_sb, on_dil, w_out, x, gt)


def _sb_masks():
    lane = lax.broadcasted_iota(jnp.int32, (1, 2 * HEAD_DIM), 1)
    hm0 = lane < HEAD_DIM
    rel = lax.broadcasted_iota(jnp.int32, (TQ, KB), 0) - lax.broadcasted_iota(jnp.int32, (TQ, KB), 1)
    kr = lax.broadcasted_iota(jnp.int32, (KB, KB), 0)
    kc = lax.broadcasted_iota(jnp.int32, (KB, KB), 1)
    return hm0, rel, kr, kc


def _headnorm_pair(o, gv, hm0):
    o2 = o * o
    ms0 = jnp.sum(jnp.where(hm0, o2, 0.0), axis=-1, keepdims=True) * (1.0 / HEAD_DIM)
    ms1 = jnp.sum(jnp.where(hm0, 0.0, o2), axis=-1, keepdims=True) * (1.0 / HEAD_DIM)
    r = jnp.where(hm0, lax.rsqrt(ms0 + EPS), lax.rsqrt(ms1 + EPS))
    return (o * r) * gv


SB_DEAD = -104.0


def _alive(c_l):
    return (jnp.max(c_l) > SB_DEAD).astype(jnp.int32)


def _sb_fwd(qkv6, g_sb, nb, seq, carry=None):
    nq = seq // TQ

    def body(q_ref, k_ref, v_ref, g_ref, o_ref, on_ref):
        qi = pl.program_id(2)
        hm0, rel, kr, kc = _sb_masks()
        upper = _twice((kr > kc).astype(BF))
        heads = _dil_masks()[0]
        qs = _stack_heads(q_ref[...] * SCALE, heads)
        causal2 = jnp.concatenate([rel] * GRP_HEADS, axis=0) > 0

        def block(kj, causal, c_l, acc):
            ks = pl.multiple_of(kj * KB, KB)
            z = _nt(qs, k_ref[pl.ds(ks, KB), :])
            sp = _softplus(z)
            spm = sp if causal is None else jnp.where(causal, sp, 0.0)
            suf = _nn2(spm, upper)
            w = jnp.exp((z - sp) + (c_l - suf))
            if causal is not None:
                w = jnp.where(causal, w, 0.0)
            return c_l - (suf[:, 0:1] + spm[:, 0:1]), acc + _nn(w.astype(BF), v_ref[pl.ds(ks, KB), :])

        c_l, acc = block(qi, causal2, jnp.zeros((GRP_HEADS * TQ, 1), F32), jnp.zeros((GRP_HEADS * TQ, GRP_W), F32))

        def cond(carry):
            return jnp.logical_and(carry[0] <= qi, carry[1] > 0)

        def kbody(carry):
            it, _, c_l, acc = carry
            c_l, acc = block(qi - it, None, c_l, acc)
            return it + 1, _alive(c_l), c_l, acc

        acc = lax.while_loop(cond, kbody, (jnp.int32(1), _alive(c_l), c_l, acc))[3]
        o = _unstack_heads(acc, heads, TQ)
        o_ref[...] = o.astype(BF)
        gv = g_ref[...]
        for half in range(GRP_W // 128):
            lanes = slice(half * 128, (half + 1) * 128)
            on_ref[:, lanes] = _headnorm_pair(o[:, lanes], gv[:, lanes], hm0).astype(BF)

    w = GRP_W
    full = lambda i: pl.BlockSpec((None, None, seq, w), lambda b, hp, q: (i, b, 0, hp))
    qblk = pl.BlockSpec((None, None, TQ, w), lambda b, hp, q: (0, b, q, hp))
    oblk = pl.BlockSpec((None, TQ, w), lambda b, hp, q: (b, q, hp))
    return _call(
        body, "sb_fwd", (nb, N_HEADS // GRP_HEADS, nq),
        [qblk, full(1), full(2), pl.BlockSpec((1, w), lambda b, hp, q: (0, hp))],
        [oblk, oblk],
        [_sds((nb, seq, D_GRP), BF), _sds((nb, seq, D_GRP), BF)],
        (qkv6, qkv6, qkv6, g_sb), carry=carry)


def _sb_bwd(qkv6, do, nb, seq, carry=None):
    nq = seq // TQ
    nk = seq // KB

    def body(q_ref, k_ref, v_ref, do_ref, out_ref, dk_acc, dv_acc, g_st, s_st):
        qi = pl.program_id(2)
        hm0, rel, kr, kc = _sb_masks()
        upper = _twice((kr > kc).astype(BF))
        lower = (kr < kc).astype(BF)

        @pl.when(qi == 0)
        def _():
            dk_acc[...] = jnp.zeros_like(dk_acc)
            dv_acc[...] = jnp.zeros_like(dv_acc)

        heads = _dil_masks()[0]
        qs = _stack_heads(q_ref[...] * SCALE, heads)
        dos = _stack_heads(do_ref[...], heads)
        causal2 = jnp.concatenate([rel] * GRP_HEADS, axis=0) > 0

        def weights(kj, causal, c_l):
            ks = pl.multiple_of(kj * KB, KB)
            vb = v_ref[pl.ds(ks, KB), :]
            z = _nt(qs, k_ref[pl.ds(ks, KB), :])
            sp = _softplus(z)
            spm = sp if causal is None else jnp.where(causal, sp, 0.0)
            suf = _nn2(spm, upper)
            lsz = z - sp
            w = jnp.exp(lsz + (c_l - suf))
            if causal is not None:
                w = jnp.where(causal, w, 0.0)
            g_st[kj] = w * _nt(dos, vb)
            s_st[kj] = jnp.exp(lsz)
            dv_acc[pl.ds(ks, KB), :] += _tn(w.astype(BF), dos)
            return c_l - (suf[:, 0:1] + spm[:, 0:1])

        zc = jnp.zeros((GRP_HEADS * TQ, 1), F32)
        c_l = weights(qi, causal2, zc)

        def acond(carry):
            return jnp.logical_and(carry[0] <= qi, carry[1] > 0)

        def abody(carry):
            c_l = weights(qi - carry[0], None, carry[2])
            return carry[0] + 1, _alive(c_l), c_l

        n_used = lax.while_loop(acond, abody, (jnp.int32(1), _alive(c_l), c_l))[0]

        def grads(kj, causal, c_g, dq):
            ks = pl.multiple_of(kj * KB, KB)
            kb = k_ref[pl.ds(ks, KB), :]
            g = g_st[kj]
            sig = s_st[kj]
            pre = _nn(g.astype(BF), lower)
            dz = g * (1.0 - sig) - sig * (pre + c_g)
            if causal is not None:
                dz = jnp.where(causal, dz, 0.0)
            dzb = dz.astype(BF)
            dk_acc[pl.ds(ks, KB), :] += _tn(dzb, qs)
            return c_g + (pre[:, KB - 1:KB] + g[:, KB - 1:KB]), dq + _nn(dzb, kb)

        c_g, dq = lax.fori_loop(qi - n_used + 1, qi, lambda kj, cr: grads(kj, None, *cr),
                                (zc, jnp.zeros((GRP_HEADS * TQ, GRP_W), F32)))
        _, dq = grads(qi, causal2, c_g, dq)
        dq = _unstack_heads(dq, heads, TQ) * SCALE
        out_ref[0, pl.ds(pl.multiple_of(qi * TQ, TQ), TQ), :] = dq.astype(BF)

        @pl.when(qi == nq - 1)
        def _():
            out_ref[1] = dk_acc[...].astype(BF)
            out_ref[2] = dv_acc[...].astype(BF)

    w = GRP_W
    full = lambda i: pl.BlockSpec((None, None, seq, w), lambda b, hp, q: (i, b, 0, hp))
    qblk = pl.BlockSpec((None, None, TQ, w), lambda b, hp, q: (0, b, q, hp))
    oblk = pl.BlockSpec((None, TQ, w), lambda b, hp, q: (b, q, hp))
    return _call(
        body, "sb_bwd", (nb, N_HEADS // GRP_HEADS, nq),
        [qblk, full(1), full(2), oblk],
        [pl.BlockSpec((3, None, seq, w), lambda b, hp, q: (0, b, 0, hp))],
        [_sds((6, nb, seq, D_GRP), BF)], (qkv6, qkv6, qkv6, do),
        scratch=[pltpu.VMEM((seq, w), F32), pltpu.VMEM((seq, w), F32),
                 pltpu.VMEM((nk, GRP_HEADS * TQ, KB), F32), pltpu.VMEM((nk, GRP_HEADS * TQ, KB), F32)],
        carry=carry)


def _t5_bucket(n):
    max_exact = N_BUCKETS // 2
    nf = np.maximum(n, 1).astype(np.float32)
    large = max_exact + (np.log(nf / max_exact) / math.log(MAX_DISTANCE / max_exact)
                         * (N_BUCKETS - max_exact)).astype(np.int32)
    large = np.minimum(large, N_BUCKETS - 1)
    return np.where(n < max_exact, n, large).astype(np.int32)


def _bucket_map(dilation):
    step = BLOCK + np.arange(BLOCK)[:, None] - np.arange(2 * BLOCK)[None, :]
    return _t5_bucket(np.clip(step, 0, N_STEPS) * dilation)


GRP_HEADS = 4
GRP_W = GRP_HEADS * HEAD_DIM


def _dil_masks():
    lane = lax.broadcasted_iota(jnp.int32, (1, GRP_W), 1)
    heads = [jnp.logical_and(lane >= HEAD_DIM * i, lane < HEAD_DIM * (i + 1)) for i in range(GRP_HEADS)]
    iq = jnp.bitwise_and(lax.broadcasted_iota(jnp.int32, (GRP_HEADS * BLOCK, BLOCK), 0), BLOCK - 1)
    ik = lax.broadcasted_iota(jnp.int32, (GRP_HEADS * BLOCK, BLOCK), 1)
    return heads, ik <= iq, ik >= iq


def _stack_heads(x, heads):
    zero = jnp.zeros_like(x)
    return jnp.concatenate([jnp.where(hm, x, zero) for hm in heads], axis=0)


def _unstack_heads(xs, heads, rows=BLOCK):
    out = xs[0:rows]
    for i in range(1, GRP_HEADS):
        out = jnp.where(heads[i], xs[i * rows:(i + 1) * rows], out)
    return out


def _dil_rows(n):
    rs = pl.multiple_of(n * BLOCK, BLOCK)
    ps = pl.multiple_of(jnp.maximum(n - 1, 0) * BLOCK, BLOCK)
    return pl.ds(rs, BLOCK), pl.ds(ps, BLOCK)


def _dil_probs(qs, kc, kp, b_ref, gi, valid_c, valid_p):
    rows = slice(gi * GRP_HEADS * BLOCK, (gi + 1) * GRP_HEADS * BLOCK)
    zc = _nt(qs, kc) * SCALE + b_ref[rows, BLOCK:2 * BLOCK]
    zp = _nt(qs, kp) * SCALE + b_ref[rows, 0:BLOCK]
    zc = jnp.where(valid_c, zc, NEG_INF)
    zp = jnp.where(valid_p, zp, NEG_INF)
    m = jnp.maximum(jnp.max(zc, axis=-1, keepdims=True), jnp.max(zp, axis=-1, keepdims=True))
    ec = jnp.exp(zc - m)
    ep = jnp.exp(zp - m)
    den = jnp.sum(ec, axis=-1, keepdims=True) + jnp.sum(ep, axis=-1, keepdims=True)
    return ec, ep, den, m


def _dil_fwd(qkv6r, base, bias, nb, sub_len, dilation):
    n_blk = sub_len // BLOCK

    def body(q_ref, k_ref, v_ref, b_ref, o_ref, l_ref):
        heads, valid_c, valid_p0 = _dil_masks()

        def nbody(n, carry):
            cur, prev = _dil_rows(n)
            valid_p = jnp.logical_and(valid_p0, n > 0)
            for gi in range(N_HEADS // GRP_HEADS):
                lanes = slice(gi * GRP_W, (gi + 1) * GRP_W)
                qs = _stack_heads(q_ref[cur, lanes], heads)
                ec, ep, den, m = _dil_probs(qs, k_ref[cur, lanes], k_ref[prev, lanes], b_ref, gi, valid_c, valid_p)
                o = (_nn(ec.astype(BF), v_ref[cur, lanes]) + _nn(ep.astype(BF), v_ref[prev, lanes])) / den
                o_ref[cur, lanes] = _unstack_heads(o, heads).astype(BF)
                l_ref[cur, lanes] = _unstack_heads(jnp.broadcast_to(m + jnp.log(den), o.shape), heads)
            return carry

        lax.fori_loop(0, n_blk, nbody, 0)

    seqblk = lambda i: pl.BlockSpec((None, None, sub_len, D_GRP), lambda b, r: (i, b, 0, r))
    oblk = pl.BlockSpec((None, sub_len, D_GRP), lambda b, r: (b, 0, r))
    shp = _sds((nb, sub_len, dilation * D_GRP), F32)
    return pl.pallas_call(
        body, name="dil_fwd_%d" % dilation, grid=(nb, dilation),
        in_specs=[seqblk(base), seqblk(base + 1), seqblk(base + 2), _whole(bias)],
        out_specs=[oblk, oblk], out_shape=[_sds(shp.shape, BF), shp],
        compiler_params=_cp(2))(qkv6r, qkv6r, qkv6r, bias)


def _dil_bwd(qkv6r, base, bias, do_c, dd_c, nb, sub_len, dilation, carry=None):
    n_blk = sub_len // BLOCK

    def body(q_ref, k_ref, v_ref, b_ref, do_ref, dd_ref, out_ref, a_ref, dk_acc, dv_acc):
        heads, valid_c, valid_p0 = _dil_masks()
        first = jnp.logical_and(pl.program_id(0) == 0, pl.program_id(1) == 0)

        @pl.when(first)
        def _():
            a_ref[...] = jnp.zeros_like(a_ref)

        dk_acc[...] = jnp.zeros_like(dk_acc)
        dv_acc[...] = jnp.zeros_like(dv_acc)

        def nbody(n, carry):
            cur, prev = _dil_rows(n)
            valid_p = jnp.logical_and(valid_p0, n > 0)
            for gi in range(N_HEADS // GRP_HEADS):
                lanes = slice(gi * GRP_W, (gi + 1) * GRP_W)
                kc, kp = k_ref[cur, lanes], k_ref[prev, lanes]
                vc, vp = v_ref[cur, lanes], v_ref[prev, lanes]
                qs = _stack_heads(q_ref[cur, lanes], heads)
                dos = _stack_heads(do_ref[cur, lanes], heads).astype(BF)
                dds = jnp.sum(_stack_heads(dd_ref[cur, lanes], heads), axis=-1, keepdims=True) * (1.0 / HEAD_DIM)
                ec, ep, den, _ = _dil_probs(qs, kc, kp, b_ref, gi, valid_c, valid_p)
                inv = 1.0 / den
                pc = ec * inv
                pp = ep * inv
                dzc = pc * (_nt(dos, vc) + dds)
                dzp = pp * (_nt(dos, vp) + dds)
                rows = slice(gi * GRP_HEADS * BLOCK, (gi + 1) * GRP_HEADS * BLOCK)
                a_ref[rows, BLOCK:2 * BLOCK] += dzc
                a_ref[rows, 0:BLOCK] += dzp
                dzcb = (dzc * SCALE).astype(BF)
                dzpb = (dzp * SCALE).astype(BF)
                out_ref[0, cur, lanes] = _unstack_heads(_nn(dzcb, kc) + _nn(dzpb, kp), heads).astype(BF)
                dk_acc[cur, lanes] += _tn(dzcb, qs)
                dk_acc[prev, lanes] += _tn(dzpb, qs)
                dv_acc[cur, lanes] += _tn(pc.astype(BF), dos)
                dv_acc[prev, lanes] += _tn(pp.astype(BF), dos)
            return carry

        lax.fori_loop(0, n_blk, nbody, 0)
        out_ref[1] = dk_acc[...].astype(BF)
        out_ref[2] = dv_acc[...].astype(BF)

    seqblk = lambda i: pl.BlockSpec((None, None, sub_len, D_GRP), lambda b, r: (i, b, 0, r))
    oblk = pl.BlockSpec((None, sub_len, D_GRP), lambda b, r: (b, 0, r))
    return _call(
        body, "dil_bwd_%d" % dilation, (nb, dilation),
        [seqblk(base), seqblk(base + 1), seqblk(base + 2), _whole(bias), oblk, oblk],
        [pl.BlockSpec((3, None, sub_len, D_GRP), lambda b, r: (0, b, 0, r)),
         pl.BlockSpec((N_HEADS * BLOCK, 2 * BLOCK), lambda b, r: (0, 0))],
        [_sds((3, nb, sub_len, dilation * D_GRP), BF), _sds((N_HEADS * BLOCK, 2 * BLOCK), F32)],
        (qkv6r, qkv6r, qkv6r, bias, do_c, dd_c),
        scratch=[pltpu.VMEM((sub_len, D_GRP), F32)] * 2, carry=carry)


def _group_ones():
    idx = np.arange(D_GRP) // HEAD_DIM
    return jnp.asarray((idx[:, None] == idx[None, :]).astype(np.float32), dtype=BF)


def _dil_alphas(l1, l4, l16):
    mx = jnp.maximum(jnp.maximum(l1, l4), l16)
    e1 = jnp.exp(l1 - mx)
    e4 = jnp.exp(l4 - mx)
    e16 = jnp.exp(l16 - mx)
    den = e1 + e4 + e16
    return e1 / den, e4 / den, e16 / den


def _residue_spec(dil):
    return pl.BlockSpec((TM // dil, dil * D_GRP), lambda m: (m, 0))


def _from_residue(src, dil, cg, buf):
    if dil == 1:
        return src[:, cg * 128:(cg + 1) * 128].astype(F32)
    for r in range(dil):
        buf[pl.ds(r, TM // dil, stride=dil), :] = (
            src[:, r * D_GRP + cg * 128:r * D_GRP + (cg + 1) * 128].astype(F32))
    return buf[...]


def _to_residue(dst, dil, cg, buf, val):
    if dil == 1:
        dst[:, cg * 128:(cg + 1) * 128] = val.astype(dst.dtype)
        return
    buf[...] = val
    for r in range(dil):
        dst[:, r * D_GRP + cg * 128:r * D_GRP + (cg + 1) * 128] = (
            buf[pl.ds(r, TM // dil, stride=dil), :].astype(dst.dtype))


def _pair_sum(x, hm0):
    s0 = jnp.sum(jnp.where(hm0, x, 0.0), axis=-1, keepdims=True)
    s1 = jnp.sum(jnp.where(hm0, 0.0, x), axis=-1, keepdims=True)
    return jnp.where(hm0, s0, s1)


def _dil_comb(os, ls, g_dil):
    t = os[0].shape[0]
    dils = [dil for _, dil in DIL_CONFIGS]

    def body(o1, l1, o4, l4, o16, l16, g_ref, o_ref, on_ref, b0, b1, b2, b3):
        hm0 = lax.broadcasted_iota(jnp.int32, (1, 128), 1) < HEAD_DIM
        for cg in range(D_GRP // 128):
            lanes = slice(cg * 128, (cg + 1) * 128)
            ov = [_from_residue(src, dil, cg, buf) for src, dil, buf in zip((o1, o4, o16), dils, (None, b0, b1))]
            lv = [_from_residue(src, dil, cg, buf) for src, dil, buf in zip((l1, l4, l16), dils, (None, b2, b3))]
            a1, a4, a16 = _dil_alphas(*lv)
            o = a1 * ov[0] + a4 * ov[1] + a16 * ov[2]
            o_ref[:, lanes] = o.astype(BF)
            on_ref[:, lanes] = _headnorm_pair(o, g_ref[:, lanes], hm0).astype(BF)

    blk = pl.BlockSpec((TM, D_GRP), lambda m: (m, 0))
    specs = [_residue_spec(dil) for dil in dils for _ in range(2)]
    return pl.pallas_call(
        body, name="dil_comb", grid=(t // TM,),
        in_specs=specs + [pl.BlockSpec((1, D_GRP), lambda m: (0, 0))],
        out_specs=[blk, blk],
        out_shape=[_sds((t, D_GRP), BF), _sds((t, D_GRP), BF)],
        scratch_shapes=[pltpu.VMEM((TM, 128), F32)] * 4,
        compiler_params=_cp(1))(os[0], ls[0], os[1], ls[1], os[2], ls[2], g_dil)


def _dil_comb_bwd(do, os, ls):
    t = do.shape[0]
    dils = [dil for _, dil in DIL_CONFIGS]

    def body(do_ref, o1, l1, o4, l4, o16, l16, d1, d4, d16, e1, e4, e16, b0, b1, b2, b3):
        hm0 = lax.broadcasted_iota(jnp.int32, (1, 128), 1) < HEAD_DIM
        for cg in range(D_GRP // 128):
            dov = do_ref[:, cg * 128:(cg + 1) * 128].astype(F32)
            ov = [_from_residue(src, dil, cg, buf) for src, dil, buf in zip((o1, o4, o16), dils, (None, b0, b1))]
            lv = [_from_residue(src, dil, cg, buf) for src, dil, buf in zip((l1, l4, l16), dils, (None, b2, b3))]
            al = _dil_alphas(*lv)
            sbar = al[0] * _pair_sum(dov * ov[0], hm0)
            for a_c, o_c in zip(al[1:], ov[1:]):
                sbar = sbar + a_c * _pair_sum(dov * o_c, hm0)
            for a_c, dil, dref, eref in zip(al, dils, (d1, d4, d16), (e1, e4, e16)):
                _to_residue(dref, dil, cg, b0, a_c * dov)
                _to_residue(eref, dil, cg, b1, -a_c * sbar)

    specs = [_residue_spec(dil) for dil in dils]
    return pl.pallas_call(
        body, name="dil_comb_bwd", grid=(t // TM,),
        in_specs=[pl.BlockSpec((TM, D_GRP), lambda m: (m, 0))] + [sp for sp in specs for _ in range(2)],
        out_specs=specs + specs,
        out_shape=[_sds((t // dil, dil * D_GRP), BF) for dil in dils]
        + [_sds((t // dil, dil * D_GRP), F32) for dil in dils],
        scratch_shapes=[pltpu.VMEM((TM, 128), F32)] * 4,
        compiler_params=_cp(1))(do, os[0], ls[0], os[1], ls[1], os[2], ls[2])


def _dqkv_dil_sum(ds, dqkv6):
    t = dqkv6.shape[1]
    dils = [dil for _, dil in DIL_CONFIGS]

    def body(*refs):
        srcs, o_ref, acc = refs[:len(dils)], refs[len(dils) + 1], refs[len(dils) + 2]
        for a in range(3):
            for cg in range(D_GRP // 128):
                for src, dil in zip(srcs, dils):
                    for r in range(dil):
                        part = src[a, :, r * D_GRP + cg * 128:r * D_GRP + (cg + 1) * 128].astype(F32)
                        rows = pl.ds(r, TM // dil, stride=dil) if dil > 1 else slice(None)
                        if dil == dils[0]:
                            acc[rows, :] = part
                        else:
                            acc[rows, :] += part
                o_ref[a, :, cg * 128:(cg + 1) * 128] = acc[...].astype(BF)

    return pl.pallas_call(
        body, name="dqkv_dil_sum", grid=(t // TM,),
        in_specs=[pl.BlockSpec((3, TM // dil, dil * D_GRP), lambda m: (0, m, 0)) for dil in dils]
        + [pl.BlockSpec(memory_space=pl.ANY)],
        out_specs=pl.BlockSpec((3, TM, D_GRP), lambda m: (1, m, 0)),
        out_shape=_sds((6, t, D_GRP), BF), input_output_aliases={len(dils): 0},
        scratch_shapes=[pltpu.VMEM((TM, 128), F32)],
        compiler_params=_cp(1))(*ds, dqkv6)


def _relbias_grad(a_all, onehot):
    def body(a_ref, oh_ref, o_ref):
        acc = jnp.zeros((N_HEADS, N_BUCKETS), F32)
        for c in range(len(DIL_CONFIGS)):
            av = a_ref[c]
            hi = av.astype(BF)
            lo = (av - hi.astype(F32)).astype(BF)
            acc = acc + _nt(hi, oh_ref[c]) + _nt(lo, oh_ref[c])
        o_ref[...] = acc

    return pl.pallas_call(body, name="relbias_grad", out_shape=_sds((N_HEADS, N_BUCKETS), F32),
                          compiler_params=_cp())(a_all, onehot)


def _headnorm_bwd(dn, o, gv, mv):
    ms = _nn2(o * o, mv) * (1.0 / HEAD_DIM)
    r = lax.rsqrt(ms + EPS)
    nrm = o * r
    dg = jnp.sum(dn * nrm, axis=0, keepdims=True)
    dnn = dn * gv
    do = r * (dnn - nrm * (_nn2(dnn * nrm, mv) * (1.0 / HEAD_DIM)))
    return do, dg


def _mix_bwd_out(dx, gt, tv, w_out, o_sb, o_dil, on_sb, on_dil, g_sb, g_dil, ones_g, seq, carry=None):
    t, d = dx.shape
    per = seq // TM
    nb = t // seq

    def body(dx_ref, gt_ref, t_ref, w_ref, osb, odl, onsb, ondl, gsb, gdl, m_ref,
             dosb, dodl, dgt_ref, dgsb, dgdl, dw_ref):
        m = pl.program_id(0)
        dxv = dx_ref[...]
        dt = (gt_ref[...] * dxv).astype(BF)
        _acc_rows(dgt_ref, jnp.sum(dxv * t_ref[...].astype(F32), axis=0, keepdims=True), m % per == 0)
        mv = _twice(m_ref[...])
        don_sb = _nt(dt, w_ref[0:D_GRP, :])
        don_dl = _nt(dt, w_ref[D_GRP:2 * D_GRP, :])
        do1, dg1 = _headnorm_bwd(don_sb, osb[...].astype(F32), gsb[...], mv)
        do2, dg2 = _headnorm_bwd(don_dl, odl[...].astype(F32), gdl[...], mv)
        dosb[...] = do1.astype(BF)
        dodl[...] = do2.astype(BF)
        _acc_rows(dgsb, dg1, m == 0)
        _acc_rows(dgdl, dg2, m == 0)
        p1 = _tn(onsb[...], dt)
        p2 = _tn(ondl[...], dt)

        @pl.when(m == 0)
        def _():
            dw_ref[0:D_GRP, :] = p1
            dw_ref[D_GRP:2 * D_GRP, :] = p2

        @pl.when(m != 0)
        def _():
            dw_ref[0:D_GRP, :] += p1
            dw_ref[D_GRP:2 * D_GRP, :] += p2

    row = pl.BlockSpec((TM, d), lambda m: (m, 0))
    half = pl.BlockSpec((TM, D_GRP), lambda m: (m, 0))
    ex = pl.BlockSpec((None, 1, d), lambda m: (m // per, 0, 0))
    gvec = pl.BlockSpec((1, D_GRP), lambda m: (0, 0))
    wblk = pl.BlockSpec((2 * D_GRP, d), lambda m: (0, 0))
    return _call(
        body, "mix_bwd_out", (t // TM,),
        [row, ex, row, wblk, half, half, half, half, gvec, gvec, pl.BlockSpec((D_GRP, D_GRP), lambda m: (0, 0))],
        [half, half, ex, gvec, gvec, wblk],
        [_sds((t, D_GRP), BF), _sds((t, D_GRP), BF), _sds((nb, 1, d), F32),
         _sds((1, D_GRP), F32), _sds((1, D_GRP), F32), _sds((2 * D_GRP, d), F32)],
        (dx, gt, tv, w_out, o_sb, o_dil, on_sb, on_dil, g_sb, g_dil, ones_g), carry=carry)


def _dw_in(h, dqkv6, carry=None):
    t, d = h.shape
    wc = 6 * D_GRP // N_CHIPS

    def body(h_ref, g_ref, o_ref):
        kt = pl.program_id(0)
        hv = h_ref[...]
        for j in range(N_CHIPS):
            p = _tn(hv, _chip_cols(g_ref, j, wc))

            @pl.when(kt == 0)
            def _(p=p, j=j):
                o_ref[j, 0] = p

            @pl.when(kt != 0)
            def _(p=p, j=j):
                o_ref[j, 0] += p

    return _call(
        body, "dw_in", (t // TK_W,),
        [pl.BlockSpec((TK_W, d), lambda kt: (kt, 0)), pl.BlockSpec((6, TK_W, D_GRP), lambda kt: (0, kt, 0))],
        [pl.BlockSpec((N_CHIPS, 1, d, wc), lambda kt: (0, 0, 0, 0))],
        [_sds((N_CHIPS, 1, d, wc), F32)], (h, dqkv6), carry=carry)


def _mix_bwd_dh(dqkv6, w_in, x, g, sc, dxo, seq, carry=None):
    _, t, _ = dqkv6.shape
    d = x.shape[-1]
    wc = w_in.shape[-1]
    per = seq // TM
    nb = t // seq

    def body(g6_ref, w_ref, x_ref, g_ref, sc_ref, dxo_ref, dx_ref, dsh_ref, dsc_ref, dg_ref):
        m = pl.program_id(0)
        dh = _nt(_chip_cols(g6_ref, 0, wc), w_ref[0, 0])
        for j in range(1, N_CHIPS):
            dh = dh + _nt(_chip_cols(g6_ref, j, wc), w_ref[j, 0])
        dx, dsh, dsc, dg = _modnorm_bwd_tile(dh, x_ref[...], g_ref[...], sc_ref[...], dxo_ref[...])
        dx_ref[...] = dx
        _acc_rows(dsh_ref, dsh, m % per == 0)
        _acc_rows(dsc_ref, dsc, m % per == 0)
        _acc_rows(dg_ref, dg, m == 0)

    row = pl.BlockSpec((TM, d), lambda m: (m, 0))
    ex = pl.BlockSpec((None, 1, d), lambda m: (m // per, 0, 0))
    vec = pl.BlockSpec((1, d), lambda m: (0, 0))
    return _call(
        body, "mix_bwd_dh", (t // TM,),
        [pl.BlockSpec((6, TM, D_GRP), lambda m: (0, m, 0)), _whole(w_in), row, vec, ex, row],
        [row, ex, ex, vec],
        [_sds((t, d), F32), _sds((nb, 1, d), F32), _sds((nb, 1, d), F32), _sds((1, d), F32)],
        (dqkv6, w_in, x, g, sc, dxo), carry=carry)


def _ffn_down_loss(s, wd, x, gt, seq, coef, g, target):
    _, t, fs = s.shape
    d = x.shape[-1]
    per = seq // TM
    steps = t // TM

    def body(s_ref, w_ref, x_ref, gt_ref, g_ref, t_ref, f_ref, dx_ref, dg_ref, loss_ref, lacc):
        m = pl.program_id(0)
        f = _nn(s_ref[0], w_ref[0, 0])
        for j in range(1, N_CHIPS):
            f = f + _nn(s_ref[j], w_ref[j, 0])
        f_ref[...] = f.astype(BF)
        xv = x_ref[...] + (coef * gt_ref[...]) * f
        gv = g_ref[...]
        r = lax.rsqrt(jnp.mean(xv * xv, axis=-1, keepdims=True) + EPS)
        n = xv * r
        err = n * gv - t_ref[...]
        dy = err * (1.0 / d)
        _acc_rows(dg_ref, jnp.sum(dy * n, axis=0, keepdims=True), m == 0)
        dn = dy * gv
        dx_ref[...] = r * (dn - n * jnp.mean(dn * n, axis=-1, keepdims=True))
        _acc_rows(lacc, jnp.sum(err * err, axis=0, keepdims=True), m == 0)

        @pl.when(m == steps - 1)
        def _():
            tot = jnp.sum(lacc[...], axis=-1, keepdims=True) * (0.5 / d)
            loss_ref[...] = jnp.broadcast_to(tot, (1, 128))

    row = pl.BlockSpec((TM, d), lambda m: (m, 0))
    vec = pl.BlockSpec((1, d), lambda m: (0, 0))
    return pl.pallas_call(
        body, name="ffn_down_loss", grid=(steps,),
        in_specs=[pl.BlockSpec((N_CHIPS, TM, fs), lambda m: (0, m, 0)), _whole(wd), row,
                  pl.BlockSpec((None, 1, d), lambda m: (m // per, 0, 0)), vec, row],
        out_specs=[row, row, vec, pl.BlockSpec((1, 128), lambda m: (0, 0))],
        out_shape=[_sds((t, d), BF), _sds((t, d), F32), _sds((1, d), F32), _sds((1, 128), F32)],
        scratch_shapes=[pltpu.VMEM((1, d), F32)],
        compiler_params=_cp(1))(s, wd, x, gt, g, target)


def _row_tile(rows, cols):
    best = rows
    for tr in range(8, rows + 1, 8):
        if rows % tr == 0 and tr * cols * 4 <= (1 << 20):
            best = tr
    if best * cols * 4 > (1 << 21):
        best = 8
    return best


def _adamw(w, g_arr, g_sel, m, v):
    rows, cols = w.shape
    tr = _row_tile(rows, cols)
    b1c = 1.0 - ADAM_B1 ** ADAM_STEP
    b2c = 1.0 - ADAM_B2 ** ADAM_STEP

    def body(w_ref, g_ref, m_ref, v_ref, go_ref, d_ref, mo_ref, vo_ref):
        gv = g_ref[...]
        mn = ADAM_B1 * m_ref[...] + (1.0 - ADAM_B1) * gv
        vn = ADAM_B2 * v_ref[...] + (1.0 - ADAM_B2) * (gv * gv)
        go_ref[...] = gv
        mo_ref[...] = mn
        vo_ref[...] = vn
        d_ref[...] = -ADAM_LR * ((mn / b1c) / (jnp.sqrt(vn / b2c) + ADAM_EPS) + ADAM_WD * w_ref[...])

    blk = pl.BlockSpec((tr, cols), lambda i: (i, 0))
    shp = _sds((rows, cols), F32)
    return pl.pallas_call(
        body, name="adamw", grid=(rows // tr,),
        in_specs=[blk, pl.BlockSpec((None, tr, cols), lambda i: (g_sel, i, 0)), blk, blk],
        out_specs=[blk] * 4, out_shape=[shp] * 4,
        compiler_params=_cp(1))(w, g_arr, m, v)


def _flip(v, bit):
    return 1 - v if bit else v


def _my_place():
    x, y, c = lax.axis_index("x"), lax.axis_index("y"), lax.axis_index("c")
    return x, y, c


class _Exchange:
    def __init__(self, operands, out_shape, aliases, sems, start, finish):
        self.operands, self.out_shape, self.aliases, self.sems = list(operands), list(out_shape), dict(aliases), list(sems)
        self.start, self.finish = start, finish


def _join(exchanges):
    exchanges = [e for e in exchanges if e is not None]
    if not exchanges:
        return None
    ops, outs, sems, aliases, spans = [], [], [], {}, []
    for e in exchanges:
        spans.append((len(ops), len(outs), len(sems), e))
        for i, j in e.aliases.items():
            aliases[len(ops) + i] = len(outs) + j
        ops += e.operands
        outs += e.out_shape
        sems += e.sems

    def run(which):
        def go(ins, res, sm):
            for io, oo, so, e in spans:
                getattr(e, which)(ins[io:io + len(e.operands)], res[oo:oo + len(e.out_shape)], sm[so:so + len(e.sems)])
        return go

    return _Exchange(ops, outs, aliases, sems, run("start"), run("finish"))


def _call(body, name, grid, in_specs, out_specs, out_shape, args, scratch=(), carry=None, io_alias=None):
    in_specs, out_specs, out_shape, scratch = list(in_specs), list(out_specs), list(out_shape), list(scratch)
    io_alias = dict(io_alias or {})
    if carry is None:
        return pl.pallas_call(body, name=name, grid=grid, in_specs=in_specs, out_specs=out_specs,
                              out_shape=out_shape, scratch_shapes=scratch, input_output_aliases=io_alias,
                              compiler_params=_cp(len(grid)))(*args)
    n_in, n_out, n_s = len(in_specs), len(out_specs), len(scratch)
    c_in, c_out = len(carry.operands), len(carry.out_shape)
    any_spec = pl.BlockSpec(memory_space=pl.ANY)

    def wrapped(*refs):
        ins, cins = refs[:n_in], refs[n_in:n_in + c_in]
        o0 = n_in + c_in
        outs, couts = refs[o0:o0 + n_out], refs[o0 + n_out:o0 + n_out + c_out]
        s0 = o0 + n_out + c_out
        scr, sems = refs[s0:s0 + n_s], refs[s0 + n_s:]
        first = pl.program_id(0) == 0
        last = pl.program_id(0) == grid[0] - 1
        for ax in range(1, len(grid)):
            first = jnp.logical_and(first, pl.program_id(ax) == 0)
            last = jnp.logical_and(last, pl.program_id(ax) == grid[ax] - 1)

        @pl.when(first)
        def _():
            carry.start(cins, couts, sems)

        body(*ins, *outs, *scr)

        @pl.when(last)
        def _():
            carry.finish(cins, couts, sems)

    return pl.pallas_call(
        wrapped, name=name, grid=grid, in_specs=in_specs + [any_spec] * c_in,
        out_specs=out_specs + [any_spec] * c_out, out_shape=out_shape + carry.out_shape,
        scratch_shapes=scratch + carry.sems,
        input_output_aliases={**io_alias, **{n_in + i: n_out + j for i, j in carry.aliases.items()}},
        compiler_params=_cp(len(grid)))(*args, *carry.operands)


def _whole_call(body, name, args, out_shape, scratch, carry=None):
    vm = pl.BlockSpec(memory_space=pltpu.VMEM)
    any_spec = pl.BlockSpec(memory_space=pl.ANY)
    out_shape, scratch = list(out_shape), list(scratch)
    n_in, n_out, n_s = len(args), len(out_shape), len(scratch)
    if carry is None:
        return pl.pallas_call(body, name=name, in_specs=[vm] * n_in, out_specs=[vm] * n_out, out_shape=out_shape,
                              scratch_shapes=scratch, compiler_params=_cp())(*args)
    c_in, c_out = len(carry.operands), len(carry.out_shape)

    def wrapped(*refs):
        ins, cins = refs[:n_in], refs[n_in:n_in + c_in]
        o0 = n_in + c_in
        outs, couts = refs[o0:o0 + n_out], refs[o0 + n_out:o0 + n_out + c_out]
        s0 = o0 + n_out + c_out
        scr, sems = refs[s0:s0 + n_s], refs[s0 + n_s:]
        carry.start(cins, couts, sems)
        body(*ins, *outs, *scr)
        carry.finish(cins, couts, sems)

    return pl.pallas_call(
        wrapped, name=name, in_specs=[vm] * n_in + [any_spec] * c_in, out_specs=[vm] * n_out + [any_spec] * c_out,
        out_shape=out_shape + carry.out_shape, scratch_shapes=scratch + carry.sems,
        input_output_aliases={n_in + i: n_out + j for i, j in carry.aliases.items()},
        compiler_params=_cp())(*args, *carry.operands)


def _alone(name, ex):
    any_spec = pl.BlockSpec(memory_space=pl.ANY)
    c_in, c_out = len(ex.operands), len(ex.out_shape)

    def body(*refs):
        ins, outs, sems = refs[:c_in], refs[c_in:c_in + c_out], refs[c_in + c_out:]
        ex.start(ins, outs, sems)
        ex.finish(ins, outs, sems)

    return pl.pallas_call(
        body, name=name, in_specs=[any_spec] * c_in, out_specs=[any_spec] * c_out, out_shape=ex.out_shape,
        scratch_shapes=ex.sems, input_output_aliases=ex.aliases, compiler_params=_cp())(*ex.operands)


def _ada_fwd(c_pad, w_ada, b_shard, carry=None):
    d = c_pad.shape[-1]
    cols = w_ada.shape[-1]
    chunk = 384

    def body(c_ref, w_ref, b_ref, call_ref, mod_ref, part, s1, r1, s2, r2):
        x, y, c = _my_place()
        dev = 4 * x + 2 * y + c
        chip = 2 * x + y
        call_ref[dev] = c_ref[...]

        def c_copy(k):
            px, py, pc = _flip(x, (k >> 2) & 1), _flip(y, (k >> 1) & 1), _flip(c, k & 1)
            return px, py, pc

        sends = []
        for k in range(1, N_DEV):
            px, py, pc = c_copy(k)
            cp = pltpu.make_async_remote_copy(src_ref=c_ref, dst_ref=call_ref.at[dev], send_sem=s1.at[k - 1],
                                              recv_sem=r1.at[k - 1], device_id=(px, py, pc), device_id_type=MESH)
            cp.start()
            sends.append(cp)
        for k in range(1, N_DEV):
            px, py, pc = c_copy(k)
            pltpu.make_async_remote_copy(src_ref=c_ref, dst_ref=call_ref.at[4 * px + 2 * py + pc],
                                         send_sem=s1.at[k - 1], recv_sem=r1.at[k - 1],
                                         device_id=(px, py, pc), device_id_type=MESH).wait_recv()
        for cp in sends:
            cp.wait_send()

        cs = call_ref[...].reshape(N_DEV * 8, d)
        sc = (cs * jax.nn.sigmoid(cs)).astype(BF)
        for n0 in range(0, cols, chunk):
            blk = _nn(sc, w_ref[:, n0:n0 + chunk].astype(BF)) + b_ref[:, n0:n0 + chunk]
            part[:, :, n0:n0 + chunk] = blk.reshape(N_DEV, 8, chunk)

        mod_ref[chip] = part[dev]
        sends = []
        for kk in range(1, N_CHIPS):
            px, py = _flip(x, (kk >> 1) & 1), _flip(y, kk & 1)
            cp = pltpu.make_async_remote_copy(src_ref=part.at[4 * px + 2 * py + c], dst_ref=mod_ref.at[chip],
                                              send_sem=s2.at[kk - 1], recv_sem=r2.at[kk - 1],
                                              device_id=(px, py, c), device_id_type=MESH)
            cp.start()
            sends.append(cp)
        for kk in range(1, N_CHIPS):
            px, py = _flip(x, (kk >> 1) & 1), _flip(y, kk & 1)
            pltpu.make_async_remote_copy(src_ref=part.at[dev], dst_ref=mod_ref.at[2 * px + py],
                                         send_sem=s2.at[kk - 1], recv_sem=r2.at[kk - 1],
                                         device_id=(px, py, c), device_id_type=MESH).wait_recv()
        for cp in sends:
            cp.wait_send()

    return _whole_call(
        body, "ada_fwd", (c_pad, w_ada, b_shard),
        [_sds((N_DEV, 8, d), F32), _sds((N_CHIPS, 8, cols), F32)],
        [pltpu.VMEM((N_DEV, 8, cols), F32),
         pltpu.SemaphoreType.DMA((N_DEV - 1,)), pltpu.SemaphoreType.DMA((N_DEV - 1,)),
         pltpu.SemaphoreType.DMA((N_CHIPS - 1,)), pltpu.SemaphoreType.DMA((N_CHIPS - 1,))], carry=carry)


def _ag_weights(bufs, kks=(1, 2, 3), relative=False):
    n, nk = len(bufs), len(kks)

    def half(b, which):
        hr = bufs[b].shape[2] // 2
        return pl.ds(pl.multiple_of(which * hr, 16), hr)

    def copies(outs, sems, b, i, kk):
        x, y, c = _my_place()
        chip = 2 * x + y
        px, py = _flip(x, (kk >> 1) & 1), _flip(y, kk & 1)
        mine, theirs = (0, kk) if relative else (chip, 2 * px + py)
        landing = kk if relative else chip
        k = nk * b + i
        send = pltpu.make_async_remote_copy(
            src_ref=outs[b].at[mine, :, half(b, c), :], dst_ref=outs[b].at[landing, :, half(b, c), :],
            send_sem=sems[0].at[k], recv_sem=sems[1].at[k], device_id=(px, py, c), device_id_type=MESH)
        got = outs[b].at[theirs, :, half(b, c), :]
        recv = pltpu.make_async_remote_copy(
            src_ref=got, dst_ref=got, send_sem=sems[0].at[k], recv_sem=sems[1].at[k],
            device_id=(px, py, c), device_id_type=MESH)
        fwd = pltpu.make_async_remote_copy(
            src_ref=got, dst_ref=got, send_sem=sems[2].at[k], recv_sem=sems[3].at[k],
            device_id=(x, y, 1 - c), device_id_type=MESH)
        other = outs[b].at[theirs, :, half(b, 1 - c), :]
        back = pltpu.make_async_remote_copy(
            src_ref=other, dst_ref=other, send_sem=sems[2].at[k], recv_sem=sems[3].at[k],
            device_id=(x, y, 1 - c), device_id_type=MESH)
        return send, recv, fwd, back

    def each(outs, sems):
        for b in range(n):
            for i, kk in enumerate(kks):
                yield copies(outs, sems, b, i, kk)

    def start(ins, outs, sems):
        for send, _, _, _ in each(outs, sems):
            send.start()

    def finish(ins, outs, sems):
        for _, recv, fwd, _ in each(outs, sems):
            recv.wait_recv()
            fwd.start()
        for send, _, fwd, back in each(outs, sems):
            back.wait_recv()
            send.wait_send()
            fwd.wait_send()

    return _Exchange(bufs, [_sds(s.shape, s.dtype) for s in bufs], {i: i for i in range(n)},
                     [pltpu.SemaphoreType.DMA((nk * n,))] * 4, start, finish)


def _rs_d2d(grads):
    n = len(grads)

    def copy(ins, outs, sems, b):
        x, y, c = _my_place()
        hr = grads[b].shape[2] // 2
        theirs = pl.ds(pl.multiple_of((1 - c) * hr, 8), hr)
        return pltpu.make_async_remote_copy(
            src_ref=ins[b].at[:, :, theirs, :], dst_ref=outs[b], send_sem=sems[0].at[b], recv_sem=sems[1].at[b],
            device_id=(x, y, 1 - c), device_id_type=MESH)

    def start(ins, outs, sems):
        for b in range(n):
            copy(ins, outs, sems, b).start()

    def finish(ins, outs, sems):
        for b in range(n):
            copy(ins, outs, sems, b).wait()

    return _Exchange(grads, [_sds(g.shape[:2] + (g.shape[2] // 2, g.shape[3]), F32) for g in grads], {},
                     [pltpu.SemaphoreType.DMA((n,))] * 2, start, finish)


def _add_halves(core, g, land):
    nchip, ng, rows, cols = g.shape
    hr = rows // 2
    tr = _row_tile(hr, cols)
    steps = hr // tr

    def body(core_ref, g_ref, l_ref, o_ref):
        del core_ref
        o_ref[...] = (g_ref[...] + l_ref[...]).astype(BF)

    return pl.pallas_call(
        body, name="add_halves",
        grid_spec=pltpu.PrefetchScalarGridSpec(
            num_scalar_prefetch=1, grid=(nchip, ng, steps),
            in_specs=[pl.BlockSpec((None, None, tr, cols), lambda j, a, i, cr: (j, a, cr[0] * steps + i, 0)),
                      pl.BlockSpec((None, None, tr, cols), lambda j, a, i, cr: (j, a, i, 0))],
            out_specs=pl.BlockSpec((None, None, tr, cols), lambda j, a, i, cr: (j, a, i, 0))),
        out_shape=_sds((nchip, ng, hr, cols), BF),
        compiler_params=_cp(3))(core, g, land)


def _rs_ici(parts, relative=False):
    n = len(parts)

    def copies(ins, outs, sems):
        x, y, c = _my_place()
        chip = 2 * x + y
        for b in range(n):
            for kk in range(1, N_CHIPS):
                px, py = _flip(x, (kk >> 1) & 1), _flip(y, kk & 1)
                k = 3 * b + kk - 1
                theirs, landing = (kk, kk) if relative else (2 * px + py, chip)
                send = pltpu.make_async_remote_copy(
                    src_ref=ins[b].at[theirs], dst_ref=outs[b].at[landing],
                    send_sem=sems[0].at[k], recv_sem=sems[1].at[k], device_id=(px, py, c), device_id_type=MESH)
                slot = outs[b].at[theirs]
                recv = pltpu.make_async_remote_copy(
                    src_ref=slot, dst_ref=slot, send_sem=sems[0].at[k], recv_sem=sems[1].at[k],
                    device_id=(px, py, c), device_id_type=MESH)
                yield send, recv

    def start(ins, outs, sems):
        for send, _ in copies(ins, outs, sems):
            send.start()

    def finish(ins, outs, sems):
        for send, recv in copies(ins, outs, sems):
            recv.wait_recv()
            send.wait_send()

    return _Exchange(parts, [_sds(p.shape, p.dtype) for p in parts], {},
                     [pltpu.SemaphoreType.DMA((3 * n,))] * 2, start, finish)


def _sum_chips(place, part, land, relative=False):
    nchip, ng, hr, cols = land.shape
    tr = _row_tile(hr, cols)
    steps = hr // tr

    def body(place_ref, p_ref, l1, l2, l3, o_ref):
        del place_ref
        o_ref[...] = ((p_ref[...].astype(F32) + l1[...].astype(F32)) + l2[...].astype(F32)) + l3[...].astype(F32)

    def slot(k):
        if relative:
            return pl.BlockSpec((None, None, tr, cols), lambda a, i, pr: (k, a, i, 0))
        return pl.BlockSpec((None, None, tr, cols), lambda a, i, pr: (jnp.bitwise_xor(pr[1], k), a, i, 0))

    return pl.pallas_call(
        body, name="sum_chips",
        grid_spec=pltpu.PrefetchScalarGridSpec(
            num_scalar_prefetch=1, grid=(ng, steps),
            in_specs=[slot(0), slot(1), slot(2), slot(3)],
            out_specs=pl.BlockSpec((None, tr, cols), lambda a, i, pr: (a, pr[0] * steps + i, 0))),
        out_shape=_sds((ng, 2 * hr, cols), F32),
        compiler_params=_cp(2))(place, part, land, land, land)


def _rs_final(bufs):
    n = len(bufs)

    def copy(outs, sems, b, which):
        x, y, c = _my_place()
        hr = bufs[b].shape[1] // 2
        rows = outs[b].at[:, pl.ds(pl.multiple_of((c if which == 0 else 1 - c) * hr, 8), hr), :]
        return pltpu.make_async_remote_copy(
            src_ref=rows, dst_ref=rows, send_sem=sems[0].at[b], recv_sem=sems[1].at[b],
            device_id=(x, y, 1 - c), device_id_type=MESH)

    def start(ins, outs, sems):
        for b in range(n):
            copy(outs, sems, b, 0).start()

    def finish(ins, outs, sems):
        for b in range(n):
            copy(outs, sems, b, 0).wait_send()
            copy(outs, sems, b, 1).wait_recv()

    return _Exchange(bufs, [_sds(h.shape, F32) for h in bufs], {i: i for i in range(n)},
                     [pltpu.SemaphoreType.DMA((n,))] * 2, start, finish)


def _small_sync(smalls, dmod_blk, c_all, carry=None):
    d = c_all.shape[-1]
    cols = dmod_blk.shape[-1]
    chunk = 384

    def body(sm_ref, dm_ref, c_ref, sum_ref, gw_ref, sm_all, dm_all, ssem, rsem):
        x, y, c = _my_place()
        dev = 4 * x + 2 * y + c
        chip = 2 * x + y
        sm_all[dev] = sm_ref[...]
        dm_all[dev] = dm_ref[chip]
        sends = []
        for k in range(1, N_DEV):
            px, py, pc = _flip(x, (k >> 2) & 1), _flip(y, (k >> 1) & 1), _flip(c, k & 1)
            a = pltpu.make_async_remote_copy(src_ref=sm_ref, dst_ref=sm_all.at[dev], send_sem=ssem.at[2 * (k - 1)],
                                             recv_sem=rsem.at[2 * (k - 1)], device_id=(px, py, pc),
                                             device_id_type=MESH)
            b = pltpu.make_async_remote_copy(src_ref=dm_ref.at[2 * px + py], dst_ref=dm_all.at[dev],
                                             send_sem=ssem.at[2 * (k - 1) + 1], recv_sem=rsem.at[2 * (k - 1) + 1],
                                             device_id=(px, py, pc), device_id_type=MESH)
            a.start()
            b.start()
            sends += [a, b]
        for k in range(1, N_DEV):
            px, py, pc = _flip(x, (k >> 2) & 1), _flip(y, (k >> 1) & 1), _flip(c, k & 1)
            pdev = 4 * px + 2 * py + pc
            pltpu.make_async_remote_copy(src_ref=sm_ref, dst_ref=sm_all.at[pdev], send_sem=ssem.at[2 * (k - 1)],
                                         recv_sem=rsem.at[2 * (k - 1)], device_id=(px, py, pc),
                                         device_id_type=MESH).wait_recv()
            pltpu.make_async_remote_copy(src_ref=dm_ref.at[chip], dst_ref=dm_all.at[pdev],
                                         send_sem=ssem.at[2 * (k - 1) + 1], recv_sem=rsem.at[2 * (k - 1) + 1],
                                         device_id=(px, py, pc), device_id_type=MESH).wait_recv()
        for cp in sends:
            cp.wait_send()

        tot = sm_all[0]
        for q in range(1, N_DEV):
            tot = tot + sm_all[q]
        sum_ref[...] = tot

        cs = c_ref[...].reshape(N_DEV * 8, d)
        sc = (cs * jax.nn.sigmoid(cs)).astype(BF)
        for n0 in range(0, cols, chunk):
            dmv = dm_all[:, :, n0:n0 + chunk].reshape(N_DEV * 8, chunk).astype(BF)
            gw_ref[:, n0:n0 + chunk] = _tn(sc, dmv)

    return _whole_call(
        body, "small_sync", (smalls, dmod_blk, c_all),
        [_sds(smalls.shape, F32), _sds((d, cols), F32)],
        [pltpu.VMEM((N_DEV,) + smalls.shape, F32), pltpu.VMEM((N_DEV, 8, cols), F32),
         pltpu.SemaphoreType.DMA((2 * (N_DEV - 1),)), pltpu.SemaphoreType.DMA((2 * (N_DEV - 1),))], carry=carry)


def _bucket_onehot():
    maps = np.stack([_bucket_map(dil).reshape(-1) for _, dil in DIL_CONFIGS])
    return (jnp.asarray(maps)[:, None, :] == jnp.arange(N_BUCKETS, dtype=jnp.int32)[None, :, None]).astype(BF)


def _dil_bias(rel_t, onehot):
    def body(r_ref, oh_ref, o_ref):
        rv = r_ref[...]
        hi = rv.astype(BF)
        lo = (rv - hi.astype(F32)).astype(BF)
        for c in range(len(DIL_CONFIGS)):
            o_ref[c] = _nn(hi, oh_ref[c]) + _nn(lo, oh_ref[c])

    return pl.pallas_call(body, name="dil_bias",
                          out_shape=_sds((len(DIL_CONFIGS), N_HEADS, BLOCK * 2 * BLOCK), F32),
                          compiler_params=_cp())(rel_t, onehot)


def _rowsum8(a):
    def body(a_ref, o_ref):
        o_ref[...] = jnp.sum(a_ref[...], axis=0, keepdims=True)

    return pl.pallas_call(body, name="rowsum8", out_shape=_sds((1, a.shape[1]), F32), compiler_params=_cp())(a)


def _local_step(x, mod, target, w, gains, rel_bias, place=None):
    nb, seq, d = x.shape
    t = nb * seq
    dist = place is not None
    core = place[0:1] if dist else None
    x0 = x.reshape(t, d)
    tgt = target.reshape(t, d)
    md = [mod[:, i:i + 1, :] for i in range(N_MOD)]
    sh1, sc1, gt1, sh2, sc2, gt2, sh3, sc3, gt3 = md
    g1, g2, g3 = gains["g_ffn1"], gains["g_mix"], gains["g_ffn2"]
    ones_g = _group_ones()

    def partial_sums(grads, lands):
        return [_add_halves(core, g, l) for g, l in zip(grads, lands)]

    def chip_sums(parts, lands):
        return [_sum_chips(place, p, l, relative=True) for p, l in zip(parts, lands)]

    gu1 = w["gu1"]
    res = _ffn_up(x0, g1, sc1, sh1, gu1, seq,
                  carry=_join([_ag_weights([w["d1"]], relative=True), _ag_weights([w["win"]])]) if dist else None)
    h1, a1, u1, s1 = res[:4]
    wd1, w_in = res[4:] if dist else (w["d1"], w["win"])
    f1, x1 = _ffn_down(s1, wd1, x0, gt1, seq, 0.5)

    res = _qkv_proj(x1, g2, sc2, sh2, w_in, seq, carry=_ag_weights([w["wout"]]) if dist else None)
    h2, qkv6, qkv_r4, qkv_r16 = res[:4]
    w_out2 = (res[4] if dist else w["wout"]).reshape(2 * D_GRP, d)
    qkv6b = qkv6.reshape(6, nb, seq, D_GRP)
    res = _sb_fwd(qkv6b, gains["g_sb_out"], nb, seq,
                  carry=_ag_weights([w["gu2"], w["d2"]], relative=True) if dist else None)
    o_sb, on_sb = res[:2]
    wgu2, wd2 = res[2:] if dist else (w["gu2"], w["d2"])
    onehot = _bucket_onehot()
    bias = _dil_bias(rel_bias.T, onehot).reshape(len(DIL_CONFIGS), N_HEADS * BLOCK, 2 * BLOCK)
    o_cs, l_cs = [], []
    qkv_rs = [(qkv6b, 3), (qkv_r4, 0), (qkv_r16, 0)]
    for ci, (_, dil) in enumerate(DIL_CONFIGS):
        sub = seq // dil
        arr, base = qkv_rs[ci]
        arr = arr.reshape(base + 3, nb, sub, dil * D_GRP)
        qkv_rs[ci] = (arr, base)
        o_c, l_c = _dil_fwd(arr, base, bias[ci], nb, sub, dil)
        o_cs.append(o_c.reshape(t // dil, dil * D_GRP))
        l_cs.append(l_c.reshape(t // dil, dil * D_GRP))
    o_dil, on_dil = _dil_comb(o_cs, l_cs, gains["g_dil_out"])
    tmix, x2 = _mix_out(on_sb.reshape(t, D_GRP), on_dil, w_out2, x1, gt2, seq)

    h3, a3, u3, s3 = _ffn_up(x2, g3, sc3, sh3, wgu2, seq)
    f3, dx3, dg_final, loss = _ffn_down_loss(s3, wd2, x2, gt3, seq, 0.5, gains["g_final"], tgt)

    da3, du3, df3, dgt3, dx2, dsh3, dsc3, dg3 = _ffn_bwd_x(dx3, gt3, f3, wd2, a3, u3, wgu2, x2, g3, sc3, seq, 0.5)
    grads2 = _ffn_bwd_w(h3, da3, du3, s3, df3)

    res = _mix_bwd_out(
        dx2, gt2, tmix, w_out2, o_sb.reshape(t, D_GRP), o_dil, on_sb.reshape(t, D_GRP), on_dil,
        gains["g_sb_out"], gains["g_dil_out"], ones_g, seq, carry=_rs_d2d(grads2) if dist else None)
    do_sb, do_dil, dgt2, dg_sb, dg_dil, dw_out = res[:6]
    parts2 = partial_sums(grads2, res[6:]) if dist else None
    dw_out = dw_out.reshape(N_CHIPS, 1, 2 * D_GRP // N_CHIPS, d)
    res = _sb_bwd(qkv6b, do_sb.reshape(nb, seq, D_GRP), nb, seq,
                  carry=_join([_rs_ici(parts2, relative=True), _rs_d2d([dw_out])]) if dist else None)
    dqkv6 = res[0]
    halves2 = chip_sums(parts2, res[1:2]) if dist else None
    part_wo = partial_sums([dw_out], res[2:3]) if dist else None
    dcs = _dil_comb_bwd(do_dil, o_cs, l_cs)
    dsum, a_tiles = [], []
    for ci, (_, dil) in enumerate(DIL_CONFIGS):
        sub = seq // dil
        do_c = dcs[ci].reshape(nb, sub, dil * D_GRP)
        dd_c = dcs[3 + ci].reshape(nb, sub, dil * D_GRP)
        res = _dil_bwd(qkv_rs[ci][0], qkv_rs[ci][1], bias[ci], do_c, dd_c, nb, sub, dil)
        dsum.append(res[0].reshape(3, t // dil, dil * D_GRP))
        a_tiles.append(res[1].reshape(N_HEADS, BLOCK * 2 * BLOCK))
    dqkv6 = _dqkv_dil_sum(dsum, dqkv6.reshape(6, t, D_GRP))
    drel = _relbias_grad(jnp.stack(a_tiles), onehot)
    dx1, dsh2, dsc2, dg2 = _mix_bwd_dh(dqkv6, w_in, x1, g2, sc2, dx2, seq)

    da1, du1, df1, dgt1 = _ffn_bwd_ds(dx1, gt1, f1, wd1, a1, u1, seq, 0.5)
    g_dn = _ffn_bwd_w(h1, da1, du1, s1, df1, terms=(2,))
    res = _ffn_bwd_w(h1, da1, du1, s1, df1, terms=(0, 1), carry=_rs_d2d(g_dn) if dist else None)
    g_gu = res[:1]
    grads1 = g_gu + g_dn
    parts_dn = partial_sums(g_dn, res[1:2]) if dist else None
    res = _dw_in(h2, dqkv6, carry=_join([_rs_ici(parts_dn, relative=True), _rs_ici(part_wo), _rs_d2d(g_gu),
                                         _rs_final(halves2)]) if dist else None)
    grads_m = [res[0], dw_out]
    if dist:
        sums_dn = chip_sums(parts_dn, res[1:2])
        sum_wo = [_sum_chips(place, p, l) for p, l in zip(part_wo, res[2:3])]
        parts_gu = partial_sums(g_gu, res[3:4])
        grads2 = res[4:5]
    res = _ffn_bwd_dh(da1, du1, gu1, x0, g1, sc1, dx1, seq,
                      carry=_join([_rs_ici(parts_gu, relative=True), _rs_d2d(grads_m[:1])]) if dist else None)
    dx0, dsh1, dsc1, dg1 = res[:4]
    pending = None
    if dist:
        pending = (chip_sums(parts_gu, res[4:5]) + sums_dn + sum_wo, partial_sums(grads_m[:1], res[5:6]))

    dmod = jnp.concatenate([dsh1, dsc1, dgt1, dsh2, dsc2, dgt2, dsh3, dsc3, dgt3], axis=1)
    return dict(grad_x=dx0.reshape(nb, seq, d), loss=loss[0, 0], dmod=dmod.reshape(nb, N_MOD * d),
                dffn1=grads1, dffn2=grads2[0], dwin=grads_m[0], dwout=grads_m[1], pending=pending,
                dg_ffn1=dg1, dg_mix=dg2, dg_ffn2=dg3, dg_final=dg_final, dg_sb=dg_sb, dg_dil=dg_dil,
                drel=drel.T)


_SMALL_ORDER = (("b_ada", N_MOD * 1024), ("g_ffn1", 1024), ("g_mix", 1024), ("g_ffn2", 1024), ("g_final", 1024),
                ("g_sb_out", D_GRP), ("g_dil_out", D_GRP), ("rel_bias", N_BUCKETS * N_HEADS))


def _pack_small(parts, extra=None):
    flat = [parts[name].reshape(-1).astype(F32) for name, _ in _SMALL_ORDER]
    used = sum(sz for _, sz in _SMALL_ORDER)
    pad = SMALL_ROWS * 128 - used
    tail = jnp.zeros((pad,), F32)
    if extra is not None:
        tail = tail.at[0].set(extra)
    return jnp.concatenate(flat + [tail]).reshape(SMALL_ROWS, 128)


def _unpack_small(packed, shapes):
    flat = packed.reshape(-1)
    out, off = {}, 0
    for name, sz in _SMALL_ORDER:
        out[name] = flat[off:off + sz].reshape(shapes[name])
        off += sz
    return out, flat[off]


def kernel(x, c, w_ada, b_ada, g_ffn1, w1_gate, w1_up, w1_down, g_mix, w_in, g_sb_out, g_dil_out, w_out, rel_bias, g_ffn2, w2_gate, w2_up, w2_down, g_final, loss_target, m_w_ada, m_b_ada, m_g_ffn1, m_w1_gate, m_w1_up, m_w1_down, m_g_mix, m_w_in, m_g_sb_out, m_g_dil_out, m_w_out, m_rel_bias, m_g_ffn2, m_w2_gate, m_w2_up, m_w2_down, m_g_final, v_w_ada, v_b_ada, v_g_ffn1, v_w1_gate, v_w1_up, v_w1_down, v_g_mix, v_w_in, v_g_sb_out, v_g_dil_out, v_w_out, v_rel_bias, v_g_ffn2, v_w2_gate, v_w2_up, v_w2_down, v_g_final):
    nb, seq, d = x.shape
    xi, yi, ci = lax.axis_index("x"), lax.axis_index("y"), lax.axis_index("c")
    chip = 2 * xi + yi
    ada_cols = w_ada.shape[-1]

    c_pad = jnp.zeros((8, d), F32).at[:nb].set(c)
    b_shard = lax.dynamic_slice(b_ada, (0, chip * ada_cols), (1, ada_cols))
    shards = dict(gu1=jnp.stack([w1_gate[0], w1_up[0]]), d1=w1_down, win=w_in, wout=w_out,
                  gu2=jnp.stack([w2_gate[0], w2_up[0]]), d2=w2_down)
    bufs = {k: lax.dynamic_update_slice(lax.empty((N_CHIPS,) + s.shape, BF), s.astype(BF)[None],
                                        (chip if k in ("win", "wout") else 0, 0, 0, 0))
            for k, s in shards.items()}
    c_all, mod_blk, bufs["gu1"] = _ada_fwd(c_pad, w_ada[0], b_shard,
                                           carry=_ag_weights([bufs["gu1"]], relative=True))
    mod = jnp.transpose(mod_blk[:, :nb, :], (1, 0, 2)).reshape(nb, N_MOD, d)

    gains = dict(g_ffn1=g_ffn1, g_mix=g_mix, g_ffn2=g_ffn2, g_final=g_final.reshape(1, d),
                 g_sb_out=g_sb_out.reshape(1, D_GRP), g_dil_out=g_dil_out.reshape(1, D_GRP))
    place = jnp.stack([ci, chip]).astype(jnp.int32)
    r = _local_step(x, mod, loss_target, bufs, gains, rel_bias, place)

    dmod = r["dmod"]
    dmod_pad = jnp.zeros((8, N_MOD * d), F32).at[:nb].set(dmod)
    dmod_blk = jnp.transpose(dmod_pad.reshape(8, N_CHIPS, ada_cols), (1, 0, 2))
    small_parts = dict(b_ada=_rowsum8(dmod_pad), g_ffn1=r["dg_ffn1"], g_mix=r["dg_mix"], g_ffn2=r["dg_ffn2"],
                       g_final=r["dg_final"], g_sb_out=r["dg_sb"], g_dil_out=r["dg_dil"], rel_bias=r["drel"])
    halves1, parts_m = r["pending"]
    res = _small_sync(_pack_small(small_parts, r["loss"]), dmod_blk, c_all,
                      carry=_join([_rs_final(halves1), _rs_ici(parts_m)]))
    small_sum, g_wada, gffn1_gu, gffn1_dn, gwout = res[:5]
    halves_m = [_sum_chips(place, p, l) for p, l in zip(parts_m, res[5:6])]
    gwin, = _alone("rs_last", _rs_final(halves_m))
    gffn2 = r["dffn2"]

    small_w = dict(b_ada=b_ada, g_ffn1=g_ffn1, g_mix=g_mix, g_ffn2=g_ffn2, g_final=g_final,
                   g_sb_out=g_sb_out, g_dil_out=g_dil_out, rel_bias=rel_bias)
    small_m = dict(b_ada=m_b_ada, g_ffn1=m_g_ffn1, g_mix=m_g_mix, g_ffn2=m_g_ffn2, g_final=m_g_final,
                   g_sb_out=m_g_sb_out, g_dil_out=m_g_dil_out, rel_bias=m_rel_bias)
    small_v = dict(b_ada=v_b_ada, g_ffn1=v_g_ffn1, g_mix=v_g_mix, g_ffn2=v_g_ffn2, g_final=v_g_final,
                   g_sb_out=v_g_sb_out, g_dil_out=v_g_dil_out, rel_bias=v_rel_bias)
    shapes = {k: v.shape for k, v in small_w.items()}
    sg, sd, sm, sv = _adamw(_pack_small(small_w), small_sum.reshape(1, SMALL_ROWS, 128), 0,
                            _pack_small(small_m), _pack_small(small_v))
    sg, loss = _unpack_small(sg, shapes)
    sd, _ = _unpack_small(sd, shapes)
    sm, _ = _unpack_small(sm, shapes)
    sv, _ = _unpack_small(sv, shapes)

    big = {}

    def upd(name, w, g_arr, sel, m, v, transposed=False):
        swap = (lambda a: jnp.swapaxes(a, -1, -2)) if transposed else (lambda a: a)
        w2, m2, v2 = [swap(a)[0] for a in (w, m, v)]
        big[name] = [swap(a[None]) for a in _adamw(w2, g_arr, sel, m2, v2)]

    upd("w_ada", w_ada, g_wada.reshape(1, d, ada_cols), 0, m_w_ada, v_w_ada)
    upd("w1_gate", w1_gate, gffn1_gu, 0, m_w1_gate, v_w1_gate, transposed=True)
    upd("w1_up", w1_up, gffn1_gu, 1, m_w1_up, v_w1_up, transposed=True)
    upd("w1_down", w1_down, gffn1_dn, 0, m_w1_down, v_w1_down)
    upd("w_in", w_in, gwin, 0, m_w_in, v_w_in)
    upd("w_out", w_out, gwout, 0, m_w_out, v_w_out)
    upd("w2_gate", w2_gate, gffn2, 0, m_w2_gate, v_w2_gate, transposed=True)
    upd("w2_up", w2_up, gffn2, 1, m_w2_up, v_w2_up, transposed=True)
    upd("w2_down", w2_down, gffn2, 2, m_w2_down, v_w2_down)

    names = ["w_ada", "b_ada", "g_ffn1", "w1_gate", "w1_up", "w1_down", "g_mix", "w_in", "g_sb_out", "g_dil_out",
             "w_out", "rel_bias", "g_ffn2", "w2_gate", "w2_up", "w2_down", "g_final"]
    outs = [loss, r["grad_x"]]
    for k, small in enumerate((sg, sd, sm, sv)):
        for name in names:
            outs.append(big[name][k] if name in big else small[name])
    return tuple(outs)
```
